```python
import math
import jax, jax.numpy as jnp
from jax import lax
import numpy as np

D_MODEL = 1024
BATCH = 8
SEQ = 2048
DEPTH = 1

GDN_HEADS = 4
GDN_HEAD_DIM = 128
GDN_WIDTH = GDN_HEADS * GDN_HEAD_DIM
GDN_CONV = 4
GDN_CHUNK = 64
DIL_HEADS = 8
DIL_HEAD_DIM = 64
DIL_WIDTH = DIL_HEADS * DIL_HEAD_DIM
DIL_PATTERNS = ((128, 1), (512, 4), (2048, 16))
BAND_BLOCK = 128
MIX_WIDTH = GDN_WIDTH + DIL_WIDTH
IN_COLS = 3 * GDN_WIDTH + GDN_WIDTH + 2 * GDN_HEADS + 3 * DIL_WIDTH
D_FF = 2816
FFN_CONV = 3
EPS = 1e-6

kernel_name = "hymba_gdn_dilated_convffn"


def rmsnorm(x, w):
    xf = x.astype(jnp.float32)
    y = xf * lax.rsqrt(jnp.mean(xf * xf, axis=-1, keepdims=True) + EPS)
    return (y * w.astype(jnp.float32)).astype(x.dtype)


def l2norm(x):
    return x * lax.rsqrt(jnp.sum(x * x, axis=-1, keepdims=True) + EPS)


def causal_dwconv(x, w):
    K = w.shape[0]
    S = x.shape[1]
    xp = jnp.pad(x, ((0, 0), (K - 1, 0), (0, 0)))
    out = xp[:, 0:S, :] * w[0]
    for i in range(1, K):
        out = out + xp[:, i:i + S, :] * w[i]
    return out


def gated_delta_rule(q, k, v, g, beta):
    B, S, H, Dk = q.shape
    Dv = v.shape[-1]
    C = GDN_CHUNK
    nc = S // C

    def chunk_vec(t):
        return t.reshape(B, nc, C, H, t.shape[-1]).transpose(1, 0, 3, 2, 4)

    def chunk_sc(t):
        return t.reshape(B, nc, C, H).transpose(1, 0, 3, 2)

    qc, kc, vc = chunk_vec(q), chunk_vec(k), chunk_vec(v)
    bc = chunk_sc(beta)
    gc = jnp.cumsum(chunk_sc(g), axis=-1)
    idx = jnp.arange(C)
    incl = idx[:, None] >= idx[None, :]
    strict = idx[:, None] > idx[None, :]
    diff = gc[..., :, None] - gc[..., None, :]
    dec_incl = jnp.where(incl, jnp.exp(jnp.where(incl, diff, 0.0)), 0.0)
    dec_strict = jnp.where(strict, dec_incl, 0.0)
    kk = jnp.einsum('nbhtd,nbhjd->nbhtj', kc, kc)
    lmat = dec_strict * kk * bc[..., None, :]
    gam = jnp.exp(gc)[..., None]
    rhs = jnp.concatenate([vc, gam * kc], axis=-1)
    sol = lax.linalg.triangular_solve(lmat, rhs, left_side=True, lower=True, unit_diagonal=True)
    u_v, w_k = sol[..., :Dv], sol[..., Dv:]
    attn = dec_incl * jnp.einsum('nbhtd,nbhjd->nbhtj', qc, kc) * bc[..., None, :]
    q_dec = gam * qc
    k_end = kc * (jnp.exp(gc[..., -1:] - gc) * bc)[..., None]
    g_end = jnp.exp(gc[..., -1])

    def step(state, xs):
        uv, wk, qd, at, ke, ge = xs
        u = uv - jnp.einsum('bhck,bhkv->bhcv', wk, state)
        o = jnp.einsum('bhck,bhkv->bhcv', qd, state) + jnp.einsum('bhcj,bhjv->bhcv', at, u)
        state = ge[..., None, None] * state + jnp.einsum('bhck,bhcv->bhkv', ke, u)
        return state, o

    s0 = jnp.zeros((B, H, Dk, Dv), jnp.float32)
    _, o = lax.scan(step, s0, (u_v, w_k, q_dec, attn, k_end, g_end))
    return o.transpose(1, 0, 3, 2, 4).reshape(B, S, H, Dv)


def band_attention(q, k, v, n_back):
    L, Dh = q.shape[-2], q.shape[-1]
    nb = -(-L // BAND_BLOCK)
    pad = nb * BAND_BLOCK - L
    padcfg = [(0, 0)] * (q.ndim - 2) + [(0, pad), (0, 0)]
    lead = q.shape[:-2]

    def blocks(t):
        return jnp.pad(t, padcfg).reshape(*lead, nb, BAND_BLOCK, Dh)

    qb, kb, vb = blocks(q), blocks(k), blocks(v)

    def with_prev(t):
        prev = jnp.concatenate([jnp.zeros_like(t[..., :1, :, :]), t[..., :-1, :, :]], axis=-3)
        return jnp.concatenate([prev, t], axis=-2)

    kk, vv = with_prev(kb), with_prev(vb)
    s = jnp.einsum('...nqd,...nkd->...nqk', qb, kk)
    blk = jnp.arange(nb)[:, None, None] * BAND_BLOCK
    qpos = blk + jnp.arange(BAND_BLOCK)[None, :, None]
    kpos = blk - BAND_BLOCK + jnp.arange(2 * BAND_BLOCK)[None, None, :]
    dist = qpos - kpos
    valid = (dist >= 0) & (dist <= n_back) & (kpos >= 0)
    s = jnp.where(valid, s, -jnp.inf)
    m = jnp.max(s, axis=-1)
    p = jnp.exp(s - m[..., None])
    den = jnp.sum(p, axis=-1)
    num = jnp.einsum('...nqk,...nkd->...nqd', p, vv)
    num = num.reshape(*lead, nb * BAND_BLOCK, Dh)[..., :L, :]
    den = den.reshape(*lead, nb * BAND_BLOCK)[..., :L]
    m = m.reshape(*lead, nb * BAND_BLOCK)[..., :L]
    return num, den, m


def dilated_attention(q, k, v):
    B, S, H, Dh = q.shape
    qf = q.astype(jnp.float32).transpose(0, 2, 1, 3) * (Dh ** -0.5)
    kf = k.astype(jnp.float32).transpose(0, 2, 1, 3)
    vf = v.astype(jnp.float32).transpose(0, 2, 1, 3)
    nums, dens, ms = [], [], []
    for window, dil in DIL_PATTERNS:
        L = S // dil

        def stride(t):
            return t.reshape(B, H, L, dil, Dh).transpose(0, 1, 3, 2, 4)

        num, den, m = band_attention(stride(qf), stride(kf), stride(vf), window // dil)
        nums.append(num.transpose(0, 1, 3, 2, 4).reshape(B, H, S, Dh))
        dens.append(den.transpose(0, 1, 3, 2).reshape(B, H, S))
        ms.append(m.transpose(0, 1, 3, 2).reshape(B, H, S))
    m_all = jnp.maximum(jnp.maximum(ms[0], ms[1]), ms[2])
    wts = [jnp.exp(mi - m_all) for mi in ms]
    num_tot = wts[0][..., None] * nums[0] + wts[1][..., None] * nums[1] + wts[2][..., None] * nums[2]
    den_tot = wts[0] * dens[0] + wts[1] * dens[1] + wts[2] * dens[2]
    out = num_tot / den_tot[..., None]
    return out.transpose(0, 2, 1, 3).reshape(B, S, H * Dh)


def hybrid_layer(x, norm1_w, w_in, conv_qkv_w, a_log, dt_bias, gdn_norm_w, w_out,
                 norm2_w, w_up, ffn_conv_w, w_down):
    B, S, _ = x.shape
    h = rmsnorm(x, norm1_w)
    proj = h @ w_in
    o1 = 3 * GDN_WIDTH
    o2 = o1 + GDN_WIDTH
    o3 = o2 + GDN_HEADS
    o4 = o3 + GDN_HEADS
    qkv_a, z_a, b_a, a_a, qkv_b = (proj[..., :o1], proj[..., o1:o2], proj[..., o2:o3],
                                   proj[..., o3:o4], proj[..., o4:])
    qkv_a = jax.nn.silu(causal_dwconv(qkv_a, conv_qkv_w)).astype(jnp.float32)
    qa = qkv_a[..., :GDN_WIDTH].reshape(B, S, GDN_HEADS, GDN_HEAD_DIM)
    ka = qkv_a[..., GDN_WIDTH:2 * GDN_WIDTH].reshape(B, S, GDN_HEADS, GDN_HEAD_DIM)
    va = qkv_a[..., 2 * GDN_WIDTH:].reshape(B, S, GDN_HEADS, GDN_HEAD_DIM)
    qa = l2norm(qa) * (GDN_HEAD_DIM ** -0.5)
    ka = l2norm(ka)
    beta = jax.nn.sigmoid(b_a.astype(jnp.float32))
    g = -jnp.exp(a_log.astype(jnp.float32)) * jax.nn.softplus(
        a_a.astype(jnp.float32) + dt_bias.astype(jnp.float32))
    o_a = gated_delta_rule(qa, ka, va, g, beta)
    z = z_a.astype(jnp.float32).reshape(B, S, GDN_HEADS, GDN_HEAD_DIM)
    o_a = (rmsnorm(o_a, gdn_norm_w) * jax.nn.silu(z)).reshape(B, S, GDN_WIDTH).astype(x.dtype)
    qb = qkv_b[..., :DIL_WIDTH].reshape(B, S, DIL_HEADS, DIL_HEAD_DIM)
    kb = qkv_b[..., DIL_WIDTH:2 * DIL_WIDTH].reshape(B, S, DIL_HEADS, DIL_HEAD_DIM)
    vb = qkv_b[..., 2 * DIL_WIDTH:].reshape(B, S, DIL_HEADS, DIL_HEAD_DIM)
    o_b = dilated_attention(qb, kb, vb).astype(x.dtype)
    x = x + jnp.concatenate([o_a, o_b], axis=-1) @ w_out
    h = rmsnorm(x, norm2_w)
    u = causal_dwconv(h @ w_up, ffn_conv_w)
    gate, up = u[..., :D_FF], u[..., D_FF:]
    x = x + (jax.nn.silu(gate) * up) @ w_down
    return x


def _fwd_setup_inputs(seed: int = 0) -> dict:
    key = jax.random.key(seed)
    ks = jax.random.split(key, 14)
    f32 = jnp.float32
    x = jax.random.normal(ks[0], (BATCH, SEQ, D_MODEL), f32)
    norm1_w = 1.0 + 0.02 * jax.random.normal(ks[1], (DEPTH, D_MODEL), f32)
    w_in = jax.random.normal(ks[2], (DEPTH, D_MODEL, IN_COLS), f32) * D_MODEL ** -0.5
    conv_qkv_w = jax.random.normal(ks[3], (DEPTH, GDN_CONV, 3 * GDN_WIDTH), f32) * GDN_CONV ** -0.5
    a_log = jnp.log(jax.random.uniform(ks[4], (DEPTH, GDN_HEADS), f32, 1.0, 16.0))
    dt = jnp.exp(jax.random.uniform(ks[5], (DEPTH, GDN_HEADS), f32, math.log(1e-3), math.log(1e-1)))
    dt_bias = dt + jnp.log(-jnp.expm1(-dt))
    gdn_norm_w = 1.0 + 0.02 * jax.random.normal(ks[6], (DEPTH, GDN_HEAD_DIM), f32)
    w_out = jax.random.normal(ks[7], (DEPTH, MIX_WIDTH, D_MODEL), f32) * MIX_WIDTH ** -0.5
    norm2_w = 1.0 + 0.02 * jax.random.normal(ks[8], (DEPTH, D_MODEL), f32)
    w_up = jax.random.normal(ks[9], (DEPTH, D_MODEL, 2 * D_FF), f32) * D_MODEL ** -0.5
    ffn_conv_w = jax.random.normal(ks[10], (DEPTH, FFN_CONV, 2 * D_FF), f32) * FFN_CONV ** -0.5
    w_down = jax.random.normal(ks[11], (DEPTH, D_FF, D_MODEL), f32) * D_FF ** -0.5
    final_norm_w = 1.0 + 0.02 * jax.random.normal(ks[12], (D_MODEL,), f32)
    return {"x": x, "norm1_w": norm1_w, "w_in": w_in, "conv_qkv_w": conv_qkv_w,
            "a_log": a_log, "dt_bias": dt_bias, "gdn_norm_w": gdn_norm_w, "w_out": w_out,
            "norm2_w": norm2_w, "w_up": w_up, "ffn_conv_w": ffn_conv_w, "w_down": w_down,
            "final_norm_w": final_norm_w}


def _fwd_reference(x, norm1_w, w_in, conv_qkv_w, a_log, dt_bias, gdn_norm_w, w_out,
              norm2_w, w_up, ffn_conv_w, w_down, final_norm_w):
    for l in range(DEPTH):
        x = hybrid_layer(x, norm1_w[l], w_in[l], conv_qkv_w[l], a_log[l], dt_bias[l],
                         gdn_norm_w[l], w_out[l], norm2_w[l], w_up[l], ffn_conv_w[l], w_down[l])
    return rmsnorm(x, final_norm_w)


import jax as _jax
import jax.numpy as _jnp

TWIN_FORMAT = 'train_step'
FWD_PARAMS = ['x', 'norm1_w', 'w_in', 'conv_qkv_w', 'a_log', 'dt_bias', 'gdn_norm_w', 'w_out', 'norm2_w', 'w_up', 'ffn_conv_w', 'w_down', 'final_norm_w']
TWIN_WEIGHTS = ['norm1_w', 'w_in', 'conv_qkv_w', 'a_log', 'dt_bias', 'gdn_norm_w', 'w_out', 'norm2_w', 'w_up', 'ffn_conv_w', 'w_down', 'final_norm_w']
TWIN_DIFF_INPUT = 'x'
TWIN_INPUTS = ['x', 'norm1_w', 'w_in', 'conv_qkv_w', 'a_log', 'dt_bias', 'gdn_norm_w', 'w_out', 'norm2_w', 'w_up', 'ffn_conv_w', 'w_down', 'final_norm_w', 'loss_target', 'm_norm1_w', 'm_w_in', 'm_conv_qkv_w', 'm_a_log', 'm_dt_bias', 'm_gdn_norm_w', 'm_w_out', 'm_norm2_w', 'm_w_up', 'm_ffn_conv_w', 'm_w_down', 'm_final_norm_w', 'v_norm1_w', 'v_w_in', 'v_conv_qkv_w', 'v_a_log', 'v_dt_bias', 'v_gdn_norm_w', 'v_w_out', 'v_norm2_w', 'v_w_up', 'v_ffn_conv_w', 'v_w_down', 'v_final_norm_w']
TWIN_OUTPUTS = ['loss', 'grad_x', 'grad_norm1_w', 'grad_w_in', 'grad_conv_qkv_w', 'grad_a_log', 'grad_dt_bias', 'grad_gdn_norm_w', 'grad_w_out', 'grad_norm2_w', 'grad_w_up', 'grad_ffn_conv_w', 'grad_w_down', 'grad_final_norm_w', 'delta_norm1_w', 'delta_w_in', 'delta_conv_qkv_w', 'delta_a_log', 'delta_dt_bias', 'delta_gdn_norm_w', 'delta_w_out', 'delta_norm2_w', 'delta_w_up', 'delta_ffn_conv_w', 'delta_w_down', 'delta_final_norm_w', 'new_m_norm1_w', 'new_m_w_in', 'new_m_conv_qkv_w', 'new_m_a_log', 'new_m_dt_bias', 'new_m_gdn_norm_w', 'new_m_w_out', 'new_m_norm2_w', 'new_m_w_up', 'new_m_ffn_conv_w', 'new_m_w_down', 'new_m_final_norm_w', 'new_v_norm1_w', 'new_v_w_in', 'new_v_conv_qkv_w', 'new_v_a_log', 'new_v_dt_bias', 'new_v_gdn_norm_w', 'new_v_w_out', 'new_v_norm2_w', 'new_v_w_up', 'new_v_ffn_conv_w', 'new_v_w_down', 'new_v_final_norm_w']
TWIN_LEAF_KINDS = {'loss': 'loss', 'grad_x': 'grad_x', 'grad_norm1_w': 'grad_w', 'grad_w_in': 'grad_w', 'grad_conv_qkv_w': 'grad_w', 'grad_a_log': 'grad_w', 'grad_dt_bias': 'grad_w', 'grad_gdn_norm_w': 'grad_w', 'grad_w_out': 'grad_w', 'grad_norm2_w': 'grad_w', 'grad_w_up': 'grad_w', 'grad_ffn_conv_w': 'grad_w', 'grad_w_down': 'grad_w', 'grad_final_norm_w': 'grad_w', 'delta_norm1_w': 'delta_w', 'delta_w_in': 'delta_w', 'delta_conv_qkv_w': 'delta_w', 'delta_a_log': 'delta_w', 'delta_dt_bias': 'delta_w', 'delta_gdn_norm_w': 'delta_w', 'delta_w_out': 'delta_w', 'delta_norm2_w': 'delta_w', 'delta_w_up': 'delta_w', 'delta_ffn_conv_w': 'delta_w', 'delta_w_down': 'delta_w', 'delta_final_norm_w': 'delta_w', 'new_m_norm1_w': 'new_m', 'new_m_w_in': 'new_m', 'new_m_conv_qkv_w': 'new_m', 'new_m_a_log': 'new_m', 'new_m_dt_bias': 'new_m', 'new_m_gdn_norm_w': 'new_m', 'new_m_w_out': 'new_m', 'new_m_norm2_w': 'new_m', 'new_m_w_up': 'new_m', 'new_m_ffn_conv_w': 'new_m', 'new_m_w_down': 'new_m', 'new_m_final_norm_w': 'new_m', 'new_v_norm1_w': 'new_v', 'new_v_w_in': 'new_v', 'new_v_conv_qkv_w': 'new_v', 'new_v_a_log': 'new_v', 'new_v_dt_bias': 'new_v', 'new_v_gdn_norm_w': 'new_v', 'new_v_w_out': 'new_v', 'new_v_norm2_w': 'new_v', 'new_v_w_up': 'new_v', 'new_v_ffn_conv_w': 'new_v', 'new_v_w_down': 'new_v', 'new_v_final_norm_w': 'new_v'}


def _forward(args):
    return _fwd_reference(*[args[k] for k in FWD_PARAMS])


def _output_shape():
    out = _jax.eval_shape(lambda: _forward(_fwd_setup_inputs(0)))
    return out.shape, out.dtype

N_MICROBATCH = 1
ADAM_LR = 0.001
ADAM_B1 = 0.9
ADAM_B2 = 0.999
ADAM_EPS = 1e-08
ADAM_WD = 0.01
ADAM_STEP = 10
PER_EXAMPLE_BATCH_AXIS = {'x': 0, 'loss_target': 0}
SHARED_INPUTS = []
_WEIGHT_DTYPES = {'norm1_w': _jnp.float32, 'w_in': _jnp.float32, 'conv_qkv_w': _jnp.float32, 'a_log': _jnp.float32, 'dt_bias': _jnp.float32, 'gdn_norm_w': _jnp.float32, 'w_out': _jnp.float32, 'norm2_w': _jnp.float32, 'w_up': _jnp.float32, 'ffn_conv_w': _jnp.float32, 'w_down': _jnp.float32, 'final_norm_w': _jnp.float32}
MOMENT_SCALE = {'norm1_w': 9.352926e-02, 'w_in': 4.994012e-02, 'conv_qkv_w': 5.882907e-02, 'a_log': 2.478705e-01, 'dt_bias': 2.327122e-01, 'gdn_norm_w': 1.620029e-01, 'w_out': 5.610397e-02, 'norm2_w': 8.982774e-02, 'w_up': 3.790295e-02, 'ffn_conv_w': 3.778220e-02, 'w_down': 6.175959e-02, 'final_norm_w': 1.601902e+01}


def _to_microbatches(a, axis):
    t = _jnp.moveaxis(a, axis, 0)
    t = t.reshape((N_MICROBATCH, t.shape[0] // N_MICROBATCH) + t.shape[1:])
    return _jnp.moveaxis(t, 1, axis + 1)


def setup_inputs(seed: int = 0) -> dict:
    inp = _fwd_setup_inputs(seed)
    key = _jax.random.fold_in(_jax.random.key(seed), 7919)
    shape, _ = _output_shape()
    out = dict(inp)
    out["loss_target"] = _jax.random.normal(_jax.random.fold_in(key, 0), shape, _jnp.float32)
    for i, name in enumerate(TWIN_WEIGHTS):
        w = inp[name].astype(_jnp.float32)
        if MOMENT_SCALE is None:
            s = _jnp.sqrt(_jnp.mean(_jnp.square(w)) + 1e-30)
        else:
            s = MOMENT_SCALE[name]
        km, kv = _jax.random.split(_jax.random.fold_in(key, i + 1))
        out[name] = w
        out["m_" + name] = s * _jax.random.normal(km, w.shape, _jnp.float32)
        out["v_" + name] = (s * s) * _jax.random.uniform(kv, w.shape, _jnp.float32, 0.5, 1.5)
    if N_MICROBATCH > 1:
        for name, axis in PER_EXAMPLE_BATCH_AXIS.items():
            out[name] = _to_microbatches(out[name], axis)
    return {'x': out['x'], 'norm1_w': out['norm1_w'], 'w_in': out['w_in'], 'conv_qkv_w': out['conv_qkv_w'], 'a_log': out['a_log'], 'dt_bias': out['dt_bias'], 'gdn_norm_w': out['gdn_norm_w'], 'w_out': out['w_out'], 'norm2_w': out['norm2_w'], 'w_up': out['w_up'], 'ffn_conv_w': out['ffn_conv_w'], 'w_down': out['w_down'], 'final_norm_w': out['final_norm_w'], 'loss_target': out['loss_target'], 'm_norm1_w': out['m_norm1_w'], 'm_w_in': out['m_w_in'], 'm_conv_qkv_w': out['m_conv_qkv_w'], 'm_a_log': out['m_a_log'], 'm_dt_bias': out['m_dt_bias'], 'm_gdn_norm_w': out['m_gdn_norm_w'], 'm_w_out': out['m_w_out'], 'm_norm2_w': out['m_norm2_w'], 'm_w_up': out['m_w_up'], 'm_ffn_conv_w': out['m_ffn_conv_w'], 'm_w_down': out['m_w_down'], 'm_final_norm_w': out['m_final_norm_w'], 'v_norm1_w': out['v_norm1_w'], 'v_w_in': out['v_w_in'], 'v_conv_qkv_w': out['v_conv_qkv_w'], 'v_a_log': out['v_a_log'], 'v_dt_bias': out['v_dt_bias'], 'v_gdn_norm_w': out['v_gdn_norm_w'], 'v_w_out': out['v_w_out'], 'v_norm2_w': out['v_norm2_w'], 'v_w_up': out['v_w_up'], 'v_ffn_conv_w': out['v_ffn_conv_w'], 'v_w_down': out['v_w_down'], 'v_final_norm_w': out['v_final_norm_w']}


def _loss(weights, diff, rest, loss_target):
    with _jax.named_scope("forward"):
        args = {**rest, TWIN_DIFF_INPUT: diff, **{k: w.astype(_WEIGHT_DTYPES[k]) for k, w in weights.items()}}
        y = _forward(args)
    with _jax.named_scope("loss_head"):
        err = _jnp.square(y.astype(_jnp.float32) - loss_target)
        return 0.5 * _jnp.sum(_jnp.mean(err, axis=-1)) if err.ndim else 0.5 * err


def _adamw(w, g, m, v):
    m = ADAM_B1 * m + (1.0 - ADAM_B1) * g
    v = ADAM_B2 * v + (1.0 - ADAM_B2) * _jnp.square(g)
    m_hat = m / (1.0 - ADAM_B1 ** ADAM_STEP)
    v_hat = v / (1.0 - ADAM_B2 ** ADAM_STEP)
    delta = -ADAM_LR * (m_hat / (_jnp.sqrt(v_hat) + ADAM_EPS) + ADAM_WD * w)
    return delta, m, v


def reference(x, norm1_w, w_in, conv_qkv_w, a_log, dt_bias, gdn_norm_w, w_out, norm2_w, w_up, ffn_conv_w, w_down, final_norm_w, loss_target, m_norm1_w, m_w_in, m_conv_qkv_w, m_a_log, m_dt_bias, m_gdn_norm_w, m_w_out, m_norm2_w, m_w_up, m_ffn_conv_w, m_w_down, m_final_norm_w, v_norm1_w, v_w_in, v_conv_qkv_w, v_a_log, v_dt_bias, v_gdn_norm_w, v_w_out, v_norm2_w, v_w_up, v_ffn_conv_w, v_w_down, v_final_norm_w):
    given = dict(x=x, norm1_w=norm1_w, w_in=w_in, conv_qkv_w=conv_qkv_w, a_log=a_log, dt_bias=dt_bias, gdn_norm_w=gdn_norm_w, w_out=w_out, norm2_w=norm2_w, w_up=w_up, ffn_conv_w=ffn_conv_w, w_down=w_down, final_norm_w=final_norm_w, loss_target=loss_target, m_norm1_w=m_norm1_w, m_w_in=m_w_in, m_conv_qkv_w=m_conv_qkv_w, m_a_log=m_a_log, m_dt_bias=m_dt_bias, m_gdn_norm_w=m_gdn_norm_w, m_w_out=m_w_out, m_norm2_w=m_norm2_w, m_w_up=m_w_up, m_ffn_conv_w=m_ffn_conv_w, m_w_down=m_w_down, m_final_norm_w=m_final_norm_w, v_norm1_w=v_norm1_w, v_w_in=v_w_in, v_conv_qkv_w=v_conv_qkv_w, v_a_log=v_a_log, v_dt_bias=v_dt_bias, v_gdn_norm_w=v_gdn_norm_w, v_w_out=v_w_out, v_norm2_w=v_norm2_w, v_w_up=v_w_up, v_ffn_conv_w=v_ffn_conv_w, v_w_down=v_w_down, v_final_norm_w=v_final_norm_w)
    weights = {n: given[n] for n in TWIN_WEIGHTS}
    shared = {n: given[n] for n in SHARED_INPUTS}
    per_example = {n: given[n] for n in ['x']}
    grad_fn = _jax.value_and_grad(_loss, argnums=(0, 1))

    def one_microbatch(ex, loss_target):
        ex = dict(ex)
        diff = ex.pop(TWIN_DIFF_INPUT)
        return grad_fn(weights, diff, {**shared, **ex}, loss_target)

    if N_MICROBATCH == 1:
        loss, (grad_w, grad_x) = one_microbatch(per_example, given["loss_target"])
    else:
        def body(carry, xs):
            loss_sum, grad_sum = carry
            l_k, (gw_k, gx_k) = one_microbatch(xs[0], xs[1])
            with _jax.named_scope("update"):
                return (loss_sum + l_k, _jax.tree.map(_jnp.add, grad_sum, gw_k)), gx_k

        init = (_jnp.zeros((), _jnp.float32), _jax.tree.map(_jnp.zeros_like, weights))
        (loss, grad_w), grad_x = _jax.lax.scan(body, init, (per_example, given["loss_target"]))
    with _jax.named_scope("update"):
        delta_w, new_m, new_v = {}, {}, {}
        for n in TWIN_WEIGHTS:
            delta_w[n], new_m[n], new_v[n] = _adamw(weights[n], grad_w[n], given["m_" + n], given["v_" + n])
    return (loss, grad_x, *[grad_w[n] for n in TWIN_WEIGHTS], *[delta_w[n] for n in TWIN_WEIGHTS],
            *[new_m[n] for n in TWIN_WEIGHTS], *[new_v[n] for n in TWIN_WEIGHTS])
```

```python
import functools
import math

import numpy as np
import jax
import jax.numpy as jnp
from jax import lax
from jax.experimental import pallas as pl
from jax.experimental.pallas import tpu as pltpu

F32 = jnp.float32
BF16 = jnp.bfloat16
_MXU = jnp.bfloat16
_HI = lax.Precision.HIGHEST
EPS = 1e-6
V7X_VMEM_LIMIT = 56 * 1024 * 1024
MESH = pl.DeviceIdType.MESH

D_MODEL = 1024
GDN_HEADS, GDN_DIM, GDN_CHUNK, GDN_CONV = 4, 128, 64, 4
GDN_WIDTH = GDN_HEADS * GDN_DIM
DIL_HEADS, DIL_DIM = 8, 64
DIL_WIDTH = DIL_HEADS * DIL_DIM
D_FF, FFN_CONV = 2816, 3
IN_COLS = 3592
P_COLS = 3840
P_Z, P_QKVB, P_BA = 1536, 2048, 3584
ATT_T = 256
ADAM_LR, ADAM_B1, ADAM_B2, ADAM_EPS, ADAM_WD, ADAM_STEP = 0.001, 0.9, 0.999, 1e-08, 0.01, 10
N_CHIPS = 4


def _cparams(sem=None, vmem=None):
    kw = {}
    if sem is not None:
        kw["dimension_semantics"] = sem
    if vmem is not None:
        kw["vmem_limit_bytes"] = vmem
    return pltpu.CompilerParams(**kw)


def _silu(x):
    return x * jax.nn.sigmoid(x)


def _pick_tile(n, cap):
    best = None
    for t in range(128, min(n, cap) + 1, 128):
        if n % t == 0:
            best = t
    return best or n


def _mm(a, b, mode, *, out_dtype=F32, residual=None, name):
    if mode == "nn":
        (M, K), (_, N) = a.shape, b.shape
    elif mode == "nt":
        (M, K), (N, _) = a.shape, b.shape
    else:
        (K, M), (_, N) = a.shape, b.shape
    tm, tn = _pick_tile(M, 1024), _pick_tile(N, 1536)

    def vmem(tm, tn):
        return 2 * (tm * K * a.dtype.itemsize + tn * K * b.dtype.itemsize
                    + tm * tn * (jnp.dtype(out_dtype).itemsize + (4 if residual is not None else 0))) + 3 * tm * tn * 4

    while vmem(tm, tn) > 40 * 1024 * 1024:
        if tm >= tn and tm % 256 == 0:
            tm //= 2
        elif tn % 256 == 0:
            tn //= 2
        else:
            tm //= 2
    a_spec = pl.BlockSpec((K, tm), lambda j, i: (0, i)) if mode == "tn" else pl.BlockSpec((tm, K), lambda j, i: (i, 0))
    b_spec = pl.BlockSpec((tn, K), lambda j, i: (j, 0)) if mode == "nt" else pl.BlockSpec((K, tn), lambda j, i: (0, j))
    o_spec = pl.BlockSpec((tm, tn), lambda j, i: (i, j))
    dims = {"nn": (((1,), (0,)), ((), ())), "nt": (((1,), (1,)), ((), ())), "tn": (((0,), (0,)), ((), ()))}[mode]

    def body(*refs):
        a_ref, b_ref = refs[0], refs[1]
        o_ref = refs[-1]
        acc = lax.dot_general(a_ref[...].astype(_MXU), b_ref[...].astype(_MXU), dims, preferred_element_type=F32)
        if residual is not None:
            acc = acc + refs[2][...]
        o_ref[...] = acc.astype(out_dtype)

    ins, specs = [a, b], [a_spec, b_spec]
    if residual is not None:
        ins.append(residual)
        specs.append(o_spec)
    return pl.pallas_call(
        body, name=name, grid=(N // tn, M // tm), in_specs=specs, out_specs=o_spec,
        out_shape=jax.ShapeDtypeStruct((M, N), out_dtype),
        compiler_params=_cparams(("parallel", "parallel"), V7X_VMEM_LIMIT),
    )(*ins)


def _rmsnorm_fwd(x, w, name):
    S, D = x.shape
    T = _pick_tile(S, 512)

    def body(x_ref, w_ref, o_ref):
        xv = x_ref[...]
        rs = lax.rsqrt(jnp.mean(xv * xv, axis=-1, keepdims=True) + EPS)
        o_ref[...] = (xv * rs * w_ref[...]).astype(o_ref.dtype)

    return pl.pallas_call(
        body, name=name, grid=(S // T,),
        in_specs=[pl.BlockSpec((T, D), lambda i: (i, 0)), pl.BlockSpec((1, D), lambda i: (0, 0))],
        out_specs=pl.BlockSpec((T, D), lambda i: (i, 0)),
        out_shape=jax.ShapeDtypeStruct((S, D), _MXU),
        compiler_params=_cparams(("parallel",)),
    )(x, w)


def _rmsnorm_bwd(dh, x, w, dres, name):
    S, D = x.shape
    T = _pick_tile(S, 512)

    def body(dh_ref, x_ref, w_ref, dres_ref, dx_ref, dw_ref):
        xv = x_ref[...]
        rs = lax.rsqrt(jnp.mean(xv * xv, axis=-1, keepdims=True) + EPS)
        xn = xv * rs
        dhv = dh_ref[...]
        dxn = dhv * w_ref[...]
        dx_ref[...] = dres_ref[...] + rs * (dxn - xn * jnp.mean(dxn * xn, axis=-1, keepdims=True))

        @pl.when(pl.program_id(0) == 0)
        def _():
            dw_ref[...] = jnp.zeros_like(dw_ref)

        dw_ref[...] += jnp.sum(dhv * xn, axis=0, keepdims=True)

    row = pl.BlockSpec((T, D), lambda i: (i, 0))
    vec = pl.BlockSpec((1, D), lambda i: (0, 0))
    return pl.pallas_call(
        body, name=name, grid=(S // T,), in_specs=[row, row, vec, row], out_specs=(row, vec),
        out_shape=(jax.ShapeDtypeStruct((S, D), F32), jax.ShapeDtypeStruct((1, D), F32)),
        compiler_params=_cparams(("arbitrary",)),
    )(dh, x, w, dres)


def _loss_head(x3, w, tgt, name):
    S, D = x3.shape
    T = _pick_tile(S, 512)

    def body(x_ref, w_ref, t_ref, loss_ref, dx_ref, dw_ref):
        xv = x_ref[...]
        rs = lax.rsqrt(jnp.mean(xv * xv, axis=-1, keepdims=True) + EPS)
        xn = xv * rs
        err = xn * w_ref[...] - t_ref[...]
        dy = err * (1.0 / D)
        dxn = dy * w_ref[...]
        dx_ref[...] = rs * (dxn - xn * jnp.mean(dxn * xn, axis=-1, keepdims=True))

        @pl.when(pl.program_id(0) == 0)
        def _():
            dw_ref[...] = jnp.zeros_like(dw_ref)
            loss_ref[...] = jnp.zeros_like(loss_ref)

        dw_ref[...] += jnp.sum(dy * xn, axis=0, keepdims=True)
        part = jnp.sum(jnp.sum(err * err, axis=-1, keepdims=True), axis=0, keepdims=True) * (0.5 / D)
        loss_ref[...] += jnp.broadcast_to(part, loss_ref.shape)

    row = pl.BlockSpec((T, D), lambda i: (i, 0))
    vec = pl.BlockSpec((1, D), lambda i: (0, 0))
    return pl.pallas_call(
        body, name=name, grid=(S // T,), in_specs=[row, vec, row],
        out_specs=(pl.BlockSpec((8, 128), lambda i: (0, 0)), row, vec),
        out_shape=(jax.ShapeDtypeStruct((8, 128), F32), jax.ShapeDtypeStruct((S, D), F32), jax.ShapeDtypeStruct((1, D), F32)),
        compiler_params=_cparams(("arbitrary",)),
    )(x3, w, tgt)


def _conv_taps(ext, w, K, T):
    out = None
    for i in range(K):
        lo = 8 - (K - 1) + i
        term = ext[lo:lo + T, :] * w[i:i + 1, :]
        out = term if out is None else out + term
    return out


def _conv_taps_t(ext, w, K, T):
    out = None
    for i in range(K):
        lo = (K - 1) - i
        term = ext[lo:lo + T, :] * w[i:i + 1, :]
        out = term if out is None else out + term
    return out


def _tri_masks(C):
    r = lax.broadcasted_iota(jnp.int32, (C, C), 0)
    c = lax.broadcasted_iota(jnp.int32, (C, C), 1)
    return r == c, r >= c, r > c, r <= c


def _dot(a, b, dims):
    return lax.dot_general(a, b, (dims, ((), ())), precision=_HI, preferred_element_type=F32)


_NN, _NT, _TN = ((1,), (0,)), ((1,), (1,)), ((0,), (0,))


def _softplus(x):
    return jnp.maximum(x, 0.0) + jnp.log(1.0 + jnp.exp(-jnp.abs(x)))


def _gdn_stage1(cq, ck, cv, b_col, a_col, alog, dtb):
    C = cq.shape[0]
    eye, incl, strict, incl_t = _tri_masks(C)
    qn = cq * lax.rsqrt(jnp.sum(cq * cq, axis=-1, keepdims=True) + EPS) * (GDN_DIM ** -0.5)
    kn = ck * lax.rsqrt(jnp.sum(ck * ck, axis=-1, keepdims=True) + EPS)
    beta = jax.nn.sigmoid(b_col)
    g = -jnp.exp(alog) * _softplus(a_col + dtb)
    g_row = jnp.sum(jnp.where(eye, g, 0.0), axis=0, keepdims=True)
    beta_row = jnp.sum(jnp.where(eye, beta, 0.0), axis=0, keepdims=True)
    gc_col = jnp.sum(jnp.where(incl, g_row, 0.0), axis=1, keepdims=True)
    gc_row = jnp.sum(jnp.where(incl_t, g, 0.0), axis=0, keepdims=True)
    dec = jnp.where(incl, jnp.exp(jnp.where(incl, gc_col - gc_row, 0.0)), 0.0)
    kk = _dot(kn, kn, _NT)
    qk = _dot(qn, kn, _NT)
    lmat = jnp.where(strict, dec * kk * beta_row, 0.0)
    attn = dec * qk * beta_row
    gam = jnp.exp(gc_col)
    gc_last = gc_col[C - 1:C, :]
    k_end = kn * (jnp.exp(gc_last - gc_col) * beta)
    return lmat, cv, gam * kn, gam * qn, attn, k_end, jnp.exp(gc_last)


def _tri_inv(lmat):
    C = lmat.shape[0]
    eye = _tri_masks(C)[0]
    p = jnp.where(eye, 1.0, 0.0) - lmat
    lp = _dot(lmat, lmat, _NN)
    n = int(math.log2(C))
    for s in range(1, n):
        p = p + _dot(p, lp, _NN)
        if s < n - 1:
            lp = _dot(lp, lp, _NN)
    return p


def _gated_norm(o, z, gnw):
    on = o * lax.rsqrt(jnp.mean(o * o, axis=-1, keepdims=True) + EPS) * gnw
    return on * _silu(z)


def _gdn_conv(prev_ref, cur_ref, cw_ref, first):
    prev = prev_ref[...] * jnp.where(first, 0.0, 1.0)
    ext = jnp.concatenate([prev, cur_ref[...]], axis=0)
    return _conv_taps(ext, cw_ref[...], GDN_CONV, GDN_CHUNK)


def _gdn_specs(nc, rev):
    C = GDN_CHUNK
    ci = (lambda i: nc - 1 - i) if rev else (lambda i: i)
    return dict(
        cur=pl.BlockSpec((C, 3 * GDN_WIDTH), lambda i: (ci(i), 0)),
        prev=pl.BlockSpec((8, 3 * GDN_WIDTH), lambda i: (jnp.maximum(ci(i) * (C // 8) - 1, 0), 0)),
        z=pl.BlockSpec((C, GDN_WIDTH), lambda i: (ci(i), P_Z // GDN_WIDTH)),
        ba=pl.BlockSpec((C, 128), lambda i: (ci(i), P_BA // 128)),
        cw=pl.BlockSpec((GDN_CONV, 3 * GDN_WIDTH), lambda i: (0, 0)),
        vec=pl.BlockSpec((1, 128), lambda i: (0, 0)),
        oa=pl.BlockSpec((C, GDN_WIDTH), lambda i: (ci(i), 0)),
        st=pl.BlockSpec((1, GDN_HEADS, GDN_DIM, GDN_DIM), lambda i: (ci(i), 0, 0, 0)),
    )


def _gdn_fwd(proj, conv_w, gp, gnw):
    S = proj.shape[0]
    nc = S // GDN_CHUNK
    sp = _gdn_specs(nc, False)

    def body(cur_ref, prev_ref, z_ref, ba_ref, cw_ref, gp_ref, gnw_ref, oa_ref, st_ref, s_scr):
        i = pl.program_id(0)

        @pl.when(i == 0)
        def _():
            s_scr[...] = jnp.zeros_like(s_scr)

        c = _silu(_gdn_conv(prev_ref, cur_ref, cw_ref, i == 0))
        ba = ba_ref[...]
        for h in range(GDN_HEADS):
            lo = h * GDN_DIM
            cq, ck, cv = (c[:, o + lo:o + lo + GDN_DIM] for o in (0, GDN_WIDTH, 2 * GDN_WIDTH))
            lmat, v, rk, q_dec, attn, k_end, g_end = _gdn_stage1(
                cq, ck, cv, ba[:, h:h + 1], ba[:, 4 + h:5 + h], gp_ref[0:1, h:h + 1], gp_ref[0:1, 4 + h:5 + h])
            t = _tri_inv(lmat)
            st = s_scr[h]
            st_ref[0, h] = st
            u = _dot(t, v, _NN) - _dot(_dot(t, rk, _NN), st, _NN)
            o = _dot(q_dec, st, _NN) + _dot(attn, u, _NN)
            s_scr[h] = g_end * st + _dot(k_end, u, _TN)
            oa_ref[:, lo:lo + GDN_DIM] = _gated_norm(o, z_ref[:, lo:lo + GDN_DIM], gnw_ref[...])

    return pl.pallas_call(
        body, name="gdn_fwd", grid=(nc,),
        in_specs=[sp["cur"], sp["prev"], sp["z"], sp["ba"], sp["cw"], sp["vec"], sp["vec"]],
        out_specs=(sp["oa"], sp["st"]),
        out_shape=(jax.ShapeDtypeStruct((S, GDN_WIDTH), F32),
                   jax.ShapeDtypeStruct((nc, GDN_HEADS, GDN_DIM, GDN_DIM), F32)),
        scratch_shapes=[pltpu.VMEM((GDN_HEADS, GDN_DIM, GDN_DIM), F32)],
        compiler_params=_cparams(("arbitrary",)),
    )(proj, proj, proj, proj, conv_w, gp, gnw)


def _gdn_bwd(proj, conv_w, gp, gnw, states, d_oa):
    S = proj.shape[0]
    nc = S // GDN_CHUNK
    sp = _gdn_specs(nc, True)
    C = GDN_CHUNK

    def body(cur_ref, prev_ref, z_ref, ba_ref, cw_ref, gp_ref, gnw_ref, st_ref, doa_ref,
             dpre_ref, dz_ref, dba_ref, dgp_ref, dgnw_ref, ds_scr):
        i = pl.program_id(0)

        @pl.when(i == 0)
        def _():
            ds_scr[...] = jnp.zeros_like(ds_scr)
            dgp_ref[...] = jnp.zeros_like(dgp_ref)
            dgnw_ref[...] = jnp.zeros_like(dgnw_ref)

        pre = _gdn_conv(prev_ref, cur_ref, cw_ref, i == nc - 1)
        sg = jax.nn.sigmoid(pre)
        c = pre * sg
        dsilu = sg * (1.0 + pre * (1.0 - sg))
        ba = ba_ref[...]
        lane = lax.broadcasted_iota(jnp.int32, (C, 128), 1)
        lane1 = lax.broadcasted_iota(jnp.int32, (1, 128), 1)
        dba = jnp.zeros((C, 128), F32)
        dgp = jnp.zeros((1, 128), F32)
        dgnw = jnp.zeros((1, 128), F32)
        for h in range(GDN_HEADS):
            lo = h * GDN_DIM
            cq, ck, cv = (c[:, o + lo:o + lo + GDN_DIM] for o in (0, GDN_WIDTH, 2 * GDN_WIDTH))
            args = (cq, ck, cv, ba[:, h:h + 1], ba[:, 4 + h:5 + h], gp_ref[0:1, h:h + 1], gp_ref[0:1, 4 + h:5 + h])
            (lmat, v, rk, q_dec, attn, k_end, g_end), vjp1 = jax.vjp(_gdn_stage1, *args)
            t = _tri_inv(lmat)
            st = st_ref[0, h]
            u_v, w_k = _dot(t, v, _NN), _dot(t, rk, _NN)
            u = u_v - _dot(w_k, st, _NN)
            o = _dot(q_dec, st, _NN) + _dot(attn, u, _NN)
            zh = z_ref[:, lo:lo + GDN_DIM]
            _, vjp2 = jax.vjp(_gated_norm, o, zh, gnw_ref[...])
            do, dzh, dgn = vjp2(doa_ref[:, lo:lo + GDN_DIM])
            dz_ref[:, lo:lo + GDN_DIM] = dzh
            dgnw = dgnw + dgn
            ds_new = ds_scr[h]
            du = _dot(attn, do, _TN) + _dot(k_end, ds_new, _NN)
            d_attn = _dot(do, u, _NT)
            d_qdec = _dot(do, st, _NT)
            d_kend = _dot(u, ds_new, _NT)
            d_ge = jnp.sum(jnp.sum(st * ds_new, axis=1, keepdims=True), axis=0, keepdims=True)
            d_wk = -_dot(du, st, _NT)
            ds_scr[h] = g_end * ds_new + _dot(q_dec, do, _TN) - _dot(w_k, du, _TN)
            d_v = _dot(t, du, _TN)
            d_rk = _dot(t, d_wk, _TN)
            d_l = -(_dot(d_v, u_v, _NT) + _dot(d_rk, w_k, _NT))
            dcq, dck, dcv, db, da, dalog, ddtb = vjp1((d_l, d_v, d_rk, d_qdec, d_attn, d_kend, d_ge))
            for o_, dcx in ((0, dcq), (GDN_WIDTH, dck), (2 * GDN_WIDTH, dcv)):
                dpre_ref[:, o_ + lo:o_ + lo + GDN_DIM] = dcx * dsilu[:, o_ + lo:o_ + lo + GDN_DIM]
            dba = dba + jnp.where(lane == h, db, 0.0) + jnp.where(lane == 4 + h, da, 0.0)
            dgp = dgp + jnp.where(lane1 == h, dalog, 0.0) + jnp.where(lane1 == 4 + h, ddtb, 0.0)
        dba_ref[...] = dba
        dgp_ref[0:1, :] += dgp
        dgnw_ref[...] += dgnw

    return pl.pallas_call(
        body, name="gdn_bwd", grid=(nc,),
        in_specs=[sp["cur"], sp["prev"], sp["z"], sp["ba"], sp["cw"], sp["vec"], sp["vec"], sp["st"], sp["oa"]],
        out_specs=(sp["cur"], sp["oa"], pl.BlockSpec((C, 128), lambda i: (nc - 1 - i, 0)),
                   pl.BlockSpec((8, 128), lambda i: (0, 0)), sp["vec"]),
        out_shape=(jax.ShapeDtypeStruct((S, 3 * GDN_WIDTH), F32), jax.ShapeDtypeStruct((S, GDN_WIDTH), F32),
                   jax.ShapeDtypeStruct((S, 128), F32), jax.ShapeDtypeStruct((8, 128), F32),
                   jax.ShapeDtypeStruct((1, 128), F32)),
        scratch_shapes=[pltpu.VMEM((GDN_HEADS, GDN_DIM, GDN_DIM), F32)],
        compiler_params=_cparams(("arbitrary",)),
    )(proj, proj, proj, proj, conv_w, gp, gnw, states, d_oa)


def _conv_bwd(dpre, x, xcol0, w, K, name, tc):
    S, Cc = dpre.shape
    T = _pick_tile(S, 256)
    nt, ncol = S // T, Cc // tc
    xo = xcol0 // tc

    def body(d_ref, dn_ref, x_ref, xp_ref, w_ref, dx_ref, dw_ref):
        i = pl.program_id(1)
        dn = dn_ref[...] * jnp.where(i == nt - 1, 0.0, 1.0)
        dv = d_ref[...]
        ext_d = jnp.concatenate([dv, dn], axis=0)
        wv = w_ref[...]
        dx_ref[...] = _conv_taps_t(ext_d, wv, K, T).astype(dx_ref.dtype)
        xp = xp_ref[...] * jnp.where(i == 0, 0.0, 1.0)
        ext_x = jnp.concatenate([xp, x_ref[...]], axis=0)

        @pl.when(i == 0)
        def _():
            dw_ref[...] = jnp.zeros_like(dw_ref)

        for k in range(K):
            lo = 8 - (K - 1) + k
            dw_ref[k:k + 1, :] += jnp.sum(dv * ext_x[lo:lo + T, :], axis=0, keepdims=True)

    r8 = T // 8
    return pl.pallas_call(
        body, name=name, grid=(ncol, nt),
        in_specs=[pl.BlockSpec((T, tc), lambda j, i: (i, j)),
                  pl.BlockSpec((8, tc), lambda j, i: (jnp.minimum((i + 1) * r8, S // 8 - 1), j)),
                  pl.BlockSpec((T, tc), lambda j, i: (i, j + xo)),
                  pl.BlockSpec((8, tc), lambda j, i: (jnp.maximum(i * r8 - 1, 0), j + xo)),
                  pl.BlockSpec((K, tc), lambda j, i: (0, j))],
        out_specs=(pl.BlockSpec((T, tc), lambda j, i: (i, j)), pl.BlockSpec((K, tc), lambda j, i: (0, j))),
        out_shape=(jax.ShapeDtypeStruct((S, Cc), _MXU), jax.ShapeDtypeStruct((K, Cc), F32)),
        compiler_params=_cparams(("parallel", "arbitrary")),
    )(dpre, dpre, x, x, w)


def _dil_bias(nt):
    T = ATT_T
    d = (np.arange(nt)[:, None, None] * T + np.arange(T)[None, :, None] - np.arange(T)[None, None, :])
    cnt = ((d >= 0) & (d <= 128)).astype(np.float64) + ((d >= 0) & (d % 4 == 0) & (d <= 512)) + ((d >= 0) & (d % 16 == 0))
    return jnp.asarray(np.where(cnt > 0, np.log(np.maximum(cnt, 1.0)), -1e30), dtype=F32)


def _attn_fwd(proj):
    S = proj.shape[0]
    T = ATT_T
    nt = S // T
    bias = _dil_bias(nt)
    scale = DIL_DIM ** -0.5
    npair = DIL_WIDTH // 128
    qb0, kb0, vb0 = P_QKVB // 128, (P_QKVB + DIL_WIDTH) // 128, (P_QKVB + 2 * DIL_WIDTH) // 128

    def body(q_ref, k_ref, v_ref, b_ref, o_ref, lse_ref):
        i = pl.program_id(1)
        qs = (q_ref[...] * scale).astype(_MXU)

        def step(j, carry):
            kt = k_ref[pl.ds(pl.multiple_of(j * T, T), T), :].astype(_MXU)
            vt = v_ref[pl.ds(pl.multiple_of(j * T, T), T), :].astype(_MXU)
            bt = b_ref[i - j]
            out = []
            for hh in range(2):
                m, l, acc = carry[hh]
                sl = slice(hh * DIL_DIM, (hh + 1) * DIL_DIM)
                s = lax.dot_general(qs[:, sl], kt[:, sl], (_NT, ((), ())), preferred_element_type=F32) + bt
                m_new = jnp.maximum(m, jnp.max(s, axis=-1, keepdims=True))
                p = jnp.exp(s - m_new)
                a = jnp.exp(m - m_new)
                l = a * l + jnp.sum(p, axis=-1, keepdims=True)
                acc = a * acc + lax.dot_general(p.astype(_MXU), vt[:, sl], (_NN, ((), ())), preferred_element_type=F32)
                out.append((m_new, l, acc))
            return tuple(out)

        init = tuple((jnp.full((T, 1), -1e30, F32), jnp.zeros((T, 1), F32), jnp.zeros((T, DIL_DIM), F32)) for _ in range(2))
        res = lax.fori_loop(0, i + 1, step, init)
        for hh in range(2):
            m, l, acc = res[hh]
            sl = slice(hh * DIL_DIM, (hh + 1) * DIL_DIM)
            o_ref[:, sl] = acc / l
            lse_ref[:, sl] = jnp.broadcast_to(m + jnp.log(l), (T, DIL_DIM))

    return pl.pallas_call(
        body, name="attn_fwd", grid=(npair, nt),
        in_specs=[pl.BlockSpec((T, 128), lambda p, i: (i, qb0 + p)),
                  pl.BlockSpec((S, 128), lambda p, i: (0, kb0 + p)),
                  pl.BlockSpec((S, 128), lambda p, i: (0, vb0 + p)),
                  pl.BlockSpec((nt, T, T), lambda p, i: (0, 0, 0))],
        out_specs=(pl.BlockSpec((T, 128), lambda p, i: (i, p)), pl.BlockSpec((T, 128), lambda p, i: (i, p))),
        out_shape=(jax.ShapeDtypeStruct((S, DIL_WIDTH), F32), jax.ShapeDtypeStruct((S, DIL_WIDTH), F32)),
        compiler_params=_cparams(("parallel", "parallel")),
    )(proj, proj, proj, bias)


def _attn_bwd(proj, o_b, lse, d_ob):
    S = proj.shape[0]
    T = ATT_T
    nt = S // T
    bias = _dil_bias(nt)
    scale = DIL_DIM ** -0.5
    npair = DIL_WIDTH // 128
    qb0, kb0, vb0 = P_QKVB // 128, (P_QKVB + DIL_WIDTH) // 128, (P_QKVB + 2 * DIL_WIDTH) // 128

    def body(q_ref, k_ref, v_ref, o_ref, lse_ref, do_ref, b_ref, dq_ref, dk_ref, dv_ref, dq_scr):
        j = pl.program_id(1)

        @pl.when(j == 0)
        def _():
            dq_scr[...] = jnp.zeros_like(dq_scr)

        kt = k_ref[...].astype(_MXU)
        vt = v_ref[...].astype(_MXU)

        def step(i, carry):
            rows = pl.ds(pl.multiple_of(i * T, T), T)
            qs = (q_ref[rows, :] * scale).astype(_MXU)
            dov = do_ref[rows, :]
            prod = dov * o_ref[rows, :]
            lsev = lse_ref[rows, :]
            dob = dov.astype(_MXU)
            bt = b_ref[i - j]
            out = []
            dqs = []
            for hh in range(2):
                dk, dv = carry[hh]
                sl = slice(hh * DIL_DIM, (hh + 1) * DIL_DIM)
                s = lax.dot_general(qs[:, sl], kt[:, sl], (_NT, ((), ())), preferred_element_type=F32) + bt
                p = jnp.exp(s - lsev[:, hh * DIL_DIM:hh * DIL_DIM + 1])
                delta = jnp.sum(prod[:, sl], axis=-1, keepdims=True)
                dp = lax.dot_general(dob[:, sl], vt[:, sl], (_NT, ((), ())), preferred_element_type=F32)
                ds = (p * (dp - delta)).astype(_MXU)
                dv = dv + lax.dot_general(p.astype(_MXU), dob[:, sl], (_TN, ((), ())), preferred_element_type=F32)
                dk = dk + lax.dot_general(ds, qs[:, sl], (_TN, ((), ())), preferred_element_type=F32)
                dqs.append(lax.dot_general(ds, kt[:, sl], (_NN, ((), ())), preferred_element_type=F32) * scale)
                out.append((dk, dv))
            dq_scr[rows, :] += jnp.concatenate(dqs, axis=1)
            return tuple(out)

        init = tuple((jnp.zeros((T, DIL_DIM), F32), jnp.zeros((T, DIL_DIM), F32)) for _ in range(2))
        res = lax.fori_loop(j, nt, step, init)
        dk_ref[...] = jnp.concatenate([res[0][0], res[1][0]], axis=1).astype(dk_ref.dtype)
        dv_ref[...] = jnp.concatenate([res[0][1], res[1][1]], axis=1).astype(dv_ref.dtype)

        @pl.when(j == nt - 1)
        def _():
            dq_ref[...] = dq_scr[...].astype(dq_ref.dtype)

    full = lambda c0: pl.BlockSpec((S, 128), lambda p, j: (0, c0 + p))
    tile = lambda c0: pl.BlockSpec((T, 128), lambda p, j: (j, c0 + p))
    out3 = jax.ShapeDtypeStruct((S, DIL_WIDTH), _MXU)
    return pl.pallas_call(
        body, name="attn_bwd", grid=(npair, nt),
        in_specs=[full(qb0), tile(kb0), tile(vb0), full(0), full(0), full(0),
                  pl.BlockSpec((nt, T, T), lambda p, j: (0, 0, 0))],
        out_specs=(full(0), tile(0), tile(0)),
        out_shape=(out3, out3, out3),
        scratch_shapes=[pltpu.VMEM((S, 128), F32)],
        compiler_params=_cparams(("parallel", "arbitrary")),
    )(proj, proj, proj, o_b, lse, d_ob, bias)


def _ffn_act(up_g, up_u, cw_g, cw_u):
    S, Cc = up_g.shape
    T, tc = _pick_tile(S, 256), _pick_tile(Cc, 1536)
    r8 = T // 8

    def body(g_ref, gp_ref, u_ref, up_ref, wg_ref, wu_ref, o_ref):
        keep = jnp.where(pl.program_id(1) == 0, 0.0, 1.0)
        cg = _conv_taps(jnp.concatenate([gp_ref[...] * keep, g_ref[...]], axis=0), wg_ref[...], FFN_CONV, T)
        cu = _conv_taps(jnp.concatenate([up_ref[...] * keep, u_ref[...]], axis=0), wu_ref[...], FFN_CONV, T)
        o_ref[...] = (_silu(cg) * cu).astype(o_ref.dtype)

    cur = pl.BlockSpec((T, tc), lambda j, i: (i, j))
    prev = pl.BlockSpec((8, tc), lambda j, i: (jnp.maximum(i * r8 - 1, 0), j))
    wsp = pl.BlockSpec((FFN_CONV, tc), lambda j, i: (0, j))
    return pl.pallas_call(
        body, name="ffn_act", grid=(Cc // tc, S // T),
        in_specs=[cur, prev, cur, prev, wsp, wsp], out_specs=cur,
        out_shape=jax.ShapeDtypeStruct((S, Cc), _MXU),
        compiler_params=_cparams(("parallel", "parallel")),
    )(up_g, up_g, up_u, up_u, cw_g, cw_u)


def _ffn_act_bwd(d_act, up_g, up_u, cw_g, cw_u):
    S, Cc = up_g.shape
    T, tc = _pick_tile(S, 256), _pick_tile(Cc, 1536)
    r8 = T // 8
    nt = S // T
    K = FFN_CONV

    def body(da_ref, dan_ref, g_ref, gp_ref, gn_ref, u_ref, up_ref, un_ref, wg_ref, wu_ref,
             dg_ref, du_ref, dwg_ref, dwu_ref):
        i = pl.program_id(1)
        keep_p = jnp.where(i == 0, 0.0, 1.0)
        keep_n = jnp.where(i == nt - 1, 0.0, 1.0)
        wg, wu = wg_ref[...], wu_ref[...]
        xg = jnp.concatenate([gp_ref[...] * keep_p, g_ref[...], gn_ref[...] * keep_n], axis=0)
        xu = jnp.concatenate([up_ref[...] * keep_p, u_ref[...], un_ref[...] * keep_n], axis=0)
        cg = _conv_taps(xg, wg, K, T + 8)
        cu = _conv_taps(xu, wu, K, T + 8)
        da = jnp.concatenate([da_ref[...], dan_ref[...] * keep_n], axis=0)
        sg = jax.nn.sigmoid(cg)
        d_cg = da * cu * (sg * (1.0 + cg * (1.0 - sg)))
        d_cu = da * (cg * sg)
        dg_ref[...] = _conv_taps_t(d_cg, wg, K, T).astype(dg_ref.dtype)
        du_ref[...] = _conv_taps_t(d_cu, wu, K, T).astype(du_ref.dtype)

        @pl.when(i == 0)
        def _():
            dwg_ref[...] = jnp.zeros_like(dwg_ref)
            dwu_ref[...] = jnp.zeros_like(dwu_ref)

        for k in range(K):
            lo = 8 - (K - 1) + k
            dwg_ref[k:k + 1, :] += jnp.sum(d_cg[0:T, :] * xg[lo:lo + T, :], axis=0, keepdims=True)
            dwu_ref[k:k + 1, :] += jnp.sum(d_cu[0:T, :] * xu[lo:lo + T, :], axis=0, keepdims=True)

    cur = pl.BlockSpec((T, tc), lambda j, i: (i, j))
    prev = pl.BlockSpec((8, tc), lambda j, i: (jnp.maximum(i * r8 - 1, 0), j))
    nxt = pl.BlockSpec((8, tc), lambda j, i: (jnp.minimum((i + 1) * r8, S // 8 - 1), j))
    wsp = pl.BlockSpec((K, tc), lambda j, i: (0, j))
    return pl.pallas_call(
        body, name="ffn_act_bwd", grid=(Cc // tc, nt),
        in_specs=[cur, nxt, cur, prev, nxt, cur, prev, nxt, wsp, wsp],
        out_specs=(cur, cur, wsp, wsp),
        out_shape=(jax.ShapeDtypeStruct((S, Cc), _MXU), jax.ShapeDtypeStruct((S, Cc), _MXU),
                   jax.ShapeDtypeStruct((K, Cc), F32), jax.ShapeDtypeStruct((K, Cc), F32)),
        compiler_params=_cparams(("parallel", "arbitrary")),
    )(d_act, d_act, up_g, up_g, up_g, up_u, up_u, up_u, cw_g, cw_u)


def _local_step(x, tgt, n1w, n2w, fnw, gp, gnw, wp, conv_w, w_out, w_g, w_u, fcw_g, fcw_u, w_down):
    h1 = _rmsnorm_fwd(x, n1w, "norm1")
    proj = _mm(h1, wp, "nn", name="proj")
    o_a, states = _gdn_fwd(proj, conv_w, gp, gnw)
    o_b, lse = _attn_fwd(proj)
    x2 = _mm(o_b, w_out[GDN_WIDTH:], "nn", residual=_mm(o_a, w_out[:GDN_WIDTH], "nn", residual=x, name="outproj_a"),
             name="outproj_b")
    h2 = _rmsnorm_fwd(x2, n2w, "norm2")
    up_g = _mm(h2, w_g, "nn", name="up_gate")
    up_u = _mm(h2, w_u, "nn", name="up_up")
    act = _ffn_act(up_g, up_u, fcw_g, fcw_u)
    x3 = _mm(act, w_down, "nn", residual=x2, name="down")
    loss, dx3, d_fnw = _loss_head(x3, fnw, tgt, "loss_head")
    d_act = _mm(dx3, w_down, "nt", name="d_act")
    d_wdown = _mm(act, dx3, "tn", name="d_wdown")
    d_upg, d_upu, d_fcwg, d_fcwu = _ffn_act_bwd(d_act, up_g, up_u, fcw_g, fcw_u)
    d_wg = _mm(h2, d_upg, "tn", name="d_wgate")
    d_wu = _mm(h2, d_upu, "tn", name="d_wup")
    d_h2 = _mm(d_upu, w_u, "nt", residual=_mm(d_upg, w_g, "nt", name="d_h2_a"), name="d_h2_b")
    dx2, d_n2w = _rmsnorm_bwd(d_h2, x2, n2w, dx3, "norm2_bwd")
    d_oa = _mm(dx2, w_out[:GDN_WIDTH], "nt", name="d_oa")
    d_ob = _mm(dx2, w_out[GDN_WIDTH:], "nt", name="d_ob")
    d_wout_a = _mm(o_a, dx2, "tn", name="d_wout_a")
    d_wout_b = _mm(o_b, dx2, "tn", name="d_wout_b")
    dq_b, dk_b, dv_b = _attn_bwd(proj, o_b, lse, d_ob)
    d_pre, d_z, d_ba, d_gp, d_gnw = _gdn_bwd(proj, conv_w, gp, gnw, states, d_oa)
    d_qkva, d_convw = _conv_bwd(d_pre, proj, 0, conv_w, GDN_CONV, "gdn_conv_bwd", 512)
    d_proj = jnp.concatenate([d_qkva, d_z.astype(_MXU), dq_b, dk_b, dv_b, d_ba.astype(_MXU),
                              jnp.zeros((x.shape[0], P_COLS - P_BA - 128), _MXU)], axis=1)
    d_wp = _mm(h1, d_proj, "tn", name="d_wp")
    d_h1 = _mm(d_proj, wp, "nt", name="d_h1")
    dx, d_n1w = _rmsnorm_bwd(d_h1, x, n1w, dx2, "norm1_bwd")
    grads = dict(wp=d_wp, conv_w=d_convw, w_out_a=d_wout_a, w_out_b=d_wout_b, w_g=d_wg, w_u=d_wu,
                 fcw_g=d_fcwg, fcw_u=d_fcwu, w_down=d_wdown, n1w=d_n1w, n2w=d_n2w, fnw=d_fnw, gp=d_gp, gnw=d_gnw)
    return loss, dx, grads


_HBM = pl.BlockSpec(memory_space=pltpu.HBM)


def _pos():
    return lax.axis_index("x"), lax.axis_index("y"), lax.axis_index("c")


def _other_chips(x, y):
    return [(1 - x, y), (x, 1 - y), (1 - x, 1 - y)]


def _gather_weights(shards):
    n = len(shards)

    def body(*refs):
        ins, outs = refs[:n], refs[n:2 * n]
        send_sems, recv_sems, loc_sems = refs[2 * n:]
        x, y, c = _pos()
        q = 2 * x + y
        chips = _other_chips(x, y)
        local = [pltpu.make_async_copy(ins[a], outs[a].at[q], loc_sems.at[a]) for a in range(n)]
        for cp in local:
            cp.start()

        def copy(a, j, block):
            px, py = chips[j]
            return pltpu.make_async_remote_copy(
                src_ref=ins[a], dst_ref=outs[a].at[block], send_sem=send_sems.at[3 * a + j],
                recv_sem=recv_sems.at[3 * a + j], device_id=(px, py, c), device_id_type=MESH)

        sends = [copy(a, j, q) for a in range(n) for j in range(3)]
        for cp in sends:
            cp.start()
        for a in range(n):
            for j, (px, py) in enumerate(chips):
                copy(a, j, 2 * px + py).wait_recv()
        for cp in sends:
            cp.wait_send()
        for cp in local:
            cp.wait()

    return pl.pallas_call(
        body, name="gather_weights", in_specs=[_HBM] * n, out_specs=[_HBM] * n,
        out_shape=[jax.ShapeDtypeStruct((N_CHIPS,) + s.shape, s.dtype) for s in shards],
        scratch_shapes=[pltpu.SemaphoreType.DMA((3 * n,)), pltpu.SemaphoreType.DMA((3 * n,)),
                        pltpu.SemaphoreType.DMA((n,))],
    )(*shards)


def _sibling_exchange(send, small):
    def body(send_ref, small_ref, recv_ref, all_ref, send_sems, recv_sems, loc_sem):
        x, y, c = _pos()
        me = 4 * x + 2 * y + c
        mine = pltpu.make_async_copy(small_ref, all_ref.at[me], loc_sem)
        mine.start()
        big = pltpu.make_async_remote_copy(src_ref=send_ref, dst_ref=recv_ref, send_sem=send_sems.at[0],
                                           recv_sem=recv_sems.at[0], device_id=(x, y, 1 - c), device_id_type=MESH)
        big.start()

        def peer(r):
            dx, dy, dc = (r >> 2) & 1, (r >> 1) & 1, r & 1
            px = x if dx == 0 else 1 - x
            py = y if dy == 0 else 1 - y
            pc = c if dc == 0 else 1 - c
            return px, py, pc

        def small_copy(r, slot):
            return pltpu.make_async_remote_copy(src_ref=small_ref, dst_ref=all_ref.at[slot], send_sem=send_sems.at[r],
                                                recv_sem=recv_sems.at[r], device_id=peer(r), device_id_type=MESH)

        sends = [small_copy(r, me) for r in range(1, 8)]
        for cp in sends:
            cp.start()
        for r in range(1, 8):
            px, py, pc = peer(r)
            small_copy(r, 4 * px + 2 * py + pc).wait_recv()
        big.wait_recv()
        big.wait_send()
        for cp in sends:
            cp.wait_send()
        mine.wait()

    return pl.pallas_call(
        body, name="sibling_exchange", in_specs=[_HBM, _HBM], out_specs=[_HBM, _HBM],
        out_shape=[jax.ShapeDtypeStruct(send.shape, send.dtype), jax.ShapeDtypeStruct((8,) + small.shape, small.dtype)],
        scratch_shapes=[pltpu.SemaphoreType.DMA((8,)), pltpu.SemaphoreType.DMA((8,)), pltpu.SemaphoreType.DMA],
    )(send, small)


def _chip_exchange(part):
    def body(part_ref, recv_ref, send_sems, recv_sems):
        x, y, c = _pos()
        chips = _other_chips(x, y)

        def copy(j):
            px, py = chips[j]
            return pltpu.make_async_remote_copy(src_ref=part_ref.at[2 * px + py], dst_ref=recv_ref.at[j],
                                                send_sem=send_sems.at[j], recv_sem=recv_sems.at[j],
                                                device_id=(px, py, c), device_id_type=MESH)

        sends = [copy(j) for j in range(3)]
        for cp in sends:
            cp.start()
        for j in range(3):
            copy(j).wait_recv()
        for cp in sends:
            cp.wait_send()

    return pl.pallas_call(
        body, name="chip_exchange", in_specs=[_HBM], out_specs=_HBM,
        out_shape=jax.ShapeDtypeStruct((3,) + part.shape[1:], part.dtype),
        scratch_shapes=[pltpu.SemaphoreType.DMA((3,)), pltpu.SemaphoreType.DMA((3,))],
    )(part)


def _sibling_share(half):
    def body(half_ref, full_ref, send_sem, recv_sem, loc_sem):
        x, y, c = _pos()
        mine = pltpu.make_async_copy(half_ref, full_ref.at[c], loc_sem)
        mine.start()
        cp = pltpu.make_async_remote_copy(src_ref=half_ref, dst_ref=full_ref.at[c], send_sem=send_sem, recv_sem=recv_sem,
                                          device_id=(x, y, 1 - c), device_id_type=MESH)
        cp.start()
        pltpu.make_async_remote_copy(src_ref=half_ref, dst_ref=full_ref.at[1 - c], send_sem=send_sem, recv_sem=recv_sem,
                                     device_id=(x, y, 1 - c), device_id_type=MESH).wait_recv()
        cp.wait_send()
        mine.wait()

    return pl.pallas_call(
        body, name="sibling_share", in_specs=[_HBM], out_specs=_HBM,
        out_shape=jax.ShapeDtypeStruct((2,) + half.shape, half.dtype),
        scratch_shapes=[pltpu.SemaphoreType.DMA, pltpu.SemaphoreType.DMA, pltpu.SemaphoreType.DMA],
    )(half)


def _add_sibling(own, recv):
    nb, Rh, Wd = own.shape
    T = Rh // 2

    def body(a_ref, b_ref, o32_ref, o16_ref):
        s = a_ref[...] + b_ref[...].astype(F32)
        o32_ref[...] = s
        o16_ref[...] = s.astype(o16_ref.dtype)

    sp = pl.BlockSpec((1, T, Wd), lambda b, i: (b, i, 0))
    return pl.pallas_call(
        body, name="add_sibling", grid=(nb, Rh // T), in_specs=[sp, sp], out_specs=(sp, sp),
        out_shape=(jax.ShapeDtypeStruct(own.shape, F32), jax.ShapeDtypeStruct(own.shape, recv.dtype)),
        compiler_params=_cparams(("parallel", "parallel")),
    )(own, recv)


def _add_chips(own, recv3, small_all):
    Rh, Wd = own.shape
    T = Rh // 2

    def body(a_ref, b_ref, s_ref, o_ref, so_ref):
        acc = a_ref[...]
        for j in range(3):
            acc = acc + b_ref[j].astype(F32)
        o_ref[...] = acc
        tot = s_ref[0]
        for d in range(1, 8):
            tot = tot + s_ref[d]
        so_ref[...] = tot

    return pl.pallas_call(
        body, name="add_chips", grid=(Rh // T,),
        in_specs=[pl.BlockSpec((T, Wd), lambda i: (i, 0)), pl.BlockSpec((3, T, Wd), lambda i: (0, i, 0)),
                  pl.BlockSpec(small_all.shape, lambda i: (0, 0, 0))],
        out_specs=(pl.BlockSpec((T, Wd), lambda i: (i, 0)), pl.BlockSpec(small_all.shape[1:], lambda i: (0, 0))),
        out_shape=(jax.ShapeDtypeStruct(own.shape, F32), jax.ShapeDtypeStruct(small_all.shape[1:], F32)),
        compiler_params=_cparams(("arbitrary",)),
    )(own, recv3, small_all)


def _adamw(w, g, m, v, name):
    R, Cc = w.shape
    T = max([t for t in range(8, 257, 8) if R % t == 0], default=R)
    c1 = 1.0 / (1.0 - ADAM_B1 ** ADAM_STEP)
    c2 = 1.0 / (1.0 - ADAM_B2 ** ADAM_STEP)

    def body(w_ref, g_ref, m_ref, v_ref, d_ref, mo_ref, vo_ref):
        gv = g_ref[...]
        mn = ADAM_B1 * m_ref[...] + (1.0 - ADAM_B1) * gv
        vn = ADAM_B2 * v_ref[...] + (1.0 - ADAM_B2) * (gv * gv)
        mo_ref[...] = mn
        vo_ref[...] = vn
        d_ref[...] = -ADAM_LR * ((mn * c1) / (jnp.sqrt(vn * c2) + ADAM_EPS) + ADAM_WD * w_ref[...])

    sp = pl.BlockSpec((T, Cc), lambda i: (i, 0))
    sh = jax.ShapeDtypeStruct((R, Cc), F32)
    return pl.pallas_call(
        body, name=name, grid=(R // T,), in_specs=[sp] * 4, out_specs=(sp, sp, sp), out_shape=(sh, sh, sh),
        compiler_params=_cparams(("parallel",)),
    )(w, g, m, v)


SEC_ROWS = 3328
SMALL_ROWS = 8


def _pad_lanes(v, n=D_MODEL):
    return jnp.pad(v, ((0, 0), (0, n - v.shape[1])))


def kernel(x, norm1_w, w_in, conv_qkv_w, a_log, dt_bias, gdn_norm_w, w_out, norm2_w, w_up, ffn_conv_w, w_down, final_norm_w, loss_target, m_norm1_w, m_w_in, m_conv_qkv_w, m_a_log, m_dt_bias, m_gdn_norm_w, m_w_out, m_norm2_w, m_w_up, m_ffn_conv_w, m_w_down, m_final_norm_w, v_norm1_w, v_w_in, v_conv_qkv_w, v_a_log, v_dt_bias, v_gdn_norm_w, v_w_out, v_norm2_w, v_w_up, v_ffn_conv_w, v_w_down, v_final_norm_w):
    c = lax.axis_index("c")
    q = 2 * lax.axis_index("x") + lax.axis_index("y")
    S = x.shape[1]
    g_in, g_out, g_up, g_down, g_conv, g_fconv = _gather_weights(
        [w_in[0].astype(_MXU), w_out[0].astype(_MXU), w_up[0].astype(_MXU), w_down[0].astype(_MXU),
         conv_qkv_w[0], ffn_conv_w[0]])
    w_in_f = jnp.concatenate([g_in[i] for i in range(N_CHIPS)], axis=1)
    wp = jnp.concatenate([w_in_f[:, :P_QKVB], w_in_f[:, P_QKVB + 8:], w_in_f[:, P_QKVB:P_QKVB + 8],
                          jnp.zeros((D_MODEL, P_COLS - P_BA - 8), _MXU)], axis=1)
    w_out_f = g_out.reshape(D_MODEL, D_MODEL)
    w_g = jnp.concatenate([g_up[0], g_up[1]], axis=1)
    w_u = jnp.concatenate([g_up[2], g_up[3]], axis=1)
    w_down_f = g_down.reshape(D_FF, D_MODEL)
    conv_f = jnp.concatenate([g_conv[i] for i in range(N_CHIPS)], axis=1)
    fcw_g = jnp.concatenate([g_fconv[0], g_fconv[1]], axis=1)
    fcw_u = jnp.concatenate([g_fconv[2], g_fconv[3]], axis=1)
    gp = _pad_lanes(jnp.concatenate([a_log, dt_bias], axis=1), 128)
    fnw = final_norm_w[None, :]
    loss_l, dx, g = _local_step(x[0], loss_target[0], norm1_w, norm2_w, fnw, gp, gdn_norm_w, wp, conv_f, w_out_f,
                                w_g, w_u, fcw_g, fcw_u, w_down_f)
    loss = lax.psum(loss_l[0, 0], ("x", "y", "c"))
    d_wp = g["wp"]
    d_win = jnp.concatenate([d_wp[:, :P_QKVB], d_wp[:, P_BA:P_BA + 8], d_wp[:, P_QKVB:P_BA]], axis=1)
    ncol_in, ncol_up, nrow_out, nrow_down, ncol_conv = IN_COLS // N_CHIPS, 2 * D_FF // N_CHIPS, D_MODEL // N_CHIPS, D_FF // N_CHIPS, 3 * GDN_WIDTH // N_CHIPS
    d_up = (g["w_g"], g["w_u"])
    d_fc = (g["fcw_g"], g["fcw_u"])
    d_wo = (g["w_out_a"], g["w_out_b"])
    secs = []
    for j in range(N_CHIPS):
        hj, lj = j // 2, j % 2
        parts = [d_win[:, ncol_in * j:ncol_in * (j + 1)], d_wo[hj][nrow_out * lj:nrow_out * (lj + 1)],
                 d_up[hj][:, ncol_up * lj:ncol_up * (lj + 1)], g["w_down"][nrow_down * j:nrow_down * (j + 1)],
                 g["conv_w"][:, ncol_conv * j:ncol_conv * (j + 1)], d_fc[hj][:, ncol_up * lj:ncol_up * (lj + 1)]]
        flat = jnp.concatenate([p.reshape(-1) for p in parts])
        secs.append(jnp.pad(flat, (0, SEC_ROWS * D_MODEL - flat.shape[0])).reshape(SEC_ROWS, D_MODEL))
    packed = jnp.stack(secs)
    small = jnp.concatenate([g["n1w"], g["n2w"], g["fnw"], _pad_lanes(g["gp"][0:1]), _pad_lanes(g["gnw"]),
                             jnp.zeros((SMALL_ROWS - 5, D_MODEL), F32)], axis=0)
    Rh = SEC_ROWS // 2
    mine = lax.dynamic_slice_in_dim(packed, c * Rh, Rh, axis=1)
    theirs = lax.dynamic_slice_in_dim(packed, (1 - c) * Rh, Rh, axis=1).astype(_MXU)
    got, small_all = _sibling_exchange(theirs, small)
    part32, part16 = _add_sibling(mine, got)
    got3 = _chip_exchange(part16)
    half, small_red = _add_chips(lax.dynamic_index_in_dim(part32, q, axis=0, keepdims=False), got3, small_all)
    red = _sibling_share(half).reshape(-1)
    shapes = [(D_MODEL, ncol_in), (nrow_out, D_MODEL), (D_MODEL, ncol_up), (nrow_down, D_MODEL), (GDN_CONV, ncol_conv),
              (FFN_CONV, ncol_up)]
    gr, off = [], 0
    for shp in shapes:
        n = shp[0] * shp[1]
        gr.append(red[off:off + n].reshape(shp))
        off += n
    g_w_in, g_w_out, g_w_up, g_w_down, g_conv_w, g_fconv_w = gr
    g_n1w, g_n2w, g_fnw = small_red[0:1], small_red[1:2], small_red[2]
    g_alog, g_dtb, g_gnw = small_red[3:4, 0:4], small_red[3:4, 4:8], small_red[4:5, 0:128]
    big = {}
    for nm, w, gg, m, v in (("w_in", w_in, g_w_in, m_w_in, v_w_in), ("conv_qkv_w", conv_qkv_w, g_conv_w, m_conv_qkv_w, v_conv_qkv_w),
                            ("w_out", w_out, g_w_out, m_w_out, v_w_out), ("w_up", w_up, g_w_up, m_w_up, v_w_up),
                            ("ffn_conv_w", ffn_conv_w, g_fconv_w, m_ffn_conv_w, v_ffn_conv_w),
                            ("w_down", w_down, g_w_down, m_w_down, v_w_down)):
        d_, m_, v_ = _adamw(w[0], gg, m[0], v[0], "adamw_" + nm)
        big[nm] = (gg[None], d_[None], m_[None], v_[None])

    def pack_small(n1, n2, fn, al, db, gn):
        return jnp.concatenate([n1, n2, fn[None, :], _pad_lanes(jnp.concatenate([al, db], axis=1)), _pad_lanes(gn),
                                jnp.zeros((SMALL_ROWS - 5, D_MODEL), F32)], axis=0)

    sw = pack_small(norm1_w, norm2_w, final_norm_w, a_log, dt_bias, gdn_norm_w)
    sm = pack_small(m_norm1_w, m_norm2_w, m_final_norm_w, m_a_log, m_dt_bias, m_gdn_norm_w)
    sv = pack_small(v_norm1_w, v_norm2_w, v_final_norm_w, v_a_log, v_dt_bias, v_gdn_norm_w)
    sd, smn, svn = _adamw(sw, small_red, sm, sv, "adamw_small")

    def unpack_small(t):
        return dict(norm1_w=t[0:1], norm2_w=t[1:2], final_norm_w=t[2], a_log=t[3:4, 0:4], dt_bias=t[3:4, 4:8],
                    gdn_norm_w=t[4:5, 0:128])

    sg = dict(norm1_w=g_n1w, norm2_w=g_n2w, final_norm_w=g_fnw, a_log=g_alog, dt_bias=g_dtb, gdn_norm_w=g_gnw)
    sd, smn, svn = unpack_small(sd), unpack_small(smn), unpack_small(svn)
    names = ["norm1_w", "w_in", "conv_qkv_w", "a_log", "dt_bias", "gdn_norm_w", "w_out", "norm2_w", "w_up",
             "ffn_conv_w", "w_down", "final_norm_w"]
    grads = [big[n][0] if n in big else sg[n] for n in names]
    deltas = [big[n][1] if n in big else sd[n] for n in names]
    new_m = [big[n][2] if n in big else smn[n] for n in names]
    new_v = [big[n][3] if n in big else svn[n] for n in names]
    return (loss, dx[None], *grads, *deltas, *new_m, *new_v)
```

```python
import functools
import math

import numpy as np
import jax
import jax.numpy as jnp
from jax import lax
from jax.experimental import pallas as pl
from jax.experimental.pallas import tpu as pltpu

F32 = jnp.float32
BF16 = jnp.bfloat16
_MXU = jnp.bfloat16
_HI = lax.Precision.HIGHEST
EPS = 1e-6
V7X_VMEM_LIMIT = 56 * 1024 * 1024
MESH = pl.DeviceIdType.MESH

D_MODEL = 1024
GDN_HEADS, GDN_DIM, GDN_CHUNK, GDN_CONV = 4, 128, 64, 4
GDN_WIDTH = GDN_HEADS * GDN_DIM
DIL_HEADS, DIL_DIM = 8, 64
DIL_WIDTH = DIL_HEADS * DIL_DIM
D_FF, FFN_CONV = 2816, 3
IN_COLS = 3592
P_COLS = 3840
P_Z, P_QKVB, P_BA = 1536, 2048, 3584
ATT_T = 256
ADAM_LR, ADAM_B1, ADAM_B2, ADAM_EPS, ADAM_WD, ADAM_STEP = 0.001, 0.9, 0.999, 1e-08, 0.01, 10
N_CHIPS = 4


def _cparams(sem=None, vmem=None):
    kw = {}
    if sem is not None:
        kw["dimension_semantics"] = sem
    if vmem is not None:
        kw["vmem_limit_bytes"] = vmem
    return pltpu.CompilerParams(**kw)


def _silu(x):
    return x * jax.nn.sigmoid(x)


def _pick_tile(n, cap):
    best = None
    for t in range(128, min(n, cap) + 1, 128):
        if n % t == 0:
            best = t
    return best or n


def _mm(a, b, mode, *, out_dtype=F32, residual=None, name):
    if mode == "nn":
        (M, K), (_, N) = a.shape, b.shape
    elif mode == "nt":
        (M, K), (N, _) = a.shape, b.shape
    else:
        (K, M), (_, N) = a.shape, b.shape
    tm, tn = _pick_tile(M, 1024), _pick_tile(N, 1536)

    def vmem(tm, tn):
        return 2 * (tm * K * a.dtype.itemsize + tn * K * b.dtype.itemsize
                    + tm * tn * (jnp.dtype(out_dtype).itemsize + (4 if residual is not None else 0))) + 3 * tm * tn * 4

    while vmem(tm, tn) > 40 * 1024 * 1024:
        if tm >= tn and tm % 256 == 0:
            tm //= 2
        elif tn % 256 == 0:
            tn //= 2
        else:
            tm //= 2
    a_spec = pl.BlockSpec((K, tm), lambda j, i: (0, i)) if mode == "tn" else pl.BlockSpec((tm, K), lambda j, i: (i, 0))
    b_spec = pl.BlockSpec((tn, K), lambda j, i: (j, 0)) if mode == "nt" else pl.BlockSpec((K, tn), lambda j, i: (0, j))
    o_spec = pl.BlockSpec((tm, tn), lambda j, i: (i, j))
    dims = {"nn": (((1,), (0,)), ((), ())), "nt": (((1,), (1,)), ((), ())), "tn": (((0,), (0,)), ((), ()))}[mode]

    def body(*refs):
        a_ref, b_ref = refs[0], refs[1]
        o_ref = refs[-1]
        acc = lax.dot_general(a_ref[...].astype(_MXU), b_ref[...].astype(_MXU), dims, preferred_element_type=F32)
        if residual is not None:
            acc = acc + refs[2][...]
        o_ref[...] = acc.astype(out_dtype)

    ins, specs = [a, b], [a_spec, b_spec]
    if residual is not None:
        ins.append(residual)
        specs.append(o_spec)
    return pl.pallas_call(
        body, name=name, grid=(N // tn, M // tm), in_specs=specs, out_specs=o_spec,
        out_shape=jax.ShapeDtypeStruct((M, N), out_dtype),
        compiler_params=_cparams(("parallel", "parallel"), V7X_VMEM_LIMIT),
    )(*ins)


def _rmsnorm_fwd(x, w, name):
    S, D = x.shape
    T = _pick_tile(S, 512)

    def body(x_ref, w_ref, o_ref):
        xv = x_ref[...]
        rs = lax.rsqrt(jnp.mean(xv * xv, axis=-1, keepdims=True) + EPS)
        o_ref[...] = (xv * rs * w_ref[...]).astype(o_ref.dtype)

    return pl.pallas_call(
        body, name=name, grid=(S // T,),
        in_specs=[pl.BlockSpec((T, D), lambda i: (i, 0)), pl.BlockSpec((1, D), lambda i: (0, 0))],
        out_specs=pl.BlockSpec((T, D), lambda i: (i, 0)),
        out_shape=jax.ShapeDtypeStruct((S, D), _MXU),
        compiler_params=_cparams(("parallel",)),
    )(x, w)


def _rmsnorm_bwd(dh, x, w, dres, name):
    S, D = x.shape
    T = _pick_tile(S, 512)

    def body(dh_ref, x_ref, w_ref, dres_ref, dx_ref, dw_ref):
        xv = x_ref[...]
        rs = lax.rsqrt(jnp.mean(xv * xv, axis=-1, keepdims=True) + EPS)
        xn = xv * rs
        dhv = dh_ref[...]
        dxn = dhv * w_ref[...]
        dx_ref[...] = dres_ref[...] + rs * (dxn - xn * jnp.mean(dxn * xn, axis=-1, keepdims=True))

        @pl.when(pl.program_id(0) == 0)
        def _():
            dw_ref[...] = jnp.zeros_like(dw_ref)

        dw_ref[...] += jnp.sum(dhv * xn, axis=0, keepdims=True)

    row = pl.BlockSpec((T, D), lambda i: (i, 0))
    vec = pl.BlockSpec((1, D), lambda i: (0, 0))
    return pl.pallas_call(
        body, name=name, grid=(S // T,), in_specs=[row, row, vec, row], out_specs=(row, vec),
        out_shape=(jax.ShapeDtypeStruct((S, D), F32), jax.ShapeDtypeStruct((1, D), F32)),
        compiler_params=_cparams(("arbitrary",)),
    )(dh, x, w, dres)


def _loss_head(x3, w, tgt, name):
    S, D = x3.shape
    T = _pick_tile(S, 512)

    def body(x_ref, w_ref, t_ref, loss_ref, dx_ref, dw_ref):
        xv = x_ref[...]
        rs = lax.rsqrt(jnp.mean(xv * xv, axis=-1, keepdims=True) + EPS)
        xn = xv * rs
        err = xn * w_ref[...] - t_ref[...]
        dy = err * (1.0 / D)
        dxn = dy * w_ref[...]
        dx_ref[...] = rs * (dxn - xn * jnp.mean(dxn * xn, axis=-1, keepdims=True))

        @pl.when(pl.program_id(0) == 0)
        def _():
            dw_ref[...] = jnp.zeros_like(dw_ref)
            loss_ref[...] = jnp.zeros_like(loss_ref)

        dw_ref[...] += jnp.sum(dy * xn, axis=0, keepdims=True)
        part = jnp.sum(jnp.sum(err * err, axis=-1, keepdims=True), axis=0, keepdims=True) * (0.5 / D)
        loss_ref[...] += jnp.broadcast_to(part, loss_ref.shape)

    row = pl.BlockSpec((T, D), lambda i: (i, 0))
    vec = pl.BlockSpec((1, D), lambda i: (0, 0))
    return pl.pallas_call(
        body, name=name, grid=(S // T,), in_specs=[row, vec, row],
        out_specs=(pl.BlockSpec((8, 128), lambda i: (0, 0)), row, vec),
        out_shape=(jax.ShapeDtypeStruct((8, 128), F32), jax.ShapeDtypeStruct((S, D), F32), jax.ShapeDtypeStruct((1, D), F32)),
        compiler_params=_cparams(("arbitrary",)),
    )(x3, w, tgt)


def _conv_taps(ext, w, K, T):
    out = None
    for i in range(K):
        lo = 8 - (K - 1) + i
        term = ext[lo:lo + T, :] * w[i:i + 1, :]
        out = term if out is None else out + term
    return out


def _conv_taps_t(ext, w, K, T):
    out = None
    for i in range(K):
        lo = (K - 1) - i
        term = ext[lo:lo + T, :] * w[i:i + 1, :]
        out = term if out is None else out + term
    return out


def _tri_masks(C):
    r = lax.broadcasted_iota(jnp.int32, (C, C), 0)
    c = lax.broadcasted_iota(jnp.int32, (C, C), 1)
    return r == c, r >= c, r > c, r <= c


_NN, _NT, _TN = ((1,), (0,)), ((1,), (1,)), ((0,), (0,))
_GDN_PASSES = dict(qk=1, inv=1, sol=1, scan=1, bwd=1)


def _bdot_raw(a, b, kind, passes):
    dims = ({"NN": ((2,), (1,)), "NT": ((2,), (2,)), "TN": ((1,), (1,))}[kind], ((0,), (0,)))
    if passes == 0:
        return lax.dot_general(a, b, dims, precision=_HI, preferred_element_type=F32)
    ah, bh = a.astype(BF16), b.astype(BF16)
    out = lax.dot_general(ah, bh, dims, preferred_element_type=F32)
    if passes == 3:
        al, bl = (a - ah.astype(F32)).astype(BF16), (b - bh.astype(F32)).astype(BF16)
        out = out + lax.dot_general(ah, bl, dims, preferred_element_type=F32) + lax.dot_general(al, bh, dims, preferred_element_type=F32)
    return out


@functools.partial(jax.custom_vjp, nondiff_argnums=(2, 3))
def _bdot(a, b, kind, passes):
    return _bdot_raw(a, b, kind, passes)


def _bdot_fwd(a, b, kind, passes):
    return _bdot_raw(a, b, kind, passes), (a, b)


def _bdot_bwd(kind, passes, res, ct):
    a, b = res
    if kind == "NN":
        return _bdot_raw(ct, b, "NT", passes), _bdot_raw(a, ct, "TN", passes)
    if kind == "NT":
        return _bdot_raw(ct, b, "NN", passes), _bdot_raw(ct, a, "TN", passes)
    return _bdot_raw(b, ct, "NT", passes), _bdot_raw(a, ct, "NN", passes)


_bdot.defvjp(_bdot_fwd, _bdot_bwd)


def _softplus(x):
    return jnp.maximum(x, 0.0) + jnp.log(1.0 + jnp.exp(-jnp.abs(x)))


def _gdn_stage1(cq, ck, cv, b_col, a_col, alog, dtb, dot=_bdot_raw):
    C = cq.shape[1]
    eye, incl, strict, incl_t = _tri_masks(C)
    qn = cq * lax.rsqrt(jnp.sum(cq * cq, axis=-1, keepdims=True) + EPS) * (GDN_DIM ** -0.5)
    kn = ck * lax.rsqrt(jnp.sum(ck * ck, axis=-1, keepdims=True) + EPS)
    beta = jax.nn.sigmoid(b_col)
    g = -jnp.exp(alog) * _softplus(a_col + dtb)
    g_row = jnp.sum(jnp.where(eye, g, 0.0), axis=1, keepdims=True)
    beta_row = jnp.sum(jnp.where(eye, beta, 0.0), axis=1, keepdims=True)
    gc_col = jnp.sum(jnp.where(incl, g_row, 0.0), axis=2, keepdims=True)
    gc_row = jnp.sum(jnp.where(incl_t, g, 0.0), axis=1, keepdims=True)
    dec = jnp.where(incl, jnp.exp(jnp.where(incl, gc_col - gc_row, 0.0)), 0.0)
    kk = dot(kn, kn, "NT", _GDN_PASSES["qk"])
    qk = dot(qn, kn, "NT", _GDN_PASSES["qk"])
    lmat = jnp.where(strict, dec * kk * beta_row, 0.0)
    attn = dec * qk * beta_row
    gam = jnp.exp(gc_col)
    gc_last = gc_col[:, C - 1:C, :]
    k_end = kn * (jnp.exp(gc_last - gc_col) * beta)
    return lmat, cv, gam * kn, gam * qn, attn, k_end, jnp.exp(gc_last)


def _tri_inv(lmat):
    C = lmat.shape[1]
    eye = _tri_masks(C)[0]
    ps = _GDN_PASSES["inv"]
    p = jnp.where(eye, 1.0, 0.0) - lmat
    lp = _bdot_raw(lmat, lmat, "NN", ps)
    n = int(math.log2(C))
    for s in range(1, n):
        p = p + _bdot_raw(p, lp, "NN", ps)
        if s < n - 1:
            lp = _bdot_raw(lp, lp, "NN", ps)
    return p


def _gated_norm(o, z, gnw):
    on = o * lax.rsqrt(jnp.mean(o * o, axis=-1, keepdims=True) + EPS) * gnw
    return on * _silu(z)


GDN_PG = 2
GDN_SG = 4


def _gdn_pairs(c, ba, gp, G):
    C, W, H = GDN_CHUNK, GDN_WIDTH, GDN_HEADS
    pairs = [(j, h) for j in range(G) for h in range(H)]
    cq, ck, cv = (jnp.stack([c[C * j:C * (j + 1), o + GDN_DIM * h:o + GDN_DIM * (h + 1)] for j, h in pairs]) for o in (0, W, 2 * W))
    b_col = jnp.stack([ba[C * j:C * (j + 1), h:h + 1] for j, h in pairs])
    a_col = jnp.stack([ba[C * j:C * (j + 1), H + h:H + h + 1] for j, h in pairs])
    alog = jnp.stack([gp[0:1, h:h + 1] for j, h in pairs])
    dtb = jnp.stack([gp[0:1, H + h:H + h + 1] for j, h in pairs])
    return pairs, (cq, ck, cv, b_col, a_col, alog, dtb)


def _gdn_pre_specs(S, G):
    C = GDN_CHUNK
    T = C * G
    return dict(
        cur=pl.BlockSpec((T, 3 * GDN_WIDTH), lambda i: (i, 0)),
        prev=pl.BlockSpec((8, 3 * GDN_WIDTH), lambda i: (jnp.maximum(i * (T // 8) - 1, 0), 0)),
        ba=pl.BlockSpec((T, 128), lambda i: (i, P_BA // 128)),
        cw=pl.BlockSpec((GDN_CONV, 3 * GDN_WIDTH), lambda i: (0, 0)),
        vec=pl.BlockSpec((1, 128), lambda i: (0, 0)),
        hd=pl.BlockSpec((GDN_HEADS, T, GDN_DIM), lambda i: (0, i, 0)),
        hc=pl.BlockSpec((GDN_HEADS, T, C), lambda i: (0, i, 0)),
        ge=pl.BlockSpec((G, GDN_HEADS, 8, 128), lambda i: (i, 0, 0, 0)),
    )


def _hd_shape(S, last=GDN_DIM):
    return jax.ShapeDtypeStruct((GDN_HEADS, S, last), F32)


def _gdn_pre(proj, conv_w, gp):
    S = proj.shape[0]
    C, G = GDN_CHUNK, GDN_PG
    nc = S // C
    sp = _gdn_pre_specs(S, G)

    def body(cur_ref, prev_ref, ba_ref, cw_ref, gp_ref, uv_ref, wk_ref, qd_ref, ke_ref, at_ref, ti_ref, ge_ref):
        prev = prev_ref[...] * jnp.where(pl.program_id(0) == 0, 0.0, 1.0)
        c = _silu(_conv_taps(jnp.concatenate([prev, cur_ref[...]], axis=0), cw_ref[...], GDN_CONV, C * G))
        pairs, args = _gdn_pairs(c, ba_ref[...], gp_ref[...], G)
        lmat, v, rk, q_dec, attn, k_end, g_end = _gdn_stage1(*args)
        t = _tri_inv(lmat)
        u_v = _bdot_raw(t, v, "NN", _GDN_PASSES["sol"])
        w_k = _bdot_raw(t, rk, "NN", _GDN_PASSES["sol"])
        for b, (j, h) in enumerate(pairs):
            rows = slice(C * j, C * (j + 1))
            uv_ref[h, rows, :] = u_v[b]
            wk_ref[h, rows, :] = w_k[b]
            qd_ref[h, rows, :] = q_dec[b]
            ke_ref[h, rows, :] = k_end[b]
            at_ref[h, rows, :] = attn[b]
            ti_ref[h, rows, :] = t[b]
            ge_ref[j, h] = jnp.broadcast_to(g_end[b], (8, 128))

    return pl.pallas_call(
        body, name="gdn_pre", grid=(nc // G,),
        in_specs=[sp["cur"], sp["prev"], sp["ba"], sp["cw"], sp["vec"]],
        out_specs=(sp["hd"], sp["hd"], sp["hd"], sp["hd"], sp["hc"], sp["hc"], sp["ge"]),
        out_shape=(_hd_shape(S), _hd_shape(S), _hd_shape(S), _hd_shape(S), _hd_shape(S, C), _hd_shape(S, C),
                   jax.ShapeDtypeStruct((nc, GDN_HEADS, 8, 128), F32)),
        compiler_params=_cparams(("parallel",)),
    )(proj, proj, proj, conv_w, gp)


def _gdn_scan_specs(S, G, rev):
    C = GDN_CHUNK
    T = C * G
    n = S // T
    ci = (lambda i: n - 1 - i) if rev else (lambda i: i)
    return dict(
        hd=pl.BlockSpec((GDN_HEADS, T, GDN_DIM), lambda i: (0, ci(i), 0)),
        hc=pl.BlockSpec((GDN_HEADS, T, C), lambda i: (0, ci(i), 0)),
        ge=pl.BlockSpec((G, GDN_HEADS, 8, 128), lambda i: (ci(i), 0, 0, 0)),
        z=pl.BlockSpec((T, GDN_WIDTH), lambda i: (ci(i), P_Z // GDN_WIDTH)),
        oa=pl.BlockSpec((T, GDN_WIDTH), lambda i: (ci(i), 0)),
        vec=pl.BlockSpec((1, 128), lambda i: (0, 0)),
        st=pl.BlockSpec((G, GDN_HEADS, GDN_DIM, GDN_DIM), lambda i: (ci(i), 0, 0, 0)),
    )


def _gdn_scan(u_v, w_k, q_dec, k_end, attn, g_end, proj, gnw):
    S = proj.shape[0]
    C, G = GDN_CHUNK, GDN_SG
    nc = S // C
    sp = _gdn_scan_specs(S, G, False)
    ps = _GDN_PASSES["scan"]

    def body(uv_ref, wk_ref, qd_ref, ke_ref, at_ref, ge_ref, z_ref, gnw_ref, oa_ref, st_ref, s_scr):
        @pl.when(pl.program_id(0) == 0)
        def _():
            s_scr[...] = jnp.zeros_like(s_scr)

        for j in range(G):
            rows = slice(C * j, C * (j + 1))
            st = s_scr[...]
            st_ref[j] = st
            u = uv_ref[:, rows, :] - _bdot_raw(wk_ref[:, rows, :], st, "NN", ps)
            o = _bdot_raw(qd_ref[:, rows, :], st, "NN", ps) + _bdot_raw(at_ref[:, rows, :], u, "NN", ps)
            s_scr[...] = ge_ref[j][:, 0:1, 0:1] * st + _bdot_raw(ke_ref[:, rows, :], u, "TN", ps)
            for h in range(GDN_HEADS):
                cols = slice(GDN_DIM * h, GDN_DIM * (h + 1))
                oa_ref[rows, cols] = _gated_norm(o[h], z_ref[rows, cols], gnw_ref[...])

    return pl.pallas_call(
        body, name="gdn_scan", grid=(nc // G,),
        in_specs=[sp["hd"], sp["hd"], sp["hd"], sp["hd"], sp["hc"], sp["ge"], sp["z"], sp["vec"]],
        out_specs=(sp["oa"], sp["st"]),
        out_shape=(jax.ShapeDtypeStruct((S, GDN_WIDTH), F32),
                   jax.ShapeDtypeStruct((nc, GDN_HEADS, GDN_DIM, GDN_DIM), F32)),
        scratch_shapes=[pltpu.VMEM((GDN_HEADS, GDN_DIM, GDN_DIM), F32)],
        compiler_params=_cparams(("arbitrary",)),
    )(u_v, w_k, q_dec, k_end, attn, g_end, proj, gnw)


def _gdn_scan_bwd(u_v, w_k, q_dec, k_end, attn, g_end, proj, gnw, states, d_oa):
    S = proj.shape[0]
    C, G = GDN_CHUNK, GDN_SG
    nc = S // C
    sp = _gdn_scan_specs(S, G, True)
    ps, pb = _GDN_PASSES["scan"], _GDN_PASSES["bwd"]

    def body(uv_ref, wk_ref, qd_ref, ke_ref, at_ref, ge_ref, z_ref, gnw_ref, st_ref, doa_ref,
             duv_ref, dwk_ref, dqd_ref, dke_ref, dat_ref, dge_ref, dz_ref, dgnw_ref, ds_scr):
        @pl.when(pl.program_id(0) == 0)
        def _():
            ds_scr[...] = jnp.zeros_like(ds_scr)
            dgnw_ref[...] = jnp.zeros_like(dgnw_ref)

        dgnw = jnp.zeros((1, 128), F32)
        for j in reversed(range(G)):
            rows = slice(C * j, C * (j + 1))
            st = st_ref[j]
            wk, qd, ke, at = wk_ref[:, rows, :], qd_ref[:, rows, :], ke_ref[:, rows, :], at_ref[:, rows, :]
            u = uv_ref[:, rows, :] - _bdot_raw(wk, st, "NN", ps)
            o = _bdot_raw(qd, st, "NN", ps) + _bdot_raw(at, u, "NN", ps)
            dos = []
            for h in range(GDN_HEADS):
                cols = slice(GDN_DIM * h, GDN_DIM * (h + 1))
                _, vjp2 = jax.vjp(_gated_norm, o[h], z_ref[rows, cols], gnw_ref[...])
                do_h, dz_h, dgn = vjp2(doa_ref[rows, cols])
                dz_ref[rows, cols] = dz_h
                dgnw = dgnw + dgn
                dos.append(do_h)
            do = jnp.stack(dos)
            ds_new = ds_scr[...]
            du = _bdot_raw(at, do, "TN", pb) + _bdot_raw(ke, ds_new, "NN", pb)
            duv_ref[:, rows, :] = du
            dat_ref[:, rows, :] = _bdot_raw(do, u, "NT", pb)
            dqd_ref[:, rows, :] = _bdot_raw(do, st, "NT", pb)
            dke_ref[:, rows, :] = _bdot_raw(u, ds_new, "NT", pb)
            dwk_ref[:, rows, :] = -_bdot_raw(du, st, "NT", pb)
            d_ge = jnp.sum(jnp.sum(st * ds_new, axis=2, keepdims=True), axis=1, keepdims=True)
            dge_ref[j] = jnp.broadcast_to(d_ge, (GDN_HEADS, 8, 128))
            ds_scr[...] = ge_ref[j][:, 0:1, 0:1] * ds_new + _bdot_raw(qd, do, "TN", pb) - _bdot_raw(wk, du, "TN", pb)
        dgnw_ref[...] += dgnw

    return pl.pallas_call(
        body, name="gdn_scan_bwd", grid=(nc // G,),
        in_specs=[sp["hd"], sp["hd"], sp["hd"], sp["hd"], sp["hc"], sp["ge"], sp["z"], sp["vec"], sp["st"], sp["oa"]],
        out_specs=(sp["hd"], sp["hd"], sp["hd"], sp["hd"], sp["hc"], sp["ge"], sp["oa"], sp["vec"]),
        out_shape=(_hd_shape(S), _hd_shape(S), _hd_shape(S), _hd_shape(S), _hd_shape(S, C),
                   jax.ShapeDtypeStruct((nc, GDN_HEADS, 8, 128), F32), jax.ShapeDtypeStruct((S, GDN_WIDTH), F32),
                   jax.ShapeDtypeStruct((1, 128), F32)),
        scratch_shapes=[pltpu.VMEM((GDN_HEADS, GDN_DIM, GDN_DIM), F32)],
        compiler_params=_cparams(("arbitrary",)),
    )(u_v, w_k, q_dec, k_end, attn, g_end, proj, gnw, states, d_oa)


def _gdn_post(proj, conv_w, gp, tinv, u_v, w_k, d_uv, d_wk, d_qd, d_ke, d_at, d_ge):
    S = proj.shape[0]
    C, G = GDN_CHUNK, GDN_PG
    nc = S // C
    sp = _gdn_pre_specs(S, G)
    pb = _GDN_PASSES["bwd"]

    def body(cur_ref, prev_ref, ba_ref, cw_ref, gp_ref, ti_ref, uv_ref, wk_ref, duv_ref, dwk_ref, dqd_ref, dke_ref,
             dat_ref, dge_ref, dpre_ref, dba_ref, dgp_ref):
        i = pl.program_id(0)

        @pl.when(i == 0)
        def _():
            dgp_ref[...] = jnp.zeros_like(dgp_ref)

        prev = prev_ref[...] * jnp.where(i == 0, 0.0, 1.0)
        pre = _conv_taps(jnp.concatenate([prev, cur_ref[...]], axis=0), cw_ref[...], GDN_CONV, C * G)
        sg = jax.nn.sigmoid(pre)
        dsilu = sg * (1.0 + pre * (1.0 - sg))
        pairs, args = _gdn_pairs(pre * sg, ba_ref[...], gp_ref[...], G)
        _, vjp1 = jax.vjp(functools.partial(_gdn_stage1, dot=_bdot), *args)

        def take(ref):
            return jnp.stack([ref[h, C * j:C * (j + 1), :] for j, h in pairs])

        t, u_v, w_k = take(ti_ref), take(uv_ref), take(wk_ref)
        d_v = _bdot_raw(t, take(duv_ref), "TN", pb)
        d_rk = _bdot_raw(t, take(dwk_ref), "TN", pb)
        d_l = -(_bdot_raw(d_v, u_v, "NT", pb) + _bdot_raw(d_rk, w_k, "NT", pb))
        d_ge = jnp.stack([dge_ref[j, h][0:1, 0:1] for j, h in pairs])
        dcq, dck, dcv, db, da, dalog, ddtb = vjp1((d_l, d_v, d_rk, take(dqd_ref), take(dat_ref), take(dke_ref), d_ge))
        lane = lax.broadcasted_iota(jnp.int32, (C, 128), 1)
        lane1 = lax.broadcasted_iota(jnp.int32, (1, 128), 1)
        dgp = jnp.zeros((1, 128), F32)
        for j in range(G):
            rows = slice(C * j, C * (j + 1))
            dba = jnp.zeros((C, 128), F32)
            for h in range(GDN_HEADS):
                b = GDN_HEADS * j + h
                for o_, dcx in ((0, dcq), (GDN_WIDTH, dck), (2 * GDN_WIDTH, dcv)):
                    cols = slice(o_ + GDN_DIM * h, o_ + GDN_DIM * (h + 1))
                    dpre_ref[rows, cols] = dcx[b] * dsilu[rows, cols]
                dba = dba + jnp.where(lane == h, db[b], 0.0) + jnp.where(lane == GDN_HEADS + h, da[b], 0.0)
                dgp = dgp + jnp.where(lane1 == h, dalog[b], 0.0) + jnp.where(lane1 == GDN_HEADS + h, ddtb[b], 0.0)
            dba_ref[rows, :] = dba
        dgp_ref[0:1, :] += dgp

    T = C * G
    return pl.pallas_call(
        body, name="gdn_post", grid=(nc // G,),
        in_specs=[sp["cur"], sp["prev"], sp["ba"], sp["cw"], sp["vec"], sp["hc"], sp["hd"], sp["hd"], sp["hd"], sp["hd"],
                  sp["hd"], sp["hd"], sp["hc"], sp["ge"]],
        out_specs=(sp["cur"], pl.BlockSpec((T, 128), lambda i: (i, 0)), pl.BlockSpec((8, 128), lambda i: (0, 0))),
        out_shape=(jax.ShapeDtypeStruct((S, 3 * GDN_WIDTH), F32), jax.ShapeDtypeStruct((S, 128), F32),
                   jax.ShapeDtypeStruct((8, 128), F32)),
        compiler_params=_cparams(("arbitrary",)),
    )(proj, proj, proj, conv_w, gp, tinv, u_v, w_k, d_uv, d_wk, d_qd, d_ke, d_at, d_ge)


def _conv_bwd(dpre, x, xcol0, w, K, name, tc):
    S, Cc = dpre.shape
    T = _pick_tile(S, 256)
    nt, ncol = S // T, Cc // tc
    xo = xcol0 // tc

    def body(d_ref, dn_ref, x_ref, xp_ref, w_ref, dx_ref, dw_ref):
        i = pl.program_id(1)
        dn = dn_ref[...] * jnp.where(i == nt - 1, 0.0, 1.0)
        dv = d_ref[...]
        ext_d = jnp.concatenate([dv, dn], axis=0)
        wv = w_ref[...]
        dx_ref[...] = _conv_taps_t(ext_d, wv, K, T).astype(dx_ref.dtype)
        xp = xp_ref[...] * jnp.where(i == 0, 0.0, 1.0)
        ext_x = jnp.concatenate([xp, x_ref[...]], axis=0)

        @pl.when(i == 0)
        def _():
            dw_ref[...] = jnp.zeros_like(dw_ref)

        for k in range(K):
            lo = 8 - (K - 1) + k
            dw_ref[k:k + 1, :] += jnp.sum(dv * ext_x[lo:lo + T, :], axis=0, keepdims=True)

    r8 = T // 8
    return pl.pallas_call(
        body, name=name, grid=(ncol, nt),
        in_specs=[pl.BlockSpec((T, tc), lambda j, i: (i, j)),
                  pl.BlockSpec((8, tc), lambda j, i: (jnp.minimum((i + 1) * r8, S // 8 - 1), j)),
                  pl.BlockSpec((T, tc), lambda j, i: (i, j + xo)),
                  pl.BlockSpec((8, tc), lambda j, i: (jnp.maximum(i * r8 - 1, 0), j + xo)),
                  pl.BlockSpec((K, tc), lambda j, i: (0, j))],
        out_specs=(pl.BlockSpec((T, tc), lambda j, i: (i, j)), pl.BlockSpec((K, tc), lambda j, i: (0, j))),
        out_shape=(jax.ShapeDtypeStruct((S, Cc), _MXU), jax.ShapeDtypeStruct((K, Cc), F32)),
        compiler_params=_cparams(("parallel", "arbitrary")),
    )(dpre, dpre, x, x, w)


def _dil_bias(nt):
    T = ATT_T
    d = (np.arange(nt)[:, None, None] * T + np.arange(T)[None, :, None] - np.arange(T)[None, None, :])
    cnt = ((d >= 0) & (d <= 128)).astype(np.float64) + ((d >= 0) & (d % 4 == 0) & (d <= 512)) + ((d >= 0) & (d % 16 == 0))
    return jnp.asarray(np.where(cnt > 0, np.log(np.maximum(cnt, 1.0)), -1e30), dtype=F32)


def _attn_fwd(proj):
    S = proj.shape[0]
    T = ATT_T
    nt = S // T
    bias = _dil_bias(nt)
    scale = DIL_DIM ** -0.5
    npair = DIL_WIDTH // 128
    qb0, kb0, vb0 = P_QKVB // 128, (P_QKVB + DIL_WIDTH) // 128, (P_QKVB + 2 * DIL_WIDTH) // 128

    def body(q_ref, k_ref, v_ref, b_ref, o_ref, lse_ref):
        i = pl.program_id(1)
        qs = (q_ref[...] * scale).astype(_MXU)

        def step(j, carry):
            kt = k_ref[pl.ds(pl.multiple_of(j * T, T), T), :].astype(_MXU)
            vt = v_ref[pl.ds(pl.multiple_of(j * T, T), T), :].astype(_MXU)
            bt = b_ref[i - j]
            out = []
            for hh in range(2):
                m, l, acc = carry[hh]
                sl = slice(hh * DIL_DIM, (hh + 1) * DIL_DIM)
                s = lax.dot_general(qs[:, sl], kt[:, sl], (_NT, ((), ())), preferred_element_type=F32) + bt
                m_new = jnp.maximum(m, jnp.max(s, axis=-1, keepdims=True))
                p = jnp.exp(s - m_new)
                a = jnp.exp(m - m_new)
                l = a * l + jnp.sum(p, axis=-1, keepdims=True)
                acc = a * acc + lax.dot_general(p.astype(_MXU), vt[:, sl], (_NN, ((), ())), preferred_element_type=F32)
                out.append((m_new, l, acc))
            return tuple(out)

        init = tuple((jnp.full((T, 1), -1e30, F32), jnp.zeros((T, 1), F32), jnp.zeros((T, DIL_DIM), F32)) for _ in range(2))
        res = lax.fori_loop(0, i + 1, step, init)
        for hh in range(2):
            m, l, acc = res[hh]
            sl = slice(hh * DIL_DIM, (hh + 1) * DIL_DIM)
            o_ref[:, sl] = acc / l
            lse_ref[:, sl] = jnp.broadcast_to(m + jnp.log(l), (T, DIL_DIM))

    return pl.pallas_call(
        body, name="attn_fwd", grid=(npair, nt),
        in_specs=[pl.BlockSpec((T, 128), lambda p, i: (i, qb0 + p)),
                  pl.BlockSpec((S, 128), lambda p, i: (0, kb0 + p)),
                  pl.BlockSpec((S, 128), lambda p, i: (0, vb0 + p)),
                  pl.BlockSpec((nt, T, T), lambda p, i: (0, 0, 0))],
        out_specs=(pl.BlockSpec((T, 128), lambda p, i: (i, p)), pl.BlockSpec((T, 128), lambda p, i: (i, p))),
        out_shape=(jax.ShapeDtypeStruct((S, DIL_WIDTH), F32), jax.ShapeDtypeStruct((S, DIL_WIDTH), F32)),
        compiler_params=_cparams(("parallel", "parallel")),
    )(proj, proj, proj, bias)


def _attn_bwd(proj, o_b, lse, d_ob):
    S = proj.shape[0]
    T = ATT_T
    nt = S // T
    bias = _dil_bias(nt)
    scale = DIL_DIM ** -0.5
    npair = DIL_WIDTH // 128
    qb0, kb0, vb0 = P_QKVB // 128, (P_QKVB + DIL_WIDTH) // 128, (P_QKVB + 2 * DIL_WIDTH) // 128

    def body(q_ref, k_ref, v_ref, o_ref, lse_ref, do_ref, b_ref, dq_ref, dk_ref, dv_ref, dq_scr):
        j = pl.program_id(1)

        @pl.when(j == 0)
        def _():
            dq_scr[...] = jnp.zeros_like(dq_scr)

        kt = k_ref[...].astype(_MXU)
        vt = v_ref[...].astype(_MXU)

        def step(i, carry):
            rows = pl.ds(pl.multiple_of(i * T, T), T)
            qs = (q_ref[rows, :] * scale).astype(_MXU)
            dov = do_ref[rows, :]
            prod = dov * o_ref[rows, :]
            lsev = lse_ref[rows, :]
            dob = dov.astype(_MXU)
            bt = b_ref[i - j]
            out = []
            dqs = []
            for hh in range(2):
                dk, dv = carry[hh]
                sl = slice(hh * DIL_DIM, (hh + 1) * DIL_DIM)
                s = lax.dot_general(qs[:, sl], kt[:, sl], (_NT, ((), ())), preferred_element_type=F32) + bt
                p = jnp.exp(s - lsev[:, hh * DIL_DIM:hh * DIL_DIM + 1])
                delta = jnp.sum(prod[:, sl], axis=-1, keepdims=True)
                dp = lax.dot_general(dob[:, sl], vt[:, sl], (_NT, ((), ())), preferred_element_type=F32)
                ds = (p * (dp - delta)).astype(_MXU)
                dv = dv + lax.dot_general(p.astype(_MXU), dob[:, sl], (_TN, ((), ())), preferred_element_type=F32)
                dk = dk + lax.dot_general(ds, qs[:, sl], (_TN, ((), ())), preferred_element_type=F32)
                dqs.append(lax.dot_general(ds, kt[:, sl], (_NN, ((), ())), preferred_element_type=F32) * scale)
                out.append((dk, dv))
            dq_scr[rows, :] += jnp.concatenate(dqs, axis=1)
            return tuple(out)

        init = tuple((jnp.zeros((T, DIL_DIM), F32), jnp.zeros((T, DIL_DIM), F32)) for _ in range(2))
        res = lax.fori_loop(j, nt, step, init)
        dk_ref[...] = jnp.concatenate([res[0][0], res[1][0]], axis=1).astype(dk_ref.dtype)
        dv_ref[...] = jnp.concatenate([res[0][1], res[1][1]], axis=1).astype(dv_ref.dtype)

        @pl.when(j == nt - 1)
        def _():
            dq_ref[...] = dq_scr[...].astype(dq_ref.dtype)

    full = lambda c0: pl.BlockSpec((S, 128), lambda p, j: (0, c0 + p))
    tile = lambda c0: pl.BlockSpec((T, 128), lambda p, j: (j, c0 + p))
    out3 = jax.ShapeDtypeStruct((S, DIL_WIDTH), _MXU)
    return pl.pallas_call(
        body, name="attn_bwd", grid=(npair, nt),
        in_specs=[full(qb0), tile(kb0), tile(vb0), full(0), full(0), full(0),
                  pl.BlockSpec((nt, T, T), lambda p, j: (0, 0, 0))],
        out_specs=(full(0), tile(0), tile(0)),
        out_shape=(out3, out3, out3),
        scratch_shapes=[pltpu.VMEM((S, 128), F32)],
        compiler_params=_cparams(("parallel", "arbitrary")),
    )(proj, proj, proj, o_b, lse, d_ob, bias)


def _ffn_act(up_g, up_u, cw_g, cw_u):
    S, Cc = up_g.shape
    T, tc = _pick_tile(S, 256), _pick_tile(Cc, 1536)
    r8 = T // 8

    def body(g_ref, gp_ref, u_ref, up_ref, wg_ref, wu_ref, o_ref):
        keep = jnp.where(pl.program_id(1) == 0, 0.0, 1.0)
        cg = _conv_taps(jnp.concatenate([gp_ref[...] * keep, g_ref[...]], axis=0), wg_ref[...], FFN_CONV, T)
        cu = _conv_taps(jnp.concatenate([up_ref[...] * keep, u_ref[...]], axis=0), wu_ref[...], FFN_CONV, T)
        o_ref[...] = (_silu(cg) * cu).astype(o_ref.dtype)

    cur = pl.BlockSpec((T, tc), lambda j, i: (i, j))
    prev = pl.BlockSpec((8, tc), lambda j, i: (jnp.maximum(i * r8 - 1, 0), j))
    wsp = pl.BlockSpec((FFN_CONV, tc), lambda j, i: (0, j))
    return pl.pallas_call(
        body, name="ffn_act", grid=(Cc // tc, S // T),
        in_specs=[cur, prev, cur, prev, wsp, wsp], out_specs=cur,
        out_shape=jax.ShapeDtypeStruct((S, Cc), _MXU),
        compiler_params=_cparams(("parallel", "parallel")),
    )(up_g, up_g, up_u, up_u, cw_g, cw_u)


def _ffn_act_bwd(d_act, up_g, up_u, cw_g, cw_u):
    S, Cc = up_g.shape
    T, tc = _pick_tile(S, 256), _pick_tile(Cc, 1536)
    r8 = T // 8
    nt = S // T
    K = FFN_CONV

    def body(da_ref, dan_ref, g_ref, gp_ref, gn_ref, u_ref, up_ref, un_ref, wg_ref, wu_ref,
             dg_ref, du_ref, dwg_ref, dwu_ref):
        i = pl.program_id(1)
        keep_p = jnp.where(i == 0, 0.0, 1.0)
        keep_n = jnp.where(i == nt - 1, 0.0, 1.0)
        wg, wu = wg_ref[...], wu_ref[...]
        xg = jnp.concatenate([gp_ref[...] * keep_p, g_ref[...], gn_ref[...] * keep_n], axis=0)
        xu = jnp.concatenate([up_ref[...] * keep_p, u_ref[...], un_ref[...] * keep_n], axis=0)
        cg = _conv_taps(xg, wg, K, T + 8)
        cu = _conv_taps(xu, wu, K, T + 8)
        da = jnp.concatenate([da_ref[...], dan_ref[...] * keep_n], axis=0)
        sg = jax.nn.sigmoid(cg)
        d_cg = da * cu * (sg * (1.0 + cg * (1.0 - sg)))
        d_cu = da * (cg * sg)
        dg_ref[...] = _conv_taps_t(d_cg, wg, K, T).astype(dg_ref.dtype)
        du_ref[...] = _conv_taps_t(d_cu, wu, K, T).astype(du_ref.dtype)

        @pl.when(i == 0)
        def _():
            dwg_ref[...] = jnp.zeros_like(dwg_ref)
            dwu_ref[...] = jnp.zeros_like(dwu_ref)

        for k in range(K):
            lo = 8 - (K - 1) + k
            dwg_ref[k:k + 1, :] += jnp.sum(d_cg[0:T, :] * xg[lo:lo + T, :], axis=0, keepdims=True)
            dwu_ref[k:k + 1, :] += jnp.sum(d_cu[0:T, :] * xu[lo:lo + T, :], axis=0, keepdims=True)

    cur = pl.BlockSpec((T, tc), lambda j, i: (i, j))
    prev = pl.BlockSpec((8, tc), lambda j, i: (jnp.maximum(i * r8 - 1, 0), j))
    nxt = pl.BlockSpec((8, tc), lambda j, i: (jnp.minimum((i + 1) * r8, S // 8 - 1), j))
    wsp = pl.BlockSpec((K, tc), lambda j, i: (0, j))
    return pl.pallas_call(
        body, name="ffn_act_bwd", grid=(Cc // tc, nt),
        in_specs=[cur, nxt, cur, prev, nxt, cur, prev, nxt, wsp, wsp],
        out_specs=(cur, cur, wsp, wsp),
        out_shape=(jax.ShapeDtypeStruct((S, Cc), _MXU), jax.ShapeDtypeStruct((S, Cc), _MXU),
                   jax.ShapeDtypeStruct((K, Cc), F32), jax.ShapeDtypeStruct((K, Cc), F32)),
        compiler_params=_cparams(("parallel", "arbitrary")),
    )(d_act, d_act, up_g, up_g, up_g, up_u, up_u, up_u, cw_g, cw_u)


def _local_step(x, tgt, n1w, n2w, fnw, gp, gnw, wp, conv_w, w_out, w_g, w_u, fcw_g, fcw_u, w_down):
    h1 = _rmsnorm_fwd(x, n1w, "norm1")
    proj = _mm(h1, wp, "nn", name="proj")
    u_v, w_k, q_dec, k_end, attn, tinv, g_end = _gdn_pre(proj, conv_w, gp)
    o_a, states = _gdn_scan(u_v, w_k, q_dec, k_end, attn, g_end, proj, gnw)
    o_b, lse = _attn_fwd(proj)
    x2 = _mm(o_b, w_out[GDN_WIDTH:], "nn", residual=_mm(o_a, w_out[:GDN_WIDTH], "nn", residual=x, name="outproj_a"),
             name="outproj_b")
    h2 = _rmsnorm_fwd(x2, n2w, "norm2")
    up_g = _mm(h2, w_g, "nn", name="up_gate")
    up_u = _mm(h2, w_u, "nn", name="up_up")
    act = _ffn_act(up_g, up_u, fcw_g, fcw_u)
    x3 = _mm(act, w_down, "nn", residual=x2, name="down")
    loss, dx3, d_fnw = _loss_head(x3, fnw, tgt, "loss_head")
    d_act = _mm(dx3, w_down, "nt", name="d_act")
    d_wdown = _mm(act, dx3, "tn", name="d_wdown")
    d_upg, d_upu, d_fcwg, d_fcwu = _ffn_act_bwd(d_act, up_g, up_u, fcw_g, fcw_u)
    d_wg = _mm(h2, d_upg, "tn", name="d_wgate")
    d_wu = _mm(h2, d_upu, "tn", name="d_wup")
    d_h2 = _mm(d_upu, w_u, "nt", residual=_mm(d_upg, w_g, "nt", name="d_h2_a"), name="d_h2_b")
    dx2, d_n2w = _rmsnorm_bwd(d_h2, x2, n2w, dx3, "norm2_bwd")
    d_oa = _mm(dx2, w_out[:GDN_WIDTH], "nt", name="d_oa")
    d_ob = _mm(dx2, w_out[GDN_WIDTH:], "nt", name="d_ob")
    d_wout_a = _mm(o_a, dx2, "tn", name="d_wout_a")
    d_wout_b = _mm(o_b, dx2, "tn", name="d_wout_b")
    dq_b, dk_b, dv_b = _attn_bwd(proj, o_b, lse, d_ob)
    d_uv, d_wk, d_qd, d_ke, d_at, d_ge, d_z, d_gnw = _gdn_scan_bwd(u_v, w_k, q_dec, k_end, attn, g_end, proj, gnw, states, d_oa)
    d_pre, d_ba, d_gp = _gdn_post(proj, conv_w, gp, tinv, u_v, w_k, d_uv, d_wk, d_qd, d_ke, d_at, d_ge)
    d_qkva, d_convw = _conv_bwd(d_pre, proj, 0, conv_w, GDN_CONV, "gdn_conv_bwd", 512)
    d_proj = jnp.concatenate([d_qkva, d_z.astype(_MXU), dq_b, dk_b, dv_b, d_ba.astype(_MXU),
                              jnp.zeros((x.shape[0], P_COLS - P_BA - 128), _MXU)], axis=1)
    d_wp = _mm(h1, d_proj, "tn", name="d_wp")
    d_h1 = _mm(d_proj, wp, "nt", name="d_h1")
    dx, d_n1w = _rmsnorm_bwd(d_h1, x, n1w, dx2, "norm1_bwd")
    grads = dict(wp=d_wp, conv_w=d_convw, w_out_a=d_wout_a, w_out_b=d_wout_b, w_g=d_wg, w_u=d_wu,
                 fcw_g=d_fcwg, fcw_u=d_fcwu, w_down=d_wdown, n1w=d_n1w, n2w=d_n2w, fnw=d_fnw, gp=d_gp, gnw=d_gnw)
    return loss, dx, grads


_HBM = pl.BlockSpec(memory_space=pltpu.HBM)


def _pos():
    return lax.axis_index("x"), lax.axis_index("y"), lax.axis_index("c")


def _other_chips(x, y):
    return [(1 - x, y), (x, 1 - y), (1 - x, 1 - y)]


def _gather_weights(shards):
    n = len(shards)

    def body(*refs):
        ins, outs = refs[:n], refs[n:2 * n]
        send_sems, recv_sems, loc_sems = refs[2 * n:]
        x, y, c = _pos()
        q = 2 * x + y
        chips = _other_chips(x, y)
        local = [pltpu.make_async_copy(ins[a], outs[a].at[q], loc_sems.at[a]) for a in range(n)]
        for cp in local:
            cp.start()

        def copy(a, j, block):
            px, py = chips[j]
            return pltpu.make_async_remote_copy(
                src_ref=ins[a], dst_ref=outs[a].at[block], send_sem=send_sems.at[3 * a + j],
                recv_sem=recv_sems.at[3 * a + j], device_id=(px, py, c), device_id_type=MESH)

        sends = [copy(a, j, q) for a in range(n) for j in range(3)]
        for cp in sends:
            cp.start()
        for a in range(n):
            for j, (px, py) in enumerate(chips):
                copy(a, j, 2 * px + py).wait_recv()
        for cp in sends:
            cp.wait_send()
        for cp in local:
            cp.wait()

    return pl.pallas_call(
        body, name="gather_weights", in_specs=[_HBM] * n, out_specs=[_HBM] * n,
        out_shape=[jax.ShapeDtypeStruct((N_CHIPS,) + s.shape, s.dtype) for s in shards],
        scratch_shapes=[pltpu.SemaphoreType.DMA((3 * n,)), pltpu.SemaphoreType.DMA((3 * n,)),
                        pltpu.SemaphoreType.DMA((n,))],
    )(*shards)


def _sibling_exchange(send, small):
    def body(send_ref, small_ref, recv_ref, all_ref, send_sems, recv_sems, loc_sem):
        x, y, c = _pos()
        me = 4 * x + 2 * y + c
        mine = pltpu.make_async_copy(small_ref, all_ref.at[me], loc_sem)
        mine.start()
        big = pltpu.make_async_remote_copy(src_ref=send_ref, dst_ref=recv_ref, send_sem=send_sems.at[0],
                                           recv_sem=recv_sems.at[0], device_id=(x, y, 1 - c), device_id_type=MESH)
        big.start()

        def peer(r):
            dx, dy, dc = (r >> 2) & 1, (r >> 1) & 1, r & 1
            px = x if dx == 0 else 1 - x
            py = y if dy == 0 else 1 - y
            pc = c if dc == 0 else 1 - c
            return px, py, pc

        def small_copy(r, slot):
            return pltpu.make_async_remote_copy(src_ref=small_ref, dst_ref=all_ref.at[slot], send_sem=send_sems.at[r],
                                                recv_sem=recv_sems.at[r], device_id=peer(r), device_id_type=MESH)

        sends = [small_copy(r, me) for r in range(1, 8)]
        for cp in sends:
            cp.start()
        for r in range(1, 8):
            px, py, pc = peer(r)
            small_copy(r, 4 * px + 2 * py + pc).wait_recv()
        big.wait_recv()
        big.wait_send()
        for cp in sends:
            cp.wait_send()
        mine.wait()

    return pl.pallas_call(
        body, name="sibling_exchange", in_specs=[_HBM, _HBM], out_specs=[_HBM, _HBM],
        out_shape=[jax.ShapeDtypeStruct(send.shape, send.dtype), jax.ShapeDtypeStruct((8,) + small.shape, small.dtype)],
        scratch_shapes=[pltpu.SemaphoreType.DMA((8,)), pltpu.SemaphoreType.DMA((8,)), pltpu.SemaphoreType.DMA],
    )(send, small)


def _chip_exchange(part):
    def body(part_ref, recv_ref, send_sems, recv_sems):
        x, y, c = _pos()
        chips = _other_chips(x, y)

        def copy(j):
            px, py = chips[j]
            return pltpu.make_async_remote_copy(src_ref=part_ref.at[2 * px + py], dst_ref=recv_ref.at[j],
                                                send_sem=send_sems.at[j], recv_sem=recv_sems.at[j],
                                                device_id=(px, py, c), device_id_type=MESH)

        sends = [copy(j) for j in range(3)]
        for cp in sends:
            cp.start()
        for j in range(3):
            copy(j).wait_recv()
        for cp in sends:
            cp.wait_send()

    return pl.pallas_call(
        body, name="chip_exchange", in_specs=[_HBM], out_specs=_HBM,
        out_shape=jax.ShapeDtypeStruct((3,) + part.shape[1:], part.dtype),
        scratch_shapes=[pltpu.SemaphoreType.DMA((3,)), pltpu.SemaphoreType.DMA((3,))],
    )(part)


def _sibling_share(half):
    def body(half_ref, full_ref, send_sem, recv_sem, loc_sem):
        x, y, c = _pos()
        mine = pltpu.make_async_copy(half_ref, full_ref.at[c], loc_sem)
        mine.start()
        cp = pltpu.make_async_remote_copy(src_ref=half_ref, dst_ref=full_ref.at[c], send_sem=send_sem, recv_sem=recv_sem,
                                          device_id=(x, y, 1 - c), device_id_type=MESH)
        cp.start()
        pltpu.make_async_remote_copy(src_ref=half_ref, dst_ref=full_ref.at[1 - c], send_sem=send_sem, recv_sem=recv_sem,
                                     device_id=(x, y, 1 - c), device_id_type=MESH).wait_recv()
        cp.wait_send()
        mine.wait()

    return pl.pallas_call(
        body, name="sibling_share", in_specs=[_HBM], out_specs=_HBM,
        out_shape=jax.ShapeDtypeStruct((2,) + half.shape, half.dtype),
        scratch_shapes=[pltpu.SemaphoreType.DMA, pltpu.SemaphoreType.DMA, pltpu.SemaphoreType.DMA],
    )(half)


def _add_sibling(own, recv):
    nb, Rh, Wd = own.shape
    T = Rh // 2

    def body(a_ref, b_ref, o32_ref, o16_ref):
        s = a_ref[...] + b_ref[...].astype(F32)
        o32_ref[...] = s
        o16_ref[...] = s.astype(o16_ref.dtype)

    sp = pl.BlockSpec((1, T, Wd), lambda b, i: (b, i, 0))
    return pl.pallas_call(
        body, name="add_sibling", grid=(nb, Rh // T), in_specs=[sp, sp], out_specs=(sp, sp),
        out_shape=(jax.ShapeDtypeStruct(own.shape, F32), jax.ShapeDtypeStruct(own.shape, recv.dtype)),
        compiler_params=_cparams(("parallel", "parallel")),
    )(own, recv)


def _add_chips(own, recv3, small_all):
    Rh, Wd = own.shape
    T = Rh // 2

    def body(a_ref, b_ref, s_ref, o_ref, so_ref):
        acc = a_ref[...]
        for j in range(3):
            acc = acc + b_ref[j].astype(F32)
        o_ref[...] = acc
        tot = s_ref[0]
        for d in range(1, 8):
            tot = tot + s_ref[d]
        so_ref[...] = tot

    return pl.pallas_call(
        body, name="add_chips", grid=(Rh // T,),
        in_specs=[pl.BlockSpec((T, Wd), lambda i: (i, 0)), pl.BlockSpec((3, T, Wd), lambda i: (0, i, 0)),
                  pl.BlockSpec(small_all.shape, lambda i: (0, 0, 0))],
        out_specs=(pl.BlockSpec((T, Wd), lambda i: (i, 0)), pl.BlockSpec(small_all.shape[1:], lambda i: (0, 0))),
        out_shape=(jax.ShapeDtypeStruct(own.shape, F32), jax.ShapeDtypeStruct(small_all.shape[1:], F32)),
        compiler_params=_cparams(("arbitrary",)),
    )(own, recv3, small_all)


def _adamw(w, g, m, v, name):
    R, Cc = w.shape
    T = max([t for t in range(8, 257, 8) if R % t == 0], default=R)
    c1 = 1.0 / (1.0 - ADAM_B1 ** ADAM_STEP)
    c2 = 1.0 / (1.0 - ADAM_B2 ** ADAM_STEP)

    def body(w_ref, g_ref, m_ref, v_ref, d_ref, mo_ref, vo_ref):
        gv = g_ref[...]
        mn = ADAM_B1 * m_ref[...] + (1.0 - ADAM_B1) * gv
        vn = ADAM_B2 * v_ref[...] + (1.0 - ADAM_B2) * (gv * gv)
        mo_ref[...] = mn
        vo_ref[...] = vn
        d_ref[...] = -ADAM_LR * ((mn * c1) / (jnp.sqrt(vn * c2) + ADAM_EPS) + ADAM_WD * w_ref[...])

    sp = pl.BlockSpec((T, Cc), lambda i: (i, 0))
    sh = jax.ShapeDtypeStruct((R, Cc), F32)
    return pl.pallas_call(
        body, name=name, grid=(R // T,), in_specs=[sp] * 4, out_specs=(sp, sp, sp), out_shape=(sh, sh, sh),
        compiler_params=_cparams(("parallel",)),
    )(w, g, m, v)


SEC_ROWS = 3328
SMALL_ROWS = 8


def _pad_lanes(v, n=D_MODEL):
    return jnp.pad(v, ((0, 0), (0, n - v.shape[1])))


def kernel(x, norm1_w, w_in, conv_qkv_w, a_log, dt_bias, gdn_norm_w, w_out, norm2_w, w_up, ffn_conv_w, w_down, final_norm_w, loss_target, m_norm1_w, m_w_in, m_conv_qkv_w, m_a_log, m_dt_bias, m_gdn_norm_w, m_w_out, m_norm2_w, m_w_up, m_ffn_conv_w, m_w_down, m_final_norm_w, v_norm1_w, v_w_in, v_conv_qkv_w, v_a_log, v_dt_bias, v_gdn_norm_w, v_w_out, v_norm2_w, v_w_up, v_ffn_conv_w, v_w_down, v_final_norm_w):
    c = lax.axis_index("c")
    q = 2 * lax.axis_index("x") + lax.axis_index("y")
    S = x.shape[1]
    g_in, g_out, g_up, g_down, g_conv, g_fconv = _gather_weights(
        [w_in[0].astype(_MXU), w_out[0].astype(_MXU), w_up[0].astype(_MXU), w_down[0].astype(_MXU),
         conv_qkv_w[0], ffn_conv_w[0]])
    w_in_f = jnp.concatenate([g_in[i] for i in range(N_CHIPS)], axis=1)
    wp = jnp.concatenate([w_in_f[:, :P_QKVB], w_in_f[:, P_QKVB + 8:], w_in_f[:, P_QKVB:P_QKVB + 8],
                          jnp.zeros((D_MODEL, P_COLS - P_BA - 8), _MXU)], axis=1)
    w_out_f = g_out.reshape(D_MODEL, D_MODEL)
    w_g = jnp.concatenate([g_up[0], g_up[1]], axis=1)
    w_u = jnp.concatenate([g_up[2], g_up[3]], axis=1)
    w_down_f = g_down.reshape(D_FF, D_MODEL)
    conv_f = jnp.concatenate([g_conv[i] for i in range(N_CHIPS)], axis=1)
    fcw_g = jnp.concatenate([g_fconv[0], g_fconv[1]], axis=1)
    fcw_u = jnp.concatenate([g_fconv[2], g_fconv[3]], axis=1)
    gp = _pad_lanes(jnp.concatenate([a_log, dt_bias], axis=1), 128)
    fnw = final_norm_w[None, :]
    loss_l, dx, g = _local_step(x[0], loss_target[0], norm1_w, norm2_w, fnw, gp, gdn_norm_w, wp, conv_f, w_out_f,
                                w_g, w_u, fcw_g, fcw_u, w_down_f)
    loss = lax.psum(loss_l[0, 0], ("x", "y", "c"))
    d_wp = g["wp"]
    d_win = jnp.concatenate([d_wp[:, :P_QKVB], d_wp[:, P_BA:P_BA + 8], d_wp[:, P_QKVB:P_BA]], axis=1)
    ncol_in, ncol_up, nrow_out, nrow_down, ncol_conv = IN_COLS // N_CHIPS, 2 * D_FF // N_CHIPS, D_MODEL // N_CHIPS, D_FF // N_CHIPS, 3 * GDN_WIDTH // N_CHIPS
    d_up = (g["w_g"], g["w_u"])
    d_fc = (g["fcw_g"], g["fcw_u"])
    d_wo = (g["w_out_a"], g["w_out_b"])
    secs = []
    for j in range(N_CHIPS):
        hj, lj = j // 2, j % 2
        parts = [d_win[:, ncol_in * j:ncol_in * (j + 1)], d_wo[hj][nrow_out * lj:nrow_out * (lj + 1)],
                 d_up[hj][:, ncol_up * lj:ncol_up * (lj + 1)], g["w_down"][nrow_down * j:nrow_down * (j + 1)],
                 g["conv_w"][:, ncol_conv * j:ncol_conv * (j + 1)], d_fc[hj][:, ncol_up * lj:ncol_up * (lj + 1)]]
        flat = jnp.concatenate([p.reshape(-1) for p in parts])
        secs.append(jnp.pad(flat, (0, SEC_ROWS * D_MODEL - flat.shape[0])).reshape(SEC_ROWS, D_MODEL))
    packed = jnp.stack(secs)
    small = jnp.concatenate([g["n1w"], g["n2w"], g["fnw"], _pad_lanes(g["gp"][0:1]), _pad_lanes(g["gnw"]),
                             jnp.zeros((SMALL_ROWS - 5, D_MODEL), F32)], axis=0)
    Rh = SEC_ROWS // 2
    mine = lax.dynamic_slice_in_dim(packed, c * Rh, Rh, axis=1)
    theirs = lax.dynamic_slice_in_dim(packed, (1 - c) * Rh, Rh, axis=1).astype(_MXU)
    got, small_all = _sibling_exchange(theirs, small)
    part32, part16 = _add_sibling(mine, got)
    got3 = _chip_exchange(part16)
    half, small_red = _add_chips(lax.dynamic_index_in_dim(part32, q, axis=0, keepdims=False), got3, small_all)
    red = _sibling_share(half).reshape(-1)
    shapes = [(D_MODEL, ncol_in), (nrow_out, D_MODEL), (D_MODEL, ncol_up), (nrow_down, D_MODEL), (GDN_CONV, ncol_conv),
              (FFN_CONV, ncol_up)]
    gr, off = [], 0
    for shp in shapes:
        n = shp[0] * shp[1]
        gr.append(red[off:off + n].reshape(shp))
        off += n
    g_w_in, g_w_out, g_w_up, g_w_down, g_conv_w, g_fconv_w = gr
    g_n1w, g_n2w, g_fnw = small_red[0:1], small_red[1:2], small_red[2]
    g_alog, g_dtb, g_gnw = small_red[3:4, 0:4], small_red[3:4, 4:8], small_red[4:5, 0:128]
    big = {}
    for nm, w, gg, m, v in (("w_in", w_in, g_w_in, m_w_in, v_w_in), ("conv_qkv_w", conv_qkv_w, g_conv_w, m_conv_qkv_w, v_conv_qkv_w),
                            ("w_out", w_out, g_w_out, m_w_out, v_w_out), ("w_up", w_up, g_w_up, m_w_up, v_w_up),
                            ("ffn_conv_w", ffn_conv_w, g_fconv_w, m_ffn_conv_w, v_ffn_conv_w),
                            ("w_down", w_down, g_w_down, m_w_down, v_w_down)):
        d_, m_, v_ = _adamw(w[0], gg, m[0], v[0], "adamw_" + nm)
        big[nm] = (gg[None], d_[None], m_[None], v_[None])

    def pack_small(n1, n2, fn, al, db, gn):
        return jnp.concatenate([n1, n2, fn[None, :], _pad_lanes(jnp.concatenate([al, db], axis=1)), _pad_lanes(gn),
                                jnp.zeros((SMALL_ROWS - 5, D_MODEL), F32)], axis=0)

    sw = pack_small(norm1_w, norm2_w, final_norm_w, a_log, dt_bias, gdn_norm_w)
    sm = pack_small(m_norm1_w, m_norm2_w, m_final_norm_w, m_a_log, m_dt_bias, m_gdn_norm_w)
    sv = pack_small(v_norm1_w, v_norm2_w, v_final_norm_w, v_a_log, v_dt_bias, v_gdn_norm_w)
    sd, smn, svn = _adamw(sw, small_red, sm, sv, "adamw_small")

    def unpack_small(t):
        return dict(norm1_w=t[0:1], norm2_w=t[1:2], final_norm_w=t[2], a_log=t[3:4, 0:4], dt_bias=t[3:4, 4:8],
                    gdn_norm_w=t[4:5, 0:128])

    sg = dict(norm1_w=g_n1w, norm2_w=g_n2w, final_norm_w=g_fnw, a_log=g_alog, dt_bias=g_dtb, gdn_norm_w=g_gnw)
    sd, smn, svn = unpack_small(sd), unpack_small(smn), unpack_small(svn)
    names = ["norm1_w", "w_in", "conv_qkv_w", "a_log", "dt_bias", "gdn_norm_w", "w_out", "norm2_w", "w_up",
             "ffn_conv_w", "w_down", "final_norm_w"]
    grads = [big[n][0] if n in big else sg[n] for n in names]
    deltas = [big[n][1] if n in big else sd[n] for n in names]
    new_m = [big[n][2] if n in big else smn[n] for n in names]
    new_v = [big[n][3] if n in big else svn[n] for n in names]
    return (loss, dx[None], *grads, *deltas, *new_m, *new_v)
```

```python
import functools
import math

import numpy as np
import jax
import jax.numpy as jnp
from jax import lax
from jax.experimental import pallas as pl
from jax.experimental.pallas import tpu as pltpu

F32 = jnp.float32
BF16 = jnp.bfloat16
_MXU = jnp.bfloat16
_HI = lax.Precision.HIGHEST
EPS = 1e-6
V7X_VMEM_LIMIT = 56 * 1024 * 1024
MESH = pl.DeviceIdType.MESH

D_MODEL = 1024
GDN_HEADS, GDN_DIM, GDN_CHUNK, GDN_CONV = 4, 128, 64, 4
GDN_WIDTH = GDN_HEADS * GDN_DIM
DIL_HEADS, DIL_DIM = 8, 64
DIL_WIDTH = DIL_HEADS * DIL_DIM
D_FF, FFN_CONV = 2816, 3
IN_COLS = 3592
P_COLS = 3840
P_Z, P_QKVB, P_BA = 1536, 2048, 3584
ATT_T = 256
ADAM_LR, ADAM_B1, ADAM_B2, ADAM_EPS, ADAM_WD, ADAM_STEP = 0.001, 0.9, 0.999, 1e-08, 0.01, 10
N_CHIPS = 4


def _cparams(sem=None, vmem=None):
    kw = {}
    if sem is not None:
        kw["dimension_semantics"] = sem
    if vmem is not None:
        kw["vmem_limit_bytes"] = vmem
    return pltpu.CompilerParams(**kw)


def _silu(x):
    return x * jax.nn.sigmoid(x)


def _pick_tile(n, cap):
    best = None
    for t in range(128, min(n, cap) + 1, 128):
        if n % t == 0:
            best = t
    return best or n


def _mm(a, b, mode, *, out_dtype=F32, residual=None, name, b_blocks=False, place=None, into=None, tn=None):
    if mode == "nn":
        M, K = a.shape
        N = b.shape[0] * b.shape[2] if b_blocks else b.shape[1]
    elif mode == "nt":
        (M, K), (N, _) = a.shape, b.shape
    else:
        (K, M), (_, N) = a.shape, b.shape
    tm = _pick_tile(M, 1024)
    tn = b.shape[2] if b_blocks else (tn or _pick_tile(N, 1536))

    def vmem(tm, tn):
        return 2 * (tm * K * a.dtype.itemsize + tn * K * b.dtype.itemsize
                    + tm * tn * (jnp.dtype(out_dtype).itemsize + (4 if residual is not None else 0))) + 3 * tm * tn * 4

    fixed_tn = b_blocks or (place is not None and place[0] == "blocks")
    while vmem(tm, tn) > 40 * 1024 * 1024:
        if (tm >= tn or fixed_tn) and tm % 256 == 0:
            tm //= 2
        elif tn % 256 == 0 and not fixed_tn:
            tn //= 2
        else:
            tm //= 2
    a_spec = pl.BlockSpec((K, tm), lambda j, i: (0, i)) if mode == "tn" else pl.BlockSpec((tm, K), lambda j, i: (i, 0))
    if b_blocks:
        b_spec = pl.BlockSpec((None, K, tn), lambda j, i: (j, 0, 0))
    else:
        b_spec = pl.BlockSpec((tn, K), lambda j, i: (j, 0)) if mode == "nt" else pl.BlockSpec((K, tn), lambda j, i: (0, j))
    r_spec = pl.BlockSpec((tm, tn), lambda j, i: (i, j))
    if place is None:
        o_spec, o_shape = r_spec, (M, N)
    elif place[0] == "rows":
        off = place[2] // tm
        o_spec, o_shape = pl.BlockSpec((tm, tn), lambda j, i: (i + off, j)), (place[1], N)
    else:
        off = place[2]
        o_spec, o_shape = pl.BlockSpec((None, tm, tn), lambda j, i: (j + off, i, 0)), (place[1], M, tn)
    dims = {"nn": (((1,), (0,)), ((), ())), "nt": (((1,), (1,)), ((), ())), "tn": (((0,), (0,)), ((), ()))}[mode]

    def body(*refs):
        a_ref, b_ref = refs[0], refs[1]
        o_ref = refs[-1]
        acc = lax.dot_general(a_ref[...].astype(_MXU), b_ref[...].astype(_MXU), dims, preferred_element_type=F32)
        if residual is not None:
            acc = acc + refs[2][...]
        o_ref[...] = acc.astype(out_dtype)

    ins, specs, alias = [a, b], [a_spec, b_spec], {}
    if residual is not None:
        ins.append(residual)
        specs.append(r_spec)
    if into is not None:
        alias = {len(ins): 0}
        ins.append(into)
        specs.append(pl.BlockSpec(memory_space=pl.ANY))
    return pl.pallas_call(
        body, name=name, grid=(N // tn, M // tm), in_specs=specs, out_specs=o_spec,
        out_shape=jax.ShapeDtypeStruct(o_shape, out_dtype), input_output_aliases=alias,
        compiler_params=_cparams(("parallel", "parallel"), V7X_VMEM_LIMIT),
    )(*ins)


def _mm_nt_blocks(a_list, b4, name):
    M = a_list[0].shape[0]
    nb, N, Kb = b4.shape
    tm, tn = _pick_tile(M, 512), _pick_tile(N, 512)

    def body(a0_ref, a1_ref, b_ref, o_ref):
        acc = None
        for blk in range(nb):
            a_ref = (a0_ref, a1_ref)[blk // 2]
            lo = (blk % 2) * Kb
            t = lax.dot_general(a_ref[:, lo:lo + Kb].astype(_MXU), b_ref[blk].astype(_MXU), (((1,), (1,)), ((), ())),
                                preferred_element_type=F32)
            acc = t if acc is None else acc + t
        o_ref[...] = acc

    a_spec = pl.BlockSpec((tm, 2 * Kb), lambda j, i: (i, 0))
    return pl.pallas_call(
        body, name=name, grid=(N // tn, M // tm),
        in_specs=[a_spec, a_spec, pl.BlockSpec((nb, tn, Kb), lambda j, i: (0, j, 0))],
        out_specs=pl.BlockSpec((tm, tn), lambda j, i: (i, j)), out_shape=jax.ShapeDtypeStruct((M, N), F32),
        compiler_params=_cparams(("parallel", "parallel"), V7X_VMEM_LIMIT),
    )(a_list[0], a_list[1], b4)


def _wp_assemble(g_in):
    nb, Dm, Wb = g_in.shape
    T = 256
    n_lo = P_QKVB - 2 * Wb

    def body(g_ref, o_ref):
        g2 = g_ref[2]
        o_ref[...] = jnp.concatenate(
            [g_ref[0], g_ref[1], g2[:, :n_lo], g2[:, n_lo + 8:], g_ref[3], g2[:, n_lo:n_lo + 8],
             jnp.zeros((T, P_COLS - P_BA - 8), g_in.dtype)], axis=1)

    return pl.pallas_call(
        body, name="wp_assemble", grid=(Dm // T,), in_specs=[pl.BlockSpec((nb, T, Wb), lambda i: (0, i, 0))],
        out_specs=pl.BlockSpec((T, P_COLS), lambda i: (i, 0)), out_shape=jax.ShapeDtypeStruct((Dm, P_COLS), g_in.dtype),
        compiler_params=_cparams(("parallel",)),
    )(g_in)


def _win_split(d_wp):
    Dm = d_wp.shape[0]
    Wb = IN_COLS // N_CHIPS
    T = 256

    def body(x_ref, o_ref):
        xv = x_ref[...]
        o_ref[0] = xv[:, 0:Wb]
        o_ref[1] = xv[:, Wb:2 * Wb]
        o_ref[2] = jnp.concatenate([xv[:, 2 * Wb:P_QKVB], xv[:, P_BA:P_BA + 8], xv[:, P_QKVB:3 * Wb - 8]], axis=1)
        o_ref[3] = xv[:, 3 * Wb - 8:P_BA]

    return pl.pallas_call(
        body, name="win_split", grid=(Dm // T,), in_specs=[pl.BlockSpec((T, P_COLS), lambda i: (i, 0))],
        out_specs=pl.BlockSpec((N_CHIPS, T, Wb), lambda i: (0, i, 0)),
        out_shape=jax.ShapeDtypeStruct((N_CHIPS, Dm, Wb), F32), compiler_params=_cparams(("parallel",)),
    )(d_wp)


def _rmsnorm_fwd(x, w, name):
    S, D = x.shape
    T = _pick_tile(S, 512)

    def body(x_ref, w_ref, o_ref):
        xv = x_ref[...]
        rs = lax.rsqrt(jnp.mean(xv * xv, axis=-1, keepdims=True) + EPS)
        o_ref[...] = (xv * rs * w_ref[...]).astype(o_ref.dtype)

    return pl.pallas_call(
        body, name=name, grid=(S // T,),
        in_specs=[pl.BlockSpec((T, D), lambda i: (i, 0)), pl.BlockSpec((1, D), lambda i: (0, 0))],
        out_specs=pl.BlockSpec((T, D), lambda i: (i, 0)),
        out_shape=jax.ShapeDtypeStruct((S, D), _MXU),
        compiler_params=_cparams(("parallel",)),
    )(x, w)


def _rmsnorm_bwd(dh, x, w, dres, name):
    S, D = x.shape
    T = _pick_tile(S, 512)

    def body(dh_ref, x_ref, w_ref, dres_ref, dx_ref, dw_ref):
        xv = x_ref[...]
        rs = lax.rsqrt(jnp.mean(xv * xv, axis=-1, keepdims=True) + EPS)
        xn = xv * rs
        dhv = dh_ref[...]
        dxn = dhv * w_ref[...]
        dx_ref[...] = dres_ref[...] + rs * (dxn - xn * jnp.mean(dxn * xn, axis=-1, keepdims=True))

        @pl.when(pl.program_id(0) == 0)
        def _():
            dw_ref[...] = jnp.zeros_like(dw_ref)

        dw_ref[...] += jnp.sum(dhv * xn, axis=0, keepdims=True)

    row = pl.BlockSpec((T, D), lambda i: (i, 0))
    vec = pl.BlockSpec((1, D), lambda i: (0, 0))
    return pl.pallas_call(
        body, name=name, grid=(S // T,), in_specs=[row, row, vec, row], out_specs=(row, vec),
        out_shape=(jax.ShapeDtypeStruct((S, D), F32), jax.ShapeDtypeStruct((1, D), F32)),
        compiler_params=_cparams(("arbitrary",)),
    )(dh, x, w, dres)


def _loss_head(x3, w, tgt, name):
    S, D = x3.shape
    T = _pick_tile(S, 512)

    def body(x_ref, w_ref, t_ref, loss_ref, dx_ref, dw_ref):
        xv = x_ref[...]
        rs = lax.rsqrt(jnp.mean(xv * xv, axis=-1, keepdims=True) + EPS)
        xn = xv * rs
        err = xn * w_ref[...] - t_ref[...]
        dy = err * (1.0 / D)
        dxn = dy * w_ref[...]
        dx_ref[...] = rs * (dxn - xn * jnp.mean(dxn * xn, axis=-1, keepdims=True))

        @pl.when(pl.program_id(0) == 0)
        def _():
            dw_ref[...] = jnp.zeros_like(dw_ref)
            loss_ref[...] = jnp.zeros_like(loss_ref)

        dw_ref[...] += jnp.sum(dy * xn, axis=0, keepdims=True)
        part = jnp.sum(jnp.sum(err * err, axis=-1, keepdims=True), axis=0, keepdims=True) * (0.5 / D)
        loss_ref[...] += jnp.broadcast_to(part, loss_ref.shape)

    row = pl.BlockSpec((T, D), lambda i: (i, 0))
    vec = pl.BlockSpec((1, D), lambda i: (0, 0))
    return pl.pallas_call(
        body, name=name, grid=(S // T,), in_specs=[row, vec, row],
        out_specs=(pl.BlockSpec((8, 128), lambda i: (0, 0)), row, vec),
        out_shape=(jax.ShapeDtypeStruct((8, 128), F32), jax.ShapeDtypeStruct((S, D), F32), jax.ShapeDtypeStruct((1, D), F32)),
        compiler_params=_cparams(("arbitrary",)),
    )(x3, w, tgt)


def _conv_taps(ext, w, K, T):
    out = None
    for i in range(K):
        lo = 8 - (K - 1) + i
        term = ext[lo:lo + T, :] * w[i:i + 1, :]
        out = term if out is None else out + term
    return out


def _conv_taps_t(ext, w, K, T):
    out = None
    for i in range(K):
        lo = (K - 1) - i
        term = ext[lo:lo + T, :] * w[i:i + 1, :]
        out = term if out is None else out + term
    return out


def _tri_masks(C):
    r = lax.broadcasted_iota(jnp.int32, (C, C), 0)
    c = lax.broadcasted_iota(jnp.int32, (C, C), 1)
    return r == c, r >= c, r > c, r <= c


_NN, _NT, _TN = ((1,), (0,)), ((1,), (1,)), ((0,), (0,))
_GDN_PASSES = dict(qk=1, inv=1, sol=1, scan=1, bwd=1)


def _bdot_raw(a, b, kind, passes):
    dims = ({"NN": ((2,), (1,)), "NT": ((2,), (2,)), "TN": ((1,), (1,))}[kind], ((0,), (0,)))
    if passes == 0:
        return lax.dot_general(a, b, dims, precision=_HI, preferred_element_type=F32)
    ah, bh = a.astype(BF16), b.astype(BF16)
    out = lax.dot_general(ah, bh, dims, preferred_element_type=F32)
    if passes == 3:
        al, bl = (a - ah.astype(F32)).astype(BF16), (b - bh.astype(F32)).astype(BF16)
        out = out + lax.dot_general(ah, bl, dims, preferred_element_type=F32) + lax.dot_general(al, bh, dims, preferred_element_type=F32)
    return out


@functools.partial(jax.custom_vjp, nondiff_argnums=(2, 3))
def _bdot(a, b, kind, passes):
    return _bdot_raw(a, b, kind, passes)


def _bdot_fwd(a, b, kind, passes):
    return _bdot_raw(a, b, kind, passes), (a, b)


def _bdot_bwd(kind, passes, res, ct):
    a, b = res
    if kind == "NN":
        return _bdot_raw(ct, b, "NT", passes), _bdot_raw(a, ct, "TN", passes)
    if kind == "NT":
        return _bdot_raw(ct, b, "NN", passes), _bdot_raw(ct, a, "TN", passes)
    return _bdot_raw(b, ct, "NT", passes), _bdot_raw(a, ct, "NN", passes)


_bdot.defvjp(_bdot_fwd, _bdot_bwd)


def _softplus(x):
    return jnp.maximum(x, 0.0) + jnp.log(1.0 + jnp.exp(-jnp.abs(x)))


def _gdn_stage1(cq, ck, cv, b_col, a_col, alog, dtb, dot=_bdot_raw):
    C = cq.shape[1]
    eye, incl, strict, incl_t = _tri_masks(C)
    qn = cq * lax.rsqrt(jnp.sum(cq * cq, axis=-1, keepdims=True) + EPS) * (GDN_DIM ** -0.5)
    kn = ck * lax.rsqrt(jnp.sum(ck * ck, axis=-1, keepdims=True) + EPS)
    beta = jax.nn.sigmoid(b_col)
    g = -jnp.exp(alog) * _softplus(a_col + dtb)
    g_row = jnp.sum(jnp.where(eye, g, 0.0), axis=1, keepdims=True)
    beta_row = jnp.sum(jnp.where(eye, beta, 0.0), axis=1, keepdims=True)
    gc_col = jnp.sum(jnp.where(incl, g_row, 0.0), axis=2, keepdims=True)
    gc_row = jnp.sum(jnp.where(incl_t, g, 0.0), axis=1, keepdims=True)
    dec = jnp.where(incl, jnp.exp(jnp.where(incl, gc_col - gc_row, 0.0)), 0.0)
    kk = dot(kn, kn, "NT", _GDN_PASSES["qk"])
    qk = dot(qn, kn, "NT", _GDN_PASSES["qk"])
    lmat = jnp.where(strict, dec * kk * beta_row, 0.0)
    attn = dec * qk * beta_row
    gam = jnp.exp(gc_col)
    gc_last = gc_col[:, C - 1:C, :]
    k_end = kn * (jnp.exp(gc_last - gc_col) * beta)
    return lmat, cv, gam * kn, gam * qn, attn, k_end, jnp.exp(gc_last)


def _tri_inv(lmat):
    C = lmat.shape[1]
    eye = _tri_masks(C)[0]
    ps = _GDN_PASSES["inv"]
    p = jnp.where(eye, 1.0, 0.0) - lmat
    lp = _bdot_raw(lmat, lmat, "NN", ps)
    n = int(math.log2(C))
    for s in range(1, n):
        p = p + _bdot_raw(p, lp, "NN", ps)
        if s < n - 1:
            lp = _bdot_raw(lp, lp, "NN", ps)
    return p


def _gated_norm(o, z, gnw):
    on = o * lax.rsqrt(jnp.mean(o * o, axis=-1, keepdims=True) + EPS) * gnw
    return on * _silu(z)


GDN_PG = 2
GDN_SG = 4


def _gdn_pairs(c, ba, gp, G):
    C, W, H = GDN_CHUNK, GDN_WIDTH, GDN_HEADS
    pairs = [(j, h) for j in range(G) for h in range(H)]
    cq, ck, cv = (jnp.stack([c[C * j:C * (j + 1), o + GDN_DIM * h:o + GDN_DIM * (h + 1)] for j, h in pairs]) for o in (0, W, 2 * W))
    b_col = jnp.stack([ba[C * j:C * (j + 1), h:h + 1] for j, h in pairs])
    a_col = jnp.stack([ba[C * j:C * (j + 1), H + h:H + h + 1] for j, h in pairs])
    alog = jnp.stack([gp[0:1, h:h + 1] for j, h in pairs])
    dtb = jnp.stack([gp[0:1, H + h:H + h + 1] for j, h in pairs])
    return pairs, (cq, ck, cv, b_col, a_col, alog, dtb)


def _gdn_pre_specs(S, G):
    C = GDN_CHUNK
    T = C * G
    return dict(
        cur=pl.BlockSpec((T, 3 * GDN_WIDTH), lambda i: (i, 0)),
        prev=pl.BlockSpec((8, 3 * GDN_WIDTH), lambda i: (jnp.maximum(i * (T // 8) - 1, 0), 0)),
        ba=pl.BlockSpec((T, 128), lambda i: (i, P_BA // 128)),
        cw=pl.BlockSpec((GDN_CONV, 3 * GDN_WIDTH), lambda i: (0, 0)),
        vec=pl.BlockSpec((1, 128), lambda i: (0, 0)),
        hd=pl.BlockSpec((GDN_HEADS, T, GDN_DIM), lambda i: (0, i, 0)),
        hc=pl.BlockSpec((GDN_HEADS, T, C), lambda i: (0, i, 0)),
        ge=pl.BlockSpec((G, GDN_HEADS, 8, 128), lambda i: (i, 0, 0, 0)),
    )


def _hd_shape(S, last=GDN_DIM):
    return jax.ShapeDtypeStruct((GDN_HEADS, S, last), F32)


def _gdn_pre(proj, conv_w, gp):
    S = proj.shape[0]
    C, G = GDN_CHUNK, GDN_PG
    nc = S // C
    sp = _gdn_pre_specs(S, G)

    def body(cur_ref, prev_ref, ba_ref, cw_ref, gp_ref, uv_ref, wk_ref, qd_ref, ke_ref, at_ref, ti_ref, ge_ref):
        prev = prev_ref[...] * jnp.where(pl.program_id(0) == 0, 0.0, 1.0)
        c = _silu(_conv_taps(jnp.concatenate([prev, cur_ref[...]], axis=0), cw_ref[...], GDN_CONV, C * G))
        pairs, args = _gdn_pairs(c, ba_ref[...], gp_ref[...], G)
        lmat, v, rk, q_dec, attn, k_end, g_end = _gdn_stage1(*args)
        t = _tri_inv(lmat)
        u_v = _bdot_raw(t, v, "NN", _GDN_PASSES["sol"])
        w_k = _bdot_raw(t, rk, "NN", _GDN_PASSES["sol"])
        for b, (j, h) in enumerate(pairs):
            rows = slice(C * j, C * (j + 1))
            uv_ref[h, rows, :] = u_v[b]
            wk_ref[h, rows, :] = w_k[b]
            qd_ref[h, rows, :] = q_dec[b]
            ke_ref[h, rows, :] = k_end[b]
            at_ref[h, rows, :] = attn[b]
            ti_ref[h, rows, :] = t[b]
            ge_ref[j, h] = jnp.broadcast_to(g_end[b], (8, 128))

    return pl.pallas_call(
        body, name="gdn_pre", grid=(nc // G,),
        in_specs=[sp["cur"], sp["prev"], sp["ba"], sp["cw"], sp["vec"]],
        out_specs=(sp["hd"], sp["hd"], sp["hd"], sp["hd"], sp["hc"], sp["hc"], sp["ge"]),
        out_shape=(_hd_shape(S), _hd_shape(S), _hd_shape(S), _hd_shape(S), _hd_shape(S, C), _hd_shape(S, C),
                   jax.ShapeDtypeStruct((nc, GDN_HEADS, 8, 128), F32)),
        compiler_params=_cparams(("parallel",)),
    )(proj, proj, proj, conv_w, gp)


def _gdn_scan_specs(S, G, rev):
    C = GDN_CHUNK
    T = C * G
    n = S // T
    ci = (lambda i: n - 1 - i) if rev else (lambda i: i)
    return dict(
        hd=pl.BlockSpec((GDN_HEADS, T, GDN_DIM), lambda i: (0, ci(i), 0)),
        hc=pl.BlockSpec((GDN_HEADS, T, C), lambda i: (0, ci(i), 0)),
        ge=pl.BlockSpec((G, GDN_HEADS, 8, 128), lambda i: (ci(i), 0, 0, 0)),
        z=pl.BlockSpec((T, GDN_WIDTH), lambda i: (ci(i), P_Z // GDN_WIDTH)),
        oa=pl.BlockSpec((T, GDN_WIDTH), lambda i: (ci(i), 0)),
        vec=pl.BlockSpec((1, 128), lambda i: (0, 0)),
        st=pl.BlockSpec((G, GDN_HEADS, GDN_DIM, GDN_DIM), lambda i: (ci(i), 0, 0, 0)),
    )


def _gdn_scan(u_v, w_k, q_dec, k_end, attn, g_end, proj, gnw):
    S = proj.shape[0]
    C, G = GDN_CHUNK, GDN_SG
    nc = S // C
    sp = _gdn_scan_specs(S, G, False)
    ps = _GDN_PASSES["scan"]

    def body(uv_ref, wk_ref, qd_ref, ke_ref, at_ref, ge_ref, z_ref, gnw_ref, oa_ref, st_ref, s_scr):
        @pl.when(pl.program_id(0) == 0)
        def _():
            s_scr[...] = jnp.zeros_like(s_scr)

        for j in range(G):
            rows = slice(C * j, C * (j + 1))
            st = s_scr[...]
            st_ref[j] = st
            u = uv_ref[:, rows, :] - _bdot_raw(wk_ref[:, rows, :], st, "NN", ps)
            o = _bdot_raw(qd_ref[:, rows, :], st, "NN", ps) + _bdot_raw(at_ref[:, rows, :], u, "NN", ps)
            s_scr[...] = ge_ref[j][:, 0:1, 0:1] * st + _bdot_raw(ke_ref[:, rows, :], u, "TN", ps)
            for h in range(GDN_HEADS):
                cols = slice(GDN_DIM * h, GDN_DIM * (h + 1))
                oa_ref[rows, cols] = _gated_norm(o[h], z_ref[rows, cols], gnw_ref[...])

    return pl.pallas_call(
        body, name="gdn_scan", grid=(nc // G,),
        in_specs=[sp["hd"], sp["hd"], sp["hd"], sp["hd"], sp["hc"], sp["ge"], sp["z"], sp["vec"]],
        out_specs=(sp["oa"], sp["st"]),
        out_shape=(jax.ShapeDtypeStruct((S, GDN_WIDTH), F32),
                   jax.ShapeDtypeStruct((nc, GDN_HEADS, GDN_DIM, GDN_DIM), F32)),
        scratch_shapes=[pltpu.VMEM((GDN_HEADS, GDN_DIM, GDN_DIM), F32)],
        compiler_params=_cparams(("arbitrary",)),
    )(u_v, w_k, q_dec, k_end, attn, g_end, proj, gnw)


def _gdn_scan_bwd(u_v, w_k, q_dec, k_end, attn, g_end, proj, gnw, states, d_oa):
    S = proj.shape[0]
    C, G = GDN_CHUNK, GDN_SG
    nc = S // C
    sp = _gdn_scan_specs(S, G, True)
    ps, pb = _GDN_PASSES["scan"], _GDN_PASSES["bwd"]

    def body(uv_ref, wk_ref, qd_ref, ke_ref, at_ref, ge_ref, z_ref, gnw_ref, st_ref, doa_ref,
             duv_ref, dwk_ref, dqd_ref, dke_ref, dat_ref, dge_ref, dz_ref, dgnw_ref, ds_scr):
        @pl.when(pl.program_id(0) == 0)
        def _():
            ds_scr[...] = jnp.zeros_like(ds_scr)
            dgnw_ref[...] = jnp.zeros_like(dgnw_ref)

        dgnw = jnp.zeros((1, 128), F32)
        for j in reversed(range(G)):
            rows = slice(C * j, C * (j + 1))
            st = st_ref[j]
            wk, qd, ke, at = wk_ref[:, rows, :], qd_ref[:, rows, :], ke_ref[:, rows, :], at_ref[:, rows, :]
            u = uv_ref[:, rows, :] - _bdot_raw(wk, st, "NN", ps)
            o = _bdot_raw(qd, st, "NN", ps) + _bdot_raw(at, u, "NN", ps)
            dos = []
            for h in range(GDN_HEADS):
                cols = slice(GDN_DIM * h, GDN_DIM * (h + 1))
                _, vjp2 = jax.vjp(_gated_norm, o[h], z_ref[rows, cols], gnw_ref[...])
                do_h, dz_h, dgn = vjp2(doa_ref[rows, cols])
                dz_ref[rows, cols] = dz_h
                dgnw = dgnw + dgn
                dos.append(do_h)
            do = jnp.stack(dos)
            ds_new = ds_scr[...]
            du = _bdot_raw(at, do, "TN", pb) + _bdot_raw(ke, ds_new, "NN", pb)
            duv_ref[:, rows, :] = du
            dat_ref[:, rows, :] = _bdot_raw(do, u, "NT", pb)
            dqd_ref[:, rows, :] = _bdot_raw(do, st, "NT", pb)
            dke_ref[:, rows, :] = _bdot_raw(u, ds_new, "NT", pb)
            dwk_ref[:, rows, :] = -_bdot_raw(du, st, "NT", pb)
            d_ge = jnp.sum(jnp.sum(st * ds_new, axis=2, keepdims=True), axis=1, keepdims=True)
            dge_ref[j] = jnp.broadcast_to(d_ge, (GDN_HEADS, 8, 128))
            ds_scr[...] = ge_ref[j][:, 0:1, 0:1] * ds_new + _bdot_raw(qd, do, "TN", pb) - _bdot_raw(wk, du, "TN", pb)
        dgnw_ref[...] += dgnw

    return pl.pallas_call(
        body, name="gdn_scan_bwd", grid=(nc // G,),
        in_specs=[sp["hd"], sp["hd"], sp["hd"], sp["hd"], sp["hc"], sp["ge"], sp["z"], sp["vec"], sp["st"], sp["oa"]],
        out_specs=(sp["hd"], sp["hd"], sp["hd"], sp["hd"], sp["hc"], sp["ge"], sp["oa"], sp["vec"]),
        out_shape=(_hd_shape(S), _hd_shape(S), _hd_shape(S), _hd_shape(S), _hd_shape(S, C),
                   jax.ShapeDtypeStruct((nc, GDN_HEADS, 8, 128), F32), jax.ShapeDtypeStruct((S, GDN_WIDTH), F32),
                   jax.ShapeDtypeStruct((1, 128), F32)),
        scratch_shapes=[pltpu.VMEM((GDN_HEADS, GDN_DIM, GDN_DIM), F32)],
        compiler_params=_cparams(("arbitrary",)),
    )(u_v, w_k, q_dec, k_end, attn, g_end, proj, gnw, states, d_oa)


def _gdn_post(proj, conv_w, gp, tinv, u_v, w_k, d_uv, d_wk, d_qd, d_ke, d_at, d_ge):
    S = proj.shape[0]
    C, G = GDN_CHUNK, GDN_PG
    nc = S // C
    sp = _gdn_pre_specs(S, G)
    pb = _GDN_PASSES["bwd"]

    def body(cur_ref, prev_ref, ba_ref, cw_ref, gp_ref, ti_ref, uv_ref, wk_ref, duv_ref, dwk_ref, dqd_ref, dke_ref,
             dat_ref, dge_ref, dpre_ref, dba_ref, dgp_ref):
        i = pl.program_id(0)

        @pl.when(i == 0)
        def _():
            dgp_ref[...] = jnp.zeros_like(dgp_ref)

        prev = prev_ref[...] * jnp.where(i == 0, 0.0, 1.0)
        pre = _conv_taps(jnp.concatenate([prev, cur_ref[...]], axis=0), cw_ref[...], GDN_CONV, C * G)
        sg = jax.nn.sigmoid(pre)
        dsilu = sg * (1.0 + pre * (1.0 - sg))
        pairs, args = _gdn_pairs(pre * sg, ba_ref[...], gp_ref[...], G)
        _, vjp1 = jax.vjp(functools.partial(_gdn_stage1, dot=_bdot), *args)

        def take(ref):
            return jnp.stack([ref[h, C * j:C * (j + 1), :] for j, h in pairs])

        t, u_v, w_k = take(ti_ref), take(uv_ref), take(wk_ref)
        d_v = _bdot_raw(t, take(duv_ref), "TN", pb)
        d_rk = _bdot_raw(t, take(dwk_ref), "TN", pb)
        d_l = -(_bdot_raw(d_v, u_v, "NT", pb) + _bdot_raw(d_rk, w_k, "NT", pb))
        d_ge = jnp.stack([dge_ref[j, h][0:1, 0:1] for j, h in pairs])
        dcq, dck, dcv, db, da, dalog, ddtb = vjp1((d_l, d_v, d_rk, take(dqd_ref), take(dat_ref), take(dke_ref), d_ge))
        lane = lax.broadcasted_iota(jnp.int32, (C, 128), 1)
        lane1 = lax.broadcasted_iota(jnp.int32, (1, 128), 1)
        dgp = jnp.zeros((1, 128), F32)
        for j in range(G):
            rows = slice(C * j, C * (j + 1))
            dba = jnp.zeros((C, 128), F32)
            for h in range(GDN_HEADS):
                b = GDN_HEADS * j + h
                for o_, dcx in ((0, dcq), (GDN_WIDTH, dck), (2 * GDN_WIDTH, dcv)):
                    cols = slice(o_ + GDN_DIM * h, o_ + GDN_DIM * (h + 1))
                    dpre_ref[rows, cols] = dcx[b] * dsilu[rows, cols]
                dba = dba + jnp.where(lane == h, db[b], 0.0) + jnp.where(lane == GDN_HEADS + h, da[b], 0.0)
                dgp = dgp + jnp.where(lane1 == h, dalog[b], 0.0) + jnp.where(lane1 == GDN_HEADS + h, ddtb[b], 0.0)
            dba_ref[rows, :] = dba
        dgp_ref[0:1, :] += dgp

    T = C * G
    return pl.pallas_call(
        body, name="gdn_post", grid=(nc // G,),
        in_specs=[sp["cur"], sp["prev"], sp["ba"], sp["cw"], sp["vec"], sp["hc"], sp["hd"], sp["hd"], sp["hd"], sp["hd"],
                  sp["hd"], sp["hd"], sp["hc"], sp["ge"]],
        out_specs=(sp["cur"], pl.BlockSpec((T, 128), lambda i: (i, 0)), pl.BlockSpec((8, 128), lambda i: (0, 0))),
        out_shape=(jax.ShapeDtypeStruct((S, 3 * GDN_WIDTH), F32), jax.ShapeDtypeStruct((S, 128), F32),
                   jax.ShapeDtypeStruct((8, 128), F32)),
        compiler_params=_cparams(("arbitrary",)),
    )(proj, proj, proj, conv_w, gp, tinv, u_v, w_k, d_uv, d_wk, d_qd, d_ke, d_at, d_ge)


def _conv_bwd(dpre, x, xcol0, w, K, name, tc):
    S, Cc = dpre.shape
    T = _pick_tile(S, 256)
    nt, ncol = S // T, Cc // tc
    xo = xcol0 // tc

    def body(d_ref, dn_ref, x_ref, xp_ref, w_ref, dx_ref, dw_ref):
        i = pl.program_id(1)
        dn = dn_ref[...] * jnp.where(i == nt - 1, 0.0, 1.0)
        dv = d_ref[...]
        ext_d = jnp.concatenate([dv, dn], axis=0)
        wv = w_ref[...]
        dx_ref[...] = _conv_taps_t(ext_d, wv, K, T).astype(dx_ref.dtype)
        xp = xp_ref[...] * jnp.where(i == 0, 0.0, 1.0)
        ext_x = jnp.concatenate([xp, x_ref[...]], axis=0)

        @pl.when(i == 0)
        def _():
            dw_ref[...] = jnp.zeros_like(dw_ref)

        for k in range(K):
            lo = 8 - (K - 1) + k
            dw_ref[k:k + 1, :] += jnp.sum(dv * ext_x[lo:lo + T, :], axis=0, keepdims=True)

    r8 = T // 8
    return pl.pallas_call(
        body, name=name, grid=(ncol, nt),
        in_specs=[pl.BlockSpec((T, tc), lambda j, i: (i, j)),
                  pl.BlockSpec((8, tc), lambda j, i: (jnp.minimum((i + 1) * r8, S // 8 - 1), j)),
                  pl.BlockSpec((T, tc), lambda j, i: (i, j + xo)),
                  pl.BlockSpec((8, tc), lambda j, i: (jnp.maximum(i * r8 - 1, 0), j + xo)),
                  pl.BlockSpec((K, tc), lambda j, i: (0, j))],
        out_specs=(pl.BlockSpec((T, tc), lambda j, i: (i, j)), pl.BlockSpec((K, tc), lambda j, i: (0, j))),
        out_shape=(jax.ShapeDtypeStruct((S, Cc), _MXU), jax.ShapeDtypeStruct((K, Cc), F32)),
        compiler_params=_cparams(("parallel", "arbitrary")),
    )(dpre, dpre, x, x, w)


def _dil_bias(nt):
    T = ATT_T
    d = (np.arange(nt)[:, None, None] * T + np.arange(T)[None, :, None] - np.arange(T)[None, None, :])
    cnt = ((d >= 0) & (d <= 128)).astype(np.float64) + ((d >= 0) & (d % 4 == 0) & (d <= 512)) + ((d >= 0) & (d % 16 == 0))
    return jnp.asarray(np.where(cnt > 0, np.log(np.maximum(cnt, 1.0)), -1e30), dtype=F32)


def _attn_fwd(proj):
    S = proj.shape[0]
    T = ATT_T
    nt = S // T
    bias = _dil_bias(nt)
    scale = DIL_DIM ** -0.5
    npair = DIL_WIDTH // 128
    qb0, kb0, vb0 = P_QKVB // 128, (P_QKVB + DIL_WIDTH) // 128, (P_QKVB + 2 * DIL_WIDTH) // 128

    def body(q_ref, k_ref, v_ref, b_ref, o_ref, lse_ref):
        i = pl.program_id(1)
        qs = (q_ref[...] * scale).astype(_MXU)

        def step(j, carry):
            kt = k_ref[pl.ds(pl.multiple_of(j * T, T), T), :].astype(_MXU)
            vt = v_ref[pl.ds(pl.multiple_of(j * T, T), T), :].astype(_MXU)
            bt = b_ref[i - j]
            out = []
            for hh in range(2):
                m, l, acc = carry[hh]
                sl = slice(hh * DIL_DIM, (hh + 1) * DIL_DIM)
                s = lax.dot_general(qs[:, sl], kt[:, sl], (_NT, ((), ())), preferred_element_type=F32) + bt
                m_new = jnp.maximum(m, jnp.max(s, axis=-1, keepdims=True))
                p = jnp.exp(s - m_new)
                a = jnp.exp(m - m_new)
                l = a * l + jnp.sum(p, axis=-1, keepdims=True)
                acc = a * acc + lax.dot_general(p.astype(_MXU), vt[:, sl], (_NN, ((), ())), preferred_element_type=F32)
                out.append((m_new, l, acc))
            return tuple(out)

        init = tuple((jnp.full((T, 1), -1e30, F32), jnp.zeros((T, 1), F32), jnp.zeros((T, DIL_DIM), F32)) for _ in range(2))
        res = lax.fori_loop(0, i + 1, step, init)
        for hh in range(2):
            m, l, acc = res[hh]
            sl = slice(hh * DIL_DIM, (hh + 1) * DIL_DIM)
            o_ref[:, sl] = acc / l
            lse_ref[:, sl] = jnp.broadcast_to(m + jnp.log(l), (T, DIL_DIM))

    return pl.pallas_call(
        body, name="attn_fwd", grid=(npair, nt),
        in_specs=[pl.BlockSpec((T, 128), lambda p, i: (i, qb0 + p)),
                  pl.BlockSpec((S, 128), lambda p, i: (0, kb0 + p)),
                  pl.BlockSpec((S, 128), lambda p, i: (0, vb0 + p)),
                  pl.BlockSpec((nt, T, T), lambda p, i: (0, 0, 0))],
        out_specs=(pl.BlockSpec((T, 128), lambda p, i: (i, p)), pl.BlockSpec((T, 128), lambda p, i: (i, p))),
        out_shape=(jax.ShapeDtypeStruct((S, DIL_WIDTH), F32), jax.ShapeDtypeStruct((S, DIL_WIDTH), F32)),
        compiler_params=_cparams(("parallel", "parallel")),
    )(proj, proj, proj, bias)


def _attn_bwd(proj, o_b, lse, d_ob):
    S = proj.shape[0]
    T = ATT_T
    nt = S // T
    bias = _dil_bias(nt)
    scale = DIL_DIM ** -0.5
    npair = DIL_WIDTH // 128
    qb0, kb0, vb0 = P_QKVB // 128, (P_QKVB + DIL_WIDTH) // 128, (P_QKVB + 2 * DIL_WIDTH) // 128

    def body(q_ref, k_ref, v_ref, o_ref, lse_ref, do_ref, b_ref, dq_ref, dk_ref, dv_ref, dq_scr):
        j = pl.program_id(1)

        @pl.when(j == 0)
        def _():
            dq_scr[...] = jnp.zeros_like(dq_scr)

        kt = k_ref[...].astype(_MXU)
        vt = v_ref[...].astype(_MXU)

        def step(i, carry):
            rows = pl.ds(pl.multiple_of(i * T, T), T)
            qs = (q_ref[rows, :] * scale).astype(_MXU)
            dov = do_ref[rows, :]
            prod = dov * o_ref[rows, :]
            lsev = lse_ref[rows, :]
            dob = dov.astype(_MXU)
            bt = b_ref[i - j]
            out = []
            dqs = []
            for hh in range(2):
                dk, dv = carry[hh]
                sl = slice(hh * DIL_DIM, (hh + 1) * DIL_DIM)
                s = lax.dot_general(qs[:, sl], kt[:, sl], (_NT, ((), ())), preferred_element_type=F32) + bt
                p = jnp.exp(s - lsev[:, hh * DIL_DIM:hh * DIL_DIM + 1])
                delta = jnp.sum(prod[:, sl], axis=-1, keepdims=True)
                dp = lax.dot_general(dob[:, sl], vt[:, sl], (_NT, ((), ())), preferred_element_type=F32)
                ds = (p * (dp - delta)).astype(_MXU)
                dv = dv + lax.dot_general(p.astype(_MXU), dob[:, sl], (_TN, ((), ())), preferred_element_type=F32)
                dk = dk + lax.dot_general(ds, qs[:, sl], (_TN, ((), ())), preferred_element_type=F32)
                dqs.append(lax.dot_general(ds, kt[:, sl], (_NN, ((), ())), preferred_element_type=F32) * scale)
                out.append((dk, dv))
            dq_scr[rows, :] += jnp.concatenate(dqs, axis=1)
            return tuple(out)

        init = tuple((jnp.zeros((T, DIL_DIM), F32), jnp.zeros((T, DIL_DIM), F32)) for _ in range(2))
        res = lax.fori_loop(j, nt, step, init)
        dk_ref[...] = jnp.concatenate([res[0][0], res[1][0]], axis=1).astype(dk_ref.dtype)
        dv_ref[...] = jnp.concatenate([res[0][1], res[1][1]], axis=1).astype(dv_ref.dtype)

        @pl.when(j == nt - 1)
        def _():
            dq_ref[...] = dq_scr[...].astype(dq_ref.dtype)

    full = lambda c0: pl.BlockSpec((S, 128), lambda p, j: (0, c0 + p))
    tile = lambda c0: pl.BlockSpec((T, 128), lambda p, j: (j, c0 + p))
    out3 = jax.ShapeDtypeStruct((S, DIL_WIDTH), _MXU)
    return pl.pallas_call(
        body, name="attn_bwd", grid=(npair, nt),
        in_specs=[full(qb0), tile(kb0), tile(vb0), full(0), full(0), full(0),
                  pl.BlockSpec((nt, T, T), lambda p, j: (0, 0, 0))],
        out_specs=(full(0), tile(0), tile(0)),
        out_shape=(out3, out3, out3),
        scratch_shapes=[pltpu.VMEM((S, 128), F32)],
        compiler_params=_cparams(("parallel", "arbitrary")),
    )(proj, proj, proj, o_b, lse, d_ob, bias)


def _ffn_act(up, cw):
    S, Cc = up.shape[0], up.shape[1] // 2
    T, tc = _pick_tile(S, 256), _pick_tile(Cc, 1536)
    r8 = T // 8
    nct = Cc // tc

    def body(g_ref, gp_ref, u_ref, up_ref, wg_ref, wu_ref, o_ref):
        keep = jnp.where(pl.program_id(1) == 0, 0.0, 1.0)
        cg = _conv_taps(jnp.concatenate([gp_ref[...] * keep, g_ref[...]], axis=0), wg_ref[...], FFN_CONV, T)
        cu = _conv_taps(jnp.concatenate([up_ref[...] * keep, u_ref[...]], axis=0), wu_ref[...], FFN_CONV, T)
        o_ref[...] = (_silu(cg) * cu).astype(o_ref.dtype)

    cur = lambda o: pl.BlockSpec((T, tc), lambda j, i: (i, j + o))
    prev = lambda o: pl.BlockSpec((8, tc), lambda j, i: (jnp.maximum(i * r8 - 1, 0), j + o))
    wsp = lambda o: pl.BlockSpec((FFN_CONV, tc), lambda j, i: (0, j + o))
    return pl.pallas_call(
        body, name="ffn_act", grid=(nct, S // T),
        in_specs=[cur(0), prev(0), cur(nct), prev(nct), wsp(0), wsp(nct)], out_specs=cur(0),
        out_shape=jax.ShapeDtypeStruct((S, Cc), _MXU),
        compiler_params=_cparams(("parallel", "parallel")),
    )(up, up, up, up, cw, cw)


def _ffn_act_bwd(d_act, up, cw):
    S, Cc = up.shape[0], up.shape[1] // 2
    T, tc = _pick_tile(S, 256), _pick_tile(Cc, 1536)
    r8 = T // 8
    nt = S // T
    nct = Cc // tc
    K = FFN_CONV

    def body(da_ref, dan_ref, g_ref, gp_ref, gn_ref, u_ref, up_ref, un_ref, wg_ref, wu_ref,
             dg_ref, du_ref, dwg_ref, dwu_ref):
        i = pl.program_id(1)
        keep_p = jnp.where(i == 0, 0.0, 1.0)
        keep_n = jnp.where(i == nt - 1, 0.0, 1.0)
        wg, wu = wg_ref[...], wu_ref[...]
        xg = jnp.concatenate([gp_ref[...] * keep_p, g_ref[...], gn_ref[...] * keep_n], axis=0)
        xu = jnp.concatenate([up_ref[...] * keep_p, u_ref[...], un_ref[...] * keep_n], axis=0)
        cg = _conv_taps(xg, wg, K, T + 8)
        cu = _conv_taps(xu, wu, K, T + 8)
        da = jnp.concatenate([da_ref[...], dan_ref[...] * keep_n], axis=0)
        sg = jax.nn.sigmoid(cg)
        d_cg = da * cu * (sg * (1.0 + cg * (1.0 - sg)))
        d_cu = da * (cg * sg)
        dg_ref[...] = _conv_taps_t(d_cg, wg, K, T).astype(dg_ref.dtype)
        du_ref[...] = _conv_taps_t(d_cu, wu, K, T).astype(du_ref.dtype)

        @pl.when(i == 0)
        def _():
            dwg_ref[...] = jnp.zeros_like(dwg_ref)
            dwu_ref[...] = jnp.zeros_like(dwu_ref)

        for k in range(K):
            lo = 8 - (K - 1) + k
            dwg_ref[k:k + 1, :] += jnp.sum(d_cg[0:T, :] * xg[lo:lo + T, :], axis=0, keepdims=True)
            dwu_ref[k:k + 1, :] += jnp.sum(d_cu[0:T, :] * xu[lo:lo + T, :], axis=0, keepdims=True)

    cur = lambda o: pl.BlockSpec((T, tc), lambda j, i: (i, j + o))
    prev = lambda o: pl.BlockSpec((8, tc), lambda j, i: (jnp.maximum(i * r8 - 1, 0), j + o))
    nxt = lambda o: pl.BlockSpec((8, tc), lambda j, i: (jnp.minimum((i + 1) * r8, S // 8 - 1), j + o))
    wsp = lambda o: pl.BlockSpec((K, tc), lambda j, i: (0, j + o))
    return pl.pallas_call(
        body, name="ffn_act_bwd", grid=(nct, nt),
        in_specs=[cur(0), nxt(0), cur(0), prev(0), nxt(0), cur(nct), prev(nct), nxt(nct), wsp(0), wsp(nct)],
        out_specs=(cur(0), cur(0), wsp(0), wsp(0)),
        out_shape=(jax.ShapeDtypeStruct((S, Cc), _MXU), jax.ShapeDtypeStruct((S, Cc), _MXU),
                   jax.ShapeDtypeStruct((K, Cc), F32), jax.ShapeDtypeStruct((K, Cc), F32)),
        compiler_params=_cparams(("parallel", "arbitrary")),
    )(d_act, d_act, up, up, up, up, up, up, cw, cw)


def _local_step(x, tgt, n1w, n2w, fnw, gp, gnw, wp, conv_w, w_out, w_up4, fcw, w_down):
    h1 = _rmsnorm_fwd(x, n1w, "norm1")
    proj = _mm(h1, wp, "nn", name="proj")
    u_v, w_k, q_dec, k_end, attn, tinv, g_end = _gdn_pre(proj, conv_w, gp)
    o_a, states = _gdn_scan(u_v, w_k, q_dec, k_end, attn, g_end, proj, gnw)
    o_b, lse = _attn_fwd(proj)
    x2 = _mm(o_b, w_out[GDN_WIDTH:], "nn", residual=_mm(o_a, w_out[:GDN_WIDTH], "nn", residual=x, name="outproj_a"),
             name="outproj_b")
    h2 = _rmsnorm_fwd(x2, n2w, "norm2")
    up = _mm(h2, w_up4, "nn", b_blocks=True, name="up")
    act = _ffn_act(up, fcw)
    x3 = _mm(act, w_down, "nn", residual=x2, name="down")
    loss, dx3, d_fnw = _loss_head(x3, fnw, tgt, "loss_head")
    d_act = _mm(dx3, w_down, "nt", name="d_act")
    d_wdown = _mm(act, dx3, "tn", name="d_wdown")
    d_upg, d_upu, d_fcwg, d_fcwu = _ffn_act_bwd(d_act, up, fcw)
    d_wup = _mm(h2, d_upg, "tn", place=("blocks", N_CHIPS, 0), tn=w_up4.shape[2], name="d_wgate")
    d_wup = _mm(h2, d_upu, "tn", place=("blocks", N_CHIPS, N_CHIPS // 2), tn=w_up4.shape[2], into=d_wup, name="d_wup")
    d_h2 = _mm_nt_blocks([d_upg, d_upu], w_up4, "d_h2")
    dx2, d_n2w = _rmsnorm_bwd(d_h2, x2, n2w, dx3, "norm2_bwd")
    d_oa = _mm(dx2, w_out[:GDN_WIDTH], "nt", name="d_oa")
    d_ob = _mm(dx2, w_out[GDN_WIDTH:], "nt", name="d_ob")
    d_wout = _mm(o_a, dx2, "tn", place=("rows", D_MODEL, 0), name="d_wout_a")
    d_wout = _mm(o_b, dx2, "tn", place=("rows", D_MODEL, GDN_WIDTH), into=d_wout, name="d_wout_b")
    dq_b, dk_b, dv_b = _attn_bwd(proj, o_b, lse, d_ob)
    d_uv, d_wk, d_qd, d_ke, d_at, d_ge, d_z, d_gnw = _gdn_scan_bwd(u_v, w_k, q_dec, k_end, attn, g_end, proj, gnw, states, d_oa)
    d_pre, d_ba, d_gp = _gdn_post(proj, conv_w, gp, tinv, u_v, w_k, d_uv, d_wk, d_qd, d_ke, d_at, d_ge)
    d_qkva, d_convw = _conv_bwd(d_pre, proj, 0, conv_w, GDN_CONV, "gdn_conv_bwd", 512)
    d_proj = jnp.concatenate([d_qkva, d_z.astype(_MXU), dq_b, dk_b, dv_b, d_ba.astype(_MXU),
                              jnp.zeros((x.shape[0], P_COLS - P_BA - 128), _MXU)], axis=1)
    d_wp = _mm(h1, d_proj, "tn", name="d_wp")
    d_h1 = _mm(d_proj, wp, "nt", name="d_h1")
    dx, d_n1w = _rmsnorm_bwd(d_h1, x, n1w, dx2, "norm1_bwd")
    grads = dict(wp=d_wp, conv_w=d_convw, w_out=d_wout, w_up=d_wup, fcw_g=d_fcwg, fcw_u=d_fcwu, w_down=d_wdown,
                 n1w=d_n1w, n2w=d_n2w, fnw=d_fnw, gp=d_gp, gnw=d_gnw)
    return loss, dx, grads


_HBM = pl.BlockSpec(memory_space=pltpu.HBM)


def _pos():
    return lax.axis_index("x"), lax.axis_index("y"), lax.axis_index("c")


def _other_chips(x, y):
    return [(1 - x, y), (x, 1 - y), (1 - x, 1 - y)]


def _gather_weights(shards):
    n = len(shards)

    def body(*refs):
        ins, outs = refs[:n], refs[n:2 * n]
        send_sems, recv_sems, loc_sems = refs[2 * n:]
        x, y, c = _pos()
        q = 2 * x + y
        chips = _other_chips(x, y)
        local = [pltpu.make_async_copy(ins[a], outs[a].at[q], loc_sems.at[a]) for a in range(n)]
        for cp in local:
            cp.start()

        def copy(a, j, block):
            px, py = chips[j]
            return pltpu.make_async_remote_copy(
                src_ref=ins[a], dst_ref=outs[a].at[block], send_sem=send_sems.at[3 * a + j],
                recv_sem=recv_sems.at[3 * a + j], device_id=(px, py, c), device_id_type=MESH)

        sends = [copy(a, j, q) for a in range(n) for j in range(3)]
        for cp in sends:
            cp.start()
        for a in range(n):
            for j, (px, py) in enumerate(chips):
                copy(a, j, 2 * px + py).wait_recv()
        for cp in sends:
            cp.wait_send()
        for cp in local:
            cp.wait()

    return pl.pallas_call(
        body, name="gather_weights", in_specs=[_HBM] * n, out_specs=[_HBM] * n,
        out_shape=[jax.ShapeDtypeStruct((N_CHIPS,) + s.shape, s.dtype) for s in shards],
        scratch_shapes=[pltpu.SemaphoreType.DMA((3 * n,)), pltpu.SemaphoreType.DMA((3 * n,)),
                        pltpu.SemaphoreType.DMA((n,))],
    )(*shards)


def _half_rows(ref, c, rh):
    return ref.at[:, pl.ds(pl.multiple_of(c * rh, 8), rh), :]


def _grad_sibling(fams, small):
    n = len(fams)
    rhs = [f.shape[1] // 2 for f in fams]

    def body(*refs):
        ins, small_ref = refs[:n], refs[n]
        outs, all_ref = refs[n + 1:2 * n + 1], refs[2 * n + 1]
        send_sems, recv_sems, loc_sem = refs[2 * n + 2:]
        x, y, c = _pos()
        me = 4 * x + 2 * y + c
        mine = pltpu.make_async_copy(small_ref, all_ref.at[me], loc_sem)
        mine.start()
        bigs = [pltpu.make_async_remote_copy(src_ref=_half_rows(ins[a], 1 - c, rhs[a]), dst_ref=outs[a],
                                             send_sem=send_sems.at[7 + a], recv_sem=recv_sems.at[7 + a],
                                             device_id=(x, y, 1 - c), device_id_type=MESH) for a in range(n)]
        for cp in bigs:
            cp.start()

        def peer(r):
            dx, dy, dc = (r >> 2) & 1, (r >> 1) & 1, r & 1
            px = x if dx == 0 else 1 - x
            py = y if dy == 0 else 1 - y
            pc = c if dc == 0 else 1 - c
            return px, py, pc

        def small_copy(r, slot):
            return pltpu.make_async_remote_copy(src_ref=small_ref, dst_ref=all_ref.at[slot], send_sem=send_sems.at[r - 1],
                                                recv_sem=recv_sems.at[r - 1], device_id=peer(r), device_id_type=MESH)

        sends = [small_copy(r, me) for r in range(1, 8)]
        for cp in sends:
            cp.start()
        for r in range(1, 8):
            px, py, pc = peer(r)
            small_copy(r, 4 * px + 2 * py + pc).wait_recv()
        for cp in bigs:
            cp.wait_recv()
        for cp in bigs + sends:
            cp.wait_send()
        mine.wait()

    return pl.pallas_call(
        body, name="grad_sibling", in_specs=[_HBM] * (n + 1), out_specs=[_HBM] * (n + 1),
        out_shape=[jax.ShapeDtypeStruct((f.shape[0], f.shape[1] // 2, f.shape[2]), f.dtype) for f in fams]
        + [jax.ShapeDtypeStruct((8,) + small.shape, small.dtype)],
        scratch_shapes=[pltpu.SemaphoreType.DMA((7 + n,)), pltpu.SemaphoreType.DMA((7 + n,)), pltpu.SemaphoreType.DMA],
    )(*fams, small)


def _grad_chips(parts):
    n = len(parts)

    def body(*refs):
        ins, outs = refs[:n], refs[n:2 * n]
        send_sems, recv_sems = refs[2 * n:]
        x, y, c = _pos()
        chips = _other_chips(x, y)

        def copy(a, j):
            px, py = chips[j]
            return pltpu.make_async_remote_copy(src_ref=ins[a].at[2 * px + py], dst_ref=outs[a].at[j],
                                                send_sem=send_sems.at[3 * a + j], recv_sem=recv_sems.at[3 * a + j],
                                                device_id=(px, py, c), device_id_type=MESH)

        sends = [copy(a, j) for a in range(n) for j in range(3)]
        for cp in sends:
            cp.start()
        for a in range(n):
            for j in range(3):
                copy(a, j).wait_recv()
        for cp in sends:
            cp.wait_send()

    return pl.pallas_call(
        body, name="grad_chips", in_specs=[_HBM] * n, out_specs=[_HBM] * n,
        out_shape=[jax.ShapeDtypeStruct((3,) + p.shape[1:], p.dtype) for p in parts],
        scratch_shapes=[pltpu.SemaphoreType.DMA((3 * n,)), pltpu.SemaphoreType.DMA((3 * n,))],
    )(*parts)


def _grad_share(fulls):
    n = len(fulls)
    rhs = [f.shape[0] // 2 for f in fulls]

    def body(*refs):
        ins, outs = refs[:n], refs[n:2 * n]
        send_sems, recv_sems = refs[2 * n:]
        x, y, c = _pos()

        def copy(a, half):
            rows = pl.ds(pl.multiple_of(half * rhs[a], 8), rhs[a])
            return pltpu.make_async_remote_copy(src_ref=ins[a].at[rows, :], dst_ref=outs[a].at[rows, :],
                                                send_sem=send_sems.at[a], recv_sem=recv_sems.at[a],
                                                device_id=(x, y, 1 - c), device_id_type=MESH)

        sends = [copy(a, c) for a in range(n)]
        for cp in sends:
            cp.start()
        for a in range(n):
            copy(a, 1 - c).wait_recv()
        for cp in sends:
            cp.wait_send()

    return pl.pallas_call(
        body, name="grad_share", in_specs=[_HBM] * n, out_specs=[_HBM] * n,
        out_shape=[jax.ShapeDtypeStruct(f.shape, f.dtype) for f in fulls],
        input_output_aliases={a: a for a in range(n)},
        scratch_shapes=[pltpu.SemaphoreType.DMA((n,)), pltpu.SemaphoreType.DMA((n,))],
    )(*fulls)


def _add_sibling(own, recv, cq, name):
    nb, R, Cc = own.shape
    Rh = R // 2

    def body(cq_ref, a_ref, b_ref, o32_ref, o16_ref):
        s = a_ref[...] + b_ref[...]
        o32_ref[...] = s
        o16_ref[...] = s.astype(o16_ref.dtype)

    sp = pl.BlockSpec((1, Rh, Cc), lambda b, s: (b, 0, 0))
    gs = pltpu.PrefetchScalarGridSpec(
        num_scalar_prefetch=1, grid=(nb,),
        in_specs=[pl.BlockSpec((1, Rh, Cc), lambda b, s: (b, s[0], 0)), sp], out_specs=[sp, sp])
    return pl.pallas_call(
        body, name=name, grid_spec=gs,
        out_shape=[jax.ShapeDtypeStruct((nb, Rh, Cc), F32), jax.ShapeDtypeStruct((nb, Rh, Cc), _MXU)],
        compiler_params=_cparams(("parallel",)),
    )(cq, own, recv)


def _add_chips(part32, recv3, cq, name):
    nb, Rh, Cc = part32.shape

    def body(cq_ref, a_ref, b_ref, o_ref):
        acc = a_ref[0]
        for j in range(3):
            acc = acc + b_ref[j].astype(F32)
        o_ref[...] = acc

    gs = pltpu.PrefetchScalarGridSpec(
        num_scalar_prefetch=1, grid=(1,),
        in_specs=[pl.BlockSpec((1, Rh, Cc), lambda i, s: (s[1], 0, 0)), pl.BlockSpec((3, Rh, Cc), lambda i, s: (0, 0, 0))],
        out_specs=pl.BlockSpec((Rh, Cc), lambda i, s: (s[0], 0)))
    return pl.pallas_call(
        body, name=name, grid_spec=gs, out_shape=jax.ShapeDtypeStruct((2 * Rh, Cc), F32),
        compiler_params=_cparams(("arbitrary",)),
    )(cq, part32, recv3)


def _sum_devices(small_all):
    def body(s_ref, o_ref):
        tot = s_ref[0]
        for d in range(1, 8):
            tot = tot + s_ref[d]
        o_ref[...] = tot

    return pl.pallas_call(body, name="sum_devices", out_shape=jax.ShapeDtypeStruct(small_all.shape[1:], F32))(small_all)


def _adamw(w, g, m, v, name):
    R, Cc = w.shape
    T = max([t for t in range(8, 257, 8) if R % t == 0], default=R)
    c1 = 1.0 / (1.0 - ADAM_B1 ** ADAM_STEP)
    c2 = 1.0 / (1.0 - ADAM_B2 ** ADAM_STEP)

    def body(w_ref, g_ref, m_ref, v_ref, d_ref, mo_ref, vo_ref):
        gv = g_ref[...]
        mn = ADAM_B1 * m_ref[...] + (1.0 - ADAM_B1) * gv
        vn = ADAM_B2 * v_ref[...] + (1.0 - ADAM_B2) * (gv * gv)
        mo_ref[...] = mn
        vo_ref[...] = vn
        d_ref[...] = -ADAM_LR * ((mn * c1) / (jnp.sqrt(vn * c2) + ADAM_EPS) + ADAM_WD * w_ref[...])

    sp = pl.BlockSpec((T, Cc), lambda i: (i, 0))
    sh = jax.ShapeDtypeStruct((R, Cc), F32)
    return pl.pallas_call(
        body, name=name, grid=(R // T,), in_specs=[sp] * 4, out_specs=(sp, sp, sp), out_shape=(sh, sh, sh),
        compiler_params=_cparams(("parallel",)),
    )(w, g, m, v)


SMALL_ROWS = 32
REPL_ROWS = 8


def _pad_lanes(v, n=D_MODEL):
    return jnp.pad(v, ((0, 0), (0, n - v.shape[1])))


def kernel(x, norm1_w, w_in, conv_qkv_w, a_log, dt_bias, gdn_norm_w, w_out, norm2_w, w_up, ffn_conv_w, w_down, final_norm_w, loss_target, m_norm1_w, m_w_in, m_conv_qkv_w, m_a_log, m_dt_bias, m_gdn_norm_w, m_w_out, m_norm2_w, m_w_up, m_ffn_conv_w, m_w_down, m_final_norm_w, v_norm1_w, v_w_in, v_conv_qkv_w, v_a_log, v_dt_bias, v_gdn_norm_w, v_w_out, v_norm2_w, v_w_up, v_ffn_conv_w, v_w_down, v_final_norm_w):
    c = lax.axis_index("c")
    q = 2 * lax.axis_index("x") + lax.axis_index("y")
    S = x.shape[1]
    g_in, g_out, g_up, g_down, g_conv, g_fconv = _gather_weights(
        [w_in[0].astype(_MXU), w_out[0].astype(_MXU), w_up[0].astype(_MXU), w_down[0].astype(_MXU),
         conv_qkv_w[0], ffn_conv_w[0]])
    wp = _wp_assemble(g_in)
    w_out_f = g_out.reshape(D_MODEL, D_MODEL)
    w_down_f = g_down.reshape(D_FF, D_MODEL)
    conv_f = jnp.concatenate([g_conv[i] for i in range(N_CHIPS)], axis=1)
    fcw = jnp.concatenate([g_fconv[i] for i in range(N_CHIPS)], axis=1)
    gp = _pad_lanes(jnp.concatenate([a_log, dt_bias], axis=1), 128)
    fnw = final_norm_w[None, :]
    loss_l, dx, g = _local_step(x[0], loss_target[0], norm1_w, norm2_w, fnw, gp, gdn_norm_w, wp, conv_f, w_out_f,
                                g_up, fcw, w_down_f)
    loss = lax.psum(loss_l[0, 0], ("x", "y", "c"))
    fams = [_win_split(g["wp"]), g["w_up"], g["w_out"].reshape(N_CHIPS, D_MODEL // N_CHIPS, D_MODEL),
            g["w_down"].reshape(N_CHIPS, D_FF // N_CHIPS, D_MODEL)]
    n_fc = FFN_CONV * D_FF

    def rows_of(v):
        flat = v.reshape(-1)
        return jnp.pad(flat, (0, -flat.shape[0] % D_MODEL)).reshape(-1, D_MODEL)

    small = jnp.concatenate([g["n1w"], g["n2w"], g["fnw"], _pad_lanes(g["gp"][0:1]), _pad_lanes(g["gnw"]),
                             rows_of(g["conv_w"]), rows_of(g["fcw_g"]), rows_of(g["fcw_u"])], axis=0)
    small = jnp.pad(small, ((0, SMALL_ROWS - small.shape[0]), (0, 0)))
    cq = jnp.stack([c, q]).astype(jnp.int32)
    *got, small_all = _grad_sibling(fams, small)
    fam_names = ["w_in", "w_up", "w_out", "w_down"]
    parts = [_add_sibling(f, r, cq, "add_sibling_" + nm) for f, r, nm in zip(fams, got, fam_names)]
    got3 = _grad_chips([p[1] for p in parts])
    halves = [_add_chips(p[0], r3, cq, "add_chips_" + nm) for p, r3, nm in zip(parts, got3, fam_names)]
    g_w_in, g_w_up, g_w_out, g_w_down = _grad_share(halves)
    small_red = _sum_devices(small_all)
    r0 = 5
    r1 = r0 + GDN_CONV * 3 * GDN_WIDTH // D_MODEL
    r2 = r1 + -(-n_fc // D_MODEL)
    conv_red = small_red[r0:r1].reshape(GDN_CONV, 3 * GDN_WIDTH)
    fc_red = jnp.concatenate([small_red[r1:r2].reshape(-1)[:n_fc].reshape(FFN_CONV, D_FF),
                              small_red[r2:2 * r2 - r1].reshape(-1)[:n_fc].reshape(FFN_CONV, D_FF)], axis=1)
    g_conv_w = lax.dynamic_slice_in_dim(conv_red, q * (3 * GDN_WIDTH // N_CHIPS), 3 * GDN_WIDTH // N_CHIPS, axis=1)
    g_fconv_w = lax.dynamic_slice_in_dim(fc_red, q * (2 * D_FF // N_CHIPS), 2 * D_FF // N_CHIPS, axis=1)
    g_n1w, g_n2w, g_fnw = small_red[0:1], small_red[1:2], small_red[2]
    g_alog, g_dtb, g_gnw = small_red[3:4, 0:4], small_red[3:4, 4:8], small_red[4:5, 0:128]
    big = {}
    for nm, w, gg, m, v in (("w_in", w_in, g_w_in, m_w_in, v_w_in), ("conv_qkv_w", conv_qkv_w, g_conv_w, m_conv_qkv_w, v_conv_qkv_w),
                            ("w_out", w_out, g_w_out, m_w_out, v_w_out), ("w_up", w_up, g_w_up, m_w_up, v_w_up),
                            ("ffn_conv_w", ffn_conv_w, g_fconv_w, m_ffn_conv_w, v_ffn_conv_w),
                            ("w_down", w_down, g_w_down, m_w_down, v_w_down)):
        d_, m_, v_ = _adamw(w[0], gg, m[0], v[0], "adamw_" + nm)
        big[nm] = (gg[None], d_[None], m_[None], v_[None])

    def pack_small(n1, n2, fn, al, db, gn):
        return jnp.concatenate([n1, n2, fn[None, :], _pad_lanes(jnp.concatenate([al, db], axis=1)), _pad_lanes(gn),
                                jnp.zeros((REPL_ROWS - 5, D_MODEL), F32)], axis=0)

    sw = pack_small(norm1_w, norm2_w, final_norm_w, a_log, dt_bias, gdn_norm_w)
    sm = pack_small(m_norm1_w, m_norm2_w, m_final_norm_w, m_a_log, m_dt_bias, m_gdn_norm_w)
    sv = pack_small(v_norm1_w, v_norm2_w, v_final_norm_w, v_a_log, v_dt_bias, v_gdn_norm_w)
    sd, smn, svn = _adamw(sw, small_red[:REPL_ROWS], sm, sv, "adamw_small")

    def unpack_small(t):
        return dict(norm1_w=t[0:1], norm2_w=t[1:2], final_norm_w=t[2], a_log=t[3:4, 0:4], dt_bias=t[3:4, 4:8],
                    gdn_norm_w=t[4:5, 0:128])

    sg = dict(norm1_w=g_n1w, norm2_w=g_n2w, final_norm_w=g_fnw, a_log=g_alog, dt_bias=g_dtb, gdn_norm_w=g_gnw)
    sd, smn, svn = unpack_small(sd), unpack_small(smn), unpack_small(svn)
    names = ["norm1_w", "w_in", "conv_qkv_w", "a_log", "dt_bias", "gdn_norm_w", "w_out", "norm2_w", "w_up",
             "ffn_conv_w", "w_down", "final_norm_w"]
    grads = [big[n][0] if n in big else sg[n] for n in names]
    deltas = [big[n][1] if n in big else sd[n] for n in names]
    new_m = [big[n][2] if n in big else smn[n] for n in names]
    new_v = [big[n][3] if n in big else svn[n] for n in names]
    return (loss, dx[None], *grads, *deltas, *new_m, *new_v)
```

```python
import functools
import math

import numpy as np
import jax
import jax.numpy as jnp
from jax import lax
from jax.experimental import pallas as pl
from jax.experimental.pallas import tpu as pltpu

F32 = jnp.float32
BF16 = jnp.bfloat16
_MXU = jnp.bfloat16
_HI = lax.Precision.HIGHEST
EPS = 1e-6
V7X_VMEM_LIMIT = 56 * 1024 * 1024
MESH = pl.DeviceIdType.MESH

D_MODEL = 1024
GDN_HEADS, GDN_DIM, GDN_CHUNK, GDN_CONV = 4, 128, 64, 4
GDN_WIDTH = GDN_HEADS * GDN_DIM
DIL_HEADS, DIL_DIM = 8, 64
DIL_WIDTH = DIL_HEADS * DIL_DIM
D_FF, FFN_CONV = 2816, 3
IN_COLS = 3592
P_COLS = 3840
P_Z, P_QKVB, P_BA = 1536, 2048, 3584
ATT_T = 256
ADAM_LR, ADAM_B1, ADAM_B2, ADAM_EPS, ADAM_WD, ADAM_STEP = 0.001, 0.9, 0.999, 1e-08, 0.01, 10
N_CHIPS = 4


def _cparams(sem=None, vmem=None):
    kw = {}
    if sem is not None:
        kw["dimension_semantics"] = sem
    if vmem is not None:
        kw["vmem_limit_bytes"] = vmem
    return pltpu.CompilerParams(**kw)


def _silu(x):
    return x * jax.nn.sigmoid(x)


def _pick_tile(n, cap):
    best = None
    for t in range(128, min(n, cap) + 1, 128):
        if n % t == 0:
            best = t
    return best or n


def _mm(a, b, mode, *, out_dtype=F32, residual=None, name, b_blocks=False, place=None, into=None, tn=None):
    if mode == "nn":
        M, K = a.shape
        N = b.shape[0] * b.shape[2] if b_blocks else b.shape[1]
    elif mode == "nt":
        (M, K), (N, _) = a.shape, b.shape
    else:
        (K, M), (_, N) = a.shape, b.shape
    tm = _pick_tile(M, 1024)
    tn = b.shape[2] if b_blocks else (tn or _pick_tile(N, 1536))

    def vmem(tm, tn):
        return 2 * (tm * K * a.dtype.itemsize + tn * K * b.dtype.itemsize
                    + tm * tn * (jnp.dtype(out_dtype).itemsize + (4 if residual is not None else 0))) + 3 * tm * tn * 4

    fixed_tn = b_blocks or (place is not None and place[0] == "blocks")
    while vmem(tm, tn) > 40 * 1024 * 1024:
        if (tm >= tn or fixed_tn) and tm % 256 == 0:
            tm //= 2
        elif tn % 256 == 0 and not fixed_tn:
            tn //= 2
        else:
            tm //= 2
    a_spec = pl.BlockSpec((K, tm), lambda j, i: (0, i)) if mode == "tn" else pl.BlockSpec((tm, K), lambda j, i: (i, 0))
    if b_blocks:
        b_spec = pl.BlockSpec((None, K, tn), lambda j, i: (j, 0, 0))
    else:
        b_spec = pl.BlockSpec((tn, K), lambda j, i: (j, 0)) if mode == "nt" else pl.BlockSpec((K, tn), lambda j, i: (0, j))
    r_spec = pl.BlockSpec((tm, tn), lambda j, i: (i, j))
    if place is None:
        o_spec, o_shape = r_spec, (M, N)
    elif place[0] == "rows":
        off = place[2] // tm
        o_spec, o_shape = pl.BlockSpec((tm, tn), lambda j, i: (i + off, j)), (place[1], N)
    else:
        off = place[2]
        o_spec, o_shape = pl.BlockSpec((None, tm, tn), lambda j, i: (j + off, i, 0)), (place[1], M, tn)
    dims = {"nn": (((1,), (0,)), ((), ())), "nt": (((1,), (1,)), ((), ())), "tn": (((0,), (0,)), ((), ()))}[mode]

    def body(*refs):
        a_ref, b_ref = refs[0], refs[1]
        o_ref = refs[-1]
        acc = lax.dot_general(a_ref[...].astype(_MXU), b_ref[...].astype(_MXU), dims, preferred_element_type=F32)
        if residual is not None:
            acc = acc + refs[2][...]
        o_ref[...] = acc.astype(out_dtype)

    ins, specs, alias = [a, b], [a_spec, b_spec], {}
    if residual is not None:
        ins.append(residual)
        specs.append(r_spec)
    if into is not None:
        alias = {len(ins): 0}
        ins.append(into)
        specs.append(pl.BlockSpec(memory_space=pl.ANY))
    return pl.pallas_call(
        body, name=name, grid=(N // tn, M // tm), in_specs=specs, out_specs=o_spec,
        out_shape=jax.ShapeDtypeStruct(o_shape, out_dtype), input_output_aliases=alias,
        compiler_params=_cparams(("parallel", "parallel"), V7X_VMEM_LIMIT),
    )(*ins)


def _mm_nt_blocks(a_list, b4, name):
    M = a_list[0].shape[0]
    nb, N, Kb = b4.shape
    tm, tn = _pick_tile(M, 512), _pick_tile(N, 512)

    def body(a0_ref, a1_ref, b_ref, o_ref):
        acc = None
        for blk in range(nb):
            a_ref = (a0_ref, a1_ref)[blk // 2]
            lo = (blk % 2) * Kb
            t = lax.dot_general(a_ref[:, lo:lo + Kb].astype(_MXU), b_ref[blk].astype(_MXU), (((1,), (1,)), ((), ())),
                                preferred_element_type=F32)
            acc = t if acc is None else acc + t
        o_ref[...] = acc

    a_spec = pl.BlockSpec((tm, 2 * Kb), lambda j, i: (i, 0))
    return pl.pallas_call(
        body, name=name, grid=(N // tn, M // tm),
        in_specs=[a_spec, a_spec, pl.BlockSpec((nb, tn, Kb), lambda j, i: (0, j, 0))],
        out_specs=pl.BlockSpec((tm, tn), lambda j, i: (i, j)), out_shape=jax.ShapeDtypeStruct((M, N), F32),
        compiler_params=_cparams(("parallel", "parallel"), V7X_VMEM_LIMIT),
    )(a_list[0], a_list[1], b4)


def _wp_assemble(g_in):
    nb, Dm, Wb = g_in.shape
    T = 256
    n_lo = P_QKVB - 2 * Wb

    def body(g_ref, o_ref):
        g2 = g_ref[2]
        o_ref[...] = jnp.concatenate(
            [g_ref[0], g_ref[1], g2[:, :n_lo], g2[:, n_lo + 8:], g_ref[3], g2[:, n_lo:n_lo + 8],
             jnp.zeros((T, P_COLS - P_BA - 8), g_in.dtype)], axis=1)

    return pl.pallas_call(
        body, name="wp_assemble", grid=(Dm // T,), in_specs=[pl.BlockSpec((nb, T, Wb), lambda i: (0, i, 0))],
        out_specs=pl.BlockSpec((T, P_COLS), lambda i: (i, 0)), out_shape=jax.ShapeDtypeStruct((Dm, P_COLS), g_in.dtype),
        compiler_params=_cparams(("parallel",)),
    )(g_in)


def _win_split(d_wp):
    Dm = d_wp.shape[0]
    Wb = IN_COLS // N_CHIPS
    T = 256

    def body(x_ref, o_ref):
        xv = x_ref[...]
        o_ref[0] = xv[:, 0:Wb]
        o_ref[1] = xv[:, Wb:2 * Wb]
        o_ref[2] = jnp.concatenate([xv[:, 2 * Wb:P_QKVB], xv[:, P_BA:P_BA + 8], xv[:, P_QKVB:3 * Wb - 8]], axis=1)
        o_ref[3] = xv[:, 3 * Wb - 8:P_BA]

    return pl.pallas_call(
        body, name="win_split", grid=(Dm // T,), in_specs=[pl.BlockSpec((T, P_COLS), lambda i: (i, 0))],
        out_specs=pl.BlockSpec((N_CHIPS, T, Wb), lambda i: (0, i, 0)),
        out_shape=jax.ShapeDtypeStruct((N_CHIPS, Dm, Wb), F32), compiler_params=_cparams(("parallel",)),
    )(d_wp)


def _rmsnorm_fwd(x, w, name):
    S, D = x.shape
    T = _pick_tile(S, 512)

    def body(x_ref, w_ref, o_ref):
        xv = x_ref[...]
        rs = lax.rsqrt(jnp.mean(xv * xv, axis=-1, keepdims=True) + EPS)
        o_ref[...] = (xv * rs * w_ref[...]).astype(o_ref.dtype)

    return pl.pallas_call(
        body, name=name, grid=(S // T,),
        in_specs=[pl.BlockSpec((T, D), lambda i: (i, 0)), pl.BlockSpec((1, D), lambda i: (0, 0))],
        out_specs=pl.BlockSpec((T, D), lambda i: (i, 0)),
        out_shape=jax.ShapeDtypeStruct((S, D), _MXU),
        compiler_params=_cparams(("parallel",)),
    )(x, w)


def _rmsnorm_bwd(dh, x, w, dres, name):
    S, D = x.shape
    T = _pick_tile(S, 512)

    def body(dh_ref, x_ref, w_ref, dres_ref, dx_ref, dw_ref):
        xv = x_ref[...]
        rs = lax.rsqrt(jnp.mean(xv * xv, axis=-1, keepdims=True) + EPS)
        xn = xv * rs
        dhv = dh_ref[...]
        dxn = dhv * w_ref[...]
        dx_ref[...] = dres_ref[...] + rs * (dxn - xn * jnp.mean(dxn * xn, axis=-1, keepdims=True))

        @pl.when(pl.program_id(0) == 0)
        def _():
            dw_ref[...] = jnp.zeros_like(dw_ref)

        dw_ref[...] += jnp.sum(dhv * xn, axis=0, keepdims=True)

    row = pl.BlockSpec((T, D), lambda i: (i, 0))
    vec = pl.BlockSpec((1, D), lambda i: (0, 0))
    return pl.pallas_call(
        body, name=name, grid=(S // T,), in_specs=[row, row, vec, row], out_specs=(row, vec),
        out_shape=(jax.ShapeDtypeStruct((S, D), F32), jax.ShapeDtypeStruct((1, D), F32)),
        compiler_params=_cparams(("arbitrary",)),
    )(dh, x, w, dres)


def _loss_head(x3, w, tgt, name):
    S, D = x3.shape
    T = _pick_tile(S, 512)

    def body(x_ref, w_ref, t_ref, loss_ref, dx_ref, dw_ref):
        xv = x_ref[...]
        rs = lax.rsqrt(jnp.mean(xv * xv, axis=-1, keepdims=True) + EPS)
        xn = xv * rs
        err = xn * w_ref[...] - t_ref[...]
        dy = err * (1.0 / D)
        dxn = dy * w_ref[...]
        dx_ref[...] = rs * (dxn - xn * jnp.mean(dxn * xn, axis=-1, keepdims=True))

        @pl.when(pl.program_id(0) == 0)
        def _():
            dw_ref[...] = jnp.zeros_like(dw_ref)
            loss_ref[...] = jnp.zeros_like(loss_ref)

        dw_ref[...] += jnp.sum(dy * xn, axis=0, keepdims=True)
        part = jnp.sum(jnp.sum(err * err, axis=-1, keepdims=True), axis=0, keepdims=True) * (0.5 / D)
        loss_ref[...] += jnp.broadcast_to(part, loss_ref.shape)

    row = pl.BlockSpec((T, D), lambda i: (i, 0))
    vec = pl.BlockSpec((1, D), lambda i: (0, 0))
    return pl.pallas_call(
        body, name=name, grid=(S // T,), in_specs=[row, vec, row],
        out_specs=(pl.BlockSpec((8, 128), lambda i: (0, 0)), row, vec),
        out_shape=(jax.ShapeDtypeStruct((8, 128), F32), jax.ShapeDtypeStruct((S, D), F32), jax.ShapeDtypeStruct((1, D), F32)),
        compiler_params=_cparams(("arbitrary",)),
    )(x3, w, tgt)


def _conv_taps(ext, w, K, T):
    out = None
    for i in range(K):
        lo = 8 - (K - 1) + i
        term = ext[lo:lo + T, :] * w[i:i + 1, :]
        out = term if out is None else out + term
    return out


def _conv_taps_t(ext, w, K, T):
    out = None
    for i in range(K):
        lo = (K - 1) - i
        term = ext[lo:lo + T, :] * w[i:i + 1, :]
        out = term if out is None else out + term
    return out


def _tri_masks(C):
    r = lax.broadcasted_iota(jnp.int32, (C, C), 0)
    c = lax.broadcasted_iota(jnp.int32, (C, C), 1)
    return r == c, r >= c, r > c, r <= c


_NN, _NT, _TN = ((1,), (0,)), ((1,), (1,)), ((0,), (0,))
_GDN_PASSES = dict(qk=1, inv=1, sol=1, scan=1, bwd=1)


def _bdot_raw(a, b, kind, passes):
    dims = ({"NN": ((2,), (1,)), "NT": ((2,), (2,)), "TN": ((1,), (1,))}[kind], ((0,), (0,)))
    if passes == 0:
        return lax.dot_general(a, b, dims, precision=_HI, preferred_element_type=F32)
    ah, bh = a.astype(BF16), b.astype(BF16)
    out = lax.dot_general(ah, bh, dims, preferred_element_type=F32)
    if passes == 3:
        al, bl = (a - ah.astype(F32)).astype(BF16), (b - bh.astype(F32)).astype(BF16)
        out = out + lax.dot_general(ah, bl, dims, preferred_element_type=F32) + lax.dot_general(al, bh, dims, preferred_element_type=F32)
    return out


@functools.partial(jax.custom_vjp, nondiff_argnums=(2, 3))
def _bdot(a, b, kind, passes):
    return _bdot_raw(a, b, kind, passes)


def _bdot_fwd(a, b, kind, passes):
    return _bdot_raw(a, b, kind, passes), (a, b)


def _bdot_bwd(kind, passes, res, ct):
    a, b = res
    if kind == "NN":
        return _bdot_raw(ct, b, "NT", passes), _bdot_raw(a, ct, "TN", passes)
    if kind == "NT":
        return _bdot_raw(ct, b, "NN", passes), _bdot_raw(ct, a, "TN", passes)
    return _bdot_raw(b, ct, "NT", passes), _bdot_raw(a, ct, "NN", passes)


_bdot.defvjp(_bdot_fwd, _bdot_bwd)


def _softplus(x):
    return jnp.maximum(x, 0.0) + jnp.log(1.0 + jnp.exp(-jnp.abs(x)))


def _gdn_stage1(cq, ck, cv, b_col, a_col, alog, dtb, dot=_bdot_raw):
    C = cq.shape[1]
    eye, incl, strict, incl_t = _tri_masks(C)
    qn = cq * lax.rsqrt(jnp.sum(cq * cq, axis=-1, keepdims=True) + EPS) * (GDN_DIM ** -0.5)
    kn = ck * lax.rsqrt(jnp.sum(ck * ck, axis=-1, keepdims=True) + EPS)
    beta = jax.nn.sigmoid(b_col)
    g = -jnp.exp(alog) * _softplus(a_col + dtb)
    g_row = jnp.sum(jnp.where(eye, g, 0.0), axis=1, keepdims=True)
    beta_row = jnp.sum(jnp.where(eye, beta, 0.0), axis=1, keepdims=True)
    gc_col = jnp.sum(jnp.where(incl, g_row, 0.0), axis=2, keepdims=True)
    gc_row = jnp.sum(jnp.where(incl_t, g, 0.0), axis=1, keepdims=True)
    dec = jnp.where(incl, jnp.exp(jnp.where(incl, gc_col - gc_row, 0.0)), 0.0)
    kk = dot(kn, kn, "NT", _GDN_PASSES["qk"])
    qk = dot(qn, kn, "NT", _GDN_PASSES["qk"])
    lmat = jnp.where(strict, dec * kk * beta_row, 0.0)
    attn = dec * qk * beta_row
    gam = jnp.exp(gc_col)
    gc_last = gc_col[:, C - 1:C, :]
    k_end = kn * (jnp.exp(gc_last - gc_col) * beta)
    return lmat, cv, gam * kn, gam * qn, attn, k_end, jnp.exp(gc_last)


def _tri_inv(lmat):
    C = lmat.shape[1]
    eye = _tri_masks(C)[0]
    ps = _GDN_PASSES["inv"]
    p = jnp.where(eye, 1.0, 0.0) - lmat
    lp = _bdot_raw(lmat, lmat, "NN", ps)
    n = int(math.log2(C))
    for s in range(1, n):
        p = p + _bdot_raw(p, lp, "NN", ps)
        if s < n - 1:
            lp = _bdot_raw(lp, lp, "NN", ps)
    return p


def _gated_norm(o, z, gnw):
    on = o * lax.rsqrt(jnp.mean(o * o, axis=-1, keepdims=True) + EPS) * gnw
    return on * _silu(z)


GDN_PG = 2
GDN_SG = 4


def _gdn_pairs(c, ba, gp, G):
    C, W, H = GDN_CHUNK, GDN_WIDTH, GDN_HEADS
    pairs = [(j, h) for j in range(G) for h in range(H)]
    cq, ck, cv = (jnp.stack([c[C * j:C * (j + 1), o + GDN_DIM * h:o + GDN_DIM * (h + 1)] for j, h in pairs]) for o in (0, W, 2 * W))
    b_col = jnp.stack([ba[C * j:C * (j + 1), h:h + 1] for j, h in pairs])
    a_col = jnp.stack([ba[C * j:C * (j + 1), H + h:H + h + 1] for j, h in pairs])
    alog = jnp.stack([gp[0:1, h:h + 1] for j, h in pairs])
    dtb = jnp.stack([gp[0:1, H + h:H + h + 1] for j, h in pairs])
    return pairs, (cq, ck, cv, b_col, a_col, alog, dtb)


def _gdn_pre_specs(S, G):
    C = GDN_CHUNK
    T = C * G
    return dict(
        cur=pl.BlockSpec((T, 3 * GDN_WIDTH), lambda i: (i, 0)),
        prev=pl.BlockSpec((8, 3 * GDN_WIDTH), lambda i: (jnp.maximum(i * (T // 8) - 1, 0), 0)),
        ba=pl.BlockSpec((T, 128), lambda i: (i, P_BA // 128)),
        cw=pl.BlockSpec((GDN_CONV, 3 * GDN_WIDTH), lambda i: (0, 0)),
        vec=pl.BlockSpec((1, 128), lambda i: (0, 0)),
        hd=pl.BlockSpec((GDN_HEADS, T, GDN_DIM), lambda i: (0, i, 0)),
        hc=pl.BlockSpec((GDN_HEADS, T, C), lambda i: (0, i, 0)),
        ge=pl.BlockSpec((G, GDN_HEADS, 8, 128), lambda i: (i, 0, 0, 0)),
    )


def _hd_shape(S, last=GDN_DIM):
    return jax.ShapeDtypeStruct((GDN_HEADS, S, last), F32)


def _gdn_pre(proj, conv_w, gp):
    S = proj.shape[0]
    C, G = GDN_CHUNK, GDN_PG
    nc = S // C
    sp = _gdn_pre_specs(S, G)

    def body(cur_ref, prev_ref, ba_ref, cw_ref, gp_ref, uv_ref, wk_ref, qd_ref, ke_ref, at_ref, ti_ref, ge_ref):
        prev = prev_ref[...] * jnp.where(pl.program_id(0) == 0, 0.0, 1.0)
        c = _silu(_conv_taps(jnp.concatenate([prev, cur_ref[...]], axis=0), cw_ref[...], GDN_CONV, C * G))
        pairs, args = _gdn_pairs(c, ba_ref[...], gp_ref[...], G)
        lmat, v, rk, q_dec, attn, k_end, g_end = _gdn_stage1(*args)
        t = _tri_inv(lmat)
        u_v = _bdot_raw(t, v, "NN", _GDN_PASSES["sol"])
        w_k = _bdot_raw(t, rk, "NN", _GDN_PASSES["sol"])
        for b, (j, h) in enumerate(pairs):
            rows = slice(C * j, C * (j + 1))
            uv_ref[h, rows, :] = u_v[b]
            wk_ref[h, rows, :] = w_k[b]
            qd_ref[h, rows, :] = q_dec[b]
            ke_ref[h, rows, :] = k_end[b]
            at_ref[h, rows, :] = attn[b]
            ti_ref[h, rows, :] = t[b]
            ge_ref[j, h] = jnp.broadcast_to(g_end[b], (8, 128))

    return pl.pallas_call(
        body, name="gdn_pre", grid=(nc // G,),
        in_specs=[sp["cur"], sp["prev"], sp["ba"], sp["cw"], sp["vec"]],
        out_specs=(sp["hd"], sp["hd"], sp["hd"], sp["hd"], sp["hc"], sp["hc"], sp["ge"]),
        out_shape=(_hd_shape(S), _hd_shape(S), _hd_shape(S), _hd_shape(S), _hd_shape(S, C), _hd_shape(S, C),
                   jax.ShapeDtypeStruct((nc, GDN_HEADS, 8, 128), F32)),
        compiler_params=_cparams(("parallel",)),
    )(proj, proj, proj, conv_w, gp)


def _gdn_scan_specs(S, G, rev):
    C = GDN_CHUNK
    T = C * G
    n = S // T
    ci = (lambda i: n - 1 - i) if rev else (lambda i: i)
    return dict(
        hd=pl.BlockSpec((GDN_HEADS, T, GDN_DIM), lambda i: (0, ci(i), 0)),
        hc=pl.BlockSpec((GDN_HEADS, T, C), lambda i: (0, ci(i), 0)),
        ge=pl.BlockSpec((G, GDN_HEADS, 8, 128), lambda i: (ci(i), 0, 0, 0)),
        z=pl.BlockSpec((T, GDN_WIDTH), lambda i: (ci(i), P_Z // GDN_WIDTH)),
        oa=pl.BlockSpec((T, GDN_WIDTH), lambda i: (ci(i), 0)),
        vec=pl.BlockSpec((1, 128), lambda i: (0, 0)),
        st=pl.BlockSpec((G, GDN_HEADS, GDN_DIM, GDN_DIM), lambda i: (ci(i), 0, 0, 0)),
    )


def _gdn_scan(u_v, w_k, q_dec, k_end, attn, g_end, proj, gnw):
    S = proj.shape[0]
    C, G = GDN_CHUNK, GDN_SG
    nc = S // C
    sp = _gdn_scan_specs(S, G, False)
    ps = _GDN_PASSES["scan"]

    def body(uv_ref, wk_ref, qd_ref, ke_ref, at_ref, ge_ref, z_ref, gnw_ref, oa_ref, st_ref, s_scr):
        @pl.when(pl.program_id(0) == 0)
        def _():
            s_scr[...] = jnp.zeros_like(s_scr)

        for j in range(G):
            rows = slice(C * j, C * (j + 1))
            st = s_scr[...]
            st_ref[j] = st
            u = uv_ref[:, rows, :] - _bdot_raw(wk_ref[:, rows, :], st, "NN", ps)
            o = _bdot_raw(qd_ref[:, rows, :], st, "NN", ps) + _bdot_raw(at_ref[:, rows, :], u, "NN", ps)
            s_scr[...] = ge_ref[j][:, 0:1, 0:1] * st + _bdot_raw(ke_ref[:, rows, :], u, "TN", ps)
            for h in range(GDN_HEADS):
                cols = slice(GDN_DIM * h, GDN_DIM * (h + 1))
                oa_ref[rows, cols] = _gated_norm(o[h], z_ref[rows, cols], gnw_ref[...])

    return pl.pallas_call(
        body, name="gdn_scan", grid=(nc // G,),
        in_specs=[sp["hd"], sp["hd"], sp["hd"], sp["hd"], sp["hc"], sp["ge"], sp["z"], sp["vec"]],
        out_specs=(sp["oa"], sp["st"]),
        out_shape=(jax.ShapeDtypeStruct((S, GDN_WIDTH), F32),
                   jax.ShapeDtypeStruct((nc, GDN_HEADS, GDN_DIM, GDN_DIM), F32)),
        scratch_shapes=[pltpu.VMEM((GDN_HEADS, GDN_DIM, GDN_DIM), F32)],
        compiler_params=_cparams(("arbitrary",)),
    )(u_v, w_k, q_dec, k_end, attn, g_end, proj, gnw)


def _gdn_scan_bwd(u_v, w_k, q_dec, k_end, attn, g_end, proj, gnw, states, d_oa):
    S = proj.shape[0]
    C, G = GDN_CHUNK, GDN_SG
    nc = S // C
    sp = _gdn_scan_specs(S, G, True)
    ps, pb = _GDN_PASSES["scan"], _GDN_PASSES["bwd"]

    def body(uv_ref, wk_ref, qd_ref, ke_ref, at_ref, ge_ref, z_ref, gnw_ref, st_ref, doa_ref,
             duv_ref, dwk_ref, dqd_ref, dke_ref, dat_ref, dge_ref, dz_ref, dgnw_ref, ds_scr):
        @pl.when(pl.program_id(0) == 0)
        def _():
            ds_scr[...] = jnp.zeros_like(ds_scr)
            dgnw_ref[...] = jnp.zeros_like(dgnw_ref)

        dgnw = jnp.zeros((1, 128), F32)
        for j in reversed(range(G)):
            rows = slice(C * j, C * (j + 1))
            st = st_ref[j]
            wk, qd, ke, at = wk_ref[:, rows, :], qd_ref[:, rows, :], ke_ref[:, rows, :], at_ref[:, rows, :]
            u = uv_ref[:, rows, :] - _bdot_raw(wk, st, "NN", ps)
            o = _bdot_raw(qd, st, "NN", ps) + _bdot_raw(at, u, "NN", ps)
            dos = []
            for h in range(GDN_HEADS):
                cols = slice(GDN_DIM * h, GDN_DIM * (h + 1))
                _, vjp2 = jax.vjp(_gated_norm, o[h], z_ref[rows, cols], gnw_ref[...])
                do_h, dz_h, dgn = vjp2(doa_ref[rows, cols])
                dz_ref[rows, cols] = dz_h
                dgnw = dgnw + dgn
                dos.append(do_h)
            do = jnp.stack(dos)
            ds_new = ds_scr[...]
            du = _bdot_raw(at, do, "TN", pb) + _bdot_raw(ke, ds_new, "NN", pb)
            duv_ref[:, rows, :] = du
            dat_ref[:, rows, :] = _bdot_raw(do, u, "NT", pb)
            dqd_ref[:, rows, :] = _bdot_raw(do, st, "NT", pb)
            dke_ref[:, rows, :] = _bdot_raw(u, ds_new, "NT", pb)
            dwk_ref[:, rows, :] = -_bdot_raw(du, st, "NT", pb)
            d_ge = jnp.sum(jnp.sum(st * ds_new, axis=2, keepdims=True), axis=1, keepdims=True)
            dge_ref[j] = jnp.broadcast_to(d_ge, (GDN_HEADS, 8, 128))
            ds_scr[...] = ge_ref[j][:, 0:1, 0:1] * ds_new + _bdot_raw(qd, do, "TN", pb) - _bdot_raw(wk, du, "TN", pb)
        dgnw_ref[...] += dgnw

    return pl.pallas_call(
        body, name="gdn_scan_bwd", grid=(nc // G,),
        in_specs=[sp["hd"], sp["hd"], sp["hd"], sp["hd"], sp["hc"], sp["ge"], sp["z"], sp["vec"], sp["st"], sp["oa"]],
        out_specs=(sp["hd"], sp["hd"], sp["hd"], sp["hd"], sp["hc"], sp["ge"], sp["oa"], sp["vec"]),
        out_shape=(_hd_shape(S), _hd_shape(S), _hd_shape(S), _hd_shape(S), _hd_shape(S, C),
                   jax.ShapeDtypeStruct((nc, GDN_HEADS, 8, 128), F32), jax.ShapeDtypeStruct((S, GDN_WIDTH), F32),
                   jax.ShapeDtypeStruct((1, 128), F32)),
        scratch_shapes=[pltpu.VMEM((GDN_HEADS, GDN_DIM, GDN_DIM), F32)],
        compiler_params=_cparams(("arbitrary",)),
    )(u_v, w_k, q_dec, k_end, attn, g_end, proj, gnw, states, d_oa)


def _gdn_post(proj, conv_w, gp, tinv, u_v, w_k, d_uv, d_wk, d_qd, d_ke, d_at, d_ge):
    S = proj.shape[0]
    C, G = GDN_CHUNK, GDN_PG
    nc = S // C
    sp = _gdn_pre_specs(S, G)
    pb = _GDN_PASSES["bwd"]

    def body(cur_ref, prev_ref, ba_ref, cw_ref, gp_ref, ti_ref, uv_ref, wk_ref, duv_ref, dwk_ref, dqd_ref, dke_ref,
             dat_ref, dge_ref, dpre_ref, dba_ref, dgp_ref):
        i = pl.program_id(0)

        @pl.when(i == 0)
        def _():
            dgp_ref[...] = jnp.zeros_like(dgp_ref)

        prev = prev_ref[...] * jnp.where(i == 0, 0.0, 1.0)
        pre = _conv_taps(jnp.concatenate([prev, cur_ref[...]], axis=0), cw_ref[...], GDN_CONV, C * G)
        sg = jax.nn.sigmoid(pre)
        dsilu = sg * (1.0 + pre * (1.0 - sg))
        pairs, args = _gdn_pairs(pre * sg, ba_ref[...], gp_ref[...], G)
        _, vjp1 = jax.vjp(functools.partial(_gdn_stage1, dot=_bdot), *args)

        def take(ref):
            return jnp.stack([ref[h, C * j:C * (j + 1), :] for j, h in pairs])

        t, u_v, w_k = take(ti_ref), take(uv_ref), take(wk_ref)
        d_v = _bdot_raw(t, take(duv_ref), "TN", pb)
        d_rk = _bdot_raw(t, take(dwk_ref), "TN", pb)
        d_l = -(_bdot_raw(d_v, u_v, "NT", pb) + _bdot_raw(d_rk, w_k, "NT", pb))
        d_ge = jnp.stack([dge_ref[j, h][0:1, 0:1] for j, h in pairs])
        dcq, dck, dcv, db, da, dalog, ddtb = vjp1((d_l, d_v, d_rk, take(dqd_ref), take(dat_ref), take(dke_ref), d_ge))
        lane = lax.broadcasted_iota(jnp.int32, (C, 128), 1)
        lane1 = lax.broadcasted_iota(jnp.int32, (1, 128), 1)
        dgp = jnp.zeros((1, 128), F32)
        for j in range(G):
            rows = slice(C * j, C * (j + 1))
            dba = jnp.zeros((C, 128), F32)
            for h in range(GDN_HEADS):
                b = GDN_HEADS * j + h
                for o_, dcx in ((0, dcq), (GDN_WIDTH, dck), (2 * GDN_WIDTH, dcv)):
                    cols = slice(o_ + GDN_DIM * h, o_ + GDN_DIM * (h + 1))
                    dpre_ref[rows, cols] = dcx[b] * dsilu[rows, cols]
                dba = dba + jnp.where(lane == h, db[b], 0.0) + jnp.where(lane == GDN_HEADS + h, da[b], 0.0)
                dgp = dgp + jnp.where(lane1 == h, dalog[b], 0.0) + jnp.where(lane1 == GDN_HEADS + h, ddtb[b], 0.0)
            dba_ref[rows, :] = dba
        dgp_ref[0:1, :] += dgp

    T = C * G
    return pl.pallas_call(
        body, name="gdn_post", grid=(nc // G,),
        in_specs=[sp["cur"], sp["prev"], sp["ba"], sp["cw"], sp["vec"], sp["hc"], sp["hd"], sp["hd"], sp["hd"], sp["hd"],
                  sp["hd"], sp["hd"], sp["hc"], sp["ge"]],
        out_specs=(sp["cur"], pl.BlockSpec((T, 128), lambda i: (i, 0)), pl.BlockSpec((8, 128), lambda i: (0, 0))),
        out_shape=(jax.ShapeDtypeStruct((S, 3 * GDN_WIDTH), F32), jax.ShapeDtypeStruct((S, 128), F32),
                   jax.ShapeDtypeStruct((8, 128), F32)),
        compiler_params=_cparams(("arbitrary",)),
    )(proj, proj, proj, conv_w, gp, tinv, u_v, w_k, d_uv, d_wk, d_qd, d_ke, d_at, d_ge)


def _conv_bwd(dpre, x, xcol0, w, K, name, tc):
    S, Cc = dpre.shape
    T = _pick_tile(S, 256)
    nt, ncol = S // T, Cc // tc
    xo = xcol0 // tc

    def body(d_ref, dn_ref, x_ref, xp_ref, w_ref, dx_ref, dw_ref):
        i = pl.program_id(1)
        dn = dn_ref[...] * jnp.where(i == nt - 1, 0.0, 1.0)
        dv = d_ref[...]
        ext_d = jnp.concatenate([dv, dn], axis=0)
        wv = w_ref[...]
        dx_ref[...] = _conv_taps_t(ext_d, wv, K, T).astype(dx_ref.dtype)
        xp = xp_ref[...] * jnp.where(i == 0, 0.0, 1.0)
        ext_x = jnp.concatenate([xp, x_ref[...]], axis=0)

        @pl.when(i == 0)
        def _():
            dw_ref[...] = jnp.zeros_like(dw_ref)

        for k in range(K):
            lo = 8 - (K - 1) + k
            dw_ref[k:k + 1, :] += jnp.sum(dv * ext_x[lo:lo + T, :], axis=0, keepdims=True)

    r8 = T // 8
    return pl.pallas_call(
        body, name=name, grid=(ncol, nt),
        in_specs=[pl.BlockSpec((T, tc), lambda j, i: (i, j)),
                  pl.BlockSpec((8, tc), lambda j, i: (jnp.minimum((i + 1) * r8, S // 8 - 1), j)),
                  pl.BlockSpec((T, tc), lambda j, i: (i, j + xo)),
                  pl.BlockSpec((8, tc), lambda j, i: (jnp.maximum(i * r8 - 1, 0), j + xo)),
                  pl.BlockSpec((K, tc), lambda j, i: (0, j))],
        out_specs=(pl.BlockSpec((T, tc), lambda j, i: (i, j)), pl.BlockSpec((K, tc), lambda j, i: (0, j))),
        out_shape=(jax.ShapeDtypeStruct((S, Cc), _MXU), jax.ShapeDtypeStruct((K, Cc), F32)),
        compiler_params=_cparams(("parallel", "arbitrary")),
    )(dpre, dpre, x, x, w)


def _dil_bias(nt):
    T = ATT_T
    d = (np.arange(nt)[:, None, None] * T + np.arange(T)[None, :, None] - np.arange(T)[None, None, :])
    cnt = ((d >= 0) & (d <= 128)).astype(np.float64) + ((d >= 0) & (d % 4 == 0) & (d <= 512)) + ((d >= 0) & (d % 16 == 0))
    return jnp.asarray(np.where(cnt > 0, np.log(np.maximum(cnt, 1.0)), -1e30), dtype=F32)


def _attn_fwd(proj):
    S = proj.shape[0]
    T = ATT_T
    nt = S // T
    bias = _dil_bias(nt)
    scale = DIL_DIM ** -0.5
    npair = DIL_WIDTH // 128
    qb0, kb0, vb0 = P_QKVB // 128, (P_QKVB + DIL_WIDTH) // 128, (P_QKVB + 2 * DIL_WIDTH) // 128

    def body(q_ref, k_ref, v_ref, b_ref, o_ref, lse_ref):
        i = pl.program_id(1)
        qs = (q_ref[...] * scale).astype(_MXU)

        def step(j, carry):
            kt = k_ref[pl.ds(pl.multiple_of(j * T, T), T), :].astype(_MXU)
            vt = v_ref[pl.ds(pl.multiple_of(j * T, T), T), :].astype(_MXU)
            bt = b_ref[i - j]
            out = []
            for hh in range(2):
                m, l, acc = carry[hh]
                sl = slice(hh * DIL_DIM, (hh + 1) * DIL_DIM)
                s = lax.dot_general(qs[:, sl], kt[:, sl], (_NT, ((), ())), preferred_element_type=F32) + bt
                m_new = jnp.maximum(m, jnp.max(s, axis=-1, keepdims=True))
                p = jnp.exp(s - m_new)
                a = jnp.exp(m - m_new)
                l = a * l + jnp.sum(p, axis=-1, keepdims=True)
                acc = a * acc + lax.dot_general(p.astype(_MXU), vt[:, sl], (_NN, ((), ())), preferred_element_type=F32)
                out.append((m_new, l, acc))
            return tuple(out)

        init = tuple((jnp.full((T, 1), -1e30, F32), jnp.zeros((T, 1), F32), jnp.zeros((T, DIL_DIM), F32)) for _ in range(2))
        res = lax.fori_loop(0, i + 1, step, init)
        for hh in range(2):
            m, l, acc = res[hh]
            sl = slice(hh * DIL_DIM, (hh + 1) * DIL_DIM)
            o_ref[:, sl] = acc / l
            lse_ref[:, sl] = jnp.broadcast_to(m + jnp.log(l), (T, DIL_DIM))

    return pl.pallas_call(
        body, name="attn_fwd", grid=(npair, nt),
        in_specs=[pl.BlockSpec((T, 128), lambda p, i: (i, qb0 + p)),
                  pl.BlockSpec((S, 128), lambda p, i: (0, kb0 + p)),
                  pl.BlockSpec((S, 128), lambda p, i: (0, vb0 + p)),
                  pl.BlockSpec((nt, T, T), lambda p, i: (0, 0, 0))],
        out_specs=(pl.BlockSpec((T, 128), lambda p, i: (i, p)), pl.BlockSpec((T, 128), lambda p, i: (i, p))),
        out_shape=(jax.ShapeDtypeStruct((S, DIL_WIDTH), F32), jax.ShapeDtypeStruct((S, DIL_WIDTH), F32)),
        compiler_params=_cparams(("parallel", "parallel")),
    )(proj, proj, proj, bias)


def _attn_bwd(proj, o_b, lse, d_ob):
    S = proj.shape[0]
    T = ATT_T
    nt = S // T
    bias = _dil_bias(nt)
    scale = DIL_DIM ** -0.5
    npair = DIL_WIDTH // 128
    qb0, kb0, vb0 = P_QKVB // 128, (P_QKVB + DIL_WIDTH) // 128, (P_QKVB + 2 * DIL_WIDTH) // 128

    def body(q_ref, k_ref, v_ref, o_ref, lse_ref, do_ref, b_ref, dq_ref, dk_ref, dv_ref, dq_scr):
        j = pl.program_id(1)

        @pl.when(j == 0)
        def _():
            dq_scr[...] = jnp.zeros_like(dq_scr)

        kt = k_ref[...].astype(_MXU)
        vt = v_ref[...].astype(_MXU)

        def step(i, carry):
            rows = pl.ds(pl.multiple_of(i * T, T), T)
            qs = (q_ref[rows, :] * scale).astype(_MXU)
            dov = do_ref[rows, :]
            prod = dov * o_ref[rows, :]
            lsev = lse_ref[rows, :]
            dob = dov.astype(_MXU)
            bt = b_ref[i - j]
            out = []
            dqs = []
            for hh in range(2):
                dk, dv = carry[hh]
                sl = slice(hh * DIL_DIM, (hh + 1) * DIL_DIM)
                s = lax.dot_general(qs[:, sl], kt[:, sl], (_NT, ((), ())), preferred_element_type=F32) + bt
                p = jnp.exp(s - lsev[:, hh * DIL_DIM:hh * DIL_DIM + 1])
                delta = jnp.sum(prod[:, sl], axis=-1, keepdims=True)
                dp = lax.dot_general(dob[:, sl], vt[:, sl], (_NT, ((), ())), preferred_element_type=F32)
                ds = (p * (dp - delta)).astype(_MXU)
                dv = dv + lax.dot_general(p.astype(_MXU), dob[:, sl], (_TN, ((), ())), preferred_element_type=F32)
                dk = dk + lax.dot_general(ds, qs[:, sl], (_TN, ((), ())), preferred_element_type=F32)
                dqs.append(lax.dot_general(ds, kt[:, sl], (_NN, ((), ())), preferred_element_type=F32) * scale)
                out.append((dk, dv))
            dq_scr[rows, :] += jnp.concatenate(dqs, axis=1)
            return tuple(out)

        init = tuple((jnp.zeros((T, DIL_DIM), F32), jnp.zeros((T, DIL_DIM), F32)) for _ in range(2))
        res = lax.fori_loop(j, nt, step, init)
        dk_ref[...] = jnp.concatenate([res[0][0], res[1][0]], axis=1).astype(dk_ref.dtype)
        dv_ref[...] = jnp.concatenate([res[0][1], res[1][1]], axis=1).astype(dv_ref.dtype)

        @pl.when(j == nt - 1)
        def _():
            dq_ref[...] = dq_scr[...].astype(dq_ref.dtype)

    full = lambda c0: pl.BlockSpec((S, 128), lambda p, j: (0, c0 + p))
    tile = lambda c0: pl.BlockSpec((T, 128), lambda p, j: (j, c0 + p))
    out3 = jax.ShapeDtypeStruct((S, DIL_WIDTH), _MXU)
    return pl.pallas_call(
        body, name="attn_bwd", grid=(npair, nt),
        in_specs=[full(qb0), tile(kb0), tile(vb0), full(0), full(0), full(0),
                  pl.BlockSpec((nt, T, T), lambda p, j: (0, 0, 0))],
        out_specs=(full(0), tile(0), tile(0)),
        out_shape=(out3, out3, out3),
        scratch_shapes=[pltpu.VMEM((S, 128), F32)],
        compiler_params=_cparams(("parallel", "arbitrary")),
    )(proj, proj, proj, o_b, lse, d_ob, bias)


def _ffn_act(up, cw):
    S, Cc = up.shape[0], up.shape[1] // 2
    T, tc = _pick_tile(S, 256), _pick_tile(Cc, 1536)
    r8 = T // 8
    nct = Cc // tc

    def body(g_ref, gp_ref, u_ref, up_ref, wg_ref, wu_ref, o_ref):
        keep = jnp.where(pl.program_id(1) == 0, 0.0, 1.0)
        cg = _conv_taps(jnp.concatenate([gp_ref[...] * keep, g_ref[...]], axis=0), wg_ref[...], FFN_CONV, T)
        cu = _conv_taps(jnp.concatenate([up_ref[...] * keep, u_ref[...]], axis=0), wu_ref[...], FFN_CONV, T)
        o_ref[...] = (_silu(cg) * cu).astype(o_ref.dtype)

    cur = lambda o: pl.BlockSpec((T, tc), lambda j, i: (i, j + o))
    prev = lambda o: pl.BlockSpec((8, tc), lambda j, i: (jnp.maximum(i * r8 - 1, 0), j + o))
    wsp = lambda o: pl.BlockSpec((FFN_CONV, tc), lambda j, i: (0, j + o))
    return pl.pallas_call(
        body, name="ffn_act", grid=(nct, S // T),
        in_specs=[cur(0), prev(0), cur(nct), prev(nct), wsp(0), wsp(nct)], out_specs=cur(0),
        out_shape=jax.ShapeDtypeStruct((S, Cc), _MXU),
        compiler_params=_cparams(("parallel", "parallel")),
    )(up, up, up, up, cw, cw)


def _ffn_act_bwd(d_act, up, cw):
    S, Cc = up.shape[0], up.shape[1] // 2
    T, tc = _pick_tile(S, 256), _pick_tile(Cc, 1536)
    r8 = T // 8
    nt = S // T
    nct = Cc // tc
    K = FFN_CONV

    def body(da_ref, dan_ref, g_ref, gp_ref, gn_ref, u_ref, up_ref, un_ref, wg_ref, wu_ref,
             dg_ref, du_ref, dwg_ref, dwu_ref):
        i = pl.program_id(1)
        keep_p = jnp.where(i == 0, 0.0, 1.0)
        keep_n = jnp.where(i == nt - 1, 0.0, 1.0)
        wg, wu = wg_ref[...], wu_ref[...]
        xg = jnp.concatenate([gp_ref[...] * keep_p, g_ref[...], gn_ref[...] * keep_n], axis=0)
        xu = jnp.concatenate([up_ref[...] * keep_p, u_ref[...], un_ref[...] * keep_n], axis=0)
        cg = _conv_taps(xg, wg, K, T + 8)
        cu = _conv_taps(xu, wu, K, T + 8)
        da = jnp.concatenate([da_ref[...], dan_ref[...] * keep_n], axis=0)
        sg = jax.nn.sigmoid(cg)
        d_cg = da * cu * (sg * (1.0 + cg * (1.0 - sg)))
        d_cu = da * (cg * sg)
        dg_ref[...] = _conv_taps_t(d_cg, wg, K, T).astype(dg_ref.dtype)
        du_ref[...] = _conv_taps_t(d_cu, wu, K, T).astype(du_ref.dtype)

        @pl.when(i == 0)
        def _():
            dwg_ref[...] = jnp.zeros_like(dwg_ref)
            dwu_ref[...] = jnp.zeros_like(dwu_ref)

        for k in range(K):
            lo = 8 - (K - 1) + k
            dwg_ref[k:k + 1, :] += jnp.sum(d_cg[0:T, :] * xg[lo:lo + T, :], axis=0, keepdims=True)
            dwu_ref[k:k + 1, :] += jnp.sum(d_cu[0:T, :] * xu[lo:lo + T, :], axis=0, keepdims=True)

    cur = lambda o: pl.BlockSpec((T, tc), lambda j, i: (i, j + o))
    prev = lambda o: pl.BlockSpec((8, tc), lambda j, i: (jnp.maximum(i * r8 - 1, 0), j + o))
    nxt = lambda o: pl.BlockSpec((8, tc), lambda j, i: (jnp.minimum((i + 1) * r8, S // 8 - 1), j + o))
    wsp = lambda o: pl.BlockSpec((K, tc), lambda j, i: (0, j + o))
    return pl.pallas_call(
        body, name="ffn_act_bwd", grid=(nct, nt),
        in_specs=[cur(0), nxt(0), cur(0), prev(0), nxt(0), cur(nct), prev(nct), nxt(nct), wsp(0), wsp(nct)],
        out_specs=(cur(0), cur(0), wsp(0), wsp(0)),
        out_shape=(jax.ShapeDtypeStruct((S, Cc), _MXU), jax.ShapeDtypeStruct((S, Cc), _MXU),
                   jax.ShapeDtypeStruct((K, Cc), F32), jax.ShapeDtypeStruct((K, Cc), F32)),
        compiler_params=_cparams(("parallel", "arbitrary")),
    )(d_act, d_act, up, up, up, up, up, up, cw, cw)


def _local_step(x, tgt, n1w, n2w, fnw, gp, gnw, wp, conv_w, w_out, w_up4, fcw, w_down):
    h1 = _rmsnorm_fwd(x, n1w, "norm1")
    proj = _mm(h1, wp, "nn", name="proj")
    u_v, w_k, q_dec, k_end, attn, tinv, g_end = _gdn_pre(proj, conv_w, gp)
    o_a, states = _gdn_scan(u_v, w_k, q_dec, k_end, attn, g_end, proj, gnw)
    o_b, lse = _attn_fwd(proj)
    x2 = _mm(o_b, w_out[GDN_WIDTH:], "nn", residual=_mm(o_a, w_out[:GDN_WIDTH], "nn", residual=x, name="outproj_a"),
             name="outproj_b")
    h2 = _rmsnorm_fwd(x2, n2w, "norm2")
    up = _mm(h2, w_up4, "nn", b_blocks=True, name="up")
    act = _ffn_act(up, fcw)
    x3 = _mm(act, w_down, "nn", residual=x2, name="down")
    loss, dx3, d_fnw = _loss_head(x3, fnw, tgt, "loss_head")
    d_act = _mm(dx3, w_down, "nt", name="d_act")
    d_wdown = _mm(act, dx3, "tn", name="d_wdown")
    d_upg, d_upu, d_fcwg, d_fcwu = _ffn_act_bwd(d_act, up, fcw)
    d_wup = _mm(h2, d_upg, "tn", place=("blocks", N_CHIPS, 0), tn=w_up4.shape[2], name="d_wgate")
    d_wup = _mm(h2, d_upu, "tn", place=("blocks", N_CHIPS, N_CHIPS // 2), tn=w_up4.shape[2], into=d_wup, name="d_wup")
    d_h2 = _mm_nt_blocks([d_upg, d_upu], w_up4, "d_h2")
    dx2, d_n2w = _rmsnorm_bwd(d_h2, x2, n2w, dx3, "norm2_bwd")
    d_oa = _mm(dx2, w_out[:GDN_WIDTH], "nt", name="d_oa")
    d_ob = _mm(dx2, w_out[GDN_WIDTH:], "nt", name="d_ob")
    d_wout = _mm(o_a, dx2, "tn", place=("rows", D_MODEL, 0), name="d_wout_a")
    d_wout = _mm(o_b, dx2, "tn", place=("rows", D_MODEL, GDN_WIDTH), into=d_wout, name="d_wout_b")
    dq_b, dk_b, dv_b = _attn_bwd(proj, o_b, lse, d_ob)
    d_uv, d_wk, d_qd, d_ke, d_at, d_ge, d_z, d_gnw = _gdn_scan_bwd(u_v, w_k, q_dec, k_end, attn, g_end, proj, gnw, states, d_oa)
    d_pre, d_ba, d_gp = _gdn_post(proj, conv_w, gp, tinv, u_v, w_k, d_uv, d_wk, d_qd, d_ke, d_at, d_ge)
    d_qkva, d_convw = _conv_bwd(d_pre, proj, 0, conv_w, GDN_CONV, "gdn_conv_bwd", 512)
    d_proj = jnp.concatenate([d_qkva, d_z.astype(_MXU), dq_b, dk_b, dv_b, d_ba.astype(_MXU),
                              jnp.zeros((x.shape[0], P_COLS - P_BA - 128), _MXU)], axis=1)
    d_wp = _mm(h1, d_proj, "tn", name="d_wp")
    d_h1 = _mm(d_proj, wp, "nt", name="d_h1")
    dx, d_n1w = _rmsnorm_bwd(d_h1, x, n1w, dx2, "norm1_bwd")
    grads = dict(wp=d_wp, conv_w=d_convw, w_out=d_wout, w_up=d_wup, fcw_g=d_fcwg, fcw_u=d_fcwu, w_down=d_wdown,
                 n1w=d_n1w, n2w=d_n2w, fnw=d_fnw, gp=d_gp, gnw=d_gnw)
    return loss, dx, grads


_HBM = pl.BlockSpec(memory_space=pltpu.HBM)


def _pos():
    return lax.axis_index("x"), lax.axis_index("y"), lax.axis_index("c")


def _other_chips(x, y):
    return [(1 - x, y), (x, 1 - y), (1 - x, 1 - y)]


def _halvable(shape):
    return shape[0] % 32 == 0


def _rows_of_half(shape, half):
    if not _halvable(shape):
        return pl.ds(0, shape[0])
    return pl.ds(pl.multiple_of(half * (shape[0] // 2), 16), shape[0] // 2)


def _gather_halves(shards, name):
    n = len(shards)
    shapes = [s.shape for s in shards]

    def body(*refs):
        ins, outs = refs[:n], refs[n:2 * n]
        send_sems, recv_sems = refs[2 * n:]
        x, y, c = _pos()
        q = 2 * x + y
        chips = _other_chips(x, y)

        def copy(a, j, block):
            px, py = chips[j]
            rows = _rows_of_half(shapes[a], c)
            return pltpu.make_async_remote_copy(
                src_ref=ins[a].at[rows, :], dst_ref=outs[a].at[block, rows, :], send_sem=send_sems.at[3 * a + j],
                recv_sem=recv_sems.at[3 * a + j], device_id=(px, py, c), device_id_type=MESH)

        sends = [copy(a, j, q) for a in range(n) for j in range(3)]
        for cp in sends:
            cp.start()
        for a in range(n):
            for j, (px, py) in enumerate(chips):
                copy(a, j, 2 * px + py).wait_recv()
        for cp in sends:
            cp.wait_send()

    return pl.pallas_call(
        body, name=name, in_specs=[_HBM] * n, out_specs=[_HBM] * n,
        out_shape=[jax.ShapeDtypeStruct((N_CHIPS,) + s.shape, s.dtype) for s in shards],
        scratch_shapes=[pltpu.SemaphoreType.DMA((3 * n,)), pltpu.SemaphoreType.DMA((3 * n,))],
    )(*shards)


def _sibling_fill(gathered, name):
    big = [a for a, g in enumerate(gathered) if _halvable(g.shape[1:])]
    n = len(gathered)

    def body(*refs):
        ins, outs = refs[:n], refs[n:2 * n]
        send_sems, recv_sems = refs[2 * n:]
        x, y, c = _pos()
        chips = _other_chips(x, y)

        def copy(k, j, half):
            a = big[k]
            px, py = chips[j]
            rows = _rows_of_half(gathered[a].shape[1:], half)
            return pltpu.make_async_remote_copy(
                src_ref=ins[a].at[2 * px + py, rows, :], dst_ref=outs[a].at[2 * px + py, rows, :],
                send_sem=send_sems.at[3 * k + j], recv_sem=recv_sems.at[3 * k + j],
                device_id=(x, y, 1 - c), device_id_type=MESH)

        sends = [copy(k, j, c) for k in range(len(big)) for j in range(3)]
        for cp in sends:
            cp.start()
        for k in range(len(big)):
            for j in range(3):
                copy(k, j, 1 - c).wait_recv()
        for cp in sends:
            cp.wait_send()

    return pl.pallas_call(
        body, name=name, in_specs=[_HBM] * n, out_specs=[_HBM] * n,
        out_shape=[jax.ShapeDtypeStruct(g.shape, g.dtype) for g in gathered],
        input_output_aliases={a: a for a in range(n)},
        scratch_shapes=[pltpu.SemaphoreType.DMA((3 * len(big),)), pltpu.SemaphoreType.DMA((3 * len(big),))],
    )(*gathered)


def _place_own(shards, gathered, cq, name):
    n = len(shards)
    steps = 4

    def body(cq_ref, *refs):
        for a in range(n):
            refs[2 * n + a][...] = refs[a][...]

    def tile(shape):
        return shape[0] // steps if _halvable(shape) else shape[0]

    in_specs = [pl.BlockSpec((tile(s.shape), s.shape[1]), (lambda i, s_: (i, 0)) if _halvable(s.shape) else (lambda i, s_: (0, 0)))
                for s in shards]
    in_specs += [pl.BlockSpec(memory_space=pl.ANY)] * n
    out_specs = [pl.BlockSpec((None, tile(s.shape), s.shape[1]),
                              (lambda i, s_: (s_[1], i, 0)) if _halvable(s.shape) else (lambda i, s_: (s_[1], 0, 0)))
                 for s in shards]
    gs = pltpu.PrefetchScalarGridSpec(num_scalar_prefetch=1, grid=(steps,), in_specs=in_specs, out_specs=out_specs)
    return pl.pallas_call(
        body, name=name, grid_spec=gs, out_shape=[jax.ShapeDtypeStruct(g.shape, g.dtype) for g in gathered],
        input_output_aliases={1 + n + a: a for a in range(n)},
        compiler_params=_cparams(("arbitrary",)),
    )(cq, *shards, *gathered)


def _half_rows(ref, c, rh):
    return ref.at[:, pl.ds(pl.multiple_of(c * rh, 8), rh), :]


def _grad_sibling(fams, small):
    n = len(fams)
    rhs = [f.shape[1] // 2 for f in fams]

    def body(*refs):
        ins, small_ref = refs[:n], refs[n]
        outs, all_ref = refs[n + 1:2 * n + 1], refs[2 * n + 1]
        send_sems, recv_sems, loc_sem = refs[2 * n + 2:]
        x, y, c = _pos()
        me = 4 * x + 2 * y + c
        mine = pltpu.make_async_copy(small_ref, all_ref.at[me], loc_sem)
        mine.start()
        bigs = [pltpu.make_async_remote_copy(src_ref=_half_rows(ins[a], 1 - c, rhs[a]), dst_ref=outs[a],
                                             send_sem=send_sems.at[7 + a], recv_sem=recv_sems.at[7 + a],
                                             device_id=(x, y, 1 - c), device_id_type=MESH) for a in range(n)]
        for cp in bigs:
            cp.start()

        def peer(r):
            dx, dy, dc = (r >> 2) & 1, (r >> 1) & 1, r & 1
            px = x if dx == 0 else 1 - x
            py = y if dy == 0 else 1 - y
            pc = c if dc == 0 else 1 - c
            return px, py, pc

        def small_copy(r, slot):
            return pltpu.make_async_remote_copy(src_ref=small_ref, dst_ref=all_ref.at[slot], send_sem=send_sems.at[r - 1],
                                                recv_sem=recv_sems.at[r - 1], device_id=peer(r), device_id_type=MESH)

        sends = [small_copy(r, me) for r in range(1, 8)]
        for cp in sends:
            cp.start()
        for r in range(1, 8):
            px, py, pc = peer(r)
            small_copy(r, 4 * px + 2 * py + pc).wait_recv()
        for cp in bigs:
            cp.wait_recv()
        for cp in bigs + sends:
            cp.wait_send()
        mine.wait()

    return pl.pallas_call(
        body, name="grad_sibling", in_specs=[_HBM] * (n + 1), out_specs=[_HBM] * (n + 1),
        out_shape=[jax.ShapeDtypeStruct((f.shape[0], f.shape[1] // 2, f.shape[2]), f.dtype) for f in fams]
        + [jax.ShapeDtypeStruct((8,) + small.shape, small.dtype)],
        scratch_shapes=[pltpu.SemaphoreType.DMA((7 + n,)), pltpu.SemaphoreType.DMA((7 + n,)), pltpu.SemaphoreType.DMA],
    )(*fams, small)


def _grad_chips(parts):
    n = len(parts)

    def body(*refs):
        ins, outs = refs[:n], refs[n:2 * n]
        send_sems, recv_sems = refs[2 * n:]
        x, y, c = _pos()
        chips = _other_chips(x, y)

        def copy(a, j):
            px, py = chips[j]
            return pltpu.make_async_remote_copy(src_ref=ins[a].at[2 * px + py], dst_ref=outs[a].at[j],
                                                send_sem=send_sems.at[3 * a + j], recv_sem=recv_sems.at[3 * a + j],
                                                device_id=(px, py, c), device_id_type=MESH)

        sends = [copy(a, j) for a in range(n) for j in range(3)]
        for cp in sends:
            cp.start()
        for a in range(n):
            for j in range(3):
                copy(a, j).wait_recv()
        for cp in sends:
            cp.wait_send()

    return pl.pallas_call(
        body, name="grad_chips", in_specs=[_HBM] * n, out_specs=[_HBM] * n,
        out_shape=[jax.ShapeDtypeStruct((3,) + p.shape[1:], p.dtype) for p in parts],
        scratch_shapes=[pltpu.SemaphoreType.DMA((3 * n,)), pltpu.SemaphoreType.DMA((3 * n,))],
    )(*parts)


def _grad_share(fulls):
    n = len(fulls)
    rhs = [f.shape[0] // 2 for f in fulls]

    def body(*refs):
        ins, outs = refs[:n], refs[n:2 * n]
        send_sems, recv_sems = refs[2 * n:]
        x, y, c = _pos()

        def copy(a, half):
            rows = pl.ds(pl.multiple_of(half * rhs[a], 8), rhs[a])
            return pltpu.make_async_remote_copy(src_ref=ins[a].at[rows, :], dst_ref=outs[a].at[rows, :],
                                                send_sem=send_sems.at[a], recv_sem=recv_sems.at[a],
                                                device_id=(x, y, 1 - c), device_id_type=MESH)

        sends = [copy(a, c) for a in range(n)]
        for cp in sends:
            cp.start()
        for a in range(n):
            copy(a, 1 - c).wait_recv()
        for cp in sends:
            cp.wait_send()

    return pl.pallas_call(
        body, name="grad_share", in_specs=[_HBM] * n, out_specs=[_HBM] * n,
        out_shape=[jax.ShapeDtypeStruct(f.shape, f.dtype) for f in fulls],
        input_output_aliases={a: a for a in range(n)},
        scratch_shapes=[pltpu.SemaphoreType.DMA((n,)), pltpu.SemaphoreType.DMA((n,))],
    )(*fulls)


def _add_sibling(own, recv, cq, name):
    nb, R, Cc = own.shape
    Rh = R // 2

    def body(cq_ref, a_ref, b_ref, o32_ref, o16_ref):
        s = a_ref[...] + b_ref[...]
        o32_ref[...] = s
        o16_ref[...] = s.astype(o16_ref.dtype)

    sp = pl.BlockSpec((1, Rh, Cc), lambda b, s: (b, 0, 0))
    gs = pltpu.PrefetchScalarGridSpec(
        num_scalar_prefetch=1, grid=(nb,),
        in_specs=[pl.BlockSpec((1, Rh, Cc), lambda b, s: (b, s[0], 0)), sp], out_specs=[sp, sp])
    return pl.pallas_call(
        body, name=name, grid_spec=gs,
        out_shape=[jax.ShapeDtypeStruct((nb, Rh, Cc), F32), jax.ShapeDtypeStruct((nb, Rh, Cc), _MXU)],
        compiler_params=_cparams(("parallel",)),
    )(cq, own, recv)


def _add_chips(part32, recv3, cq, name):
    nb, Rh, Cc = part32.shape

    def body(cq_ref, a_ref, b_ref, o_ref):
        acc = a_ref[0]
        for j in range(3):
            acc = acc + b_ref[j].astype(F32)
        o_ref[...] = acc

    gs = pltpu.PrefetchScalarGridSpec(
        num_scalar_prefetch=1, grid=(1,),
        in_specs=[pl.BlockSpec((1, Rh, Cc), lambda i, s: (s[1], 0, 0)), pl.BlockSpec((3, Rh, Cc), lambda i, s: (0, 0, 0))],
        out_specs=pl.BlockSpec((Rh, Cc), lambda i, s: (s[0], 0)))
    return pl.pallas_call(
        body, name=name, grid_spec=gs, out_shape=jax.ShapeDtypeStruct((2 * Rh, Cc), F32),
        compiler_params=_cparams(("arbitrary",)),
    )(cq, part32, recv3)


def _sum_devices(small_all):
    def body(s_ref, o_ref):
        tot = s_ref[0]
        for d in range(1, 8):
            tot = tot + s_ref[d]
        o_ref[...] = tot

    return pl.pallas_call(body, name="sum_devices", out_shape=jax.ShapeDtypeStruct(small_all.shape[1:], F32))(small_all)


def _adamw(w, g, m, v, name):
    R, Cc = w.shape
    T = max([t for t in range(8, 257, 8) if R % t == 0], default=R)
    c1 = 1.0 / (1.0 - ADAM_B1 ** ADAM_STEP)
    c2 = 1.0 / (1.0 - ADAM_B2 ** ADAM_STEP)

    def body(w_ref, g_ref, m_ref, v_ref, d_ref, mo_ref, vo_ref):
        gv = g_ref[...]
        mn = ADAM_B1 * m_ref[...] + (1.0 - ADAM_B1) * gv
        vn = ADAM_B2 * v_ref[...] + (1.0 - ADAM_B2) * (gv * gv)
        mo_ref[...] = mn
        vo_ref[...] = vn
        d_ref[...] = -ADAM_LR * ((mn * c1) / (jnp.sqrt(vn * c2) + ADAM_EPS) + ADAM_WD * w_ref[...])

    sp = pl.BlockSpec((T, Cc), lambda i: (i, 0))
    sh = jax.ShapeDtypeStruct((R, Cc), F32)
    return pl.pallas_call(
        body, name=name, grid=(R // T,), in_specs=[sp] * 4, out_specs=(sp, sp, sp), out_shape=(sh, sh, sh),
        compiler_params=_cparams(("parallel",)),
    )(w, g, m, v)


SMALL_ROWS = 32
REPL_ROWS = 8


def _pad_lanes(v, n=D_MODEL):
    return jnp.pad(v, ((0, 0), (0, n - v.shape[1])))


def kernel(x, norm1_w, w_in, conv_qkv_w, a_log, dt_bias, gdn_norm_w, w_out, norm2_w, w_up, ffn_conv_w, w_down, final_norm_w, loss_target, m_norm1_w, m_w_in, m_conv_qkv_w, m_a_log, m_dt_bias, m_gdn_norm_w, m_w_out, m_norm2_w, m_w_up, m_ffn_conv_w, m_w_down, m_final_norm_w, v_norm1_w, v_w_in, v_conv_qkv_w, v_a_log, v_dt_bias, v_gdn_norm_w, v_w_out, v_norm2_w, v_w_up, v_ffn_conv_w, v_w_down, v_final_norm_w):
    c = lax.axis_index("c")
    q = 2 * lax.axis_index("x") + lax.axis_index("y")
    S = x.shape[1]
    cq = jnp.stack([c, q]).astype(jnp.int32)

    def gather(shards, tag):
        got = _gather_halves(shards, "gather_" + tag)
        got = _sibling_fill(got, "fill_" + tag)
        return _place_own(shards, got, cq, "place_" + tag)

    g_in, g_conv, g_fconv = gather([w_in[0].astype(_MXU), conv_qkv_w[0], ffn_conv_w[0]], "in")
    g_out, g_up, g_down = gather([w_out[0].astype(_MXU), w_up[0].astype(_MXU), w_down[0].astype(_MXU)], "rest")
    wp = _wp_assemble(g_in)
    w_out_f = g_out.reshape(D_MODEL, D_MODEL)
    w_down_f = g_down.reshape(D_FF, D_MODEL)
    conv_f = jnp.concatenate([g_conv[i] for i in range(N_CHIPS)], axis=1)
    fcw = jnp.concatenate([g_fconv[i] for i in range(N_CHIPS)], axis=1)
    gp = _pad_lanes(jnp.concatenate([a_log, dt_bias], axis=1), 128)
    fnw = final_norm_w[None, :]
    loss_l, dx, g = _local_step(x[0], loss_target[0], norm1_w, norm2_w, fnw, gp, gdn_norm_w, wp, conv_f, w_out_f,
                                g_up, fcw, w_down_f)
    loss = lax.psum(loss_l[0, 0], ("x", "y", "c"))
    fams = [_win_split(g["wp"]), g["w_up"], g["w_out"].reshape(N_CHIPS, D_MODEL // N_CHIPS, D_MODEL),
            g["w_down"].reshape(N_CHIPS, D_FF // N_CHIPS, D_MODEL)]
    n_fc = FFN_CONV * D_FF

    def rows_of(v):
        flat = v.reshape(-1)
        return jnp.pad(flat, (0, -flat.shape[0] % D_MODEL)).reshape(-1, D_MODEL)

    small = jnp.concatenate([g["n1w"], g["n2w"], g["fnw"], _pad_lanes(g["gp"][0:1]), _pad_lanes(g["gnw"]),
                             rows_of(g["conv_w"]), rows_of(g["fcw_g"]), rows_of(g["fcw_u"])], axis=0)
    small = jnp.pad(small, ((0, SMALL_ROWS - small.shape[0]), (0, 0)))
    *got, small_all = _grad_sibling(fams, small)
    fam_names = ["w_in", "w_up", "w_out", "w_down"]
    parts = [_add_sibling(f, r, cq, "add_sibling_" + nm) for f, r, nm in zip(fams, got, fam_names)]
    got3 = _grad_chips([p[1] for p in parts])
    halves = [_add_chips(p[0], r3, cq, "add_chips_" + nm) for p, r3, nm in zip(parts, got3, fam_names)]
    g_w_in, g_w_up, g_w_out, g_w_down = _grad_share(halves)
    small_red = _sum_devices(small_all)
    r0 = 5
    r1 = r0 + GDN_CONV * 3 * GDN_WIDTH // D_MODEL
    r2 = r1 + -(-n_fc // D_MODEL)
    conv_red = small_red[r0:r1].reshape(GDN_CONV, 3 * GDN_WIDTH)
    fc_red = jnp.concatenate([small_red[r1:r2].reshape(-1)[:n_fc].reshape(FFN_CONV, D_FF),
                              small_red[r2:2 * r2 - r1].reshape(-1)[:n_fc].reshape(FFN_CONV, D_FF)], axis=1)
    g_conv_w = lax.dynamic_slice_in_dim(conv_red, q * (3 * GDN_WIDTH // N_CHIPS), 3 * GDN_WIDTH // N_CHIPS, axis=1)
    g_fconv_w = lax.dynamic_slice_in_dim(fc_red, q * (2 * D_FF // N_CHIPS), 2 * D_FF // N_CHIPS, axis=1)
    g_n1w, g_n2w, g_fnw = small_red[0:1], small_red[1:2], small_red[2]
    g_alog, g_dtb, g_gnw = small_red[3:4, 0:4], small_red[3:4, 4:8], small_red[4:5, 0:128]
    big = {}
    for nm, w, gg, m, v in (("w_in", w_in, g_w_in, m_w_in, v_w_in), ("conv_qkv_w", conv_qkv_w, g_conv_w, m_conv_qkv_w, v_conv_qkv_w),
                            ("w_out", w_out, g_w_out, m_w_out, v_w_out), ("w_up", w_up, g_w_up, m_w_up, v_w_up),
                            ("ffn_conv_w", ffn_conv_w, g_fconv_w, m_ffn_conv_w, v_ffn_conv_w),
                            ("w_down", w_down, g_w_down, m_w_down, v_w_down)):
        d_, m_, v_ = _adamw(w[0], gg, m[0], v[0], "adamw_" + nm)
        big[nm] = (gg[None], d_[None], m_[None], v_[None])

    def pack_small(n1, n2, fn, al, db, gn):
        return jnp.concatenate([n1, n2, fn[None, :], _pad_lanes(jnp.concatenate([al, db], axis=1)), _pad_lanes(gn),
                                jnp.zeros((REPL_ROWS - 5, D_MODEL), F32)], axis=0)

    sw = pack_small(norm1_w, norm2_w, final_norm_w, a_log, dt_bias, gdn_norm_w)
    sm = pack_small(m_norm1_w, m_norm2_w, m_final_norm_w, m_a_log, m_dt_bias, m_gdn_norm_w)
    sv = pack_small(v_norm1_w, v_norm2_w, v_final_norm_w, v_a_log, v_dt_bias, v_gdn_norm_w)
    sd, smn, svn = _adamw(sw, small_red[:REPL_ROWS], sm, sv, "adamw_small")

    def unpack_small(t):
        return dict(norm1_w=t[0:1], norm2_w=t[1:2], final_norm_w=t[2], a_log=t[3:4, 0:4], dt_bias=t[3:4, 4:8],
                    gdn_norm_w=t[4:5, 0:128])

    sg = dict(norm1_w=g_n1w, norm2_w=g_n2w, final_norm_w=g_fnw, a_log=g_alog, dt_bias=g_dtb, gdn_norm_w=g_gnw)
    sd, smn, svn = unpack_small(sd), unpack_small(smn), unpack_small(svn)
    names = ["norm1_w", "w_in", "conv_qkv_w", "a_log", "dt_bias", "gdn_norm_w", "w_out", "norm2_w", "w_up",
             "ffn_conv_w", "w_down", "final_norm_w"]
    grads = [big[n][0] if n in big else sg[n] for n in names]
    deltas = [big[n][1] if n in big else sd[n] for n in names]
    new_m = [big[n][2] if n in big else smn[n] for n in names]
    new_v = [big[n][3] if n in big else svn[n] for n in names]
    return (loss, dx[None], *grads, *deltas, *new_m, *new_v)
```

```python
import functools
import math

import numpy as np
import jax
import jax.numpy as jnp
from jax import lax
from jax.experimental import pallas as pl
from jax.experimental.pallas import tpu as pltpu

F32 = jnp.float32
BF16 = jnp.bfloat16
_MXU = jnp.bfloat16
_HI = lax.Precision.HIGHEST
EPS = 1e-6
V7X_VMEM_LIMIT = 56 * 1024 * 1024
MESH = pl.DeviceIdType.MESH

D_MODEL = 1024
GDN_HEADS, GDN_DIM, GDN_CHUNK, GDN_CONV = 4, 128, 64, 4
GDN_WIDTH = GDN_HEADS * GDN_DIM
DIL_HEADS, DIL_DIM = 8, 64
DIL_WIDTH = DIL_HEADS * DIL_DIM
D_FF, FFN_CONV = 2816, 3
IN_COLS = 3592
P_COLS = 3840
P_Z, P_QKVB, P_BA = 1536, 2048, 3584
ATT_T = 256
ADAM_LR, ADAM_B1, ADAM_B2, ADAM_EPS, ADAM_WD, ADAM_STEP = 0.001, 0.9, 0.999, 1e-08, 0.01, 10
N_CHIPS = 4


def _cparams(sem=None, vmem=None):
    kw = {}
    if sem is not None:
        kw["dimension_semantics"] = sem
    if vmem is not None:
        kw["vmem_limit_bytes"] = vmem
    return pltpu.CompilerParams(**kw)


def _silu(x):
    return x * jax.nn.sigmoid(x)


def _pick_tile(n, cap):
    best = None
    for t in range(128, min(n, cap) + 1, 128):
        if n % t == 0:
            best = t
    return best or n


def _mm(a, b, mode, *, out_dtype=F32, residual=None, name, b_blocks=False, place=None, into=None, tn=None):
    if mode == "nn":
        M, K = a.shape
        N = b.shape[0] * b.shape[2] if b_blocks else b.shape[1]
    elif mode == "nt":
        (M, K), (N, _) = a.shape, b.shape
    else:
        (K, M), (_, N) = a.shape, b.shape
    tm = _pick_tile(M, 1024)
    tn = b.shape[2] if b_blocks else (tn or _pick_tile(N, 1536))

    def vmem(tm, tn):
        return 2 * (tm * K * a.dtype.itemsize + tn * K * b.dtype.itemsize
                    + tm * tn * (jnp.dtype(out_dtype).itemsize + (4 if residual is not None else 0))) + 3 * tm * tn * 4

    fixed_tn = b_blocks or (place is not None and place[0] == "blocks")
    while vmem(tm, tn) > 40 * 1024 * 1024:
        if (tm >= tn or fixed_tn) and tm % 256 == 0:
            tm //= 2
        elif tn % 256 == 0 and not fixed_tn:
            tn //= 2
        else:
            tm //= 2
    a_spec = pl.BlockSpec((K, tm), lambda j, i: (0, i)) if mode == "tn" else pl.BlockSpec((tm, K), lambda j, i: (i, 0))
    if b_blocks:
        b_spec = pl.BlockSpec((None, K, tn), lambda j, i: (j, 0, 0))
    else:
        b_spec = pl.BlockSpec((tn, K), lambda j, i: (j, 0)) if mode == "nt" else pl.BlockSpec((K, tn), lambda j, i: (0, j))
    r_spec = pl.BlockSpec((tm, tn), lambda j, i: (i, j))
    if place is None:
        o_spec, o_shape = r_spec, (M, N)
    elif place[0] == "rows":
        off = place[2] // tm
        o_spec, o_shape = pl.BlockSpec((tm, tn), lambda j, i: (i + off, j)), (place[1], N)
    else:
        off = place[2]
        o_spec, o_shape = pl.BlockSpec((None, tm, tn), lambda j, i: (j + off, i, 0)), (place[1], M, tn)
    dims = {"nn": (((1,), (0,)), ((), ())), "nt": (((1,), (1,)), ((), ())), "tn": (((0,), (0,)), ((), ()))}[mode]

    def body(*refs):
        a_ref, b_ref = refs[0], refs[1]
        o_ref = refs[-1]
        acc = lax.dot_general(a_ref[...].astype(_MXU), b_ref[...].astype(_MXU), dims, preferred_element_type=F32)
        if residual is not None:
            acc = acc + refs[2][...]
        o_ref[...] = acc.astype(out_dtype)

    ins, specs, alias = [a, b], [a_spec, b_spec], {}
    if residual is not None:
        ins.append(residual)
        specs.append(r_spec)
    if into is not None:
        alias = {len(ins): 0}
        ins.append(into)
        specs.append(pl.BlockSpec(memory_space=pl.ANY))
    return pl.pallas_call(
        body, name=name, grid=(N // tn, M // tm), in_specs=specs, out_specs=o_spec,
        out_shape=jax.ShapeDtypeStruct(o_shape, out_dtype), input_output_aliases=alias,
        compiler_params=_cparams(("parallel", "parallel"), V7X_VMEM_LIMIT),
    )(*ins)


def _mm_nt_blocks(a_list, b4, name):
    M = a_list[0].shape[0]
    nb, N, Kb = b4.shape
    tm, tn = _pick_tile(M, 512), _pick_tile(N, 512)

    def body(a0_ref, a1_ref, b_ref, o_ref):
        acc = None
        for blk in range(nb):
            a_ref = (a0_ref, a1_ref)[blk // 2]
            lo = (blk % 2) * Kb
            t = lax.dot_general(a_ref[:, lo:lo + Kb].astype(_MXU), b_ref[blk].astype(_MXU), (((1,), (1,)), ((), ())),
                                preferred_element_type=F32)
            acc = t if acc is None else acc + t
        o_ref[...] = acc

    a_spec = pl.BlockSpec((tm, 2 * Kb), lambda j, i: (i, 0))
    return pl.pallas_call(
        body, name=name, grid=(N // tn, M // tm),
        in_specs=[a_spec, a_spec, pl.BlockSpec((nb, tn, Kb), lambda j, i: (0, j, 0))],
        out_specs=pl.BlockSpec((tm, tn), lambda j, i: (i, j)), out_shape=jax.ShapeDtypeStruct((M, N), F32),
        compiler_params=_cparams(("parallel", "parallel"), V7X_VMEM_LIMIT),
    )(a_list[0], a_list[1], b4)


def _wp_assemble(g_in):
    nb, Dm, Wb = g_in.shape
    T = 256
    n_lo = P_QKVB - 2 * Wb

    def body(g_ref, o_ref):
        g2 = g_ref[2]
        o_ref[...] = jnp.concatenate(
            [g_ref[0], g_ref[1], g2[:, :n_lo], g2[:, n_lo + 8:], g_ref[3], g2[:, n_lo:n_lo + 8],
             jnp.zeros((T, P_COLS - P_BA - 8), g_in.dtype)], axis=1)

    return pl.pallas_call(
        body, name="wp_assemble", grid=(Dm // T,), in_specs=[pl.BlockSpec((nb, T, Wb), lambda i: (0, i, 0))],
        out_specs=pl.BlockSpec((T, P_COLS), lambda i: (i, 0)), out_shape=jax.ShapeDtypeStruct((Dm, P_COLS), g_in.dtype),
        compiler_params=_cparams(("parallel",)),
    )(g_in)


def _win_split(d_wp):
    Dm = d_wp.shape[0]
    Wb = IN_COLS // N_CHIPS
    T = 256

    def body(x_ref, o_ref):
        xv = x_ref[...]
        o_ref[0] = xv[:, 0:Wb]
        o_ref[1] = xv[:, Wb:2 * Wb]
        o_ref[2] = jnp.concatenate([xv[:, 2 * Wb:P_QKVB], xv[:, P_BA:P_BA + 8], xv[:, P_QKVB:3 * Wb - 8]], axis=1)
        o_ref[3] = xv[:, 3 * Wb - 8:P_BA]

    return pl.pallas_call(
        body, name="win_split", grid=(Dm // T,), in_specs=[pl.BlockSpec((T, P_COLS), lambda i: (i, 0))],
        out_specs=pl.BlockSpec((N_CHIPS, T, Wb), lambda i: (0, i, 0)),
        out_shape=jax.ShapeDtypeStruct((N_CHIPS, Dm, Wb), F32), compiler_params=_cparams(("parallel",)),
    )(d_wp)


def _rmsnorm_fwd(x, w, name):
    S, D = x.shape
    T = _pick_tile(S, 512)

    def body(x_ref, w_ref, o_ref):
        xv = x_ref[...]
        rs = lax.rsqrt(jnp.mean(xv * xv, axis=-1, keepdims=True) + EPS)
        o_ref[...] = (xv * rs * w_ref[...]).astype(o_ref.dtype)

    return pl.pallas_call(
        body, name=name, grid=(S // T,),
        in_specs=[pl.BlockSpec((T, D), lambda i: (i, 0)), pl.BlockSpec((1, D), lambda i: (0, 0))],
        out_specs=pl.BlockSpec((T, D), lambda i: (i, 0)),
        out_shape=jax.ShapeDtypeStruct((S, D), _MXU),
        compiler_params=_cparams(("parallel",)),
    )(x, w)


def _rmsnorm_bwd(dh, x, w, dres, name):
    S, D = x.shape
    T = _pick_tile(S, 512)

    def body(dh_ref, x_ref, w_ref, dres_ref, dx_ref, dw_ref):
        xv = x_ref[...]
        rs = lax.rsqrt(jnp.mean(xv * xv, axis=-1, keepdims=True) + EPS)
        xn = xv * rs
        dhv = dh_ref[...]
        dxn = dhv * w_ref[...]
        dx_ref[...] = dres_ref[...] + rs * (dxn - xn * jnp.mean(dxn * xn, axis=-1, keepdims=True))

        @pl.when(pl.program_id(0) == 0)
        def _():
            dw_ref[...] = jnp.zeros_like(dw_ref)

        dw_ref[...] += jnp.sum(dhv * xn, axis=0, keepdims=True)

    row = pl.BlockSpec((T, D), lambda i: (i, 0))
    vec = pl.BlockSpec((1, D), lambda i: (0, 0))
    return pl.pallas_call(
        body, name=name, grid=(S // T,), in_specs=[row, row, vec, row], out_specs=(row, vec),
        out_shape=(jax.ShapeDtypeStruct((S, D), F32), jax.ShapeDtypeStruct((1, D), F32)),
        compiler_params=_cparams(("arbitrary",)),
    )(dh, x, w, dres)


def _loss_head(x3, w, tgt, name):
    S, D = x3.shape
    T = _pick_tile(S, 512)

    def body(x_ref, w_ref, t_ref, loss_ref, dx_ref, dw_ref):
        xv = x_ref[...]
        rs = lax.rsqrt(jnp.mean(xv * xv, axis=-1, keepdims=True) + EPS)
        xn = xv * rs
        err = xn * w_ref[...] - t_ref[...]
        dy = err * (1.0 / D)
        dxn = dy * w_ref[...]
        dx_ref[...] = rs * (dxn - xn * jnp.mean(dxn * xn, axis=-1, keepdims=True))

        @pl.when(pl.program_id(0) == 0)
        def _():
            dw_ref[...] = jnp.zeros_like(dw_ref)
            loss_ref[...] = jnp.zeros_like(loss_ref)

        dw_ref[...] += jnp.sum(dy * xn, axis=0, keepdims=True)
        part = jnp.sum(jnp.sum(err * err, axis=-1, keepdims=True), axis=0, keepdims=True) * (0.5 / D)
        loss_ref[...] += jnp.broadcast_to(part, loss_ref.shape)

    row = pl.BlockSpec((T, D), lambda i: (i, 0))
    vec = pl.BlockSpec((1, D), lambda i: (0, 0))
    return pl.pallas_call(
        body, name=name, grid=(S // T,), in_specs=[row, vec, row],
        out_specs=(pl.BlockSpec((8, 128), lambda i: (0, 0)), row, vec),
        out_shape=(jax.ShapeDtypeStruct((8, 128), F32), jax.ShapeDtypeStruct((S, D), F32), jax.ShapeDtypeStruct((1, D), F32)),
        compiler_params=_cparams(("arbitrary",)),
    )(x3, w, tgt)


def _conv_taps(ext, w, K, T):
    out = None
    for i in range(K):
        lo = 8 - (K - 1) + i
        term = ext[lo:lo + T, :] * w[i:i + 1, :]
        out = term if out is None else out + term
    return out


def _conv_taps_t(ext, w, K, T):
    out = None
    for i in range(K):
        lo = (K - 1) - i
        term = ext[lo:lo + T, :] * w[i:i + 1, :]
        out = term if out is None else out + term
    return out


def _tri_masks(C):
    r = lax.broadcasted_iota(jnp.int32, (C, C), 0)
    c = lax.broadcasted_iota(jnp.int32, (C, C), 1)
    return r == c, r >= c, r > c, r <= c


_NN, _NT, _TN = ((1,), (0,)), ((1,), (1,)), ((0,), (0,))
_GDN_PASSES = dict(qk=1, inv=1, sol=1, scan=1, bwd=1)


def _bdot_raw(a, b, kind, passes):
    dims = ({"NN": ((2,), (1,)), "NT": ((2,), (2,)), "TN": ((1,), (1,))}[kind], ((0,), (0,)))
    if passes == 0:
        return lax.dot_general(a, b, dims, precision=_HI, preferred_element_type=F32)
    ah, bh = a.astype(BF16), b.astype(BF16)
    out = lax.dot_general(ah, bh, dims, preferred_element_type=F32)
    if passes == 3:
        al, bl = (a - ah.astype(F32)).astype(BF16), (b - bh.astype(F32)).astype(BF16)
        out = out + lax.dot_general(ah, bl, dims, preferred_element_type=F32) + lax.dot_general(al, bh, dims, preferred_element_type=F32)
    return out


@functools.partial(jax.custom_vjp, nondiff_argnums=(2, 3))
def _bdot(a, b, kind, passes):
    return _bdot_raw(a, b, kind, passes)


def _bdot_fwd(a, b, kind, passes):
    return _bdot_raw(a, b, kind, passes), (a, b)


def _bdot_bwd(kind, passes, res, ct):
    a, b = res
    if kind == "NN":
        return _bdot_raw(ct, b, "NT", passes), _bdot_raw(a, ct, "TN", passes)
    if kind == "NT":
        return _bdot_raw(ct, b, "NN", passes), _bdot_raw(ct, a, "TN", passes)
    return _bdot_raw(b, ct, "NT", passes), _bdot_raw(a, ct, "NN", passes)


_bdot.defvjp(_bdot_fwd, _bdot_bwd)


def _softplus(x):
    return jnp.maximum(x, 0.0) + jnp.log(1.0 + jnp.exp(-jnp.abs(x)))


def _gdn_stage1(cq, ck, cv, b_col, a_col, alog, dtb, dot=_bdot_raw):
    C = cq.shape[1]
    eye, incl, strict, incl_t = _tri_masks(C)
    qn = cq * lax.rsqrt(jnp.sum(cq * cq, axis=-1, keepdims=True) + EPS) * (GDN_DIM ** -0.5)
    kn = ck * lax.rsqrt(jnp.sum(ck * ck, axis=-1, keepdims=True) + EPS)
    beta = jax.nn.sigmoid(b_col)
    g = -jnp.exp(alog) * _softplus(a_col + dtb)
    g_row = jnp.sum(jnp.where(eye, g, 0.0), axis=1, keepdims=True)
    beta_row = jnp.sum(jnp.where(eye, beta, 0.0), axis=1, keepdims=True)
    gc_col = jnp.sum(jnp.where(incl, g_row, 0.0), axis=2, keepdims=True)
    gc_row = jnp.sum(jnp.where(incl_t, g, 0.0), axis=1, keepdims=True)
    dec = jnp.where(incl, jnp.exp(jnp.where(incl, gc_col - gc_row, 0.0)), 0.0)
    kk = dot(kn, kn, "NT", _GDN_PASSES["qk"])
    qk = dot(qn, kn, "NT", _GDN_PASSES["qk"])
    lmat = jnp.where(strict, dec * kk * beta_row, 0.0)
    attn = dec * qk * beta_row
    gam = jnp.exp(gc_col)
    gc_last = gc_col[:, C - 1:C, :]
    k_end = kn * (jnp.exp(gc_last - gc_col) * beta)
    return lmat, cv, gam * kn, gam * qn, attn, k_end, jnp.exp(gc_last)


def _tri_inv(lmat):
    C = lmat.shape[1]
    eye = _tri_masks(C)[0]
    ps = _GDN_PASSES["inv"]
    p = jnp.where(eye, 1.0, 0.0) - lmat
    lp = _bdot_raw(lmat, lmat, "NN", ps)
    n = int(math.log2(C))
    for s in range(1, n):
        p = p + _bdot_raw(p, lp, "NN", ps)
        if s < n - 1:
            lp = _bdot_raw(lp, lp, "NN", ps)
    return p


def _gated_norm(o, z, gnw):
    on = o * lax.rsqrt(jnp.mean(o * o, axis=-1, keepdims=True) + EPS) * gnw
    return on * _silu(z)


GDN_PG = 2
GDN_SG = 4


def _gdn_pairs(c, ba, gp, G):
    C, W, H = GDN_CHUNK, GDN_WIDTH, GDN_HEADS
    pairs = [(j, h) for j in range(G) for h in range(H)]
    cq, ck, cv = (jnp.stack([c[C * j:C * (j + 1), o + GDN_DIM * h:o + GDN_DIM * (h + 1)] for j, h in pairs]) for o in (0, W, 2 * W))
    b_col = jnp.stack([ba[C * j:C * (j + 1), h:h + 1] for j, h in pairs])
    a_col = jnp.stack([ba[C * j:C * (j + 1), H + h:H + h + 1] for j, h in pairs])
    alog = jnp.stack([gp[0:1, h:h + 1] for j, h in pairs])
    dtb = jnp.stack([gp[0:1, H + h:H + h + 1] for j, h in pairs])
    return pairs, (cq, ck, cv, b_col, a_col, alog, dtb)


def _gdn_pre_specs(S, G):
    C = GDN_CHUNK
    T = C * G
    return dict(
        cur=pl.BlockSpec((T, 3 * GDN_WIDTH), lambda i: (i, 0)),
        prev=pl.BlockSpec((8, 3 * GDN_WIDTH), lambda i: (jnp.maximum(i * (T // 8) - 1, 0), 0)),
        ba=pl.BlockSpec((T, 128), lambda i: (i, P_BA // 128)),
        cw=pl.BlockSpec((GDN_CONV, 3 * GDN_WIDTH), lambda i: (0, 0)),
        vec=pl.BlockSpec((1, 128), lambda i: (0, 0)),
        hd=pl.BlockSpec((GDN_HEADS, T, GDN_DIM), lambda i: (0, i, 0)),
        hc=pl.BlockSpec((GDN_HEADS, T, C), lambda i: (0, i, 0)),
        ge=pl.BlockSpec((G, GDN_HEADS, 8, 128), lambda i: (i, 0, 0, 0)),
    )


def _hd_shape(S, last=GDN_DIM):
    return jax.ShapeDtypeStruct((GDN_HEADS, S, last), F32)


def _gdn_pre(proj, conv_w, gp):
    S = proj.shape[0]
    C, G = GDN_CHUNK, GDN_PG
    nc = S // C
    sp = _gdn_pre_specs(S, G)

    def body(cur_ref, prev_ref, ba_ref, cw_ref, gp_ref, uv_ref, wk_ref, qd_ref, ke_ref, at_ref, ti_ref, ge_ref):
        prev = prev_ref[...] * jnp.where(pl.program_id(0) == 0, 0.0, 1.0)
        c = _silu(_conv_taps(jnp.concatenate([prev, cur_ref[...]], axis=0), cw_ref[...], GDN_CONV, C * G))
        pairs, args = _gdn_pairs(c, ba_ref[...], gp_ref[...], G)
        lmat, v, rk, q_dec, attn, k_end, g_end = _gdn_stage1(*args)
        t = _tri_inv(lmat)
        u_v = _bdot_raw(t, v, "NN", _GDN_PASSES["sol"])
        w_k = _bdot_raw(t, rk, "NN", _GDN_PASSES["sol"])
        for b, (j, h) in enumerate(pairs):
            rows = slice(C * j, C * (j + 1))
            uv_ref[h, rows, :] = u_v[b]
            wk_ref[h, rows, :] = w_k[b]
            qd_ref[h, rows, :] = q_dec[b]
            ke_ref[h, rows, :] = k_end[b]
            at_ref[h, rows, :] = attn[b]
            ti_ref[h, rows, :] = t[b]
            ge_ref[j, h] = jnp.broadcast_to(g_end[b], (8, 128))

    return pl.pallas_call(
        body, name="gdn_pre", grid=(nc // G,),
        in_specs=[sp["cur"], sp["prev"], sp["ba"], sp["cw"], sp["vec"]],
        out_specs=(sp["hd"], sp["hd"], sp["hd"], sp["hd"], sp["hc"], sp["hc"], sp["ge"]),
        out_shape=(_hd_shape(S), _hd_shape(S), _hd_shape(S), _hd_shape(S), _hd_shape(S, C), _hd_shape(S, C),
                   jax.ShapeDtypeStruct((nc, GDN_HEADS, 8, 128), F32)),
        compiler_params=_cparams(("parallel",)),
    )(proj, proj, proj, conv_w, gp)


def _gdn_scan_specs(S, G, rev):
    C = GDN_CHUNK
    T = C * G
    n = S // T
    ci = (lambda i: n - 1 - i) if rev else (lambda i: i)
    return dict(
        hd=pl.BlockSpec((GDN_HEADS, T, GDN_DIM), lambda i: (0, ci(i), 0)),
        hc=pl.BlockSpec((GDN_HEADS, T, C), lambda i: (0, ci(i), 0)),
        ge=pl.BlockSpec((G, GDN_HEADS, 8, 128), lambda i: (ci(i), 0, 0, 0)),
        z=pl.BlockSpec((T, GDN_WIDTH), lambda i: (ci(i), P_Z // GDN_WIDTH)),
        oa=pl.BlockSpec((T, GDN_WIDTH), lambda i: (ci(i), 0)),
        vec=pl.BlockSpec((1, 128), lambda i: (0, 0)),
        st=pl.BlockSpec((G, GDN_HEADS, GDN_DIM, GDN_DIM), lambda i: (ci(i), 0, 0, 0)),
    )


def _gdn_scan(u_v, w_k, q_dec, k_end, attn, g_end, proj, gnw):
    S = proj.shape[0]
    C, G = GDN_CHUNK, GDN_SG
    nc = S // C
    sp = _gdn_scan_specs(S, G, False)
    ps = _GDN_PASSES["scan"]

    def body(uv_ref, wk_ref, qd_ref, ke_ref, at_ref, ge_ref, z_ref, gnw_ref, oa_ref, st_ref, s_scr):
        @pl.when(pl.program_id(0) == 0)
        def _():
            s_scr[...] = jnp.zeros_like(s_scr)

        for j in range(G):
            rows = slice(C * j, C * (j + 1))
            st = s_scr[...]
            st_ref[j] = st
            u = uv_ref[:, rows, :] - _bdot_raw(wk_ref[:, rows, :], st, "NN", ps)
            o = _bdot_raw(qd_ref[:, rows, :], st, "NN", ps) + _bdot_raw(at_ref[:, rows, :], u, "NN", ps)
            s_scr[...] = ge_ref[j][:, 0:1, 0:1] * st + _bdot_raw(ke_ref[:, rows, :], u, "TN", ps)
            for h in range(GDN_HEADS):
                cols = slice(GDN_DIM * h, GDN_DIM * (h + 1))
                oa_ref[rows, cols] = _gated_norm(o[h], z_ref[rows, cols], gnw_ref[...])

    return pl.pallas_call(
        body, name="gdn_scan", grid=(nc // G,),
        in_specs=[sp["hd"], sp["hd"], sp["hd"], sp["hd"], sp["hc"], sp["ge"], sp["z"], sp["vec"]],
        out_specs=(sp["oa"], sp["st"]),
        out_shape=(jax.ShapeDtypeStruct((S, GDN_WIDTH), F32),
                   jax.ShapeDtypeStruct((nc, GDN_HEADS, GDN_DIM, GDN_DIM), F32)),
        scratch_shapes=[pltpu.VMEM((GDN_HEADS, GDN_DIM, GDN_DIM), F32)],
        compiler_params=_cparams(("arbitrary",)),
    )(u_v, w_k, q_dec, k_end, attn, g_end, proj, gnw)


def _gdn_scan_bwd(u_v, w_k, q_dec, k_end, attn, g_end, proj, gnw, states, d_oa):
    S = proj.shape[0]
    C, G = GDN_CHUNK, GDN_SG
    nc = S // C
    sp = _gdn_scan_specs(S, G, True)
    ps, pb = _GDN_PASSES["scan"], _GDN_PASSES["bwd"]

    def body(uv_ref, wk_ref, qd_ref, ke_ref, at_ref, ge_ref, z_ref, gnw_ref, st_ref, doa_ref,
             duv_ref, dwk_ref, dqd_ref, dke_ref, dat_ref, dge_ref, dz_ref, dgnw_ref, ds_scr):
        @pl.when(pl.program_id(0) == 0)
        def _():
            ds_scr[...] = jnp.zeros_like(ds_scr)
            dgnw_ref[...] = jnp.zeros_like(dgnw_ref)

        dgnw = jnp.zeros((1, 128), F32)
        for j in reversed(range(G)):
            rows = slice(C * j, C * (j + 1))
            st = st_ref[j]
            wk, qd, ke, at = wk_ref[:, rows, :], qd_ref[:, rows, :], ke_ref[:, rows, :], at_ref[:, rows, :]
            u = uv_ref[:, rows, :] - _bdot_raw(wk, st, "NN", ps)
            o = _bdot_raw(qd, st, "NN", ps) + _bdot_raw(at, u, "NN", ps)
            dos = []
            for h in range(GDN_HEADS):
                cols = slice(GDN_DIM * h, GDN_DIM * (h + 1))
                _, vjp2 = jax.vjp(_gated_norm, o[h], z_ref[rows, cols], gnw_ref[...])
                do_h, dz_h, dgn = vjp2(doa_ref[rows, cols])
                dz_ref[rows, cols] = dz_h
                dgnw = dgnw + dgn
                dos.append(do_h)
            do = jnp.stack(dos)
            ds_new = ds_scr[...]
            du = _bdot_raw(at, do, "TN", pb) + _bdot_raw(ke, ds_new, "NN", pb)
            duv_ref[:, rows, :] = du
            dat_ref[:, rows, :] = _bdot_raw(do, u, "NT", pb)
            dqd_ref[:, rows, :] = _bdot_raw(do, st, "NT", pb)
            dke_ref[:, rows, :] = _bdot_raw(u, ds_new, "NT", pb)
            dwk_ref[:, rows, :] = -_bdot_raw(du, st, "NT", pb)
            d_ge = jnp.sum(jnp.sum(st * ds_new, axis=2, keepdims=True), axis=1, keepdims=True)
            dge_ref[j] = jnp.broadcast_to(d_ge, (GDN_HEADS, 8, 128))
            ds_scr[...] = ge_ref[j][:, 0:1, 0:1] * ds_new + _bdot_raw(qd, do, "TN", pb) - _bdot_raw(wk, du, "TN", pb)
        dgnw_ref[...] += dgnw

    return pl.pallas_call(
        body, name="gdn_scan_bwd", grid=(nc // G,),
        in_specs=[sp["hd"], sp["hd"], sp["hd"], sp["hd"], sp["hc"], sp["ge"], sp["z"], sp["vec"], sp["st"], sp["oa"]],
        out_specs=(sp["hd"], sp["hd"], sp["hd"], sp["hd"], sp["hc"], sp["ge"], sp["oa"], sp["vec"]),
        out_shape=(_hd_shape(S), _hd_shape(S), _hd_shape(S), _hd_shape(S), _hd_shape(S, C),
                   jax.ShapeDtypeStruct((nc, GDN_HEADS, 8, 128), F32), jax.ShapeDtypeStruct((S, GDN_WIDTH), F32),
                   jax.ShapeDtypeStruct((1, 128), F32)),
        scratch_shapes=[pltpu.VMEM((GDN_HEADS, GDN_DIM, GDN_DIM), F32)],
        compiler_params=_cparams(("arbitrary",)),
    )(u_v, w_k, q_dec, k_end, attn, g_end, proj, gnw, states, d_oa)


def _gdn_post(proj, conv_w, gp, tinv, u_v, w_k, d_uv, d_wk, d_qd, d_ke, d_at, d_ge):
    S = proj.shape[0]
    C, G = GDN_CHUNK, GDN_PG
    nc = S // C
    sp = _gdn_pre_specs(S, G)
    pb = _GDN_PASSES["bwd"]

    def body(cur_ref, prev_ref, ba_ref, cw_ref, gp_ref, ti_ref, uv_ref, wk_ref, duv_ref, dwk_ref, dqd_ref, dke_ref,
             dat_ref, dge_ref, dpre_ref, dba_ref, dgp_ref):
        i = pl.program_id(0)

        @pl.when(i == 0)
        def _():
            dgp_ref[...] = jnp.zeros_like(dgp_ref)

        prev = prev_ref[...] * jnp.where(i == 0, 0.0, 1.0)
        pre = _conv_taps(jnp.concatenate([prev, cur_ref[...]], axis=0), cw_ref[...], GDN_CONV, C * G)
        sg = jax.nn.sigmoid(pre)
        dsilu = sg * (1.0 + pre * (1.0 - sg))
        pairs, args = _gdn_pairs(pre * sg, ba_ref[...], gp_ref[...], G)
        _, vjp1 = jax.vjp(functools.partial(_gdn_stage1, dot=_bdot), *args)

        def take(ref):
            return jnp.stack([ref[h, C * j:C * (j + 1), :] for j, h in pairs])

        t, u_v, w_k = take(ti_ref), take(uv_ref), take(wk_ref)
        d_v = _bdot_raw(t, take(duv_ref), "TN", pb)
        d_rk = _bdot_raw(t, take(dwk_ref), "TN", pb)
        d_l = -(_bdot_raw(d_v, u_v, "NT", pb) + _bdot_raw(d_rk, w_k, "NT", pb))
        d_ge = jnp.stack([dge_ref[j, h][0:1, 0:1] for j, h in pairs])
        dcq, dck, dcv, db, da, dalog, ddtb = vjp1((d_l, d_v, d_rk, take(dqd_ref), take(dat_ref), take(dke_ref), d_ge))
        lane = lax.broadcasted_iota(jnp.int32, (C, 128), 1)
        lane1 = lax.broadcasted_iota(jnp.int32, (1, 128), 1)
        dgp = jnp.zeros((1, 128), F32)
        for j in range(G):
            rows = slice(C * j, C * (j + 1))
            dba = jnp.zeros((C, 128), F32)
            for h in range(GDN_HEADS):
                b = GDN_HEADS * j + h
                for o_, dcx in ((0, dcq), (GDN_WIDTH, dck), (2 * GDN_WIDTH, dcv)):
                    cols = slice(o_ + GDN_DIM * h, o_ + GDN_DIM * (h + 1))
                    dpre_ref[rows, cols] = dcx[b] * dsilu[rows, cols]
                dba = dba + jnp.where(lane == h, db[b], 0.0) + jnp.where(lane == GDN_HEADS + h, da[b], 0.0)
                dgp = dgp + jnp.where(lane1 == h, dalog[b], 0.0) + jnp.where(lane1 == GDN_HEADS + h, ddtb[b], 0.0)
            dba_ref[rows, :] = dba
        dgp_ref[0:1, :] += dgp

    T = C * G
    return pl.pallas_call(
        body, name="gdn_post", grid=(nc // G,),
        in_specs=[sp["cur"], sp["prev"], sp["ba"], sp["cw"], sp["vec"], sp["hc"], sp["hd"], sp["hd"], sp["hd"], sp["hd"],
                  sp["hd"], sp["hd"], sp["hc"], sp["ge"]],
        out_specs=(sp["cur"], pl.BlockSpec((T, 128), lambda i: (i, 0)), pl.BlockSpec((8, 128), lambda i: (0, 0))),
        out_shape=(jax.ShapeDtypeStruct((S, 3 * GDN_WIDTH), F32), jax.ShapeDtypeStruct((S, 128), F32),
                   jax.ShapeDtypeStruct((8, 128), F32)),
        compiler_params=_cparams(("arbitrary",)),
    )(proj, proj, proj, conv_w, gp, tinv, u_v, w_k, d_uv, d_wk, d_qd, d_ke, d_at, d_ge)


def _conv_bwd(dpre, x, xcol0, w, K, name, tc):
    S, Cc = dpre.shape
    T = _pick_tile(S, 256)
    nt, ncol = S // T, Cc // tc
    xo = xcol0 // tc

    def body(d_ref, dn_ref, x_ref, xp_ref, w_ref, dx_ref, dw_ref):
        i = pl.program_id(1)
        dn = dn_ref[...] * jnp.where(i == nt - 1, 0.0, 1.0)
        dv = d_ref[...]
        ext_d = jnp.concatenate([dv, dn], axis=0)
        wv = w_ref[...]
        dx_ref[...] = _conv_taps_t(ext_d, wv, K, T).astype(dx_ref.dtype)
        xp = xp_ref[...] * jnp.where(i == 0, 0.0, 1.0)
        ext_x = jnp.concatenate([xp, x_ref[...]], axis=0)

        @pl.when(i == 0)
        def _():
            dw_ref[...] = jnp.zeros_like(dw_ref)

        for k in range(K):
            lo = 8 - (K - 1) + k
            dw_ref[k:k + 1, :] += jnp.sum(dv * ext_x[lo:lo + T, :], axis=0, keepdims=True)

    r8 = T // 8
    return pl.pallas_call(
        body, name=name, grid=(ncol, nt),
        in_specs=[pl.BlockSpec((T, tc), lambda j, i: (i, j)),
                  pl.BlockSpec((8, tc), lambda j, i: (jnp.minimum((i + 1) * r8, S // 8 - 1), j)),
                  pl.BlockSpec((T, tc), lambda j, i: (i, j + xo)),
                  pl.BlockSpec((8, tc), lambda j, i: (jnp.maximum(i * r8 - 1, 0), j + xo)),
                  pl.BlockSpec((K, tc), lambda j, i: (0, j))],
        out_specs=(pl.BlockSpec((T, tc), lambda j, i: (i, j)), pl.BlockSpec((K, tc), lambda j, i: (0, j))),
        out_shape=(jax.ShapeDtypeStruct((S, Cc), _MXU), jax.ShapeDtypeStruct((K, Cc), F32)),
        compiler_params=_cparams(("parallel", "arbitrary")),
    )(dpre, dpre, x, x, w)


def _dil_bias(nt):
    T = ATT_T
    d = (np.arange(nt)[:, None, None] * T + np.arange(T)[None, :, None] - np.arange(T)[None, None, :])
    cnt = ((d >= 0) & (d <= 128)).astype(np.float64) + ((d >= 0) & (d % 4 == 0) & (d <= 512)) + ((d >= 0) & (d % 16 == 0))
    return jnp.asarray(np.where(cnt > 0, np.log(np.maximum(cnt, 1.0)), -1e30), dtype=F32)


def _attn_fwd(proj):
    S = proj.shape[0]
    T = ATT_T
    nt = S // T
    bias = _dil_bias(nt)
    scale = DIL_DIM ** -0.5
    npair = DIL_WIDTH // 128
    qb0, kb0, vb0 = P_QKVB // 128, (P_QKVB + DIL_WIDTH) // 128, (P_QKVB + 2 * DIL_WIDTH) // 128

    def body(q_ref, k_ref, v_ref, b_ref, o_ref, lse_ref):
        i = pl.program_id(1)
        qs = (q_ref[...] * scale).astype(_MXU)

        def step(j, carry):
            kt = k_ref[pl.ds(pl.multiple_of(j * T, T), T), :].astype(_MXU)
            vt = v_ref[pl.ds(pl.multiple_of(j * T, T), T), :].astype(_MXU)
            bt = b_ref[i - j]
            out = []
            for hh in range(2):
                m, l, acc = carry[hh]
                sl = slice(hh * DIL_DIM, (hh + 1) * DIL_DIM)
                s = lax.dot_general(qs[:, sl], kt[:, sl], (_NT, ((), ())), preferred_element_type=F32) + bt
                m_new = jnp.maximum(m, jnp.max(s, axis=-1, keepdims=True))
                p = jnp.exp(s - m_new)
                a = jnp.exp(m - m_new)
                l = a * l + jnp.sum(p, axis=-1, keepdims=True)
                acc = a * acc + lax.dot_general(p.astype(_MXU), vt[:, sl], (_NN, ((), ())), preferred_element_type=F32)
                out.append((m_new, l, acc))
            return tuple(out)

        init = tuple((jnp.full((T, 1), -1e30, F32), jnp.zeros((T, 1), F32), jnp.zeros((T, DIL_DIM), F32)) for _ in range(2))
        res = lax.fori_loop(0, i + 1, step, init)
        for hh in range(2):
            m, l, acc = res[hh]
            sl = slice(hh * DIL_DIM, (hh + 1) * DIL_DIM)
            o_ref[:, sl] = acc / l
            lse_ref[:, sl] = jnp.broadcast_to(m + jnp.log(l), (T, DIL_DIM))

    return pl.pallas_call(
        body, name="attn_fwd", grid=(npair, nt),
        in_specs=[pl.BlockSpec((T, 128), lambda p, i: (i, qb0 + p)),
                  pl.BlockSpec((S, 128), lambda p, i: (0, kb0 + p)),
                  pl.BlockSpec((S, 128), lambda p, i: (0, vb0 + p)),
                  pl.BlockSpec((nt, T, T), lambda p, i: (0, 0, 0))],
        out_specs=(pl.BlockSpec((T, 128), lambda p, i: (i, p)), pl.BlockSpec((T, 128), lambda p, i: (i, p))),
        out_shape=(jax.ShapeDtypeStruct((S, DIL_WIDTH), F32), jax.ShapeDtypeStruct((S, DIL_WIDTH), F32)),
        compiler_params=_cparams(("parallel", "parallel")),
    )(proj, proj, proj, bias)


def _attn_bwd(proj, o_b, lse, d_ob):
    S = proj.shape[0]
    T = ATT_T
    nt = S // T
    bias = _dil_bias(nt)
    scale = DIL_DIM ** -0.5
    npair = DIL_WIDTH // 128
    qb0, kb0, vb0 = P_QKVB // 128, (P_QKVB + DIL_WIDTH) // 128, (P_QKVB + 2 * DIL_WIDTH) // 128

    def body(q_ref, k_ref, v_ref, o_ref, lse_ref, do_ref, b_ref, dq_ref, dk_ref, dv_ref, dq_scr):
        j = pl.program_id(1)

        @pl.when(j == 0)
        def _():
            dq_scr[...] = jnp.zeros_like(dq_scr)

        kt = k_ref[...].astype(_MXU)
        vt = v_ref[...].astype(_MXU)

        def step(i, carry):
            rows = pl.ds(pl.multiple_of(i * T, T), T)
            qs = (q_ref[rows, :] * scale).astype(_MXU)
            dov = do_ref[rows, :]
            prod = dov * o_ref[rows, :]
            lsev = lse_ref[rows, :]
            dob = dov.astype(_MXU)
            bt = b_ref[i - j]
            out = []
            dqs = []
            for hh in range(2):
                dk, dv = carry[hh]
                sl = slice(hh * DIL_DIM, (hh + 1) * DIL_DIM)
                s = lax.dot_general(qs[:, sl], kt[:, sl], (_NT, ((), ())), preferred_element_type=F32) + bt
                p = jnp.exp(s - lsev[:, hh * DIL_DIM:hh * DIL_DIM + 1])
                delta = jnp.sum(prod[:, sl], axis=-1, keepdims=True)
                dp = lax.dot_general(dob[:, sl], vt[:, sl], (_NT, ((), ())), preferred_element_type=F32)
                ds = (p * (dp - delta)).astype(_MXU)
                dv = dv + lax.dot_general(p.astype(_MXU), dob[:, sl], (_TN, ((), ())), preferred_element_type=F32)
                dk = dk + lax.dot_general(ds, qs[:, sl], (_TN, ((), ())), preferred_element_type=F32)
                dqs.append(lax.dot_general(ds, kt[:, sl], (_NN, ((), ())), preferred_element_type=F32) * scale)
                out.append((dk, dv))
            dq_scr[rows, :] += jnp.concatenate(dqs, axis=1)
            return tuple(out)

        init = tuple((jnp.zeros((T, DIL_DIM), F32), jnp.zeros((T, DIL_DIM), F32)) for _ in range(2))
        res = lax.fori_loop(j, nt, step, init)
        dk_ref[...] = jnp.concatenate([res[0][0], res[1][0]], axis=1).astype(dk_ref.dtype)
        dv_ref[...] = jnp.concatenate([res[0][1], res[1][1]], axis=1).astype(dv_ref.dtype)

        @pl.when(j == nt - 1)
        def _():
            dq_ref[...] = dq_scr[...].astype(dq_ref.dtype)

    full = lambda c0: pl.BlockSpec((S, 128), lambda p, j: (0, c0 + p))
    tile = lambda c0: pl.BlockSpec((T, 128), lambda p, j: (j, c0 + p))
    out3 = jax.ShapeDtypeStruct((S, DIL_WIDTH), _MXU)
    return pl.pallas_call(
        body, name="attn_bwd", grid=(npair, nt),
        in_specs=[full(qb0), tile(kb0), tile(vb0), full(0), full(0), full(0),
                  pl.BlockSpec((nt, T, T), lambda p, j: (0, 0, 0))],
        out_specs=(full(0), tile(0), tile(0)),
        out_shape=(out3, out3, out3),
        scratch_shapes=[pltpu.VMEM((S, 128), F32)],
        compiler_params=_cparams(("parallel", "arbitrary")),
    )(proj, proj, proj, o_b, lse, d_ob, bias)


def _ffn_act(up, cw):
    S, Cc = up.shape[0], up.shape[1] // 2
    T, tc = _pick_tile(S, 256), _pick_tile(Cc, 1536)
    r8 = T // 8
    nct = Cc // tc

    def body(g_ref, gp_ref, u_ref, up_ref, wg_ref, wu_ref, o_ref):
        keep = jnp.where(pl.program_id(1) == 0, 0.0, 1.0)
        cg = _conv_taps(jnp.concatenate([gp_ref[...] * keep, g_ref[...]], axis=0), wg_ref[...], FFN_CONV, T)
        cu = _conv_taps(jnp.concatenate([up_ref[...] * keep, u_ref[...]], axis=0), wu_ref[...], FFN_CONV, T)
        o_ref[...] = (_silu(cg) * cu).astype(o_ref.dtype)

    cur = lambda o: pl.BlockSpec((T, tc), lambda j, i: (i, j + o))
    prev = lambda o: pl.BlockSpec((8, tc), lambda j, i: (jnp.maximum(i * r8 - 1, 0), j + o))
    wsp = lambda o: pl.BlockSpec((FFN_CONV, tc), lambda j, i: (0, j + o))
    return pl.pallas_call(
        body, name="ffn_act", grid=(nct, S // T),
        in_specs=[cur(0), prev(0), cur(nct), prev(nct), wsp(0), wsp(nct)], out_specs=cur(0),
        out_shape=jax.ShapeDtypeStruct((S, Cc), _MXU),
        compiler_params=_cparams(("parallel", "parallel")),
    )(up, up, up, up, cw, cw)


def _ffn_act_bwd(d_act, up, cw):
    S, Cc = up.shape[0], up.shape[1] // 2
    T, tc = _pick_tile(S, 256), _pick_tile(Cc, 1536)
    r8 = T // 8
    nt = S // T
    nct = Cc // tc
    K = FFN_CONV

    def body(da_ref, dan_ref, g_ref, gp_ref, gn_ref, u_ref, up_ref, un_ref, wg_ref, wu_ref,
             dg_ref, du_ref, dwg_ref, dwu_ref):
        i = pl.program_id(1)
        keep_p = jnp.where(i == 0, 0.0, 1.0)
        keep_n = jnp.where(i == nt - 1, 0.0, 1.0)
        wg, wu = wg_ref[...], wu_ref[...]
        xg = jnp.concatenate([gp_ref[...] * keep_p, g_ref[...], gn_ref[...] * keep_n], axis=0)
        xu = jnp.concatenate([up_ref[...] * keep_p, u_ref[...], un_ref[...] * keep_n], axis=0)
        cg = _conv_taps(xg, wg, K, T + 8)
        cu = _conv_taps(xu, wu, K, T + 8)
        da = jnp.concatenate([da_ref[...], dan_ref[...] * keep_n], axis=0)
        sg = jax.nn.sigmoid(cg)
        d_cg = da * cu * (sg * (1.0 + cg * (1.0 - sg)))
        d_cu = da * (cg * sg)
        dg_ref[...] = _conv_taps_t(d_cg, wg, K, T).astype(dg_ref.dtype)
        du_ref[...] = _conv_taps_t(d_cu, wu, K, T).astype(du_ref.dtype)

        @pl.when(i == 0)
        def _():
            dwg_ref[...] = jnp.zeros_like(dwg_ref)
            dwu_ref[...] = jnp.zeros_like(dwu_ref)

        for k in range(K):
            lo = 8 - (K - 1) + k
            dwg_ref[k:k + 1, :] += jnp.sum(d_cg[0:T, :] * xg[lo:lo + T, :], axis=0, keepdims=True)
            dwu_ref[k:k + 1, :] += jnp.sum(d_cu[0:T, :] * xu[lo:lo + T, :], axis=0, keepdims=True)

    cur = lambda o: pl.BlockSpec((T, tc), lambda j, i: (i, j + o))
    prev = lambda o: pl.BlockSpec((8, tc), lambda j, i: (jnp.maximum(i * r8 - 1, 0), j + o))
    nxt = lambda o: pl.BlockSpec((8, tc), lambda j, i: (jnp.minimum((i + 1) * r8, S // 8 - 1), j + o))
    wsp = lambda o: pl.BlockSpec((K, tc), lambda j, i: (0, j + o))
    return pl.pallas_call(
        body, name="ffn_act_bwd", grid=(nct, nt),
        in_specs=[cur(0), nxt(0), cur(0), prev(0), nxt(0), cur(nct), prev(nct), nxt(nct), wsp(0), wsp(nct)],
        out_specs=(cur(0), cur(0), wsp(0), wsp(0)),
        out_shape=(jax.ShapeDtypeStruct((S, Cc), _MXU), jax.ShapeDtypeStruct((S, Cc), _MXU),
                   jax.ShapeDtypeStruct((K, Cc), F32), jax.ShapeDtypeStruct((K, Cc), F32)),
        compiler_params=_cparams(("parallel", "arbitrary")),
    )(d_act, d_act, up, up, up, up, up, up, cw, cw)


def _local_step(x, tgt, n1w, n2w, fnw, gp, gnw, wp, conv_w, fcw, rest_weights):
    h1 = _rmsnorm_fwd(x, n1w, "norm1")
    proj = _mm(h1, wp, "nn", name="proj")
    u_v, w_k, q_dec, k_end, attn, tinv, g_end = _gdn_pre(proj, conv_w, gp)
    o_a, states = _gdn_scan(u_v, w_k, q_dec, k_end, attn, g_end, proj, gnw)
    o_b, lse = _attn_fwd(proj)
    w_out, w_up4, w_down = rest_weights([o_a, o_b])
    x2 = _mm(o_b, w_out[GDN_WIDTH:], "nn", residual=_mm(o_a, w_out[:GDN_WIDTH], "nn", residual=x, name="outproj_a"),
             name="outproj_b")
    h2 = _rmsnorm_fwd(x2, n2w, "norm2")
    up = _mm(h2, w_up4, "nn", b_blocks=True, name="up")
    act = _ffn_act(up, fcw)
    x3 = _mm(act, w_down, "nn", residual=x2, name="down")
    loss, dx3, d_fnw = _loss_head(x3, fnw, tgt, "loss_head")
    d_act = _mm(dx3, w_down, "nt", name="d_act")
    d_wdown = _mm(act, dx3, "tn", name="d_wdown")
    d_upg, d_upu, d_fcwg, d_fcwu = _ffn_act_bwd(d_act, up, fcw)
    d_wup = _mm(h2, d_upg, "tn", place=("blocks", N_CHIPS, 0), tn=w_up4.shape[2], name="d_wgate")
    d_wup = _mm(h2, d_upu, "tn", place=("blocks", N_CHIPS, N_CHIPS // 2), tn=w_up4.shape[2], into=d_wup, name="d_wup")
    d_h2 = _mm_nt_blocks([d_upg, d_upu], w_up4, "d_h2")
    dx2, d_n2w = _rmsnorm_bwd(d_h2, x2, n2w, dx3, "norm2_bwd")
    d_oa = _mm(dx2, w_out[:GDN_WIDTH], "nt", name="d_oa")
    d_ob = _mm(dx2, w_out[GDN_WIDTH:], "nt", name="d_ob")
    d_wout = _mm(o_a, dx2, "tn", place=("rows", D_MODEL, 0), name="d_wout_a")
    d_wout = _mm(o_b, dx2, "tn", place=("rows", D_MODEL, GDN_WIDTH), into=d_wout, name="d_wout_b")
    dq_b, dk_b, dv_b = _attn_bwd(proj, o_b, lse, d_ob)
    d_uv, d_wk, d_qd, d_ke, d_at, d_ge, d_z, d_gnw = _gdn_scan_bwd(u_v, w_k, q_dec, k_end, attn, g_end, proj, gnw, states, d_oa)
    d_pre, d_ba, d_gp = _gdn_post(proj, conv_w, gp, tinv, u_v, w_k, d_uv, d_wk, d_qd, d_ke, d_at, d_ge)
    d_qkva, d_convw = _conv_bwd(d_pre, proj, 0, conv_w, GDN_CONV, "gdn_conv_bwd", 512)
    d_proj = jnp.concatenate([d_qkva, d_z.astype(_MXU), dq_b, dk_b, dv_b, d_ba.astype(_MXU),
                              jnp.zeros((x.shape[0], P_COLS - P_BA - 128), _MXU)], axis=1)
    d_wp = _mm(h1, d_proj, "tn", name="d_wp")
    d_h1 = _mm(d_proj, wp, "nt", name="d_h1")
    dx, d_n1w = _rmsnorm_bwd(d_h1, x, n1w, dx2, "norm1_bwd")
    grads = dict(wp=d_wp, conv_w=d_convw, w_out=d_wout, w_up=d_wup, fcw_g=d_fcwg, fcw_u=d_fcwu, w_down=d_wdown,
                 n1w=d_n1w, n2w=d_n2w, fnw=d_fnw, gp=d_gp, gnw=d_gnw)
    return loss, dx, grads


_HBM = pl.BlockSpec(memory_space=pltpu.HBM)


def _pos():
    return lax.axis_index("x"), lax.axis_index("y"), lax.axis_index("c")


def _other_chips(x, y):
    return [(1 - x, y), (x, 1 - y), (1 - x, 1 - y)]


def _halvable(shape):
    return shape[0] % 32 == 0


def _rows_of_half(shape, half):
    if not _halvable(shape):
        return pl.ds(0, shape[0])
    return pl.ds(pl.multiple_of(half * (shape[0] // 2), 16), shape[0] // 2)


def _gather_halves(shards, name):
    n = len(shards)
    shapes = [s.shape for s in shards]

    def body(*refs):
        ins, outs = refs[:n], refs[n:2 * n]
        send_sems, recv_sems = refs[2 * n:]
        x, y, c = _pos()
        q = 2 * x + y
        chips = _other_chips(x, y)

        def copy(a, j, block):
            px, py = chips[j]
            rows = _rows_of_half(shapes[a], c)
            return pltpu.make_async_remote_copy(
                src_ref=ins[a].at[rows, :], dst_ref=outs[a].at[block, rows, :], send_sem=send_sems.at[3 * a + j],
                recv_sem=recv_sems.at[3 * a + j], device_id=(px, py, c), device_id_type=MESH)

        sends = [copy(a, j, q) for a in range(n) for j in range(3)]
        for cp in sends:
            cp.start()
        for a in range(n):
            for j, (px, py) in enumerate(chips):
                copy(a, j, 2 * px + py).wait_recv()
        for cp in sends:
            cp.wait_send()

    return pl.pallas_call(
        body, name=name, in_specs=[_HBM] * n, out_specs=[_HBM] * n,
        out_shape=[jax.ShapeDtypeStruct((N_CHIPS,) + s.shape, s.dtype) for s in shards],
        scratch_shapes=[pltpu.SemaphoreType.DMA((3 * n,)), pltpu.SemaphoreType.DMA((3 * n,))],
    )(*shards)


_SEM = pl.BlockSpec(memory_space=pltpu.SEMAPHORE)
_ANY = pl.BlockSpec(memory_space=pl.ANY)
_DATAFLOW = pltpu.SideEffectType.DATAFLOW_SIDE_EFFECTING


def _in_hbm(a):
    return pltpu.with_memory_space_constraint(a, pltpu.HBM)


def _halves_copy(src_refs, land_refs, send_sems, recv_sems, shapes, a, j, block, x, y, c):
    px, py = _other_chips(x, y)[j]
    rows = _rows_of_half(shapes[a], c)
    return pltpu.make_async_remote_copy(
        src_ref=src_refs[a].at[rows, :], dst_ref=land_refs[a].at[block, rows, :], send_sem=send_sems.at[3 * a + j],
        recv_sem=recv_sems.at[3 * a + j], device_id=(px, py, c), device_id_type=MESH)


def _gather_halves_start(shards, after, name):
    n = len(shards)
    shapes = [s.shape for s in shards]

    def body(*refs):
        ins, lands = refs[:n], refs[n:2 * n]
        send_sems, recv_sems = refs[2 * n + 1], refs[2 * n + 2]
        token = refs[-1]
        x, y, c = _pos()
        q = 2 * x + y
        for a in range(n):
            for j in range(3):
                _halves_copy(ins, lands, send_sems, recv_sems, shapes, a, j, q, x, y, c).start()
        token[...] = jnp.zeros_like(token)

    land_shapes = [(N_CHIPS,) + s.shape for s in shards]
    return pl.pallas_call(
        body, name=name,
        out_shape=(pltpu.SemaphoreType.DMA((3 * n,)), pltpu.SemaphoreType.DMA((3 * n,)),
                   *[pltpu.HBM(s.shape, s.dtype) for s in shards],
                   *[pltpu.HBM(ls, s.dtype) for ls, s in zip(land_shapes, shards)],
                   jax.ShapeDtypeStruct((8, 128), F32)),
        in_specs=[_HBM] * (2 * n) + [_ANY],
        out_specs=(_SEM, _SEM, *[_HBM] * (2 * n), pl.BlockSpec(memory_space=pltpu.VMEM)),
        input_output_aliases={a: 2 + a for a in range(2 * n)},
        compiler_params=pltpu.CompilerParams(has_side_effects=_DATAFLOW),
    )(*[_in_hbm(s) for s in shards], *[_in_hbm(lax.empty(ls, s.dtype)) for ls, s in zip(land_shapes, shards)], after)


def _gather_halves_wait(started, after, name):
    send_sems, recv_sems, *thru = started
    n = len(thru) // 2
    shapes = [t.shape for t in thru[:n]]

    def body(*refs):
        ins, lands = refs[:n], refs[n:2 * n]
        send_sems, recv_sems = refs[2 * n], refs[2 * n + 1]
        x, y, c = _pos()
        q = 2 * x + y
        chips = _other_chips(x, y)
        for a in range(n):
            for j, (px, py) in enumerate(chips):
                _halves_copy(ins, lands, send_sems, recv_sems, shapes, a, j, q, x, y, c).wait_send()
                _halves_copy(ins, lands, send_sems, recv_sems, shapes, a, j, 2 * px + py, x, y, c).wait_recv()

    outs = pl.pallas_call(
        body, name=name, out_shape=[pltpu.HBM(t.shape, t.dtype) for t in thru],
        in_specs=[_HBM] * (2 * n) + [_SEM, _SEM] + [_ANY] * len(after), out_specs=[_HBM] * (2 * n),
        input_output_aliases={a: a for a in range(2 * n)},
        compiler_params=pltpu.CompilerParams(has_side_effects=_DATAFLOW),
    )(*thru, send_sems, recv_sems, *after)
    return outs[n:]


def _sibling_fill(gathered, name):
    big = [a for a, g in enumerate(gathered) if _halvable(g.shape[1:])]
    n = len(gathered)

    def body(*refs):
        ins, outs = refs[:n], refs[n:2 * n]
        send_sems, recv_sems = refs[2 * n:]
        x, y, c = _pos()
        chips = _other_chips(x, y)

        def copy(k, j, half):
            a = big[k]
            px, py = chips[j]
            rows = _rows_of_half(gathered[a].shape[1:], half)
            return pltpu.make_async_remote_copy(
                src_ref=ins[a].at[2 * px + py, rows, :], dst_ref=outs[a].at[2 * px + py, rows, :],
                send_sem=send_sems.at[3 * k + j], recv_sem=recv_sems.at[3 * k + j],
                device_id=(x, y, 1 - c), device_id_type=MESH)

        sends = [copy(k, j, c) for k in range(len(big)) for j in range(3)]
        for cp in sends:
            cp.start()
        for k in range(len(big)):
            for j in range(3):
                copy(k, j, 1 - c).wait_recv()
        for cp in sends:
            cp.wait_send()

    return pl.pallas_call(
        body, name=name, in_specs=[_HBM] * n, out_specs=[_HBM] * n,
        out_shape=[jax.ShapeDtypeStruct(g.shape, g.dtype) for g in gathered],
        input_output_aliases={a: a for a in range(n)},
        scratch_shapes=[pltpu.SemaphoreType.DMA((3 * len(big),)), pltpu.SemaphoreType.DMA((3 * len(big),))],
    )(*gathered)


def _place_own(shards, gathered, cq, name):
    n = len(shards)
    steps = 4

    def body(cq_ref, *refs):
        for a in range(n):
            refs[2 * n + a][...] = refs[a][...]

    def tile(shape):
        return shape[0] // steps if _halvable(shape) else shape[0]

    in_specs = [pl.BlockSpec((tile(s.shape), s.shape[1]), (lambda i, s_: (i, 0)) if _halvable(s.shape) else (lambda i, s_: (0, 0)))
                for s in shards]
    in_specs += [pl.BlockSpec(memory_space=pl.ANY)] * n
    out_specs = [pl.BlockSpec((None, tile(s.shape), s.shape[1]),
                              (lambda i, s_: (s_[1], i, 0)) if _halvable(s.shape) else (lambda i, s_: (s_[1], 0, 0)))
                 for s in shards]
    gs = pltpu.PrefetchScalarGridSpec(num_scalar_prefetch=1, grid=(steps,), in_specs=in_specs, out_specs=out_specs)
    return pl.pallas_call(
        body, name=name, grid_spec=gs, out_shape=[jax.ShapeDtypeStruct(g.shape, g.dtype) for g in gathered],
        input_output_aliases={1 + n + a: a for a in range(n)},
        compiler_params=_cparams(("arbitrary",)),
    )(cq, *shards, *gathered)


def _half_rows(ref, c, rh):
    return ref.at[:, pl.ds(pl.multiple_of(c * rh, 8), rh), :]


def _grad_sibling(fams, small):
    n = len(fams)
    rhs = [f.shape[1] // 2 for f in fams]

    def body(*refs):
        ins, small_ref = refs[:n], refs[n]
        outs, all_ref = refs[n + 1:2 * n + 1], refs[2 * n + 1]
        send_sems, recv_sems, loc_sem = refs[2 * n + 2:]
        x, y, c = _pos()
        me = 4 * x + 2 * y + c
        mine = pltpu.make_async_copy(small_ref, all_ref.at[me], loc_sem)
        mine.start()
        bigs = [pltpu.make_async_remote_copy(src_ref=_half_rows(ins[a], 1 - c, rhs[a]), dst_ref=outs[a],
                                             send_sem=send_sems.at[7 + a], recv_sem=recv_sems.at[7 + a],
                                             device_id=(x, y, 1 - c), device_id_type=MESH) for a in range(n)]
        for cp in bigs:
            cp.start()

        def peer(r):
            dx, dy, dc = (r >> 2) & 1, (r >> 1) & 1, r & 1
            px = x if dx == 0 else 1 - x
            py = y if dy == 0 else 1 - y
            pc = c if dc == 0 else 1 - c
            return px, py, pc

        def small_copy(r, slot):
            return pltpu.make_async_remote_copy(src_ref=small_ref, dst_ref=all_ref.at[slot], send_sem=send_sems.at[r - 1],
                                                recv_sem=recv_sems.at[r - 1], device_id=peer(r), device_id_type=MESH)

        sends = [small_copy(r, me) for r in range(1, 8)]
        for cp in sends:
            cp.start()
        for r in range(1, 8):
            px, py, pc = peer(r)
            small_copy(r, 4 * px + 2 * py + pc).wait_recv()
        for cp in bigs:
            cp.wait_recv()
        for cp in bigs + sends:
            cp.wait_send()
        mine.wait()

    return pl.pallas_call(
        body, name="grad_sibling", in_specs=[_HBM] * (n + 1), out_specs=[_HBM] * (n + 1),
        out_shape=[jax.ShapeDtypeStruct((f.shape[0], f.shape[1] // 2, f.shape[2]), f.dtype) for f in fams]
        + [jax.ShapeDtypeStruct((8,) + small.shape, small.dtype)],
        scratch_shapes=[pltpu.SemaphoreType.DMA((7 + n,)), pltpu.SemaphoreType.DMA((7 + n,)), pltpu.SemaphoreType.DMA],
    )(*fams, small)


def _grad_chips(parts):
    n = len(parts)

    def body(*refs):
        ins, outs = refs[:n], refs[n:2 * n]
        send_sems, recv_sems = refs[2 * n:]
        x, y, c = _pos()
        chips = _other_chips(x, y)

        def copy(a, j):
            px, py = chips[j]
            return pltpu.make_async_remote_copy(src_ref=ins[a].at[2 * px + py], dst_ref=outs[a].at[j],
                                                send_sem=send_sems.at[3 * a + j], recv_sem=recv_sems.at[3 * a + j],
                                                device_id=(px, py, c), device_id_type=MESH)

        sends = [copy(a, j) for a in range(n) for j in range(3)]
        for cp in sends:
            cp.start()
        for a in range(n):
            for j in range(3):
                copy(a, j).wait_recv()
        for cp in sends:
            cp.wait_send()

    return pl.pallas_call(
        body, name="grad_chips", in_specs=[_HBM] * n, out_specs=[_HBM] * n,
        out_shape=[jax.ShapeDtypeStruct((3,) + p.shape[1:], p.dtype) for p in parts],
        scratch_shapes=[pltpu.SemaphoreType.DMA((3 * n,)), pltpu.SemaphoreType.DMA((3 * n,))],
    )(*parts)


def _grad_share(fulls):
    n = len(fulls)
    rhs = [f.shape[0] // 2 for f in fulls]

    def body(*refs):
        ins, outs = refs[:n], refs[n:2 * n]
        send_sems, recv_sems = refs[2 * n:]
        x, y, c = _pos()

        def copy(a, half):
            rows = pl.ds(pl.multiple_of(half * rhs[a], 8), rhs[a])
            return pltpu.make_async_remote_copy(src_ref=ins[a].at[rows, :], dst_ref=outs[a].at[rows, :],
                                                send_sem=send_sems.at[a], recv_sem=recv_sems.at[a],
                                                device_id=(x, y, 1 - c), device_id_type=MESH)

        sends = [copy(a, c) for a in range(n)]
        for cp in sends:
            cp.start()
        for a in range(n):
            copy(a, 1 - c).wait_recv()
        for cp in sends:
            cp.wait_send()

    return pl.pallas_call(
        body, name="grad_share", in_specs=[_HBM] * n, out_specs=[_HBM] * n,
        out_shape=[jax.ShapeDtypeStruct(f.shape, f.dtype) for f in fulls],
        input_output_aliases={a: a for a in range(n)},
        scratch_shapes=[pltpu.SemaphoreType.DMA((n,)), pltpu.SemaphoreType.DMA((n,))],
    )(*fulls)


def _add_sibling(own, recv, cq, name):
    nb, R, Cc = own.shape
    Rh = R // 2

    def body(cq_ref, a_ref, b_ref, o32_ref, o16_ref):
        s = a_ref[...] + b_ref[...]
        o32_ref[...] = s
        o16_ref[...] = s.astype(o16_ref.dtype)

    sp = pl.BlockSpec((1, Rh, Cc), lambda b, s: (b, 0, 0))
    gs = pltpu.PrefetchScalarGridSpec(
        num_scalar_prefetch=1, grid=(nb,),
        in_specs=[pl.BlockSpec((1, Rh, Cc), lambda b, s: (b, s[0], 0)), sp], out_specs=[sp, sp])
    return pl.pallas_call(
        body, name=name, grid_spec=gs,
        out_shape=[jax.ShapeDtypeStruct((nb, Rh, Cc), F32), jax.ShapeDtypeStruct((nb, Rh, Cc), _MXU)],
        compiler_params=_cparams(("parallel",)),
    )(cq, own, recv)


def _add_chips(part32, recv3, cq, name):
    nb, Rh, Cc = part32.shape

    def body(cq_ref, a_ref, b_ref, o_ref):
        acc = a_ref[0]
        for j in range(3):
            acc = acc + b_ref[j].astype(F32)
        o_ref[...] = acc

    gs = pltpu.PrefetchScalarGridSpec(
        num_scalar_prefetch=1, grid=(1,),
        in_specs=[pl.BlockSpec((1, Rh, Cc), lambda i, s: (s[1], 0, 0)), pl.BlockSpec((3, Rh, Cc), lambda i, s: (0, 0, 0))],
        out_specs=pl.BlockSpec((Rh, Cc), lambda i, s: (s[0], 0)))
    return pl.pallas_call(
        body, name=name, grid_spec=gs, out_shape=jax.ShapeDtypeStruct((2 * Rh, Cc), F32),
        compiler_params=_cparams(("arbitrary",)),
    )(cq, part32, recv3)


def _sum_devices(small_all):
    def body(s_ref, o_ref):
        tot = s_ref[0]
        for d in range(1, 8):
            tot = tot + s_ref[d]
        o_ref[...] = tot

    return pl.pallas_call(body, name="sum_devices", out_shape=jax.ShapeDtypeStruct(small_all.shape[1:], F32))(small_all)


def _adamw(w, g, m, v, name):
    R, Cc = w.shape
    T = max([t for t in range(8, 257, 8) if R % t == 0], default=R)
    c1 = 1.0 / (1.0 - ADAM_B1 ** ADAM_STEP)
    c2 = 1.0 / (1.0 - ADAM_B2 ** ADAM_STEP)

    def body(w_ref, g_ref, m_ref, v_ref, d_ref, mo_ref, vo_ref):
        gv = g_ref[...]
        mn = ADAM_B1 * m_ref[...] + (1.0 - ADAM_B1) * gv
        vn = ADAM_B2 * v_ref[...] + (1.0 - ADAM_B2) * (gv * gv)
        mo_ref[...] = mn
        vo_ref[...] = vn
        d_ref[...] = -ADAM_LR * ((mn * c1) / (jnp.sqrt(vn * c2) + ADAM_EPS) + ADAM_WD * w_ref[...])

    sp = pl.BlockSpec((T, Cc), lambda i: (i, 0))
    sh = jax.ShapeDtypeStruct((R, Cc), F32)
    return pl.pallas_call(
        body, name=name, grid=(R // T,), in_specs=[sp] * 4, out_specs=(sp, sp, sp), out_shape=(sh, sh, sh),
        compiler_params=_cparams(("parallel",)),
    )(w, g, m, v)


SMALL_ROWS = 32
REPL_ROWS = 8


def _pad_lanes(v, n=D_MODEL):
    return jnp.pad(v, ((0, 0), (0, n - v.shape[1])))


def kernel(x, norm1_w, w_in, conv_qkv_w, a_log, dt_bias, gdn_norm_w, w_out, norm2_w, w_up, ffn_conv_w, w_down, final_norm_w, loss_target, m_norm1_w, m_w_in, m_conv_qkv_w, m_a_log, m_dt_bias, m_gdn_norm_w, m_w_out, m_norm2_w, m_w_up, m_ffn_conv_w, m_w_down, m_final_norm_w, v_norm1_w, v_w_in, v_conv_qkv_w, v_a_log, v_dt_bias, v_gdn_norm_w, v_w_out, v_norm2_w, v_w_up, v_ffn_conv_w, v_w_down, v_final_norm_w):
    c = lax.axis_index("c")
    q = 2 * lax.axis_index("x") + lax.axis_index("y")
    S = x.shape[1]
    cq = jnp.stack([c, q]).astype(jnp.int32)

    def gather(shards, tag):
        got = _gather_halves(shards, "gather_" + tag)
        got = _sibling_fill(got, "fill_" + tag)
        return _place_own(shards, got, cq, "place_" + tag)

    g_in, g_conv, g_fconv = gather([w_in[0].astype(_MXU), conv_qkv_w[0], ffn_conv_w[0]], "in")
    rest = [w_out[0].astype(_MXU), w_up[0].astype(_MXU), w_down[0].astype(_MXU)]
    *rest_started, token = _gather_halves_start(rest, g_conv, "gather_rest_start")

    def rest_weights(after):
        got = _gather_halves_wait(rest_started, after, "gather_rest_wait")
        got = _sibling_fill(got, "fill_rest")
        g_out, g_up, g_down = _place_own(rest, got, cq, "place_rest")
        return g_out.reshape(D_MODEL, D_MODEL), g_up, g_down.reshape(D_FF, D_MODEL)
    wp = _wp_assemble(g_in)
    conv_f = jnp.concatenate([g_conv[i] for i in range(N_CHIPS)], axis=1)
    fcw = jnp.concatenate([g_fconv[i] for i in range(N_CHIPS)], axis=1)
    gp = _pad_lanes(jnp.concatenate([a_log, dt_bias], axis=1), 128)
    fnw = final_norm_w[None, :]
    loss_l, dx, g = _local_step(x[0], loss_target[0], norm1_w + token[0:1, 0:1], norm2_w, fnw, gp, gdn_norm_w, wp, conv_f,
                                fcw, rest_weights)
    loss = lax.psum(loss_l[0, 0], ("x", "y", "c"))
    fams = [_win_split(g["wp"]), g["w_up"], g["w_out"].reshape(N_CHIPS, D_MODEL // N_CHIPS, D_MODEL),
            g["w_down"].reshape(N_CHIPS, D_FF // N_CHIPS, D_MODEL)]
    n_fc = FFN_CONV * D_FF

    def rows_of(v):
        flat = v.reshape(-1)
        return jnp.pad(flat, (0, -flat.shape[0] % D_MODEL)).reshape(-1, D_MODEL)

    small = jnp.concatenate([g["n1w"], g["n2w"], g["fnw"], _pad_lanes(g["gp"][0:1]), _pad_lanes(g["gnw"]),
                             rows_of(g["conv_w"]), rows_of(g["fcw_g"]), rows_of(g["fcw_u"])], axis=0)
    small = jnp.pad(small, ((0, SMALL_ROWS - small.shape[0]), (0, 0)))
    *got, small_all = _grad_sibling(fams, small)
    fam_names = ["w_in", "w_up", "w_out", "w_down"]
    parts = [_add_sibling(f, r, cq, "add_sibling_" + nm) for f, r, nm in zip(fams, got, fam_names)]
    got3 = _grad_chips([p[1] for p in parts])
    halves = [_add_chips(p[0], r3, cq, "add_chips_" + nm) for p, r3, nm in zip(parts, got3, fam_names)]
    g_w_in, g_w_up, g_w_out, g_w_down = _grad_share(halves)
    small_red = _sum_devices(small_all)
    r0 = 5
    r1 = r0 + GDN_CONV * 3 * GDN_WIDTH // D_MODEL
    r2 = r1 + -(-n_fc // D_MODEL)
    conv_red = small_red[r0:r1].reshape(GDN_CONV, 3 * GDN_WIDTH)
    fc_red = jnp.concatenate([small_red[r1:r2].reshape(-1)[:n_fc].reshape(FFN_CONV, D_FF),
                              small_red[r2:2 * r2 - r1].reshape(-1)[:n_fc].reshape(FFN_CONV, D_FF)], axis=1)
    g_conv_w = lax.dynamic_slice_in_dim(conv_red, q * (3 * GDN_WIDTH // N_CHIPS), 3 * GDN_WIDTH // N_CHIPS, axis=1)
    g_fconv_w = lax.dynamic_slice_in_dim(fc_red, q * (2 * D_FF // N_CHIPS), 2 * D_FF // N_CHIPS, axis=1)
    g_n1w, g_n2w, g_fnw = small_red[0:1], small_red[1:2], small_red[2]
    g_alog, g_dtb, g_gnw = small_red[3:4, 0:4], small_red[3:4, 4:8], small_red[4:5, 0:128]
    big = {}
    for nm, w, gg, m, v in (("w_in", w_in, g_w_in, m_w_in, v_w_in), ("conv_qkv_w", conv_qkv_w, g_conv_w, m_conv_qkv_w, v_conv_qkv_w),
                            ("w_out", w_out, g_w_out, m_w_out, v_w_out), ("w_up", w_up, g_w_up, m_w_up, v_w_up),
                            ("ffn_conv_w", ffn_conv_w, g_fconv_w, m_ffn_conv_w, v_ffn_conv_w),
                            ("w_down", w_down, g_w_down, m_w_down, v_w_down)):
        d_, m_, v_ = _adamw(w[0], gg, m[0], v[0], "adamw_" + nm)
        big[nm] = (gg[None], d_[None], m_[None], v_[None])

    def pack_small(n1, n2, fn, al, db, gn):
        return jnp.concatenate([n1, n2, fn[None, :], _pad_lanes(jnp.concatenate([al, db], axis=1)), _pad_lanes(gn),
                                jnp.zeros((REPL_ROWS - 5, D_MODEL), F32)], axis=0)

    sw = pack_small(norm1_w, norm2_w, final_norm_w, a_log, dt_bias, gdn_norm_w)
    sm = pack_small(m_norm1_w, m_norm2_w, m_final_norm_w, m_a_log, m_dt_bias, m_gdn_norm_w)
    sv = pack_small(v_norm1_w, v_norm2_w, v_final_norm_w, v_a_log, v_dt_bias, v_gdn_norm_w)
    sd, smn, svn = _adamw(sw, small_red[:REPL_ROWS], sm, sv, "adamw_small")

    def unpack_small(t):
        return dict(norm1_w=t[0:1], norm2_w=t[1:2], final_norm_w=t[2], a_log=t[3:4, 0:4], dt_bias=t[3:4, 4:8],
                    gdn_norm_w=t[4:5, 0:128])

    sg = dict(norm1_w=g_n1w, norm2_w=g_n2w, final_norm_w=g_fnw, a_log=g_alog, dt_bias=g_dtb, gdn_norm_w=g_gnw)
    sd, smn, svn = unpack_small(sd), unpack_small(smn), unpack_small(svn)
    names = ["norm1_w", "w_in", "conv_qkv_w", "a_log", "dt_bias", "gdn_norm_w", "w_out", "norm2_w", "w_up",
             "ffn_conv_w", "w_down", "final_norm_w"]
    grads = [big[n][0] if n in big else sg[n] for n in names]
    deltas = [big[n][1] if n in big else sd[n] for n in names]
    new_m = [big[n][2] if n in big else smn[n] for n in names]
    new_v = [big[n][3] if n in big else svn[n] for n in names]
    return (loss, dx[None], *grads, *deltas, *new_m, *new_v)
```

```python
import functools
import math

import numpy as np
import jax
import jax.numpy as jnp
from jax import lax
from jax.experimental import pallas as pl
from jax.experimental.pallas import tpu as pltpu

F32 = jnp.float32
BF16 = jnp.bfloat16
_MXU = jnp.bfloat16
_HI = lax.Precision.HIGHEST
EPS = 1e-6
V7X_VMEM_LIMIT = 56 * 1024 * 1024
MESH = pl.DeviceIdType.MESH

D_MODEL = 1024
GDN_HEADS, GDN_DIM, GDN_CHUNK, GDN_CONV = 4, 128, 64, 4
GDN_WIDTH = GDN_HEADS * GDN_DIM
DIL_HEADS, DIL_DIM = 8, 64
DIL_WIDTH = DIL_HEADS * DIL_DIM
D_FF, FFN_CONV = 2816, 3
IN_COLS = 3592
P_COLS = 3840
P_Z, P_QKVB, P_BA = 1536, 2048, 3584
ATT_T = 256
ADAM_LR, ADAM_B1, ADAM_B2, ADAM_EPS, ADAM_WD, ADAM_STEP = 0.001, 0.9, 0.999, 1e-08, 0.01, 10
N_CHIPS = 4


def _cparams(sem=None, vmem=None):
    kw = {}
    if sem is not None:
        kw["dimension_semantics"] = sem
    if vmem is not None:
        kw["vmem_limit_bytes"] = vmem
    return pltpu.CompilerParams(**kw)


def _silu(x):
    return x * jax.nn.sigmoid(x)


def _pick_tile(n, cap):
    best = None
    for t in range(128, min(n, cap) + 1, 128):
        if n % t == 0:
            best = t
    return best or n


def _mm(a, b, mode, *, out_dtype=F32, residual=None, name, b_blocks=False, place=None, into=None, tn=None):
    if mode == "nn":
        M, K = a.shape
        N = b.shape[0] * b.shape[2] if b_blocks else b.shape[1]
    elif mode == "nt":
        (M, K), (N, _) = a.shape, b.shape
    else:
        (K, M), (_, N) = a.shape, b.shape
    tm = _pick_tile(M, 1024)
    tn = b.shape[2] if b_blocks else (tn or _pick_tile(N, 1536))

    def vmem(tm, tn):
        return 2 * (tm * K * a.dtype.itemsize + tn * K * b.dtype.itemsize
                    + tm * tn * (jnp.dtype(out_dtype).itemsize + (4 if residual is not None else 0))) + 3 * tm * tn * 4

    fixed_tn = b_blocks or (place is not None and place[0] == "blocks")
    while vmem(tm, tn) > 40 * 1024 * 1024:
        if (tm >= tn or fixed_tn) and tm % 256 == 0:
            tm //= 2
        elif tn % 256 == 0 and not fixed_tn:
            tn //= 2
        else:
            tm //= 2
    a_spec = pl.BlockSpec((K, tm), lambda j, i: (0, i)) if mode == "tn" else pl.BlockSpec((tm, K), lambda j, i: (i, 0))
    if b_blocks:
        b_spec = pl.BlockSpec((None, K, tn), lambda j, i: (j, 0, 0))
    else:
        b_spec = pl.BlockSpec((tn, K), lambda j, i: (j, 0)) if mode == "nt" else pl.BlockSpec((K, tn), lambda j, i: (0, j))
    r_spec = pl.BlockSpec((tm, tn), lambda j, i: (i, j))
    if place is None:
        o_spec, o_shape = r_spec, (M, N)
    elif place[0] == "rows":
        off = place[2] // tm
        o_spec, o_shape = pl.BlockSpec((tm, tn), lambda j, i: (i + off, j)), (place[1], N)
    else:
        off = place[2]
        o_spec, o_shape = pl.BlockSpec((None, tm, tn), lambda j, i: (j + off, i, 0)), (place[1], M, tn)
    dims = {"nn": (((1,), (0,)), ((), ())), "nt": (((1,), (1,)), ((), ())), "tn": (((0,), (0,)), ((), ()))}[mode]

    def body(*refs):
        a_ref, b_ref = refs[0], refs[1]
        o_ref = refs[-1]
        acc = lax.dot_general(a_ref[...].astype(_MXU), b_ref[...].astype(_MXU), dims, preferred_element_type=F32)
        if residual is not None:
            acc = acc + refs[2][...]
        o_ref[...] = acc.astype(out_dtype)

    ins, specs, alias = [a, b], [a_spec, b_spec], {}
    if residual is not None:
        ins.append(residual)
        specs.append(r_spec)
    if into is not None:
        alias = {len(ins): 0}
        ins.append(into)
        specs.append(pl.BlockSpec(memory_space=pl.ANY))
    return pl.pallas_call(
        body, name=name, grid=(N // tn, M // tm), in_specs=specs, out_specs=o_spec,
        out_shape=jax.ShapeDtypeStruct(o_shape, out_dtype), input_output_aliases=alias,
        compiler_params=_cparams(("parallel", "parallel"), V7X_VMEM_LIMIT),
    )(*ins)


def _mm_nt_blocks(a_list, b4, name):
    M = a_list[0].shape[0]
    nb, N, Kb = b4.shape
    tm, tn = _pick_tile(M, 512), _pick_tile(N, 512)

    def body(a0_ref, a1_ref, b_ref, o_ref):
        acc = None
        for blk in range(nb):
            a_ref = (a0_ref, a1_ref)[blk // 2]
            lo = (blk % 2) * Kb
            t = lax.dot_general(a_ref[:, lo:lo + Kb].astype(_MXU), b_ref[blk].astype(_MXU), (((1,), (1,)), ((), ())),
                                preferred_element_type=F32)
            acc = t if acc is None else acc + t
        o_ref[...] = acc

    a_spec = pl.BlockSpec((tm, 2 * Kb), lambda j, i: (i, 0))
    return pl.pallas_call(
        body, name=name, grid=(N // tn, M // tm),
        in_specs=[a_spec, a_spec, pl.BlockSpec((nb, tn, Kb), lambda j, i: (0, j, 0))],
        out_specs=pl.BlockSpec((tm, tn), lambda j, i: (i, j)), out_shape=jax.ShapeDtypeStruct((M, N), F32),
        compiler_params=_cparams(("parallel", "parallel"), V7X_VMEM_LIMIT),
    )(a_list[0], a_list[1], b4)


def _wp_assemble(g_in):
    nb, Dm, Wb = g_in.shape
    T = 256
    n_lo = P_QKVB - 2 * Wb

    def body(g_ref, o_ref):
        g2 = g_ref[2]
        o_ref[...] = jnp.concatenate(
            [g_ref[0], g_ref[1], g2[:, :n_lo], g2[:, n_lo + 8:], g_ref[3], g2[:, n_lo:n_lo + 8],
             jnp.zeros((T, P_COLS - P_BA - 8), g_in.dtype)], axis=1)

    return pl.pallas_call(
        body, name="wp_assemble", grid=(Dm // T,), in_specs=[pl.BlockSpec((nb, T, Wb), lambda i: (0, i, 0))],
        out_specs=pl.BlockSpec((T, P_COLS), lambda i: (i, 0)), out_shape=jax.ShapeDtypeStruct((Dm, P_COLS), g_in.dtype),
        compiler_params=_cparams(("parallel",)),
    )(g_in)


def _win_split(d_wp):
    Dm = d_wp.shape[0]
    Wb = IN_COLS // N_CHIPS
    T = 256

    def body(x_ref, o_ref):
        xv = x_ref[...]
        o_ref[0] = xv[:, 0:Wb]
        o_ref[1] = xv[:, Wb:2 * Wb]
        o_ref[2] = jnp.concatenate([xv[:, 2 * Wb:P_QKVB], xv[:, P_BA:P_BA + 8], xv[:, P_QKVB:3 * Wb - 8]], axis=1)
        o_ref[3] = xv[:, 3 * Wb - 8:P_BA]

    return pl.pallas_call(
        body, name="win_split", grid=(Dm // T,), in_specs=[pl.BlockSpec((T, P_COLS), lambda i: (i, 0))],
        out_specs=pl.BlockSpec((N_CHIPS, T, Wb), lambda i: (0, i, 0)),
        out_shape=jax.ShapeDtypeStruct((N_CHIPS, Dm, Wb), F32), compiler_params=_cparams(("parallel",)),
    )(d_wp)


def _rmsnorm_fwd(x, w, name):
    S, D = x.shape
    T = _pick_tile(S, 512)

    def body(x_ref, w_ref, o_ref):
        xv = x_ref[...]
        rs = lax.rsqrt(jnp.mean(xv * xv, axis=-1, keepdims=True) + EPS)
        o_ref[...] = (xv * rs * w_ref[...]).astype(o_ref.dtype)

    return pl.pallas_call(
        body, name=name, grid=(S // T,),
        in_specs=[pl.BlockSpec((T, D), lambda i: (i, 0)), pl.BlockSpec((1, D), lambda i: (0, 0))],
        out_specs=pl.BlockSpec((T, D), lambda i: (i, 0)),
        out_shape=jax.ShapeDtypeStruct((S, D), _MXU),
        compiler_params=_cparams(("parallel",)),
    )(x, w)


def _rmsnorm_bwd(dh, x, w, dres, name):
    S, D = x.shape
    T = _pick_tile(S, 512)

    def body(dh_ref, x_ref, w_ref, dres_ref, dx_ref, dw_ref):
        xv = x_ref[...]
        rs = lax.rsqrt(jnp.mean(xv * xv, axis=-1, keepdims=True) + EPS)
        xn = xv * rs
        dhv = dh_ref[...]
        dxn = dhv * w_ref[...]
        dx_ref[...] = dres_ref[...] + rs * (dxn - xn * jnp.mean(dxn * xn, axis=-1, keepdims=True))

        @pl.when(pl.program_id(0) == 0)
        def _():
            dw_ref[...] = jnp.zeros_like(dw_ref)

        dw_ref[...] += jnp.sum(dhv * xn, axis=0, keepdims=True)

    row = pl.BlockSpec((T, D), lambda i: (i, 0))
    vec = pl.BlockSpec((1, D), lambda i: (0, 0))
    return pl.pallas_call(
        body, name=name, grid=(S // T,), in_specs=[row, row, vec, row], out_specs=(row, vec),
        out_shape=(jax.ShapeDtypeStruct((S, D), F32), jax.ShapeDtypeStruct((1, D), F32)),
        compiler_params=_cparams(("arbitrary",)),
    )(dh, x, w, dres)


def _loss_head(x3, w, tgt, name):
    S, D = x3.shape
    T = _pick_tile(S, 512)

    def body(x_ref, w_ref, t_ref, loss_ref, dx_ref, dw_ref):
        xv = x_ref[...]
        rs = lax.rsqrt(jnp.mean(xv * xv, axis=-1, keepdims=True) + EPS)
        xn = xv * rs
        err = xn * w_ref[...] - t_ref[...]
        dy = err * (1.0 / D)
        dxn = dy * w_ref[...]
        dx_ref[...] = rs * (dxn - xn * jnp.mean(dxn * xn, axis=-1, keepdims=True))

        @pl.when(pl.program_id(0) == 0)
        def _():
            dw_ref[...] = jnp.zeros_like(dw_ref)
            loss_ref[...] = jnp.zeros_like(loss_ref)

        dw_ref[...] += jnp.sum(dy * xn, axis=0, keepdims=True)
        part = jnp.sum(jnp.sum(err * err, axis=-1, keepdims=True), axis=0, keepdims=True) * (0.5 / D)
        loss_ref[...] += jnp.broadcast_to(part, loss_ref.shape)

    row = pl.BlockSpec((T, D), lambda i: (i, 0))
    vec = pl.BlockSpec((1, D), lambda i: (0, 0))
    return pl.pallas_call(
        body, name=name, grid=(S // T,), in_specs=[row, vec, row],
        out_specs=(pl.BlockSpec((8, 128), lambda i: (0, 0)), row, vec),
        out_shape=(jax.ShapeDtypeStruct((8, 128), F32), jax.ShapeDtypeStruct((S, D), F32), jax.ShapeDtypeStruct((1, D), F32)),
        compiler_params=_cparams(("arbitrary",)),
    )(x3, w, tgt)


def _conv_taps(ext, w, K, T):
    out = None
    for i in range(K):
        lo = 8 - (K - 1) + i
        term = ext[lo:lo + T, :] * w[i:i + 1, :]
        out = term if out is None else out + term
    return out


def _conv_taps_t(ext, w, K, T):
    out = None
    for i in range(K):
        lo = (K - 1) - i
        term = ext[lo:lo + T, :] * w[i:i + 1, :]
        out = term if out is None else out + term
    return out


def _tri_masks(C):
    r = lax.broadcasted_iota(jnp.int32, (C, C), 0)
    c = lax.broadcasted_iota(jnp.int32, (C, C), 1)
    return r == c, r >= c, r > c, r <= c


_NN, _NT, _TN = ((1,), (0,)), ((1,), (1,)), ((0,), (0,))
_GDN_PASSES = dict(qk=1, inv=1, sol=1, scan=1, bwd=1)


def _bdot_raw(a, b, kind, passes):
    dims = ({"NN": ((2,), (1,)), "NT": ((2,), (2,)), "TN": ((1,), (1,))}[kind], ((0,), (0,)))
    if passes == 0:
        return lax.dot_general(a, b, dims, precision=_HI, preferred_element_type=F32)
    ah, bh = a.astype(BF16), b.astype(BF16)
    out = lax.dot_general(ah, bh, dims, preferred_element_type=F32)
    if passes == 3:
        al, bl = (a - ah.astype(F32)).astype(BF16), (b - bh.astype(F32)).astype(BF16)
        out = out + lax.dot_general(ah, bl, dims, preferred_element_type=F32) + lax.dot_general(al, bh, dims, preferred_element_type=F32)
    return out


@functools.partial(jax.custom_vjp, nondiff_argnums=(2, 3))
def _bdot(a, b, kind, passes):
    return _bdot_raw(a, b, kind, passes)


def _bdot_fwd(a, b, kind, passes):
    return _bdot_raw(a, b, kind, passes), (a, b)


def _bdot_bwd(kind, passes, res, ct):
    a, b = res
    if kind == "NN":
        return _bdot_raw(ct, b, "NT", passes), _bdot_raw(a, ct, "TN", passes)
    if kind == "NT":
        return _bdot_raw(ct, b, "NN", passes), _bdot_raw(ct, a, "TN", passes)
    return _bdot_raw(b, ct, "NT", passes), _bdot_raw(a, ct, "NN", passes)


_bdot.defvjp(_bdot_fwd, _bdot_bwd)


def _softplus(x):
    return jnp.maximum(x, 0.0) + jnp.log(1.0 + jnp.exp(-jnp.abs(x)))


def _gdn_stage1(cq, ck, cv, b_col, a_col, alog, dtb, dot=_bdot_raw):
    C = cq.shape[1]
    eye, incl, strict, incl_t = _tri_masks(C)
    qn = cq * lax.rsqrt(jnp.sum(cq * cq, axis=-1, keepdims=True) + EPS) * (GDN_DIM ** -0.5)
    kn = ck * lax.rsqrt(jnp.sum(ck * ck, axis=-1, keepdims=True) + EPS)
    beta = jax.nn.sigmoid(b_col)
    g = -jnp.exp(alog) * _softplus(a_col + dtb)
    g_row = jnp.sum(jnp.where(eye, g, 0.0), axis=1, keepdims=True)
    beta_row = jnp.sum(jnp.where(eye, beta, 0.0), axis=1, keepdims=True)
    gc_col = jnp.sum(jnp.where(incl, g_row, 0.0), axis=2, keepdims=True)
    gc_row = jnp.sum(jnp.where(incl_t, g, 0.0), axis=1, keepdims=True)
    dec = jnp.where(incl, jnp.exp(jnp.where(incl, gc_col - gc_row, 0.0)), 0.0)
    kk = dot(kn, kn, "NT", _GDN_PASSES["qk"])
    qk = dot(qn, kn, "NT", _GDN_PASSES["qk"])
    lmat = jnp.where(strict, dec * kk * beta_row, 0.0)
    attn = dec * qk * beta_row
    gam = jnp.exp(gc_col)
    gc_last = gc_col[:, C - 1:C, :]
    k_end = kn * (jnp.exp(gc_last - gc_col) * beta)
    return lmat, cv, gam * kn, gam * qn, attn, k_end, jnp.exp(gc_last)


def _tri_inv(lmat):
    C = lmat.shape[1]
    eye = _tri_masks(C)[0]
    ps = _GDN_PASSES["inv"]
    p = jnp.where(eye, 1.0, 0.0) - lmat
    lp = _bdot_raw(lmat, lmat, "NN", ps)
    n = int(math.log2(C))
    for s in range(1, n):
        p = p + _bdot_raw(p, lp, "NN", ps)
        if s < n - 1:
            lp = _bdot_raw(lp, lp, "NN", ps)
    return p


def _gated_norm(o, z, gnw):
    on = o * lax.rsqrt(jnp.mean(o * o, axis=-1, keepdims=True) + EPS) * gnw
    return on * _silu(z)


GDN_PG = 2
GDN_SG = 4


def _gdn_pairs(c, ba, gp, G):
    C, W, H = GDN_CHUNK, GDN_WIDTH, GDN_HEADS
    pairs = [(j, h) for j in range(G) for h in range(H)]
    cq, ck, cv = (jnp.stack([c[C * j:C * (j + 1), o + GDN_DIM * h:o + GDN_DIM * (h + 1)] for j, h in pairs]) for o in (0, W, 2 * W))
    b_col = jnp.stack([ba[C * j:C * (j + 1), h:h + 1] for j, h in pairs])
    a_col = jnp.stack([ba[C * j:C * (j + 1), H + h:H + h + 1] for j, h in pairs])
    alog = jnp.stack([gp[0:1, h:h + 1] for j, h in pairs])
    dtb = jnp.stack([gp[0:1, H + h:H + h + 1] for j, h in pairs])
    return pairs, (cq, ck, cv, b_col, a_col, alog, dtb)


def _gdn_pre_specs(S, G):
    C = GDN_CHUNK
    T = C * G
    return dict(
        cur=pl.BlockSpec((T, 3 * GDN_WIDTH), lambda i: (i, 0)),
        prev=pl.BlockSpec((8, 3 * GDN_WIDTH), lambda i: (jnp.maximum(i * (T // 8) - 1, 0), 0)),
        ba=pl.BlockSpec((T, 128), lambda i: (i, P_BA // 128)),
        cw=pl.BlockSpec((GDN_CONV, 3 * GDN_WIDTH), lambda i: (0, 0)),
        vec=pl.BlockSpec((1, 128), lambda i: (0, 0)),
        hd=pl.BlockSpec((GDN_HEADS, T, GDN_DIM), lambda i: (0, i, 0)),
        hc=pl.BlockSpec((GDN_HEADS, T, C), lambda i: (0, i, 0)),
        ge=pl.BlockSpec((G, GDN_HEADS, 8, 128), lambda i: (i, 0, 0, 0)),
    )


def _hd_shape(S, last=GDN_DIM):
    return jax.ShapeDtypeStruct((GDN_HEADS, S, last), F32)


def _gdn_pre(proj, conv_w, gp):
    S = proj.shape[0]
    C, G = GDN_CHUNK, GDN_PG
    nc = S // C
    sp = _gdn_pre_specs(S, G)

    def body(cur_ref, prev_ref, ba_ref, cw_ref, gp_ref, uv_ref, wk_ref, qd_ref, ke_ref, at_ref, ti_ref, ge_ref):
        prev = prev_ref[...] * jnp.where(pl.program_id(0) == 0, 0.0, 1.0)
        c = _silu(_conv_taps(jnp.concatenate([prev, cur_ref[...]], axis=0), cw_ref[...], GDN_CONV, C * G))
        pairs, args = _gdn_pairs(c, ba_ref[...], gp_ref[...], G)
        lmat, v, rk, q_dec, attn, k_end, g_end = _gdn_stage1(*args)
        t = _tri_inv(lmat)
        u_v = _bdot_raw(t, v, "NN", _GDN_PASSES["sol"])
        w_k = _bdot_raw(t, rk, "NN", _GDN_PASSES["sol"])
        for b, (j, h) in enumerate(pairs):
            rows = slice(C * j, C * (j + 1))
            uv_ref[h, rows, :] = u_v[b]
            wk_ref[h, rows, :] = w_k[b]
            qd_ref[h, rows, :] = q_dec[b]
            ke_ref[h, rows, :] = k_end[b]
            at_ref[h, rows, :] = attn[b]
            ti_ref[h, rows, :] = t[b]
            ge_ref[j, h] = jnp.broadcast_to(g_end[b], (8, 128))

    return pl.pallas_call(
        body, name="gdn_pre", grid=(nc // G,),
        in_specs=[sp["cur"], sp["prev"], sp["ba"], sp["cw"], sp["vec"]],
        out_specs=(sp["hd"], sp["hd"], sp["hd"], sp["hd"], sp["hc"], sp["hc"], sp["ge"]),
        out_shape=(_hd_shape(S), _hd_shape(S), _hd_shape(S), _hd_shape(S), _hd_shape(S, C), _hd_shape(S, C),
                   jax.ShapeDtypeStruct((nc, GDN_HEADS, 8, 128), F32)),
        compiler_params=_cparams(("parallel",)),
    )(proj, proj, proj, conv_w, gp)


def _gdn_scan_specs(S, G, rev):
    C = GDN_CHUNK
    T = C * G
    n = S // T
    ci = (lambda i: n - 1 - i) if rev else (lambda i: i)
    return dict(
        hd=pl.BlockSpec((GDN_HEADS, T, GDN_DIM), lambda i: (0, ci(i), 0)),
        hc=pl.BlockSpec((GDN_HEADS, T, C), lambda i: (0, ci(i), 0)),
        ge=pl.BlockSpec((G, GDN_HEADS, 8, 128), lambda i: (ci(i), 0, 0, 0)),
        z=pl.BlockSpec((T, GDN_WIDTH), lambda i: (ci(i), P_Z // GDN_WIDTH)),
        oa=pl.BlockSpec((T, GDN_WIDTH), lambda i: (ci(i), 0)),
        vec=pl.BlockSpec((1, 128), lambda i: (0, 0)),
        st=pl.BlockSpec((G, GDN_HEADS, GDN_DIM, GDN_DIM), lambda i: (ci(i), 0, 0, 0)),
    )


def _gdn_scan(u_v, w_k, q_dec, k_end, attn, g_end, proj, gnw):
    S = proj.shape[0]
    C, G = GDN_CHUNK, GDN_SG
    nc = S // C
    sp = _gdn_scan_specs(S, G, False)
    ps = _GDN_PASSES["scan"]

    def body(uv_ref, wk_ref, qd_ref, ke_ref, at_ref, ge_ref, z_ref, gnw_ref, oa_ref, st_ref, s_scr):
        @pl.when(pl.program_id(0) == 0)
        def _():
            s_scr[...] = jnp.zeros_like(s_scr)

        for j in range(G):
            rows = slice(C * j, C * (j + 1))
            st = s_scr[...]
            st_ref[j] = st
            u = uv_ref[:, rows, :] - _bdot_raw(wk_ref[:, rows, :], st, "NN", ps)
            o = _bdot_raw(qd_ref[:, rows, :], st, "NN", ps) + _bdot_raw(at_ref[:, rows, :], u, "NN", ps)
            s_scr[...] = ge_ref[j][:, 0:1, 0:1] * st + _bdot_raw(ke_ref[:, rows, :], u, "TN", ps)
            for h in range(GDN_HEADS):
                cols = slice(GDN_DIM * h, GDN_DIM * (h + 1))
                oa_ref[rows, cols] = _gated_norm(o[h], z_ref[rows, cols], gnw_ref[...])

    return pl.pallas_call(
        body, name="gdn_scan", grid=(nc // G,),
        in_specs=[sp["hd"], sp["hd"], sp["hd"], sp["hd"], sp["hc"], sp["ge"], sp["z"], sp["vec"]],
        out_specs=(sp["oa"], sp["st"]),
        out_shape=(jax.ShapeDtypeStruct((S, GDN_WIDTH), F32),
                   jax.ShapeDtypeStruct((nc, GDN_HEADS, GDN_DIM, GDN_DIM), F32)),
        scratch_shapes=[pltpu.VMEM((GDN_HEADS, GDN_DIM, GDN_DIM), F32)],
        compiler_params=_cparams(("arbitrary",)),
    )(u_v, w_k, q_dec, k_end, attn, g_end, proj, gnw)


def _gdn_scan_bwd(u_v, w_k, q_dec, k_end, attn, g_end, proj, gnw, states, d_oa):
    S = proj.shape[0]
    C, G = GDN_CHUNK, GDN_SG
    nc = S // C
    sp = _gdn_scan_specs(S, G, True)
    ps, pb = _GDN_PASSES["scan"], _GDN_PASSES["bwd"]

    def body(uv_ref, wk_ref, qd_ref, ke_ref, at_ref, ge_ref, z_ref, gnw_ref, st_ref, doa_ref,
             duv_ref, dwk_ref, dqd_ref, dke_ref, dat_ref, dge_ref, dz_ref, dgnw_ref, ds_scr):
        @pl.when(pl.program_id(0) == 0)
        def _():
            ds_scr[...] = jnp.zeros_like(ds_scr)
            dgnw_ref[...] = jnp.zeros_like(dgnw_ref)

        dgnw = jnp.zeros((1, 128), F32)
        for j in reversed(range(G)):
            rows = slice(C * j, C * (j + 1))
            st = st_ref[j]
            wk, qd, ke, at = wk_ref[:, rows, :], qd_ref[:, rows, :], ke_ref[:, rows, :], at_ref[:, rows, :]
            u = uv_ref[:, rows, :] - _bdot_raw(wk, st, "NN", ps)
            o = _bdot_raw(qd, st, "NN", ps) + _bdot_raw(at, u, "NN", ps)
            dos = []
            for h in range(GDN_HEADS):
                cols = slice(GDN_DIM * h, GDN_DIM * (h + 1))
                _, vjp2 = jax.vjp(_gated_norm, o[h], z_ref[rows, cols], gnw_ref[...])
                do_h, dz_h, dgn = vjp2(doa_ref[rows, cols])
                dz_ref[rows, cols] = dz_h
                dgnw = dgnw + dgn
                dos.append(do_h)
            do = jnp.stack(dos)
            ds_new = ds_scr[...]
            du = _bdot_raw(at, do, "TN", pb) + _bdot_raw(ke, ds_new, "NN", pb)
            duv_ref[:, rows, :] = du
            dat_ref[:, rows, :] = _bdot_raw(do, u, "NT", pb)
            dqd_ref[:, rows, :] = _bdot_raw(do, st, "NT", pb)
            dke_ref[:, rows, :] = _bdot_raw(u, ds_new, "NT", pb)
            dwk_ref[:, rows, :] = -_bdot_raw(du, st, "NT", pb)
            d_ge = jnp.sum(jnp.sum(st * ds_new, axis=2, keepdims=True), axis=1, keepdims=True)
            dge_ref[j] = jnp.broadcast_to(d_ge, (GDN_HEADS, 8, 128))
            ds_scr[...] = ge_ref[j][:, 0:1, 0:1] * ds_new + _bdot_raw(qd, do, "TN", pb) - _bdot_raw(wk, du, "TN", pb)
        dgnw_ref[...] += dgnw

    return pl.pallas_call(
        body, name="gdn_scan_bwd", grid=(nc // G,),
        in_specs=[sp["hd"], sp["hd"], sp["hd"], sp["hd"], sp["hc"], sp["ge"], sp["z"], sp["vec"], sp["st"], sp["oa"]],
        out_specs=(sp["hd"], sp["hd"], sp["hd"], sp["hd"], sp["hc"], sp["ge"], sp["oa"], sp["vec"]),
        out_shape=(_hd_shape(S), _hd_shape(S), _hd_shape(S), _hd_shape(S), _hd_shape(S, C),
                   jax.ShapeDtypeStruct((nc, GDN_HEADS, 8, 128), F32), jax.ShapeDtypeStruct((S, GDN_WIDTH), F32),
                   jax.ShapeDtypeStruct((1, 128), F32)),
        scratch_shapes=[pltpu.VMEM((GDN_HEADS, GDN_DIM, GDN_DIM), F32)],
        compiler_params=_cparams(("arbitrary",)),
    )(u_v, w_k, q_dec, k_end, attn, g_end, proj, gnw, states, d_oa)


def _gdn_post(proj, conv_w, gp, tinv, u_v, w_k, d_uv, d_wk, d_qd, d_ke, d_at, d_ge):
    S = proj.shape[0]
    C, G = GDN_CHUNK, GDN_PG
    nc = S // C
    sp = _gdn_pre_specs(S, G)
    pb = _GDN_PASSES["bwd"]

    def body(cur_ref, prev_ref, ba_ref, cw_ref, gp_ref, ti_ref, uv_ref, wk_ref, duv_ref, dwk_ref, dqd_ref, dke_ref,
             dat_ref, dge_ref, dpre_ref, dba_ref, dgp_ref):
        i = pl.program_id(0)

        @pl.when(i == 0)
        def _():
            dgp_ref[...] = jnp.zeros_like(dgp_ref)

        prev = prev_ref[...] * jnp.where(i == 0, 0.0, 1.0)
        pre = _conv_taps(jnp.concatenate([prev, cur_ref[...]], axis=0), cw_ref[...], GDN_CONV, C * G)
        sg = jax.nn.sigmoid(pre)
        dsilu = sg * (1.0 + pre * (1.0 - sg))
        pairs, args = _gdn_pairs(pre * sg, ba_ref[...], gp_ref[...], G)
        _, vjp1 = jax.vjp(functools.partial(_gdn_stage1, dot=_bdot), *args)

        def take(ref):
            return jnp.stack([ref[h, C * j:C * (j + 1), :] for j, h in pairs])

        t, u_v, w_k = take(ti_ref), take(uv_ref), take(wk_ref)
        d_v = _bdot_raw(t, take(duv_ref), "TN", pb)
        d_rk = _bdot_raw(t, take(dwk_ref), "TN", pb)
        d_l = -(_bdot_raw(d_v, u_v, "NT", pb) + _bdot_raw(d_rk, w_k, "NT", pb))
        d_ge = jnp.stack([dge_ref[j, h][0:1, 0:1] for j, h in pairs])
        dcq, dck, dcv, db, da, dalog, ddtb = vjp1((d_l, d_v, d_rk, take(dqd_ref), take(dat_ref), take(dke_ref), d_ge))
        lane = lax.broadcasted_iota(jnp.int32, (C, 128), 1)
        lane1 = lax.broadcasted_iota(jnp.int32, (1, 128), 1)
        dgp = jnp.zeros((1, 128), F32)
        for j in range(G):
            rows = slice(C * j, C * (j + 1))
            dba = jnp.zeros((C, 128), F32)
            for h in range(GDN_HEADS):
                b = GDN_HEADS * j + h
                for o_, dcx in ((0, dcq), (GDN_WIDTH, dck), (2 * GDN_WIDTH, dcv)):
                    cols = slice(o_ + GDN_DIM * h, o_ + GDN_DIM * (h + 1))
                    dpre_ref[rows, cols] = dcx[b] * dsilu[rows, cols]
                dba = dba + jnp.where(lane == h, db[b], 0.0) + jnp.where(lane == GDN_HEADS + h, da[b], 0.0)
                dgp = dgp + jnp.where(lane1 == h, dalog[b], 0.0) + jnp.where(lane1 == GDN_HEADS + h, ddtb[b], 0.0)
            dba_ref[rows, :] = dba
        dgp_ref[0:1, :] += dgp

    T = C * G
    return pl.pallas_call(
        body, name="gdn_post", grid=(nc // G,),
        in_specs=[sp["cur"], sp["prev"], sp["ba"], sp["cw"], sp["vec"], sp["hc"], sp["hd"], sp["hd"], sp["hd"], sp["hd"],
                  sp["hd"], sp["hd"], sp["hc"], sp["ge"]],
        out_specs=(sp["cur"], pl.BlockSpec((T, 128), lambda i: (i, 0)), pl.BlockSpec((8, 128), lambda i: (0, 0))),
        out_shape=(jax.ShapeDtypeStruct((S, 3 * GDN_WIDTH), F32), jax.ShapeDtypeStruct((S, 128), F32),
                   jax.ShapeDtypeStruct((8, 128), F32)),
        compiler_params=_cparams(("arbitrary",)),
    )(proj, proj, proj, conv_w, gp, tinv, u_v, w_k, d_uv, d_wk, d_qd, d_ke, d_at, d_ge)


def _conv_bwd(dpre, x, xcol0, w, K, name, tc):
    S, Cc = dpre.shape
    T = _pick_tile(S, 256)
    nt, ncol = S // T, Cc // tc
    xo = xcol0 // tc

    def body(d_ref, dn_ref, x_ref, xp_ref, w_ref, dx_ref, dw_ref):
        i = pl.program_id(1)
        dn = dn_ref[...] * jnp.where(i == nt - 1, 0.0, 1.0)
        dv = d_ref[...]
        ext_d = jnp.concatenate([dv, dn], axis=0)
        wv = w_ref[...]
        dx_ref[...] = _conv_taps_t(ext_d, wv, K, T).astype(dx_ref.dtype)
        xp = xp_ref[...] * jnp.where(i == 0, 0.0, 1.0)
        ext_x = jnp.concatenate([xp, x_ref[...]], axis=0)

        @pl.when(i == 0)
        def _():
            dw_ref[...] = jnp.zeros_like(dw_ref)

        for k in range(K):
            lo = 8 - (K - 1) + k
            dw_ref[k:k + 1, :] += jnp.sum(dv * ext_x[lo:lo + T, :], axis=0, keepdims=True)

    r8 = T // 8
    return pl.pallas_call(
        body, name=name, grid=(ncol, nt),
        in_specs=[pl.BlockSpec((T, tc), lambda j, i: (i, j)),
                  pl.BlockSpec((8, tc), lambda j, i: (jnp.minimum((i + 1) * r8, S // 8 - 1), j)),
                  pl.BlockSpec((T, tc), lambda j, i: (i, j + xo)),
                  pl.BlockSpec((8, tc), lambda j, i: (jnp.maximum(i * r8 - 1, 0), j + xo)),
                  pl.BlockSpec((K, tc), lambda j, i: (0, j))],
        out_specs=(pl.BlockSpec((T, tc), lambda j, i: (i, j)), pl.BlockSpec((K, tc), lambda j, i: (0, j))),
        out_shape=(jax.ShapeDtypeStruct((S, Cc), _MXU), jax.ShapeDtypeStruct((K, Cc), F32)),
        compiler_params=_cparams(("parallel", "arbitrary")),
    )(dpre, dpre, x, x, w)


def _dil_bias(nt):
    T = ATT_T
    d = (np.arange(nt)[:, None, None] * T + np.arange(T)[None, :, None] - np.arange(T)[None, None, :])
    cnt = ((d >= 0) & (d <= 128)).astype(np.float64) + ((d >= 0) & (d % 4 == 0) & (d <= 512)) + ((d >= 0) & (d % 16 == 0))
    return jnp.asarray(np.where(cnt > 0, np.log(np.maximum(cnt, 1.0)), -1e30), dtype=F32)


def _attn_fwd(proj):
    S = proj.shape[0]
    T = ATT_T
    nt = S // T
    bias = _dil_bias(nt)
    scale = DIL_DIM ** -0.5
    npair = DIL_WIDTH // 128
    qb0, kb0, vb0 = P_QKVB // 128, (P_QKVB + DIL_WIDTH) // 128, (P_QKVB + 2 * DIL_WIDTH) // 128

    def body(q_ref, k_ref, v_ref, b_ref, o_ref, lse_ref):
        i = pl.program_id(1)
        qs = (q_ref[...] * scale).astype(_MXU)

        def step(j, carry):
            kt = k_ref[pl.ds(pl.multiple_of(j * T, T), T), :].astype(_MXU)
            vt = v_ref[pl.ds(pl.multiple_of(j * T, T), T), :].astype(_MXU)
            bt = b_ref[i - j]
            out = []
            for hh in range(2):
                m, l, acc = carry[hh]
                sl = slice(hh * DIL_DIM, (hh + 1) * DIL_DIM)
                s = lax.dot_general(qs[:, sl], kt[:, sl], (_NT, ((), ())), preferred_element_type=F32) + bt
                m_new = jnp.maximum(m, jnp.max(s, axis=-1, keepdims=True))
                p = jnp.exp(s - m_new)
                a = jnp.exp(m - m_new)
                l = a * l + jnp.sum(p, axis=-1, keepdims=True)
                acc = a * acc + lax.dot_general(p.astype(_MXU), vt[:, sl], (_NN, ((), ())), preferred_element_type=F32)
                out.append((m_new, l, acc))
            return tuple(out)

        init = tuple((jnp.full((T, 1), -1e30, F32), jnp.zeros((T, 1), F32), jnp.zeros((T, DIL_DIM), F32)) for _ in range(2))
        res = lax.fori_loop(0, i + 1, step, init)
        for hh in range(2):
            m, l, acc = res[hh]
            sl = slice(hh * DIL_DIM, (hh + 1) * DIL_DIM)
            o_ref[:, sl] = acc / l
            lse_ref[:, sl] = jnp.broadcast_to(m + jnp.log(l), (T, DIL_DIM))

    return pl.pallas_call(
        body, name="attn_fwd", grid=(npair, nt),
        in_specs=[pl.BlockSpec((T, 128), lambda p, i: (i, qb0 + p)),
                  pl.BlockSpec((S, 128), lambda p, i: (0, kb0 + p)),
                  pl.BlockSpec((S, 128), lambda p, i: (0, vb0 + p)),
                  pl.BlockSpec((nt, T, T), lambda p, i: (0, 0, 0))],
        out_specs=(pl.BlockSpec((T, 128), lambda p, i: (i, p)), pl.BlockSpec((T, 128), lambda p, i: (i, p))),
        out_shape=(jax.ShapeDtypeStruct((S, DIL_WIDTH), F32), jax.ShapeDtypeStruct((S, DIL_WIDTH), F32)),
        compiler_params=_cparams(("parallel", "parallel")),
    )(proj, proj, proj, bias)


def _attn_bwd(proj, o_b, lse, d_ob):
    S = proj.shape[0]
    T = ATT_T
    nt = S // T
    bias = _dil_bias(nt)
    scale = DIL_DIM ** -0.5
    npair = DIL_WIDTH // 128
    qb0, kb0, vb0 = P_QKVB // 128, (P_QKVB + DIL_WIDTH) // 128, (P_QKVB + 2 * DIL_WIDTH) // 128

    def body(q_ref, k_ref, v_ref, o_ref, lse_ref, do_ref, b_ref, dq_ref, dk_ref, dv_ref, dq_scr):
        j = pl.program_id(1)

        @pl.when(j == 0)
        def _():
            dq_scr[...] = jnp.zeros_like(dq_scr)

        kt = k_ref[...].astype(_MXU)
        vt = v_ref[...].astype(_MXU)

        def step(i, carry):
            rows = pl.ds(pl.multiple_of(i * T, T), T)
            qs = (q_ref[rows, :] * scale).astype(_MXU)
            dov = do_ref[rows, :]
            prod = dov * o_ref[rows, :]
            lsev = lse_ref[rows, :]
            dob = dov.astype(_MXU)
            bt = b_ref[i - j]
            out = []
            dqs = []
            for hh in range(2):
                dk, dv = carry[hh]
                sl = slice(hh * DIL_DIM, (hh + 1) * DIL_DIM)
                s = lax.dot_general(qs[:, sl], kt[:, sl], (_NT, ((), ())), preferred_element_type=F32) + bt
                p = jnp.exp(s - lsev[:, hh * DIL_DIM:hh * DIL_DIM + 1])
                delta = jnp.sum(prod[:, sl], axis=-1, keepdims=True)
                dp = lax.dot_general(dob[:, sl], vt[:, sl], (_NT, ((), ())), preferred_element_type=F32)
                ds = (p * (dp - delta)).astype(_MXU)
                dv = dv + lax.dot_general(p.astype(_MXU), dob[:, sl], (_TN, ((), ())), preferred_element_type=F32)
                dk = dk + lax.dot_general(ds, qs[:, sl], (_TN, ((), ())), preferred_element_type=F32)
                dqs.append(lax.dot_general(ds, kt[:, sl], (_NN, ((), ())), preferred_element_type=F32) * scale)
                out.append((dk, dv))
            dq_scr[rows, :] += jnp.concatenate(dqs, axis=1)
            return tuple(out)

        init = tuple((jnp.zeros((T, DIL_DIM), F32), jnp.zeros((T, DIL_DIM), F32)) for _ in range(2))
        res = lax.fori_loop(j, nt, step, init)
        dk_ref[...] = jnp.concatenate([res[0][0], res[1][0]], axis=1).astype(dk_ref.dtype)
        dv_ref[...] = jnp.concatenate([res[0][1], res[1][1]], axis=1).astype(dv_ref.dtype)

        @pl.when(j == nt - 1)
        def _():
            dq_ref[...] = dq_scr[...].astype(dq_ref.dtype)

    full = lambda c0: pl.BlockSpec((S, 128), lambda p, j: (0, c0 + p))
    tile = lambda c0: pl.BlockSpec((T, 128), lambda p, j: (j, c0 + p))
    out3 = jax.ShapeDtypeStruct((S, DIL_WIDTH), _MXU)
    return pl.pallas_call(
        body, name="attn_bwd", grid=(npair, nt),
        in_specs=[full(qb0), tile(kb0), tile(vb0), full(0), full(0), full(0),
                  pl.BlockSpec((nt, T, T), lambda p, j: (0, 0, 0))],
        out_specs=(full(0), tile(0), tile(0)),
        out_shape=(out3, out3, out3),
        scratch_shapes=[pltpu.VMEM((S, 128), F32)],
        compiler_params=_cparams(("parallel", "arbitrary")),
    )(proj, proj, proj, o_b, lse, d_ob, bias)


def _ffn_act(up, cw):
    S, Cc = up.shape[0], up.shape[1] // 2
    T, tc = _pick_tile(S, 256), _pick_tile(Cc, 1536)
    r8 = T // 8
    nct = Cc // tc

    def body(g_ref, gp_ref, u_ref, up_ref, wg_ref, wu_ref, o_ref):
        keep = jnp.where(pl.program_id(1) == 0, 0.0, 1.0)
        cg = _conv_taps(jnp.concatenate([gp_ref[...] * keep, g_ref[...]], axis=0), wg_ref[...], FFN_CONV, T)
        cu = _conv_taps(jnp.concatenate([up_ref[...] * keep, u_ref[...]], axis=0), wu_ref[...], FFN_CONV, T)
        o_ref[...] = (_silu(cg) * cu).astype(o_ref.dtype)

    cur = lambda o: pl.BlockSpec((T, tc), lambda j, i: (i, j + o))
    prev = lambda o: pl.BlockSpec((8, tc), lambda j, i: (jnp.maximum(i * r8 - 1, 0), j + o))
    wsp = lambda o: pl.BlockSpec((FFN_CONV, tc), lambda j, i: (0, j + o))
    return pl.pallas_call(
        body, name="ffn_act", grid=(nct, S // T),
        in_specs=[cur(0), prev(0), cur(nct), prev(nct), wsp(0), wsp(nct)], out_specs=cur(0),
        out_shape=jax.ShapeDtypeStruct((S, Cc), _MXU),
        compiler_params=_cparams(("parallel", "parallel")),
    )(up, up, up, up, cw, cw)


def _ffn_act_bwd(d_act, up, cw):
    S, Cc = up.shape[0], up.shape[1] // 2
    T, tc = _pick_tile(S, 256), _pick_tile(Cc, 1536)
    r8 = T // 8
    nt = S // T
    nct = Cc // tc
    K = FFN_CONV

    def body(da_ref, dan_ref, g_ref, gp_ref, gn_ref, u_ref, up_ref, un_ref, wg_ref, wu_ref,
             dg_ref, du_ref, dwg_ref, dwu_ref):
        i = pl.program_id(1)
        keep_p = jnp.where(i == 0, 0.0, 1.0)
        keep_n = jnp.where(i == nt - 1, 0.0, 1.0)
        wg, wu = wg_ref[...], wu_ref[...]
        xg = jnp.concatenate([gp_ref[...] * keep_p, g_ref[...], gn_ref[...] * keep_n], axis=0)
        xu = jnp.concatenate([up_ref[...] * keep_p, u_ref[...], un_ref[...] * keep_n], axis=0)
        cg = _conv_taps(xg, wg, K, T + 8)
        cu = _conv_taps(xu, wu, K, T + 8)
        da = jnp.concatenate([da_ref[...], dan_ref[...] * keep_n], axis=0)
        sg = jax.nn.sigmoid(cg)
        d_cg = da * cu * (sg * (1.0 + cg * (1.0 - sg)))
        d_cu = da * (cg * sg)
        dg_ref[...] = _conv_taps_t(d_cg, wg, K, T).astype(dg_ref.dtype)
        du_ref[...] = _conv_taps_t(d_cu, wu, K, T).astype(du_ref.dtype)

        @pl.when(i == 0)
        def _():
            dwg_ref[...] = jnp.zeros_like(dwg_ref)
            dwu_ref[...] = jnp.zeros_like(dwu_ref)

        for k in range(K):
            lo = 8 - (K - 1) + k
            dwg_ref[k:k + 1, :] += jnp.sum(d_cg[0:T, :] * xg[lo:lo + T, :], axis=0, keepdims=True)
            dwu_ref[k:k + 1, :] += jnp.sum(d_cu[0:T, :] * xu[lo:lo + T, :], axis=0, keepdims=True)

    cur = lambda o: pl.BlockSpec((T, tc), lambda j, i: (i, j + o))
    prev = lambda o: pl.BlockSpec((8, tc), lambda j, i: (jnp.maximum(i * r8 - 1, 0), j + o))
    nxt = lambda o: pl.BlockSpec((8, tc), lambda j, i: (jnp.minimum((i + 1) * r8, S // 8 - 1), j + o))
    wsp = lambda o: pl.BlockSpec((K, tc), lambda j, i: (0, j + o))
    return pl.pallas_call(
        body, name="ffn_act_bwd", grid=(nct, nt),
        in_specs=[cur(0), nxt(0), cur(0), prev(0), nxt(0), cur(nct), prev(nct), nxt(nct), wsp(0), wsp(nct)],
        out_specs=(cur(0), cur(0), wsp(0), wsp(0)),
        out_shape=(jax.ShapeDtypeStruct((S, Cc), _MXU), jax.ShapeDtypeStruct((S, Cc), _MXU),
                   jax.ShapeDtypeStruct((K, Cc), F32), jax.ShapeDtypeStruct((K, Cc), F32)),
        compiler_params=_cparams(("parallel", "arbitrary")),
    )(d_act, d_act, up, up, up, up, up, up, cw, cw)


def _local_step(x, tgt, n1w, n2w, fnw, gp, gnw, wp, conv_w, fcw, rest_weights, early_grads):
    h1 = _rmsnorm_fwd(x, n1w, "norm1")
    proj = _mm(h1, wp, "nn", name="proj")
    u_v, w_k, q_dec, k_end, attn, tinv, g_end = _gdn_pre(proj, conv_w, gp)
    o_a, states = _gdn_scan(u_v, w_k, q_dec, k_end, attn, g_end, proj, gnw)
    o_b, lse = _attn_fwd(proj)
    w_out, w_up4, w_down = rest_weights([o_a, o_b])
    x2 = _mm(o_b, w_out[GDN_WIDTH:], "nn", residual=_mm(o_a, w_out[:GDN_WIDTH], "nn", residual=x, name="outproj_a"),
             name="outproj_b")
    h2 = _rmsnorm_fwd(x2, n2w, "norm2")
    up = _mm(h2, w_up4, "nn", b_blocks=True, name="up")
    act = _ffn_act(up, fcw)
    x3 = _mm(act, w_down, "nn", residual=x2, name="down")
    loss, dx3, d_fnw = _loss_head(x3, fnw, tgt, "loss_head")
    d_act = _mm(dx3, w_down, "nt", name="d_act")
    d_wdown = _mm(act, dx3, "tn", name="d_wdown")
    d_upg, d_upu, d_fcwg, d_fcwu = _ffn_act_bwd(d_act, up, fcw)
    d_wup = _mm(h2, d_upg, "tn", place=("blocks", N_CHIPS, 0), tn=w_up4.shape[2], name="d_wgate")
    d_wup = _mm(h2, d_upu, "tn", place=("blocks", N_CHIPS, N_CHIPS // 2), tn=w_up4.shape[2], into=d_wup, name="d_wup")
    token = early_grads(d_wup, d_wdown)
    d_h2 = _mm_nt_blocks([d_upg, d_upu], w_up4, "d_h2")
    dx2, d_n2w = _rmsnorm_bwd(d_h2, x2, n2w + token[0:1, 0:1], dx3, "norm2_bwd")
    d_oa = _mm(dx2, w_out[:GDN_WIDTH], "nt", name="d_oa")
    d_ob = _mm(dx2, w_out[GDN_WIDTH:], "nt", name="d_ob")
    d_wout = _mm(o_a, dx2, "tn", place=("rows", D_MODEL, 0), name="d_wout_a")
    d_wout = _mm(o_b, dx2, "tn", place=("rows", D_MODEL, GDN_WIDTH), into=d_wout, name="d_wout_b")
    dq_b, dk_b, dv_b = _attn_bwd(proj, o_b, lse, d_ob)
    d_uv, d_wk, d_qd, d_ke, d_at, d_ge, d_z, d_gnw = _gdn_scan_bwd(u_v, w_k, q_dec, k_end, attn, g_end, proj, gnw, states, d_oa)
    d_pre, d_ba, d_gp = _gdn_post(proj, conv_w, gp, tinv, u_v, w_k, d_uv, d_wk, d_qd, d_ke, d_at, d_ge)
    d_qkva, d_convw = _conv_bwd(d_pre, proj, 0, conv_w, GDN_CONV, "gdn_conv_bwd", 512)
    d_proj = jnp.concatenate([d_qkva, d_z.astype(_MXU), dq_b, dk_b, dv_b, d_ba.astype(_MXU),
                              jnp.zeros((x.shape[0], P_COLS - P_BA - 128), _MXU)], axis=1)
    d_wp = _mm(h1, d_proj, "tn", name="d_wp")
    d_h1 = _mm(d_proj, wp, "nt", name="d_h1")
    dx, d_n1w = _rmsnorm_bwd(d_h1, x, n1w, dx2, "norm1_bwd")
    grads = dict(wp=d_wp, conv_w=d_convw, w_out=d_wout, w_up=d_wup, fcw_g=d_fcwg, fcw_u=d_fcwu, w_down=d_wdown,
                 n1w=d_n1w, n2w=d_n2w, fnw=d_fnw, gp=d_gp, gnw=d_gnw)
    return loss, dx, grads


_HBM = pl.BlockSpec(memory_space=pltpu.HBM)


def _pos():
    return lax.axis_index("x"), lax.axis_index("y"), lax.axis_index("c")


def _other_chips(x, y):
    return [(1 - x, y), (x, 1 - y), (1 - x, 1 - y)]


def _halvable(shape):
    return shape[0] % 32 == 0


def _rows_of_half(shape, half):
    if not _halvable(shape):
        return pl.ds(0, shape[0])
    return pl.ds(pl.multiple_of(half * (shape[0] // 2), 16), shape[0] // 2)


def _gather_halves(shards, name):
    n = len(shards)
    shapes = [s.shape for s in shards]

    def body(*refs):
        ins, outs = refs[:n], refs[n:2 * n]
        send_sems, recv_sems = refs[2 * n:]
        x, y, c = _pos()
        q = 2 * x + y
        chips = _other_chips(x, y)

        def copy(a, j, block):
            px, py = chips[j]
            rows = _rows_of_half(shapes[a], c)
            return pltpu.make_async_remote_copy(
                src_ref=ins[a].at[rows, :], dst_ref=outs[a].at[block, rows, :], send_sem=send_sems.at[3 * a + j],
                recv_sem=recv_sems.at[3 * a + j], device_id=(px, py, c), device_id_type=MESH)

        sends = [copy(a, j, q) for a in range(n) for j in range(3)]
        for cp in sends:
            cp.start()
        for a in range(n):
            for j, (px, py) in enumerate(chips):
                copy(a, j, 2 * px + py).wait_recv()
        for cp in sends:
            cp.wait_send()

    return pl.pallas_call(
        body, name=name, in_specs=[_HBM] * n, out_specs=[_HBM] * n,
        out_shape=[jax.ShapeDtypeStruct((N_CHIPS,) + s.shape, s.dtype) for s in shards],
        scratch_shapes=[pltpu.SemaphoreType.DMA((3 * n,)), pltpu.SemaphoreType.DMA((3 * n,))],
    )(*shards)


_SEM = pl.BlockSpec(memory_space=pltpu.SEMAPHORE)
_ANY = pl.BlockSpec(memory_space=pl.ANY)
_DATAFLOW = pltpu.SideEffectType.DATAFLOW_SIDE_EFFECTING


def _in_hbm(a):
    return pltpu.with_memory_space_constraint(a, pltpu.HBM)


def _halves_copy(src_refs, land_refs, send_sems, recv_sems, shapes, a, j, block, x, y, c):
    px, py = _other_chips(x, y)[j]
    rows = _rows_of_half(shapes[a], c)
    return pltpu.make_async_remote_copy(
        src_ref=src_refs[a].at[rows, :], dst_ref=land_refs[a].at[block, rows, :], send_sem=send_sems.at[3 * a + j],
        recv_sem=recv_sems.at[3 * a + j], device_id=(px, py, c), device_id_type=MESH)


def _gather_halves_start(shards, after, name):
    n = len(shards)
    shapes = [s.shape for s in shards]

    def body(*refs):
        ins, lands = refs[:n], refs[n:2 * n]
        send_sems, recv_sems = refs[2 * n + 1], refs[2 * n + 2]
        token = refs[-1]
        x, y, c = _pos()
        q = 2 * x + y
        for a in range(n):
            for j in range(3):
                _halves_copy(ins, lands, send_sems, recv_sems, shapes, a, j, q, x, y, c).start()
        token[...] = jnp.zeros_like(token)

    land_shapes = [(N_CHIPS,) + s.shape for s in shards]
    return pl.pallas_call(
        body, name=name,
        out_shape=(pltpu.SemaphoreType.DMA((3 * n,)), pltpu.SemaphoreType.DMA((3 * n,)),
                   *[pltpu.HBM(s.shape, s.dtype) for s in shards],
                   *[pltpu.HBM(ls, s.dtype) for ls, s in zip(land_shapes, shards)],
                   jax.ShapeDtypeStruct((8, 128), F32)),
        in_specs=[_HBM] * (2 * n) + [_ANY],
        out_specs=(_SEM, _SEM, *[_HBM] * (2 * n), pl.BlockSpec(memory_space=pltpu.VMEM)),
        input_output_aliases={a: 2 + a for a in range(2 * n)},
        compiler_params=pltpu.CompilerParams(has_side_effects=_DATAFLOW),
    )(*[_in_hbm(s) for s in shards], *[_in_hbm(lax.empty(ls, s.dtype)) for ls, s in zip(land_shapes, shards)], after)


def _gather_halves_wait(started, after, name):
    send_sems, recv_sems, *thru = started
    n = len(thru) // 2
    shapes = [t.shape for t in thru[:n]]

    def body(*refs):
        ins, lands = refs[:n], refs[n:2 * n]
        send_sems, recv_sems = refs[2 * n], refs[2 * n + 1]
        x, y, c = _pos()
        q = 2 * x + y
        chips = _other_chips(x, y)
        for a in range(n):
            for j, (px, py) in enumerate(chips):
                _halves_copy(ins, lands, send_sems, recv_sems, shapes, a, j, q, x, y, c).wait_send()
                _halves_copy(ins, lands, send_sems, recv_sems, shapes, a, j, 2 * px + py, x, y, c).wait_recv()

    outs = pl.pallas_call(
        body, name=name, out_shape=[pltpu.HBM(t.shape, t.dtype) for t in thru],
        in_specs=[_HBM] * (2 * n) + [_SEM, _SEM] + [_ANY] * len(after), out_specs=[_HBM] * (2 * n),
        input_output_aliases={a: a for a in range(2 * n)},
        compiler_params=pltpu.CompilerParams(has_side_effects=_DATAFLOW),
    )(*thru, send_sems, recv_sems, *after)
    return outs[n:]


def _sibling_fill(gathered, name):
    big = [a for a, g in enumerate(gathered) if _halvable(g.shape[1:])]
    n = len(gathered)

    def body(*refs):
        ins, outs = refs[:n], refs[n:2 * n]
        send_sems, recv_sems = refs[2 * n:]
        x, y, c = _pos()
        chips = _other_chips(x, y)

        def copy(k, j, half):
            a = big[k]
            px, py = chips[j]
            rows = _rows_of_half(gathered[a].shape[1:], half)
            return pltpu.make_async_remote_copy(
                src_ref=ins[a].at[2 * px + py, rows, :], dst_ref=outs[a].at[2 * px + py, rows, :],
                send_sem=send_sems.at[3 * k + j], recv_sem=recv_sems.at[3 * k + j],
                device_id=(x, y, 1 - c), device_id_type=MESH)

        sends = [copy(k, j, c) for k in range(len(big)) for j in range(3)]
        for cp in sends:
            cp.start()
        for k in range(len(big)):
            for j in range(3):
                copy(k, j, 1 - c).wait_recv()
        for cp in sends:
            cp.wait_send()

    return pl.pallas_call(
        body, name=name, in_specs=[_HBM] * n, out_specs=[_HBM] * n,
        out_shape=[jax.ShapeDtypeStruct(g.shape, g.dtype) for g in gathered],
        input_output_aliases={a: a for a in range(n)},
        scratch_shapes=[pltpu.SemaphoreType.DMA((3 * len(big),)), pltpu.SemaphoreType.DMA((3 * len(big),))],
    )(*gathered)


def _place_own(shards, gathered, cq, name):
    n = len(shards)
    steps = 4

    def body(cq_ref, *refs):
        for a in range(n):
            refs[2 * n + a][...] = refs[a][...]

    def tile(shape):
        return shape[0] // steps if _halvable(shape) else shape[0]

    in_specs = [pl.BlockSpec((tile(s.shape), s.shape[1]), (lambda i, s_: (i, 0)) if _halvable(s.shape) else (lambda i, s_: (0, 0)))
                for s in shards]
    in_specs += [pl.BlockSpec(memory_space=pl.ANY)] * n
    out_specs = [pl.BlockSpec((None, tile(s.shape), s.shape[1]),
                              (lambda i, s_: (s_[1], i, 0)) if _halvable(s.shape) else (lambda i, s_: (s_[1], 0, 0)))
                 for s in shards]
    gs = pltpu.PrefetchScalarGridSpec(num_scalar_prefetch=1, grid=(steps,), in_specs=in_specs, out_specs=out_specs)
    return pl.pallas_call(
        body, name=name, grid_spec=gs, out_shape=[jax.ShapeDtypeStruct(g.shape, g.dtype) for g in gathered],
        input_output_aliases={1 + n + a: a for a in range(n)},
        compiler_params=_cparams(("arbitrary",)),
    )(cq, *shards, *gathered)


def _half_rows(ref, c, rh):
    return ref.at[:, pl.ds(pl.multiple_of(c * rh, 8), rh), :]


def _grad_sibling(fams, small, name):
    n = len(fams)
    ns = 0 if small is None else 1
    rhs = [f.shape[1] // 2 for f in fams]

    def body(*refs):
        ins, outs = refs[:n], refs[n + ns:2 * n + ns]
        send_sems, recv_sems = refs[2 * (n + ns)], refs[2 * (n + ns) + 1]
        x, y, c = _pos()
        bigs = [pltpu.make_async_remote_copy(src_ref=_half_rows(ins[a], 1 - c, rhs[a]), dst_ref=outs[a],
                                             send_sem=send_sems.at[7 * ns + a], recv_sem=recv_sems.at[7 * ns + a],
                                             device_id=(x, y, 1 - c), device_id_type=MESH) for a in range(n)]
        for cp in bigs:
            cp.start()
        sends = []
        if ns:
            small_ref, all_ref, loc_sem = refs[n], refs[2 * n + 1], refs[2 * n + 4]
            me = 4 * x + 2 * y + c
            mine = pltpu.make_async_copy(small_ref, all_ref.at[me], loc_sem)
            mine.start()

            def peer(r):
                dx, dy, dc = (r >> 2) & 1, (r >> 1) & 1, r & 1
                return (x if dx == 0 else 1 - x), (y if dy == 0 else 1 - y), (c if dc == 0 else 1 - c)

            def small_copy(r, slot):
                return pltpu.make_async_remote_copy(src_ref=small_ref, dst_ref=all_ref.at[slot], send_sem=send_sems.at[r - 1],
                                                    recv_sem=recv_sems.at[r - 1], device_id=peer(r), device_id_type=MESH)

            sends = [small_copy(r, me) for r in range(1, 8)]
            for cp in sends:
                cp.start()
            for r in range(1, 8):
                px, py, pc = peer(r)
                small_copy(r, 4 * px + 2 * py + pc).wait_recv()
        for cp in bigs:
            cp.wait_recv()
        for cp in bigs + sends:
            cp.wait_send()
        if ns:
            mine.wait()

    return pl.pallas_call(
        body, name=name, in_specs=[_HBM] * (n + ns), out_specs=[_HBM] * (n + ns),
        out_shape=[jax.ShapeDtypeStruct((f.shape[0], f.shape[1] // 2, f.shape[2]), f.dtype) for f in fams]
        + ([jax.ShapeDtypeStruct((8,) + small.shape, small.dtype)] if ns else []),
        scratch_shapes=[pltpu.SemaphoreType.DMA((7 * ns + n,)), pltpu.SemaphoreType.DMA((7 * ns + n,))]
        + ([pltpu.SemaphoreType.DMA] if ns else []),
    )(*fams, *([small] if ns else []))


def _chips_copy(src_refs, land_refs, send_sems, recv_sems, a, j, x, y, c):
    px, py = _other_chips(x, y)[j]
    return pltpu.make_async_remote_copy(src_ref=src_refs[a].at[2 * px + py], dst_ref=land_refs[a].at[j],
                                        send_sem=send_sems.at[3 * a + j], recv_sem=recv_sems.at[3 * a + j],
                                        device_id=(px, py, c), device_id_type=MESH)


def _grad_chips_start(parts, name):
    n = len(parts)

    def body(*refs):
        ins, lands = refs[:n], refs[n:2 * n]
        send_sems, recv_sems = refs[2 * n], refs[2 * n + 1]
        token = refs[-1]
        x, y, c = _pos()
        for a in range(n):
            for j in range(3):
                _chips_copy(ins, lands, send_sems, recv_sems, a, j, x, y, c).start()
        token[...] = jnp.zeros_like(token)

    land_shapes = [(3,) + p.shape[1:] for p in parts]
    return pl.pallas_call(
        body, name=name,
        out_shape=(pltpu.SemaphoreType.DMA((3 * n,)), pltpu.SemaphoreType.DMA((3 * n,)),
                   *[pltpu.HBM(p.shape, p.dtype) for p in parts],
                   *[pltpu.HBM(ls, p.dtype) for ls, p in zip(land_shapes, parts)],
                   jax.ShapeDtypeStruct((8, 128), F32)),
        in_specs=[_HBM] * (2 * n),
        out_specs=(_SEM, _SEM, *[_HBM] * (2 * n), pl.BlockSpec(memory_space=pltpu.VMEM)),
        input_output_aliases={a: 2 + a for a in range(2 * n)},
        compiler_params=pltpu.CompilerParams(has_side_effects=_DATAFLOW),
    )(*[_in_hbm(p) for p in parts], *[_in_hbm(lax.empty(ls, p.dtype)) for ls, p in zip(land_shapes, parts)])


def _grad_chips_wait(started, after, name):
    send_sems, recv_sems, *thru = started
    n = len(thru) // 2

    def body(*refs):
        ins, lands = refs[:n], refs[n:2 * n]
        send_sems, recv_sems = refs[2 * n], refs[2 * n + 1]
        x, y, c = _pos()
        for a in range(n):
            for j in range(3):
                cp = _chips_copy(ins, lands, send_sems, recv_sems, a, j, x, y, c)
                cp.wait_send()
                cp.wait_recv()

    outs = pl.pallas_call(
        body, name=name, out_shape=[pltpu.HBM(t.shape, t.dtype) for t in thru],
        in_specs=[_HBM] * (2 * n) + [_SEM, _SEM] + [_ANY] * len(after), out_specs=[_HBM] * (2 * n),
        input_output_aliases={a: a for a in range(2 * n)},
        compiler_params=pltpu.CompilerParams(has_side_effects=_DATAFLOW),
    )(*thru, send_sems, recv_sems, *after)
    return outs[n:]


def _grad_chips(parts):
    n = len(parts)

    def body(*refs):
        ins, outs = refs[:n], refs[n:2 * n]
        send_sems, recv_sems = refs[2 * n:]
        x, y, c = _pos()
        chips = _other_chips(x, y)

        def copy(a, j):
            px, py = chips[j]
            return pltpu.make_async_remote_copy(src_ref=ins[a].at[2 * px + py], dst_ref=outs[a].at[j],
                                                send_sem=send_sems.at[3 * a + j], recv_sem=recv_sems.at[3 * a + j],
                                                device_id=(px, py, c), device_id_type=MESH)

        sends = [copy(a, j) for a in range(n) for j in range(3)]
        for cp in sends:
            cp.start()
        for a in range(n):
            for j in range(3):
                copy(a, j).wait_recv()
        for cp in sends:
            cp.wait_send()

    return pl.pallas_call(
        body, name="grad_chips", in_specs=[_HBM] * n, out_specs=[_HBM] * n,
        out_shape=[jax.ShapeDtypeStruct((3,) + p.shape[1:], p.dtype) for p in parts],
        scratch_shapes=[pltpu.SemaphoreType.DMA((3 * n,)), pltpu.SemaphoreType.DMA((3 * n,))],
    )(*parts)


def _grad_share(fulls):
    n = len(fulls)
    rhs = [f.shape[0] // 2 for f in fulls]

    def body(*refs):
        ins, outs = refs[:n], refs[n:2 * n]
        send_sems, recv_sems = refs[2 * n:]
        x, y, c = _pos()

        def copy(a, half):
            rows = pl.ds(pl.multiple_of(half * rhs[a], 8), rhs[a])
            return pltpu.make_async_remote_copy(src_ref=ins[a].at[rows, :], dst_ref=outs[a].at[rows, :],
                                                send_sem=send_sems.at[a], recv_sem=recv_sems.at[a],
                                                device_id=(x, y, 1 - c), device_id_type=MESH)

        sends = [copy(a, c) for a in range(n)]
        for cp in sends:
            cp.start()
        for a in range(n):
            copy(a, 1 - c).wait_recv()
        for cp in sends:
            cp.wait_send()

    return pl.pallas_call(
        body, name="grad_share", in_specs=[_HBM] * n, out_specs=[_HBM] * n,
        out_shape=[jax.ShapeDtypeStruct(f.shape, f.dtype) for f in fulls],
        input_output_aliases={a: a for a in range(n)},
        scratch_shapes=[pltpu.SemaphoreType.DMA((n,)), pltpu.SemaphoreType.DMA((n,))],
    )(*fulls)


def _add_sibling(own, recv, cq, name):
    nb, R, Cc = own.shape
    Rh = R // 2

    def body(cq_ref, a_ref, b_ref, o32_ref, o16_ref):
        s = a_ref[...] + b_ref[...]
        o32_ref[...] = s
        o16_ref[...] = s.astype(o16_ref.dtype)

    sp = pl.BlockSpec((1, Rh, Cc), lambda b, s: (b, 0, 0))
    gs = pltpu.PrefetchScalarGridSpec(
        num_scalar_prefetch=1, grid=(nb,),
        in_specs=[pl.BlockSpec((1, Rh, Cc), lambda b, s: (b, s[0], 0)), sp], out_specs=[sp, sp])
    return pl.pallas_call(
        body, name=name, grid_spec=gs,
        out_shape=[jax.ShapeDtypeStruct((nb, Rh, Cc), F32), jax.ShapeDtypeStruct((nb, Rh, Cc), _MXU)],
        compiler_params=_cparams(("parallel",)),
    )(cq, own, recv)


def _add_chips(part32, recv3, cq, name):
    nb, Rh, Cc = part32.shape

    def body(cq_ref, a_ref, b_ref, o_ref):
        acc = a_ref[0]
        for j in range(3):
            acc = acc + b_ref[j].astype(F32)
        o_ref[...] = acc

    gs = pltpu.PrefetchScalarGridSpec(
        num_scalar_prefetch=1, grid=(1,),
        in_specs=[pl.BlockSpec((1, Rh, Cc), lambda i, s: (s[1], 0, 0)), pl.BlockSpec((3, Rh, Cc), lambda i, s: (0, 0, 0))],
        out_specs=pl.BlockSpec((Rh, Cc), lambda i, s: (s[0], 0)))
    return pl.pallas_call(
        body, name=name, grid_spec=gs, out_shape=jax.ShapeDtypeStruct((2 * Rh, Cc), F32),
        compiler_params=_cparams(("arbitrary",)),
    )(cq, part32, recv3)


def _sum_devices(small_all):
    def body(s_ref, o_ref):
        tot = s_ref[0]
        for d in range(1, 8):
            tot = tot + s_ref[d]
        o_ref[...] = tot

    return pl.pallas_call(body, name="sum_devices", out_shape=jax.ShapeDtypeStruct(small_all.shape[1:], F32))(small_all)


def _adamw(w, g, m, v, name):
    R, Cc = w.shape
    T = max([t for t in range(8, 257, 8) if R % t == 0], default=R)
    c1 = 1.0 / (1.0 - ADAM_B1 ** ADAM_STEP)
    c2 = 1.0 / (1.0 - ADAM_B2 ** ADAM_STEP)

    def body(w_ref, g_ref, m_ref, v_ref, d_ref, mo_ref, vo_ref):
        gv = g_ref[...]
        mn = ADAM_B1 * m_ref[...] + (1.0 - ADAM_B1) * gv
        vn = ADAM_B2 * v_ref[...] + (1.0 - ADAM_B2) * (gv * gv)
        mo_ref[...] = mn
        vo_ref[...] = vn
        d_ref[...] = -ADAM_LR * ((mn * c1) / (jnp.sqrt(vn * c2) + ADAM_EPS) + ADAM_WD * w_ref[...])

    sp = pl.BlockSpec((T, Cc), lambda i: (i, 0))
    sh = jax.ShapeDtypeStruct((R, Cc), F32)
    return pl.pallas_call(
        body, name=name, grid=(R // T,), in_specs=[sp] * 4, out_specs=(sp, sp, sp), out_shape=(sh, sh, sh),
        compiler_params=_cparams(("parallel",)),
    )(w, g, m, v)


SMALL_ROWS = 32
REPL_ROWS = 8


def _pad_lanes(v, n=D_MODEL):
    return jnp.pad(v, ((0, 0), (0, n - v.shape[1])))


def kernel(x, norm1_w, w_in, conv_qkv_w, a_log, dt_bias, gdn_norm_w, w_out, norm2_w, w_up, ffn_conv_w, w_down, final_norm_w, loss_target, m_norm1_w, m_w_in, m_conv_qkv_w, m_a_log, m_dt_bias, m_gdn_norm_w, m_w_out, m_norm2_w, m_w_up, m_ffn_conv_w, m_w_down, m_final_norm_w, v_norm1_w, v_w_in, v_conv_qkv_w, v_a_log, v_dt_bias, v_gdn_norm_w, v_w_out, v_norm2_w, v_w_up, v_ffn_conv_w, v_w_down, v_final_norm_w):
    c = lax.axis_index("c")
    q = 2 * lax.axis_index("x") + lax.axis_index("y")
    S = x.shape[1]
    cq = jnp.stack([c, q]).astype(jnp.int32)

    def gather(shards, tag):
        got = _gather_halves(shards, "gather_" + tag)
        got = _sibling_fill(got, "fill_" + tag)
        return _place_own(shards, got, cq, "place_" + tag)

    g_in, g_conv, g_fconv = gather([w_in[0].astype(_MXU), conv_qkv_w[0], ffn_conv_w[0]], "in")
    rest = [w_out[0].astype(_MXU), w_up[0].astype(_MXU), w_down[0].astype(_MXU)]
    *rest_started, token = _gather_halves_start(rest, g_conv, "gather_rest_start")

    def rest_weights(after):
        got = _gather_halves_wait(rest_started, after, "gather_rest_wait")
        got = _sibling_fill(got, "fill_rest")
        g_out, g_up, g_down = _place_own(rest, got, cq, "place_rest")
        return g_out.reshape(D_MODEL, D_MODEL), g_up, g_down.reshape(D_FF, D_MODEL)
    wp = _wp_assemble(g_in)
    conv_f = jnp.concatenate([g_conv[i] for i in range(N_CHIPS)], axis=1)
    fcw = jnp.concatenate([g_fconv[i] for i in range(N_CHIPS)], axis=1)
    gp = _pad_lanes(jnp.concatenate([a_log, dt_bias], axis=1), 128)
    fnw = final_norm_w[None, :]
    early = {}

    def early_grads(d_wup, d_wdown):
        fams_e = [d_wup, d_wdown.reshape(N_CHIPS, D_FF // N_CHIPS, D_MODEL)]
        got_e = _grad_sibling(fams_e, None, "grad_sibling_early")
        early["parts"] = [_add_sibling(f, r, cq, "add_sibling_" + nm) for f, r, nm in zip(fams_e, got_e, ("w_up", "w_down"))]
        *early["started"], tok = _grad_chips_start([p[1] for p in early["parts"]], "grad_chips_start")
        return tok

    loss_l, dx, g = _local_step(x[0], loss_target[0], norm1_w + token[0:1, 0:1], norm2_w, fnw, gp, gdn_norm_w, wp, conv_f,
                                fcw, rest_weights, early_grads)
    loss = lax.psum(loss_l[0, 0], ("x", "y", "c"))
    fams = [_win_split(g["wp"]), g["w_out"].reshape(N_CHIPS, D_MODEL // N_CHIPS, D_MODEL)]
    n_fc = FFN_CONV * D_FF

    def rows_of(v):
        flat = v.reshape(-1)
        return jnp.pad(flat, (0, -flat.shape[0] % D_MODEL)).reshape(-1, D_MODEL)

    small = jnp.concatenate([g["n1w"], g["n2w"], g["fnw"], _pad_lanes(g["gp"][0:1]), _pad_lanes(g["gnw"]),
                             rows_of(g["conv_w"]), rows_of(g["fcw_g"]), rows_of(g["fcw_u"])], axis=0)
    small = jnp.pad(small, ((0, SMALL_ROWS - small.shape[0]), (0, 0)))
    *got, small_all = _grad_sibling(fams, small, "grad_sibling")
    parts = [_add_sibling(f, r, cq, "add_sibling_" + nm) for f, r, nm in zip(fams, got, ("w_in", "w_out"))]
    got3 = _grad_chips([p[1] for p in parts])
    got3_e = _grad_chips_wait(early["started"], [dx, g["wp"]], "grad_chips_wait")
    halves = [_add_chips(p[0], r3, cq, "add_chips_" + nm)
              for p, r3, nm in zip(parts + early["parts"], list(got3) + list(got3_e), ("w_in", "w_out", "w_up", "w_down"))]
    g_w_in, g_w_out, g_w_up, g_w_down = _grad_share(halves)
    small_red = _sum_devices(small_all)
    r0 = 5
    r1 = r0 + GDN_CONV * 3 * GDN_WIDTH // D_MODEL
    r2 = r1 + -(-n_fc // D_MODEL)
    conv_red = small_red[r0:r1].reshape(GDN_CONV, 3 * GDN_WIDTH)
    fc_red = jnp.concatenate([small_red[r1:r2].reshape(-1)[:n_fc].reshape(FFN_CONV, D_FF),
                              small_red[r2:2 * r2 - r1].reshape(-1)[:n_fc].reshape(FFN_CONV, D_FF)], axis=1)
    g_conv_w = lax.dynamic_slice_in_dim(conv_red, q * (3 * GDN_WIDTH // N_CHIPS), 3 * GDN_WIDTH // N_CHIPS, axis=1)
    g_fconv_w = lax.dynamic_slice_in_dim(fc_red, q * (2 * D_FF // N_CHIPS), 2 * D_FF // N_CHIPS, axis=1)
    g_n1w, g_n2w, g_fnw = small_red[0:1], small_red[1:2], small_red[2]
    g_alog, g_dtb, g_gnw = small_red[3:4, 0:4], small_red[3:4, 4:8], small_red[4:5, 0:128]
    big = {}
    for nm, w, gg, m, v in (("w_in", w_in, g_w_in, m_w_in, v_w_in), ("conv_qkv_w", conv_qkv_w, g_conv_w, m_conv_qkv_w, v_conv_qkv_w),
                            ("w_out", w_out, g_w_out, m_w_out, v_w_out), ("w_up", w_up, g_w_up, m_w_up, v_w_up),
                            ("ffn_conv_w", ffn_conv_w, g_fconv_w, m_ffn_conv_w, v_ffn_conv_w),
                            ("w_down", w_down, g_w_down, m_w_down, v_w_down)):
        d_, m_, v_ = _adamw(w[0], gg, m[0], v[0], "adamw_" + nm)
        big[nm] = (gg[None], d_[None], m_[None], v_[None])

    def pack_small(n1, n2, fn, al, db, gn):
        return jnp.concatenate([n1, n2, fn[None, :], _pad_lanes(jnp.concatenate([al, db], axis=1)), _pad_lanes(gn),
                                jnp.zeros((REPL_ROWS - 5, D_MODEL), F32)], axis=0)

    sw = pack_small(norm1_w, norm2_w, final_norm_w, a_log, dt_bias, gdn_norm_w)
    sm = pack_small(m_norm1_w, m_norm2_w, m_final_norm_w, m_a_log, m_dt_bias, m_gdn_norm_w)
    sv = pack_small(v_norm1_w, v_norm2_w, v_final_norm_w, v_a_log, v_dt_bias, v_gdn_norm_w)
    sd, smn, svn = _adamw(sw, small_red[:REPL_ROWS], sm, sv, "adamw_small")

    def unpack_small(t):
        return dict(norm1_w=t[0:1], norm2_w=t[1:2], final_norm_w=t[2], a_log=t[3:4, 0:4], dt_bias=t[3:4, 4:8],
                    gdn_norm_w=t[4:5, 0:128])

    sg = dict(norm1_w=g_n1w, norm2_w=g_n2w, final_norm_w=g_fnw, a_log=g_alog, dt_bias=g_dtb, gdn_norm_w=g_gnw)
    sd, smn, svn = unpack_small(sd), unpack_small(smn), unpack_small(svn)
    names = ["norm1_w", "w_in", "conv_qkv_w", "a_log", "dt_bias", "gdn_norm_w", "w_out", "norm2_w", "w_up",
             "ffn_conv_w", "w_down", "final_norm_w"]
    grads = [big[n][0] if n in big else sg[n] for n in names]
    deltas = [big[n][1] if n in big else sd[n] for n in names]
    new_m = [big[n][2] if n in big else smn[n] for n in names]
    new_v = [big[n][3] if n in big else svn[n] for n in names]
    return (loss, dx[None], *grads, *deltas, *new_m, *new_v)
```

```python
import functools
import math

import numpy as np
import jax
import jax.numpy as jnp
from jax import lax
from jax.experimental import pallas as pl
from jax.experimental.pallas import tpu as pltpu

F32 = jnp.float32
BF16 = jnp.bfloat16
_MXU = jnp.bfloat16
_HI = lax.Precision.HIGHEST
EPS = 1e-6
V7X_VMEM_LIMIT = 56 * 1024 * 1024
MESH = pl.DeviceIdType.MESH

D_MODEL = 1024
GDN_HEADS, GDN_DIM, GDN_CHUNK, GDN_CONV = 4, 128, 64, 4
GDN_WIDTH = GDN_HEADS * GDN_DIM
DIL_HEADS, DIL_DIM = 8, 64
DIL_WIDTH = DIL_HEADS * DIL_DIM
D_FF, FFN_CONV = 2816, 3
IN_COLS = 3592
P_COLS = 3840
P_Z, P_QKVB, P_BA = 1536, 2048, 3584
ATT_T = 1024
ADAM_LR, ADAM_B1, ADAM_B2, ADAM_EPS, ADAM_WD, ADAM_STEP = 0.001, 0.9, 0.999, 1e-08, 0.01, 10
N_CHIPS = 4


def _cparams(sem=None, vmem=None):
    kw = {}
    if sem is not None:
        kw["dimension_semantics"] = sem
    if vmem is not None:
        kw["vmem_limit_bytes"] = vmem
    return pltpu.CompilerParams(**kw)


def _silu(x):
    return x * jax.nn.sigmoid(x)


def _pick_tile(n, cap):
    best = None
    for t in range(128, min(n, cap) + 1, 128):
        if n % t == 0:
            best = t
    return best or n


def _mm(a, b, mode, *, out_dtype=F32, residual=None, name, b_blocks=False, place=None, into=None, tn=None):
    if mode == "nn":
        M, K = a.shape
        N = b.shape[0] * b.shape[2] if b_blocks else b.shape[1]
    elif mode == "nt":
        (M, K), (N, _) = a.shape, b.shape
    else:
        (K, M), (_, N) = a.shape, b.shape
    tm = _pick_tile(M, 1024)
    tn = b.shape[2] if b_blocks else (tn or _pick_tile(N, 1536))

    def vmem(tm, tn):
        return 2 * (tm * K * a.dtype.itemsize + tn * K * b.dtype.itemsize
                    + tm * tn * (jnp.dtype(out_dtype).itemsize + (4 if residual is not None else 0))) + 3 * tm * tn * 4

    fixed_tn = b_blocks or (place is not None and place[0] == "blocks")
    while vmem(tm, tn) > 40 * 1024 * 1024:
        if (tm >= tn or fixed_tn) and tm % 256 == 0:
            tm //= 2
        elif tn % 256 == 0 and not fixed_tn:
            tn //= 2
        else:
            tm //= 2
    a_spec = pl.BlockSpec((K, tm), lambda j, i: (0, i)) if mode == "tn" else pl.BlockSpec((tm, K), lambda j, i: (i, 0))
    if b_blocks:
        b_spec = pl.BlockSpec((None, K, tn), lambda j, i: (j, 0, 0))
    else:
        b_spec = pl.BlockSpec((tn, K), lambda j, i: (j, 0)) if mode == "nt" else pl.BlockSpec((K, tn), lambda j, i: (0, j))
    r_spec = pl.BlockSpec((tm, tn), lambda j, i: (i, j))
    if place is None:
        o_spec, o_shape = r_spec, (M, N)
    elif place[0] == "rows":
        off = place[2] // tm
        o_spec, o_shape = pl.BlockSpec((tm, tn), lambda j, i: (i + off, j)), (place[1], N)
    else:
        off = place[2]
        o_spec, o_shape = pl.BlockSpec((None, tm, tn), lambda j, i: (j + off, i, 0)), (place[1], M, tn)
    dims = {"nn": (((1,), (0,)), ((), ())), "nt": (((1,), (1,)), ((), ())), "tn": (((0,), (0,)), ((), ()))}[mode]

    def body(*refs):
        a_ref, b_ref = refs[0], refs[1]
        o_ref = refs[-1]
        acc = lax.dot_general(a_ref[...].astype(_MXU), b_ref[...].astype(_MXU), dims, preferred_element_type=F32)
        if residual is not None:
            acc = acc + refs[2][...]
        o_ref[...] = acc.astype(out_dtype)

    ins, specs, alias = [a, b], [a_spec, b_spec], {}
    if residual is not None:
        ins.append(residual)
        specs.append(r_spec)
    if into is not None:
        alias = {len(ins): 0}
        ins.append(into)
        specs.append(pl.BlockSpec(memory_space=pl.ANY))
    return pl.pallas_call(
        body, name=name, grid=(N // tn, M // tm), in_specs=specs, out_specs=o_spec,
        out_shape=jax.ShapeDtypeStruct(o_shape, out_dtype), input_output_aliases=alias,
        compiler_params=_cparams(("parallel", "parallel"), V7X_VMEM_LIMIT),
    )(*ins)


def _mm_nt_blocks(a_list, b4, name):
    M = a_list[0].shape[0]
    nb, N, Kb = b4.shape
    tm, tn = _pick_tile(M, 512), _pick_tile(N, 512)

    def body(a0_ref, a1_ref, b_ref, o_ref):
        acc = None
        for blk in range(nb):
            a_ref = (a0_ref, a1_ref)[blk // 2]
            lo = (blk % 2) * Kb
            t = lax.dot_general(a_ref[:, lo:lo + Kb].astype(_MXU), b_ref[blk].astype(_MXU), (((1,), (1,)), ((), ())),
                                preferred_element_type=F32)
            acc = t if acc is None else acc + t
        o_ref[...] = acc

    a_spec = pl.BlockSpec((tm, 2 * Kb), lambda j, i: (i, 0))
    return pl.pallas_call(
        body, name=name, grid=(N // tn, M // tm),
        in_specs=[a_spec, a_spec, pl.BlockSpec((nb, tn, Kb), lambda j, i: (0, j, 0))],
        out_specs=pl.BlockSpec((tm, tn), lambda j, i: (i, j)), out_shape=jax.ShapeDtypeStruct((M, N), F32),
        compiler_params=_cparams(("parallel", "parallel"), V7X_VMEM_LIMIT),
    )(a_list[0], a_list[1], b4)


def _wp_assemble(g_in):
    nb, Dm, Wb = g_in.shape
    T = 256
    n_lo = P_QKVB - 2 * Wb

    def body(g_ref, o_ref):
        g2 = g_ref[2]
        o_ref[...] = jnp.concatenate(
            [g_ref[0], g_ref[1], g2[:, :n_lo], g2[:, n_lo + 8:], g_ref[3], g2[:, n_lo:n_lo + 8],
             jnp.zeros((T, P_COLS - P_BA - 8), g_in.dtype)], axis=1)

    return pl.pallas_call(
        body, name="wp_assemble", grid=(Dm // T,), in_specs=[pl.BlockSpec((nb, T, Wb), lambda i: (0, i, 0))],
        out_specs=pl.BlockSpec((T, P_COLS), lambda i: (i, 0)), out_shape=jax.ShapeDtypeStruct((Dm, P_COLS), g_in.dtype),
        compiler_params=_cparams(("parallel",)),
    )(g_in)


def _win_split(d_wp):
    Dm = d_wp.shape[0]
    Wb = IN_COLS // N_CHIPS
    T = 256

    def body(x_ref, o_ref):
        xv = x_ref[...]
        o_ref[0] = xv[:, 0:Wb]
        o_ref[1] = xv[:, Wb:2 * Wb]
        o_ref[2] = jnp.concatenate([xv[:, 2 * Wb:P_QKVB], xv[:, P_BA:P_BA + 8], xv[:, P_QKVB:3 * Wb - 8]], axis=1)
        o_ref[3] = xv[:, 3 * Wb - 8:P_BA]

    return pl.pallas_call(
        body, name="win_split", grid=(Dm // T,), in_specs=[pl.BlockSpec((T, P_COLS), lambda i: (i, 0))],
        out_specs=pl.BlockSpec((N_CHIPS, T, Wb), lambda i: (0, i, 0)),
        out_shape=jax.ShapeDtypeStruct((N_CHIPS, Dm, Wb), F32), compiler_params=_cparams(("parallel",)),
    )(d_wp)


def _rmsnorm_fwd(x, w, name):
    S, D = x.shape
    T = _pick_tile(S, 512)

    def body(x_ref, w_ref, o_ref):
        xv = x_ref[...]
        rs = lax.rsqrt(jnp.mean(xv * xv, axis=-1, keepdims=True) + EPS)
        o_ref[...] = (xv * rs * w_ref[...]).astype(o_ref.dtype)

    return pl.pallas_call(
        body, name=name, grid=(S // T,),
        in_specs=[pl.BlockSpec((T, D), lambda i: (i, 0)), pl.BlockSpec((1, D), lambda i: (0, 0))],
        out_specs=pl.BlockSpec((T, D), lambda i: (i, 0)),
        out_shape=jax.ShapeDtypeStruct((S, D), _MXU),
        compiler_params=_cparams(("parallel",)),
    )(x, w)


def _rmsnorm_bwd(dh, x, w, dres, name):
    S, D = x.shape
    T = _pick_tile(S, 512)

    def body(dh_ref, x_ref, w_ref, dres_ref, dx_ref, dw_ref):
        xv = x_ref[...]
        rs = lax.rsqrt(jnp.mean(xv * xv, axis=-1, keepdims=True) + EPS)
        xn = xv * rs
        dhv = dh_ref[...]
        dxn = dhv * w_ref[...]
        dx_ref[...] = dres_ref[...] + rs * (dxn - xn * jnp.mean(dxn * xn, axis=-1, keepdims=True))

        @pl.when(pl.program_id(0) == 0)
        def _():
            dw_ref[...] = jnp.zeros_like(dw_ref)

        dw_ref[...] += jnp.sum(dhv * xn, axis=0, keepdims=True)

    row = pl.BlockSpec((T, D), lambda i: (i, 0))
    vec = pl.BlockSpec((1, D), lambda i: (0, 0))
    return pl.pallas_call(
        body, name=name, grid=(S // T,), in_specs=[row, row, vec, row], out_specs=(row, vec),
        out_shape=(jax.ShapeDtypeStruct((S, D), F32), jax.ShapeDtypeStruct((1, D), F32)),
        compiler_params=_cparams(("arbitrary",)),
    )(dh, x, w, dres)


def _loss_head(x3, w, tgt, name):
    S, D = x3.shape
    T = _pick_tile(S, 512)

    def body(x_ref, w_ref, t_ref, loss_ref, dx_ref, dw_ref):
        xv = x_ref[...]
        rs = lax.rsqrt(jnp.mean(xv * xv, axis=-1, keepdims=True) + EPS)
        xn = xv * rs
        err = xn * w_ref[...] - t_ref[...]
        dy = err * (1.0 / D)
        dxn = dy * w_ref[...]
        dx_ref[...] = rs * (dxn - xn * jnp.mean(dxn * xn, axis=-1, keepdims=True))

        @pl.when(pl.program_id(0) == 0)
        def _():
            dw_ref[...] = jnp.zeros_like(dw_ref)
            loss_ref[...] = jnp.zeros_like(loss_ref)

        dw_ref[...] += jnp.sum(dy * xn, axis=0, keepdims=True)
        part = jnp.sum(jnp.sum(err * err, axis=-1, keepdims=True), axis=0, keepdims=True) * (0.5 / D)
        loss_ref[...] += jnp.broadcast_to(part, loss_ref.shape)

    row = pl.BlockSpec((T, D), lambda i: (i, 0))
    vec = pl.BlockSpec((1, D), lambda i: (0, 0))
    return pl.pallas_call(
        body, name=name, grid=(S // T,), in_specs=[row, vec, row],
        out_specs=(pl.BlockSpec((8, 128), lambda i: (0, 0)), row, vec),
        out_shape=(jax.ShapeDtypeStruct((8, 128), F32), jax.ShapeDtypeStruct((S, D), F32), jax.ShapeDtypeStruct((1, D), F32)),
        compiler_params=_cparams(("arbitrary",)),
    )(x3, w, tgt)


def _conv_taps(ext, w, K, T):
    out = None
    for i in range(K):
        lo = 8 - (K - 1) + i
        term = ext[lo:lo + T, :] * w[i:i + 1, :]
        out = term if out is None else out + term
    return out


def _conv_taps_t(ext, w, K, T):
    out = None
    for i in range(K):
        lo = (K - 1) - i
        term = ext[lo:lo + T, :] * w[i:i + 1, :]
        out = term if out is None else out + term
    return out


def _tri_masks(C):
    r = lax.broadcasted_iota(jnp.int32, (C, C), 0)
    c = lax.broadcasted_iota(jnp.int32, (C, C), 1)
    return r == c, r >= c, r > c, r <= c


_NN, _NT, _TN = ((1,), (0,)), ((1,), (1,)), ((0,), (0,))
_GDN_PASSES = dict(qk=1, inv=1, sol=1, scan=1, bwd=1)


def _bdot_raw(a, b, kind, passes):
    dims = ({"NN": ((2,), (1,)), "NT": ((2,), (2,)), "TN": ((1,), (1,))}[kind], ((0,), (0,)))
    if passes == 0:
        return lax.dot_general(a, b, dims, precision=_HI, preferred_element_type=F32)
    ah, bh = a.astype(BF16), b.astype(BF16)
    out = lax.dot_general(ah, bh, dims, preferred_element_type=F32)
    if passes == 3:
        al, bl = (a - ah.astype(F32)).astype(BF16), (b - bh.astype(F32)).astype(BF16)
        out = out + lax.dot_general(ah, bl, dims, preferred_element_type=F32) + lax.dot_general(al, bh, dims, preferred_element_type=F32)
    return out


@functools.partial(jax.custom_vjp, nondiff_argnums=(2, 3))
def _bdot(a, b, kind, passes):
    return _bdot_raw(a, b, kind, passes)


def _bdot_fwd(a, b, kind, passes):
    return _bdot_raw(a, b, kind, passes), (a, b)


def _bdot_bwd(kind, passes, res, ct):
    a, b = res
    if kind == "NN":
        return _bdot_raw(ct, b, "NT", passes), _bdot_raw(a, ct, "TN", passes)
    if kind == "NT":
        return _bdot_raw(ct, b, "NN", passes), _bdot_raw(ct, a, "TN", passes)
    return _bdot_raw(b, ct, "NT", passes), _bdot_raw(a, ct, "NN", passes)


_bdot.defvjp(_bdot_fwd, _bdot_bwd)


def _softplus(x):
    return jnp.maximum(x, 0.0) + jnp.log(1.0 + jnp.exp(-jnp.abs(x)))


def _gdn_stage1(cq, ck, cv, b_col, a_col, alog, dtb, dot=_bdot_raw):
    C = cq.shape[1]
    eye, incl, strict, incl_t = _tri_masks(C)
    qn = cq * lax.rsqrt(jnp.sum(cq * cq, axis=-1, keepdims=True) + EPS) * (GDN_DIM ** -0.5)
    kn = ck * lax.rsqrt(jnp.sum(ck * ck, axis=-1, keepdims=True) + EPS)
    beta = jax.nn.sigmoid(b_col)
    g = -jnp.exp(alog) * _softplus(a_col + dtb)
    g_row = jnp.sum(jnp.where(eye, g, 0.0), axis=1, keepdims=True)
    beta_row = jnp.sum(jnp.where(eye, beta, 0.0), axis=1, keepdims=True)
    gc_col = jnp.sum(jnp.where(incl, g_row, 0.0), axis=2, keepdims=True)
    gc_row = jnp.sum(jnp.where(incl_t, g, 0.0), axis=1, keepdims=True)
    dec = jnp.where(incl, jnp.exp(jnp.where(incl, gc_col - gc_row, 0.0)), 0.0)
    kk = dot(kn, kn, "NT", _GDN_PASSES["qk"])
    qk = dot(qn, kn, "NT", _GDN_PASSES["qk"])
    lmat = jnp.where(strict, dec * kk * beta_row, 0.0)
    attn = dec * qk * beta_row
    gam = jnp.exp(gc_col)
    gc_last = gc_col[:, C - 1:C, :]
    k_end = kn * (jnp.exp(gc_last - gc_col) * beta)
    return lmat, cv, gam * kn, gam * qn, attn, k_end, jnp.exp(gc_last)


def _tri_inv(lmat):
    C = lmat.shape[1]
    eye = _tri_masks(C)[0]
    ps = _GDN_PASSES["inv"]
    p = jnp.where(eye, 1.0, 0.0) - lmat
    lp = _bdot_raw(lmat, lmat, "NN", ps)
    n = int(math.log2(C))
    for s in range(1, n):
        p = p + _bdot_raw(p, lp, "NN", ps)
        if s < n - 1:
            lp = _bdot_raw(lp, lp, "NN", ps)
    return p


def _gated_norm(o, z, gnw):
    on = o * lax.rsqrt(jnp.mean(o * o, axis=-1, keepdims=True) + EPS) * gnw
    return on * _silu(z)


GDN_PG = 2
GDN_SG = 4


def _gdn_pairs(c, ba, gp, G):
    C, W, H = GDN_CHUNK, GDN_WIDTH, GDN_HEADS
    pairs = [(j, h) for j in range(G) for h in range(H)]
    cq, ck, cv = (jnp.stack([c[C * j:C * (j + 1), o + GDN_DIM * h:o + GDN_DIM * (h + 1)] for j, h in pairs]) for o in (0, W, 2 * W))
    b_col = jnp.stack([ba[C * j:C * (j + 1), h:h + 1] for j, h in pairs])
    a_col = jnp.stack([ba[C * j:C * (j + 1), H + h:H + h + 1] for j, h in pairs])
    alog = jnp.stack([gp[0:1, h:h + 1] for j, h in pairs])
    dtb = jnp.stack([gp[0:1, H + h:H + h + 1] for j, h in pairs])
    return pairs, (cq, ck, cv, b_col, a_col, alog, dtb)


def _gdn_pre_specs(S, G):
    C = GDN_CHUNK
    T = C * G
    return dict(
        cur=pl.BlockSpec((T, 3 * GDN_WIDTH), lambda i: (i, 0)),
        prev=pl.BlockSpec((8, 3 * GDN_WIDTH), lambda i: (jnp.maximum(i * (T // 8) - 1, 0), 0)),
        ba=pl.BlockSpec((T, 128), lambda i: (i, P_BA // 128)),
        cw=pl.BlockSpec((GDN_CONV, 3 * GDN_WIDTH), lambda i: (0, 0)),
        vec=pl.BlockSpec((1, 128), lambda i: (0, 0)),
        hd=pl.BlockSpec((GDN_HEADS, T, GDN_DIM), lambda i: (0, i, 0)),
        hc=pl.BlockSpec((GDN_HEADS, T, C), lambda i: (0, i, 0)),
        ge=pl.BlockSpec((G, GDN_HEADS, 8, 128), lambda i: (i, 0, 0, 0)),
    )


def _hd_shape(S, last=GDN_DIM):
    return jax.ShapeDtypeStruct((GDN_HEADS, S, last), F32)


def _gdn_pre(proj, conv_w, gp):
    S = proj.shape[0]
    C, G = GDN_CHUNK, GDN_PG
    nc = S // C
    sp = _gdn_pre_specs(S, G)

    def body(cur_ref, prev_ref, ba_ref, cw_ref, gp_ref, uv_ref, wk_ref, qd_ref, ke_ref, at_ref, ti_ref, ge_ref):
        prev = prev_ref[...] * jnp.where(pl.program_id(0) == 0, 0.0, 1.0)
        c = _silu(_conv_taps(jnp.concatenate([prev, cur_ref[...]], axis=0), cw_ref[...], GDN_CONV, C * G))
        pairs, args = _gdn_pairs(c, ba_ref[...], gp_ref[...], G)
        lmat, v, rk, q_dec, attn, k_end, g_end = _gdn_stage1(*args)
        t = _tri_inv(lmat)
        u_v = _bdot_raw(t, v, "NN", _GDN_PASSES["sol"])
        w_k = _bdot_raw(t, rk, "NN", _GDN_PASSES["sol"])
        for b, (j, h) in enumerate(pairs):
            rows = slice(C * j, C * (j + 1))
            uv_ref[h, rows, :] = u_v[b]
            wk_ref[h, rows, :] = w_k[b]
            qd_ref[h, rows, :] = q_dec[b]
            ke_ref[h, rows, :] = k_end[b]
            at_ref[h, rows, :] = attn[b]
            ti_ref[h, rows, :] = t[b]
            ge_ref[j, h] = jnp.broadcast_to(g_end[b], (8, 128))

    return pl.pallas_call(
        body, name="gdn_pre", grid=(nc // G,),
        in_specs=[sp["cur"], sp["prev"], sp["ba"], sp["cw"], sp["vec"]],
        out_specs=(sp["hd"], sp["hd"], sp["hd"], sp["hd"], sp["hc"], sp["hc"], sp["ge"]),
        out_shape=(_hd_shape(S), _hd_shape(S), _hd_shape(S), _hd_shape(S), _hd_shape(S, C), _hd_shape(S, C),
                   jax.ShapeDtypeStruct((nc, GDN_HEADS, 8, 128), F32)),
        compiler_params=_cparams(("parallel",)),
    )(proj, proj, proj, conv_w, gp)


def _gdn_scan_specs(S, G, rev):
    C = GDN_CHUNK
    T = C * G
    n = S // T
    ci = (lambda i: n - 1 - i) if rev else (lambda i: i)
    return dict(
        hd=pl.BlockSpec((GDN_HEADS, T, GDN_DIM), lambda i: (0, ci(i), 0)),
        hc=pl.BlockSpec((GDN_HEADS, T, C), lambda i: (0, ci(i), 0)),
        ge=pl.BlockSpec((G, GDN_HEADS, 8, 128), lambda i: (ci(i), 0, 0, 0)),
        z=pl.BlockSpec((T, GDN_WIDTH), lambda i: (ci(i), P_Z // GDN_WIDTH)),
        oa=pl.BlockSpec((T, GDN_WIDTH), lambda i: (ci(i), 0)),
        vec=pl.BlockSpec((1, 128), lambda i: (0, 0)),
        st=pl.BlockSpec((G, GDN_HEADS, GDN_DIM, GDN_DIM), lambda i: (ci(i), 0, 0, 0)),
    )


def _gdn_scan(u_v, w_k, q_dec, k_end, attn, g_end, proj, gnw):
    S = proj.shape[0]
    C, G = GDN_CHUNK, GDN_SG
    nc = S // C
    sp = _gdn_scan_specs(S, G, False)
    ps = _GDN_PASSES["scan"]

    def body(uv_ref, wk_ref, qd_ref, ke_ref, at_ref, ge_ref, z_ref, gnw_ref, oa_ref, st_ref, s_scr):
        @pl.when(pl.program_id(0) == 0)
        def _():
            s_scr[...] = jnp.zeros_like(s_scr)

        for j in range(G):
            rows = slice(C * j, C * (j + 1))
            st = s_scr[...]
            st_ref[j] = st
            u = uv_ref[:, rows, :] - _bdot_raw(wk_ref[:, rows, :], st, "NN", ps)
            o = _bdot_raw(qd_ref[:, rows, :], st, "NN", ps) + _bdot_raw(at_ref[:, rows, :], u, "NN", ps)
            s_scr[...] = ge_ref[j][:, 0:1, 0:1] * st + _bdot_raw(ke_ref[:, rows, :], u, "TN", ps)
            for h in range(GDN_HEADS):
                cols = slice(GDN_DIM * h, GDN_DIM * (h + 1))
                oa_ref[rows, cols] = _gated_norm(o[h], z_ref[rows, cols], gnw_ref[...])

    return pl.pallas_call(
        body, name="gdn_scan", grid=(nc // G,),
        in_specs=[sp["hd"], sp["hd"], sp["hd"], sp["hd"], sp["hc"], sp["ge"], sp["z"], sp["vec"]],
        out_specs=(sp["oa"], sp["st"]),
        out_shape=(jax.ShapeDtypeStruct((S, GDN_WIDTH + DIL_WIDTH), F32),
                   jax.ShapeDtypeStruct((nc, GDN_HEADS, GDN_DIM, GDN_DIM), F32)),
        scratch_shapes=[pltpu.VMEM((GDN_HEADS, GDN_DIM, GDN_DIM), F32)],
        compiler_params=_cparams(("arbitrary",)),
    )(u_v, w_k, q_dec, k_end, attn, g_end, proj, gnw)


def _gdn_scan_bwd(u_v, w_k, q_dec, k_end, attn, g_end, proj, gnw, states, d_oa):
    S = proj.shape[0]
    C, G = GDN_CHUNK, GDN_SG
    nc = S // C
    sp = _gdn_scan_specs(S, G, True)
    ps, pb = _GDN_PASSES["scan"], _GDN_PASSES["bwd"]

    def body(uv_ref, wk_ref, qd_ref, ke_ref, at_ref, ge_ref, z_ref, gnw_ref, st_ref, doa_ref,
             duv_ref, dwk_ref, dqd_ref, dke_ref, dat_ref, dge_ref, dz_ref, dgnw_ref, ds_scr):
        @pl.when(pl.program_id(0) == 0)
        def _():
            ds_scr[...] = jnp.zeros_like(ds_scr)
            dgnw_ref[...] = jnp.zeros_like(dgnw_ref)

        dgnw = jnp.zeros((1, 128), F32)
        for j in reversed(range(G)):
            rows = slice(C * j, C * (j + 1))
            st = st_ref[j]
            wk, qd, ke, at = wk_ref[:, rows, :], qd_ref[:, rows, :], ke_ref[:, rows, :], at_ref[:, rows, :]
            u = uv_ref[:, rows, :] - _bdot_raw(wk, st, "NN", ps)
            o = _bdot_raw(qd, st, "NN", ps) + _bdot_raw(at, u, "NN", ps)
            dos = []
            for h in range(GDN_HEADS):
                cols = slice(GDN_DIM * h, GDN_DIM * (h + 1))
                _, vjp2 = jax.vjp(_gated_norm, o[h], z_ref[rows, cols], gnw_ref[...])
                do_h, dz_h, dgn = vjp2(doa_ref[rows, cols])
                dz_ref[rows, cols] = dz_h
                dgnw = dgnw + dgn
                dos.append(do_h)
            do = jnp.stack(dos)
            ds_new = ds_scr[...]
            du = _bdot_raw(at, do, "TN", pb) + _bdot_raw(ke, ds_new, "NN", pb)
            duv_ref[:, rows, :] = du
            dat_ref[:, rows, :] = _bdot_raw(do, u, "NT", pb)
            dqd_ref[:, rows, :] = _bdot_raw(do, st, "NT", pb)
            dke_ref[:, rows, :] = _bdot_raw(u, ds_new, "NT", pb)
            dwk_ref[:, rows, :] = -_bdot_raw(du, st, "NT", pb)
            d_ge = jnp.sum(jnp.sum(st * ds_new, axis=2, keepdims=True), axis=1, keepdims=True)
            dge_ref[j] = jnp.broadcast_to(d_ge, (GDN_HEADS, 8, 128))
            ds_scr[...] = ge_ref[j][:, 0:1, 0:1] * ds_new + _bdot_raw(qd, do, "TN", pb) - _bdot_raw(wk, du, "TN", pb)
        dgnw_ref[...] += dgnw

    return pl.pallas_call(
        body, name="gdn_scan_bwd", grid=(nc // G,),
        in_specs=[sp["hd"], sp["hd"], sp["hd"], sp["hd"], sp["hc"], sp["ge"], sp["z"], sp["vec"], sp["st"], sp["oa"]],
        out_specs=(sp["hd"], sp["hd"], sp["hd"], sp["hd"], sp["hc"], sp["ge"], sp["oa"], sp["vec"]),
        out_shape=(_hd_shape(S), _hd_shape(S), _hd_shape(S), _hd_shape(S), _hd_shape(S, C),
                   jax.ShapeDtypeStruct((nc, GDN_HEADS, 8, 128), F32), jax.ShapeDtypeStruct((S, GDN_WIDTH), F32),
                   jax.ShapeDtypeStruct((1, 128), F32)),
        scratch_shapes=[pltpu.VMEM((GDN_HEADS, GDN_DIM, GDN_DIM), F32)],
        compiler_params=_cparams(("arbitrary",)),
    )(u_v, w_k, q_dec, k_end, attn, g_end, proj, gnw, states, d_oa)


def _gdn_post(proj, conv_w, gp, tinv, u_v, w_k, d_uv, d_wk, d_qd, d_ke, d_at, d_ge):
    S = proj.shape[0]
    C, G = GDN_CHUNK, GDN_PG
    nc = S // C
    sp = _gdn_pre_specs(S, G)
    pb = _GDN_PASSES["bwd"]

    def body(cur_ref, prev_ref, ba_ref, cw_ref, gp_ref, ti_ref, uv_ref, wk_ref, duv_ref, dwk_ref, dqd_ref, dke_ref,
             dat_ref, dge_ref, dpre_ref, dba_ref, dgp_ref):
        i = pl.program_id(0)

        @pl.when(i == 0)
        def _():
            dgp_ref[...] = jnp.zeros_like(dgp_ref)

        prev = prev_ref[...] * jnp.where(i == 0, 0.0, 1.0)
        pre = _conv_taps(jnp.concatenate([prev, cur_ref[...]], axis=0), cw_ref[...], GDN_CONV, C * G)
        sg = jax.nn.sigmoid(pre)
        dsilu = sg * (1.0 + pre * (1.0 - sg))
        pairs, args = _gdn_pairs(pre * sg, ba_ref[...], gp_ref[...], G)
        _, vjp1 = jax.vjp(functools.partial(_gdn_stage1, dot=_bdot), *args)

        def take(ref):
            return jnp.stack([ref[h, C * j:C * (j + 1), :] for j, h in pairs])

        t, u_v, w_k = take(ti_ref), take(uv_ref), take(wk_ref)
        d_v = _bdot_raw(t, take(duv_ref), "TN", pb)
        d_rk = _bdot_raw(t, take(dwk_ref), "TN", pb)
        d_l = -(_bdot_raw(d_v, u_v, "NT", pb) + _bdot_raw(d_rk, w_k, "NT", pb))
        d_ge = jnp.stack([dge_ref[j, h][0:1, 0:1] for j, h in pairs])
        dcq, dck, dcv, db, da, dalog, ddtb = vjp1((d_l, d_v, d_rk, take(dqd_ref), take(dat_ref), take(dke_ref), d_ge))
        lane = lax.broadcasted_iota(jnp.int32, (C, 128), 1)
        lane1 = lax.broadcasted_iota(jnp.int32, (1, 128), 1)
        dgp = jnp.zeros((1, 128), F32)
        for j in range(G):
            rows = slice(C * j, C * (j + 1))
            dba = jnp.zeros((C, 128), F32)
            for h in range(GDN_HEADS):
                b = GDN_HEADS * j + h
                for o_, dcx in ((0, dcq), (GDN_WIDTH, dck), (2 * GDN_WIDTH, dcv)):
                    cols = slice(o_ + GDN_DIM * h, o_ + GDN_DIM * (h + 1))
                    dpre_ref[rows, cols] = dcx[b] * dsilu[rows, cols]
                dba = dba + jnp.where(lane == h, db[b], 0.0) + jnp.where(lane == GDN_HEADS + h, da[b], 0.0)
                dgp = dgp + jnp.where(lane1 == h, dalog[b], 0.0) + jnp.where(lane1 == GDN_HEADS + h, ddtb[b], 0.0)
            dba_ref[rows, :] = dba
        dgp_ref[0:1, :] += dgp

    T = C * G
    return pl.pallas_call(
        body, name="gdn_post", grid=(nc // G,),
        in_specs=[sp["cur"], sp["prev"], sp["ba"], sp["cw"], sp["vec"], sp["hc"], sp["hd"], sp["hd"], sp["hd"], sp["hd"],
                  sp["hd"], sp["hd"], sp["hc"], sp["ge"]],
        out_specs=(sp["cur"], pl.BlockSpec((T, 128), lambda i: (i, 0)), pl.BlockSpec((8, 128), lambda i: (0, 0))),
        out_shape=(jax.ShapeDtypeStruct((S, 3 * GDN_WIDTH), F32), jax.ShapeDtypeStruct((S, 128), F32),
                   jax.ShapeDtypeStruct((8, 128), F32)),
        compiler_params=_cparams(("arbitrary",)),
    )(proj, proj, proj, conv_w, gp, tinv, u_v, w_k, d_uv, d_wk, d_qd, d_ke, d_at, d_ge)


def _conv_bwd(dpre, x, xcol0, w, K, name, tc):
    S, Cc = dpre.shape
    T = _pick_tile(S, 256)
    nt, ncol = S // T, Cc // tc
    xo = xcol0 // tc

    def body(d_ref, dn_ref, x_ref, xp_ref, w_ref, dx_ref, dw_ref):
        i = pl.program_id(1)
        dn = dn_ref[...] * jnp.where(i == nt - 1, 0.0, 1.0)
        dv = d_ref[...]
        ext_d = jnp.concatenate([dv, dn], axis=0)
        wv = w_ref[...]
        dx_ref[...] = _conv_taps_t(ext_d, wv, K, T).astype(dx_ref.dtype)
        xp = xp_ref[...] * jnp.where(i == 0, 0.0, 1.0)
        ext_x = jnp.concatenate([xp, x_ref[...]], axis=0)

        @pl.when(i == 0)
        def _():
            dw_ref[...] = jnp.zeros_like(dw_ref)

        for k in range(K):
            lo = 8 - (K - 1) + k
            dw_ref[k:k + 1, :] += jnp.sum(dv * ext_x[lo:lo + T, :], axis=0, keepdims=True)

    r8 = T // 8
    return pl.pallas_call(
        body, name=name, grid=(ncol, nt),
        in_specs=[pl.BlockSpec((T, tc), lambda j, i: (i, j)),
                  pl.BlockSpec((8, tc), lambda j, i: (jnp.minimum((i + 1) * r8, S // 8 - 1), j)),
                  pl.BlockSpec((T, tc), lambda j, i: (i, j + xo)),
                  pl.BlockSpec((8, tc), lambda j, i: (jnp.maximum(i * r8 - 1, 0), j + xo)),
                  pl.BlockSpec((K, tc), lambda j, i: (0, j))],
        out_specs=(pl.BlockSpec((T, tc), lambda j, i: (i, j)), pl.BlockSpec((K, tc), lambda j, i: (0, j))),
        out_shape=(jax.ShapeDtypeStruct((S, Cc), _MXU), jax.ShapeDtypeStruct((K, Cc), F32)),
        compiler_params=_cparams(("parallel", "arbitrary")),
    )(dpre, dpre, x, x, w)


def _dil_bias(nt):
    T = ATT_T
    d = (np.arange(nt)[:, None, None] * T + np.arange(T)[None, None, :] - np.arange(T)[None, :, None])
    cnt = ((d >= 0) & (d <= 128)).astype(np.float64) + ((d >= 0) & (d % 4 == 0) & (d <= 512)) + ((d >= 0) & (d % 16 == 0))
    return jnp.asarray(np.where(cnt > 0, np.log(np.maximum(cnt, 1.0)), -1e30), dtype=F32)


def _attn_fwd(proj, mix):
    S = proj.shape[0]
    T = ATT_T
    nt = S // T
    bias = _dil_bias(nt)
    scale = DIL_DIM ** -0.5
    npair = DIL_WIDTH // 128
    qb0, kb0, vb0 = P_QKVB // 128, (P_QKVB + DIL_WIDTH) // 128, (P_QKVB + 2 * DIL_WIDTH) // 128

    def body(q_ref, k_ref, v_ref, b_ref, mix_ref, o_ref, lse_ref):
        i = pl.program_id(1)
        qs = (q_ref[...] * scale).astype(_MXU)

        def step(j, carry):
            kt = k_ref[pl.ds(pl.multiple_of(j * T, T), T), :].astype(_MXU)
            vt = v_ref[pl.ds(pl.multiple_of(j * T, T), T), :].astype(_MXU)
            bt = b_ref[i - j]
            out = []
            for hh in range(2):
                m, l, acc = carry[hh]
                sl = slice(hh * DIL_DIM, (hh + 1) * DIL_DIM)
                s = lax.dot_general(kt[:, sl], qs[:, sl], (_NT, ((), ())), preferred_element_type=F32) + bt
                m_new = jnp.maximum(m, jnp.max(s, axis=0, keepdims=True))
                p = jnp.exp(s - m_new)
                a = jnp.exp(m - m_new)
                l = a * l + jnp.sum(p, axis=0, keepdims=True)
                acc = a * acc + lax.dot_general(vt[:, sl], p.astype(_MXU), (_TN, ((), ())), preferred_element_type=F32)
                out.append((m_new, l, acc))
            return tuple(out)

        init = tuple((jnp.full((1, T), -1e30, F32), jnp.zeros((1, T), F32), jnp.zeros((DIL_DIM, T), F32)) for _ in range(2))
        res = lax.fori_loop(0, i + 1, step, init)
        lse_ref[...] = jnp.zeros_like(lse_ref)
        for hh in range(2):
            m, l, acc = res[hh]
            o_ref[:, hh * DIL_DIM:(hh + 1) * DIL_DIM] = (acc / l).T
            lse_ref[hh:hh + 1, :] = m + jnp.log(l)

    return pl.pallas_call(
        body, name="attn_fwd", grid=(npair, nt),
        in_specs=[pl.BlockSpec((T, 128), lambda p, i: (i, qb0 + p)),
                  pl.BlockSpec((S, 128), lambda p, i: (0, kb0 + p)),
                  pl.BlockSpec((S, 128), lambda p, i: (0, vb0 + p)),
                  pl.BlockSpec((nt, T, T), lambda p, i: (0, 0, 0)), pl.BlockSpec(memory_space=pl.ANY)],
        out_specs=(pl.BlockSpec((T, 128), lambda p, i: (i, GDN_WIDTH // 128 + p)),
                   pl.BlockSpec((None, None, 8, T), lambda p, i: (p, i, 0, 0))),
        out_shape=(jax.ShapeDtypeStruct(mix.shape, F32), jax.ShapeDtypeStruct((npair, nt, 8, T), F32)),
        input_output_aliases={4: 0},
        compiler_params=_cparams(("parallel", "parallel")),
    )(proj, proj, proj, bias, mix)


def _attn_bwd(proj, mix, lse, d_mix):
    S = proj.shape[0]
    T = ATT_T
    nt = S // T
    bias = _dil_bias(nt)
    scale = DIL_DIM ** -0.5
    npair = DIL_WIDTH // 128
    qb0, kb0, vb0 = P_QKVB // 128, (P_QKVB + DIL_WIDTH) // 128, (P_QKVB + 2 * DIL_WIDTH) // 128

    def body(q_ref, k_ref, v_ref, o_ref, lse_ref, do_ref, b_ref, dq_ref, dk_ref, dv_ref, dq_scr):
        j = pl.program_id(1)

        @pl.when(j == 0)
        def _():
            dq_scr[...] = jnp.zeros_like(dq_scr)

        kt = k_ref[...].astype(_MXU)
        vt = v_ref[...].astype(_MXU)
        ones = jnp.ones((8, DIL_DIM), F32)

        def step(i, carry):
            rows = pl.ds(pl.multiple_of(i * T, T), T)
            qs = (q_ref[rows, :] * scale).astype(_MXU)
            dov = do_ref[rows, :]
            prod = dov * o_ref[rows, :]
            lsev = lse_ref[i]
            dob = dov.astype(_MXU)
            bt = b_ref[i - j]
            out = []
            dqs = []
            for hh in range(2):
                dk, dv = carry[hh]
                sl = slice(hh * DIL_DIM, (hh + 1) * DIL_DIM)
                s = lax.dot_general(kt[:, sl], qs[:, sl], (_NT, ((), ())), preferred_element_type=F32) + bt
                p = jnp.exp(s - lsev[hh:hh + 1, :])
                delta = lax.dot_general(ones, prod[:, sl], (_NT, ((), ())), precision=_HI, preferred_element_type=F32)[0:1, :]
                dp = lax.dot_general(vt[:, sl], dob[:, sl], (_NT, ((), ())), preferred_element_type=F32)
                ds = (p * (dp - delta)).astype(_MXU)
                dv = dv + lax.dot_general(p.astype(_MXU), dob[:, sl], (_NN, ((), ())), preferred_element_type=F32)
                dk = dk + lax.dot_general(ds, qs[:, sl], (_NN, ((), ())), preferred_element_type=F32)
                dqs.append(lax.dot_general(ds, kt[:, sl], (_TN, ((), ())), preferred_element_type=F32) * scale)
                out.append((dk, dv))
            dq_scr[rows, :] += jnp.concatenate(dqs, axis=1)
            return tuple(out)

        init = tuple((jnp.zeros((T, DIL_DIM), F32), jnp.zeros((T, DIL_DIM), F32)) for _ in range(2))
        res = lax.fori_loop(j, nt, step, init)
        dk_ref[...] = jnp.concatenate([res[0][0], res[1][0]], axis=1).astype(dk_ref.dtype)
        dv_ref[...] = jnp.concatenate([res[0][1], res[1][1]], axis=1).astype(dv_ref.dtype)

        @pl.when(j == nt - 1)
        def _():
            dq_ref[...] = dq_scr[...].astype(dq_ref.dtype)

    full = lambda c0: pl.BlockSpec((S, 128), lambda p, j: (0, c0 + p))
    tile = lambda c0: pl.BlockSpec((T, 128), lambda p, j: (j, c0 + p))
    out3 = jax.ShapeDtypeStruct((S, DIL_WIDTH), _MXU)
    return pl.pallas_call(
        body, name="attn_bwd", grid=(npair, nt),
        in_specs=[full(qb0), tile(kb0), tile(vb0), full(GDN_WIDTH // 128),
                  pl.BlockSpec((None, nt, 8, T), lambda p, j: (p, 0, 0, 0)), full(GDN_WIDTH // 128),
                  pl.BlockSpec((nt, T, T), lambda p, j: (0, 0, 0))],
        out_specs=(full(0), tile(0), tile(0)),
        out_shape=(out3, out3, out3),
        scratch_shapes=[pltpu.VMEM((S, 128), F32)],
        compiler_params=_cparams(("parallel", "arbitrary")),
    )(proj, proj, proj, mix, lse, d_mix, bias)


def _ffn_act(up, cw):
    S, Cc = up.shape[0], up.shape[1] // 2
    T, tc = _pick_tile(S, 256), _pick_tile(Cc, 1536)
    r8 = T // 8
    nct = Cc // tc

    def body(g_ref, gp_ref, u_ref, up_ref, wg_ref, wu_ref, o_ref):
        keep = jnp.where(pl.program_id(1) == 0, 0.0, 1.0)
        cg = _conv_taps(jnp.concatenate([gp_ref[...] * keep, g_ref[...]], axis=0), wg_ref[...], FFN_CONV, T)
        cu = _conv_taps(jnp.concatenate([up_ref[...] * keep, u_ref[...]], axis=0), wu_ref[...], FFN_CONV, T)
        o_ref[...] = (_silu(cg) * cu).astype(o_ref.dtype)

    cur = lambda o: pl.BlockSpec((T, tc), lambda j, i: (i, j + o))
    prev = lambda o: pl.BlockSpec((8, tc), lambda j, i: (jnp.maximum(i * r8 - 1, 0), j + o))
    wsp = lambda o: pl.BlockSpec((FFN_CONV, tc), lambda j, i: (0, j + o))
    return pl.pallas_call(
        body, name="ffn_act", grid=(nct, S // T),
        in_specs=[cur(0), prev(0), cur(nct), prev(nct), wsp(0), wsp(nct)], out_specs=cur(0),
        out_shape=jax.ShapeDtypeStruct((S, Cc), _MXU),
        compiler_params=_cparams(("parallel", "parallel")),
    )(up, up, up, up, cw, cw)


def _ffn_act_bwd(d_act, up, cw):
    S, Cc = up.shape[0], up.shape[1] // 2
    T, tc = _pick_tile(S, 256), _pick_tile(Cc, 1536)
    r8 = T // 8
    nt = S // T
    nct = Cc // tc
    K = FFN_CONV

    def body(da_ref, dan_ref, g_ref, gp_ref, gn_ref, u_ref, up_ref, un_ref, wg_ref, wu_ref,
             dg_ref, du_ref, dwg_ref, dwu_ref):
        i = pl.program_id(1)
        keep_p = jnp.where(i == 0, 0.0, 1.0)
        keep_n = jnp.where(i == nt - 1, 0.0, 1.0)
        wg, wu = wg_ref[...], wu_ref[...]
        xg = jnp.concatenate([gp_ref[...] * keep_p, g_ref[...], gn_ref[...] * keep_n], axis=0)
        xu = jnp.concatenate([up_ref[...] * keep_p, u_ref[...], un_ref[...] * keep_n], axis=0)
        cg = _conv_taps(xg, wg, K, T + 8)
        cu = _conv_taps(xu, wu, K, T + 8)
        da = jnp.concatenate([da_ref[...], dan_ref[...] * keep_n], axis=0)
        sg = jax.nn.sigmoid(cg)
        d_cg = da * cu * (sg * (1.0 + cg * (1.0 - sg)))
        d_cu = da * (cg * sg)
        dg_ref[...] = _conv_taps_t(d_cg, wg, K, T).astype(dg_ref.dtype)
        du_ref[...] = _conv_taps_t(d_cu, wu, K, T).astype(du_ref.dtype)

        @pl.when(i == 0)
        def _():
            dwg_ref[...] = jnp.zeros_like(dwg_ref)
            dwu_ref[...] = jnp.zeros_like(dwu_ref)

        for k in range(K):
            lo = 8 - (K - 1) + k
            dwg_ref[k:k + 1, :] += jnp.sum(d_cg[0:T, :] * xg[lo:lo + T, :], axis=0, keepdims=True)
            dwu_ref[k:k + 1, :] += jnp.sum(d_cu[0:T, :] * xu[lo:lo + T, :], axis=0, keepdims=True)

    cur = lambda o: pl.BlockSpec((T, tc), lambda j, i: (i, j + o))
    prev = lambda o: pl.BlockSpec((8, tc), lambda j, i: (jnp.maximum(i * r8 - 1, 0), j + o))
    nxt = lambda o: pl.BlockSpec((8, tc), lambda j, i: (jnp.minimum((i + 1) * r8, S // 8 - 1), j + o))
    wsp = lambda o: pl.BlockSpec((K, tc), lambda j, i: (0, j + o))
    return pl.pallas_call(
        body, name="ffn_act_bwd", grid=(nct, nt),
        in_specs=[cur(0), nxt(0), cur(0), prev(0), nxt(0), cur(nct), prev(nct), nxt(nct), wsp(0), wsp(nct)],
        out_specs=(cur(0), cur(0), wsp(0), wsp(0)),
        out_shape=(jax.ShapeDtypeStruct((S, Cc), _MXU), jax.ShapeDtypeStruct((S, Cc), _MXU),
                   jax.ShapeDtypeStruct((K, Cc), F32), jax.ShapeDtypeStruct((K, Cc), F32)),
        compiler_params=_cparams(("parallel", "arbitrary")),
    )(d_act, d_act, up, up, up, up, up, up, cw, cw)


def _local_step(x, tgt, n1w, n2w, fnw, gp, gnw, wp, conv_w, fcw, rest_weights, early_grads):
    h1 = _rmsnorm_fwd(x, n1w, "norm1")
    proj = _mm(h1, wp, "nn", name="proj")
    u_v, w_k, q_dec, k_end, attn, tinv, g_end = _gdn_pre(proj, conv_w, gp)
    mix, states = _gdn_scan(u_v, w_k, q_dec, k_end, attn, g_end, proj, gnw)
    mix, lse = _attn_fwd(proj, mix)
    w_out, w_up4, w_down = rest_weights([mix])
    x2 = _mm(mix, w_out, "nn", residual=x, name="outproj")
    h2 = _rmsnorm_fwd(x2, n2w, "norm2")
    up = _mm(h2, w_up4, "nn", b_blocks=True, name="up")
    act = _ffn_act(up, fcw)
    x3 = _mm(act, w_down, "nn", residual=x2, name="down")
    loss, dx3, d_fnw = _loss_head(x3, fnw, tgt, "loss_head")
    d_act = _mm(dx3, w_down, "nt", name="d_act")
    d_wdown = _mm(act, dx3, "tn", name="d_wdown")
    d_upg, d_upu, d_fcwg, d_fcwu = _ffn_act_bwd(d_act, up, fcw)
    d_wup = _mm(h2, d_upg, "tn", place=("blocks", N_CHIPS, 0), tn=w_up4.shape[2], name="d_wgate")
    d_wup = _mm(h2, d_upu, "tn", place=("blocks", N_CHIPS, N_CHIPS // 2), tn=w_up4.shape[2], into=d_wup, name="d_wup")
    token = early_grads(d_wup, d_wdown)
    d_h2 = _mm_nt_blocks([d_upg, d_upu], w_up4, "d_h2")
    dx2, d_n2w = _rmsnorm_bwd(d_h2, x2, n2w + token[0:1, 0:1], dx3, "norm2_bwd")
    d_mix = _mm(dx2, w_out, "nt", name="d_mix")
    d_wout = _mm(mix, dx2, "tn", name="d_wout")
    dq_b, dk_b, dv_b = _attn_bwd(proj, mix, lse, d_mix)
    d_uv, d_wk, d_qd, d_ke, d_at, d_ge, d_z, d_gnw = _gdn_scan_bwd(u_v, w_k, q_dec, k_end, attn, g_end, proj, gnw, states, d_mix)
    d_pre, d_ba, d_gp = _gdn_post(proj, conv_w, gp, tinv, u_v, w_k, d_uv, d_wk, d_qd, d_ke, d_at, d_ge)
    d_qkva, d_convw = _conv_bwd(d_pre, proj, 0, conv_w, GDN_CONV, "gdn_conv_bwd", 512)
    d_proj = jnp.concatenate([d_qkva, d_z.astype(_MXU), dq_b, dk_b, dv_b, d_ba.astype(_MXU),
                              jnp.zeros((x.shape[0], P_COLS - P_BA - 128), _MXU)], axis=1)
    d_wp = _mm(h1, d_proj, "tn", name="d_wp")
    d_h1 = _mm(d_proj, wp, "nt", name="d_h1")
    dx, d_n1w = _rmsnorm_bwd(d_h1, x, n1w, dx2, "norm1_bwd")
    grads = dict(wp=d_wp, conv_w=d_convw, w_out=d_wout, w_up=d_wup, fcw_g=d_fcwg, fcw_u=d_fcwu, w_down=d_wdown,
                 n1w=d_n1w, n2w=d_n2w, fnw=d_fnw, gp=d_gp, gnw=d_gnw)
    return loss, dx, grads


_HBM = pl.BlockSpec(memory_space=pltpu.HBM)


def _pos():
    return lax.axis_index("x"), lax.axis_index("y"), lax.axis_index("c")


def _other_chips(x, y):
    return [(1 - x, y), (x, 1 - y), (1 - x, 1 - y)]


def _halvable(shape):
    return shape[0] % 32 == 0


def _rows_of_half(shape, half):
    if not _halvable(shape):
        return pl.ds(0, shape[0])
    return pl.ds(pl.multiple_of(half * (shape[0] // 2), 16), shape[0] // 2)


def _gather_halves(shards, name):
    n = len(shards)
    shapes = [s.shape for s in shards]

    def body(*refs):
        ins, outs = refs[:n], refs[n:2 * n]
        send_sems, recv_sems = refs[2 * n:]
        x, y, c = _pos()
        q = 2 * x + y
        chips = _other_chips(x, y)

        def copy(a, j, block):
            px, py = chips[j]
            rows = _rows_of_half(shapes[a], c)
            return pltpu.make_async_remote_copy(
                src_ref=ins[a].at[rows, :], dst_ref=outs[a].at[block, rows, :], send_sem=send_sems.at[3 * a + j],
                recv_sem=recv_sems.at[3 * a + j], device_id=(px, py, c), device_id_type=MESH)

        sends = [copy(a, j, q) for a in range(n) for j in range(3)]
        for cp in sends:
            cp.start()
        for a in range(n):
            for j, (px, py) in enumerate(chips):
                copy(a, j, 2 * px + py).wait_recv()
        for cp in sends:
            cp.wait_send()

    return pl.pallas_call(
        body, name=name, in_specs=[_HBM] * n, out_specs=[_HBM] * n,
        out_shape=[jax.ShapeDtypeStruct((N_CHIPS,) + s.shape, s.dtype) for s in shards],
        scratch_shapes=[pltpu.SemaphoreType.DMA((3 * n,)), pltpu.SemaphoreType.DMA((3 * n,))],
    )(*shards)


_SEM = pl.BlockSpec(memory_space=pltpu.SEMAPHORE)
_ANY = pl.BlockSpec(memory_space=pl.ANY)
_DATAFLOW = pltpu.SideEffectType.DATAFLOW_SIDE_EFFECTING


def _in_hbm(a):
    return pltpu.with_memory_space_constraint(a, pltpu.HBM)


def _halves_copy(src_refs, land_refs, send_sems, recv_sems, shapes, a, j, block, x, y, c):
    px, py = _other_chips(x, y)[j]
    rows = _rows_of_half(shapes[a], c)
    return pltpu.make_async_remote_copy(
        src_ref=src_refs[a].at[rows, :], dst_ref=land_refs[a].at[block, rows, :], send_sem=send_sems.at[3 * a + j],
        recv_sem=recv_sems.at[3 * a + j], device_id=(px, py, c), device_id_type=MESH)


def _gather_halves_start(shards, after, name):
    n = len(shards)
    shapes = [s.shape for s in shards]

    def body(*refs):
        ins, lands = refs[:n], refs[n:2 * n]
        send_sems, recv_sems = refs[2 * n + 1], refs[2 * n + 2]
        token = refs[-1]
        x, y, c = _pos()
        q = 2 * x + y
        for a in range(n):
            for j in range(3):
                _halves_copy(ins, lands, send_sems, recv_sems, shapes, a, j, q, x, y, c).start()
        token[...] = jnp.zeros_like(token)

    land_shapes = [(N_CHIPS,) + s.shape for s in shards]
    return pl.pallas_call(
        body, name=name,
        out_shape=(pltpu.SemaphoreType.DMA((3 * n,)), pltpu.SemaphoreType.DMA((3 * n,)),
                   *[pltpu.HBM(s.shape, s.dtype) for s in shards],
                   *[pltpu.HBM(ls, s.dtype) for ls, s in zip(land_shapes, shards)],
                   jax.ShapeDtypeStruct((8, 128), F32)),
        in_specs=[_HBM] * (2 * n) + [_ANY],
        out_specs=(_SEM, _SEM, *[_HBM] * (2 * n), pl.BlockSpec(memory_space=pltpu.VMEM)),
        input_output_aliases={a: 2 + a for a in range(2 * n)},
        compiler_params=pltpu.CompilerParams(has_side_effects=_DATAFLOW),
    )(*[_in_hbm(s) for s in shards], *[_in_hbm(lax.empty(ls, s.dtype)) for ls, s in zip(land_shapes, shards)], after)


def _gather_halves_wait(started, after, name):
    send_sems, recv_sems, *thru = started
    n = len(thru) // 2
    shapes = [t.shape for t in thru[:n]]

    def body(*refs):
        ins, lands = refs[:n], refs[n:2 * n]
        send_sems, recv_sems = refs[2 * n], refs[2 * n + 1]
        x, y, c = _pos()
        q = 2 * x + y
        chips = _other_chips(x, y)
        for a in range(n):
            for j, (px, py) in enumerate(chips):
                _halves_copy(ins, lands, send_sems, recv_sems, shapes, a, j, q, x, y, c).wait_send()
                _halves_copy(ins, lands, send_sems, recv_sems, shapes, a, j, 2 * px + py, x, y, c).wait_recv()

    outs = pl.pallas_call(
        body, name=name, out_shape=[pltpu.HBM(t.shape, t.dtype) for t in thru],
        in_specs=[_HBM] * (2 * n) + [_SEM, _SEM] + [_ANY] * len(after), out_specs=[_HBM] * (2 * n),
        input_output_aliases={a: a for a in range(2 * n)},
        compiler_params=pltpu.CompilerParams(has_side_effects=_DATAFLOW),
    )(*thru, send_sems, recv_sems, *after)
    return outs[n:]


def _sibling_fill(gathered, name):
    big = [a for a, g in enumerate(gathered) if _halvable(g.shape[1:])]
    n = len(gathered)

    def body(*refs):
        ins, outs = refs[:n], refs[n:2 * n]
        send_sems, recv_sems = refs[2 * n:]
        x, y, c = _pos()
        chips = _other_chips(x, y)

        def copy(k, j, half):
            a = big[k]
            px, py = chips[j]
            rows = _rows_of_half(gathered[a].shape[1:], half)
            return pltpu.make_async_remote_copy(
                src_ref=ins[a].at[2 * px + py, rows, :], dst_ref=outs[a].at[2 * px + py, rows, :],
                send_sem=send_sems.at[3 * k + j], recv_sem=recv_sems.at[3 * k + j],
                device_id=(x, y, 1 - c), device_id_type=MESH)

        sends = [copy(k, j, c) for k in range(len(big)) for j in range(3)]
        for cp in sends:
            cp.start()
        for k in range(len(big)):
            for j in range(3):
                copy(k, j, 1 - c).wait_recv()
        for cp in sends:
            cp.wait_send()

    return pl.pallas_call(
        body, name=name, in_specs=[_HBM] * n, out_specs=[_HBM] * n,
        out_shape=[jax.ShapeDtypeStruct(g.shape, g.dtype) for g in gathered],
        input_output_aliases={a: a for a in range(n)},
        scratch_shapes=[pltpu.SemaphoreType.DMA((3 * len(big),)), pltpu.SemaphoreType.DMA((3 * len(big),))],
    )(*gathered)


def _place_own(shards, gathered, cq, name):
    n = len(shards)
    steps = 4

    def body(cq_ref, *refs):
        for a in range(n):
            refs[2 * n + a][...] = refs[a][...]

    def tile(shape):
        return shape[0] // steps if _halvable(shape) else shape[0]

    in_specs = [pl.BlockSpec((tile(s.shape), s.shape[1]), (lambda i, s_: (i, 0)) if _halvable(s.shape) else (lambda i, s_: (0, 0)))
                for s in shards]
    in_specs += [pl.BlockSpec(memory_space=pl.ANY)] * n
    out_specs = [pl.BlockSpec((None, tile(s.shape), s.shape[1]),
                              (lambda i, s_: (s_[1], i, 0)) if _halvable(s.shape) else (lambda i, s_: (s_[1], 0, 0)))
                 for s in shards]
    gs = pltpu.PrefetchScalarGridSpec(num_scalar_prefetch=1, grid=(steps,), in_specs=in_specs, out_specs=out_specs)
    return pl.pallas_call(
        body, name=name, grid_spec=gs, out_shape=[jax.ShapeDtypeStruct(g.shape, g.dtype) for g in gathered],
        input_output_aliases={1 + n + a: a for a in range(n)},
        compiler_params=_cparams(("arbitrary",)),
    )(cq, *shards, *gathered)


def _half_rows(ref, c, rh):
    return ref.at[:, pl.ds(pl.multiple_of(c * rh, 8), rh), :]


def _grad_sibling(fams, small, name):
    n = len(fams)
    ns = 0 if small is None else 1
    rhs = [f.shape[1] // 2 for f in fams]

    def body(*refs):
        ins, outs = refs[:n], refs[n + ns:2 * n + ns]
        send_sems, recv_sems = refs[2 * (n + ns)], refs[2 * (n + ns) + 1]
        x, y, c = _pos()
        bigs = [pltpu.make_async_remote_copy(src_ref=_half_rows(ins[a], 1 - c, rhs[a]), dst_ref=outs[a],
                                             send_sem=send_sems.at[7 * ns + a], recv_sem=recv_sems.at[7 * ns + a],
                                             device_id=(x, y, 1 - c), device_id_type=MESH) for a in range(n)]
        for cp in bigs:
            cp.start()
        sends = []
        if ns:
            small_ref, all_ref, loc_sem = refs[n], refs[2 * n + 1], refs[2 * n + 4]
            me = 4 * x + 2 * y + c
            mine = pltpu.make_async_copy(small_ref, all_ref.at[me], loc_sem)
            mine.start()

            def peer(r):
                dx, dy, dc = (r >> 2) & 1, (r >> 1) & 1, r & 1
                return (x if dx == 0 else 1 - x), (y if dy == 0 else 1 - y), (c if dc == 0 else 1 - c)

            def small_copy(r, slot):
                return pltpu.make_async_remote_copy(src_ref=small_ref, dst_ref=all_ref.at[slot], send_sem=send_sems.at[r - 1],
                                                    recv_sem=recv_sems.at[r - 1], device_id=peer(r), device_id_type=MESH)

            sends = [small_copy(r, me) for r in range(1, 8)]
            for cp in sends:
                cp.start()
            for r in range(1, 8):
                px, py, pc = peer(r)
                small_copy(r, 4 * px + 2 * py + pc).wait_recv()
        for cp in bigs:
            cp.wait_recv()
        for cp in bigs + sends:
            cp.wait_send()
        if ns:
            mine.wait()

    return pl.pallas_call(
        body, name=name, in_specs=[_HBM] * (n + ns), out_specs=[_HBM] * (n + ns),
        out_shape=[jax.ShapeDtypeStruct((f.shape[0], f.shape[1] // 2, f.shape[2]), f.dtype) for f in fams]
        + ([jax.ShapeDtypeStruct((8,) + small.shape, small.dtype)] if ns else []),
        scratch_shapes=[pltpu.SemaphoreType.DMA((7 * ns + n,)), pltpu.SemaphoreType.DMA((7 * ns + n,))]
        + ([pltpu.SemaphoreType.DMA] if ns else []),
    )(*fams, *([small] if ns else []))


def _chips_copy(src_refs, land_refs, send_sems, recv_sems, a, j, x, y, c):
    px, py = _other_chips(x, y)[j]
    return pltpu.make_async_remote_copy(src_ref=src_refs[a].at[2 * px + py], dst_ref=land_refs[a].at[j],
                                        send_sem=send_sems.at[3 * a + j], recv_sem=recv_sems.at[3 * a + j],
                                        device_id=(px, py, c), device_id_type=MESH)


def _grad_chips_start(parts, name):
    n = len(parts)

    def body(*refs):
        ins, lands = refs[:n], refs[n:2 * n]
        send_sems, recv_sems = refs[2 * n], refs[2 * n + 1]
        token = refs[-1]
        x, y, c = _pos()
        for a in range(n):
            for j in range(3):
                _chips_copy(ins, lands, send_sems, recv_sems, a, j, x, y, c).start()
        token[...] = jnp.zeros_like(token)

    land_shapes = [(3,) + p.shape[1:] for p in parts]
    return pl.pallas_call(
        body, name=name,
        out_shape=(pltpu.SemaphoreType.DMA((3 * n,)), pltpu.SemaphoreType.DMA((3 * n,)),
                   *[pltpu.HBM(p.shape, p.dtype) for p in parts],
                   *[pltpu.HBM(ls, p.dtype) for ls, p in zip(land_shapes, parts)],
                   jax.ShapeDtypeStruct((8, 128), F32)),
        in_specs=[_HBM] * (2 * n),
        out_specs=(_SEM, _SEM, *[_HBM] * (2 * n), pl.BlockSpec(memory_space=pltpu.VMEM)),
        input_output_aliases={a: 2 + a for a in range(2 * n)},
        compiler_params=pltpu.CompilerParams(has_side_effects=_DATAFLOW),
    )(*[_in_hbm(p) for p in parts], *[_in_hbm(lax.empty(ls, p.dtype)) for ls, p in zip(land_shapes, parts)])


def _grad_chips_wait(started, after, name):
    send_sems, recv_sems, *thru = started
    n = len(thru) // 2

    def body(*refs):
        ins, lands = refs[:n], refs[n:2 * n]
        send_sems, recv_sems = refs[2 * n], refs[2 * n + 1]
        x, y, c = _pos()
        for a in range(n):
            for j in range(3):
                cp = _chips_copy(ins, lands, send_sems, recv_sems, a, j, x, y, c)
                cp.wait_send()
                cp.wait_recv()

    outs = pl.pallas_call(
        body, name=name, out_shape=[pltpu.HBM(t.shape, t.dtype) for t in thru],
        in_specs=[_HBM] * (2 * n) + [_SEM, _SEM] + [_ANY] * len(after), out_specs=[_HBM] * (2 * n),
        input_output_aliases={a: a for a in range(2 * n)},
        compiler_params=pltpu.CompilerParams(has_side_effects=_DATAFLOW),
    )(*thru, send_sems, recv_sems, *after)
    return outs[n:]


def _grad_chips(parts):
    n = len(parts)

    def body(*refs):
        ins, outs = refs[:n], refs[n:2 * n]
        send_sems, recv_sems = refs[2 * n:]
        x, y, c = _pos()
        chips = _other_chips(x, y)

        def copy(a, j):
            px, py = chips[j]
            return pltpu.make_async_remote_copy(src_ref=ins[a].at[2 * px + py], dst_ref=outs[a].at[j],
                                                send_sem=send_sems.at[3 * a + j], recv_sem=recv_sems.at[3 * a + j],
                                                device_id=(px, py, c), device_id_type=MESH)

        sends = [copy(a, j) for a in range(n) for j in range(3)]
        for cp in sends:
            cp.start()
        for a in range(n):
            for j in range(3):
                copy(a, j).wait_recv()
        for cp in sends:
            cp.wait_send()

    return pl.pallas_call(
        body, name="grad_chips", in_specs=[_HBM] * n, out_specs=[_HBM] * n,
        out_shape=[jax.ShapeDtypeStruct((3,) + p.shape[1:], p.dtype) for p in parts],
        scratch_shapes=[pltpu.SemaphoreType.DMA((3 * n,)), pltpu.SemaphoreType.DMA((3 * n,))],
    )(*parts)


def _grad_share(fulls):
    n = len(fulls)
    rhs = [f.shape[0] // 2 for f in fulls]

    def body(*refs):
        ins, outs = refs[:n], refs[n:2 * n]
        send_sems, recv_sems = refs[2 * n:]
        x, y, c = _pos()

        def copy(a, half):
            rows = pl.ds(pl.multiple_of(half * rhs[a], 8), rhs[a])
            return pltpu.make_async_remote_copy(src_ref=ins[a].at[rows, :], dst_ref=outs[a].at[rows, :],
                                                send_sem=send_sems.at[a], recv_sem=recv_sems.at[a],
                                                device_id=(x, y, 1 - c), device_id_type=MESH)

        sends = [copy(a, c) for a in range(n)]
        for cp in sends:
            cp.start()
        for a in range(n):
            copy(a, 1 - c).wait_recv()
        for cp in sends:
            cp.wait_send()

    return pl.pallas_call(
        body, name="grad_share", in_specs=[_HBM] * n, out_specs=[_HBM] * n,
        out_shape=[jax.ShapeDtypeStruct(f.shape, f.dtype) for f in fulls],
        input_output_aliases={a: a for a in range(n)},
        scratch_shapes=[pltpu.SemaphoreType.DMA((n,)), pltpu.SemaphoreType.DMA((n,))],
    )(*fulls)


def _add_sibling(own, recv, cq, name):
    nb, R, Cc = own.shape
    Rh = R // 2

    def body(cq_ref, a_ref, b_ref, o32_ref, o16_ref):
        s = a_ref[...] + b_ref[...]
        o32_ref[...] = s
        o16_ref[...] = s.astype(o16_ref.dtype)

    sp = pl.BlockSpec((1, Rh, Cc), lambda b, s: (b, 0, 0))
    gs = pltpu.PrefetchScalarGridSpec(
        num_scalar_prefetch=1, grid=(nb,),
        in_specs=[pl.BlockSpec((1, Rh, Cc), lambda b, s: (b, s[0], 0)), sp], out_specs=[sp, sp])
    return pl.pallas_call(
        body, name=name, grid_spec=gs,
        out_shape=[jax.ShapeDtypeStruct((nb, Rh, Cc), F32), jax.ShapeDtypeStruct((nb, Rh, Cc), _MXU)],
        compiler_params=_cparams(("parallel",)),
    )(cq, own, recv)


def _add_chips(part32, recv3, cq, name):
    nb, Rh, Cc = part32.shape

    def body(cq_ref, a_ref, b_ref, o_ref):
        acc = a_ref[0]
        for j in range(3):
            acc = acc + b_ref[j].astype(F32)
        o_ref[...] = acc

    gs = pltpu.PrefetchScalarGridSpec(
        num_scalar_prefetch=1, grid=(1,),
        in_specs=[pl.BlockSpec((1, Rh, Cc), lambda i, s: (s[1], 0, 0)), pl.BlockSpec((3, Rh, Cc), lambda i, s: (0, 0, 0))],
        out_specs=pl.BlockSpec((Rh, Cc), lambda i, s: (s[0], 0)))
    return pl.pallas_call(
        body, name=name, grid_spec=gs, out_shape=jax.ShapeDtypeStruct((2 * Rh, Cc), F32),
        compiler_params=_cparams(("arbitrary",)),
    )(cq, part32, recv3)


def _sum_devices(small_all):
    def body(s_ref, o_ref):
        tot = s_ref[0]
        for d in range(1, 8):
            tot = tot + s_ref[d]
        o_ref[...] = tot

    return pl.pallas_call(body, name="sum_devices", out_shape=jax.ShapeDtypeStruct(small_all.shape[1:], F32))(small_all)


def _adamw(w, g, m, v, name):
    R, Cc = w.shape
    T = max([t for t in range(8, 257, 8) if R % t == 0], default=R)
    c1 = 1.0 / (1.0 - ADAM_B1 ** ADAM_STEP)
    c2 = 1.0 / (1.0 - ADAM_B2 ** ADAM_STEP)

    def body(w_ref, g_ref, m_ref, v_ref, d_ref, mo_ref, vo_ref):
        gv = g_ref[...]
        mn = ADAM_B1 * m_ref[...] + (1.0 - ADAM_B1) * gv
        vn = ADAM_B2 * v_ref[...] + (1.0 - ADAM_B2) * (gv * gv)
        mo_ref[...] = mn
        vo_ref[...] = vn
        d_ref[...] = -ADAM_LR * ((mn * c1) / (jnp.sqrt(vn * c2) + ADAM_EPS) + ADAM_WD * w_ref[...])

    sp = pl.BlockSpec((T, Cc), lambda i: (i, 0))
    sh = jax.ShapeDtypeStruct((R, Cc), F32)
    return pl.pallas_call(
        body, name=name, grid=(R // T,), in_specs=[sp] * 4, out_specs=(sp, sp, sp), out_shape=(sh, sh, sh),
        compiler_params=_cparams(("parallel",)),
    )(w, g, m, v)


SMALL_ROWS = 32
REPL_ROWS = 8


def _pad_lanes(v, n=D_MODEL):
    return jnp.pad(v, ((0, 0), (0, n - v.shape[1])))


def kernel(x, norm1_w, w_in, conv_qkv_w, a_log, dt_bias, gdn_norm_w, w_out, norm2_w, w_up, ffn_conv_w, w_down, final_norm_w, loss_target, m_norm1_w, m_w_in, m_conv_qkv_w, m_a_log, m_dt_bias, m_gdn_norm_w, m_w_out, m_norm2_w, m_w_up, m_ffn_conv_w, m_w_down, m_final_norm_w, v_norm1_w, v_w_in, v_conv_qkv_w, v_a_log, v_dt_bias, v_gdn_norm_w, v_w_out, v_norm2_w, v_w_up, v_ffn_conv_w, v_w_down, v_final_norm_w):
    c = lax.axis_index("c")
    q = 2 * lax.axis_index("x") + lax.axis_index("y")
    S = x.shape[1]
    cq = jnp.stack([c, q]).astype(jnp.int32)

    def gather(shards, tag):
        got = _gather_halves(shards, "gather_" + tag)
        got = _sibling_fill(got, "fill_" + tag)
        return _place_own(shards, got, cq, "place_" + tag)

    g_in, g_conv, g_fconv = gather([w_in[0].astype(_MXU), conv_qkv_w[0], ffn_conv_w[0]], "in")
    rest = [w_out[0].astype(_MXU), w_up[0].astype(_MXU), w_down[0].astype(_MXU)]
    *rest_started, token = _gather_halves_start(rest, g_conv, "gather_rest_start")

    def rest_weights(after):
        got = _gather_halves_wait(rest_started, after, "gather_rest_wait")
        got = _sibling_fill(got, "fill_rest")
        g_out, g_up, g_down = _place_own(rest, got, cq, "place_rest")
        return g_out.reshape(D_MODEL, D_MODEL), g_up, g_down.reshape(D_FF, D_MODEL)
    wp = _wp_assemble(g_in)
    conv_f = jnp.concatenate([g_conv[i] for i in range(N_CHIPS)], axis=1)
    fcw = jnp.concatenate([g_fconv[i] for i in range(N_CHIPS)], axis=1)
    gp = _pad_lanes(jnp.concatenate([a_log, dt_bias], axis=1), 128)
    fnw = final_norm_w[None, :]
    early = {}

    def early_grads(d_wup, d_wdown):
        fams_e = [d_wup, d_wdown.reshape(N_CHIPS, D_FF // N_CHIPS, D_MODEL)]
        got_e = _grad_sibling(fams_e, None, "grad_sibling_early")
        early["parts"] = [_add_sibling(f, r, cq, "add_sibling_" + nm) for f, r, nm in zip(fams_e, got_e, ("w_up", "w_down"))]
        *early["started"], tok = _grad_chips_start([p[1] for p in early["parts"]], "grad_chips_start")
        return tok

    loss_l, dx, g = _local_step(x[0], loss_target[0], norm1_w + token[0:1, 0:1], norm2_w, fnw, gp, gdn_norm_w, wp, conv_f,
                                fcw, rest_weights, early_grads)
    fams = [_win_split(g["wp"]), g["w_out"].reshape(N_CHIPS, D_MODEL // N_CHIPS, D_MODEL)]
    n_fc = FFN_CONV * D_FF

    def rows_of(v):
        flat = v.reshape(-1)
        return jnp.pad(flat, (0, -flat.shape[0] % D_MODEL)).reshape(-1, D_MODEL)

    gp_row = _pad_lanes(jnp.concatenate([g["gp"][0:1, 0:8], loss_l[0:1, 0:1]], axis=1))
    small = jnp.concatenate([g["n1w"], g["n2w"], g["fnw"], gp_row, _pad_lanes(g["gnw"]),
                             rows_of(g["conv_w"]), rows_of(g["fcw_g"]), rows_of(g["fcw_u"])], axis=0)
    small = jnp.pad(small, ((0, SMALL_ROWS - small.shape[0]), (0, 0)))
    *got, small_all = _grad_sibling(fams, small, "grad_sibling")
    parts = [_add_sibling(f, r, cq, "add_sibling_" + nm) for f, r, nm in zip(fams, got, ("w_in", "w_out"))]
    got3 = _grad_chips([p[1] for p in parts])
    got3_e = _grad_chips_wait(early["started"], [dx, g["wp"]], "grad_chips_wait")
    halves = [_add_chips(p[0], r3, cq, "add_chips_" + nm)
              for p, r3, nm in zip(parts + early["parts"], list(got3) + list(got3_e), ("w_in", "w_out", "w_up", "w_down"))]
    g_w_in, g_w_out, g_w_up, g_w_down = _grad_share(halves)
    small_red = _sum_devices(small_all)
    loss = small_red[3, 8]
    r0 = 5
    r1 = r0 + GDN_CONV * 3 * GDN_WIDTH // D_MODEL
    r2 = r1 + -(-n_fc // D_MODEL)
    conv_red = small_red[r0:r1].reshape(GDN_CONV, 3 * GDN_WIDTH)
    fc_red = jnp.concatenate([small_red[r1:r2].reshape(-1)[:n_fc].reshape(FFN_CONV, D_FF),
                              small_red[r2:2 * r2 - r1].reshape(-1)[:n_fc].reshape(FFN_CONV, D_FF)], axis=1)
    g_conv_w = lax.dynamic_slice_in_dim(conv_red, q * (3 * GDN_WIDTH // N_CHIPS), 3 * GDN_WIDTH // N_CHIPS, axis=1)
    g_fconv_w = lax.dynamic_slice_in_dim(fc_red, q * (2 * D_FF // N_CHIPS), 2 * D_FF // N_CHIPS, axis=1)
    g_n1w, g_n2w, g_fnw = small_red[0:1], small_red[1:2], small_red[2]
    g_alog, g_dtb, g_gnw = small_red[3:4, 0:4], small_red[3:4, 4:8], small_red[4:5, 0:128]
    big = {}
    for nm, w, gg, m, v in (("w_in", w_in, g_w_in, m_w_in, v_w_in), ("conv_qkv_w", conv_qkv_w, g_conv_w, m_conv_qkv_w, v_conv_qkv_w),
                            ("w_out", w_out, g_w_out, m_w_out, v_w_out), ("w_up", w_up, g_w_up, m_w_up, v_w_up),
                            ("ffn_conv_w", ffn_conv_w, g_fconv_w, m_ffn_conv_w, v_ffn_conv_w),
                            ("w_down", w_down, g_w_down, m_w_down, v_w_down)):
        d_, m_, v_ = _adamw(w[0], gg, m[0], v[0], "adamw_" + nm)
        big[nm] = (gg[None], d_[None], m_[None], v_[None])

    def pack_small(n1, n2, fn, al, db, gn):
        return jnp.concatenate([n1, n2, fn[None, :], _pad_lanes(jnp.concatenate([al, db], axis=1)), _pad_lanes(gn),
                                jnp.zeros((REPL_ROWS - 5, D_MODEL), F32)], axis=0)

    sw = pack_small(norm1_w, norm2_w, final_norm_w, a_log, dt_bias, gdn_norm_w)
    sm = pack_small(m_norm1_w, m_norm2_w, m_final_norm_w, m_a_log, m_dt_bias, m_gdn_norm_w)
    sv = pack_small(v_norm1_w, v_norm2_w, v_final_norm_w, v_a_log, v_dt_bias, v_gdn_norm_w)
    sd, smn, svn = _adamw(sw, small_red[:REPL_ROWS], sm, sv, "adamw_small")

    def unpack_small(t):
        return dict(norm1_w=t[0:1], norm2_w=t[1:2], final_norm_w=t[2], a_log=t[3:4, 0:4], dt_bias=t[3:4, 4:8],
                    gdn_norm_w=t[4:5, 0:128])

    sg = dict(norm1_w=g_n1w, norm2_w=g_n2w, final_norm_w=g_fnw, a_log=g_alog, dt_bias=g_dtb, gdn_norm_w=g_gnw)
    sd, smn, svn = unpack_small(sd), unpack_small(smn), unpack_small(svn)
    names = ["norm1_w", "w_in", "conv_qkv_w", "a_log", "dt_bias", "gdn_norm_w", "w_out", "norm2_w", "w_up",
             "ffn_conv_w", "w_down", "final_norm_w"]
    grads = [big[n][0] if n in big else sg[n] for n in names]
    deltas = [big[n][1] if n in big else sd[n] for n in names]
    new_m = [big[n][2] if n in big else smn[n] for n in names]
    new_v = [big[n][3] if n in big else svn[n] for n in names]
    return (loss, dx[None], *grads, *deltas, *new_m, *new_v)
```

```python
import functools
import math

import numpy as np
import jax
import jax.numpy as jnp
from jax import lax
from jax.experimental import pallas as pl
from jax.experimental.pallas import tpu as pltpu

F32 = jnp.float32
BF16 = jnp.bfloat16
_MXU = jnp.bfloat16
_HI = lax.Precision.HIGHEST
EPS = 1e-6
V7X_VMEM_LIMIT = 56 * 1024 * 1024
MESH = pl.DeviceIdType.MESH

D_MODEL = 1024
GDN_HEADS, GDN_DIM, GDN_CHUNK, GDN_CONV = 4, 128, 64, 4
GDN_WIDTH = GDN_HEADS * GDN_DIM
DIL_HEADS, DIL_DIM = 8, 64
DIL_WIDTH = DIL_HEADS * DIL_DIM
D_FF, FFN_CONV = 2816, 3
IN_COLS = 3592
P_COLS = 3840
P_Z, P_QKVB, P_BA = 1536, 2048, 3584
ATT_T = 1024
ADAM_LR, ADAM_B1, ADAM_B2, ADAM_EPS, ADAM_WD, ADAM_STEP = 0.001, 0.9, 0.999, 1e-08, 0.01, 10
N_CHIPS = 4


def _cparams(sem=None, vmem=None):
    kw = {}
    if sem is not None:
        kw["dimension_semantics"] = sem
    if vmem is not None:
        kw["vmem_limit_bytes"] = vmem
    return pltpu.CompilerParams(**kw)


def _silu(x):
    return x * jax.nn.sigmoid(x)


def _pick_tile(n, cap):
    best = None
    for t in range(128, min(n, cap) + 1, 128):
        if n % t == 0:
            best = t
    return best or n


def _mm(a, b, mode, *, out_dtype=F32, residual=None, name, b_blocks=False, place=None, into=None, tn=None):
    if mode == "nn":
        M, K = a.shape
        N = b.shape[0] * b.shape[2] if b_blocks else b.shape[1]
    elif mode == "nt":
        (M, K), (N, _) = a.shape, b.shape
    else:
        (K, M), (_, N) = a.shape, b.shape
    tm = _pick_tile(M, 1024)
    tn = b.shape[2] if b_blocks else (tn or _pick_tile(N, 1536))

    def vmem(tm, tn):
        return 2 * (tm * K * a.dtype.itemsize + tn * K * b.dtype.itemsize
                    + tm * tn * (jnp.dtype(out_dtype).itemsize + (4 if residual is not None else 0))) + 3 * tm * tn * 4

    fixed_tn = b_blocks or (place is not None and place[0] == "blocks")
    while vmem(tm, tn) > 40 * 1024 * 1024:
        if (tm >= tn or fixed_tn) and tm % 256 == 0:
            tm //= 2
        elif tn % 256 == 0 and not fixed_tn:
            tn //= 2
        else:
            tm //= 2
    a_spec = pl.BlockSpec((K, tm), lambda j, i: (0, i)) if mode == "tn" else pl.BlockSpec((tm, K), lambda j, i: (i, 0))
    if b_blocks:
        b_spec = pl.BlockSpec((None, K, tn), lambda j, i: (j, 0, 0))
    else:
        b_spec = pl.BlockSpec((tn, K), lambda j, i: (j, 0)) if mode == "nt" else pl.BlockSpec((K, tn), lambda j, i: (0, j))
    r_spec = pl.BlockSpec((tm, tn), lambda j, i: (i, j))
    if place is None:
        o_spec, o_shape = r_spec, (M, N)
    elif place[0] == "rows":
        off = place[2] // tm
        o_spec, o_shape = pl.BlockSpec((tm, tn), lambda j, i: (i + off, j)), (place[1], N)
    else:
        off = place[2]
        o_spec, o_shape = pl.BlockSpec((None, tm, tn), lambda j, i: (j + off, i, 0)), (place[1], M, tn)
    dims = {"nn": (((1,), (0,)), ((), ())), "nt": (((1,), (1,)), ((), ())), "tn": (((0,), (0,)), ((), ()))}[mode]

    def body(*refs):
        a_ref, b_ref = refs[0], refs[1]
        o_ref = refs[-1]
        acc = lax.dot_general(a_ref[...].astype(_MXU), b_ref[...].astype(_MXU), dims, preferred_element_type=F32)
        if residual is not None:
            acc = acc + refs[2][...]
        o_ref[...] = acc.astype(out_dtype)

    ins, specs, alias = [a, b], [a_spec, b_spec], {}
    if residual is not None:
        ins.append(residual)
        specs.append(r_spec)
    if into is not None:
        alias = {len(ins): 0}
        ins.append(into)
        specs.append(pl.BlockSpec(memory_space=pl.ANY))
    return pl.pallas_call(
        body, name=name, grid=(N // tn, M // tm), in_specs=specs, out_specs=o_spec,
        out_shape=jax.ShapeDtypeStruct(o_shape, out_dtype), input_output_aliases=alias,
        compiler_params=_cparams(("parallel", "parallel"), V7X_VMEM_LIMIT),
    )(*ins)


def _mm_nt_blocks(a_list, b4, name):
    M = a_list[0].shape[0]
    nb, N, Kb = b4.shape
    tm, tn = _pick_tile(M, 512), _pick_tile(N, 512)

    def body(a0_ref, a1_ref, b_ref, o_ref):
        acc = None
        for blk in range(nb):
            a_ref = (a0_ref, a1_ref)[blk // 2]
            lo = (blk % 2) * Kb
            t = lax.dot_general(a_ref[:, lo:lo + Kb].astype(_MXU), b_ref[blk].astype(_MXU), (((1,), (1,)), ((), ())),
                                preferred_element_type=F32)
            acc = t if acc is None else acc + t
        o_ref[...] = acc

    a_spec = pl.BlockSpec((tm, 2 * Kb), lambda j, i: (i, 0))
    return pl.pallas_call(
        body, name=name, grid=(N // tn, M // tm),
        in_specs=[a_spec, a_spec, pl.BlockSpec((nb, tn, Kb), lambda j, i: (0, j, 0))],
        out_specs=pl.BlockSpec((tm, tn), lambda j, i: (i, j)), out_shape=jax.ShapeDtypeStruct((M, N), F32),
        compiler_params=_cparams(("parallel", "parallel"), V7X_VMEM_LIMIT),
    )(a_list[0], a_list[1], b4)


def _wp_assemble(g_in):
    nb, Dm, Wb = g_in.shape
    T = 256
    n_lo = P_QKVB - 2 * Wb

    def body(g_ref, o_ref):
        g2 = g_ref[2]
        o_ref[...] = jnp.concatenate(
            [g_ref[0], g_ref[1], g2[:, :n_lo], g2[:, n_lo + 8:], g_ref[3], g2[:, n_lo:n_lo + 8],
             jnp.zeros((T, P_COLS - P_BA - 8), g_in.dtype)], axis=1)

    return pl.pallas_call(
        body, name="wp_assemble", grid=(Dm // T,), in_specs=[pl.BlockSpec((nb, T, Wb), lambda i: (0, i, 0))],
        out_specs=pl.BlockSpec((T, P_COLS), lambda i: (i, 0)), out_shape=jax.ShapeDtypeStruct((Dm, P_COLS), g_in.dtype),
        compiler_params=_cparams(("parallel",)),
    )(g_in)


def _win_split(d_wp):
    Dm = d_wp.shape[0]
    Wb = IN_COLS // N_CHIPS
    T = 256

    def body(x_ref, o_ref):
        xv = x_ref[...]
        o_ref[0] = xv[:, 0:Wb]
        o_ref[1] = xv[:, Wb:2 * Wb]
        o_ref[2] = jnp.concatenate([xv[:, 2 * Wb:P_QKVB], xv[:, P_BA:P_BA + 8], xv[:, P_QKVB:3 * Wb - 8]], axis=1)
        o_ref[3] = xv[:, 3 * Wb - 8:P_BA]

    return pl.pallas_call(
        body, name="win_split", grid=(Dm // T,), in_specs=[pl.BlockSpec((T, P_COLS), lambda i: (i, 0))],
        out_specs=pl.BlockSpec((N_CHIPS, T, Wb), lambda i: (0, i, 0)),
        out_shape=jax.ShapeDtypeStruct((N_CHIPS, Dm, Wb), F32), compiler_params=_cparams(("parallel",)),
    )(d_wp)


def _rmsnorm_fwd(x, w, name):
    S, D = x.shape
    T = _pick_tile(S, 512)

    def body(x_ref, w_ref, o_ref):
        xv = x_ref[...]
        rs = lax.rsqrt(jnp.mean(xv * xv, axis=-1, keepdims=True) + EPS)
        o_ref[...] = (xv * rs * w_ref[...]).astype(o_ref.dtype)

    return pl.pallas_call(
        body, name=name, grid=(S // T,),
        in_specs=[pl.BlockSpec((T, D), lambda i: (i, 0)), pl.BlockSpec((1, D), lambda i: (0, 0))],
        out_specs=pl.BlockSpec((T, D), lambda i: (i, 0)),
        out_shape=jax.ShapeDtypeStruct((S, D), _MXU),
        compiler_params=_cparams(("parallel",)),
    )(x, w)


def _rmsnorm_bwd(dh, x, w, dres, name):
    S, D = x.shape
    T = _pick_tile(S, 512)

    def body(dh_ref, x_ref, w_ref, dres_ref, dx_ref, dw_ref):
        xv = x_ref[...]
        rs = lax.rsqrt(jnp.mean(xv * xv, axis=-1, keepdims=True) + EPS)
        xn = xv * rs
        dhv = dh_ref[...]
        dxn = dhv * w_ref[...]
        dx_ref[...] = dres_ref[...] + rs * (dxn - xn * jnp.mean(dxn * xn, axis=-1, keepdims=True))

        @pl.when(pl.program_id(0) == 0)
        def _():
            dw_ref[...] = jnp.zeros_like(dw_ref)

        dw_ref[...] += jnp.sum(dhv * xn, axis=0, keepdims=True)

    row = pl.BlockSpec((T, D), lambda i: (i, 0))
    vec = pl.BlockSpec((1, D), lambda i: (0, 0))
    return pl.pallas_call(
        body, name=name, grid=(S // T,), in_specs=[row, row, vec, row], out_specs=(row, vec),
        out_shape=(jax.ShapeDtypeStruct((S, D), F32), jax.ShapeDtypeStruct((1, D), F32)),
        compiler_params=_cparams(("arbitrary",)),
    )(dh, x, w, dres)


def _loss_head(x3, w, tgt, name):
    S, D = x3.shape
    T = _pick_tile(S, 512)

    def body(x_ref, w_ref, t_ref, loss_ref, dx_ref, dw_ref):
        xv = x_ref[...]
        rs = lax.rsqrt(jnp.mean(xv * xv, axis=-1, keepdims=True) + EPS)
        xn = xv * rs
        err = xn * w_ref[...] - t_ref[...]
        dy = err * (1.0 / D)
        dxn = dy * w_ref[...]
        dx_ref[...] = rs * (dxn - xn * jnp.mean(dxn * xn, axis=-1, keepdims=True))

        @pl.when(pl.program_id(0) == 0)
        def _():
            dw_ref[...] = jnp.zeros_like(dw_ref)
            loss_ref[...] = jnp.zeros_like(loss_ref)

        dw_ref[...] += jnp.sum(dy * xn, axis=0, keepdims=True)
        part = jnp.sum(jnp.sum(err * err, axis=-1, keepdims=True), axis=0, keepdims=True) * (0.5 / D)
        loss_ref[...] += jnp.broadcast_to(part, loss_ref.shape)

    row = pl.BlockSpec((T, D), lambda i: (i, 0))
    vec = pl.BlockSpec((1, D), lambda i: (0, 0))
    return pl.pallas_call(
        body, name=name, grid=(S // T,), in_specs=[row, vec, row],
        out_specs=(pl.BlockSpec((8, 128), lambda i: (0, 0)), row, vec),
        out_shape=(jax.ShapeDtypeStruct((8, 128), F32), jax.ShapeDtypeStruct((S, D), F32), jax.ShapeDtypeStruct((1, D), F32)),
        compiler_params=_cparams(("arbitrary",)),
    )(x3, w, tgt)


def _conv_taps(ext, w, K, T):
    out = None
    for i in range(K):
        lo = 8 - (K - 1) + i
        term = ext[lo:lo + T, :] * w[i:i + 1, :]
        out = term if out is None else out + term
    return out


def _conv_taps_t(ext, w, K, T):
    out = None
    for i in range(K):
        lo = (K - 1) - i
        term = ext[lo:lo + T, :] * w[i:i + 1, :]
        out = term if out is None else out + term
    return out


def _tri_masks(C):
    r = lax.broadcasted_iota(jnp.int32, (C, C), 0)
    c = lax.broadcasted_iota(jnp.int32, (C, C), 1)
    return r == c, r >= c, r > c, r <= c


_NN, _NT, _TN = ((1,), (0,)), ((1,), (1,)), ((0,), (0,))
_GDN_PASSES = dict(qk=1, inv=1, sol=1, scan=1, bwd=1)


def _bdot_raw(a, b, kind, passes):
    dims = ({"NN": ((2,), (1,)), "NT": ((2,), (2,)), "TN": ((1,), (1,))}[kind], ((0,), (0,)))
    if passes == 0:
        return lax.dot_general(a, b, dims, precision=_HI, preferred_element_type=F32)
    ah, bh = a.astype(BF16), b.astype(BF16)
    out = lax.dot_general(ah, bh, dims, preferred_element_type=F32)
    if passes == 3:
        al, bl = (a - ah.astype(F32)).astype(BF16), (b - bh.astype(F32)).astype(BF16)
        out = out + lax.dot_general(ah, bl, dims, preferred_element_type=F32) + lax.dot_general(al, bh, dims, preferred_element_type=F32)
    return out


@functools.partial(jax.custom_vjp, nondiff_argnums=(2, 3))
def _bdot(a, b, kind, passes):
    return _bdot_raw(a, b, kind, passes)


def _bdot_fwd(a, b, kind, passes):
    return _bdot_raw(a, b, kind, passes), (a, b)


def _bdot_bwd(kind, passes, res, ct):
    a, b = res
    if kind == "NN":
        return _bdot_raw(ct, b, "NT", passes), _bdot_raw(a, ct, "TN", passes)
    if kind == "NT":
        return _bdot_raw(ct, b, "NN", passes), _bdot_raw(ct, a, "TN", passes)
    return _bdot_raw(b, ct, "NT", passes), _bdot_raw(a, ct, "NN", passes)


_bdot.defvjp(_bdot_fwd, _bdot_bwd)


def _softplus(x):
    return jnp.maximum(x, 0.0) + jnp.log(1.0 + jnp.exp(-jnp.abs(x)))


def _gdn_stage1(cq, ck, cv, b_col, a_col, alog, dtb, dot=_bdot_raw):
    C = cq.shape[1]
    eye, incl, strict, incl_t = _tri_masks(C)
    qn = cq * lax.rsqrt(jnp.sum(cq * cq, axis=-1, keepdims=True) + EPS) * (GDN_DIM ** -0.5)
    kn = ck * lax.rsqrt(jnp.sum(ck * ck, axis=-1, keepdims=True) + EPS)
    beta = jax.nn.sigmoid(b_col)
    g = -jnp.exp(alog) * _softplus(a_col + dtb)
    g_row = jnp.sum(jnp.where(eye, g, 0.0), axis=1, keepdims=True)
    beta_row = jnp.sum(jnp.where(eye, beta, 0.0), axis=1, keepdims=True)
    gc_col = jnp.sum(jnp.where(incl, g_row, 0.0), axis=2, keepdims=True)
    gc_row = jnp.sum(jnp.where(incl_t, g, 0.0), axis=1, keepdims=True)
    dec = jnp.where(incl, jnp.exp(jnp.where(incl, gc_col - gc_row, 0.0)), 0.0)
    kk = dot(kn, kn, "NT", _GDN_PASSES["qk"])
    qk = dot(qn, kn, "NT", _GDN_PASSES["qk"])
    lmat = jnp.where(strict, dec * kk * beta_row, 0.0)
    attn = dec * qk * beta_row
    gam = jnp.exp(gc_col)
    gc_last = gc_col[:, C - 1:C, :]
    k_end = kn * (jnp.exp(gc_last - gc_col) * beta)
    return lmat, cv, gam * kn, gam * qn, attn, k_end, jnp.exp(gc_last)


def _tri_inv(lmat):
    C = lmat.shape[1]
    eye = _tri_masks(C)[0]
    ps = _GDN_PASSES["inv"]
    p = jnp.where(eye, 1.0, 0.0) - lmat
    lp = _bdot_raw(lmat, lmat, "NN", ps)
    n = int(math.log2(C))
    for s in range(1, n):
        p = p + _bdot_raw(p, lp, "NN", ps)
        if s < n - 1:
            lp = _bdot_raw(lp, lp, "NN", ps)
    return p


def _gated_norm(o, z, gnw):
    on = o * lax.rsqrt(jnp.mean(o * o, axis=-1, keepdims=True) + EPS) * gnw
    return on * _silu(z)


GDN_PG = 2
GDN_SG = 4


def _gdn_pairs(c, ba, gp, G):
    C, W, H = GDN_CHUNK, GDN_WIDTH, GDN_HEADS
    pairs = [(j, h) for j in range(G) for h in range(H)]
    cq, ck, cv = (jnp.stack([c[C * j:C * (j + 1), o + GDN_DIM * h:o + GDN_DIM * (h + 1)] for j, h in pairs]) for o in (0, W, 2 * W))
    b_col = jnp.stack([ba[C * j:C * (j + 1), h:h + 1] for j, h in pairs])
    a_col = jnp.stack([ba[C * j:C * (j + 1), H + h:H + h + 1] for j, h in pairs])
    alog = jnp.stack([gp[0:1, h:h + 1] for j, h in pairs])
    dtb = jnp.stack([gp[0:1, H + h:H + h + 1] for j, h in pairs])
    return pairs, (cq, ck, cv, b_col, a_col, alog, dtb)


def _gdn_pre_specs(S, G):
    C = GDN_CHUNK
    T = C * G
    return dict(
        cur=pl.BlockSpec((T, 3 * GDN_WIDTH), lambda i: (i, 0)),
        prev=pl.BlockSpec((8, 3 * GDN_WIDTH), lambda i: (jnp.maximum(i * (T // 8) - 1, 0), 0)),
        ba=pl.BlockSpec((T, 128), lambda i: (i, P_BA // 128)),
        cw=pl.BlockSpec((GDN_CONV, 3 * GDN_WIDTH), lambda i: (0, 0)),
        vec=pl.BlockSpec((1, 128), lambda i: (0, 0)),
        hd=pl.BlockSpec((GDN_HEADS, T, GDN_DIM), lambda i: (0, i, 0)),
        hc=pl.BlockSpec((GDN_HEADS, T, C), lambda i: (0, i, 0)),
        ge=pl.BlockSpec((G, GDN_HEADS, 8, 128), lambda i: (i, 0, 0, 0)),
    )


def _hd_shape(S, last=GDN_DIM):
    return jax.ShapeDtypeStruct((GDN_HEADS, S, last), F32)


def _gdn_pre(proj, conv_w, gp):
    S = proj.shape[0]
    C, G = GDN_CHUNK, GDN_PG
    nc = S // C
    sp = _gdn_pre_specs(S, G)

    def body(cur_ref, prev_ref, ba_ref, cw_ref, gp_ref, uv_ref, wk_ref, qd_ref, ke_ref, at_ref, ti_ref, ge_ref):
        prev = prev_ref[...] * jnp.where(pl.program_id(0) == 0, 0.0, 1.0)
        c = _silu(_conv_taps(jnp.concatenate([prev, cur_ref[...]], axis=0), cw_ref[...], GDN_CONV, C * G))
        pairs, args = _gdn_pairs(c, ba_ref[...], gp_ref[...], G)
        lmat, v, rk, q_dec, attn, k_end, g_end = _gdn_stage1(*args)
        t = _tri_inv(lmat)
        u_v = _bdot_raw(t, v, "NN", _GDN_PASSES["sol"])
        w_k = _bdot_raw(t, rk, "NN", _GDN_PASSES["sol"])
        for b, (j, h) in enumerate(pairs):
            rows = slice(C * j, C * (j + 1))
            uv_ref[h, rows, :] = u_v[b]
            wk_ref[h, rows, :] = w_k[b]
            qd_ref[h, rows, :] = q_dec[b]
            ke_ref[h, rows, :] = k_end[b]
            at_ref[h, rows, :] = attn[b]
            ti_ref[h, rows, :] = t[b]
            ge_ref[j, h] = jnp.broadcast_to(g_end[b], (8, 128))

    return pl.pallas_call(
        body, name="gdn_pre", grid=(nc // G,),
        in_specs=[sp["cur"], sp["prev"], sp["ba"], sp["cw"], sp["vec"]],
        out_specs=(sp["hd"], sp["hd"], sp["hd"], sp["hd"], sp["hc"], sp["hc"], sp["ge"]),
        out_shape=(_hd_shape(S), _hd_shape(S), _hd_shape(S), _hd_shape(S), _hd_shape(S, C), _hd_shape(S, C),
                   jax.ShapeDtypeStruct((nc, GDN_HEADS, 8, 128), F32)),
        compiler_params=_cparams(("parallel",)),
    )(proj, proj, proj, conv_w, gp)


def _gdn_scan_specs(S, G, rev):
    C = GDN_CHUNK
    T = C * G
    n = S // T
    ci = (lambda i: n - 1 - i) if rev else (lambda i: i)
    return dict(
        hd=pl.BlockSpec((GDN_HEADS, T, GDN_DIM), lambda i: (0, ci(i), 0)),
        hc=pl.BlockSpec((GDN_HEADS, T, C), lambda i: (0, ci(i), 0)),
        ge=pl.BlockSpec((G, GDN_HEADS, 8, 128), lambda i: (ci(i), 0, 0, 0)),
        z=pl.BlockSpec((T, GDN_WIDTH), lambda i: (ci(i), P_Z // GDN_WIDTH)),
        oa=pl.BlockSpec((T, GDN_WIDTH), lambda i: (ci(i), 0)),
        vec=pl.BlockSpec((1, 128), lambda i: (0, 0)),
        st=pl.BlockSpec((G, GDN_HEADS, GDN_DIM, GDN_DIM), lambda i: (ci(i), 0, 0, 0)),
    )


def _gdn_scan(u_v, w_k, q_dec, k_end, attn, g_end, proj, gnw):
    S = proj.shape[0]
    C, G = GDN_CHUNK, GDN_SG
    nc = S // C
    sp = _gdn_scan_specs(S, G, False)
    ps = _GDN_PASSES["scan"]

    def body(uv_ref, wk_ref, qd_ref, ke_ref, at_ref, ge_ref, z_ref, gnw_ref, oa_ref, st_ref, s_scr):
        @pl.when(pl.program_id(0) == 0)
        def _():
            s_scr[...] = jnp.zeros_like(s_scr)

        for j in range(G):
            rows = slice(C * j, C * (j + 1))
            st = s_scr[...]
            st_ref[j] = st
            u = uv_ref[:, rows, :] - _bdot_raw(wk_ref[:, rows, :], st, "NN", ps)
            o = _bdot_raw(qd_ref[:, rows, :], st, "NN", ps) + _bdot_raw(at_ref[:, rows, :], u, "NN", ps)
            s_scr[...] = ge_ref[j][:, 0:1, 0:1] * st + _bdot_raw(ke_ref[:, rows, :], u, "TN", ps)
            for h in range(GDN_HEADS):
                cols = slice(GDN_DIM * h, GDN_DIM * (h + 1))
                oa_ref[rows, cols] = _gated_norm(o[h], z_ref[rows, cols], gnw_ref[...])

    return pl.pallas_call(
        body, name="gdn_scan", grid=(nc // G,),
        in_specs=[sp["hd"], sp["hd"], sp["hd"], sp["hd"], sp["hc"], sp["ge"], sp["z"], sp["vec"]],
        out_specs=(sp["oa"], sp["st"]),
        out_shape=(jax.ShapeDtypeStruct((S, GDN_WIDTH + DIL_WIDTH), F32),
                   jax.ShapeDtypeStruct((nc, GDN_HEADS, GDN_DIM, GDN_DIM), F32)),
        scratch_shapes=[pltpu.VMEM((GDN_HEADS, GDN_DIM, GDN_DIM), F32)],
        compiler_params=_cparams(("arbitrary",)),
    )(u_v, w_k, q_dec, k_end, attn, g_end, proj, gnw)


def _gdn_scan_bwd(u_v, w_k, q_dec, k_end, attn, g_end, proj, gnw, states, d_oa):
    S = proj.shape[0]
    C, G = GDN_CHUNK, GDN_SG
    nc = S // C
    sp = _gdn_scan_specs(S, G, True)
    ps, pb = _GDN_PASSES["scan"], _GDN_PASSES["bwd"]

    def body(uv_ref, wk_ref, qd_ref, ke_ref, at_ref, ge_ref, z_ref, gnw_ref, st_ref, doa_ref,
             duv_ref, dwk_ref, dqd_ref, dke_ref, dat_ref, dge_ref, dz_ref, dgnw_ref, ds_scr):
        @pl.when(pl.program_id(0) == 0)
        def _():
            ds_scr[...] = jnp.zeros_like(ds_scr)
            dgnw_ref[...] = jnp.zeros_like(dgnw_ref)

        dgnw = jnp.zeros((1, 128), F32)
        for j in reversed(range(G)):
            rows = slice(C * j, C * (j + 1))
            st = st_ref[j]
            wk, qd, ke, at = wk_ref[:, rows, :], qd_ref[:, rows, :], ke_ref[:, rows, :], at_ref[:, rows, :]
            u = uv_ref[:, rows, :] - _bdot_raw(wk, st, "NN", ps)
            o = _bdot_raw(qd, st, "NN", ps) + _bdot_raw(at, u, "NN", ps)
            dos = []
            for h in range(GDN_HEADS):
                cols = slice(GDN_DIM * h, GDN_DIM * (h + 1))
                _, vjp2 = jax.vjp(_gated_norm, o[h], z_ref[rows, cols], gnw_ref[...])
                do_h, dz_h, dgn = vjp2(doa_ref[rows, cols])
                dz_ref[rows, cols] = dz_h
                dgnw = dgnw + dgn
                dos.append(do_h)
            do = jnp.stack(dos)
            ds_new = ds_scr[...]
            du = _bdot_raw(at, do, "TN", pb) + _bdot_raw(ke, ds_new, "NN", pb)
            duv_ref[:, rows, :] = du
            dat_ref[:, rows, :] = _bdot_raw(do, u, "NT", pb)
            dqd_ref[:, rows, :] = _bdot_raw(do, st, "NT", pb)
            dke_ref[:, rows, :] = _bdot_raw(u, ds_new, "NT", pb)
            dwk_ref[:, rows, :] = -_bdot_raw(du, st, "NT", pb)
            d_ge = jnp.sum(jnp.sum(st * ds_new, axis=2, keepdims=True), axis=1, keepdims=True)
            dge_ref[j] = jnp.broadcast_to(d_ge, (GDN_HEADS, 8, 128))
            ds_scr[...] = ge_ref[j][:, 0:1, 0:1] * ds_new + _bdot_raw(qd, do, "TN", pb) - _bdot_raw(wk, du, "TN", pb)
        dgnw_ref[...] += dgnw

    return pl.pallas_call(
        body, name="gdn_scan_bwd", grid=(nc // G,),
        in_specs=[sp["hd"], sp["hd"], sp["hd"], sp["hd"], sp["hc"], sp["ge"], sp["z"], sp["vec"], sp["st"], sp["oa"]],
        out_specs=(sp["hd"], sp["hd"], sp["hd"], sp["hd"], sp["hc"], sp["ge"], sp["oa"], sp["vec"]),
        out_shape=(_hd_shape(S), _hd_shape(S), _hd_shape(S), _hd_shape(S), _hd_shape(S, C),
                   jax.ShapeDtypeStruct((nc, GDN_HEADS, 8, 128), F32), jax.ShapeDtypeStruct((S, GDN_WIDTH), F32),
                   jax.ShapeDtypeStruct((1, 128), F32)),
        scratch_shapes=[pltpu.VMEM((GDN_HEADS, GDN_DIM, GDN_DIM), F32)],
        compiler_params=_cparams(("arbitrary",)),
    )(u_v, w_k, q_dec, k_end, attn, g_end, proj, gnw, states, d_oa)


def _gdn_post(proj, conv_w, gp, tinv, u_v, w_k, d_uv, d_wk, d_qd, d_ke, d_at, d_ge):
    S = proj.shape[0]
    C, G = GDN_CHUNK, GDN_PG
    nc = S // C
    sp = _gdn_pre_specs(S, G)
    pb = _GDN_PASSES["bwd"]

    def body(cur_ref, prev_ref, ba_ref, cw_ref, gp_ref, ti_ref, uv_ref, wk_ref, duv_ref, dwk_ref, dqd_ref, dke_ref,
             dat_ref, dge_ref, dpre_ref, dba_ref, dgp_ref):
        i = pl.program_id(0)

        @pl.when(i == 0)
        def _():
            dgp_ref[...] = jnp.zeros_like(dgp_ref)

        prev = prev_ref[...] * jnp.where(i == 0, 0.0, 1.0)
        pre = _conv_taps(jnp.concatenate([prev, cur_ref[...]], axis=0), cw_ref[...], GDN_CONV, C * G)
        sg = jax.nn.sigmoid(pre)
        dsilu = sg * (1.0 + pre * (1.0 - sg))
        pairs, args = _gdn_pairs(pre * sg, ba_ref[...], gp_ref[...], G)
        _, vjp1 = jax.vjp(functools.partial(_gdn_stage1, dot=_bdot), *args)

        def take(ref):
            return jnp.stack([ref[h, C * j:C * (j + 1), :] for j, h in pairs])

        t, u_v, w_k = take(ti_ref), take(uv_ref), take(wk_ref)
        d_v = _bdot_raw(t, take(duv_ref), "TN", pb)
        d_rk = _bdot_raw(t, take(dwk_ref), "TN", pb)
        d_l = -(_bdot_raw(d_v, u_v, "NT", pb) + _bdot_raw(d_rk, w_k, "NT", pb))
        d_ge = jnp.stack([dge_ref[j, h][0:1, 0:1] for j, h in pairs])
        dcq, dck, dcv, db, da, dalog, ddtb = vjp1((d_l, d_v, d_rk, take(dqd_ref), take(dat_ref), take(dke_ref), d_ge))
        lane = lax.broadcasted_iota(jnp.int32, (C, 128), 1)
        lane1 = lax.broadcasted_iota(jnp.int32, (1, 128), 1)
        dgp = jnp.zeros((1, 128), F32)
        for j in range(G):
            rows = slice(C * j, C * (j + 1))
            dba = jnp.zeros((C, 128), F32)
            for h in range(GDN_HEADS):
                b = GDN_HEADS * j + h
                for o_, dcx in ((0, dcq), (GDN_WIDTH, dck), (2 * GDN_WIDTH, dcv)):
                    cols = slice(o_ + GDN_DIM * h, o_ + GDN_DIM * (h + 1))
                    dpre_ref[rows, cols] = dcx[b] * dsilu[rows, cols]
                dba = dba + jnp.where(lane == h, db[b], 0.0) + jnp.where(lane == GDN_HEADS + h, da[b], 0.0)
                dgp = dgp + jnp.where(lane1 == h, dalog[b], 0.0) + jnp.where(lane1 == GDN_HEADS + h, ddtb[b], 0.0)
            dba_ref[rows, :] = dba
        dgp_ref[0:1, :] += dgp

    T = C * G
    return pl.pallas_call(
        body, name="gdn_post", grid=(nc // G,),
        in_specs=[sp["cur"], sp["prev"], sp["ba"], sp["cw"], sp["vec"], sp["hc"], sp["hd"], sp["hd"], sp["hd"], sp["hd"],
                  sp["hd"], sp["hd"], sp["hc"], sp["ge"]],
        out_specs=(sp["cur"], pl.BlockSpec((T, 128), lambda i: (i, 0)), pl.BlockSpec((8, 128), lambda i: (0, 0))),
        out_shape=(jax.ShapeDtypeStruct((S, 3 * GDN_WIDTH), F32), jax.ShapeDtypeStruct((S, 128), F32),
                   jax.ShapeDtypeStruct((8, 128), F32)),
        compiler_params=_cparams(("arbitrary",)),
    )(proj, proj, proj, conv_w, gp, tinv, u_v, w_k, d_uv, d_wk, d_qd, d_ke, d_at, d_ge)


def _conv_bwd(dpre, x, xcol0, w, K, name, tc):
    S, Cc = dpre.shape
    T = _pick_tile(S, 256)
    nt, ncol = S // T, Cc // tc
    xo = xcol0 // tc

    def body(d_ref, dn_ref, x_ref, xp_ref, w_ref, dx_ref, dw_ref):
        i = pl.program_id(1)
        dn = dn_ref[...] * jnp.where(i == nt - 1, 0.0, 1.0)
        dv = d_ref[...]
        ext_d = jnp.concatenate([dv, dn], axis=0)
        wv = w_ref[...]
        dx_ref[...] = _conv_taps_t(ext_d, wv, K, T).astype(dx_ref.dtype)
        xp = xp_ref[...] * jnp.where(i == 0, 0.0, 1.0)
        ext_x = jnp.concatenate([xp, x_ref[...]], axis=0)

        @pl.when(i == 0)
        def _():
            dw_ref[...] = jnp.zeros_like(dw_ref)

        for k in range(K):
            lo = 8 - (K - 1) + k
            dw_ref[k:k + 1, :] += jnp.sum(dv * ext_x[lo:lo + T, :], axis=0, keepdims=True)

    r8 = T // 8
    return pl.pallas_call(
        body, name=name, grid=(ncol, nt),
        in_specs=[pl.BlockSpec((T, tc), lambda j, i: (i, j)),
                  pl.BlockSpec((8, tc), lambda j, i: (jnp.minimum((i + 1) * r8, S // 8 - 1), j)),
                  pl.BlockSpec((T, tc), lambda j, i: (i, j + xo)),
                  pl.BlockSpec((8, tc), lambda j, i: (jnp.maximum(i * r8 - 1, 0), j + xo)),
                  pl.BlockSpec((K, tc), lambda j, i: (0, j))],
        out_specs=(pl.BlockSpec((T, tc), lambda j, i: (i, j)), pl.BlockSpec((K, tc), lambda j, i: (0, j))),
        out_shape=(jax.ShapeDtypeStruct((S, Cc), _MXU), jax.ShapeDtypeStruct((K, Cc), F32)),
        compiler_params=_cparams(("parallel", "arbitrary")),
    )(dpre, dpre, x, x, w)


def _dil_bias(nt, T):
    d = (np.arange(nt)[:, None, None] * T + np.arange(T)[None, None, :] - np.arange(T)[None, :, None])
    cnt = ((d >= 0) & (d <= 128)).astype(np.float64) + ((d >= 0) & (d % 4 == 0) & (d <= 512)) + ((d >= 0) & (d % 16 == 0))
    return jnp.asarray(np.where(cnt > 0, np.log(np.maximum(cnt, 1.0)), -1e30), dtype=F32)


def _attn_fwd(proj, mix):
    S = proj.shape[0]
    T = min(ATT_T, S)
    nt = S // T
    bias = _dil_bias(nt, T)
    scale = DIL_DIM ** -0.5
    npair = DIL_WIDTH // 128
    qb0, kb0, vb0 = P_QKVB // 128, (P_QKVB + DIL_WIDTH) // 128, (P_QKVB + 2 * DIL_WIDTH) // 128

    def body(q_ref, k_ref, v_ref, b_ref, mix_ref, o_ref, lse_ref):
        i = pl.program_id(1)
        qs = (q_ref[...] * scale).astype(_MXU)

        def step(j, carry):
            kt = k_ref[pl.ds(pl.multiple_of(j * T, T), T), :].astype(_MXU)
            vt = v_ref[pl.ds(pl.multiple_of(j * T, T), T), :].astype(_MXU)
            bt = b_ref[i - j]
            out = []
            for hh in range(2):
                m, l, acc = carry[hh]
                sl = slice(hh * DIL_DIM, (hh + 1) * DIL_DIM)
                s = lax.dot_general(kt[:, sl], qs[:, sl], (_NT, ((), ())), preferred_element_type=F32) + bt
                m_new = jnp.maximum(m, jnp.max(s, axis=0, keepdims=True))
                p = jnp.exp(s - m_new)
                a = jnp.exp(m - m_new)
                l = a * l + jnp.sum(p, axis=0, keepdims=True)
                acc = a * acc + lax.dot_general(vt[:, sl], p.astype(_MXU), (_TN, ((), ())), preferred_element_type=F32)
                out.append((m_new, l, acc))
            return tuple(out)

        init = tuple((jnp.full((1, T), -1e30, F32), jnp.zeros((1, T), F32), jnp.zeros((DIL_DIM, T), F32)) for _ in range(2))
        res = lax.fori_loop(0, i + 1, step, init)
        lse_ref[...] = jnp.zeros_like(lse_ref)
        for hh in range(2):
            m, l, acc = res[hh]
            o_ref[:, hh * DIL_DIM:(hh + 1) * DIL_DIM] = (acc / l).T
            lse_ref[hh:hh + 1, :] = m + jnp.log(l)

    return pl.pallas_call(
        body, name="attn_fwd", grid=(npair, nt),
        in_specs=[pl.BlockSpec((T, 128), lambda p, i: (i, qb0 + p)),
                  pl.BlockSpec((S, 128), lambda p, i: (0, kb0 + p)),
                  pl.BlockSpec((S, 128), lambda p, i: (0, vb0 + p)),
                  pl.BlockSpec((nt, T, T), lambda p, i: (0, 0, 0)), pl.BlockSpec(memory_space=pl.ANY)],
        out_specs=(pl.BlockSpec((T, 128), lambda p, i: (i, GDN_WIDTH // 128 + p)),
                   pl.BlockSpec((None, None, 8, T), lambda p, i: (p, i, 0, 0))),
        out_shape=(jax.ShapeDtypeStruct(mix.shape, F32), jax.ShapeDtypeStruct((npair, nt, 8, T), F32)),
        input_output_aliases={4: 0},
        compiler_params=_cparams(("parallel", "parallel")),
    )(proj, proj, proj, bias, mix)


def _attn_bwd(proj, mix, lse, d_mix):
    S = proj.shape[0]
    T = min(ATT_T, S)
    nt = S // T
    bias = _dil_bias(nt, T)
    scale = DIL_DIM ** -0.5
    npair = DIL_WIDTH // 128
    qb0, kb0, vb0 = P_QKVB // 128, (P_QKVB + DIL_WIDTH) // 128, (P_QKVB + 2 * DIL_WIDTH) // 128

    def body(q_ref, k_ref, v_ref, o_ref, lse_ref, do_ref, b_ref, dq_ref, dk_ref, dv_ref, dq_scr):
        j = pl.program_id(1)

        @pl.when(j == 0)
        def _():
            dq_scr[...] = jnp.zeros_like(dq_scr)

        kt = k_ref[...].astype(_MXU)
        vt = v_ref[...].astype(_MXU)
        ones = jnp.ones((8, DIL_DIM), F32)

        def step(i, carry):
            rows = pl.ds(pl.multiple_of(i * T, T), T)
            qs = (q_ref[rows, :] * scale).astype(_MXU)
            dov = do_ref[rows, :]
            prod = dov * o_ref[rows, :]
            lsev = lse_ref[i]
            dob = dov.astype(_MXU)
            bt = b_ref[i - j]
            out = []
            dqs = []
            for hh in range(2):
                dk, dv = carry[hh]
                sl = slice(hh * DIL_DIM, (hh + 1) * DIL_DIM)
                s = lax.dot_general(kt[:, sl], qs[:, sl], (_NT, ((), ())), preferred_element_type=F32) + bt
                p = jnp.exp(s - lsev[hh:hh + 1, :])
                delta = lax.dot_general(ones, prod[:, sl], (_NT, ((), ())), precision=_HI, preferred_element_type=F32)[0:1, :]
                dp = lax.dot_general(vt[:, sl], dob[:, sl], (_NT, ((), ())), preferred_element_type=F32)
                ds = (p * (dp - delta)).astype(_MXU)
                dv = dv + lax.dot_general(p.astype(_MXU), dob[:, sl], (_NN, ((), ())), preferred_element_type=F32)
                dk = dk + lax.dot_general(ds, qs[:, sl], (_NN, ((), ())), preferred_element_type=F32)
                dqs.append(lax.dot_general(ds, kt[:, sl], (_TN, ((), ())), preferred_element_type=F32) * scale)
                out.append((dk, dv))
            dq_scr[rows, :] += jnp.concatenate(dqs, axis=1)
            return tuple(out)

        init = tuple((jnp.zeros((T, DIL_DIM), F32), jnp.zeros((T, DIL_DIM), F32)) for _ in range(2))
        res = lax.fori_loop(j, nt, step, init)
        dk_ref[...] = jnp.concatenate([res[0][0], res[1][0]], axis=1).astype(dk_ref.dtype)
        dv_ref[...] = jnp.concatenate([res[0][1], res[1][1]], axis=1).astype(dv_ref.dtype)

        @pl.when(j == nt - 1)
        def _():
            dq_ref[...] = dq_scr[...].astype(dq_ref.dtype)

    full = lambda c0: pl.BlockSpec((S, 128), lambda p, j: (0, c0 + p))
    tile = lambda c0: pl.BlockSpec((T, 128), lambda p, j: (j, c0 + p))
    out3 = jax.ShapeDtypeStruct((S, DIL_WIDTH), _MXU)
    return pl.pallas_call(
        body, name="attn_bwd", grid=(npair, nt),
        in_specs=[full(qb0), tile(kb0), tile(vb0), full(GDN_WIDTH // 128),
                  pl.BlockSpec((None, nt, 8, T), lambda p, j: (p, 0, 0, 0)), full(GDN_WIDTH // 128),
                  pl.BlockSpec((nt, T, T), lambda p, j: (0, 0, 0))],
        out_specs=(full(0), tile(0), tile(0)),
        out_shape=(out3, out3, out3),
        scratch_shapes=[pltpu.VMEM((S, 128), F32)],
        compiler_params=_cparams(("parallel", "arbitrary")),
    )(proj, proj, proj, mix, lse, d_mix, bias)


def _ffn_act(up, cw):
    S, Cc = up.shape[0], up.shape[1] // 2
    T, tc = _pick_tile(S, 256), _pick_tile(Cc, 1536)
    r8 = T // 8
    nct = Cc // tc

    def body(g_ref, gp_ref, u_ref, up_ref, wg_ref, wu_ref, o_ref):
        keep = jnp.where(pl.program_id(1) == 0, 0.0, 1.0)
        cg = _conv_taps(jnp.concatenate([gp_ref[...] * keep, g_ref[...]], axis=0), wg_ref[...], FFN_CONV, T)
        cu = _conv_taps(jnp.concatenate([up_ref[...] * keep, u_ref[...]], axis=0), wu_ref[...], FFN_CONV, T)
        o_ref[...] = (_silu(cg) * cu).astype(o_ref.dtype)

    cur = lambda o: pl.BlockSpec((T, tc), lambda j, i: (i, j + o))
    prev = lambda o: pl.BlockSpec((8, tc), lambda j, i: (jnp.maximum(i * r8 - 1, 0), j + o))
    wsp = lambda o: pl.BlockSpec((FFN_CONV, tc), lambda j, i: (0, j + o))
    return pl.pallas_call(
        body, name="ffn_act", grid=(nct, S // T),
        in_specs=[cur(0), prev(0), cur(nct), prev(nct), wsp(0), wsp(nct)], out_specs=cur(0),
        out_shape=jax.ShapeDtypeStruct((S, Cc), _MXU),
        compiler_params=_cparams(("parallel", "parallel")),
    )(up, up, up, up, cw, cw)


def _ffn_act_bwd(d_act, up, cw):
    S, Cc = up.shape[0], up.shape[1] // 2
    T, tc = _pick_tile(S, 256), _pick_tile(Cc, 1536)
    r8 = T // 8
    nt = S // T
    nct = Cc // tc
    K = FFN_CONV

    def body(da_ref, dan_ref, g_ref, gp_ref, gn_ref, u_ref, up_ref, un_ref, wg_ref, wu_ref,
             dg_ref, du_ref, dwg_ref, dwu_ref):
        i = pl.program_id(1)
        keep_p = jnp.where(i == 0, 0.0, 1.0)
        keep_n = jnp.where(i == nt - 1, 0.0, 1.0)
        wg, wu = wg_ref[...], wu_ref[...]
        xg = jnp.concatenate([gp_ref[...] * keep_p, g_ref[...], gn_ref[...] * keep_n], axis=0)
        xu = jnp.concatenate([up_ref[...] * keep_p, u_ref[...], un_ref[...] * keep_n], axis=0)
        cg = _conv_taps(xg, wg, K, T + 8)
        cu = _conv_taps(xu, wu, K, T + 8)
        da = jnp.concatenate([da_ref[...], dan_ref[...] * keep_n], axis=0)
        sg = jax.nn.sigmoid(cg)
        d_cg = da * cu * (sg * (1.0 + cg * (1.0 - sg)))
        d_cu = da * (cg * sg)
        dg_ref[...] = _conv_taps_t(d_cg, wg, K, T).astype(dg_ref.dtype)
        du_ref[...] = _conv_taps_t(d_cu, wu, K, T).astype(du_ref.dtype)

        @pl.when(i == 0)
        def _():
            dwg_ref[...] = jnp.zeros_like(dwg_ref)
            dwu_ref[...] = jnp.zeros_like(dwu_ref)

        for k in range(K):
            lo = 8 - (K - 1) + k
            dwg_ref[k:k + 1, :] += jnp.sum(d_cg[0:T, :] * xg[lo:lo + T, :], axis=0, keepdims=True)
            dwu_ref[k:k + 1, :] += jnp.sum(d_cu[0:T, :] * xu[lo:lo + T, :], axis=0, keepdims=True)

    cur = lambda o: pl.BlockSpec((T, tc), lambda j, i: (i, j + o))
    prev = lambda o: pl.BlockSpec((8, tc), lambda j, i: (jnp.maximum(i * r8 - 1, 0), j + o))
    nxt = lambda o: pl.BlockSpec((8, tc), lambda j, i: (jnp.minimum((i + 1) * r8, S // 8 - 1), j + o))
    wsp = lambda o: pl.BlockSpec((K, tc), lambda j, i: (0, j + o))
    return pl.pallas_call(
        body, name="ffn_act_bwd", grid=(nct, nt),
        in_specs=[cur(0), nxt(0), cur(0), prev(0), nxt(0), cur(nct), prev(nct), nxt(nct), wsp(0), wsp(nct)],
        out_specs=(cur(0), cur(0), wsp(0), wsp(0)),
        out_shape=(jax.ShapeDtypeStruct((S, Cc), _MXU), jax.ShapeDtypeStruct((S, Cc), _MXU),
                   jax.ShapeDtypeStruct((K, Cc), F32), jax.ShapeDtypeStruct((K, Cc), F32)),
        compiler_params=_cparams(("parallel", "arbitrary")),
    )(d_act, d_act, up, up, up, up, up, up, cw, cw)


def _local_step(x, tgt, n1w, n2w, fnw, gp, gnw, wp, conv_w, fcw, rest_weights, early_grads):
    h1 = _rmsnorm_fwd(x, n1w, "norm1")
    proj = _mm(h1, wp, "nn", name="proj")
    u_v, w_k, q_dec, k_end, attn, tinv, g_end = _gdn_pre(proj, conv_w, gp)
    mix, states = _gdn_scan(u_v, w_k, q_dec, k_end, attn, g_end, proj, gnw)
    mix, lse = _attn_fwd(proj, mix)
    w_out, w_up4, w_down = rest_weights([mix])
    x2 = _mm(mix, w_out, "nn", residual=x, name="outproj")
    h2 = _rmsnorm_fwd(x2, n2w, "norm2")
    up = _mm(h2, w_up4, "nn", b_blocks=True, name="up")
    act = _ffn_act(up, fcw)
    x3 = _mm(act, w_down, "nn", residual=x2, name="down")
    loss, dx3, d_fnw = _loss_head(x3, fnw, tgt, "loss_head")
    d_act = _mm(dx3, w_down, "nt", name="d_act")
    d_wdown = _mm(act, dx3, "tn", name="d_wdown")
    d_upg, d_upu, d_fcwg, d_fcwu = _ffn_act_bwd(d_act, up, fcw)
    d_wup = _mm(h2, d_upg, "tn", place=("blocks", N_CHIPS, 0), tn=w_up4.shape[2], name="d_wgate")
    d_wup = _mm(h2, d_upu, "tn", place=("blocks", N_CHIPS, N_CHIPS // 2), tn=w_up4.shape[2], into=d_wup, name="d_wup")
    token = early_grads[0](d_wup, d_wdown)
    d_h2 = _mm_nt_blocks([d_upg, d_upu], w_up4, "d_h2")
    dx2, d_n2w = _rmsnorm_bwd(d_h2, x2, n2w + token[0:1, 0:1], dx3, "norm2_bwd")
    token = early_grads[1](dx2)
    d_mix = _mm(dx2, w_out, "nt", name="d_mix")
    d_wout = _mm(mix, dx2, "tn", name="d_wout")
    dq_b, dk_b, dv_b = _attn_bwd(proj, mix, lse, d_mix)
    d_uv, d_wk, d_qd, d_ke, d_at, d_ge, d_z, d_gnw = _gdn_scan_bwd(u_v, w_k, q_dec, k_end, attn, g_end, proj,
                                                                   gnw + token[0:1, 0:1], states, d_mix)
    d_pre, d_ba, d_gp = _gdn_post(proj, conv_w, gp, tinv, u_v, w_k, d_uv, d_wk, d_qd, d_ke, d_at, d_ge)
    d_qkva, d_convw = _conv_bwd(d_pre, proj, 0, conv_w, GDN_CONV, "gdn_conv_bwd", 512)
    d_proj = jnp.concatenate([d_qkva, d_z.astype(_MXU), dq_b, dk_b, dv_b, d_ba.astype(_MXU),
                              jnp.zeros((x.shape[0], P_COLS - P_BA - 128), _MXU)], axis=1)
    d_wp = _mm(h1, d_proj, "tn", name="d_wp")
    d_h1 = _mm(d_proj, wp, "nt", name="d_h1")
    dx, d_n1w = _rmsnorm_bwd(d_h1, x, n1w, dx2, "norm1_bwd")
    grads = dict(wp=d_wp, conv_w=d_convw, w_out=d_wout, w_up=d_wup, fcw_g=d_fcwg, fcw_u=d_fcwu, w_down=d_wdown,
                 n1w=d_n1w, n2w=d_n2w, fnw=d_fnw, gp=d_gp, gnw=d_gnw)
    return loss, dx, grads


_HBM = pl.BlockSpec(memory_space=pltpu.HBM)


def _pos():
    return lax.axis_index("x"), lax.axis_index("y"), lax.axis_index("c")


def _other_chips(x, y):
    return [(1 - x, y), (x, 1 - y), (1 - x, 1 - y)]


def _halvable(shape):
    return shape[0] % 32 == 0


def _rows_of_half(shape, half):
    if not _halvable(shape):
        return pl.ds(0, shape[0])
    return pl.ds(pl.multiple_of(half * (shape[0] // 2), 16), shape[0] // 2)


def _gather_halves(shards, name):
    n = len(shards)
    shapes = [s.shape for s in shards]

    def body(*refs):
        ins, outs = refs[:n], refs[n:2 * n]
        send_sems, recv_sems = refs[2 * n:]
        x, y, c = _pos()
        q = 2 * x + y
        chips = _other_chips(x, y)

        def copy(a, j, block):
            px, py = chips[j]
            rows = _rows_of_half(shapes[a], c)
            return pltpu.make_async_remote_copy(
                src_ref=ins[a].at[rows, :], dst_ref=outs[a].at[block, rows, :], send_sem=send_sems.at[3 * a + j],
                recv_sem=recv_sems.at[3 * a + j], device_id=(px, py, c), device_id_type=MESH)

        sends = [copy(a, j, q) for a in range(n) for j in range(3)]
        for cp in sends:
            cp.start()
        for a in range(n):
            for j, (px, py) in enumerate(chips):
                copy(a, j, 2 * px + py).wait_recv()
        for cp in sends:
            cp.wait_send()

    return pl.pallas_call(
        body, name=name, in_specs=[_HBM] * n, out_specs=[_HBM] * n,
        out_shape=[jax.ShapeDtypeStruct((N_CHIPS,) + s.shape, s.dtype) for s in shards],
        scratch_shapes=[pltpu.SemaphoreType.DMA((3 * n,)), pltpu.SemaphoreType.DMA((3 * n,))],
    )(*shards)


_SEM = pl.BlockSpec(memory_space=pltpu.SEMAPHORE)
_ANY = pl.BlockSpec(memory_space=pl.ANY)
_DATAFLOW = pltpu.SideEffectType.DATAFLOW_SIDE_EFFECTING


def _in_hbm(a):
    return pltpu.with_memory_space_constraint(a, pltpu.HBM)


def _halves_copy(src_refs, land_refs, send_sems, recv_sems, shapes, a, j, block, x, y, c):
    px, py = _other_chips(x, y)[j]
    rows = _rows_of_half(shapes[a], c)
    return pltpu.make_async_remote_copy(
        src_ref=src_refs[a].at[rows, :], dst_ref=land_refs[a].at[block, rows, :], send_sem=send_sems.at[3 * a + j],
        recv_sem=recv_sems.at[3 * a + j], device_id=(px, py, c), device_id_type=MESH)


def _gather_halves_start(shards, after, name):
    n = len(shards)
    shapes = [s.shape for s in shards]

    def body(*refs):
        ins, lands = refs[:n], refs[n:2 * n]
        send_sems, recv_sems = refs[2 * n + 1], refs[2 * n + 2]
        token = refs[-1]
        x, y, c = _pos()
        q = 2 * x + y
        for a in range(n):
            for j in range(3):
                _halves_copy(ins, lands, send_sems, recv_sems, shapes, a, j, q, x, y, c).start()
        token[...] = jnp.zeros_like(token)

    land_shapes = [(N_CHIPS,) + s.shape for s in shards]
    return pl.pallas_call(
        body, name=name,
        out_shape=(pltpu.SemaphoreType.DMA((3 * n,)), pltpu.SemaphoreType.DMA((3 * n,)),
                   *[pltpu.HBM(s.shape, s.dtype) for s in shards],
                   *[pltpu.HBM(ls, s.dtype) for ls, s in zip(land_shapes, shards)],
                   jax.ShapeDtypeStruct((8, 128), F32)),
        in_specs=[_HBM] * (2 * n) + [_ANY],
        out_specs=(_SEM, _SEM, *[_HBM] * (2 * n), pl.BlockSpec(memory_space=pltpu.VMEM)),
        input_output_aliases={a: 2 + a for a in range(2 * n)},
        compiler_params=pltpu.CompilerParams(has_side_effects=_DATAFLOW),
    )(*[_in_hbm(s) for s in shards], *[_in_hbm(lax.empty(ls, s.dtype)) for ls, s in zip(land_shapes, shards)], after)


def _gather_halves_wait(started, after, name):
    send_sems, recv_sems, *thru = started
    n = len(thru) // 2
    shapes = [t.shape for t in thru[:n]]

    def body(*refs):
        ins, lands = refs[:n], refs[n:2 * n]
        send_sems, recv_sems = refs[2 * n], refs[2 * n + 1]
        x, y, c = _pos()
        q = 2 * x + y
        chips = _other_chips(x, y)
        for a in range(n):
            for j, (px, py) in enumerate(chips):
                _halves_copy(ins, lands, send_sems, recv_sems, shapes, a, j, q, x, y, c).wait_send()
                _halves_copy(ins, lands, send_sems, recv_sems, shapes, a, j, 2 * px + py, x, y, c).wait_recv()

    outs = pl.pallas_call(
        body, name=name, out_shape=[pltpu.HBM(t.shape, t.dtype) for t in thru],
        in_specs=[_HBM] * (2 * n) + [_SEM, _SEM] + [_ANY] * len(after), out_specs=[_HBM] * (2 * n),
        input_output_aliases={a: a for a in range(2 * n)},
        compiler_params=pltpu.CompilerParams(has_side_effects=_DATAFLOW),
    )(*thru, send_sems, recv_sems, *after)
    return outs[:n], outs[n:]


def _sibling_fill(gathered, name):
    big = [a for a, g in enumerate(gathered) if _halvable(g.shape[1:])]
    n = len(gathered)

    def body(*refs):
        ins, outs = refs[:n], refs[n:2 * n]
        send_sems, recv_sems = refs[2 * n:]
        x, y, c = _pos()
        chips = _other_chips(x, y)

        def copy(k, j, half):
            a = big[k]
            px, py = chips[j]
            rows = _rows_of_half(gathered[a].shape[1:], half)
            return pltpu.make_async_remote_copy(
                src_ref=ins[a].at[2 * px + py, rows, :], dst_ref=outs[a].at[2 * px + py, rows, :],
                send_sem=send_sems.at[3 * k + j], recv_sem=recv_sems.at[3 * k + j],
                device_id=(x, y, 1 - c), device_id_type=MESH)

        sends = [copy(k, j, c) for k in range(len(big)) for j in range(3)]
        for cp in sends:
            cp.start()
        for k in range(len(big)):
            for j in range(3):
                copy(k, j, 1 - c).wait_recv()
        for cp in sends:
            cp.wait_send()

    return pl.pallas_call(
        body, name=name, in_specs=[_HBM] * n, out_specs=[_HBM] * n,
        out_shape=[jax.ShapeDtypeStruct(g.shape, g.dtype) for g in gathered],
        input_output_aliases={a: a for a in range(n)},
        scratch_shapes=[pltpu.SemaphoreType.DMA((3 * len(big),)), pltpu.SemaphoreType.DMA((3 * len(big),))],
    )(*gathered)


def _place_own(shards, gathered, cq, name):
    n = len(shards)
    steps = 4

    def body(cq_ref, *refs):
        for a in range(n):
            refs[2 * n + a][...] = refs[a][...]

    def tile(shape):
        return shape[0] // steps if _halvable(shape) else shape[0]

    in_specs = [pl.BlockSpec((tile(s.shape), s.shape[1]), (lambda i, s_: (i, 0)) if _halvable(s.shape) else (lambda i, s_: (0, 0)))
                for s in shards]
    in_specs += [pl.BlockSpec(memory_space=pl.ANY)] * n
    out_specs = [pl.BlockSpec((None, tile(s.shape), s.shape[1]),
                              (lambda i, s_: (s_[1], i, 0)) if _halvable(s.shape) else (lambda i, s_: (s_[1], 0, 0)))
                 for s in shards]
    gs = pltpu.PrefetchScalarGridSpec(num_scalar_prefetch=1, grid=(steps,), in_specs=in_specs, out_specs=out_specs)
    return pl.pallas_call(
        body, name=name, grid_spec=gs, out_shape=[jax.ShapeDtypeStruct(g.shape, g.dtype) for g in gathered],
        input_output_aliases={1 + n + a: a for a in range(n)},
        compiler_params=_cparams(("arbitrary",)),
    )(cq, *shards, *gathered)


def _half_rows(ref, c, rh):
    return ref.at[:, pl.ds(pl.multiple_of(c * rh, 8), rh), :]


def _grad_sibling(fams, small, name):
    n = len(fams)
    ns = 0 if small is None else 1
    rhs = [f.shape[1] // 2 for f in fams]

    def body(*refs):
        ins, outs = refs[:n], refs[n + ns:2 * n + ns]
        send_sems, recv_sems = refs[2 * (n + ns)], refs[2 * (n + ns) + 1]
        x, y, c = _pos()
        bigs = [pltpu.make_async_remote_copy(src_ref=_half_rows(ins[a], 1 - c, rhs[a]), dst_ref=outs[a],
                                             send_sem=send_sems.at[7 * ns + a], recv_sem=recv_sems.at[7 * ns + a],
                                             device_id=(x, y, 1 - c), device_id_type=MESH) for a in range(n)]
        for cp in bigs:
            cp.start()
        sends = []
        if ns:
            small_ref, all_ref, loc_sem = refs[n], refs[2 * n + 1], refs[2 * n + 4]
            me = 4 * x + 2 * y + c
            mine = pltpu.make_async_copy(small_ref, all_ref.at[me], loc_sem)
            mine.start()

            def peer(r):
                dx, dy, dc = (r >> 2) & 1, (r >> 1) & 1, r & 1
                return (x if dx == 0 else 1 - x), (y if dy == 0 else 1 - y), (c if dc == 0 else 1 - c)

            def small_copy(r, slot):
                return pltpu.make_async_remote_copy(src_ref=small_ref, dst_ref=all_ref.at[slot], send_sem=send_sems.at[r - 1],
                                                    recv_sem=recv_sems.at[r - 1], device_id=peer(r), device_id_type=MESH)

            sends = [small_copy(r, me) for r in range(1, 8)]
            for cp in sends:
                cp.start()
            for r in range(1, 8):
                px, py, pc = peer(r)
                small_copy(r, 4 * px + 2 * py + pc).wait_recv()
        for cp in bigs:
            cp.wait_recv()
        for cp in bigs + sends:
            cp.wait_send()
        if ns:
            mine.wait()

    return pl.pallas_call(
        body, name=name, in_specs=[_HBM] * (n + ns), out_specs=[_HBM] * (n + ns),
        out_shape=[jax.ShapeDtypeStruct((f.shape[0], f.shape[1] // 2, f.shape[2]), f.dtype) for f in fams]
        + ([jax.ShapeDtypeStruct((8,) + small.shape, small.dtype)] if ns else []),
        scratch_shapes=[pltpu.SemaphoreType.DMA((7 * ns + n,)), pltpu.SemaphoreType.DMA((7 * ns + n,))]
        + ([pltpu.SemaphoreType.DMA] if ns else []),
    )(*fams, *([small] if ns else []))


def _chips_copy(src_refs, land_refs, send_sems, recv_sems, a, j, x, y, c):
    px, py = _other_chips(x, y)[j]
    return pltpu.make_async_remote_copy(src_ref=src_refs[a].at[2 * px + py], dst_ref=land_refs[a].at[j],
                                        send_sem=send_sems.at[3 * a + j], recv_sem=recv_sems.at[3 * a + j],
                                        device_id=(px, py, c), device_id_type=MESH)


def _grad_chips_start(parts, name):
    n = len(parts)

    def body(*refs):
        ins, lands = refs[:n], refs[n:2 * n]
        send_sems, recv_sems = refs[2 * n], refs[2 * n + 1]
        token = refs[-1]
        x, y, c = _pos()
        for a in range(n):
            for j in range(3):
                _chips_copy(ins, lands, send_sems, recv_sems, a, j, x, y, c).start()
        token[...] = jnp.zeros_like(token)

    land_shapes = [(3,) + p.shape[1:] for p in parts]
    return pl.pallas_call(
        body, name=name,
        out_shape=(pltpu.SemaphoreType.DMA((3 * n,)), pltpu.SemaphoreType.DMA((3 * n,)),
                   *[pltpu.HBM(p.shape, p.dtype) for p in parts],
                   *[pltpu.HBM(ls, p.dtype) for ls, p in zip(land_shapes, parts)],
                   jax.ShapeDtypeStruct((8, 128), F32)),
        in_specs=[_HBM] * (2 * n),
        out_specs=(_SEM, _SEM, *[_HBM] * (2 * n), pl.BlockSpec(memory_space=pltpu.VMEM)),
        input_output_aliases={a: 2 + a for a in range(2 * n)},
        compiler_params=pltpu.CompilerParams(has_side_effects=_DATAFLOW),
    )(*[_in_hbm(p) for p in parts], *[_in_hbm(lax.empty(ls, p.dtype)) for ls, p in zip(land_shapes, parts)])


def _grad_chips_wait(started, after, name):
    send_sems, recv_sems, *thru = started
    n = len(thru) // 2

    def body(*refs):
        ins, lands = refs[:n], refs[n:2 * n]
        send_sems, recv_sems = refs[2 * n], refs[2 * n + 1]
        x, y, c = _pos()
        for a in range(n):
            for j in range(3):
                cp = _chips_copy(ins, lands, send_sems, recv_sems, a, j, x, y, c)
                cp.wait_send()
                cp.wait_recv()

    outs = pl.pallas_call(
        body, name=name, out_shape=[pltpu.HBM(t.shape, t.dtype) for t in thru],
        in_specs=[_HBM] * (2 * n) + [_SEM, _SEM] + [_ANY] * len(after), out_specs=[_HBM] * (2 * n),
        input_output_aliases={a: a for a in range(2 * n)},
        compiler_params=pltpu.CompilerParams(has_side_effects=_DATAFLOW),
    )(*thru, send_sems, recv_sems, *after)
    return outs[n:]


def _sibling_copy(src_refs, land_refs, send_sems, recv_sems, rhs, a, c, x, y):
    return pltpu.make_async_remote_copy(src_ref=_half_rows(src_refs[a], 1 - c, rhs[a]), dst_ref=land_refs[a],
                                        send_sem=send_sems.at[a], recv_sem=recv_sems.at[a],
                                        device_id=(x, y, 1 - c), device_id_type=MESH)


def _grad_sibling_start(fams, name):
    n = len(fams)
    rhs = [f.shape[1] // 2 for f in fams]

    def body(*refs):
        ins, lands = refs[:n], refs[n:2 * n]
        send_sems, recv_sems = refs[2 * n], refs[2 * n + 1]
        token = refs[-1]
        x, y, c = _pos()
        for a in range(n):
            _sibling_copy(ins, lands, send_sems, recv_sems, rhs, a, c, x, y).start()
        token[...] = jnp.zeros_like(token)

    land_shapes = [(f.shape[0], f.shape[1] // 2, f.shape[2]) for f in fams]
    return pl.pallas_call(
        body, name=name,
        out_shape=(pltpu.SemaphoreType.DMA((n,)), pltpu.SemaphoreType.DMA((n,)),
                   *[pltpu.HBM(f.shape, f.dtype) for f in fams],
                   *[pltpu.HBM(ls, f.dtype) for ls, f in zip(land_shapes, fams)],
                   jax.ShapeDtypeStruct((8, 128), F32)),
        in_specs=[_HBM] * (2 * n),
        out_specs=(_SEM, _SEM, *[_HBM] * (2 * n), pl.BlockSpec(memory_space=pltpu.VMEM)),
        input_output_aliases={a: 2 + a for a in range(2 * n)},
        compiler_params=pltpu.CompilerParams(has_side_effects=_DATAFLOW),
    )(*[_in_hbm(f) for f in fams], *[_in_hbm(lax.empty(ls, f.dtype)) for ls, f in zip(land_shapes, fams)])


def _grad_sibling_wait(started, after, name):
    send_sems, recv_sems, *thru = started
    n = len(thru) // 2
    rhs = [t.shape[1] // 2 for t in thru[:n]]

    def body(*refs):
        ins, lands = refs[:n], refs[n:2 * n]
        send_sems, recv_sems = refs[2 * n], refs[2 * n + 1]
        x, y, c = _pos()
        for a in range(n):
            cp = _sibling_copy(ins, lands, send_sems, recv_sems, rhs, a, c, x, y)
            cp.wait_send()
            cp.wait_recv()

    outs = pl.pallas_call(
        body, name=name, out_shape=[pltpu.HBM(t.shape, t.dtype) for t in thru],
        in_specs=[_HBM] * (2 * n) + [_SEM, _SEM] + [_ANY] * len(after), out_specs=[_HBM] * (2 * n),
        input_output_aliases={a: a for a in range(2 * n)},
        compiler_params=pltpu.CompilerParams(has_side_effects=_DATAFLOW),
    )(*thru, send_sems, recv_sems, *after)
    return outs[:n], outs[n:]


def _grad_share(fulls, name):
    n = len(fulls)
    rhs = [f.shape[0] // 2 for f in fulls]

    def body(*refs):
        ins, outs = refs[:n], refs[n:2 * n]
        send_sems, recv_sems = refs[2 * n:]
        x, y, c = _pos()

        def copy(a, half):
            rows = pl.ds(pl.multiple_of(half * rhs[a], 8), rhs[a])
            return pltpu.make_async_remote_copy(src_ref=ins[a].at[rows, :], dst_ref=outs[a].at[rows, :],
                                                send_sem=send_sems.at[a], recv_sem=recv_sems.at[a],
                                                device_id=(x, y, 1 - c), device_id_type=MESH)

        sends = [copy(a, c) for a in range(n)]
        for cp in sends:
            cp.start()
        for a in range(n):
            copy(a, 1 - c).wait_recv()
        for cp in sends:
            cp.wait_send()

    return pl.pallas_call(
        body, name=name, in_specs=[_HBM] * n, out_specs=[_HBM] * n,
        out_shape=[jax.ShapeDtypeStruct(f.shape, f.dtype) for f in fulls],
        input_output_aliases={a: a for a in range(n)},
        scratch_shapes=[pltpu.SemaphoreType.DMA((n,)), pltpu.SemaphoreType.DMA((n,))],
    )(*fulls)


def _add_sibling(own, recv, cq, name):
    nb, R, Cc = own.shape
    Rh = R // 2

    def body(cq_ref, a_ref, b_ref, o32_ref, o16_ref):
        s = a_ref[...] + b_ref[...]
        o32_ref[...] = s
        o16_ref[...] = s.astype(o16_ref.dtype)

    sp = pl.BlockSpec((1, Rh, Cc), lambda b, s: (b, 0, 0))
    gs = pltpu.PrefetchScalarGridSpec(
        num_scalar_prefetch=1, grid=(nb,),
        in_specs=[pl.BlockSpec((1, Rh, Cc), lambda b, s: (b, s[0], 0)), sp], out_specs=[sp, sp])
    return pl.pallas_call(
        body, name=name, grid_spec=gs,
        out_shape=[jax.ShapeDtypeStruct((nb, Rh, Cc), F32), jax.ShapeDtypeStruct((nb, Rh, Cc), _MXU)],
        compiler_params=_cparams(("parallel",)),
    )(cq, own, recv)


def _add_chips(part32, recv3, cq, name):
    nb, Rh, Cc = part32.shape

    def body(cq_ref, a_ref, b_ref, o_ref):
        acc = a_ref[0]
        for j in range(3):
            acc = acc + b_ref[j].astype(F32)
        o_ref[...] = acc

    gs = pltpu.PrefetchScalarGridSpec(
        num_scalar_prefetch=1, grid=(1,),
        in_specs=[pl.BlockSpec((1, Rh, Cc), lambda i, s: (s[1], 0, 0)), pl.BlockSpec((3, Rh, Cc), lambda i, s: (0, 0, 0))],
        out_specs=pl.BlockSpec((Rh, Cc), lambda i, s: (s[0], 0)))
    return pl.pallas_call(
        body, name=name, grid_spec=gs, out_shape=jax.ShapeDtypeStruct((2 * Rh, Cc), F32),
        compiler_params=_cparams(("arbitrary",)),
    )(cq, part32, recv3)


def _sum_devices(small_all):
    def body(s_ref, o_ref):
        tot = s_ref[0]
        for d in range(1, 8):
            tot = tot + s_ref[d]
        o_ref[...] = tot

    return pl.pallas_call(body, name="sum_devices", out_shape=jax.ShapeDtypeStruct(small_all.shape[1:], F32))(small_all)


def _adamw(w, g, m, v, name):
    R, Cc = w.shape
    T = max([t for t in range(8, 257, 8) if R % t == 0], default=R)
    c1 = 1.0 / (1.0 - ADAM_B1 ** ADAM_STEP)
    c2 = 1.0 / (1.0 - ADAM_B2 ** ADAM_STEP)

    def body(w_ref, g_ref, m_ref, v_ref, d_ref, mo_ref, vo_ref):
        gv = g_ref[...]
        mn = ADAM_B1 * m_ref[...] + (1.0 - ADAM_B1) * gv
        vn = ADAM_B2 * v_ref[...] + (1.0 - ADAM_B2) * (gv * gv)
        mo_ref[...] = mn
        vo_ref[...] = vn
        d_ref[...] = -ADAM_LR * ((mn * c1) / (jnp.sqrt(vn * c2) + ADAM_EPS) + ADAM_WD * w_ref[...])

    sp = pl.BlockSpec((T, Cc), lambda i: (i, 0))
    sh = jax.ShapeDtypeStruct((R, Cc), F32)
    return pl.pallas_call(
        body, name=name, grid=(R // T,), in_specs=[sp] * 4, out_specs=(sp, sp, sp), out_shape=(sh, sh, sh),
        compiler_params=_cparams(("parallel",)),
    )(w, g, m, v)


SMALL_ROWS = 32
REPL_ROWS = 8


def _pad_lanes(v, n=D_MODEL):
    return jnp.pad(v, ((0, 0), (0, n - v.shape[1])))


def kernel(x, norm1_w, w_in, conv_qkv_w, a_log, dt_bias, gdn_norm_w, w_out, norm2_w, w_up, ffn_conv_w, w_down, final_norm_w, loss_target, m_norm1_w, m_w_in, m_conv_qkv_w, m_a_log, m_dt_bias, m_gdn_norm_w, m_w_out, m_norm2_w, m_w_up, m_ffn_conv_w, m_w_down, m_final_norm_w, v_norm1_w, v_w_in, v_conv_qkv_w, v_a_log, v_dt_bias, v_gdn_norm_w, v_w_out, v_norm2_w, v_w_up, v_ffn_conv_w, v_w_down, v_final_norm_w):
    c = lax.axis_index("c")
    q = 2 * lax.axis_index("x") + lax.axis_index("y")
    S = x.shape[1]
    cq = jnp.stack([c, q]).astype(jnp.int32)

    def gather(shards, tag):
        got = _gather_halves(shards, "gather_" + tag)
        got = _sibling_fill(got, "fill_" + tag)
        return _place_own(shards, got, cq, "place_" + tag)

    g_in, g_conv, g_fconv = gather([w_in[0].astype(_MXU), conv_qkv_w[0], ffn_conv_w[0]], "in")
    rest = [w_out[0].astype(_MXU), w_up[0].astype(_MXU), w_down[0].astype(_MXU)]
    *rest_started, token = _gather_halves_start(rest, g_conv, "gather_rest_start")

    def rest_weights(after):
        shards, got = _gather_halves_wait(rest_started, after, "gather_rest_wait")
        got = _sibling_fill(got, "fill_rest")
        g_out, g_up, g_down = _place_own(shards, got, cq, "place_rest")
        return g_out.reshape(D_MODEL, D_MODEL), g_up, g_down.reshape(D_FF, D_MODEL)
    wp = _wp_assemble(g_in)
    conv_f = jnp.concatenate([g_conv[i] for i in range(N_CHIPS)], axis=1)
    fcw = jnp.concatenate([g_fconv[i] for i in range(N_CHIPS)], axis=1)
    gp = _pad_lanes(jnp.concatenate([a_log, dt_bias], axis=1), 128)
    fnw = final_norm_w[None, :]
    early = {}

    def early_sibling(d_wup, d_wdown):
        *early["sibling"], tok = _grad_sibling_start([d_wup, d_wdown.reshape(N_CHIPS, D_FF // N_CHIPS, D_MODEL)],
                                                     "grad_sibling_early_start")
        return tok

    def early_chips(dx2):
        fams_e, got_e = _grad_sibling_wait(early["sibling"], [dx2], "grad_sibling_early_wait")
        early["parts"] = [_add_sibling(f, r, cq, "add_sibling_" + nm) for f, r, nm in zip(fams_e, got_e, ("w_up", "w_down"))]
        *early["started"], tok = _grad_chips_start([p[1] for p in early["parts"]], "grad_chips_start")
        return tok

    early_grads = (early_sibling, early_chips)

    loss_l, dx, g = _local_step(x[0], loss_target[0], norm1_w + token[0:1, 0:1], norm2_w, fnw, gp, gdn_norm_w, wp, conv_f,
                                fcw, rest_weights, early_grads)
    fams = [_win_split(g["wp"]), g["w_out"].reshape(N_CHIPS, D_MODEL // N_CHIPS, D_MODEL)]
    n_fc = FFN_CONV * D_FF

    def rows_of(v):
        flat = v.reshape(-1)
        return jnp.pad(flat, (0, -flat.shape[0] % D_MODEL)).reshape(-1, D_MODEL)

    gp_row = _pad_lanes(jnp.concatenate([g["gp"][0:1, 0:8], loss_l[0:1, 0:1]], axis=1))
    small = jnp.concatenate([g["n1w"], g["n2w"], g["fnw"], gp_row, _pad_lanes(g["gnw"]),
                             rows_of(g["conv_w"]), rows_of(g["fcw_g"]), rows_of(g["fcw_u"])], axis=0)
    small = jnp.pad(small, ((0, SMALL_ROWS - small.shape[0]), (0, 0)))
    *got, small_all = _grad_sibling(fams, small, "grad_sibling")
    parts = [_add_sibling(f, r, cq, "add_sibling_" + nm) for f, r, nm in zip(fams, got, ("w_in", "w_out"))]
    *late_started, late_token = _grad_chips_start([p[1] for p in parts], "grad_chips_late_start")
    got3_e = _grad_chips_wait(early["started"], [dx, g["wp"], late_token], "grad_chips_wait")
    g_w_up, g_w_down = _grad_share(
        [_add_chips(p[0], r3, cq, "add_chips_" + nm) for p, r3, nm in zip(early["parts"], got3_e, ("w_up", "w_down"))],
        "grad_share_early")
    big = {}

    def adamw_big(nm, w, gg, m, v):
        d_, m_, v_ = _adamw(w[0], gg, m[0], v[0], "adamw_" + nm)
        big[nm] = (gg[None], d_[None], m_[None], v_[None])

    adamw_big("w_up", w_up, g_w_up, m_w_up, v_w_up)
    adamw_big("w_down", w_down, g_w_down, m_w_down, v_w_down)
    got3 = _grad_chips_wait(late_started, [big["w_up"][1], big["w_down"][1]], "grad_chips_late_wait")
    g_w_in, g_w_out = _grad_share(
        [_add_chips(p[0], r3, cq, "add_chips_" + nm) for p, r3, nm in zip(parts, got3, ("w_in", "w_out"))],
        "grad_share_late")
    small_red = _sum_devices(small_all)
    loss = small_red[3, 8]
    r0 = 5
    r1 = r0 + GDN_CONV * 3 * GDN_WIDTH // D_MODEL
    r2 = r1 + -(-n_fc // D_MODEL)
    conv_red = small_red[r0:r1].reshape(GDN_CONV, 3 * GDN_WIDTH)
    fc_red = jnp.concatenate([small_red[r1:r2].reshape(-1)[:n_fc].reshape(FFN_CONV, D_FF),
                              small_red[r2:2 * r2 - r1].reshape(-1)[:n_fc].reshape(FFN_CONV, D_FF)], axis=1)
    g_conv_w = lax.dynamic_slice_in_dim(conv_red, q * (3 * GDN_WIDTH // N_CHIPS), 3 * GDN_WIDTH // N_CHIPS, axis=1)
    g_fconv_w = lax.dynamic_slice_in_dim(fc_red, q * (2 * D_FF // N_CHIPS), 2 * D_FF // N_CHIPS, axis=1)
    g_n1w, g_n2w, g_fnw = small_red[0:1], small_red[1:2], small_red[2]
    g_alog, g_dtb, g_gnw = small_red[3:4, 0:4], small_red[3:4, 4:8], small_red[4:5, 0:128]
    for nm, w, gg, m, v in (("w_in", w_in, g_w_in, m_w_in, v_w_in), ("conv_qkv_w", conv_qkv_w, g_conv_w, m_conv_qkv_w, v_conv_qkv_w),
                            ("w_out", w_out, g_w_out, m_w_out, v_w_out),
                            ("ffn_conv_w", ffn_conv_w, g_fconv_w, m_ffn_conv_w, v_ffn_conv_w)):
        adamw_big(nm, w, gg, m, v)

    def pack_small(n1, n2, fn, al, db, gn):
        return jnp.concatenate([n1, n2, fn[None, :], _pad_lanes(jnp.concatenate([al, db], axis=1)), _pad_lanes(gn),
                                jnp.zeros((REPL_ROWS - 5, D_MODEL), F32)], axis=0)

    sw = pack_small(norm1_w, norm2_w, final_norm_w, a_log, dt_bias, gdn_norm_w)
    sm = pack_small(m_norm1_w, m_norm2_w, m_final_norm_w, m_a_log, m_dt_bias, m_gdn_norm_w)
    sv = pack_small(v_norm1_w, v_norm2_w, v_final_norm_w, v_a_log, v_dt_bias, v_gdn_norm_w)
    sd, smn, svn = _adamw(sw, small_red[:REPL_ROWS], sm, sv, "adamw_small")

    def unpack_small(t):
        return dict(norm1_w=t[0:1], norm2_w=t[1:2], final_norm_w=t[2], a_log=t[3:4, 0:4], dt_bias=t[3:4, 4:8],
                    gdn_norm_w=t[4:5, 0:128])

    sg = dict(norm1_w=g_n1w, norm2_w=g_n2w, final_norm_w=g_fnw, a_log=g_alog, dt_bias=g_dtb, gdn_norm_w=g_gnw)
    sd, smn, svn = unpack_small(sd), unpack_small(smn), unpack_small(svn)
    names = ["norm1_w", "w_in", "conv_qkv_w", "a_log", "dt_bias", "gdn_norm_w", "w_out", "norm2_w", "w_up",
             "ffn_conv_w", "w_down", "final_norm_w"]
    grads = [big[n][0] if n in big else sg[n] for n in names]
    deltas = [big[n][1] if n in big else sd[n] for n in names]
    new_m = [big[n][2] if n in big else smn[n] for n in names]
    new_v = [big[n][3] if n in big else svn[n] for n in names]
    return (loss, dx[None], *grads, *deltas, *new_m, *new_v)
```

```python
import functools
import math

import numpy as np
import jax
import jax.numpy as jnp
from jax import lax
from jax.experimental import pallas as pl
from jax.experimental.pallas import tpu as pltpu

F32 = jnp.float32
BF16 = jnp.bfloat16
_MXU = jnp.bfloat16
_HI = lax.Precision.HIGHEST
EPS = 1e-6
V7X_VMEM_LIMIT = 56 * 1024 * 1024
MESH = pl.DeviceIdType.MESH

D_MODEL = 1024
GDN_HEADS, GDN_DIM, GDN_CHUNK, GDN_CONV = 4, 128, 64, 4
GDN_WIDTH = GDN_HEADS * GDN_DIM
DIL_HEADS, DIL_DIM = 8, 64
DIL_WIDTH = DIL_HEADS * DIL_DIM
D_FF, FFN_CONV = 2816, 3
IN_COLS = 3592
P_COLS = 3840
P_Z, P_QKVB, P_BA = 1536, 2048, 3584
ATT_T = 1024
ADAM_LR, ADAM_B1, ADAM_B2, ADAM_EPS, ADAM_WD, ADAM_STEP = 0.001, 0.9, 0.999, 1e-08, 0.01, 10
N_CHIPS = 4


def _cparams(sem=None, vmem=None):
    kw = {}
    if sem is not None:
        kw["dimension_semantics"] = sem
    if vmem is not None:
        kw["vmem_limit_bytes"] = vmem
    return pltpu.CompilerParams(**kw)


def _silu(x):
    return x * jax.nn.sigmoid(x)


def _pick_tile(n, cap):
    best = None
    for t in range(128, min(n, cap) + 1, 128):
        if n % t == 0:
            best = t
    return best or n


def _mm(a, b, mode, *, out_dtype=F32, residual=None, name, b_blocks=False, place=None, into=None, tn=None):
    if mode == "nn":
        M, K = a.shape
        N = b.shape[0] * b.shape[2] if b_blocks else b.shape[1]
    elif mode == "nt":
        (M, K), (N, _) = a.shape, b.shape
    else:
        (K, M), (_, N) = a.shape, b.shape
    tm = _pick_tile(M, 1024)
    tn = b.shape[2] if b_blocks else (tn or _pick_tile(N, 1536))

    def vmem(tm, tn):
        return 2 * (tm * K * a.dtype.itemsize + tn * K * b.dtype.itemsize
                    + tm * tn * (jnp.dtype(out_dtype).itemsize + (4 if residual is not None else 0))) + 3 * tm * tn * 4

    fixed_tn = b_blocks or (place is not None and place[0] == "blocks")
    while vmem(tm, tn) > 40 * 1024 * 1024:
        if (tm >= tn or fixed_tn) and tm % 256 == 0:
            tm //= 2
        elif tn % 256 == 0 and not fixed_tn:
            tn //= 2
        else:
            tm //= 2
    a_spec = pl.BlockSpec((K, tm), lambda j, i: (0, i)) if mode == "tn" else pl.BlockSpec((tm, K), lambda j, i: (i, 0))
    if b_blocks:
        b_spec = pl.BlockSpec((None, K, tn), lambda j, i: (j, 0, 0))
    else:
        b_spec = pl.BlockSpec((tn, K), lambda j, i: (j, 0)) if mode == "nt" else pl.BlockSpec((K, tn), lambda j, i: (0, j))
    r_spec = pl.BlockSpec((tm, tn), lambda j, i: (i, j))
    if place is None:
        o_spec, o_shape = r_spec, (M, N)
    elif place[0] == "rows":
        off = place[2] // tm
        o_spec, o_shape = pl.BlockSpec((tm, tn), lambda j, i: (i + off, j)), (place[1], N)
    else:
        off = place[2]
        o_spec, o_shape = pl.BlockSpec((None, tm, tn), lambda j, i: (j + off, i, 0)), (place[1], M, tn)
    dims = {"nn": (((1,), (0,)), ((), ())), "nt": (((1,), (1,)), ((), ())), "tn": (((0,), (0,)), ((), ()))}[mode]

    def body(*refs):
        a_ref, b_ref = refs[0], refs[1]
        o_ref = refs[-1]
        acc = lax.dot_general(a_ref[...].astype(_MXU), b_ref[...].astype(_MXU), dims, preferred_element_type=F32)
        if residual is not None:
            acc = acc + refs[2][...]
        o_ref[...] = acc.astype(out_dtype)

    ins, specs, alias = [a, b], [a_spec, b_spec], {}
    if residual is not None:
        ins.append(residual)
        specs.append(r_spec)
    if into is not None:
        alias = {len(ins): 0}
        ins.append(into)
        specs.append(pl.BlockSpec(memory_space=pl.ANY))
    return pl.pallas_call(
        body, name=name, grid=(N // tn, M // tm), in_specs=specs, out_specs=o_spec,
        out_shape=jax.ShapeDtypeStruct(o_shape, out_dtype), input_output_aliases=alias,
        compiler_params=_cparams(("parallel", "parallel"), V7X_VMEM_LIMIT),
    )(*ins)


def _mm_nt_blocks(a_list, b4, name):
    M = a_list[0].shape[0]
    nb, N, Kb = b4.shape
    tm, tn = _pick_tile(M, 512), _pick_tile(N, 512)

    def body(a0_ref, a1_ref, b_ref, o_ref):
        acc = None
        for blk in range(nb):
            a_ref = (a0_ref, a1_ref)[blk // 2]
            lo = (blk % 2) * Kb
            t = lax.dot_general(a_ref[:, lo:lo + Kb].astype(_MXU), b_ref[blk].astype(_MXU), (((1,), (1,)), ((), ())),
                                preferred_element_type=F32)
            acc = t if acc is None else acc + t
        o_ref[...] = acc

    a_spec = pl.BlockSpec((tm, 2 * Kb), lambda j, i: (i, 0))
    return pl.pallas_call(
        body, name=name, grid=(N // tn, M // tm),
        in_specs=[a_spec, a_spec, pl.BlockSpec((nb, tn, Kb), lambda j, i: (0, j, 0))],
        out_specs=pl.BlockSpec((tm, tn), lambda j, i: (i, j)), out_shape=jax.ShapeDtypeStruct((M, N), F32),
        compiler_params=_cparams(("parallel", "parallel"), V7X_VMEM_LIMIT),
    )(a_list[0], a_list[1], b4)


def _wp_assemble(g_in):
    nb, Dm, Wb = g_in.shape
    T = 256
    n_lo = P_QKVB - 2 * Wb

    def body(g_ref, o_ref):
        g2 = g_ref[2]
        o_ref[...] = jnp.concatenate(
            [g_ref[0], g_ref[1], g2[:, :n_lo], g2[:, n_lo + 8:], g_ref[3], g2[:, n_lo:n_lo + 8],
             jnp.zeros((T, P_COLS - P_BA - 8), g_in.dtype)], axis=1)

    return pl.pallas_call(
        body, name="wp_assemble", grid=(Dm // T,), in_specs=[pl.BlockSpec((nb, T, Wb), lambda i: (0, i, 0))],
        out_specs=pl.BlockSpec((T, P_COLS), lambda i: (i, 0)), out_shape=jax.ShapeDtypeStruct((Dm, P_COLS), g_in.dtype),
        compiler_params=_cparams(("parallel",)),
    )(g_in)


def _win_split(d_wp):
    Dm = d_wp.shape[0]
    Wb = IN_COLS // N_CHIPS
    T = 256

    def body(x_ref, o_ref):
        xv = x_ref[...]
        o_ref[0] = xv[:, 0:Wb]
        o_ref[1] = xv[:, Wb:2 * Wb]
        o_ref[2] = jnp.concatenate([xv[:, 2 * Wb:P_QKVB], xv[:, P_BA:P_BA + 8], xv[:, P_QKVB:3 * Wb - 8]], axis=1)
        o_ref[3] = xv[:, 3 * Wb - 8:P_BA]

    return pl.pallas_call(
        body, name="win_split", grid=(Dm // T,), in_specs=[pl.BlockSpec((T, P_COLS), lambda i: (i, 0))],
        out_specs=pl.BlockSpec((N_CHIPS, T, Wb), lambda i: (0, i, 0)),
        out_shape=jax.ShapeDtypeStruct((N_CHIPS, Dm, Wb), F32), compiler_params=_cparams(("parallel",)),
    )(d_wp)


def _rmsnorm_fwd(x, w, name):
    S, D = x.shape
    T = _pick_tile(S, 512)

    def body(x_ref, w_ref, o_ref):
        xv = x_ref[...]
        rs = lax.rsqrt(jnp.mean(xv * xv, axis=-1, keepdims=True) + EPS)
        o_ref[...] = (xv * rs * w_ref[...]).astype(o_ref.dtype)

    return pl.pallas_call(
        body, name=name, grid=(S // T,),
        in_specs=[pl.BlockSpec((T, D), lambda i: (i, 0)), pl.BlockSpec((1, D), lambda i: (0, 0))],
        out_specs=pl.BlockSpec((T, D), lambda i: (i, 0)),
        out_shape=jax.ShapeDtypeStruct((S, D), _MXU),
        compiler_params=_cparams(("parallel",)),
    )(x, w)


def _rmsnorm_bwd(dh, x, w, dres, name):
    S, D = x.shape
    T = _pick_tile(S, 512)

    def body(dh_ref, x_ref, w_ref, dres_ref, dx_ref, dw_ref):
        xv = x_ref[...]
        rs = lax.rsqrt(jnp.mean(xv * xv, axis=-1, keepdims=True) + EPS)
        xn = xv * rs
        dhv = dh_ref[...]
        dxn = dhv * w_ref[...]
        dx_ref[...] = dres_ref[...] + rs * (dxn - xn * jnp.mean(dxn * xn, axis=-1, keepdims=True))

        @pl.when(pl.program_id(0) == 0)
        def _():
            dw_ref[...] = jnp.zeros_like(dw_ref)

        dw_ref[...] += jnp.sum(dhv * xn, axis=0, keepdims=True)

    row = pl.BlockSpec((T, D), lambda i: (i, 0))
    vec = pl.BlockSpec((1, D), lambda i: (0, 0))
    return pl.pallas_call(
        body, name=name, grid=(S // T,), in_specs=[row, row, vec, row], out_specs=(row, vec),
        out_shape=(jax.ShapeDtypeStruct((S, D), F32), jax.ShapeDtypeStruct((1, D), F32)),
        compiler_params=_cparams(("arbitrary",)),
    )(dh, x, w, dres)


def _loss_head(x3, w, tgt, name):
    S, D = x3.shape
    T = _pick_tile(S, 512)

    def body(x_ref, w_ref, t_ref, loss_ref, dx_ref, dw_ref):
        xv = x_ref[...]
        rs = lax.rsqrt(jnp.mean(xv * xv, axis=-1, keepdims=True) + EPS)
        xn = xv * rs
        err = xn * w_ref[...] - t_ref[...]
        dy = err * (1.0 / D)
        dxn = dy * w_ref[...]
        dx_ref[...] = rs * (dxn - xn * jnp.mean(dxn * xn, axis=-1, keepdims=True))

        @pl.when(pl.program_id(0) == 0)
        def _():
            dw_ref[...] = jnp.zeros_like(dw_ref)
            loss_ref[...] = jnp.zeros_like(loss_ref)

        dw_ref[...] += jnp.sum(dy * xn, axis=0, keepdims=True)
        part = jnp.sum(jnp.sum(err * err, axis=-1, keepdims=True), axis=0, keepdims=True) * (0.5 / D)
        loss_ref[...] += jnp.broadcast_to(part, loss_ref.shape)

    row = pl.BlockSpec((T, D), lambda i: (i, 0))
    vec = pl.BlockSpec((1, D), lambda i: (0, 0))
    return pl.pallas_call(
        body, name=name, grid=(S // T,), in_specs=[row, vec, row],
        out_specs=(pl.BlockSpec((8, 128), lambda i: (0, 0)), row, vec),
        out_shape=(jax.ShapeDtypeStruct((8, 128), F32), jax.ShapeDtypeStruct((S, D), F32), jax.ShapeDtypeStruct((1, D), F32)),
        compiler_params=_cparams(("arbitrary",)),
    )(x3, w, tgt)


def _conv_taps(ext, w, K, T):
    out = None
    for i in range(K):
        lo = 8 - (K - 1) + i
        term = ext[lo:lo + T, :] * w[i:i + 1, :]
        out = term if out is None else out + term
    return out


def _conv_taps_t(ext, w, K, T):
    out = None
    for i in range(K):
        lo = (K - 1) - i
        term = ext[lo:lo + T, :] * w[i:i + 1, :]
        out = term if out is None else out + term
    return out


def _tri_masks(C):
    r = lax.broadcasted_iota(jnp.int32, (C, C), 0)
    c = lax.broadcasted_iota(jnp.int32, (C, C), 1)
    return r == c, r >= c, r > c, r <= c


_NN, _NT, _TN = ((1,), (0,)), ((1,), (1,)), ((0,), (0,))
_GDN_PASSES = dict(qk=1, inv=1, sol=1, scan=1, bwd=1)


def _bdot_raw(a, b, kind, passes):
    dims = ({"NN": ((2,), (1,)), "NT": ((2,), (2,)), "TN": ((1,), (1,))}[kind], ((0,), (0,)))
    if passes == 0:
        return lax.dot_general(a, b, dims, precision=_HI, preferred_element_type=F32)
    ah, bh = a.astype(BF16), b.astype(BF16)
    out = lax.dot_general(ah, bh, dims, preferred_element_type=F32)
    if passes == 3:
        al, bl = (a - ah.astype(F32)).astype(BF16), (b - bh.astype(F32)).astype(BF16)
        out = out + lax.dot_general(ah, bl, dims, preferred_element_type=F32) + lax.dot_general(al, bh, dims, preferred_element_type=F32)
    return out


@functools.partial(jax.custom_vjp, nondiff_argnums=(2, 3))
def _bdot(a, b, kind, passes):
    return _bdot_raw(a, b, kind, passes)


def _bdot_fwd(a, b, kind, passes):
    return _bdot_raw(a, b, kind, passes), (a, b)


def _bdot_bwd(kind, passes, res, ct):
    a, b = res
    if kind == "NN":
        return _bdot_raw(ct, b, "NT", passes), _bdot_raw(a, ct, "TN", passes)
    if kind == "NT":
        return _bdot_raw(ct, b, "NN", passes), _bdot_raw(ct, a, "TN", passes)
    return _bdot_raw(b, ct, "NT", passes), _bdot_raw(a, ct, "NN", passes)


_bdot.defvjp(_bdot_fwd, _bdot_bwd)


def _softplus(x):
    return jnp.maximum(x, 0.0) + jnp.log(1.0 + jnp.exp(-jnp.abs(x)))


def _gdn_stage1(cq, ck, cv, b_col, a_col, alog, dtb, dot=_bdot_raw):
    C = cq.shape[1]
    eye, incl, strict, incl_t = _tri_masks(C)
    qn = cq * lax.rsqrt(jnp.sum(cq * cq, axis=-1, keepdims=True) + EPS) * (GDN_DIM ** -0.5)
    kn = ck * lax.rsqrt(jnp.sum(ck * ck, axis=-1, keepdims=True) + EPS)
    beta = jax.nn.sigmoid(b_col)
    g = -jnp.exp(alog) * _softplus(a_col + dtb)
    g_row = jnp.sum(jnp.where(eye, g, 0.0), axis=1, keepdims=True)
    beta_row = jnp.sum(jnp.where(eye, beta, 0.0), axis=1, keepdims=True)
    gc_col = jnp.sum(jnp.where(incl, g_row, 0.0), axis=2, keepdims=True)
    gc_row = jnp.sum(jnp.where(incl_t, g, 0.0), axis=1, keepdims=True)
    dec = jnp.where(incl, jnp.exp(jnp.where(incl, gc_col - gc_row, 0.0)), 0.0)
    kk = dot(kn, kn, "NT", _GDN_PASSES["qk"])
    qk = dot(qn, kn, "NT", _GDN_PASSES["qk"])
    lmat = jnp.where(strict, dec * kk * beta_row, 0.0)
    attn = dec * qk * beta_row
    gam = jnp.exp(gc_col)
    gc_last = gc_col[:, C - 1:C, :]
    k_end = kn * (jnp.exp(gc_last - gc_col) * beta)
    return lmat, cv, gam * kn, gam * qn, attn, k_end, jnp.exp(gc_last)


def _tri_inv(lmat):
    C = lmat.shape[1]
    eye = _tri_masks(C)[0]
    ps = _GDN_PASSES["inv"]
    p = jnp.where(eye, 1.0, 0.0) - lmat
    lp = _bdot_raw(lmat, lmat, "NN", ps)
    n = int(math.log2(C))
    for s in range(1, n):
        p = p + _bdot_raw(p, lp, "NN", ps)
        if s < n - 1:
            lp = _bdot_raw(lp, lp, "NN", ps)
    return p


def _gated_norm(o, z, gnw):
    on = o * lax.rsqrt(jnp.mean(o * o, axis=-1, keepdims=True) + EPS) * gnw
    return on * _silu(z)


GDN_PG = 2
GDN_SG = 4


def _gdn_pairs(c, ba, gp, G):
    C, W, H = GDN_CHUNK, GDN_WIDTH, GDN_HEADS
    pairs = [(j, h) for j in range(G) for h in range(H)]
    cq, ck, cv = (jnp.stack([c[C * j:C * (j + 1), o + GDN_DIM * h:o + GDN_DIM * (h + 1)] for j, h in pairs]) for o in (0, W, 2 * W))
    b_col = jnp.stack([ba[C * j:C * (j + 1), h:h + 1] for j, h in pairs])
    a_col = jnp.stack([ba[C * j:C * (j + 1), H + h:H + h + 1] for j, h in pairs])
    alog = jnp.stack([gp[0:1, h:h + 1] for j, h in pairs])
    dtb = jnp.stack([gp[0:1, H + h:H + h + 1] for j, h in pairs])
    return pairs, (cq, ck, cv, b_col, a_col, alog, dtb)


def _gdn_pre_specs(S, G):
    C = GDN_CHUNK
    T = C * G
    return dict(
        cur=pl.BlockSpec((T, 3 * GDN_WIDTH), lambda i: (i, 0)),
        prev=pl.BlockSpec((8, 3 * GDN_WIDTH), lambda i: (jnp.maximum(i * (T // 8) - 1, 0), 0)),
        ba=pl.BlockSpec((T, 128), lambda i: (i, P_BA // 128)),
        cw=pl.BlockSpec((GDN_CONV, 3 * GDN_WIDTH), lambda i: (0, 0)),
        vec=pl.BlockSpec((1, 128), lambda i: (0, 0)),
        hd=pl.BlockSpec((GDN_HEADS, T, GDN_DIM), lambda i: (0, i, 0)),
        hc=pl.BlockSpec((GDN_HEADS, T, C), lambda i: (0, i, 0)),
        ge=pl.BlockSpec((G, GDN_HEADS, 8, 128), lambda i: (i, 0, 0, 0)),
    )


def _hd_shape(S, last=GDN_DIM):
    return jax.ShapeDtypeStruct((GDN_HEADS, S, last), F32)


def _gdn_pre(proj, conv_w, gp):
    S = proj.shape[0]
    C, G = GDN_CHUNK, GDN_PG
    nc = S // C
    sp = _gdn_pre_specs(S, G)

    def body(cur_ref, prev_ref, ba_ref, cw_ref, gp_ref, uv_ref, wk_ref, qd_ref, ke_ref, at_ref, ti_ref, ge_ref):
        prev = prev_ref[...] * jnp.where(pl.program_id(0) == 0, 0.0, 1.0)
        c = _silu(_conv_taps(jnp.concatenate([prev, cur_ref[...]], axis=0), cw_ref[...], GDN_CONV, C * G))
        pairs, args = _gdn_pairs(c, ba_ref[...], gp_ref[...], G)
        lmat, v, rk, q_dec, attn, k_end, g_end = _gdn_stage1(*args)
        t = _tri_inv(lmat)
        u_v = _bdot_raw(t, v, "NN", _GDN_PASSES["sol"])
        w_k = _bdot_raw(t, rk, "NN", _GDN_PASSES["sol"])
        for b, (j, h) in enumerate(pairs):
            rows = slice(C * j, C * (j + 1))
            uv_ref[h, rows, :] = u_v[b]
            wk_ref[h, rows, :] = w_k[b]
            qd_ref[h, rows, :] = q_dec[b]
            ke_ref[h, rows, :] = k_end[b]
            at_ref[h, rows, :] = attn[b]
            ti_ref[h, rows, :] = t[b]
            ge_ref[j, h] = jnp.broadcast_to(g_end[b], (8, 128))

    return pl.pallas_call(
        body, name="gdn_pre", grid=(nc // G,),
        in_specs=[sp["cur"], sp["prev"], sp["ba"], sp["cw"], sp["vec"]],
        out_specs=(sp["hd"], sp["hd"], sp["hd"], sp["hd"], sp["hc"], sp["hc"], sp["ge"]),
        out_shape=(_hd_shape(S), _hd_shape(S), _hd_shape(S), _hd_shape(S), _hd_shape(S, C), _hd_shape(S, C),
                   jax.ShapeDtypeStruct((nc, GDN_HEADS, 8, 128), F32)),
        compiler_params=_cparams(("parallel",)),
    )(proj, proj, proj, conv_w, gp)


def _gdn_scan_specs(S, G, rev):
    C = GDN_CHUNK
    T = C * G
    n = S // T
    ci = (lambda i: n - 1 - i) if rev else (lambda i: i)
    return dict(
        hd=pl.BlockSpec((GDN_HEADS, T, GDN_DIM), lambda i: (0, ci(i), 0)),
        hc=pl.BlockSpec((GDN_HEADS, T, C), lambda i: (0, ci(i), 0)),
        ge=pl.BlockSpec((G, GDN_HEADS, 8, 128), lambda i: (ci(i), 0, 0, 0)),
        z=pl.BlockSpec((T, GDN_WIDTH), lambda i: (ci(i), P_Z // GDN_WIDTH)),
        oa=pl.BlockSpec((T, GDN_WIDTH), lambda i: (ci(i), 0)),
        vec=pl.BlockSpec((1, 128), lambda i: (0, 0)),
        st=pl.BlockSpec((G, GDN_HEADS, GDN_DIM, GDN_DIM), lambda i: (ci(i), 0, 0, 0)),
    )


def _gdn_scan(u_v, w_k, q_dec, k_end, attn, g_end, proj, gnw):
    S = proj.shape[0]
    C, G = GDN_CHUNK, GDN_SG
    nc = S // C
    sp = _gdn_scan_specs(S, G, False)
    ps = _GDN_PASSES["scan"]

    def body(uv_ref, wk_ref, qd_ref, ke_ref, at_ref, ge_ref, z_ref, gnw_ref, oa_ref, st_ref, s_scr):
        @pl.when(pl.program_id(0) == 0)
        def _():
            s_scr[...] = jnp.zeros_like(s_scr)

        for j in range(G):
            rows = slice(C * j, C * (j + 1))
            st = s_scr[...]
            st_ref[j] = st
            u = uv_ref[:, rows, :] - _bdot_raw(wk_ref[:, rows, :], st, "NN", ps)
            o = _bdot_raw(qd_ref[:, rows, :], st, "NN", ps) + _bdot_raw(at_ref[:, rows, :], u, "NN", ps)
            s_scr[...] = ge_ref[j][:, 0:1, 0:1] * st + _bdot_raw(ke_ref[:, rows, :], u, "TN", ps)
            for h in range(GDN_HEADS):
                cols = slice(GDN_DIM * h, GDN_DIM * (h + 1))
                oa_ref[rows, cols] = _gated_norm(o[h], z_ref[rows, cols], gnw_ref[...])

    return pl.pallas_call(
        body, name="gdn_scan", grid=(nc // G,),
        in_specs=[sp["hd"], sp["hd"], sp["hd"], sp["hd"], sp["hc"], sp["ge"], sp["z"], sp["vec"]],
        out_specs=(sp["oa"], sp["st"]),
        out_shape=(jax.ShapeDtypeStruct((S, GDN_WIDTH + DIL_WIDTH), F32),
                   jax.ShapeDtypeStruct((nc, GDN_HEADS, GDN_DIM, GDN_DIM), F32)),
        scratch_shapes=[pltpu.VMEM((GDN_HEADS, GDN_DIM, GDN_DIM), F32)],
        compiler_params=_cparams(("arbitrary",)),
    )(u_v, w_k, q_dec, k_end, attn, g_end, proj, gnw)


def _gdn_scan_bwd(u_v, w_k, q_dec, k_end, attn, g_end, proj, gnw, states, d_oa):
    S = proj.shape[0]
    C, G = GDN_CHUNK, GDN_SG
    nc = S // C
    sp = _gdn_scan_specs(S, G, True)
    ps, pb = _GDN_PASSES["scan"], _GDN_PASSES["bwd"]

    def body(uv_ref, wk_ref, qd_ref, ke_ref, at_ref, ge_ref, z_ref, gnw_ref, st_ref, doa_ref,
             duv_ref, dwk_ref, dqd_ref, dke_ref, dat_ref, dge_ref, dz_ref, dgnw_ref, ds_scr):
        @pl.when(pl.program_id(0) == 0)
        def _():
            ds_scr[...] = jnp.zeros_like(ds_scr)
            dgnw_ref[...] = jnp.zeros_like(dgnw_ref)

        dgnw = jnp.zeros((1, 128), F32)
        for j in reversed(range(G)):
            rows = slice(C * j, C * (j + 1))
            st = st_ref[j]
            wk, qd, ke, at = wk_ref[:, rows, :], qd_ref[:, rows, :], ke_ref[:, rows, :], at_ref[:, rows, :]
            u = uv_ref[:, rows, :] - _bdot_raw(wk, st, "NN", ps)
            o = _bdot_raw(qd, st, "NN", ps) + _bdot_raw(at, u, "NN", ps)
            dos = []
            for h in range(GDN_HEADS):
                cols = slice(GDN_DIM * h, GDN_DIM * (h + 1))
                _, vjp2 = jax.vjp(_gated_norm, o[h], z_ref[rows, cols], gnw_ref[...])
                do_h, dz_h, dgn = vjp2(doa_ref[rows, cols])
                dz_ref[rows, cols] = dz_h
                dgnw = dgnw + dgn
                dos.append(do_h)
            do = jnp.stack(dos)
            ds_new = ds_scr[...]
            du = _bdot_raw(at, do, "TN", pb) + _bdot_raw(ke, ds_new, "NN", pb)
            duv_ref[:, rows, :] = du
            dat_ref[:, rows, :] = _bdot_raw(do, u, "NT", pb)
            dqd_ref[:, rows, :] = _bdot_raw(do, st, "NT", pb)
            dke_ref[:, rows, :] = _bdot_raw(u, ds_new, "NT", pb)
            dwk_ref[:, rows, :] = -_bdot_raw(du, st, "NT", pb)
            d_ge = jnp.sum(jnp.sum(st * ds_new, axis=2, keepdims=True), axis=1, keepdims=True)
            dge_ref[j] = jnp.broadcast_to(d_ge, (GDN_HEADS, 8, 128))
            ds_scr[...] = ge_ref[j][:, 0:1, 0:1] * ds_new + _bdot_raw(qd, do, "TN", pb) - _bdot_raw(wk, du, "TN", pb)
        dgnw_ref[...] += dgnw

    return pl.pallas_call(
        body, name="gdn_scan_bwd", grid=(nc // G,),
        in_specs=[sp["hd"], sp["hd"], sp["hd"], sp["hd"], sp["hc"], sp["ge"], sp["z"], sp["vec"], sp["st"], sp["oa"]],
        out_specs=(sp["hd"], sp["hd"], sp["hd"], sp["hd"], sp["hc"], sp["ge"], sp["oa"], sp["vec"]),
        out_shape=(_hd_shape(S), _hd_shape(S), _hd_shape(S), _hd_shape(S), _hd_shape(S, C),
                   jax.ShapeDtypeStruct((nc, GDN_HEADS, 8, 128), F32), jax.ShapeDtypeStruct((S, GDN_WIDTH), F32),
                   jax.ShapeDtypeStruct((1, 128), F32)),
        scratch_shapes=[pltpu.VMEM((GDN_HEADS, GDN_DIM, GDN_DIM), F32)],
        compiler_params=_cparams(("arbitrary",)),
    )(u_v, w_k, q_dec, k_end, attn, g_end, proj, gnw, states, d_oa)


def _gdn_post(proj, conv_w, gp, tinv, u_v, w_k, d_uv, d_wk, d_qd, d_ke, d_at, d_ge):
    S = proj.shape[0]
    C, G = GDN_CHUNK, GDN_PG
    nc = S // C
    sp = _gdn_pre_specs(S, G)
    pb = _GDN_PASSES["bwd"]

    def body(cur_ref, prev_ref, ba_ref, cw_ref, gp_ref, ti_ref, uv_ref, wk_ref, duv_ref, dwk_ref, dqd_ref, dke_ref,
             dat_ref, dge_ref, dpre_ref, dba_ref, dgp_ref):
        i = pl.program_id(0)

        @pl.when(i == 0)
        def _():
            dgp_ref[...] = jnp.zeros_like(dgp_ref)

        prev = prev_ref[...] * jnp.where(i == 0, 0.0, 1.0)
        pre = _conv_taps(jnp.concatenate([prev, cur_ref[...]], axis=0), cw_ref[...], GDN_CONV, C * G)
        sg = jax.nn.sigmoid(pre)
        dsilu = sg * (1.0 + pre * (1.0 - sg))
        pairs, args = _gdn_pairs(pre * sg, ba_ref[...], gp_ref[...], G)
        _, vjp1 = jax.vjp(functools.partial(_gdn_stage1, dot=_bdot), *args)

        def take(ref):
            return jnp.stack([ref[h, C * j:C * (j + 1), :] for j, h in pairs])

        t, u_v, w_k = take(ti_ref), take(uv_ref), take(wk_ref)
        d_v = _bdot_raw(t, take(duv_ref), "TN", pb)
        d_rk = _bdot_raw(t, take(dwk_ref), "TN", pb)
        d_l = -(_bdot_raw(d_v, u_v, "NT", pb) + _bdot_raw(d_rk, w_k, "NT", pb))
        d_ge = jnp.stack([dge_ref[j, h][0:1, 0:1] for j, h in pairs])
        dcq, dck, dcv, db, da, dalog, ddtb = vjp1((d_l, d_v, d_rk, take(dqd_ref), take(dat_ref), take(dke_ref), d_ge))
        lane = lax.broadcasted_iota(jnp.int32, (C, 128), 1)
        lane1 = lax.broadcasted_iota(jnp.int32, (1, 128), 1)
        dgp = jnp.zeros((1, 128), F32)
        for j in range(G):
            rows = slice(C * j, C * (j + 1))
            dba = jnp.zeros((C, 128), F32)
            for h in range(GDN_HEADS):
                b = GDN_HEADS * j + h
                for o_, dcx in ((0, dcq), (GDN_WIDTH, dck), (2 * GDN_WIDTH, dcv)):
                    cols = slice(o_ + GDN_DIM * h, o_ + GDN_DIM * (h + 1))
                    dpre_ref[rows, cols] = dcx[b] * dsilu[rows, cols]
                dba = dba + jnp.where(lane == h, db[b], 0.0) + jnp.where(lane == GDN_HEADS + h, da[b], 0.0)
                dgp = dgp + jnp.where(lane1 == h, dalog[b], 0.0) + jnp.where(lane1 == GDN_HEADS + h, ddtb[b], 0.0)
            dba_ref[rows, :] = dba
        dgp_ref[0:1, :] += dgp

    T = C * G
    return pl.pallas_call(
        body, name="gdn_post", grid=(nc // G,),
        in_specs=[sp["cur"], sp["prev"], sp["ba"], sp["cw"], sp["vec"], sp["hc"], sp["hd"], sp["hd"], sp["hd"], sp["hd"],
                  sp["hd"], sp["hd"], sp["hc"], sp["ge"]],
        out_specs=(sp["cur"], pl.BlockSpec((T, 128), lambda i: (i, 0)), pl.BlockSpec((8, 128), lambda i: (0, 0))),
        out_shape=(jax.ShapeDtypeStruct((S, 3 * GDN_WIDTH), F32), jax.ShapeDtypeStruct((S, 128), F32),
                   jax.ShapeDtypeStruct((8, 128), F32)),
        compiler_params=_cparams(("arbitrary",)),
    )(proj, proj, proj, conv_w, gp, tinv, u_v, w_k, d_uv, d_wk, d_qd, d_ke, d_at, d_ge)


def _conv_bwd(dpre, x, xcol0, w, K, name, tc):
    S, Cc = dpre.shape
    T = _pick_tile(S, 256)
    nt, ncol = S // T, Cc // tc
    xo = xcol0 // tc

    def body(d_ref, dn_ref, x_ref, xp_ref, w_ref, dx_ref, dw_ref):
        i = pl.program_id(1)
        dn = dn_ref[...] * jnp.where(i == nt - 1, 0.0, 1.0)
        dv = d_ref[...]
        ext_d = jnp.concatenate([dv, dn], axis=0)
        wv = w_ref[...]
        dx_ref[...] = _conv_taps_t(ext_d, wv, K, T).astype(dx_ref.dtype)
        xp = xp_ref[...] * jnp.where(i == 0, 0.0, 1.0)
        ext_x = jnp.concatenate([xp, x_ref[...]], axis=0)

        @pl.when(i == 0)
        def _():
            dw_ref[...] = jnp.zeros_like(dw_ref)

        for k in range(K):
            lo = 8 - (K - 1) + k
            dw_ref[k:k + 1, :] += jnp.sum(dv * ext_x[lo:lo + T, :], axis=0, keepdims=True)

    r8 = T // 8
    return pl.pallas_call(
        body, name=name, grid=(ncol, nt),
        in_specs=[pl.BlockSpec((T, tc), lambda j, i: (i, j)),
                  pl.BlockSpec((8, tc), lambda j, i: (jnp.minimum((i + 1) * r8, S // 8 - 1), j)),
                  pl.BlockSpec((T, tc), lambda j, i: (i, j + xo)),
                  pl.BlockSpec((8, tc), lambda j, i: (jnp.maximum(i * r8 - 1, 0), j + xo)),
                  pl.BlockSpec((K, tc), lambda j, i: (0, j))],
        out_specs=(pl.BlockSpec((T, tc), lambda j, i: (i, j)), pl.BlockSpec((K, tc), lambda j, i: (0, j))),
        out_shape=(jax.ShapeDtypeStruct((S, Cc), _MXU), jax.ShapeDtypeStruct((K, Cc), F32)),
        compiler_params=_cparams(("parallel", "arbitrary")),
    )(dpre, dpre, x, x, w)


def _dil_bias(nt, T):
    d = (np.arange(nt)[:, None, None] * T + np.arange(T)[None, None, :] - np.arange(T)[None, :, None])
    cnt = ((d >= 0) & (d <= 128)).astype(np.float64) + ((d >= 0) & (d % 4 == 0) & (d <= 512)) + ((d >= 0) & (d % 16 == 0))
    return jnp.asarray(np.where(cnt > 0, np.log(np.maximum(cnt, 1.0)), -1e30), dtype=F32)


def _attn_fwd(proj, mix):
    S = proj.shape[0]
    T = min(ATT_T, S)
    nt = S // T
    bias = _dil_bias(nt, T)
    scale = DIL_DIM ** -0.5
    npair = DIL_WIDTH // 128
    qb0, kb0, vb0 = P_QKVB // 128, (P_QKVB + DIL_WIDTH) // 128, (P_QKVB + 2 * DIL_WIDTH) // 128

    def body(q_ref, k_ref, v_ref, b_ref, mix_ref, o_ref, lse_ref):
        i = pl.program_id(1)
        qs = (q_ref[...] * scale).astype(_MXU)

        def step(j, carry):
            kt = k_ref[pl.ds(pl.multiple_of(j * T, T), T), :].astype(_MXU)
            vt = v_ref[pl.ds(pl.multiple_of(j * T, T), T), :].astype(_MXU)
            bt = b_ref[i - j]
            out = []
            for hh in range(2):
                m, l, acc = carry[hh]
                sl = slice(hh * DIL_DIM, (hh + 1) * DIL_DIM)
                s = lax.dot_general(kt[:, sl], qs[:, sl], (_NT, ((), ())), preferred_element_type=F32) + bt
                m_new = jnp.maximum(m, jnp.max(s, axis=0, keepdims=True))
                p = jnp.exp(s - m_new)
                a = jnp.exp(m - m_new)
                l = a * l + jnp.sum(p, axis=0, keepdims=True)
                acc = a * acc + lax.dot_general(vt[:, sl], p.astype(_MXU), (_TN, ((), ())), preferred_element_type=F32)
                out.append((m_new, l, acc))
            return tuple(out)

        init = tuple((jnp.full((1, T), -1e30, F32), jnp.zeros((1, T), F32), jnp.zeros((DIL_DIM, T), F32)) for _ in range(2))
        res = lax.fori_loop(0, i + 1, step, init)
        lse_ref[...] = jnp.zeros_like(lse_ref)
        for hh in range(2):
            m, l, acc = res[hh]
            o_ref[:, hh * DIL_DIM:(hh + 1) * DIL_DIM] = (acc / l).T
            lse_ref[hh:hh + 1, :] = m + jnp.log(l)

    return pl.pallas_call(
        body, name="attn_fwd", grid=(npair, nt),
        in_specs=[pl.BlockSpec((T, 128), lambda p, i: (i, qb0 + p)),
                  pl.BlockSpec((S, 128), lambda p, i: (0, kb0 + p)),
                  pl.BlockSpec((S, 128), lambda p, i: (0, vb0 + p)),
                  pl.BlockSpec((nt, T, T), lambda p, i: (0, 0, 0)), pl.BlockSpec(memory_space=pl.ANY)],
        out_specs=(pl.BlockSpec((T, 128), lambda p, i: (i, GDN_WIDTH // 128 + p)),
                   pl.BlockSpec((None, None, 8, T), lambda p, i: (p, i, 0, 0))),
        out_shape=(jax.ShapeDtypeStruct(mix.shape, F32), jax.ShapeDtypeStruct((npair, nt, 8, T), F32)),
        input_output_aliases={4: 0},
        compiler_params=_cparams(("parallel", "parallel")),
    )(proj, proj, proj, bias, mix)


def _attn_bwd(proj, mix, lse, d_mix):
    S = proj.shape[0]
    T = min(ATT_T, S)
    nt = S // T
    bias = _dil_bias(nt, T)
    scale = DIL_DIM ** -0.5
    npair = DIL_WIDTH // 128
    qb0, kb0, vb0 = P_QKVB // 128, (P_QKVB + DIL_WIDTH) // 128, (P_QKVB + 2 * DIL_WIDTH) // 128

    def body(q_ref, k_ref, v_ref, o_ref, lse_ref, do_ref, b_ref, dq_ref, dk_ref, dv_ref, dq_scr):
        j = pl.program_id(1)

        @pl.when(j == 0)
        def _():
            dq_scr[...] = jnp.zeros_like(dq_scr)

        kt = k_ref[...].astype(_MXU)
        vt = v_ref[...].astype(_MXU)
        ones = jnp.ones((8, DIL_DIM), F32)

        def step(i, carry):
            rows = pl.ds(pl.multiple_of(i * T, T), T)
            qs = (q_ref[rows, :] * scale).astype(_MXU)
            dov = do_ref[rows, :]
            prod = dov * o_ref[rows, :]
            lsev = lse_ref[i]
            dob = dov.astype(_MXU)
            bt = b_ref[i - j]
            out = []
            dqs = []
            for hh in range(2):
                dk, dv = carry[hh]
                sl = slice(hh * DIL_DIM, (hh + 1) * DIL_DIM)
                s = lax.dot_general(kt[:, sl], qs[:, sl], (_NT, ((), ())), preferred_element_type=F32) + bt
                p = jnp.exp(s - lsev[hh:hh + 1, :])
                delta = lax.dot_general(ones, prod[:, sl], (_NT, ((), ())), precision=_HI, preferred_element_type=F32)[0:1, :]
                dp = lax.dot_general(vt[:, sl], dob[:, sl], (_NT, ((), ())), preferred_element_type=F32)
                ds = (p * (dp - delta)).astype(_MXU)
                dv = dv + lax.dot_general(p.astype(_MXU), dob[:, sl], (_NN, ((), ())), preferred_element_type=F32)
                dk = dk + lax.dot_general(ds, qs[:, sl], (_NN, ((), ())), preferred_element_type=F32)
                dqs.append(lax.dot_general(ds, kt[:, sl], (_TN, ((), ())), preferred_element_type=F32) * scale)
                out.append((dk, dv))
            dq_scr[rows, :] += jnp.concatenate(dqs, axis=1)
            return tuple(out)

        init = tuple((jnp.zeros((T, DIL_DIM), F32), jnp.zeros((T, DIL_DIM), F32)) for _ in range(2))
        res = lax.fori_loop(j, nt, step, init)
        dk_ref[...] = jnp.concatenate([res[0][0], res[1][0]], axis=1).astype(dk_ref.dtype)
        dv_ref[...] = jnp.concatenate([res[0][1], res[1][1]], axis=1).astype(dv_ref.dtype)

        @pl.when(j == nt - 1)
        def _():
            dq_ref[...] = dq_scr[...].astype(dq_ref.dtype)

    full = lambda c0: pl.BlockSpec((S, 128), lambda p, j: (0, c0 + p))
    tile = lambda c0: pl.BlockSpec((T, 128), lambda p, j: (j, c0 + p))
    out3 = jax.ShapeDtypeStruct((S, DIL_WIDTH), _MXU)
    return pl.pallas_call(
        body, name="attn_bwd", grid=(npair, nt),
        in_specs=[full(qb0), tile(kb0), tile(vb0), full(GDN_WIDTH // 128),
                  pl.BlockSpec((None, nt, 8, T), lambda p, j: (p, 0, 0, 0)), full(GDN_WIDTH // 128),
                  pl.BlockSpec((nt, T, T), lambda p, j: (0, 0, 0))],
        out_specs=(full(0), tile(0), tile(0)),
        out_shape=(out3, out3, out3),
        scratch_shapes=[pltpu.VMEM((S, 128), F32)],
        compiler_params=_cparams(("parallel", "arbitrary")),
    )(proj, proj, proj, mix, lse, d_mix, bias)


def _ffn_act(up, cw):
    S, Cc = up.shape[0], up.shape[1] // 2
    T, tc = _pick_tile(S, 256), _pick_tile(Cc, 1536)
    r8 = T // 8
    nct = Cc // tc

    def body(g_ref, gp_ref, u_ref, up_ref, wg_ref, wu_ref, o_ref):
        keep = jnp.where(pl.program_id(1) == 0, 0.0, 1.0)
        cg = _conv_taps(jnp.concatenate([gp_ref[...] * keep, g_ref[...]], axis=0), wg_ref[...], FFN_CONV, T)
        cu = _conv_taps(jnp.concatenate([up_ref[...] * keep, u_ref[...]], axis=0), wu_ref[...], FFN_CONV, T)
        o_ref[...] = (_silu(cg) * cu).astype(o_ref.dtype)

    cur = lambda o: pl.BlockSpec((T, tc), lambda j, i: (i, j + o))
    prev = lambda o: pl.BlockSpec((8, tc), lambda j, i: (jnp.maximum(i * r8 - 1, 0), j + o))
    wsp = lambda o: pl.BlockSpec((FFN_CONV, tc), lambda j, i: (0, j + o))
    return pl.pallas_call(
        body, name="ffn_act", grid=(nct, S // T),
        in_specs=[cur(0), prev(0), cur(nct), prev(nct), wsp(0), wsp(nct)], out_specs=cur(0),
        out_shape=jax.ShapeDtypeStruct((S, Cc), _MXU),
        compiler_params=_cparams(("parallel", "parallel")),
    )(up, up, up, up, cw, cw)


def _ffn_act_bwd(d_act, up, cw):
    S, Cc = up.shape[0], up.shape[1] // 2
    T, tc = _pick_tile(S, 256), _pick_tile(Cc, 1536)
    r8 = T // 8
    nt = S // T
    nct = Cc // tc
    K = FFN_CONV

    def body(da_ref, dan_ref, g_ref, gp_ref, gn_ref, u_ref, up_ref, un_ref, wg_ref, wu_ref,
             dg_ref, du_ref, dwg_ref, dwu_ref):
        i = pl.program_id(1)
        keep_p = jnp.where(i == 0, 0.0, 1.0)
        keep_n = jnp.where(i == nt - 1, 0.0, 1.0)
        wg, wu = wg_ref[...], wu_ref[...]
        xg = jnp.concatenate([gp_ref[...] * keep_p, g_ref[...], gn_ref[...] * keep_n], axis=0)
        xu = jnp.concatenate([up_ref[...] * keep_p, u_ref[...], un_ref[...] * keep_n], axis=0)
        cg = _conv_taps(xg, wg, K, T + 8)
        cu = _conv_taps(xu, wu, K, T + 8)
        da = jnp.concatenate([da_ref[...], dan_ref[...] * keep_n], axis=0)
        sg = jax.nn.sigmoid(cg)
        d_cg = da * cu * (sg * (1.0 + cg * (1.0 - sg)))
        d_cu = da * (cg * sg)
        dg_ref[...] = _conv_taps_t(d_cg, wg, K, T).astype(dg_ref.dtype)
        du_ref[...] = _conv_taps_t(d_cu, wu, K, T).astype(du_ref.dtype)

        @pl.when(i == 0)
        def _():
            dwg_ref[...] = jnp.zeros_like(dwg_ref)
            dwu_ref[...] = jnp.zeros_like(dwu_ref)

        for k in range(K):
            lo = 8 - (K - 1) + k
            dwg_ref[k:k + 1, :] += jnp.sum(d_cg[0:T, :] * xg[lo:lo + T, :], axis=0, keepdims=True)
            dwu_ref[k:k + 1, :] += jnp.sum(d_cu[0:T, :] * xu[lo:lo + T, :], axis=0, keepdims=True)

    cur = lambda o: pl.BlockSpec((T, tc), lambda j, i: (i, j + o))
    prev = lambda o: pl.BlockSpec((8, tc), lambda j, i: (jnp.maximum(i * r8 - 1, 0), j + o))
    nxt = lambda o: pl.BlockSpec((8, tc), lambda j, i: (jnp.minimum((i + 1) * r8, S // 8 - 1), j + o))
    wsp = lambda o: pl.BlockSpec((K, tc), lambda j, i: (0, j + o))
    return pl.pallas_call(
        body, name="ffn_act_bwd", grid=(nct, nt),
        in_specs=[cur(0), nxt(0), cur(0), prev(0), nxt(0), cur(nct), prev(nct), nxt(nct), wsp(0), wsp(nct)],
        out_specs=(cur(0), cur(0), wsp(0), wsp(0)),
        out_shape=(jax.ShapeDtypeStruct((S, Cc), _MXU), jax.ShapeDtypeStruct((S, Cc), _MXU),
                   jax.ShapeDtypeStruct((K, Cc), F32), jax.ShapeDtypeStruct((K, Cc), F32)),
        compiler_params=_cparams(("parallel", "arbitrary")),
    )(d_act, d_act, up, up, up, up, up, up, cw, cw)


def _local_step(x, tgt, n1w, n2w, fnw, gp, gnw, wp, conv_w, fcw, rest_weights, early_grads):
    h1 = _rmsnorm_fwd(x, n1w, "norm1")
    proj = _mm(h1, wp, "nn", name="proj")
    u_v, w_k, q_dec, k_end, attn, tinv, g_end = _gdn_pre(proj, conv_w, gp)
    mix, states = _gdn_scan(u_v, w_k, q_dec, k_end, attn, g_end, proj, gnw)
    mix, lse = _attn_fwd(proj, mix)
    w_out, w_up4, w_down = rest_weights([mix])
    x2 = _mm(mix, w_out, "nn", residual=x, name="outproj")
    h2 = _rmsnorm_fwd(x2, n2w, "norm2")
    up = _mm(h2, w_up4, "nn", b_blocks=True, name="up")
    act = _ffn_act(up, fcw)
    x3 = _mm(act, w_down, "nn", residual=x2, name="down")
    loss, dx3, d_fnw = _loss_head(x3, fnw, tgt, "loss_head")
    d_act = _mm(dx3, w_down, "nt", name="d_act")
    d_wdown = _mm(act, dx3, "tn", name="d_wdown")
    d_upg, d_upu, d_fcwg, d_fcwu = _ffn_act_bwd(d_act, up, fcw)
    d_wup = _mm(h2, d_upg, "tn", place=("blocks", N_CHIPS, 0), tn=w_up4.shape[2], name="d_wgate")
    d_wup = _mm(h2, d_upu, "tn", place=("blocks", N_CHIPS, N_CHIPS // 2), tn=w_up4.shape[2], into=d_wup, name="d_wup")
    token = early_grads[0](d_wup, d_wdown)
    d_h2 = _mm_nt_blocks([d_upg, d_upu], w_up4, "d_h2")
    dx2, d_n2w = _rmsnorm_bwd(d_h2, x2, n2w + token[0:1, 0:1], dx3, "norm2_bwd")
    token = early_grads[1](dx2)
    d_mix = _mm(dx2, w_out, "nt", name="d_mix")
    d_wout = _mm(mix, dx2, "tn", name="d_wout")
    dq_b, dk_b, dv_b = _attn_bwd(proj, mix, lse, d_mix)
    d_uv, d_wk, d_qd, d_ke, d_at, d_ge, d_z, d_gnw = _gdn_scan_bwd(u_v, w_k, q_dec, k_end, attn, g_end, proj,
                                                                   gnw + token[0:1, 0:1], states, d_mix)
    d_pre, d_ba, d_gp = _gdn_post(proj, conv_w, gp, tinv, u_v, w_k, d_uv, d_wk, d_qd, d_ke, d_at, d_ge)
    d_qkva, d_convw = _conv_bwd(d_pre, proj, 0, conv_w, GDN_CONV, "gdn_conv_bwd", 512)
    d_proj = jnp.concatenate([d_qkva, d_z.astype(_MXU), dq_b, dk_b, dv_b, d_ba.astype(_MXU),
                              jnp.zeros((x.shape[0], P_COLS - P_BA - 128), _MXU)], axis=1)
    d_wp = _mm(h1, d_proj, "tn", name="d_wp")
    token = early_grads[2](d_wp, d_wout)
    d_h1 = _mm(d_proj, wp, "nt", name="d_h1")
    dx, d_n1w = _rmsnorm_bwd(d_h1, x, n1w + token[0:1, 0:1], dx2, "norm1_bwd")
    grads = dict(wp=d_wp, conv_w=d_convw, w_out=d_wout, w_up=d_wup, fcw_g=d_fcwg, fcw_u=d_fcwu, w_down=d_wdown,
                 n1w=d_n1w, n2w=d_n2w, fnw=d_fnw, gp=d_gp, gnw=d_gnw)
    return loss, dx, grads


_HBM = pl.BlockSpec(memory_space=pltpu.HBM)


def _pos():
    return lax.axis_index("x"), lax.axis_index("y"), lax.axis_index("c")


def _other_chips(x, y):
    return [(1 - x, y), (x, 1 - y), (1 - x, 1 - y)]


def _halvable(shape):
    return shape[0] % 32 == 0


def _rows_of_half(shape, half):
    if not _halvable(shape):
        return pl.ds(0, shape[0])
    return pl.ds(pl.multiple_of(half * (shape[0] // 2), 16), shape[0] // 2)


def _gather_halves(shards, name):
    n = len(shards)
    shapes = [s.shape for s in shards]

    def body(*refs):
        ins, outs = refs[:n], refs[n:2 * n]
        send_sems, recv_sems = refs[2 * n:]
        x, y, c = _pos()
        q = 2 * x + y
        chips = _other_chips(x, y)

        def copy(a, j, block):
            px, py = chips[j]
            rows = _rows_of_half(shapes[a], c)
            return pltpu.make_async_remote_copy(
                src_ref=ins[a].at[rows, :], dst_ref=outs[a].at[block, rows, :], send_sem=send_sems.at[3 * a + j],
                recv_sem=recv_sems.at[3 * a + j], device_id=(px, py, c), device_id_type=MESH)

        sends = [copy(a, j, q) for a in range(n) for j in range(3)]
        for cp in sends:
            cp.start()
        for a in range(n):
            for j, (px, py) in enumerate(chips):
                copy(a, j, 2 * px + py).wait_recv()
        for cp in sends:
            cp.wait_send()

    return pl.pallas_call(
        body, name=name, in_specs=[_HBM] * n, out_specs=[_HBM] * n,
        out_shape=[jax.ShapeDtypeStruct((N_CHIPS,) + s.shape, s.dtype) for s in shards],
        scratch_shapes=[pltpu.SemaphoreType.DMA((3 * n,)), pltpu.SemaphoreType.DMA((3 * n,))],
    )(*shards)


_SEM = pl.BlockSpec(memory_space=pltpu.SEMAPHORE)
_ANY = pl.BlockSpec(memory_space=pl.ANY)
_DATAFLOW = pltpu.SideEffectType.DATAFLOW_SIDE_EFFECTING


def _in_hbm(a):
    return pltpu.with_memory_space_constraint(a, pltpu.HBM)


def _halves_copy(src_refs, land_refs, send_sems, recv_sems, shapes, a, j, block, x, y, c):
    px, py = _other_chips(x, y)[j]
    rows = _rows_of_half(shapes[a], c)
    return pltpu.make_async_remote_copy(
        src_ref=src_refs[a].at[rows, :], dst_ref=land_refs[a].at[block, rows, :], send_sem=send_sems.at[3 * a + j],
        recv_sem=recv_sems.at[3 * a + j], device_id=(px, py, c), device_id_type=MESH)


def _gather_halves_start(shards, after, name):
    n = len(shards)
    shapes = [s.shape for s in shards]

    def body(*refs):
        ins, lands = refs[:n], refs[n:2 * n]
        send_sems, recv_sems = refs[2 * n + 1], refs[2 * n + 2]
        token = refs[-1]
        x, y, c = _pos()
        q = 2 * x + y
        for a in range(n):
            for j in range(3):
                _halves_copy(ins, lands, send_sems, recv_sems, shapes, a, j, q, x, y, c).start()
        token[...] = jnp.zeros_like(token)

    land_shapes = [(N_CHIPS,) + s.shape for s in shards]
    return pl.pallas_call(
        body, name=name,
        out_shape=(pltpu.SemaphoreType.DMA((3 * n,)), pltpu.SemaphoreType.DMA((3 * n,)),
                   *[pltpu.HBM(s.shape, s.dtype) for s in shards],
                   *[pltpu.HBM(ls, s.dtype) for ls, s in zip(land_shapes, shards)],
                   jax.ShapeDtypeStruct((8, 128), F32)),
        in_specs=[_HBM] * (2 * n) + [_ANY],
        out_specs=(_SEM, _SEM, *[_HBM] * (2 * n), pl.BlockSpec(memory_space=pltpu.VMEM)),
        input_output_aliases={a: 2 + a for a in range(2 * n)},
        compiler_params=pltpu.CompilerParams(has_side_effects=_DATAFLOW),
    )(*[_in_hbm(s) for s in shards], *[_in_hbm(lax.empty(ls, s.dtype)) for ls, s in zip(land_shapes, shards)], after)


def _gather_halves_wait(started, after, name):
    send_sems, recv_sems, *thru = started
    n = len(thru) // 2
    shapes = [t.shape for t in thru[:n]]

    def body(*refs):
        ins, lands = refs[:n], refs[n:2 * n]
        send_sems, recv_sems = refs[2 * n], refs[2 * n + 1]
        x, y, c = _pos()
        q = 2 * x + y
        chips = _other_chips(x, y)
        for a in range(n):
            for j, (px, py) in enumerate(chips):
                _halves_copy(ins, lands, send_sems, recv_sems, shapes, a, j, q, x, y, c).wait_send()
                _halves_copy(ins, lands, send_sems, recv_sems, shapes, a, j, 2 * px + py, x, y, c).wait_recv()

    outs = pl.pallas_call(
        body, name=name, out_shape=[pltpu.HBM(t.shape, t.dtype) for t in thru],
        in_specs=[_HBM] * (2 * n) + [_SEM, _SEM] + [_ANY] * len(after), out_specs=[_HBM] * (2 * n),
        input_output_aliases={a: a for a in range(2 * n)},
        compiler_params=pltpu.CompilerParams(has_side_effects=_DATAFLOW),
    )(*thru, send_sems, recv_sems, *after)
    return outs[:n], outs[n:]


def _sibling_fill(gathered, name):
    big = [a for a, g in enumerate(gathered) if _halvable(g.shape[1:])]
    n = len(gathered)

    def body(*refs):
        ins, outs = refs[:n], refs[n:2 * n]
        send_sems, recv_sems = refs[2 * n:]
        x, y, c = _pos()
        chips = _other_chips(x, y)

        def copy(k, j, half):
            a = big[k]
            px, py = chips[j]
            rows = _rows_of_half(gathered[a].shape[1:], half)
            return pltpu.make_async_remote_copy(
                src_ref=ins[a].at[2 * px + py, rows, :], dst_ref=outs[a].at[2 * px + py, rows, :],
                send_sem=send_sems.at[3 * k + j], recv_sem=recv_sems.at[3 * k + j],
                device_id=(x, y, 1 - c), device_id_type=MESH)

        sends = [copy(k, j, c) for k in range(len(big)) for j in range(3)]
        for cp in sends:
            cp.start()
        for k in range(len(big)):
            for j in range(3):
                copy(k, j, 1 - c).wait_recv()
        for cp in sends:
            cp.wait_send()

    return pl.pallas_call(
        body, name=name, in_specs=[_HBM] * n, out_specs=[_HBM] * n,
        out_shape=[jax.ShapeDtypeStruct(g.shape, g.dtype) for g in gathered],
        input_output_aliases={a: a for a in range(n)},
        scratch_shapes=[pltpu.SemaphoreType.DMA((3 * len(big),)), pltpu.SemaphoreType.DMA((3 * len(big),))],
    )(*gathered)


def _place_own(shards, gathered, cq, name):
    n = len(shards)
    steps = 4

    def body(cq_ref, *refs):
        for a in range(n):
            refs[2 * n + a][...] = refs[a][...]

    def tile(shape):
        return shape[0] // steps if _halvable(shape) else shape[0]

    in_specs = [pl.BlockSpec((tile(s.shape), s.shape[1]), (lambda i, s_: (i, 0)) if _halvable(s.shape) else (lambda i, s_: (0, 0)))
                for s in shards]
    in_specs += [pl.BlockSpec(memory_space=pl.ANY)] * n
    out_specs = [pl.BlockSpec((None, tile(s.shape), s.shape[1]),
                              (lambda i, s_: (s_[1], i, 0)) if _halvable(s.shape) else (lambda i, s_: (s_[1], 0, 0)))
                 for s in shards]
    gs = pltpu.PrefetchScalarGridSpec(num_scalar_prefetch=1, grid=(steps,), in_specs=in_specs, out_specs=out_specs)
    return pl.pallas_call(
        body, name=name, grid_spec=gs, out_shape=[jax.ShapeDtypeStruct(g.shape, g.dtype) for g in gathered],
        input_output_aliases={1 + n + a: a for a in range(n)},
        compiler_params=_cparams(("arbitrary",)),
    )(cq, *shards, *gathered)


def _half_rows(ref, c, rh):
    return ref.at[:, pl.ds(pl.multiple_of(c * rh, 8), rh), :]


def _grad_sibling(fams, small, name):
    n = len(fams)
    ns = 0 if small is None else 1
    rhs = [f.shape[1] // 2 for f in fams]

    def body(*refs):
        ins, outs = refs[:n], refs[n + ns:2 * n + ns]
        send_sems, recv_sems = refs[2 * (n + ns)], refs[2 * (n + ns) + 1]
        x, y, c = _pos()
        bigs = [pltpu.make_async_remote_copy(src_ref=_half_rows(ins[a], 1 - c, rhs[a]), dst_ref=outs[a],
                                             send_sem=send_sems.at[7 * ns + a], recv_sem=recv_sems.at[7 * ns + a],
                                             device_id=(x, y, 1 - c), device_id_type=MESH) for a in range(n)]
        for cp in bigs:
            cp.start()
        sends = []
        if ns:
            small_ref, all_ref, loc_sem = refs[n], refs[2 * n + 1], refs[2 * n + 4]
            me = 4 * x + 2 * y + c
            mine = pltpu.make_async_copy(small_ref, all_ref.at[me], loc_sem)
            mine.start()

            def peer(r):
                dx, dy, dc = (r >> 2) & 1, (r >> 1) & 1, r & 1
                return (x if dx == 0 else 1 - x), (y if dy == 0 else 1 - y), (c if dc == 0 else 1 - c)

            def small_copy(r, slot):
                return pltpu.make_async_remote_copy(src_ref=small_ref, dst_ref=all_ref.at[slot], send_sem=send_sems.at[r - 1],
                                                    recv_sem=recv_sems.at[r - 1], device_id=peer(r), device_id_type=MESH)

            sends = [small_copy(r, me) for r in range(1, 8)]
            for cp in sends:
                cp.start()
            for r in range(1, 8):
                px, py, pc = peer(r)
                small_copy(r, 4 * px + 2 * py + pc).wait_recv()
        for cp in bigs:
            cp.wait_recv()
        for cp in bigs + sends:
            cp.wait_send()
        if ns:
            mine.wait()

    return pl.pallas_call(
        body, name=name, in_specs=[_HBM] * (n + ns), out_specs=[_HBM] * (n + ns),
        out_shape=[jax.ShapeDtypeStruct((f.shape[0], f.shape[1] // 2, f.shape[2]), f.dtype) for f in fams]
        + ([jax.ShapeDtypeStruct((8,) + small.shape, small.dtype)] if ns else []),
        scratch_shapes=[pltpu.SemaphoreType.DMA((7 * ns + n,)), pltpu.SemaphoreType.DMA((7 * ns + n,))]
        + ([pltpu.SemaphoreType.DMA] if ns else []),
    )(*fams, *([small] if ns else []))


def _chips_copy(src_refs, land_refs, send_sems, recv_sems, a, j, x, y, c):
    px, py = _other_chips(x, y)[j]
    return pltpu.make_async_remote_copy(src_ref=src_refs[a].at[2 * px + py], dst_ref=land_refs[a].at[j],
                                        send_sem=send_sems.at[3 * a + j], recv_sem=recv_sems.at[3 * a + j],
                                        device_id=(px, py, c), device_id_type=MESH)


def _grad_chips_start(parts, name):
    n = len(parts)

    def body(*refs):
        ins, lands = refs[:n], refs[n:2 * n]
        send_sems, recv_sems = refs[2 * n], refs[2 * n + 1]
        token = refs[-1]
        x, y, c = _pos()
        for a in range(n):
            for j in range(3):
                _chips_copy(ins, lands, send_sems, recv_sems, a, j, x, y, c).start()
        token[...] = jnp.zeros_like(token)

    land_shapes = [(3,) + p.shape[1:] for p in parts]
    return pl.pallas_call(
        body, name=name,
        out_shape=(pltpu.SemaphoreType.DMA((3 * n,)), pltpu.SemaphoreType.DMA((3 * n,)),
                   *[pltpu.HBM(p.shape, p.dtype) for p in parts],
                   *[pltpu.HBM(ls, p.dtype) for ls, p in zip(land_shapes, parts)],
                   jax.ShapeDtypeStruct((8, 128), F32)),
        in_specs=[_HBM] * (2 * n),
        out_specs=(_SEM, _SEM, *[_HBM] * (2 * n), pl.BlockSpec(memory_space=pltpu.VMEM)),
        input_output_aliases={a: 2 + a for a in range(2 * n)},
        compiler_params=pltpu.CompilerParams(has_side_effects=_DATAFLOW),
    )(*[_in_hbm(p) for p in parts], *[_in_hbm(lax.empty(ls, p.dtype)) for ls, p in zip(land_shapes, parts)])


def _grad_chips_wait(started, after, name):
    send_sems, recv_sems, *thru = started
    n = len(thru) // 2

    def body(*refs):
        ins, lands = refs[:n], refs[n:2 * n]
        send_sems, recv_sems = refs[2 * n], refs[2 * n + 1]
        x, y, c = _pos()
        for a in range(n):
            for j in range(3):
                cp = _chips_copy(ins, lands, send_sems, recv_sems, a, j, x, y, c)
                cp.wait_send()
                cp.wait_recv()

    outs = pl.pallas_call(
        body, name=name, out_shape=[pltpu.HBM(t.shape, t.dtype) for t in thru],
        in_specs=[_HBM] * (2 * n) + [_SEM, _SEM] + [_ANY] * len(after), out_specs=[_HBM] * (2 * n),
        input_output_aliases={a: a for a in range(2 * n)},
        compiler_params=pltpu.CompilerParams(has_side_effects=_DATAFLOW),
    )(*thru, send_sems, recv_sems, *after)
    return outs[n:]


def _sibling_copy(src_refs, land_refs, send_sems, recv_sems, rhs, a, c, x, y):
    return pltpu.make_async_remote_copy(src_ref=_half_rows(src_refs[a], 1 - c, rhs[a]), dst_ref=land_refs[a],
                                        send_sem=send_sems.at[a], recv_sem=recv_sems.at[a],
                                        device_id=(x, y, 1 - c), device_id_type=MESH)


def _grad_sibling_start(fams, name):
    n = len(fams)
    rhs = [f.shape[1] // 2 for f in fams]

    def body(*refs):
        ins, lands = refs[:n], refs[n:2 * n]
        send_sems, recv_sems = refs[2 * n], refs[2 * n + 1]
        token = refs[-1]
        x, y, c = _pos()
        for a in range(n):
            _sibling_copy(ins, lands, send_sems, recv_sems, rhs, a, c, x, y).start()
        token[...] = jnp.zeros_like(token)

    land_shapes = [(f.shape[0], f.shape[1] // 2, f.shape[2]) for f in fams]
    return pl.pallas_call(
        body, name=name,
        out_shape=(pltpu.SemaphoreType.DMA((n,)), pltpu.SemaphoreType.DMA((n,)),
                   *[pltpu.HBM(f.shape, f.dtype) for f in fams],
                   *[pltpu.HBM(ls, f.dtype) for ls, f in zip(land_shapes, fams)],
                   jax.ShapeDtypeStruct((8, 128), F32)),
        in_specs=[_HBM] * (2 * n),
        out_specs=(_SEM, _SEM, *[_HBM] * (2 * n), pl.BlockSpec(memory_space=pltpu.VMEM)),
        input_output_aliases={a: 2 + a for a in range(2 * n)},
        compiler_params=pltpu.CompilerParams(has_side_effects=_DATAFLOW),
    )(*[_in_hbm(f) for f in fams], *[_in_hbm(lax.empty(ls, f.dtype)) for ls, f in zip(land_shapes, fams)])


def _grad_sibling_wait(started, after, name):
    send_sems, recv_sems, *thru = started
    n = len(thru) // 2
    rhs = [t.shape[1] // 2 for t in thru[:n]]

    def body(*refs):
        ins, lands = refs[:n], refs[n:2 * n]
        send_sems, recv_sems = refs[2 * n], refs[2 * n + 1]
        x, y, c = _pos()
        for a in range(n):
            cp = _sibling_copy(ins, lands, send_sems, recv_sems, rhs, a, c, x, y)
            cp.wait_send()
            cp.wait_recv()

    outs = pl.pallas_call(
        body, name=name, out_shape=[pltpu.HBM(t.shape, t.dtype) for t in thru],
        in_specs=[_HBM] * (2 * n) + [_SEM, _SEM] + [_ANY] * len(after), out_specs=[_HBM] * (2 * n),
        input_output_aliases={a: a for a in range(2 * n)},
        compiler_params=pltpu.CompilerParams(has_side_effects=_DATAFLOW),
    )(*thru, send_sems, recv_sems, *after)
    return outs[:n], outs[n:]


def _grad_share(fulls, name):
    n = len(fulls)
    rhs = [f.shape[0] // 2 for f in fulls]

    def body(*refs):
        ins, outs = refs[:n], refs[n:2 * n]
        send_sems, recv_sems = refs[2 * n:]
        x, y, c = _pos()

        def copy(a, half):
            rows = pl.ds(pl.multiple_of(half * rhs[a], 8), rhs[a])
            return pltpu.make_async_remote_copy(src_ref=ins[a].at[rows, :], dst_ref=outs[a].at[rows, :],
                                                send_sem=send_sems.at[a], recv_sem=recv_sems.at[a],
                                                device_id=(x, y, 1 - c), device_id_type=MESH)

        sends = [copy(a, c) for a in range(n)]
        for cp in sends:
            cp.start()
        for a in range(n):
            copy(a, 1 - c).wait_recv()
        for cp in sends:
            cp.wait_send()

    return pl.pallas_call(
        body, name=name, in_specs=[_HBM] * n, out_specs=[_HBM] * n,
        out_shape=[jax.ShapeDtypeStruct(f.shape, f.dtype) for f in fulls],
        input_output_aliases={a: a for a in range(n)},
        scratch_shapes=[pltpu.SemaphoreType.DMA((n,)), pltpu.SemaphoreType.DMA((n,))],
    )(*fulls)


def _add_sibling(own, recv, cq, name):
    nb, R, Cc = own.shape
    Rh = R // 2

    def body(cq_ref, a_ref, b_ref, o32_ref, o16_ref):
        s = a_ref[...] + b_ref[...]
        o32_ref[...] = s
        o16_ref[...] = s.astype(o16_ref.dtype)

    sp = pl.BlockSpec((1, Rh, Cc), lambda b, s: (b, 0, 0))
    gs = pltpu.PrefetchScalarGridSpec(
        num_scalar_prefetch=1, grid=(nb,),
        in_specs=[pl.BlockSpec((1, Rh, Cc), lambda b, s: (b, s[0], 0)), sp], out_specs=[sp, sp])
    return pl.pallas_call(
        body, name=name, grid_spec=gs,
        out_shape=[jax.ShapeDtypeStruct((nb, Rh, Cc), F32), jax.ShapeDtypeStruct((nb, Rh, Cc), _MXU)],
        compiler_params=_cparams(("parallel",)),
    )(cq, own, recv)


def _add_chips(part32, recv3, cq, name):
    nb, Rh, Cc = part32.shape

    def body(cq_ref, a_ref, b_ref, o_ref):
        acc = a_ref[0]
        for j in range(3):
            acc = acc + b_ref[j].astype(F32)
        o_ref[...] = acc

    gs = pltpu.PrefetchScalarGridSpec(
        num_scalar_prefetch=1, grid=(1,),
        in_specs=[pl.BlockSpec((1, Rh, Cc), lambda i, s: (s[1], 0, 0)), pl.BlockSpec((3, Rh, Cc), lambda i, s: (0, 0, 0))],
        out_specs=pl.BlockSpec((Rh, Cc), lambda i, s: (s[0], 0)))
    return pl.pallas_call(
        body, name=name, grid_spec=gs, out_shape=jax.ShapeDtypeStruct((2 * Rh, Cc), F32),
        compiler_params=_cparams(("arbitrary",)),
    )(cq, part32, recv3)


def _sum_devices(small_all):
    def body(s_ref, o_ref):
        tot = s_ref[0]
        for d in range(1, 8):
            tot = tot + s_ref[d]
        o_ref[...] = tot

    return pl.pallas_call(body, name="sum_devices", out_shape=jax.ShapeDtypeStruct(small_all.shape[1:], F32))(small_all)


def _adamw(w, g, m, v, name):
    R, Cc = w.shape
    T = max([t for t in range(8, 257, 8) if R % t == 0], default=R)
    c1 = 1.0 / (1.0 - ADAM_B1 ** ADAM_STEP)
    c2 = 1.0 / (1.0 - ADAM_B2 ** ADAM_STEP)

    def body(w_ref, g_ref, m_ref, v_ref, d_ref, mo_ref, vo_ref):
        gv = g_ref[...]
        mn = ADAM_B1 * m_ref[...] + (1.0 - ADAM_B1) * gv
        vn = ADAM_B2 * v_ref[...] + (1.0 - ADAM_B2) * (gv * gv)
        mo_ref[...] = mn
        vo_ref[...] = vn
        d_ref[...] = -ADAM_LR * ((mn * c1) / (jnp.sqrt(vn * c2) + ADAM_EPS) + ADAM_WD * w_ref[...])

    sp = pl.BlockSpec((T, Cc), lambda i: (i, 0))
    sh = jax.ShapeDtypeStruct((R, Cc), F32)
    return pl.pallas_call(
        body, name=name, grid=(R // T,), in_specs=[sp] * 4, out_specs=(sp, sp, sp), out_shape=(sh, sh, sh),
        compiler_params=_cparams(("parallel",)),
    )(w, g, m, v)


SMALL_ROWS = 32
REPL_ROWS = 8


def _pad_lanes(v, n=D_MODEL):
    return jnp.pad(v, ((0, 0), (0, n - v.shape[1])))


def kernel(x, norm1_w, w_in, conv_qkv_w, a_log, dt_bias, gdn_norm_w, w_out, norm2_w, w_up, ffn_conv_w, w_down, final_norm_w, loss_target, m_norm1_w, m_w_in, m_conv_qkv_w, m_a_log, m_dt_bias, m_gdn_norm_w, m_w_out, m_norm2_w, m_w_up, m_ffn_conv_w, m_w_down, m_final_norm_w, v_norm1_w, v_w_in, v_conv_qkv_w, v_a_log, v_dt_bias, v_gdn_norm_w, v_w_out, v_norm2_w, v_w_up, v_ffn_conv_w, v_w_down, v_final_norm_w):
    c = lax.axis_index("c")
    q = 2 * lax.axis_index("x") + lax.axis_index("y")
    S = x.shape[1]
    cq = jnp.stack([c, q]).astype(jnp.int32)

    def gather(shards, tag):
        got = _gather_halves(shards, "gather_" + tag)
        got = _sibling_fill(got, "fill_" + tag)
        return _place_own(shards, got, cq, "place_" + tag)

    g_in, g_conv, g_fconv = gather([w_in[0].astype(_MXU), conv_qkv_w[0], ffn_conv_w[0]], "in")
    rest = [w_out[0].astype(_MXU), w_up[0].astype(_MXU), w_down[0].astype(_MXU)]
    *rest_started, token = _gather_halves_start(rest, g_conv, "gather_rest_start")

    def rest_weights(after):
        shards, got = _gather_halves_wait(rest_started, after, "gather_rest_wait")
        got = _sibling_fill(got, "fill_rest")
        g_out, g_up, g_down = _place_own(shards, got, cq, "place_rest")
        return g_out.reshape(D_MODEL, D_MODEL), g_up, g_down.reshape(D_FF, D_MODEL)
    wp = _wp_assemble(g_in)
    conv_f = jnp.concatenate([g_conv[i] for i in range(N_CHIPS)], axis=1)
    fcw = jnp.concatenate([g_fconv[i] for i in range(N_CHIPS)], axis=1)
    gp = _pad_lanes(jnp.concatenate([a_log, dt_bias], axis=1), 128)
    fnw = final_norm_w[None, :]
    early = {}

    def early_sibling(d_wup, d_wdown):
        *early["sibling"], tok = _grad_sibling_start([d_wup, d_wdown.reshape(N_CHIPS, D_FF // N_CHIPS, D_MODEL)],
                                                     "grad_sibling_early_start")
        return tok

    def early_chips(dx2):
        fams_e, got_e = _grad_sibling_wait(early["sibling"], [dx2], "grad_sibling_early_wait")
        early["parts"] = [_add_sibling(f, r, cq, "add_sibling_" + nm) for f, r, nm in zip(fams_e, got_e, ("w_up", "w_down"))]
        *early["started"], tok = _grad_chips_start([p[1] for p in early["parts"]], "grad_chips_start")
        return tok

    def late_sibling(d_wp, d_wout):
        *early["late_sibling"], tok = _grad_sibling_start(
            [_win_split(d_wp), d_wout.reshape(N_CHIPS, D_MODEL // N_CHIPS, D_MODEL)], "grad_sibling_late_start")
        return tok

    early_grads = (early_sibling, early_chips, late_sibling)

    loss_l, dx, g = _local_step(x[0], loss_target[0], norm1_w + token[0:1, 0:1], norm2_w, fnw, gp, gdn_norm_w, wp, conv_f,
                                fcw, rest_weights, early_grads)
    n_fc = FFN_CONV * D_FF

    def rows_of(v):
        flat = v.reshape(-1)
        return jnp.pad(flat, (0, -flat.shape[0] % D_MODEL)).reshape(-1, D_MODEL)

    gp_row = _pad_lanes(jnp.concatenate([g["gp"][0:1, 0:8], loss_l[0:1, 0:1]], axis=1))
    small = jnp.concatenate([g["n1w"], g["n2w"], g["fnw"], gp_row, _pad_lanes(g["gnw"]),
                             rows_of(g["conv_w"]), rows_of(g["fcw_g"]), rows_of(g["fcw_u"])], axis=0)
    small = jnp.pad(small, ((0, SMALL_ROWS - small.shape[0]), (0, 0)))
    (small_all,) = _grad_sibling([], small, "gather_small")
    fams, got = _grad_sibling_wait(early["late_sibling"], [dx], "grad_sibling_late_wait")
    parts = [_add_sibling(f, r, cq, "add_sibling_" + nm) for f, r, nm in zip(fams, got, ("w_in", "w_out"))]
    *late_started, late_token = _grad_chips_start([p[1] for p in parts], "grad_chips_late_start")
    got3_e = _grad_chips_wait(early["started"], [dx, g["wp"], late_token], "grad_chips_wait")
    g_w_up, g_w_down = _grad_share(
        [_add_chips(p[0], r3, cq, "add_chips_" + nm) for p, r3, nm in zip(early["parts"], got3_e, ("w_up", "w_down"))],
        "grad_share_early")
    big = {}

    def adamw_big(nm, w, gg, m, v):
        d_, m_, v_ = _adamw(w[0], gg, m[0], v[0], "adamw_" + nm)
        big[nm] = (gg[None], d_[None], m_[None], v_[None])

    adamw_big("w_up", w_up, g_w_up, m_w_up, v_w_up)
    adamw_big("w_down", w_down, g_w_down, m_w_down, v_w_down)
    got3 = _grad_chips_wait(late_started, [big["w_up"][1], big["w_down"][1]], "grad_chips_late_wait")
    g_w_in, g_w_out = _grad_share(
        [_add_chips(p[0], r3, cq, "add_chips_" + nm) for p, r3, nm in zip(parts, got3, ("w_in", "w_out"))],
        "grad_share_late")
    small_red = _sum_devices(small_all)
    loss = small_red[3, 8]
    r0 = 5
    r1 = r0 + GDN_CONV * 3 * GDN_WIDTH // D_MODEL
    r2 = r1 + -(-n_fc // D_MODEL)
    conv_red = small_red[r0:r1].reshape(GDN_CONV, 3 * GDN_WIDTH)
    fc_red = jnp.concatenate([small_red[r1:r2].reshape(-1)[:n_fc].reshape(FFN_CONV, D_FF),
                              small_red[r2:2 * r2 - r1].reshape(-1)[:n_fc].reshape(FFN_CONV, D_FF)], axis=1)
    g_conv_w = lax.dynamic_slice_in_dim(conv_red, q * (3 * GDN_WIDTH // N_CHIPS), 3 * GDN_WIDTH // N_CHIPS, axis=1)
    g_fconv_w = lax.dynamic_slice_in_dim(fc_red, q * (2 * D_FF // N_CHIPS), 2 * D_FF // N_CHIPS, axis=1)
    g_n1w, g_n2w, g_fnw = small_red[0:1], small_red[1:2], small_red[2]
    g_alog, g_dtb, g_gnw = small_red[3:4, 0:4], small_red[3:4, 4:8], small_red[4:5, 0:128]
    for nm, w, gg, m, v in (("w_in", w_in, g_w_in, m_w_in, v_w_in), ("conv_qkv_w", conv_qkv_w, g_conv_w, m_conv_qkv_w, v_conv_qkv_w),
                            ("w_out", w_out, g_w_out, m_w_out, v_w_out),
                            ("ffn_conv_w", ffn_conv_w, g_fconv_w, m_ffn_conv_w, v_ffn_conv_w)):
        adamw_big(nm, w, gg, m, v)

    def pack_small(n1, n2, fn, al, db, gn):
        return jnp.concatenate([n1, n2, fn[None, :], _pad_lanes(jnp.concatenate([al, db], axis=1)), _pad_lanes(gn),
                                jnp.zeros((REPL_ROWS - 5, D_MODEL), F32)], axis=0)

    sw = pack_small(norm1_w, norm2_w, final_norm_w, a_log, dt_bias, gdn_norm_w)
    sm = pack_small(m_norm1_w, m_norm2_w, m_final_norm_w, m_a_log, m_dt_bias, m_gdn_norm_w)
    sv = pack_small(v_norm1_w, v_norm2_w, v_final_norm_w, v_a_log, v_dt_bias, v_gdn_norm_w)
    sd, smn, svn = _adamw(sw, small_red[:REPL_ROWS], sm, sv, "adamw_small")

    def unpack_small(t):
        return dict(norm1_w=t[0:1], norm2_w=t[1:2], final_norm_w=t[2], a_log=t[3:4, 0:4], dt_bias=t[3:4, 4:8],
                    gdn_norm_w=t[4:5, 0:128])

    sg = dict(norm1_w=g_n1w, norm2_w=g_n2w, final_norm_w=g_fnw, a_log=g_alog, dt_bias=g_dtb, gdn_norm_w=g_gnw)
    sd, smn, svn = unpack_small(sd), unpack_small(smn), unpack_small(svn)
    names = ["norm1_w", "w_in", "conv_qkv_w", "a_log", "dt_bias", "gdn_norm_w", "w_out", "norm2_w", "w_up",
             "ffn_conv_w", "w_down", "final_norm_w"]
    grads = [big[n][0] if n in big else sg[n] for n in names]
    deltas = [big[n][1] if n in big else sd[n] for n in names]
    new_m = [big[n][2] if n in big else smn[n] for n in names]
    new_v = [big[n][3] if n in big else svn[n] for n in names]
    return (loss, dx[None], *grads, *deltas, *new_m, *new_v)
```

```python
import functools
import math

import numpy as np
import jax
import jax.numpy as jnp
from jax import lax
from jax.experimental import pallas as pl
from jax.experimental.pallas import tpu as pltpu

F32 = jnp.float32
BF16 = jnp.bfloat16
_MXU = jnp.bfloat16
_HI = lax.Precision.HIGHEST
EPS = 1e-6
V7X_VMEM_LIMIT = 56 * 1024 * 1024
MESH = pl.DeviceIdType.MESH

D_MODEL = 1024
GDN_HEADS, GDN_DIM, GDN_CHUNK, GDN_CONV = 4, 128, 64, 4
GDN_WIDTH = GDN_HEADS * GDN_DIM
DIL_HEADS, DIL_DIM = 8, 64
DIL_WIDTH = DIL_HEADS * DIL_DIM
D_FF, FFN_CONV = 2816, 3
IN_COLS = 3592
P_COLS = 3840
P_Z, P_QKVB, P_BA = 1536, 2048, 3584
ATT_T = 1024
ADAM_LR, ADAM_B1, ADAM_B2, ADAM_EPS, ADAM_WD, ADAM_STEP = 0.001, 0.9, 0.999, 1e-08, 0.01, 10
N_CHIPS = 4


def _cparams(sem=None, vmem=None):
    kw = {}
    if sem is not None:
        kw["dimension_semantics"] = sem
    if vmem is not None:
        kw["vmem_limit_bytes"] = vmem
    return pltpu.CompilerParams(**kw)


def _silu(x):
    return x * jax.nn.sigmoid(x)


def _pick_tile(n, cap):
    best = None
    for t in range(128, min(n, cap) + 1, 128):
        if n % t == 0:
            best = t
    return best or n


def _mm(a, b, mode, *, out_dtype=F32, residual=None, name, b_blocks=False, place=None, into=None, tn=None):
    if mode == "nn":
        M, K = a.shape
        N = b.shape[0] * b.shape[2] if b_blocks else b.shape[1]
    elif mode == "nt":
        (M, K), (N, _) = a.shape, b.shape
    else:
        (K, M), (_, N) = a.shape, b.shape
    tm = _pick_tile(M, 1024)
    tn = b.shape[2] if b_blocks else (tn or _pick_tile(N, 1536))

    def vmem(tm, tn):
        return 2 * (tm * K * a.dtype.itemsize + tn * K * b.dtype.itemsize
                    + tm * tn * (jnp.dtype(out_dtype).itemsize + (4 if residual is not None else 0))) + 3 * tm * tn * 4

    fixed_tn = b_blocks or (place is not None and place[0] == "blocks")
    while vmem(tm, tn) > 40 * 1024 * 1024:
        if (tm >= tn or fixed_tn) and tm % 256 == 0:
            tm //= 2
        elif tn % 256 == 0 and not fixed_tn:
            tn //= 2
        else:
            tm //= 2
    a_spec = pl.BlockSpec((K, tm), lambda j, i: (0, i)) if mode == "tn" else pl.BlockSpec((tm, K), lambda j, i: (i, 0))
    if b_blocks:
        b_spec = pl.BlockSpec((None, K, tn), lambda j, i: (j, 0, 0))
    else:
        b_spec = pl.BlockSpec((tn, K), lambda j, i: (j, 0)) if mode == "nt" else pl.BlockSpec((K, tn), lambda j, i: (0, j))
    r_spec = pl.BlockSpec((tm, tn), lambda j, i: (i, j))
    if place is None:
        o_spec, o_shape = r_spec, (M, N)
    elif place[0] == "rows":
        off = place[2] // tm
        o_spec, o_shape = pl.BlockSpec((tm, tn), lambda j, i: (i + off, j)), (place[1], N)
    else:
        off = place[2]
        o_spec, o_shape = pl.BlockSpec((None, tm, tn), lambda j, i: (j + off, i, 0)), (place[1], M, tn)
    dims = {"nn": (((1,), (0,)), ((), ())), "nt": (((1,), (1,)), ((), ())), "tn": (((0,), (0,)), ((), ()))}[mode]

    def body(*refs):
        a_ref, b_ref = refs[0], refs[1]
        o_ref = refs[-1]
        acc = lax.dot_general(a_ref[...].astype(_MXU), b_ref[...].astype(_MXU), dims, preferred_element_type=F32)
        if residual is not None:
            acc = acc + refs[2][...]
        o_ref[...] = acc.astype(out_dtype)

    ins, specs, alias = [a, b], [a_spec, b_spec], {}
    if residual is not None:
        ins.append(residual)
        specs.append(r_spec)
    if into is not None:
        alias = {len(ins): 0}
        ins.append(into)
        specs.append(pl.BlockSpec(memory_space=pl.ANY))
    return pl.pallas_call(
        body, name=name, grid=(N // tn, M // tm), in_specs=specs, out_specs=o_spec,
        out_shape=jax.ShapeDtypeStruct(o_shape, out_dtype), input_output_aliases=alias,
        compiler_params=_cparams(("parallel", "parallel"), V7X_VMEM_LIMIT),
    )(*ins)


def _mm_nt_blocks(a_list, b4, name):
    M = a_list[0].shape[0]
    nb, N, Kb = b4.shape
    tm, tn = _pick_tile(M, 512), _pick_tile(N, 512)

    def body(a0_ref, a1_ref, b_ref, o_ref):
        acc = None
        for blk in range(nb):
            a_ref = (a0_ref, a1_ref)[blk // 2]
            lo = (blk % 2) * Kb
            t = lax.dot_general(a_ref[:, lo:lo + Kb].astype(_MXU), b_ref[blk].astype(_MXU), (((1,), (1,)), ((), ())),
                                preferred_element_type=F32)
            acc = t if acc is None else acc + t
        o_ref[...] = acc

    a_spec = pl.BlockSpec((tm, 2 * Kb), lambda j, i: (i, 0))
    return pl.pallas_call(
        body, name=name, grid=(N // tn, M // tm),
        in_specs=[a_spec, a_spec, pl.BlockSpec((nb, tn, Kb), lambda j, i: (0, j, 0))],
        out_specs=pl.BlockSpec((tm, tn), lambda j, i: (i, j)), out_shape=jax.ShapeDtypeStruct((M, N), F32),
        compiler_params=_cparams(("parallel", "parallel"), V7X_VMEM_LIMIT),
    )(a_list[0], a_list[1], b4)


def _wp_assemble(g_in):
    nb, Dm, Wb = g_in.shape
    T = 256
    n_lo = P_QKVB - 2 * Wb

    def body(g_ref, o_ref):
        g2 = g_ref[2]
        o_ref[...] = jnp.concatenate(
            [g_ref[0], g_ref[1], g2[:, :n_lo], g2[:, n_lo + 8:], g_ref[3], g2[:, n_lo:n_lo + 8],
             jnp.zeros((T, P_COLS - P_BA - 8), g_in.dtype)], axis=1)

    return pl.pallas_call(
        body, name="wp_assemble", grid=(Dm // T,), in_specs=[pl.BlockSpec((nb, T, Wb), lambda i: (0, i, 0))],
        out_specs=pl.BlockSpec((T, P_COLS), lambda i: (i, 0)), out_shape=jax.ShapeDtypeStruct((Dm, P_COLS), g_in.dtype),
        compiler_params=_cparams(("parallel",)),
    )(g_in)


def _win_split(d_wp):
    Dm = d_wp.shape[0]
    Wb = IN_COLS // N_CHIPS
    T = 256

    def body(x_ref, o_ref):
        xv = x_ref[...]
        o_ref[0] = xv[:, 0:Wb]
        o_ref[1] = xv[:, Wb:2 * Wb]
        o_ref[2] = jnp.concatenate([xv[:, 2 * Wb:P_QKVB], xv[:, P_BA:P_BA + 8], xv[:, P_QKVB:3 * Wb - 8]], axis=1)
        o_ref[3] = xv[:, 3 * Wb - 8:P_BA]

    return pl.pallas_call(
        body, name="win_split", grid=(Dm // T,), in_specs=[pl.BlockSpec((T, P_COLS), lambda i: (i, 0))],
        out_specs=pl.BlockSpec((N_CHIPS, T, Wb), lambda i: (0, i, 0)),
        out_shape=jax.ShapeDtypeStruct((N_CHIPS, Dm, Wb), F32), compiler_params=_cparams(("parallel",)),
    )(d_wp)


def _rmsnorm_fwd(x, w, name):
    S, D = x.shape
    T = _pick_tile(S, 512)

    def body(x_ref, w_ref, o_ref):
        xv = x_ref[...]
        rs = lax.rsqrt(jnp.mean(xv * xv, axis=-1, keepdims=True) + EPS)
        o_ref[...] = (xv * rs * w_ref[...]).astype(o_ref.dtype)

    return pl.pallas_call(
        body, name=name, grid=(S // T,),
        in_specs=[pl.BlockSpec((T, D), lambda i: (i, 0)), pl.BlockSpec((1, D), lambda i: (0, 0))],
        out_specs=pl.BlockSpec((T, D), lambda i: (i, 0)),
        out_shape=jax.ShapeDtypeStruct((S, D), _MXU),
        compiler_params=_cparams(("parallel",)),
    )(x, w)


def _rmsnorm_bwd(dh, x, w, dres, name):
    S, D = x.shape
    T = _pick_tile(S, 512)

    def body(dh_ref, x_ref, w_ref, dres_ref, dx_ref, dw_ref):
        xv = x_ref[...]
        rs = lax.rsqrt(jnp.mean(xv * xv, axis=-1, keepdims=True) + EPS)
        xn = xv * rs
        dhv = dh_ref[...]
        dxn = dhv * w_ref[...]
        dx_ref[...] = dres_ref[...] + rs * (dxn - xn * jnp.mean(dxn * xn, axis=-1, keepdims=True))

        @pl.when(pl.program_id(0) == 0)
        def _():
            dw_ref[...] = jnp.zeros_like(dw_ref)

        dw_ref[...] += jnp.sum(dhv * xn, axis=0, keepdims=True)

    row = pl.BlockSpec((T, D), lambda i: (i, 0))
    vec = pl.BlockSpec((1, D), lambda i: (0, 0))
    return pl.pallas_call(
        body, name=name, grid=(S // T,), in_specs=[row, row, vec, row], out_specs=(row, vec),
        out_shape=(jax.ShapeDtypeStruct((S, D), F32), jax.ShapeDtypeStruct((1, D), F32)),
        compiler_params=_cparams(("arbitrary",)),
    )(dh, x, w, dres)


def _loss_head(x3, w, tgt, name):
    S, D = x3.shape
    T = _pick_tile(S, 512)

    def body(x_ref, w_ref, t_ref, loss_ref, dx_ref, dw_ref):
        xv = x_ref[...]
        rs = lax.rsqrt(jnp.mean(xv * xv, axis=-1, keepdims=True) + EPS)
        xn = xv * rs
        err = xn * w_ref[...] - t_ref[...]
        dy = err * (1.0 / D)
        dxn = dy * w_ref[...]
        dx_ref[...] = rs * (dxn - xn * jnp.mean(dxn * xn, axis=-1, keepdims=True))

        @pl.when(pl.program_id(0) == 0)
        def _():
            dw_ref[...] = jnp.zeros_like(dw_ref)
            loss_ref[...] = jnp.zeros_like(loss_ref)

        dw_ref[...] += jnp.sum(dy * xn, axis=0, keepdims=True)
        part = jnp.sum(jnp.sum(err * err, axis=-1, keepdims=True), axis=0, keepdims=True) * (0.5 / D)
        loss_ref[...] += jnp.broadcast_to(part, loss_ref.shape)

    row = pl.BlockSpec((T, D), lambda i: (i, 0))
    vec = pl.BlockSpec((1, D), lambda i: (0, 0))
    return pl.pallas_call(
        body, name=name, grid=(S // T,), in_specs=[row, vec, row],
        out_specs=(pl.BlockSpec((8, 128), lambda i: (0, 0)), row, vec),
        out_shape=(jax.ShapeDtypeStruct((8, 128), F32), jax.ShapeDtypeStruct((S, D), F32), jax.ShapeDtypeStruct((1, D), F32)),
        compiler_params=_cparams(("arbitrary",)),
    )(x3, w, tgt)


def _conv_taps(ext, w, K, T):
    out = None
    for i in range(K):
        lo = 8 - (K - 1) + i
        term = ext[lo:lo + T, :] * w[i:i + 1, :]
        out = term if out is None else out + term
    return out


def _conv_taps_t(ext, w, K, T):
    out = None
    for i in range(K):
        lo = (K - 1) - i
        term = ext[lo:lo + T, :] * w[i:i + 1, :]
        out = term if out is None else out + term
    return out


def _tri_masks(C):
    r = lax.broadcasted_iota(jnp.int32, (C, C), 0)
    c = lax.broadcasted_iota(jnp.int32, (C, C), 1)
    return r == c, r >= c, r > c, r <= c


_NN, _NT, _TN = ((1,), (0,)), ((1,), (1,)), ((0,), (0,))
_GDN_PASSES = dict(qk=1, inv=1, sol=1, scan=1, bwd=1)


def _bdot_raw(a, b, kind, passes):
    dims = ({"NN": ((2,), (1,)), "NT": ((2,), (2,)), "TN": ((1,), (1,))}[kind], ((0,), (0,)))
    if passes == 0:
        return lax.dot_general(a, b, dims, precision=_HI, preferred_element_type=F32)
    ah, bh = a.astype(BF16), b.astype(BF16)
    out = lax.dot_general(ah, bh, dims, preferred_element_type=F32)
    if passes == 3:
        al, bl = (a - ah.astype(F32)).astype(BF16), (b - bh.astype(F32)).astype(BF16)
        out = out + lax.dot_general(ah, bl, dims, preferred_element_type=F32) + lax.dot_general(al, bh, dims, preferred_element_type=F32)
    return out


@functools.partial(jax.custom_vjp, nondiff_argnums=(2, 3))
def _bdot(a, b, kind, passes):
    return _bdot_raw(a, b, kind, passes)


def _bdot_fwd(a, b, kind, passes):
    return _bdot_raw(a, b, kind, passes), (a, b)


def _bdot_bwd(kind, passes, res, ct):
    a, b = res
    if kind == "NN":
        return _bdot_raw(ct, b, "NT", passes), _bdot_raw(a, ct, "TN", passes)
    if kind == "NT":
        return _bdot_raw(ct, b, "NN", passes), _bdot_raw(ct, a, "TN", passes)
    return _bdot_raw(b, ct, "NT", passes), _bdot_raw(a, ct, "NN", passes)


_bdot.defvjp(_bdot_fwd, _bdot_bwd)


def _softplus(x):
    return jnp.maximum(x, 0.0) + jnp.log(1.0 + jnp.exp(-jnp.abs(x)))


def _gdn_stage1(cq, ck, cv, b_col, a_col, alog, dtb, dot=_bdot_raw):
    C = cq.shape[1]
    eye, incl, strict, incl_t = _tri_masks(C)
    qn = cq * lax.rsqrt(jnp.sum(cq * cq, axis=-1, keepdims=True) + EPS) * (GDN_DIM ** -0.5)
    kn = ck * lax.rsqrt(jnp.sum(ck * ck, axis=-1, keepdims=True) + EPS)
    beta = jax.nn.sigmoid(b_col)
    g = -jnp.exp(alog) * _softplus(a_col + dtb)
    g_row = jnp.sum(jnp.where(eye, g, 0.0), axis=1, keepdims=True)
    beta_row = jnp.sum(jnp.where(eye, beta, 0.0), axis=1, keepdims=True)
    gc_col = jnp.sum(jnp.where(incl, g_row, 0.0), axis=2, keepdims=True)
    gc_row = jnp.sum(jnp.where(incl_t, g, 0.0), axis=1, keepdims=True)
    dec = jnp.where(incl, jnp.exp(jnp.where(incl, gc_col - gc_row, 0.0)), 0.0)
    kk = dot(kn, kn, "NT", _GDN_PASSES["qk"])
    qk = dot(qn, kn, "NT", _GDN_PASSES["qk"])
    lmat = jnp.where(strict, dec * kk * beta_row, 0.0)
    attn = dec * qk * beta_row
    gam = jnp.exp(gc_col)
    gc_last = gc_col[:, C - 1:C, :]
    k_end = kn * (jnp.exp(gc_last - gc_col) * beta)
    return lmat, cv, gam * kn, gam * qn, attn, k_end, jnp.exp(gc_last)


def _tri_inv(lmat):
    C = lmat.shape[1]
    eye = _tri_masks(C)[0]
    ps = _GDN_PASSES["inv"]
    p = jnp.where(eye, 1.0, 0.0) - lmat
    lp = _bdot_raw(lmat, lmat, "NN", ps)
    n = int(math.log2(C))
    for s in range(1, n):
        p = p + _bdot_raw(p, lp, "NN", ps)
        if s < n - 1:
            lp = _bdot_raw(lp, lp, "NN", ps)
    return p


def _gated_norm(o, z, gnw):
    on = o * lax.rsqrt(jnp.mean(o * o, axis=-1, keepdims=True) + EPS) * gnw
    return on * _silu(z)


GDN_PG = 2
GDN_SG = 4


def _gdn_pairs(c, ba, gp, G):
    C, W, H = GDN_CHUNK, GDN_WIDTH, GDN_HEADS
    pairs = [(j, h) for j in range(G) for h in range(H)]
    cq, ck, cv = (jnp.stack([c[C * j:C * (j + 1), o + GDN_DIM * h:o + GDN_DIM * (h + 1)] for j, h in pairs]) for o in (0, W, 2 * W))
    b_col = jnp.stack([ba[C * j:C * (j + 1), h:h + 1] for j, h in pairs])
    a_col = jnp.stack([ba[C * j:C * (j + 1), H + h:H + h + 1] for j, h in pairs])
    alog = jnp.stack([gp[0:1, h:h + 1] for j, h in pairs])
    dtb = jnp.stack([gp[0:1, H + h:H + h + 1] for j, h in pairs])
    return pairs, (cq, ck, cv, b_col, a_col, alog, dtb)


def _gdn_pre_specs(S, G):
    C = GDN_CHUNK
    T = C * G
    return dict(
        cur=pl.BlockSpec((T, 3 * GDN_WIDTH), lambda i: (i, 0)),
        prev=pl.BlockSpec((8, 3 * GDN_WIDTH), lambda i: (jnp.maximum(i * (T // 8) - 1, 0), 0)),
        ba=pl.BlockSpec((T, 128), lambda i: (i, P_BA // 128)),
        cw=pl.BlockSpec((GDN_CONV, 3 * GDN_WIDTH), lambda i: (0, 0)),
        vec=pl.BlockSpec((1, 128), lambda i: (0, 0)),
        hd=pl.BlockSpec((GDN_HEADS, T, GDN_DIM), lambda i: (0, i, 0)),
        hc=pl.BlockSpec((GDN_HEADS, T, C), lambda i: (0, i, 0)),
        ge=pl.BlockSpec((G, GDN_HEADS, 8, 128), lambda i: (i, 0, 0, 0)),
    )


def _hd_shape(S, last=GDN_DIM):
    return jax.ShapeDtypeStruct((GDN_HEADS, S, last), F32)


def _gdn_pre(proj, conv_w, gp):
    S = proj.shape[0]
    C, G = GDN_CHUNK, GDN_PG
    nc = S // C
    sp = _gdn_pre_specs(S, G)

    def body(cur_ref, prev_ref, ba_ref, cw_ref, gp_ref, uv_ref, wk_ref, qd_ref, ke_ref, at_ref, ti_ref, ge_ref):
        prev = prev_ref[...] * jnp.where(pl.program_id(0) == 0, 0.0, 1.0)
        c = _silu(_conv_taps(jnp.concatenate([prev, cur_ref[...]], axis=0), cw_ref[...], GDN_CONV, C * G))
        pairs, args = _gdn_pairs(c, ba_ref[...], gp_ref[...], G)
        lmat, v, rk, q_dec, attn, k_end, g_end = _gdn_stage1(*args)
        t = _tri_inv(lmat)
        u_v = _bdot_raw(t, v, "NN", _GDN_PASSES["sol"])
        w_k = _bdot_raw(t, rk, "NN", _GDN_PASSES["sol"])
        for b, (j, h) in enumerate(pairs):
            rows = slice(C * j, C * (j + 1))
            uv_ref[h, rows, :] = u_v[b]
            wk_ref[h, rows, :] = w_k[b]
            qd_ref[h, rows, :] = q_dec[b]
            ke_ref[h, rows, :] = k_end[b]
            at_ref[h, rows, :] = attn[b]
            ti_ref[h, rows, :] = t[b]
            ge_ref[j, h] = jnp.broadcast_to(g_end[b], (8, 128))

    return pl.pallas_call(
        body, name="gdn_pre", grid=(nc // G,),
        in_specs=[sp["cur"], sp["prev"], sp["ba"], sp["cw"], sp["vec"]],
        out_specs=(sp["hd"], sp["hd"], sp["hd"], sp["hd"], sp["hc"], sp["hc"], sp["ge"]),
        out_shape=(_hd_shape(S), _hd_shape(S), _hd_shape(S), _hd_shape(S), _hd_shape(S, C), _hd_shape(S, C),
                   jax.ShapeDtypeStruct((nc, GDN_HEADS, 8, 128), F32)),
        compiler_params=_cparams(("parallel",)),
    )(proj, proj, proj, conv_w, gp)


def _gdn_scan_specs(S, G, rev):
    C = GDN_CHUNK
    T = C * G
    n = S // T
    ci = (lambda i: n - 1 - i) if rev else (lambda i: i)
    return dict(
        hd=pl.BlockSpec((GDN_HEADS, T, GDN_DIM), lambda i: (0, ci(i), 0)),
        hc=pl.BlockSpec((GDN_HEADS, T, C), lambda i: (0, ci(i), 0)),
        ge=pl.BlockSpec((G, GDN_HEADS, 8, 128), lambda i: (ci(i), 0, 0, 0)),
        z=pl.BlockSpec((T, GDN_WIDTH), lambda i: (ci(i), P_Z // GDN_WIDTH)),
        oa=pl.BlockSpec((T, GDN_WIDTH), lambda i: (ci(i), 0)),
        vec=pl.BlockSpec((1, 128), lambda i: (0, 0)),
        st=pl.BlockSpec((G, GDN_HEADS, GDN_DIM, GDN_DIM), lambda i: (ci(i), 0, 0, 0)),
    )


def _gdn_scan(u_v, w_k, q_dec, k_end, attn, g_end, proj, gnw):
    S = proj.shape[0]
    C, G = GDN_CHUNK, GDN_SG
    nc = S // C
    sp = _gdn_scan_specs(S, G, False)
    ps = _GDN_PASSES["scan"]

    def body(uv_ref, wk_ref, qd_ref, ke_ref, at_ref, ge_ref, z_ref, gnw_ref, oa_ref, st_ref, s_scr):
        @pl.when(pl.program_id(0) == 0)
        def _():
            s_scr[...] = jnp.zeros_like(s_scr)

        for j in range(G):
            rows = slice(C * j, C * (j + 1))
            st = s_scr[...]
            st_ref[j] = st
            u = uv_ref[:, rows, :] - _bdot_raw(wk_ref[:, rows, :], st, "NN", ps)
            o = _bdot_raw(qd_ref[:, rows, :], st, "NN", ps) + _bdot_raw(at_ref[:, rows, :], u, "NN", ps)
            s_scr[...] = ge_ref[j][:, 0:1, 0:1] * st + _bdot_raw(ke_ref[:, rows, :], u, "TN", ps)
            for h in range(GDN_HEADS):
                cols = slice(GDN_DIM * h, GDN_DIM * (h + 1))
                oa_ref[rows, cols] = _gated_norm(o[h], z_ref[rows, cols], gnw_ref[...])

    return pl.pallas_call(
        body, name="gdn_scan", grid=(nc // G,),
        in_specs=[sp["hd"], sp["hd"], sp["hd"], sp["hd"], sp["hc"], sp["ge"], sp["z"], sp["vec"]],
        out_specs=(sp["oa"], sp["st"]),
        out_shape=(jax.ShapeDtypeStruct((S, GDN_WIDTH + DIL_WIDTH), F32),
                   jax.ShapeDtypeStruct((nc, GDN_HEADS, GDN_DIM, GDN_DIM), F32)),
        scratch_shapes=[pltpu.VMEM((GDN_HEADS, GDN_DIM, GDN_DIM), F32)],
        compiler_params=_cparams(("arbitrary",)),
    )(u_v, w_k, q_dec, k_end, attn, g_end, proj, gnw)


def _gdn_scan_bwd(u_v, w_k, q_dec, k_end, attn, g_end, proj, gnw, states, d_oa):
    S = proj.shape[0]
    C, G = GDN_CHUNK, GDN_SG
    nc = S // C
    sp = _gdn_scan_specs(S, G, True)
    ps, pb = _GDN_PASSES["scan"], _GDN_PASSES["bwd"]

    def body(uv_ref, wk_ref, qd_ref, ke_ref, at_ref, ge_ref, z_ref, gnw_ref, st_ref, doa_ref,
             duv_ref, dwk_ref, dqd_ref, dke_ref, dat_ref, dge_ref, dz_ref, dgnw_ref, ds_scr):
        @pl.when(pl.program_id(0) == 0)
        def _():
            ds_scr[...] = jnp.zeros_like(ds_scr)
            dgnw_ref[...] = jnp.zeros_like(dgnw_ref)

        dgnw = jnp.zeros((1, 128), F32)
        for j in reversed(range(G)):
            rows = slice(C * j, C * (j + 1))
            st = st_ref[j]
            wk, qd, ke, at = wk_ref[:, rows, :], qd_ref[:, rows, :], ke_ref[:, rows, :], at_ref[:, rows, :]
            u = uv_ref[:, rows, :] - _bdot_raw(wk, st, "NN", ps)
            o = _bdot_raw(qd, st, "NN", ps) + _bdot_raw(at, u, "NN", ps)
            dos = []
            for h in range(GDN_HEADS):
                cols = slice(GDN_DIM * h, GDN_DIM * (h + 1))
                _, vjp2 = jax.vjp(_gated_norm, o[h], z_ref[rows, cols], gnw_ref[...])
                do_h, dz_h, dgn = vjp2(doa_ref[rows, cols])
                dz_ref[rows, cols] = dz_h
                dgnw = dgnw + dgn
                dos.append(do_h)
            do = jnp.stack(dos)
            ds_new = ds_scr[...]
            du = _bdot_raw(at, do, "TN", pb) + _bdot_raw(ke, ds_new, "NN", pb)
            duv_ref[:, rows, :] = du
            dat_ref[:, rows, :] = _bdot_raw(do, u, "NT", pb)
            dqd_ref[:, rows, :] = _bdot_raw(do, st, "NT", pb)
            dke_ref[:, rows, :] = _bdot_raw(u, ds_new, "NT", pb)
            dwk_ref[:, rows, :] = -_bdot_raw(du, st, "NT", pb)
            d_ge = jnp.sum(jnp.sum(st * ds_new, axis=2, keepdims=True), axis=1, keepdims=True)
            dge_ref[j] = jnp.broadcast_to(d_ge, (GDN_HEADS, 8, 128))
            ds_scr[...] = ge_ref[j][:, 0:1, 0:1] * ds_new + _bdot_raw(qd, do, "TN", pb) - _bdot_raw(wk, du, "TN", pb)
        dgnw_ref[...] += dgnw

    return pl.pallas_call(
        body, name="gdn_scan_bwd", grid=(nc // G,),
        in_specs=[sp["hd"], sp["hd"], sp["hd"], sp["hd"], sp["hc"], sp["ge"], sp["z"], sp["vec"], sp["st"], sp["oa"]],
        out_specs=(sp["hd"], sp["hd"], sp["hd"], sp["hd"], sp["hc"], sp["ge"], sp["oa"], sp["vec"]),
        out_shape=(_hd_shape(S), _hd_shape(S), _hd_shape(S), _hd_shape(S), _hd_shape(S, C),
                   jax.ShapeDtypeStruct((nc, GDN_HEADS, 8, 128), F32), jax.ShapeDtypeStruct((S, GDN_WIDTH), F32),
                   jax.ShapeDtypeStruct((1, 128), F32)),
        scratch_shapes=[pltpu.VMEM((GDN_HEADS, GDN_DIM, GDN_DIM), F32)],
        compiler_params=_cparams(("arbitrary",)),
    )(u_v, w_k, q_dec, k_end, attn, g_end, proj, gnw, states, d_oa)


def _gdn_post(proj, conv_w, gp, tinv, u_v, w_k, d_uv, d_wk, d_qd, d_ke, d_at, d_ge):
    S = proj.shape[0]
    C, G = GDN_CHUNK, GDN_PG
    nc = S // C
    sp = _gdn_pre_specs(S, G)
    pb = _GDN_PASSES["bwd"]

    def body(cur_ref, prev_ref, ba_ref, cw_ref, gp_ref, ti_ref, uv_ref, wk_ref, duv_ref, dwk_ref, dqd_ref, dke_ref,
             dat_ref, dge_ref, dpre_ref, dba_ref, dgp_ref):
        i = pl.program_id(0)

        @pl.when(i == 0)
        def _():
            dgp_ref[...] = jnp.zeros_like(dgp_ref)

        prev = prev_ref[...] * jnp.where(i == 0, 0.0, 1.0)
        pre = _conv_taps(jnp.concatenate([prev, cur_ref[...]], axis=0), cw_ref[...], GDN_CONV, C * G)
        sg = jax.nn.sigmoid(pre)
        dsilu = sg * (1.0 + pre * (1.0 - sg))
        pairs, args = _gdn_pairs(pre * sg, ba_ref[...], gp_ref[...], G)
        _, vjp1 = jax.vjp(functools.partial(_gdn_stage1, dot=_bdot), *args)

        def take(ref):
            return jnp.stack([ref[h, C * j:C * (j + 1), :] for j, h in pairs])

        t, u_v, w_k = take(ti_ref), take(uv_ref), take(wk_ref)
        d_v = _bdot_raw(t, take(duv_ref), "TN", pb)
        d_rk = _bdot_raw(t, take(dwk_ref), "TN", pb)
        d_l = -(_bdot_raw(d_v, u_v, "NT", pb) + _bdot_raw(d_rk, w_k, "NT", pb))
        d_ge = jnp.stack([dge_ref[j, h][0:1, 0:1] for j, h in pairs])
        dcq, dck, dcv, db, da, dalog, ddtb = vjp1((d_l, d_v, d_rk, take(dqd_ref), take(dat_ref), take(dke_ref), d_ge))
        lane = lax.broadcasted_iota(jnp.int32, (C, 128), 1)
        lane1 = lax.broadcasted_iota(jnp.int32, (1, 128), 1)
        dgp = jnp.zeros((1, 128), F32)
        for j in range(G):
            rows = slice(C * j, C * (j + 1))
            dba = jnp.zeros((C, 128), F32)
            for h in range(GDN_HEADS):
                b = GDN_HEADS * j + h
                for o_, dcx in ((0, dcq), (GDN_WIDTH, dck), (2 * GDN_WIDTH, dcv)):
                    cols = slice(o_ + GDN_DIM * h, o_ + GDN_DIM * (h + 1))
                    dpre_ref[rows, cols] = dcx[b] * dsilu[rows, cols]
                dba = dba + jnp.where(lane == h, db[b], 0.0) + jnp.where(lane == GDN_HEADS + h, da[b], 0.0)
                dgp = dgp + jnp.where(lane1 == h, dalog[b], 0.0) + jnp.where(lane1 == GDN_HEADS + h, ddtb[b], 0.0)
            dba_ref[rows, :] = dba
        dgp_ref[0:1, :] += dgp

    T = C * G
    return pl.pallas_call(
        body, name="gdn_post", grid=(nc // G,),
        in_specs=[sp["cur"], sp["prev"], sp["ba"], sp["cw"], sp["vec"], sp["hc"], sp["hd"], sp["hd"], sp["hd"], sp["hd"],
                  sp["hd"], sp["hd"], sp["hc"], sp["ge"]],
        out_specs=(sp["cur"], pl.BlockSpec((T, 128), lambda i: (i, 0)), pl.BlockSpec((8, 128), lambda i: (0, 0))),
        out_shape=(jax.ShapeDtypeStruct((S, 3 * GDN_WIDTH), F32), jax.ShapeDtypeStruct((S, 128), F32),
                   jax.ShapeDtypeStruct((8, 128), F32)),
        compiler_params=_cparams(("arbitrary",)),
    )(proj, proj, proj, conv_w, gp, tinv, u_v, w_k, d_uv, d_wk, d_qd, d_ke, d_at, d_ge)


def _conv_bwd(dpre, x, xcol0, w, K, name, tc):
    S, Cc = dpre.shape
    T = _pick_tile(S, 256)
    nt, ncol = S // T, Cc // tc
    xo = xcol0 // tc

    def body(d_ref, dn_ref, x_ref, xp_ref, w_ref, dx_ref, dw_ref):
        i = pl.program_id(1)
        dn = dn_ref[...] * jnp.where(i == nt - 1, 0.0, 1.0)
        dv = d_ref[...]
        ext_d = jnp.concatenate([dv, dn], axis=0)
        wv = w_ref[...]
        dx_ref[...] = _conv_taps_t(ext_d, wv, K, T).astype(dx_ref.dtype)
        xp = xp_ref[...] * jnp.where(i == 0, 0.0, 1.0)
        ext_x = jnp.concatenate([xp, x_ref[...]], axis=0)

        @pl.when(i == 0)
        def _():
            dw_ref[...] = jnp.zeros_like(dw_ref)

        for k in range(K):
            lo = 8 - (K - 1) + k
            dw_ref[k:k + 1, :] += jnp.sum(dv * ext_x[lo:lo + T, :], axis=0, keepdims=True)

    r8 = T // 8
    return pl.pallas_call(
        body, name=name, grid=(ncol, nt),
        in_specs=[pl.BlockSpec((T, tc), lambda j, i: (i, j)),
                  pl.BlockSpec((8, tc), lambda j, i: (jnp.minimum((i + 1) * r8, S // 8 - 1), j)),
                  pl.BlockSpec((T, tc), lambda j, i: (i, j + xo)),
                  pl.BlockSpec((8, tc), lambda j, i: (jnp.maximum(i * r8 - 1, 0), j + xo)),
                  pl.BlockSpec((K, tc), lambda j, i: (0, j))],
        out_specs=(pl.BlockSpec((T, tc), lambda j, i: (i, j)), pl.BlockSpec((K, tc), lambda j, i: (0, j))),
        out_shape=(jax.ShapeDtypeStruct((S, Cc), _MXU), jax.ShapeDtypeStruct((K, Cc), F32)),
        compiler_params=_cparams(("parallel", "arbitrary")),
    )(dpre, dpre, x, x, w)


def _dil_bias(nt, T):
    d = (np.arange(nt)[:, None, None] * T + np.arange(T)[None, None, :] - np.arange(T)[None, :, None])
    cnt = ((d >= 0) & (d <= 128)).astype(np.float64) + ((d >= 0) & (d % 4 == 0) & (d <= 512)) + ((d >= 0) & (d % 16 == 0))
    return jnp.asarray(np.where(cnt > 0, np.log(np.maximum(cnt, 1.0)), -1e30), dtype=F32)


def _attn_fwd(proj, mix):
    S = proj.shape[0]
    T = min(ATT_T, S)
    nt = S // T
    bias = _dil_bias(nt, T)
    scale = DIL_DIM ** -0.5
    npair = DIL_WIDTH // 128
    qb0, kb0, vb0 = P_QKVB // 128, (P_QKVB + DIL_WIDTH) // 128, (P_QKVB + 2 * DIL_WIDTH) // 128

    def body(q_ref, k_ref, v_ref, b_ref, mix_ref, o_ref, lse_ref):
        i = pl.program_id(1)
        qs = (q_ref[...] * scale).astype(_MXU)

        def step(j, carry):
            kt = k_ref[pl.ds(pl.multiple_of(j * T, T), T), :].astype(_MXU)
            vt = v_ref[pl.ds(pl.multiple_of(j * T, T), T), :].astype(_MXU)
            bt = b_ref[i - j]
            out = []
            for hh in range(2):
                m, l, acc = carry[hh]
                sl = slice(hh * DIL_DIM, (hh + 1) * DIL_DIM)
                s = lax.dot_general(kt[:, sl], qs[:, sl], (_NT, ((), ())), preferred_element_type=F32) + bt
                m_new = jnp.maximum(m, jnp.max(s, axis=0, keepdims=True))
                p = jnp.exp(s - m_new)
                a = jnp.exp(m - m_new)
                l = a * l + jnp.sum(p, axis=0, keepdims=True)
                acc = a * acc + lax.dot_general(vt[:, sl], p.astype(_MXU), (_TN, ((), ())), preferred_element_type=F32)
                out.append((m_new, l, acc))
            return tuple(out)

        init = tuple((jnp.full((1, T), -1e30, F32), jnp.zeros((1, T), F32), jnp.zeros((DIL_DIM, T), F32)) for _ in range(2))
        res = lax.fori_loop(0, i + 1, step, init)
        lse_ref[...] = jnp.zeros_like(lse_ref)
        for hh in range(2):
            m, l, acc = res[hh]
            o_ref[:, hh * DIL_DIM:(hh + 1) * DIL_DIM] = (acc / l).T
            lse_ref[hh:hh + 1, :] = m + jnp.log(l)

    return pl.pallas_call(
        body, name="attn_fwd", grid=(npair, nt),
        in_specs=[pl.BlockSpec((T, 128), lambda p, i: (i, qb0 + p)),
                  pl.BlockSpec((S, 128), lambda p, i: (0, kb0 + p)),
                  pl.BlockSpec((S, 128), lambda p, i: (0, vb0 + p)),
                  pl.BlockSpec((nt, T, T), lambda p, i: (0, 0, 0)), pl.BlockSpec(memory_space=pl.ANY)],
        out_specs=(pl.BlockSpec((T, 128), lambda p, i: (i, GDN_WIDTH // 128 + p)),
                   pl.BlockSpec((None, None, 8, T), lambda p, i: (p, i, 0, 0))),
        out_shape=(jax.ShapeDtypeStruct(mix.shape, F32), jax.ShapeDtypeStruct((npair, nt, 8, T), F32)),
        input_output_aliases={4: 0},
        compiler_params=_cparams(("parallel", "parallel")),
    )(proj, proj, proj, bias, mix)


def _attn_bwd(proj, mix, lse, d_mix):
    S = proj.shape[0]
    T = min(ATT_T, S)
    nt = S // T
    bias = _dil_bias(nt, T)
    scale = DIL_DIM ** -0.5
    npair = DIL_WIDTH // 128
    qb0, kb0, vb0 = P_QKVB // 128, (P_QKVB + DIL_WIDTH) // 128, (P_QKVB + 2 * DIL_WIDTH) // 128

    def body(q_ref, k_ref, v_ref, o_ref, lse_ref, do_ref, b_ref, dq_ref, dk_ref, dv_ref, dq_scr):
        j = pl.program_id(1)

        @pl.when(j == 0)
        def _():
            dq_scr[...] = jnp.zeros_like(dq_scr)

        kt = k_ref[...].astype(_MXU)
        vt = v_ref[...].astype(_MXU)
        ones = jnp.ones((8, DIL_DIM), F32)

        def step(i, carry):
            rows = pl.ds(pl.multiple_of(i * T, T), T)
            qs = (q_ref[rows, :] * scale).astype(_MXU)
            dov = do_ref[rows, :]
            prod = dov * o_ref[rows, :]
            lsev = lse_ref[i]
            dob = dov.astype(_MXU)
            bt = b_ref[i - j]
            out = []
            dqs = []
            for hh in range(2):
                dk, dv = carry[hh]
                sl = slice(hh * DIL_DIM, (hh + 1) * DIL_DIM)
                s = lax.dot_general(kt[:, sl], qs[:, sl], (_NT, ((), ())), preferred_element_type=F32) + bt
                p = jnp.exp(s - lsev[hh:hh + 1, :])
                delta = lax.dot_general(ones, prod[:, sl], (_NT, ((), ())), precision=_HI, preferred_element_type=F32)[0:1, :]
                dp = lax.dot_general(vt[:, sl], dob[:, sl], (_NT, ((), ())), preferred_element_type=F32)
                ds = (p * (dp - delta)).astype(_MXU)
                dv = dv + lax.dot_general(p.astype(_MXU), dob[:, sl], (_NN, ((), ())), preferred_element_type=F32)
                dk = dk + lax.dot_general(ds, qs[:, sl], (_NN, ((), ())), preferred_element_type=F32)
                dqs.append(lax.dot_general(ds, kt[:, sl], (_TN, ((), ())), preferred_element_type=F32) * scale)
                out.append((dk, dv))
            dq_scr[rows, :] += jnp.concatenate(dqs, axis=1)
            return tuple(out)

        init = tuple((jnp.zeros((T, DIL_DIM), F32), jnp.zeros((T, DIL_DIM), F32)) for _ in range(2))
        res = lax.fori_loop(j, nt, step, init)
        dk_ref[...] = jnp.concatenate([res[0][0], res[1][0]], axis=1).astype(dk_ref.dtype)
        dv_ref[...] = jnp.concatenate([res[0][1], res[1][1]], axis=1).astype(dv_ref.dtype)

        @pl.when(j == nt - 1)
        def _():
            dq_ref[...] = dq_scr[...].astype(dq_ref.dtype)

    full = lambda c0: pl.BlockSpec((S, 128), lambda p, j: (0, c0 + p))
    tile = lambda c0: pl.BlockSpec((T, 128), lambda p, j: (j, c0 + p))
    out3 = jax.ShapeDtypeStruct((S, DIL_WIDTH), _MXU)
    return pl.pallas_call(
        body, name="attn_bwd", grid=(npair, nt),
        in_specs=[full(qb0), tile(kb0), tile(vb0), full(GDN_WIDTH // 128),
                  pl.BlockSpec((None, nt, 8, T), lambda p, j: (p, 0, 0, 0)), full(GDN_WIDTH // 128),
                  pl.BlockSpec((nt, T, T), lambda p, j: (0, 0, 0))],
        out_specs=(full(0), tile(0), tile(0)),
        out_shape=(out3, out3, out3),
        scratch_shapes=[pltpu.VMEM((S, 128), F32)],
        compiler_params=_cparams(("parallel", "arbitrary")),
    )(proj, proj, proj, mix, lse, d_mix, bias)


def _ffn_act(up, cw):
    S, Cc = up.shape[0], up.shape[1] // 2
    T, tc = _pick_tile(S, 256), _pick_tile(Cc, 1536)
    r8 = T // 8
    nct = Cc // tc

    def body(g_ref, gp_ref, u_ref, up_ref, wg_ref, wu_ref, o_ref):
        keep = jnp.where(pl.program_id(1) == 0, 0.0, 1.0)
        cg = _conv_taps(jnp.concatenate([gp_ref[...] * keep, g_ref[...]], axis=0), wg_ref[...], FFN_CONV, T)
        cu = _conv_taps(jnp.concatenate([up_ref[...] * keep, u_ref[...]], axis=0), wu_ref[...], FFN_CONV, T)
        o_ref[...] = (_silu(cg) * cu).astype(o_ref.dtype)

    cur = lambda o: pl.BlockSpec((T, tc), lambda j, i: (i, j + o))
    prev = lambda o: pl.BlockSpec((8, tc), lambda j, i: (jnp.maximum(i * r8 - 1, 0), j + o))
    wsp = lambda o: pl.BlockSpec((FFN_CONV, tc), lambda j, i: (0, j + o))
    return pl.pallas_call(
        body, name="ffn_act", grid=(nct, S // T),
        in_specs=[cur(0), prev(0), cur(nct), prev(nct), wsp(0), wsp(nct)], out_specs=cur(0),
        out_shape=jax.ShapeDtypeStruct((S, Cc), _MXU),
        compiler_params=_cparams(("parallel", "parallel")),
    )(up, up, up, up, cw, cw)


def _ffn_act_bwd(d_act, up, cw):
    S, Cc = up.shape[0], up.shape[1] // 2
    T, tc = _pick_tile(S, 256), _pick_tile(Cc, 1536)
    r8 = T // 8
    nt = S // T
    nct = Cc // tc
    K = FFN_CONV

    def body(da_ref, dan_ref, g_ref, gp_ref, gn_ref, u_ref, up_ref, un_ref, wg_ref, wu_ref,
             dg_ref, du_ref, dwg_ref, dwu_ref):
        i = pl.program_id(1)
        keep_p = jnp.where(i == 0, 0.0, 1.0)
        keep_n = jnp.where(i == nt - 1, 0.0, 1.0)
        wg, wu = wg_ref[...], wu_ref[...]
        xg = jnp.concatenate([gp_ref[...] * keep_p, g_ref[...], gn_ref[...] * keep_n], axis=0)
        xu = jnp.concatenate([up_ref[...] * keep_p, u_ref[...], un_ref[...] * keep_n], axis=0)
        cg = _conv_taps(xg, wg, K, T + 8)
        cu = _conv_taps(xu, wu, K, T + 8)
        da = jnp.concatenate([da_ref[...], dan_ref[...] * keep_n], axis=0)
        sg = jax.nn.sigmoid(cg)
        d_cg = da * cu * (sg * (1.0 + cg * (1.0 - sg)))
        d_cu = da * (cg * sg)
        dg_ref[...] = _conv_taps_t(d_cg, wg, K, T).astype(dg_ref.dtype)
        du_ref[...] = _conv_taps_t(d_cu, wu, K, T).astype(du_ref.dtype)

        @pl.when(i == 0)
        def _():
            dwg_ref[...] = jnp.zeros_like(dwg_ref)
            dwu_ref[...] = jnp.zeros_like(dwu_ref)

        for k in range(K):
            lo = 8 - (K - 1) + k
            dwg_ref[k:k + 1, :] += jnp.sum(d_cg[0:T, :] * xg[lo:lo + T, :], axis=0, keepdims=True)
            dwu_ref[k:k + 1, :] += jnp.sum(d_cu[0:T, :] * xu[lo:lo + T, :], axis=0, keepdims=True)

    cur = lambda o: pl.BlockSpec((T, tc), lambda j, i: (i, j + o))
    prev = lambda o: pl.BlockSpec((8, tc), lambda j, i: (jnp.maximum(i * r8 - 1, 0), j + o))
    nxt = lambda o: pl.BlockSpec((8, tc), lambda j, i: (jnp.minimum((i + 1) * r8, S // 8 - 1), j + o))
    wsp = lambda o: pl.BlockSpec((K, tc), lambda j, i: (0, j + o))
    return pl.pallas_call(
        body, name="ffn_act_bwd", grid=(nct, nt),
        in_specs=[cur(0), nxt(0), cur(0), prev(0), nxt(0), cur(nct), prev(nct), nxt(nct), wsp(0), wsp(nct)],
        out_specs=(cur(0), cur(0), wsp(0), wsp(0)),
        out_shape=(jax.ShapeDtypeStruct((S, Cc), _MXU), jax.ShapeDtypeStruct((S, Cc), _MXU),
                   jax.ShapeDtypeStruct((K, Cc), F32), jax.ShapeDtypeStruct((K, Cc), F32)),
        compiler_params=_cparams(("parallel", "arbitrary")),
    )(d_act, d_act, up, up, up, up, up, up, cw, cw)


def _local_step(x, tgt, n1w, n2w, fnw, gp, gnw, wp, conv_w, fcw, rest_weights, early_grads):
    h1 = _rmsnorm_fwd(x, n1w, "norm1")
    proj = _mm(h1, wp, "nn", name="proj")
    u_v, w_k, q_dec, k_end, attn, tinv, g_end = _gdn_pre(proj, conv_w, gp)
    mix, states = _gdn_scan(u_v, w_k, q_dec, k_end, attn, g_end, proj, gnw)
    mix, lse = _attn_fwd(proj, mix)
    w_out, w_up4, w_down = rest_weights([mix])
    x2 = _mm(mix, w_out, "nn", residual=x, name="outproj")
    h2 = _rmsnorm_fwd(x2, n2w, "norm2")
    up = _mm(h2, w_up4, "nn", b_blocks=True, name="up")
    act = _ffn_act(up, fcw)
    x3 = _mm(act, w_down, "nn", residual=x2, name="down")
    loss, dx3, d_fnw = _loss_head(x3, fnw, tgt, "loss_head")
    d_act = _mm(dx3, w_down, "nt", name="d_act")
    d_wdown = _mm(act, dx3, "tn", name="d_wdown")
    d_upg, d_upu, d_fcwg, d_fcwu = _ffn_act_bwd(d_act, up, fcw)
    d_wup = _mm(h2, d_upg, "tn", place=("blocks", N_CHIPS, 0), tn=w_up4.shape[2], name="d_wgate")
    d_wup = _mm(h2, d_upu, "tn", place=("blocks", N_CHIPS, N_CHIPS // 2), tn=w_up4.shape[2], into=d_wup, name="d_wup")
    token = early_grads[0](d_wup, d_wdown)
    d_h2 = _mm_nt_blocks([d_upg, d_upu], w_up4, "d_h2")
    dx2, d_n2w = _rmsnorm_bwd(d_h2, x2, n2w + token[0:1, 0:1], dx3, "norm2_bwd")
    token = early_grads[1](dx2)
    d_mix = _mm(dx2, w_out, "nt", name="d_mix")
    d_wout = _mm(mix, dx2, "tn", name="d_wout")
    dq_b, dk_b, dv_b = _attn_bwd(proj, mix, lse, d_mix)
    d_uv, d_wk, d_qd, d_ke, d_at, d_ge, d_z, d_gnw = _gdn_scan_bwd(u_v, w_k, q_dec, k_end, attn, g_end, proj,
                                                                   gnw + token[0:1, 0:1], states, d_mix)
    d_pre, d_ba, d_gp = _gdn_post(proj, conv_w, gp, tinv, u_v, w_k, d_uv, d_wk, d_qd, d_ke, d_at, d_ge)
    d_qkva, d_convw = _conv_bwd(d_pre, proj, 0, conv_w, GDN_CONV, "gdn_conv_bwd", 512)
    d_proj = jnp.concatenate([d_qkva, d_z.astype(_MXU), dq_b, dk_b, dv_b, d_ba.astype(_MXU),
                              jnp.zeros((x.shape[0], P_COLS - P_BA - 128), _MXU)], axis=1)
    d_wp = _mm(h1, d_proj, "tn", name="d_wp")
    token = early_grads[2](d_wp, d_wout)
    d_h1 = _mm(d_proj, wp, "nt", name="d_h1")
    dx, d_n1w = _rmsnorm_bwd(d_h1, x, n1w + token[0:1, 0:1], dx2, "norm1_bwd")
    grads = dict(wp=d_wp, conv_w=d_convw, w_out=d_wout, w_up=d_wup, fcw_g=d_fcwg, fcw_u=d_fcwu, w_down=d_wdown,
                 n1w=d_n1w, n2w=d_n2w, fnw=d_fnw, gp=d_gp, gnw=d_gnw)
    return loss, dx, grads


_HBM = pl.BlockSpec(memory_space=pltpu.HBM)


def _pos():
    return lax.axis_index("x"), lax.axis_index("y"), lax.axis_index("c")


def _other_chips(x, y):
    return [(1 - x, y), (x, 1 - y), (1 - x, 1 - y)]


def _halvable(shape):
    return shape[0] % 32 == 0


def _rows_of_half(shape, half):
    if not _halvable(shape):
        return pl.ds(0, shape[0])
    return pl.ds(pl.multiple_of(half * (shape[0] // 2), 16), shape[0] // 2)


def _gather_halves(shards, name):
    n = len(shards)
    shapes = [s.shape for s in shards]

    def body(*refs):
        ins, outs = refs[:n], refs[n:2 * n]
        send_sems, recv_sems = refs[2 * n:]
        x, y, c = _pos()
        q = 2 * x + y
        chips = _other_chips(x, y)

        def copy(a, j, block):
            px, py = chips[j]
            rows = _rows_of_half(shapes[a], c)
            return pltpu.make_async_remote_copy(
                src_ref=ins[a].at[rows, :], dst_ref=outs[a].at[block, rows, :], send_sem=send_sems.at[3 * a + j],
                recv_sem=recv_sems.at[3 * a + j], device_id=(px, py, c), device_id_type=MESH)

        sends = [copy(a, j, q) for a in range(n) for j in range(3)]
        for cp in sends:
            cp.start()
        for a in range(n):
            for j, (px, py) in enumerate(chips):
                copy(a, j, 2 * px + py).wait_recv()
        for cp in sends:
            cp.wait_send()

    return pl.pallas_call(
        body, name=name, in_specs=[_HBM] * n, out_specs=[_HBM] * n,
        out_shape=[jax.ShapeDtypeStruct((N_CHIPS,) + s.shape, s.dtype) for s in shards],
        scratch_shapes=[pltpu.SemaphoreType.DMA((3 * n,)), pltpu.SemaphoreType.DMA((3 * n,))],
    )(*shards)


_SEM = pl.BlockSpec(memory_space=pltpu.SEMAPHORE)
_ANY = pl.BlockSpec(memory_space=pl.ANY)
_DATAFLOW = pltpu.SideEffectType.DATAFLOW_SIDE_EFFECTING


def _in_hbm(a):
    return pltpu.with_memory_space_constraint(a, pltpu.HBM)


def _halves_copy(src_refs, land_refs, send_sems, recv_sems, shapes, a, j, block, x, y, c):
    px, py = _other_chips(x, y)[j]
    rows = _rows_of_half(shapes[a], c)
    return pltpu.make_async_remote_copy(
        src_ref=src_refs[a].at[rows, :], dst_ref=land_refs[a].at[block, rows, :], send_sem=send_sems.at[3 * a + j],
        recv_sem=recv_sems.at[3 * a + j], device_id=(px, py, c), device_id_type=MESH)


def _gather_halves_start(shards, after, name):
    n = len(shards)
    shapes = [s.shape for s in shards]

    def body(*refs):
        ins, lands = refs[:n], refs[n:2 * n]
        send_sems, recv_sems = refs[2 * n + 1], refs[2 * n + 2]
        token = refs[-1]
        x, y, c = _pos()
        q = 2 * x + y
        for a in range(n):
            for j in range(3):
                _halves_copy(ins, lands, send_sems, recv_sems, shapes, a, j, q, x, y, c).start()
        token[...] = jnp.zeros_like(token)

    land_shapes = [(N_CHIPS,) + s.shape for s in shards]
    return pl.pallas_call(
        body, name=name,
        out_shape=(pltpu.SemaphoreType.DMA((3 * n,)), pltpu.SemaphoreType.DMA((3 * n,)),
                   *[pltpu.HBM(s.shape, s.dtype) for s in shards],
                   *[pltpu.HBM(ls, s.dtype) for ls, s in zip(land_shapes, shards)],
                   jax.ShapeDtypeStruct((8, 128), F32)),
        in_specs=[_HBM] * (2 * n) + [_ANY],
        out_specs=(_SEM, _SEM, *[_HBM] * (2 * n), pl.BlockSpec(memory_space=pltpu.VMEM)),
        input_output_aliases={a: 2 + a for a in range(2 * n)},
        compiler_params=pltpu.CompilerParams(has_side_effects=_DATAFLOW),
    )(*[_in_hbm(s) for s in shards], *[_in_hbm(lax.empty(ls, s.dtype)) for ls, s in zip(land_shapes, shards)], after)


def _gather_halves_wait(started, after, name):
    send_sems, recv_sems, *thru = started
    n = len(thru) // 2
    shapes = [t.shape for t in thru[:n]]

    def body(*refs):
        ins, lands = refs[:n], refs[n:2 * n]
        send_sems, recv_sems = refs[2 * n], refs[2 * n + 1]
        x, y, c = _pos()
        q = 2 * x + y
        chips = _other_chips(x, y)
        for a in range(n):
            for j, (px, py) in enumerate(chips):
                _halves_copy(ins, lands, send_sems, recv_sems, shapes, a, j, q, x, y, c).wait_send()
                _halves_copy(ins, lands, send_sems, recv_sems, shapes, a, j, 2 * px + py, x, y, c).wait_recv()

    outs = pl.pallas_call(
        body, name=name, out_shape=[pltpu.HBM(t.shape, t.dtype) for t in thru],
        in_specs=[_HBM] * (2 * n) + [_SEM, _SEM] + [_ANY] * len(after), out_specs=[_HBM] * (2 * n),
        input_output_aliases={a: a for a in range(2 * n)},
        compiler_params=pltpu.CompilerParams(has_side_effects=_DATAFLOW),
    )(*thru, send_sems, recv_sems, *after)
    return outs[:n], outs[n:]


def _sibling_fill(gathered, name):
    big = [a for a, g in enumerate(gathered) if _halvable(g.shape[1:])]
    n = len(gathered)

    def body(*refs):
        ins, outs = refs[:n], refs[n:2 * n]
        send_sems, recv_sems = refs[2 * n:]
        x, y, c = _pos()
        chips = _other_chips(x, y)

        def copy(k, j, half):
            a = big[k]
            px, py = chips[j]
            rows = _rows_of_half(gathered[a].shape[1:], half)
            return pltpu.make_async_remote_copy(
                src_ref=ins[a].at[2 * px + py, rows, :], dst_ref=outs[a].at[2 * px + py, rows, :],
                send_sem=send_sems.at[3 * k + j], recv_sem=recv_sems.at[3 * k + j],
                device_id=(x, y, 1 - c), device_id_type=MESH)

        sends = [copy(k, j, c) for k in range(len(big)) for j in range(3)]
        for cp in sends:
            cp.start()
        for k in range(len(big)):
            for j in range(3):
                copy(k, j, 1 - c).wait_recv()
        for cp in sends:
            cp.wait_send()

    return pl.pallas_call(
        body, name=name, in_specs=[_HBM] * n, out_specs=[_HBM] * n,
        out_shape=[jax.ShapeDtypeStruct(g.shape, g.dtype) for g in gathered],
        input_output_aliases={a: a for a in range(n)},
        scratch_shapes=[pltpu.SemaphoreType.DMA((3 * len(big),)), pltpu.SemaphoreType.DMA((3 * len(big),))],
    )(*gathered)


def _place_own(shards, gathered, cq, name):
    n = len(shards)
    steps = 4

    def body(cq_ref, *refs):
        for a in range(n):
            refs[2 * n + a][...] = refs[a][...]

    def tile(shape):
        return shape[0] // steps if _halvable(shape) else shape[0]

    in_specs = [pl.BlockSpec((tile(s.shape), s.shape[1]), (lambda i, s_: (i, 0)) if _halvable(s.shape) else (lambda i, s_: (0, 0)))
                for s in shards]
    in_specs += [pl.BlockSpec(memory_space=pl.ANY)] * n
    out_specs = [pl.BlockSpec((None, tile(s.shape), s.shape[1]),
                              (lambda i, s_: (s_[1], i, 0)) if _halvable(s.shape) else (lambda i, s_: (s_[1], 0, 0)))
                 for s in shards]
    gs = pltpu.PrefetchScalarGridSpec(num_scalar_prefetch=1, grid=(steps,), in_specs=in_specs, out_specs=out_specs)
    return pl.pallas_call(
        body, name=name, grid_spec=gs, out_shape=[jax.ShapeDtypeStruct(g.shape, g.dtype) for g in gathered],
        input_output_aliases={1 + n + a: a for a in range(n)},
        compiler_params=_cparams(("arbitrary",)),
    )(cq, *shards, *gathered)


def _half_rows(ref, c, rh):
    return ref.at[:, pl.ds(pl.multiple_of(c * rh, 8), rh), :]


def _grad_sibling(fams, small, name):
    n = len(fams)
    ns = 0 if small is None else 1
    rhs = [f.shape[1] // 2 for f in fams]

    def body(*refs):
        ins, outs = refs[:n], refs[n + ns:2 * n + ns]
        send_sems, recv_sems = refs[2 * (n + ns)], refs[2 * (n + ns) + 1]
        x, y, c = _pos()
        bigs = [pltpu.make_async_remote_copy(src_ref=_half_rows(ins[a], 1 - c, rhs[a]), dst_ref=outs[a],
                                             send_sem=send_sems.at[7 * ns + a], recv_sem=recv_sems.at[7 * ns + a],
                                             device_id=(x, y, 1 - c), device_id_type=MESH) for a in range(n)]
        for cp in bigs:
            cp.start()
        sends = []
        if ns:
            small_ref, all_ref, loc_sem = refs[n], refs[2 * n + 1], refs[2 * n + 4]
            me = 4 * x + 2 * y + c
            mine = pltpu.make_async_copy(small_ref, all_ref.at[me], loc_sem)
            mine.start()

            def peer(r):
                dx, dy, dc = (r >> 2) & 1, (r >> 1) & 1, r & 1
                return (x if dx == 0 else 1 - x), (y if dy == 0 else 1 - y), (c if dc == 0 else 1 - c)

            def small_copy(r, slot):
                return pltpu.make_async_remote_copy(src_ref=small_ref, dst_ref=all_ref.at[slot], send_sem=send_sems.at[r - 1],
                                                    recv_sem=recv_sems.at[r - 1], device_id=peer(r), device_id_type=MESH)

            sends = [small_copy(r, me) for r in range(1, 8)]
            for cp in sends:
                cp.start()
            for r in range(1, 8):
                px, py, pc = peer(r)
                small_copy(r, 4 * px + 2 * py + pc).wait_recv()
        for cp in bigs:
            cp.wait_recv()
        for cp in bigs + sends:
            cp.wait_send()
        if ns:
            mine.wait()

    return pl.pallas_call(
        body, name=name, in_specs=[_HBM] * (n + ns), out_specs=[_HBM] * (n + ns),
        out_shape=[jax.ShapeDtypeStruct((f.shape[0], f.shape[1] // 2, f.shape[2]), f.dtype) for f in fams]
        + ([jax.ShapeDtypeStruct((8,) + small.shape, small.dtype)] if ns else []),
        scratch_shapes=[pltpu.SemaphoreType.DMA((7 * ns + n,)), pltpu.SemaphoreType.DMA((7 * ns + n,))]
        + ([pltpu.SemaphoreType.DMA] if ns else []),
    )(*fams, *([small] if ns else []))


def _chips_copy(src_refs, land_refs, send_sems, recv_sems, a, j, x, y, c):
    px, py = _other_chips(x, y)[j]
    return pltpu.make_async_remote_copy(src_ref=src_refs[a].at[2 * px + py], dst_ref=land_refs[a].at[j],
                                        send_sem=send_sems.at[3 * a + j], recv_sem=recv_sems.at[3 * a + j],
                                        device_id=(px, py, c), device_id_type=MESH)


def _grad_chips_start(parts, name):
    n = len(parts)

    def body(*refs):
        ins, lands = refs[:n], refs[n:2 * n]
        send_sems, recv_sems = refs[2 * n], refs[2 * n + 1]
        token = refs[-1]
        x, y, c = _pos()
        for a in range(n):
            for j in range(3):
                _chips_copy(ins, lands, send_sems, recv_sems, a, j, x, y, c).start()
        token[...] = jnp.zeros_like(token)

    land_shapes = [(3,) + p.shape[1:] for p in parts]
    return pl.pallas_call(
        body, name=name,
        out_shape=(pltpu.SemaphoreType.DMA((3 * n,)), pltpu.SemaphoreType.DMA((3 * n,)),
                   *[pltpu.HBM(p.shape, p.dtype) for p in parts],
                   *[pltpu.HBM(ls, p.dtype) for ls, p in zip(land_shapes, parts)],
                   jax.ShapeDtypeStruct((8, 128), F32)),
        in_specs=[_HBM] * (2 * n),
        out_specs=(_SEM, _SEM, *[_HBM] * (2 * n), pl.BlockSpec(memory_space=pltpu.VMEM)),
        input_output_aliases={a: 2 + a for a in range(2 * n)},
        compiler_params=pltpu.CompilerParams(has_side_effects=_DATAFLOW),
    )(*[_in_hbm(p) for p in parts], *[_in_hbm(lax.empty(ls, p.dtype)) for ls, p in zip(land_shapes, parts)])


def _grad_chips_wait(started, after, name):
    send_sems, recv_sems, *thru = started
    n = len(thru) // 2

    def body(*refs):
        ins, lands = refs[:n], refs[n:2 * n]
        send_sems, recv_sems = refs[2 * n], refs[2 * n + 1]
        x, y, c = _pos()
        for a in range(n):
            for j in range(3):
                cp = _chips_copy(ins, lands, send_sems, recv_sems, a, j, x, y, c)
                cp.wait_send()
                cp.wait_recv()

    outs = pl.pallas_call(
        body, name=name, out_shape=[pltpu.HBM(t.shape, t.dtype) for t in thru],
        in_specs=[_HBM] * (2 * n) + [_SEM, _SEM] + [_ANY] * len(after), out_specs=[_HBM] * (2 * n),
        input_output_aliases={a: a for a in range(2 * n)},
        compiler_params=pltpu.CompilerParams(has_side_effects=_DATAFLOW),
    )(*thru, send_sems, recv_sems, *after)
    return outs[n:]


def _sibling_copy(src_refs, land_refs, send_sems, recv_sems, rhs, a, c, x, y):
    return pltpu.make_async_remote_copy(src_ref=_half_rows(src_refs[a], 1 - c, rhs[a]), dst_ref=land_refs[a],
                                        send_sem=send_sems.at[a], recv_sem=recv_sems.at[a],
                                        device_id=(x, y, 1 - c), device_id_type=MESH)


def _grad_sibling_start(fams, name):
    n = len(fams)
    rhs = [f.shape[1] // 2 for f in fams]

    def body(*refs):
        ins, lands = refs[:n], refs[n:2 * n]
        send_sems, recv_sems = refs[2 * n], refs[2 * n + 1]
        token = refs[-1]
        x, y, c = _pos()
        for a in range(n):
            _sibling_copy(ins, lands, send_sems, recv_sems, rhs, a, c, x, y).start()
        token[...] = jnp.zeros_like(token)

    land_shapes = [(f.shape[0], f.shape[1] // 2, f.shape[2]) for f in fams]
    return pl.pallas_call(
        body, name=name,
        out_shape=(pltpu.SemaphoreType.DMA((n,)), pltpu.SemaphoreType.DMA((n,)),
                   *[pltpu.HBM(f.shape, f.dtype) for f in fams],
                   *[pltpu.HBM(ls, f.dtype) for ls, f in zip(land_shapes, fams)],
                   jax.ShapeDtypeStruct((8, 128), F32)),
        in_specs=[_HBM] * (2 * n),
        out_specs=(_SEM, _SEM, *[_HBM] * (2 * n), pl.BlockSpec(memory_space=pltpu.VMEM)),
        input_output_aliases={a: 2 + a for a in range(2 * n)},
        compiler_params=pltpu.CompilerParams(has_side_effects=_DATAFLOW),
    )(*[_in_hbm(f) for f in fams], *[_in_hbm(lax.empty(ls, f.dtype)) for ls, f in zip(land_shapes, fams)])


def _grad_sibling_wait(started, after, name):
    send_sems, recv_sems, *thru = started
    n = len(thru) // 2
    rhs = [t.shape[1] // 2 for t in thru[:n]]

    def body(*refs):
        ins, lands = refs[:n], refs[n:2 * n]
        send_sems, recv_sems = refs[2 * n], refs[2 * n + 1]
        x, y, c = _pos()
        for a in range(n):
            cp = _sibling_copy(ins, lands, send_sems, recv_sems, rhs, a, c, x, y)
            cp.wait_send()
            cp.wait_recv()

    outs = pl.pallas_call(
        body, name=name, out_shape=[pltpu.HBM(t.shape, t.dtype) for t in thru],
        in_specs=[_HBM] * (2 * n) + [_SEM, _SEM] + [_ANY] * len(after), out_specs=[_HBM] * (2 * n),
        input_output_aliases={a: a for a in range(2 * n)},
        compiler_params=pltpu.CompilerParams(has_side_effects=_DATAFLOW),
    )(*thru, send_sems, recv_sems, *after)
    return outs[:n], outs[n:]


def _grad_share(fulls, name, small=None):
    n = len(fulls)
    ns = 0 if small is None else 1
    rhs = [f.shape[0] // 2 for f in fulls]

    def body(*refs):
        ins, outs = refs[:n], refs[n + ns:2 * n + ns]
        send_sems, recv_sems = refs[2 * (n + ns)], refs[2 * (n + ns) + 1]
        x, y, c = _pos()

        def copy(a, half):
            rows = pl.ds(pl.multiple_of(half * rhs[a], 8), rhs[a])
            return pltpu.make_async_remote_copy(src_ref=ins[a].at[rows, :], dst_ref=outs[a].at[rows, :],
                                                send_sem=send_sems.at[7 * ns + a], recv_sem=recv_sems.at[7 * ns + a],
                                                device_id=(x, y, 1 - c), device_id_type=MESH)

        sends = [copy(a, c) for a in range(n)]
        for cp in sends:
            cp.start()
        if ns:
            small_ref, all_ref = refs[n], refs[2 * n + 1]
            me = 4 * x + 2 * y + c

            def peer(r):
                dx, dy, dc = (r >> 2) & 1, (r >> 1) & 1, r & 1
                return (x if dx == 0 else 1 - x), (y if dy == 0 else 1 - y), (c if dc == 0 else 1 - c)

            def small_copy(r, slot):
                return pltpu.make_async_remote_copy(src_ref=small_ref, dst_ref=all_ref.at[slot], send_sem=send_sems.at[r - 1],
                                                    recv_sem=recv_sems.at[r - 1], device_id=peer(r), device_id_type=MESH)

            smalls = [small_copy(r, me) for r in range(1, 8)]
            for cp in smalls:
                cp.start()
            for r in range(1, 8):
                px, py, pc = peer(r)
                small_copy(r, 4 * px + 2 * py + pc).wait_recv()
            sends = sends + smalls
        for a in range(n):
            copy(a, 1 - c).wait_recv()
        for cp in sends:
            cp.wait_send()

    return pl.pallas_call(
        body, name=name, in_specs=[_HBM] * (n + ns), out_specs=[_HBM] * (n + ns),
        out_shape=[jax.ShapeDtypeStruct(f.shape, f.dtype) for f in fulls]
        + ([jax.ShapeDtypeStruct((8,) + small.shape, small.dtype)] if ns else []),
        input_output_aliases={a: a for a in range(n)},
        scratch_shapes=[pltpu.SemaphoreType.DMA((7 * ns + n,)), pltpu.SemaphoreType.DMA((7 * ns + n,))],
    )(*fulls, *([small] if ns else []))


def _add_sibling(own, recv, cq, name):
    nb, R, Cc = own.shape
    Rh = R // 2

    def body(cq_ref, a_ref, b_ref, o32_ref, o16_ref):
        s = a_ref[...] + b_ref[...]
        o32_ref[...] = s
        o16_ref[...] = s.astype(o16_ref.dtype)

    sp = pl.BlockSpec((1, Rh, Cc), lambda b, s: (b, 0, 0))
    gs = pltpu.PrefetchScalarGridSpec(
        num_scalar_prefetch=1, grid=(nb,),
        in_specs=[pl.BlockSpec((1, Rh, Cc), lambda b, s: (b, s[0], 0)), sp], out_specs=[sp, sp])
    return pl.pallas_call(
        body, name=name, grid_spec=gs,
        out_shape=[jax.ShapeDtypeStruct((nb, Rh, Cc), F32), jax.ShapeDtypeStruct((nb, Rh, Cc), _MXU)],
        compiler_params=_cparams(("parallel",)),
    )(cq, own, recv)


def _add_chips(part32, recv3, cq, name):
    nb, Rh, Cc = part32.shape

    def body(cq_ref, a_ref, b_ref, o_ref):
        acc = a_ref[0]
        for j in range(3):
            acc = acc + b_ref[j].astype(F32)
        o_ref[...] = acc

    gs = pltpu.PrefetchScalarGridSpec(
        num_scalar_prefetch=1, grid=(1,),
        in_specs=[pl.BlockSpec((1, Rh, Cc), lambda i, s: (s[1], 0, 0)), pl.BlockSpec((3, Rh, Cc), lambda i, s: (0, 0, 0))],
        out_specs=pl.BlockSpec((Rh, Cc), lambda i, s: (s[0], 0)))
    return pl.pallas_call(
        body, name=name, grid_spec=gs, out_shape=jax.ShapeDtypeStruct((2 * Rh, Cc), F32),
        compiler_params=_cparams(("arbitrary",)),
    )(cq, part32, recv3)


def _sum_devices(small_all, small, me):
    def body(me_ref, all_ref, own_ref, o_ref):
        tot = None
        for d in range(8):
            term = jnp.where(me_ref[0] == d, own_ref[...], all_ref[d])
            tot = term if tot is None else tot + term
        o_ref[...] = tot

    gs = pltpu.PrefetchScalarGridSpec(
        num_scalar_prefetch=1, grid=(1,),
        in_specs=[pl.BlockSpec(small_all.shape, lambda i, s: (0, 0, 0)), pl.BlockSpec(small.shape, lambda i, s: (0, 0))],
        out_specs=pl.BlockSpec(small.shape, lambda i, s: (0, 0)))
    return pl.pallas_call(body, name="sum_devices", grid_spec=gs,
                          out_shape=jax.ShapeDtypeStruct(small.shape, F32))(me, small_all, small)


def _adamw(w, g, m, v, name):
    R, Cc = w.shape
    T = max([t for t in range(8, 257, 8) if R % t == 0], default=R)
    c1 = 1.0 / (1.0 - ADAM_B1 ** ADAM_STEP)
    c2 = 1.0 / (1.0 - ADAM_B2 ** ADAM_STEP)

    def body(w_ref, g_ref, m_ref, v_ref, d_ref, mo_ref, vo_ref):
        gv = g_ref[...]
        mn = ADAM_B1 * m_ref[...] + (1.0 - ADAM_B1) * gv
        vn = ADAM_B2 * v_ref[...] + (1.0 - ADAM_B2) * (gv * gv)
        mo_ref[...] = mn
        vo_ref[...] = vn
        d_ref[...] = -ADAM_LR * ((mn * c1) / (jnp.sqrt(vn * c2) + ADAM_EPS) + ADAM_WD * w_ref[...])

    sp = pl.BlockSpec((T, Cc), lambda i: (i, 0))
    sh = jax.ShapeDtypeStruct((R, Cc), F32)
    return pl.pallas_call(
        body, name=name, grid=(R // T,), in_specs=[sp] * 4, out_specs=(sp, sp, sp), out_shape=(sh, sh, sh),
        compiler_params=_cparams(("parallel",)),
    )(w, g, m, v)


SMALL_ROWS = 32
REPL_ROWS = 8


def _pad_lanes(v, n=D_MODEL):
    return jnp.pad(v, ((0, 0), (0, n - v.shape[1])))


def kernel(x, norm1_w, w_in, conv_qkv_w, a_log, dt_bias, gdn_norm_w, w_out, norm2_w, w_up, ffn_conv_w, w_down, final_norm_w, loss_target, m_norm1_w, m_w_in, m_conv_qkv_w, m_a_log, m_dt_bias, m_gdn_norm_w, m_w_out, m_norm2_w, m_w_up, m_ffn_conv_w, m_w_down, m_final_norm_w, v_norm1_w, v_w_in, v_conv_qkv_w, v_a_log, v_dt_bias, v_gdn_norm_w, v_w_out, v_norm2_w, v_w_up, v_ffn_conv_w, v_w_down, v_final_norm_w):
    c = lax.axis_index("c")
    q = 2 * lax.axis_index("x") + lax.axis_index("y")
    S = x.shape[1]
    cq = jnp.stack([c, q]).astype(jnp.int32)

    *in_started, in_token = _gather_halves_start([w_in[0].astype(_MXU), conv_qkv_w[0], ffn_conv_w[0]], x, "gather_in_start")
    w_in_l, m_w_in_l, v_w_in_l = (a + in_token[0:1, 0:1] for a in (w_in, m_w_in, v_w_in))
    in_shards, got_in = _gather_halves_wait(in_started, [w_in_l, m_w_in_l, v_w_in_l], "gather_in_wait")
    g_in, g_conv, g_fconv = _place_own(in_shards, _sibling_fill(got_in, "fill_in"), cq, "place_in")
    rest = [w_out[0].astype(_MXU), w_up[0].astype(_MXU), w_down[0].astype(_MXU)]
    *rest_started, token = _gather_halves_start(rest, g_conv, "gather_rest_start")

    def rest_weights(after):
        shards, got = _gather_halves_wait(rest_started, after, "gather_rest_wait")
        got = _sibling_fill(got, "fill_rest")
        g_out, g_up, g_down = _place_own(shards, got, cq, "place_rest")
        return g_out.reshape(D_MODEL, D_MODEL), g_up, g_down.reshape(D_FF, D_MODEL)
    wp = _wp_assemble(g_in)
    conv_f = jnp.concatenate([g_conv[i] for i in range(N_CHIPS)], axis=1)
    fcw = jnp.concatenate([g_fconv[i] for i in range(N_CHIPS)], axis=1)
    gp = _pad_lanes(jnp.concatenate([a_log, dt_bias], axis=1), 128)
    fnw = final_norm_w[None, :]
    early = {}

    def early_sibling(d_wup, d_wdown):
        *early["sibling"], tok = _grad_sibling_start([d_wup, d_wdown.reshape(N_CHIPS, D_FF // N_CHIPS, D_MODEL)],
                                                     "grad_sibling_early_start")
        return tok

    def early_chips(dx2):
        fams_e, got_e = _grad_sibling_wait(early["sibling"], [dx2], "grad_sibling_early_wait")
        early["parts"] = [_add_sibling(f, r, cq, "add_sibling_" + nm) for f, r, nm in zip(fams_e, got_e, ("w_up", "w_down"))]
        *early["started"], tok = _grad_chips_start([p[1] for p in early["parts"]], "grad_chips_start")
        return tok

    def late_sibling(d_wp, d_wout):
        *early["late_sibling"], tok = _grad_sibling_start(
            [_win_split(d_wp), d_wout.reshape(N_CHIPS, D_MODEL // N_CHIPS, D_MODEL)], "grad_sibling_late_start")
        return tok

    early_grads = (early_sibling, early_chips, late_sibling)

    loss_l, dx, g = _local_step(x[0], loss_target[0], norm1_w + token[0:1, 0:1], norm2_w, fnw, gp, gdn_norm_w, wp, conv_f,
                                fcw, rest_weights, early_grads)
    n_fc = FFN_CONV * D_FF

    def rows_of(v):
        flat = v.reshape(-1)
        return jnp.pad(flat, (0, -flat.shape[0] % D_MODEL)).reshape(-1, D_MODEL)

    gp_row = _pad_lanes(jnp.concatenate([g["gp"][0:1, 0:8], loss_l[0:1, 0:1]], axis=1))
    small = jnp.concatenate([g["n1w"], g["n2w"], g["fnw"], gp_row, _pad_lanes(g["gnw"]),
                             rows_of(g["conv_w"]), rows_of(g["fcw_g"]), rows_of(g["fcw_u"])], axis=0)
    small = jnp.pad(small, ((0, SMALL_ROWS - small.shape[0]), (0, 0)))
    fams, got = _grad_sibling_wait(early["late_sibling"], [dx], "grad_sibling_late_wait")
    parts = [_add_sibling(f, r, cq, "add_sibling_" + nm) for f, r, nm in zip(fams, got, ("w_in", "w_out"))]
    *late_started, late_token = _grad_chips_start([p[1] for p in parts], "grad_chips_late_start")
    got3_e = _grad_chips_wait(early["started"], [dx, g["wp"], late_token], "grad_chips_wait")
    g_w_up, g_w_down = _grad_share(
        [_add_chips(p[0], r3, cq, "add_chips_" + nm) for p, r3, nm in zip(early["parts"], got3_e, ("w_up", "w_down"))],
        "grad_share_early")
    big = {}

    def adamw_big(nm, w, gg, m, v):
        d_, m_, v_ = _adamw(w[0], gg, m[0], v[0], "adamw_" + nm)
        big[nm] = (gg[None], d_[None], m_[None], v_[None])

    adamw_big("w_up", w_up, g_w_up, m_w_up, v_w_up)
    adamw_big("w_down", w_down, g_w_down, m_w_down, v_w_down)
    got3 = _grad_chips_wait(late_started, [big["w_up"][1], big["w_down"][1]], "grad_chips_late_wait")
    g_w_in, g_w_out, small_all = _grad_share(
        [_add_chips(p[0], r3, cq, "add_chips_" + nm) for p, r3, nm in zip(parts, got3, ("w_in", "w_out"))],
        "grad_share_late", small)
    small_red = _sum_devices(small_all, small, (2 * q + c).astype(jnp.int32).reshape(1))
    loss = small_red[3, 8]
    r0 = 5
    r1 = r0 + GDN_CONV * 3 * GDN_WIDTH // D_MODEL
    r2 = r1 + -(-n_fc // D_MODEL)
    conv_red = small_red[r0:r1].reshape(GDN_CONV, 3 * GDN_WIDTH)
    fc_red = jnp.concatenate([small_red[r1:r2].reshape(-1)[:n_fc].reshape(FFN_CONV, D_FF),
                              small_red[r2:2 * r2 - r1].reshape(-1)[:n_fc].reshape(FFN_CONV, D_FF)], axis=1)
    g_conv_w = lax.dynamic_slice_in_dim(conv_red, q * (3 * GDN_WIDTH // N_CHIPS), 3 * GDN_WIDTH // N_CHIPS, axis=1)
    g_fconv_w = lax.dynamic_slice_in_dim(fc_red, q * (2 * D_FF // N_CHIPS), 2 * D_FF // N_CHIPS, axis=1)
    g_n1w, g_n2w, g_fnw = small_red[0:1], small_red[1:2], small_red[2]
    g_alog, g_dtb, g_gnw = small_red[3:4, 0:4], small_red[3:4, 4:8], small_red[4:5, 0:128]
    for nm, w, gg, m, v in (("w_in", w_in_l, g_w_in, m_w_in_l, v_w_in_l), ("conv_qkv_w", conv_qkv_w, g_conv_w, m_conv_qkv_w, v_conv_qkv_w),
                            ("w_out", w_out, g_w_out, m_w_out, v_w_out),
                            ("ffn_conv_w", ffn_conv_w, g_fconv_w, m_ffn_conv_w, v_ffn_conv_w)):
        adamw_big(nm, w, gg, m, v)

    def pack_small(n1, n2, fn, al, db, gn):
        return jnp.concatenate([n1, n2, fn[None, :], _pad_lanes(jnp.concatenate([al, db], axis=1)), _pad_lanes(gn),
                                jnp.zeros((REPL_ROWS - 5, D_MODEL), F32)], axis=0)

    sw = pack_small(norm1_w, norm2_w, final_norm_w, a_log, dt_bias, gdn_norm_w)
    sm = pack_small(m_norm1_w, m_norm2_w, m_final_norm_w, m_a_log, m_dt_bias, m_gdn_norm_w)
    sv = pack_small(v_norm1_w, v_norm2_w, v_final_norm_w, v_a_log, v_dt_bias, v_gdn_norm_w)
    sd, smn, svn = _adamw(sw, small_red[:REPL_ROWS], sm, sv, "adamw_small")

    def unpack_small(t):
        return dict(norm1_w=t[0:1], norm2_w=t[1:2], final_norm_w=t[2], a_log=t[3:4, 0:4], dt_bias=t[3:4, 4:8],
                    gdn_norm_w=t[4:5, 0:128])

    sg = dict(norm1_w=g_n1w, norm2_w=g_n2w, final_norm_w=g_fnw, a_log=g_alog, dt_bias=g_dtb, gdn_norm_w=g_gnw)
    sd, smn, svn = unpack_small(sd), unpack_small(smn), unpack_small(svn)
    names = ["norm1_w", "w_in", "conv_qkv_w", "a_log", "dt_bias", "gdn_norm_w", "w_out", "norm2_w", "w_up",
             "ffn_conv_w", "w_down", "final_norm_w"]
    grads = [big[n][0] if n in big else sg[n] for n in names]
    deltas = [big[n][1] if n in big else sd[n] for n in names]
    new_m = [big[n][2] if n in big else smn[n] for n in names]
    new_v = [big[n][3] if n in big else svn[n] for n in names]
    return (loss, dx[None], *grads, *deltas, *new_m, *new_v)
```

```python
import functools
import math

import numpy as np
import jax
import jax.numpy as jnp
from jax import lax
from jax.experimental import pallas as pl
from jax.experimental.pallas import tpu as pltpu

F32 = jnp.float32
BF16 = jnp.bfloat16
_MXU = jnp.bfloat16
_HI = lax.Precision.HIGHEST
EPS = 1e-6
V7X_VMEM_LIMIT = 56 * 1024 * 1024
MESH = pl.DeviceIdType.MESH

D_MODEL = 1024
GDN_HEADS, GDN_DIM, GDN_CHUNK, GDN_CONV = 4, 128, 64, 4
GDN_WIDTH = GDN_HEADS * GDN_DIM
DIL_HEADS, DIL_DIM = 8, 64
DIL_WIDTH = DIL_HEADS * DIL_DIM
D_FF, FFN_CONV = 2816, 3
IN_COLS = 3592
P_COLS = 3840
P_Z, P_QKVB, P_BA = 1536, 2048, 3584
ATT_T = 1024
ADAM_LR, ADAM_B1, ADAM_B2, ADAM_EPS, ADAM_WD, ADAM_STEP = 0.001, 0.9, 0.999, 1e-08, 0.01, 10
N_CHIPS = 4


def _cparams(sem=None, vmem=None):
    kw = {}
    if sem is not None:
        kw["dimension_semantics"] = sem
    if vmem is not None:
        kw["vmem_limit_bytes"] = vmem
    return pltpu.CompilerParams(**kw)


def _silu(x):
    return x * jax.nn.sigmoid(x)


def _pick_tile(n, cap):
    best = None
    for t in range(128, min(n, cap) + 1, 128):
        if n % t == 0:
            best = t
    return best or n


def _mm(a, b, mode, *, out_dtype=F32, residual=None, name, b_blocks=False, place=None, into=None, tn=None):
    if mode == "nn":
        M, K = a.shape
        N = b.shape[0] * b.shape[2] if b_blocks else b.shape[1]
    elif mode == "nt":
        (M, K), (N, _) = a.shape, b.shape
    else:
        (K, M), (_, N) = a.shape, b.shape
    tm = _pick_tile(M, 1024)
    tn = b.shape[2] if b_blocks else (tn or _pick_tile(N, 1536))

    def vmem(tm, tn):
        return 2 * (tm * K * a.dtype.itemsize + tn * K * b.dtype.itemsize
                    + tm * tn * (jnp.dtype(out_dtype).itemsize + (4 if residual is not None else 0))) + 3 * tm * tn * 4

    fixed_tn = b_blocks or (place is not None and place[0] == "blocks")
    while vmem(tm, tn) > 40 * 1024 * 1024:
        if (tm >= tn or fixed_tn) and tm % 256 == 0:
            tm //= 2
        elif tn % 256 == 0 and not fixed_tn:
            tn //= 2
        else:
            tm //= 2
    a_spec = pl.BlockSpec((K, tm), lambda j, i: (0, i)) if mode == "tn" else pl.BlockSpec((tm, K), lambda j, i: (i, 0))
    if b_blocks:
        b_spec = pl.BlockSpec((None, K, tn), lambda j, i: (j, 0, 0))
    else:
        b_spec = pl.BlockSpec((tn, K), lambda j, i: (j, 0)) if mode == "nt" else pl.BlockSpec((K, tn), lambda j, i: (0, j))
    r_spec = pl.BlockSpec((tm, tn), lambda j, i: (i, j))
    if place is None:
        o_spec, o_shape = r_spec, (M, N)
    elif place[0] == "rows":
        off = place[2] // tm
        o_spec, o_shape = pl.BlockSpec((tm, tn), lambda j, i: (i + off, j)), (place[1], N)
    else:
        off = place[2]
        o_spec, o_shape = pl.BlockSpec((None, tm, tn), lambda j, i: (j + off, i, 0)), (place[1], M, tn)
    dims = {"nn": (((1,), (0,)), ((), ())), "nt": (((1,), (1,)), ((), ())), "tn": (((0,), (0,)), ((), ()))}[mode]

    def body(*refs):
        a_ref, b_ref = refs[0], refs[1]
        o_ref = refs[-1]
        acc = lax.dot_general(a_ref[...].astype(_MXU), b_ref[...].astype(_MXU), dims, preferred_element_type=F32)
        if residual is not None:
            acc = acc + refs[2][...]
        o_ref[...] = acc.astype(out_dtype)

    ins, specs, alias = [a, b], [a_spec, b_spec], {}
    if residual is not None:
        ins.append(residual)
        specs.append(r_spec)
    if into is not None:
        alias = {len(ins): 0}
        ins.append(into)
        specs.append(pl.BlockSpec(memory_space=pl.ANY))
    return pl.pallas_call(
        body, name=name, grid=(N // tn, M // tm), in_specs=specs, out_specs=o_spec,
        out_shape=jax.ShapeDtypeStruct(o_shape, out_dtype), input_output_aliases=alias,
        compiler_params=_cparams(("parallel", "parallel"), V7X_VMEM_LIMIT),
    )(*ins)


def _mm_nt_blocks(a_list, b4, name):
    M = a_list[0].shape[0]
    nb, N, Kb = b4.shape
    tm, tn = _pick_tile(M, 512), _pick_tile(N, 512)

    def body(a0_ref, a1_ref, b_ref, o_ref):
        acc = None
        for blk in range(nb):
            a_ref = (a0_ref, a1_ref)[blk // 2]
            lo = (blk % 2) * Kb
            t = lax.dot_general(a_ref[:, lo:lo + Kb].astype(_MXU), b_ref[blk].astype(_MXU), (((1,), (1,)), ((), ())),
                                preferred_element_type=F32)
            acc = t if acc is None else acc + t
        o_ref[...] = acc

    a_spec = pl.BlockSpec((tm, 2 * Kb), lambda j, i: (i, 0))
    return pl.pallas_call(
        body, name=name, grid=(N // tn, M // tm),
        in_specs=[a_spec, a_spec, pl.BlockSpec((nb, tn, Kb), lambda j, i: (0, j, 0))],
        out_specs=pl.BlockSpec((tm, tn), lambda j, i: (i, j)), out_shape=jax.ShapeDtypeStruct((M, N), F32),
        compiler_params=_cparams(("parallel", "parallel"), V7X_VMEM_LIMIT),
    )(a_list[0], a_list[1], b4)


def _wp_assemble(g_in):
    nb, Dm, Wb = g_in.shape
    T = 256
    n_lo = P_QKVB - 2 * Wb

    def body(g_ref, o_ref):
        g2 = g_ref[2]
        o_ref[...] = jnp.concatenate(
            [g_ref[0], g_ref[1], g2[:, :n_lo], g2[:, n_lo + 8:], g_ref[3], g2[:, n_lo:n_lo + 8],
             jnp.zeros((T, P_COLS - P_BA - 8), g_in.dtype)], axis=1)

    return pl.pallas_call(
        body, name="wp_assemble", grid=(Dm // T,), in_specs=[pl.BlockSpec((nb, T, Wb), lambda i: (0, i, 0))],
        out_specs=pl.BlockSpec((T, P_COLS), lambda i: (i, 0)), out_shape=jax.ShapeDtypeStruct((Dm, P_COLS), g_in.dtype),
        compiler_params=_cparams(("parallel",)),
    )(g_in)


def _win_split(d_wp):
    Dm = d_wp.shape[0]
    Wb = IN_COLS // N_CHIPS
    T = 256

    def body(x_ref, o_ref):
        xv = x_ref[...]
        o_ref[0] = xv[:, 0:Wb]
        o_ref[1] = xv[:, Wb:2 * Wb]
        o_ref[2] = jnp.concatenate([xv[:, 2 * Wb:P_QKVB], xv[:, P_BA:P_BA + 8], xv[:, P_QKVB:3 * Wb - 8]], axis=1)
        o_ref[3] = xv[:, 3 * Wb - 8:P_BA]

    return pl.pallas_call(
        body, name="win_split", grid=(Dm // T,), in_specs=[pl.BlockSpec((T, P_COLS), lambda i: (i, 0))],
        out_specs=pl.BlockSpec((N_CHIPS, T, Wb), lambda i: (0, i, 0)),
        out_shape=jax.ShapeDtypeStruct((N_CHIPS, Dm, Wb), F32), compiler_params=_cparams(("parallel",)),
    )(d_wp)


def _rmsnorm_fwd(x, w, name):
    S, D = x.shape
    T = _pick_tile(S, 512)

    def body(x_ref, w_ref, o_ref):
        xv = x_ref[...]
        rs = lax.rsqrt(jnp.mean(xv * xv, axis=-1, keepdims=True) + EPS)
        o_ref[...] = (xv * rs * w_ref[...]).astype(o_ref.dtype)

    return pl.pallas_call(
        body, name=name, grid=(S // T,),
        in_specs=[pl.BlockSpec((T, D), lambda i: (i, 0)), pl.BlockSpec((1, D), lambda i: (0, 0))],
        out_specs=pl.BlockSpec((T, D), lambda i: (i, 0)),
        out_shape=jax.ShapeDtypeStruct((S, D), _MXU),
        compiler_params=_cparams(("parallel",)),
    )(x, w)


def _rmsnorm_bwd(dh, x, w, dres, name):
    S, D = x.shape
    T = _pick_tile(S, 512)

    def body(dh_ref, x_ref, w_ref, dres_ref, dx_ref, dw_ref):
        xv = x_ref[...]
        rs = lax.rsqrt(jnp.mean(xv * xv, axis=-1, keepdims=True) + EPS)
        xn = xv * rs
        dhv = dh_ref[...]
        dxn = dhv * w_ref[...]
        dx_ref[...] = dres_ref[...] + rs * (dxn - xn * jnp.mean(dxn * xn, axis=-1, keepdims=True))

        @pl.when(pl.program_id(0) == 0)
        def _():
            dw_ref[...] = jnp.zeros_like(dw_ref)

        dw_ref[...] += jnp.sum(dhv * xn, axis=0, keepdims=True)

    row = pl.BlockSpec((T, D), lambda i: (i, 0))
    vec = pl.BlockSpec((1, D), lambda i: (0, 0))
    return pl.pallas_call(
        body, name=name, grid=(S // T,), in_specs=[row, row, vec, row], out_specs=(row, vec),
        out_shape=(jax.ShapeDtypeStruct((S, D), F32), jax.ShapeDtypeStruct((1, D), F32)),
        compiler_params=_cparams(("arbitrary",)),
    )(dh, x, w, dres)


def _loss_head(x3, w, tgt, name):
    S, D = x3.shape
    T = _pick_tile(S, 512)

    def body(x_ref, w_ref, t_ref, loss_ref, dx_ref, dw_ref):
        xv = x_ref[...]
        rs = lax.rsqrt(jnp.mean(xv * xv, axis=-1, keepdims=True) + EPS)
        xn = xv * rs
        err = xn * w_ref[...] - t_ref[...]
        dy = err * (1.0 / D)
        dxn = dy * w_ref[...]
        dx_ref[...] = rs * (dxn - xn * jnp.mean(dxn * xn, axis=-1, keepdims=True))

        @pl.when(pl.program_id(0) == 0)
        def _():
            dw_ref[...] = jnp.zeros_like(dw_ref)
            loss_ref[...] = jnp.zeros_like(loss_ref)

        dw_ref[...] += jnp.sum(dy * xn, axis=0, keepdims=True)
        part = jnp.sum(jnp.sum(err * err, axis=-1, keepdims=True), axis=0, keepdims=True) * (0.5 / D)
        loss_ref[...] += jnp.broadcast_to(part, loss_ref.shape)

    row = pl.BlockSpec((T, D), lambda i: (i, 0))
    vec = pl.BlockSpec((1, D), lambda i: (0, 0))
    return pl.pallas_call(
        body, name=name, grid=(S // T,), in_specs=[row, vec, row],
        out_specs=(pl.BlockSpec((8, 128), lambda i: (0, 0)), row, vec),
        out_shape=(jax.ShapeDtypeStruct((8, 128), F32), jax.ShapeDtypeStruct((S, D), F32), jax.ShapeDtypeStruct((1, D), F32)),
        compiler_params=_cparams(("arbitrary",)),
    )(x3, w, tgt)


def _shifted(ext, back, lo, n):
    if back == 0:
        return ext[lo:lo + n, :]
    return pltpu.roll(ext, back % ext.shape[0], 0)[lo:lo + n, :]


def _conv_windows(ext, K, T):
    return [_shifted(ext, (K - 1) - i, 8, T) for i in range(K)]


def _conv_taps(ext, w, K, T):
    out = None
    for i, win in enumerate(_conv_windows(ext, K, T)):
        term = win * w[i:i + 1, :]
        out = term if out is None else out + term
    return out


def _conv_taps_t(ext, w, K, T):
    out = None
    for i in range(K):
        term = _shifted(ext, i - (K - 1), 0, T) * w[i:i + 1, :]
        out = term if out is None else out + term
    return out


def _tri_masks(C):
    r = lax.broadcasted_iota(jnp.int32, (C, C), 0)
    c = lax.broadcasted_iota(jnp.int32, (C, C), 1)
    return r == c, r >= c, r > c, r <= c


_NN, _NT, _TN = ((1,), (0,)), ((1,), (1,)), ((0,), (0,))
_GDN_PASSES = dict(qk=1, inv=1, sol=1, scan=1, bwd=1)


def _bdot_raw(a, b, kind, passes):
    dims = ({"NN": ((2,), (1,)), "NT": ((2,), (2,)), "TN": ((1,), (1,))}[kind], ((0,), (0,)))
    if passes == 0:
        return lax.dot_general(a, b, dims, precision=_HI, preferred_element_type=F32)
    ah, bh = a.astype(BF16), b.astype(BF16)
    out = lax.dot_general(ah, bh, dims, preferred_element_type=F32)
    if passes == 3:
        al, bl = (a - ah.astype(F32)).astype(BF16), (b - bh.astype(F32)).astype(BF16)
        out = out + lax.dot_general(ah, bl, dims, preferred_element_type=F32) + lax.dot_general(al, bh, dims, preferred_element_type=F32)
    return out


@functools.partial(jax.custom_vjp, nondiff_argnums=(2, 3))
def _bdot(a, b, kind, passes):
    return _bdot_raw(a, b, kind, passes)


def _bdot_fwd(a, b, kind, passes):
    return _bdot_raw(a, b, kind, passes), (a, b)


def _bdot_bwd(kind, passes, res, ct):
    a, b = res
    if kind == "NN":
        return _bdot_raw(ct, b, "NT", passes), _bdot_raw(a, ct, "TN", passes)
    if kind == "NT":
        return _bdot_raw(ct, b, "NN", passes), _bdot_raw(ct, a, "TN", passes)
    return _bdot_raw(b, ct, "NT", passes), _bdot_raw(a, ct, "NN", passes)


_bdot.defvjp(_bdot_fwd, _bdot_bwd)


def _softplus(x):
    return jnp.maximum(x, 0.0) + jnp.log(1.0 + jnp.exp(-jnp.abs(x)))


def _gdn_stage1(cq, ck, cv, b_col, a_col, alog, dtb, dot=_bdot_raw):
    C = cq.shape[1]
    eye, incl, strict, incl_t = _tri_masks(C)
    qn = cq * lax.rsqrt(jnp.sum(cq * cq, axis=-1, keepdims=True) + EPS) * (GDN_DIM ** -0.5)
    kn = ck * lax.rsqrt(jnp.sum(ck * ck, axis=-1, keepdims=True) + EPS)
    beta = jax.nn.sigmoid(b_col)
    g = -jnp.exp(alog) * _softplus(a_col + dtb)
    g_row = jnp.sum(jnp.where(eye, g, 0.0), axis=1, keepdims=True)
    beta_row = jnp.sum(jnp.where(eye, beta, 0.0), axis=1, keepdims=True)
    gc_col = jnp.sum(jnp.where(incl, g_row, 0.0), axis=2, keepdims=True)
    gc_row = jnp.sum(jnp.where(incl_t, g, 0.0), axis=1, keepdims=True)
    dec = jnp.where(incl, jnp.exp(jnp.where(incl, gc_col - gc_row, 0.0)), 0.0)
    kk = dot(kn, kn, "NT", _GDN_PASSES["qk"])
    qk = dot(qn, kn, "NT", _GDN_PASSES["qk"])
    lmat = jnp.where(strict, dec * kk * beta_row, 0.0)
    attn = dec * qk * beta_row
    gam = jnp.exp(gc_col)
    gc_last = gc_col[:, C - 1:C, :]
    k_end = kn * (jnp.exp(gc_last - gc_col) * beta)
    return lmat, cv, gam * kn, gam * qn, attn, k_end, jnp.exp(gc_last)


def _tri_inv(lmat):
    C = lmat.shape[1]
    eye = _tri_masks(C)[0]
    ps = _GDN_PASSES["inv"]
    p = jnp.where(eye, 1.0, 0.0) - lmat
    lp = _bdot_raw(lmat, lmat, "NN", ps)
    n = int(math.log2(C))
    for s in range(1, n):
        p = p + _bdot_raw(p, lp, "NN", ps)
        if s < n - 1:
            lp = _bdot_raw(lp, lp, "NN", ps)
    return p


def _gated_norm(o, z, gnw):
    on = o * lax.rsqrt(jnp.mean(o * o, axis=-1, keepdims=True) + EPS) * gnw
    return on * _silu(z)


GDN_PG = 2
GDN_SG = 4


def _gdn_pairs(c, ba, gp, G):
    C, W, H = GDN_CHUNK, GDN_WIDTH, GDN_HEADS
    pairs = [(j, h) for j in range(G) for h in range(H)]
    cq, ck, cv = (jnp.stack([c[C * j:C * (j + 1), o + GDN_DIM * h:o + GDN_DIM * (h + 1)] for j, h in pairs]) for o in (0, W, 2 * W))
    b_col = jnp.stack([ba[C * j:C * (j + 1), h:h + 1] for j, h in pairs])
    a_col = jnp.stack([ba[C * j:C * (j + 1), H + h:H + h + 1] for j, h in pairs])
    alog = jnp.stack([gp[0:1, h:h + 1] for j, h in pairs])
    dtb = jnp.stack([gp[0:1, H + h:H + h + 1] for j, h in pairs])
    return pairs, (cq, ck, cv, b_col, a_col, alog, dtb)


def _gdn_pre_specs(S, G):
    C = GDN_CHUNK
    T = C * G
    return dict(
        cur=pl.BlockSpec((T, 3 * GDN_WIDTH), lambda i: (i, 0)),
        prev=pl.BlockSpec((8, 3 * GDN_WIDTH), lambda i: (jnp.maximum(i * (T // 8) - 1, 0), 0)),
        ba=pl.BlockSpec((T, 128), lambda i: (i, P_BA // 128)),
        cw=pl.BlockSpec((GDN_CONV, 3 * GDN_WIDTH), lambda i: (0, 0)),
        vec=pl.BlockSpec((1, 128), lambda i: (0, 0)),
        hd=pl.BlockSpec((GDN_HEADS, T, GDN_DIM), lambda i: (0, i, 0)),
        hc=pl.BlockSpec((GDN_HEADS, T, C), lambda i: (0, i, 0)),
        ge=pl.BlockSpec((G, GDN_HEADS, 8, 128), lambda i: (i, 0, 0, 0)),
    )


def _hd_shape(S, last=GDN_DIM):
    return jax.ShapeDtypeStruct((GDN_HEADS, S, last), F32)


def _gdn_pre(proj, conv_w, gp):
    S = proj.shape[0]
    C, G = GDN_CHUNK, GDN_PG
    nc = S // C
    sp = _gdn_pre_specs(S, G)

    def body(cur_ref, prev_ref, ba_ref, cw_ref, gp_ref, uv_ref, wk_ref, qd_ref, ke_ref, at_ref, ti_ref, ge_ref):
        prev = prev_ref[...] * jnp.where(pl.program_id(0) == 0, 0.0, 1.0)
        c = _silu(_conv_taps(jnp.concatenate([prev, cur_ref[...]], axis=0), cw_ref[...], GDN_CONV, C * G))
        pairs, args = _gdn_pairs(c, ba_ref[...], gp_ref[...], G)
        lmat, v, rk, q_dec, attn, k_end, g_end = _gdn_stage1(*args)
        t = _tri_inv(lmat)
        u_v = _bdot_raw(t, v, "NN", _GDN_PASSES["sol"])
        w_k = _bdot_raw(t, rk, "NN", _GDN_PASSES["sol"])
        for b, (j, h) in enumerate(pairs):
            rows = slice(C * j, C * (j + 1))
            uv_ref[h, rows, :] = u_v[b]
            wk_ref[h, rows, :] = w_k[b]
            qd_ref[h, rows, :] = q_dec[b]
            ke_ref[h, rows, :] = k_end[b]
            at_ref[h, rows, :] = attn[b]
            ti_ref[h, rows, :] = t[b]
            ge_ref[j, h] = jnp.broadcast_to(g_end[b], (8, 128))

    return pl.pallas_call(
        body, name="gdn_pre", grid=(nc // G,),
        in_specs=[sp["cur"], sp["prev"], sp["ba"], sp["cw"], sp["vec"]],
        out_specs=(sp["hd"], sp["hd"], sp["hd"], sp["hd"], sp["hc"], sp["hc"], sp["ge"]),
        out_shape=(_hd_shape(S), _hd_shape(S), _hd_shape(S), _hd_shape(S), _hd_shape(S, C), _hd_shape(S, C),
                   jax.ShapeDtypeStruct((nc, GDN_HEADS, 8, 128), F32)),
        compiler_params=_cparams(("parallel",)),
    )(proj, proj, proj, conv_w, gp)


def _gdn_scan_specs(S, G, rev):
    C = GDN_CHUNK
    T = C * G
    n = S // T
    ci = (lambda i: n - 1 - i) if rev else (lambda i: i)
    return dict(
        hd=pl.BlockSpec((GDN_HEADS, T, GDN_DIM), lambda i: (0, ci(i), 0)),
        hc=pl.BlockSpec((GDN_HEADS, T, C), lambda i: (0, ci(i), 0)),
        ge=pl.BlockSpec((G, GDN_HEADS, 8, 128), lambda i: (ci(i), 0, 0, 0)),
        z=pl.BlockSpec((T, GDN_WIDTH), lambda i: (ci(i), P_Z // GDN_WIDTH)),
        oa=pl.BlockSpec((T, GDN_WIDTH), lambda i: (ci(i), 0)),
        vec=pl.BlockSpec((1, 128), lambda i: (0, 0)),
        st=pl.BlockSpec((G, GDN_HEADS, GDN_DIM, GDN_DIM), lambda i: (ci(i), 0, 0, 0)),
    )


def _gdn_scan(u_v, w_k, q_dec, k_end, attn, g_end, proj, gnw):
    S = proj.shape[0]
    C, G = GDN_CHUNK, GDN_SG
    nc = S // C
    sp = _gdn_scan_specs(S, G, False)
    ps = _GDN_PASSES["scan"]

    def body(uv_ref, wk_ref, qd_ref, ke_ref, at_ref, ge_ref, z_ref, gnw_ref, oa_ref, st_ref, s_scr):
        @pl.when(pl.program_id(0) == 0)
        def _():
            s_scr[...] = jnp.zeros_like(s_scr)

        for j in range(G):
            rows = slice(C * j, C * (j + 1))
            st = s_scr[...]
            st_ref[j] = st
            u = uv_ref[:, rows, :] - _bdot_raw(wk_ref[:, rows, :], st, "NN", ps)
            o = _bdot_raw(qd_ref[:, rows, :], st, "NN", ps) + _bdot_raw(at_ref[:, rows, :], u, "NN", ps)
            s_scr[...] = ge_ref[j][:, 0:1, 0:1] * st + _bdot_raw(ke_ref[:, rows, :], u, "TN", ps)
            for h in range(GDN_HEADS):
                cols = slice(GDN_DIM * h, GDN_DIM * (h + 1))
                oa_ref[rows, cols] = _gated_norm(o[h], z_ref[rows, cols], gnw_ref[...])

    return pl.pallas_call(
        body, name="gdn_scan", grid=(nc // G,),
        in_specs=[sp["hd"], sp["hd"], sp["hd"], sp["hd"], sp["hc"], sp["ge"], sp["z"], sp["vec"]],
        out_specs=(sp["oa"], sp["st"]),
        out_shape=(jax.ShapeDtypeStruct((S, GDN_WIDTH + DIL_WIDTH), F32),
                   jax.ShapeDtypeStruct((nc, GDN_HEADS, GDN_DIM, GDN_DIM), F32)),
        scratch_shapes=[pltpu.VMEM((GDN_HEADS, GDN_DIM, GDN_DIM), F32)],
        compiler_params=_cparams(("arbitrary",)),
    )(u_v, w_k, q_dec, k_end, attn, g_end, proj, gnw)


def _gdn_scan_bwd(u_v, w_k, q_dec, k_end, attn, g_end, proj, gnw, states, d_oa):
    S = proj.shape[0]
    C, G = GDN_CHUNK, GDN_SG
    nc = S // C
    sp = _gdn_scan_specs(S, G, True)
    ps, pb = _GDN_PASSES["scan"], _GDN_PASSES["bwd"]

    def body(uv_ref, wk_ref, qd_ref, ke_ref, at_ref, ge_ref, z_ref, gnw_ref, st_ref, doa_ref,
             duv_ref, dwk_ref, dqd_ref, dke_ref, dat_ref, dge_ref, dz_ref, dgnw_ref, ds_scr):
        @pl.when(pl.program_id(0) == 0)
        def _():
            ds_scr[...] = jnp.zeros_like(ds_scr)
            dgnw_ref[...] = jnp.zeros_like(dgnw_ref)

        dgnw = jnp.zeros((1, 128), F32)
        for j in reversed(range(G)):
            rows = slice(C * j, C * (j + 1))
            st = st_ref[j]
            wk, qd, ke, at = wk_ref[:, rows, :], qd_ref[:, rows, :], ke_ref[:, rows, :], at_ref[:, rows, :]
            u = uv_ref[:, rows, :] - _bdot_raw(wk, st, "NN", ps)
            o = _bdot_raw(qd, st, "NN", ps) + _bdot_raw(at, u, "NN", ps)
            dos = []
            for h in range(GDN_HEADS):
                cols = slice(GDN_DIM * h, GDN_DIM * (h + 1))
                _, vjp2 = jax.vjp(_gated_norm, o[h], z_ref[rows, cols], gnw_ref[...])
                do_h, dz_h, dgn = vjp2(doa_ref[rows, cols])
                dz_ref[rows, cols] = dz_h
                dgnw = dgnw + dgn
                dos.append(do_h)
            do = jnp.stack(dos)
            ds_new = ds_scr[...]
            du = _bdot_raw(at, do, "TN", pb) + _bdot_raw(ke, ds_new, "NN", pb)
            duv_ref[:, rows, :] = du
            dat_ref[:, rows, :] = _bdot_raw(do, u, "NT", pb)
            dqd_ref[:, rows, :] = _bdot_raw(do, st, "NT", pb)
            dke_ref[:, rows, :] = _bdot_raw(u, ds_new, "NT", pb)
            dwk_ref[:, rows, :] = -_bdot_raw(du, st, "NT", pb)
            d_ge = jnp.sum(jnp.sum(st * ds_new, axis=2, keepdims=True), axis=1, keepdims=True)
            dge_ref[j] = jnp.broadcast_to(d_ge, (GDN_HEADS, 8, 128))
            ds_scr[...] = ge_ref[j][:, 0:1, 0:1] * ds_new + _bdot_raw(qd, do, "TN", pb) - _bdot_raw(wk, du, "TN", pb)
        dgnw_ref[...] += dgnw

    return pl.pallas_call(
        body, name="gdn_scan_bwd", grid=(nc // G,),
        in_specs=[sp["hd"], sp["hd"], sp["hd"], sp["hd"], sp["hc"], sp["ge"], sp["z"], sp["vec"], sp["st"], sp["oa"]],
        out_specs=(sp["hd"], sp["hd"], sp["hd"], sp["hd"], sp["hc"], sp["ge"], sp["oa"], sp["vec"]),
        out_shape=(_hd_shape(S), _hd_shape(S), _hd_shape(S), _hd_shape(S), _hd_shape(S, C),
                   jax.ShapeDtypeStruct((nc, GDN_HEADS, 8, 128), F32), jax.ShapeDtypeStruct((S, GDN_WIDTH), F32),
                   jax.ShapeDtypeStruct((1, 128), F32)),
        scratch_shapes=[pltpu.VMEM((GDN_HEADS, GDN_DIM, GDN_DIM), F32)],
        compiler_params=_cparams(("arbitrary",)),
    )(u_v, w_k, q_dec, k_end, attn, g_end, proj, gnw, states, d_oa)


def _gdn_post(proj, conv_w, gp, tinv, u_v, w_k, d_uv, d_wk, d_qd, d_ke, d_at, d_ge):
    S = proj.shape[0]
    C, G = GDN_CHUNK, GDN_PG
    nc = S // C
    sp = _gdn_pre_specs(S, G)
    pb = _GDN_PASSES["bwd"]

    def body(cur_ref, prev_ref, ba_ref, cw_ref, gp_ref, ti_ref, uv_ref, wk_ref, duv_ref, dwk_ref, dqd_ref, dke_ref,
             dat_ref, dge_ref, dpre_ref, dba_ref, dgp_ref):
        i = pl.program_id(0)

        @pl.when(i == 0)
        def _():
            dgp_ref[...] = jnp.zeros_like(dgp_ref)

        prev = prev_ref[...] * jnp.where(i == 0, 0.0, 1.0)
        pre = _conv_taps(jnp.concatenate([prev, cur_ref[...]], axis=0), cw_ref[...], GDN_CONV, C * G)
        sg = jax.nn.sigmoid(pre)
        dsilu = sg * (1.0 + pre * (1.0 - sg))
        pairs, args = _gdn_pairs(pre * sg, ba_ref[...], gp_ref[...], G)
        _, vjp1 = jax.vjp(functools.partial(_gdn_stage1, dot=_bdot), *args)

        def take(ref):
            return jnp.stack([ref[h, C * j:C * (j + 1), :] for j, h in pairs])

        t, u_v, w_k = take(ti_ref), take(uv_ref), take(wk_ref)
        d_v = _bdot_raw(t, take(duv_ref), "TN", pb)
        d_rk = _bdot_raw(t, take(dwk_ref), "TN", pb)
        d_l = -(_bdot_raw(d_v, u_v, "NT", pb) + _bdot_raw(d_rk, w_k, "NT", pb))
        d_ge = jnp.stack([dge_ref[j, h][0:1, 0:1] for j, h in pairs])
        dcq, dck, dcv, db, da, dalog, ddtb = vjp1((d_l, d_v, d_rk, take(dqd_ref), take(dat_ref), take(dke_ref), d_ge))
        lane = lax.broadcasted_iota(jnp.int32, (C, 128), 1)
        lane1 = lax.broadcasted_iota(jnp.int32, (1, 128), 1)
        dgp = jnp.zeros((1, 128), F32)
        for j in range(G):
            rows = slice(C * j, C * (j + 1))
            dba = jnp.zeros((C, 128), F32)
            for h in range(GDN_HEADS):
                b = GDN_HEADS * j + h
                for o_, dcx in ((0, dcq), (GDN_WIDTH, dck), (2 * GDN_WIDTH, dcv)):
                    cols = slice(o_ + GDN_DIM * h, o_ + GDN_DIM * (h + 1))
                    dpre_ref[rows, cols] = dcx[b] * dsilu[rows, cols]
                dba = dba + jnp.where(lane == h, db[b], 0.0) + jnp.where(lane == GDN_HEADS + h, da[b], 0.0)
                dgp = dgp + jnp.where(lane1 == h, dalog[b], 0.0) + jnp.where(lane1 == GDN_HEADS + h, ddtb[b], 0.0)
            dba_ref[rows, :] = dba
        dgp_ref[0:1, :] += dgp

    T = C * G
    return pl.pallas_call(
        body, name="gdn_post", grid=(nc // G,),
        in_specs=[sp["cur"], sp["prev"], sp["ba"], sp["cw"], sp["vec"], sp["hc"], sp["hd"], sp["hd"], sp["hd"], sp["hd"],
                  sp["hd"], sp["hd"], sp["hc"], sp["ge"]],
        out_specs=(sp["cur"], pl.BlockSpec((T, 128), lambda i: (i, 0)), pl.BlockSpec((8, 128), lambda i: (0, 0))),
        out_shape=(jax.ShapeDtypeStruct((S, 3 * GDN_WIDTH), F32), jax.ShapeDtypeStruct((S, 128), F32),
                   jax.ShapeDtypeStruct((8, 128), F32)),
        compiler_params=_cparams(("arbitrary",)),
    )(proj, proj, proj, conv_w, gp, tinv, u_v, w_k, d_uv, d_wk, d_qd, d_ke, d_at, d_ge)


def _conv_bwd(dpre, x, xcol0, w, K, name, tc):
    S, Cc = dpre.shape
    T = _pick_tile(S, 256)
    nt, ncol = S // T, Cc // tc
    xo = xcol0 // tc

    def body(d_ref, dn_ref, x_ref, xp_ref, w_ref, dx_ref, dw_ref):
        i = pl.program_id(1)
        dn = dn_ref[...] * jnp.where(i == nt - 1, 0.0, 1.0)
        dv = d_ref[...]
        ext_d = jnp.concatenate([dv, dn], axis=0)
        dx_ref[...] = _conv_taps_t(ext_d, w_ref[...], K, T).astype(dx_ref.dtype)
        xp = xp_ref[...] * jnp.where(i == 0, 0.0, 1.0)
        ext_x = jnp.concatenate([xp, x_ref[...]], axis=0)

        @pl.when(i == 0)
        def _():
            dw_ref[...] = jnp.zeros_like(dw_ref)

        for k in range(K):
            dw_ref[k:k + 1, :] += jnp.sum(dv * _shifted(ext_x, (K - 1) - k, 8, T), axis=0, keepdims=True)

    r8 = T // 8
    return pl.pallas_call(
        body, name=name, grid=(ncol, nt),
        in_specs=[pl.BlockSpec((T, tc), lambda j, i: (i, j)),
                  pl.BlockSpec((8, tc), lambda j, i: (jnp.minimum((i + 1) * r8, S // 8 - 1), j)),
                  pl.BlockSpec((T, tc), lambda j, i: (i, j + xo)),
                  pl.BlockSpec((8, tc), lambda j, i: (jnp.maximum(i * r8 - 1, 0), j + xo)),
                  pl.BlockSpec((K, tc), lambda j, i: (0, j))],
        out_specs=(pl.BlockSpec((T, tc), lambda j, i: (i, j)), pl.BlockSpec((K, tc), lambda j, i: (0, j))),
        out_shape=(jax.ShapeDtypeStruct((S, Cc), _MXU), jax.ShapeDtypeStruct((K, Cc), F32)),
        compiler_params=_cparams(("parallel", "arbitrary")),
    )(dpre, dpre, x, x, w)


def _dil_bias(nt, T):
    d = (np.arange(nt)[:, None, None] * T + np.arange(T)[None, None, :] - np.arange(T)[None, :, None])
    cnt = ((d >= 0) & (d <= 128)).astype(np.float64) + ((d >= 0) & (d % 4 == 0) & (d <= 512)) + ((d >= 0) & (d % 16 == 0))
    return jnp.asarray(np.where(cnt > 0, np.log(np.maximum(cnt, 1.0)), -1e30), dtype=F32)


def _attn_fwd(proj, mix):
    S = proj.shape[0]
    T = min(ATT_T, S)
    nt = S // T
    bias = _dil_bias(nt, T)
    scale = DIL_DIM ** -0.5
    npair = DIL_WIDTH // 128
    qb0, kb0, vb0 = P_QKVB // 128, (P_QKVB + DIL_WIDTH) // 128, (P_QKVB + 2 * DIL_WIDTH) // 128

    def body(q_ref, k_ref, v_ref, b_ref, mix_ref, o_ref, lse_ref):
        i = pl.program_id(1)
        qs = (q_ref[...] * scale).astype(_MXU)

        def step(j, carry):
            kt = k_ref[pl.ds(pl.multiple_of(j * T, T), T), :].astype(_MXU)
            vt = v_ref[pl.ds(pl.multiple_of(j * T, T), T), :].astype(_MXU)
            bt = b_ref[i - j]
            out = []
            for hh in range(2):
                m, l, acc = carry[hh]
                sl = slice(hh * DIL_DIM, (hh + 1) * DIL_DIM)
                s = lax.dot_general(kt[:, sl], qs[:, sl], (_NT, ((), ())), preferred_element_type=F32) + bt
                m_new = jnp.maximum(m, jnp.max(s, axis=0, keepdims=True))
                p = jnp.exp(s - m_new)
                a = jnp.exp(m - m_new)
                l = a * l + jnp.sum(p, axis=0, keepdims=True)
                acc = a * acc + lax.dot_general(vt[:, sl], p.astype(_MXU), (_TN, ((), ())), preferred_element_type=F32)
                out.append((m_new, l, acc))
            return tuple(out)

        init = tuple((jnp.full((1, T), -1e30, F32), jnp.zeros((1, T), F32), jnp.zeros((DIL_DIM, T), F32)) for _ in range(2))
        res = lax.fori_loop(0, i + 1, step, init)
        lse_ref[...] = jnp.zeros_like(lse_ref)
        for hh in range(2):
            m, l, acc = res[hh]
            o_ref[:, hh * DIL_DIM:(hh + 1) * DIL_DIM] = (acc / l).T
            lse_ref[hh:hh + 1, :] = m + jnp.log(l)

    return pl.pallas_call(
        body, name="attn_fwd", grid=(npair, nt),
        in_specs=[pl.BlockSpec((T, 128), lambda p, i: (i, qb0 + p)),
                  pl.BlockSpec((S, 128), lambda p, i: (0, kb0 + p)),
                  pl.BlockSpec((S, 128), lambda p, i: (0, vb0 + p)),
                  pl.BlockSpec((nt, T, T), lambda p, i: (0, 0, 0)), pl.BlockSpec(memory_space=pl.ANY)],
        out_specs=(pl.BlockSpec((T, 128), lambda p, i: (i, GDN_WIDTH // 128 + p)),
                   pl.BlockSpec((None, None, 8, T), lambda p, i: (p, i, 0, 0))),
        out_shape=(jax.ShapeDtypeStruct(mix.shape, F32), jax.ShapeDtypeStruct((npair, nt, 8, T), F32)),
        input_output_aliases={4: 0},
        compiler_params=_cparams(("parallel", "parallel")),
    )(proj, proj, proj, bias, mix)


def _attn_bwd(proj, mix, lse, d_mix):
    S = proj.shape[0]
    T = min(ATT_T, S)
    nt = S // T
    bias = _dil_bias(nt, T)
    scale = DIL_DIM ** -0.5
    npair = DIL_WIDTH // 128
    qb0, kb0, vb0 = P_QKVB // 128, (P_QKVB + DIL_WIDTH) // 128, (P_QKVB + 2 * DIL_WIDTH) // 128

    def body(q_ref, k_ref, v_ref, o_ref, lse_ref, do_ref, b_ref, dq_ref, dk_ref, dv_ref, dq_scr):
        j = pl.program_id(1)

        @pl.when(j == 0)
        def _():
            dq_scr[...] = jnp.zeros_like(dq_scr)

        kt = k_ref[...].astype(_MXU)
        vt = v_ref[...].astype(_MXU)
        ones = jnp.ones((8, DIL_DIM), F32)

        def step(i, carry):
            rows = pl.ds(pl.multiple_of(i * T, T), T)
            qs = (q_ref[rows, :] * scale).astype(_MXU)
            dov = do_ref[rows, :]
            prod = dov * o_ref[rows, :]
            lsev = lse_ref[i]
            dob = dov.astype(_MXU)
            bt = b_ref[i - j]
            out = []
            dqs = []
            for hh in range(2):
                dk, dv = carry[hh]
                sl = slice(hh * DIL_DIM, (hh + 1) * DIL_DIM)
                s = lax.dot_general(kt[:, sl], qs[:, sl], (_NT, ((), ())), preferred_element_type=F32) + bt
                p = jnp.exp(s - lsev[hh:hh + 1, :])
                delta = lax.dot_general(ones, prod[:, sl], (_NT, ((), ())), precision=_HI, preferred_element_type=F32)[0:1, :]
                dp = lax.dot_general(vt[:, sl], dob[:, sl], (_NT, ((), ())), preferred_element_type=F32)
                ds = (p * (dp - delta)).astype(_MXU)
                dv = dv + lax.dot_general(p.astype(_MXU), dob[:, sl], (_NN, ((), ())), preferred_element_type=F32)
                dk = dk + lax.dot_general(ds, qs[:, sl], (_NN, ((), ())), preferred_element_type=F32)
                dqs.append(lax.dot_general(ds, kt[:, sl], (_TN, ((), ())), preferred_element_type=F32) * scale)
                out.append((dk, dv))
            dq_scr[rows, :] += jnp.concatenate(dqs, axis=1)
            return tuple(out)

        init = tuple((jnp.zeros((T, DIL_DIM), F32), jnp.zeros((T, DIL_DIM), F32)) for _ in range(2))
        res = lax.fori_loop(j, nt, step, init)
        dk_ref[...] = jnp.concatenate([res[0][0], res[1][0]], axis=1).astype(dk_ref.dtype)
        dv_ref[...] = jnp.concatenate([res[0][1], res[1][1]], axis=1).astype(dv_ref.dtype)

        @pl.when(j == nt - 1)
        def _():
            dq_ref[...] = dq_scr[...].astype(dq_ref.dtype)

    full = lambda c0: pl.BlockSpec((S, 128), lambda p, j: (0, c0 + p))
    tile = lambda c0: pl.BlockSpec((T, 128), lambda p, j: (j, c0 + p))
    out3 = jax.ShapeDtypeStruct((S, DIL_WIDTH), _MXU)
    return pl.pallas_call(
        body, name="attn_bwd", grid=(npair, nt),
        in_specs=[full(qb0), tile(kb0), tile(vb0), full(GDN_WIDTH // 128),
                  pl.BlockSpec((None, nt, 8, T), lambda p, j: (p, 0, 0, 0)), full(GDN_WIDTH // 128),
                  pl.BlockSpec((nt, T, T), lambda p, j: (0, 0, 0))],
        out_specs=(full(0), tile(0), tile(0)),
        out_shape=(out3, out3, out3),
        scratch_shapes=[pltpu.VMEM((S, 128), F32)],
        compiler_params=_cparams(("parallel", "arbitrary")),
    )(proj, proj, proj, mix, lse, d_mix, bias)


def _ffn_act(up, cw):
    S, Cc = up.shape[0], up.shape[1] // 2
    T, tc = _pick_tile(S, 256), _pick_tile(Cc, 1536)
    r8 = T // 8
    nct = Cc // tc

    def body(g_ref, gp_ref, u_ref, up_ref, wg_ref, wu_ref, o_ref):
        keep = jnp.where(pl.program_id(1) == 0, 0.0, 1.0)
        cg = _conv_taps(jnp.concatenate([gp_ref[...] * keep, g_ref[...]], axis=0), wg_ref[...], FFN_CONV, T)
        cu = _conv_taps(jnp.concatenate([up_ref[...] * keep, u_ref[...]], axis=0), wu_ref[...], FFN_CONV, T)
        o_ref[...] = (_silu(cg) * cu).astype(o_ref.dtype)

    cur = lambda o: pl.BlockSpec((T, tc), lambda j, i: (i, j + o))
    prev = lambda o: pl.BlockSpec((8, tc), lambda j, i: (jnp.maximum(i * r8 - 1, 0), j + o))
    wsp = lambda o: pl.BlockSpec((FFN_CONV, tc), lambda j, i: (0, j + o))
    return pl.pallas_call(
        body, name="ffn_act", grid=(nct, S // T),
        in_specs=[cur(0), prev(0), cur(nct), prev(nct), wsp(0), wsp(nct)], out_specs=cur(0),
        out_shape=jax.ShapeDtypeStruct((S, Cc), _MXU),
        compiler_params=_cparams(("parallel", "parallel")),
    )(up, up, up, up, cw, cw)


def _ffn_act_bwd(d_act, up, cw):
    S, Cc = up.shape[0], up.shape[1] // 2
    T, tc = _pick_tile(S, 256), _pick_tile(Cc, 1536)
    r8 = T // 8
    nt = S // T
    nct = Cc // tc
    K = FFN_CONV

    def body(da_ref, dan_ref, g_ref, gp_ref, gn_ref, u_ref, up_ref, un_ref, wg_ref, wu_ref,
             dg_ref, du_ref, dwg_ref, dwu_ref):
        i = pl.program_id(1)
        keep_p = jnp.where(i == 0, 0.0, 1.0)
        keep_n = jnp.where(i == nt - 1, 0.0, 1.0)
        wg, wu = wg_ref[...], wu_ref[...]
        xg = jnp.concatenate([gp_ref[...] * keep_p, g_ref[...], gn_ref[...] * keep_n], axis=0)
        xu = jnp.concatenate([up_ref[...] * keep_p, u_ref[...], un_ref[...] * keep_n], axis=0)
        cg = _conv_taps(xg, wg, K, T + 8)
        cu = _conv_taps(xu, wu, K, T + 8)
        da = jnp.concatenate([da_ref[...], dan_ref[...] * keep_n], axis=0)
        sg = jax.nn.sigmoid(cg)
        d_cg = da * cu * (sg * (1.0 + cg * (1.0 - sg)))
        d_cu = da * (cg * sg)
        dg_ref[...] = _conv_taps_t(d_cg, wg, K, T).astype(dg_ref.dtype)
        du_ref[...] = _conv_taps_t(d_cu, wu, K, T).astype(du_ref.dtype)

        @pl.when(i == 0)
        def _():
            dwg_ref[...] = jnp.zeros_like(dwg_ref)
            dwu_ref[...] = jnp.zeros_like(dwu_ref)

        for k in range(K):
            dwg_ref[k:k + 1, :] += jnp.sum(d_cg[0:T, :] * _shifted(xg, (K - 1) - k, 8, T), axis=0, keepdims=True)
            dwu_ref[k:k + 1, :] += jnp.sum(d_cu[0:T, :] * _shifted(xu, (K - 1) - k, 8, T), axis=0, keepdims=True)

    cur = lambda o: pl.BlockSpec((T, tc), lambda j, i: (i, j + o))
    prev = lambda o: pl.BlockSpec((8, tc), lambda j, i: (jnp.maximum(i * r8 - 1, 0), j + o))
    nxt = lambda o: pl.BlockSpec((8, tc), lambda j, i: (jnp.minimum((i + 1) * r8, S // 8 - 1), j + o))
    wsp = lambda o: pl.BlockSpec((K, tc), lambda j, i: (0, j + o))
    return pl.pallas_call(
        body, name="ffn_act_bwd", grid=(nct, nt),
        in_specs=[cur(0), nxt(0), cur(0), prev(0), nxt(0), cur(nct), prev(nct), nxt(nct), wsp(0), wsp(nct)],
        out_specs=(cur(0), cur(0), wsp(0), wsp(0)),
        out_shape=(jax.ShapeDtypeStruct((S, Cc), _MXU), jax.ShapeDtypeStruct((S, Cc), _MXU),
                   jax.ShapeDtypeStruct((K, Cc), F32), jax.ShapeDtypeStruct((K, Cc), F32)),
        compiler_params=_cparams(("parallel", "arbitrary")),
    )(d_act, d_act, up, up, up, up, up, up, cw, cw)


def _local_step(x, tgt, h1, n1w, n2w, fnw, gp, gnw, wp, conv_w, fcw, rest_weights, early_grads):
    proj = _mm(h1, wp, "nn", name="proj")
    u_v, w_k, q_dec, k_end, attn, tinv, g_end = _gdn_pre(proj, conv_w, gp)
    mix, states = _gdn_scan(u_v, w_k, q_dec, k_end, attn, g_end, proj, gnw)
    mix, lse = _attn_fwd(proj, mix)
    w_out, w_up4, w_down = rest_weights([mix])
    x2 = _mm(mix, w_out, "nn", residual=x, name="outproj")
    h2 = _rmsnorm_fwd(x2, n2w, "norm2")
    up = _mm(h2, w_up4, "nn", b_blocks=True, name="up")
    act = _ffn_act(up, fcw)
    x3 = _mm(act, w_down, "nn", residual=x2, name="down")
    loss, dx3, d_fnw = _loss_head(x3, fnw, tgt, "loss_head")
    d_act = _mm(dx3, w_down, "nt", name="d_act")
    d_wdown = _mm(act, dx3, "tn", name="d_wdown")
    d_upg, d_upu, d_fcwg, d_fcwu = _ffn_act_bwd(d_act, up, fcw)
    d_wup = _mm(h2, d_upg, "tn", place=("blocks", N_CHIPS, 0), tn=w_up4.shape[2], name="d_wgate")
    d_wup = _mm(h2, d_upu, "tn", place=("blocks", N_CHIPS, N_CHIPS // 2), tn=w_up4.shape[2], into=d_wup, name="d_wup")
    token = early_grads[0](d_wup, d_wdown)
    d_h2 = _mm_nt_blocks([d_upg, d_upu], w_up4, "d_h2")
    dx2, d_n2w = _rmsnorm_bwd(d_h2, x2, n2w + token[0:1, 0:1], dx3, "norm2_bwd")
    token = early_grads[1](dx2)
    d_mix = _mm(dx2, w_out, "nt", name="d_mix")
    d_wout = _mm(mix, dx2, "tn", name="d_wout")
    dq_b, dk_b, dv_b = _attn_bwd(proj, mix, lse, d_mix)
    d_uv, d_wk, d_qd, d_ke, d_at, d_ge, d_z, d_gnw = _gdn_scan_bwd(u_v, w_k, q_dec, k_end, attn, g_end, proj,
                                                                   gnw + token[0:1, 0:1], states, d_mix)
    d_pre, d_ba, d_gp = _gdn_post(proj, conv_w, gp, tinv, u_v, w_k, d_uv, d_wk, d_qd, d_ke, d_at, d_ge)
    d_qkva, d_convw = _conv_bwd(d_pre, proj, 0, conv_w, GDN_CONV, "gdn_conv_bwd", 512)
    d_proj = jnp.concatenate([d_qkva, d_z.astype(_MXU), dq_b, dk_b, dv_b, d_ba.astype(_MXU),
                              jnp.zeros((x.shape[0], P_COLS - P_BA - 128), _MXU)], axis=1)
    d_wp = _mm(h1, d_proj, "tn", name="d_wp")
    token = early_grads[2](d_wp, d_wout)
    d_h1 = _mm(d_proj, wp, "nt", name="d_h1")
    dx, d_n1w = _rmsnorm_bwd(d_h1, x, n1w + token[0:1, 0:1], dx2, "norm1_bwd")
    grads = dict(wp=d_wp, conv_w=d_convw, w_out=d_wout, w_up=d_wup, fcw_g=d_fcwg, fcw_u=d_fcwu, w_down=d_wdown,
                 n1w=d_n1w, n2w=d_n2w, fnw=d_fnw, gp=d_gp, gnw=d_gnw)
    return loss, dx, grads


_HBM = pl.BlockSpec(memory_space=pltpu.HBM)


def _pos():
    return lax.axis_index("x"), lax.axis_index("y"), lax.axis_index("c")


def _other_chips(x, y):
    return [(1 - x, y), (x, 1 - y), (1 - x, 1 - y)]


def _halvable(shape):
    return shape[0] % 32 == 0


def _rows_of_half(shape, half):
    if not _halvable(shape):
        return pl.ds(0, shape[0])
    return pl.ds(pl.multiple_of(half * (shape[0] // 2), 16), shape[0] // 2)


_SEM = pl.BlockSpec(memory_space=pltpu.SEMAPHORE)
_ANY = pl.BlockSpec(memory_space=pl.ANY)
_DATAFLOW = pltpu.SideEffectType.DATAFLOW_SIDE_EFFECTING


def _in_hbm(a):
    return pltpu.with_memory_space_constraint(a, pltpu.HBM)


def _halves_copy(src_refs, land_refs, send_sems, recv_sems, shapes, a, j, block, x, y, c):
    px, py = _other_chips(x, y)[j]
    rows = _rows_of_half(shapes[a], c)
    return pltpu.make_async_remote_copy(
        src_ref=src_refs[a].at[rows, :], dst_ref=land_refs[a].at[block, rows, :], send_sem=send_sems.at[3 * a + j],
        recv_sem=recv_sems.at[3 * a + j], device_id=(px, py, c), device_id_type=MESH)


def _gather_halves_start(shards, after, name):
    n = len(shards)
    shapes = [s.shape for s in shards]

    def body(*refs):
        ins, lands = refs[:n], refs[n:2 * n]
        send_sems, recv_sems = refs[2 * n + 1], refs[2 * n + 2]
        token = refs[-1]
        x, y, c = _pos()
        q = 2 * x + y
        for a in range(n):
            for j in range(3):
                _halves_copy(ins, lands, send_sems, recv_sems, shapes, a, j, q, x, y, c).start()
        token[...] = jnp.zeros_like(token)

    land_shapes = [(N_CHIPS,) + s.shape for s in shards]
    return pl.pallas_call(
        body, name=name,
        out_shape=(pltpu.SemaphoreType.DMA((3 * n,)), pltpu.SemaphoreType.DMA((3 * n,)),
                   *[pltpu.HBM(s.shape, s.dtype) for s in shards],
                   *[pltpu.HBM(ls, s.dtype) for ls, s in zip(land_shapes, shards)],
                   jax.ShapeDtypeStruct((8, 128), F32)),
        in_specs=[_HBM] * (2 * n) + [_ANY],
        out_specs=(_SEM, _SEM, *[_HBM] * (2 * n), pl.BlockSpec(memory_space=pltpu.VMEM)),
        input_output_aliases={a: 2 + a for a in range(2 * n)},
        compiler_params=pltpu.CompilerParams(has_side_effects=_DATAFLOW),
    )(*[_in_hbm(s) for s in shards], *[_in_hbm(lax.empty(ls, s.dtype)) for ls, s in zip(land_shapes, shards)], after)


def _gather_halves_wait(started, after, name):
    send_sems, recv_sems, *thru = started
    n = len(thru) // 2
    shapes = [t.shape for t in thru[:n]]

    def body(*refs):
        ins, lands = refs[:n], refs[n:2 * n]
        send_sems, recv_sems = refs[2 * n], refs[2 * n + 1]
        x, y, c = _pos()
        q = 2 * x + y
        chips = _other_chips(x, y)
        for a in range(n):
            for j, (px, py) in enumerate(chips):
                _halves_copy(ins, lands, send_sems, recv_sems, shapes, a, j, q, x, y, c).wait_send()
                _halves_copy(ins, lands, send_sems, recv_sems, shapes, a, j, 2 * px + py, x, y, c).wait_recv()

    outs = pl.pallas_call(
        body, name=name, out_shape=[pltpu.HBM(t.shape, t.dtype) for t in thru],
        in_specs=[_HBM] * (2 * n) + [_SEM, _SEM] + [_ANY] * len(after), out_specs=[_HBM] * (2 * n),
        input_output_aliases={a: a for a in range(2 * n)},
        compiler_params=pltpu.CompilerParams(has_side_effects=_DATAFLOW),
    )(*thru, send_sems, recv_sems, *after)
    return outs[:n], outs[n:]


def _sibling_fill(gathered, name):
    big = [a for a, g in enumerate(gathered) if _halvable(g.shape[1:])]
    n = len(gathered)

    def body(*refs):
        ins, outs = refs[:n], refs[n:2 * n]
        send_sems, recv_sems = refs[2 * n:]
        x, y, c = _pos()
        chips = _other_chips(x, y)

        def copy(k, j, half):
            a = big[k]
            px, py = chips[j]
            rows = _rows_of_half(gathered[a].shape[1:], half)
            return pltpu.make_async_remote_copy(
                src_ref=ins[a].at[2 * px + py, rows, :], dst_ref=outs[a].at[2 * px + py, rows, :],
                send_sem=send_sems.at[3 * k + j], recv_sem=recv_sems.at[3 * k + j],
                device_id=(x, y, 1 - c), device_id_type=MESH)

        sends = [copy(k, j, c) for k in range(len(big)) for j in range(3)]
        for cp in sends:
            cp.start()
        for k in range(len(big)):
            for j in range(3):
                copy(k, j, 1 - c).wait_recv()
        for cp in sends:
            cp.wait_send()

    return pl.pallas_call(
        body, name=name, in_specs=[_HBM] * n, out_specs=[_HBM] * n,
        out_shape=[jax.ShapeDtypeStruct(g.shape, g.dtype) for g in gathered],
        input_output_aliases={a: a for a in range(n)},
        scratch_shapes=[pltpu.SemaphoreType.DMA((3 * len(big),)), pltpu.SemaphoreType.DMA((3 * len(big),))],
    )(*gathered)


def _place_own(shards, gathered, cq, name):
    n = len(shards)
    steps = 4

    def body(cq_ref, *refs):
        for a in range(n):
            refs[2 * n + a][...] = refs[a][...]

    def tile(shape):
        return shape[0] // steps if _halvable(shape) else shape[0]

    in_specs = [pl.BlockSpec((tile(s.shape), s.shape[1]), (lambda i, s_: (i, 0)) if _halvable(s.shape) else (lambda i, s_: (0, 0)))
                for s in shards]
    in_specs += [pl.BlockSpec(memory_space=pl.ANY)] * n
    out_specs = [pl.BlockSpec((None, tile(s.shape), s.shape[1]),
                              (lambda i, s_: (s_[1], i, 0)) if _halvable(s.shape) else (lambda i, s_: (s_[1], 0, 0)))
                 for s in shards]
    gs = pltpu.PrefetchScalarGridSpec(num_scalar_prefetch=1, grid=(steps,), in_specs=in_specs, out_specs=out_specs)
    return pl.pallas_call(
        body, name=name, grid_spec=gs, out_shape=[jax.ShapeDtypeStruct(g.shape, g.dtype) for g in gathered],
        input_output_aliases={1 + n + a: a for a in range(n)},
        compiler_params=_cparams(("arbitrary",)),
    )(cq, *shards, *gathered)


def _half_rows(ref, c, rh):
    return ref.at[:, pl.ds(pl.multiple_of(c * rh, 8), rh), :]


def _chips_copy(src_refs, land_refs, send_sems, recv_sems, a, j, x, y, c):
    px, py = _other_chips(x, y)[j]
    return pltpu.make_async_remote_copy(src_ref=src_refs[a].at[2 * px + py], dst_ref=land_refs[a].at[j],
                                        send_sem=send_sems.at[3 * a + j], recv_sem=recv_sems.at[3 * a + j],
                                        device_id=(px, py, c), device_id_type=MESH)


def _grad_chips_start(parts, name):
    n = len(parts)

    def body(*refs):
        ins, lands = refs[:n], refs[n:2 * n]
        send_sems, recv_sems = refs[2 * n], refs[2 * n + 1]
        token = refs[-1]
        x, y, c = _pos()
        for a in range(n):
            for j in range(3):
                _chips_copy(ins, lands, send_sems, recv_sems, a, j, x, y, c).start()
        token[...] = jnp.zeros_like(token)

    land_shapes = [(3,) + p.shape[1:] for p in parts]
    return pl.pallas_call(
        body, name=name,
        out_shape=(pltpu.SemaphoreType.DMA((3 * n,)), pltpu.SemaphoreType.DMA((3 * n,)),
                   *[pltpu.HBM(p.shape, p.dtype) for p in parts],
                   *[pltpu.HBM(ls, p.dtype) for ls, p in zip(land_shapes, parts)],
                   jax.ShapeDtypeStruct((8, 128), F32)),
        in_specs=[_HBM] * (2 * n),
        out_specs=(_SEM, _SEM, *[_HBM] * (2 * n), pl.BlockSpec(memory_space=pltpu.VMEM)),
        input_output_aliases={a: 2 + a for a in range(2 * n)},
        compiler_params=pltpu.CompilerParams(has_side_effects=_DATAFLOW),
    )(*[_in_hbm(p) for p in parts], *[_in_hbm(lax.empty(ls, p.dtype)) for ls, p in zip(land_shapes, parts)])


def _grad_chips_wait(started, after, name):
    send_sems, recv_sems, *thru = started
    n = len(thru) // 2

    def body(*refs):
        ins, lands = refs[:n], refs[n:2 * n]
        send_sems, recv_sems = refs[2 * n], refs[2 * n + 1]
        x, y, c = _pos()
        for a in range(n):
            for j in range(3):
                cp = _chips_copy(ins, lands, send_sems, recv_sems, a, j, x, y, c)
                cp.wait_send()
                cp.wait_recv()

    outs = pl.pallas_call(
        body, name=name, out_shape=[pltpu.HBM(t.shape, t.dtype) for t in thru],
        in_specs=[_HBM] * (2 * n) + [_SEM, _SEM] + [_ANY] * len(after), out_specs=[_HBM] * (2 * n),
        input_output_aliases={a: a for a in range(2 * n)},
        compiler_params=pltpu.CompilerParams(has_side_effects=_DATAFLOW),
    )(*thru, send_sems, recv_sems, *after)
    return outs[n:]


def _sibling_copy(src_refs, land_refs, send_sems, recv_sems, rhs, a, c, x, y):
    return pltpu.make_async_remote_copy(src_ref=_half_rows(src_refs[a], 1 - c, rhs[a]), dst_ref=land_refs[a],
                                        send_sem=send_sems.at[a], recv_sem=recv_sems.at[a],
                                        device_id=(x, y, 1 - c), device_id_type=MESH)


def _grad_sibling_start(fams, name):
    n = len(fams)
    rhs = [f.shape[1] // 2 for f in fams]

    def body(*refs):
        ins, lands = refs[:n], refs[n:2 * n]
        send_sems, recv_sems = refs[2 * n], refs[2 * n + 1]
        token = refs[-1]
        x, y, c = _pos()
        for a in range(n):
            _sibling_copy(ins, lands, send_sems, recv_sems, rhs, a, c, x, y).start()
        token[...] = jnp.zeros_like(token)

    land_shapes = [(f.shape[0], f.shape[1] // 2, f.shape[2]) for f in fams]
    return pl.pallas_call(
        body, name=name,
        out_shape=(pltpu.SemaphoreType.DMA((n,)), pltpu.SemaphoreType.DMA((n,)),
                   *[pltpu.HBM(f.shape, f.dtype) for f in fams],
                   *[pltpu.HBM(ls, f.dtype) for ls, f in zip(land_shapes, fams)],
                   jax.ShapeDtypeStruct((8, 128), F32)),
        in_specs=[_HBM] * (2 * n),
        out_specs=(_SEM, _SEM, *[_HBM] * (2 * n), pl.BlockSpec(memory_space=pltpu.VMEM)),
        input_output_aliases={a: 2 + a for a in range(2 * n)},
        compiler_params=pltpu.CompilerParams(has_side_effects=_DATAFLOW),
    )(*[_in_hbm(f) for f in fams], *[_in_hbm(lax.empty(ls, f.dtype)) for ls, f in zip(land_shapes, fams)])


def _grad_sibling_wait(started, after, name):
    send_sems, recv_sems, *thru = started
    n = len(thru) // 2
    rhs = [t.shape[1] // 2 for t in thru[:n]]

    def body(*refs):
        ins, lands = refs[:n], refs[n:2 * n]
        send_sems, recv_sems = refs[2 * n], refs[2 * n + 1]
        x, y, c = _pos()
        for a in range(n):
            cp = _sibling_copy(ins, lands, send_sems, recv_sems, rhs, a, c, x, y)
            cp.wait_send()
            cp.wait_recv()

    outs = pl.pallas_call(
        body, name=name, out_shape=[pltpu.HBM(t.shape, t.dtype) for t in thru],
        in_specs=[_HBM] * (2 * n) + [_SEM, _SEM] + [_ANY] * len(after), out_specs=[_HBM] * (2 * n),
        input_output_aliases={a: a for a in range(2 * n)},
        compiler_params=pltpu.CompilerParams(has_side_effects=_DATAFLOW),
    )(*thru, send_sems, recv_sems, *after)
    return outs[:n], outs[n:]


def _grad_share(fulls, name, small=None):
    n = len(fulls)
    ns = 0 if small is None else 1
    rhs = [f.shape[0] // 2 for f in fulls]

    def body(*refs):
        ins, outs = refs[:n], refs[n + ns:2 * n + ns]
        send_sems, recv_sems = refs[2 * (n + ns)], refs[2 * (n + ns) + 1]
        x, y, c = _pos()

        def copy(a, half):
            rows = pl.ds(pl.multiple_of(half * rhs[a], 8), rhs[a])
            return pltpu.make_async_remote_copy(src_ref=ins[a].at[rows, :], dst_ref=outs[a].at[rows, :],
                                                send_sem=send_sems.at[7 * ns + a], recv_sem=recv_sems.at[7 * ns + a],
                                                device_id=(x, y, 1 - c), device_id_type=MESH)

        sends = [copy(a, c) for a in range(n)]
        for cp in sends:
            cp.start()
        if ns:
            small_ref, all_ref = refs[n], refs[2 * n + 1]
            me = 4 * x + 2 * y + c

            def peer(r):
                dx, dy, dc = (r >> 2) & 1, (r >> 1) & 1, r & 1
                return (x if dx == 0 else 1 - x), (y if dy == 0 else 1 - y), (c if dc == 0 else 1 - c)

            def small_copy(r, slot):
                return pltpu.make_async_remote_copy(src_ref=small_ref, dst_ref=all_ref.at[slot], send_sem=send_sems.at[r - 1],
                                                    recv_sem=recv_sems.at[r - 1], device_id=peer(r), device_id_type=MESH)

            smalls = [small_copy(r, me) for r in range(1, 8)]
            for cp in smalls:
                cp.start()
            for r in range(1, 8):
                px, py, pc = peer(r)
                small_copy(r, 4 * px + 2 * py + pc).wait_recv()
            sends = sends + smalls
        for a in range(n):
            copy(a, 1 - c).wait_recv()
        for cp in sends:
            cp.wait_send()

    return pl.pallas_call(
        body, name=name, in_specs=[_HBM] * (n + ns), out_specs=[_HBM] * (n + ns),
        out_shape=[jax.ShapeDtypeStruct(f.shape, f.dtype) for f in fulls]
        + ([jax.ShapeDtypeStruct((8,) + small.shape, small.dtype)] if ns else []),
        input_output_aliases={a: a for a in range(n)},
        scratch_shapes=[pltpu.SemaphoreType.DMA((7 * ns + n,)), pltpu.SemaphoreType.DMA((7 * ns + n,))],
    )(*fulls, *([small] if ns else []))


def _add_sibling(own, recv, cq, name):
    nb, R, Cc = own.shape
    Rh = R // 2

    def body(cq_ref, a_ref, b_ref, o32_ref, o16_ref):
        s = a_ref[...] + b_ref[...]
        o32_ref[...] = s
        o16_ref[...] = s.astype(o16_ref.dtype)

    sp = pl.BlockSpec((1, Rh, Cc), lambda b, s: (b, 0, 0))
    gs = pltpu.PrefetchScalarGridSpec(
        num_scalar_prefetch=1, grid=(nb,),
        in_specs=[pl.BlockSpec((1, Rh, Cc), lambda b, s: (b, s[0], 0)), sp], out_specs=[sp, sp])
    return pl.pallas_call(
        body, name=name, grid_spec=gs,
        out_shape=[jax.ShapeDtypeStruct((nb, Rh, Cc), F32), jax.ShapeDtypeStruct((nb, Rh, Cc), _MXU)],
        compiler_params=_cparams(("parallel",)),
    )(cq, own, recv)


def _add_chips(part32, recv3, cq, name):
    nb, Rh, Cc = part32.shape

    def body(cq_ref, a_ref, b_ref, o_ref):
        acc = a_ref[0]
        for j in range(3):
            acc = acc + b_ref[j].astype(F32)
        o_ref[...] = acc

    gs = pltpu.PrefetchScalarGridSpec(
        num_scalar_prefetch=1, grid=(1,),
        in_specs=[pl.BlockSpec((1, Rh, Cc), lambda i, s: (s[1], 0, 0)), pl.BlockSpec((3, Rh, Cc), lambda i, s: (0, 0, 0))],
        out_specs=pl.BlockSpec((Rh, Cc), lambda i, s: (s[0], 0)))
    return pl.pallas_call(
        body, name=name, grid_spec=gs, out_shape=jax.ShapeDtypeStruct((2 * Rh, Cc), F32),
        compiler_params=_cparams(("arbitrary",)),
    )(cq, part32, recv3)


def _sum_devices(small_all, small, me):
    def body(me_ref, all_ref, own_ref, o_ref):
        tot = None
        for d in range(8):
            term = jnp.where(me_ref[0] == d, own_ref[...], all_ref[d])
            tot = term if tot is None else tot + term
        o_ref[...] = tot

    gs = pltpu.PrefetchScalarGridSpec(
        num_scalar_prefetch=1, grid=(1,),
        in_specs=[pl.BlockSpec(small_all.shape, lambda i, s: (0, 0, 0)), pl.BlockSpec(small.shape, lambda i, s: (0, 0))],
        out_specs=pl.BlockSpec(small.shape, lambda i, s: (0, 0)))
    return pl.pallas_call(body, name="sum_devices", grid_spec=gs,
                          out_shape=jax.ShapeDtypeStruct(small.shape, F32))(me, small_all, small)


def _adamw(w, g, m, v, name):
    R, Cc = w.shape
    T = max([t for t in range(8, 257, 8) if R % t == 0], default=R)
    c1 = 1.0 / (1.0 - ADAM_B1 ** ADAM_STEP)
    c2 = 1.0 / (1.0 - ADAM_B2 ** ADAM_STEP)

    def body(w_ref, g_ref, m_ref, v_ref, d_ref, mo_ref, vo_ref):
        gv = g_ref[...]
        mn = ADAM_B1 * m_ref[...] + (1.0 - ADAM_B1) * gv
        vn = ADAM_B2 * v_ref[...] + (1.0 - ADAM_B2) * (gv * gv)
        mo_ref[...] = mn
        vo_ref[...] = vn
        d_ref[...] = -ADAM_LR * ((mn * c1) / (jnp.sqrt(vn * c2) + ADAM_EPS) + ADAM_WD * w_ref[...])

    sp = pl.BlockSpec((T, Cc), lambda i: (i, 0))
    sh = jax.ShapeDtypeStruct((R, Cc), F32)
    return pl.pallas_call(
        body, name=name, grid=(R // T,), in_specs=[sp] * 4, out_specs=(sp, sp, sp), out_shape=(sh, sh, sh),
        compiler_params=_cparams(("parallel",)),
    )(w, g, m, v)


SMALL_ROWS = 32
REPL_ROWS = 8


def _pad_lanes(v, n=D_MODEL):
    return jnp.pad(v, ((0, 0), (0, n - v.shape[1])))


def kernel(x, norm1_w, w_in, conv_qkv_w, a_log, dt_bias, gdn_norm_w, w_out, norm2_w, w_up, ffn_conv_w, w_down, final_norm_w, loss_target, m_norm1_w, m_w_in, m_conv_qkv_w, m_a_log, m_dt_bias, m_gdn_norm_w, m_w_out, m_norm2_w, m_w_up, m_ffn_conv_w, m_w_down, m_final_norm_w, v_norm1_w, v_w_in, v_conv_qkv_w, v_a_log, v_dt_bias, v_gdn_norm_w, v_w_out, v_norm2_w, v_w_up, v_ffn_conv_w, v_w_down, v_final_norm_w):
    c = lax.axis_index("c")
    q = 2 * lax.axis_index("x") + lax.axis_index("y")
    S = x.shape[1]
    cq = jnp.stack([c, q]).astype(jnp.int32)

    *in_started, in_token = _gather_halves_start([w_in[0].astype(_MXU), conv_qkv_w[0], ffn_conv_w[0]], x, "gather_in_start")
    w_in_l, m_w_in_l, v_w_in_l = (a + in_token[0:1, 0:1] for a in (w_in, m_w_in, v_w_in))
    h1 = _rmsnorm_fwd(x[0], norm1_w + in_token[0:1, 0:1], "norm1")
    rest = [(a[0] + in_token[0:1, 0:1]).astype(_MXU) for a in (w_out, w_up, w_down)]
    in_shards, got_in = _gather_halves_wait(in_started, [w_in_l, m_w_in_l, v_w_in_l, h1, *rest], "gather_in_wait")
    g_in, g_conv, g_fconv = _place_own(in_shards, _sibling_fill(got_in, "fill_in"), cq, "place_in")
    *rest_started, token = _gather_halves_start(rest, g_conv, "gather_rest_start")

    def rest_weights(after):
        shards, got = _gather_halves_wait(rest_started, after, "gather_rest_wait")
        got = _sibling_fill(got, "fill_rest")
        g_out, g_up, g_down = _place_own(shards, got, cq, "place_rest")
        return g_out.reshape(D_MODEL, D_MODEL), g_up, g_down.reshape(D_FF, D_MODEL)
    wp = _wp_assemble(g_in)
    conv_f = jnp.concatenate([g_conv[i] for i in range(N_CHIPS)], axis=1)
    fcw = jnp.concatenate([g_fconv[i] for i in range(N_CHIPS)], axis=1)
    gp = _pad_lanes(jnp.concatenate([a_log, dt_bias], axis=1), 128)
    fnw = final_norm_w[None, :]
    early = {}

    def early_sibling(d_wup, d_wdown):
        *early["sibling"], tok = _grad_sibling_start([d_wup, d_wdown.reshape(N_CHIPS, D_FF // N_CHIPS, D_MODEL)],
                                                     "grad_sibling_early_start")
        return tok

    def early_chips(dx2):
        fams_e, got_e = _grad_sibling_wait(early["sibling"], [dx2], "grad_sibling_early_wait")
        early["parts"] = [_add_sibling(f, r, cq, "add_sibling_" + nm) for f, r, nm in zip(fams_e, got_e, ("w_up", "w_down"))]
        *early["started"], tok = _grad_chips_start([p[1] for p in early["parts"]], "grad_chips_start")
        return tok

    def late_sibling(d_wp, d_wout):
        *early["late_sibling"], tok = _grad_sibling_start(
            [_win_split(d_wp), d_wout.reshape(N_CHIPS, D_MODEL // N_CHIPS, D_MODEL)], "grad_sibling_late_start")
        return tok

    early_grads = (early_sibling, early_chips, late_sibling)

    loss_l, dx, g = _local_step(x[0], loss_target[0], h1, norm1_w, norm2_w + token[0:1, 0:1], fnw, gp, gdn_norm_w, wp,
                                conv_f, fcw, rest_weights, early_grads)
    n_fc = FFN_CONV * D_FF

    def rows_of(v):
        flat = v.reshape(-1)
        return jnp.pad(flat, (0, -flat.shape[0] % D_MODEL)).reshape(-1, D_MODEL)

    gp_row = _pad_lanes(jnp.concatenate([g["gp"][0:1, 0:8], loss_l[0:1, 0:1]], axis=1))
    small = jnp.concatenate([g["n1w"], g["n2w"], g["fnw"], gp_row, _pad_lanes(g["gnw"]),
                             rows_of(g["conv_w"]), rows_of(g["fcw_g"]), rows_of(g["fcw_u"])], axis=0)
    small = jnp.pad(small, ((0, SMALL_ROWS - small.shape[0]), (0, 0)))
    fams, got = _grad_sibling_wait(early["late_sibling"], [dx], "grad_sibling_late_wait")
    parts = [_add_sibling(f, r, cq, "add_sibling_" + nm) for f, r, nm in zip(fams, got, ("w_in", "w_out"))]
    *late_started, late_token = _grad_chips_start([p[1] for p in parts], "grad_chips_late_start")
    got3_e = _grad_chips_wait(early["started"], [dx, g["wp"], late_token], "grad_chips_wait")
    g_w_up, g_w_down = _grad_share(
        [_add_chips(p[0], r3, cq, "add_chips_" + nm) for p, r3, nm in zip(early["parts"], got3_e, ("w_up", "w_down"))],
        "grad_share_early")
    big = {}

    def adamw_big(nm, w, gg, m, v):
        d_, m_, v_ = _adamw(w[0], gg, m[0], v[0], "adamw_" + nm)
        big[nm] = (gg[None], d_[None], m_[None], v_[None])

    adamw_big("w_up", w_up, g_w_up, m_w_up, v_w_up)
    adamw_big("w_down", w_down, g_w_down, m_w_down, v_w_down)
    got3 = _grad_chips_wait(late_started, [big["w_up"][1], big["w_down"][1]], "grad_chips_late_wait")
    g_w_in, g_w_out, small_all = _grad_share(
        [_add_chips(p[0], r3, cq, "add_chips_" + nm) for p, r3, nm in zip(parts, got3, ("w_in", "w_out"))],
        "grad_share_late", small)
    small_red = _sum_devices(small_all, small, (2 * q + c).astype(jnp.int32).reshape(1))
    loss = small_red[3, 8]
    r0 = 5
    r1 = r0 + GDN_CONV * 3 * GDN_WIDTH // D_MODEL
    r2 = r1 + -(-n_fc // D_MODEL)
    conv_red = small_red[r0:r1].reshape(GDN_CONV, 3 * GDN_WIDTH)
    fc_red = jnp.concatenate([small_red[r1:r2].reshape(-1)[:n_fc].reshape(FFN_CONV, D_FF),
                              small_red[r2:2 * r2 - r1].reshape(-1)[:n_fc].reshape(FFN_CONV, D_FF)], axis=1)
    g_conv_w = lax.dynamic_slice_in_dim(conv_red, q * (3 * GDN_WIDTH // N_CHIPS), 3 * GDN_WIDTH // N_CHIPS, axis=1)
    g_fconv_w = lax.dynamic_slice_in_dim(fc_red, q * (2 * D_FF // N_CHIPS), 2 * D_FF // N_CHIPS, axis=1)
    g_n1w, g_n2w, g_fnw = small_red[0:1], small_red[1:2], small_red[2]
    g_alog, g_dtb, g_gnw = small_red[3:4, 0:4], small_red[3:4, 4:8], small_red[4:5, 0:128]
    for nm, w, gg, m, v in (("w_in", w_in_l, g_w_in, m_w_in_l, v_w_in_l), ("conv_qkv_w", conv_qkv_w, g_conv_w, m_conv_qkv_w, v_conv_qkv_w),
                            ("w_out", w_out, g_w_out, m_w_out, v_w_out),
                            ("ffn_conv_w", ffn_conv_w, g_fconv_w, m_ffn_conv_w, v_ffn_conv_w)):
        adamw_big(nm, w, gg, m, v)

    def pack_small(n1, n2, fn, al, db, gn):
        return jnp.concatenate([n1, n2, fn[None, :], _pad_lanes(jnp.concatenate([al, db], axis=1)), _pad_lanes(gn),
                                jnp.zeros((REPL_ROWS - 5, D_MODEL), F32)], axis=0)

    sw = pack_small(norm1_w, norm2_w, final_norm_w, a_log, dt_bias, gdn_norm_w)
    sm = pack_small(m_norm1_w, m_norm2_w, m_final_norm_w, m_a_log, m_dt_bias, m_gdn_norm_w)
    sv = pack_small(v_norm1_w, v_norm2_w, v_final_norm_w, v_a_log, v_dt_bias, v_gdn_norm_w)
    sd, smn, svn = _adamw(sw, small_red[:REPL_ROWS], sm, sv, "adamw_small")

    def unpack_small(t):
        return dict(norm1_w=t[0:1], norm2_w=t[1:2], final_norm_w=t[2], a_log=t[3:4, 0:4], dt_bias=t[3:4, 4:8],
                    gdn_norm_w=t[4:5, 0:128])

    sg = dict(norm1_w=g_n1w, norm2_w=g_n2w, final_norm_w=g_fnw, a_log=g_alog, dt_bias=g_dtb, gdn_norm_w=g_gnw)
    sd, smn, svn = unpack_small(sd), unpack_small(smn), unpack_small(svn)
    names = ["norm1_w", "w_in", "conv_qkv_w", "a_log", "dt_bias", "gdn_norm_w", "w_out", "norm2_w", "w_up",
             "ffn_conv_w", "w_down", "final_norm_w"]
    grads = [big[n][0] if n in big else sg[n] for n in names]
    deltas = [big[n][1] if n in big else sd[n] for n in names]
    new_m = [big[n][2] if n in big else smn[n] for n in names]
    new_v = [big[n][3] if n in big else svn[n] for n in names]
    return (loss, dx[None], *grads, *deltas, *new_m, *new_v)
```

```python
import functools
import math

import numpy as np
import jax
import jax.numpy as jnp
from jax import lax
from jax.experimental import pallas as pl
from jax.experimental.pallas import tpu as pltpu

F32 = jnp.float32
BF16 = jnp.bfloat16
_MXU = jnp.bfloat16
_HI = lax.Precision.HIGHEST
EPS = 1e-6
V7X_VMEM_LIMIT = 56 * 1024 * 1024
MESH = pl.DeviceIdType.MESH

D_MODEL = 1024
GDN_HEADS, GDN_DIM, GDN_CHUNK, GDN_CONV = 4, 128, 64, 4
GDN_WIDTH = GDN_HEADS * GDN_DIM
DIL_HEADS, DIL_DIM = 8, 64
DIL_WIDTH = DIL_HEADS * DIL_DIM
D_FF, FFN_CONV = 2816, 3
IN_COLS = 3592
P_COLS = 3840
P_Z, P_QKVB, P_BA = 1536, 2048, 3584
ATT_T = 1024
ADAM_LR, ADAM_B1, ADAM_B2, ADAM_EPS, ADAM_WD, ADAM_STEP = 0.001, 0.9, 0.999, 1e-08, 0.01, 10
N_CHIPS = 4


def _cparams(sem=None, vmem=None):
    kw = {}
    if sem is not None:
        kw["dimension_semantics"] = sem
    if vmem is not None:
        kw["vmem_limit_bytes"] = vmem
    return pltpu.CompilerParams(**kw)


def _silu(x):
    return x * jax.nn.sigmoid(x)


def _pick_tile(n, cap):
    best = None
    for t in range(128, min(n, cap) + 1, 128):
        if n % t == 0:
            best = t
    return best or n


def _mm(a, b, mode, *, out_dtype=F32, residual=None, name, b_blocks=False, place=None, into=None, tn=None):
    if mode == "nn":
        M, K = a.shape
        N = b.shape[0] * b.shape[2] if b_blocks else b.shape[1]
    elif mode == "nt":
        (M, K), (N, _) = a.shape, b.shape
    else:
        (K, M), (_, N) = a.shape, b.shape
    tm = _pick_tile(M, 1024)
    tn = b.shape[2] if b_blocks else (tn or _pick_tile(N, 1536))

    def vmem(tm, tn):
        return 2 * (tm * K * a.dtype.itemsize + tn * K * b.dtype.itemsize
                    + tm * tn * (jnp.dtype(out_dtype).itemsize + (4 if residual is not None else 0))) + 3 * tm * tn * 4

    fixed_tn = b_blocks or (place is not None and place[0] == "blocks")
    while vmem(tm, tn) > 40 * 1024 * 1024:
        if (tm >= tn or fixed_tn) and tm % 256 == 0:
            tm //= 2
        elif tn % 256 == 0 and not fixed_tn:
            tn //= 2
        else:
            tm //= 2
    a_spec = pl.BlockSpec((K, tm), lambda j, i: (0, i)) if mode == "tn" else pl.BlockSpec((tm, K), lambda j, i: (i, 0))
    if b_blocks:
        b_spec = pl.BlockSpec((None, K, tn), lambda j, i: (j, 0, 0))
    else:
        b_spec = pl.BlockSpec((tn, K), lambda j, i: (j, 0)) if mode == "nt" else pl.BlockSpec((K, tn), lambda j, i: (0, j))
    r_spec = pl.BlockSpec((tm, tn), lambda j, i: (i, j))
    if place is None:
        o_spec, o_shape = r_spec, (M, N)
    elif place[0] == "rows":
        off = place[2] // tm
        o_spec, o_shape = pl.BlockSpec((tm, tn), lambda j, i: (i + off, j)), (place[1], N)
    else:
        off = place[2]
        o_spec, o_shape = pl.BlockSpec((None, tm, tn), lambda j, i: (j + off, i, 0)), (place[1], M, tn)
    dims = {"nn": (((1,), (0,)), ((), ())), "nt": (((1,), (1,)), ((), ())), "tn": (((0,), (0,)), ((), ()))}[mode]

    def body(*refs):
        a_ref, b_ref = refs[0], refs[1]
        o_ref = refs[-1]
        acc = lax.dot_general(a_ref[...].astype(_MXU), b_ref[...].astype(_MXU), dims, preferred_element_type=F32)
        if residual is not None:
            acc = acc + refs[2][...]
        o_ref[...] = acc.astype(out_dtype)

    ins, specs, alias = [a, b], [a_spec, b_spec], {}
    if residual is not None:
        ins.append(residual)
        specs.append(r_spec)
    if into is not None:
        alias = {len(ins): 0}
        ins.append(into)
        specs.append(pl.BlockSpec(memory_space=pl.ANY))
    return pl.pallas_call(
        body, name=name, grid=(N // tn, M // tm), in_specs=specs, out_specs=o_spec,
        out_shape=jax.ShapeDtypeStruct(o_shape, out_dtype), input_output_aliases=alias,
        compiler_params=_cparams(("parallel", "parallel"), V7X_VMEM_LIMIT),
    )(*ins)


def _mm_nt_blocks(a_list, b4, name):
    M = a_list[0].shape[0]
    nb, N, Kb = b4.shape
    tm, tn = _pick_tile(M, 512), _pick_tile(N, 512)

    def body(a0_ref, a1_ref, b_ref, o_ref):
        acc = None
        for blk in range(nb):
            a_ref = (a0_ref, a1_ref)[blk // 2]
            lo = (blk % 2) * Kb
            t = lax.dot_general(a_ref[:, lo:lo + Kb].astype(_MXU), b_ref[blk].astype(_MXU), (((1,), (1,)), ((), ())),
                                preferred_element_type=F32)
            acc = t if acc is None else acc + t
        o_ref[...] = acc

    a_spec = pl.BlockSpec((tm, 2 * Kb), lambda j, i: (i, 0))
    return pl.pallas_call(
        body, name=name, grid=(N // tn, M // tm),
        in_specs=[a_spec, a_spec, pl.BlockSpec((nb, tn, Kb), lambda j, i: (0, j, 0))],
        out_specs=pl.BlockSpec((tm, tn), lambda j, i: (i, j)), out_shape=jax.ShapeDtypeStruct((M, N), F32),
        compiler_params=_cparams(("parallel", "parallel"), V7X_VMEM_LIMIT),
    )(a_list[0], a_list[1], b4)


def _wp_assemble(g_in, after=()):
    nb, Dm, Wb = g_in.shape
    T = 256
    n_lo = P_QKVB - 2 * Wb

    def body(g_ref, *rest):
        g2 = g_ref[2]
        rest[-1][...] = jnp.concatenate(
            [g_ref[0], g_ref[1], g2[:, :n_lo], g2[:, n_lo + 8:], g_ref[3], g2[:, n_lo:n_lo + 8],
             jnp.zeros((T, P_COLS - P_BA - 8), g_in.dtype)], axis=1)

    return pl.pallas_call(
        body, name="wp_assemble", grid=(Dm // T,),
        in_specs=[pl.BlockSpec((nb, T, Wb), lambda i: (0, i, 0))] + [pl.BlockSpec(memory_space=pl.ANY)] * len(after),
        out_specs=pl.BlockSpec((T, P_COLS), lambda i: (i, 0)), out_shape=jax.ShapeDtypeStruct((Dm, P_COLS), g_in.dtype),
        compiler_params=_cparams(("parallel",)),
    )(g_in, *after)


def _win_split(d_wp):
    Dm = d_wp.shape[0]
    Wb = IN_COLS // N_CHIPS
    T = 256

    def body(x_ref, o_ref):
        xv = x_ref[...]
        o_ref[0] = xv[:, 0:Wb]
        o_ref[1] = xv[:, Wb:2 * Wb]
        o_ref[2] = jnp.concatenate([xv[:, 2 * Wb:P_QKVB], xv[:, P_BA:P_BA + 8], xv[:, P_QKVB:3 * Wb - 8]], axis=1)
        o_ref[3] = xv[:, 3 * Wb - 8:P_BA]

    return pl.pallas_call(
        body, name="win_split", grid=(Dm // T,), in_specs=[pl.BlockSpec((T, P_COLS), lambda i: (i, 0))],
        out_specs=pl.BlockSpec((N_CHIPS, T, Wb), lambda i: (0, i, 0)),
        out_shape=jax.ShapeDtypeStruct((N_CHIPS, Dm, Wb), F32), compiler_params=_cparams(("parallel",)),
    )(d_wp)


def _rmsnorm_fwd(x, w, name):
    S, D = x.shape
    T = _pick_tile(S, 512)

    def body(x_ref, w_ref, o_ref):
        xv = x_ref[...]
        rs = lax.rsqrt(jnp.mean(xv * xv, axis=-1, keepdims=True) + EPS)
        o_ref[...] = (xv * rs * w_ref[...]).astype(o_ref.dtype)

    return pl.pallas_call(
        body, name=name, grid=(S // T,),
        in_specs=[pl.BlockSpec((T, D), lambda i: (i, 0)), pl.BlockSpec((1, D), lambda i: (0, 0))],
        out_specs=pl.BlockSpec((T, D), lambda i: (i, 0)),
        out_shape=jax.ShapeDtypeStruct((S, D), _MXU),
        compiler_params=_cparams(("parallel",)),
    )(x, w)


def _rmsnorm_bwd(dh, x, w, dres, name):
    S, D = x.shape
    T = _pick_tile(S, 512)

    def body(dh_ref, x_ref, w_ref, dres_ref, dx_ref, dw_ref):
        xv = x_ref[...]
        rs = lax.rsqrt(jnp.mean(xv * xv, axis=-1, keepdims=True) + EPS)
        xn = xv * rs
        dhv = dh_ref[...]
        dxn = dhv * w_ref[...]
        dx_ref[...] = dres_ref[...] + rs * (dxn - xn * jnp.mean(dxn * xn, axis=-1, keepdims=True))

        @pl.when(pl.program_id(0) == 0)
        def _():
            dw_ref[...] = jnp.zeros_like(dw_ref)

        dw_ref[...] += jnp.sum(dhv * xn, axis=0, keepdims=True)

    row = pl.BlockSpec((T, D), lambda i: (i, 0))
    vec = pl.BlockSpec((1, D), lambda i: (0, 0))
    return pl.pallas_call(
        body, name=name, grid=(S // T,), in_specs=[row, row, vec, row], out_specs=(row, vec),
        out_shape=(jax.ShapeDtypeStruct((S, D), F32), jax.ShapeDtypeStruct((1, D), F32)),
        compiler_params=_cparams(("arbitrary",)),
    )(dh, x, w, dres)


def _loss_head(x3, w, tgt, name):
    S, D = x3.shape
    T = _pick_tile(S, 512)

    def body(x_ref, w_ref, t_ref, loss_ref, dx_ref, dw_ref):
        xv = x_ref[...]
        rs = lax.rsqrt(jnp.mean(xv * xv, axis=-1, keepdims=True) + EPS)
        xn = xv * rs
        err = xn * w_ref[...] - t_ref[...]
        dy = err * (1.0 / D)
        dxn = dy * w_ref[...]
        dx_ref[...] = rs * (dxn - xn * jnp.mean(dxn * xn, axis=-1, keepdims=True))

        @pl.when(pl.program_id(0) == 0)
        def _():
            dw_ref[...] = jnp.zeros_like(dw_ref)
            loss_ref[...] = jnp.zeros_like(loss_ref)

        dw_ref[...] += jnp.sum(dy * xn, axis=0, keepdims=True)
        part = jnp.sum(jnp.sum(err * err, axis=-1, keepdims=True), axis=0, keepdims=True) * (0.5 / D)
        loss_ref[...] += jnp.broadcast_to(part, loss_ref.shape)

    row = pl.BlockSpec((T, D), lambda i: (i, 0))
    vec = pl.BlockSpec((1, D), lambda i: (0, 0))
    return pl.pallas_call(
        body, name=name, grid=(S // T,), in_specs=[row, vec, row],
        out_specs=(pl.BlockSpec((8, 128), lambda i: (0, 0)), row, vec),
        out_shape=(jax.ShapeDtypeStruct((8, 128), F32), jax.ShapeDtypeStruct((S, D), F32), jax.ShapeDtypeStruct((1, D), F32)),
        compiler_params=_cparams(("arbitrary",)),
    )(x3, w, tgt)


def _shifted(ext, back, lo, n):
    if back == 0:
        return ext[lo:lo + n, :]
    return pltpu.roll(ext, back % ext.shape[0], 0)[lo:lo + n, :]


def _conv_windows(ext, K, T):
    return [_shifted(ext, (K - 1) - i, 8, T) for i in range(K)]


def _conv_taps(ext, w, K, T):
    out = None
    for i, win in enumerate(_conv_windows(ext, K, T)):
        term = win * w[i:i + 1, :]
        out = term if out is None else out + term
    return out


def _conv_taps_t(ext, w, K, T):
    out = None
    for i in range(K):
        term = _shifted(ext, i - (K - 1), 0, T) * w[i:i + 1, :]
        out = term if out is None else out + term
    return out


def _tri_masks(C):
    r = lax.broadcasted_iota(jnp.int32, (C, C), 0)
    c = lax.broadcasted_iota(jnp.int32, (C, C), 1)
    return r == c, r >= c, r > c, r <= c


_NN, _NT, _TN = ((1,), (0,)), ((1,), (1,)), ((0,), (0,))
_GDN_PASSES = dict(qk=1, inv=1, sol=1, scan=1, bwd=1)


def _bdot_raw(a, b, kind, passes):
    dims = ({"NN": ((2,), (1,)), "NT": ((2,), (2,)), "TN": ((1,), (1,))}[kind], ((0,), (0,)))
    if passes == 0:
        return lax.dot_general(a, b, dims, precision=_HI, preferred_element_type=F32)
    ah, bh = a.astype(BF16), b.astype(BF16)
    out = lax.dot_general(ah, bh, dims, preferred_element_type=F32)
    if passes == 3:
        al, bl = (a - ah.astype(F32)).astype(BF16), (b - bh.astype(F32)).astype(BF16)
        out = out + lax.dot_general(ah, bl, dims, preferred_element_type=F32) + lax.dot_general(al, bh, dims, preferred_element_type=F32)
    return out


@functools.partial(jax.custom_vjp, nondiff_argnums=(2, 3))
def _bdot(a, b, kind, passes):
    return _bdot_raw(a, b, kind, passes)


def _bdot_fwd(a, b, kind, passes):
    return _bdot_raw(a, b, kind, passes), (a, b)


def _bdot_bwd(kind, passes, res, ct):
    a, b = res
    if kind == "NN":
        return _bdot_raw(ct, b, "NT", passes), _bdot_raw(a, ct, "TN", passes)
    if kind == "NT":
        return _bdot_raw(ct, b, "NN", passes), _bdot_raw(ct, a, "TN", passes)
    return _bdot_raw(b, ct, "NT", passes), _bdot_raw(a, ct, "NN", passes)


_bdot.defvjp(_bdot_fwd, _bdot_bwd)


def _softplus(x):
    return jnp.maximum(x, 0.0) + jnp.log(1.0 + jnp.exp(-jnp.abs(x)))


def _gdn_stage1(cq, ck, cv, b_col, a_col, alog, dtb, dot=_bdot_raw):
    C = cq.shape[1]
    eye, incl, strict, incl_t = _tri_masks(C)
    qn = cq * lax.rsqrt(jnp.sum(cq * cq, axis=-1, keepdims=True) + EPS) * (GDN_DIM ** -0.5)
    kn = ck * lax.rsqrt(jnp.sum(ck * ck, axis=-1, keepdims=True) + EPS)
    beta = jax.nn.sigmoid(b_col)
    g = -jnp.exp(alog) * _softplus(a_col + dtb)
    g_row = jnp.sum(jnp.where(eye, g, 0.0), axis=1, keepdims=True)
    beta_row = jnp.sum(jnp.where(eye, beta, 0.0), axis=1, keepdims=True)
    gc_col = jnp.sum(jnp.where(incl, g_row, 0.0), axis=2, keepdims=True)
    gc_row = jnp.sum(jnp.where(incl_t, g, 0.0), axis=1, keepdims=True)
    dec = jnp.where(incl, jnp.exp(jnp.where(incl, gc_col - gc_row, 0.0)), 0.0)
    kk = dot(kn, kn, "NT", _GDN_PASSES["qk"])
    qk = dot(qn, kn, "NT", _GDN_PASSES["qk"])
    lmat = jnp.where(strict, dec * kk * beta_row, 0.0)
    attn = dec * qk * beta_row
    gam = jnp.exp(gc_col)
    gc_last = gc_col[:, C - 1:C, :]
    k_end = kn * (jnp.exp(gc_last - gc_col) * beta)
    return lmat, cv, gam * kn, gam * qn, attn, k_end, jnp.exp(gc_last)


def _tri_inv(lmat):
    C = lmat.shape[1]
    eye = _tri_masks(C)[0]
    ps = _GDN_PASSES["inv"]
    p = jnp.where(eye, 1.0, 0.0) - lmat
    lp = _bdot_raw(lmat, lmat, "NN", ps)
    n = int(math.log2(C))
    for s in range(1, n):
        p = p + _bdot_raw(p, lp, "NN", ps)
        if s < n - 1:
            lp = _bdot_raw(lp, lp, "NN", ps)
    return p


def _gated_norm(o, z, gnw):
    on = o * lax.rsqrt(jnp.mean(o * o, axis=-1, keepdims=True) + EPS) * gnw
    return on * _silu(z)


GDN_PG = 2
GDN_SG = 4


def _gdn_pairs(c, ba, gp, G):
    C, W, H = GDN_CHUNK, GDN_WIDTH, GDN_HEADS
    pairs = [(j, h) for j in range(G) for h in range(H)]
    cq, ck, cv = (jnp.stack([c[C * j:C * (j + 1), o + GDN_DIM * h:o + GDN_DIM * (h + 1)] for j, h in pairs]) for o in (0, W, 2 * W))
    b_col = jnp.stack([ba[C * j:C * (j + 1), h:h + 1] for j, h in pairs])
    a_col = jnp.stack([ba[C * j:C * (j + 1), H + h:H + h + 1] for j, h in pairs])
    alog = jnp.stack([gp[0:1, h:h + 1] for j, h in pairs])
    dtb = jnp.stack([gp[0:1, H + h:H + h + 1] for j, h in pairs])
    return pairs, (cq, ck, cv, b_col, a_col, alog, dtb)


def _gdn_pre_specs(S, G):
    C = GDN_CHUNK
    T = C * G
    return dict(
        cur=pl.BlockSpec((T, 3 * GDN_WIDTH), lambda i: (i, 0)),
        prev=pl.BlockSpec((8, 3 * GDN_WIDTH), lambda i: (jnp.maximum(i * (T // 8) - 1, 0), 0)),
        ba=pl.BlockSpec((T, 128), lambda i: (i, P_BA // 128)),
        cw=pl.BlockSpec((GDN_CONV, 3 * GDN_WIDTH), lambda i: (0, 0)),
        vec=pl.BlockSpec((1, 128), lambda i: (0, 0)),
        hd=pl.BlockSpec((GDN_HEADS, T, GDN_DIM), lambda i: (0, i, 0)),
        hc=pl.BlockSpec((GDN_HEADS, T, C), lambda i: (0, i, 0)),
        ge=pl.BlockSpec((G, GDN_HEADS, 8, 128), lambda i: (i, 0, 0, 0)),
    )


def _hd_shape(S, last=GDN_DIM):
    return jax.ShapeDtypeStruct((GDN_HEADS, S, last), F32)


def _gdn_pre(proj, conv_w, gp):
    S = proj.shape[0]
    C, G = GDN_CHUNK, GDN_PG
    nc = S // C
    sp = _gdn_pre_specs(S, G)

    def body(cur_ref, prev_ref, ba_ref, cw_ref, gp_ref, uv_ref, wk_ref, qd_ref, ke_ref, at_ref, ti_ref, ge_ref):
        prev = prev_ref[...] * jnp.where(pl.program_id(0) == 0, 0.0, 1.0)
        c = _silu(_conv_taps(jnp.concatenate([prev, cur_ref[...]], axis=0), cw_ref[...], GDN_CONV, C * G))
        pairs, args = _gdn_pairs(c, ba_ref[...], gp_ref[...], G)
        lmat, v, rk, q_dec, attn, k_end, g_end = _gdn_stage1(*args)
        t = _tri_inv(lmat)
        u_v = _bdot_raw(t, v, "NN", _GDN_PASSES["sol"])
        w_k = _bdot_raw(t, rk, "NN", _GDN_PASSES["sol"])
        for b, (j, h) in enumerate(pairs):
            rows = slice(C * j, C * (j + 1))
            uv_ref[h, rows, :] = u_v[b]
            wk_ref[h, rows, :] = w_k[b]
            qd_ref[h, rows, :] = q_dec[b]
            ke_ref[h, rows, :] = k_end[b]
            at_ref[h, rows, :] = attn[b]
            ti_ref[h, rows, :] = t[b]
            ge_ref[j, h] = jnp.broadcast_to(g_end[b], (8, 128))

    return pl.pallas_call(
        body, name="gdn_pre", grid=(nc // G,),
        in_specs=[sp["cur"], sp["prev"], sp["ba"], sp["cw"], sp["vec"]],
        out_specs=(sp["hd"], sp["hd"], sp["hd"], sp["hd"], sp["hc"], sp["hc"], sp["ge"]),
        out_shape=(_hd_shape(S), _hd_shape(S), _hd_shape(S), _hd_shape(S), _hd_shape(S, C), _hd_shape(S, C),
                   jax.ShapeDtypeStruct((nc, GDN_HEADS, 8, 128), F32)),
        compiler_params=_cparams(("parallel",)),
    )(proj, proj, proj, conv_w, gp)


def _gdn_scan_specs(S, G, rev):
    C = GDN_CHUNK
    T = C * G
    n = S // T
    ci = (lambda i: n - 1 - i) if rev else (lambda i: i)
    return dict(
        hd=pl.BlockSpec((GDN_HEADS, T, GDN_DIM), lambda i: (0, ci(i), 0)),
        hc=pl.BlockSpec((GDN_HEADS, T, C), lambda i: (0, ci(i), 0)),
        ge=pl.BlockSpec((G, GDN_HEADS, 8, 128), lambda i: (ci(i), 0, 0, 0)),
        z=pl.BlockSpec((T, GDN_WIDTH), lambda i: (ci(i), P_Z // GDN_WIDTH)),
        oa=pl.BlockSpec((T, GDN_WIDTH), lambda i: (ci(i), 0)),
        vec=pl.BlockSpec((1, 128), lambda i: (0, 0)),
        st=pl.BlockSpec((G, GDN_HEADS, GDN_DIM, GDN_DIM), lambda i: (ci(i), 0, 0, 0)),
    )


def _gdn_scan(u_v, w_k, q_dec, k_end, attn, g_end, proj, gnw):
    S = proj.shape[0]
    C, G = GDN_CHUNK, GDN_SG
    nc = S // C
    sp = _gdn_scan_specs(S, G, False)
    ps = _GDN_PASSES["scan"]

    def body(uv_ref, wk_ref, qd_ref, ke_ref, at_ref, ge_ref, z_ref, gnw_ref, oa_ref, st_ref, s_scr):
        @pl.when(pl.program_id(0) == 0)
        def _():
            s_scr[...] = jnp.zeros_like(s_scr)

        for j in range(G):
            rows = slice(C * j, C * (j + 1))
            st = s_scr[...]
            st_ref[j] = st
            u = uv_ref[:, rows, :] - _bdot_raw(wk_ref[:, rows, :], st, "NN", ps)
            o = _bdot_raw(qd_ref[:, rows, :], st, "NN", ps) + _bdot_raw(at_ref[:, rows, :], u, "NN", ps)
            s_scr[...] = ge_ref[j][:, 0:1, 0:1] * st + _bdot_raw(ke_ref[:, rows, :], u, "TN", ps)
            for h in range(GDN_HEADS):
                cols = slice(GDN_DIM * h, GDN_DIM * (h + 1))
                oa_ref[rows, cols] = _gated_norm(o[h], z_ref[rows, cols], gnw_ref[...])

    return pl.pallas_call(
        body, name="gdn_scan", grid=(nc // G,),
        in_specs=[sp["hd"], sp["hd"], sp["hd"], sp["hd"], sp["hc"], sp["ge"], sp["z"], sp["vec"]],
        out_specs=(sp["oa"], sp["st"]),
        out_shape=(jax.ShapeDtypeStruct((S, GDN_WIDTH + DIL_WIDTH), F32),
                   jax.ShapeDtypeStruct((nc, GDN_HEADS, GDN_DIM, GDN_DIM), F32)),
        scratch_shapes=[pltpu.VMEM((GDN_HEADS, GDN_DIM, GDN_DIM), F32)],
        compiler_params=_cparams(("arbitrary",)),
    )(u_v, w_k, q_dec, k_end, attn, g_end, proj, gnw)


def _gdn_scan_bwd(u_v, w_k, q_dec, k_end, attn, g_end, proj, gnw, states, d_oa):
    S = proj.shape[0]
    C, G = GDN_CHUNK, GDN_SG
    nc = S // C
    sp = _gdn_scan_specs(S, G, True)
    ps, pb = _GDN_PASSES["scan"], _GDN_PASSES["bwd"]

    def body(uv_ref, wk_ref, qd_ref, ke_ref, at_ref, ge_ref, z_ref, gnw_ref, st_ref, doa_ref,
             duv_ref, dwk_ref, dqd_ref, dke_ref, dat_ref, dge_ref, dz_ref, dgnw_ref, ds_scr):
        @pl.when(pl.program_id(0) == 0)
        def _():
            ds_scr[...] = jnp.zeros_like(ds_scr)
            dgnw_ref[...] = jnp.zeros_like(dgnw_ref)

        dgnw = jnp.zeros((1, 128), F32)
        for j in reversed(range(G)):
            rows = slice(C * j, C * (j + 1))
            st = st_ref[j]
            wk, qd, ke, at = wk_ref[:, rows, :], qd_ref[:, rows, :], ke_ref[:, rows, :], at_ref[:, rows, :]
            u = uv_ref[:, rows, :] - _bdot_raw(wk, st, "NN", ps)
            o = _bdot_raw(qd, st, "NN", ps) + _bdot_raw(at, u, "NN", ps)
            dos = []
            for h in range(GDN_HEADS):
                cols = slice(GDN_DIM * h, GDN_DIM * (h + 1))
                _, vjp2 = jax.vjp(_gated_norm, o[h], z_ref[rows, cols], gnw_ref[...])
                do_h, dz_h, dgn = vjp2(doa_ref[rows, cols])
                dz_ref[rows, cols] = dz_h
                dgnw = dgnw + dgn
                dos.append(do_h)
            do = jnp.stack(dos)
            ds_new = ds_scr[...]
            du = _bdot_raw(at, do, "TN", pb) + _bdot_raw(ke, ds_new, "NN", pb)
            duv_ref[:, rows, :] = du
            dat_ref[:, rows, :] = _bdot_raw(do, u, "NT", pb)
            dqd_ref[:, rows, :] = _bdot_raw(do, st, "NT", pb)
            dke_ref[:, rows, :] = _bdot_raw(u, ds_new, "NT", pb)
            dwk_ref[:, rows, :] = -_bdot_raw(du, st, "NT", pb)
            d_ge = jnp.sum(jnp.sum(st * ds_new, axis=2, keepdims=True), axis=1, keepdims=True)
            dge_ref[j] = jnp.broadcast_to(d_ge, (GDN_HEADS, 8, 128))
            ds_scr[...] = ge_ref[j][:, 0:1, 0:1] * ds_new + _bdot_raw(qd, do, "TN", pb) - _bdot_raw(wk, du, "TN", pb)
        dgnw_ref[...] += dgnw

    return pl.pallas_call(
        body, name="gdn_scan_bwd", grid=(nc // G,),
        in_specs=[sp["hd"], sp["hd"], sp["hd"], sp["hd"], sp["hc"], sp["ge"], sp["z"], sp["vec"], sp["st"], sp["oa"]],
        out_specs=(sp["hd"], sp["hd"], sp["hd"], sp["hd"], sp["hc"], sp["ge"], sp["oa"], sp["vec"]),
        out_shape=(_hd_shape(S), _hd_shape(S), _hd_shape(S), _hd_shape(S), _hd_shape(S, C),
                   jax.ShapeDtypeStruct((nc, GDN_HEADS, 8, 128), F32), jax.ShapeDtypeStruct((S, GDN_WIDTH), F32),
                   jax.ShapeDtypeStruct((1, 128), F32)),
        scratch_shapes=[pltpu.VMEM((GDN_HEADS, GDN_DIM, GDN_DIM), F32)],
        compiler_params=_cparams(("arbitrary",)),
    )(u_v, w_k, q_dec, k_end, attn, g_end, proj, gnw, states, d_oa)


def _gdn_post(proj, conv_w, gp, tinv, u_v, w_k, d_uv, d_wk, d_qd, d_ke, d_at, d_ge):
    S = proj.shape[0]
    C, G = GDN_CHUNK, GDN_PG
    nc = S // C
    sp = _gdn_pre_specs(S, G)
    pb = _GDN_PASSES["bwd"]

    def body(cur_ref, prev_ref, ba_ref, cw_ref, gp_ref, ti_ref, uv_ref, wk_ref, duv_ref, dwk_ref, dqd_ref, dke_ref,
             dat_ref, dge_ref, dpre_ref, dba_ref, dgp_ref):
        i = pl.program_id(0)

        @pl.when(i == 0)
        def _():
            dgp_ref[...] = jnp.zeros_like(dgp_ref)

        prev = prev_ref[...] * jnp.where(i == 0, 0.0, 1.0)
        pre = _conv_taps(jnp.concatenate([prev, cur_ref[...]], axis=0), cw_ref[...], GDN_CONV, C * G)
        sg = jax.nn.sigmoid(pre)
        dsilu = sg * (1.0 + pre * (1.0 - sg))
        pairs, args = _gdn_pairs(pre * sg, ba_ref[...], gp_ref[...], G)
        _, vjp1 = jax.vjp(functools.partial(_gdn_stage1, dot=_bdot), *args)

        def take(ref):
            return jnp.stack([ref[h, C * j:C * (j + 1), :] for j, h in pairs])

        t, u_v, w_k = take(ti_ref), take(uv_ref), take(wk_ref)
        d_v = _bdot_raw(t, take(duv_ref), "TN", pb)
        d_rk = _bdot_raw(t, take(dwk_ref), "TN", pb)
        d_l = -(_bdot_raw(d_v, u_v, "NT", pb) + _bdot_raw(d_rk, w_k, "NT", pb))
        d_ge = jnp.stack([dge_ref[j, h][0:1, 0:1] for j, h in pairs])
        dcq, dck, dcv, db, da, dalog, ddtb = vjp1((d_l, d_v, d_rk, take(dqd_ref), take(dat_ref), take(dke_ref), d_ge))
        lane = lax.broadcasted_iota(jnp.int32, (C, 128), 1)
        lane1 = lax.broadcasted_iota(jnp.int32, (1, 128), 1)
        dgp = jnp.zeros((1, 128), F32)
        for j in range(G):
            rows = slice(C * j, C * (j + 1))
            dba = jnp.zeros((C, 128), F32)
            for h in range(GDN_HEADS):
                b = GDN_HEADS * j + h
                for o_, dcx in ((0, dcq), (GDN_WIDTH, dck), (2 * GDN_WIDTH, dcv)):
                    cols = slice(o_ + GDN_DIM * h, o_ + GDN_DIM * (h + 1))
                    dpre_ref[rows, cols] = dcx[b] * dsilu[rows, cols]
                dba = dba + jnp.where(lane == h, db[b], 0.0) + jnp.where(lane == GDN_HEADS + h, da[b], 0.0)
                dgp = dgp + jnp.where(lane1 == h, dalog[b], 0.0) + jnp.where(lane1 == GDN_HEADS + h, ddtb[b], 0.0)
            dba_ref[rows, :] = dba
        dgp_ref[0:1, :] += dgp

    T = C * G
    return pl.pallas_call(
        body, name="gdn_post", grid=(nc // G,),
        in_specs=[sp["cur"], sp["prev"], sp["ba"], sp["cw"], sp["vec"], sp["hc"], sp["hd"], sp["hd"], sp["hd"], sp["hd"],
                  sp["hd"], sp["hd"], sp["hc"], sp["ge"]],
        out_specs=(sp["cur"], pl.BlockSpec((T, 128), lambda i: (i, 0)), pl.BlockSpec((8, 128), lambda i: (0, 0))),
        out_shape=(jax.ShapeDtypeStruct((S, 3 * GDN_WIDTH), F32), jax.ShapeDtypeStruct((S, 128), F32),
                   jax.ShapeDtypeStruct((8, 128), F32)),
        compiler_params=_cparams(("arbitrary",)),
    )(proj, proj, proj, conv_w, gp, tinv, u_v, w_k, d_uv, d_wk, d_qd, d_ke, d_at, d_ge)


def _conv_bwd(dpre, x, xcol0, w, K, name, tc):
    S, Cc = dpre.shape
    T = _pick_tile(S, 256)
    nt, ncol = S // T, Cc // tc
    xo = xcol0 // tc

    def body(d_ref, dn_ref, x_ref, xp_ref, w_ref, dx_ref, dw_ref):
        i = pl.program_id(1)
        dn = dn_ref[...] * jnp.where(i == nt - 1, 0.0, 1.0)
        dv = d_ref[...]
        ext_d = jnp.concatenate([dv, dn], axis=0)
        dx_ref[...] = _conv_taps_t(ext_d, w_ref[...], K, T).astype(dx_ref.dtype)
        xp = xp_ref[...] * jnp.where(i == 0, 0.0, 1.0)
        ext_x = jnp.concatenate([xp, x_ref[...]], axis=0)

        @pl.when(i == 0)
        def _():
            dw_ref[...] = jnp.zeros_like(dw_ref)

        for k in range(K):
            dw_ref[k:k + 1, :] += jnp.sum(dv * _shifted(ext_x, (K - 1) - k, 8, T), axis=0, keepdims=True)

    r8 = T // 8
    return pl.pallas_call(
        body, name=name, grid=(ncol, nt),
        in_specs=[pl.BlockSpec((T, tc), lambda j, i: (i, j)),
                  pl.BlockSpec((8, tc), lambda j, i: (jnp.minimum((i + 1) * r8, S // 8 - 1), j)),
                  pl.BlockSpec((T, tc), lambda j, i: (i, j + xo)),
                  pl.BlockSpec((8, tc), lambda j, i: (jnp.maximum(i * r8 - 1, 0), j + xo)),
                  pl.BlockSpec((K, tc), lambda j, i: (0, j))],
        out_specs=(pl.BlockSpec((T, tc), lambda j, i: (i, j)), pl.BlockSpec((K, tc), lambda j, i: (0, j))),
        out_shape=(jax.ShapeDtypeStruct((S, Cc), _MXU), jax.ShapeDtypeStruct((K, Cc), F32)),
        compiler_params=_cparams(("parallel", "arbitrary")),
    )(dpre, dpre, x, x, w)


def _dil_bias(nt, T):
    d = (np.arange(nt)[:, None, None] * T + np.arange(T)[None, None, :] - np.arange(T)[None, :, None])
    cnt = ((d >= 0) & (d <= 128)).astype(np.float64) + ((d >= 0) & (d % 4 == 0) & (d <= 512)) + ((d >= 0) & (d % 16 == 0))
    return jnp.asarray(np.where(cnt > 0, np.log(np.maximum(cnt, 1.0)), -1e30), dtype=F32)


def _attn_fwd(proj, mix):
    S = proj.shape[0]
    T = min(ATT_T, S)
    nt = S // T
    bias = _dil_bias(nt, T)
    scale = DIL_DIM ** -0.5
    npair = DIL_WIDTH // 128
    qb0, kb0, vb0 = P_QKVB // 128, (P_QKVB + DIL_WIDTH) // 128, (P_QKVB + 2 * DIL_WIDTH) // 128

    def body(q_ref, k_ref, v_ref, b_ref, mix_ref, o_ref, lse_ref):
        i = pl.program_id(1)
        qs = (q_ref[...] * scale).astype(_MXU)

        def step(j, carry):
            kt = k_ref[pl.ds(pl.multiple_of(j * T, T), T), :].astype(_MXU)
            vt = v_ref[pl.ds(pl.multiple_of(j * T, T), T), :].astype(_MXU)
            bt = b_ref[i - j]
            out = []
            for hh in range(2):
                m, l, acc = carry[hh]
                sl = slice(hh * DIL_DIM, (hh + 1) * DIL_DIM)
                s = lax.dot_general(kt[:, sl], qs[:, sl], (_NT, ((), ())), preferred_element_type=F32) + bt
                m_new = jnp.maximum(m, jnp.max(s, axis=0, keepdims=True))
                p = jnp.exp(s - m_new)
                a = jnp.exp(m - m_new)
                l = a * l + jnp.sum(p, axis=0, keepdims=True)
                acc = a * acc + lax.dot_general(vt[:, sl], p.astype(_MXU), (_TN, ((), ())), preferred_element_type=F32)
                out.append((m_new, l, acc))
            return tuple(out)

        init = tuple((jnp.full((1, T), -1e30, F32), jnp.zeros((1, T), F32), jnp.zeros((DIL_DIM, T), F32)) for _ in range(2))
        res = lax.fori_loop(0, i + 1, step, init)
        lse_ref[...] = jnp.zeros_like(lse_ref)
        for hh in range(2):
            m, l, acc = res[hh]
            o_ref[:, hh * DIL_DIM:(hh + 1) * DIL_DIM] = (acc / l).T
            lse_ref[hh:hh + 1, :] = m + jnp.log(l)

    return pl.pallas_call(
        body, name="attn_fwd", grid=(npair, nt),
        in_specs=[pl.BlockSpec((T, 128), lambda p, i: (i, qb0 + p)),
                  pl.BlockSpec((S, 128), lambda p, i: (0, kb0 + p)),
                  pl.BlockSpec((S, 128), lambda p, i: (0, vb0 + p)),
                  pl.BlockSpec((nt, T, T), lambda p, i: (0, 0, 0)), pl.BlockSpec(memory_space=pl.ANY)],
        out_specs=(pl.BlockSpec((T, 128), lambda p, i: (i, GDN_WIDTH // 128 + p)),
                   pl.BlockSpec((None, None, 8, T), lambda p, i: (p, i, 0, 0))),
        out_shape=(jax.ShapeDtypeStruct(mix.shape, F32), jax.ShapeDtypeStruct((npair, nt, 8, T), F32)),
        input_output_aliases={4: 0},
        compiler_params=_cparams(("parallel", "parallel")),
    )(proj, proj, proj, bias, mix)


def _attn_bwd(proj, mix, lse, d_mix):
    S = proj.shape[0]
    T = min(ATT_T, S)
    nt = S // T
    bias = _dil_bias(nt, T)
    scale = DIL_DIM ** -0.5
    npair = DIL_WIDTH // 128
    qb0, kb0, vb0 = P_QKVB // 128, (P_QKVB + DIL_WIDTH) // 128, (P_QKVB + 2 * DIL_WIDTH) // 128

    def body(q_ref, k_ref, v_ref, o_ref, lse_ref, do_ref, b_ref, dq_ref, dk_ref, dv_ref, dq_scr):
        j = pl.program_id(1)

        @pl.when(j == 0)
        def _():
            dq_scr[...] = jnp.zeros_like(dq_scr)

        kt = k_ref[...].astype(_MXU)
        vt = v_ref[...].astype(_MXU)
        ones = jnp.ones((8, DIL_DIM), F32)

        def step(i, carry):
            rows = pl.ds(pl.multiple_of(i * T, T), T)
            qs = (q_ref[rows, :] * scale).astype(_MXU)
            dov = do_ref[rows, :]
            prod = dov * o_ref[rows, :]
            lsev = lse_ref[i]
            dob = dov.astype(_MXU)
            bt = b_ref[i - j]
            out = []
            dqs = []
            for hh in range(2):
                dk, dv = carry[hh]
                sl = slice(hh * DIL_DIM, (hh + 1) * DIL_DIM)
                s = lax.dot_general(kt[:, sl], qs[:, sl], (_NT, ((), ())), preferred_element_type=F32) + bt
                p = jnp.exp(s - lsev[hh:hh + 1, :])
                delta = lax.dot_general(ones, prod[:, sl], (_NT, ((), ())), precision=_HI, preferred_element_type=F32)[0:1, :]
                dp = lax.dot_general(vt[:, sl], dob[:, sl], (_NT, ((), ())), preferred_element_type=F32)
                ds = (p * (dp - delta)).astype(_MXU)
                dv = dv + lax.dot_general(p.astype(_MXU), dob[:, sl], (_NN, ((), ())), preferred_element_type=F32)
                dk = dk + lax.dot_general(ds, qs[:, sl], (_NN, ((), ())), preferred_element_type=F32)
                dqs.append(lax.dot_general(ds, kt[:, sl], (_TN, ((), ())), preferred_element_type=F32) * scale)
                out.append((dk, dv))
            dq_scr[rows, :] += jnp.concatenate(dqs, axis=1)
            return tuple(out)

        init = tuple((jnp.zeros((T, DIL_DIM), F32), jnp.zeros((T, DIL_DIM), F32)) for _ in range(2))
        res = lax.fori_loop(j, nt, step, init)
        dk_ref[...] = jnp.concatenate([res[0][0], res[1][0]], axis=1).astype(dk_ref.dtype)
        dv_ref[...] = jnp.concatenate([res[0][1], res[1][1]], axis=1).astype(dv_ref.dtype)

        @pl.when(j == nt - 1)
        def _():
            dq_ref[...] = dq_scr[...].astype(dq_ref.dtype)

    full = lambda c0: pl.BlockSpec((S, 128), lambda p, j: (0, c0 + p))
    tile = lambda c0: pl.BlockSpec((T, 128), lambda p, j: (j, c0 + p))
    out3 = jax.ShapeDtypeStruct((S, DIL_WIDTH), _MXU)
    return pl.pallas_call(
        body, name="attn_bwd", grid=(npair, nt),
        in_specs=[full(qb0), tile(kb0), tile(vb0), full(GDN_WIDTH // 128),
                  pl.BlockSpec((None, nt, 8, T), lambda p, j: (p, 0, 0, 0)), full(GDN_WIDTH // 128),
                  pl.BlockSpec((nt, T, T), lambda p, j: (0, 0, 0))],
        out_specs=(full(0), tile(0), tile(0)),
        out_shape=(out3, out3, out3),
        scratch_shapes=[pltpu.VMEM((S, 128), F32)],
        compiler_params=_cparams(("parallel", "arbitrary")),
    )(proj, proj, proj, mix, lse, d_mix, bias)


def _ffn_act(up, cw):
    S, Cc = up.shape[0], up.shape[1] // 2
    T, tc = _pick_tile(S, 256), _pick_tile(Cc, 1536)
    r8 = T // 8
    nct = Cc // tc

    def body(g_ref, gp_ref, u_ref, up_ref, wg_ref, wu_ref, o_ref):
        keep = jnp.where(pl.program_id(1) == 0, 0.0, 1.0)
        cg = _conv_taps(jnp.concatenate([gp_ref[...] * keep, g_ref[...]], axis=0), wg_ref[...], FFN_CONV, T)
        cu = _conv_taps(jnp.concatenate([up_ref[...] * keep, u_ref[...]], axis=0), wu_ref[...], FFN_CONV, T)
        o_ref[...] = (_silu(cg) * cu).astype(o_ref.dtype)

    cur = lambda o: pl.BlockSpec((T, tc), lambda j, i: (i, j + o))
    prev = lambda o: pl.BlockSpec((8, tc), lambda j, i: (jnp.maximum(i * r8 - 1, 0), j + o))
    wsp = lambda o: pl.BlockSpec((FFN_CONV, tc), lambda j, i: (0, j + o))
    return pl.pallas_call(
        body, name="ffn_act", grid=(nct, S // T),
        in_specs=[cur(0), prev(0), cur(nct), prev(nct), wsp(0), wsp(nct)], out_specs=cur(0),
        out_shape=jax.ShapeDtypeStruct((S, Cc), _MXU),
        compiler_params=_cparams(("parallel", "parallel")),
    )(up, up, up, up, cw, cw)


def _ffn_act_bwd(d_act, up, cw):
    S, Cc = up.shape[0], up.shape[1] // 2
    T, tc = _pick_tile(S, 256), _pick_tile(Cc, 1536)
    r8 = T // 8
    nt = S // T
    nct = Cc // tc
    K = FFN_CONV

    def body(da_ref, dan_ref, g_ref, gp_ref, gn_ref, u_ref, up_ref, un_ref, wg_ref, wu_ref,
             dg_ref, du_ref, dwg_ref, dwu_ref):
        i = pl.program_id(1)
        keep_p = jnp.where(i == 0, 0.0, 1.0)
        keep_n = jnp.where(i == nt - 1, 0.0, 1.0)
        wg, wu = wg_ref[...], wu_ref[...]
        xg = jnp.concatenate([gp_ref[...] * keep_p, g_ref[...], gn_ref[...] * keep_n], axis=0)
        xu = jnp.concatenate([up_ref[...] * keep_p, u_ref[...], un_ref[...] * keep_n], axis=0)
        cg = _conv_taps(xg, wg, K, T + 8)
        cu = _conv_taps(xu, wu, K, T + 8)
        da = jnp.concatenate([da_ref[...], dan_ref[...] * keep_n], axis=0)
        sg = jax.nn.sigmoid(cg)
        d_cg = da * cu * (sg * (1.0 + cg * (1.0 - sg)))
        d_cu = da * (cg * sg)
        dg_ref[...] = _conv_taps_t(d_cg, wg, K, T).astype(dg_ref.dtype)
        du_ref[...] = _conv_taps_t(d_cu, wu, K, T).astype(du_ref.dtype)

        @pl.when(i == 0)
        def _():
            dwg_ref[...] = jnp.zeros_like(dwg_ref)
            dwu_ref[...] = jnp.zeros_like(dwu_ref)

        for k in range(K):
            dwg_ref[k:k + 1, :] += jnp.sum(d_cg[0:T, :] * _shifted(xg, (K - 1) - k, 8, T), axis=0, keepdims=True)
            dwu_ref[k:k + 1, :] += jnp.sum(d_cu[0:T, :] * _shifted(xu, (K - 1) - k, 8, T), axis=0, keepdims=True)

    cur = lambda o: pl.BlockSpec((T, tc), lambda j, i: (i, j + o))
    prev = lambda o: pl.BlockSpec((8, tc), lambda j, i: (jnp.maximum(i * r8 - 1, 0), j + o))
    nxt = lambda o: pl.BlockSpec((8, tc), lambda j, i: (jnp.minimum((i + 1) * r8, S // 8 - 1), j + o))
    wsp = lambda o: pl.BlockSpec((K, tc), lambda j, i: (0, j + o))
    return pl.pallas_call(
        body, name="ffn_act_bwd", grid=(nct, nt),
        in_specs=[cur(0), nxt(0), cur(0), prev(0), nxt(0), cur(nct), prev(nct), nxt(nct), wsp(0), wsp(nct)],
        out_specs=(cur(0), cur(0), wsp(0), wsp(0)),
        out_shape=(jax.ShapeDtypeStruct((S, Cc), _MXU), jax.ShapeDtypeStruct((S, Cc), _MXU),
                   jax.ShapeDtypeStruct((K, Cc), F32), jax.ShapeDtypeStruct((K, Cc), F32)),
        compiler_params=_cparams(("parallel", "arbitrary")),
    )(d_act, d_act, up, up, up, up, up, up, cw, cw)


def _local_step(x, tgt, h1, n1w, n2w, fnw, gp, gnw, wp, conv_w, fcw, rest_weights, early_grads):
    proj = _mm(h1, wp, "nn", name="proj")
    u_v, w_k, q_dec, k_end, attn, tinv, g_end = _gdn_pre(proj, conv_w, gp)
    mix, states = _gdn_scan(u_v, w_k, q_dec, k_end, attn, g_end, proj, gnw)
    mix, lse = _attn_fwd(proj, mix)
    w_out, w_up4, w_down = rest_weights([mix])
    x2 = _mm(mix, w_out, "nn", residual=x, name="outproj")
    h2 = _rmsnorm_fwd(x2, n2w, "norm2")
    up = _mm(h2, w_up4, "nn", b_blocks=True, name="up")
    act = _ffn_act(up, fcw)
    x3 = _mm(act, w_down, "nn", residual=x2, name="down")
    loss, dx3, d_fnw = _loss_head(x3, fnw, tgt, "loss_head")
    d_act = _mm(dx3, w_down, "nt", name="d_act")
    d_wdown = _mm(act, dx3, "tn", name="d_wdown")
    d_upg, d_upu, d_fcwg, d_fcwu = _ffn_act_bwd(d_act, up, fcw)
    d_wup = _mm(h2, d_upg, "tn", place=("blocks", N_CHIPS, 0), tn=w_up4.shape[2], name="d_wgate")
    d_wup = _mm(h2, d_upu, "tn", place=("blocks", N_CHIPS, N_CHIPS // 2), tn=w_up4.shape[2], into=d_wup, name="d_wup")
    token = early_grads[0](d_wup, d_wdown)
    d_h2 = _mm_nt_blocks([d_upg, d_upu], w_up4, "d_h2")
    dx2, d_n2w = _rmsnorm_bwd(d_h2, x2, n2w + token[0:1, 0:1], dx3, "norm2_bwd")
    token = early_grads[1](dx2)
    d_mix = _mm(dx2, w_out, "nt", name="d_mix")
    d_wout = _mm(mix, dx2, "tn", name="d_wout")
    dq_b, dk_b, dv_b = _attn_bwd(proj, mix, lse, d_mix)
    d_uv, d_wk, d_qd, d_ke, d_at, d_ge, d_z, d_gnw = _gdn_scan_bwd(u_v, w_k, q_dec, k_end, attn, g_end, proj,
                                                                   gnw + token[0:1, 0:1], states, d_mix)
    d_pre, d_ba, d_gp = _gdn_post(proj, conv_w, gp, tinv, u_v, w_k, d_uv, d_wk, d_qd, d_ke, d_at, d_ge)
    d_qkva, d_convw = _conv_bwd(d_pre, proj, 0, conv_w, GDN_CONV, "gdn_conv_bwd", 512)
    d_proj = jnp.concatenate([d_qkva, d_z.astype(_MXU), dq_b, dk_b, dv_b, d_ba.astype(_MXU),
                              jnp.zeros((x.shape[0], P_COLS - P_BA - 128), _MXU)], axis=1)
    d_wp = _mm(h1, d_proj, "tn", name="d_wp")
    token = early_grads[2](d_wp, d_wout)
    d_h1 = _mm(d_proj, wp, "nt", name="d_h1")
    dx, d_n1w = _rmsnorm_bwd(d_h1, x, n1w + token[0:1, 0:1], dx2, "norm1_bwd")
    grads = dict(wp=d_wp, conv_w=d_convw, w_out=d_wout, w_up=d_wup, fcw_g=d_fcwg, fcw_u=d_fcwu, w_down=d_wdown,
                 n1w=d_n1w, n2w=d_n2w, fnw=d_fnw, gp=d_gp, gnw=d_gnw)
    return loss, dx, grads


_HBM = pl.BlockSpec(memory_space=pltpu.HBM)


def _pos():
    return lax.axis_index("x"), lax.axis_index("y"), lax.axis_index("c")


def _other_chips(x, y):
    return [(1 - x, y), (x, 1 - y), (1 - x, 1 - y)]


def _halvable(shape):
    return shape[0] % 32 == 0


def _rows_of_half(shape, half):
    if not _halvable(shape):
        return pl.ds(0, shape[0])
    return pl.ds(pl.multiple_of(half * (shape[0] // 2), 16), shape[0] // 2)


_SEM = pl.BlockSpec(memory_space=pltpu.SEMAPHORE)
_ANY = pl.BlockSpec(memory_space=pl.ANY)
_DATAFLOW = pltpu.SideEffectType.DATAFLOW_SIDE_EFFECTING


def _in_hbm(a):
    return pltpu.with_memory_space_constraint(a, pltpu.HBM)


def _halves_copy(src_refs, land_refs, send_sems, recv_sems, shapes, a, j, block, x, y, c):
    px, py = _other_chips(x, y)[j]
    rows = _rows_of_half(shapes[a], c)
    return pltpu.make_async_remote_copy(
        src_ref=src_refs[a].at[rows, :], dst_ref=land_refs[a].at[block, rows, :], send_sem=send_sems.at[3 * a + j],
        recv_sem=recv_sems.at[3 * a + j], device_id=(px, py, c), device_id_type=MESH)


def _gather_halves_start(shards, after, name):
    n = len(shards)
    shapes = [s.shape for s in shards]

    def body(*refs):
        ins, lands = refs[:n], refs[n:2 * n]
        send_sems, recv_sems = refs[2 * n + 1], refs[2 * n + 2]
        token = refs[-1]
        x, y, c = _pos()
        q = 2 * x + y
        for a in range(n):
            for j in range(3):
                _halves_copy(ins, lands, send_sems, recv_sems, shapes, a, j, q, x, y, c).start()
        token[...] = jnp.zeros_like(token)

    land_shapes = [(N_CHIPS,) + s.shape for s in shards]
    return pl.pallas_call(
        body, name=name,
        out_shape=(pltpu.SemaphoreType.DMA((3 * n,)), pltpu.SemaphoreType.DMA((3 * n,)),
                   *[pltpu.HBM(s.shape, s.dtype) for s in shards],
                   *[pltpu.HBM(ls, s.dtype) for ls, s in zip(land_shapes, shards)],
                   jax.ShapeDtypeStruct((8, 128), F32)),
        in_specs=[_HBM] * (2 * n) + [_ANY],
        out_specs=(_SEM, _SEM, *[_HBM] * (2 * n), pl.BlockSpec(memory_space=pltpu.VMEM)),
        input_output_aliases={a: 2 + a for a in range(2 * n)},
        compiler_params=pltpu.CompilerParams(has_side_effects=_DATAFLOW),
    )(*[_in_hbm(s) for s in shards], *[_in_hbm(lax.empty(ls, s.dtype)) for ls, s in zip(land_shapes, shards)], after)


def _gather_halves_wait(started, after, name):
    send_sems, recv_sems, *thru = started
    n = len(thru) // 2
    shapes = [t.shape for t in thru[:n]]

    def body(*refs):
        ins, lands = refs[:n], refs[n:2 * n]
        send_sems, recv_sems = refs[2 * n], refs[2 * n + 1]
        x, y, c = _pos()
        q = 2 * x + y
        chips = _other_chips(x, y)
        for a in range(n):
            for j, (px, py) in enumerate(chips):
                _halves_copy(ins, lands, send_sems, recv_sems, shapes, a, j, q, x, y, c).wait_send()
                _halves_copy(ins, lands, send_sems, recv_sems, shapes, a, j, 2 * px + py, x, y, c).wait_recv()

    outs = pl.pallas_call(
        body, name=name, out_shape=[pltpu.HBM(t.shape, t.dtype) for t in thru],
        in_specs=[_HBM] * (2 * n) + [_SEM, _SEM] + [_ANY] * len(after), out_specs=[_HBM] * (2 * n),
        input_output_aliases={a: a for a in range(2 * n)},
        compiler_params=pltpu.CompilerParams(has_side_effects=_DATAFLOW),
    )(*thru, send_sems, recv_sems, *after)
    return outs[:n], outs[n:]


def _sibling_fill(gathered, name):
    big = [a for a, g in enumerate(gathered) if _halvable(g.shape[1:])]
    n = len(gathered)

    def body(*refs):
        ins, outs = refs[:n], refs[n:2 * n]
        send_sems, recv_sems = refs[2 * n:]
        x, y, c = _pos()
        chips = _other_chips(x, y)

        def copy(k, j, half):
            a = big[k]
            px, py = chips[j]
            rows = _rows_of_half(gathered[a].shape[1:], half)
            return pltpu.make_async_remote_copy(
                src_ref=ins[a].at[2 * px + py, rows, :], dst_ref=outs[a].at[2 * px + py, rows, :],
                send_sem=send_sems.at[3 * k + j], recv_sem=recv_sems.at[3 * k + j],
                device_id=(x, y, 1 - c), device_id_type=MESH)

        sends = [copy(k, j, c) for k in range(len(big)) for j in range(3)]
        for cp in sends:
            cp.start()
        for k in range(len(big)):
            for j in range(3):
                copy(k, j, 1 - c).wait_recv()
        for cp in sends:
            cp.wait_send()

    return pl.pallas_call(
        body, name=name, in_specs=[_HBM] * n, out_specs=[_HBM] * n,
        out_shape=[jax.ShapeDtypeStruct(g.shape, g.dtype) for g in gathered],
        input_output_aliases={a: a for a in range(n)},
        scratch_shapes=[pltpu.SemaphoreType.DMA((3 * len(big),)), pltpu.SemaphoreType.DMA((3 * len(big),))],
    )(*gathered)


def _place_own(shards, gathered, cq, name):
    n = len(shards)
    steps = 4

    def body(cq_ref, *refs):
        for a in range(n):
            refs[2 * n + a][...] = refs[a][...]

    def tile(shape):
        return shape[0] // steps if _halvable(shape) else shape[0]

    in_specs = [pl.BlockSpec((tile(s.shape), s.shape[1]), (lambda i, s_: (i, 0)) if _halvable(s.shape) else (lambda i, s_: (0, 0)))
                for s in shards]
    in_specs += [pl.BlockSpec(memory_space=pl.ANY)] * n
    out_specs = [pl.BlockSpec((None, tile(s.shape), s.shape[1]),
                              (lambda i, s_: (s_[1], i, 0)) if _halvable(s.shape) else (lambda i, s_: (s_[1], 0, 0)))
                 for s in shards]
    gs = pltpu.PrefetchScalarGridSpec(num_scalar_prefetch=1, grid=(steps,), in_specs=in_specs, out_specs=out_specs)
    return pl.pallas_call(
        body, name=name, grid_spec=gs, out_shape=[jax.ShapeDtypeStruct(g.shape, g.dtype) for g in gathered],
        input_output_aliases={1 + n + a: a for a in range(n)},
        compiler_params=_cparams(("arbitrary",)),
    )(cq, *shards, *gathered)


def _half_rows(ref, c, rh):
    return ref.at[:, pl.ds(pl.multiple_of(c * rh, 8), rh), :]


def _chips_copy(src_refs, land_refs, send_sems, recv_sems, a, j, x, y, c):
    px, py = _other_chips(x, y)[j]
    return pltpu.make_async_remote_copy(src_ref=src_refs[a].at[2 * px + py], dst_ref=land_refs[a].at[j],
                                        send_sem=send_sems.at[3 * a + j], recv_sem=recv_sems.at[3 * a + j],
                                        device_id=(px, py, c), device_id_type=MESH)


def _grad_chips_start(parts, name):
    n = len(parts)

    def body(*refs):
        ins, lands = refs[:n], refs[n:2 * n]
        send_sems, recv_sems = refs[2 * n], refs[2 * n + 1]
        token = refs[-1]
        x, y, c = _pos()
        for a in range(n):
            for j in range(3):
                _chips_copy(ins, lands, send_sems, recv_sems, a, j, x, y, c).start()
        token[...] = jnp.zeros_like(token)

    land_shapes = [(3,) + p.shape[1:] for p in parts]
    return pl.pallas_call(
        body, name=name,
        out_shape=(pltpu.SemaphoreType.DMA((3 * n,)), pltpu.SemaphoreType.DMA((3 * n,)),
                   *[pltpu.HBM(p.shape, p.dtype) for p in parts],
                   *[pltpu.HBM(ls, p.dtype) for ls, p in zip(land_shapes, parts)],
                   jax.ShapeDtypeStruct((8, 128), F32)),
        in_specs=[_HBM] * (2 * n),
        out_specs=(_SEM, _SEM, *[_HBM] * (2 * n), pl.BlockSpec(memory_space=pltpu.VMEM)),
        input_output_aliases={a: 2 + a for a in range(2 * n)},
        compiler_params=pltpu.CompilerParams(has_side_effects=_DATAFLOW),
    )(*[_in_hbm(p) for p in parts], *[_in_hbm(lax.empty(ls, p.dtype)) for ls, p in zip(land_shapes, parts)])


def _grad_chips_wait(started, after, name):
    send_sems, recv_sems, *thru = started
    n = len(thru) // 2

    def body(*refs):
        ins, lands = refs[:n], refs[n:2 * n]
        send_sems, recv_sems = refs[2 * n], refs[2 * n + 1]
        x, y, c = _pos()
        for a in range(n):
            for j in range(3):
                cp = _chips_copy(ins, lands, send_sems, recv_sems, a, j, x, y, c)
                cp.wait_send()
                cp.wait_recv()

    outs = pl.pallas_call(
        body, name=name, out_shape=[pltpu.HBM(t.shape, t.dtype) for t in thru],
        in_specs=[_HBM] * (2 * n) + [_SEM, _SEM] + [_ANY] * len(after), out_specs=[_HBM] * (2 * n),
        input_output_aliases={a: a for a in range(2 * n)},
        compiler_params=pltpu.CompilerParams(has_side_effects=_DATAFLOW),
    )(*thru, send_sems, recv_sems, *after)
    return outs[n:]


def _sibling_copy(src_refs, land_refs, send_sems, recv_sems, rhs, a, c, x, y):
    return pltpu.make_async_remote_copy(src_ref=_half_rows(src_refs[a], 1 - c, rhs[a]), dst_ref=land_refs[a],
                                        send_sem=send_sems.at[a], recv_sem=recv_sems.at[a],
                                        device_id=(x, y, 1 - c), device_id_type=MESH)


def _grad_sibling_start(fams, name):
    n = len(fams)
    rhs = [f.shape[1] // 2 for f in fams]

    def body(*refs):
        ins, lands = refs[:n], refs[n:2 * n]
        send_sems, recv_sems = refs[2 * n], refs[2 * n + 1]
        token = refs[-1]
        x, y, c = _pos()
        for a in range(n):
            _sibling_copy(ins, lands, send_sems, recv_sems, rhs, a, c, x, y).start()
        token[...] = jnp.zeros_like(token)

    land_shapes = [(f.shape[0], f.shape[1] // 2, f.shape[2]) for f in fams]
    return pl.pallas_call(
        body, name=name,
        out_shape=(pltpu.SemaphoreType.DMA((n,)), pltpu.SemaphoreType.DMA((n,)),
                   *[pltpu.HBM(f.shape, f.dtype) for f in fams],
                   *[pltpu.HBM(ls, f.dtype) for ls, f in zip(land_shapes, fams)],
                   jax.ShapeDtypeStruct((8, 128), F32)),
        in_specs=[_HBM] * (2 * n),
        out_specs=(_SEM, _SEM, *[_HBM] * (2 * n), pl.BlockSpec(memory_space=pltpu.VMEM)),
        input_output_aliases={a: 2 + a for a in range(2 * n)},
        compiler_params=pltpu.CompilerParams(has_side_effects=_DATAFLOW),
    )(*[_in_hbm(f) for f in fams], *[_in_hbm(lax.empty(ls, f.dtype)) for ls, f in zip(land_shapes, fams)])


def _grad_sibling_wait(started, after, name):
    send_sems, recv_sems, *thru = started
    n = len(thru) // 2
    rhs = [t.shape[1] // 2 for t in thru[:n]]

    def body(*refs):
        ins, lands = refs[:n], refs[n:2 * n]
        send_sems, recv_sems = refs[2 * n], refs[2 * n + 1]
        x, y, c = _pos()
        for a in range(n):
            cp = _sibling_copy(ins, lands, send_sems, recv_sems, rhs, a, c, x, y)
            cp.wait_send()
            cp.wait_recv()

    outs = pl.pallas_call(
        body, name=name, out_shape=[pltpu.HBM(t.shape, t.dtype) for t in thru],
        in_specs=[_HBM] * (2 * n) + [_SEM, _SEM] + [_ANY] * len(after), out_specs=[_HBM] * (2 * n),
        input_output_aliases={a: a for a in range(2 * n)},
        compiler_params=pltpu.CompilerParams(has_side_effects=_DATAFLOW),
    )(*thru, send_sems, recv_sems, *after)
    return outs[:n], outs[n:]


def _grad_share(fulls, name, small=None):
    n = len(fulls)
    ns = 0 if small is None else 1
    rhs = [f.shape[0] // 2 for f in fulls]

    def body(*refs):
        ins, outs = refs[:n], refs[n + ns:2 * n + ns]
        send_sems, recv_sems = refs[2 * (n + ns)], refs[2 * (n + ns) + 1]
        x, y, c = _pos()

        def copy(a, half):
            rows = pl.ds(pl.multiple_of(half * rhs[a], 8), rhs[a])
            return pltpu.make_async_remote_copy(src_ref=ins[a].at[rows, :], dst_ref=outs[a].at[rows, :],
                                                send_sem=send_sems.at[7 * ns + a], recv_sem=recv_sems.at[7 * ns + a],
                                                device_id=(x, y, 1 - c), device_id_type=MESH)

        sends = [copy(a, c) for a in range(n)]
        for cp in sends:
            cp.start()
        if ns:
            small_ref, all_ref = refs[n], refs[2 * n + 1]
            me = 4 * x + 2 * y + c

            def peer(r):
                dx, dy, dc = (r >> 2) & 1, (r >> 1) & 1, r & 1
                return (x if dx == 0 else 1 - x), (y if dy == 0 else 1 - y), (c if dc == 0 else 1 - c)

            def small_copy(r, slot):
                return pltpu.make_async_remote_copy(src_ref=small_ref, dst_ref=all_ref.at[slot], send_sem=send_sems.at[r - 1],
                                                    recv_sem=recv_sems.at[r - 1], device_id=peer(r), device_id_type=MESH)

            smalls = [small_copy(r, me) for r in range(1, 8)]
            for cp in smalls:
                cp.start()
            for r in range(1, 8):
                px, py, pc = peer(r)
                small_copy(r, 4 * px + 2 * py + pc).wait_recv()
            sends = sends + smalls
        for a in range(n):
            copy(a, 1 - c).wait_recv()
        for cp in sends:
            cp.wait_send()

    return pl.pallas_call(
        body, name=name, in_specs=[_HBM] * (n + ns), out_specs=[_HBM] * (n + ns),
        out_shape=[jax.ShapeDtypeStruct(f.shape, f.dtype) for f in fulls]
        + ([jax.ShapeDtypeStruct((8,) + small.shape, small.dtype)] if ns else []),
        input_output_aliases={a: a for a in range(n)},
        scratch_shapes=[pltpu.SemaphoreType.DMA((7 * ns + n,)), pltpu.SemaphoreType.DMA((7 * ns + n,))],
    )(*fulls, *([small] if ns else []))


def _add_sibling(own, recv, cq, name):
    nb, R, Cc = own.shape
    Rh = R // 2

    def body(cq_ref, a_ref, b_ref, o32_ref, o16_ref):
        s = a_ref[...] + b_ref[...]
        o32_ref[...] = s
        o16_ref[...] = s.astype(o16_ref.dtype)

    sp = pl.BlockSpec((1, Rh, Cc), lambda b, s: (b, 0, 0))
    gs = pltpu.PrefetchScalarGridSpec(
        num_scalar_prefetch=1, grid=(nb,),
        in_specs=[pl.BlockSpec((1, Rh, Cc), lambda b, s: (b, s[0], 0)), sp], out_specs=[sp, sp])
    return pl.pallas_call(
        body, name=name, grid_spec=gs,
        out_shape=[jax.ShapeDtypeStruct((nb, Rh, Cc), F32), jax.ShapeDtypeStruct((nb, Rh, Cc), _MXU)],
        compiler_params=_cparams(("parallel",)),
    )(cq, own, recv)


def _add_chips(part32, recv3, cq, name):
    nb, Rh, Cc = part32.shape

    def body(cq_ref, a_ref, b_ref, o_ref):
        acc = a_ref[0]
        for j in range(3):
            acc = acc + b_ref[j].astype(F32)
        o_ref[...] = acc

    gs = pltpu.PrefetchScalarGridSpec(
        num_scalar_prefetch=1, grid=(1,),
        in_specs=[pl.BlockSpec((1, Rh, Cc), lambda i, s: (s[1], 0, 0)), pl.BlockSpec((3, Rh, Cc), lambda i, s: (0, 0, 0))],
        out_specs=pl.BlockSpec((Rh, Cc), lambda i, s: (s[0], 0)))
    return pl.pallas_call(
        body, name=name, grid_spec=gs, out_shape=jax.ShapeDtypeStruct((2 * Rh, Cc), F32),
        compiler_params=_cparams(("arbitrary",)),
    )(cq, part32, recv3)


def _sum_devices(small_all, small, me):
    def body(me_ref, all_ref, own_ref, o_ref):
        tot = None
        for d in range(8):
            term = jnp.where(me_ref[0] == d, own_ref[...], all_ref[d])
            tot = term if tot is None else tot + term
        o_ref[...] = tot

    gs = pltpu.PrefetchScalarGridSpec(
        num_scalar_prefetch=1, grid=(1,),
        in_specs=[pl.BlockSpec(small_all.shape, lambda i, s: (0, 0, 0)), pl.BlockSpec(small.shape, lambda i, s: (0, 0))],
        out_specs=pl.BlockSpec(small.shape, lambda i, s: (0, 0)))
    return pl.pallas_call(body, name="sum_devices", grid_spec=gs,
                          out_shape=jax.ShapeDtypeStruct(small.shape, F32))(me, small_all, small)


def _adamw(w, g, m, v, name):
    R, Cc = w.shape
    T = max([t for t in range(8, 257, 8) if R % t == 0], default=R)
    c1 = 1.0 / (1.0 - ADAM_B1 ** ADAM_STEP)
    c2 = 1.0 / (1.0 - ADAM_B2 ** ADAM_STEP)

    def body(w_ref, g_ref, m_ref, v_ref, d_ref, mo_ref, vo_ref):
        gv = g_ref[...]
        mn = ADAM_B1 * m_ref[...] + (1.0 - ADAM_B1) * gv
        vn = ADAM_B2 * v_ref[...] + (1.0 - ADAM_B2) * (gv * gv)
        mo_ref[...] = mn
        vo_ref[...] = vn
        d_ref[...] = -ADAM_LR * ((mn * c1) / (jnp.sqrt(vn * c2) + ADAM_EPS) + ADAM_WD * w_ref[...])

    sp = pl.BlockSpec((T, Cc), lambda i: (i, 0))
    sh = jax.ShapeDtypeStruct((R, Cc), F32)
    return pl.pallas_call(
        body, name=name, grid=(R // T,), in_specs=[sp] * 4, out_specs=(sp, sp, sp), out_shape=(sh, sh, sh),
        compiler_params=_cparams(("parallel",)),
    )(w, g, m, v)


SMALL_ROWS = 32
REPL_ROWS = 8


def _pad_lanes(v, n=D_MODEL):
    return jnp.pad(v, ((0, 0), (0, n - v.shape[1])))


def kernel(x, norm1_w, w_in, conv_qkv_w, a_log, dt_bias, gdn_norm_w, w_out, norm2_w, w_up, ffn_conv_w, w_down, final_norm_w, loss_target, m_norm1_w, m_w_in, m_conv_qkv_w, m_a_log, m_dt_bias, m_gdn_norm_w, m_w_out, m_norm2_w, m_w_up, m_ffn_conv_w, m_w_down, m_final_norm_w, v_norm1_w, v_w_in, v_conv_qkv_w, v_a_log, v_dt_bias, v_gdn_norm_w, v_w_out, v_norm2_w, v_w_up, v_ffn_conv_w, v_w_down, v_final_norm_w):
    c = lax.axis_index("c")
    q = 2 * lax.axis_index("x") + lax.axis_index("y")
    S = x.shape[1]
    cq = jnp.stack([c, q]).astype(jnp.int32)

    *in_started, in_token = _gather_halves_start([w_in[0].astype(_MXU), conv_qkv_w[0], ffn_conv_w[0]], x, "gather_in_start")
    w_in_l, m_w_in_l, v_w_in_l = (a + in_token[0:1, 0:1] for a in (w_in, m_w_in, v_w_in))
    h1 = _rmsnorm_fwd(x[0], norm1_w + in_token[0:1, 0:1], "norm1")
    rest = [(a[0] + in_token[0:1, 0:1]).astype(_MXU) for a in (w_out, w_up, w_down)]
    in_shards, got_in = _gather_halves_wait(in_started, [w_in_l, m_w_in_l, v_w_in_l, h1, *rest], "gather_in_wait")
    g_in, g_conv, g_fconv = _place_own(in_shards, _sibling_fill(got_in, "fill_in"), cq, "place_in")
    *rest_started, token = _gather_halves_start(rest, g_conv, "gather_rest_start")

    def rest_weights(after):
        shards, got = _gather_halves_wait(rest_started, after, "gather_rest_wait")
        got = _sibling_fill(got, "fill_rest")
        g_out, g_up, g_down = _place_own(shards, got, cq, "place_rest")
        return g_out.reshape(D_MODEL, D_MODEL), g_up, g_down.reshape(D_FF, D_MODEL)
    wp = _wp_assemble(g_in, [token])
    conv_f = jnp.concatenate([g_conv[i] for i in range(N_CHIPS)], axis=1)
    fcw = jnp.concatenate([g_fconv[i] for i in range(N_CHIPS)], axis=1)
    gp = _pad_lanes(jnp.concatenate([a_log, dt_bias], axis=1), 128)
    fnw = final_norm_w[None, :]
    early = {}

    def early_sibling(d_wup, d_wdown):
        *early["sibling"], tok = _grad_sibling_start([d_wup, d_wdown.reshape(N_CHIPS, D_FF // N_CHIPS, D_MODEL)],
                                                     "grad_sibling_early_start")
        return tok

    def early_chips(dx2):
        fams_e, got_e = _grad_sibling_wait(early["sibling"], [dx2], "grad_sibling_early_wait")
        early["parts"] = [_add_sibling(f, r, cq, "add_sibling_" + nm) for f, r, nm in zip(fams_e, got_e, ("w_up", "w_down"))]
        *early["started"], tok = _grad_chips_start([p[1] for p in early["parts"]], "grad_chips_start")
        return tok

    def late_sibling(d_wp, d_wout):
        *early["late_sibling"], tok = _grad_sibling_start(
            [_win_split(d_wp), d_wout.reshape(N_CHIPS, D_MODEL // N_CHIPS, D_MODEL)], "grad_sibling_late_start")
        return tok

    early_grads = (early_sibling, early_chips, late_sibling)

    loss_l, dx, g = _local_step(x[0], loss_target[0], h1, norm1_w, norm2_w + token[0:1, 0:1], fnw, gp, gdn_norm_w, wp,
                                conv_f, fcw, rest_weights, early_grads)
    n_fc = FFN_CONV * D_FF

    def rows_of(v):
        flat = v.reshape(-1)
        return jnp.pad(flat, (0, -flat.shape[0] % D_MODEL)).reshape(-1, D_MODEL)

    gp_row = _pad_lanes(jnp.concatenate([g["gp"][0:1, 0:8], loss_l[0:1, 0:1]], axis=1))
    small = jnp.concatenate([g["n1w"], g["n2w"], g["fnw"], gp_row, _pad_lanes(g["gnw"]),
                             rows_of(g["conv_w"]), rows_of(g["fcw_g"]), rows_of(g["fcw_u"])], axis=0)
    small = jnp.pad(small, ((0, SMALL_ROWS - small.shape[0]), (0, 0)))
    fams, got = _grad_sibling_wait(early["late_sibling"], [dx], "grad_sibling_late_wait")
    parts = [_add_sibling(f, r, cq, "add_sibling_" + nm) for f, r, nm in zip(fams, got, ("w_in", "w_out"))]
    *late_started, late_token = _grad_chips_start([p[1] for p in parts], "grad_chips_late_start")
    got3_e = _grad_chips_wait(early["started"], [dx, g["wp"], late_token], "grad_chips_wait")
    g_w_up, g_w_down = _grad_share(
        [_add_chips(p[0], r3, cq, "add_chips_" + nm) for p, r3, nm in zip(early["parts"], got3_e, ("w_up", "w_down"))],
        "grad_share_early")
    big = {}

    def adamw_big(nm, w, gg, m, v):
        d_, m_, v_ = _adamw(w[0], gg, m[0], v[0], "adamw_" + nm)
        big[nm] = (gg[None], d_[None], m_[None], v_[None])

    adamw_big("w_up", w_up, g_w_up, m_w_up, v_w_up)
    adamw_big("w_down", w_down, g_w_down, m_w_down, v_w_down)
    got3 = _grad_chips_wait(late_started, [big["w_up"][1], big["w_down"][1]], "grad_chips_late_wait")
    g_w_in, g_w_out, small_all = _grad_share(
        [_add_chips(p[0], r3, cq, "add_chips_" + nm) for p, r3, nm in zip(parts, got3, ("w_in", "w_out"))],
        "grad_share_late", small)
    small_red = _sum_devices(small_all, small, (2 * q + c).astype(jnp.int32).reshape(1))
    loss = small_red[3, 8]
    r0 = 5
    r1 = r0 + GDN_CONV * 3 * GDN_WIDTH // D_MODEL
    r2 = r1 + -(-n_fc // D_MODEL)
    conv_red = small_red[r0:r1].reshape(GDN_CONV, 3 * GDN_WIDTH)
    fc_red = jnp.concatenate([small_red[r1:r2].reshape(-1)[:n_fc].reshape(FFN_CONV, D_FF),
                              small_red[r2:2 * r2 - r1].reshape(-1)[:n_fc].reshape(FFN_CONV, D_FF)], axis=1)
    g_conv_w = lax.dynamic_slice_in_dim(conv_red, q * (3 * GDN_WIDTH // N_CHIPS), 3 * GDN_WIDTH // N_CHIPS, axis=1)
    g_fconv_w = lax.dynamic_slice_in_dim(fc_red, q * (2 * D_FF // N_CHIPS), 2 * D_FF // N_CHIPS, axis=1)
    g_n1w, g_n2w, g_fnw = small_red[0:1], small_red[1:2], small_red[2]
    g_alog, g_dtb, g_gnw = small_red[3:4, 0:4], small_red[3:4, 4:8], small_red[4:5, 0:128]
    for nm, w, gg, m, v in (("w_in", w_in_l, g_w_in, m_w_in_l, v_w_in_l), ("conv_qkv_w", conv_qkv_w, g_conv_w, m_conv_qkv_w, v_conv_qkv_w),
                            ("w_out", w_out, g_w_out, m_w_out, v_w_out),
                            ("ffn_conv_w", ffn_conv_w, g_fconv_w, m_ffn_conv_w, v_ffn_conv_w)):
        adamw_big(nm, w, gg, m, v)

    def pack_small(n1, n2, fn, al, db, gn):
        return jnp.concatenate([n1, n2, fn[None, :], _pad_lanes(jnp.concatenate([al, db], axis=1)), _pad_lanes(gn),
                                jnp.zeros((REPL_ROWS - 5, D_MODEL), F32)], axis=0)

    sw = pack_small(norm1_w, norm2_w, final_norm_w, a_log, dt_bias, gdn_norm_w)
    sm = pack_small(m_norm1_w, m_norm2_w, m_final_norm_w, m_a_log, m_dt_bias, m_gdn_norm_w)
    sv = pack_small(v_norm1_w, v_norm2_w, v_final_norm_w, v_a_log, v_dt_bias, v_gdn_norm_w)
    sd, smn, svn = _adamw(sw, small_red[:REPL_ROWS], sm, sv, "adamw_small")

    def unpack_small(t):
        return dict(norm1_w=t[0:1], norm2_w=t[1:2], final_norm_w=t[2], a_log=t[3:4, 0:4], dt_bias=t[3:4, 4:8],
                    gdn_norm_w=t[4:5, 0:128])

    sg = dict(norm1_w=g_n1w, norm2_w=g_n2w, final_norm_w=g_fnw, a_log=g_alog, dt_bias=g_dtb, gdn_norm_w=g_gnw)
    sd, smn, svn = unpack_small(sd), unpack_small(smn), unpack_small(svn)
    names = ["norm1_w", "w_in", "conv_qkv_w", "a_log", "dt_bias", "gdn_norm_w", "w_out", "norm2_w", "w_up",
             "ffn_conv_w", "w_down", "final_norm_w"]
    grads = [big[n][0] if n in big else sg[n] for n in names]
    deltas = [big[n][1] if n in big else sd[n] for n in names]
    new_m = [big[n][2] if n in big else smn[n] for n in names]
    new_v = [big[n][3] if n in big else svn[n] for n in names]
    return (loss, dx[None], *grads, *deltas, *new_m, *new_v)
```

```python
import functools
import math

import numpy as np
import jax
import jax.numpy as jnp
from jax import lax
from jax.experimental import pallas as pl
from jax.experimental.pallas import tpu as pltpu

F32 = jnp.float32
BF16 = jnp.bfloat16
_MXU = jnp.bfloat16
_HI = lax.Precision.HIGHEST
EPS = 1e-6
V7X_VMEM_LIMIT = 56 * 1024 * 1024
MESH = pl.DeviceIdType.MESH

D_MODEL = 1024
GDN_HEADS, GDN_DIM, GDN_CHUNK, GDN_CONV = 4, 128, 64, 4
GDN_WIDTH = GDN_HEADS * GDN_DIM
DIL_HEADS, DIL_DIM = 8, 64
DIL_WIDTH = DIL_HEADS * DIL_DIM
D_FF, FFN_CONV = 2816, 3
IN_COLS = 3592
P_COLS = 3840
P_Z, P_QKVB, P_BA = 1536, 2048, 3584
ATT_T = 1024
ADAM_LR, ADAM_B1, ADAM_B2, ADAM_EPS, ADAM_WD, ADAM_STEP = 0.001, 0.9, 0.999, 1e-08, 0.01, 10
N_CHIPS = 4


def _cparams(sem=None, vmem=None):
    kw = {}
    if sem is not None:
        kw["dimension_semantics"] = sem
    if vmem is not None:
        kw["vmem_limit_bytes"] = vmem
    return pltpu.CompilerParams(**kw)


def _silu(x):
    return x * jax.nn.sigmoid(x)


def _pick_tile(n, cap):
    best = None
    for t in range(128, min(n, cap) + 1, 128):
        if n % t == 0:
            best = t
    return best or n


def _mm(a, b, mode, *, out_dtype=F32, residual=None, name, b_blocks=False, place=None, into=None, tn=None):
    if mode == "nn":
        M, K = a.shape
        N = b.shape[0] * b.shape[2] if b_blocks else b.shape[1]
    elif mode == "nt":
        (M, K), (N, _) = a.shape, b.shape
    else:
        (K, M), (_, N) = a.shape, b.shape
    tm = _pick_tile(M, 1024)
    tn = b.shape[2] if b_blocks else (tn or _pick_tile(N, 1536))

    def vmem(tm, tn):
        return 2 * (tm * K * a.dtype.itemsize + tn * K * b.dtype.itemsize
                    + tm * tn * (jnp.dtype(out_dtype).itemsize + (4 if residual is not None else 0))) + 3 * tm * tn * 4

    fixed_tn = b_blocks or (place is not None and place[0] == "blocks")
    while vmem(tm, tn) > 40 * 1024 * 1024:
        if (tm >= tn or fixed_tn) and tm % 256 == 0:
            tm //= 2
        elif tn % 256 == 0 and not fixed_tn:
            tn //= 2
        else:
            tm //= 2
    a_spec = pl.BlockSpec((K, tm), lambda j, i: (0, i)) if mode == "tn" else pl.BlockSpec((tm, K), lambda j, i: (i, 0))
    if b_blocks:
        b_spec = pl.BlockSpec((None, K, tn), lambda j, i: (j, 0, 0))
    else:
        b_spec = pl.BlockSpec((tn, K), lambda j, i: (j, 0)) if mode == "nt" else pl.BlockSpec((K, tn), lambda j, i: (0, j))
    r_spec = pl.BlockSpec((tm, tn), lambda j, i: (i, j))
    if place is None:
        o_spec, o_shape = r_spec, (M, N)
    elif place[0] == "rows":
        off = place[2] // tm
        o_spec, o_shape = pl.BlockSpec((tm, tn), lambda j, i: (i + off, j)), (place[1], N)
    else:
        off = place[2]
        o_spec, o_shape = pl.BlockSpec((None, tm, tn), lambda j, i: (j + off, i, 0)), (place[1], M, tn)
    dims = {"nn": (((1,), (0,)), ((), ())), "nt": (((1,), (1,)), ((), ())), "tn": (((0,), (0,)), ((), ()))}[mode]

    def body(*refs):
        a_ref, b_ref = refs[0], refs[1]
        o_ref = refs[-1]
        acc = lax.dot_general(a_ref[...].astype(_MXU), b_ref[...].astype(_MXU), dims, preferred_element_type=F32)
        if residual is not None:
            acc = acc + refs[2][...]
        o_ref[...] = acc.astype(out_dtype)

    ins, specs, alias = [a, b], [a_spec, b_spec], {}
    if residual is not None:
        ins.append(residual)
        specs.append(r_spec)
    if into is not None:
        alias = {len(ins): 0}
        ins.append(into)
        specs.append(pl.BlockSpec(memory_space=pl.ANY))
    return pl.pallas_call(
        body, name=name, grid=(N // tn, M // tm), in_specs=specs, out_specs=o_spec,
        out_shape=jax.ShapeDtypeStruct(o_shape, out_dtype), input_output_aliases=alias,
        compiler_params=_cparams(("parallel", "parallel"), V7X_VMEM_LIMIT),
    )(*ins)


def _mm_nt_blocks(a_list, b4, name):
    M = a_list[0].shape[0]
    nb, N, Kb = b4.shape
    tm, tn = _pick_tile(M, 1024), _pick_tile(N, 512)

    def body(a0_ref, a1_ref, b_ref, o_ref):
        acc = None
        for blk in range(nb):
            a_ref = (a0_ref, a1_ref)[blk // 2]
            lo = (blk % 2) * Kb
            t = lax.dot_general(a_ref[:, lo:lo + Kb].astype(_MXU), b_ref[blk].astype(_MXU), (((1,), (1,)), ((), ())),
                                preferred_element_type=F32)
            acc = t if acc is None else acc + t
        o_ref[...] = acc

    a_spec = pl.BlockSpec((tm, 2 * Kb), lambda j, i: (i, 0))
    return pl.pallas_call(
        body, name=name, grid=(N // tn, M // tm),
        in_specs=[a_spec, a_spec, pl.BlockSpec((nb, tn, Kb), lambda j, i: (0, j, 0))],
        out_specs=pl.BlockSpec((tm, tn), lambda j, i: (i, j)), out_shape=jax.ShapeDtypeStruct((M, N), F32),
        compiler_params=_cparams(("parallel", "parallel"), V7X_VMEM_LIMIT),
    )(a_list[0], a_list[1], b4)


def _wp_assemble(g_in, after=()):
    nb, Dm, Wb = g_in.shape
    T = 256
    n_lo = P_QKVB - 2 * Wb

    def body(g_ref, *rest):
        g2 = g_ref[2]
        rest[-1][...] = jnp.concatenate(
            [g_ref[0], g_ref[1], g2[:, :n_lo], g2[:, n_lo + 8:], g_ref[3], g2[:, n_lo:n_lo + 8],
             jnp.zeros((T, P_COLS - P_BA - 8), g_in.dtype)], axis=1)

    return pl.pallas_call(
        body, name="wp_assemble", grid=(Dm // T,),
        in_specs=[pl.BlockSpec((nb, T, Wb), lambda i: (0, i, 0))] + [pl.BlockSpec(memory_space=pl.ANY)] * len(after),
        out_specs=pl.BlockSpec((T, P_COLS), lambda i: (i, 0)), out_shape=jax.ShapeDtypeStruct((Dm, P_COLS), g_in.dtype),
        compiler_params=_cparams(("parallel",)),
    )(g_in, *after)


def _win_split(d_wp):
    Dm = d_wp.shape[0]
    Wb = IN_COLS // N_CHIPS
    T = 256

    def body(x_ref, o_ref):
        xv = x_ref[...]
        o_ref[0] = xv[:, 0:Wb]
        o_ref[1] = xv[:, Wb:2 * Wb]
        o_ref[2] = jnp.concatenate([xv[:, 2 * Wb:P_QKVB], xv[:, P_BA:P_BA + 8], xv[:, P_QKVB:3 * Wb - 8]], axis=1)
        o_ref[3] = xv[:, 3 * Wb - 8:P_BA]

    return pl.pallas_call(
        body, name="win_split", grid=(Dm // T,), in_specs=[pl.BlockSpec((T, P_COLS), lambda i: (i, 0))],
        out_specs=pl.BlockSpec((N_CHIPS, T, Wb), lambda i: (0, i, 0)),
        out_shape=jax.ShapeDtypeStruct((N_CHIPS, Dm, Wb), F32), compiler_params=_cparams(("parallel",)),
    )(d_wp)


def _rmsnorm_fwd(x, w, name):
    S, D = x.shape
    T = _pick_tile(S, 512)

    def body(x_ref, w_ref, o_ref):
        xv = x_ref[...]
        rs = lax.rsqrt(jnp.mean(xv * xv, axis=-1, keepdims=True) + EPS)
        o_ref[...] = (xv * rs * w_ref[...]).astype(o_ref.dtype)

    return pl.pallas_call(
        body, name=name, grid=(S // T,),
        in_specs=[pl.BlockSpec((T, D), lambda i: (i, 0)), pl.BlockSpec((1, D), lambda i: (0, 0))],
        out_specs=pl.BlockSpec((T, D), lambda i: (i, 0)),
        out_shape=jax.ShapeDtypeStruct((S, D), _MXU),
        compiler_params=_cparams(("parallel",)),
    )(x, w)


def _rmsnorm_bwd(dh, x, w, dres, name):
    S, D = x.shape
    T = _pick_tile(S, 512)

    def body(dh_ref, x_ref, w_ref, dres_ref, dx_ref, dxn_ref, dw_ref):
        xv = x_ref[...]
        rs = lax.rsqrt(jnp.mean(xv * xv, axis=-1, keepdims=True) + EPS)
        xn = xv * rs
        dhv = dh_ref[...]
        dxn = dhv * w_ref[...]
        dxv = dres_ref[...] + rs * (dxn - xn * jnp.mean(dxn * xn, axis=-1, keepdims=True))
        dx_ref[...] = dxv
        dxn_ref[...] = dxv.astype(dxn_ref.dtype)

        @pl.when(pl.program_id(0) == 0)
        def _():
            dw_ref[...] = jnp.zeros_like(dw_ref)

        dw_ref[...] += jnp.sum(dhv * xn, axis=0, keepdims=True)

    row = pl.BlockSpec((T, D), lambda i: (i, 0))
    vec = pl.BlockSpec((1, D), lambda i: (0, 0))
    return pl.pallas_call(
        body, name=name, grid=(S // T,), in_specs=[row, row, vec, row], out_specs=(row, row, vec),
        out_shape=(jax.ShapeDtypeStruct((S, D), F32), jax.ShapeDtypeStruct((S, D), _MXU), jax.ShapeDtypeStruct((1, D), F32)),
        compiler_params=_cparams(("arbitrary",)),
    )(dh, x, w, dres)


def _loss_head(x3, w, tgt, name):
    S, D = x3.shape
    T = _pick_tile(S, 512)

    def body(x_ref, w_ref, t_ref, loss_ref, dx_ref, dxn_ref, dw_ref):
        xv = x_ref[...]
        rs = lax.rsqrt(jnp.mean(xv * xv, axis=-1, keepdims=True) + EPS)
        xn = xv * rs
        err = xn * w_ref[...] - t_ref[...]
        dy = err * (1.0 / D)
        dxn = dy * w_ref[...]
        dxv = rs * (dxn - xn * jnp.mean(dxn * xn, axis=-1, keepdims=True))
        dx_ref[...] = dxv
        dxn_ref[...] = dxv.astype(dxn_ref.dtype)

        @pl.when(pl.program_id(0) == 0)
        def _():
            dw_ref[...] = jnp.zeros_like(dw_ref)
            loss_ref[...] = jnp.zeros_like(loss_ref)

        dw_ref[...] += jnp.sum(dy * xn, axis=0, keepdims=True)
        part = jnp.sum(jnp.sum(err * err, axis=-1, keepdims=True), axis=0, keepdims=True) * (0.5 / D)
        loss_ref[...] += jnp.broadcast_to(part, loss_ref.shape)

    row = pl.BlockSpec((T, D), lambda i: (i, 0))
    vec = pl.BlockSpec((1, D), lambda i: (0, 0))
    return pl.pallas_call(
        body, name=name, grid=(S // T,), in_specs=[row, vec, row],
        out_specs=(pl.BlockSpec((8, 128), lambda i: (0, 0)), row, row, vec),
        out_shape=(jax.ShapeDtypeStruct((8, 128), F32), jax.ShapeDtypeStruct((S, D), F32), jax.ShapeDtypeStruct((S, D), _MXU),
                   jax.ShapeDtypeStruct((1, D), F32)),
        compiler_params=_cparams(("arbitrary",)),
    )(x3, w, tgt)


def _shifted(ext, back, lo, n):
    if back == 0:
        return ext[lo:lo + n, :]
    return pltpu.roll(ext, back % ext.shape[0], 0)[lo:lo + n, :]


def _conv_windows(ext, K, T):
    return [_shifted(ext, (K - 1) - i, 8, T) for i in range(K)]


def _conv_taps(ext, w, K, T):
    out = None
    for i, win in enumerate(_conv_windows(ext, K, T)):
        term = win * w[i:i + 1, :]
        out = term if out is None else out + term
    return out


def _conv_taps_t(ext, w, K, T):
    out = None
    for i in range(K):
        term = _shifted(ext, i - (K - 1), 0, T) * w[i:i + 1, :]
        out = term if out is None else out + term
    return out


def _tri_masks(C):
    r = lax.broadcasted_iota(jnp.int32, (C, C), 0)
    c = lax.broadcasted_iota(jnp.int32, (C, C), 1)
    return r == c, r >= c, r > c, r <= c


_NN, _NT, _TN = ((1,), (0,)), ((1,), (1,)), ((0,), (0,))
_GDN_PASSES = dict(qk=1, inv=1, sol=1, scan=1, bwd=1)


def _bdot_raw(a, b, kind, passes):
    dims = ({"NN": ((2,), (1,)), "NT": ((2,), (2,)), "TN": ((1,), (1,))}[kind], ((0,), (0,)))
    if passes == 0:
        return lax.dot_general(a, b, dims, precision=_HI, preferred_element_type=F32)
    ah, bh = a.astype(BF16), b.astype(BF16)
    out = lax.dot_general(ah, bh, dims, preferred_element_type=F32)
    if passes == 3:
        al, bl = (a - ah.astype(F32)).astype(BF16), (b - bh.astype(F32)).astype(BF16)
        out = out + lax.dot_general(ah, bl, dims, preferred_element_type=F32) + lax.dot_general(al, bh, dims, preferred_element_type=F32)
    return out


@functools.partial(jax.custom_vjp, nondiff_argnums=(2, 3))
def _bdot(a, b, kind, passes):
    return _bdot_raw(a, b, kind, passes)


def _bdot_fwd(a, b, kind, passes):
    return _bdot_raw(a, b, kind, passes), (a, b)


def _bdot_bwd(kind, passes, res, ct):
    a, b = res
    if kind == "NN":
        return _bdot_raw(ct, b, "NT", passes), _bdot_raw(a, ct, "TN", passes)
    if kind == "NT":
        return _bdot_raw(ct, b, "NN", passes), _bdot_raw(ct, a, "TN", passes)
    return _bdot_raw(b, ct, "NT", passes), _bdot_raw(a, ct, "NN", passes)


_bdot.defvjp(_bdot_fwd, _bdot_bwd)


def _softplus(x):
    return jnp.maximum(x, 0.0) + jnp.log(1.0 + jnp.exp(-jnp.abs(x)))


def _gdn_stage1(cq, ck, cv, b_col, a_col, alog, dtb, dot=_bdot_raw):
    C = cq.shape[1]
    eye, incl, strict, incl_t = _tri_masks(C)
    qn = cq * lax.rsqrt(jnp.sum(cq * cq, axis=-1, keepdims=True) + EPS) * (GDN_DIM ** -0.5)
    kn = ck * lax.rsqrt(jnp.sum(ck * ck, axis=-1, keepdims=True) + EPS)
    beta = jax.nn.sigmoid(b_col)
    g = -jnp.exp(alog) * _softplus(a_col + dtb)
    g_row = jnp.sum(jnp.where(eye, g, 0.0), axis=1, keepdims=True)
    beta_row = jnp.sum(jnp.where(eye, beta, 0.0), axis=1, keepdims=True)
    gc_col = jnp.sum(jnp.where(incl, g_row, 0.0), axis=2, keepdims=True)
    gc_row = jnp.sum(jnp.where(incl_t, g, 0.0), axis=1, keepdims=True)
    dec = jnp.where(incl, jnp.exp(jnp.where(incl, gc_col - gc_row, 0.0)), 0.0)
    kk = dot(kn, kn, "NT", _GDN_PASSES["qk"])
    qk = dot(qn, kn, "NT", _GDN_PASSES["qk"])
    lmat = jnp.where(strict, dec * kk * beta_row, 0.0)
    attn = dec * qk * beta_row
    gam = jnp.exp(gc_col)
    gc_last = gc_col[:, C - 1:C, :]
    k_end = kn * (jnp.exp(gc_last - gc_col) * beta)
    return lmat, cv, gam * kn, gam * qn, attn, k_end, jnp.exp(gc_last)


def _tri_inv(lmat):
    C = lmat.shape[1]
    eye = _tri_masks(C)[0]
    ps = _GDN_PASSES["inv"]
    p = jnp.where(eye, 1.0, 0.0) - lmat
    lp = _bdot_raw(lmat, lmat, "NN", ps)
    n = int(math.log2(C))
    for s in range(1, n):
        p = p + _bdot_raw(p, lp, "NN", ps)
        if s < n - 1:
            lp = _bdot_raw(lp, lp, "NN", ps)
    return p


def _gated_norm(o, z, gnw):
    on = o * lax.rsqrt(jnp.mean(o * o, axis=-1, keepdims=True) + EPS) * gnw
    return on * _silu(z)


GDN_PG = 2
GDN_SG = 4


def _gdn_pairs(c, ba, gp, G):
    C, W, H = GDN_CHUNK, GDN_WIDTH, GDN_HEADS
    pairs = [(j, h) for j in range(G) for h in range(H)]
    cq, ck, cv = (jnp.stack([c[C * j:C * (j + 1), o + GDN_DIM * h:o + GDN_DIM * (h + 1)] for j, h in pairs]) for o in (0, W, 2 * W))
    b_col = jnp.stack([ba[C * j:C * (j + 1), h:h + 1] for j, h in pairs])
    a_col = jnp.stack([ba[C * j:C * (j + 1), H + h:H + h + 1] for j, h in pairs])
    alog = jnp.stack([gp[0:1, h:h + 1] for j, h in pairs])
    dtb = jnp.stack([gp[0:1, H + h:H + h + 1] for j, h in pairs])
    return pairs, (cq, ck, cv, b_col, a_col, alog, dtb)


def _gdn_pre_specs(S, G):
    C = GDN_CHUNK
    T = C * G
    return dict(
        cur=pl.BlockSpec((T, 3 * GDN_WIDTH), lambda i: (i, 0)),
        prev=pl.BlockSpec((8, 3 * GDN_WIDTH), lambda i: (jnp.maximum(i * (T // 8) - 1, 0), 0)),
        ba=pl.BlockSpec((T, 128), lambda i: (i, P_BA // 128)),
        cw=pl.BlockSpec((GDN_CONV, 3 * GDN_WIDTH), lambda i: (0, 0)),
        vec=pl.BlockSpec((1, 128), lambda i: (0, 0)),
        hd=pl.BlockSpec((GDN_HEADS, T, GDN_DIM), lambda i: (0, i, 0)),
        hc=pl.BlockSpec((GDN_HEADS, T, C), lambda i: (0, i, 0)),
        ge=pl.BlockSpec((G, GDN_HEADS, 8, 128), lambda i: (i, 0, 0, 0)),
    )


def _hd_shape(S, last=GDN_DIM):
    return jax.ShapeDtypeStruct((GDN_HEADS, S, last), F32)


def _gdn_pre(proj, conv_w, gp):
    S = proj.shape[0]
    C, G = GDN_CHUNK, GDN_PG
    nc = S // C
    sp = _gdn_pre_specs(S, G)

    def body(cur_ref, prev_ref, ba_ref, cw_ref, gp_ref, uv_ref, wk_ref, qd_ref, ke_ref, at_ref, ti_ref, ge_ref):
        prev = prev_ref[...] * jnp.where(pl.program_id(0) == 0, 0.0, 1.0)
        c = _silu(_conv_taps(jnp.concatenate([prev, cur_ref[...]], axis=0), cw_ref[...], GDN_CONV, C * G))
        pairs, args = _gdn_pairs(c, ba_ref[...], gp_ref[...], G)
        lmat, v, rk, q_dec, attn, k_end, g_end = _gdn_stage1(*args)
        t = _tri_inv(lmat)
        u_v = _bdot_raw(t, v, "NN", _GDN_PASSES["sol"])
        w_k = _bdot_raw(t, rk, "NN", _GDN_PASSES["sol"])
        for b, (j, h) in enumerate(pairs):
            rows = slice(C * j, C * (j + 1))
            uv_ref[h, rows, :] = u_v[b]
            wk_ref[h, rows, :] = w_k[b]
            qd_ref[h, rows, :] = q_dec[b]
            ke_ref[h, rows, :] = k_end[b]
            at_ref[h, rows, :] = attn[b]
            ti_ref[h, rows, :] = t[b]
            ge_ref[j, h] = jnp.broadcast_to(g_end[b], (8, 128))

    return pl.pallas_call(
        body, name="gdn_pre", grid=(nc // G,),
        in_specs=[sp["cur"], sp["prev"], sp["ba"], sp["cw"], sp["vec"]],
        out_specs=(sp["hd"], sp["hd"], sp["hd"], sp["hd"], sp["hc"], sp["hc"], sp["ge"]),
        out_shape=(_hd_shape(S), _hd_shape(S), _hd_shape(S), _hd_shape(S), _hd_shape(S, C), _hd_shape(S, C),
                   jax.ShapeDtypeStruct((nc, GDN_HEADS, 8, 128), F32)),
        compiler_params=_cparams(("parallel",)),
    )(proj, proj, proj, conv_w, gp)


def _gdn_scan_specs(S, G, rev):
    C = GDN_CHUNK
    T = C * G
    n = S // T
    ci = (lambda i: n - 1 - i) if rev else (lambda i: i)
    return dict(
        hd=pl.BlockSpec((GDN_HEADS, T, GDN_DIM), lambda i: (0, ci(i), 0)),
        hc=pl.BlockSpec((GDN_HEADS, T, C), lambda i: (0, ci(i), 0)),
        ge=pl.BlockSpec((G, GDN_HEADS, 8, 128), lambda i: (ci(i), 0, 0, 0)),
        z=pl.BlockSpec((T, GDN_WIDTH), lambda i: (ci(i), P_Z // GDN_WIDTH)),
        oa=pl.BlockSpec((T, GDN_WIDTH), lambda i: (ci(i), 0)),
        vec=pl.BlockSpec((1, 128), lambda i: (0, 0)),
        st=pl.BlockSpec((G, GDN_HEADS, GDN_DIM, GDN_DIM), lambda i: (ci(i), 0, 0, 0)),
    )


def _gdn_scan(u_v, w_k, q_dec, k_end, attn, g_end, proj, gnw):
    S = proj.shape[0]
    C, G = GDN_CHUNK, GDN_SG
    nc = S // C
    sp = _gdn_scan_specs(S, G, False)
    ps = _GDN_PASSES["scan"]

    def body(uv_ref, wk_ref, qd_ref, ke_ref, at_ref, ge_ref, z_ref, gnw_ref, oa_ref, st_ref, s_scr):
        @pl.when(pl.program_id(0) == 0)
        def _():
            s_scr[...] = jnp.zeros_like(s_scr)

        for j in range(G):
            rows = slice(C * j, C * (j + 1))
            st = s_scr[...]
            st_ref[j] = st
            u = uv_ref[:, rows, :] - _bdot_raw(wk_ref[:, rows, :], st, "NN", ps)
            o = _bdot_raw(qd_ref[:, rows, :], st, "NN", ps) + _bdot_raw(at_ref[:, rows, :], u, "NN", ps)
            s_scr[...] = ge_ref[j][:, 0:1, 0:1] * st + _bdot_raw(ke_ref[:, rows, :], u, "TN", ps)
            for h in range(GDN_HEADS):
                cols = slice(GDN_DIM * h, GDN_DIM * (h + 1))
                oa_ref[rows, cols] = _gated_norm(o[h], z_ref[rows, cols], gnw_ref[...])

    return pl.pallas_call(
        body, name="gdn_scan", grid=(nc // G,),
        in_specs=[sp["hd"], sp["hd"], sp["hd"], sp["hd"], sp["hc"], sp["ge"], sp["z"], sp["vec"]],
        out_specs=(sp["oa"], sp["st"]),
        out_shape=(jax.ShapeDtypeStruct((S, GDN_WIDTH + DIL_WIDTH), F32),
                   jax.ShapeDtypeStruct((nc, GDN_HEADS, GDN_DIM, GDN_DIM), F32)),
        scratch_shapes=[pltpu.VMEM((GDN_HEADS, GDN_DIM, GDN_DIM), F32)],
        compiler_params=_cparams(("arbitrary",)),
    )(u_v, w_k, q_dec, k_end, attn, g_end, proj, gnw)


def _gdn_scan_bwd(u_v, w_k, q_dec, k_end, attn, g_end, proj, gnw, states, d_oa):
    S = proj.shape[0]
    C, G = GDN_CHUNK, GDN_SG
    nc = S // C
    sp = _gdn_scan_specs(S, G, True)
    ps, pb = _GDN_PASSES["scan"], _GDN_PASSES["bwd"]

    def body(uv_ref, wk_ref, qd_ref, ke_ref, at_ref, ge_ref, z_ref, gnw_ref, st_ref, doa_ref,
             duv_ref, dwk_ref, dqd_ref, dke_ref, dat_ref, dge_ref, dz_ref, dgnw_ref, ds_scr):
        @pl.when(pl.program_id(0) == 0)
        def _():
            ds_scr[...] = jnp.zeros_like(ds_scr)
            dgnw_ref[...] = jnp.zeros_like(dgnw_ref)

        dgnw = jnp.zeros((1, 128), F32)
        for j in reversed(range(G)):
            rows = slice(C * j, C * (j + 1))
            st = st_ref[j]
            wk, qd, ke, at = wk_ref[:, rows, :], qd_ref[:, rows, :], ke_ref[:, rows, :], at_ref[:, rows, :]
            u = uv_ref[:, rows, :] - _bdot_raw(wk, st, "NN", ps)
            o = _bdot_raw(qd, st, "NN", ps) + _bdot_raw(at, u, "NN", ps)
            dos = []
            for h in range(GDN_HEADS):
                cols = slice(GDN_DIM * h, GDN_DIM * (h + 1))
                _, vjp2 = jax.vjp(_gated_norm, o[h], z_ref[rows, cols], gnw_ref[...])
                do_h, dz_h, dgn = vjp2(doa_ref[rows, cols])
                dz_ref[rows, cols] = dz_h
                dgnw = dgnw + dgn
                dos.append(do_h)
            do = jnp.stack(dos)
            ds_new = ds_scr[...]
            du = _bdot_raw(at, do, "TN", pb) + _bdot_raw(ke, ds_new, "NN", pb)
            duv_ref[:, rows, :] = du
            dat_ref[:, rows, :] = _bdot_raw(do, u, "NT", pb)
            dqd_ref[:, rows, :] = _bdot_raw(do, st, "NT", pb)
            dke_ref[:, rows, :] = _bdot_raw(u, ds_new, "NT", pb)
            dwk_ref[:, rows, :] = -_bdot_raw(du, st, "NT", pb)
            d_ge = jnp.sum(jnp.sum(st * ds_new, axis=2, keepdims=True), axis=1, keepdims=True)
            dge_ref[j] = jnp.broadcast_to(d_ge, (GDN_HEADS, 8, 128))
            ds_scr[...] = ge_ref[j][:, 0:1, 0:1] * ds_new + _bdot_raw(qd, do, "TN", pb) - _bdot_raw(wk, du, "TN", pb)
        dgnw_ref[...] += dgnw

    return pl.pallas_call(
        body, name="gdn_scan_bwd", grid=(nc // G,),
        in_specs=[sp["hd"], sp["hd"], sp["hd"], sp["hd"], sp["hc"], sp["ge"], sp["z"], sp["vec"], sp["st"], sp["oa"]],
        out_specs=(sp["hd"], sp["hd"], sp["hd"], sp["hd"], sp["hc"], sp["ge"], sp["oa"], sp["vec"]),
        out_shape=(_hd_shape(S), _hd_shape(S), _hd_shape(S), _hd_shape(S), _hd_shape(S, C),
                   jax.ShapeDtypeStruct((nc, GDN_HEADS, 8, 128), F32), jax.ShapeDtypeStruct((S, GDN_WIDTH), F32),
                   jax.ShapeDtypeStruct((1, 128), F32)),
        scratch_shapes=[pltpu.VMEM((GDN_HEADS, GDN_DIM, GDN_DIM), F32)],
        compiler_params=_cparams(("arbitrary",)),
    )(u_v, w_k, q_dec, k_end, attn, g_end, proj, gnw, states, d_oa)


def _gdn_post(proj, conv_w, gp, tinv, u_v, w_k, d_uv, d_wk, d_qd, d_ke, d_at, d_ge):
    S = proj.shape[0]
    C, G = GDN_CHUNK, GDN_PG
    nc = S // C
    sp = _gdn_pre_specs(S, G)
    pb = _GDN_PASSES["bwd"]

    def body(cur_ref, prev_ref, ba_ref, cw_ref, gp_ref, ti_ref, uv_ref, wk_ref, duv_ref, dwk_ref, dqd_ref, dke_ref,
             dat_ref, dge_ref, dpre_ref, dba_ref, dgp_ref):
        i = pl.program_id(0)

        @pl.when(i == 0)
        def _():
            dgp_ref[...] = jnp.zeros_like(dgp_ref)

        prev = prev_ref[...] * jnp.where(i == 0, 0.0, 1.0)
        pre = _conv_taps(jnp.concatenate([prev, cur_ref[...]], axis=0), cw_ref[...], GDN_CONV, C * G)
        sg = jax.nn.sigmoid(pre)
        dsilu = sg * (1.0 + pre * (1.0 - sg))
        pairs, args = _gdn_pairs(pre * sg, ba_ref[...], gp_ref[...], G)
        _, vjp1 = jax.vjp(functools.partial(_gdn_stage1, dot=_bdot), *args)

        def take(ref):
            return jnp.stack([ref[h, C * j:C * (j + 1), :] for j, h in pairs])

        t, u_v, w_k = take(ti_ref), take(uv_ref), take(wk_ref)
        d_v = _bdot_raw(t, take(duv_ref), "TN", pb)
        d_rk = _bdot_raw(t, take(dwk_ref), "TN", pb)
        d_l = -(_bdot_raw(d_v, u_v, "NT", pb) + _bdot_raw(d_rk, w_k, "NT", pb))
        d_ge = jnp.stack([dge_ref[j, h][0:1, 0:1] for j, h in pairs])
        dcq, dck, dcv, db, da, dalog, ddtb = vjp1((d_l, d_v, d_rk, take(dqd_ref), take(dat_ref), take(dke_ref), d_ge))
        lane = lax.broadcasted_iota(jnp.int32, (C, 128), 1)
        lane1 = lax.broadcasted_iota(jnp.int32, (1, 128), 1)
        dgp = jnp.zeros((1, 128), F32)
        for j in range(G):
            rows = slice(C * j, C * (j + 1))
            dba = jnp.zeros((C, 128), F32)
            for h in range(GDN_HEADS):
                b = GDN_HEADS * j + h
                for o_, dcx in ((0, dcq), (GDN_WIDTH, dck), (2 * GDN_WIDTH, dcv)):
                    cols = slice(o_ + GDN_DIM * h, o_ + GDN_DIM * (h + 1))
                    dpre_ref[rows, cols] = dcx[b] * dsilu[rows, cols]
                dba = dba + jnp.where(lane == h, db[b], 0.0) + jnp.where(lane == GDN_HEADS + h, da[b], 0.0)
                dgp = dgp + jnp.where(lane1 == h, dalog[b], 0.0) + jnp.where(lane1 == GDN_HEADS + h, ddtb[b], 0.0)
            dba_ref[rows, :] = dba
        dgp_ref[0:1, :] += dgp

    T = C * G
    return pl.pallas_call(
        body, name="gdn_post", grid=(nc // G,),
        in_specs=[sp["cur"], sp["prev"], sp["ba"], sp["cw"], sp["vec"], sp["hc"], sp["hd"], sp["hd"], sp["hd"], sp["hd"],
                  sp["hd"], sp["hd"], sp["hc"], sp["ge"]],
        out_specs=(sp["cur"], pl.BlockSpec((T, 128), lambda i: (i, 0)), pl.BlockSpec((8, 128), lambda i: (0, 0))),
        out_shape=(jax.ShapeDtypeStruct((S, 3 * GDN_WIDTH), F32), jax.ShapeDtypeStruct((S, 128), F32),
                   jax.ShapeDtypeStruct((8, 128), F32)),
        compiler_params=_cparams(("arbitrary",)),
    )(proj, proj, proj, conv_w, gp, tinv, u_v, w_k, d_uv, d_wk, d_qd, d_ke, d_at, d_ge)


def _conv_bwd(dpre, x, xcol0, w, K, name, tc):
    S, Cc = dpre.shape
    T = _pick_tile(S, 256)
    nt, ncol = S // T, Cc // tc
    xo = xcol0 // tc

    def body(d_ref, dn_ref, x_ref, xp_ref, w_ref, dx_ref, dw_ref):
        i = pl.program_id(1)
        dn = dn_ref[...] * jnp.where(i == nt - 1, 0.0, 1.0)
        dv = d_ref[...]
        ext_d = jnp.concatenate([dv, dn], axis=0)
        dx_ref[...] = _conv_taps_t(ext_d, w_ref[...], K, T).astype(dx_ref.dtype)
        xp = xp_ref[...] * jnp.where(i == 0, 0.0, 1.0)
        ext_x = jnp.concatenate([xp, x_ref[...]], axis=0)

        @pl.when(i == 0)
        def _():
            dw_ref[...] = jnp.zeros_like(dw_ref)

        for k in range(K):
            dw_ref[k:k + 1, :] += jnp.sum(dv * _shifted(ext_x, (K - 1) - k, 8, T), axis=0, keepdims=True)

    r8 = T // 8
    return pl.pallas_call(
        body, name=name, grid=(ncol, nt),
        in_specs=[pl.BlockSpec((T, tc), lambda j, i: (i, j)),
                  pl.BlockSpec((8, tc), lambda j, i: (jnp.minimum((i + 1) * r8, S // 8 - 1), j)),
                  pl.BlockSpec((T, tc), lambda j, i: (i, j + xo)),
                  pl.BlockSpec((8, tc), lambda j, i: (jnp.maximum(i * r8 - 1, 0), j + xo)),
                  pl.BlockSpec((K, tc), lambda j, i: (0, j))],
        out_specs=(pl.BlockSpec((T, tc), lambda j, i: (i, j)), pl.BlockSpec((K, tc), lambda j, i: (0, j))),
        out_shape=(jax.ShapeDtypeStruct((S, Cc), _MXU), jax.ShapeDtypeStruct((K, Cc), F32)),
        compiler_params=_cparams(("parallel", "arbitrary")),
    )(dpre, dpre, x, x, w)


def _dil_bias(nt, T):
    d = (np.arange(nt)[:, None, None] * T + np.arange(T)[None, None, :] - np.arange(T)[None, :, None])
    cnt = ((d >= 0) & (d <= 128)).astype(np.float64) + ((d >= 0) & (d % 4 == 0) & (d <= 512)) + ((d >= 0) & (d % 16 == 0))
    return jnp.asarray(np.where(cnt > 0, np.log(np.maximum(cnt, 1.0)), -1e30), dtype=F32)


def _attn_fwd(proj, mix):
    S = proj.shape[0]
    T = min(ATT_T, S)
    nt = S // T
    bias = _dil_bias(nt, T)
    scale = DIL_DIM ** -0.5
    npair = DIL_WIDTH // 128
    qb0, kb0, vb0 = P_QKVB // 128, (P_QKVB + DIL_WIDTH) // 128, (P_QKVB + 2 * DIL_WIDTH) // 128

    def body(q_ref, k_ref, v_ref, b_ref, mix_ref, o_ref, lse_ref):
        i = pl.program_id(1)
        qs = (q_ref[...] * scale).astype(_MXU)

        def step(j, carry):
            kt = k_ref[pl.ds(pl.multiple_of(j * T, T), T), :].astype(_MXU)
            vt = v_ref[pl.ds(pl.multiple_of(j * T, T), T), :].astype(_MXU)
            bt = b_ref[i - j]
            out = []
            for hh in range(2):
                m, l, acc = carry[hh]
                sl = slice(hh * DIL_DIM, (hh + 1) * DIL_DIM)
                s = lax.dot_general(kt[:, sl], qs[:, sl], (_NT, ((), ())), preferred_element_type=F32) + bt
                m_new = jnp.maximum(m, jnp.max(s, axis=0, keepdims=True))
                p = jnp.exp(s - m_new)
                a = jnp.exp(m - m_new)
                l = a * l + jnp.sum(p, axis=0, keepdims=True)
                acc = a * acc + lax.dot_general(vt[:, sl], p.astype(_MXU), (_TN, ((), ())), preferred_element_type=F32)
                out.append((m_new, l, acc))
            return tuple(out)

        init = tuple((jnp.full((1, T), -1e30, F32), jnp.zeros((1, T), F32), jnp.zeros((DIL_DIM, T), F32)) for _ in range(2))
        res = lax.fori_loop(0, i + 1, step, init)
        lse_ref[...] = jnp.zeros_like(lse_ref)
        for hh in range(2):
            m, l, acc = res[hh]
            o_ref[:, hh * DIL_DIM:(hh + 1) * DIL_DIM] = (acc / l).T
            lse_ref[hh:hh + 1, :] = m + jnp.log(l)

    return pl.pallas_call(
        body, name="attn_fwd", grid=(npair, nt),
        in_specs=[pl.BlockSpec((T, 128), lambda p, i: (i, qb0 + p)),
                  pl.BlockSpec((S, 128), lambda p, i: (0, kb0 + p)),
                  pl.BlockSpec((S, 128), lambda p, i: (0, vb0 + p)),
                  pl.BlockSpec((nt, T, T), lambda p, i: (0, 0, 0)), pl.BlockSpec(memory_space=pl.ANY)],
        out_specs=(pl.BlockSpec((T, 128), lambda p, i: (i, GDN_WIDTH // 128 + p)),
                   pl.BlockSpec((None, None, 8, T), lambda p, i: (p, i, 0, 0))),
        out_shape=(jax.ShapeDtypeStruct(mix.shape, F32), jax.ShapeDtypeStruct((npair, nt, 8, T), F32)),
        input_output_aliases={4: 0},
        compiler_params=_cparams(("parallel", "parallel")),
    )(proj, proj, proj, bias, mix)


def _attn_bwd(proj, mix, lse, d_mix):
    S = proj.shape[0]
    T = min(ATT_T, S)
    nt = S // T
    bias = _dil_bias(nt, T)
    scale = DIL_DIM ** -0.5
    npair = DIL_WIDTH // 128
    qb0, kb0, vb0 = P_QKVB // 128, (P_QKVB + DIL_WIDTH) // 128, (P_QKVB + 2 * DIL_WIDTH) // 128

    def body(q_ref, k_ref, v_ref, o_ref, lse_ref, do_ref, b_ref, dq_ref, dk_ref, dv_ref, dq_scr):
        j = pl.program_id(1)

        @pl.when(j == 0)
        def _():
            dq_scr[...] = jnp.zeros_like(dq_scr)

        kt = k_ref[...].astype(_MXU)
        vt = v_ref[...].astype(_MXU)
        ones = jnp.ones((8, DIL_DIM), F32)

        def step(i, carry):
            rows = pl.ds(pl.multiple_of(i * T, T), T)
            qs = (q_ref[rows, :] * scale).astype(_MXU)
            dov = do_ref[rows, :]
            prod = dov * o_ref[rows, :]
            lsev = lse_ref[i]
            dob = dov.astype(_MXU)
            bt = b_ref[i - j]
            out = []
            dqs = []
            for hh in range(2):
                dk, dv = carry[hh]
                sl = slice(hh * DIL_DIM, (hh + 1) * DIL_DIM)
                s = lax.dot_general(kt[:, sl], qs[:, sl], (_NT, ((), ())), preferred_element_type=F32) + bt
                p = jnp.exp(s - lsev[hh:hh + 1, :])
                delta = lax.dot_general(ones, prod[:, sl], (_NT, ((), ())), precision=_HI, preferred_element_type=F32)[0:1, :]
                dp = lax.dot_general(vt[:, sl], dob[:, sl], (_NT, ((), ())), preferred_element_type=F32)
                ds = (p * (dp - delta)).astype(_MXU)
                dv = dv + lax.dot_general(p.astype(_MXU), dob[:, sl], (_NN, ((), ())), preferred_element_type=F32)
                dk = dk + lax.dot_general(ds, qs[:, sl], (_NN, ((), ())), preferred_element_type=F32)
                dqs.append(lax.dot_general(ds, kt[:, sl], (_TN, ((), ())), preferred_element_type=F32) * scale)
                out.append((dk, dv))
            dq_scr[rows, :] += jnp.concatenate(dqs, axis=1)
            return tuple(out)

        init = tuple((jnp.zeros((T, DIL_DIM), F32), jnp.zeros((T, DIL_DIM), F32)) for _ in range(2))
        res = lax.fori_loop(j, nt, step, init)
        dk_ref[...] = jnp.concatenate([res[0][0], res[1][0]], axis=1).astype(dk_ref.dtype)
        dv_ref[...] = jnp.concatenate([res[0][1], res[1][1]], axis=1).astype(dv_ref.dtype)

        @pl.when(j == nt - 1)
        def _():
            dq_ref[...] = dq_scr[...].astype(dq_ref.dtype)

    full = lambda c0: pl.BlockSpec((S, 128), lambda p, j: (0, c0 + p))
    tile = lambda c0: pl.BlockSpec((T, 128), lambda p, j: (j, c0 + p))
    out3 = jax.ShapeDtypeStruct((S, DIL_WIDTH), _MXU)
    return pl.pallas_call(
        body, name="attn_bwd", grid=(npair, nt),
        in_specs=[full(qb0), tile(kb0), tile(vb0), full(GDN_WIDTH // 128),
                  pl.BlockSpec((None, nt, 8, T), lambda p, j: (p, 0, 0, 0)), full(GDN_WIDTH // 128),
                  pl.BlockSpec((nt, T, T), lambda p, j: (0, 0, 0))],
        out_specs=(full(0), tile(0), tile(0)),
        out_shape=(out3, out3, out3),
        scratch_shapes=[pltpu.VMEM((S, 128), F32)],
        compiler_params=_cparams(("parallel", "arbitrary")),
    )(proj, proj, proj, mix, lse, d_mix, bias)


def _ffn_act(up, cw):
    S, Cc = up.shape[0], up.shape[1] // 2
    T, tc = _pick_tile(S, 256), _pick_tile(Cc, 1536)
    r16 = T // 16
    nct = Cc // tc

    def body(g_ref, gp_ref, u_ref, up_ref, wg_ref, wu_ref, o_ref):
        keep = jnp.where(pl.program_id(1) == 0, 0.0, 1.0)
        cg = _conv_taps(jnp.concatenate([gp_ref[8:16, :].astype(F32) * keep, g_ref[...].astype(F32)], axis=0),
                        wg_ref[...], FFN_CONV, T)
        cu = _conv_taps(jnp.concatenate([up_ref[8:16, :].astype(F32) * keep, u_ref[...].astype(F32)], axis=0),
                        wu_ref[...], FFN_CONV, T)
        o_ref[...] = (_silu(cg) * cu).astype(o_ref.dtype)

    cur = lambda o: pl.BlockSpec((T, tc), lambda j, i: (i, j + o))
    prev = lambda o: pl.BlockSpec((16, tc), lambda j, i: (jnp.maximum(i * r16 - 1, 0), j + o))
    wsp = lambda o: pl.BlockSpec((FFN_CONV, tc), lambda j, i: (0, j + o))
    return pl.pallas_call(
        body, name="ffn_act", grid=(nct, S // T),
        in_specs=[cur(0), prev(0), cur(nct), prev(nct), wsp(0), wsp(nct)], out_specs=cur(0),
        out_shape=jax.ShapeDtypeStruct((S, Cc), _MXU),
        compiler_params=_cparams(("parallel", "parallel")),
    )(up, up, up, up, cw, cw)


def _ffn_act_bwd(d_act, up, cw):
    S, Cc = up.shape[0], up.shape[1] // 2
    T, tc = _pick_tile(S, 256), _pick_tile(Cc, 1536)
    r8, r16 = T // 8, T // 16
    nt = S // T
    nct = Cc // tc
    K = FFN_CONV

    def body(da_ref, dan_ref, g_ref, gp_ref, gn_ref, u_ref, up_ref, un_ref, wg_ref, wu_ref,
             dg_ref, du_ref, dwg_ref, dwu_ref):
        i = pl.program_id(1)
        keep_p = jnp.where(i == 0, 0.0, 1.0)
        keep_n = jnp.where(i == nt - 1, 0.0, 1.0)
        wg, wu = wg_ref[...], wu_ref[...]
        xg = jnp.concatenate([gp_ref[8:16, :].astype(F32) * keep_p, g_ref[...].astype(F32),
                              gn_ref[0:8, :].astype(F32) * keep_n], axis=0)
        xu = jnp.concatenate([up_ref[8:16, :].astype(F32) * keep_p, u_ref[...].astype(F32),
                              un_ref[0:8, :].astype(F32) * keep_n], axis=0)
        cg = _conv_taps(xg, wg, K, T + 8)
        cu = _conv_taps(xu, wu, K, T + 8)
        da = jnp.concatenate([da_ref[...], dan_ref[...] * keep_n], axis=0)
        sg = jax.nn.sigmoid(cg)
        d_cg = da * cu * (sg * (1.0 + cg * (1.0 - sg)))
        d_cu = da * (cg * sg)
        dg_ref[...] = _conv_taps_t(d_cg, wg, K, T).astype(dg_ref.dtype)
        du_ref[...] = _conv_taps_t(d_cu, wu, K, T).astype(du_ref.dtype)

        @pl.when(i == 0)
        def _():
            dwg_ref[...] = jnp.zeros_like(dwg_ref)
            dwu_ref[...] = jnp.zeros_like(dwu_ref)

        for k in range(K):
            dwg_ref[k:k + 1, :] += jnp.sum(d_cg[0:T, :] * _shifted(xg, (K - 1) - k, 8, T), axis=0, keepdims=True)
            dwu_ref[k:k + 1, :] += jnp.sum(d_cu[0:T, :] * _shifted(xu, (K - 1) - k, 8, T), axis=0, keepdims=True)

    cur = lambda o: pl.BlockSpec((T, tc), lambda j, i: (i, j + o))
    prev = lambda o: pl.BlockSpec((16, tc), lambda j, i: (jnp.maximum(i * r16 - 1, 0), j + o))
    nxt = lambda o: pl.BlockSpec((16, tc), lambda j, i: (jnp.minimum((i + 1) * r16, S // 16 - 1), j + o))
    nxt8 = pl.BlockSpec((8, tc), lambda j, i: (jnp.minimum((i + 1) * r8, S // 8 - 1), j))
    wsp = lambda o: pl.BlockSpec((K, tc), lambda j, i: (0, j + o))
    return pl.pallas_call(
        body, name="ffn_act_bwd", grid=(nct, nt),
        in_specs=[cur(0), nxt8, cur(0), prev(0), nxt(0), cur(nct), prev(nct), nxt(nct), wsp(0), wsp(nct)],
        out_specs=(cur(0), cur(0), wsp(0), wsp(0)),
        out_shape=(jax.ShapeDtypeStruct((S, Cc), _MXU), jax.ShapeDtypeStruct((S, Cc), _MXU),
                   jax.ShapeDtypeStruct((K, Cc), F32), jax.ShapeDtypeStruct((K, Cc), F32)),
        compiler_params=_cparams(("parallel", "arbitrary")),
    )(d_act, d_act, up, up, up, up, up, up, cw, cw)


def _local_step(x, tgt, h1, n1w, n2w, fnw, gp, gnw, wp, conv_w, fcw, rest_weights, early_grads):
    proj = _mm(h1, wp, "nn", name="proj")
    u_v, w_k, q_dec, k_end, attn, tinv, g_end = _gdn_pre(proj, conv_w, gp)
    mix, states = _gdn_scan(u_v, w_k, q_dec, k_end, attn, g_end, proj, gnw)
    mix, lse = _attn_fwd(proj, mix)
    w_out, w_up4, w_down = rest_weights([mix])
    x2 = _mm(mix, w_out, "nn", residual=x, name="outproj")
    h2 = _rmsnorm_fwd(x2, n2w, "norm2")
    up = _mm(h2, w_up4, "nn", b_blocks=True, out_dtype=_MXU, name="up")
    act = _ffn_act(up, fcw)
    x3 = _mm(act, w_down, "nn", residual=x2, name="down")
    loss, dx3, dx3n, d_fnw = _loss_head(x3, fnw, tgt, "loss_head")
    d_act = _mm(dx3n, w_down, "nt", name="d_act")
    d_wdown = _mm(act, dx3n, "tn", name="d_wdown")
    d_upg, d_upu, d_fcwg, d_fcwu = _ffn_act_bwd(d_act, up, fcw)
    d_wup = _mm(h2, d_upg, "tn", place=("blocks", N_CHIPS, 0), tn=w_up4.shape[2], name="d_wgate")
    d_wup = _mm(h2, d_upu, "tn", place=("blocks", N_CHIPS, N_CHIPS // 2), tn=w_up4.shape[2], into=d_wup, name="d_wup")
    token = early_grads[0](d_wup, d_wdown)
    d_h2 = _mm_nt_blocks([d_upg, d_upu], w_up4, "d_h2")
    dx2, dx2n, d_n2w = _rmsnorm_bwd(d_h2, x2, n2w + token[0:1, 0:1], dx3, "norm2_bwd")
    token = early_grads[1](dx2)
    d_mix = _mm(dx2n, w_out, "nt", name="d_mix")
    d_wout = _mm(mix, dx2n, "tn", name="d_wout")
    dq_b, dk_b, dv_b = _attn_bwd(proj, mix, lse, d_mix)
    d_uv, d_wk, d_qd, d_ke, d_at, d_ge, d_z, d_gnw = _gdn_scan_bwd(u_v, w_k, q_dec, k_end, attn, g_end, proj,
                                                                   gnw + token[0:1, 0:1], states, d_mix)
    d_pre, d_ba, d_gp = _gdn_post(proj, conv_w, gp, tinv, u_v, w_k, d_uv, d_wk, d_qd, d_ke, d_at, d_ge)
    d_qkva, d_convw = _conv_bwd(d_pre, proj, 0, conv_w, GDN_CONV, "gdn_conv_bwd", 512)
    d_proj = jnp.concatenate([d_qkva, d_z.astype(_MXU), dq_b, dk_b, dv_b, d_ba.astype(_MXU),
                              jnp.zeros((x.shape[0], P_COLS - P_BA - 128), _MXU)], axis=1)
    d_wp = _mm(h1, d_proj, "tn", name="d_wp")
    token = early_grads[2](d_wp, d_wout)
    d_h1 = _mm(d_proj, wp, "nt", name="d_h1")
    dx, _, d_n1w = _rmsnorm_bwd(d_h1, x, n1w + token[0:1, 0:1], dx2, "norm1_bwd")
    grads = dict(wp=d_wp, conv_w=d_convw, w_out=d_wout, w_up=d_wup, fcw_g=d_fcwg, fcw_u=d_fcwu, w_down=d_wdown,
                 n1w=d_n1w, n2w=d_n2w, fnw=d_fnw, gp=d_gp, gnw=d_gnw)
    return loss, dx, grads


_HBM = pl.BlockSpec(memory_space=pltpu.HBM)


def _pos():
    return lax.axis_index("x"), lax.axis_index("y"), lax.axis_index("c")


def _other_chips(x, y):
    return [(1 - x, y), (x, 1 - y), (1 - x, 1 - y)]


def _halvable(shape):
    return shape[0] % 32 == 0


def _rows_of_half(shape, half):
    if not _halvable(shape):
        return pl.ds(0, shape[0])
    return pl.ds(pl.multiple_of(half * (shape[0] // 2), 16), shape[0] // 2)


_SEM = pl.BlockSpec(memory_space=pltpu.SEMAPHORE)
_ANY = pl.BlockSpec(memory_space=pl.ANY)
_DATAFLOW = pltpu.SideEffectType.DATAFLOW_SIDE_EFFECTING


def _in_hbm(a):
    return pltpu.with_memory_space_constraint(a, pltpu.HBM)


def _halves_copy(src_refs, land_refs, send_sems, recv_sems, shapes, a, j, block, x, y, c):
    px, py = _other_chips(x, y)[j]
    rows = _rows_of_half(shapes[a], c)
    return pltpu.make_async_remote_copy(
        src_ref=src_refs[a].at[rows, :], dst_ref=land_refs[a].at[block, rows, :], send_sem=send_sems.at[3 * a + j],
        recv_sem=recv_sems.at[3 * a + j], device_id=(px, py, c), device_id_type=MESH)


def _gather_halves_start(shards, after, name):
    n = len(shards)
    shapes = [s.shape for s in shards]

    def body(*refs):
        ins, lands = refs[:n], refs[n:2 * n]
        send_sems, recv_sems = refs[2 * n + 1], refs[2 * n + 2]
        token = refs[-1]
        x, y, c = _pos()
        q = 2 * x + y
        for a in range(n):
            for j in range(3):
                _halves_copy(ins, lands, send_sems, recv_sems, shapes, a, j, q, x, y, c).start()
        token[...] = jnp.zeros_like(token)

    land_shapes = [(N_CHIPS,) + s.shape for s in shards]
    return pl.pallas_call(
        body, name=name,
        out_shape=(pltpu.SemaphoreType.DMA((3 * n,)), pltpu.SemaphoreType.DMA((3 * n,)),
                   *[pltpu.HBM(s.shape, s.dtype) for s in shards],
                   *[pltpu.HBM(ls, s.dtype) for ls, s in zip(land_shapes, shards)],
                   jax.ShapeDtypeStruct((8, 128), F32)),
        in_specs=[_HBM] * (2 * n) + [_ANY],
        out_specs=(_SEM, _SEM, *[_HBM] * (2 * n), pl.BlockSpec(memory_space=pltpu.VMEM)),
        input_output_aliases={a: 2 + a for a in range(2 * n)},
        compiler_params=pltpu.CompilerParams(has_side_effects=_DATAFLOW),
    )(*[_in_hbm(s) for s in shards], *[_in_hbm(lax.empty(ls, s.dtype)) for ls, s in zip(land_shapes, shards)], after)


def _gather_halves_wait(started, after, name):
    send_sems, recv_sems, *thru = started
    n = len(thru) // 2
    shapes = [t.shape for t in thru[:n]]

    def body(*refs):
        ins, lands = refs[:n], refs[n:2 * n]
        send_sems, recv_sems = refs[2 * n], refs[2 * n + 1]
        x, y, c = _pos()
        q = 2 * x + y
        chips = _other_chips(x, y)
        for a in range(n):
            for j, (px, py) in enumerate(chips):
                _halves_copy(ins, lands, send_sems, recv_sems, shapes, a, j, q, x, y, c).wait_send()
                _halves_copy(ins, lands, send_sems, recv_sems, shapes, a, j, 2 * px + py, x, y, c).wait_recv()

    outs = pl.pallas_call(
        body, name=name, out_shape=[pltpu.HBM(t.shape, t.dtype) for t in thru],
        in_specs=[_HBM] * (2 * n) + [_SEM, _SEM] + [_ANY] * len(after), out_specs=[_HBM] * (2 * n),
        input_output_aliases={a: a for a in range(2 * n)},
        compiler_params=pltpu.CompilerParams(has_side_effects=_DATAFLOW),
    )(*thru, send_sems, recv_sems, *after)
    return outs[:n], outs[n:]


def _sibling_fill(gathered, name):
    big = [a for a, g in enumerate(gathered) if _halvable(g.shape[1:])]
    n = len(gathered)

    def body(*refs):
        ins, outs = refs[:n], refs[n:2 * n]
        send_sems, recv_sems = refs[2 * n:]
        x, y, c = _pos()
        chips = _other_chips(x, y)

        def copy(k, j, half):
            a = big[k]
            px, py = chips[j]
            rows = _rows_of_half(gathered[a].shape[1:], half)
            return pltpu.make_async_remote_copy(
                src_ref=ins[a].at[2 * px + py, rows, :], dst_ref=outs[a].at[2 * px + py, rows, :],
                send_sem=send_sems.at[3 * k + j], recv_sem=recv_sems.at[3 * k + j],
                device_id=(x, y, 1 - c), device_id_type=MESH)

        sends = [copy(k, j, c) for k in range(len(big)) for j in range(3)]
        for cp in sends:
            cp.start()
        for k in range(len(big)):
            for j in range(3):
                copy(k, j, 1 - c).wait_recv()
        for cp in sends:
            cp.wait_send()

    return pl.pallas_call(
        body, name=name, in_specs=[_HBM] * n, out_specs=[_HBM] * n,
        out_shape=[jax.ShapeDtypeStruct(g.shape, g.dtype) for g in gathered],
        input_output_aliases={a: a for a in range(n)},
        scratch_shapes=[pltpu.SemaphoreType.DMA((3 * len(big),)), pltpu.SemaphoreType.DMA((3 * len(big),))],
    )(*gathered)


def _place_own(shards, gathered, cq, name):
    n = len(shards)
    steps = 4

    def body(cq_ref, *refs):
        for a in range(n):
            refs[2 * n + a][...] = refs[a][...]

    def tile(shape):
        return shape[0] // steps if _halvable(shape) else shape[0]

    in_specs = [pl.BlockSpec((tile(s.shape), s.shape[1]), (lambda i, s_: (i, 0)) if _halvable(s.shape) else (lambda i, s_: (0, 0)))
                for s in shards]
    in_specs += [pl.BlockSpec(memory_space=pl.ANY)] * n
    out_specs = [pl.BlockSpec((None, tile(s.shape), s.shape[1]),
                              (lambda i, s_: (s_[1], i, 0)) if _halvable(s.shape) else (lambda i, s_: (s_[1], 0, 0)))
                 for s in shards]
    gs = pltpu.PrefetchScalarGridSpec(num_scalar_prefetch=1, grid=(steps,), in_specs=in_specs, out_specs=out_specs)
    return pl.pallas_call(
        body, name=name, grid_spec=gs, out_shape=[jax.ShapeDtypeStruct(g.shape, g.dtype) for g in gathered],
        input_output_aliases={1 + n + a: a for a in range(n)},
        compiler_params=_cparams(("arbitrary",)),
    )(cq, *shards, *gathered)


def _half_rows(ref, c, rh):
    return ref.at[:, pl.ds(pl.multiple_of(c * rh, 8), rh), :]


def _chips_copy(src_refs, land_refs, send_sems, recv_sems, a, j, x, y, c):
    px, py = _other_chips(x, y)[j]
    return pltpu.make_async_remote_copy(src_ref=src_refs[a].at[2 * px + py], dst_ref=land_refs[a].at[j],
                                        send_sem=send_sems.at[3 * a + j], recv_sem=recv_sems.at[3 * a + j],
                                        device_id=(px, py, c), device_id_type=MESH)


def _grad_chips_start(parts, name):
    n = len(parts)

    def body(*refs):
        ins, lands = refs[:n], refs[n:2 * n]
        send_sems, recv_sems = refs[2 * n], refs[2 * n + 1]
        token = refs[-1]
        x, y, c = _pos()
        for a in range(n):
            for j in range(3):
                _chips_copy(ins, lands, send_sems, recv_sems, a, j, x, y, c).start()
        token[...] = jnp.zeros_like(token)

    land_shapes = [(3,) + p.shape[1:] for p in parts]
    return pl.pallas_call(
        body, name=name,
        out_shape=(pltpu.SemaphoreType.DMA((3 * n,)), pltpu.SemaphoreType.DMA((3 * n,)),
                   *[pltpu.HBM(p.shape, p.dtype) for p in parts],
                   *[pltpu.HBM(ls, p.dtype) for ls, p in zip(land_shapes, parts)],
                   jax.ShapeDtypeStruct((8, 128), F32)),
        in_specs=[_HBM] * (2 * n),
        out_specs=(_SEM, _SEM, *[_HBM] * (2 * n), pl.BlockSpec(memory_space=pltpu.VMEM)),
        input_output_aliases={a: 2 + a for a in range(2 * n)},
        compiler_params=pltpu.CompilerParams(has_side_effects=_DATAFLOW),
    )(*[_in_hbm(p) for p in parts], *[_in_hbm(lax.empty(ls, p.dtype)) for ls, p in zip(land_shapes, parts)])


def _grad_chips_wait(started, after, name):
    send_sems, recv_sems, *thru = started
    n = len(thru) // 2

    def body(*refs):
        ins, lands = refs[:n], refs[n:2 * n]
        send_sems, recv_sems = refs[2 * n], refs[2 * n + 1]
        x, y, c = _pos()
        for a in range(n):
            for j in range(3):
                cp = _chips_copy(ins, lands, send_sems, recv_sems, a, j, x, y, c)
                cp.wait_send()
                cp.wait_recv()

    outs = pl.pallas_call(
        body, name=name, out_shape=[pltpu.HBM(t.shape, t.dtype) for t in thru],
        in_specs=[_HBM] * (2 * n) + [_SEM, _SEM] + [_ANY] * len(after), out_specs=[_HBM] * (2 * n),
        input_output_aliases={a: a for a in range(2 * n)},
        compiler_params=pltpu.CompilerParams(has_side_effects=_DATAFLOW),
    )(*thru, send_sems, recv_sems, *after)
    return outs[n:]


def _sibling_copy(src_refs, land_refs, send_sems, recv_sems, rhs, a, c, x, y):
    return pltpu.make_async_remote_copy(src_ref=_half_rows(src_refs[a], 1 - c, rhs[a]), dst_ref=land_refs[a],
                                        send_sem=send_sems.at[a], recv_sem=recv_sems.at[a],
                                        device_id=(x, y, 1 - c), device_id_type=MESH)


def _grad_sibling_start(fams, name):
    n = len(fams)
    rhs = [f.shape[1] // 2 for f in fams]

    def body(*refs):
        ins, lands = refs[:n], refs[n:2 * n]
        send_sems, recv_sems = refs[2 * n], refs[2 * n + 1]
        token = refs[-1]
        x, y, c = _pos()
        for a in range(n):
            _sibling_copy(ins, lands, send_sems, recv_sems, rhs, a, c, x, y).start()
        token[...] = jnp.zeros_like(token)

    land_shapes = [(f.shape[0], f.shape[1] // 2, f.shape[2]) for f in fams]
    return pl.pallas_call(
        body, name=name,
        out_shape=(pltpu.SemaphoreType.DMA((n,)), pltpu.SemaphoreType.DMA((n,)),
                   *[pltpu.HBM(f.shape, f.dtype) for f in fams],
                   *[pltpu.HBM(ls, f.dtype) for ls, f in zip(land_shapes, fams)],
                   jax.ShapeDtypeStruct((8, 128), F32)),
        in_specs=[_HBM] * (2 * n),
        out_specs=(_SEM, _SEM, *[_HBM] * (2 * n), pl.BlockSpec(memory_space=pltpu.VMEM)),
        input_output_aliases={a: 2 + a for a in range(2 * n)},
        compiler_params=pltpu.CompilerParams(has_side_effects=_DATAFLOW),
    )(*[_in_hbm(f) for f in fams], *[_in_hbm(lax.empty(ls, f.dtype)) for ls, f in zip(land_shapes, fams)])


def _grad_sibling_wait(started, after, name):
    send_sems, recv_sems, *thru = started
    n = len(thru) // 2
    rhs = [t.shape[1] // 2 for t in thru[:n]]

    def body(*refs):
        ins, lands = refs[:n], refs[n:2 * n]
        send_sems, recv_sems = refs[2 * n], refs[2 * n + 1]
        x, y, c = _pos()
        for a in range(n):
            cp = _sibling_copy(ins, lands, send_sems, recv_sems, rhs, a, c, x, y)
            cp.wait_send()
            cp.wait_recv()

    outs = pl.pallas_call(
        body, name=name, out_shape=[pltpu.HBM(t.shape, t.dtype) for t in thru],
        in_specs=[_HBM] * (2 * n) + [_SEM, _SEM] + [_ANY] * len(after), out_specs=[_HBM] * (2 * n),
        input_output_aliases={a: a for a in range(2 * n)},
        compiler_params=pltpu.CompilerParams(has_side_effects=_DATAFLOW),
    )(*thru, send_sems, recv_sems, *after)
    return outs[:n], outs[n:]


def _grad_share(fulls, name, small=None):
    n = len(fulls)
    ns = 0 if small is None else 1
    rhs = [f.shape[0] // 2 for f in fulls]

    def body(*refs):
        ins, outs = refs[:n], refs[n + ns:2 * n + ns]
        send_sems, recv_sems = refs[2 * (n + ns)], refs[2 * (n + ns) + 1]
        x, y, c = _pos()

        def copy(a, half):
            rows = pl.ds(pl.multiple_of(half * rhs[a], 8), rhs[a])
            return pltpu.make_async_remote_copy(src_ref=ins[a].at[rows, :], dst_ref=outs[a].at[rows, :],
                                                send_sem=send_sems.at[7 * ns + a], recv_sem=recv_sems.at[7 * ns + a],
                                                device_id=(x, y, 1 - c), device_id_type=MESH)

        sends = [copy(a, c) for a in range(n)]
        for cp in sends:
            cp.start()
        if ns:
            small_ref, all_ref = refs[n], refs[2 * n + 1]
            me = 4 * x + 2 * y + c

            def peer(r):
                dx, dy, dc = (r >> 2) & 1, (r >> 1) & 1, r & 1
                return (x if dx == 0 else 1 - x), (y if dy == 0 else 1 - y), (c if dc == 0 else 1 - c)

            def small_copy(r, slot):
                return pltpu.make_async_remote_copy(src_ref=small_ref, dst_ref=all_ref.at[slot], send_sem=send_sems.at[r - 1],
                                                    recv_sem=recv_sems.at[r - 1], device_id=peer(r), device_id_type=MESH)

            smalls = [small_copy(r, me) for r in range(1, 8)]
            for cp in smalls:
                cp.start()
            for r in range(1, 8):
                px, py, pc = peer(r)
                small_copy(r, 4 * px + 2 * py + pc).wait_recv()
            sends = sends + smalls
        for a in range(n):
            copy(a, 1 - c).wait_recv()
        for cp in sends:
            cp.wait_send()

    return pl.pallas_call(
        body, name=name, in_specs=[_HBM] * (n + ns), out_specs=[_HBM] * (n + ns),
        out_shape=[jax.ShapeDtypeStruct(f.shape, f.dtype) for f in fulls]
        + ([jax.ShapeDtypeStruct((8,) + small.shape, small.dtype)] if ns else []),
        input_output_aliases={a: a for a in range(n)},
        scratch_shapes=[pltpu.SemaphoreType.DMA((7 * ns + n,)), pltpu.SemaphoreType.DMA((7 * ns + n,))],
    )(*fulls, *([small] if ns else []))


def _add_sibling(own, recv, cq, name):
    nb, R, Cc = own.shape
    Rh = R // 2

    def body(cq_ref, a_ref, b_ref, o32_ref, o16_ref):
        s = a_ref[...] + b_ref[...]
        o32_ref[...] = s
        o16_ref[...] = s.astype(o16_ref.dtype)

    sp = pl.BlockSpec((1, Rh, Cc), lambda b, s: (b, 0, 0))
    gs = pltpu.PrefetchScalarGridSpec(
        num_scalar_prefetch=1, grid=(nb,),
        in_specs=[pl.BlockSpec((1, Rh, Cc), lambda b, s: (b, s[0], 0)), sp], out_specs=[sp, sp])
    return pl.pallas_call(
        body, name=name, grid_spec=gs,
        out_shape=[jax.ShapeDtypeStruct((nb, Rh, Cc), F32), jax.ShapeDtypeStruct((nb, Rh, Cc), _MXU)],
        compiler_params=_cparams(("parallel",)),
    )(cq, own, recv)


def _add_chips(part32, recv3, cq, name):
    nb, Rh, Cc = part32.shape

    def body(cq_ref, a_ref, b_ref, o_ref):
        acc = a_ref[0]
        for j in range(3):
            acc = acc + b_ref[j].astype(F32)
        o_ref[...] = acc

    gs = pltpu.PrefetchScalarGridSpec(
        num_scalar_prefetch=1, grid=(1,),
        in_specs=[pl.BlockSpec((1, Rh, Cc), lambda i, s: (s[1], 0, 0)), pl.BlockSpec((3, Rh, Cc), lambda i, s: (0, 0, 0))],
        out_specs=pl.BlockSpec((Rh, Cc), lambda i, s: (s[0], 0)))
    return pl.pallas_call(
        body, name=name, grid_spec=gs, out_shape=jax.ShapeDtypeStruct((2 * Rh, Cc), F32),
        compiler_params=_cparams(("arbitrary",)),
    )(cq, part32, recv3)


def _sum_devices(small_all, small, me):
    def body(me_ref, all_ref, own_ref, o_ref):
        tot = None
        for d in range(8):
            term = jnp.where(me_ref[0] == d, own_ref[...], all_ref[d])
            tot = term if tot is None else tot + term
        o_ref[...] = tot

    gs = pltpu.PrefetchScalarGridSpec(
        num_scalar_prefetch=1, grid=(1,),
        in_specs=[pl.BlockSpec(small_all.shape, lambda i, s: (0, 0, 0)), pl.BlockSpec(small.shape, lambda i, s: (0, 0))],
        out_specs=pl.BlockSpec(small.shape, lambda i, s: (0, 0)))
    return pl.pallas_call(body, name="sum_devices", grid_spec=gs,
                          out_shape=jax.ShapeDtypeStruct(small.shape, F32))(me, small_all, small)


def _adamw(w, g, m, v, name):
    R, Cc = w.shape
    T = max([t for t in range(8, 257, 8) if R % t == 0], default=R)
    c1 = 1.0 / (1.0 - ADAM_B1 ** ADAM_STEP)
    c2 = 1.0 / (1.0 - ADAM_B2 ** ADAM_STEP)

    def body(w_ref, g_ref, m_ref, v_ref, d_ref, mo_ref, vo_ref):
        gv = g_ref[...]
        mn = ADAM_B1 * m_ref[...] + (1.0 - ADAM_B1) * gv
        vn = ADAM_B2 * v_ref[...] + (1.0 - ADAM_B2) * (gv * gv)
        mo_ref[...] = mn
        vo_ref[...] = vn
        d_ref[...] = -ADAM_LR * ((mn * c1) / (jnp.sqrt(vn * c2) + ADAM_EPS) + ADAM_WD * w_ref[...])

    sp = pl.BlockSpec((T, Cc), lambda i: (i, 0))
    sh = jax.ShapeDtypeStruct((R, Cc), F32)
    return pl.pallas_call(
        body, name=name, grid=(R // T,), in_specs=[sp] * 4, out_specs=(sp, sp, sp), out_shape=(sh, sh, sh),
        compiler_params=_cparams(("parallel",)),
    )(w, g, m, v)


SMALL_ROWS = 32
REPL_ROWS = 8


def _pad_lanes(v, n=D_MODEL):
    return jnp.pad(v, ((0, 0), (0, n - v.shape[1])))


def kernel(x, norm1_w, w_in, conv_qkv_w, a_log, dt_bias, gdn_norm_w, w_out, norm2_w, w_up, ffn_conv_w, w_down, final_norm_w, loss_target, m_norm1_w, m_w_in, m_conv_qkv_w, m_a_log, m_dt_bias, m_gdn_norm_w, m_w_out, m_norm2_w, m_w_up, m_ffn_conv_w, m_w_down, m_final_norm_w, v_norm1_w, v_w_in, v_conv_qkv_w, v_a_log, v_dt_bias, v_gdn_norm_w, v_w_out, v_norm2_w, v_w_up, v_ffn_conv_w, v_w_down, v_final_norm_w):
    c = lax.axis_index("c")
    q = 2 * lax.axis_index("x") + lax.axis_index("y")
    S = x.shape[1]
    cq = jnp.stack([c, q]).astype(jnp.int32)

    *in_started, in_token = _gather_halves_start([w_in[0].astype(_MXU), conv_qkv_w[0], ffn_conv_w[0]], x, "gather_in_start")
    w_in_l, m_w_in_l, v_w_in_l = (a + in_token[0:1, 0:1] for a in (w_in, m_w_in, v_w_in))
    h1 = _rmsnorm_fwd(x[0], norm1_w + in_token[0:1, 0:1], "norm1")
    rest = [(a[0] + in_token[0:1, 0:1]).astype(_MXU) for a in (w_out, w_up, w_down)]
    in_shards, got_in = _gather_halves_wait(in_started, [w_in_l, m_w_in_l, v_w_in_l, h1, *rest], "gather_in_wait")
    g_in, g_conv, g_fconv = _place_own(in_shards, _sibling_fill(got_in, "fill_in"), cq, "place_in")
    *rest_started, token = _gather_halves_start(rest, g_conv, "gather_rest_start")

    def rest_weights(after):
        shards, got = _gather_halves_wait(rest_started, after, "gather_rest_wait")
        got = _sibling_fill(got, "fill_rest")
        g_out, g_up, g_down = _place_own(shards, got, cq, "place_rest")
        return g_out.reshape(D_MODEL, D_MODEL), g_up, g_down.reshape(D_FF, D_MODEL)
    wp = _wp_assemble(g_in, [token])
    conv_f = jnp.concatenate([g_conv[i] for i in range(N_CHIPS)], axis=1)
    fcw = jnp.concatenate([g_fconv[i] for i in range(N_CHIPS)], axis=1)
    gp = _pad_lanes(jnp.concatenate([a_log, dt_bias], axis=1), 128)
    fnw = final_norm_w[None, :]
    early = {}

    def early_sibling(d_wup, d_wdown):
        *early["sibling"], tok = _grad_sibling_start([d_wup, d_wdown.reshape(N_CHIPS, D_FF // N_CHIPS, D_MODEL)],
                                                     "grad_sibling_early_start")
        return tok

    def early_chips(dx2):
        fams_e, got_e = _grad_sibling_wait(early["sibling"], [dx2], "grad_sibling_early_wait")
        early["parts"] = [_add_sibling(f, r, cq, "add_sibling_" + nm) for f, r, nm in zip(fams_e, got_e, ("w_up", "w_down"))]
        *early["started"], tok = _grad_chips_start([p[1] for p in early["parts"]], "grad_chips_start")
        return tok

    def late_sibling(d_wp, d_wout):
        *early["late_sibling"], tok = _grad_sibling_start(
            [_win_split(d_wp), d_wout.reshape(N_CHIPS, D_MODEL // N_CHIPS, D_MODEL)], "grad_sibling_late_start")
        return tok

    early_grads = (early_sibling, early_chips, late_sibling)

    loss_l, dx, g = _local_step(x[0], loss_target[0], h1, norm1_w, norm2_w + token[0:1, 0:1], fnw, gp, gdn_norm_w, wp,
                                conv_f, fcw, rest_weights, early_grads)
    n_fc = FFN_CONV * D_FF

    def rows_of(v):
        flat = v.reshape(-1)
        return jnp.pad(flat, (0, -flat.shape[0] % D_MODEL)).reshape(-1, D_MODEL)

    gp_row = _pad_lanes(jnp.concatenate([g["gp"][0:1, 0:8], loss_l[0:1, 0:1]], axis=1))
    small = jnp.concatenate([g["n1w"], g["n2w"], g["fnw"], gp_row, _pad_lanes(g["gnw"]),
                             rows_of(g["conv_w"]), rows_of(g["fcw_g"]), rows_of(g["fcw_u"])], axis=0)
    small = jnp.pad(small, ((0, SMALL_ROWS - small.shape[0]), (0, 0)))
    fams, got = _grad_sibling_wait(early["late_sibling"], [dx], "grad_sibling_late_wait")
    parts = [_add_sibling(f, r, cq, "add_sibling_" + nm) for f, r, nm in zip(fams, got, ("w_in", "w_out"))]
    *late_started, late_token = _grad_chips_start([p[1] for p in parts], "grad_chips_late_start")
    got3_e = _grad_chips_wait(early["started"], [dx, g["wp"], late_token], "grad_chips_wait")
    g_w_up, g_w_down = _grad_share(
        [_add_chips(p[0], r3, cq, "add_chips_" + nm) for p, r3, nm in zip(early["parts"], got3_e, ("w_up", "w_down"))],
        "grad_share_early")
    big = {}

    def adamw_big(nm, w, gg, m, v):
        d_, m_, v_ = _adamw(w[0], gg, m[0], v[0], "adamw_" + nm)
        big[nm] = (gg[None], d_[None], m_[None], v_[None])

    adamw_big("w_up", w_up, g_w_up, m_w_up, v_w_up)
    adamw_big("w_down", w_down, g_w_down, m_w_down, v_w_down)
    got3 = _grad_chips_wait(late_started, [big["w_up"][1], big["w_down"][1]], "grad_chips_late_wait")
    g_w_in, g_w_out, small_all = _grad_share(
        [_add_chips(p[0], r3, cq, "add_chips_" + nm) for p, r3, nm in zip(parts, got3, ("w_in", "w_out"))],
        "grad_share_late", small)
    small_red = _sum_devices(small_all, small, (2 * q + c).astype(jnp.int32).reshape(1))
    loss = small_red[3, 8]
    r0 = 5
    r1 = r0 + GDN_CONV * 3 * GDN_WIDTH // D_MODEL
    r2 = r1 + -(-n_fc // D_MODEL)
    conv_red = small_red[r0:r1].reshape(GDN_CONV, 3 * GDN_WIDTH)
    fc_red = jnp.concatenate([small_red[r1:r2].reshape(-1)[:n_fc].reshape(FFN_CONV, D_FF),
                              small_red[r2:2 * r2 - r1].reshape(-1)[:n_fc].reshape(FFN_CONV, D_FF)], axis=1)
    g_conv_w = lax.dynamic_slice_in_dim(conv_red, q * (3 * GDN_WIDTH // N_CHIPS), 3 * GDN_WIDTH // N_CHIPS, axis=1)
    g_fconv_w = lax.dynamic_slice_in_dim(fc_red, q * (2 * D_FF // N_CHIPS), 2 * D_FF // N_CHIPS, axis=1)
    g_n1w, g_n2w, g_fnw = small_red[0:1], small_red[1:2], small_red[2]
    g_alog, g_dtb, g_gnw = small_red[3:4, 0:4], small_red[3:4, 4:8], small_red[4:5, 0:128]
    for nm, w, gg, m, v in (("w_in", w_in_l, g_w_in, m_w_in_l, v_w_in_l), ("conv_qkv_w", conv_qkv_w, g_conv_w, m_conv_qkv_w, v_conv_qkv_w),
                            ("w_out", w_out, g_w_out, m_w_out, v_w_out),
                            ("ffn_conv_w", ffn_conv_w, g_fconv_w, m_ffn_conv_w, v_ffn_conv_w)):
        adamw_big(nm, w, gg, m, v)

    def pack_small(n1, n2, fn, al, db, gn):
        return jnp.concatenate([n1, n2, fn[None, :], _pad_lanes(jnp.concatenate([al, db], axis=1)), _pad_lanes(gn),
                                jnp.zeros((REPL_ROWS - 5, D_MODEL), F32)], axis=0)

    sw = pack_small(norm1_w, norm2_w, final_norm_w, a_log, dt_bias, gdn_norm_w)
    sm = pack_small(m_norm1_w, m_norm2_w, m_final_norm_w, m_a_log, m_dt_bias, m_gdn_norm_w)
    sv = pack_small(v_norm1_w, v_norm2_w, v_final_norm_w, v_a_log, v_dt_bias, v_gdn_norm_w)
    sd, smn, svn = _adamw(sw, small_red[:REPL_ROWS], sm, sv, "adamw_small")

    def unpack_small(t):
        return dict(norm1_w=t[0:1], norm2_w=t[1:2], final_norm_w=t[2], a_log=t[3:4, 0:4], dt_bias=t[3:4, 4:8],
                    gdn_norm_w=t[4:5, 0:128])

    sg = dict(norm1_w=g_n1w, norm2_w=g_n2w, final_norm_w=g_fnw, a_log=g_alog, dt_bias=g_dtb, gdn_norm_w=g_gnw)
    sd, smn, svn = unpack_small(sd), unpack_small(smn), unpack_small(svn)
    names = ["norm1_w", "w_in", "conv_qkv_w", "a_log", "dt_bias", "gdn_norm_w", "w_out", "norm2_w", "w_up",
             "ffn_conv_w", "w_down", "final_norm_w"]
    grads = [big[n][0] if n in big else sg[n] for n in names]
    deltas = [big[n][1] if n in big else sd[n] for n in names]
    new_m = [big[n][2] if n in big else smn[n] for n in names]
    new_v = [big[n][3] if n in big else svn[n] for n in names]
    return (loss, dx[None], *grads, *deltas, *new_m, *new_v)
```

```python
import functools
import math

import numpy as np
import jax
import jax.numpy as jnp
from jax import lax
from jax.experimental import pallas as pl
from jax.experimental.pallas import tpu as pltpu

F32 = jnp.float32
BF16 = jnp.bfloat16
_MXU = jnp.bfloat16
_HI = lax.Precision.HIGHEST
EPS = 1e-6
V7X_VMEM_LIMIT = 56 * 1024 * 1024
MESH = pl.DeviceIdType.MESH

D_MODEL = 1024
GDN_HEADS, GDN_DIM, GDN_CHUNK, GDN_CONV = 4, 128, 64, 4
GDN_WIDTH = GDN_HEADS * GDN_DIM
DIL_HEADS, DIL_DIM = 8, 64
DIL_WIDTH = DIL_HEADS * DIL_DIM
D_FF, FFN_CONV = 2816, 3
IN_COLS = 3592
P_COLS = 3840
P_Z, P_QKVB, P_BA = 1536, 2048, 3584
ATT_T = 1024
ADAM_LR, ADAM_B1, ADAM_B2, ADAM_EPS, ADAM_WD, ADAM_STEP = 0.001, 0.9, 0.999, 1e-08, 0.01, 10
N_CHIPS = 4


def _cparams(sem=None, vmem=None):
    kw = {}
    if sem is not None:
        kw["dimension_semantics"] = sem
    if vmem is not None:
        kw["vmem_limit_bytes"] = vmem
    return pltpu.CompilerParams(**kw)


def _silu(x):
    return x * jax.nn.sigmoid(x)


def _pick_tile(n, cap):
    best = None
    for t in range(128, min(n, cap) + 1, 128):
        if n % t == 0:
            best = t
    return best or n


def _mm(a, b, mode, *, out_dtype=F32, residual=None, name, b_blocks=False, place=None, into=None, tn=None):
    if mode == "nn":
        M, K = a.shape
        N = b.shape[0] * b.shape[2] if b_blocks else b.shape[1]
    elif mode == "nt":
        (M, K), (N, _) = a.shape, b.shape
    else:
        (K, M), (_, N) = a.shape, b.shape
    tm = _pick_tile(M, 1024)
    tn = b.shape[2] if b_blocks else (tn or _pick_tile(N, 1536))

    def vmem(tm, tn):
        return 2 * (tm * K * a.dtype.itemsize + tn * K * b.dtype.itemsize
                    + tm * tn * (jnp.dtype(out_dtype).itemsize + (4 if residual is not None else 0))) + 3 * tm * tn * 4

    fixed_tn = b_blocks or (place is not None and place[0] == "blocks")
    while vmem(tm, tn) > 40 * 1024 * 1024:
        if (tm >= tn or fixed_tn) and tm % 256 == 0:
            tm //= 2
        elif tn % 256 == 0 and not fixed_tn:
            tn //= 2
        else:
            tm //= 2
    a_spec = pl.BlockSpec((K, tm), lambda j, i: (0, i)) if mode == "tn" else pl.BlockSpec((tm, K), lambda j, i: (i, 0))
    if b_blocks:
        b_spec = pl.BlockSpec((None, K, tn), lambda j, i: (j, 0, 0))
    else:
        b_spec = pl.BlockSpec((tn, K), lambda j, i: (j, 0)) if mode == "nt" else pl.BlockSpec((K, tn), lambda j, i: (0, j))
    r_spec = pl.BlockSpec((tm, tn), lambda j, i: (i, j))
    if place is None:
        o_spec, o_shape = r_spec, (M, N)
    elif place[0] == "rows":
        off = place[2] // tm
        o_spec, o_shape = pl.BlockSpec((tm, tn), lambda j, i: (i + off, j)), (place[1], N)
    else:
        off = place[2]
        o_spec, o_shape = pl.BlockSpec((None, tm, tn), lambda j, i: (j + off, i, 0)), (place[1], M, tn)
    dims = {"nn": (((1,), (0,)), ((), ())), "nt": (((1,), (1,)), ((), ())), "tn": (((0,), (0,)), ((), ()))}[mode]

    def body(*refs):
        a_ref, b_ref = refs[0], refs[1]
        o_ref = refs[-1]
        acc = lax.dot_general(a_ref[...].astype(_MXU), b_ref[...].astype(_MXU), dims, preferred_element_type=F32)
        if residual is not None:
            acc = acc + refs[2][...]
        o_ref[...] = acc.astype(out_dtype)

    ins, specs, alias = [a, b], [a_spec, b_spec], {}
    if residual is not None:
        ins.append(residual)
        specs.append(r_spec)
    if into is not None:
        alias = {len(ins): 0}
        ins.append(into)
        specs.append(pl.BlockSpec(memory_space=pl.ANY))
    return pl.pallas_call(
        body, name=name, grid=(N // tn, M // tm), in_specs=specs, out_specs=o_spec,
        out_shape=jax.ShapeDtypeStruct(o_shape, out_dtype), input_output_aliases=alias,
        compiler_params=_cparams(("parallel", "parallel"), V7X_VMEM_LIMIT),
    )(*ins)


def _mm_nt_blocks(a_list, b4, name):
    M = a_list[0].shape[0]
    nb, N, Kb = b4.shape
    tm, tn = _pick_tile(M, 1024), _pick_tile(N, 512)

    def body(a0_ref, a1_ref, b_ref, o_ref):
        acc = None
        for blk in range(nb):
            a_ref = (a0_ref, a1_ref)[blk // 2]
            lo = (blk % 2) * Kb
            t = lax.dot_general(a_ref[:, lo:lo + Kb].astype(_MXU), b_ref[blk].astype(_MXU), (((1,), (1,)), ((), ())),
                                preferred_element_type=F32)
            acc = t if acc is None else acc + t
        o_ref[...] = acc

    a_spec = pl.BlockSpec((tm, 2 * Kb), lambda j, i: (i, 0))
    return pl.pallas_call(
        body, name=name, grid=(N // tn, M // tm),
        in_specs=[a_spec, a_spec, pl.BlockSpec((nb, tn, Kb), lambda j, i: (0, j, 0))],
        out_specs=pl.BlockSpec((tm, tn), lambda j, i: (i, j)), out_shape=jax.ShapeDtypeStruct((M, N), F32),
        compiler_params=_cparams(("parallel", "parallel"), V7X_VMEM_LIMIT),
    )(a_list[0], a_list[1], b4)


def _wp_assemble(g_in, after=()):
    nb, Dm, Wb = g_in.shape
    T = 256
    n_lo = P_QKVB - 2 * Wb

    def body(g_ref, *rest):
        g2 = g_ref[2]
        rest[-1][...] = jnp.concatenate(
            [g_ref[0], g_ref[1], g2[:, :n_lo], g2[:, n_lo + 8:], g_ref[3], g2[:, n_lo:n_lo + 8],
             jnp.zeros((T, P_COLS - P_BA - 8), g_in.dtype)], axis=1)

    return pl.pallas_call(
        body, name="wp_assemble", grid=(Dm // T,),
        in_specs=[pl.BlockSpec((nb, T, Wb), lambda i: (0, i, 0))] + [pl.BlockSpec(memory_space=pl.ANY)] * len(after),
        out_specs=pl.BlockSpec((T, P_COLS), lambda i: (i, 0)), out_shape=jax.ShapeDtypeStruct((Dm, P_COLS), g_in.dtype),
        compiler_params=_cparams(("parallel",)),
    )(g_in, *after)


def _win_split(d_wp):
    Dm = d_wp.shape[0]
    Wb = IN_COLS // N_CHIPS
    T = 256

    def body(x_ref, o_ref):
        xv = x_ref[...]
        o_ref[0] = xv[:, 0:Wb]
        o_ref[1] = xv[:, Wb:2 * Wb]
        o_ref[2] = jnp.concatenate([xv[:, 2 * Wb:P_QKVB], xv[:, P_BA:P_BA + 8], xv[:, P_QKVB:3 * Wb - 8]], axis=1)
        o_ref[3] = xv[:, 3 * Wb - 8:P_BA]

    return pl.pallas_call(
        body, name="win_split", grid=(Dm // T,), in_specs=[pl.BlockSpec((T, P_COLS), lambda i: (i, 0))],
        out_specs=pl.BlockSpec((N_CHIPS, T, Wb), lambda i: (0, i, 0)),
        out_shape=jax.ShapeDtypeStruct((N_CHIPS, Dm, Wb), F32), compiler_params=_cparams(("parallel",)),
    )(d_wp)


def _rmsnorm_fwd(x, w, name):
    S, D = x.shape
    T = _pick_tile(S, 512)

    def body(x_ref, w_ref, o_ref):
        xv = x_ref[...]
        rs = lax.rsqrt(jnp.mean(xv * xv, axis=-1, keepdims=True) + EPS)
        o_ref[...] = (xv * rs * w_ref[...]).astype(o_ref.dtype)

    return pl.pallas_call(
        body, name=name, grid=(S // T,),
        in_specs=[pl.BlockSpec((T, D), lambda i: (i, 0)), pl.BlockSpec((1, D), lambda i: (0, 0))],
        out_specs=pl.BlockSpec((T, D), lambda i: (i, 0)),
        out_shape=jax.ShapeDtypeStruct((S, D), _MXU),
        compiler_params=_cparams(("parallel",)),
    )(x, w)


def _rmsnorm_bwd(dh, x, w, dres, name):
    S, D = x.shape
    T = _pick_tile(S, 512)

    def body(dh_ref, x_ref, w_ref, dres_ref, dx_ref, dxn_ref, dw_ref):
        xv = x_ref[...]
        rs = lax.rsqrt(jnp.mean(xv * xv, axis=-1, keepdims=True) + EPS)
        xn = xv * rs
        dhv = dh_ref[...]
        dxn = dhv * w_ref[...]
        dxv = dres_ref[...] + rs * (dxn - xn * jnp.mean(dxn * xn, axis=-1, keepdims=True))
        dx_ref[...] = dxv
        dxn_ref[...] = dxv.astype(dxn_ref.dtype)

        @pl.when(pl.program_id(0) == 0)
        def _():
            dw_ref[...] = jnp.zeros_like(dw_ref)

        dw_ref[...] += jnp.sum(dhv * xn, axis=0, keepdims=True)

    row = pl.BlockSpec((T, D), lambda i: (i, 0))
    vec = pl.BlockSpec((1, D), lambda i: (0, 0))
    return pl.pallas_call(
        body, name=name, grid=(S // T,), in_specs=[row, row, vec, row], out_specs=(row, row, vec),
        out_shape=(jax.ShapeDtypeStruct((S, D), F32), jax.ShapeDtypeStruct((S, D), _MXU), jax.ShapeDtypeStruct((1, D), F32)),
        compiler_params=_cparams(("arbitrary",)),
    )(dh, x, w, dres)


def _loss_head(x3, w, tgt, name):
    S, D = x3.shape
    T = _pick_tile(S, 512)

    def body(x_ref, w_ref, t_ref, loss_ref, dx_ref, dxn_ref, dw_ref):
        xv = x_ref[...]
        rs = lax.rsqrt(jnp.mean(xv * xv, axis=-1, keepdims=True) + EPS)
        xn = xv * rs
        err = xn * w_ref[...] - t_ref[...]
        dy = err * (1.0 / D)
        dxn = dy * w_ref[...]
        dxv = rs * (dxn - xn * jnp.mean(dxn * xn, axis=-1, keepdims=True))
        dx_ref[...] = dxv
        dxn_ref[...] = dxv.astype(dxn_ref.dtype)

        @pl.when(pl.program_id(0) == 0)
        def _():
            dw_ref[...] = jnp.zeros_like(dw_ref)
            loss_ref[...] = jnp.zeros_like(loss_ref)

        dw_ref[...] += jnp.sum(dy * xn, axis=0, keepdims=True)
        part = jnp.sum(jnp.sum(err * err, axis=-1, keepdims=True), axis=0, keepdims=True) * (0.5 / D)
        loss_ref[...] += jnp.broadcast_to(part, loss_ref.shape)

    row = pl.BlockSpec((T, D), lambda i: (i, 0))
    vec = pl.BlockSpec((1, D), lambda i: (0, 0))
    return pl.pallas_call(
        body, name=name, grid=(S // T,), in_specs=[row, vec, row],
        out_specs=(pl.BlockSpec((8, 128), lambda i: (0, 0)), row, row, vec),
        out_shape=(jax.ShapeDtypeStruct((8, 128), F32), jax.ShapeDtypeStruct((S, D), F32), jax.ShapeDtypeStruct((S, D), _MXU),
                   jax.ShapeDtypeStruct((1, D), F32)),
        compiler_params=_cparams(("arbitrary",)),
    )(x3, w, tgt)


def _shifted(ext, back, lo, n):
    if back == 0:
        return ext[lo:lo + n, :]
    return pltpu.roll(ext, back % ext.shape[0], 0)[lo:lo + n, :]


def _conv_windows(ext, K, T):
    return [_shifted(ext, (K - 1) - i, 8, T) for i in range(K)]


def _conv_taps(ext, w, K, T):
    out = None
    for i, win in enumerate(_conv_windows(ext, K, T)):
        term = win * w[i:i + 1, :]
        out = term if out is None else out + term
    return out


def _conv_taps_t(ext, w, K, T):
    out = None
    for i in range(K):
        term = _shifted(ext, i - (K - 1), 0, T) * w[i:i + 1, :]
        out = term if out is None else out + term
    return out


def _tri_masks(C):
    r = lax.broadcasted_iota(jnp.int32, (C, C), 0)
    c = lax.broadcasted_iota(jnp.int32, (C, C), 1)
    return r == c, r >= c, r > c, r <= c


_NN, _NT, _TN = ((1,), (0,)), ((1,), (1,)), ((0,), (0,))
_GDN_PASSES = dict(qk=1, inv=1, sol=1, scan=1, bwd=1)


def _bdot_raw(a, b, kind, passes):
    dims = ({"NN": ((2,), (1,)), "NT": ((2,), (2,)), "TN": ((1,), (1,))}[kind], ((0,), (0,)))
    if passes == 0:
        return lax.dot_general(a, b, dims, precision=_HI, preferred_element_type=F32)
    ah, bh = a.astype(BF16), b.astype(BF16)
    out = lax.dot_general(ah, bh, dims, preferred_element_type=F32)
    if passes == 3:
        al, bl = (a - ah.astype(F32)).astype(BF16), (b - bh.astype(F32)).astype(BF16)
        out = out + lax.dot_general(ah, bl, dims, preferred_element_type=F32) + lax.dot_general(al, bh, dims, preferred_element_type=F32)
    return out


@functools.partial(jax.custom_vjp, nondiff_argnums=(2, 3))
def _bdot(a, b, kind, passes):
    return _bdot_raw(a, b, kind, passes)


def _bdot_fwd(a, b, kind, passes):
    return _bdot_raw(a, b, kind, passes), (a, b)


def _bdot_bwd(kind, passes, res, ct):
    a, b = res
    if kind == "NN":
        return _bdot_raw(ct, b, "NT", passes), _bdot_raw(a, ct, "TN", passes)
    if kind == "NT":
        return _bdot_raw(ct, b, "NN", passes), _bdot_raw(ct, a, "TN", passes)
    return _bdot_raw(b, ct, "NT", passes), _bdot_raw(a, ct, "NN", passes)


_bdot.defvjp(_bdot_fwd, _bdot_bwd)


def _softplus(x):
    return jnp.maximum(x, 0.0) + jnp.log(1.0 + jnp.exp(-jnp.abs(x)))


def _gdn_stage1(cq, ck, cv, b_col, a_col, alog, dtb, dot=_bdot_raw):
    C = cq.shape[1]
    eye, incl, strict, incl_t = _tri_masks(C)
    qn = cq * lax.rsqrt(jnp.sum(cq * cq, axis=-1, keepdims=True) + EPS) * (GDN_DIM ** -0.5)
    kn = ck * lax.rsqrt(jnp.sum(ck * ck, axis=-1, keepdims=True) + EPS)
    beta = jax.nn.sigmoid(b_col)
    g = -jnp.exp(alog) * _softplus(a_col + dtb)
    g_row = jnp.sum(jnp.where(eye, g, 0.0), axis=1, keepdims=True)
    beta_row = jnp.sum(jnp.where(eye, beta, 0.0), axis=1, keepdims=True)
    gc_col = jnp.sum(jnp.where(incl, g_row, 0.0), axis=2, keepdims=True)
    gc_row = jnp.sum(jnp.where(incl_t, g, 0.0), axis=1, keepdims=True)
    dec = jnp.where(incl, jnp.exp(jnp.where(incl, gc_col - gc_row, 0.0)), 0.0)
    kk = dot(kn, kn, "NT", _GDN_PASSES["qk"])
    qk = dot(qn, kn, "NT", _GDN_PASSES["qk"])
    lmat = jnp.where(strict, dec * kk * beta_row, 0.0)
    attn = dec * qk * beta_row
    gam = jnp.exp(gc_col)
    gc_last = gc_col[:, C - 1:C, :]
    k_end = kn * (jnp.exp(gc_last - gc_col) * beta)
    return lmat, cv, gam * kn, gam * qn, attn, k_end, jnp.exp(gc_last)


def _tri_inv(lmat):
    C = lmat.shape[1]
    eye = _tri_masks(C)[0]
    ps = _GDN_PASSES["inv"]
    p = jnp.where(eye, 1.0, 0.0) - lmat
    lp = _bdot_raw(lmat, lmat, "NN", ps)
    n = int(math.log2(C))
    for s in range(1, n):
        p = p + _bdot_raw(p, lp, "NN", ps)
        if s < n - 1:
            lp = _bdot_raw(lp, lp, "NN", ps)
    return p


def _gated_norm(o, z, gnw):
    on = o * lax.rsqrt(jnp.mean(o * o, axis=-1, keepdims=True) + EPS) * gnw
    return on * _silu(z)


GDN_PG = 2
GDN_SG = 4


def _gdn_pairs(c, ba, gp, G):
    C, W, H = GDN_CHUNK, GDN_WIDTH, GDN_HEADS
    pairs = [(j, h) for j in range(G) for h in range(H)]
    cq, ck, cv = (jnp.stack([c[C * j:C * (j + 1), o + GDN_DIM * h:o + GDN_DIM * (h + 1)] for j, h in pairs]) for o in (0, W, 2 * W))
    b_col = jnp.stack([ba[C * j:C * (j + 1), h:h + 1] for j, h in pairs])
    a_col = jnp.stack([ba[C * j:C * (j + 1), H + h:H + h + 1] for j, h in pairs])
    alog = jnp.stack([gp[0:1, h:h + 1] for j, h in pairs])
    dtb = jnp.stack([gp[0:1, H + h:H + h + 1] for j, h in pairs])
    return pairs, (cq, ck, cv, b_col, a_col, alog, dtb)


def _gdn_pre_specs(S, G):
    C = GDN_CHUNK
    T = C * G
    return dict(
        cur=pl.BlockSpec((T, 3 * GDN_WIDTH), lambda i: (i, 0)),
        prev=pl.BlockSpec((8, 3 * GDN_WIDTH), lambda i: (jnp.maximum(i * (T // 8) - 1, 0), 0)),
        ba=pl.BlockSpec((T, 128), lambda i: (i, P_BA // 128)),
        cw=pl.BlockSpec((GDN_CONV, 3 * GDN_WIDTH), lambda i: (0, 0)),
        vec=pl.BlockSpec((1, 128), lambda i: (0, 0)),
        hd=pl.BlockSpec((GDN_HEADS, T, GDN_DIM), lambda i: (0, i, 0)),
        hc=pl.BlockSpec((GDN_HEADS, T, C), lambda i: (0, i, 0)),
        ge=pl.BlockSpec((G, GDN_HEADS, 8, 128), lambda i: (i, 0, 0, 0)),
    )


def _hd_shape(S, last=GDN_DIM):
    return jax.ShapeDtypeStruct((GDN_HEADS, S, last), F32)


def _gdn_pre(proj, conv_w, gp):
    S = proj.shape[0]
    C, G = GDN_CHUNK, GDN_PG
    nc = S // C
    sp = _gdn_pre_specs(S, G)

    def body(cur_ref, prev_ref, ba_ref, cw_ref, gp_ref, uv_ref, wk_ref, qd_ref, ke_ref, at_ref, ti_ref, ge_ref):
        prev = prev_ref[...] * jnp.where(pl.program_id(0) == 0, 0.0, 1.0)
        c = _silu(_conv_taps(jnp.concatenate([prev, cur_ref[...]], axis=0), cw_ref[...], GDN_CONV, C * G))
        pairs, args = _gdn_pairs(c, ba_ref[...], gp_ref[...], G)
        lmat, v, rk, q_dec, attn, k_end, g_end = _gdn_stage1(*args)
        t = _tri_inv(lmat)
        u_v = _bdot_raw(t, v, "NN", _GDN_PASSES["sol"])
        w_k = _bdot_raw(t, rk, "NN", _GDN_PASSES["sol"])
        for b, (j, h) in enumerate(pairs):
            rows = slice(C * j, C * (j + 1))
            uv_ref[h, rows, :] = u_v[b]
            wk_ref[h, rows, :] = w_k[b]
            qd_ref[h, rows, :] = q_dec[b]
            ke_ref[h, rows, :] = k_end[b]
            at_ref[h, rows, :] = attn[b]
            ti_ref[h, rows, :] = t[b]
            ge_ref[j, h] = jnp.broadcast_to(g_end[b], (8, 128))

    return pl.pallas_call(
        body, name="gdn_pre", grid=(nc // G,),
        in_specs=[sp["cur"], sp["prev"], sp["ba"], sp["cw"], sp["vec"]],
        out_specs=(sp["hd"], sp["hd"], sp["hd"], sp["hd"], sp["hc"], sp["hc"], sp["ge"]),
        out_shape=(_hd_shape(S), _hd_shape(S), _hd_shape(S), _hd_shape(S), _hd_shape(S, C), _hd_shape(S, C),
                   jax.ShapeDtypeStruct((nc, GDN_HEADS, 8, 128), F32)),
        compiler_params=_cparams(("parallel",)),
    )(proj, proj, proj, conv_w, gp)


def _gdn_scan_specs(S, G, rev):
    C = GDN_CHUNK
    T = C * G
    n = S // T
    ci = (lambda i: n - 1 - i) if rev else (lambda i: i)
    return dict(
        hd=pl.BlockSpec((GDN_HEADS, T, GDN_DIM), lambda i: (0, ci(i), 0)),
        hc=pl.BlockSpec((GDN_HEADS, T, C), lambda i: (0, ci(i), 0)),
        ge=pl.BlockSpec((G, GDN_HEADS, 8, 128), lambda i: (ci(i), 0, 0, 0)),
        z=pl.BlockSpec((T, GDN_WIDTH), lambda i: (ci(i), P_Z // GDN_WIDTH)),
        oa=pl.BlockSpec((T, GDN_WIDTH), lambda i: (ci(i), 0)),
        vec=pl.BlockSpec((1, 128), lambda i: (0, 0)),
        st=pl.BlockSpec((G, GDN_HEADS, GDN_DIM, GDN_DIM), lambda i: (ci(i), 0, 0, 0)),
    )


def _gdn_scan(u_v, w_k, q_dec, k_end, attn, g_end, proj, gnw):
    S = proj.shape[0]
    C, G = GDN_CHUNK, GDN_SG
    nc = S // C
    sp = _gdn_scan_specs(S, G, False)
    ps = _GDN_PASSES["scan"]

    def body(uv_ref, wk_ref, qd_ref, ke_ref, at_ref, ge_ref, z_ref, gnw_ref, oa_ref, st_ref, s_scr):
        @pl.when(pl.program_id(0) == 0)
        def _():
            s_scr[...] = jnp.zeros_like(s_scr)

        for j in range(G):
            rows = slice(C * j, C * (j + 1))
            st = s_scr[...]
            st_ref[j] = st
            u = uv_ref[:, rows, :] - _bdot_raw(wk_ref[:, rows, :], st, "NN", ps)
            o = _bdot_raw(qd_ref[:, rows, :], st, "NN", ps) + _bdot_raw(at_ref[:, rows, :], u, "NN", ps)
            s_scr[...] = ge_ref[j][:, 0:1, 0:1] * st + _bdot_raw(ke_ref[:, rows, :], u, "TN", ps)
            for h in range(GDN_HEADS):
                cols = slice(GDN_DIM * h, GDN_DIM * (h + 1))
                oa_ref[rows, cols] = _gated_norm(o[h], z_ref[rows, cols], gnw_ref[...])

    return pl.pallas_call(
        body, name="gdn_scan", grid=(nc // G,),
        in_specs=[sp["hd"], sp["hd"], sp["hd"], sp["hd"], sp["hc"], sp["ge"], sp["z"], sp["vec"]],
        out_specs=(sp["oa"], sp["st"]),
        out_shape=(jax.ShapeDtypeStruct((S, GDN_WIDTH + DIL_WIDTH), F32),
                   jax.ShapeDtypeStruct((nc, GDN_HEADS, GDN_DIM, GDN_DIM), F32)),
        scratch_shapes=[pltpu.VMEM((GDN_HEADS, GDN_DIM, GDN_DIM), F32)],
        compiler_params=_cparams(("arbitrary",)),
    )(u_v, w_k, q_dec, k_end, attn, g_end, proj, gnw)


def _gdn_scan_bwd(u_v, w_k, q_dec, k_end, attn, g_end, proj, gnw, states, d_oa):
    S = proj.shape[0]
    C, G = GDN_CHUNK, GDN_SG
    nc = S // C
    sp = _gdn_scan_specs(S, G, True)
    ps, pb = _GDN_PASSES["scan"], _GDN_PASSES["bwd"]

    def body(uv_ref, wk_ref, qd_ref, ke_ref, at_ref, ge_ref, z_ref, gnw_ref, st_ref, doa_ref,
             duv_ref, dwk_ref, dqd_ref, dke_ref, dat_ref, dge_ref, dz_ref, dgnw_ref, ds_scr):
        @pl.when(pl.program_id(0) == 0)
        def _():
            ds_scr[...] = jnp.zeros_like(ds_scr)
            dgnw_ref[...] = jnp.zeros_like(dgnw_ref)

        dgnw = jnp.zeros((1, 128), F32)
        for j in reversed(range(G)):
            rows = slice(C * j, C * (j + 1))
            st = st_ref[j]
            wk, qd, ke, at = wk_ref[:, rows, :], qd_ref[:, rows, :], ke_ref[:, rows, :], at_ref[:, rows, :]
            u = uv_ref[:, rows, :] - _bdot_raw(wk, st, "NN", ps)
            o = _bdot_raw(qd, st, "NN", ps) + _bdot_raw(at, u, "NN", ps)
            dos = []
            for h in range(GDN_HEADS):
                cols = slice(GDN_DIM * h, GDN_DIM * (h + 1))
                _, vjp2 = jax.vjp(_gated_norm, o[h], z_ref[rows, cols], gnw_ref[...])
                do_h, dz_h, dgn = vjp2(doa_ref[rows, cols])
                dz_ref[rows, cols] = dz_h
                dgnw = dgnw + dgn
                dos.append(do_h)
            do = jnp.stack(dos)
            ds_new = ds_scr[...]
            du = _bdot_raw(at, do, "TN", pb) + _bdot_raw(ke, ds_new, "NN", pb)
            duv_ref[:, rows, :] = du
            dat_ref[:, rows, :] = _bdot_raw(do, u, "NT", pb)
            dqd_ref[:, rows, :] = _bdot_raw(do, st, "NT", pb)
            dke_ref[:, rows, :] = _bdot_raw(u, ds_new, "NT", pb)
            dwk_ref[:, rows, :] = -_bdot_raw(du, st, "NT", pb)
            d_ge = jnp.sum(jnp.sum(st * ds_new, axis=2, keepdims=True), axis=1, keepdims=True)
            dge_ref[j] = jnp.broadcast_to(d_ge, (GDN_HEADS, 8, 128))
            ds_scr[...] = ge_ref[j][:, 0:1, 0:1] * ds_new + _bdot_raw(qd, do, "TN", pb) - _bdot_raw(wk, du, "TN", pb)
        dgnw_ref[...] += dgnw

    return pl.pallas_call(
        body, name="gdn_scan_bwd", grid=(nc // G,),
        in_specs=[sp["hd"], sp["hd"], sp["hd"], sp["hd"], sp["hc"], sp["ge"], sp["z"], sp["vec"], sp["st"], sp["oa"]],
        out_specs=(sp["hd"], sp["hd"], sp["hd"], sp["hd"], sp["hc"], sp["ge"], sp["oa"], sp["vec"]),
        out_shape=(_hd_shape(S), _hd_shape(S), _hd_shape(S), _hd_shape(S), _hd_shape(S, C),
                   jax.ShapeDtypeStruct((nc, GDN_HEADS, 8, 128), F32), jax.ShapeDtypeStruct((S, GDN_WIDTH), F32),
                   jax.ShapeDtypeStruct((1, 128), F32)),
        scratch_shapes=[pltpu.VMEM((GDN_HEADS, GDN_DIM, GDN_DIM), F32)],
        compiler_params=_cparams(("arbitrary",)),
    )(u_v, w_k, q_dec, k_end, attn, g_end, proj, gnw, states, d_oa)


def _gdn_post(proj, conv_w, gp, tinv, u_v, w_k, d_uv, d_wk, d_qd, d_ke, d_at, d_ge):
    S = proj.shape[0]
    C, G = GDN_CHUNK, GDN_PG
    nc = S // C
    sp = _gdn_pre_specs(S, G)
    pb = _GDN_PASSES["bwd"]

    def body(cur_ref, prev_ref, ba_ref, cw_ref, gp_ref, ti_ref, uv_ref, wk_ref, duv_ref, dwk_ref, dqd_ref, dke_ref,
             dat_ref, dge_ref, dpre_ref, dba_ref, dgp_ref):
        i = pl.program_id(0)

        @pl.when(i == 0)
        def _():
            dgp_ref[...] = jnp.zeros_like(dgp_ref)

        prev = prev_ref[...] * jnp.where(i == 0, 0.0, 1.0)
        pre = _conv_taps(jnp.concatenate([prev, cur_ref[...]], axis=0), cw_ref[...], GDN_CONV, C * G)
        sg = jax.nn.sigmoid(pre)
        dsilu = sg * (1.0 + pre * (1.0 - sg))
        pairs, args = _gdn_pairs(pre * sg, ba_ref[...], gp_ref[...], G)
        _, vjp1 = jax.vjp(functools.partial(_gdn_stage1, dot=_bdot), *args)

        def take(ref):
            return jnp.stack([ref[h, C * j:C * (j + 1), :] for j, h in pairs])

        t, u_v, w_k = take(ti_ref), take(uv_ref), take(wk_ref)
        d_v = _bdot_raw(t, take(duv_ref), "TN", pb)
        d_rk = _bdot_raw(t, take(dwk_ref), "TN", pb)
        d_l = -(_bdot_raw(d_v, u_v, "NT", pb) + _bdot_raw(d_rk, w_k, "NT", pb))
        d_ge = jnp.stack([dge_ref[j, h][0:1, 0:1] for j, h in pairs])
        dcq, dck, dcv, db, da, dalog, ddtb = vjp1((d_l, d_v, d_rk, take(dqd_ref), take(dat_ref), take(dke_ref), d_ge))
        lane = lax.broadcasted_iota(jnp.int32, (C, 128), 1)
        lane1 = lax.broadcasted_iota(jnp.int32, (1, 128), 1)
        dgp = jnp.zeros((1, 128), F32)
        for j in range(G):
            rows = slice(C * j, C * (j + 1))
            dba = jnp.zeros((C, 128), F32)
            for h in range(GDN_HEADS):
                b = GDN_HEADS * j + h
                for o_, dcx in ((0, dcq), (GDN_WIDTH, dck), (2 * GDN_WIDTH, dcv)):
                    cols = slice(o_ + GDN_DIM * h, o_ + GDN_DIM * (h + 1))
                    dpre_ref[rows, cols] = dcx[b] * dsilu[rows, cols]
                dba = dba + jnp.where(lane == h, db[b], 0.0) + jnp.where(lane == GDN_HEADS + h, da[b], 0.0)
                dgp = dgp + jnp.where(lane1 == h, dalog[b], 0.0) + jnp.where(lane1 == GDN_HEADS + h, ddtb[b], 0.0)
            dba_ref[rows, :] = dba
        dgp_ref[0:1, :] += dgp

    T = C * G
    return pl.pallas_call(
        body, name="gdn_post", grid=(nc // G,),
        in_specs=[sp["cur"], sp["prev"], sp["ba"], sp["cw"], sp["vec"], sp["hc"], sp["hd"], sp["hd"], sp["hd"], sp["hd"],
                  sp["hd"], sp["hd"], sp["hc"], sp["ge"]],
        out_specs=(sp["cur"], pl.BlockSpec((T, 128), lambda i: (i, 0)), pl.BlockSpec((8, 128), lambda i: (0, 0))),
        out_shape=(jax.ShapeDtypeStruct((S, 3 * GDN_WIDTH), F32), jax.ShapeDtypeStruct((S, 128), F32),
                   jax.ShapeDtypeStruct((8, 128), F32)),
        compiler_params=_cparams(("arbitrary",)),
    )(proj, proj, proj, conv_w, gp, tinv, u_v, w_k, d_uv, d_wk, d_qd, d_ke, d_at, d_ge)


def _conv_bwd(dpre, x, xcol0, w, K, name, tc):
    S, Cc = dpre.shape
    T = _pick_tile(S, 256)
    nt, ncol = S // T, Cc // tc
    xo = xcol0 // tc

    def body(d_ref, dn_ref, x_ref, xp_ref, w_ref, dx_ref, dw_ref):
        i = pl.program_id(1)
        dn = dn_ref[...] * jnp.where(i == nt - 1, 0.0, 1.0)
        dv = d_ref[...]
        ext_d = jnp.concatenate([dv, dn], axis=0)
        dx_ref[...] = _conv_taps_t(ext_d, w_ref[...], K, T).astype(dx_ref.dtype)
        xp = xp_ref[...] * jnp.where(i == 0, 0.0, 1.0)
        ext_x = jnp.concatenate([xp, x_ref[...]], axis=0)

        @pl.when(i == 0)
        def _():
            dw_ref[...] = jnp.zeros_like(dw_ref)

        for k in range(K):
            dw_ref[k:k + 1, :] += jnp.sum(dv * _shifted(ext_x, (K - 1) - k, 8, T), axis=0, keepdims=True)

    r8 = T // 8
    return pl.pallas_call(
        body, name=name, grid=(ncol, nt),
        in_specs=[pl.BlockSpec((T, tc), lambda j, i: (i, j)),
                  pl.BlockSpec((8, tc), lambda j, i: (jnp.minimum((i + 1) * r8, S // 8 - 1), j)),
                  pl.BlockSpec((T, tc), lambda j, i: (i, j + xo)),
                  pl.BlockSpec((8, tc), lambda j, i: (jnp.maximum(i * r8 - 1, 0), j + xo)),
                  pl.BlockSpec((K, tc), lambda j, i: (0, j))],
        out_specs=(pl.BlockSpec((T, tc), lambda j, i: (i, j)), pl.BlockSpec((K, tc), lambda j, i: (0, j))),
        out_shape=(jax.ShapeDtypeStruct((S, Cc), _MXU), jax.ShapeDtypeStruct((K, Cc), F32)),
        compiler_params=_cparams(("parallel", "arbitrary")),
    )(dpre, dpre, x, x, w)


def _dil_bias(nt, T):
    d = (np.arange(nt)[:, None, None] * T + np.arange(T)[None, None, :] - np.arange(T)[None, :, None])
    cnt = ((d >= 0) & (d <= 128)).astype(np.float64) + ((d >= 0) & (d % 4 == 0) & (d <= 512)) + ((d >= 0) & (d % 16 == 0))
    return jnp.asarray(np.where(cnt > 0, np.log(np.maximum(cnt, 1.0)), -1e30), dtype=F32)


def _attn_fwd(proj, mix):
    S = proj.shape[0]
    T = min(ATT_T, S)
    nt = S // T
    bias = _dil_bias(nt, T)
    scale = DIL_DIM ** -0.5
    npair = DIL_WIDTH // 128
    qb0, kb0, vb0 = P_QKVB // 128, (P_QKVB + DIL_WIDTH) // 128, (P_QKVB + 2 * DIL_WIDTH) // 128

    def body(q_ref, k_ref, v_ref, b_ref, mix_ref, o_ref, lse_ref):
        i = pl.program_id(1)
        qs = (q_ref[...] * scale).astype(_MXU)

        def step(j, carry):
            kt = k_ref[pl.ds(pl.multiple_of(j * T, T), T), :].astype(_MXU)
            vt = v_ref[pl.ds(pl.multiple_of(j * T, T), T), :].astype(_MXU)
            bt = b_ref[i - j]
            out = []
            for hh in range(2):
                m, l, acc = carry[hh]
                sl = slice(hh * DIL_DIM, (hh + 1) * DIL_DIM)
                s = lax.dot_general(kt[:, sl], qs[:, sl], (_NT, ((), ())), preferred_element_type=F32) + bt
                m_new = jnp.maximum(m, jnp.max(s, axis=0, keepdims=True))
                p = jnp.exp(s - m_new)
                a = jnp.exp(m - m_new)
                l = a * l + jnp.sum(p, axis=0, keepdims=True)
                acc = a * acc + lax.dot_general(vt[:, sl], p.astype(_MXU), (_TN, ((), ())), preferred_element_type=F32)
                out.append((m_new, l, acc))
            return tuple(out)

        init = tuple((jnp.full((1, T), -1e30, F32), jnp.zeros((1, T), F32), jnp.zeros((DIL_DIM, T), F32)) for _ in range(2))
        res = lax.fori_loop(0, i + 1, step, init)
        lse_ref[...] = jnp.zeros_like(lse_ref)
        for hh in range(2):
            m, l, acc = res[hh]
            o_ref[:, hh * DIL_DIM:(hh + 1) * DIL_DIM] = (acc / l).T
            lse_ref[hh:hh + 1, :] = m + jnp.log(l)

    return pl.pallas_call(
        body, name="attn_fwd", grid=(npair, nt),
        in_specs=[pl.BlockSpec((T, 128), lambda p, i: (i, qb0 + p)),
                  pl.BlockSpec((S, 128), lambda p, i: (0, kb0 + p)),
                  pl.BlockSpec((S, 128), lambda p, i: (0, vb0 + p)),
                  pl.BlockSpec((nt, T, T), lambda p, i: (0, 0, 0)), pl.BlockSpec(memory_space=pl.ANY)],
        out_specs=(pl.BlockSpec((T, 128), lambda p, i: (i, GDN_WIDTH // 128 + p)),
                   pl.BlockSpec((None, None, 8, T), lambda p, i: (p, i, 0, 0))),
        out_shape=(jax.ShapeDtypeStruct(mix.shape, F32), jax.ShapeDtypeStruct((npair, nt, 8, T), F32)),
        input_output_aliases={4: 0},
        compiler_params=_cparams(("parallel", "parallel")),
    )(proj, proj, proj, bias, mix)


def _attn_bwd(proj, mix, lse, d_mix):
    S = proj.shape[0]
    T = min(ATT_T, S)
    nt = S // T
    bias = _dil_bias(nt, T)
    scale = DIL_DIM ** -0.5
    npair = DIL_WIDTH // 128
    qb0, kb0, vb0 = P_QKVB // 128, (P_QKVB + DIL_WIDTH) // 128, (P_QKVB + 2 * DIL_WIDTH) // 128

    def body(q_ref, k_ref, v_ref, o_ref, lse_ref, do_ref, b_ref, dq_ref, dk_ref, dv_ref, dq_scr):
        j = pl.program_id(1)

        @pl.when(j == 0)
        def _():
            dq_scr[...] = jnp.zeros_like(dq_scr)

        kt = k_ref[...].astype(_MXU)
        vt = v_ref[...].astype(_MXU)
        ones = jnp.ones((8, DIL_DIM), F32)

        def step(i, carry):
            rows = pl.ds(pl.multiple_of(i * T, T), T)
            qs = (q_ref[rows, :] * scale).astype(_MXU)
            dov = do_ref[rows, :]
            prod = dov * o_ref[rows, :]
            lsev = lse_ref[i]
            dob = dov.astype(_MXU)
            bt = b_ref[i - j]
            out = []
            dqs = []
            for hh in range(2):
                dk, dv = carry[hh]
                sl = slice(hh * DIL_DIM, (hh + 1) * DIL_DIM)
                s = lax.dot_general(kt[:, sl], qs[:, sl], (_NT, ((), ())), preferred_element_type=F32) + bt
                p = jnp.exp(s - lsev[hh:hh + 1, :])
                delta = lax.dot_general(ones, prod[:, sl], (_NT, ((), ())), precision=_HI, preferred_element_type=F32)[0:1, :]
                dp = lax.dot_general(vt[:, sl], dob[:, sl], (_NT, ((), ())), preferred_element_type=F32)
                ds = (p * (dp - delta)).astype(_MXU)
                dv = dv + lax.dot_general(p.astype(_MXU), dob[:, sl], (_NN, ((), ())), preferred_element_type=F32)
                dk = dk + lax.dot_general(ds, qs[:, sl], (_NN, ((), ())), preferred_element_type=F32)
                dqs.append(lax.dot_general(ds, kt[:, sl], (_TN, ((), ())), preferred_element_type=F32) * scale)
                out.append((dk, dv))
            dq_scr[rows, :] += jnp.concatenate(dqs, axis=1)
            return tuple(out)

        init = tuple((jnp.zeros((T, DIL_DIM), F32), jnp.zeros((T, DIL_DIM), F32)) for _ in range(2))
        res = lax.fori_loop(j, nt, step, init)
        dk_ref[...] = jnp.concatenate([res[0][0], res[1][0]], axis=1).astype(dk_ref.dtype)
        dv_ref[...] = jnp.concatenate([res[0][1], res[1][1]], axis=1).astype(dv_ref.dtype)

        @pl.when(j == nt - 1)
        def _():
            dq_ref[...] = dq_scr[...].astype(dq_ref.dtype)

    full = lambda c0: pl.BlockSpec((S, 128), lambda p, j: (0, c0 + p))
    tile = lambda c0: pl.BlockSpec((T, 128), lambda p, j: (j, c0 + p))
    out3 = jax.ShapeDtypeStruct((S, DIL_WIDTH), _MXU)
    return pl.pallas_call(
        body, name="attn_bwd", grid=(npair, nt),
        in_specs=[full(qb0), tile(kb0), tile(vb0), full(GDN_WIDTH // 128),
                  pl.BlockSpec((None, nt, 8, T), lambda p, j: (p, 0, 0, 0)), full(GDN_WIDTH // 128),
                  pl.BlockSpec((nt, T, T), lambda p, j: (0, 0, 0))],
        out_specs=(full(0), tile(0), tile(0)),
        out_shape=(out3, out3, out3),
        scratch_shapes=[pltpu.VMEM((S, 128), F32)],
        compiler_params=_cparams(("parallel", "arbitrary")),
    )(proj, proj, proj, mix, lse, d_mix, bias)


def _ffn_act(up, cw):
    S, Cc = up.shape[0], up.shape[1] // 2
    T, tc = _pick_tile(S, 256), _pick_tile(Cc, 1536)
    r16 = T // 16
    nct = Cc // tc

    def body(g_ref, gp_ref, u_ref, up_ref, wg_ref, wu_ref, o_ref):
        keep = jnp.where(pl.program_id(1) == 0, 0.0, 1.0)
        cg = _conv_taps(jnp.concatenate([gp_ref[8:16, :].astype(F32) * keep, g_ref[...].astype(F32)], axis=0),
                        wg_ref[...], FFN_CONV, T)
        cu = _conv_taps(jnp.concatenate([up_ref[8:16, :].astype(F32) * keep, u_ref[...].astype(F32)], axis=0),
                        wu_ref[...], FFN_CONV, T)
        o_ref[...] = (_silu(cg) * cu).astype(o_ref.dtype)

    cur = lambda o: pl.BlockSpec((T, tc), lambda j, i: (i, j + o))
    prev = lambda o: pl.BlockSpec((16, tc), lambda j, i: (jnp.maximum(i * r16 - 1, 0), j + o))
    wsp = lambda o: pl.BlockSpec((FFN_CONV, tc), lambda j, i: (0, j + o))
    return pl.pallas_call(
        body, name="ffn_act", grid=(nct, S // T),
        in_specs=[cur(0), prev(0), cur(nct), prev(nct), wsp(0), wsp(nct)], out_specs=cur(0),
        out_shape=jax.ShapeDtypeStruct((S, Cc), _MXU),
        compiler_params=_cparams(("parallel", "parallel")),
    )(up, up, up, up, cw, cw)


def _ffn_act_bwd(d_act, up, cw):
    S, Cc = up.shape[0], up.shape[1] // 2
    T, tc = _pick_tile(S, 256), _pick_tile(Cc, 1536)
    r8, r16 = T // 8, T // 16
    nt = S // T
    nct = Cc // tc
    K = FFN_CONV

    def body(da_ref, dan_ref, g_ref, gp_ref, gn_ref, u_ref, up_ref, un_ref, wg_ref, wu_ref,
             dg_ref, du_ref, dwg_ref, dwu_ref):
        i = pl.program_id(1)
        keep_p = jnp.where(i == 0, 0.0, 1.0)
        keep_n = jnp.where(i == nt - 1, 0.0, 1.0)
        wg, wu = wg_ref[...], wu_ref[...]
        xg = jnp.concatenate([gp_ref[8:16, :].astype(F32) * keep_p, g_ref[...].astype(F32),
                              gn_ref[0:8, :].astype(F32) * keep_n], axis=0)
        xu = jnp.concatenate([up_ref[8:16, :].astype(F32) * keep_p, u_ref[...].astype(F32),
                              un_ref[0:8, :].astype(F32) * keep_n], axis=0)
        cg = _conv_taps(xg, wg, K, T + 8)
        cu = _conv_taps(xu, wu, K, T + 8)
        da = jnp.concatenate([da_ref[...], dan_ref[...] * keep_n], axis=0)
        sg = jax.nn.sigmoid(cg)
        d_cg = da * cu * (sg * (1.0 + cg * (1.0 - sg)))
        d_cu = da * (cg * sg)
        dg_ref[...] = _conv_taps_t(d_cg, wg, K, T).astype(dg_ref.dtype)
        du_ref[...] = _conv_taps_t(d_cu, wu, K, T).astype(du_ref.dtype)

        @pl.when(i == 0)
        def _():
            dwg_ref[...] = jnp.zeros_like(dwg_ref)
            dwu_ref[...] = jnp.zeros_like(dwu_ref)

        for k in range(K):
            dwg_ref[k:k + 1, :] += jnp.sum(d_cg[0:T, :] * _shifted(xg, (K - 1) - k, 8, T), axis=0, keepdims=True)
            dwu_ref[k:k + 1, :] += jnp.sum(d_cu[0:T, :] * _shifted(xu, (K - 1) - k, 8, T), axis=0, keepdims=True)

    cur = lambda o: pl.BlockSpec((T, tc), lambda j, i: (i, j + o))
    prev = lambda o: pl.BlockSpec((16, tc), lambda j, i: (jnp.maximum(i * r16 - 1, 0), j + o))
    nxt = lambda o: pl.BlockSpec((16, tc), lambda j, i: (jnp.minimum((i + 1) * r16, S // 16 - 1), j + o))
    nxt8 = pl.BlockSpec((8, tc), lambda j, i: (jnp.minimum((i + 1) * r8, S // 8 - 1), j))
    wsp = lambda o: pl.BlockSpec((K, tc), lambda j, i: (0, j + o))
    return pl.pallas_call(
        body, name="ffn_act_bwd", grid=(nct, nt),
        in_specs=[cur(0), nxt8, cur(0), prev(0), nxt(0), cur(nct), prev(nct), nxt(nct), wsp(0), wsp(nct)],
        out_specs=(cur(0), cur(0), wsp(0), wsp(0)),
        out_shape=(jax.ShapeDtypeStruct((S, Cc), _MXU), jax.ShapeDtypeStruct((S, Cc), _MXU),
                   jax.ShapeDtypeStruct((K, Cc), F32), jax.ShapeDtypeStruct((K, Cc), F32)),
        compiler_params=_cparams(("parallel", "arbitrary")),
    )(d_act, d_act, up, up, up, up, up, up, cw, cw)


def _local_step(x, tgt, h1, n1w, n2w, fnw, gp, gnw, wp, conv_w, fcw, rest_weights, early_grads):
    proj = _mm(h1, wp, "nn", name="proj")
    u_v, w_k, q_dec, k_end, attn, tinv, g_end = _gdn_pre(proj, conv_w, gp)
    mix, states = _gdn_scan(u_v, w_k, q_dec, k_end, attn, g_end, proj, gnw)
    mix, lse = _attn_fwd(proj, mix)
    w_out, w_up4, w_down = rest_weights([mix])
    x2 = _mm(mix, w_out, "nn", residual=x, name="outproj")
    h2 = _rmsnorm_fwd(x2, n2w, "norm2")
    up = _mm(h2, w_up4, "nn", b_blocks=True, out_dtype=_MXU, name="up")
    act = _ffn_act(up, fcw)
    x3 = _mm(act, w_down, "nn", residual=x2, name="down")
    loss, dx3, dx3n, d_fnw = _loss_head(x3, fnw, tgt, "loss_head")
    d_act = _mm(dx3n, w_down, "nt", name="d_act")
    d_wdown = _mm(act, dx3n, "tn", name="d_wdown")
    d_upg, d_upu, d_fcwg, d_fcwu = _ffn_act_bwd(d_act, up, fcw)
    d_wup = _mm(h2, d_upg, "tn", place=("blocks", N_CHIPS, 0), tn=w_up4.shape[2], name="d_wgate")
    d_wup = _mm(h2, d_upu, "tn", place=("blocks", N_CHIPS, N_CHIPS // 2), tn=w_up4.shape[2], into=d_wup, name="d_wup")
    token = early_grads[0](d_wup, d_wdown)
    d_h2 = _mm_nt_blocks([d_upg, d_upu], w_up4, "d_h2")
    dx2, dx2n, d_n2w = _rmsnorm_bwd(d_h2, x2, n2w + token[0:1, 0:1], dx3, "norm2_bwd")
    token = early_grads[1](dx2)
    d_mix = _mm(dx2n, w_out, "nt", name="d_mix")
    d_wout = _mm(mix, dx2n, "tn", name="d_wout")
    dq_b, dk_b, dv_b = _attn_bwd(proj, mix, lse, d_mix)
    d_uv, d_wk, d_qd, d_ke, d_at, d_ge, d_z, d_gnw = _gdn_scan_bwd(u_v, w_k, q_dec, k_end, attn, g_end, proj,
                                                                   gnw + token[0:1, 0:1], states, d_mix)
    d_pre, d_ba, d_gp = _gdn_post(proj, conv_w, gp, tinv, u_v, w_k, d_uv, d_wk, d_qd, d_ke, d_at, d_ge)
    d_qkva, d_convw = _conv_bwd(d_pre, proj, 0, conv_w, GDN_CONV, "gdn_conv_bwd", 512)
    d_proj = jnp.concatenate([d_qkva, d_z.astype(_MXU), dq_b, dk_b, dv_b, d_ba.astype(_MXU),
                              jnp.zeros((x.shape[0], P_COLS - P_BA - 128), _MXU)], axis=1)
    d_wp = _mm(h1, d_proj, "tn", name="d_wp")
    token = early_grads[2](d_wp, d_wout)
    d_h1 = _mm(d_proj, wp, "nt", name="d_h1")
    dx, _, d_n1w = _rmsnorm_bwd(d_h1, x, n1w + token[0:1, 0:1], dx2, "norm1_bwd")
    grads = dict(wp=d_wp, conv_w=d_convw, w_out=d_wout, w_up=d_wup, fcw_g=d_fcwg, fcw_u=d_fcwu, w_down=d_wdown,
                 n1w=d_n1w, n2w=d_n2w, fnw=d_fnw, gp=d_gp, gnw=d_gnw)
    return loss, dx, grads


_HBM = pl.BlockSpec(memory_space=pltpu.HBM)


def _pos():
    return lax.axis_index("x"), lax.axis_index("y"), lax.axis_index("c")


def _other_chips(x, y):
    return [(1 - x, y), (x, 1 - y), (1 - x, 1 - y)]


def _halvable(shape):
    return shape[0] % 32 == 0


def _rows_of_half(shape, half):
    if not _halvable(shape):
        return pl.ds(0, shape[0])
    return pl.ds(pl.multiple_of(half * (shape[0] // 2), 16), shape[0] // 2)


_SEM = pl.BlockSpec(memory_space=pltpu.SEMAPHORE)
_ANY = pl.BlockSpec(memory_space=pl.ANY)
_DATAFLOW = pltpu.SideEffectType.DATAFLOW_SIDE_EFFECTING


def _in_hbm(a):
    return pltpu.with_memory_space_constraint(a, pltpu.HBM)


def _halves_copy(src_refs, land_refs, send_sems, recv_sems, shapes, a, j, block, x, y, c):
    px, py = _other_chips(x, y)[j]
    rows = _rows_of_half(shapes[a], c)
    return pltpu.make_async_remote_copy(
        src_ref=src_refs[a].at[rows, :], dst_ref=land_refs[a].at[block, rows, :], send_sem=send_sems.at[3 * a + j],
        recv_sem=recv_sems.at[3 * a + j], device_id=(px, py, c), device_id_type=MESH)


def _gather_halves_start(shards, after, name):
    n = len(shards)
    shapes = [s.shape for s in shards]

    def body(*refs):
        ins, lands = refs[:n], refs[n:2 * n]
        send_sems, recv_sems = refs[2 * n + 1], refs[2 * n + 2]
        token = refs[-1]
        x, y, c = _pos()
        q = 2 * x + y
        for a in range(n):
            for j in range(3):
                _halves_copy(ins, lands, send_sems, recv_sems, shapes, a, j, q, x, y, c).start()
        token[...] = jnp.zeros_like(token)

    land_shapes = [(N_CHIPS,) + s.shape for s in shards]
    return pl.pallas_call(
        body, name=name,
        out_shape=(pltpu.SemaphoreType.DMA((3 * n,)), pltpu.SemaphoreType.DMA((3 * n,)),
                   *[pltpu.HBM(s.shape, s.dtype) for s in shards],
                   *[pltpu.HBM(ls, s.dtype) for ls, s in zip(land_shapes, shards)],
                   jax.ShapeDtypeStruct((8, 128), F32)),
        in_specs=[_HBM] * (2 * n) + [_ANY],
        out_specs=(_SEM, _SEM, *[_HBM] * (2 * n), pl.BlockSpec(memory_space=pltpu.VMEM)),
        input_output_aliases={a: 2 + a for a in range(2 * n)},
        compiler_params=pltpu.CompilerParams(has_side_effects=_DATAFLOW),
    )(*[_in_hbm(s) for s in shards], *[_in_hbm(lax.empty(ls, s.dtype)) for ls, s in zip(land_shapes, shards)], after)


def _gather_halves_wait(started, after, name):
    send_sems, recv_sems, *thru = started
    n = len(thru) // 2
    shapes = [t.shape for t in thru[:n]]

    def body(*refs):
        ins, lands = refs[:n], refs[n:2 * n]
        send_sems, recv_sems = refs[2 * n], refs[2 * n + 1]
        x, y, c = _pos()
        q = 2 * x + y
        chips = _other_chips(x, y)
        for a in range(n):
            for j, (px, py) in enumerate(chips):
                _halves_copy(ins, lands, send_sems, recv_sems, shapes, a, j, q, x, y, c).wait_send()
                _halves_copy(ins, lands, send_sems, recv_sems, shapes, a, j, 2 * px + py, x, y, c).wait_recv()

    outs = pl.pallas_call(
        body, name=name, out_shape=[pltpu.HBM(t.shape, t.dtype) for t in thru],
        in_specs=[_HBM] * (2 * n) + [_SEM, _SEM] + [_ANY] * len(after), out_specs=[_HBM] * (2 * n),
        input_output_aliases={a: a for a in range(2 * n)},
        compiler_params=pltpu.CompilerParams(has_side_effects=_DATAFLOW),
    )(*thru, send_sems, recv_sems, *after)
    return outs[:n], outs[n:]


def _sibling_fill(gathered, name):
    big = [a for a, g in enumerate(gathered) if _halvable(g.shape[1:])]
    n = len(gathered)

    def body(*refs):
        ins, outs = refs[:n], refs[n:2 * n]
        send_sems, recv_sems = refs[2 * n:]
        x, y, c = _pos()
        chips = _other_chips(x, y)

        def copy(k, j, half):
            a = big[k]
            px, py = chips[j]
            rows = _rows_of_half(gathered[a].shape[1:], half)
            return pltpu.make_async_remote_copy(
                src_ref=ins[a].at[2 * px + py, rows, :], dst_ref=outs[a].at[2 * px + py, rows, :],
                send_sem=send_sems.at[3 * k + j], recv_sem=recv_sems.at[3 * k + j],
                device_id=(x, y, 1 - c), device_id_type=MESH)

        sends = [copy(k, j, c) for k in range(len(big)) for j in range(3)]
        for cp in sends:
            cp.start()
        for k in range(len(big)):
            for j in range(3):
                copy(k, j, 1 - c).wait_recv()
        for cp in sends:
            cp.wait_send()

    return pl.pallas_call(
        body, name=name, in_specs=[_HBM] * n, out_specs=[_HBM] * n,
        out_shape=[jax.ShapeDtypeStruct(g.shape, g.dtype) for g in gathered],
        input_output_aliases={a: a for a in range(n)},
        scratch_shapes=[pltpu.SemaphoreType.DMA((3 * len(big),)), pltpu.SemaphoreType.DMA((3 * len(big),))],
    )(*gathered)


def _place_own(shards, gathered, cq, name):
    n = len(shards)
    steps = 4

    def body(cq_ref, *refs):
        for a in range(n):
            refs[2 * n + a][...] = refs[a][...]

    def tile(shape):
        return shape[0] // steps if _halvable(shape) else shape[0]

    in_specs = [pl.BlockSpec((tile(s.shape), s.shape[1]), (lambda i, s_: (i, 0)) if _halvable(s.shape) else (lambda i, s_: (0, 0)))
                for s in shards]
    in_specs += [pl.BlockSpec(memory_space=pl.ANY)] * n
    out_specs = [pl.BlockSpec((None, tile(s.shape), s.shape[1]),
                              (lambda i, s_: (s_[1], i, 0)) if _halvable(s.shape) else (lambda i, s_: (s_[1], 0, 0)))
                 for s in shards]
    gs = pltpu.PrefetchScalarGridSpec(num_scalar_prefetch=1, grid=(steps,), in_specs=in_specs, out_specs=out_specs)
    return pl.pallas_call(
        body, name=name, grid_spec=gs, out_shape=[jax.ShapeDtypeStruct(g.shape, g.dtype) for g in gathered],
        input_output_aliases={1 + n + a: a for a in range(n)},
        compiler_params=_cparams(("arbitrary",)),
    )(cq, *shards, *gathered)


def _half_rows(ref, c, rh):
    return ref.at[:, pl.ds(pl.multiple_of(c * rh, 8), rh), :]


def _chips_copy(src_refs, land_refs, send_sems, recv_sems, a, j, x, y, c):
    px, py = _other_chips(x, y)[j]
    return pltpu.make_async_remote_copy(src_ref=src_refs[a].at[2 * px + py], dst_ref=land_refs[a].at[j],
                                        send_sem=send_sems.at[3 * a + j], recv_sem=recv_sems.at[3 * a + j],
                                        device_id=(px, py, c), device_id_type=MESH)


def _grad_chips_start(parts, name):
    n = len(parts)

    def body(*refs):
        ins, lands = refs[:n], refs[n:2 * n]
        send_sems, recv_sems = refs[2 * n], refs[2 * n + 1]
        token = refs[-1]
        x, y, c = _pos()
        for a in range(n):
            for j in range(3):
                _chips_copy(ins, lands, send_sems, recv_sems, a, j, x, y, c).start()
        token[...] = jnp.zeros_like(token)

    land_shapes = [(3,) + p.shape[1:] for p in parts]
    return pl.pallas_call(
        body, name=name,
        out_shape=(pltpu.SemaphoreType.DMA((3 * n,)), pltpu.SemaphoreType.DMA((3 * n,)),
                   *[pltpu.HBM(p.shape, p.dtype) for p in parts],
                   *[pltpu.HBM(ls, p.dtype) for ls, p in zip(land_shapes, parts)],
                   jax.ShapeDtypeStruct((8, 128), F32)),
        in_specs=[_HBM] * (2 * n),
        out_specs=(_SEM, _SEM, *[_HBM] * (2 * n), pl.BlockSpec(memory_space=pltpu.VMEM)),
        input_output_aliases={a: 2 + a for a in range(2 * n)},
        compiler_params=pltpu.CompilerParams(has_side_effects=_DATAFLOW),
    )(*[_in_hbm(p) for p in parts], *[_in_hbm(lax.empty(ls, p.dtype)) for ls, p in zip(land_shapes, parts)])


def _grad_chips_wait(started, after, name):
    send_sems, recv_sems, *thru = started
    n = len(thru) // 2

    def body(*refs):
        ins, lands = refs[:n], refs[n:2 * n]
        send_sems, recv_sems = refs[2 * n], refs[2 * n + 1]
        x, y, c = _pos()
        for a in range(n):
            for j in range(3):
                cp = _chips_copy(ins, lands, send_sems, recv_sems, a, j, x, y, c)
                cp.wait_send()
                cp.wait_recv()

    outs = pl.pallas_call(
        body, name=name, out_shape=[pltpu.HBM(t.shape, t.dtype) for t in thru],
        in_specs=[_HBM] * (2 * n) + [_SEM, _SEM] + [_ANY] * len(after), out_specs=[_HBM] * (2 * n),
        input_output_aliases={a: a for a in range(2 * n)},
        compiler_params=pltpu.CompilerParams(has_side_effects=_DATAFLOW),
    )(*thru, send_sems, recv_sems, *after)
    return outs[n:]


def _sibling_copy(src_refs, land_refs, send_sems, recv_sems, rhs, a, c, x, y):
    return pltpu.make_async_remote_copy(src_ref=_half_rows(src_refs[a], 1 - c, rhs[a]), dst_ref=land_refs[a],
                                        send_sem=send_sems.at[a], recv_sem=recv_sems.at[a],
                                        device_id=(x, y, 1 - c), device_id_type=MESH)


def _grad_sibling_start(fams, name):
    n = len(fams)
    rhs = [f.shape[1] // 2 for f in fams]

    def body(*refs):
        ins, lands = refs[:n], refs[n:2 * n]
        send_sems, recv_sems = refs[2 * n], refs[2 * n + 1]
        token = refs[-1]
        x, y, c = _pos()
        for a in range(n):
            _sibling_copy(ins, lands, send_sems, recv_sems, rhs, a, c, x, y).start()
        token[...] = jnp.zeros_like(token)

    land_shapes = [(f.shape[0], f.shape[1] // 2, f.shape[2]) for f in fams]
    return pl.pallas_call(
        body, name=name,
        out_shape=(pltpu.SemaphoreType.DMA((n,)), pltpu.SemaphoreType.DMA((n,)),
                   *[pltpu.HBM(f.shape, f.dtype) for f in fams],
                   *[pltpu.HBM(ls, f.dtype) for ls, f in zip(land_shapes, fams)],
                   jax.ShapeDtypeStruct((8, 128), F32)),
        in_specs=[_HBM] * (2 * n),
        out_specs=(_SEM, _SEM, *[_HBM] * (2 * n), pl.BlockSpec(memory_space=pltpu.VMEM)),
        input_output_aliases={a: 2 + a for a in range(2 * n)},
        compiler_params=pltpu.CompilerParams(has_side_effects=_DATAFLOW),
    )(*[_in_hbm(f) for f in fams], *[_in_hbm(lax.empty(ls, f.dtype)) for ls, f in zip(land_shapes, fams)])


def _grad_sibling_wait(started, after, name):
    send_sems, recv_sems, *thru = started
    n = len(thru) // 2
    rhs = [t.shape[1] // 2 for t in thru[:n]]

    def body(*refs):
        ins, lands = refs[:n], refs[n:2 * n]
        send_sems, recv_sems = refs[2 * n], refs[2 * n + 1]
        x, y, c = _pos()
        for a in range(n):
            cp = _sibling_copy(ins, lands, send_sems, recv_sems, rhs, a, c, x, y)
            cp.wait_send()
            cp.wait_recv()

    outs = pl.pallas_call(
        body, name=name, out_shape=[pltpu.HBM(t.shape, t.dtype) for t in thru],
        in_specs=[_HBM] * (2 * n) + [_SEM, _SEM] + [_ANY] * len(after), out_specs=[_HBM] * (2 * n),
        input_output_aliases={a: a for a in range(2 * n)},
        compiler_params=pltpu.CompilerParams(has_side_effects=_DATAFLOW),
    )(*thru, send_sems, recv_sems, *after)
    return outs[:n], outs[n:]


def _grad_share(fulls, name, small=None):
    n = len(fulls)
    ns = 0 if small is None else 1
    rhs = [f.shape[0] // 2 for f in fulls]

    def body(*refs):
        ins, outs = refs[:n], refs[n + ns:2 * n + ns]
        send_sems, recv_sems = refs[2 * (n + ns)], refs[2 * (n + ns) + 1]
        x, y, c = _pos()

        def copy(a, half):
            rows = pl.ds(pl.multiple_of(half * rhs[a], 8), rhs[a])
            return pltpu.make_async_remote_copy(src_ref=ins[a].at[rows, :], dst_ref=outs[a].at[rows, :],
                                                send_sem=send_sems.at[7 * ns + a], recv_sem=recv_sems.at[7 * ns + a],
                                                device_id=(x, y, 1 - c), device_id_type=MESH)

        sends = [copy(a, c) for a in range(n)]
        for cp in sends:
            cp.start()
        if ns:
            small_ref, all_ref = refs[n], refs[2 * n + 1]
            me = 4 * x + 2 * y + c

            def peer(r):
                dx, dy, dc = (r >> 2) & 1, (r >> 1) & 1, r & 1
                return (x if dx == 0 else 1 - x), (y if dy == 0 else 1 - y), (c if dc == 0 else 1 - c)

            def small_copy(r, slot):
                return pltpu.make_async_remote_copy(src_ref=small_ref, dst_ref=all_ref.at[slot], send_sem=send_sems.at[r - 1],
                                                    recv_sem=recv_sems.at[r - 1], device_id=peer(r), device_id_type=MESH)

            smalls = [small_copy(r, me) for r in range(1, 8)]
            for cp in smalls:
                cp.start()
            for r in range(1, 8):
                px, py, pc = peer(r)
                small_copy(r, 4 * px + 2 * py + pc).wait_recv()
            sends = sends + smalls
        for a in range(n):
            copy(a, 1 - c).wait_recv()
        for cp in sends:
            cp.wait_send()

    return pl.pallas_call(
        body, name=name, in_specs=[_HBM] * (n + ns), out_specs=[_HBM] * (n + ns),
        out_shape=[jax.ShapeDtypeStruct(f.shape, f.dtype) for f in fulls]
        + ([jax.ShapeDtypeStruct((8,) + small.shape, small.dtype)] if ns else []),
        input_output_aliases={a: a for a in range(n)},
        scratch_shapes=[pltpu.SemaphoreType.DMA((7 * ns + n,)), pltpu.SemaphoreType.DMA((7 * ns + n,))],
    )(*fulls, *([small] if ns else []))


def _add_sibling(own, recv, cq, name):
    nb, R, Cc = own.shape
    Rh = R // 2

    def body(cq_ref, a_ref, b_ref, o32_ref, o16_ref):
        s = a_ref[...] + b_ref[...]
        o32_ref[...] = s
        o16_ref[...] = s.astype(o16_ref.dtype)

    sp = pl.BlockSpec((1, Rh, Cc), lambda b, s: (b, 0, 0))
    gs = pltpu.PrefetchScalarGridSpec(
        num_scalar_prefetch=1, grid=(nb,),
        in_specs=[pl.BlockSpec((1, Rh, Cc), lambda b, s: (b, s[0], 0)), sp], out_specs=[sp, sp])
    return pl.pallas_call(
        body, name=name, grid_spec=gs,
        out_shape=[jax.ShapeDtypeStruct((nb, Rh, Cc), F32), jax.ShapeDtypeStruct((nb, Rh, Cc), _MXU)],
        compiler_params=_cparams(("parallel",)),
    )(cq, own, recv)


def _add_chips(part32, recv3, cq, name):
    nb, Rh, Cc = part32.shape

    def body(cq_ref, a_ref, b_ref, o_ref):
        acc = a_ref[0]
        for j in range(3):
            acc = acc + b_ref[j].astype(F32)
        o_ref[...] = acc

    gs = pltpu.PrefetchScalarGridSpec(
        num_scalar_prefetch=1, grid=(1,),
        in_specs=[pl.BlockSpec((1, Rh, Cc), lambda i, s: (s[1], 0, 0)), pl.BlockSpec((3, Rh, Cc), lambda i, s: (0, 0, 0))],
        out_specs=pl.BlockSpec((Rh, Cc), lambda i, s: (s[0], 0)))
    return pl.pallas_call(
        body, name=name, grid_spec=gs, out_shape=jax.ShapeDtypeStruct((2 * Rh, Cc), F32),
        compiler_params=_cparams(("arbitrary",)),
    )(cq, part32, recv3)


def _sum_devices(small_all, small, me):
    def body(me_ref, all_ref, own_ref, o_ref):
        tot = None
        for d in range(8):
            term = jnp.where(me_ref[0] == d, own_ref[...], all_ref[d])
            tot = term if tot is None else tot + term
        o_ref[...] = tot

    gs = pltpu.PrefetchScalarGridSpec(
        num_scalar_prefetch=1, grid=(1,),
        in_specs=[pl.BlockSpec(small_all.shape, lambda i, s: (0, 0, 0)), pl.BlockSpec(small.shape, lambda i, s: (0, 0))],
        out_specs=pl.BlockSpec(small.shape, lambda i, s: (0, 0)))
    return pl.pallas_call(body, name="sum_devices", grid_spec=gs,
                          out_shape=jax.ShapeDtypeStruct(small.shape, F32))(me, small_all, small)


def _adamw(w, g, m, v, name):
    R, Cc = w.shape
    T = max([t for t in range(8, 257, 8) if R % t == 0], default=R)
    c1 = 1.0 / (1.0 - ADAM_B1 ** ADAM_STEP)
    c2 = 1.0 / (1.0 - ADAM_B2 ** ADAM_STEP)

    def body(w_ref, g_ref, m_ref, v_ref, d_ref, mo_ref, vo_ref):
        gv = g_ref[...]
        mn = ADAM_B1 * m_ref[...] + (1.0 - ADAM_B1) * gv
        vn = ADAM_B2 * v_ref[...] + (1.0 - ADAM_B2) * (gv * gv)
        mo_ref[...] = mn
        vo_ref[...] = vn
        d_ref[...] = -ADAM_LR * ((mn * c1) / (jnp.sqrt(vn * c2) + ADAM_EPS) + ADAM_WD * w_ref[...])

    sp = pl.BlockSpec((T, Cc), lambda i: (i, 0))
    sh = jax.ShapeDtypeStruct((R, Cc), F32)
    return pl.pallas_call(
        body, name=name, grid=(R // T,), in_specs=[sp] * 4, out_specs=(sp, sp, sp), out_shape=(sh, sh, sh),
        compiler_params=_cparams(("parallel",)),
    )(w, g, m, v)


SMALL_ROWS = 32
REPL_ROWS = 8


def _pad_lanes(v, n=D_MODEL):
    return jnp.pad(v, ((0, 0), (0, n - v.shape[1])))


def kernel(x, norm1_w, w_in, conv_qkv_w, a_log, dt_bias, gdn_norm_w, w_out, norm2_w, w_up, ffn_conv_w, w_down, final_norm_w, loss_target, m_norm1_w, m_w_in, m_conv_qkv_w, m_a_log, m_dt_bias, m_gdn_norm_w, m_w_out, m_norm2_w, m_w_up, m_ffn_conv_w, m_w_down, m_final_norm_w, v_norm1_w, v_w_in, v_conv_qkv_w, v_a_log, v_dt_bias, v_gdn_norm_w, v_w_out, v_norm2_w, v_w_up, v_ffn_conv_w, v_w_down, v_final_norm_w):
    c = lax.axis_index("c")
    q = 2 * lax.axis_index("x") + lax.axis_index("y")
    S = x.shape[1]
    cq = jnp.stack([c, q]).astype(jnp.int32)

    *in_started, in_token = _gather_halves_start([w_in[0].astype(_MXU), conv_qkv_w[0], ffn_conv_w[0]], x, "gather_in_start")
    (w_in_l, m_w_in_l, v_w_in_l, w_out_l, w_up_l, w_down_l, n1w_l), in_token = lax.optimization_barrier(
        ((w_in, m_w_in, v_w_in, w_out, w_up, w_down, norm1_w), in_token))
    h1 = _rmsnorm_fwd(x[0], n1w_l, "norm1")
    rest = [a[0].astype(_MXU) for a in (w_out_l, w_up_l, w_down_l)]
    in_shards, got_in = _gather_halves_wait(in_started, [w_in_l, m_w_in_l, v_w_in_l, h1, *rest], "gather_in_wait")
    g_in, g_conv, g_fconv = _place_own(in_shards, _sibling_fill(got_in, "fill_in"), cq, "place_in")
    *rest_started, token = _gather_halves_start(rest, g_conv, "gather_rest_start")

    def rest_weights(after):
        shards, got = _gather_halves_wait(rest_started, after, "gather_rest_wait")
        got = _sibling_fill(got, "fill_rest")
        g_out, g_up, g_down = _place_own(shards, got, cq, "place_rest")
        return g_out.reshape(D_MODEL, D_MODEL), g_up, g_down.reshape(D_FF, D_MODEL)
    wp = _wp_assemble(g_in, [token])
    conv_f = jnp.concatenate([g_conv[i] for i in range(N_CHIPS)], axis=1)
    fcw = jnp.concatenate([g_fconv[i] for i in range(N_CHIPS)], axis=1)
    gp = _pad_lanes(jnp.concatenate([a_log, dt_bias], axis=1), 128)
    fnw = final_norm_w[None, :]
    early = {}

    def early_sibling(d_wup, d_wdown):
        *early["sibling"], tok = _grad_sibling_start([d_wup, d_wdown.reshape(N_CHIPS, D_FF // N_CHIPS, D_MODEL)],
                                                     "grad_sibling_early_start")
        return tok

    def early_chips(dx2):
        fams_e, got_e = _grad_sibling_wait(early["sibling"], [dx2], "grad_sibling_early_wait")
        early["parts"] = [_add_sibling(f, r, cq, "add_sibling_" + nm) for f, r, nm in zip(fams_e, got_e, ("w_up", "w_down"))]
        *early["started"], tok = _grad_chips_start([p[1] for p in early["parts"]], "grad_chips_start")
        return tok

    def late_sibling(d_wp, d_wout):
        *early["late_sibling"], tok = _grad_sibling_start(
            [_win_split(d_wp), d_wout.reshape(N_CHIPS, D_MODEL // N_CHIPS, D_MODEL)], "grad_sibling_late_start")
        return tok

    early_grads = (early_sibling, early_chips, late_sibling)

    loss_l, dx, g = _local_step(x[0], loss_target[0], h1, norm1_w, norm2_w + token[0:1, 0:1], fnw, gp, gdn_norm_w, wp,
                                conv_f, fcw, rest_weights, early_grads)
    n_fc = FFN_CONV * D_FF

    def rows_of(v):
        flat = v.reshape(-1)
        return jnp.pad(flat, (0, -flat.shape[0] % D_MODEL)).reshape(-1, D_MODEL)

    gp_row = _pad_lanes(jnp.concatenate([g["gp"][0:1, 0:8], loss_l[0:1, 0:1]], axis=1))
    small = jnp.concatenate([g["n1w"], g["n2w"], g["fnw"], gp_row, _pad_lanes(g["gnw"]),
                             rows_of(g["conv_w"]), rows_of(g["fcw_g"]), rows_of(g["fcw_u"])], axis=0)
    small = jnp.pad(small, ((0, SMALL_ROWS - small.shape[0]), (0, 0)))
    fams, got = _grad_sibling_wait(early["late_sibling"], [dx], "grad_sibling_late_wait")
    parts = [_add_sibling(f, r, cq, "add_sibling_" + nm) for f, r, nm in zip(fams, got, ("w_in", "w_out"))]
    *late_started, late_token = _grad_chips_start([p[1] for p in parts], "grad_chips_late_start")
    got3_e = _grad_chips_wait(early["started"], [dx, g["wp"], late_token], "grad_chips_wait")
    g_w_up, g_w_down = _grad_share(
        [_add_chips(p[0], r3, cq, "add_chips_" + nm) for p, r3, nm in zip(early["parts"], got3_e, ("w_up", "w_down"))],
        "grad_share_early")
    big = {}

    def adamw_big(nm, w, gg, m, v):
        d_, m_, v_ = _adamw(w[0], gg, m[0], v[0], "adamw_" + nm)
        big[nm] = (gg[None], d_[None], m_[None], v_[None])

    adamw_big("w_up", w_up, g_w_up, m_w_up, v_w_up)
    adamw_big("w_down", w_down, g_w_down, m_w_down, v_w_down)
    got3 = _grad_chips_wait(late_started, [big["w_up"][1], big["w_down"][1]], "grad_chips_late_wait")
    g_w_in, g_w_out, small_all = _grad_share(
        [_add_chips(p[0], r3, cq, "add_chips_" + nm) for p, r3, nm in zip(parts, got3, ("w_in", "w_out"))],
        "grad_share_late", small)
    small_red = _sum_devices(small_all, small, (2 * q + c).astype(jnp.int32).reshape(1))
    loss = small_red[3, 8]
    r0 = 5
    r1 = r0 + GDN_CONV * 3 * GDN_WIDTH // D_MODEL
    r2 = r1 + -(-n_fc // D_MODEL)
    conv_red = small_red[r0:r1].reshape(GDN_CONV, 3 * GDN_WIDTH)
    fc_red = jnp.concatenate([small_red[r1:r2].reshape(-1)[:n_fc].reshape(FFN_CONV, D_FF),
                              small_red[r2:2 * r2 - r1].reshape(-1)[:n_fc].reshape(FFN_CONV, D_FF)], axis=1)
    g_conv_w = lax.dynamic_slice_in_dim(conv_red, q * (3 * GDN_WIDTH // N_CHIPS), 3 * GDN_WIDTH // N_CHIPS, axis=1)
    g_fconv_w = lax.dynamic_slice_in_dim(fc_red, q * (2 * D_FF // N_CHIPS), 2 * D_FF // N_CHIPS, axis=1)
    g_n1w, g_n2w, g_fnw = small_red[0:1], small_red[1:2], small_red[2]
    g_alog, g_dtb, g_gnw = small_red[3:4, 0:4], small_red[3:4, 4:8], small_red[4:5, 0:128]
    for nm, w, gg, m, v in (("w_in", w_in_l, g_w_in, m_w_in_l, v_w_in_l), ("conv_qkv_w", conv_qkv_w, g_conv_w, m_conv_qkv_w, v_conv_qkv_w),
                            ("w_out", w_out, g_w_out, m_w_out, v_w_out),
                            ("ffn_conv_w", ffn_conv_w, g_fconv_w, m_ffn_conv_w, v_ffn_conv_w)):
        adamw_big(nm, w, gg, m, v)

    def pack_small(n1, n2, fn, al, db, gn):
        return jnp.concatenate([n1, n2, fn[None, :], _pad_lanes(jnp.concatenate([al, db], axis=1)), _pad_lanes(gn),
                                jnp.zeros((REPL_ROWS - 5, D_MODEL), F32)], axis=0)

    sw = pack_small(norm1_w, norm2_w, final_norm_w, a_log, dt_bias, gdn_norm_w)
    sm = pack_small(m_norm1_w, m_norm2_w, m_final_norm_w, m_a_log, m_dt_bias, m_gdn_norm_w)
    sv = pack_small(v_norm1_w, v_norm2_w, v_final_norm_w, v_a_log, v_dt_bias, v_gdn_norm_w)
    sd, smn, svn = _adamw(sw, small_red[:REPL_ROWS], sm, sv, "adamw_small")

    def unpack_small(t):
        return dict(norm1_w=t[0:1], norm2_w=t[1:2], final_norm_w=t[2], a_log=t[3:4, 0:4], dt_bias=t[3:4, 4:8],
                    gdn_norm_w=t[4:5, 0:128])

    sg = dict(norm1_w=g_n1w, norm2_w=g_n2w, final_norm_w=g_fnw, a_log=g_alog, dt_bias=g_dtb, gdn_norm_w=g_gnw)
    sd, smn, svn = unpack_small(sd), unpack_small(smn), unpack_small(svn)
    names = ["norm1_w", "w_in", "conv_qkv_w", "a_log", "dt_bias", "gdn_norm_w", "w_out", "norm2_w", "w_up",
             "ffn_conv_w", "w_down", "final_norm_w"]
    grads = [big[n][0] if n in big else sg[n] for n in names]
    deltas = [big[n][1] if n in big else sd[n] for n in names]
    new_m = [big[n][2] if n in big else smn[n] for n in names]
    new_v = [big[n][3] if n in big else svn[n] for n in names]
    return (loss, dx[None], *grads, *deltas, *new_m, *new_v)
```

```python
import functools
import math

import numpy as np
import jax
import jax.numpy as jnp
from jax import lax
from jax.experimental import pallas as pl
from jax.experimental.pallas import tpu as pltpu

F32 = jnp.float32
BF16 = jnp.bfloat16
_MXU = jnp.bfloat16
_HI = lax.Precision.HIGHEST
EPS = 1e-6
V7X_VMEM_LIMIT = 56 * 1024 * 1024
MESH = pl.DeviceIdType.MESH

D_MODEL = 1024
GDN_HEADS, GDN_DIM, GDN_CHUNK, GDN_CONV = 4, 128, 64, 4
GDN_WIDTH = GDN_HEADS * GDN_DIM
DIL_HEADS, DIL_DIM = 8, 64
DIL_WIDTH = DIL_HEADS * DIL_DIM
D_FF, FFN_CONV = 2816, 3
IN_COLS = 3592
P_COLS = 3840
P_Z, P_QKVB, P_BA = 1536, 2048, 3584
ATT_T = 1024
ADAM_LR, ADAM_B1, ADAM_B2, ADAM_EPS, ADAM_WD, ADAM_STEP = 0.001, 0.9, 0.999, 1e-08, 0.01, 10
N_CHIPS = 4


def _cparams(sem=None, vmem=None):
    kw = {}
    if sem is not None:
        kw["dimension_semantics"] = sem
    if vmem is not None:
        kw["vmem_limit_bytes"] = vmem
    return pltpu.CompilerParams(**kw)


def _silu(x):
    return x * jax.nn.sigmoid(x)


def _pick_tile(n, cap):
    best = None
    for t in range(128, min(n, cap) + 1, 128):
        if n % t == 0:
            best = t
    return best or n


def _mm(a, b, mode, *, out_dtype=F32, residual=None, name, b_blocks=False, place=None, into=None, tn=None):
    if mode == "nn":
        M, K = a.shape
        N = b.shape[0] * b.shape[2] if b_blocks else b.shape[1]
    elif mode == "nt":
        (M, K), (N, _) = a.shape, b.shape
    else:
        (K, M), (_, N) = a.shape, b.shape
    tm = _pick_tile(M, 1024)
    tn = b.shape[2] if b_blocks else (tn or _pick_tile(N, 1536))

    def vmem(tm, tn):
        return 2 * (tm * K * a.dtype.itemsize + tn * K * b.dtype.itemsize
                    + tm * tn * (jnp.dtype(out_dtype).itemsize + (4 if residual is not None else 0))) + 3 * tm * tn * 4

    fixed_tn = b_blocks or (place is not None and place[0] == "blocks")
    while vmem(tm, tn) > 40 * 1024 * 1024:
        if (tm >= tn or fixed_tn) and tm % 256 == 0:
            tm //= 2
        elif tn % 256 == 0 and not fixed_tn:
            tn //= 2
        else:
            tm //= 2
    a_spec = pl.BlockSpec((K, tm), lambda j, i: (0, i)) if mode == "tn" else pl.BlockSpec((tm, K), lambda j, i: (i, 0))
    if b_blocks:
        b_spec = pl.BlockSpec((None, K, tn), lambda j, i: (j, 0, 0))
    else:
        b_spec = pl.BlockSpec((tn, K), lambda j, i: (j, 0)) if mode == "nt" else pl.BlockSpec((K, tn), lambda j, i: (0, j))
    r_spec = pl.BlockSpec((tm, tn), lambda j, i: (i, j))
    if place is None:
        o_spec, o_shape = r_spec, (M, N)
    elif place[0] == "rows":
        off = place[2] // tm
        o_spec, o_shape = pl.BlockSpec((tm, tn), lambda j, i: (i + off, j)), (place[1], N)
    else:
        off = place[2]
        o_spec, o_shape = pl.BlockSpec((None, tm, tn), lambda j, i: (j + off, i, 0)), (place[1], M, tn)
    dims = {"nn": (((1,), (0,)), ((), ())), "nt": (((1,), (1,)), ((), ())), "tn": (((0,), (0,)), ((), ()))}[mode]

    def body(*refs):
        a_ref, b_ref = refs[0], refs[1]
        o_ref = refs[-1]
        acc = lax.dot_general(a_ref[...].astype(_MXU), b_ref[...].astype(_MXU), dims, preferred_element_type=F32)
        if residual is not None:
            acc = acc + refs[2][...]
        o_ref[...] = acc.astype(out_dtype)

    ins, specs, alias = [a, b], [a_spec, b_spec], {}
    if residual is not None:
        ins.append(residual)
        specs.append(r_spec)
    if into is not None:
        alias = {len(ins): 0}
        ins.append(into)
        specs.append(pl.BlockSpec(memory_space=pl.ANY))
    return pl.pallas_call(
        body, name=name, grid=(N // tn, M // tm), in_specs=specs, out_specs=o_spec,
        out_shape=jax.ShapeDtypeStruct(o_shape, out_dtype), input_output_aliases=alias,
        compiler_params=_cparams(("parallel", "parallel"), V7X_VMEM_LIMIT),
    )(*ins)


def _mm_nt_blocks(a_list, b4, name):
    M = a_list[0].shape[0]
    nb, N, Kb = b4.shape
    tm, tn = _pick_tile(M, 1024), _pick_tile(N, 512)

    def body(a0_ref, a1_ref, b_ref, o_ref):
        acc = None
        for blk in range(nb):
            a_ref = (a0_ref, a1_ref)[blk // 2]
            lo = (blk % 2) * Kb
            t = lax.dot_general(a_ref[:, lo:lo + Kb].astype(_MXU), b_ref[blk].astype(_MXU), (((1,), (1,)), ((), ())),
                                preferred_element_type=F32)
            acc = t if acc is None else acc + t
        o_ref[...] = acc

    a_spec = pl.BlockSpec((tm, 2 * Kb), lambda j, i: (i, 0))
    return pl.pallas_call(
        body, name=name, grid=(N // tn, M // tm),
        in_specs=[a_spec, a_spec, pl.BlockSpec((nb, tn, Kb), lambda j, i: (0, j, 0))],
        out_specs=pl.BlockSpec((tm, tn), lambda j, i: (i, j)), out_shape=jax.ShapeDtypeStruct((M, N), F32),
        compiler_params=_cparams(("parallel", "parallel"), V7X_VMEM_LIMIT),
    )(a_list[0], a_list[1], b4)


def _wp_assemble(g_in, after=()):
    nb, Dm, Wb = g_in.shape
    T = 256
    n_lo = P_QKVB - 2 * Wb

    def body(g_ref, *rest):
        g2 = g_ref[2]
        rest[-1][...] = jnp.concatenate(
            [g_ref[0], g_ref[1], g2[:, :n_lo], g2[:, n_lo + 8:], g_ref[3], g2[:, n_lo:n_lo + 8],
             jnp.zeros((T, P_COLS - P_BA - 8), g_in.dtype)], axis=1)

    return pl.pallas_call(
        body, name="wp_assemble", grid=(Dm // T,),
        in_specs=[pl.BlockSpec((nb, T, Wb), lambda i: (0, i, 0))] + [pl.BlockSpec(memory_space=pl.ANY)] * len(after),
        out_specs=pl.BlockSpec((T, P_COLS), lambda i: (i, 0)), out_shape=jax.ShapeDtypeStruct((Dm, P_COLS), g_in.dtype),
        compiler_params=_cparams(("parallel",)),
    )(g_in, *after)


def _win_split(d_wp):
    Dm = d_wp.shape[0]
    Wb = IN_COLS // N_CHIPS
    T = 256

    def body(x_ref, o_ref):
        xv = x_ref[...]
        o_ref[0] = xv[:, 0:Wb]
        o_ref[1] = xv[:, Wb:2 * Wb]
        o_ref[2] = jnp.concatenate([xv[:, 2 * Wb:P_QKVB], xv[:, P_BA:P_BA + 8], xv[:, P_QKVB:3 * Wb - 8]], axis=1)
        o_ref[3] = xv[:, 3 * Wb - 8:P_BA]

    return pl.pallas_call(
        body, name="win_split", grid=(Dm // T,), in_specs=[pl.BlockSpec((T, P_COLS), lambda i: (i, 0))],
        out_specs=pl.BlockSpec((N_CHIPS, T, Wb), lambda i: (0, i, 0)),
        out_shape=jax.ShapeDtypeStruct((N_CHIPS, Dm, Wb), F32), compiler_params=_cparams(("parallel",)),
    )(d_wp)


def _rmsnorm_fwd(x, w, name):
    S, D = x.shape
    T = _pick_tile(S, 512)

    def body(x_ref, w_ref, o_ref):
        xv = x_ref[...]
        rs = lax.rsqrt(jnp.mean(xv * xv, axis=-1, keepdims=True) + EPS)
        o_ref[...] = (xv * rs * w_ref[...]).astype(o_ref.dtype)

    return pl.pallas_call(
        body, name=name, grid=(S // T,),
        in_specs=[pl.BlockSpec((T, D), lambda i: (i, 0)), pl.BlockSpec((1, D), lambda i: (0, 0))],
        out_specs=pl.BlockSpec((T, D), lambda i: (i, 0)),
        out_shape=jax.ShapeDtypeStruct((S, D), _MXU),
        compiler_params=_cparams(("parallel",)),
    )(x, w)


def _rmsnorm_bwd(dh, x, w, dres, name):
    S, D = x.shape
    T = _pick_tile(S, 512)

    def body(dh_ref, x_ref, w_ref, dres_ref, dx_ref, dxn_ref, dw_ref):
        xv = x_ref[...]
        rs = lax.rsqrt(jnp.mean(xv * xv, axis=-1, keepdims=True) + EPS)
        xn = xv * rs
        dhv = dh_ref[...]
        dxn = dhv * w_ref[...]
        dxv = dres_ref[...] + rs * (dxn - xn * jnp.mean(dxn * xn, axis=-1, keepdims=True))
        dx_ref[...] = dxv
        dxn_ref[...] = dxv.astype(dxn_ref.dtype)

        @pl.when(pl.program_id(0) == 0)
        def _():
            dw_ref[...] = jnp.zeros_like(dw_ref)

        dw_ref[...] += jnp.sum(dhv * xn, axis=0, keepdims=True)

    row = pl.BlockSpec((T, D), lambda i: (i, 0))
    vec = pl.BlockSpec((1, D), lambda i: (0, 0))
    return pl.pallas_call(
        body, name=name, grid=(S // T,), in_specs=[row, row, vec, row], out_specs=(row, row, vec),
        out_shape=(jax.ShapeDtypeStruct((S, D), F32), jax.ShapeDtypeStruct((S, D), _MXU), jax.ShapeDtypeStruct((1, D), F32)),
        compiler_params=_cparams(("arbitrary",)),
    )(dh, x, w, dres)


def _loss_head(x3, w, tgt, name):
    S, D = x3.shape
    T = _pick_tile(S, 512)

    def body(x_ref, w_ref, t_ref, loss_ref, dx_ref, dxn_ref, dw_ref):
        xv = x_ref[...]
        rs = lax.rsqrt(jnp.mean(xv * xv, axis=-1, keepdims=True) + EPS)
        xn = xv * rs
        err = xn * w_ref[...] - t_ref[...]
        dy = err * (1.0 / D)
        dxn = dy * w_ref[...]
        dxv = rs * (dxn - xn * jnp.mean(dxn * xn, axis=-1, keepdims=True))
        dx_ref[...] = dxv
        dxn_ref[...] = dxv.astype(dxn_ref.dtype)

        @pl.when(pl.program_id(0) == 0)
        def _():
            dw_ref[...] = jnp.zeros_like(dw_ref)
            loss_ref[...] = jnp.zeros_like(loss_ref)

        dw_ref[...] += jnp.sum(dy * xn, axis=0, keepdims=True)
        part = jnp.sum(jnp.sum(err * err, axis=-1, keepdims=True), axis=0, keepdims=True) * (0.5 / D)
        loss_ref[...] += jnp.broadcast_to(part, loss_ref.shape)

    row = pl.BlockSpec((T, D), lambda i: (i, 0))
    vec = pl.BlockSpec((1, D), lambda i: (0, 0))
    return pl.pallas_call(
        body, name=name, grid=(S // T,), in_specs=[row, vec, row],
        out_specs=(pl.BlockSpec((8, 128), lambda i: (0, 0)), row, row, vec),
        out_shape=(jax.ShapeDtypeStruct((8, 128), F32), jax.ShapeDtypeStruct((S, D), F32), jax.ShapeDtypeStruct((S, D), _MXU),
                   jax.ShapeDtypeStruct((1, D), F32)),
        compiler_params=_cparams(("arbitrary",)),
    )(x3, w, tgt)


def _shifted(ext, back, lo, n):
    if back == 0:
        return ext[lo:lo + n, :]
    return pltpu.roll(ext, back % ext.shape[0], 0)[lo:lo + n, :]


def _conv_windows(ext, K, T):
    return [_shifted(ext, (K - 1) - i, 8, T) for i in range(K)]


def _conv_taps(ext, w, K, T):
    out = None
    for i, win in enumerate(_conv_windows(ext, K, T)):
        term = win * w[i:i + 1, :]
        out = term if out is None else out + term
    return out


def _conv_taps_t(ext, w, K, T):
    out = None
    for i in range(K):
        term = _shifted(ext, i - (K - 1), 0, T) * w[i:i + 1, :]
        out = term if out is None else out + term
    return out


def _tri_masks(C):
    r = lax.broadcasted_iota(jnp.int32, (C, C), 0)
    c = lax.broadcasted_iota(jnp.int32, (C, C), 1)
    return r == c, r >= c, r > c, r <= c


_NN, _NT, _TN = ((1,), (0,)), ((1,), (1,)), ((0,), (0,))
_GDN_PASSES = dict(qk=1, inv=1, sol=1, scan=1, bwd=1)


def _bdot_raw(a, b, kind, passes):
    dims = ({"NN": ((2,), (1,)), "NT": ((2,), (2,)), "TN": ((1,), (1,))}[kind], ((0,), (0,)))
    if passes == 0:
        return lax.dot_general(a, b, dims, precision=_HI, preferred_element_type=F32)
    ah, bh = a.astype(BF16), b.astype(BF16)
    out = lax.dot_general(ah, bh, dims, preferred_element_type=F32)
    if passes == 3:
        al, bl = (a - ah.astype(F32)).astype(BF16), (b - bh.astype(F32)).astype(BF16)
        out = out + lax.dot_general(ah, bl, dims, preferred_element_type=F32) + lax.dot_general(al, bh, dims, preferred_element_type=F32)
    return out


@functools.partial(jax.custom_vjp, nondiff_argnums=(2, 3))
def _bdot(a, b, kind, passes):
    return _bdot_raw(a, b, kind, passes)


def _bdot_fwd(a, b, kind, passes):
    return _bdot_raw(a, b, kind, passes), (a, b)


def _bdot_bwd(kind, passes, res, ct):
    a, b = res
    if kind == "NN":
        return _bdot_raw(ct, b, "NT", passes), _bdot_raw(a, ct, "TN", passes)
    if kind == "NT":
        return _bdot_raw(ct, b, "NN", passes), _bdot_raw(ct, a, "TN", passes)
    return _bdot_raw(b, ct, "NT", passes), _bdot_raw(a, ct, "NN", passes)


_bdot.defvjp(_bdot_fwd, _bdot_bwd)


def _softplus(x):
    return jnp.maximum(x, 0.0) + jnp.log(1.0 + jnp.exp(-jnp.abs(x)))


def _gdn_stage1(cq, ck, cv, b_col, a_col, alog, dtb, dot=_bdot_raw):
    C = cq.shape[1]
    eye, incl, strict, incl_t = _tri_masks(C)
    qn = cq * lax.rsqrt(jnp.sum(cq * cq, axis=-1, keepdims=True) + EPS) * (GDN_DIM ** -0.5)
    kn = ck * lax.rsqrt(jnp.sum(ck * ck, axis=-1, keepdims=True) + EPS)
    beta = jax.nn.sigmoid(b_col)
    g = -jnp.exp(alog) * _softplus(a_col + dtb)
    g_row = jnp.sum(jnp.where(eye, g, 0.0), axis=1, keepdims=True)
    beta_row = jnp.sum(jnp.where(eye, beta, 0.0), axis=1, keepdims=True)
    gc_col = jnp.sum(jnp.where(incl, g_row, 0.0), axis=2, keepdims=True)
    gc_row = jnp.sum(jnp.where(incl_t, g, 0.0), axis=1, keepdims=True)
    dec = jnp.where(incl, jnp.exp(jnp.where(incl, gc_col - gc_row, 0.0)), 0.0)
    kk = dot(kn, kn, "NT", _GDN_PASSES["qk"])
    qk = dot(qn, kn, "NT", _GDN_PASSES["qk"])
    lmat = jnp.where(strict, dec * kk * beta_row, 0.0)
    attn = dec * qk * beta_row
    gam = jnp.exp(gc_col)
    gc_last = gc_col[:, C - 1:C, :]
    k_end = kn * (jnp.exp(gc_last - gc_col) * beta)
    return lmat, cv, gam * kn, gam * qn, attn, k_end, jnp.exp(gc_last)


def _tri_inv(lmat):
    C = lmat.shape[1]
    eye = _tri_masks(C)[0]
    ps = _GDN_PASSES["inv"]
    p = jnp.where(eye, 1.0, 0.0) - lmat
    lp = _bdot_raw(lmat, lmat, "NN", ps)
    n = int(math.log2(C))
    for s in range(1, n):
        p = p + _bdot_raw(p, lp, "NN", ps)
        if s < n - 1:
            lp = _bdot_raw(lp, lp, "NN", ps)
    return p


def _gated_norm(o, z, gnw):
    on = o * lax.rsqrt(jnp.mean(o * o, axis=-1, keepdims=True) + EPS) * gnw
    return on * _silu(z)


GDN_PG = 2
GDN_SG = 4


def _gdn_pairs(c, ba, gp, G):
    C, W, H = GDN_CHUNK, GDN_WIDTH, GDN_HEADS
    pairs = [(j, h) for j in range(G) for h in range(H)]
    cq, ck, cv = (jnp.stack([c[C * j:C * (j + 1), o + GDN_DIM * h:o + GDN_DIM * (h + 1)] for j, h in pairs]) for o in (0, W, 2 * W))
    b_col = jnp.stack([ba[C * j:C * (j + 1), h:h + 1] for j, h in pairs])
    a_col = jnp.stack([ba[C * j:C * (j + 1), H + h:H + h + 1] for j, h in pairs])
    alog = jnp.stack([gp[0:1, h:h + 1] for j, h in pairs])
    dtb = jnp.stack([gp[0:1, H + h:H + h + 1] for j, h in pairs])
    return pairs, (cq, ck, cv, b_col, a_col, alog, dtb)


def _gdn_pre_specs(S, G):
    C = GDN_CHUNK
    T = C * G
    return dict(
        cur=pl.BlockSpec((T, 3 * GDN_WIDTH), lambda i: (i, 0)),
        prev=pl.BlockSpec((8, 3 * GDN_WIDTH), lambda i: (jnp.maximum(i * (T // 8) - 1, 0), 0)),
        ba=pl.BlockSpec((T, 128), lambda i: (i, P_BA // 128)),
        cw=pl.BlockSpec((GDN_CONV, 3 * GDN_WIDTH), lambda i: (0, 0)),
        vec=pl.BlockSpec((1, 128), lambda i: (0, 0)),
        hd=pl.BlockSpec((GDN_HEADS, T, GDN_DIM), lambda i: (0, i, 0)),
        hc=pl.BlockSpec((GDN_HEADS, T, C), lambda i: (0, i, 0)),
        ge=pl.BlockSpec((G, GDN_HEADS, 8, 128), lambda i: (i, 0, 0, 0)),
    )


def _hd_shape(S, last=GDN_DIM):
    return jax.ShapeDtypeStruct((GDN_HEADS, S, last), F32)


def _gdn_pre(proj, conv_w, gp):
    S = proj.shape[0]
    C, G = GDN_CHUNK, GDN_PG
    nc = S // C
    sp = _gdn_pre_specs(S, G)

    def body(cur_ref, prev_ref, ba_ref, cw_ref, gp_ref, uv_ref, wk_ref, qd_ref, ke_ref, at_ref, ti_ref, ge_ref):
        prev = prev_ref[...] * jnp.where(pl.program_id(0) == 0, 0.0, 1.0)
        c = _silu(_conv_taps(jnp.concatenate([prev, cur_ref[...]], axis=0), cw_ref[...], GDN_CONV, C * G))
        pairs, args = _gdn_pairs(c, ba_ref[...], gp_ref[...], G)
        lmat, v, rk, q_dec, attn, k_end, g_end = _gdn_stage1(*args)
        t = _tri_inv(lmat)
        u_v = _bdot_raw(t, v, "NN", _GDN_PASSES["sol"])
        w_k = _bdot_raw(t, rk, "NN", _GDN_PASSES["sol"])
        for b, (j, h) in enumerate(pairs):
            rows = slice(C * j, C * (j + 1))
            uv_ref[h, rows, :] = u_v[b]
            wk_ref[h, rows, :] = w_k[b]
            qd_ref[h, rows, :] = q_dec[b]
            ke_ref[h, rows, :] = k_end[b]
            at_ref[h, rows, :] = attn[b]
            ti_ref[h, rows, :] = t[b]
            ge_ref[j, h] = jnp.broadcast_to(g_end[b], (8, 128))

    return pl.pallas_call(
        body, name="gdn_pre", grid=(nc // G,),
        in_specs=[sp["cur"], sp["prev"], sp["ba"], sp["cw"], sp["vec"]],
        out_specs=(sp["hd"], sp["hd"], sp["hd"], sp["hd"], sp["hc"], sp["hc"], sp["ge"]),
        out_shape=(_hd_shape(S), _hd_shape(S), _hd_shape(S), _hd_shape(S), _hd_shape(S, C), _hd_shape(S, C),
                   jax.ShapeDtypeStruct((nc, GDN_HEADS, 8, 128), F32)),
        compiler_params=_cparams(("parallel",)),
    )(proj, proj, proj, conv_w, gp)


def _gdn_scan_specs(S, G, rev):
    C = GDN_CHUNK
    T = C * G
    n = S // T
    ci = (lambda i: n - 1 - i) if rev else (lambda i: i)
    return dict(
        hd=pl.BlockSpec((GDN_HEADS, T, GDN_DIM), lambda i: (0, ci(i), 0)),
        hc=pl.BlockSpec((GDN_HEADS, T, C), lambda i: (0, ci(i), 0)),
        ge=pl.BlockSpec((G, GDN_HEADS, 8, 128), lambda i: (ci(i), 0, 0, 0)),
        z=pl.BlockSpec((T, GDN_WIDTH), lambda i: (ci(i), P_Z // GDN_WIDTH)),
        oa=pl.BlockSpec((T, GDN_WIDTH), lambda i: (ci(i), 0)),
        vec=pl.BlockSpec((1, 128), lambda i: (0, 0)),
        st=pl.BlockSpec((G, GDN_HEADS, GDN_DIM, GDN_DIM), lambda i: (ci(i), 0, 0, 0)),
    )


def _gdn_scan(u_v, w_k, q_dec, k_end, attn, g_end, proj, gnw):
    S = proj.shape[0]
    C, G = GDN_CHUNK, GDN_SG
    nc = S // C
    sp = _gdn_scan_specs(S, G, False)
    ps = _GDN_PASSES["scan"]

    def body(uv_ref, wk_ref, qd_ref, ke_ref, at_ref, ge_ref, z_ref, gnw_ref, oa_ref, st_ref, s_scr):
        @pl.when(pl.program_id(0) == 0)
        def _():
            s_scr[...] = jnp.zeros_like(s_scr)

        for j in range(G):
            rows = slice(C * j, C * (j + 1))
            st = s_scr[...]
            st_ref[j] = st
            u = uv_ref[:, rows, :] - _bdot_raw(wk_ref[:, rows, :], st, "NN", ps)
            o = _bdot_raw(qd_ref[:, rows, :], st, "NN", ps) + _bdot_raw(at_ref[:, rows, :], u, "NN", ps)
            s_scr[...] = ge_ref[j][:, 0:1, 0:1] * st + _bdot_raw(ke_ref[:, rows, :], u, "TN", ps)
            for h in range(GDN_HEADS):
                cols = slice(GDN_DIM * h, GDN_DIM * (h + 1))
                oa_ref[rows, cols] = _gated_norm(o[h], z_ref[rows, cols], gnw_ref[...])

    return pl.pallas_call(
        body, name="gdn_scan", grid=(nc // G,),
        in_specs=[sp["hd"], sp["hd"], sp["hd"], sp["hd"], sp["hc"], sp["ge"], sp["z"], sp["vec"]],
        out_specs=(sp["oa"], sp["st"]),
        out_shape=(jax.ShapeDtypeStruct((S, GDN_WIDTH + DIL_WIDTH), F32),
                   jax.ShapeDtypeStruct((nc, GDN_HEADS, GDN_DIM, GDN_DIM), F32)),
        scratch_shapes=[pltpu.VMEM((GDN_HEADS, GDN_DIM, GDN_DIM), F32)],
        compiler_params=_cparams(("arbitrary",)),
    )(u_v, w_k, q_dec, k_end, attn, g_end, proj, gnw)


def _gdn_scan_bwd(u_v, w_k, q_dec, k_end, attn, g_end, proj, gnw, states, d_oa):
    S = proj.shape[0]
    C, G = GDN_CHUNK, GDN_SG
    nc = S // C
    sp = _gdn_scan_specs(S, G, True)
    ps, pb = _GDN_PASSES["scan"], _GDN_PASSES["bwd"]

    def body(uv_ref, wk_ref, qd_ref, ke_ref, at_ref, ge_ref, z_ref, gnw_ref, st_ref, doa_ref,
             duv_ref, dwk_ref, dqd_ref, dke_ref, dat_ref, dge_ref, dz_ref, dgnw_ref, ds_scr):
        @pl.when(pl.program_id(0) == 0)
        def _():
            ds_scr[...] = jnp.zeros_like(ds_scr)
            dgnw_ref[...] = jnp.zeros_like(dgnw_ref)

        dgnw = jnp.zeros((1, 128), F32)
        for j in reversed(range(G)):
            rows = slice(C * j, C * (j + 1))
            st = st_ref[j]
            wk, qd, ke, at = wk_ref[:, rows, :], qd_ref[:, rows, :], ke_ref[:, rows, :], at_ref[:, rows, :]
            u = uv_ref[:, rows, :] - _bdot_raw(wk, st, "NN", ps)
            o = _bdot_raw(qd, st, "NN", ps) + _bdot_raw(at, u, "NN", ps)
            dos = []
            for h in range(GDN_HEADS):
                cols = slice(GDN_DIM * h, GDN_DIM * (h + 1))
                _, vjp2 = jax.vjp(_gated_norm, o[h], z_ref[rows, cols], gnw_ref[...])
                do_h, dz_h, dgn = vjp2(doa_ref[rows, cols])
                dz_ref[rows, cols] = dz_h
                dgnw = dgnw + dgn
                dos.append(do_h)
            do = jnp.stack(dos)
            ds_new = ds_scr[...]
            du = _bdot_raw(at, do, "TN", pb) + _bdot_raw(ke, ds_new, "NN", pb)
            duv_ref[:, rows, :] = du
            dat_ref[:, rows, :] = _bdot_raw(do, u, "NT", pb)
            dqd_ref[:, rows, :] = _bdot_raw(do, st, "NT", pb)
            dke_ref[:, rows, :] = _bdot_raw(u, ds_new, "NT", pb)
            dwk_ref[:, rows, :] = -_bdot_raw(du, st, "NT", pb)
            d_ge = jnp.sum(jnp.sum(st * ds_new, axis=2, keepdims=True), axis=1, keepdims=True)
            dge_ref[j] = jnp.broadcast_to(d_ge, (GDN_HEADS, 8, 128))
            ds_scr[...] = ge_ref[j][:, 0:1, 0:1] * ds_new + _bdot_raw(qd, do, "TN", pb) - _bdot_raw(wk, du, "TN", pb)
        dgnw_ref[...] += dgnw

    return pl.pallas_call(
        body, name="gdn_scan_bwd", grid=(nc // G,),
        in_specs=[sp["hd"], sp["hd"], sp["hd"], sp["hd"], sp["hc"], sp["ge"], sp["z"], sp["vec"], sp["st"], sp["oa"]],
        out_specs=(sp["hd"], sp["hd"], sp["hd"], sp["hd"], sp["hc"], sp["ge"], sp["oa"], sp["vec"]),
        out_shape=(_hd_shape(S), _hd_shape(S), _hd_shape(S), _hd_shape(S), _hd_shape(S, C),
                   jax.ShapeDtypeStruct((nc, GDN_HEADS, 8, 128), F32), jax.ShapeDtypeStruct((S, GDN_WIDTH), F32),
                   jax.ShapeDtypeStruct((1, 128), F32)),
        scratch_shapes=[pltpu.VMEM((GDN_HEADS, GDN_DIM, GDN_DIM), F32)],
        compiler_params=_cparams(("arbitrary",)),
    )(u_v, w_k, q_dec, k_end, attn, g_end, proj, gnw, states, d_oa)


def _gdn_post(proj, conv_w, gp, tinv, u_v, w_k, d_uv, d_wk, d_qd, d_ke, d_at, d_ge):
    S = proj.shape[0]
    C, G = GDN_CHUNK, GDN_PG
    nc = S // C
    sp = _gdn_pre_specs(S, G)
    pb = _GDN_PASSES["bwd"]

    def body(cur_ref, prev_ref, ba_ref, cw_ref, gp_ref, ti_ref, uv_ref, wk_ref, duv_ref, dwk_ref, dqd_ref, dke_ref,
             dat_ref, dge_ref, dpre_ref, dba_ref, dgp_ref):
        i = pl.program_id(0)

        @pl.when(i == 0)
        def _():
            dgp_ref[...] = jnp.zeros_like(dgp_ref)

        prev = prev_ref[...] * jnp.where(i == 0, 0.0, 1.0)
        pre = _conv_taps(jnp.concatenate([prev, cur_ref[...]], axis=0), cw_ref[...], GDN_CONV, C * G)
        sg = jax.nn.sigmoid(pre)
        dsilu = sg * (1.0 + pre * (1.0 - sg))
        pairs, args = _gdn_pairs(pre * sg, ba_ref[...], gp_ref[...], G)
        _, vjp1 = jax.vjp(functools.partial(_gdn_stage1, dot=_bdot), *args)

        def take(ref):
            return jnp.stack([ref[h, C * j:C * (j + 1), :] for j, h in pairs])

        t, u_v, w_k = take(ti_ref), take(uv_ref), take(wk_ref)
        d_v = _bdot_raw(t, take(duv_ref), "TN", pb)
        d_rk = _bdot_raw(t, take(dwk_ref), "TN", pb)
        d_l = -(_bdot_raw(d_v, u_v, "NT", pb) + _bdot_raw(d_rk, w_k, "NT", pb))
        d_ge = jnp.stack([dge_ref[j, h][0:1, 0:1] for j, h in pairs])
        dcq, dck, dcv, db, da, dalog, ddtb = vjp1((d_l, d_v, d_rk, take(dqd_ref), take(dat_ref), take(dke_ref), d_ge))
        lane = lax.broadcasted_iota(jnp.int32, (C, 128), 1)
        lane1 = lax.broadcasted_iota(jnp.int32, (1, 128), 1)
        dgp = jnp.zeros((1, 128), F32)
        for j in range(G):
            rows = slice(C * j, C * (j + 1))
            dba = jnp.zeros((C, 128), F32)
            for h in range(GDN_HEADS):
                b = GDN_HEADS * j + h
                for o_, dcx in ((0, dcq), (GDN_WIDTH, dck), (2 * GDN_WIDTH, dcv)):
                    cols = slice(o_ + GDN_DIM * h, o_ + GDN_DIM * (h + 1))
                    dpre_ref[rows, cols] = dcx[b] * dsilu[rows, cols]
                dba = dba + jnp.where(lane == h, db[b], 0.0) + jnp.where(lane == GDN_HEADS + h, da[b], 0.0)
                dgp = dgp + jnp.where(lane1 == h, dalog[b], 0.0) + jnp.where(lane1 == GDN_HEADS + h, ddtb[b], 0.0)
            dba_ref[rows, :] = dba
        dgp_ref[0:1, :] += dgp

    T = C * G
    return pl.pallas_call(
        body, name="gdn_post", grid=(nc // G,),
        in_specs=[sp["cur"], sp["prev"], sp["ba"], sp["cw"], sp["vec"], sp["hc"], sp["hd"], sp["hd"], sp["hd"], sp["hd"],
                  sp["hd"], sp["hd"], sp["hc"], sp["ge"]],
        out_specs=(sp["cur"], pl.BlockSpec((T, 128), lambda i: (i, 0)), pl.BlockSpec((8, 128), lambda i: (0, 0))),
        out_shape=(jax.ShapeDtypeStruct((S, 3 * GDN_WIDTH), F32), jax.ShapeDtypeStruct((S, 128), F32),
                   jax.ShapeDtypeStruct((8, 128), F32)),
        compiler_params=_cparams(("arbitrary",)),
    )(proj, proj, proj, conv_w, gp, tinv, u_v, w_k, d_uv, d_wk, d_qd, d_ke, d_at, d_ge)


def _conv_bwd(dpre, x, xcol0, w, K, name, tc):
    S, Cc = dpre.shape
    T = _pick_tile(S, 256)
    nt, ncol = S // T, Cc // tc
    xo = xcol0 // tc

    def body(d_ref, dn_ref, x_ref, xp_ref, w_ref, dx_ref, dw_ref):
        i = pl.program_id(1)
        dn = dn_ref[...] * jnp.where(i == nt - 1, 0.0, 1.0)
        dv = d_ref[...]
        ext_d = jnp.concatenate([dv, dn], axis=0)
        dx_ref[...] = _conv_taps_t(ext_d, w_ref[...], K, T).astype(dx_ref.dtype)
        xp = xp_ref[...] * jnp.where(i == 0, 0.0, 1.0)
        ext_x = jnp.concatenate([xp, x_ref[...]], axis=0)

        @pl.when(i == 0)
        def _():
            dw_ref[...] = jnp.zeros_like(dw_ref)

        for k in range(K):
            dw_ref[k:k + 1, :] += jnp.sum(dv * _shifted(ext_x, (K - 1) - k, 8, T), axis=0, keepdims=True)

    r8 = T // 8
    return pl.pallas_call(
        body, name=name, grid=(ncol, nt),
        in_specs=[pl.BlockSpec((T, tc), lambda j, i: (i, j)),
                  pl.BlockSpec((8, tc), lambda j, i: (jnp.minimum((i + 1) * r8, S // 8 - 1), j)),
                  pl.BlockSpec((T, tc), lambda j, i: (i, j + xo)),
                  pl.BlockSpec((8, tc), lambda j, i: (jnp.maximum(i * r8 - 1, 0), j + xo)),
                  pl.BlockSpec((K, tc), lambda j, i: (0, j))],
        out_specs=(pl.BlockSpec((T, tc), lambda j, i: (i, j)), pl.BlockSpec((K, tc), lambda j, i: (0, j))),
        out_shape=(jax.ShapeDtypeStruct((S, Cc), _MXU), jax.ShapeDtypeStruct((K, Cc), F32)),
        compiler_params=_cparams(("parallel", "arbitrary")),
    )(dpre, dpre, x, x, w)


def _dil_bias(nt, T):
    d = (np.arange(nt)[:, None, None] * T + np.arange(T)[None, None, :] - np.arange(T)[None, :, None])
    cnt = ((d >= 0) & (d <= 128)).astype(np.float64) + ((d >= 0) & (d % 4 == 0) & (d <= 512)) + ((d >= 0) & (d % 16 == 0))
    return jnp.asarray(np.where(cnt > 0, np.log(np.maximum(cnt, 1.0)), -1e30), dtype=F32)


def _attn_fwd(proj, mix):
    S = proj.shape[0]
    T = min(ATT_T, S)
    nt = S // T
    bias = _dil_bias(nt, T)
    scale = DIL_DIM ** -0.5
    npair = DIL_WIDTH // 128
    qb0, kb0, vb0 = P_QKVB // 128, (P_QKVB + DIL_WIDTH) // 128, (P_QKVB + 2 * DIL_WIDTH) // 128

    def body(q_ref, k_ref, v_ref, b_ref, mix_ref, o_ref, lse_ref):
        i = pl.program_id(1)
        qs = (q_ref[...] * scale).astype(_MXU)

        def step(j, carry):
            kt = k_ref[pl.ds(pl.multiple_of(j * T, T), T), :].astype(_MXU)
            vt = v_ref[pl.ds(pl.multiple_of(j * T, T), T), :].astype(_MXU)
            bt = b_ref[i - j]
            out = []
            for hh in range(2):
                m, l, acc = carry[hh]
                sl = slice(hh * DIL_DIM, (hh + 1) * DIL_DIM)
                s = lax.dot_general(kt[:, sl], qs[:, sl], (_NT, ((), ())), preferred_element_type=F32) + bt
                m_new = jnp.maximum(m, jnp.max(s, axis=0, keepdims=True))
                p = jnp.exp(s - m_new)
                a = jnp.exp(m - m_new)
                l = a * l + jnp.sum(p, axis=0, keepdims=True)
                acc = a * acc + lax.dot_general(vt[:, sl], p.astype(_MXU), (_TN, ((), ())), preferred_element_type=F32)
                out.append((m_new, l, acc))
            return tuple(out)

        init = tuple((jnp.full((1, T), -1e30, F32), jnp.zeros((1, T), F32), jnp.zeros((DIL_DIM, T), F32)) for _ in range(2))
        res = lax.fori_loop(0, i + 1, step, init)
        lse_ref[...] = jnp.zeros_like(lse_ref)
        for hh in range(2):
            m, l, acc = res[hh]
            o_ref[:, hh * DIL_DIM:(hh + 1) * DIL_DIM] = (acc / l).T
            lse_ref[hh:hh + 1, :] = m + jnp.log(l)

    return pl.pallas_call(
        body, name="attn_fwd", grid=(npair, nt),
        in_specs=[pl.BlockSpec((T, 128), lambda p, i: (i, qb0 + p)),
                  pl.BlockSpec((S, 128), lambda p, i: (0, kb0 + p)),
                  pl.BlockSpec((S, 128), lambda p, i: (0, vb0 + p)),
                  pl.BlockSpec((nt, T, T), lambda p, i: (0, 0, 0)), pl.BlockSpec(memory_space=pl.ANY)],
        out_specs=(pl.BlockSpec((T, 128), lambda p, i: (i, GDN_WIDTH // 128 + p)),
                   pl.BlockSpec((None, None, 8, T), lambda p, i: (p, i, 0, 0))),
        out_shape=(jax.ShapeDtypeStruct(mix.shape, F32), jax.ShapeDtypeStruct((npair, nt, 8, T), F32)),
        input_output_aliases={4: 0},
        compiler_params=_cparams(("parallel", "parallel")),
    )(proj, proj, proj, bias, mix)


def _attn_bwd(proj, mix, lse, d_mix):
    S = proj.shape[0]
    T = min(ATT_T, S)
    nt = S // T
    bias = _dil_bias(nt, T)
    scale = DIL_DIM ** -0.5
    npair = DIL_WIDTH // 128
    qb0, kb0, vb0 = P_QKVB // 128, (P_QKVB + DIL_WIDTH) // 128, (P_QKVB + 2 * DIL_WIDTH) // 128

    def body(q_ref, k_ref, v_ref, o_ref, lse_ref, do_ref, b_ref, dq_ref, dk_ref, dv_ref, dq_scr):
        j = pl.program_id(1)

        @pl.when(j == 0)
        def _():
            dq_scr[...] = jnp.zeros_like(dq_scr)

        kt = k_ref[...].astype(_MXU)
        vt = v_ref[...].astype(_MXU)
        ones = jnp.ones((8, DIL_DIM), F32)

        def step(i, carry):
            rows = pl.ds(pl.multiple_of(i * T, T), T)
            qs = (q_ref[rows, :] * scale).astype(_MXU)
            dov = do_ref[rows, :]
            prod = dov * o_ref[rows, :]
            lsev = lse_ref[i]
            dob = dov.astype(_MXU)
            bt = b_ref[i - j]
            out = []
            dqs = []
            for hh in range(2):
                dk, dv = carry[hh]
                sl = slice(hh * DIL_DIM, (hh + 1) * DIL_DIM)
                s = lax.dot_general(kt[:, sl], qs[:, sl], (_NT, ((), ())), preferred_element_type=F32) + bt
                p = jnp.exp(s - lsev[hh:hh + 1, :])
                delta = lax.dot_general(ones, prod[:, sl], (_NT, ((), ())), precision=_HI, preferred_element_type=F32)[0:1, :]
                dp = lax.dot_general(vt[:, sl], dob[:, sl], (_NT, ((), ())), preferred_element_type=F32)
                ds = (p * (dp - delta)).astype(_MXU)
                dv = dv + lax.dot_general(p.astype(_MXU), dob[:, sl], (_NN, ((), ())), preferred_element_type=F32)
                dk = dk + lax.dot_general(ds, qs[:, sl], (_NN, ((), ())), preferred_element_type=F32)
                dqs.append(lax.dot_general(ds, kt[:, sl], (_TN, ((), ())), preferred_element_type=F32) * scale)
                out.append((dk, dv))
            dq_scr[rows, :] += jnp.concatenate(dqs, axis=1)
            return tuple(out)

        init = tuple((jnp.zeros((T, DIL_DIM), F32), jnp.zeros((T, DIL_DIM), F32)) for _ in range(2))
        res = lax.fori_loop(j, nt, step, init)
        dk_ref[...] = jnp.concatenate([res[0][0], res[1][0]], axis=1).astype(dk_ref.dtype)
        dv_ref[...] = jnp.concatenate([res[0][1], res[1][1]], axis=1).astype(dv_ref.dtype)

        @pl.when(j == nt - 1)
        def _():
            dq_ref[...] = dq_scr[...].astype(dq_ref.dtype)

    full = lambda c0: pl.BlockSpec((S, 128), lambda p, j: (0, c0 + p))
    tile = lambda c0: pl.BlockSpec((T, 128), lambda p, j: (j, c0 + p))
    out3 = jax.ShapeDtypeStruct((S, DIL_WIDTH), _MXU)
    return pl.pallas_call(
        body, name="attn_bwd", grid=(npair, nt),
        in_specs=[full(qb0), tile(kb0), tile(vb0), full(GDN_WIDTH // 128),
                  pl.BlockSpec((None, nt, 8, T), lambda p, j: (p, 0, 0, 0)), full(GDN_WIDTH // 128),
                  pl.BlockSpec((nt, T, T), lambda p, j: (0, 0, 0))],
        out_specs=(full(0), tile(0), tile(0)),
        out_shape=(out3, out3, out3),
        scratch_shapes=[pltpu.VMEM((S, 128), F32)],
        compiler_params=_cparams(("parallel", "arbitrary")),
    )(proj, proj, proj, mix, lse, d_mix, bias)


def _ffn_act(up, cw):
    S, Cc = up.shape[0], up.shape[1] // 2
    T, tc = _pick_tile(S, 256), _pick_tile(Cc, 1536)
    r16 = T // 16
    nct = Cc // tc

    def body(g_ref, gp_ref, u_ref, up_ref, wg_ref, wu_ref, o_ref):
        keep = jnp.where(pl.program_id(1) == 0, 0.0, 1.0)
        cg = _conv_taps(jnp.concatenate([gp_ref[8:16, :].astype(F32) * keep, g_ref[...].astype(F32)], axis=0),
                        wg_ref[...], FFN_CONV, T)
        cu = _conv_taps(jnp.concatenate([up_ref[8:16, :].astype(F32) * keep, u_ref[...].astype(F32)], axis=0),
                        wu_ref[...], FFN_CONV, T)
        o_ref[...] = (_silu(cg) * cu).astype(o_ref.dtype)

    cur = lambda o: pl.BlockSpec((T, tc), lambda j, i: (i, j + o))
    prev = lambda o: pl.BlockSpec((16, tc), lambda j, i: (jnp.maximum(i * r16 - 1, 0), j + o))
    wsp = lambda o: pl.BlockSpec((FFN_CONV, tc), lambda j, i: (0, j + o))
    return pl.pallas_call(
        body, name="ffn_act", grid=(nct, S // T),
        in_specs=[cur(0), prev(0), cur(nct), prev(nct), wsp(0), wsp(nct)], out_specs=cur(0),
        out_shape=jax.ShapeDtypeStruct((S, Cc), _MXU),
        compiler_params=_cparams(("parallel", "parallel")),
    )(up, up, up, up, cw, cw)


def _ffn_act_bwd(d_act, up, cw):
    S, Cc = up.shape[0], up.shape[1] // 2
    T, tc = _pick_tile(S, 256), _pick_tile(Cc, 1536)
    r8, r16 = T // 8, T // 16
    nt = S // T
    nct = Cc // tc
    K = FFN_CONV

    def body(da_ref, dan_ref, g_ref, gp_ref, gn_ref, u_ref, up_ref, un_ref, wg_ref, wu_ref,
             dg_ref, du_ref, dwg_ref, dwu_ref):
        i = pl.program_id(1)
        keep_p = jnp.where(i == 0, 0.0, 1.0)
        keep_n = jnp.where(i == nt - 1, 0.0, 1.0)
        wg, wu = wg_ref[...], wu_ref[...]
        xg = jnp.concatenate([gp_ref[8:16, :].astype(F32) * keep_p, g_ref[...].astype(F32),
                              gn_ref[0:8, :].astype(F32) * keep_n], axis=0)
        xu = jnp.concatenate([up_ref[8:16, :].astype(F32) * keep_p, u_ref[...].astype(F32),
                              un_ref[0:8, :].astype(F32) * keep_n], axis=0)
        cg = _conv_taps(xg, wg, K, T + 8)
        cu = _conv_taps(xu, wu, K, T + 8)
        da = jnp.concatenate([da_ref[...], dan_ref[...] * keep_n], axis=0)
        sg = jax.nn.sigmoid(cg)
        d_cg = da * cu * (sg * (1.0 + cg * (1.0 - sg)))
        d_cu = da * (cg * sg)
        dg_ref[...] = _conv_taps_t(d_cg, wg, K, T).astype(dg_ref.dtype)
        du_ref[...] = _conv_taps_t(d_cu, wu, K, T).astype(du_ref.dtype)

        @pl.when(i == 0)
        def _():
            dwg_ref[...] = jnp.zeros_like(dwg_ref)
            dwu_ref[...] = jnp.zeros_like(dwu_ref)

        for k in range(K):
            dwg_ref[k:k + 1, :] += jnp.sum(d_cg[0:T, :] * _shifted(xg, (K - 1) - k, 8, T), axis=0, keepdims=True)
            dwu_ref[k:k + 1, :] += jnp.sum(d_cu[0:T, :] * _shifted(xu, (K - 1) - k, 8, T), axis=0, keepdims=True)

    cur = lambda o: pl.BlockSpec((T, tc), lambda j, i: (i, j + o))
    prev = lambda o: pl.BlockSpec((16, tc), lambda j, i: (jnp.maximum(i * r16 - 1, 0), j + o))
    nxt = lambda o: pl.BlockSpec((16, tc), lambda j, i: (jnp.minimum((i + 1) * r16, S // 16 - 1), j + o))
    nxt8 = pl.BlockSpec((8, tc), lambda j, i: (jnp.minimum((i + 1) * r8, S // 8 - 1), j))
    wsp = lambda o: pl.BlockSpec((K, tc), lambda j, i: (0, j + o))
    return pl.pallas_call(
        body, name="ffn_act_bwd", grid=(nct, nt),
        in_specs=[cur(0), nxt8, cur(0), prev(0), nxt(0), cur(nct), prev(nct), nxt(nct), wsp(0), wsp(nct)],
        out_specs=(cur(0), cur(0), wsp(0), wsp(0)),
        out_shape=(jax.ShapeDtypeStruct((S, Cc), _MXU), jax.ShapeDtypeStruct((S, Cc), _MXU),
                   jax.ShapeDtypeStruct((K, Cc), F32), jax.ShapeDtypeStruct((K, Cc), F32)),
        compiler_params=_cparams(("parallel", "arbitrary")),
    )(d_act, d_act, up, up, up, up, up, up, cw, cw)


def _local_step(x, tgt, h1, n1w, n2w, fnw, gp, gnw, wp, conv_w, fcw, rest_weights, early_grads):
    proj = _mm(h1, wp, "nn", name="proj")
    u_v, w_k, q_dec, k_end, attn, tinv, g_end = _gdn_pre(proj, conv_w, gp)
    mix, states = _gdn_scan(u_v, w_k, q_dec, k_end, attn, g_end, proj, gnw)
    mix, lse = _attn_fwd(proj, mix)
    w_out, w_up4, w_down = rest_weights([mix])
    x2 = _mm(mix, w_out, "nn", residual=x, name="outproj")
    h2 = _rmsnorm_fwd(x2, n2w, "norm2")
    up = _mm(h2, w_up4, "nn", b_blocks=True, out_dtype=_MXU, name="up")
    act = _ffn_act(up, fcw)
    x3 = _mm(act, w_down, "nn", residual=x2, name="down")
    loss, dx3, dx3n, d_fnw = _loss_head(x3, fnw, tgt, "loss_head")
    d_act = _mm(dx3n, w_down, "nt", name="d_act")
    d_wdown = _mm(act, dx3n, "tn", name="d_wdown")
    d_upg, d_upu, d_fcwg, d_fcwu = _ffn_act_bwd(d_act, up, fcw)
    d_wup = _mm(h2, d_upg, "tn", place=("blocks", N_CHIPS, 0), tn=w_up4.shape[2], name="d_wgate")
    d_wup = _mm(h2, d_upu, "tn", place=("blocks", N_CHIPS, N_CHIPS // 2), tn=w_up4.shape[2], into=d_wup, name="d_wup")
    token = early_grads[0](d_wup, d_wdown)
    d_h2 = _mm_nt_blocks([d_upg, d_upu], w_up4, "d_h2")
    dx2, dx2n, d_n2w = _rmsnorm_bwd(d_h2, x2, n2w + token[0:1, 0:1], dx3, "norm2_bwd")
    token = early_grads[1](dx2)
    d_mix = _mm(dx2n, w_out, "nt", name="d_mix")
    d_wout = _mm(mix, dx2n, "tn", name="d_wout")
    dq_b, dk_b, dv_b = _attn_bwd(proj, mix, lse, d_mix)
    d_uv, d_wk, d_qd, d_ke, d_at, d_ge, d_z, d_gnw = _gdn_scan_bwd(u_v, w_k, q_dec, k_end, attn, g_end, proj,
                                                                   gnw + token[0:1, 0:1], states, d_mix)
    d_pre, d_ba, d_gp = _gdn_post(proj, conv_w, gp, tinv, u_v, w_k, d_uv, d_wk, d_qd, d_ke, d_at, d_ge)
    d_qkva, d_convw = _conv_bwd(d_pre, proj, 0, conv_w, GDN_CONV, "gdn_conv_bwd", 512)
    d_proj = jnp.concatenate([d_qkva, d_z.astype(_MXU), dq_b, dk_b, dv_b, d_ba.astype(_MXU),
                              jnp.zeros((x.shape[0], P_COLS - P_BA - 128), _MXU)], axis=1)
    d_wp = _mm(h1, d_proj, "tn", name="d_wp")
    token = early_grads[2](d_wp, d_wout)
    d_h1 = _mm(d_proj, wp, "nt", name="d_h1")
    dx, _, d_n1w = _rmsnorm_bwd(d_h1, x, n1w + token[0:1, 0:1], dx2, "norm1_bwd")
    grads = dict(wp=d_wp, conv_w=d_convw, w_out=d_wout, w_up=d_wup, fcw_g=d_fcwg, fcw_u=d_fcwu, w_down=d_wdown,
                 n1w=d_n1w, n2w=d_n2w, fnw=d_fnw, gp=d_gp, gnw=d_gnw)
    return loss, dx, grads


_HBM = pl.BlockSpec(memory_space=pltpu.HBM)


def _pos():
    return lax.axis_index("x"), lax.axis_index("y"), lax.axis_index("c")


def _other_chips(x, y):
    return [(1 - x, y), (x, 1 - y), (1 - x, 1 - y)]


def _halvable(shape):
    return shape[0] % 32 == 0


def _rows_of_half(shape, half):
    if not _halvable(shape):
        return pl.ds(0, shape[0])
    return pl.ds(pl.multiple_of(half * (shape[0] // 2), 16), shape[0] // 2)


_SEM = pl.BlockSpec(memory_space=pltpu.SEMAPHORE)
_ANY = pl.BlockSpec(memory_space=pl.ANY)
_DATAFLOW = pltpu.SideEffectType.DATAFLOW_SIDE_EFFECTING


def _in_hbm(a):
    return pltpu.with_memory_space_constraint(a, pltpu.HBM)


def _halves_copy(src_refs, land_refs, send_sems, recv_sems, shapes, a, j, block, x, y, c):
    px, py = _other_chips(x, y)[j]
    rows = _rows_of_half(shapes[a], c)
    return pltpu.make_async_remote_copy(
        src_ref=src_refs[a].at[rows, :], dst_ref=land_refs[a].at[block, rows, :], send_sem=send_sems.at[3 * a + j],
        recv_sem=recv_sems.at[3 * a + j], device_id=(px, py, c), device_id_type=MESH)


def _gather_halves_start(shards, after, name):
    n = len(shards)
    shapes = [s.shape for s in shards]

    def body(*refs):
        ins, lands = refs[:n], refs[n:2 * n]
        send_sems, recv_sems = refs[2 * n + 1], refs[2 * n + 2]
        token = refs[-1]
        x, y, c = _pos()
        q = 2 * x + y
        for a in range(n):
            for j in range(3):
                _halves_copy(ins, lands, send_sems, recv_sems, shapes, a, j, q, x, y, c).start()
        token[...] = jnp.zeros_like(token)

    land_shapes = [(N_CHIPS,) + s.shape for s in shards]
    return pl.pallas_call(
        body, name=name,
        out_shape=(pltpu.SemaphoreType.DMA((3 * n,)), pltpu.SemaphoreType.DMA((3 * n,)),
                   *[pltpu.HBM(s.shape, s.dtype) for s in shards],
                   *[pltpu.HBM(ls, s.dtype) for ls, s in zip(land_shapes, shards)],
                   jax.ShapeDtypeStruct((8, 128), F32)),
        in_specs=[_HBM] * (2 * n) + [_ANY],
        out_specs=(_SEM, _SEM, *[_HBM] * (2 * n), pl.BlockSpec(memory_space=pltpu.VMEM)),
        input_output_aliases={a: 2 + a for a in range(2 * n)},
        compiler_params=pltpu.CompilerParams(has_side_effects=_DATAFLOW),
    )(*[_in_hbm(s) for s in shards], *[_in_hbm(lax.empty(ls, s.dtype)) for ls, s in zip(land_shapes, shards)], after)


def _gather_halves_wait(started, after, name):
    send_sems, recv_sems, *thru = started
    n = len(thru) // 2
    shapes = [t.shape for t in thru[:n]]

    def body(*refs):
        ins, lands = refs[:n], refs[n:2 * n]
        send_sems, recv_sems = refs[2 * n], refs[2 * n + 1]
        x, y, c = _pos()
        q = 2 * x + y
        chips = _other_chips(x, y)
        for a in range(n):
            for j, (px, py) in enumerate(chips):
                _halves_copy(ins, lands, send_sems, recv_sems, shapes, a, j, q, x, y, c).wait_send()
                _halves_copy(ins, lands, send_sems, recv_sems, shapes, a, j, 2 * px + py, x, y, c).wait_recv()

    outs = pl.pallas_call(
        body, name=name, out_shape=[pltpu.HBM(t.shape, t.dtype) for t in thru],
        in_specs=[_HBM] * (2 * n) + [_SEM, _SEM] + [_ANY] * len(after), out_specs=[_HBM] * (2 * n),
        input_output_aliases={a: a for a in range(2 * n)},
        compiler_params=pltpu.CompilerParams(has_side_effects=_DATAFLOW),
    )(*thru, send_sems, recv_sems, *after)
    return outs[:n], outs[n:]


def _sibling_fill(gathered, name):
    big = [a for a, g in enumerate(gathered) if _halvable(g.shape[1:])]
    n = len(gathered)

    def body(*refs):
        ins, outs = refs[:n], refs[n:2 * n]
        send_sems, recv_sems = refs[2 * n:]
        x, y, c = _pos()
        chips = _other_chips(x, y)

        def copy(k, j, half):
            a = big[k]
            px, py = chips[j]
            rows = _rows_of_half(gathered[a].shape[1:], half)
            return pltpu.make_async_remote_copy(
                src_ref=ins[a].at[2 * px + py, rows, :], dst_ref=outs[a].at[2 * px + py, rows, :],
                send_sem=send_sems.at[3 * k + j], recv_sem=recv_sems.at[3 * k + j],
                device_id=(x, y, 1 - c), device_id_type=MESH)

        sends = [copy(k, j, c) for k in range(len(big)) for j in range(3)]
        for cp in sends:
            cp.start()
        for k in range(len(big)):
            for j in range(3):
                copy(k, j, 1 - c).wait_recv()
        for cp in sends:
            cp.wait_send()

    return pl.pallas_call(
        body, name=name, in_specs=[_HBM] * n, out_specs=[_HBM] * n,
        out_shape=[jax.ShapeDtypeStruct(g.shape, g.dtype) for g in gathered],
        input_output_aliases={a: a for a in range(n)},
        scratch_shapes=[pltpu.SemaphoreType.DMA((3 * len(big),)), pltpu.SemaphoreType.DMA((3 * len(big),))],
    )(*gathered)


def _place_own(shards, gathered, cq, name):
    n = len(shards)
    steps = 4

    def body(cq_ref, *refs):
        for a in range(n):
            refs[2 * n + a][...] = refs[a][...]

    def tile(shape):
        return shape[0] // steps if _halvable(shape) else shape[0]

    in_specs = [pl.BlockSpec((tile(s.shape), s.shape[1]), (lambda i, s_: (i, 0)) if _halvable(s.shape) else (lambda i, s_: (0, 0)))
                for s in shards]
    in_specs += [pl.BlockSpec(memory_space=pl.ANY)] * n
    out_specs = [pl.BlockSpec((None, tile(s.shape), s.shape[1]),
                              (lambda i, s_: (s_[1], i, 0)) if _halvable(s.shape) else (lambda i, s_: (s_[1], 0, 0)))
                 for s in shards]
    gs = pltpu.PrefetchScalarGridSpec(num_scalar_prefetch=1, grid=(steps,), in_specs=in_specs, out_specs=out_specs)
    return pl.pallas_call(
        body, name=name, grid_spec=gs, out_shape=[jax.ShapeDtypeStruct(g.shape, g.dtype) for g in gathered],
        input_output_aliases={1 + n + a: a for a in range(n)},
        compiler_params=_cparams(("arbitrary",)),
    )(cq, *shards, *gathered)


def _half_rows(ref, c, rh):
    return ref.at[:, pl.ds(pl.multiple_of(c * rh, 8), rh), :]


def _chips_copy(src_refs, land_refs, send_sems, recv_sems, a, j, x, y, c):
    px, py = _other_chips(x, y)[j]
    return pltpu.make_async_remote_copy(src_ref=src_refs[a].at[2 * px + py], dst_ref=land_refs[a].at[j],
                                        send_sem=send_sems.at[3 * a + j], recv_sem=recv_sems.at[3 * a + j],
                                        device_id=(px, py, c), device_id_type=MESH)


def _grad_chips_start(parts, name):
    n = len(parts)

    def body(*refs):
        ins, lands = refs[:n], refs[n:2 * n]
        send_sems, recv_sems = refs[2 * n], refs[2 * n + 1]
        token = refs[-1]
        x, y, c = _pos()
        for a in range(n):
            for j in range(3):
                _chips_copy(ins, lands, send_sems, recv_sems, a, j, x, y, c).start()
        token[...] = jnp.zeros_like(token)

    land_shapes = [(3,) + p.shape[1:] for p in parts]
    return pl.pallas_call(
        body, name=name,
        out_shape=(pltpu.SemaphoreType.DMA((3 * n,)), pltpu.SemaphoreType.DMA((3 * n,)),
                   *[pltpu.HBM(p.shape, p.dtype) for p in parts],
                   *[pltpu.HBM(ls, p.dtype) for ls, p in zip(land_shapes, parts)],
                   jax.ShapeDtypeStruct((8, 128), F32)),
        in_specs=[_HBM] * (2 * n),
        out_specs=(_SEM, _SEM, *[_HBM] * (2 * n), pl.BlockSpec(memory_space=pltpu.VMEM)),
        input_output_aliases={a: 2 + a for a in range(2 * n)},
        compiler_params=pltpu.CompilerParams(has_side_effects=_DATAFLOW),
    )(*[_in_hbm(p) for p in parts], *[_in_hbm(lax.empty(ls, p.dtype)) for ls, p in zip(land_shapes, parts)])


def _grad_chips_wait(started, after, name):
    send_sems, recv_sems, *thru = started
    n = len(thru) // 2

    def body(*refs):
        ins, lands = refs[:n], refs[n:2 * n]
        send_sems, recv_sems = refs[2 * n], refs[2 * n + 1]
        x, y, c = _pos()
        for a in range(n):
            for j in range(3):
                cp = _chips_copy(ins, lands, send_sems, recv_sems, a, j, x, y, c)
                cp.wait_send()
                cp.wait_recv()

    outs = pl.pallas_call(
        body, name=name, out_shape=[pltpu.HBM(t.shape, t.dtype) for t in thru],
        in_specs=[_HBM] * (2 * n) + [_SEM, _SEM] + [_ANY] * len(after), out_specs=[_HBM] * (2 * n),
        input_output_aliases={a: a for a in range(2 * n)},
        compiler_params=pltpu.CompilerParams(has_side_effects=_DATAFLOW),
    )(*thru, send_sems, recv_sems, *after)
    return outs[n:]


def _sibling_copy(src_refs, land_refs, send_sems, recv_sems, rhs, a, c, x, y):
    return pltpu.make_async_remote_copy(src_ref=_half_rows(src_refs[a], 1 - c, rhs[a]), dst_ref=land_refs[a],
                                        send_sem=send_sems.at[a], recv_sem=recv_sems.at[a],
                                        device_id=(x, y, 1 - c), device_id_type=MESH)


def _grad_sibling_start(fams, name):
    n = len(fams)
    rhs = [f.shape[1] // 2 for f in fams]

    def body(*refs):
        ins, lands = refs[:n], refs[n:2 * n]
        send_sems, recv_sems = refs[2 * n], refs[2 * n + 1]
        token = refs[-1]
        x, y, c = _pos()
        for a in range(n):
            _sibling_copy(ins, lands, send_sems, recv_sems, rhs, a, c, x, y).start()
        token[...] = jnp.zeros_like(token)

    land_shapes = [(f.shape[0], f.shape[1] // 2, f.shape[2]) for f in fams]
    return pl.pallas_call(
        body, name=name,
        out_shape=(pltpu.SemaphoreType.DMA((n,)), pltpu.SemaphoreType.DMA((n,)),
                   *[pltpu.HBM(f.shape, f.dtype) for f in fams],
                   *[pltpu.HBM(ls, f.dtype) for ls, f in zip(land_shapes, fams)],
                   jax.ShapeDtypeStruct((8, 128), F32)),
        in_specs=[_HBM] * (2 * n),
        out_specs=(_SEM, _SEM, *[_HBM] * (2 * n), pl.BlockSpec(memory_space=pltpu.VMEM)),
        input_output_aliases={a: 2 + a for a in range(2 * n)},
        compiler_params=pltpu.CompilerParams(has_side_effects=_DATAFLOW),
    )(*[_in_hbm(f) for f in fams], *[_in_hbm(lax.empty(ls, f.dtype)) for ls, f in zip(land_shapes, fams)])


def _grad_sibling_wait(started, after, name):
    send_sems, recv_sems, *thru = started
    n = len(thru) // 2
    rhs = [t.shape[1] // 2 for t in thru[:n]]

    def body(*refs):
        ins, lands = refs[:n], refs[n:2 * n]
        send_sems, recv_sems = refs[2 * n], refs[2 * n + 1]
        x, y, c = _pos()
        for a in range(n):
            cp = _sibling_copy(ins, lands, send_sems, recv_sems, rhs, a, c, x, y)
            cp.wait_send()
            cp.wait_recv()

    outs = pl.pallas_call(
        body, name=name, out_shape=[pltpu.HBM(t.shape, t.dtype) for t in thru],
        in_specs=[_HBM] * (2 * n) + [_SEM, _SEM] + [_ANY] * len(after), out_specs=[_HBM] * (2 * n),
        input_output_aliases={a: a for a in range(2 * n)},
        compiler_params=pltpu.CompilerParams(has_side_effects=_DATAFLOW),
    )(*thru, send_sems, recv_sems, *after)
    return outs[:n], outs[n:]


def _grad_share(fulls, name, small=None):
    n = len(fulls)
    ns = 0 if small is None else 1
    rhs = [f.shape[0] // 2 for f in fulls]

    def body(*refs):
        ins, outs = refs[:n], refs[n + ns:2 * n + ns]
        send_sems, recv_sems = refs[2 * (n + ns)], refs[2 * (n + ns) + 1]
        x, y, c = _pos()

        def copy(a, half):
            rows = pl.ds(pl.multiple_of(half * rhs[a], 8), rhs[a])
            return pltpu.make_async_remote_copy(src_ref=ins[a].at[rows, :], dst_ref=outs[a].at[rows, :],
                                                send_sem=send_sems.at[7 * ns + a], recv_sem=recv_sems.at[7 * ns + a],
                                                device_id=(x, y, 1 - c), device_id_type=MESH)

        sends = [copy(a, c) for a in range(n)]
        for cp in sends:
            cp.start()
        if ns:
            small_ref, all_ref = refs[n], refs[2 * n + 1]
            me = 4 * x + 2 * y + c

            def peer(r):
                dx, dy, dc = (r >> 2) & 1, (r >> 1) & 1, r & 1
                return (x if dx == 0 else 1 - x), (y if dy == 0 else 1 - y), (c if dc == 0 else 1 - c)

            def small_copy(r, slot):
                return pltpu.make_async_remote_copy(src_ref=small_ref, dst_ref=all_ref.at[slot], send_sem=send_sems.at[r - 1],
                                                    recv_sem=recv_sems.at[r - 1], device_id=peer(r), device_id_type=MESH)

            smalls = [small_copy(r, me) for r in range(1, 8)]
            for cp in smalls:
                cp.start()
            for r in range(1, 8):
                px, py, pc = peer(r)
                small_copy(r, 4 * px + 2 * py + pc).wait_recv()
            sends = sends + smalls
        for a in range(n):
            copy(a, 1 - c).wait_recv()
        for cp in sends:
            cp.wait_send()

    return pl.pallas_call(
        body, name=name, in_specs=[_HBM] * (n + ns), out_specs=[_HBM] * (n + ns),
        out_shape=[jax.ShapeDtypeStruct(f.shape, f.dtype) for f in fulls]
        + ([jax.ShapeDtypeStruct((8,) + small.shape, small.dtype)] if ns else []),
        input_output_aliases={a: a for a in range(n)},
        scratch_shapes=[pltpu.SemaphoreType.DMA((7 * ns + n,)), pltpu.SemaphoreType.DMA((7 * ns + n,))],
    )(*fulls, *([small] if ns else []))


def _add_sibling(own, recv, cq, name):
    nb, R, Cc = own.shape
    Rh = R // 2

    def body(cq_ref, a_ref, b_ref, o32_ref, o16_ref):
        s = a_ref[...] + b_ref[...]
        o32_ref[...] = s
        o16_ref[...] = s.astype(o16_ref.dtype)

    sp = pl.BlockSpec((1, Rh, Cc), lambda b, s: (b, 0, 0))
    gs = pltpu.PrefetchScalarGridSpec(
        num_scalar_prefetch=1, grid=(nb,),
        in_specs=[pl.BlockSpec((1, Rh, Cc), lambda b, s: (b, s[0], 0)), sp], out_specs=[sp, sp])
    return pl.pallas_call(
        body, name=name, grid_spec=gs,
        out_shape=[jax.ShapeDtypeStruct((nb, Rh, Cc), F32), jax.ShapeDtypeStruct((nb, Rh, Cc), _MXU)],
        compiler_params=_cparams(("parallel",)),
    )(cq, own, recv)


def _add_chips(part32, recv3, cq, name):
    nb, Rh, Cc = part32.shape

    def body(cq_ref, a_ref, b_ref, o_ref):
        acc = a_ref[0]
        for j in range(3):
            acc = acc + b_ref[j].astype(F32)
        o_ref[...] = acc

    gs = pltpu.PrefetchScalarGridSpec(
        num_scalar_prefetch=1, grid=(1,),
        in_specs=[pl.BlockSpec((1, Rh, Cc), lambda i, s: (s[1], 0, 0)), pl.BlockSpec((3, Rh, Cc), lambda i, s: (0, 0, 0))],
        out_specs=pl.BlockSpec((Rh, Cc), lambda i, s: (s[0], 0)))
    return pl.pallas_call(
        body, name=name, grid_spec=gs, out_shape=jax.ShapeDtypeStruct((2 * Rh, Cc), F32),
        compiler_params=_cparams(("arbitrary",)),
    )(cq, part32, recv3)


def _sum_devices(small_all, small, me):
    def body(me_ref, all_ref, own_ref, o_ref):
        tot = None
        for d in range(8):
            term = jnp.where(me_ref[0] == d, own_ref[...], all_ref[d])
            tot = term if tot is None else tot + term
        o_ref[...] = tot

    gs = pltpu.PrefetchScalarGridSpec(
        num_scalar_prefetch=1, grid=(1,),
        in_specs=[pl.BlockSpec(small_all.shape, lambda i, s: (0, 0, 0)), pl.BlockSpec(small.shape, lambda i, s: (0, 0))],
        out_specs=pl.BlockSpec(small.shape, lambda i, s: (0, 0)))
    return pl.pallas_call(body, name="sum_devices", grid_spec=gs,
                          out_shape=jax.ShapeDtypeStruct(small.shape, F32))(me, small_all, small)


def _adamw(w, g, m, v, name):
    R, Cc = w.shape
    T = max([t for t in range(8, 257, 8) if R % t == 0], default=R)
    c1 = 1.0 / (1.0 - ADAM_B1 ** ADAM_STEP)
    c2 = 1.0 / (1.0 - ADAM_B2 ** ADAM_STEP)

    def body(w_ref, g_ref, m_ref, v_ref, d_ref, mo_ref, vo_ref):
        gv = g_ref[...]
        mn = ADAM_B1 * m_ref[...] + (1.0 - ADAM_B1) * gv
        vn = ADAM_B2 * v_ref[...] + (1.0 - ADAM_B2) * (gv * gv)
        mo_ref[...] = mn
        vo_ref[...] = vn
        d_ref[...] = -ADAM_LR * ((mn * c1) / (jnp.sqrt(vn * c2) + ADAM_EPS) + ADAM_WD * w_ref[...])

    sp = pl.BlockSpec((T, Cc), lambda i: (i, 0))
    sh = jax.ShapeDtypeStruct((R, Cc), F32)
    return pl.pallas_call(
        body, name=name, grid=(R // T,), in_specs=[sp] * 4, out_specs=(sp, sp, sp), out_shape=(sh, sh, sh),
        compiler_params=_cparams(("parallel",)),
    )(w, g, m, v)


SMALL_ROWS = 32
REPL_ROWS = 8


def _pad_lanes(v, n=D_MODEL):
    return jnp.pad(v, ((0, 0), (0, n - v.shape[1])))


def kernel(x, norm1_w, w_in, conv_qkv_w, a_log, dt_bias, gdn_norm_w, w_out, norm2_w, w_up, ffn_conv_w, w_down, final_norm_w, loss_target, m_norm1_w, m_w_in, m_conv_qkv_w, m_a_log, m_dt_bias, m_gdn_norm_w, m_w_out, m_norm2_w, m_w_up, m_ffn_conv_w, m_w_down, m_final_norm_w, v_norm1_w, v_w_in, v_conv_qkv_w, v_a_log, v_dt_bias, v_gdn_norm_w, v_w_out, v_norm2_w, v_w_up, v_ffn_conv_w, v_w_down, v_final_norm_w):
    c = lax.axis_index("c")
    q = 2 * lax.axis_index("x") + lax.axis_index("y")
    S = x.shape[1]
    cq = jnp.stack([c, q]).astype(jnp.int32)

    *in_started, in_token = _gather_halves_start([w_in[0].astype(_MXU), conv_qkv_w[0], ffn_conv_w[0]], x, "gather_in_start")
    never = in_token[0:1, 0:1] != 0.0
    w_in_l, m_w_in_l, v_w_in_l = (jnp.where(never, b, a) for a, b in ((w_in, m_w_in), (m_w_in, v_w_in), (v_w_in, w_in)))
    h1 = _rmsnorm_fwd(x[0], norm1_w + in_token[0:1, 0:1], "norm1")
    rest = [(a[0] + in_token[0:1, 0:1]).astype(_MXU) for a in (w_out, w_up, w_down)]
    in_shards, got_in = _gather_halves_wait(in_started, [w_in_l, m_w_in_l, v_w_in_l, h1, *rest], "gather_in_wait")
    g_in, g_conv, g_fconv = _place_own(in_shards, _sibling_fill(got_in, "fill_in"), cq, "place_in")
    *rest_started, token = _gather_halves_start(rest, g_conv, "gather_rest_start")

    def rest_weights(after):
        shards, got = _gather_halves_wait(rest_started, after, "gather_rest_wait")
        got = _sibling_fill(got, "fill_rest")
        g_out, g_up, g_down = _place_own(shards, got, cq, "place_rest")
        return g_out.reshape(D_MODEL, D_MODEL), g_up, g_down.reshape(D_FF, D_MODEL)
    wp = _wp_assemble(g_in, [token])
    conv_f = jnp.concatenate([g_conv[i] for i in range(N_CHIPS)], axis=1)
    fcw = jnp.concatenate([g_fconv[i] for i in range(N_CHIPS)], axis=1)
    gp = _pad_lanes(jnp.concatenate([a_log, dt_bias], axis=1), 128)
    fnw = final_norm_w[None, :]
    early = {}

    def early_sibling(d_wup, d_wdown):
        *early["sibling"], tok = _grad_sibling_start([d_wup, d_wdown.reshape(N_CHIPS, D_FF // N_CHIPS, D_MODEL)],
                                                     "grad_sibling_early_start")
        return tok

    def early_chips(dx2):
        fams_e, got_e = _grad_sibling_wait(early["sibling"], [dx2], "grad_sibling_early_wait")
        early["parts"] = [_add_sibling(f, r, cq, "add_sibling_" + nm) for f, r, nm in zip(fams_e, got_e, ("w_up", "w_down"))]
        *early["started"], tok = _grad_chips_start([p[1] for p in early["parts"]], "grad_chips_start")
        return tok

    def late_sibling(d_wp, d_wout):
        *early["late_sibling"], tok = _grad_sibling_start(
            [_win_split(d_wp), d_wout.reshape(N_CHIPS, D_MODEL // N_CHIPS, D_MODEL)], "grad_sibling_late_start")
        return tok

    early_grads = (early_sibling, early_chips, late_sibling)

    loss_l, dx, g = _local_step(x[0], loss_target[0], h1, norm1_w, norm2_w + token[0:1, 0:1], fnw, gp, gdn_norm_w, wp,
                                conv_f, fcw, rest_weights, early_grads)
    n_fc = FFN_CONV * D_FF

    def rows_of(v):
        flat = v.reshape(-1)
        return jnp.pad(flat, (0, -flat.shape[0] % D_MODEL)).reshape(-1, D_MODEL)

    gp_row = _pad_lanes(jnp.concatenate([g["gp"][0:1, 0:8], loss_l[0:1, 0:1]], axis=1))
    small = jnp.concatenate([g["n1w"], g["n2w"], g["fnw"], gp_row, _pad_lanes(g["gnw"]),
                             rows_of(g["conv_w"]), rows_of(g["fcw_g"]), rows_of(g["fcw_u"])], axis=0)
    small = jnp.pad(small, ((0, SMALL_ROWS - small.shape[0]), (0, 0)))
    fams, got = _grad_sibling_wait(early["late_sibling"], [dx], "grad_sibling_late_wait")
    parts = [_add_sibling(f, r, cq, "add_sibling_" + nm) for f, r, nm in zip(fams, got, ("w_in", "w_out"))]
    *late_started, late_token = _grad_chips_start([p[1] for p in parts], "grad_chips_late_start")
    got3_e = _grad_chips_wait(early["started"], [dx, g["wp"], late_token], "grad_chips_wait")
    g_w_up, g_w_down = _grad_share(
        [_add_chips(p[0], r3, cq, "add_chips_" + nm) for p, r3, nm in zip(early["parts"], got3_e, ("w_up", "w_down"))],
        "grad_share_early")
    big = {}

    def adamw_big(nm, w, gg, m, v):
        d_, m_, v_ = _adamw(w[0], gg, m[0], v[0], "adamw_" + nm)
        big[nm] = (gg[None], d_[None], m_[None], v_[None])

    adamw_big("w_up", w_up, g_w_up, m_w_up, v_w_up)
    adamw_big("w_down", w_down, g_w_down, m_w_down, v_w_down)
    got3 = _grad_chips_wait(late_started, [big["w_up"][1], big["w_down"][1]], "grad_chips_late_wait")
    g_w_in, g_w_out, small_all = _grad_share(
        [_add_chips(p[0], r3, cq, "add_chips_" + nm) for p, r3, nm in zip(parts, got3, ("w_in", "w_out"))],
        "grad_share_late", small)
    small_red = _sum_devices(small_all, small, (2 * q + c).astype(jnp.int32).reshape(1))
    loss = small_red[3, 8]
    r0 = 5
    r1 = r0 + GDN_CONV * 3 * GDN_WIDTH // D_MODEL
    r2 = r1 + -(-n_fc // D_MODEL)
    conv_red = small_red[r0:r1].reshape(GDN_CONV, 3 * GDN_WIDTH)
    fc_red = jnp.concatenate([small_red[r1:r2].reshape(-1)[:n_fc].reshape(FFN_CONV, D_FF),
                              small_red[r2:2 * r2 - r1].reshape(-1)[:n_fc].reshape(FFN_CONV, D_FF)], axis=1)
    g_conv_w = lax.dynamic_slice_in_dim(conv_red, q * (3 * GDN_WIDTH // N_CHIPS), 3 * GDN_WIDTH // N_CHIPS, axis=1)
    g_fconv_w = lax.dynamic_slice_in_dim(fc_red, q * (2 * D_FF // N_CHIPS), 2 * D_FF // N_CHIPS, axis=1)
    g_n1w, g_n2w, g_fnw = small_red[0:1], small_red[1:2], small_red[2]
    g_alog, g_dtb, g_gnw = small_red[3:4, 0:4], small_red[3:4, 4:8], small_red[4:5, 0:128]
    for nm, w, gg, m, v in (("w_in", w_in_l, g_w_in, m_w_in_l, v_w_in_l), ("conv_qkv_w", conv_qkv_w, g_conv_w, m_conv_qkv_w, v_conv_qkv_w),
                            ("w_out", w_out, g_w_out, m_w_out, v_w_out),
                            ("ffn_conv_w", ffn_conv_w, g_fconv_w, m_ffn_conv_w, v_ffn_conv_w)):
        adamw_big(nm, w, gg, m, v)

    def pack_small(n1, n2, fn, al, db, gn):
        return jnp.concatenate([n1, n2, fn[None, :], _pad_lanes(jnp.concatenate([al, db], axis=1)), _pad_lanes(gn),
                                jnp.zeros((REPL_ROWS - 5, D_MODEL), F32)], axis=0)

    sw = pack_small(norm1_w, norm2_w, final_norm_w, a_log, dt_bias, gdn_norm_w)
    sm = pack_small(m_norm1_w, m_norm2_w, m_final_norm_w, m_a_log, m_dt_bias, m_gdn_norm_w)
    sv = pack_small(v_norm1_w, v_norm2_w, v_final_norm_w, v_a_log, v_dt_bias, v_gdn_norm_w)
    sd, smn, svn = _adamw(sw, small_red[:REPL_ROWS], sm, sv, "adamw_small")

    def unpack_small(t):
        return dict(norm1_w=t[0:1], norm2_w=t[1:2], final_norm_w=t[2], a_log=t[3:4, 0:4], dt_bias=t[3:4, 4:8],
                    gdn_norm_w=t[4:5, 0:128])

    sg = dict(norm1_w=g_n1w, norm2_w=g_n2w, final_norm_w=g_fnw, a_log=g_alog, dt_bias=g_dtb, gdn_norm_w=g_gnw)
    sd, smn, svn = unpack_small(sd), unpack_small(smn), unpack_small(svn)
    names = ["norm1_w", "w_in", "conv_qkv_w", "a_log", "dt_bias", "gdn_norm_w", "w_out", "norm2_w", "w_up",
             "ffn_conv_w", "w_down", "final_norm_w"]
    grads = [big[n][0] if n in big else sg[n] for n in names]
    deltas = [big[n][1] if n in big else sd[n] for n in names]
    new_m = [big[n][2] if n in big else smn[n] for n in names]
    new_v = [big[n][3] if n in big else svn[n] for n in names]
    return (loss, dx[None], *grads, *deltas, *new_m, *new_v)
```

```python
import functools
import math

import numpy as np
import jax
import jax.numpy as jnp
from jax import lax
from jax.experimental import pallas as pl
from jax.experimental.pallas import tpu as pltpu

F32 = jnp.float32
BF16 = jnp.bfloat16
_MXU = jnp.bfloat16
_HI = lax.Precision.HIGHEST
EPS = 1e-6
V7X_VMEM_LIMIT = 56 * 1024 * 1024
MESH = pl.DeviceIdType.MESH

D_MODEL = 1024
GDN_HEADS, GDN_DIM, GDN_CHUNK, GDN_CONV = 4, 128, 64, 4
GDN_WIDTH = GDN_HEADS * GDN_DIM
DIL_HEADS, DIL_DIM = 8, 64
DIL_WIDTH = DIL_HEADS * DIL_DIM
D_FF, FFN_CONV = 2816, 3
IN_COLS = 3592
P_COLS = 3840
P_Z, P_QKVB, P_BA = 1536, 2048, 3584
ATT_T = 1024
ADAM_LR, ADAM_B1, ADAM_B2, ADAM_EPS, ADAM_WD, ADAM_STEP = 0.001, 0.9, 0.999, 1e-08, 0.01, 10
N_CHIPS = 4


def _cparams(sem=None, vmem=None):
    kw = {}
    if sem is not None:
        kw["dimension_semantics"] = sem
    if vmem is not None:
        kw["vmem_limit_bytes"] = vmem
    return pltpu.CompilerParams(**kw)


def _silu(x):
    return x * jax.nn.sigmoid(x)


def _pick_tile(n, cap):
    best = None
    for t in range(128, min(n, cap) + 1, 128):
        if n % t == 0:
            best = t
    return best or n


def _mm(a, b, mode, *, out_dtype=F32, residual=None, name, b_blocks=False, place=None, into=None, tn=None):
    if mode == "nn":
        M, K = a.shape
        N = b.shape[0] * b.shape[2] if b_blocks else b.shape[1]
    elif mode == "nt":
        (M, K), (N, _) = a.shape, b.shape
    else:
        (K, M), (_, N) = a.shape, b.shape
    tm = _pick_tile(M, 1024)
    tn = b.shape[2] if b_blocks else (tn or _pick_tile(N, 1536))

    def vmem(tm, tn):
        return 2 * (tm * K * a.dtype.itemsize + tn * K * b.dtype.itemsize
                    + tm * tn * (jnp.dtype(out_dtype).itemsize + (4 if residual is not None else 0))) + 3 * tm * tn * 4

    fixed_tn = b_blocks or (place is not None and place[0] == "blocks")
    while vmem(tm, tn) > 40 * 1024 * 1024:
        if (tm >= tn or fixed_tn) and tm % 256 == 0:
            tm //= 2
        elif tn % 256 == 0 and not fixed_tn:
            tn //= 2
        else:
            tm //= 2
    a_spec = pl.BlockSpec((K, tm), lambda j, i: (0, i)) if mode == "tn" else pl.BlockSpec((tm, K), lambda j, i: (i, 0))
    if b_blocks:
        b_spec = pl.BlockSpec((None, K, tn), lambda j, i: (j, 0, 0))
    else:
        b_spec = pl.BlockSpec((tn, K), lambda j, i: (j, 0)) if mode == "nt" else pl.BlockSpec((K, tn), lambda j, i: (0, j))
    r_spec = pl.BlockSpec((tm, tn), lambda j, i: (i, j))
    if place is None:
        o_spec, o_shape = r_spec, (M, N)
    elif place[0] == "rows":
        off = place[2] // tm
        o_spec, o_shape = pl.BlockSpec((tm, tn), lambda j, i: (i + off, j)), (place[1], N)
    else:
        off = place[2]
        o_spec, o_shape = pl.BlockSpec((None, tm, tn), lambda j, i: (j + off, i, 0)), (place[1], M, tn)
    dims = {"nn": (((1,), (0,)), ((), ())), "nt": (((1,), (1,)), ((), ())), "tn": (((0,), (0,)), ((), ()))}[mode]

    def body(*refs):
        a_ref, b_ref = refs[0], refs[1]
        o_ref = refs[-1]
        acc = lax.dot_general(a_ref[...].astype(_MXU), b_ref[...].astype(_MXU), dims, preferred_element_type=F32)
        if residual is not None:
            acc = acc + refs[2][...]
        o_ref[...] = acc.astype(out_dtype)

    ins, specs, alias = [a, b], [a_spec, b_spec], {}
    if residual is not None:
        ins.append(residual)
        specs.append(r_spec)
    if into is not None:
        alias = {len(ins): 0}
        ins.append(into)
        specs.append(pl.BlockSpec(memory_space=pl.ANY))
    return pl.pallas_call(
        body, name=name, grid=(N // tn, M // tm), in_specs=specs, out_specs=o_spec,
        out_shape=jax.ShapeDtypeStruct(o_shape, out_dtype), input_output_aliases=alias,
        compiler_params=_cparams(("parallel", "parallel"), V7X_VMEM_LIMIT),
    )(*ins)


def _mm_nt_blocks(a_list, b4, name):
    M = a_list[0].shape[0]
    nb, N, Kb = b4.shape
    tm, tn = _pick_tile(M, 1024), _pick_tile(N, 512)

    def body(a0_ref, a1_ref, b_ref, o_ref):
        acc = None
        for blk in range(nb):
            a_ref = (a0_ref, a1_ref)[blk // 2]
            lo = (blk % 2) * Kb
            t = lax.dot_general(a_ref[:, lo:lo + Kb].astype(_MXU), b_ref[blk].astype(_MXU), (((1,), (1,)), ((), ())),
                                preferred_element_type=F32)
            acc = t if acc is None else acc + t
        o_ref[...] = acc

    a_spec = pl.BlockSpec((tm, 2 * Kb), lambda j, i: (i, 0))
    return pl.pallas_call(
        body, name=name, grid=(N // tn, M // tm),
        in_specs=[a_spec, a_spec, pl.BlockSpec((nb, tn, Kb), lambda j, i: (0, j, 0))],
        out_specs=pl.BlockSpec((tm, tn), lambda j, i: (i, j)), out_shape=jax.ShapeDtypeStruct((M, N), F32),
        compiler_params=_cparams(("parallel", "parallel"), V7X_VMEM_LIMIT),
    )(a_list[0], a_list[1], b4)


def _wp_assemble(g_in, after=()):
    nb, Dm, Wb = g_in.shape
    T = 256
    n_lo = P_QKVB - 2 * Wb

    def body(g_ref, *rest):
        g2 = g_ref[2]
        rest[-1][...] = jnp.concatenate(
            [g_ref[0], g_ref[1], g2[:, :n_lo], g2[:, n_lo + 8:], g_ref[3], g2[:, n_lo:n_lo + 8],
             jnp.zeros((T, P_COLS - P_BA - 8), g_in.dtype)], axis=1)

    return pl.pallas_call(
        body, name="wp_assemble", grid=(Dm // T,),
        in_specs=[pl.BlockSpec((nb, T, Wb), lambda i: (0, i, 0))] + [pl.BlockSpec(memory_space=pl.ANY)] * len(after),
        out_specs=pl.BlockSpec((T, P_COLS), lambda i: (i, 0)), out_shape=jax.ShapeDtypeStruct((Dm, P_COLS), g_in.dtype),
        compiler_params=_cparams(("parallel",)),
    )(g_in, *after)


def _win_split(d_wp):
    Dm = d_wp.shape[0]
    Wb = IN_COLS // N_CHIPS
    T = 256

    def body(x_ref, o_ref):
        xv = x_ref[...]
        o_ref[0] = xv[:, 0:Wb]
        o_ref[1] = xv[:, Wb:2 * Wb]
        o_ref[2] = jnp.concatenate([xv[:, 2 * Wb:P_QKVB], xv[:, P_BA:P_BA + 8], xv[:, P_QKVB:3 * Wb - 8]], axis=1)
        o_ref[3] = xv[:, 3 * Wb - 8:P_BA]

    return pl.pallas_call(
        body, name="win_split", grid=(Dm // T,), in_specs=[pl.BlockSpec((T, P_COLS), lambda i: (i, 0))],
        out_specs=pl.BlockSpec((N_CHIPS, T, Wb), lambda i: (0, i, 0)),
        out_shape=jax.ShapeDtypeStruct((N_CHIPS, Dm, Wb), F32), compiler_params=_cparams(("parallel",)),
    )(d_wp)


def _rmsnorm_fwd(x, w, name):
    S, D = x.shape
    T = _pick_tile(S, 512)

    def body(x_ref, w_ref, o_ref):
        xv = x_ref[...]
        rs = lax.rsqrt(jnp.mean(xv * xv, axis=-1, keepdims=True) + EPS)
        o_ref[...] = (xv * rs * w_ref[...]).astype(o_ref.dtype)

    return pl.pallas_call(
        body, name=name, grid=(S // T,),
        in_specs=[pl.BlockSpec((T, D), lambda i: (i, 0)), pl.BlockSpec((1, D), lambda i: (0, 0))],
        out_specs=pl.BlockSpec((T, D), lambda i: (i, 0)),
        out_shape=jax.ShapeDtypeStruct((S, D), _MXU),
        compiler_params=_cparams(("parallel",)),
    )(x, w)


def _rmsnorm_bwd(dh, x, w, dres, name):
    S, D = x.shape
    T = _pick_tile(S, 512)

    def body(dh_ref, x_ref, w_ref, dres_ref, dx_ref, dw_ref):
        xv = x_ref[...]
        rs = lax.rsqrt(jnp.mean(xv * xv, axis=-1, keepdims=True) + EPS)
        xn = xv * rs
        dhv = dh_ref[...]
        dxn = dhv * w_ref[...]
        dx_ref[...] = dres_ref[...] + rs * (dxn - xn * jnp.mean(dxn * xn, axis=-1, keepdims=True))

        @pl.when(pl.program_id(0) == 0)
        def _():
            dw_ref[...] = jnp.zeros_like(dw_ref)

        dw_ref[...] += jnp.sum(dhv * xn, axis=0, keepdims=True)

    row = pl.BlockSpec((T, D), lambda i: (i, 0))
    vec = pl.BlockSpec((1, D), lambda i: (0, 0))
    return pl.pallas_call(
        body, name=name, grid=(S // T,), in_specs=[row, row, vec, row], out_specs=(row, vec),
        out_shape=(jax.ShapeDtypeStruct((S, D), F32), jax.ShapeDtypeStruct((1, D), F32)),
        compiler_params=_cparams(("arbitrary",)),
    )(dh, x, w, dres)


def _loss_head(x3, w, tgt, name):
    S, D = x3.shape
    T = _pick_tile(S, 512)

    def body(x_ref, w_ref, t_ref, loss_ref, dx_ref, dxn_ref, dw_ref):
        xv = x_ref[...]
        rs = lax.rsqrt(jnp.mean(xv * xv, axis=-1, keepdims=True) + EPS)
        xn = xv * rs
        err = xn * w_ref[...] - t_ref[...]
        dy = err * (1.0 / D)
        dxn = dy * w_ref[...]
        dxv = rs * (dxn - xn * jnp.mean(dxn * xn, axis=-1, keepdims=True))
        dx_ref[...] = dxv
        dxn_ref[...] = dxv.astype(dxn_ref.dtype)

        @pl.when(pl.program_id(0) == 0)
        def _():
            dw_ref[...] = jnp.zeros_like(dw_ref)
            loss_ref[...] = jnp.zeros_like(loss_ref)

        dw_ref[...] += jnp.sum(dy * xn, axis=0, keepdims=True)
        part = jnp.sum(jnp.sum(err * err, axis=-1, keepdims=True), axis=0, keepdims=True) * (0.5 / D)
        loss_ref[...] += jnp.broadcast_to(part, loss_ref.shape)

    row = pl.BlockSpec((T, D), lambda i: (i, 0))
    vec = pl.BlockSpec((1, D), lambda i: (0, 0))
    return pl.pallas_call(
        body, name=name, grid=(S // T,), in_specs=[row, vec, row],
        out_specs=(pl.BlockSpec((8, 128), lambda i: (0, 0)), row, row, vec),
        out_shape=(jax.ShapeDtypeStruct((8, 128), F32), jax.ShapeDtypeStruct((S, D), F32), jax.ShapeDtypeStruct((S, D), _MXU),
                   jax.ShapeDtypeStruct((1, D), F32)),
        compiler_params=_cparams(("arbitrary",)),
    )(x3, w, tgt)


def _shifted(ext, back, lo, n):
    if back == 0:
        return ext[lo:lo + n, :]
    return pltpu.roll(ext, back % ext.shape[0], 0)[lo:lo + n, :]


def _conv_windows(ext, K, T):
    return [_shifted(ext, (K - 1) - i, 8, T) for i in range(K)]


def _conv_taps(ext, w, K, T):
    out = None
    for i, win in enumerate(_conv_windows(ext, K, T)):
        term = win * w[i:i + 1, :]
        out = term if out is None else out + term
    return out


def _conv_taps_t(ext, w, K, T):
    out = None
    for i in range(K):
        term = _shifted(ext, i - (K - 1), 0, T) * w[i:i + 1, :]
        out = term if out is None else out + term
    return out


def _tri_masks(C):
    r = lax.broadcasted_iota(jnp.int32, (C, C), 0)
    c = lax.broadcasted_iota(jnp.int32, (C, C), 1)
    return r == c, r >= c, r > c, r <= c


_NN, _NT, _TN = ((1,), (0,)), ((1,), (1,)), ((0,), (0,))
_GDN_PASSES = dict(qk=1, inv=1, sol=1, scan=1, bwd=1)


def _bdot_raw(a, b, kind, passes):
    dims = ({"NN": ((2,), (1,)), "NT": ((2,), (2,)), "TN": ((1,), (1,))}[kind], ((0,), (0,)))
    if passes == 0:
        return lax.dot_general(a, b, dims, precision=_HI, preferred_element_type=F32)
    ah, bh = a.astype(BF16), b.astype(BF16)
    out = lax.dot_general(ah, bh, dims, preferred_element_type=F32)
    if passes == 3:
        al, bl = (a - ah.astype(F32)).astype(BF16), (b - bh.astype(F32)).astype(BF16)
        out = out + lax.dot_general(ah, bl, dims, preferred_element_type=F32) + lax.dot_general(al, bh, dims, preferred_element_type=F32)
    return out


@functools.partial(jax.custom_vjp, nondiff_argnums=(2, 3))
def _bdot(a, b, kind, passes):
    return _bdot_raw(a, b, kind, passes)


def _bdot_fwd(a, b, kind, passes):
    return _bdot_raw(a, b, kind, passes), (a, b)


def _bdot_bwd(kind, passes, res, ct):
    a, b = res
    if kind == "NN":
        return _bdot_raw(ct, b, "NT", passes), _bdot_raw(a, ct, "TN", passes)
    if kind == "NT":
        return _bdot_raw(ct, b, "NN", passes), _bdot_raw(ct, a, "TN", passes)
    return _bdot_raw(b, ct, "NT", passes), _bdot_raw(a, ct, "NN", passes)


_bdot.defvjp(_bdot_fwd, _bdot_bwd)


def _softplus(x):
    return jnp.maximum(x, 0.0) + jnp.log(1.0 + jnp.exp(-jnp.abs(x)))


def _gdn_stage1(cq, ck, cv, b_col, a_col, alog, dtb, dot=_bdot_raw):
    C = cq.shape[1]
    eye, incl, strict, incl_t = _tri_masks(C)
    qn = cq * lax.rsqrt(jnp.sum(cq * cq, axis=-1, keepdims=True) + EPS) * (GDN_DIM ** -0.5)
    kn = ck * lax.rsqrt(jnp.sum(ck * ck, axis=-1, keepdims=True) + EPS)
    beta = jax.nn.sigmoid(b_col)
    g = -jnp.exp(alog) * _softplus(a_col + dtb)
    g_row = jnp.sum(jnp.where(eye, g, 0.0), axis=1, keepdims=True)
    beta_row = jnp.sum(jnp.where(eye, beta, 0.0), axis=1, keepdims=True)
    gc_col = jnp.sum(jnp.where(incl, g_row, 0.0), axis=2, keepdims=True)
    gc_row = jnp.sum(jnp.where(incl_t, g, 0.0), axis=1, keepdims=True)
    dec = jnp.where(incl, jnp.exp(jnp.where(incl, gc_col - gc_row, 0.0)), 0.0)
    kk = dot(kn, kn, "NT", _GDN_PASSES["qk"])
    qk = dot(qn, kn, "NT", _GDN_PASSES["qk"])
    lmat = jnp.where(strict, dec * kk * beta_row, 0.0)
    attn = dec * qk * beta_row
    gam = jnp.exp(gc_col)
    gc_last = gc_col[:, C - 1:C, :]
    k_end = kn * (jnp.exp(gc_last - gc_col) * beta)
    return lmat, cv, gam * kn, gam * qn, attn, k_end, jnp.exp(gc_last)


def _tri_inv(lmat):
    C = lmat.shape[1]
    eye = _tri_masks(C)[0]
    ps = _GDN_PASSES["inv"]
    p = jnp.where(eye, 1.0, 0.0) - lmat
    lp = _bdot_raw(lmat, lmat, "NN", ps)
    n = int(math.log2(C))
    for s in range(1, n):
        p = p + _bdot_raw(p, lp, "NN", ps)
        if s < n - 1:
            lp = _bdot_raw(lp, lp, "NN", ps)
    return p


def _gated_norm(o, z, gnw):
    on = o * lax.rsqrt(jnp.mean(o * o, axis=-1, keepdims=True) + EPS) * gnw
    return on * _silu(z)


GDN_PG = 2
GDN_SG = 4


def _gdn_pairs(c, ba, gp, G):
    C, W, H = GDN_CHUNK, GDN_WIDTH, GDN_HEADS
    pairs = [(j, h) for j in range(G) for h in range(H)]
    cq, ck, cv = (jnp.stack([c[C * j:C * (j + 1), o + GDN_DIM * h:o + GDN_DIM * (h + 1)] for j, h in pairs]) for o in (0, W, 2 * W))
    b_col = jnp.stack([ba[C * j:C * (j + 1), h:h + 1] for j, h in pairs])
    a_col = jnp.stack([ba[C * j:C * (j + 1), H + h:H + h + 1] for j, h in pairs])
    alog = jnp.stack([gp[0:1, h:h + 1] for j, h in pairs])
    dtb = jnp.stack([gp[0:1, H + h:H + h + 1] for j, h in pairs])
    return pairs, (cq, ck, cv, b_col, a_col, alog, dtb)


def _gdn_pre_specs(S, G):
    C = GDN_CHUNK
    T = C * G
    return dict(
        cur=pl.BlockSpec((T, 3 * GDN_WIDTH), lambda i: (i, 0)),
        prev=pl.BlockSpec((8, 3 * GDN_WIDTH), lambda i: (jnp.maximum(i * (T // 8) - 1, 0), 0)),
        ba=pl.BlockSpec((T, 128), lambda i: (i, P_BA // 128)),
        cw=pl.BlockSpec((GDN_CONV, 3 * GDN_WIDTH), lambda i: (0, 0)),
        vec=pl.BlockSpec((1, 128), lambda i: (0, 0)),
        hd=pl.BlockSpec((GDN_HEADS, T, GDN_DIM), lambda i: (0, i, 0)),
        hc=pl.BlockSpec((GDN_HEADS, T, C), lambda i: (0, i, 0)),
        ge=pl.BlockSpec((G, GDN_HEADS, 8, 128), lambda i: (i, 0, 0, 0)),
    )


def _hd_shape(S, last=GDN_DIM):
    return jax.ShapeDtypeStruct((GDN_HEADS, S, last), F32)


def _gdn_pre(proj, conv_w, gp):
    S = proj.shape[0]
    C, G = GDN_CHUNK, GDN_PG
    nc = S // C
    sp = _gdn_pre_specs(S, G)

    def body(cur_ref, prev_ref, ba_ref, cw_ref, gp_ref, uv_ref, wk_ref, qd_ref, ke_ref, at_ref, ti_ref, ge_ref):
        prev = prev_ref[...] * jnp.where(pl.program_id(0) == 0, 0.0, 1.0)
        c = _silu(_conv_taps(jnp.concatenate([prev, cur_ref[...]], axis=0), cw_ref[...], GDN_CONV, C * G))
        pairs, args = _gdn_pairs(c, ba_ref[...], gp_ref[...], G)
        lmat, v, rk, q_dec, attn, k_end, g_end = _gdn_stage1(*args)
        t = _tri_inv(lmat)
        u_v = _bdot_raw(t, v, "NN", _GDN_PASSES["sol"])
        w_k = _bdot_raw(t, rk, "NN", _GDN_PASSES["sol"])
        for b, (j, h) in enumerate(pairs):
            rows = slice(C * j, C * (j + 1))
            uv_ref[h, rows, :] = u_v[b]
            wk_ref[h, rows, :] = w_k[b]
            qd_ref[h, rows, :] = q_dec[b]
            ke_ref[h, rows, :] = k_end[b]
            at_ref[h, rows, :] = attn[b]
            ti_ref[h, rows, :] = t[b]
            ge_ref[j, h] = jnp.broadcast_to(g_end[b], (8, 128))

    return pl.pallas_call(
        body, name="gdn_pre", grid=(nc // G,),
        in_specs=[sp["cur"], sp["prev"], sp["ba"], sp["cw"], sp["vec"]],
        out_specs=(sp["hd"], sp["hd"], sp["hd"], sp["hd"], sp["hc"], sp["hc"], sp["ge"]),
        out_shape=(_hd_shape(S), _hd_shape(S), _hd_shape(S), _hd_shape(S), _hd_shape(S, C), _hd_shape(S, C),
                   jax.ShapeDtypeStruct((nc, GDN_HEADS, 8, 128), F32)),
        compiler_params=_cparams(("parallel",)),
    )(proj, proj, proj, conv_w, gp)


def _gdn_scan_specs(S, G, rev):
    C = GDN_CHUNK
    T = C * G
    n = S // T
    ci = (lambda i: n - 1 - i) if rev else (lambda i: i)
    return dict(
        hd=pl.BlockSpec((GDN_HEADS, T, GDN_DIM), lambda i: (0, ci(i), 0)),
        hc=pl.BlockSpec((GDN_HEADS, T, C), lambda i: (0, ci(i), 0)),
        ge=pl.BlockSpec((G, GDN_HEADS, 8, 128), lambda i: (ci(i), 0, 0, 0)),
        z=pl.BlockSpec((T, GDN_WIDTH), lambda i: (ci(i), P_Z // GDN_WIDTH)),
        oa=pl.BlockSpec((T, GDN_WIDTH), lambda i: (ci(i), 0)),
        vec=pl.BlockSpec((1, 128), lambda i: (0, 0)),
        st=pl.BlockSpec((G, GDN_HEADS, GDN_DIM, GDN_DIM), lambda i: (ci(i), 0, 0, 0)),
    )


def _gdn_scan(u_v, w_k, q_dec, k_end, attn, g_end, proj, gnw):
    S = proj.shape[0]
    C, G = GDN_CHUNK, GDN_SG
    nc = S // C
    sp = _gdn_scan_specs(S, G, False)
    ps = _GDN_PASSES["scan"]

    def body(uv_ref, wk_ref, qd_ref, ke_ref, at_ref, ge_ref, z_ref, gnw_ref, oa_ref, st_ref, s_scr):
        @pl.when(pl.program_id(0) == 0)
        def _():
            s_scr[...] = jnp.zeros_like(s_scr)

        for j in range(G):
            rows = slice(C * j, C * (j + 1))
            st = s_scr[...]
            st_ref[j] = st
            u = uv_ref[:, rows, :] - _bdot_raw(wk_ref[:, rows, :], st, "NN", ps)
            o = _bdot_raw(qd_ref[:, rows, :], st, "NN", ps) + _bdot_raw(at_ref[:, rows, :], u, "NN", ps)
            s_scr[...] = ge_ref[j][:, 0:1, 0:1] * st + _bdot_raw(ke_ref[:, rows, :], u, "TN", ps)
            for h in range(GDN_HEADS):
                cols = slice(GDN_DIM * h, GDN_DIM * (h + 1))
                oa_ref[rows, cols] = _gated_norm(o[h], z_ref[rows, cols], gnw_ref[...])

    return pl.pallas_call(
        body, name="gdn_scan", grid=(nc // G,),
        in_specs=[sp["hd"], sp["hd"], sp["hd"], sp["hd"], sp["hc"], sp["ge"], sp["z"], sp["vec"]],
        out_specs=(sp["oa"], sp["st"]),
        out_shape=(jax.ShapeDtypeStruct((S, GDN_WIDTH + DIL_WIDTH), F32),
                   jax.ShapeDtypeStruct((nc, GDN_HEADS, GDN_DIM, GDN_DIM), F32)),
        scratch_shapes=[pltpu.VMEM((GDN_HEADS, GDN_DIM, GDN_DIM), F32)],
        compiler_params=_cparams(("arbitrary",)),
    )(u_v, w_k, q_dec, k_end, attn, g_end, proj, gnw)


def _gdn_scan_bwd(u_v, w_k, q_dec, k_end, attn, g_end, proj, gnw, states, d_oa):
    S = proj.shape[0]
    C, G = GDN_CHUNK, GDN_SG
    nc = S // C
    sp = _gdn_scan_specs(S, G, True)
    ps, pb = _GDN_PASSES["scan"], _GDN_PASSES["bwd"]

    def body(uv_ref, wk_ref, qd_ref, ke_ref, at_ref, ge_ref, z_ref, gnw_ref, st_ref, doa_ref,
             duv_ref, dwk_ref, dqd_ref, dke_ref, dat_ref, dge_ref, dz_ref, dgnw_ref, ds_scr):
        @pl.when(pl.program_id(0) == 0)
        def _():
            ds_scr[...] = jnp.zeros_like(ds_scr)
            dgnw_ref[...] = jnp.zeros_like(dgnw_ref)

        dgnw = jnp.zeros((1, 128), F32)
        for j in reversed(range(G)):
            rows = slice(C * j, C * (j + 1))
            st = st_ref[j]
            wk, qd, ke, at = wk_ref[:, rows, :], qd_ref[:, rows, :], ke_ref[:, rows, :], at_ref[:, rows, :]
            u = uv_ref[:, rows, :] - _bdot_raw(wk, st, "NN", ps)
            o = _bdot_raw(qd, st, "NN", ps) + _bdot_raw(at, u, "NN", ps)
            dos = []
            for h in range(GDN_HEADS):
                cols = slice(GDN_DIM * h, GDN_DIM * (h + 1))
                _, vjp2 = jax.vjp(_gated_norm, o[h], z_ref[rows, cols], gnw_ref[...])
                do_h, dz_h, dgn = vjp2(doa_ref[rows, cols])
                dz_ref[rows, cols] = dz_h
                dgnw = dgnw + dgn
                dos.append(do_h)
            do = jnp.stack(dos)
            ds_new = ds_scr[...]
            du = _bdot_raw(at, do, "TN", pb) + _bdot_raw(ke, ds_new, "NN", pb)
            duv_ref[:, rows, :] = du
            dat_ref[:, rows, :] = _bdot_raw(do, u, "NT", pb)
            dqd_ref[:, rows, :] = _bdot_raw(do, st, "NT", pb)
            dke_ref[:, rows, :] = _bdot_raw(u, ds_new, "NT", pb)
            dwk_ref[:, rows, :] = -_bdot_raw(du, st, "NT", pb)
            d_ge = jnp.sum(jnp.sum(st * ds_new, axis=2, keepdims=True), axis=1, keepdims=True)
            dge_ref[j] = jnp.broadcast_to(d_ge, (GDN_HEADS, 8, 128))
            ds_scr[...] = ge_ref[j][:, 0:1, 0:1] * ds_new + _bdot_raw(qd, do, "TN", pb) - _bdot_raw(wk, du, "TN", pb)
        dgnw_ref[...] += dgnw

    return pl.pallas_call(
        body, name="gdn_scan_bwd", grid=(nc // G,),
        in_specs=[sp["hd"], sp["hd"], sp["hd"], sp["hd"], sp["hc"], sp["ge"], sp["z"], sp["vec"], sp["st"], sp["oa"]],
        out_specs=(sp["hd"], sp["hd"], sp["hd"], sp["hd"], sp["hc"], sp["ge"], sp["oa"], sp["vec"]),
        out_shape=(_hd_shape(S), _hd_shape(S), _hd_shape(S), _hd_shape(S), _hd_shape(S, C),
                   jax.ShapeDtypeStruct((nc, GDN_HEADS, 8, 128), F32), jax.ShapeDtypeStruct((S, GDN_WIDTH), F32),
                   jax.ShapeDtypeStruct((1, 128), F32)),
        scratch_shapes=[pltpu.VMEM((GDN_HEADS, GDN_DIM, GDN_DIM), F32)],
        compiler_params=_cparams(("arbitrary",)),
    )(u_v, w_k, q_dec, k_end, attn, g_end, proj, gnw, states, d_oa)


def _gdn_post(proj, conv_w, gp, tinv, u_v, w_k, d_uv, d_wk, d_qd, d_ke, d_at, d_ge):
    S = proj.shape[0]
    C, G = GDN_CHUNK, GDN_PG
    nc = S // C
    sp = _gdn_pre_specs(S, G)
    pb = _GDN_PASSES["bwd"]

    def body(cur_ref, prev_ref, ba_ref, cw_ref, gp_ref, ti_ref, uv_ref, wk_ref, duv_ref, dwk_ref, dqd_ref, dke_ref,
             dat_ref, dge_ref, dpre_ref, dba_ref, dgp_ref):
        i = pl.program_id(0)

        @pl.when(i == 0)
        def _():
            dgp_ref[...] = jnp.zeros_like(dgp_ref)

        prev = prev_ref[...] * jnp.where(i == 0, 0.0, 1.0)
        pre = _conv_taps(jnp.concatenate([prev, cur_ref[...]], axis=0), cw_ref[...], GDN_CONV, C * G)
        sg = jax.nn.sigmoid(pre)
        dsilu = sg * (1.0 + pre * (1.0 - sg))
        pairs, args = _gdn_pairs(pre * sg, ba_ref[...], gp_ref[...], G)
        _, vjp1 = jax.vjp(functools.partial(_gdn_stage1, dot=_bdot), *args)

        def take(ref):
            return jnp.stack([ref[h, C * j:C * (j + 1), :] for j, h in pairs])

        t, u_v, w_k = take(ti_ref), take(uv_ref), take(wk_ref)
        d_v = _bdot_raw(t, take(duv_ref), "TN", pb)
        d_rk = _bdot_raw(t, take(dwk_ref), "TN", pb)
        d_l = -(_bdot_raw(d_v, u_v, "NT", pb) + _bdot_raw(d_rk, w_k, "NT", pb))
        d_ge = jnp.stack([dge_ref[j, h][0:1, 0:1] for j, h in pairs])
        dcq, dck, dcv, db, da, dalog, ddtb = vjp1((d_l, d_v, d_rk, take(dqd_ref), take(dat_ref), take(dke_ref), d_ge))
        lane = lax.broadcasted_iota(jnp.int32, (C, 128), 1)
        lane1 = lax.broadcasted_iota(jnp.int32, (1, 128), 1)
        dgp = jnp.zeros((1, 128), F32)
        for j in range(G):
            rows = slice(C * j, C * (j + 1))
            dba = jnp.zeros((C, 128), F32)
            for h in range(GDN_HEADS):
                b = GDN_HEADS * j + h
                for o_, dcx in ((0, dcq), (GDN_WIDTH, dck), (2 * GDN_WIDTH, dcv)):
                    cols = slice(o_ + GDN_DIM * h, o_ + GDN_DIM * (h + 1))
                    dpre_ref[rows, cols] = dcx[b] * dsilu[rows, cols]
                dba = dba + jnp.where(lane == h, db[b], 0.0) + jnp.where(lane == GDN_HEADS + h, da[b], 0.0)
                dgp = dgp + jnp.where(lane1 == h, dalog[b], 0.0) + jnp.where(lane1 == GDN_HEADS + h, ddtb[b], 0.0)
            dba_ref[rows, :] = dba
        dgp_ref[0:1, :] += dgp

    T = C * G
    return pl.pallas_call(
        body, name="gdn_post", grid=(nc // G,),
        in_specs=[sp["cur"], sp["prev"], sp["ba"], sp["cw"], sp["vec"], sp["hc"], sp["hd"], sp["hd"], sp["hd"], sp["hd"],
                  sp["hd"], sp["hd"], sp["hc"], sp["ge"]],
        out_specs=(sp["cur"], pl.BlockSpec((T, 128), lambda i: (i, 0)), pl.BlockSpec((8, 128), lambda i: (0, 0))),
        out_shape=(jax.ShapeDtypeStruct((S, 3 * GDN_WIDTH), F32), jax.ShapeDtypeStruct((S, 128), F32),
                   jax.ShapeDtypeStruct((8, 128), F32)),
        compiler_params=_cparams(("arbitrary",)),
    )(proj, proj, proj, conv_w, gp, tinv, u_v, w_k, d_uv, d_wk, d_qd, d_ke, d_at, d_ge)


def _conv_bwd(dpre, x, xcol0, w, K, name, tc):
    S, Cc = dpre.shape
    T = _pick_tile(S, 256)
    nt, ncol = S // T, Cc // tc
    xo = xcol0 // tc

    def body(d_ref, dn_ref, x_ref, xp_ref, w_ref, dx_ref, dw_ref):
        i = pl.program_id(1)
        dn = dn_ref[...] * jnp.where(i == nt - 1, 0.0, 1.0)
        dv = d_ref[...]
        ext_d = jnp.concatenate([dv, dn], axis=0)
        dx_ref[...] = _conv_taps_t(ext_d, w_ref[...], K, T).astype(dx_ref.dtype)
        xp = xp_ref[...] * jnp.where(i == 0, 0.0, 1.0)
        ext_x = jnp.concatenate([xp, x_ref[...]], axis=0)

        @pl.when(i == 0)
        def _():
            dw_ref[...] = jnp.zeros_like(dw_ref)

        for k in range(K):
            dw_ref[k:k + 1, :] += jnp.sum(dv * _shifted(ext_x, (K - 1) - k, 8, T), axis=0, keepdims=True)

    r8 = T // 8
    return pl.pallas_call(
        body, name=name, grid=(ncol, nt),
        in_specs=[pl.BlockSpec((T, tc), lambda j, i: (i, j)),
                  pl.BlockSpec((8, tc), lambda j, i: (jnp.minimum((i + 1) * r8, S // 8 - 1), j)),
                  pl.BlockSpec((T, tc), lambda j, i: (i, j + xo)),
                  pl.BlockSpec((8, tc), lambda j, i: (jnp.maximum(i * r8 - 1, 0), j + xo)),
                  pl.BlockSpec((K, tc), lambda j, i: (0, j))],
        out_specs=(pl.BlockSpec((T, tc), lambda j, i: (i, j)), pl.BlockSpec((K, tc), lambda j, i: (0, j))),
        out_shape=(jax.ShapeDtypeStruct((S, Cc), _MXU), jax.ShapeDtypeStruct((K, Cc), F32)),
        compiler_params=_cparams(("parallel", "arbitrary")),
    )(dpre, dpre, x, x, w)


def _dil_bias(nt, T):
    d = (np.arange(nt)[:, None, None] * T + np.arange(T)[None, None, :] - np.arange(T)[None, :, None])
    cnt = ((d >= 0) & (d <= 128)).astype(np.float64) + ((d >= 0) & (d % 4 == 0) & (d <= 512)) + ((d >= 0) & (d % 16 == 0))
    return jnp.asarray(np.where(cnt > 0, np.log(np.maximum(cnt, 1.0)), -1e30), dtype=F32)


def _attn_fwd(proj, mix):
    S = proj.shape[0]
    T = min(ATT_T, S)
    nt = S // T
    bias = _dil_bias(nt, T)
    scale = DIL_DIM ** -0.5
    npair = DIL_WIDTH // 128
    qb0, kb0, vb0 = P_QKVB // 128, (P_QKVB + DIL_WIDTH) // 128, (P_QKVB + 2 * DIL_WIDTH) // 128

    def body(q_ref, k_ref, v_ref, b_ref, mix_ref, o_ref, lse_ref):
        i = pl.program_id(1)
        qs = (q_ref[...] * scale).astype(_MXU)

        def step(j, carry):
            kt = k_ref[pl.ds(pl.multiple_of(j * T, T), T), :].astype(_MXU)
            vt = v_ref[pl.ds(pl.multiple_of(j * T, T), T), :].astype(_MXU)
            bt = b_ref[i - j]
            out = []
            for hh in range(2):
                m, l, acc = carry[hh]
                sl = slice(hh * DIL_DIM, (hh + 1) * DIL_DIM)
                s = lax.dot_general(kt[:, sl], qs[:, sl], (_NT, ((), ())), preferred_element_type=F32) + bt
                m_new = jnp.maximum(m, jnp.max(s, axis=0, keepdims=True))
                p = jnp.exp(s - m_new)
                a = jnp.exp(m - m_new)
                l = a * l + jnp.sum(p, axis=0, keepdims=True)
                acc = a * acc + lax.dot_general(vt[:, sl], p.astype(_MXU), (_TN, ((), ())), preferred_element_type=F32)
                out.append((m_new, l, acc))
            return tuple(out)

        init = tuple((jnp.full((1, T), -1e30, F32), jnp.zeros((1, T), F32), jnp.zeros((DIL_DIM, T), F32)) for _ in range(2))
        res = lax.fori_loop(0, i + 1, step, init)
        lse_ref[...] = jnp.zeros_like(lse_ref)
        for hh in range(2):
            m, l, acc = res[hh]
            o_ref[:, hh * DIL_DIM:(hh + 1) * DIL_DIM] = (acc / l).T
            lse_ref[hh:hh + 1, :] = m + jnp.log(l)

    return pl.pallas_call(
        body, name="attn_fwd", grid=(npair, nt),
        in_specs=[pl.BlockSpec((T, 128), lambda p, i: (i, qb0 + p)),
                  pl.BlockSpec((S, 128), lambda p, i: (0, kb0 + p)),
                  pl.BlockSpec((S, 128), lambda p, i: (0, vb0 + p)),
                  pl.BlockSpec((nt, T, T), lambda p, i: (0, 0, 0)), pl.BlockSpec(memory_space=pl.ANY)],
        out_specs=(pl.BlockSpec((T, 128), lambda p, i: (i, GDN_WIDTH // 128 + p)),
                   pl.BlockSpec((None, None, 8, T), lambda p, i: (p, i, 0, 0))),
        out_shape=(jax.ShapeDtypeStruct(mix.shape, F32), jax.ShapeDtypeStruct((npair, nt, 8, T), F32)),
        input_output_aliases={4: 0},
        compiler_params=_cparams(("parallel", "parallel")),
    )(proj, proj, proj, bias, mix)


def _attn_bwd(proj, mix, lse, d_mix):
    S = proj.shape[0]
    T = min(ATT_T, S)
    nt = S // T
    bias = _dil_bias(nt, T)
    scale = DIL_DIM ** -0.5
    npair = DIL_WIDTH // 128
    qb0, kb0, vb0 = P_QKVB // 128, (P_QKVB + DIL_WIDTH) // 128, (P_QKVB + 2 * DIL_WIDTH) // 128

    def body(q_ref, k_ref, v_ref, o_ref, lse_ref, do_ref, b_ref, dq_ref, dk_ref, dv_ref, dq_scr):
        j = pl.program_id(1)

        @pl.when(j == 0)
        def _():
            dq_scr[...] = jnp.zeros_like(dq_scr)

        kt = k_ref[...].astype(_MXU)
        vt = v_ref[...].astype(_MXU)
        ones = jnp.ones((8, DIL_DIM), F32)

        def step(i, carry):
            rows = pl.ds(pl.multiple_of(i * T, T), T)
            qs = (q_ref[rows, :] * scale).astype(_MXU)
            dov = do_ref[rows, :]
            prod = dov * o_ref[rows, :]
            lsev = lse_ref[i]
            dob = dov.astype(_MXU)
            bt = b_ref[i - j]
            out = []
            dqs = []
            for hh in range(2):
                dk, dv = carry[hh]
                sl = slice(hh * DIL_DIM, (hh + 1) * DIL_DIM)
                s = lax.dot_general(kt[:, sl], qs[:, sl], (_NT, ((), ())), preferred_element_type=F32) + bt
                p = jnp.exp(s - lsev[hh:hh + 1, :])
                delta = lax.dot_general(ones, prod[:, sl], (_NT, ((), ())), precision=_HI, preferred_element_type=F32)[0:1, :]
                dp = lax.dot_general(vt[:, sl], dob[:, sl], (_NT, ((), ())), preferred_element_type=F32)
                ds = (p * (dp - delta)).astype(_MXU)
                dv = dv + lax.dot_general(p.astype(_MXU), dob[:, sl], (_NN, ((), ())), preferred_element_type=F32)
                dk = dk + lax.dot_general(ds, qs[:, sl], (_NN, ((), ())), preferred_element_type=F32)
                dqs.append(lax.dot_general(ds, kt[:, sl], (_TN, ((), ())), preferred_element_type=F32) * scale)
                out.append((dk, dv))
            dq_scr[rows, :] += jnp.concatenate(dqs, axis=1)
            return tuple(out)

        init = tuple((jnp.zeros((T, DIL_DIM), F32), jnp.zeros((T, DIL_DIM), F32)) for _ in range(2))
        res = lax.fori_loop(j, nt, step, init)
        dk_ref[...] = jnp.concatenate([res[0][0], res[1][0]], axis=1).astype(dk_ref.dtype)
        dv_ref[...] = jnp.concatenate([res[0][1], res[1][1]], axis=1).astype(dv_ref.dtype)

        @pl.when(j == nt - 1)
        def _():
            dq_ref[...] = dq_scr[...].astype(dq_ref.dtype)

    full = lambda c0: pl.BlockSpec((S, 128), lambda p, j: (0, c0 + p))
    tile = lambda c0: pl.BlockSpec((T, 128), lambda p, j: (j, c0 + p))
    out3 = jax.ShapeDtypeStruct((S, DIL_WIDTH), _MXU)
    return pl.pallas_call(
        body, name="attn_bwd", grid=(npair, nt),
        in_specs=[full(qb0), tile(kb0), tile(vb0), full(GDN_WIDTH // 128),
                  pl.BlockSpec((None, nt, 8, T), lambda p, j: (p, 0, 0, 0)), full(GDN_WIDTH // 128),
                  pl.BlockSpec((nt, T, T), lambda p, j: (0, 0, 0))],
        out_specs=(full(0), tile(0), tile(0)),
        out_shape=(out3, out3, out3),
        scratch_shapes=[pltpu.VMEM((S, 128), F32)],
        compiler_params=_cparams(("parallel", "arbitrary")),
    )(proj, proj, proj, mix, lse, d_mix, bias)


def _ffn_act(up, cw):
    S, Cc = up.shape[0], up.shape[1] // 2
    T, tc = _pick_tile(S, 256), _pick_tile(Cc, 1536)
    r16 = T // 16
    nct = Cc // tc

    def body(g_ref, gp_ref, u_ref, up_ref, wg_ref, wu_ref, o_ref):
        keep = jnp.where(pl.program_id(1) == 0, 0.0, 1.0)
        cg = _conv_taps(jnp.concatenate([gp_ref[8:16, :].astype(F32) * keep, g_ref[...].astype(F32)], axis=0),
                        wg_ref[...], FFN_CONV, T)
        cu = _conv_taps(jnp.concatenate([up_ref[8:16, :].astype(F32) * keep, u_ref[...].astype(F32)], axis=0),
                        wu_ref[...], FFN_CONV, T)
        o_ref[...] = (_silu(cg) * cu).astype(o_ref.dtype)

    cur = lambda o: pl.BlockSpec((T, tc), lambda j, i: (i, j + o))
    prev = lambda o: pl.BlockSpec((16, tc), lambda j, i: (jnp.maximum(i * r16 - 1, 0), j + o))
    wsp = lambda o: pl.BlockSpec((FFN_CONV, tc), lambda j, i: (0, j + o))
    return pl.pallas_call(
        body, name="ffn_act", grid=(nct, S // T),
        in_specs=[cur(0), prev(0), cur(nct), prev(nct), wsp(0), wsp(nct)], out_specs=cur(0),
        out_shape=jax.ShapeDtypeStruct((S, Cc), _MXU),
        compiler_params=_cparams(("parallel", "parallel")),
    )(up, up, up, up, cw, cw)


def _ffn_act_bwd(d_act, up, cw):
    S, Cc = up.shape[0], up.shape[1] // 2
    T, tc = _pick_tile(S, 256), _pick_tile(Cc, 1536)
    r8, r16 = T // 8, T // 16
    nt = S // T
    nct = Cc // tc
    K = FFN_CONV

    def body(da_ref, dan_ref, g_ref, gp_ref, gn_ref, u_ref, up_ref, un_ref, wg_ref, wu_ref,
             dg_ref, du_ref, dwg_ref, dwu_ref):
        i = pl.program_id(1)
        keep_p = jnp.where(i == 0, 0.0, 1.0)
        keep_n = jnp.where(i == nt - 1, 0.0, 1.0)
        wg, wu = wg_ref[...], wu_ref[...]
        xg = jnp.concatenate([gp_ref[8:16, :].astype(F32) * keep_p, g_ref[...].astype(F32),
                              gn_ref[0:8, :].astype(F32) * keep_n], axis=0)
        xu = jnp.concatenate([up_ref[8:16, :].astype(F32) * keep_p, u_ref[...].astype(F32),
                              un_ref[0:8, :].astype(F32) * keep_n], axis=0)
        cg = _conv_taps(xg, wg, K, T + 8)
        cu = _conv_taps(xu, wu, K, T + 8)
        da = jnp.concatenate([da_ref[...], dan_ref[...] * keep_n], axis=0)
        sg = jax.nn.sigmoid(cg)
        d_cg = da * cu * (sg * (1.0 + cg * (1.0 - sg)))
        d_cu = da * (cg * sg)
        dg_ref[...] = _conv_taps_t(d_cg, wg, K, T).astype(dg_ref.dtype)
        du_ref[...] = _conv_taps_t(d_cu, wu, K, T).astype(du_ref.dtype)

        @pl.when(i == 0)
        def _():
            dwg_ref[...] = jnp.zeros_like(dwg_ref)
            dwu_ref[...] = jnp.zeros_like(dwu_ref)

        for k in range(K):
            dwg_ref[k:k + 1, :] += jnp.sum(d_cg[0:T, :] * _shifted(xg, (K - 1) - k, 8, T), axis=0, keepdims=True)
            dwu_ref[k:k + 1, :] += jnp.sum(d_cu[0:T, :] * _shifted(xu, (K - 1) - k, 8, T), axis=0, keepdims=True)

    cur = lambda o: pl.BlockSpec((T, tc), lambda j, i: (i, j + o))
    prev = lambda o: pl.BlockSpec((16, tc), lambda j, i: (jnp.maximum(i * r16 - 1, 0), j + o))
    nxt = lambda o: pl.BlockSpec((16, tc), lambda j, i: (jnp.minimum((i + 1) * r16, S // 16 - 1), j + o))
    nxt8 = pl.BlockSpec((8, tc), lambda j, i: (jnp.minimum((i + 1) * r8, S // 8 - 1), j))
    wsp = lambda o: pl.BlockSpec((K, tc), lambda j, i: (0, j + o))
    return pl.pallas_call(
        body, name="ffn_act_bwd", grid=(nct, nt),
        in_specs=[cur(0), nxt8, cur(0), prev(0), nxt(0), cur(nct), prev(nct), nxt(nct), wsp(0), wsp(nct)],
        out_specs=(cur(0), cur(0), wsp(0), wsp(0)),
        out_shape=(jax.ShapeDtypeStruct((S, Cc), _MXU), jax.ShapeDtypeStruct((S, Cc), _MXU),
                   jax.ShapeDtypeStruct((K, Cc), F32), jax.ShapeDtypeStruct((K, Cc), F32)),
        compiler_params=_cparams(("parallel", "arbitrary")),
    )(d_act, d_act, up, up, up, up, up, up, cw, cw)


def _local_step(x, tgt, h1, n1w, n2w, fnw, gp, gnw, wp, conv_w, fcw, rest_weights, early_grads):
    proj = _mm(h1, wp, "nn", name="proj")
    u_v, w_k, q_dec, k_end, attn, tinv, g_end = _gdn_pre(proj, conv_w, gp)
    mix, states = _gdn_scan(u_v, w_k, q_dec, k_end, attn, g_end, proj, gnw)
    mix, lse = _attn_fwd(proj, mix)
    w_out, w_up4, w_down = rest_weights([mix])
    x2 = _mm(mix, w_out, "nn", residual=x, name="outproj")
    h2 = _rmsnorm_fwd(x2, n2w, "norm2")
    up = _mm(h2, w_up4, "nn", b_blocks=True, out_dtype=_MXU, name="up")
    act = _ffn_act(up, fcw)
    x3 = _mm(act, w_down, "nn", residual=x2, name="down")
    loss, dx3, dx3n, d_fnw = _loss_head(x3, fnw, tgt, "loss_head")
    d_act = _mm(dx3n, w_down, "nt", name="d_act")
    d_wdown = _mm(act, dx3n, "tn", name="d_wdown")
    d_upg, d_upu, d_fcwg, d_fcwu = _ffn_act_bwd(d_act, up, fcw)
    d_wup = _mm(h2, d_upg, "tn", place=("blocks", N_CHIPS, 0), tn=w_up4.shape[2], name="d_wgate")
    d_wup = _mm(h2, d_upu, "tn", place=("blocks", N_CHIPS, N_CHIPS // 2), tn=w_up4.shape[2], into=d_wup, name="d_wup")
    token = early_grads[0](d_wup, d_wdown)
    d_h2 = _mm_nt_blocks([d_upg, d_upu], w_up4, "d_h2")
    dx2, d_n2w = _rmsnorm_bwd(d_h2, x2, n2w + token[0:1, 0:1], dx3, "norm2_bwd")
    token = early_grads[1](dx2)
    d_mix = _mm(dx2, w_out, "nt", name="d_mix")
    d_wout = _mm(mix, dx2, "tn", name="d_wout")
    dq_b, dk_b, dv_b = _attn_bwd(proj, mix, lse, d_mix)
    d_uv, d_wk, d_qd, d_ke, d_at, d_ge, d_z, d_gnw = _gdn_scan_bwd(u_v, w_k, q_dec, k_end, attn, g_end, proj,
                                                                   gnw + token[0:1, 0:1], states, d_mix)
    d_pre, d_ba, d_gp = _gdn_post(proj, conv_w, gp, tinv, u_v, w_k, d_uv, d_wk, d_qd, d_ke, d_at, d_ge)
    d_qkva, d_convw = _conv_bwd(d_pre, proj, 0, conv_w, GDN_CONV, "gdn_conv_bwd", 512)
    d_proj = jnp.concatenate([d_qkva, d_z.astype(_MXU), dq_b, dk_b, dv_b, d_ba.astype(_MXU),
                              jnp.zeros((x.shape[0], P_COLS - P_BA - 128), _MXU)], axis=1)
    d_wp = _mm(h1, d_proj, "tn", name="d_wp")
    token = early_grads[2](d_wp, d_wout)
    d_h1 = _mm(d_proj, wp, "nt", name="d_h1")
    dx, d_n1w = _rmsnorm_bwd(d_h1, x, n1w + token[0:1, 0:1], dx2, "norm1_bwd")
    grads = dict(wp=d_wp, conv_w=d_convw, w_out=d_wout, w_up=d_wup, fcw_g=d_fcwg, fcw_u=d_fcwu, w_down=d_wdown,
                 n1w=d_n1w, n2w=d_n2w, fnw=d_fnw, gp=d_gp, gnw=d_gnw)
    return loss, dx, grads


_HBM = pl.BlockSpec(memory_space=pltpu.HBM)


def _pos():
    return lax.axis_index("x"), lax.axis_index("y"), lax.axis_index("c")


def _other_chips(x, y):
    return [(1 - x, y), (x, 1 - y), (1 - x, 1 - y)]


def _halvable(shape):
    return shape[0] % 32 == 0


def _rows_of_half(shape, half):
    if not _halvable(shape):
        return pl.ds(0, shape[0])
    return pl.ds(pl.multiple_of(half * (shape[0] // 2), 16), shape[0] // 2)


_SEM = pl.BlockSpec(memory_space=pltpu.SEMAPHORE)
_ANY = pl.BlockSpec(memory_space=pl.ANY)
_DATAFLOW = pltpu.SideEffectType.DATAFLOW_SIDE_EFFECTING


def _in_hbm(a):
    return pltpu.with_memory_space_constraint(a, pltpu.HBM)


def _halves_copy(src_refs, land_refs, send_sems, recv_sems, shapes, a, j, block, x, y, c):
    px, py = _other_chips(x, y)[j]
    rows = _rows_of_half(shapes[a], c)
    return pltpu.make_async_remote_copy(
        src_ref=src_refs[a].at[rows, :], dst_ref=land_refs[a].at[block, rows, :], send_sem=send_sems.at[3 * a + j],
        recv_sem=recv_sems.at[3 * a + j], device_id=(px, py, c), device_id_type=MESH)


def _gather_halves_start(shards, after, name):
    n = len(shards)
    shapes = [s.shape for s in shards]

    def body(*refs):
        ins, lands = refs[:n], refs[n:2 * n]
        send_sems, recv_sems = refs[2 * n + 1], refs[2 * n + 2]
        token = refs[-1]
        x, y, c = _pos()
        q = 2 * x + y
        for a in range(n):
            for j in range(3):
                _halves_copy(ins, lands, send_sems, recv_sems, shapes, a, j, q, x, y, c).start()
        token[...] = jnp.zeros_like(token)

    land_shapes = [(N_CHIPS,) + s.shape for s in shards]
    return pl.pallas_call(
        body, name=name,
        out_shape=(pltpu.SemaphoreType.DMA((3 * n,)), pltpu.SemaphoreType.DMA((3 * n,)),
                   *[pltpu.HBM(s.shape, s.dtype) for s in shards],
                   *[pltpu.HBM(ls, s.dtype) for ls, s in zip(land_shapes, shards)],
                   jax.ShapeDtypeStruct((8, 128), F32)),
        in_specs=[_HBM] * (2 * n) + [_ANY],
        out_specs=(_SEM, _SEM, *[_HBM] * (2 * n), pl.BlockSpec(memory_space=pltpu.VMEM)),
        input_output_aliases={a: 2 + a for a in range(2 * n)},
        compiler_params=pltpu.CompilerParams(has_side_effects=_DATAFLOW),
    )(*[_in_hbm(s) for s in shards], *[_in_hbm(lax.empty(ls, s.dtype)) for ls, s in zip(land_shapes, shards)], after)


def _gather_halves_wait(started, after, name):
    send_sems, recv_sems, *thru = started
    n = len(thru) // 2
    shapes = [t.shape for t in thru[:n]]

    def body(*refs):
        ins, lands = refs[:n], refs[n:2 * n]
        send_sems, recv_sems = refs[2 * n], refs[2 * n + 1]
        x, y, c = _pos()
        q = 2 * x + y
        chips = _other_chips(x, y)
        for a in range(n):
            for j, (px, py) in enumerate(chips):
                _halves_copy(ins, lands, send_sems, recv_sems, shapes, a, j, q, x, y, c).wait_send()
                _halves_copy(ins, lands, send_sems, recv_sems, shapes, a, j, 2 * px + py, x, y, c).wait_recv()

    outs = pl.pallas_call(
        body, name=name, out_shape=[pltpu.HBM(t.shape, t.dtype) for t in thru],
        in_specs=[_HBM] * (2 * n) + [_SEM, _SEM] + [_ANY] * len(after), out_specs=[_HBM] * (2 * n),
        input_output_aliases={a: a for a in range(2 * n)},
        compiler_params=pltpu.CompilerParams(has_side_effects=_DATAFLOW),
    )(*thru, send_sems, recv_sems, *after)
    return outs[:n], outs[n:]


def _sibling_fill(gathered, name):
    big = [a for a, g in enumerate(gathered) if _halvable(g.shape[1:])]
    n = len(gathered)

    def body(*refs):
        ins, outs = refs[:n], refs[n:2 * n]
        send_sems, recv_sems = refs[2 * n:]
        x, y, c = _pos()
        chips = _other_chips(x, y)

        def copy(k, j, half):
            a = big[k]
            px, py = chips[j]
            rows = _rows_of_half(gathered[a].shape[1:], half)
            return pltpu.make_async_remote_copy(
                src_ref=ins[a].at[2 * px + py, rows, :], dst_ref=outs[a].at[2 * px + py, rows, :],
                send_sem=send_sems.at[3 * k + j], recv_sem=recv_sems.at[3 * k + j],
                device_id=(x, y, 1 - c), device_id_type=MESH)

        sends = [copy(k, j, c) for k in range(len(big)) for j in range(3)]
        for cp in sends:
            cp.start()
        for k in range(len(big)):
            for j in range(3):
                copy(k, j, 1 - c).wait_recv()
        for cp in sends:
            cp.wait_send()

    return pl.pallas_call(
        body, name=name, in_specs=[_HBM] * n, out_specs=[_HBM] * n,
        out_shape=[jax.ShapeDtypeStruct(g.shape, g.dtype) for g in gathered],
        input_output_aliases={a: a for a in range(n)},
        scratch_shapes=[pltpu.SemaphoreType.DMA((3 * len(big),)), pltpu.SemaphoreType.DMA((3 * len(big),))],
    )(*gathered)


def _place_own(shards, gathered, cq, name):
    n = len(shards)
    steps = 4

    def body(cq_ref, *refs):
        for a in range(n):
            refs[2 * n + a][...] = refs[a][...]

    def tile(shape):
        return shape[0] // steps if _halvable(shape) else shape[0]

    in_specs = [pl.BlockSpec((tile(s.shape), s.shape[1]), (lambda i, s_: (i, 0)) if _halvable(s.shape) else (lambda i, s_: (0, 0)))
                for s in shards]
    in_specs += [pl.BlockSpec(memory_space=pl.ANY)] * n
    out_specs = [pl.BlockSpec((None, tile(s.shape), s.shape[1]),
                              (lambda i, s_: (s_[1], i, 0)) if _halvable(s.shape) else (lambda i, s_: (s_[1], 0, 0)))
                 for s in shards]
    gs = pltpu.PrefetchScalarGridSpec(num_scalar_prefetch=1, grid=(steps,), in_specs=in_specs, out_specs=out_specs)
    return pl.pallas_call(
        body, name=name, grid_spec=gs, out_shape=[jax.ShapeDtypeStruct(g.shape, g.dtype) for g in gathered],
        input_output_aliases={1 + n + a: a for a in range(n)},
        compiler_params=_cparams(("arbitrary",)),
    )(cq, *shards, *gathered)


def _half_rows(ref, c, rh):
    return ref.at[:, pl.ds(pl.multiple_of(c * rh, 8), rh), :]


def _chips_copy(src_refs, land_refs, send_sems, recv_sems, a, j, x, y, c):
    px, py = _other_chips(x, y)[j]
    return pltpu.make_async_remote_copy(src_ref=src_refs[a].at[2 * px + py], dst_ref=land_refs[a].at[j],
                                        send_sem=send_sems.at[3 * a + j], recv_sem=recv_sems.at[3 * a + j],
                                        device_id=(px, py, c), device_id_type=MESH)


def _grad_chips_start(parts, name):
    n = len(parts)

    def body(*refs):
        ins, lands = refs[:n], refs[n:2 * n]
        send_sems, recv_sems = refs[2 * n], refs[2 * n + 1]
        token = refs[-1]
        x, y, c = _pos()
        for a in range(n):
            for j in range(3):
                _chips_copy(ins, lands, send_sems, recv_sems, a, j, x, y, c).start()
        token[...] = jnp.zeros_like(token)

    land_shapes = [(3,) + p.shape[1:] for p in parts]
    return pl.pallas_call(
        body, name=name,
        out_shape=(pltpu.SemaphoreType.DMA((3 * n,)), pltpu.SemaphoreType.DMA((3 * n,)),
                   *[pltpu.HBM(p.shape, p.dtype) for p in parts],
                   *[pltpu.HBM(ls, p.dtype) for ls, p in zip(land_shapes, parts)],
                   jax.ShapeDtypeStruct((8, 128), F32)),
        in_specs=[_HBM] * (2 * n),
        out_specs=(_SEM, _SEM, *[_HBM] * (2 * n), pl.BlockSpec(memory_space=pltpu.VMEM)),
        input_output_aliases={a: 2 + a for a in range(2 * n)},
        compiler_params=pltpu.CompilerParams(has_side_effects=_DATAFLOW),
    )(*[_in_hbm(p) for p in parts], *[_in_hbm(lax.empty(ls, p.dtype)) for ls, p in zip(land_shapes, parts)])


def _grad_chips_wait(started, after, name):
    send_sems, recv_sems, *thru = started
    n = len(thru) // 2

    def body(*refs):
        ins, lands = refs[:n], refs[n:2 * n]
        send_sems, recv_sems = refs[2 * n], refs[2 * n + 1]
        x, y, c = _pos()
        for a in range(n):
            for j in range(3):
                cp = _chips_copy(ins, lands, send_sems, recv_sems, a, j, x, y, c)
                cp.wait_send()
                cp.wait_recv()

    outs = pl.pallas_call(
        body, name=name, out_shape=[pltpu.HBM(t.shape, t.dtype) for t in thru],
        in_specs=[_HBM] * (2 * n) + [_SEM, _SEM] + [_ANY] * len(after), out_specs=[_HBM] * (2 * n),
        input_output_aliases={a: a for a in range(2 * n)},
        compiler_params=pltpu.CompilerParams(has_side_effects=_DATAFLOW),
    )(*thru, send_sems, recv_sems, *after)
    return outs[n:]


def _sibling_copy(src_refs, land_refs, send_sems, recv_sems, rhs, a, c, x, y):
    return pltpu.make_async_remote_copy(src_ref=_half_rows(src_refs[a], 1 - c, rhs[a]), dst_ref=land_refs[a],
                                        send_sem=send_sems.at[a], recv_sem=recv_sems.at[a],
                                        device_id=(x, y, 1 - c), device_id_type=MESH)


def _grad_sibling_start(fams, name):
    n = len(fams)
    rhs = [f.shape[1] // 2 for f in fams]

    def body(*refs):
        ins, lands = refs[:n], refs[n:2 * n]
        send_sems, recv_sems = refs[2 * n], refs[2 * n + 1]
        token = refs[-1]
        x, y, c = _pos()
        for a in range(n):
            _sibling_copy(ins, lands, send_sems, recv_sems, rhs, a, c, x, y).start()
        token[...] = jnp.zeros_like(token)

    land_shapes = [(f.shape[0], f.shape[1] // 2, f.shape[2]) for f in fams]
    return pl.pallas_call(
        body, name=name,
        out_shape=(pltpu.SemaphoreType.DMA((n,)), pltpu.SemaphoreType.DMA((n,)),
                   *[pltpu.HBM(f.shape, f.dtype) for f in fams],
                   *[pltpu.HBM(ls, f.dtype) for ls, f in zip(land_shapes, fams)],
                   jax.ShapeDtypeStruct((8, 128), F32)),
        in_specs=[_HBM] * (2 * n),
        out_specs=(_SEM, _SEM, *[_HBM] * (2 * n), pl.BlockSpec(memory_space=pltpu.VMEM)),
        input_output_aliases={a: 2 + a for a in range(2 * n)},
        compiler_params=pltpu.CompilerParams(has_side_effects=_DATAFLOW),
    )(*[_in_hbm(f) for f in fams], *[_in_hbm(lax.empty(ls, f.dtype)) for ls, f in zip(land_shapes, fams)])


def _grad_sibling_wait(started, after, name):
    send_sems, recv_sems, *thru = started
    n = len(thru) // 2
    rhs = [t.shape[1] // 2 for t in thru[:n]]

    def body(*refs):
        ins, lands = refs[:n], refs[n:2 * n]
        send_sems, recv_sems = refs[2 * n], refs[2 * n + 1]
        x, y, c = _pos()
        for a in range(n):
            cp = _sibling_copy(ins, lands, send_sems, recv_sems, rhs, a, c, x, y)
            cp.wait_send()
            cp.wait_recv()

    outs = pl.pallas_call(
        body, name=name, out_shape=[pltpu.HBM(t.shape, t.dtype) for t in thru],
        in_specs=[_HBM] * (2 * n) + [_SEM, _SEM] + [_ANY] * len(after), out_specs=[_HBM] * (2 * n),
        input_output_aliases={a: a for a in range(2 * n)},
        compiler_params=pltpu.CompilerParams(has_side_effects=_DATAFLOW),
    )(*thru, send_sems, recv_sems, *after)
    return outs[:n], outs[n:]


def _grad_share(fulls, name, small=None):
    n = len(fulls)
    ns = 0 if small is None else 1
    rhs = [f.shape[0] // 2 for f in fulls]

    def body(*refs):
        ins, outs = refs[:n], refs[n + ns:2 * n + ns]
        send_sems, recv_sems = refs[2 * (n + ns)], refs[2 * (n + ns) + 1]
        x, y, c = _pos()

        def copy(a, half):
            rows = pl.ds(pl.multiple_of(half * rhs[a], 8), rhs[a])
            return pltpu.make_async_remote_copy(src_ref=ins[a].at[rows, :], dst_ref=outs[a].at[rows, :],
                                                send_sem=send_sems.at[7 * ns + a], recv_sem=recv_sems.at[7 * ns + a],
                                                device_id=(x, y, 1 - c), device_id_type=MESH)

        sends = [copy(a, c) for a in range(n)]
        for cp in sends:
            cp.start()
        if ns:
            small_ref, all_ref = refs[n], refs[2 * n + 1]
            me = 4 * x + 2 * y + c

            def peer(r):
                dx, dy, dc = (r >> 2) & 1, (r >> 1) & 1, r & 1
                return (x if dx == 0 else 1 - x), (y if dy == 0 else 1 - y), (c if dc == 0 else 1 - c)

            def small_copy(r, slot):
                return pltpu.make_async_remote_copy(src_ref=small_ref, dst_ref=all_ref.at[slot], send_sem=send_sems.at[r - 1],
                                                    recv_sem=recv_sems.at[r - 1], device_id=peer(r), device_id_type=MESH)

            smalls = [small_copy(r, me) for r in range(1, 8)]
            for cp in smalls:
                cp.start()
            for r in range(1, 8):
                px, py, pc = peer(r)
                small_copy(r, 4 * px + 2 * py + pc).wait_recv()
            sends = sends + smalls
        for a in range(n):
            copy(a, 1 - c).wait_recv()
        for cp in sends:
            cp.wait_send()

    return pl.pallas_call(
        body, name=name, in_specs=[_HBM] * (n + ns), out_specs=[_HBM] * (n + ns),
        out_shape=[jax.ShapeDtypeStruct(f.shape, f.dtype) for f in fulls]
        + ([jax.ShapeDtypeStruct((8,) + small.shape, small.dtype)] if ns else []),
        input_output_aliases={a: a for a in range(n)},
        scratch_shapes=[pltpu.SemaphoreType.DMA((7 * ns + n,)), pltpu.SemaphoreType.DMA((7 * ns + n,))],
    )(*fulls, *([small] if ns else []))


def _add_sibling(own, recv, cq, name):
    nb, R, Cc = own.shape
    Rh = R // 2

    def body(cq_ref, a_ref, b_ref, o32_ref, o16_ref):
        s = a_ref[...] + b_ref[...]
        o32_ref[...] = s
        o16_ref[...] = s.astype(o16_ref.dtype)

    sp = pl.BlockSpec((1, Rh, Cc), lambda b, s: (b, 0, 0))
    gs = pltpu.PrefetchScalarGridSpec(
        num_scalar_prefetch=1, grid=(nb,),
        in_specs=[pl.BlockSpec((1, Rh, Cc), lambda b, s: (b, s[0], 0)), sp], out_specs=[sp, sp])
    return pl.pallas_call(
        body, name=name, grid_spec=gs,
        out_shape=[jax.ShapeDtypeStruct((nb, Rh, Cc), F32), jax.ShapeDtypeStruct((nb, Rh, Cc), _MXU)],
        compiler_params=_cparams(("parallel",)),
    )(cq, own, recv)


def _add_chips(part32, recv3, cq, name):
    nb, Rh, Cc = part32.shape

    def body(cq_ref, a_ref, b_ref, o_ref):
        acc = a_ref[0]
        for j in range(3):
            acc = acc + b_ref[j].astype(F32)
        o_ref[...] = acc

    gs = pltpu.PrefetchScalarGridSpec(
        num_scalar_prefetch=1, grid=(1,),
        in_specs=[pl.BlockSpec((1, Rh, Cc), lambda i, s: (s[1], 0, 0)), pl.BlockSpec((3, Rh, Cc), lambda i, s: (0, 0, 0))],
        out_specs=pl.BlockSpec((Rh, Cc), lambda i, s: (s[0], 0)))
    return pl.pallas_call(
        body, name=name, grid_spec=gs, out_shape=jax.ShapeDtypeStruct((2 * Rh, Cc), F32),
        compiler_params=_cparams(("arbitrary",)),
    )(cq, part32, recv3)


def _sum_devices(small_all, small, me):
    def body(me_ref, all_ref, own_ref, o_ref):
        tot = None
        for d in range(8):
            term = jnp.where(me_ref[0] == d, own_ref[...], all_ref[d])
            tot = term if tot is None else tot + term
        o_ref[...] = tot

    gs = pltpu.PrefetchScalarGridSpec(
        num_scalar_prefetch=1, grid=(1,),
        in_specs=[pl.BlockSpec(small_all.shape, lambda i, s: (0, 0, 0)), pl.BlockSpec(small.shape, lambda i, s: (0, 0))],
        out_specs=pl.BlockSpec(small.shape, lambda i, s: (0, 0)))
    return pl.pallas_call(body, name="sum_devices", grid_spec=gs,
                          out_shape=jax.ShapeDtypeStruct(small.shape, F32))(me, small_all, small)


def _adamw(w, g, m, v, name):
    R, Cc = w.shape
    T = max([t for t in range(8, 257, 8) if R % t == 0], default=R)
    c1 = 1.0 / (1.0 - ADAM_B1 ** ADAM_STEP)
    c2 = 1.0 / (1.0 - ADAM_B2 ** ADAM_STEP)

    def body(w_ref, g_ref, m_ref, v_ref, d_ref, mo_ref, vo_ref):
        gv = g_ref[...]
        mn = ADAM_B1 * m_ref[...] + (1.0 - ADAM_B1) * gv
        vn = ADAM_B2 * v_ref[...] + (1.0 - ADAM_B2) * (gv * gv)
        mo_ref[...] = mn
        vo_ref[...] = vn
        d_ref[...] = -ADAM_LR * ((mn * c1) / (jnp.sqrt(vn * c2) + ADAM_EPS) + ADAM_WD * w_ref[...])

    sp = pl.BlockSpec((T, Cc), lambda i: (i, 0))
    sh = jax.ShapeDtypeStruct((R, Cc), F32)
    return pl.pallas_call(
        body, name=name, grid=(R // T,), in_specs=[sp] * 4, out_specs=(sp, sp, sp), out_shape=(sh, sh, sh),
        compiler_params=_cparams(("parallel",)),
    )(w, g, m, v)


SMALL_ROWS = 32
REPL_ROWS = 8


def _pad_lanes(v, n=D_MODEL):
    return jnp.pad(v, ((0, 0), (0, n - v.shape[1])))


def kernel(x, norm1_w, w_in, conv_qkv_w, a_log, dt_bias, gdn_norm_w, w_out, norm2_w, w_up, ffn_conv_w, w_down, final_norm_w, loss_target, m_norm1_w, m_w_in, m_conv_qkv_w, m_a_log, m_dt_bias, m_gdn_norm_w, m_w_out, m_norm2_w, m_w_up, m_ffn_conv_w, m_w_down, m_final_norm_w, v_norm1_w, v_w_in, v_conv_qkv_w, v_a_log, v_dt_bias, v_gdn_norm_w, v_w_out, v_norm2_w, v_w_up, v_ffn_conv_w, v_w_down, v_final_norm_w):
    c = lax.axis_index("c")
    q = 2 * lax.axis_index("x") + lax.axis_index("y")
    S = x.shape[1]
    cq = jnp.stack([c, q]).astype(jnp.int32)

    *in_started, in_token = _gather_halves_start([w_in[0].astype(_MXU), conv_qkv_w[0], ffn_conv_w[0]], x, "gather_in_start")
    never = in_token[0:1, 0:1] != 0.0
    w_in_l, m_w_in_l, v_w_in_l = (jnp.where(never, b, a) for a, b in ((w_in, m_w_in), (m_w_in, v_w_in), (v_w_in, w_in)))
    h1 = _rmsnorm_fwd(x[0], norm1_w + in_token[0:1, 0:1], "norm1")
    rest = [(a[0] + in_token[0:1, 0:1]).astype(_MXU) for a in (w_out, w_up, w_down)]
    in_shards, got_in = _gather_halves_wait(in_started, [w_in_l, m_w_in_l, v_w_in_l, h1, *rest], "gather_in_wait")
    g_in, g_conv, g_fconv = _place_own(in_shards, _sibling_fill(got_in, "fill_in"), cq, "place_in")
    *rest_started, token = _gather_halves_start(rest, g_conv, "gather_rest_start")

    def rest_weights(after):
        shards, got = _gather_halves_wait(rest_started, after, "gather_rest_wait")
        got = _sibling_fill(got, "fill_rest")
        g_out, g_up, g_down = _place_own(shards, got, cq, "place_rest")
        return g_out.reshape(D_MODEL, D_MODEL), g_up, g_down.reshape(D_FF, D_MODEL)
    wp = _wp_assemble(g_in, [token])
    conv_f = jnp.concatenate([g_conv[i] for i in range(N_CHIPS)], axis=1)
    fcw = jnp.concatenate([g_fconv[i] for i in range(N_CHIPS)], axis=1)
    gp = _pad_lanes(jnp.concatenate([a_log, dt_bias], axis=1), 128)
    fnw = final_norm_w[None, :]
    early = {}

    def early_sibling(d_wup, d_wdown):
        *early["sibling"], tok = _grad_sibling_start([d_wup, d_wdown.reshape(N_CHIPS, D_FF // N_CHIPS, D_MODEL)],
                                                     "grad_sibling_early_start")
        return tok

    def early_chips(dx2):
        fams_e, got_e = _grad_sibling_wait(early["sibling"], [dx2], "grad_sibling_early_wait")
        early["parts"] = [_add_sibling(f, r, cq, "add_sibling_" + nm) for f, r, nm in zip(fams_e, got_e, ("w_up", "w_down"))]
        *early["started"], tok = _grad_chips_start([p[1] for p in early["parts"]], "grad_chips_start")
        return tok

    def late_sibling(d_wp, d_wout):
        *early["late_sibling"], tok = _grad_sibling_start(
            [_win_split(d_wp), d_wout.reshape(N_CHIPS, D_MODEL // N_CHIPS, D_MODEL)], "grad_sibling_late_start")
        return tok

    early_grads = (early_sibling, early_chips, late_sibling)

    loss_l, dx, g = _local_step(x[0], loss_target[0], h1, norm1_w, norm2_w + token[0:1, 0:1], fnw, gp, gdn_norm_w, wp,
                                conv_f, fcw, rest_weights, early_grads)
    n_fc = FFN_CONV * D_FF

    def rows_of(v):
        flat = v.reshape(-1)
        return jnp.pad(flat, (0, -flat.shape[0] % D_MODEL)).reshape(-1, D_MODEL)

    gp_row = _pad_lanes(jnp.concatenate([g["gp"][0:1, 0:8], loss_l[0:1, 0:1]], axis=1))
    small = jnp.concatenate([g["n1w"], g["n2w"], g["fnw"], gp_row, _pad_lanes(g["gnw"]),
                             rows_of(g["conv_w"]), rows_of(g["fcw_g"]), rows_of(g["fcw_u"])], axis=0)
    small = jnp.pad(small, ((0, SMALL_ROWS - small.shape[0]), (0, 0)))
    fams, got = _grad_sibling_wait(early["late_sibling"], [dx], "grad_sibling_late_wait")
    parts = [_add_sibling(f, r, cq, "add_sibling_" + nm) for f, r, nm in zip(fams, got, ("w_in", "w_out"))]
    *late_started, late_token = _grad_chips_start([p[1] for p in parts], "grad_chips_late_start")
    got3_e = _grad_chips_wait(early["started"], [dx, g["wp"], late_token], "grad_chips_wait")
    g_w_up, g_w_down = _grad_share(
        [_add_chips(p[0], r3, cq, "add_chips_" + nm) for p, r3, nm in zip(early["parts"], got3_e, ("w_up", "w_down"))],
        "grad_share_early")
    big = {}

    def adamw_big(nm, w, gg, m, v):
        d_, m_, v_ = _adamw(w[0], gg, m[0], v[0], "adamw_" + nm)
        big[nm] = (gg[None], d_[None], m_[None], v_[None])

    adamw_big("w_up", w_up, g_w_up, m_w_up, v_w_up)
    adamw_big("w_down", w_down, g_w_down, m_w_down, v_w_down)
    got3 = _grad_chips_wait(late_started, [big["w_up"][1], big["w_down"][1]], "grad_chips_late_wait")
    g_w_in, g_w_out, small_all = _grad_share(
        [_add_chips(p[0], r3, cq, "add_chips_" + nm) for p, r3, nm in zip(parts, got3, ("w_in", "w_out"))],
        "grad_share_late", small)
    small_red = _sum_devices(small_all, small, (2 * q + c).astype(jnp.int32).reshape(1))
    loss = small_red[3, 8]
    r0 = 5
    r1 = r0 + GDN_CONV * 3 * GDN_WIDTH // D_MODEL
    r2 = r1 + -(-n_fc // D_MODEL)
    conv_red = small_red[r0:r1].reshape(GDN_CONV, 3 * GDN_WIDTH)
    fc_red = jnp.concatenate([small_red[r1:r2].reshape(-1)[:n_fc].reshape(FFN_CONV, D_FF),
                              small_red[r2:2 * r2 - r1].reshape(-1)[:n_fc].reshape(FFN_CONV, D_FF)], axis=1)
    g_conv_w = lax.dynamic_slice_in_dim(conv_red, q * (3 * GDN_WIDTH // N_CHIPS), 3 * GDN_WIDTH // N_CHIPS, axis=1)
    g_fconv_w = lax.dynamic_slice_in_dim(fc_red, q * (2 * D_FF // N_CHIPS), 2 * D_FF // N_CHIPS, axis=1)
    g_n1w, g_n2w, g_fnw = small_red[0:1], small_red[1:2], small_red[2]
    g_alog, g_dtb, g_gnw = small_red[3:4, 0:4], small_red[3:4, 4:8], small_red[4:5, 0:128]
    for nm, w, gg, m, v in (("w_in", w_in_l, g_w_in, m_w_in_l, v_w_in_l), ("conv_qkv_w", conv_qkv_w, g_conv_w, m_conv_qkv_w, v_conv_qkv_w),
                            ("w_out", w_out, g_w_out, m_w_out, v_w_out),
                            ("ffn_conv_w", ffn_conv_w, g_fconv_w, m_ffn_conv_w, v_ffn_conv_w)):
        adamw_big(nm, w, gg, m, v)

    def pack_small(n1, n2, fn, al, db, gn):
        return jnp.concatenate([n1, n2, fn[None, :], _pad_lanes(jnp.concatenate([al, db], axis=1)), _pad_lanes(gn),
                                jnp.zeros((REPL_ROWS - 5, D_MODEL), F32)], axis=0)

    sw = pack_small(norm1_w, norm2_w, final_norm_w, a_log, dt_bias, gdn_norm_w)
    sm = pack_small(m_norm1_w, m_norm2_w, m_final_norm_w, m_a_log, m_dt_bias, m_gdn_norm_w)
    sv = pack_small(v_norm1_w, v_norm2_w, v_final_norm_w, v_a_log, v_dt_bias, v_gdn_norm_w)
    sd, smn, svn = _adamw(sw, small_red[:REPL_ROWS], sm, sv, "adamw_small")

    def unpack_small(t):
        return dict(norm1_w=t[0:1], norm2_w=t[1:2], final_norm_w=t[2], a_log=t[3:4, 0:4], dt_bias=t[3:4, 4:8],
                    gdn_norm_w=t[4:5, 0:128])

    sg = dict(norm1_w=g_n1w, norm2_w=g_n2w, final_norm_w=g_fnw, a_log=g_alog, dt_bias=g_dtb, gdn_norm_w=g_gnw)
    sd, smn, svn = unpack_small(sd), unpack_small(smn), unpack_small(svn)
    names = ["norm1_w", "w_in", "conv_qkv_w", "a_log", "dt_bias", "gdn_norm_w", "w_out", "norm2_w", "w_up",
             "ffn_conv_w", "w_down", "final_norm_w"]
    grads = [big[n][0] if n in big else sg[n] for n in names]
    deltas = [big[n][1] if n in big else sd[n] for n in names]
    new_m = [big[n][2] if n in big else smn[n] for n in names]
    new_v = [big[n][3] if n in big else svn[n] for n in names]
    return (loss, dx[None], *grads, *deltas, *new_m, *new_v)
```

```python
import functools
import math

import numpy as np
import jax
import jax.numpy as jnp
from jax import lax
from jax.experimental import pallas as pl
from jax.experimental.pallas import tpu as pltpu

F32 = jnp.float32
BF16 = jnp.bfloat16
_MXU = jnp.bfloat16
_HI = lax.Precision.HIGHEST
EPS = 1e-6
V7X_VMEM_LIMIT = 56 * 1024 * 1024
MESH = pl.DeviceIdType.MESH

D_MODEL = 1024
GDN_HEADS, GDN_DIM, GDN_CHUNK, GDN_CONV = 4, 128, 64, 4
GDN_WIDTH = GDN_HEADS * GDN_DIM
DIL_HEADS, DIL_DIM = 8, 64
DIL_WIDTH = DIL_HEADS * DIL_DIM
D_FF, FFN_CONV = 2816, 3
IN_COLS = 3592
P_COLS = 3840
P_Z, P_QKVB, P_BA = 1536, 2048, 3584
ATT_T = 1024
ADAM_LR, ADAM_B1, ADAM_B2, ADAM_EPS, ADAM_WD, ADAM_STEP = 0.001, 0.9, 0.999, 1e-08, 0.01, 10
N_CHIPS = 4


def _cparams(sem=None, vmem=None):
    kw = {}
    if sem is not None:
        kw["dimension_semantics"] = sem
    if vmem is not None:
        kw["vmem_limit_bytes"] = vmem
    return pltpu.CompilerParams(**kw)


def _silu(x):
    return x * jax.nn.sigmoid(x)


def _pick_tile(n, cap):
    best = None
    for t in range(128, min(n, cap) + 1, 128):
        if n % t == 0:
            best = t
    return best or n


def _mm(a, b, mode, *, out_dtype=F32, residual=None, name, b_blocks=False, place=None, into=None, tn=None, after=()):
    if mode == "nn":
        M, K = a.shape
        N = b.shape[0] * b.shape[2] if b_blocks else b.shape[1]
    elif mode == "nt":
        (M, K), (N, _) = a.shape, b.shape
    else:
        (K, M), (_, N) = a.shape, b.shape
    tm = _pick_tile(M, 1024)
    tn = b.shape[2] if b_blocks else (tn or _pick_tile(N, 1536))

    def vmem(tm, tn):
        return 2 * (tm * K * a.dtype.itemsize + tn * K * b.dtype.itemsize
                    + tm * tn * (jnp.dtype(out_dtype).itemsize + (4 if residual is not None else 0))) + 3 * tm * tn * 4

    fixed_tn = b_blocks or (place is not None and place[0] == "blocks")
    while vmem(tm, tn) > 40 * 1024 * 1024:
        if (tm >= tn or fixed_tn) and tm % 256 == 0:
            tm //= 2
        elif tn % 256 == 0 and not fixed_tn:
            tn //= 2
        else:
            tm //= 2
    a_spec = pl.BlockSpec((K, tm), lambda j, i: (0, i)) if mode == "tn" else pl.BlockSpec((tm, K), lambda j, i: (i, 0))
    if b_blocks:
        b_spec = pl.BlockSpec((None, K, tn), lambda j, i: (j, 0, 0))
    else:
        b_spec = pl.BlockSpec((tn, K), lambda j, i: (j, 0)) if mode == "nt" else pl.BlockSpec((K, tn), lambda j, i: (0, j))
    r_spec = pl.BlockSpec((tm, tn), lambda j, i: (i, j))
    if place is None:
        o_spec, o_shape = r_spec, (M, N)
    elif place[0] == "rows":
        off = place[2] // tm
        o_spec, o_shape = pl.BlockSpec((tm, tn), lambda j, i: (i + off, j)), (place[1], N)
    else:
        off = place[2]
        o_spec, o_shape = pl.BlockSpec((None, tm, tn), lambda j, i: (j + off, i, 0)), (place[1], M, tn)
    dims = {"nn": (((1,), (0,)), ((), ())), "nt": (((1,), (1,)), ((), ())), "tn": (((0,), (0,)), ((), ()))}[mode]

    def body(*refs):
        a_ref, b_ref = refs[0], refs[1]
        o_ref = refs[-1]
        acc = lax.dot_general(a_ref[...].astype(_MXU), b_ref[...].astype(_MXU), dims, preferred_element_type=F32)
        if residual is not None:
            acc = acc + refs[2][...]
        o_ref[...] = acc.astype(out_dtype)

    ins, specs, alias = [a, b], [a_spec, b_spec], {}
    if residual is not None:
        ins.append(residual)
        specs.append(r_spec)
    if into is not None:
        alias = {len(ins): 0}
        ins.append(into)
        specs.append(pl.BlockSpec(memory_space=pl.ANY))
    ins += list(after)
    specs += [pl.BlockSpec(memory_space=pl.ANY)] * len(after)
    return pl.pallas_call(
        body, name=name, grid=(N // tn, M // tm), in_specs=specs, out_specs=o_spec,
        out_shape=jax.ShapeDtypeStruct(o_shape, out_dtype), input_output_aliases=alias,
        compiler_params=_cparams(("parallel", "parallel"), V7X_VMEM_LIMIT),
    )(*ins)


def _mm_nt_blocks(a_list, b4, name, after=()):
    M = a_list[0].shape[0]
    nb, N, Kb = b4.shape
    tm, tn = _pick_tile(M, 1024), _pick_tile(N, 512)

    def body(a0_ref, a1_ref, b_ref, *rest):
        o_ref = rest[-1]
        acc = None
        for blk in range(nb):
            a_ref = (a0_ref, a1_ref)[blk // 2]
            lo = (blk % 2) * Kb
            t = lax.dot_general(a_ref[:, lo:lo + Kb].astype(_MXU), b_ref[blk].astype(_MXU), (((1,), (1,)), ((), ())),
                                preferred_element_type=F32)
            acc = t if acc is None else acc + t
        o_ref[...] = acc

    a_spec = pl.BlockSpec((tm, 2 * Kb), lambda j, i: (i, 0))
    return pl.pallas_call(
        body, name=name, grid=(N // tn, M // tm),
        in_specs=[a_spec, a_spec, pl.BlockSpec((nb, tn, Kb), lambda j, i: (0, j, 0))]
        + [pl.BlockSpec(memory_space=pl.ANY)] * len(after),
        out_specs=pl.BlockSpec((tm, tn), lambda j, i: (i, j)), out_shape=jax.ShapeDtypeStruct((M, N), F32),
        compiler_params=_cparams(("parallel", "parallel"), V7X_VMEM_LIMIT),
    )(a_list[0], a_list[1], b4, *after)


def _wp_assemble(g_in, after=()):
    nb, Dm, Wb = g_in.shape
    T = 256
    n_lo = P_QKVB - 2 * Wb

    def body(g_ref, *rest):
        g2 = g_ref[2]
        rest[-1][...] = jnp.concatenate(
            [g_ref[0], g_ref[1], g2[:, :n_lo], g2[:, n_lo + 8:], g_ref[3], g2[:, n_lo:n_lo + 8],
             jnp.zeros((T, P_COLS - P_BA - 8), g_in.dtype)], axis=1)

    return pl.pallas_call(
        body, name="wp_assemble", grid=(Dm // T,),
        in_specs=[pl.BlockSpec((nb, T, Wb), lambda i: (0, i, 0))] + [pl.BlockSpec(memory_space=pl.ANY)] * len(after),
        out_specs=pl.BlockSpec((T, P_COLS), lambda i: (i, 0)), out_shape=jax.ShapeDtypeStruct((Dm, P_COLS), g_in.dtype),
        compiler_params=_cparams(("parallel",)),
    )(g_in, *after)


def _win_split(d_wp):
    Dm = d_wp.shape[0]
    Wb = IN_COLS // N_CHIPS
    T = 256

    def body(x_ref, o_ref):
        xv = x_ref[...]
        o_ref[0] = xv[:, 0:Wb]
        o_ref[1] = xv[:, Wb:2 * Wb]
        o_ref[2] = jnp.concatenate([xv[:, 2 * Wb:P_QKVB], xv[:, P_BA:P_BA + 8], xv[:, P_QKVB:3 * Wb - 8]], axis=1)
        o_ref[3] = xv[:, 3 * Wb - 8:P_BA]

    return pl.pallas_call(
        body, name="win_split", grid=(Dm // T,), in_specs=[pl.BlockSpec((T, P_COLS), lambda i: (i, 0))],
        out_specs=pl.BlockSpec((N_CHIPS, T, Wb), lambda i: (0, i, 0)),
        out_shape=jax.ShapeDtypeStruct((N_CHIPS, Dm, Wb), F32), compiler_params=_cparams(("parallel",)),
    )(d_wp)


def _rmsnorm_fwd(x, w, name):
    S, D = x.shape
    T = _pick_tile(S, 512)

    def body(x_ref, w_ref, o_ref):
        xv = x_ref[...]
        rs = lax.rsqrt(jnp.mean(xv * xv, axis=-1, keepdims=True) + EPS)
        o_ref[...] = (xv * rs * w_ref[...]).astype(o_ref.dtype)

    return pl.pallas_call(
        body, name=name, grid=(S // T,),
        in_specs=[pl.BlockSpec((T, D), lambda i: (i, 0)), pl.BlockSpec((1, D), lambda i: (0, 0))],
        out_specs=pl.BlockSpec((T, D), lambda i: (i, 0)),
        out_shape=jax.ShapeDtypeStruct((S, D), _MXU),
        compiler_params=_cparams(("parallel",)),
    )(x, w)


def _rmsnorm_bwd(dh, x, w, dres, name):
    S, D = x.shape
    T = _pick_tile(S, 512)

    def body(dh_ref, x_ref, w_ref, dres_ref, dx_ref, dw_ref):
        xv = x_ref[...]
        rs = lax.rsqrt(jnp.mean(xv * xv, axis=-1, keepdims=True) + EPS)
        xn = xv * rs
        dhv = dh_ref[...]
        dxn = dhv * w_ref[...]
        dx_ref[...] = dres_ref[...] + rs * (dxn - xn * jnp.mean(dxn * xn, axis=-1, keepdims=True))

        @pl.when(pl.program_id(0) == 0)
        def _():
            dw_ref[...] = jnp.zeros_like(dw_ref)

        dw_ref[...] += jnp.sum(dhv * xn, axis=0, keepdims=True)

    row = pl.BlockSpec((T, D), lambda i: (i, 0))
    vec = pl.BlockSpec((1, D), lambda i: (0, 0))
    return pl.pallas_call(
        body, name=name, grid=(S // T,), in_specs=[row, row, vec, row], out_specs=(row, vec),
        out_shape=(jax.ShapeDtypeStruct((S, D), F32), jax.ShapeDtypeStruct((1, D), F32)),
        compiler_params=_cparams(("arbitrary",)),
    )(dh, x, w, dres)


def _loss_head(x3, w, tgt, name):
    S, D = x3.shape
    T = _pick_tile(S, 512)

    def body(x_ref, w_ref, t_ref, loss_ref, dx_ref, dxn_ref, dw_ref):
        xv = x_ref[...]
        rs = lax.rsqrt(jnp.mean(xv * xv, axis=-1, keepdims=True) + EPS)
        xn = xv * rs
        err = xn * w_ref[...] - t_ref[...]
        dy = err * (1.0 / D)
        dxn = dy * w_ref[...]
        dxv = rs * (dxn - xn * jnp.mean(dxn * xn, axis=-1, keepdims=True))
        dx_ref[...] = dxv
        dxn_ref[...] = dxv.astype(dxn_ref.dtype)

        @pl.when(pl.program_id(0) == 0)
        def _():
            dw_ref[...] = jnp.zeros_like(dw_ref)
            loss_ref[...] = jnp.zeros_like(loss_ref)

        dw_ref[...] += jnp.sum(dy * xn, axis=0, keepdims=True)
        part = jnp.sum(jnp.sum(err * err, axis=-1, keepdims=True), axis=0, keepdims=True) * (0.5 / D)
        loss_ref[...] += jnp.broadcast_to(part, loss_ref.shape)

    row = pl.BlockSpec((T, D), lambda i: (i, 0))
    vec = pl.BlockSpec((1, D), lambda i: (0, 0))
    return pl.pallas_call(
        body, name=name, grid=(S // T,), in_specs=[row, vec, row],
        out_specs=(pl.BlockSpec((8, 128), lambda i: (0, 0)), row, row, vec),
        out_shape=(jax.ShapeDtypeStruct((8, 128), F32), jax.ShapeDtypeStruct((S, D), F32), jax.ShapeDtypeStruct((S, D), _MXU),
                   jax.ShapeDtypeStruct((1, D), F32)),
        compiler_params=_cparams(("arbitrary",)),
    )(x3, w, tgt)


def _shifted(ext, back, lo, n):
    if back == 0:
        return ext[lo:lo + n, :]
    return pltpu.roll(ext, back % ext.shape[0], 0)[lo:lo + n, :]


def _conv_windows(ext, K, T):
    return [_shifted(ext, (K - 1) - i, 8, T) for i in range(K)]


def _conv_taps(ext, w, K, T):
    out = None
    for i, win in enumerate(_conv_windows(ext, K, T)):
        term = win * w[i:i + 1, :]
        out = term if out is None else out + term
    return out


def _conv_taps_t(ext, w, K, T):
    out = None
    for i in range(K):
        term = _shifted(ext, i - (K - 1), 0, T) * w[i:i + 1, :]
        out = term if out is None else out + term
    return out


def _tri_masks(C):
    r = lax.broadcasted_iota(jnp.int32, (C, C), 0)
    c = lax.broadcasted_iota(jnp.int32, (C, C), 1)
    return r == c, r >= c, r > c, r <= c


_NN, _NT, _TN = ((1,), (0,)), ((1,), (1,)), ((0,), (0,))
_GDN_PASSES = dict(qk=1, inv=1, sol=1, scan=1, bwd=1)


def _bdot_raw(a, b, kind, passes):
    dims = ({"NN": ((2,), (1,)), "NT": ((2,), (2,)), "TN": ((1,), (1,))}[kind], ((0,), (0,)))
    if passes == 0:
        return lax.dot_general(a, b, dims, precision=_HI, preferred_element_type=F32)
    ah, bh = a.astype(BF16), b.astype(BF16)
    out = lax.dot_general(ah, bh, dims, preferred_element_type=F32)
    if passes == 3:
        al, bl = (a - ah.astype(F32)).astype(BF16), (b - bh.astype(F32)).astype(BF16)
        out = out + lax.dot_general(ah, bl, dims, preferred_element_type=F32) + lax.dot_general(al, bh, dims, preferred_element_type=F32)
    return out


@functools.partial(jax.custom_vjp, nondiff_argnums=(2, 3))
def _bdot(a, b, kind, passes):
    return _bdot_raw(a, b, kind, passes)


def _bdot_fwd(a, b, kind, passes):
    return _bdot_raw(a, b, kind, passes), (a, b)


def _bdot_bwd(kind, passes, res, ct):
    a, b = res
    if kind == "NN":
        return _bdot_raw(ct, b, "NT", passes), _bdot_raw(a, ct, "TN", passes)
    if kind == "NT":
        return _bdot_raw(ct, b, "NN", passes), _bdot_raw(ct, a, "TN", passes)
    return _bdot_raw(b, ct, "NT", passes), _bdot_raw(a, ct, "NN", passes)


_bdot.defvjp(_bdot_fwd, _bdot_bwd)


def _softplus(x):
    return jnp.maximum(x, 0.0) + jnp.log(1.0 + jnp.exp(-jnp.abs(x)))


def _gdn_stage1(cq, ck, cv, b_col, a_col, alog, dtb, dot=_bdot_raw):
    C = cq.shape[1]
    eye, incl, strict, incl_t = _tri_masks(C)
    qn = cq * lax.rsqrt(jnp.sum(cq * cq, axis=-1, keepdims=True) + EPS) * (GDN_DIM ** -0.5)
    kn = ck * lax.rsqrt(jnp.sum(ck * ck, axis=-1, keepdims=True) + EPS)
    beta = jax.nn.sigmoid(b_col)
    g = -jnp.exp(alog) * _softplus(a_col + dtb)
    g_row = jnp.sum(jnp.where(eye, g, 0.0), axis=1, keepdims=True)
    beta_row = jnp.sum(jnp.where(eye, beta, 0.0), axis=1, keepdims=True)
    gc_col = jnp.sum(jnp.where(incl, g_row, 0.0), axis=2, keepdims=True)
    gc_row = jnp.sum(jnp.where(incl_t, g, 0.0), axis=1, keepdims=True)
    dec = jnp.where(incl, jnp.exp(jnp.where(incl, gc_col - gc_row, 0.0)), 0.0)
    kk = dot(kn, kn, "NT", _GDN_PASSES["qk"])
    qk = dot(qn, kn, "NT", _GDN_PASSES["qk"])
    lmat = jnp.where(strict, dec * kk * beta_row, 0.0)
    attn = dec * qk * beta_row
    gam = jnp.exp(gc_col)
    gc_last = gc_col[:, C - 1:C, :]
    k_end = kn * (jnp.exp(gc_last - gc_col) * beta)
    return lmat, cv, gam * kn, gam * qn, attn, k_end, jnp.exp(gc_last)


def _tri_inv(lmat):
    C = lmat.shape[1]
    eye = _tri_masks(C)[0]
    ps = _GDN_PASSES["inv"]
    p = jnp.where(eye, 1.0, 0.0) - lmat
    lp = _bdot_raw(lmat, lmat, "NN", ps)
    n = int(math.log2(C))
    for s in range(1, n):
        p = p + _bdot_raw(p, lp, "NN", ps)
        if s < n - 1:
            lp = _bdot_raw(lp, lp, "NN", ps)
    return p


def _gated_norm(o, z, gnw):
    on = o * lax.rsqrt(jnp.mean(o * o, axis=-1, keepdims=True) + EPS) * gnw
    return on * _silu(z)


GDN_PG = 2
GDN_SG = 4


def _gdn_pairs(c, ba, gp, G):
    C, W, H = GDN_CHUNK, GDN_WIDTH, GDN_HEADS
    pairs = [(j, h) for j in range(G) for h in range(H)]
    cq, ck, cv = (jnp.stack([c[C * j:C * (j + 1), o + GDN_DIM * h:o + GDN_DIM * (h + 1)] for j, h in pairs]) for o in (0, W, 2 * W))
    b_col = jnp.stack([ba[C * j:C * (j + 1), h:h + 1] for j, h in pairs])
    a_col = jnp.stack([ba[C * j:C * (j + 1), H + h:H + h + 1] for j, h in pairs])
    alog = jnp.stack([gp[0:1, h:h + 1] for j, h in pairs])
    dtb = jnp.stack([gp[0:1, H + h:H + h + 1] for j, h in pairs])
    return pairs, (cq, ck, cv, b_col, a_col, alog, dtb)


def _gdn_pre_specs(S, G):
    C = GDN_CHUNK
    T = C * G
    return dict(
        cur=pl.BlockSpec((T, 3 * GDN_WIDTH), lambda i: (i, 0)),
        prev=pl.BlockSpec((8, 3 * GDN_WIDTH), lambda i: (jnp.maximum(i * (T // 8) - 1, 0), 0)),
        ba=pl.BlockSpec((T, 128), lambda i: (i, P_BA // 128)),
        cw=pl.BlockSpec((GDN_CONV, 3 * GDN_WIDTH), lambda i: (0, 0)),
        vec=pl.BlockSpec((1, 128), lambda i: (0, 0)),
        hd=pl.BlockSpec((GDN_HEADS, T, GDN_DIM), lambda i: (0, i, 0)),
        hc=pl.BlockSpec((GDN_HEADS, T, C), lambda i: (0, i, 0)),
        ge=pl.BlockSpec((G, GDN_HEADS, 8, 128), lambda i: (i, 0, 0, 0)),
    )


def _hd_shape(S, last=GDN_DIM):
    return jax.ShapeDtypeStruct((GDN_HEADS, S, last), F32)


def _gdn_pre(proj, conv_w, gp):
    S = proj.shape[0]
    C, G = GDN_CHUNK, GDN_PG
    nc = S // C
    sp = _gdn_pre_specs(S, G)

    def body(cur_ref, prev_ref, ba_ref, cw_ref, gp_ref, uv_ref, wk_ref, qd_ref, ke_ref, at_ref, ti_ref, ge_ref):
        prev = prev_ref[...] * jnp.where(pl.program_id(0) == 0, 0.0, 1.0)
        c = _silu(_conv_taps(jnp.concatenate([prev, cur_ref[...]], axis=0), cw_ref[...], GDN_CONV, C * G))
        pairs, args = _gdn_pairs(c, ba_ref[...], gp_ref[...], G)
        lmat, v, rk, q_dec, attn, k_end, g_end = _gdn_stage1(*args)
        t = _tri_inv(lmat)
        u_v = _bdot_raw(t, v, "NN", _GDN_PASSES["sol"])
        w_k = _bdot_raw(t, rk, "NN", _GDN_PASSES["sol"])
        for b, (j, h) in enumerate(pairs):
            rows = slice(C * j, C * (j + 1))
            uv_ref[h, rows, :] = u_v[b]
            wk_ref[h, rows, :] = w_k[b]
            qd_ref[h, rows, :] = q_dec[b]
            ke_ref[h, rows, :] = k_end[b]
            at_ref[h, rows, :] = attn[b]
            ti_ref[h, rows, :] = t[b]
            ge_ref[j, h] = jnp.broadcast_to(g_end[b], (8, 128))

    return pl.pallas_call(
        body, name="gdn_pre", grid=(nc // G,),
        in_specs=[sp["cur"], sp["prev"], sp["ba"], sp["cw"], sp["vec"]],
        out_specs=(sp["hd"], sp["hd"], sp["hd"], sp["hd"], sp["hc"], sp["hc"], sp["ge"]),
        out_shape=(_hd_shape(S), _hd_shape(S), _hd_shape(S), _hd_shape(S), _hd_shape(S, C), _hd_shape(S, C),
                   jax.ShapeDtypeStruct((nc, GDN_HEADS, 8, 128), F32)),
        compiler_params=_cparams(("parallel",)),
    )(proj, proj, proj, conv_w, gp)


def _gdn_scan_specs(S, G, rev):
    C = GDN_CHUNK
    T = C * G
    n = S // T
    ci = (lambda i: n - 1 - i) if rev else (lambda i: i)
    return dict(
        hd=pl.BlockSpec((GDN_HEADS, T, GDN_DIM), lambda i: (0, ci(i), 0)),
        hc=pl.BlockSpec((GDN_HEADS, T, C), lambda i: (0, ci(i), 0)),
        ge=pl.BlockSpec((G, GDN_HEADS, 8, 128), lambda i: (ci(i), 0, 0, 0)),
        z=pl.BlockSpec((T, GDN_WIDTH), lambda i: (ci(i), P_Z // GDN_WIDTH)),
        oa=pl.BlockSpec((T, GDN_WIDTH), lambda i: (ci(i), 0)),
        vec=pl.BlockSpec((1, 128), lambda i: (0, 0)),
        st=pl.BlockSpec((G, GDN_HEADS, GDN_DIM, GDN_DIM), lambda i: (ci(i), 0, 0, 0)),
    )


def _gdn_scan(u_v, w_k, q_dec, k_end, attn, g_end, proj, gnw):
    S = proj.shape[0]
    C, G = GDN_CHUNK, GDN_SG
    nc = S // C
    sp = _gdn_scan_specs(S, G, False)
    ps = _GDN_PASSES["scan"]

    def body(uv_ref, wk_ref, qd_ref, ke_ref, at_ref, ge_ref, z_ref, gnw_ref, oa_ref, st_ref, s_scr):
        @pl.when(pl.program_id(0) == 0)
        def _():
            s_scr[...] = jnp.zeros_like(s_scr)

        for j in range(G):
            rows = slice(C * j, C * (j + 1))
            st = s_scr[...]
            st_ref[j] = st
            u = uv_ref[:, rows, :] - _bdot_raw(wk_ref[:, rows, :], st, "NN", ps)
            o = _bdot_raw(qd_ref[:, rows, :], st, "NN", ps) + _bdot_raw(at_ref[:, rows, :], u, "NN", ps)
            s_scr[...] = ge_ref[j][:, 0:1, 0:1] * st + _bdot_raw(ke_ref[:, rows, :], u, "TN", ps)
            for h in range(GDN_HEADS):
                cols = slice(GDN_DIM * h, GDN_DIM * (h + 1))
                oa_ref[rows, cols] = _gated_norm(o[h], z_ref[rows, cols], gnw_ref[...])

    return pl.pallas_call(
        body, name="gdn_scan", grid=(nc // G,),
        in_specs=[sp["hd"], sp["hd"], sp["hd"], sp["hd"], sp["hc"], sp["ge"], sp["z"], sp["vec"]],
        out_specs=(sp["oa"], sp["st"]),
        out_shape=(jax.ShapeDtypeStruct((S, GDN_WIDTH + DIL_WIDTH), F32),
                   jax.ShapeDtypeStruct((nc, GDN_HEADS, GDN_DIM, GDN_DIM), F32)),
        scratch_shapes=[pltpu.VMEM((GDN_HEADS, GDN_DIM, GDN_DIM), F32)],
        compiler_params=_cparams(("arbitrary",)),
    )(u_v, w_k, q_dec, k_end, attn, g_end, proj, gnw)


def _gdn_scan_bwd(u_v, w_k, q_dec, k_end, attn, g_end, proj, gnw, states, d_oa):
    S = proj.shape[0]
    C, G = GDN_CHUNK, GDN_SG
    nc = S // C
    sp = _gdn_scan_specs(S, G, True)
    ps, pb = _GDN_PASSES["scan"], _GDN_PASSES["bwd"]

    def body(uv_ref, wk_ref, qd_ref, ke_ref, at_ref, ge_ref, z_ref, gnw_ref, st_ref, doa_ref,
             duv_ref, dwk_ref, dqd_ref, dke_ref, dat_ref, dge_ref, dz_ref, dgnw_ref, ds_scr):
        @pl.when(pl.program_id(0) == 0)
        def _():
            ds_scr[...] = jnp.zeros_like(ds_scr)
            dgnw_ref[...] = jnp.zeros_like(dgnw_ref)

        dgnw = jnp.zeros((1, 128), F32)
        for j in reversed(range(G)):
            rows = slice(C * j, C * (j + 1))
            st = st_ref[j]
            wk, qd, ke, at = wk_ref[:, rows, :], qd_ref[:, rows, :], ke_ref[:, rows, :], at_ref[:, rows, :]
            u = uv_ref[:, rows, :] - _bdot_raw(wk, st, "NN", ps)
            o = _bdot_raw(qd, st, "NN", ps) + _bdot_raw(at, u, "NN", ps)
            dos = []
            for h in range(GDN_HEADS):
                cols = slice(GDN_DIM * h, GDN_DIM * (h + 1))
                _, vjp2 = jax.vjp(_gated_norm, o[h], z_ref[rows, cols], gnw_ref[...])
                do_h, dz_h, dgn = vjp2(doa_ref[rows, cols])
                dz_ref[rows, cols] = dz_h
                dgnw = dgnw + dgn
                dos.append(do_h)
            do = jnp.stack(dos)
            ds_new = ds_scr[...]
            du = _bdot_raw(at, do, "TN", pb) + _bdot_raw(ke, ds_new, "NN", pb)
            duv_ref[:, rows, :] = du
            dat_ref[:, rows, :] = _bdot_raw(do, u, "NT", pb)
            dqd_ref[:, rows, :] = _bdot_raw(do, st, "NT", pb)
            dke_ref[:, rows, :] = _bdot_raw(u, ds_new, "NT", pb)
            dwk_ref[:, rows, :] = -_bdot_raw(du, st, "NT", pb)
            d_ge = jnp.sum(jnp.sum(st * ds_new, axis=2, keepdims=True), axis=1, keepdims=True)
            dge_ref[j] = jnp.broadcast_to(d_ge, (GDN_HEADS, 8, 128))
            ds_scr[...] = ge_ref[j][:, 0:1, 0:1] * ds_new + _bdot_raw(qd, do, "TN", pb) - _bdot_raw(wk, du, "TN", pb)
        dgnw_ref[...] += dgnw

    return pl.pallas_call(
        body, name="gdn_scan_bwd", grid=(nc // G,),
        in_specs=[sp["hd"], sp["hd"], sp["hd"], sp["hd"], sp["hc"], sp["ge"], sp["z"], sp["vec"], sp["st"], sp["oa"]],
        out_specs=(sp["hd"], sp["hd"], sp["hd"], sp["hd"], sp["hc"], sp["ge"], sp["oa"], sp["vec"]),
        out_shape=(_hd_shape(S), _hd_shape(S), _hd_shape(S), _hd_shape(S), _hd_shape(S, C),
                   jax.ShapeDtypeStruct((nc, GDN_HEADS, 8, 128), F32), jax.ShapeDtypeStruct((S, GDN_WIDTH), F32),
                   jax.ShapeDtypeStruct((1, 128), F32)),
        scratch_shapes=[pltpu.VMEM((GDN_HEADS, GDN_DIM, GDN_DIM), F32)],
        compiler_params=_cparams(("arbitrary",)),
    )(u_v, w_k, q_dec, k_end, attn, g_end, proj, gnw, states, d_oa)


def _gdn_post(proj, conv_w, gp, tinv, u_v, w_k, d_uv, d_wk, d_qd, d_ke, d_at, d_ge):
    S = proj.shape[0]
    C, G = GDN_CHUNK, GDN_PG
    nc = S // C
    sp = _gdn_pre_specs(S, G)
    pb = _GDN_PASSES["bwd"]

    def body(cur_ref, prev_ref, ba_ref, cw_ref, gp_ref, ti_ref, uv_ref, wk_ref, duv_ref, dwk_ref, dqd_ref, dke_ref,
             dat_ref, dge_ref, dpre_ref, dba_ref, dgp_ref):
        i = pl.program_id(0)

        @pl.when(i == 0)
        def _():
            dgp_ref[...] = jnp.zeros_like(dgp_ref)

        prev = prev_ref[...] * jnp.where(i == 0, 0.0, 1.0)
        pre = _conv_taps(jnp.concatenate([prev, cur_ref[...]], axis=0), cw_ref[...], GDN_CONV, C * G)
        sg = jax.nn.sigmoid(pre)
        dsilu = sg * (1.0 + pre * (1.0 - sg))
        pairs, args = _gdn_pairs(pre * sg, ba_ref[...], gp_ref[...], G)
        _, vjp1 = jax.vjp(functools.partial(_gdn_stage1, dot=_bdot), *args)

        def take(ref):
            return jnp.stack([ref[h, C * j:C * (j + 1), :] for j, h in pairs])

        t, u_v, w_k = take(ti_ref), take(uv_ref), take(wk_ref)
        d_v = _bdot_raw(t, take(duv_ref), "TN", pb)
        d_rk = _bdot_raw(t, take(dwk_ref), "TN", pb)
        d_l = -(_bdot_raw(d_v, u_v, "NT", pb) + _bdot_raw(d_rk, w_k, "NT", pb))
        d_ge = jnp.stack([dge_ref[j, h][0:1, 0:1] for j, h in pairs])
        dcq, dck, dcv, db, da, dalog, ddtb = vjp1((d_l, d_v, d_rk, take(dqd_ref), take(dat_ref), take(dke_ref), d_ge))
        lane = lax.broadcasted_iota(jnp.int32, (C, 128), 1)
        lane1 = lax.broadcasted_iota(jnp.int32, (1, 128), 1)
        dgp = jnp.zeros((1, 128), F32)
        for j in range(G):
            rows = slice(C * j, C * (j + 1))
            dba = jnp.zeros((C, 128), F32)
            for h in range(GDN_HEADS):
                b = GDN_HEADS * j + h
                for o_, dcx in ((0, dcq), (GDN_WIDTH, dck), (2 * GDN_WIDTH, dcv)):
                    cols = slice(o_ + GDN_DIM * h, o_ + GDN_DIM * (h + 1))
                    dpre_ref[rows, cols] = dcx[b] * dsilu[rows, cols]
                dba = dba + jnp.where(lane == h, db[b], 0.0) + jnp.where(lane == GDN_HEADS + h, da[b], 0.0)
                dgp = dgp + jnp.where(lane1 == h, dalog[b], 0.0) + jnp.where(lane1 == GDN_HEADS + h, ddtb[b], 0.0)
            dba_ref[rows, :] = dba
        dgp_ref[0:1, :] += dgp

    T = C * G
    return pl.pallas_call(
        body, name="gdn_post", grid=(nc // G,),
        in_specs=[sp["cur"], sp["prev"], sp["ba"], sp["cw"], sp["vec"], sp["hc"], sp["hd"], sp["hd"], sp["hd"], sp["hd"],
                  sp["hd"], sp["hd"], sp["hc"], sp["ge"]],
        out_specs=(sp["cur"], pl.BlockSpec((T, 128), lambda i: (i, 0)), pl.BlockSpec((8, 128), lambda i: (0, 0))),
        out_shape=(jax.ShapeDtypeStruct((S, 3 * GDN_WIDTH), F32), jax.ShapeDtypeStruct((S, 128), F32),
                   jax.ShapeDtypeStruct((8, 128), F32)),
        compiler_params=_cparams(("arbitrary",)),
    )(proj, proj, proj, conv_w, gp, tinv, u_v, w_k, d_uv, d_wk, d_qd, d_ke, d_at, d_ge)


def _conv_bwd(dpre, x, xcol0, w, K, name, tc):
    S, Cc = dpre.shape
    T = _pick_tile(S, 256)
    nt, ncol = S // T, Cc // tc
    xo = xcol0 // tc

    def body(d_ref, dn_ref, x_ref, xp_ref, w_ref, dx_ref, dw_ref):
        i = pl.program_id(1)
        dn = dn_ref[...] * jnp.where(i == nt - 1, 0.0, 1.0)
        dv = d_ref[...]
        ext_d = jnp.concatenate([dv, dn], axis=0)
        dx_ref[...] = _conv_taps_t(ext_d, w_ref[...], K, T).astype(dx_ref.dtype)
        xp = xp_ref[...] * jnp.where(i == 0, 0.0, 1.0)
        ext_x = jnp.concatenate([xp, x_ref[...]], axis=0)

        @pl.when(i == 0)
        def _():
            dw_ref[...] = jnp.zeros_like(dw_ref)

        for k in range(K):
            dw_ref[k:k + 1, :] += jnp.sum(dv * _shifted(ext_x, (K - 1) - k, 8, T), axis=0, keepdims=True)

    r8 = T // 8
    return pl.pallas_call(
        body, name=name, grid=(ncol, nt),
        in_specs=[pl.BlockSpec((T, tc), lambda j, i: (i, j)),
                  pl.BlockSpec((8, tc), lambda j, i: (jnp.minimum((i + 1) * r8, S // 8 - 1), j)),
                  pl.BlockSpec((T, tc), lambda j, i: (i, j + xo)),
                  pl.BlockSpec((8, tc), lambda j, i: (jnp.maximum(i * r8 - 1, 0), j + xo)),
                  pl.BlockSpec((K, tc), lambda j, i: (0, j))],
        out_specs=(pl.BlockSpec((T, tc), lambda j, i: (i, j)), pl.BlockSpec((K, tc), lambda j, i: (0, j))),
        out_shape=(jax.ShapeDtypeStruct((S, Cc), _MXU), jax.ShapeDtypeStruct((K, Cc), F32)),
        compiler_params=_cparams(("parallel", "arbitrary")),
    )(dpre, dpre, x, x, w)


def _dil_bias(nt, T):
    d = (np.arange(nt)[:, None, None] * T + np.arange(T)[None, None, :] - np.arange(T)[None, :, None])
    cnt = ((d >= 0) & (d <= 128)).astype(np.float64) + ((d >= 0) & (d % 4 == 0) & (d <= 512)) + ((d >= 0) & (d % 16 == 0))
    return jnp.asarray(np.where(cnt > 0, np.log(np.maximum(cnt, 1.0)), -1e30), dtype=F32)


def _attn_fwd(proj, mix):
    S = proj.shape[0]
    T = min(ATT_T, S)
    nt = S // T
    bias = _dil_bias(nt, T)
    scale = DIL_DIM ** -0.5
    npair = DIL_WIDTH // 128
    qb0, kb0, vb0 = P_QKVB // 128, (P_QKVB + DIL_WIDTH) // 128, (P_QKVB + 2 * DIL_WIDTH) // 128

    def body(q_ref, k_ref, v_ref, b_ref, mix_ref, o_ref, lse_ref):
        i = pl.program_id(1)
        qs = (q_ref[...] * scale).astype(_MXU)

        def step(j, carry):
            kt = k_ref[pl.ds(pl.multiple_of(j * T, T), T), :].astype(_MXU)
            vt = v_ref[pl.ds(pl.multiple_of(j * T, T), T), :].astype(_MXU)
            bt = b_ref[i - j]
            out = []
            for hh in range(2):
                m, l, acc = carry[hh]
                sl = slice(hh * DIL_DIM, (hh + 1) * DIL_DIM)
                s = lax.dot_general(kt[:, sl], qs[:, sl], (_NT, ((), ())), preferred_element_type=F32) + bt
                m_new = jnp.maximum(m, jnp.max(s, axis=0, keepdims=True))
                p = jnp.exp(s - m_new)
                a = jnp.exp(m - m_new)
                l = a * l + jnp.sum(p, axis=0, keepdims=True)
                acc = a * acc + lax.dot_general(vt[:, sl], p.astype(_MXU), (_TN, ((), ())), preferred_element_type=F32)
                out.append((m_new, l, acc))
            return tuple(out)

        init = tuple((jnp.full((1, T), -1e30, F32), jnp.zeros((1, T), F32), jnp.zeros((DIL_DIM, T), F32)) for _ in range(2))
        res = lax.fori_loop(0, i + 1, step, init)
        lse_ref[...] = jnp.zeros_like(lse_ref)
        for hh in range(2):
            m, l, acc = res[hh]
            o_ref[:, hh * DIL_DIM:(hh + 1) * DIL_DIM] = (acc / l).T
            lse_ref[hh:hh + 1, :] = m + jnp.log(l)

    return pl.pallas_call(
        body, name="attn_fwd", grid=(npair, nt),
        in_specs=[pl.BlockSpec((T, 128), lambda p, i: (i, qb0 + p)),
                  pl.BlockSpec((S, 128), lambda p, i: (0, kb0 + p)),
                  pl.BlockSpec((S, 128), lambda p, i: (0, vb0 + p)),
                  pl.BlockSpec((nt, T, T), lambda p, i: (0, 0, 0)), pl.BlockSpec(memory_space=pl.ANY)],
        out_specs=(pl.BlockSpec((T, 128), lambda p, i: (i, GDN_WIDTH // 128 + p)),
                   pl.BlockSpec((None, None, 8, T), lambda p, i: (p, i, 0, 0))),
        out_shape=(jax.ShapeDtypeStruct(mix.shape, F32), jax.ShapeDtypeStruct((npair, nt, 8, T), F32)),
        input_output_aliases={4: 0},
        compiler_params=_cparams(("parallel", "parallel")),
    )(proj, proj, proj, bias, mix)


def _attn_bwd(proj, mix, lse, d_mix):
    S = proj.shape[0]
    T = min(ATT_T, S)
    nt = S // T
    bias = _dil_bias(nt, T)
    scale = DIL_DIM ** -0.5
    npair = DIL_WIDTH // 128
    qb0, kb0, vb0 = P_QKVB // 128, (P_QKVB + DIL_WIDTH) // 128, (P_QKVB + 2 * DIL_WIDTH) // 128

    def body(q_ref, k_ref, v_ref, o_ref, lse_ref, do_ref, b_ref, dq_ref, dk_ref, dv_ref, dq_scr):
        j = pl.program_id(1)

        @pl.when(j == 0)
        def _():
            dq_scr[...] = jnp.zeros_like(dq_scr)

        kt = k_ref[...].astype(_MXU)
        vt = v_ref[...].astype(_MXU)
        ones = jnp.ones((8, DIL_DIM), F32)

        def step(i, carry):
            rows = pl.ds(pl.multiple_of(i * T, T), T)
            qs = (q_ref[rows, :] * scale).astype(_MXU)
            dov = do_ref[rows, :]
            prod = dov * o_ref[rows, :]
            lsev = lse_ref[i]
            dob = dov.astype(_MXU)
            bt = b_ref[i - j]
            out = []
            dqs = []
            for hh in range(2):
                dk, dv = carry[hh]
                sl = slice(hh * DIL_DIM, (hh + 1) * DIL_DIM)
                s = lax.dot_general(kt[:, sl], qs[:, sl], (_NT, ((), ())), preferred_element_type=F32) + bt
                p = jnp.exp(s - lsev[hh:hh + 1, :])
                delta = lax.dot_general(ones, prod[:, sl], (_NT, ((), ())), precision=_HI, preferred_element_type=F32)[0:1, :]
                dp = lax.dot_general(vt[:, sl], dob[:, sl], (_NT, ((), ())), preferred_element_type=F32)
                ds = (p * (dp - delta)).astype(_MXU)
                dv = dv + lax.dot_general(p.astype(_MXU), dob[:, sl], (_NN, ((), ())), preferred_element_type=F32)
                dk = dk + lax.dot_general(ds, qs[:, sl], (_NN, ((), ())), preferred_element_type=F32)
                dqs.append(lax.dot_general(ds, kt[:, sl], (_TN, ((), ())), preferred_element_type=F32) * scale)
                out.append((dk, dv))
            dq_scr[rows, :] += jnp.concatenate(dqs, axis=1)
            return tuple(out)

        init = tuple((jnp.zeros((T, DIL_DIM), F32), jnp.zeros((T, DIL_DIM), F32)) for _ in range(2))
        res = lax.fori_loop(j, nt, step, init)
        dk_ref[...] = jnp.concatenate([res[0][0], res[1][0]], axis=1).astype(dk_ref.dtype)
        dv_ref[...] = jnp.concatenate([res[0][1], res[1][1]], axis=1).astype(dv_ref.dtype)

        @pl.when(j == nt - 1)
        def _():
            dq_ref[...] = dq_scr[...].astype(dq_ref.dtype)

    full = lambda c0: pl.BlockSpec((S, 128), lambda p, j: (0, c0 + p))
    tile = lambda c0: pl.BlockSpec((T, 128), lambda p, j: (j, c0 + p))
    out3 = jax.ShapeDtypeStruct((S, DIL_WIDTH), _MXU)
    return pl.pallas_call(
        body, name="attn_bwd", grid=(npair, nt),
        in_specs=[full(qb0), tile(kb0), tile(vb0), full(GDN_WIDTH // 128),
                  pl.BlockSpec((None, nt, 8, T), lambda p, j: (p, 0, 0, 0)), full(GDN_WIDTH // 128),
                  pl.BlockSpec((nt, T, T), lambda p, j: (0, 0, 0))],
        out_specs=(full(0), tile(0), tile(0)),
        out_shape=(out3, out3, out3),
        scratch_shapes=[pltpu.VMEM((S, 128), F32)],
        compiler_params=_cparams(("parallel", "arbitrary")),
    )(proj, proj, proj, mix, lse, d_mix, bias)


def _ffn_act(up, cw):
    S, Cc = up.shape[0], up.shape[1] // 2
    T, tc = _pick_tile(S, 256), _pick_tile(Cc, 1536)
    r16 = T // 16
    nct = Cc // tc

    def body(g_ref, gp_ref, u_ref, up_ref, wg_ref, wu_ref, o_ref):
        keep = jnp.where(pl.program_id(1) == 0, 0.0, 1.0)
        cg = _conv_taps(jnp.concatenate([gp_ref[8:16, :].astype(F32) * keep, g_ref[...].astype(F32)], axis=0),
                        wg_ref[...], FFN_CONV, T)
        cu = _conv_taps(jnp.concatenate([up_ref[8:16, :].astype(F32) * keep, u_ref[...].astype(F32)], axis=0),
                        wu_ref[...], FFN_CONV, T)
        o_ref[...] = (_silu(cg) * cu).astype(o_ref.dtype)

    cur = lambda o: pl.BlockSpec((T, tc), lambda j, i: (i, j + o))
    prev = lambda o: pl.BlockSpec((16, tc), lambda j, i: (jnp.maximum(i * r16 - 1, 0), j + o))
    wsp = lambda o: pl.BlockSpec((FFN_CONV, tc), lambda j, i: (0, j + o))
    return pl.pallas_call(
        body, name="ffn_act", grid=(nct, S // T),
        in_specs=[cur(0), prev(0), cur(nct), prev(nct), wsp(0), wsp(nct)], out_specs=cur(0),
        out_shape=jax.ShapeDtypeStruct((S, Cc), _MXU),
        compiler_params=_cparams(("parallel", "parallel")),
    )(up, up, up, up, cw, cw)


def _ffn_act_bwd(d_act, up, cw):
    S, Cc = up.shape[0], up.shape[1] // 2
    T, tc = _pick_tile(S, 256), _pick_tile(Cc, 1536)
    r8, r16 = T // 8, T // 16
    nt = S // T
    nct = Cc // tc
    K = FFN_CONV

    def body(da_ref, dan_ref, g_ref, gp_ref, gn_ref, u_ref, up_ref, un_ref, wg_ref, wu_ref,
             dg_ref, du_ref, dwg_ref, dwu_ref):
        i = pl.program_id(1)
        keep_p = jnp.where(i == 0, 0.0, 1.0)
        keep_n = jnp.where(i == nt - 1, 0.0, 1.0)
        wg, wu = wg_ref[...], wu_ref[...]
        xg = jnp.concatenate([gp_ref[8:16, :].astype(F32) * keep_p, g_ref[...].astype(F32),
                              gn_ref[0:8, :].astype(F32) * keep_n], axis=0)
        xu = jnp.concatenate([up_ref[8:16, :].astype(F32) * keep_p, u_ref[...].astype(F32),
                              un_ref[0:8, :].astype(F32) * keep_n], axis=0)
        cg = _conv_taps(xg, wg, K, T + 8)
        cu = _conv_taps(xu, wu, K, T + 8)
        da = jnp.concatenate([da_ref[...], dan_ref[...] * keep_n], axis=0)
        sg = jax.nn.sigmoid(cg)
        d_cg = da * cu * (sg * (1.0 + cg * (1.0 - sg)))
        d_cu = da * (cg * sg)
        dg_ref[...] = _conv_taps_t(d_cg, wg, K, T).astype(dg_ref.dtype)
        du_ref[...] = _conv_taps_t(d_cu, wu, K, T).astype(du_ref.dtype)

        @pl.when(i == 0)
        def _():
            dwg_ref[...] = jnp.zeros_like(dwg_ref)
            dwu_ref[...] = jnp.zeros_like(dwu_ref)

        for k in range(K):
            dwg_ref[k:k + 1, :] += jnp.sum(d_cg[0:T, :] * _shifted(xg, (K - 1) - k, 8, T), axis=0, keepdims=True)
            dwu_ref[k:k + 1, :] += jnp.sum(d_cu[0:T, :] * _shifted(xu, (K - 1) - k, 8, T), axis=0, keepdims=True)

    cur = lambda o: pl.BlockSpec((T, tc), lambda j, i: (i, j + o))
    prev = lambda o: pl.BlockSpec((16, tc), lambda j, i: (jnp.maximum(i * r16 - 1, 0), j + o))
    nxt = lambda o: pl.BlockSpec((16, tc), lambda j, i: (jnp.minimum((i + 1) * r16, S // 16 - 1), j + o))
    nxt8 = pl.BlockSpec((8, tc), lambda j, i: (jnp.minimum((i + 1) * r8, S // 8 - 1), j))
    wsp = lambda o: pl.BlockSpec((K, tc), lambda j, i: (0, j + o))
    return pl.pallas_call(
        body, name="ffn_act_bwd", grid=(nct, nt),
        in_specs=[cur(0), nxt8, cur(0), prev(0), nxt(0), cur(nct), prev(nct), nxt(nct), wsp(0), wsp(nct)],
        out_specs=(cur(0), cur(0), wsp(0), wsp(0)),
        out_shape=(jax.ShapeDtypeStruct((S, Cc), _MXU), jax.ShapeDtypeStruct((S, Cc), _MXU),
                   jax.ShapeDtypeStruct((K, Cc), F32), jax.ShapeDtypeStruct((K, Cc), F32)),
        compiler_params=_cparams(("parallel", "arbitrary")),
    )(d_act, d_act, up, up, up, up, up, up, cw, cw)


def _local_step(x, tgt, h1, n1w, n2w, fnw, gp, gnw, wp, conv_w, fcw, rest_weights, early_grads):
    proj = _mm(h1, wp, "nn", name="proj")
    u_v, w_k, q_dec, k_end, attn, tinv, g_end = _gdn_pre(proj, conv_w, gp)
    mix, states = _gdn_scan(u_v, w_k, q_dec, k_end, attn, g_end, proj, gnw)
    mix, lse = _attn_fwd(proj, mix)
    w_out, w_up4, w_down = rest_weights([mix])
    x2 = _mm(mix, w_out, "nn", residual=x, name="outproj")
    h2 = _rmsnorm_fwd(x2, n2w, "norm2")
    up = _mm(h2, w_up4, "nn", b_blocks=True, out_dtype=_MXU, name="up")
    act = _ffn_act(up, fcw)
    x3 = _mm(act, w_down, "nn", residual=x2, name="down")
    loss, dx3, dx3n, d_fnw = _loss_head(x3, fnw, tgt, "loss_head")
    d_act = _mm(dx3n, w_down, "nt", name="d_act")
    d_wdown = _mm(act, dx3n, "tn", name="d_wdown")
    d_upg, d_upu, d_fcwg, d_fcwu = _ffn_act_bwd(d_act, up, fcw)
    d_wup = _mm(h2, d_upg, "tn", place=("blocks", N_CHIPS, 0), tn=w_up4.shape[2], name="d_wgate")
    d_wup = _mm(h2, d_upu, "tn", place=("blocks", N_CHIPS, N_CHIPS // 2), tn=w_up4.shape[2], into=d_wup, name="d_wup")
    token = early_grads[0](d_wup, d_wdown)
    d_h2 = _mm_nt_blocks([d_upg, d_upu], w_up4, "d_h2", after=[token])
    dx2, d_n2w = _rmsnorm_bwd(d_h2, x2, n2w + token[0:1, 0:1], dx3, "norm2_bwd")
    token = early_grads[1](dx2)
    d_mix = _mm(dx2, w_out, "nt", name="d_mix")
    d_wout = _mm(mix, dx2, "tn", name="d_wout")
    dq_b, dk_b, dv_b = _attn_bwd(proj, mix, lse, d_mix)
    d_uv, d_wk, d_qd, d_ke, d_at, d_ge, d_z, d_gnw = _gdn_scan_bwd(u_v, w_k, q_dec, k_end, attn, g_end, proj,
                                                                   gnw + token[0:1, 0:1], states, d_mix)
    d_pre, d_ba, d_gp = _gdn_post(proj, conv_w, gp, tinv, u_v, w_k, d_uv, d_wk, d_qd, d_ke, d_at, d_ge)
    d_qkva, d_convw = _conv_bwd(d_pre, proj, 0, conv_w, GDN_CONV, "gdn_conv_bwd", 512)
    d_proj = jnp.concatenate([d_qkva, d_z.astype(_MXU), dq_b, dk_b, dv_b, d_ba.astype(_MXU),
                              jnp.zeros((x.shape[0], P_COLS - P_BA - 128), _MXU)], axis=1)
    d_wp = _mm(h1, d_proj, "tn", name="d_wp")
    token = early_grads[2](d_wp, d_wout)
    d_h1 = _mm(d_proj, wp, "nt", name="d_h1", after=[token])
    dx, d_n1w = _rmsnorm_bwd(d_h1, x, n1w + token[0:1, 0:1], dx2, "norm1_bwd")
    grads = dict(wp=d_wp, conv_w=d_convw, w_out=d_wout, w_up=d_wup, fcw_g=d_fcwg, fcw_u=d_fcwu, w_down=d_wdown,
                 n1w=d_n1w, n2w=d_n2w, fnw=d_fnw, gp=d_gp, gnw=d_gnw)
    return loss, dx, grads


_HBM = pl.BlockSpec(memory_space=pltpu.HBM)


def _pos():
    return lax.axis_index("x"), lax.axis_index("y"), lax.axis_index("c")


def _other_chips(x, y):
    return [(1 - x, y), (x, 1 - y), (1 - x, 1 - y)]


def _halvable(shape):
    return shape[0] % 32 == 0


def _rows_of_half(shape, half):
    if not _halvable(shape):
        return pl.ds(0, shape[0])
    return pl.ds(pl.multiple_of(half * (shape[0] // 2), 16), shape[0] // 2)


_SEM = pl.BlockSpec(memory_space=pltpu.SEMAPHORE)
_ANY = pl.BlockSpec(memory_space=pl.ANY)
_DATAFLOW = pltpu.SideEffectType.DATAFLOW_SIDE_EFFECTING


def _in_hbm(a):
    return pltpu.with_memory_space_constraint(a, pltpu.HBM)


def _halves_copy(src_refs, land_refs, send_sems, recv_sems, shapes, a, j, block, x, y, c):
    px, py = _other_chips(x, y)[j]
    rows = _rows_of_half(shapes[a], c)
    return pltpu.make_async_remote_copy(
        src_ref=src_refs[a].at[rows, :], dst_ref=land_refs[a].at[block, rows, :], send_sem=send_sems.at[3 * a + j],
        recv_sem=recv_sems.at[3 * a + j], device_id=(px, py, c), device_id_type=MESH)


def _gather_halves_start(shards, after, name):
    n = len(shards)
    shapes = [s.shape for s in shards]

    def body(*refs):
        ins, lands = refs[:n], refs[n:2 * n]
        send_sems, recv_sems = refs[2 * n + 1], refs[2 * n + 2]
        token = refs[-1]
        x, y, c = _pos()
        q = 2 * x + y
        for a in range(n):
            for j in range(3):
                _halves_copy(ins, lands, send_sems, recv_sems, shapes, a, j, q, x, y, c).start()
        token[...] = jnp.zeros_like(token)

    land_shapes = [(N_CHIPS,) + s.shape for s in shards]
    return pl.pallas_call(
        body, name=name,
        out_shape=(pltpu.SemaphoreType.DMA((3 * n,)), pltpu.SemaphoreType.DMA((3 * n,)),
                   *[pltpu.HBM(s.shape, s.dtype) for s in shards],
                   *[pltpu.HBM(ls, s.dtype) for ls, s in zip(land_shapes, shards)],
                   jax.ShapeDtypeStruct((8, 128), F32)),
        in_specs=[_HBM] * (2 * n) + [_ANY],
        out_specs=(_SEM, _SEM, *[_HBM] * (2 * n), pl.BlockSpec(memory_space=pltpu.VMEM)),
        input_output_aliases={a: 2 + a for a in range(2 * n)},
        compiler_params=pltpu.CompilerParams(has_side_effects=_DATAFLOW),
    )(*[_in_hbm(s) for s in shards], *[_in_hbm(lax.empty(ls, s.dtype)) for ls, s in zip(land_shapes, shards)], after)


def _gather_halves_wait(started, after, name):
    send_sems, recv_sems, *thru = started
    n = len(thru) // 2
    shapes = [t.shape for t in thru[:n]]

    def body(*refs):
        ins, lands = refs[:n], refs[n:2 * n]
        send_sems, recv_sems = refs[2 * n], refs[2 * n + 1]
        x, y, c = _pos()
        q = 2 * x + y
        chips = _other_chips(x, y)
        for a in range(n):
            for j, (px, py) in enumerate(chips):
                _halves_copy(ins, lands, send_sems, recv_sems, shapes, a, j, q, x, y, c).wait_send()
                _halves_copy(ins, lands, send_sems, recv_sems, shapes, a, j, 2 * px + py, x, y, c).wait_recv()

    outs = pl.pallas_call(
        body, name=name, out_shape=[pltpu.HBM(t.shape, t.dtype) for t in thru],
        in_specs=[_HBM] * (2 * n) + [_SEM, _SEM] + [_ANY] * len(after), out_specs=[_HBM] * (2 * n),
        input_output_aliases={a: a for a in range(2 * n)},
        compiler_params=pltpu.CompilerParams(has_side_effects=_DATAFLOW),
    )(*thru, send_sems, recv_sems, *after)
    return outs[:n], outs[n:]


def _sibling_fill(gathered, name):
    big = [a for a, g in enumerate(gathered) if _halvable(g.shape[1:])]
    n = len(gathered)

    def body(*refs):
        ins, outs = refs[:n], refs[n:2 * n]
        send_sems, recv_sems = refs[2 * n:]
        x, y, c = _pos()
        chips = _other_chips(x, y)

        def copy(k, j, half):
            a = big[k]
            px, py = chips[j]
            rows = _rows_of_half(gathered[a].shape[1:], half)
            return pltpu.make_async_remote_copy(
                src_ref=ins[a].at[2 * px + py, rows, :], dst_ref=outs[a].at[2 * px + py, rows, :],
                send_sem=send_sems.at[3 * k + j], recv_sem=recv_sems.at[3 * k + j],
                device_id=(x, y, 1 - c), device_id_type=MESH)

        sends = [copy(k, j, c) for k in range(len(big)) for j in range(3)]
        for cp in sends:
            cp.start()
        for k in range(len(big)):
            for j in range(3):
                copy(k, j, 1 - c).wait_recv()
        for cp in sends:
            cp.wait_send()

    return pl.pallas_call(
        body, name=name, in_specs=[_HBM] * n, out_specs=[_HBM] * n,
        out_shape=[jax.ShapeDtypeStruct(g.shape, g.dtype) for g in gathered],
        input_output_aliases={a: a for a in range(n)},
        scratch_shapes=[pltpu.SemaphoreType.DMA((3 * len(big),)), pltpu.SemaphoreType.DMA((3 * len(big),))],
    )(*gathered)


def _place_own(shards, gathered, cq, name):
    n = len(shards)
    steps = 4

    def body(cq_ref, *refs):
        for a in range(n):
            refs[2 * n + a][...] = refs[a][...]

    def tile(shape):
        return shape[0] // steps if _halvable(shape) else shape[0]

    in_specs = [pl.BlockSpec((tile(s.shape), s.shape[1]), (lambda i, s_: (i, 0)) if _halvable(s.shape) else (lambda i, s_: (0, 0)))
                for s in shards]
    in_specs += [pl.BlockSpec(memory_space=pl.ANY)] * n
    out_specs = [pl.BlockSpec((None, tile(s.shape), s.shape[1]),
                              (lambda i, s_: (s_[1], i, 0)) if _halvable(s.shape) else (lambda i, s_: (s_[1], 0, 0)))
                 for s in shards]
    gs = pltpu.PrefetchScalarGridSpec(num_scalar_prefetch=1, grid=(steps,), in_specs=in_specs, out_specs=out_specs)
    return pl.pallas_call(
        body, name=name, grid_spec=gs, out_shape=[jax.ShapeDtypeStruct(g.shape, g.dtype) for g in gathered],
        input_output_aliases={1 + n + a: a for a in range(n)},
        compiler_params=_cparams(("arbitrary",)),
    )(cq, *shards, *gathered)


def _half_rows(ref, c, rh):
    return ref.at[:, pl.ds(pl.multiple_of(c * rh, 8), rh), :]


def _chips_copy(src_refs, land_refs, send_sems, recv_sems, a, j, x, y, c):
    px, py = _other_chips(x, y)[j]
    return pltpu.make_async_remote_copy(src_ref=src_refs[a].at[2 * px + py], dst_ref=land_refs[a].at[j],
                                        send_sem=send_sems.at[3 * a + j], recv_sem=recv_sems.at[3 * a + j],
                                        device_id=(px, py, c), device_id_type=MESH)


def _grad_chips_start(parts, name):
    n = len(parts)

    def body(*refs):
        ins, lands = refs[:n], refs[n:2 * n]
        send_sems, recv_sems = refs[2 * n], refs[2 * n + 1]
        token = refs[-1]
        x, y, c = _pos()
        for a in range(n):
            for j in range(3):
                _chips_copy(ins, lands, send_sems, recv_sems, a, j, x, y, c).start()
        token[...] = jnp.zeros_like(token)

    land_shapes = [(3,) + p.shape[1:] for p in parts]
    return pl.pallas_call(
        body, name=name,
        out_shape=(pltpu.SemaphoreType.DMA((3 * n,)), pltpu.SemaphoreType.DMA((3 * n,)),
                   *[pltpu.HBM(p.shape, p.dtype) for p in parts],
                   *[pltpu.HBM(ls, p.dtype) for ls, p in zip(land_shapes, parts)],
                   jax.ShapeDtypeStruct((8, 128), F32)),
        in_specs=[_HBM] * (2 * n),
        out_specs=(_SEM, _SEM, *[_HBM] * (2 * n), pl.BlockSpec(memory_space=pltpu.VMEM)),
        input_output_aliases={a: 2 + a for a in range(2 * n)},
        compiler_params=pltpu.CompilerParams(has_side_effects=_DATAFLOW),
    )(*[_in_hbm(p) for p in parts], *[_in_hbm(lax.empty(ls, p.dtype)) for ls, p in zip(land_shapes, parts)])


def _grad_chips_wait(started, after, name):
    send_sems, recv_sems, *thru = started
    n = len(thru) // 2

    def body(*refs):
        ins, lands = refs[:n], refs[n:2 * n]
        send_sems, recv_sems = refs[2 * n], refs[2 * n + 1]
        x, y, c = _pos()
        for a in range(n):
            for j in range(3):
                cp = _chips_copy(ins, lands, send_sems, recv_sems, a, j, x, y, c)
                cp.wait_send()
                cp.wait_recv()

    outs = pl.pallas_call(
        body, name=name, out_shape=[pltpu.HBM(t.shape, t.dtype) for t in thru],
        in_specs=[_HBM] * (2 * n) + [_SEM, _SEM] + [_ANY] * len(after), out_specs=[_HBM] * (2 * n),
        input_output_aliases={a: a for a in range(2 * n)},
        compiler_params=pltpu.CompilerParams(has_side_effects=_DATAFLOW),
    )(*thru, send_sems, recv_sems, *after)
    return outs[n:]


def _sibling_copy(src_refs, land_refs, send_sems, recv_sems, rhs, a, c, x, y):
    return pltpu.make_async_remote_copy(src_ref=_half_rows(src_refs[a], 1 - c, rhs[a]), dst_ref=land_refs[a],
                                        send_sem=send_sems.at[a], recv_sem=recv_sems.at[a],
                                        device_id=(x, y, 1 - c), device_id_type=MESH)


def _grad_sibling_start(fams, name):
    n = len(fams)
    rhs = [f.shape[1] // 2 for f in fams]

    def body(*refs):
        ins, lands = refs[:n], refs[n:2 * n]
        send_sems, recv_sems = refs[2 * n], refs[2 * n + 1]
        token = refs[-1]
        x, y, c = _pos()
        for a in range(n):
            _sibling_copy(ins, lands, send_sems, recv_sems, rhs, a, c, x, y).start()
        token[...] = jnp.zeros_like(token)

    land_shapes = [(f.shape[0], f.shape[1] // 2, f.shape[2]) for f in fams]
    return pl.pallas_call(
        body, name=name,
        out_shape=(pltpu.SemaphoreType.DMA((n,)), pltpu.SemaphoreType.DMA((n,)),
                   *[pltpu.HBM(f.shape, f.dtype) for f in fams],
                   *[pltpu.HBM(ls, f.dtype) for ls, f in zip(land_shapes, fams)],
                   jax.ShapeDtypeStruct((8, 128), F32)),
        in_specs=[_HBM] * (2 * n),
        out_specs=(_SEM, _SEM, *[_HBM] * (2 * n), pl.BlockSpec(memory_space=pltpu.VMEM)),
        input_output_aliases={a: 2 + a for a in range(2 * n)},
        compiler_params=pltpu.CompilerParams(has_side_effects=_DATAFLOW),
    )(*[_in_hbm(f) for f in fams], *[_in_hbm(lax.empty(ls, f.dtype)) for ls, f in zip(land_shapes, fams)])


def _grad_sibling_wait(started, after, name):
    send_sems, recv_sems, *thru = started
    n = len(thru) // 2
    rhs = [t.shape[1] // 2 for t in thru[:n]]

    def body(*refs):
        ins, lands = refs[:n], refs[n:2 * n]
        send_sems, recv_sems = refs[2 * n], refs[2 * n + 1]
        x, y, c = _pos()
        for a in range(n):
            cp = _sibling_copy(ins, lands, send_sems, recv_sems, rhs, a, c, x, y)
            cp.wait_send()
            cp.wait_recv()

    outs = pl.pallas_call(
        body, name=name, out_shape=[pltpu.HBM(t.shape, t.dtype) for t in thru],
        in_specs=[_HBM] * (2 * n) + [_SEM, _SEM] + [_ANY] * len(after), out_specs=[_HBM] * (2 * n),
        input_output_aliases={a: a for a in range(2 * n)},
        compiler_params=pltpu.CompilerParams(has_side_effects=_DATAFLOW),
    )(*thru, send_sems, recv_sems, *after)
    return outs[:n], outs[n:]


def _grad_share(fulls, name, small=None):
    n = len(fulls)
    ns = 0 if small is None else 1
    rhs = [f.shape[0] // 2 for f in fulls]

    def body(*refs):
        ins, outs = refs[:n], refs[n + ns:2 * n + ns]
        send_sems, recv_sems = refs[2 * (n + ns)], refs[2 * (n + ns) + 1]
        x, y, c = _pos()

        def copy(a, half):
            rows = pl.ds(pl.multiple_of(half * rhs[a], 8), rhs[a])
            return pltpu.make_async_remote_copy(src_ref=ins[a].at[rows, :], dst_ref=outs[a].at[rows, :],
                                                send_sem=send_sems.at[7 * ns + a], recv_sem=recv_sems.at[7 * ns + a],
                                                device_id=(x, y, 1 - c), device_id_type=MESH)

        sends = [copy(a, c) for a in range(n)]
        for cp in sends:
            cp.start()
        if ns:
            small_ref, all_ref = refs[n], refs[2 * n + 1]
            me = 4 * x + 2 * y + c

            def peer(r):
                dx, dy, dc = (r >> 2) & 1, (r >> 1) & 1, r & 1
                return (x if dx == 0 else 1 - x), (y if dy == 0 else 1 - y), (c if dc == 0 else 1 - c)

            def small_copy(r, slot):
                return pltpu.make_async_remote_copy(src_ref=small_ref, dst_ref=all_ref.at[slot], send_sem=send_sems.at[r - 1],
                                                    recv_sem=recv_sems.at[r - 1], device_id=peer(r), device_id_type=MESH)

            smalls = [small_copy(r, me) for r in range(1, 8)]
            for cp in smalls:
                cp.start()
            for r in range(1, 8):
                px, py, pc = peer(r)
                small_copy(r, 4 * px + 2 * py + pc).wait_recv()
            sends = sends + smalls
        for a in range(n):
            copy(a, 1 - c).wait_recv()
        for cp in sends:
            cp.wait_send()

    return pl.pallas_call(
        body, name=name, in_specs=[_HBM] * (n + ns), out_specs=[_HBM] * (n + ns),
        out_shape=[jax.ShapeDtypeStruct(f.shape, f.dtype) for f in fulls]
        + ([jax.ShapeDtypeStruct((8,) + small.shape, small.dtype)] if ns else []),
        input_output_aliases={a: a for a in range(n)},
        scratch_shapes=[pltpu.SemaphoreType.DMA((7 * ns + n,)), pltpu.SemaphoreType.DMA((7 * ns + n,))],
    )(*fulls, *([small] if ns else []))


def _add_sibling(own, recv, cq, name):
    nb, R, Cc = own.shape
    Rh = R // 2

    def body(cq_ref, a_ref, b_ref, o32_ref, o16_ref):
        s = a_ref[...] + b_ref[...]
        o32_ref[...] = s
        o16_ref[...] = s.astype(o16_ref.dtype)

    sp = pl.BlockSpec((1, Rh, Cc), lambda b, s: (b, 0, 0))
    gs = pltpu.PrefetchScalarGridSpec(
        num_scalar_prefetch=1, grid=(nb,),
        in_specs=[pl.BlockSpec((1, Rh, Cc), lambda b, s: (b, s[0], 0)), sp], out_specs=[sp, sp])
    return pl.pallas_call(
        body, name=name, grid_spec=gs,
        out_shape=[jax.ShapeDtypeStruct((nb, Rh, Cc), F32), jax.ShapeDtypeStruct((nb, Rh, Cc), _MXU)],
        compiler_params=_cparams(("parallel",)),
    )(cq, own, recv)


def _add_chips(part32, recv3, cq, name):
    nb, Rh, Cc = part32.shape

    def body(cq_ref, a_ref, b_ref, o_ref):
        acc = a_ref[0]
        for j in range(3):
            acc = acc + b_ref[j].astype(F32)
        o_ref[...] = acc

    gs = pltpu.PrefetchScalarGridSpec(
        num_scalar_prefetch=1, grid=(1,),
        in_specs=[pl.BlockSpec((1, Rh, Cc), lambda i, s: (s[1], 0, 0)), pl.BlockSpec((3, Rh, Cc), lambda i, s: (0, 0, 0))],
        out_specs=pl.BlockSpec((Rh, Cc), lambda i, s: (s[0], 0)))
    return pl.pallas_call(
        body, name=name, grid_spec=gs, out_shape=jax.ShapeDtypeStruct((2 * Rh, Cc), F32),
        compiler_params=_cparams(("arbitrary",)),
    )(cq, part32, recv3)


def _sum_devices(small_all, small, me):
    def body(me_ref, all_ref, own_ref, o_ref):
        tot = None
        for d in range(8):
            term = jnp.where(me_ref[0] == d, own_ref[...], all_ref[d])
            tot = term if tot is None else tot + term
        o_ref[...] = tot

    gs = pltpu.PrefetchScalarGridSpec(
        num_scalar_prefetch=1, grid=(1,),
        in_specs=[pl.BlockSpec(small_all.shape, lambda i, s: (0, 0, 0)), pl.BlockSpec(small.shape, lambda i, s: (0, 0))],
        out_specs=pl.BlockSpec(small.shape, lambda i, s: (0, 0)))
    return pl.pallas_call(body, name="sum_devices", grid_spec=gs,
                          out_shape=jax.ShapeDtypeStruct(small.shape, F32))(me, small_all, small)


def _adamw(w, g, m, v, name):
    R, Cc = w.shape
    T = max([t for t in range(8, 257, 8) if R % t == 0], default=R)
    c1 = 1.0 / (1.0 - ADAM_B1 ** ADAM_STEP)
    c2 = 1.0 / (1.0 - ADAM_B2 ** ADAM_STEP)

    def body(w_ref, g_ref, m_ref, v_ref, d_ref, mo_ref, vo_ref):
        gv = g_ref[...]
        mn = ADAM_B1 * m_ref[...] + (1.0 - ADAM_B1) * gv
        vn = ADAM_B2 * v_ref[...] + (1.0 - ADAM_B2) * (gv * gv)
        mo_ref[...] = mn
        vo_ref[...] = vn
        d_ref[...] = -ADAM_LR * ((mn * c1) / (jnp.sqrt(vn * c2) + ADAM_EPS) + ADAM_WD * w_ref[...])

    sp = pl.BlockSpec((T, Cc), lambda i: (i, 0))
    sh = jax.ShapeDtypeStruct((R, Cc), F32)
    return pl.pallas_call(
        body, name=name, grid=(R // T,), in_specs=[sp] * 4, out_specs=(sp, sp, sp), out_shape=(sh, sh, sh),
        compiler_params=_cparams(("parallel",)),
    )(w, g, m, v)


SMALL_ROWS = 32
REPL_ROWS = 8


def _pad_lanes(v, n=D_MODEL):
    return jnp.pad(v, ((0, 0), (0, n - v.shape[1])))


def kernel(x, norm1_w, w_in, conv_qkv_w, a_log, dt_bias, gdn_norm_w, w_out, norm2_w, w_up, ffn_conv_w, w_down, final_norm_w, loss_target, m_norm1_w, m_w_in, m_conv_qkv_w, m_a_log, m_dt_bias, m_gdn_norm_w, m_w_out, m_norm2_w, m_w_up, m_ffn_conv_w, m_w_down, m_final_norm_w, v_norm1_w, v_w_in, v_conv_qkv_w, v_a_log, v_dt_bias, v_gdn_norm_w, v_w_out, v_norm2_w, v_w_up, v_ffn_conv_w, v_w_down, v_final_norm_w):
    c = lax.axis_index("c")
    q = 2 * lax.axis_index("x") + lax.axis_index("y")
    S = x.shape[1]
    cq = jnp.stack([c, q]).astype(jnp.int32)

    *in_started, in_token = _gather_halves_start([w_in[0].astype(_MXU), conv_qkv_w[0], ffn_conv_w[0]], x, "gather_in_start")
    w_in_l, m_w_in_l, v_w_in_l = (a + in_token[0:1, 0:1] for a in (w_in, m_w_in, v_w_in))
    h1 = _rmsnorm_fwd(x[0], norm1_w + in_token[0:1, 0:1], "norm1")
    rest = [(a[0] + in_token[0:1, 0:1]).astype(_MXU) for a in (w_out, w_up, w_down)]
    in_shards, got_in = _gather_halves_wait(in_started, [w_in_l, m_w_in_l, v_w_in_l, h1, *rest], "gather_in_wait")
    g_in, g_conv, g_fconv = _place_own(in_shards, _sibling_fill(got_in, "fill_in"), cq, "place_in")
    *rest_started, token = _gather_halves_start(rest, g_conv, "gather_rest_start")

    def rest_weights(after):
        shards, got = _gather_halves_wait(rest_started, after, "gather_rest_wait")
        got = _sibling_fill(got, "fill_rest")
        g_out, g_up, g_down = _place_own(shards, got, cq, "place_rest")
        return g_out.reshape(D_MODEL, D_MODEL), g_up, g_down.reshape(D_FF, D_MODEL)
    wp = _wp_assemble(g_in, [token])
    conv_f = jnp.concatenate([g_conv[i] for i in range(N_CHIPS)], axis=1)
    fcw = jnp.concatenate([g_fconv[i] for i in range(N_CHIPS)], axis=1)
    gp = _pad_lanes(jnp.concatenate([a_log, dt_bias], axis=1), 128)
    fnw = final_norm_w[None, :]
    early = {}

    def early_sibling(d_wup, d_wdown):
        *early["sibling"], tok = _grad_sibling_start([d_wup, d_wdown.reshape(N_CHIPS, D_FF // N_CHIPS, D_MODEL)],
                                                     "grad_sibling_early_start")
        return tok

    def early_chips(dx2):
        fams_e, got_e = _grad_sibling_wait(early["sibling"], [dx2], "grad_sibling_early_wait")
        early["parts"] = [_add_sibling(f, r, cq, "add_sibling_" + nm) for f, r, nm in zip(fams_e, got_e, ("w_up", "w_down"))]
        *early["started"], tok = _grad_chips_start([p[1] for p in early["parts"]], "grad_chips_start")
        return tok

    def late_sibling(d_wp, d_wout):
        *early["late_sibling"], tok = _grad_sibling_start(
            [_win_split(d_wp), d_wout.reshape(N_CHIPS, D_MODEL // N_CHIPS, D_MODEL)], "grad_sibling_late_start")
        return tok

    early_grads = (early_sibling, early_chips, late_sibling)

    loss_l, dx, g = _local_step(x[0], loss_target[0], h1, norm1_w, norm2_w + token[0:1, 0:1], fnw, gp, gdn_norm_w, wp,
                                conv_f, fcw, rest_weights, early_grads)
    n_fc = FFN_CONV * D_FF

    def rows_of(v):
        flat = v.reshape(-1)
        return jnp.pad(flat, (0, -flat.shape[0] % D_MODEL)).reshape(-1, D_MODEL)

    gp_row = _pad_lanes(jnp.concatenate([g["gp"][0:1, 0:8], loss_l[0:1, 0:1]], axis=1))
    small = jnp.concatenate([g["n1w"], g["n2w"], g["fnw"], gp_row, _pad_lanes(g["gnw"]),
                             rows_of(g["conv_w"]), rows_of(g["fcw_g"]), rows_of(g["fcw_u"])], axis=0)
    small = jnp.pad(small, ((0, SMALL_ROWS - small.shape[0]), (0, 0)))
    fams, got = _grad_sibling_wait(early["late_sibling"], [dx], "grad_sibling_late_wait")
    parts = [_add_sibling(f, r, cq, "add_sibling_" + nm) for f, r, nm in zip(fams, got, ("w_in", "w_out"))]
    *late_started, late_token = _grad_chips_start([p[1] for p in parts], "grad_chips_late_start")
    got3_e = _grad_chips_wait(early["started"], [dx, g["wp"], late_token], "grad_chips_wait")
    g_w_up, g_w_down = _grad_share(
        [_add_chips(p[0], r3, cq, "add_chips_" + nm) for p, r3, nm in zip(early["parts"], got3_e, ("w_up", "w_down"))],
        "grad_share_early")
    big = {}

    def adamw_big(nm, w, gg, m, v):
        d_, m_, v_ = _adamw(w[0], gg, m[0], v[0], "adamw_" + nm)
        big[nm] = (gg[None], d_[None], m_[None], v_[None])

    adamw_big("w_up", w_up, g_w_up, m_w_up, v_w_up)
    adamw_big("w_down", w_down, g_w_down, m_w_down, v_w_down)
    got3 = _grad_chips_wait(late_started, [big["w_up"][1], big["w_down"][1]], "grad_chips_late_wait")
    g_w_in, g_w_out, small_all = _grad_share(
        [_add_chips(p[0], r3, cq, "add_chips_" + nm) for p, r3, nm in zip(parts, got3, ("w_in", "w_out"))],
        "grad_share_late", small)
    small_red = _sum_devices(small_all, small, (2 * q + c).astype(jnp.int32).reshape(1))
    loss = small_red[3, 8]
    r0 = 5
    r1 = r0 + GDN_CONV * 3 * GDN_WIDTH // D_MODEL
    r2 = r1 + -(-n_fc // D_MODEL)
    conv_red = small_red[r0:r1].reshape(GDN_CONV, 3 * GDN_WIDTH)
    fc_red = jnp.concatenate([small_red[r1:r2].reshape(-1)[:n_fc].reshape(FFN_CONV, D_FF),
                              small_red[r2:2 * r2 - r1].reshape(-1)[:n_fc].reshape(FFN_CONV, D_FF)], axis=1)
    g_conv_w = lax.dynamic_slice_in_dim(conv_red, q * (3 * GDN_WIDTH // N_CHIPS), 3 * GDN_WIDTH // N_CHIPS, axis=1)
    g_fconv_w = lax.dynamic_slice_in_dim(fc_red, q * (2 * D_FF // N_CHIPS), 2 * D_FF // N_CHIPS, axis=1)
    g_n1w, g_n2w, g_fnw = small_red[0:1], small_red[1:2], small_red[2]
    g_alog, g_dtb, g_gnw = small_red[3:4, 0:4], small_red[3:4, 4:8], small_red[4:5, 0:128]
    for nm, w, gg, m, v in (("w_in", w_in_l, g_w_in, m_w_in_l, v_w_in_l), ("conv_qkv_w", conv_qkv_w, g_conv_w, m_conv_qkv_w, v_conv_qkv_w),
                            ("w_out", w_out, g_w_out, m_w_out, v_w_out),
                            ("ffn_conv_w", ffn_conv_w, g_fconv_w, m_ffn_conv_w, v_ffn_conv_w)):
        adamw_big(nm, w, gg, m, v)

    def pack_small(n1, n2, fn, al, db, gn):
        return jnp.concatenate([n1, n2, fn[None, :], _pad_lanes(jnp.concatenate([al, db], axis=1)), _pad_lanes(gn),
                                jnp.zeros((REPL_ROWS - 5, D_MODEL), F32)], axis=0)

    sw = pack_small(norm1_w, norm2_w, final_norm_w, a_log, dt_bias, gdn_norm_w)
    sm = pack_small(m_norm1_w, m_norm2_w, m_final_norm_w, m_a_log, m_dt_bias, m_gdn_norm_w)
    sv = pack_small(v_norm1_w, v_norm2_w, v_final_norm_w, v_a_log, v_dt_bias, v_gdn_norm_w)
    sd, smn, svn = _adamw(sw, small_red[:REPL_ROWS], sm, sv, "adamw_small")

    def unpack_small(t):
        return dict(norm1_w=t[0:1], norm2_w=t[1:2], final_norm_w=t[2], a_log=t[3:4, 0:4], dt_bias=t[3:4, 4:8],
                    gdn_norm_w=t[4:5, 0:128])

    sg = dict(norm1_w=g_n1w, norm2_w=g_n2w, final_norm_w=g_fnw, a_log=g_alog, dt_bias=g_dtb, gdn_norm_w=g_gnw)
    sd, smn, svn = unpack_small(sd), unpack_small(smn), unpack_small(svn)
    names = ["norm1_w", "w_in", "conv_qkv_w", "a_log", "dt_bias", "gdn_norm_w", "w_out", "norm2_w", "w_up",
             "ffn_conv_w", "w_down", "final_norm_w"]
    grads = [big[n][0] if n in big else sg[n] for n in names]
    deltas = [big[n][1] if n in big else sd[n] for n in names]
    new_m = [big[n][2] if n in big else smn[n] for n in names]
    new_v = [big[n][3] if n in big else svn[n] for n in names]
    return (loss, dx[None], *grads, *deltas, *new_m, *new_v)
```

```python
import functools
import math

import numpy as np
import jax
import jax.numpy as jnp
from jax import lax
from jax.experimental import pallas as pl
from jax.experimental.pallas import tpu as pltpu

F32 = jnp.float32
BF16 = jnp.bfloat16
_MXU = jnp.bfloat16
_HI = lax.Precision.HIGHEST
EPS = 1e-6
V7X_VMEM_LIMIT = 56 * 1024 * 1024
MESH = pl.DeviceIdType.MESH

D_MODEL = 1024
GDN_HEADS, GDN_DIM, GDN_CHUNK, GDN_CONV = 4, 128, 64, 4
GDN_WIDTH = GDN_HEADS * GDN_DIM
DIL_HEADS, DIL_DIM = 8, 64
DIL_WIDTH = DIL_HEADS * DIL_DIM
D_FF, FFN_CONV = 2816, 3
IN_COLS = 3592
P_COLS = 3840
P_Z, P_QKVB, P_BA = 1536, 2048, 3584
ATT_T = 1024
ADAM_LR, ADAM_B1, ADAM_B2, ADAM_EPS, ADAM_WD, ADAM_STEP = 0.001, 0.9, 0.999, 1e-08, 0.01, 10
N_CHIPS = 4


def _cparams(sem=None, vmem=None):
    kw = {}
    if sem is not None:
        kw["dimension_semantics"] = sem
    if vmem is not None:
        kw["vmem_limit_bytes"] = vmem
    return pltpu.CompilerParams(**kw)


def _silu(x):
    return x * jax.nn.sigmoid(x)


def _pick_tile(n, cap):
    best = None
    for t in range(128, min(n, cap) + 1, 128):
        if n % t == 0:
            best = t
    return best or n


def _mm(a, b, mode, *, out_dtype=F32, residual=None, name, b_blocks=False, place=None, into=None, tn=None, after=()):
    if mode == "nn":
        M, K = a.shape
        N = b.shape[0] * b.shape[2] if b_blocks else b.shape[1]
    elif mode == "nt":
        (M, K), (N, _) = a.shape, b.shape
    else:
        (K, M), (_, N) = a.shape, b.shape
    tm = _pick_tile(M, 1024)
    tn = b.shape[2] if b_blocks else (tn or _pick_tile(N, 1536))

    def vmem(tm, tn):
        return 2 * (tm * K * a.dtype.itemsize + tn * K * b.dtype.itemsize
                    + tm * tn * (jnp.dtype(out_dtype).itemsize + (4 if residual is not None else 0))) + 3 * tm * tn * 4

    fixed_tn = b_blocks or (place is not None and place[0] == "blocks")
    while vmem(tm, tn) > 40 * 1024 * 1024:
        if (tm >= tn or fixed_tn) and tm % 256 == 0:
            tm //= 2
        elif tn % 256 == 0 and not fixed_tn:
            tn //= 2
        else:
            tm //= 2
    a_spec = pl.BlockSpec((K, tm), lambda j, i: (0, i)) if mode == "tn" else pl.BlockSpec((tm, K), lambda j, i: (i, 0))
    if b_blocks:
        b_spec = pl.BlockSpec((None, K, tn), lambda j, i: (j, 0, 0))
    else:
        b_spec = pl.BlockSpec((tn, K), lambda j, i: (j, 0)) if mode == "nt" else pl.BlockSpec((K, tn), lambda j, i: (0, j))
    r_spec = pl.BlockSpec((tm, tn), lambda j, i: (i, j))
    if place is None:
        o_spec, o_shape = r_spec, (M, N)
    elif place[0] == "rows":
        off = place[2] // tm
        o_spec, o_shape = pl.BlockSpec((tm, tn), lambda j, i: (i + off, j)), (place[1], N)
    else:
        off = place[2]
        o_spec, o_shape = pl.BlockSpec((None, tm, tn), lambda j, i: (j + off, i, 0)), (place[1], M, tn)
    dims = {"nn": (((1,), (0,)), ((), ())), "nt": (((1,), (1,)), ((), ())), "tn": (((0,), (0,)), ((), ()))}[mode]

    def body(*refs):
        a_ref, b_ref = refs[0], refs[1]
        o_ref = refs[-1]
        acc = lax.dot_general(a_ref[...].astype(_MXU), b_ref[...].astype(_MXU), dims, preferred_element_type=F32)
        if residual is not None:
            acc = acc + refs[2][...]
        o_ref[...] = acc.astype(out_dtype)

    ins, specs, alias = [a, b], [a_spec, b_spec], {}
    if residual is not None:
        ins.append(residual)
        specs.append(r_spec)
    if into is not None:
        alias = {len(ins): 0}
        ins.append(into)
        specs.append(pl.BlockSpec(memory_space=pl.ANY))
    ins += list(after)
    specs += [pl.BlockSpec(memory_space=pl.ANY)] * len(after)
    return pl.pallas_call(
        body, name=name, grid=(N // tn, M // tm), in_specs=specs, out_specs=o_spec,
        out_shape=jax.ShapeDtypeStruct(o_shape, out_dtype), input_output_aliases=alias,
        compiler_params=_cparams(("parallel", "parallel"), V7X_VMEM_LIMIT),
    )(*ins)


def _mm_nt_blocks(a_list, b4, name, after=()):
    M = a_list[0].shape[0]
    nb, N, Kb = b4.shape
    tm, tn = _pick_tile(M, 1024), _pick_tile(N, 512)

    def body(a0_ref, a1_ref, b_ref, *rest):
        o_ref = rest[-1]
        acc = None
        for blk in range(nb):
            a_ref = (a0_ref, a1_ref)[blk // 2]
            lo = (blk % 2) * Kb
            t = lax.dot_general(a_ref[:, lo:lo + Kb].astype(_MXU), b_ref[blk].astype(_MXU), (((1,), (1,)), ((), ())),
                                preferred_element_type=F32)
            acc = t if acc is None else acc + t
        o_ref[...] = acc

    a_spec = pl.BlockSpec((tm, 2 * Kb), lambda j, i: (i, 0))
    return pl.pallas_call(
        body, name=name, grid=(N // tn, M // tm),
        in_specs=[a_spec, a_spec, pl.BlockSpec((nb, tn, Kb), lambda j, i: (0, j, 0))]
        + [pl.BlockSpec(memory_space=pl.ANY)] * len(after),
        out_specs=pl.BlockSpec((tm, tn), lambda j, i: (i, j)), out_shape=jax.ShapeDtypeStruct((M, N), F32),
        compiler_params=_cparams(("parallel", "parallel"), V7X_VMEM_LIMIT),
    )(a_list[0], a_list[1], b4, *after)


def _wp_assemble(g_in, after=()):
    nb, Dm, Wb = g_in.shape
    T = 256
    n_lo = P_QKVB - 2 * Wb

    def body(g_ref, *rest):
        g2 = g_ref[2]
        rest[-1][...] = jnp.concatenate(
            [g_ref[0], g_ref[1], g2[:, :n_lo], g2[:, n_lo + 8:], g_ref[3], g2[:, n_lo:n_lo + 8],
             jnp.zeros((T, P_COLS - P_BA - 8), g_in.dtype)], axis=1)

    return pl.pallas_call(
        body, name="wp_assemble", grid=(Dm // T,),
        in_specs=[pl.BlockSpec((nb, T, Wb), lambda i: (0, i, 0))] + [pl.BlockSpec(memory_space=pl.ANY)] * len(after),
        out_specs=pl.BlockSpec((T, P_COLS), lambda i: (i, 0)), out_shape=jax.ShapeDtypeStruct((Dm, P_COLS), g_in.dtype),
        compiler_params=_cparams(("parallel",)),
    )(g_in, *after)


def _win_split(d_wp):
    Dm = d_wp.shape[0]
    Wb = IN_COLS // N_CHIPS
    T = 256

    def body(x_ref, o_ref):
        xv = x_ref[...]
        o_ref[0] = xv[:, 0:Wb]
        o_ref[1] = xv[:, Wb:2 * Wb]
        o_ref[2] = jnp.concatenate([xv[:, 2 * Wb:P_QKVB], xv[:, P_BA:P_BA + 8], xv[:, P_QKVB:3 * Wb - 8]], axis=1)
        o_ref[3] = xv[:, 3 * Wb - 8:P_BA]

    return pl.pallas_call(
        body, name="win_split", grid=(Dm // T,), in_specs=[pl.BlockSpec((T, P_COLS), lambda i: (i, 0))],
        out_specs=pl.BlockSpec((N_CHIPS, T, Wb), lambda i: (0, i, 0)),
        out_shape=jax.ShapeDtypeStruct((N_CHIPS, Dm, Wb), F32), compiler_params=_cparams(("parallel",)),
    )(d_wp)


def _rmsnorm_fwd(x, w, name):
    S, D = x.shape
    T = _pick_tile(S, 512)

    def body(x_ref, w_ref, o_ref):
        xv = x_ref[...]
        rs = lax.rsqrt(jnp.mean(xv * xv, axis=-1, keepdims=True) + EPS)
        o_ref[...] = (xv * rs * w_ref[...]).astype(o_ref.dtype)

    return pl.pallas_call(
        body, name=name, grid=(S // T,),
        in_specs=[pl.BlockSpec((T, D), lambda i: (i, 0)), pl.BlockSpec((1, D), lambda i: (0, 0))],
        out_specs=pl.BlockSpec((T, D), lambda i: (i, 0)),
        out_shape=jax.ShapeDtypeStruct((S, D), _MXU),
        compiler_params=_cparams(("parallel",)),
    )(x, w)


def _rmsnorm_bwd(dh, x, w, dres, name):
    S, D = x.shape
    T = _pick_tile(S, 512)

    def body(dh_ref, x_ref, w_ref, dres_ref, dx_ref, dw_ref):
        xv = x_ref[...]
        rs = lax.rsqrt(jnp.mean(xv * xv, axis=-1, keepdims=True) + EPS)
        xn = xv * rs
        dhv = dh_ref[...]
        dxn = dhv * w_ref[...]
        dx_ref[...] = dres_ref[...] + rs * (dxn - xn * jnp.mean(dxn * xn, axis=-1, keepdims=True))

        @pl.when(pl.program_id(0) == 0)
        def _():
            dw_ref[...] = jnp.zeros_like(dw_ref)

        dw_ref[...] += jnp.sum(dhv * xn, axis=0, keepdims=True)

    row = pl.BlockSpec((T, D), lambda i: (i, 0))
    vec = pl.BlockSpec((1, D), lambda i: (0, 0))
    return pl.pallas_call(
        body, name=name, grid=(S // T,), in_specs=[row, row, vec, row], out_specs=(row, vec),
        out_shape=(jax.ShapeDtypeStruct((S, D), F32), jax.ShapeDtypeStruct((1, D), F32)),
        compiler_params=_cparams(("arbitrary",)),
    )(dh, x, w, dres)


def _loss_head(x3, w, tgt, name):
    S, D = x3.shape
    T = _pick_tile(S, 512)

    def body(x_ref, w_ref, t_ref, loss_ref, dx_ref, dxn_ref, dw_ref):
        xv = x_ref[...]
        rs = lax.rsqrt(jnp.mean(xv * xv, axis=-1, keepdims=True) + EPS)
        xn = xv * rs
        err = xn * w_ref[...] - t_ref[...]
        dy = err * (1.0 / D)
        dxn = dy * w_ref[...]
        dxv = rs * (dxn - xn * jnp.mean(dxn * xn, axis=-1, keepdims=True))
        dx_ref[...] = dxv
        dxn_ref[...] = dxv.astype(dxn_ref.dtype)

        @pl.when(pl.program_id(0) == 0)
        def _():
            dw_ref[...] = jnp.zeros_like(dw_ref)
            loss_ref[...] = jnp.zeros_like(loss_ref)

        dw_ref[...] += jnp.sum(dy * xn, axis=0, keepdims=True)
        part = jnp.sum(jnp.sum(err * err, axis=-1, keepdims=True), axis=0, keepdims=True) * (0.5 / D)
        loss_ref[...] += jnp.broadcast_to(part, loss_ref.shape)

    row = pl.BlockSpec((T, D), lambda i: (i, 0))
    vec = pl.BlockSpec((1, D), lambda i: (0, 0))
    return pl.pallas_call(
        body, name=name, grid=(S // T,), in_specs=[row, vec, row],
        out_specs=(pl.BlockSpec((8, 128), lambda i: (0, 0)), row, row, vec),
        out_shape=(jax.ShapeDtypeStruct((8, 128), F32), jax.ShapeDtypeStruct((S, D), F32), jax.ShapeDtypeStruct((S, D), _MXU),
                   jax.ShapeDtypeStruct((1, D), F32)),
        compiler_params=_cparams(("arbitrary",)),
    )(x3, w, tgt)


def _shifted(ext, back, lo, n):
    if back == 0:
        return ext[lo:lo + n, :]
    return pltpu.roll(ext, back % ext.shape[0], 0)[lo:lo + n, :]


def _conv_windows(ext, K, T):
    return [_shifted(ext, (K - 1) - i, 8, T) for i in range(K)]


def _conv_taps(ext, w, K, T):
    out = None
    for i, win in enumerate(_conv_windows(ext, K, T)):
        term = win * w[i:i + 1, :]
        out = term if out is None else out + term
    return out


def _conv_taps_t(ext, w, K, T):
    out = None
    for i in range(K):
        term = _shifted(ext, i - (K - 1), 0, T) * w[i:i + 1, :]
        out = term if out is None else out + term
    return out


def _tri_masks(C):
    r = lax.broadcasted_iota(jnp.int32, (C, C), 0)
    c = lax.broadcasted_iota(jnp.int32, (C, C), 1)
    return r == c, r >= c, r > c, r <= c


_NN, _NT, _TN = ((1,), (0,)), ((1,), (1,)), ((0,), (0,))
_GDN_PASSES = dict(qk=1, inv=1, sol=1, scan=1, bwd=1)


def _bdot_raw(a, b, kind, passes):
    dims = ({"NN": ((2,), (1,)), "NT": ((2,), (2,)), "TN": ((1,), (1,))}[kind], ((0,), (0,)))
    if passes == 0:
        return lax.dot_general(a, b, dims, precision=_HI, preferred_element_type=F32)
    ah, bh = a.astype(BF16), b.astype(BF16)
    out = lax.dot_general(ah, bh, dims, preferred_element_type=F32)
    if passes == 3:
        al, bl = (a - ah.astype(F32)).astype(BF16), (b - bh.astype(F32)).astype(BF16)
        out = out + lax.dot_general(ah, bl, dims, preferred_element_type=F32) + lax.dot_general(al, bh, dims, preferred_element_type=F32)
    return out


@functools.partial(jax.custom_vjp, nondiff_argnums=(2, 3))
def _bdot(a, b, kind, passes):
    return _bdot_raw(a, b, kind, passes)


def _bdot_fwd(a, b, kind, passes):
    return _bdot_raw(a, b, kind, passes), (a, b)


def _bdot_bwd(kind, passes, res, ct):
    a, b = res
    if kind == "NN":
        return _bdot_raw(ct, b, "NT", passes), _bdot_raw(a, ct, "TN", passes)
    if kind == "NT":
        return _bdot_raw(ct, b, "NN", passes), _bdot_raw(ct, a, "TN", passes)
    return _bdot_raw(b, ct, "NT", passes), _bdot_raw(a, ct, "NN", passes)


_bdot.defvjp(_bdot_fwd, _bdot_bwd)


def _softplus(x):
    return jnp.maximum(x, 0.0) + jnp.log(1.0 + jnp.exp(-jnp.abs(x)))


def _gdn_stage1(cq, ck, cv, b_col, a_col, alog, dtb, dot=_bdot_raw):
    C = cq.shape[1]
    eye, incl, strict, incl_t = _tri_masks(C)
    qn = cq * lax.rsqrt(jnp.sum(cq * cq, axis=-1, keepdims=True) + EPS) * (GDN_DIM ** -0.5)
    kn = ck * lax.rsqrt(jnp.sum(ck * ck, axis=-1, keepdims=True) + EPS)
    beta = jax.nn.sigmoid(b_col)
    g = -jnp.exp(alog) * _softplus(a_col + dtb)
    g_row = jnp.sum(jnp.where(eye, g, 0.0), axis=1, keepdims=True)
    beta_row = jnp.sum(jnp.where(eye, beta, 0.0), axis=1, keepdims=True)
    gc_col = jnp.sum(jnp.where(incl, g_row, 0.0), axis=2, keepdims=True)
    gc_row = jnp.sum(jnp.where(incl_t, g, 0.0), axis=1, keepdims=True)
    dec = jnp.where(incl, jnp.exp(jnp.where(incl, gc_col - gc_row, 0.0)), 0.0)
    kk = dot(kn, kn, "NT", _GDN_PASSES["qk"])
    qk = dot(qn, kn, "NT", _GDN_PASSES["qk"])
    lmat = jnp.where(strict, dec * kk * beta_row, 0.0)
    attn = dec * qk * beta_row
    gam = jnp.exp(gc_col)
    gc_last = gc_col[:, C - 1:C, :]
    k_end = kn * (jnp.exp(gc_last - gc_col) * beta)
    return lmat, cv, gam * kn, gam * qn, attn, k_end, jnp.exp(gc_last)


def _tri_inv(lmat):
    C = lmat.shape[1]
    eye = _tri_masks(C)[0]
    ps = _GDN_PASSES["inv"]
    p = jnp.where(eye, 1.0, 0.0) - lmat
    lp = _bdot_raw(lmat, lmat, "NN", ps)
    n = int(math.log2(C))
    for s in range(1, n):
        p = p + _bdot_raw(p, lp, "NN", ps)
        if s < n - 1:
            lp = _bdot_raw(lp, lp, "NN", ps)
    return p


def _gated_norm(o, z, gnw):
    on = o * lax.rsqrt(jnp.mean(o * o, axis=-1, keepdims=True) + EPS) * gnw
    return on * _silu(z)


GDN_PG = 2
GDN_SG = 4


def _gdn_pairs(c, ba, gp, G):
    C, W, H = GDN_CHUNK, GDN_WIDTH, GDN_HEADS
    pairs = [(j, h) for j in range(G) for h in range(H)]
    cq, ck, cv = (jnp.stack([c[C * j:C * (j + 1), o + GDN_DIM * h:o + GDN_DIM * (h + 1)] for j, h in pairs]) for o in (0, W, 2 * W))
    b_col = jnp.stack([ba[C * j:C * (j + 1), h:h + 1] for j, h in pairs])
    a_col = jnp.stack([ba[C * j:C * (j + 1), H + h:H + h + 1] for j, h in pairs])
    alog = jnp.stack([gp[0:1, h:h + 1] for j, h in pairs])
    dtb = jnp.stack([gp[0:1, H + h:H + h + 1] for j, h in pairs])
    return pairs, (cq, ck, cv, b_col, a_col, alog, dtb)


def _gdn_pre_specs(S, G):
    C = GDN_CHUNK
    T = C * G
    return dict(
        cur=pl.BlockSpec((T, 3 * GDN_WIDTH), lambda i: (i, 0)),
        prev=pl.BlockSpec((8, 3 * GDN_WIDTH), lambda i: (jnp.maximum(i * (T // 8) - 1, 0), 0)),
        ba=pl.BlockSpec((T, 128), lambda i: (i, P_BA // 128)),
        cw=pl.BlockSpec((GDN_CONV, 3 * GDN_WIDTH), lambda i: (0, 0)),
        vec=pl.BlockSpec((1, 128), lambda i: (0, 0)),
        hd=pl.BlockSpec((GDN_HEADS, T, GDN_DIM), lambda i: (0, i, 0)),
        hc=pl.BlockSpec((GDN_HEADS, T, C), lambda i: (0, i, 0)),
        ge=pl.BlockSpec((G, GDN_HEADS, 8, 128), lambda i: (i, 0, 0, 0)),
    )


def _hd_shape(S, last=GDN_DIM):
    return jax.ShapeDtypeStruct((GDN_HEADS, S, last), F32)


def _gdn_pre(proj, conv_w, gp):
    S = proj.shape[0]
    C, G = GDN_CHUNK, GDN_PG
    nc = S // C
    sp = _gdn_pre_specs(S, G)

    def body(cur_ref, prev_ref, ba_ref, cw_ref, gp_ref, uv_ref, wk_ref, qd_ref, ke_ref, at_ref, ti_ref, ge_ref):
        prev = prev_ref[...] * jnp.where(pl.program_id(0) == 0, 0.0, 1.0)
        c = _silu(_conv_taps(jnp.concatenate([prev, cur_ref[...]], axis=0), cw_ref[...], GDN_CONV, C * G))
        pairs, args = _gdn_pairs(c, ba_ref[...], gp_ref[...], G)
        lmat, v, rk, q_dec, attn, k_end, g_end = _gdn_stage1(*args)
        t = _tri_inv(lmat)
        u_v = _bdot_raw(t, v, "NN", _GDN_PASSES["sol"])
        w_k = _bdot_raw(t, rk, "NN", _GDN_PASSES["sol"])
        for b, (j, h) in enumerate(pairs):
            rows = slice(C * j, C * (j + 1))
            uv_ref[h, rows, :] = u_v[b]
            wk_ref[h, rows, :] = w_k[b]
            qd_ref[h, rows, :] = q_dec[b]
            ke_ref[h, rows, :] = k_end[b]
            at_ref[h, rows, :] = attn[b]
            ti_ref[h, rows, :] = t[b]
            ge_ref[j, h] = jnp.broadcast_to(g_end[b], (8, 128))

    return pl.pallas_call(
        body, name="gdn_pre", grid=(nc // G,),
        in_specs=[sp["cur"], sp["prev"], sp["ba"], sp["cw"], sp["vec"]],
        out_specs=(sp["hd"], sp["hd"], sp["hd"], sp["hd"], sp["hc"], sp["hc"], sp["ge"]),
        out_shape=(_hd_shape(S), _hd_shape(S), _hd_shape(S), _hd_shape(S), _hd_shape(S, C), _hd_shape(S, C),
                   jax.ShapeDtypeStruct((nc, GDN_HEADS, 8, 128), F32)),
        compiler_params=_cparams(("parallel",)),
    )(proj, proj, proj, conv_w, gp)


def _gdn_scan_specs(S, G, rev):
    C = GDN_CHUNK
    T = C * G
    n = S // T
    ci = (lambda i: n - 1 - i) if rev else (lambda i: i)
    return dict(
        hd=pl.BlockSpec((GDN_HEADS, T, GDN_DIM), lambda i: (0, ci(i), 0)),
        hc=pl.BlockSpec((GDN_HEADS, T, C), lambda i: (0, ci(i), 0)),
        ge=pl.BlockSpec((G, GDN_HEADS, 8, 128), lambda i: (ci(i), 0, 0, 0)),
        z=pl.BlockSpec((T, GDN_WIDTH), lambda i: (ci(i), P_Z // GDN_WIDTH)),
        oa=pl.BlockSpec((T, GDN_WIDTH), lambda i: (ci(i), 0)),
        vec=pl.BlockSpec((1, 128), lambda i: (0, 0)),
        st=pl.BlockSpec((G, GDN_HEADS, GDN_DIM, GDN_DIM), lambda i: (ci(i), 0, 0, 0)),
    )


def _gdn_scan(u_v, w_k, q_dec, k_end, attn, g_end, proj, gnw):
    S = proj.shape[0]
    C, G = GDN_CHUNK, GDN_SG
    nc = S // C
    sp = _gdn_scan_specs(S, G, False)
    ps = _GDN_PASSES["scan"]

    def body(uv_ref, wk_ref, qd_ref, ke_ref, at_ref, ge_ref, z_ref, gnw_ref, oa_ref, st_ref, s_scr):
        @pl.when(pl.program_id(0) == 0)
        def _():
            s_scr[...] = jnp.zeros_like(s_scr)

        for j in range(G):
            rows = slice(C * j, C * (j + 1))
            st = s_scr[...]
            st_ref[j] = st
            u = uv_ref[:, rows, :] - _bdot_raw(wk_ref[:, rows, :], st, "NN", ps)
            o = _bdot_raw(qd_ref[:, rows, :], st, "NN", ps) + _bdot_raw(at_ref[:, rows, :], u, "NN", ps)
            s_scr[...] = ge_ref[j][:, 0:1, 0:1] * st + _bdot_raw(ke_ref[:, rows, :], u, "TN", ps)
            for h in range(GDN_HEADS):
                cols = slice(GDN_DIM * h, GDN_DIM * (h + 1))
                oa_ref[rows, cols] = _gated_norm(o[h], z_ref[rows, cols], gnw_ref[...])

    return pl.pallas_call(
        body, name="gdn_scan", grid=(nc // G,),
        in_specs=[sp["hd"], sp["hd"], sp["hd"], sp["hd"], sp["hc"], sp["ge"], sp["z"], sp["vec"]],
        out_specs=(sp["oa"], sp["st"]),
        out_shape=(jax.ShapeDtypeStruct((S, GDN_WIDTH + DIL_WIDTH), F32),
                   jax.ShapeDtypeStruct((nc, GDN_HEADS, GDN_DIM, GDN_DIM), F32)),
        scratch_shapes=[pltpu.VMEM((GDN_HEADS, GDN_DIM, GDN_DIM), F32)],
        compiler_params=_cparams(("arbitrary",)),
    )(u_v, w_k, q_dec, k_end, attn, g_end, proj, gnw)


def _gdn_scan_bwd(u_v, w_k, q_dec, k_end, attn, g_end, proj, gnw, states, d_oa):
    S = proj.shape[0]
    C, G = GDN_CHUNK, GDN_SG
    nc = S // C
    sp = _gdn_scan_specs(S, G, True)
    ps, pb = _GDN_PASSES["scan"], _GDN_PASSES["bwd"]

    def body(uv_ref, wk_ref, qd_ref, ke_ref, at_ref, ge_ref, z_ref, gnw_ref, st_ref, doa_ref,
             duv_ref, dwk_ref, dqd_ref, dke_ref, dat_ref, dge_ref, dz_ref, dgnw_ref, ds_scr):
        @pl.when(pl.program_id(0) == 0)
        def _():
            ds_scr[...] = jnp.zeros_like(ds_scr)
            dgnw_ref[...] = jnp.zeros_like(dgnw_ref)

        dgnw = jnp.zeros((1, 128), F32)
        for j in reversed(range(G)):
            rows = slice(C * j, C * (j + 1))
            st = st_ref[j]
            wk, qd, ke, at = wk_ref[:, rows, :], qd_ref[:, rows, :], ke_ref[:, rows, :], at_ref[:, rows, :]
            u = uv_ref[:, rows, :] - _bdot_raw(wk, st, "NN", ps)
            o = _bdot_raw(qd, st, "NN", ps) + _bdot_raw(at, u, "NN", ps)
            dos = []
            for h in range(GDN_HEADS):
                cols = slice(GDN_DIM * h, GDN_DIM * (h + 1))
                _, vjp2 = jax.vjp(_gated_norm, o[h], z_ref[rows, cols], gnw_ref[...])
                do_h, dz_h, dgn = vjp2(doa_ref[rows, cols])
                dz_ref[rows, cols] = dz_h
                dgnw = dgnw + dgn
                dos.append(do_h)
            do = jnp.stack(dos)
            ds_new = ds_scr[...]
            du = _bdot_raw(at, do, "TN", pb) + _bdot_raw(ke, ds_new, "NN", pb)
            duv_ref[:, rows, :] = du
            dat_ref[:, rows, :] = _bdot_raw(do, u, "NT", pb)
            dqd_ref[:, rows, :] = _bdot_raw(do, st, "NT", pb)
            dke_ref[:, rows, :] = _bdot_raw(u, ds_new, "NT", pb)
            dwk_ref[:, rows, :] = -_bdot_raw(du, st, "NT", pb)
            d_ge = jnp.sum(jnp.sum(st * ds_new, axis=2, keepdims=True), axis=1, keepdims=True)
            dge_ref[j] = jnp.broadcast_to(d_ge, (GDN_HEADS, 8, 128))
            ds_scr[...] = ge_ref[j][:, 0:1, 0:1] * ds_new + _bdot_raw(qd, do, "TN", pb) - _bdot_raw(wk, du, "TN", pb)
        dgnw_ref[...] += dgnw

    return pl.pallas_call(
        body, name="gdn_scan_bwd", grid=(nc // G,),
        in_specs=[sp["hd"], sp["hd"], sp["hd"], sp["hd"], sp["hc"], sp["ge"], sp["z"], sp["vec"], sp["st"], sp["oa"]],
        out_specs=(sp["hd"], sp["hd"], sp["hd"], sp["hd"], sp["hc"], sp["ge"], sp["oa"], sp["vec"]),
        out_shape=(_hd_shape(S), _hd_shape(S), _hd_shape(S), _hd_shape(S), _hd_shape(S, C),
                   jax.ShapeDtypeStruct((nc, GDN_HEADS, 8, 128), F32), jax.ShapeDtypeStruct((S, GDN_WIDTH), F32),
                   jax.ShapeDtypeStruct((1, 128), F32)),
        scratch_shapes=[pltpu.VMEM((GDN_HEADS, GDN_DIM, GDN_DIM), F32)],
        compiler_params=_cparams(("arbitrary",)),
    )(u_v, w_k, q_dec, k_end, attn, g_end, proj, gnw, states, d_oa)


def _gdn_post(proj, conv_w, gp, tinv, u_v, w_k, d_uv, d_wk, d_qd, d_ke, d_at, d_ge):
    S = proj.shape[0]
    C, G = GDN_CHUNK, GDN_PG
    nc = S // C
    sp = _gdn_pre_specs(S, G)
    pb = _GDN_PASSES["bwd"]

    def body(cur_ref, prev_ref, ba_ref, cw_ref, gp_ref, ti_ref, uv_ref, wk_ref, duv_ref, dwk_ref, dqd_ref, dke_ref,
             dat_ref, dge_ref, dpre_ref, dba_ref, dgp_ref):
        i = pl.program_id(0)

        @pl.when(i == 0)
        def _():
            dgp_ref[...] = jnp.zeros_like(dgp_ref)

        prev = prev_ref[...] * jnp.where(i == 0, 0.0, 1.0)
        pre = _conv_taps(jnp.concatenate([prev, cur_ref[...]], axis=0), cw_ref[...], GDN_CONV, C * G)
        sg = jax.nn.sigmoid(pre)
        dsilu = sg * (1.0 + pre * (1.0 - sg))
        pairs, args = _gdn_pairs(pre * sg, ba_ref[...], gp_ref[...], G)
        _, vjp1 = jax.vjp(functools.partial(_gdn_stage1, dot=_bdot), *args)

        def take(ref):
            return jnp.stack([ref[h, C * j:C * (j + 1), :] for j, h in pairs])

        t, u_v, w_k = take(ti_ref), take(uv_ref), take(wk_ref)
        d_v = _bdot_raw(t, take(duv_ref), "TN", pb)
        d_rk = _bdot_raw(t, take(dwk_ref), "TN", pb)
        d_l = -(_bdot_raw(d_v, u_v, "NT", pb) + _bdot_raw(d_rk, w_k, "NT", pb))
        d_ge = jnp.stack([dge_ref[j, h][0:1, 0:1] for j, h in pairs])
        dcq, dck, dcv, db, da, dalog, ddtb = vjp1((d_l, d_v, d_rk, take(dqd_ref), take(dat_ref), take(dke_ref), d_ge))
        lane = lax.broadcasted_iota(jnp.int32, (C, 128), 1)
        lane1 = lax.broadcasted_iota(jnp.int32, (1, 128), 1)
        dgp = jnp.zeros((1, 128), F32)
        for j in range(G):
            rows = slice(C * j, C * (j + 1))
            dba = jnp.zeros((C, 128), F32)
            for h in range(GDN_HEADS):
                b = GDN_HEADS * j + h
                for o_, dcx in ((0, dcq), (GDN_WIDTH, dck), (2 * GDN_WIDTH, dcv)):
                    cols = slice(o_ + GDN_DIM * h, o_ + GDN_DIM * (h + 1))
                    dpre_ref[rows, cols] = dcx[b] * dsilu[rows, cols]
                dba = dba + jnp.where(lane == h, db[b], 0.0) + jnp.where(lane == GDN_HEADS + h, da[b], 0.0)
                dgp = dgp + jnp.where(lane1 == h, dalog[b], 0.0) + jnp.where(lane1 == GDN_HEADS + h, ddtb[b], 0.0)
            dba_ref[rows, :] = dba
        dgp_ref[0:1, :] += dgp

    T = C * G
    return pl.pallas_call(
        body, name="gdn_post", grid=(nc // G,),
        in_specs=[sp["cur"], sp["prev"], sp["ba"], sp["cw"], sp["vec"], sp["hc"], sp["hd"], sp["hd"], sp["hd"], sp["hd"],
                  sp["hd"], sp["hd"], sp["hc"], sp["ge"]],
        out_specs=(sp["cur"], pl.BlockSpec((T, 128), lambda i: (i, 0)), pl.BlockSpec((8, 128), lambda i: (0, 0))),
        out_shape=(jax.ShapeDtypeStruct((S, 3 * GDN_WIDTH), F32), jax.ShapeDtypeStruct((S, 128), F32),
                   jax.ShapeDtypeStruct((8, 128), F32)),
        compiler_params=_cparams(("arbitrary",)),
    )(proj, proj, proj, conv_w, gp, tinv, u_v, w_k, d_uv, d_wk, d_qd, d_ke, d_at, d_ge)


def _conv_bwd(dpre, x, xcol0, w, K, name, tc):
    S, Cc = dpre.shape
    T = _pick_tile(S, 256)
    nt, ncol = S // T, Cc // tc
    xo = xcol0 // tc

    def body(d_ref, dn_ref, x_ref, xp_ref, w_ref, dx_ref, dw_ref):
        i = pl.program_id(1)
        dn = dn_ref[...] * jnp.where(i == nt - 1, 0.0, 1.0)
        dv = d_ref[...]
        ext_d = jnp.concatenate([dv, dn], axis=0)
        dx_ref[...] = _conv_taps_t(ext_d, w_ref[...], K, T).astype(dx_ref.dtype)
        xp = xp_ref[...] * jnp.where(i == 0, 0.0, 1.0)
        ext_x = jnp.concatenate([xp, x_ref[...]], axis=0)

        @pl.when(i == 0)
        def _():
            dw_ref[...] = jnp.zeros_like(dw_ref)

        for k in range(K):
            dw_ref[k:k + 1, :] += jnp.sum(dv * _shifted(ext_x, (K - 1) - k, 8, T), axis=0, keepdims=True)

    r8 = T // 8
    return pl.pallas_call(
        body, name=name, grid=(ncol, nt),
        in_specs=[pl.BlockSpec((T, tc), lambda j, i: (i, j)),
                  pl.BlockSpec((8, tc), lambda j, i: (jnp.minimum((i + 1) * r8, S // 8 - 1), j)),
                  pl.BlockSpec((T, tc), lambda j, i: (i, j + xo)),
                  pl.BlockSpec((8, tc), lambda j, i: (jnp.maximum(i * r8 - 1, 0), j + xo)),
                  pl.BlockSpec((K, tc), lambda j, i: (0, j))],
        out_specs=(pl.BlockSpec((T, tc), lambda j, i: (i, j)), pl.BlockSpec((K, tc), lambda j, i: (0, j))),
        out_shape=(jax.ShapeDtypeStruct((S, Cc), _MXU), jax.ShapeDtypeStruct((K, Cc), F32)),
        compiler_params=_cparams(("parallel", "arbitrary")),
    )(dpre, dpre, x, x, w)


def _dil_bias(nt, T):
    d = (np.arange(nt)[:, None, None] * T + np.arange(T)[None, None, :] - np.arange(T)[None, :, None])
    cnt = ((d >= 0) & (d <= 128)).astype(np.float64) + ((d >= 0) & (d % 4 == 0) & (d <= 512)) + ((d >= 0) & (d % 16 == 0))
    return jnp.asarray(np.where(cnt > 0, np.log(np.maximum(cnt, 1.0)), -1e30), dtype=F32)


def _attn_fwd(proj, mix):
    S = proj.shape[0]
    T = min(ATT_T, S)
    nt = S // T
    bias = _dil_bias(nt, T)
    scale = DIL_DIM ** -0.5
    npair = DIL_WIDTH // 128
    qb0, kb0, vb0 = P_QKVB // 128, (P_QKVB + DIL_WIDTH) // 128, (P_QKVB + 2 * DIL_WIDTH) // 128

    def body(q_ref, k_ref, v_ref, b_ref, mix_ref, o_ref, lse_ref):
        i = pl.program_id(1)
        qs = (q_ref[...] * scale).astype(_MXU)

        def step(j, carry):
            kt = k_ref[pl.ds(pl.multiple_of(j * T, T), T), :].astype(_MXU)
            vt = v_ref[pl.ds(pl.multiple_of(j * T, T), T), :].astype(_MXU)
            bt = b_ref[i - j]
            out = []
            for hh in range(2):
                m, l, acc = carry[hh]
                sl = slice(hh * DIL_DIM, (hh + 1) * DIL_DIM)
                s = lax.dot_general(kt[:, sl], qs[:, sl], (_NT, ((), ())), preferred_element_type=F32) + bt
                m_new = jnp.maximum(m, jnp.max(s, axis=0, keepdims=True))
                p = jnp.exp(s - m_new)
                a = jnp.exp(m - m_new)
                l = a * l + jnp.sum(p, axis=0, keepdims=True)
                acc = a * acc + lax.dot_general(vt[:, sl], p.astype(_MXU), (_TN, ((), ())), preferred_element_type=F32)
                out.append((m_new, l, acc))
            return tuple(out)

        init = tuple((jnp.full((1, T), -1e30, F32), jnp.zeros((1, T), F32), jnp.zeros((DIL_DIM, T), F32)) for _ in range(2))
        res = lax.fori_loop(0, i + 1, step, init)
        lse_ref[...] = jnp.zeros_like(lse_ref)
        for hh in range(2):
            m, l, acc = res[hh]
            o_ref[:, hh * DIL_DIM:(hh + 1) * DIL_DIM] = (acc / l).T
            lse_ref[hh:hh + 1, :] = m + jnp.log(l)

    return pl.pallas_call(
        body, name="attn_fwd", grid=(npair, nt),
        in_specs=[pl.BlockSpec((T, 128), lambda p, i: (i, qb0 + p)),
                  pl.BlockSpec((S, 128), lambda p, i: (0, kb0 + p)),
                  pl.BlockSpec((S, 128), lambda p, i: (0, vb0 + p)),
                  pl.BlockSpec((nt, T, T), lambda p, i: (0, 0, 0)), pl.BlockSpec(memory_space=pl.ANY)],
        out_specs=(pl.BlockSpec((T, 128), lambda p, i: (i, GDN_WIDTH // 128 + p)),
                   pl.BlockSpec((None, None, 8, T), lambda p, i: (p, i, 0, 0))),
        out_shape=(jax.ShapeDtypeStruct(mix.shape, F32), jax.ShapeDtypeStruct((npair, nt, 8, T), F32)),
        input_output_aliases={4: 0},
        compiler_params=_cparams(("parallel", "parallel")),
    )(proj, proj, proj, bias, mix)


def _attn_bwd(proj, mix, lse, d_mix):
    S = proj.shape[0]
    T = min(ATT_T, S)
    nt = S // T
    bias = _dil_bias(nt, T)
    scale = DIL_DIM ** -0.5
    npair = DIL_WIDTH // 128
    qb0, kb0, vb0 = P_QKVB // 128, (P_QKVB + DIL_WIDTH) // 128, (P_QKVB + 2 * DIL_WIDTH) // 128

    def body(q_ref, k_ref, v_ref, o_ref, lse_ref, do_ref, b_ref, dq_ref, dk_ref, dv_ref, dq_scr):
        j = pl.program_id(1)

        @pl.when(j == 0)
        def _():
            dq_scr[...] = jnp.zeros_like(dq_scr)

        kt = k_ref[...].astype(_MXU)
        vt = v_ref[...].astype(_MXU)
        ones = jnp.ones((8, DIL_DIM), F32)

        def step(i, carry):
            rows = pl.ds(pl.multiple_of(i * T, T), T)
            qs = (q_ref[rows, :] * scale).astype(_MXU)
            dov = do_ref[rows, :]
            prod = dov * o_ref[rows, :]
            lsev = lse_ref[i]
            dob = dov.astype(_MXU)
            bt = b_ref[i - j]
            out = []
            dqs = []
            for hh in range(2):
                dk, dv = carry[hh]
                sl = slice(hh * DIL_DIM, (hh + 1) * DIL_DIM)
                s = lax.dot_general(kt[:, sl], qs[:, sl], (_NT, ((), ())), preferred_element_type=F32) + bt
                p = jnp.exp(s - lsev[hh:hh + 1, :])
                delta = lax.dot_general(ones, prod[:, sl], (_NT, ((), ())), precision=_HI, preferred_element_type=F32)[0:1, :]
                dp = lax.dot_general(vt[:, sl], dob[:, sl], (_NT, ((), ())), preferred_element_type=F32)
                ds = (p * (dp - delta)).astype(_MXU)
                dv = dv + lax.dot_general(p.astype(_MXU), dob[:, sl], (_NN, ((), ())), preferred_element_type=F32)
                dk = dk + lax.dot_general(ds, qs[:, sl], (_NN, ((), ())), preferred_element_type=F32)
                dqs.append(lax.dot_general(ds, kt[:, sl], (_TN, ((), ())), preferred_element_type=F32) * scale)
                out.append((dk, dv))
            dq_scr[rows, :] += jnp.concatenate(dqs, axis=1)
            return tuple(out)

        init = tuple((jnp.zeros((T, DIL_DIM), F32), jnp.zeros((T, DIL_DIM), F32)) for _ in range(2))
        res = lax.fori_loop(j, nt, step, init)
        dk_ref[...] = jnp.concatenate([res[0][0], res[1][0]], axis=1).astype(dk_ref.dtype)
        dv_ref[...] = jnp.concatenate([res[0][1], res[1][1]], axis=1).astype(dv_ref.dtype)

        @pl.when(j == nt - 1)
        def _():
            dq_ref[...] = dq_scr[...].astype(dq_ref.dtype)

    full = lambda c0: pl.BlockSpec((S, 128), lambda p, j: (0, c0 + p))
    tile = lambda c0: pl.BlockSpec((T, 128), lambda p, j: (j, c0 + p))
    out3 = jax.ShapeDtypeStruct((S, DIL_WIDTH), _MXU)
    return pl.pallas_call(
        body, name="attn_bwd", grid=(npair, nt),
        in_specs=[full(qb0), tile(kb0), tile(vb0), full(GDN_WIDTH // 128),
                  pl.BlockSpec((None, nt, 8, T), lambda p, j: (p, 0, 0, 0)), full(GDN_WIDTH // 128),
                  pl.BlockSpec((nt, T, T), lambda p, j: (0, 0, 0))],
        out_specs=(full(0), tile(0), tile(0)),
        out_shape=(out3, out3, out3),
        scratch_shapes=[pltpu.VMEM((S, 128), F32)],
        compiler_params=_cparams(("parallel", "arbitrary")),
    )(proj, proj, proj, mix, lse, d_mix, bias)


def _ffn_act(up, cw):
    S, Cc = up.shape[0], up.shape[1] // 2
    T, tc = _pick_tile(S, 256), _pick_tile(Cc, 1536)
    r16 = T // 16
    nct = Cc // tc

    def body(g_ref, gp_ref, u_ref, up_ref, wg_ref, wu_ref, o_ref):
        keep = jnp.where(pl.program_id(1) == 0, 0.0, 1.0)
        cg = _conv_taps(jnp.concatenate([gp_ref[8:16, :].astype(F32) * keep, g_ref[...].astype(F32)], axis=0),
                        wg_ref[...], FFN_CONV, T)
        cu = _conv_taps(jnp.concatenate([up_ref[8:16, :].astype(F32) * keep, u_ref[...].astype(F32)], axis=0),
                        wu_ref[...], FFN_CONV, T)
        o_ref[...] = (_silu(cg) * cu).astype(o_ref.dtype)

    cur = lambda o: pl.BlockSpec((T, tc), lambda j, i: (i, j + o))
    prev = lambda o: pl.BlockSpec((16, tc), lambda j, i: (jnp.maximum(i * r16 - 1, 0), j + o))
    wsp = lambda o: pl.BlockSpec((FFN_CONV, tc), lambda j, i: (0, j + o))
    return pl.pallas_call(
        body, name="ffn_act", grid=(nct, S // T),
        in_specs=[cur(0), prev(0), cur(nct), prev(nct), wsp(0), wsp(nct)], out_specs=cur(0),
        out_shape=jax.ShapeDtypeStruct((S, Cc), _MXU),
        compiler_params=_cparams(("parallel", "parallel")),
    )(up, up, up, up, cw, cw)


def _ffn_act_bwd(d_act, up, cw):
    S, Cc = up.shape[0], up.shape[1] // 2
    T, tc = _pick_tile(S, 256), _pick_tile(Cc, 1536)
    r8, r16 = T // 8, T // 16
    nt = S // T
    nct = Cc // tc
    K = FFN_CONV

    def body(da_ref, dan_ref, g_ref, gp_ref, gn_ref, u_ref, up_ref, un_ref, wg_ref, wu_ref,
             dg_ref, du_ref, dwg_ref, dwu_ref):
        i = pl.program_id(1)
        keep_p = jnp.where(i == 0, 0.0, 1.0)
        keep_n = jnp.where(i == nt - 1, 0.0, 1.0)
        wg, wu = wg_ref[...], wu_ref[...]
        xg = jnp.concatenate([gp_ref[8:16, :].astype(F32) * keep_p, g_ref[...].astype(F32),
                              gn_ref[0:8, :].astype(F32) * keep_n], axis=0)
        xu = jnp.concatenate([up_ref[8:16, :].astype(F32) * keep_p, u_ref[...].astype(F32),
                              un_ref[0:8, :].astype(F32) * keep_n], axis=0)
        cg = _conv_taps(xg, wg, K, T + 8)
        cu = _conv_taps(xu, wu, K, T + 8)
        da = jnp.concatenate([da_ref[...], dan_ref[...] * keep_n], axis=0)
        sg = jax.nn.sigmoid(cg)
        d_cg = da * cu * (sg * (1.0 + cg * (1.0 - sg)))
        d_cu = da * (cg * sg)
        dg_ref[...] = _conv_taps_t(d_cg, wg, K, T).astype(dg_ref.dtype)
        du_ref[...] = _conv_taps_t(d_cu, wu, K, T).astype(du_ref.dtype)

        @pl.when(i == 0)
        def _():
            dwg_ref[...] = jnp.zeros_like(dwg_ref)
            dwu_ref[...] = jnp.zeros_like(dwu_ref)

        for k in range(K):
            dwg_ref[k:k + 1, :] += jnp.sum(d_cg[0:T, :] * _shifted(xg, (K - 1) - k, 8, T), axis=0, keepdims=True)
            dwu_ref[k:k + 1, :] += jnp.sum(d_cu[0:T, :] * _shifted(xu, (K - 1) - k, 8, T), axis=0, keepdims=True)

    cur = lambda o: pl.BlockSpec((T, tc), lambda j, i: (i, j + o))
    prev = lambda o: pl.BlockSpec((16, tc), lambda j, i: (jnp.maximum(i * r16 - 1, 0), j + o))
    nxt = lambda o: pl.BlockSpec((16, tc), lambda j, i: (jnp.minimum((i + 1) * r16, S // 16 - 1), j + o))
    nxt8 = pl.BlockSpec((8, tc), lambda j, i: (jnp.minimum((i + 1) * r8, S // 8 - 1), j))
    wsp = lambda o: pl.BlockSpec((K, tc), lambda j, i: (0, j + o))
    return pl.pallas_call(
        body, name="ffn_act_bwd", grid=(nct, nt),
        in_specs=[cur(0), nxt8, cur(0), prev(0), nxt(0), cur(nct), prev(nct), nxt(nct), wsp(0), wsp(nct)],
        out_specs=(cur(0), cur(0), wsp(0), wsp(0)),
        out_shape=(jax.ShapeDtypeStruct((S, Cc), _MXU), jax.ShapeDtypeStruct((S, Cc), _MXU),
                   jax.ShapeDtypeStruct((K, Cc), F32), jax.ShapeDtypeStruct((K, Cc), F32)),
        compiler_params=_cparams(("parallel", "arbitrary")),
    )(d_act, d_act, up, up, up, up, up, up, cw, cw)


def _local_step(x, tgt, h1, n1w, n2w, fnw, gp, gnw, wp, conv_w, fcw, rest_weights, early_grads):
    proj = _mm(h1, wp, "nn", name="proj")
    u_v, w_k, q_dec, k_end, attn, tinv, g_end = _gdn_pre(proj, conv_w, gp)
    mix, states = _gdn_scan(u_v, w_k, q_dec, k_end, attn, g_end, proj, gnw)
    mix, lse = _attn_fwd(proj, mix)
    w_out, w_up4, w_down = rest_weights([mix])
    x2 = _mm(mix, w_out, "nn", residual=x, name="outproj")
    h2 = _rmsnorm_fwd(x2, n2w, "norm2")
    up = _mm(h2, w_up4, "nn", b_blocks=True, out_dtype=_MXU, name="up")
    act = _ffn_act(up, fcw)
    x3 = _mm(act, w_down, "nn", residual=x2, name="down")
    loss, dx3, dx3n, d_fnw = _loss_head(x3, fnw, tgt, "loss_head")
    d_act = _mm(dx3n, w_down, "nt", name="d_act")
    d_wdown = _mm(act, dx3n, "tn", name="d_wdown")
    d_upg, d_upu, d_fcwg, d_fcwu = _ffn_act_bwd(d_act, up, fcw)
    d_wup = _mm(h2, d_upg, "tn", place=("blocks", N_CHIPS, 0), tn=w_up4.shape[2], name="d_wgate")
    d_wup = _mm(h2, d_upu, "tn", place=("blocks", N_CHIPS, N_CHIPS // 2), tn=w_up4.shape[2], into=d_wup, name="d_wup")
    token = early_grads[0](d_wup, d_wdown)
    d_h2 = _mm_nt_blocks([d_upg, d_upu], w_up4, "d_h2", after=[token])
    dx2, d_n2w = _rmsnorm_bwd(d_h2, x2, n2w + token[0:1, 0:1], dx3, "norm2_bwd")
    token = early_grads[1](dx2)
    d_mix = _mm(dx2, w_out, "nt", name="d_mix")
    d_wout = _mm(mix, dx2, "tn", name="d_wout")
    dq_b, dk_b, dv_b = _attn_bwd(proj, mix, lse, d_mix)
    d_uv, d_wk, d_qd, d_ke, d_at, d_ge, d_z, d_gnw = _gdn_scan_bwd(u_v, w_k, q_dec, k_end, attn, g_end, proj,
                                                                   gnw + token[0:1, 0:1], states, d_mix)
    d_pre, d_ba, d_gp = _gdn_post(proj, conv_w, gp, tinv, u_v, w_k, d_uv, d_wk, d_qd, d_ke, d_at, d_ge)
    d_qkva, d_convw = _conv_bwd(d_pre, proj, 0, conv_w, GDN_CONV, "gdn_conv_bwd", 512)
    d_proj = jnp.concatenate([d_qkva, d_z.astype(_MXU), dq_b, dk_b, dv_b, d_ba.astype(_MXU),
                              jnp.zeros((x.shape[0], P_COLS - P_BA - 128), _MXU)], axis=1)
    d_wp = _mm(h1, d_proj, "tn", name="d_wp")
    token = early_grads[2](d_wp, d_wout)
    d_h1 = _mm(d_proj, wp, "nt", name="d_h1", after=[token])
    dx, d_n1w = _rmsnorm_bwd(d_h1, x, n1w + token[0:1, 0:1], dx2, "norm1_bwd")
    grads = dict(wp=d_wp, conv_w=d_convw, w_out=d_wout, w_up=d_wup, fcw_g=d_fcwg, fcw_u=d_fcwu, w_down=d_wdown,
                 n1w=d_n1w, n2w=d_n2w, fnw=d_fnw, gp=d_gp, gnw=d_gnw)
    return loss, dx, grads


_HBM = pl.BlockSpec(memory_space=pltpu.HBM)


def _pos():
    return lax.axis_index("x"), lax.axis_index("y"), lax.axis_index("c")


def _other_chips(x, y):
    return [(1 - x, y), (x, 1 - y), (1 - x, 1 - y)]


def _halvable(shape):
    return shape[0] % 32 == 0


def _rows_of_half(shape, half):
    if not _halvable(shape):
        return pl.ds(0, shape[0])
    return pl.ds(pl.multiple_of(half * (shape[0] // 2), 16), shape[0] // 2)


_SEM = pl.BlockSpec(memory_space=pltpu.SEMAPHORE)
_ANY = pl.BlockSpec(memory_space=pl.ANY)
_DATAFLOW = pltpu.SideEffectType.DATAFLOW_SIDE_EFFECTING


def _in_hbm(a):
    return pltpu.with_memory_space_constraint(a, pltpu.HBM)


def _halves_copy(src_refs, land_refs, send_sems, recv_sems, shapes, a, j, block, x, y, c):
    px, py = _other_chips(x, y)[j]
    rows = _rows_of_half(shapes[a], c)
    return pltpu.make_async_remote_copy(
        src_ref=src_refs[a].at[rows, :], dst_ref=land_refs[a].at[block, rows, :], send_sem=send_sems.at[3 * a + j],
        recv_sem=recv_sems.at[3 * a + j], device_id=(px, py, c), device_id_type=MESH)


def _gather_halves_start(shards, after, name):
    n = len(shards)
    shapes = [s.shape for s in shards]

    def body(*refs):
        ins, lands = refs[:n], refs[n:2 * n]
        send_sems, recv_sems = refs[2 * n + 1], refs[2 * n + 2]
        token = refs[-1]
        x, y, c = _pos()
        q = 2 * x + y
        for a in range(n):
            for j in range(3):
                _halves_copy(ins, lands, send_sems, recv_sems, shapes, a, j, q, x, y, c).start()
        token[...] = jnp.zeros_like(token)

    land_shapes = [(N_CHIPS,) + s.shape for s in shards]
    return pl.pallas_call(
        body, name=name,
        out_shape=(pltpu.SemaphoreType.DMA((3 * n,)), pltpu.SemaphoreType.DMA((3 * n,)),
                   *[pltpu.HBM(s.shape, s.dtype) for s in shards],
                   *[pltpu.HBM(ls, s.dtype) for ls, s in zip(land_shapes, shards)],
                   jax.ShapeDtypeStruct((8, 128), F32)),
        in_specs=[_HBM] * (2 * n) + [_ANY],
        out_specs=(_SEM, _SEM, *[_HBM] * (2 * n), pl.BlockSpec(memory_space=pltpu.VMEM)),
        input_output_aliases={a: 2 + a for a in range(2 * n)},
        compiler_params=pltpu.CompilerParams(has_side_effects=_DATAFLOW),
    )(*[_in_hbm(s) for s in shards], *[_in_hbm(lax.empty(ls, s.dtype)) for ls, s in zip(land_shapes, shards)], after)


def _gather_halves_wait(started, after, name):
    send_sems, recv_sems, *thru = started
    n = len(thru) // 2
    shapes = [t.shape for t in thru[:n]]

    def body(*refs):
        ins, lands = refs[:n], refs[n:2 * n]
        send_sems, recv_sems = refs[2 * n], refs[2 * n + 1]
        x, y, c = _pos()
        q = 2 * x + y
        chips = _other_chips(x, y)
        for a in range(n):
            for j, (px, py) in enumerate(chips):
                _halves_copy(ins, lands, send_sems, recv_sems, shapes, a, j, q, x, y, c).wait_send()
                _halves_copy(ins, lands, send_sems, recv_sems, shapes, a, j, 2 * px + py, x, y, c).wait_recv()

    outs = pl.pallas_call(
        body, name=name, out_shape=[pltpu.HBM(t.shape, t.dtype) for t in thru],
        in_specs=[_HBM] * (2 * n) + [_SEM, _SEM] + [_ANY] * len(after), out_specs=[_HBM] * (2 * n),
        input_output_aliases={a: a for a in range(2 * n)},
        compiler_params=pltpu.CompilerParams(has_side_effects=_DATAFLOW),
    )(*thru, send_sems, recv_sems, *after)
    return outs[:n], outs[n:]


def _sibling_fill(gathered, name):
    big = [a for a, g in enumerate(gathered) if _halvable(g.shape[1:])]
    n = len(gathered)

    def body(*refs):
        ins, outs = refs[:n], refs[n:2 * n]
        send_sems, recv_sems = refs[2 * n:]
        x, y, c = _pos()
        chips = _other_chips(x, y)

        def copy(k, j, half):
            a = big[k]
            px, py = chips[j]
            rows = _rows_of_half(gathered[a].shape[1:], half)
            return pltpu.make_async_remote_copy(
                src_ref=ins[a].at[2 * px + py, rows, :], dst_ref=outs[a].at[2 * px + py, rows, :],
                send_sem=send_sems.at[3 * k + j], recv_sem=recv_sems.at[3 * k + j],
                device_id=(x, y, 1 - c), device_id_type=MESH)

        sends = [copy(k, j, c) for k in range(len(big)) for j in range(3)]
        for cp in sends:
            cp.start()
        for k in range(len(big)):
            for j in range(3):
                copy(k, j, 1 - c).wait_recv()
        for cp in sends:
            cp.wait_send()

    return pl.pallas_call(
        body, name=name, in_specs=[_HBM] * n, out_specs=[_HBM] * n,
        out_shape=[jax.ShapeDtypeStruct(g.shape, g.dtype) for g in gathered],
        input_output_aliases={a: a for a in range(n)},
        scratch_shapes=[pltpu.SemaphoreType.DMA((3 * len(big),)), pltpu.SemaphoreType.DMA((3 * len(big),))],
    )(*gathered)


def _place_own(shards, gathered, cq, name):
    n = len(shards)
    steps = 4

    def body(cq_ref, *refs):
        for a in range(n):
            refs[2 * n + a][...] = refs[a][...]

    def tile(shape):
        return shape[0] // steps if _halvable(shape) else shape[0]

    in_specs = [pl.BlockSpec((tile(s.shape), s.shape[1]), (lambda i, s_: (i, 0)) if _halvable(s.shape) else (lambda i, s_: (0, 0)))
                for s in shards]
    in_specs += [pl.BlockSpec(memory_space=pl.ANY)] * n
    out_specs = [pl.BlockSpec((None, tile(s.shape), s.shape[1]),
                              (lambda i, s_: (s_[1], i, 0)) if _halvable(s.shape) else (lambda i, s_: (s_[1], 0, 0)))
                 for s in shards]
    gs = pltpu.PrefetchScalarGridSpec(num_scalar_prefetch=1, grid=(steps,), in_specs=in_specs, out_specs=out_specs)
    return pl.pallas_call(
        body, name=name, grid_spec=gs, out_shape=[jax.ShapeDtypeStruct(g.shape, g.dtype) for g in gathered],
        input_output_aliases={1 + n + a: a for a in range(n)},
        compiler_params=_cparams(("arbitrary",)),
    )(cq, *shards, *gathered)


def _half_rows(ref, c, rh):
    return ref.at[:, pl.ds(pl.multiple_of(c * rh, 8), rh), :]


def _chips_copy(src_refs, land_refs, send_sems, recv_sems, a, j, x, y, c):
    px, py = _other_chips(x, y)[j]
    return pltpu.make_async_remote_copy(src_ref=src_refs[a].at[2 * px + py], dst_ref=land_refs[a].at[j],
                                        send_sem=send_sems.at[3 * a + j], recv_sem=recv_sems.at[3 * a + j],
                                        device_id=(px, py, c), device_id_type=MESH)


def _grad_chips_start(parts, name):
    n = len(parts)

    def body(*refs):
        ins, lands = refs[:n], refs[n:2 * n]
        send_sems, recv_sems = refs[2 * n], refs[2 * n + 1]
        token = refs[-1]
        x, y, c = _pos()
        for a in range(n):
            for j in range(3):
                _chips_copy(ins, lands, send_sems, recv_sems, a, j, x, y, c).start()
        token[...] = jnp.zeros_like(token)

    land_shapes = [(3,) + p.shape[1:] for p in parts]
    return pl.pallas_call(
        body, name=name,
        out_shape=(pltpu.SemaphoreType.DMA((3 * n,)), pltpu.SemaphoreType.DMA((3 * n,)),
                   *[pltpu.HBM(p.shape, p.dtype) for p in parts],
                   *[pltpu.HBM(ls, p.dtype) for ls, p in zip(land_shapes, parts)],
                   jax.ShapeDtypeStruct((8, 128), F32)),
        in_specs=[_HBM] * (2 * n),
        out_specs=(_SEM, _SEM, *[_HBM] * (2 * n), pl.BlockSpec(memory_space=pltpu.VMEM)),
        input_output_aliases={a: 2 + a for a in range(2 * n)},
        compiler_params=pltpu.CompilerParams(has_side_effects=_DATAFLOW),
    )(*[_in_hbm(p) for p in parts], *[_in_hbm(lax.empty(ls, p.dtype)) for ls, p in zip(land_shapes, parts)])


def _grad_chips_wait(started, after, name):
    send_sems, recv_sems, *thru = started
    n = len(thru) // 2

    def body(*refs):
        ins, lands = refs[:n], refs[n:2 * n]
        send_sems, recv_sems = refs[2 * n], refs[2 * n + 1]
        x, y, c = _pos()
        for a in range(n):
            for j in range(3):
                cp = _chips_copy(ins, lands, send_sems, recv_sems, a, j, x, y, c)
                cp.wait_send()
                cp.wait_recv()

    outs = pl.pallas_call(
        body, name=name, out_shape=[pltpu.HBM(t.shape, t.dtype) for t in thru],
        in_specs=[_HBM] * (2 * n) + [_SEM, _SEM] + [_ANY] * len(after), out_specs=[_HBM] * (2 * n),
        input_output_aliases={a: a for a in range(2 * n)},
        compiler_params=pltpu.CompilerParams(has_side_effects=_DATAFLOW),
    )(*thru, send_sems, recv_sems, *after)
    return outs[n:]


def _sibling_copy(src_refs, land_refs, send_sems, recv_sems, rhs, a, c, x, y):
    return pltpu.make_async_remote_copy(src_ref=_half_rows(src_refs[a], 1 - c, rhs[a]), dst_ref=land_refs[a],
                                        send_sem=send_sems.at[a], recv_sem=recv_sems.at[a],
                                        device_id=(x, y, 1 - c), device_id_type=MESH)


def _grad_sibling_start(fams, name):
    n = len(fams)
    rhs = [f.shape[1] // 2 for f in fams]

    def body(*refs):
        ins, lands = refs[:n], refs[n:2 * n]
        send_sems, recv_sems = refs[2 * n], refs[2 * n + 1]
        token = refs[-1]
        x, y, c = _pos()
        for a in range(n):
            _sibling_copy(ins, lands, send_sems, recv_sems, rhs, a, c, x, y).start()
        token[...] = jnp.zeros_like(token)

    land_shapes = [(f.shape[0], f.shape[1] // 2, f.shape[2]) for f in fams]
    return pl.pallas_call(
        body, name=name,
        out_shape=(pltpu.SemaphoreType.DMA((n,)), pltpu.SemaphoreType.DMA((n,)),
                   *[pltpu.HBM(f.shape, f.dtype) for f in fams],
                   *[pltpu.HBM(ls, f.dtype) for ls, f in zip(land_shapes, fams)],
                   jax.ShapeDtypeStruct((8, 128), F32)),
        in_specs=[_HBM] * (2 * n),
        out_specs=(_SEM, _SEM, *[_HBM] * (2 * n), pl.BlockSpec(memory_space=pltpu.VMEM)),
        input_output_aliases={a: 2 + a for a in range(2 * n)},
        compiler_params=pltpu.CompilerParams(has_side_effects=_DATAFLOW),
    )(*[_in_hbm(f) for f in fams], *[_in_hbm(lax.empty(ls, f.dtype)) for ls, f in zip(land_shapes, fams)])


def _grad_sibling_wait(started, after, name):
    send_sems, recv_sems, *thru = started
    n = len(thru) // 2
    rhs = [t.shape[1] // 2 for t in thru[:n]]

    def body(*refs):
        ins, lands = refs[:n], refs[n:2 * n]
        send_sems, recv_sems = refs[2 * n], refs[2 * n + 1]
        x, y, c = _pos()
        for a in range(n):
            cp = _sibling_copy(ins, lands, send_sems, recv_sems, rhs, a, c, x, y)
            cp.wait_send()
            cp.wait_recv()

    outs = pl.pallas_call(
        body, name=name, out_shape=[pltpu.HBM(t.shape, t.dtype) for t in thru],
        in_specs=[_HBM] * (2 * n) + [_SEM, _SEM] + [_ANY] * len(after), out_specs=[_HBM] * (2 * n),
        input_output_aliases={a: a for a in range(2 * n)},
        compiler_params=pltpu.CompilerParams(has_side_effects=_DATAFLOW),
    )(*thru, send_sems, recv_sems, *after)
    return outs[:n], outs[n:]


def _grad_share(fulls, name, small=None):
    n = len(fulls)
    ns = 0 if small is None else 1
    rhs = [f.shape[0] // 2 for f in fulls]

    def body(*refs):
        ins, outs = refs[:n], refs[n + ns:2 * n + ns]
        send_sems, recv_sems = refs[2 * (n + ns)], refs[2 * (n + ns) + 1]
        x, y, c = _pos()

        def copy(a, half):
            rows = pl.ds(pl.multiple_of(half * rhs[a], 8), rhs[a])
            return pltpu.make_async_remote_copy(src_ref=ins[a].at[rows, :], dst_ref=outs[a].at[rows, :],
                                                send_sem=send_sems.at[7 * ns + a], recv_sem=recv_sems.at[7 * ns + a],
                                                device_id=(x, y, 1 - c), device_id_type=MESH)

        sends = [copy(a, c) for a in range(n)]
        for cp in sends:
            cp.start()
        if ns:
            small_ref, all_ref = refs[n], refs[2 * n + 1]
            me = 4 * x + 2 * y + c

            def peer(r):
                dx, dy, dc = (r >> 2) & 1, (r >> 1) & 1, r & 1
                return (x if dx == 0 else 1 - x), (y if dy == 0 else 1 - y), (c if dc == 0 else 1 - c)

            def small_copy(r, slot):
                return pltpu.make_async_remote_copy(src_ref=small_ref, dst_ref=all_ref.at[slot], send_sem=send_sems.at[r - 1],
                                                    recv_sem=recv_sems.at[r - 1], device_id=peer(r), device_id_type=MESH)

            smalls = [small_copy(r, me) for r in range(1, 8)]
            for cp in smalls:
                cp.start()
            for r in range(1, 8):
                px, py, pc = peer(r)
                small_copy(r, 4 * px + 2 * py + pc).wait_recv()
            sends = sends + smalls
        for a in range(n):
            copy(a, 1 - c).wait_recv()
        for cp in sends:
            cp.wait_send()

    return pl.pallas_call(
        body, name=name, in_specs=[_HBM] * (n + ns), out_specs=[_HBM] * (n + ns),
        out_shape=[jax.ShapeDtypeStruct(f.shape, f.dtype) for f in fulls]
        + ([jax.ShapeDtypeStruct((8,) + small.shape, small.dtype)] if ns else []),
        input_output_aliases={a: a for a in range(n)},
        scratch_shapes=[pltpu.SemaphoreType.DMA((7 * ns + n,)), pltpu.SemaphoreType.DMA((7 * ns + n,))],
    )(*fulls, *([small] if ns else []))


def _add_sibling(own, recv, cq, name):
    nb, R, Cc = own.shape
    Rh = R // 2

    def body(cq_ref, a_ref, b_ref, o32_ref, o16_ref):
        s = a_ref[...] + b_ref[...]
        o32_ref[...] = s
        o16_ref[...] = s.astype(o16_ref.dtype)

    sp = pl.BlockSpec((1, Rh, Cc), lambda b, s: (b, 0, 0))
    gs = pltpu.PrefetchScalarGridSpec(
        num_scalar_prefetch=1, grid=(nb,),
        in_specs=[pl.BlockSpec((1, Rh, Cc), lambda b, s: (b, s[0], 0)), sp], out_specs=[sp, sp])
    return pl.pallas_call(
        body, name=name, grid_spec=gs,
        out_shape=[jax.ShapeDtypeStruct((nb, Rh, Cc), F32), jax.ShapeDtypeStruct((nb, Rh, Cc), _MXU)],
        compiler_params=_cparams(("parallel",)),
    )(cq, own, recv)


def _add_chips(part32, recv3, cq, name):
    nb, Rh, Cc = part32.shape

    def body(cq_ref, a_ref, b_ref, o_ref):
        acc = a_ref[0]
        for j in range(3):
            acc = acc + b_ref[j].astype(F32)
        o_ref[...] = acc

    gs = pltpu.PrefetchScalarGridSpec(
        num_scalar_prefetch=1, grid=(1,),
        in_specs=[pl.BlockSpec((1, Rh, Cc), lambda i, s: (s[1], 0, 0)), pl.BlockSpec((3, Rh, Cc), lambda i, s: (0, 0, 0))],
        out_specs=pl.BlockSpec((Rh, Cc), lambda i, s: (s[0], 0)))
    return pl.pallas_call(
        body, name=name, grid_spec=gs, out_shape=jax.ShapeDtypeStruct((2 * Rh, Cc), F32),
        compiler_params=_cparams(("arbitrary",)),
    )(cq, part32, recv3)


def _sum_devices(small_all, small, me):
    def body(me_ref, all_ref, own_ref, o_ref):
        tot = None
        for d in range(8):
            term = jnp.where(me_ref[0] == d, own_ref[...], all_ref[d])
            tot = term if tot is None else tot + term
        o_ref[...] = tot

    gs = pltpu.PrefetchScalarGridSpec(
        num_scalar_prefetch=1, grid=(1,),
        in_specs=[pl.BlockSpec(small_all.shape, lambda i, s: (0, 0, 0)), pl.BlockSpec(small.shape, lambda i, s: (0, 0))],
        out_specs=pl.BlockSpec(small.shape, lambda i, s: (0, 0)))
    return pl.pallas_call(body, name="sum_devices", grid_spec=gs,
                          out_shape=jax.ShapeDtypeStruct(small.shape, F32))(me, small_all, small)


def _adamw(w, g, m, v, name):
    R, Cc = w.shape
    T = max([t for t in range(8, 257, 8) if R % t == 0], default=R)
    c1 = 1.0 / (1.0 - ADAM_B1 ** ADAM_STEP)
    c2 = 1.0 / (1.0 - ADAM_B2 ** ADAM_STEP)

    def body(w_ref, g_ref, m_ref, v_ref, d_ref, mo_ref, vo_ref):
        gv = g_ref[...]
        mn = ADAM_B1 * m_ref[...] + (1.0 - ADAM_B1) * gv
        vn = ADAM_B2 * v_ref[...] + (1.0 - ADAM_B2) * (gv * gv)
        mo_ref[...] = mn
        vo_ref[...] = vn
        d_ref[...] = -ADAM_LR * ((mn * c1) / (jnp.sqrt(vn * c2) + ADAM_EPS) + ADAM_WD * w_ref[...])

    sp = pl.BlockSpec((T, Cc), lambda i: (i, 0))
    sh = jax.ShapeDtypeStruct((R, Cc), F32)
    return pl.pallas_call(
        body, name=name, grid=(R // T,), in_specs=[sp] * 4, out_specs=(sp, sp, sp), out_shape=(sh, sh, sh),
        compiler_params=_cparams(("parallel",)),
    )(w, g, m, v)


SMALL_ROWS = 32
REPL_ROWS = 8


def _pad_lanes(v, n=D_MODEL):
    return jnp.pad(v, ((0, 0), (0, n - v.shape[1])))


def kernel(x, norm1_w, w_in, conv_qkv_w, a_log, dt_bias, gdn_norm_w, w_out, norm2_w, w_up, ffn_conv_w, w_down, final_norm_w, loss_target, m_norm1_w, m_w_in, m_conv_qkv_w, m_a_log, m_dt_bias, m_gdn_norm_w, m_w_out, m_norm2_w, m_w_up, m_ffn_conv_w, m_w_down, m_final_norm_w, v_norm1_w, v_w_in, v_conv_qkv_w, v_a_log, v_dt_bias, v_gdn_norm_w, v_w_out, v_norm2_w, v_w_up, v_ffn_conv_w, v_w_down, v_final_norm_w):
    c = lax.axis_index("c")
    q = 2 * lax.axis_index("x") + lax.axis_index("y")
    S = x.shape[1]
    cq = jnp.stack([c, q]).astype(jnp.int32)

    *in_started, in_token = _gather_halves_start([w_in[0].astype(_MXU), conv_qkv_w[0], ffn_conv_w[0]], x, "gather_in_start")
    w_in_l, m_w_in_l, v_w_in_l = (a + in_token[0:1, 0:1] for a in (w_in, m_w_in, v_w_in))
    h1 = _rmsnorm_fwd(x[0], norm1_w + in_token[0:1, 0:1], "norm1")
    rest = [(a[0] + in_token[0:1, 0:1]).astype(_MXU) for a in (w_out, w_up, w_down)]
    in_shards, got_in = _gather_halves_wait(in_started, [w_in_l, m_w_in_l, v_w_in_l, h1, *rest], "gather_in_wait")
    g_in, g_conv, g_fconv = _place_own(in_shards, _sibling_fill(got_in, "fill_in"), cq, "place_in")
    *rest_started, token = _gather_halves_start(rest, g_conv, "gather_rest_start")

    def rest_weights(after):
        shards, got = _gather_halves_wait(rest_started, after, "gather_rest_wait")
        got = _sibling_fill(got, "fill_rest")
        g_out, g_up, g_down = _place_own(shards, got, cq, "place_rest")
        return g_out.reshape(D_MODEL, D_MODEL), g_up, g_down.reshape(D_FF, D_MODEL)
    wp = _wp_assemble(g_in, [token])
    conv_f = jnp.concatenate([g_conv[i] for i in range(N_CHIPS)], axis=1)
    fcw = jnp.concatenate([g_fconv[i] for i in range(N_CHIPS)], axis=1)
    gp = _pad_lanes(jnp.concatenate([a_log, dt_bias], axis=1), 128)
    fnw = final_norm_w[None, :]
    early = {}

    def early_sibling(d_wup, d_wdown):
        *early["sibling"], tok = _grad_sibling_start([d_wup, d_wdown.reshape(N_CHIPS, D_FF // N_CHIPS, D_MODEL)],
                                                     "grad_sibling_early_start")
        return tok

    def early_chips(dx2):
        fams_e, got_e = _grad_sibling_wait(early["sibling"], [dx2], "grad_sibling_early_wait")
        early["parts"] = [_add_sibling(f, r, cq, "add_sibling_" + nm) for f, r, nm in zip(fams_e, got_e, ("w_up", "w_down"))]
        *early["started"], tok = _grad_chips_start([p[1] for p in early["parts"]], "grad_chips_start")
        return tok

    def late_sibling(d_wp, d_wout):
        *early["late_sibling"], tok = _grad_sibling_start(
            [_win_split(d_wp), d_wout.reshape(N_CHIPS, D_MODEL // N_CHIPS, D_MODEL)], "grad_sibling_late_start")
        return tok

    early_grads = (early_sibling, early_chips, late_sibling)

    loss_l, dx, g = _local_step(x[0], loss_target[0], h1, norm1_w, norm2_w + token[0:1, 0:1], fnw, gp, gdn_norm_w, wp,
                                conv_f, fcw, rest_weights, early_grads)
    n_fc = FFN_CONV * D_FF

    def rows_of(v):
        flat = v.reshape(-1)
        return jnp.pad(flat, (0, -flat.shape[0] % D_MODEL)).reshape(-1, D_MODEL)

    gp_row = _pad_lanes(jnp.concatenate([g["gp"][0:1, 0:8], loss_l[0:1, 0:1]], axis=1))
    small = jnp.concatenate([g["n1w"], g["n2w"], g["fnw"], gp_row, _pad_lanes(g["gnw"]),
                             rows_of(g["conv_w"]), rows_of(g["fcw_g"]), rows_of(g["fcw_u"])], axis=0)
    small = jnp.pad(small, ((0, SMALL_ROWS - small.shape[0]), (0, 0)))
    fams, got = _grad_sibling_wait(early["late_sibling"], [dx], "grad_sibling_late_wait")
    parts = [_add_sibling(f, r, cq, "add_sibling_" + nm) for f, r, nm in zip(fams, got, ("w_in", "w_out"))]
    *late_started, late_token = _grad_chips_start([p[1] for p in parts], "grad_chips_late_start")
    got3_e = _grad_chips_wait(early["started"], [dx, g["wp"], late_token], "grad_chips_wait")
    g_w_up, g_w_down, small_all = _grad_share(
        [_add_chips(p[0], r3, cq, "add_chips_" + nm) for p, r3, nm in zip(early["parts"], got3_e, ("w_up", "w_down"))],
        "grad_share_early", small)
    big = {}

    def adamw_big(nm, w, gg, m, v):
        d_, m_, v_ = _adamw(w[0], gg, m[0], v[0], "adamw_" + nm)
        big[nm] = (gg[None], d_[None], m_[None], v_[None])

    adamw_big("w_up", w_up, g_w_up, m_w_up, v_w_up)
    adamw_big("w_down", w_down, g_w_down, m_w_down, v_w_down)
    small_red = _sum_devices(small_all, small, (2 * q + c).astype(jnp.int32).reshape(1))
    loss = small_red[3, 8]
    r0 = 5
    r1 = r0 + GDN_CONV * 3 * GDN_WIDTH // D_MODEL
    r2 = r1 + -(-n_fc // D_MODEL)
    conv_red = small_red[r0:r1].reshape(GDN_CONV, 3 * GDN_WIDTH)
    fc_red = jnp.concatenate([small_red[r1:r2].reshape(-1)[:n_fc].reshape(FFN_CONV, D_FF),
                              small_red[r2:2 * r2 - r1].reshape(-1)[:n_fc].reshape(FFN_CONV, D_FF)], axis=1)
    g_conv_w = lax.dynamic_slice_in_dim(conv_red, q * (3 * GDN_WIDTH // N_CHIPS), 3 * GDN_WIDTH // N_CHIPS, axis=1)
    g_fconv_w = lax.dynamic_slice_in_dim(fc_red, q * (2 * D_FF // N_CHIPS), 2 * D_FF // N_CHIPS, axis=1)
    g_n1w, g_n2w, g_fnw = small_red[0:1], small_red[1:2], small_red[2]
    g_alog, g_dtb, g_gnw = small_red[3:4, 0:4], small_red[3:4, 4:8], small_red[4:5, 0:128]
    adamw_big("conv_qkv_w", conv_qkv_w, g_conv_w, m_conv_qkv_w, v_conv_qkv_w)
    adamw_big("ffn_conv_w", ffn_conv_w, g_fconv_w, m_ffn_conv_w, v_ffn_conv_w)

    def pack_small(n1, n2, fn, al, db, gn):
        return jnp.concatenate([n1, n2, fn[None, :], _pad_lanes(jnp.concatenate([al, db], axis=1)), _pad_lanes(gn),
                                jnp.zeros((REPL_ROWS - 5, D_MODEL), F32)], axis=0)

    sw = pack_small(norm1_w, norm2_w, final_norm_w, a_log, dt_bias, gdn_norm_w)
    sm = pack_small(m_norm1_w, m_norm2_w, m_final_norm_w, m_a_log, m_dt_bias, m_gdn_norm_w)
    sv = pack_small(v_norm1_w, v_norm2_w, v_final_norm_w, v_a_log, v_dt_bias, v_gdn_norm_w)
    sd, smn, svn = _adamw(sw, small_red[:REPL_ROWS], sm, sv, "adamw_small")
    got3 = _grad_chips_wait(late_started, [big["w_up"][1], big["w_down"][1], big["conv_qkv_w"][1], big["ffn_conv_w"][1], sd],
                            "grad_chips_late_wait")
    g_w_in, g_w_out = _grad_share(
        [_add_chips(p[0], r3, cq, "add_chips_" + nm) for p, r3, nm in zip(parts, got3, ("w_in", "w_out"))],
        "grad_share_late")
    adamw_big("w_in", w_in_l, g_w_in, m_w_in_l, v_w_in_l)
    adamw_big("w_out", w_out, g_w_out, m_w_out, v_w_out)

    def unpack_small(t):
        return dict(norm1_w=t[0:1], norm2_w=t[1:2], final_norm_w=t[2], a_log=t[3:4, 0:4], dt_bias=t[3:4, 4:8],
                    gdn_norm_w=t[4:5, 0:128])

    sg = dict(norm1_w=g_n1w, norm2_w=g_n2w, final_norm_w=g_fnw, a_log=g_alog, dt_bias=g_dtb, gdn_norm_w=g_gnw)
    sd, smn, svn = unpack_small(sd), unpack_small(smn), unpack_small(svn)
    names = ["norm1_w", "w_in", "conv_qkv_w", "a_log", "dt_bias", "gdn_norm_w", "w_out", "norm2_w", "w_up",
             "ffn_conv_w", "w_down", "final_norm_w"]
    grads = [big[n][0] if n in big else sg[n] for n in names]
    deltas = [big[n][1] if n in big else sd[n] for n in names]
    new_m = [big[n][2] if n in big else smn[n] for n in names]
    new_v = [big[n][3] if n in big else svn[n] for n in names]
    return (loss, dx[None], *grads, *deltas, *new_m, *new_v)
```

```python
import functools
import math

import numpy as np
import jax
import jax.numpy as jnp
from jax import lax
from jax.experimental import pallas as pl
from jax.experimental.pallas import tpu as pltpu

F32 = jnp.float32
BF16 = jnp.bfloat16
_MXU = jnp.bfloat16
_HI = lax.Precision.HIGHEST
EPS = 1e-6
V7X_VMEM_LIMIT = 56 * 1024 * 1024
MESH = pl.DeviceIdType.MESH

D_MODEL = 1024
GDN_HEADS, GDN_DIM, GDN_CHUNK, GDN_CONV = 4, 128, 64, 4
GDN_WIDTH = GDN_HEADS * GDN_DIM
DIL_HEADS, DIL_DIM = 8, 64
DIL_WIDTH = DIL_HEADS * DIL_DIM
D_FF, FFN_CONV = 2816, 3
IN_COLS = 3592
P_COLS = 3840
P_Z, P_QKVB, P_BA = 1536, 2048, 3584
ATT_T = 1024
ADAM_LR, ADAM_B1, ADAM_B2, ADAM_EPS, ADAM_WD, ADAM_STEP = 0.001, 0.9, 0.999, 1e-08, 0.01, 10
N_CHIPS = 4


def _cparams(sem=None, vmem=None):
    kw = {}
    if sem is not None:
        kw["dimension_semantics"] = sem
    if vmem is not None:
        kw["vmem_limit_bytes"] = vmem
    return pltpu.CompilerParams(**kw)


def _silu(x):
    return x * jax.nn.sigmoid(x)


def _pick_tile(n, cap):
    best = None
    for t in range(128, min(n, cap) + 1, 128):
        if n % t == 0:
            best = t
    return best or n


def _mm(a, b, mode, *, out_dtype=F32, residual=None, name, b_blocks=False, place=None, into=None, tn=None, after=()):
    if mode == "nn":
        M, K = a.shape
        N = b.shape[0] * b.shape[2] if b_blocks else b.shape[1]
    elif mode == "nt":
        (M, K), (N, _) = a.shape, b.shape
    else:
        (K, M), (_, N) = a.shape, b.shape
    tm = _pick_tile(M, 1024)
    tn = b.shape[2] if b_blocks else (tn or _pick_tile(N, 1536))

    def vmem(tm, tn):
        return 2 * (tm * K * a.dtype.itemsize + tn * K * b.dtype.itemsize
                    + tm * tn * (jnp.dtype(out_dtype).itemsize + (4 if residual is not None else 0))) + 3 * tm * tn * 4

    fixed_tn = b_blocks or (place is not None and place[0] == "blocks")
    while vmem(tm, tn) > 40 * 1024 * 1024:
        if (tm >= tn or fixed_tn) and tm % 256 == 0:
            tm //= 2
        elif tn % 256 == 0 and not fixed_tn:
            tn //= 2
        else:
            tm //= 2
    a_spec = pl.BlockSpec((K, tm), lambda j, i: (0, i)) if mode == "tn" else pl.BlockSpec((tm, K), lambda j, i: (i, 0))
    if b_blocks:
        b_spec = pl.BlockSpec((None, K, tn), lambda j, i: (j, 0, 0))
    else:
        b_spec = pl.BlockSpec((tn, K), lambda j, i: (j, 0)) if mode == "nt" else pl.BlockSpec((K, tn), lambda j, i: (0, j))
    r_spec = pl.BlockSpec((tm, tn), lambda j, i: (i, j))
    if place is None:
        o_spec, o_shape = r_spec, (M, N)
    elif place[0] == "rows":
        off = place[2] // tm
        o_spec, o_shape = pl.BlockSpec((tm, tn), lambda j, i: (i + off, j)), (place[1], N)
    else:
        off = place[2]
        o_spec, o_shape = pl.BlockSpec((None, tm, tn), lambda j, i: (j + off, i, 0)), (place[1], M, tn)
    dims = {"nn": (((1,), (0,)), ((), ())), "nt": (((1,), (1,)), ((), ())), "tn": (((0,), (0,)), ((), ()))}[mode]

    def body(*refs):
        a_ref, b_ref = refs[0], refs[1]
        o_ref = refs[-1]
        acc = lax.dot_general(a_ref[...].astype(_MXU), b_ref[...].astype(_MXU), dims, preferred_element_type=F32)
        if residual is not None:
            acc = acc + refs[2][...]
        o_ref[...] = acc.astype(out_dtype)

    ins, specs, alias = [a, b], [a_spec, b_spec], {}
    if residual is not None:
        ins.append(residual)
        specs.append(r_spec)
    if into is not None:
        alias = {len(ins): 0}
        ins.append(into)
        specs.append(pl.BlockSpec(memory_space=pl.ANY))
    ins += list(after)
    specs += [pl.BlockSpec(memory_space=pl.ANY)] * len(after)
    return pl.pallas_call(
        body, name=name, grid=(N // tn, M // tm), in_specs=specs, out_specs=o_spec,
        out_shape=jax.ShapeDtypeStruct(o_shape, out_dtype), input_output_aliases=alias,
        compiler_params=_cparams(("parallel", "parallel"), V7X_VMEM_LIMIT),
    )(*ins)


def _mm_nt_blocks(a_list, b4, name, after=()):
    M = a_list[0].shape[0]
    nb, N, Kb = b4.shape
    tm, tn = _pick_tile(M, 1024), _pick_tile(N, 512)

    def body(a0_ref, a1_ref, b_ref, *rest):
        o_ref = rest[-1]
        acc = None
        for blk in range(nb):
            a_ref = (a0_ref, a1_ref)[blk // 2]
            lo = (blk % 2) * Kb
            t = lax.dot_general(a_ref[:, lo:lo + Kb].astype(_MXU), b_ref[blk].astype(_MXU), (((1,), (1,)), ((), ())),
                                preferred_element_type=F32)
            acc = t if acc is None else acc + t
        o_ref[...] = acc

    a_spec = pl.BlockSpec((tm, 2 * Kb), lambda j, i: (i, 0))
    return pl.pallas_call(
        body, name=name, grid=(N // tn, M // tm),
        in_specs=[a_spec, a_spec, pl.BlockSpec((nb, tn, Kb), lambda j, i: (0, j, 0))]
        + [pl.BlockSpec(memory_space=pl.ANY)] * len(after),
        out_specs=pl.BlockSpec((tm, tn), lambda j, i: (i, j)), out_shape=jax.ShapeDtypeStruct((M, N), F32),
        compiler_params=_cparams(("parallel", "parallel"), V7X_VMEM_LIMIT),
    )(a_list[0], a_list[1], b4, *after)


def _wp_assemble(g_in, after=()):
    nb, Dm, Wb = g_in.shape
    T = 256
    n_lo = P_QKVB - 2 * Wb

    def body(g_ref, *rest):
        g2 = g_ref[2]
        rest[-1][...] = jnp.concatenate(
            [g_ref[0], g_ref[1], g2[:, :n_lo], g2[:, n_lo + 8:], g_ref[3], g2[:, n_lo:n_lo + 8],
             jnp.zeros((T, P_COLS - P_BA - 8), g_in.dtype)], axis=1)

    return pl.pallas_call(
        body, name="wp_assemble", grid=(Dm // T,),
        in_specs=[pl.BlockSpec((nb, T, Wb), lambda i: (0, i, 0))] + [pl.BlockSpec(memory_space=pl.ANY)] * len(after),
        out_specs=pl.BlockSpec((T, P_COLS), lambda i: (i, 0)), out_shape=jax.ShapeDtypeStruct((Dm, P_COLS), g_in.dtype),
        compiler_params=_cparams(("parallel",)),
    )(g_in, *after)


def _win_split(d_wp):
    Dm = d_wp.shape[0]
    Wb = IN_COLS // N_CHIPS
    T = 256

    def body(x_ref, o_ref):
        xv = x_ref[...]
        o_ref[0] = xv[:, 0:Wb]
        o_ref[1] = xv[:, Wb:2 * Wb]
        o_ref[2] = jnp.concatenate([xv[:, 2 * Wb:P_QKVB], xv[:, P_BA:P_BA + 8], xv[:, P_QKVB:3 * Wb - 8]], axis=1)
        o_ref[3] = xv[:, 3 * Wb - 8:P_BA]

    return pl.pallas_call(
        body, name="win_split", grid=(Dm // T,), in_specs=[pl.BlockSpec((T, P_COLS), lambda i: (i, 0))],
        out_specs=pl.BlockSpec((N_CHIPS, T, Wb), lambda i: (0, i, 0)),
        out_shape=jax.ShapeDtypeStruct((N_CHIPS, Dm, Wb), F32), compiler_params=_cparams(("parallel",)),
    )(d_wp)


def _rmsnorm_fwd(x, w, name, after=()):
    S, D = x.shape
    T = _pick_tile(S, 512)

    def body(x_ref, w_ref, *rest):
        xv = x_ref[...]
        rs = lax.rsqrt(jnp.mean(xv * xv, axis=-1, keepdims=True) + EPS)
        rest[-1][...] = (xv * rs * w_ref[...]).astype(rest[-1].dtype)

    return pl.pallas_call(
        body, name=name, grid=(S // T,),
        in_specs=[pl.BlockSpec((T, D), lambda i: (i, 0)), pl.BlockSpec((1, D), lambda i: (0, 0))] + [_ANY] * len(after),
        out_specs=pl.BlockSpec((T, D), lambda i: (i, 0)),
        out_shape=jax.ShapeDtypeStruct((S, D), _MXU),
        compiler_params=_cparams(("parallel",)),
    )(x, w, *after)


def _rmsnorm_bwd(dh, x, w, dres, name, after=()):
    S, D = x.shape
    T = _pick_tile(S, 512)

    def body(dh_ref, x_ref, w_ref, dres_ref, *rest):
        dx_ref, dw_ref = rest[-2:]
        xv = x_ref[...]
        rs = lax.rsqrt(jnp.mean(xv * xv, axis=-1, keepdims=True) + EPS)
        xn = xv * rs
        dhv = dh_ref[...]
        dxn = dhv * w_ref[...]
        dx_ref[...] = dres_ref[...] + rs * (dxn - xn * jnp.mean(dxn * xn, axis=-1, keepdims=True))

        @pl.when(pl.program_id(0) == 0)
        def _():
            dw_ref[...] = jnp.zeros_like(dw_ref)

        dw_ref[...] += jnp.sum(dhv * xn, axis=0, keepdims=True)

    row = pl.BlockSpec((T, D), lambda i: (i, 0))
    vec = pl.BlockSpec((1, D), lambda i: (0, 0))
    return pl.pallas_call(
        body, name=name, grid=(S // T,), in_specs=[row, row, vec, row] + [_ANY] * len(after), out_specs=(row, vec),
        out_shape=(jax.ShapeDtypeStruct((S, D), F32), jax.ShapeDtypeStruct((1, D), F32)),
        compiler_params=_cparams(("arbitrary",)),
    )(dh, x, w, dres, *after)


def _loss_head(x3, w, tgt, name):
    S, D = x3.shape
    T = _pick_tile(S, 512)

    def body(x_ref, w_ref, t_ref, loss_ref, dx_ref, dxn_ref, dw_ref):
        xv = x_ref[...]
        rs = lax.rsqrt(jnp.mean(xv * xv, axis=-1, keepdims=True) + EPS)
        xn = xv * rs
        err = xn * w_ref[...] - t_ref[...]
        dy = err * (1.0 / D)
        dxn = dy * w_ref[...]
        dxv = rs * (dxn - xn * jnp.mean(dxn * xn, axis=-1, keepdims=True))
        dx_ref[...] = dxv
        dxn_ref[...] = dxv.astype(dxn_ref.dtype)

        @pl.when(pl.program_id(0) == 0)
        def _():
            dw_ref[...] = jnp.zeros_like(dw_ref)
            loss_ref[...] = jnp.zeros_like(loss_ref)

        dw_ref[...] += jnp.sum(dy * xn, axis=0, keepdims=True)
        part = jnp.sum(jnp.sum(err * err, axis=-1, keepdims=True), axis=0, keepdims=True) * (0.5 / D)
        loss_ref[...] += jnp.broadcast_to(part, loss_ref.shape)

    row = pl.BlockSpec((T, D), lambda i: (i, 0))
    vec = pl.BlockSpec((1, D), lambda i: (0, 0))
    return pl.pallas_call(
        body, name=name, grid=(S // T,), in_specs=[row, vec, row],
        out_specs=(pl.BlockSpec((8, 128), lambda i: (0, 0)), row, row, vec),
        out_shape=(jax.ShapeDtypeStruct((8, 128), F32), jax.ShapeDtypeStruct((S, D), F32), jax.ShapeDtypeStruct((S, D), _MXU),
                   jax.ShapeDtypeStruct((1, D), F32)),
        compiler_params=_cparams(("arbitrary",)),
    )(x3, w, tgt)


def _shifted(ext, back, lo, n):
    if back == 0:
        return ext[lo:lo + n, :]
    return pltpu.roll(ext, back % ext.shape[0], 0)[lo:lo + n, :]


def _conv_windows(ext, K, T):
    return [_shifted(ext, (K - 1) - i, 8, T) for i in range(K)]


def _conv_taps(ext, w, K, T):
    out = None
    for i, win in enumerate(_conv_windows(ext, K, T)):
        term = win * w[i:i + 1, :]
        out = term if out is None else out + term
    return out


def _conv_taps_t(ext, w, K, T):
    out = None
    for i in range(K):
        term = _shifted(ext, i - (K - 1), 0, T) * w[i:i + 1, :]
        out = term if out is None else out + term
    return out


def _tri_masks(C):
    r = lax.broadcasted_iota(jnp.int32, (C, C), 0)
    c = lax.broadcasted_iota(jnp.int32, (C, C), 1)
    return r == c, r >= c, r > c, r <= c


_NN, _NT, _TN = ((1,), (0,)), ((1,), (1,)), ((0,), (0,))
_GDN_PASSES = dict(qk=1, inv=1, sol=1, scan=1, bwd=1)


def _bdot_raw(a, b, kind, passes):
    dims = ({"NN": ((2,), (1,)), "NT": ((2,), (2,)), "TN": ((1,), (1,))}[kind], ((0,), (0,)))
    if passes == 0:
        return lax.dot_general(a, b, dims, precision=_HI, preferred_element_type=F32)
    ah, bh = a.astype(BF16), b.astype(BF16)
    out = lax.dot_general(ah, bh, dims, preferred_element_type=F32)
    if passes == 3:
        al, bl = (a - ah.astype(F32)).astype(BF16), (b - bh.astype(F32)).astype(BF16)
        out = out + lax.dot_general(ah, bl, dims, preferred_element_type=F32) + lax.dot_general(al, bh, dims, preferred_element_type=F32)
    return out


@functools.partial(jax.custom_vjp, nondiff_argnums=(2, 3))
def _bdot(a, b, kind, passes):
    return _bdot_raw(a, b, kind, passes)


def _bdot_fwd(a, b, kind, passes):
    return _bdot_raw(a, b, kind, passes), (a, b)


def _bdot_bwd(kind, passes, res, ct):
    a, b = res
    if kind == "NN":
        return _bdot_raw(ct, b, "NT", passes), _bdot_raw(a, ct, "TN", passes)
    if kind == "NT":
        return _bdot_raw(ct, b, "NN", passes), _bdot_raw(ct, a, "TN", passes)
    return _bdot_raw(b, ct, "NT", passes), _bdot_raw(a, ct, "NN", passes)


_bdot.defvjp(_bdot_fwd, _bdot_bwd)


def _softplus(x):
    return jnp.maximum(x, 0.0) + jnp.log(1.0 + jnp.exp(-jnp.abs(x)))


def _gdn_stage1(cq, ck, cv, b_col, a_col, alog, dtb, dot=_bdot_raw):
    C = cq.shape[1]
    eye, incl, strict, incl_t = _tri_masks(C)
    qn = cq * lax.rsqrt(jnp.sum(cq * cq, axis=-1, keepdims=True) + EPS) * (GDN_DIM ** -0.5)
    kn = ck * lax.rsqrt(jnp.sum(ck * ck, axis=-1, keepdims=True) + EPS)
    beta = jax.nn.sigmoid(b_col)
    g = -jnp.exp(alog) * _softplus(a_col + dtb)
    g_row = jnp.sum(jnp.where(eye, g, 0.0), axis=1, keepdims=True)
    beta_row = jnp.sum(jnp.where(eye, beta, 0.0), axis=1, keepdims=True)
    gc_col = jnp.sum(jnp.where(incl, g_row, 0.0), axis=2, keepdims=True)
    gc_row = jnp.sum(jnp.where(incl_t, g, 0.0), axis=1, keepdims=True)
    dec = jnp.where(incl, jnp.exp(jnp.where(incl, gc_col - gc_row, 0.0)), 0.0)
    kk = dot(kn, kn, "NT", _GDN_PASSES["qk"])
    qk = dot(qn, kn, "NT", _GDN_PASSES["qk"])
    lmat = jnp.where(strict, dec * kk * beta_row, 0.0)
    attn = dec * qk * beta_row
    gam = jnp.exp(gc_col)
    gc_last = gc_col[:, C - 1:C, :]
    k_end = kn * (jnp.exp(gc_last - gc_col) * beta)
    return lmat, cv, gam * kn, gam * qn, attn, k_end, jnp.exp(gc_last)


def _tri_inv(lmat):
    C = lmat.shape[1]
    eye = _tri_masks(C)[0]
    ps = _GDN_PASSES["inv"]
    p = jnp.where(eye, 1.0, 0.0) - lmat
    lp = _bdot_raw(lmat, lmat, "NN", ps)
    n = int(math.log2(C))
    for s in range(1, n):
        p = p + _bdot_raw(p, lp, "NN", ps)
        if s < n - 1:
            lp = _bdot_raw(lp, lp, "NN", ps)
    return p


def _gated_norm(o, z, gnw):
    on = o * lax.rsqrt(jnp.mean(o * o, axis=-1, keepdims=True) + EPS) * gnw
    return on * _silu(z)


GDN_PG = 2
GDN_SG = 4


def _gdn_pairs(c, ba, gp, G):
    C, W, H = GDN_CHUNK, GDN_WIDTH, GDN_HEADS
    pairs = [(j, h) for j in range(G) for h in range(H)]
    cq, ck, cv = (jnp.stack([c[C * j:C * (j + 1), o + GDN_DIM * h:o + GDN_DIM * (h + 1)] for j, h in pairs]) for o in (0, W, 2 * W))
    b_col = jnp.stack([ba[C * j:C * (j + 1), h:h + 1] for j, h in pairs])
    a_col = jnp.stack([ba[C * j:C * (j + 1), H + h:H + h + 1] for j, h in pairs])
    alog = jnp.stack([gp[0:1, h:h + 1] for j, h in pairs])
    dtb = jnp.stack([gp[0:1, H + h:H + h + 1] for j, h in pairs])
    return pairs, (cq, ck, cv, b_col, a_col, alog, dtb)


def _gdn_pre_specs(S, G):
    C = GDN_CHUNK
    T = C * G
    return dict(
        cur=pl.BlockSpec((T, 3 * GDN_WIDTH), lambda i: (i, 0)),
        prev=pl.BlockSpec((8, 3 * GDN_WIDTH), lambda i: (jnp.maximum(i * (T // 8) - 1, 0), 0)),
        ba=pl.BlockSpec((T, 128), lambda i: (i, P_BA // 128)),
        cw=pl.BlockSpec((GDN_CONV, 3 * GDN_WIDTH), lambda i: (0, 0)),
        vec=pl.BlockSpec((1, 128), lambda i: (0, 0)),
        hd=pl.BlockSpec((GDN_HEADS, T, GDN_DIM), lambda i: (0, i, 0)),
        hc=pl.BlockSpec((GDN_HEADS, T, C), lambda i: (0, i, 0)),
        ge=pl.BlockSpec((G, GDN_HEADS, 8, 128), lambda i: (i, 0, 0, 0)),
    )


def _hd_shape(S, last=GDN_DIM):
    return jax.ShapeDtypeStruct((GDN_HEADS, S, last), F32)


def _gdn_pre(proj, conv_w, gp):
    S = proj.shape[0]
    C, G = GDN_CHUNK, GDN_PG
    nc = S // C
    sp = _gdn_pre_specs(S, G)

    def body(cur_ref, prev_ref, ba_ref, cw_ref, gp_ref, uv_ref, wk_ref, qd_ref, ke_ref, at_ref, ti_ref, ge_ref):
        prev = prev_ref[...] * jnp.where(pl.program_id(0) == 0, 0.0, 1.0)
        c = _silu(_conv_taps(jnp.concatenate([prev, cur_ref[...]], axis=0), cw_ref[...], GDN_CONV, C * G))
        pairs, args = _gdn_pairs(c, ba_ref[...], gp_ref[...], G)
        lmat, v, rk, q_dec, attn, k_end, g_end = _gdn_stage1(*args)
        t = _tri_inv(lmat)
        u_v = _bdot_raw(t, v, "NN", _GDN_PASSES["sol"])
        w_k = _bdot_raw(t, rk, "NN", _GDN_PASSES["sol"])
        for b, (j, h) in enumerate(pairs):
            rows = slice(C * j, C * (j + 1))
            uv_ref[h, rows, :] = u_v[b]
            wk_ref[h, rows, :] = w_k[b]
            qd_ref[h, rows, :] = q_dec[b]
            ke_ref[h, rows, :] = k_end[b]
            at_ref[h, rows, :] = attn[b]
            ti_ref[h, rows, :] = t[b]
            ge_ref[j, h] = jnp.broadcast_to(g_end[b], (8, 128))

    return pl.pallas_call(
        body, name="gdn_pre", grid=(nc // G,),
        in_specs=[sp["cur"], sp["prev"], sp["ba"], sp["cw"], sp["vec"]],
        out_specs=(sp["hd"], sp["hd"], sp["hd"], sp["hd"], sp["hc"], sp["hc"], sp["ge"]),
        out_shape=(_hd_shape(S), _hd_shape(S), _hd_shape(S), _hd_shape(S), _hd_shape(S, C), _hd_shape(S, C),
                   jax.ShapeDtypeStruct((nc, GDN_HEADS, 8, 128), F32)),
        compiler_params=_cparams(("parallel",)),
    )(proj, proj, proj, conv_w, gp)


def _gdn_scan_specs(S, G, rev):
    C = GDN_CHUNK
    T = C * G
    n = S // T
    ci = (lambda i: n - 1 - i) if rev else (lambda i: i)
    return dict(
        hd=pl.BlockSpec((GDN_HEADS, T, GDN_DIM), lambda i: (0, ci(i), 0)),
        hc=pl.BlockSpec((GDN_HEADS, T, C), lambda i: (0, ci(i), 0)),
        ge=pl.BlockSpec((G, GDN_HEADS, 8, 128), lambda i: (ci(i), 0, 0, 0)),
        z=pl.BlockSpec((T, GDN_WIDTH), lambda i: (ci(i), P_Z // GDN_WIDTH)),
        oa=pl.BlockSpec((T, GDN_WIDTH), lambda i: (ci(i), 0)),
        vec=pl.BlockSpec((1, 128), lambda i: (0, 0)),
        st=pl.BlockSpec((G, GDN_HEADS, GDN_DIM, GDN_DIM), lambda i: (ci(i), 0, 0, 0)),
    )


def _gdn_scan(u_v, w_k, q_dec, k_end, attn, g_end, proj, gnw):
    S = proj.shape[0]
    C, G = GDN_CHUNK, GDN_SG
    nc = S // C
    sp = _gdn_scan_specs(S, G, False)
    ps = _GDN_PASSES["scan"]

    def body(uv_ref, wk_ref, qd_ref, ke_ref, at_ref, ge_ref, z_ref, gnw_ref, oa_ref, st_ref, s_scr):
        @pl.when(pl.program_id(0) == 0)
        def _():
            s_scr[...] = jnp.zeros_like(s_scr)

        for j in range(G):
            rows = slice(C * j, C * (j + 1))
            st = s_scr[...]
            st_ref[j] = st
            u = uv_ref[:, rows, :] - _bdot_raw(wk_ref[:, rows, :], st, "NN", ps)
            o = _bdot_raw(qd_ref[:, rows, :], st, "NN", ps) + _bdot_raw(at_ref[:, rows, :], u, "NN", ps)
            s_scr[...] = ge_ref[j][:, 0:1, 0:1] * st + _bdot_raw(ke_ref[:, rows, :], u, "TN", ps)
            for h in range(GDN_HEADS):
                cols = slice(GDN_DIM * h, GDN_DIM * (h + 1))
                oa_ref[rows, cols] = _gated_norm(o[h], z_ref[rows, cols], gnw_ref[...])

    return pl.pallas_call(
        body, name="gdn_scan", grid=(nc // G,),
        in_specs=[sp["hd"], sp["hd"], sp["hd"], sp["hd"], sp["hc"], sp["ge"], sp["z"], sp["vec"]],
        out_specs=(sp["oa"], sp["st"]),
        out_shape=(jax.ShapeDtypeStruct((S, GDN_WIDTH + DIL_WIDTH), F32),
                   jax.ShapeDtypeStruct((nc, GDN_HEADS, GDN_DIM, GDN_DIM), F32)),
        scratch_shapes=[pltpu.VMEM((GDN_HEADS, GDN_DIM, GDN_DIM), F32)],
        compiler_params=_cparams(("arbitrary",)),
    )(u_v, w_k, q_dec, k_end, attn, g_end, proj, gnw)


def _gdn_scan_bwd(u_v, w_k, q_dec, k_end, attn, g_end, proj, gnw, states, d_oa):
    S = proj.shape[0]
    C, G = GDN_CHUNK, GDN_SG
    nc = S // C
    sp = _gdn_scan_specs(S, G, True)
    ps, pb = _GDN_PASSES["scan"], _GDN_PASSES["bwd"]

    def body(uv_ref, wk_ref, qd_ref, ke_ref, at_ref, ge_ref, z_ref, gnw_ref, st_ref, doa_ref,
             duv_ref, dwk_ref, dqd_ref, dke_ref, dat_ref, dge_ref, dz_ref, dgnw_ref, ds_scr):
        @pl.when(pl.program_id(0) == 0)
        def _():
            ds_scr[...] = jnp.zeros_like(ds_scr)
            dgnw_ref[...] = jnp.zeros_like(dgnw_ref)

        dgnw = jnp.zeros((1, 128), F32)
        for j in reversed(range(G)):
            rows = slice(C * j, C * (j + 1))
            st = st_ref[j]
            wk, qd, ke, at = wk_ref[:, rows, :], qd_ref[:, rows, :], ke_ref[:, rows, :], at_ref[:, rows, :]
            u = uv_ref[:, rows, :] - _bdot_raw(wk, st, "NN", ps)
            o = _bdot_raw(qd, st, "NN", ps) + _bdot_raw(at, u, "NN", ps)
            dos = []
            for h in range(GDN_HEADS):
                cols = slice(GDN_DIM * h, GDN_DIM * (h + 1))
                _, vjp2 = jax.vjp(_gated_norm, o[h], z_ref[rows, cols], gnw_ref[...])
                do_h, dz_h, dgn = vjp2(doa_ref[rows, cols])
                dz_ref[rows, cols] = dz_h
                dgnw = dgnw + dgn
                dos.append(do_h)
            do = jnp.stack(dos)
            ds_new = ds_scr[...]
            du = _bdot_raw(at, do, "TN", pb) + _bdot_raw(ke, ds_new, "NN", pb)
            duv_ref[:, rows, :] = du
            dat_ref[:, rows, :] = _bdot_raw(do, u, "NT", pb)
            dqd_ref[:, rows, :] = _bdot_raw(do, st, "NT", pb)
            dke_ref[:, rows, :] = _bdot_raw(u, ds_new, "NT", pb)
            dwk_ref[:, rows, :] = -_bdot_raw(du, st, "NT", pb)
            d_ge = jnp.sum(jnp.sum(st * ds_new, axis=2, keepdims=True), axis=1, keepdims=True)
            dge_ref[j] = jnp.broadcast_to(d_ge, (GDN_HEADS, 8, 128))
            ds_scr[...] = ge_ref[j][:, 0:1, 0:1] * ds_new + _bdot_raw(qd, do, "TN", pb) - _bdot_raw(wk, du, "TN", pb)
        dgnw_ref[...] += dgnw

    return pl.pallas_call(
        body, name="gdn_scan_bwd", grid=(nc // G,),
        in_specs=[sp["hd"], sp["hd"], sp["hd"], sp["hd"], sp["hc"], sp["ge"], sp["z"], sp["vec"], sp["st"], sp["oa"]],
        out_specs=(sp["hd"], sp["hd"], sp["hd"], sp["hd"], sp["hc"], sp["ge"], sp["oa"], sp["vec"]),
        out_shape=(_hd_shape(S), _hd_shape(S), _hd_shape(S), _hd_shape(S), _hd_shape(S, C),
                   jax.ShapeDtypeStruct((nc, GDN_HEADS, 8, 128), F32), jax.ShapeDtypeStruct((S, GDN_WIDTH), F32),
                   jax.ShapeDtypeStruct((1, 128), F32)),
        scratch_shapes=[pltpu.VMEM((GDN_HEADS, GDN_DIM, GDN_DIM), F32)],
        compiler_params=_cparams(("arbitrary",)),
    )(u_v, w_k, q_dec, k_end, attn, g_end, proj, gnw, states, d_oa)


def _gdn_post(proj, conv_w, gp, tinv, u_v, w_k, d_uv, d_wk, d_qd, d_ke, d_at, d_ge):
    S = proj.shape[0]
    C, G = GDN_CHUNK, GDN_PG
    nc = S // C
    sp = _gdn_pre_specs(S, G)
    pb = _GDN_PASSES["bwd"]

    def body(cur_ref, prev_ref, ba_ref, cw_ref, gp_ref, ti_ref, uv_ref, wk_ref, duv_ref, dwk_ref, dqd_ref, dke_ref,
             dat_ref, dge_ref, dpre_ref, dba_ref, dgp_ref):
        i = pl.program_id(0)

        @pl.when(i == 0)
        def _():
            dgp_ref[...] = jnp.zeros_like(dgp_ref)

        prev = prev_ref[...] * jnp.where(i == 0, 0.0, 1.0)
        pre = _conv_taps(jnp.concatenate([prev, cur_ref[...]], axis=0), cw_ref[...], GDN_CONV, C * G)
        sg = jax.nn.sigmoid(pre)
        dsilu = sg * (1.0 + pre * (1.0 - sg))
        pairs, args = _gdn_pairs(pre * sg, ba_ref[...], gp_ref[...], G)
        _, vjp1 = jax.vjp(functools.partial(_gdn_stage1, dot=_bdot), *args)

        def take(ref):
            return jnp.stack([ref[h, C * j:C * (j + 1), :] for j, h in pairs])

        t, u_v, w_k = take(ti_ref), take(uv_ref), take(wk_ref)
        d_v = _bdot_raw(t, take(duv_ref), "TN", pb)
        d_rk = _bdot_raw(t, take(dwk_ref), "TN", pb)
        d_l = -(_bdot_raw(d_v, u_v, "NT", pb) + _bdot_raw(d_rk, w_k, "NT", pb))
        d_ge = jnp.stack([dge_ref[j, h][0:1, 0:1] for j, h in pairs])
        dcq, dck, dcv, db, da, dalog, ddtb = vjp1((d_l, d_v, d_rk, take(dqd_ref), take(dat_ref), take(dke_ref), d_ge))
        lane = lax.broadcasted_iota(jnp.int32, (C, 128), 1)
        lane1 = lax.broadcasted_iota(jnp.int32, (1, 128), 1)
        dgp = jnp.zeros((1, 128), F32)
        for j in range(G):
            rows = slice(C * j, C * (j + 1))
            dba = jnp.zeros((C, 128), F32)
            for h in range(GDN_HEADS):
                b = GDN_HEADS * j + h
                for o_, dcx in ((0, dcq), (GDN_WIDTH, dck), (2 * GDN_WIDTH, dcv)):
                    cols = slice(o_ + GDN_DIM * h, o_ + GDN_DIM * (h + 1))
                    dpre_ref[rows, cols] = dcx[b] * dsilu[rows, cols]
                dba = dba + jnp.where(lane == h, db[b], 0.0) + jnp.where(lane == GDN_HEADS + h, da[b], 0.0)
                dgp = dgp + jnp.where(lane1 == h, dalog[b], 0.0) + jnp.where(lane1 == GDN_HEADS + h, ddtb[b], 0.0)
            dba_ref[rows, :] = dba
        dgp_ref[0:1, :] += dgp

    T = C * G
    return pl.pallas_call(
        body, name="gdn_post", grid=(nc // G,),
        in_specs=[sp["cur"], sp["prev"], sp["ba"], sp["cw"], sp["vec"], sp["hc"], sp["hd"], sp["hd"], sp["hd"], sp["hd"],
                  sp["hd"], sp["hd"], sp["hc"], sp["ge"]],
        out_specs=(sp["cur"], pl.BlockSpec((T, 128), lambda i: (i, 0)), pl.BlockSpec((8, 128), lambda i: (0, 0))),
        out_shape=(jax.ShapeDtypeStruct((S, 3 * GDN_WIDTH), F32), jax.ShapeDtypeStruct((S, 128), F32),
                   jax.ShapeDtypeStruct((8, 128), F32)),
        compiler_params=_cparams(("arbitrary",)),
    )(proj, proj, proj, conv_w, gp, tinv, u_v, w_k, d_uv, d_wk, d_qd, d_ke, d_at, d_ge)


def _conv_bwd(dpre, x, xcol0, w, K, name, tc):
    S, Cc = dpre.shape
    T = _pick_tile(S, 256)
    nt, ncol = S // T, Cc // tc
    xo = xcol0 // tc

    def body(d_ref, dn_ref, x_ref, xp_ref, w_ref, dx_ref, dw_ref):
        i = pl.program_id(1)
        dn = dn_ref[...] * jnp.where(i == nt - 1, 0.0, 1.0)
        dv = d_ref[...]
        ext_d = jnp.concatenate([dv, dn], axis=0)
        dx_ref[...] = _conv_taps_t(ext_d, w_ref[...], K, T).astype(dx_ref.dtype)
        xp = xp_ref[...] * jnp.where(i == 0, 0.0, 1.0)
        ext_x = jnp.concatenate([xp, x_ref[...]], axis=0)

        @pl.when(i == 0)
        def _():
            dw_ref[...] = jnp.zeros_like(dw_ref)

        for k in range(K):
            dw_ref[k:k + 1, :] += jnp.sum(dv * _shifted(ext_x, (K - 1) - k, 8, T), axis=0, keepdims=True)

    r8 = T // 8
    return pl.pallas_call(
        body, name=name, grid=(ncol, nt),
        in_specs=[pl.BlockSpec((T, tc), lambda j, i: (i, j)),
                  pl.BlockSpec((8, tc), lambda j, i: (jnp.minimum((i + 1) * r8, S // 8 - 1), j)),
                  pl.BlockSpec((T, tc), lambda j, i: (i, j + xo)),
                  pl.BlockSpec((8, tc), lambda j, i: (jnp.maximum(i * r8 - 1, 0), j + xo)),
                  pl.BlockSpec((K, tc), lambda j, i: (0, j))],
        out_specs=(pl.BlockSpec((T, tc), lambda j, i: (i, j)), pl.BlockSpec((K, tc), lambda j, i: (0, j))),
        out_shape=(jax.ShapeDtypeStruct((S, Cc), _MXU), jax.ShapeDtypeStruct((K, Cc), F32)),
        compiler_params=_cparams(("parallel", "arbitrary")),
    )(dpre, dpre, x, x, w)


def _dil_bias(nt, T):
    d = (np.arange(nt)[:, None, None] * T + np.arange(T)[None, None, :] - np.arange(T)[None, :, None])
    cnt = ((d >= 0) & (d <= 128)).astype(np.float64) + ((d >= 0) & (d % 4 == 0) & (d <= 512)) + ((d >= 0) & (d % 16 == 0))
    return jnp.asarray(np.where(cnt > 0, np.log(np.maximum(cnt, 1.0)), -1e30), dtype=F32)


def _attn_fwd(proj, mix):
    S = proj.shape[0]
    T = min(ATT_T, S)
    nt = S // T
    bias = _dil_bias(nt, T)
    scale = DIL_DIM ** -0.5
    npair = DIL_WIDTH // 128
    qb0, kb0, vb0 = P_QKVB // 128, (P_QKVB + DIL_WIDTH) // 128, (P_QKVB + 2 * DIL_WIDTH) // 128

    def body(q_ref, k_ref, v_ref, b_ref, mix_ref, o_ref, lse_ref):
        i = pl.program_id(1)
        qs = (q_ref[...] * scale).astype(_MXU)

        def step(j, carry):
            kt = k_ref[pl.ds(pl.multiple_of(j * T, T), T), :].astype(_MXU)
            vt = v_ref[pl.ds(pl.multiple_of(j * T, T), T), :].astype(_MXU)
            bt = b_ref[i - j]
            out = []
            for hh in range(2):
                m, l, acc = carry[hh]
                sl = slice(hh * DIL_DIM, (hh + 1) * DIL_DIM)
                s = lax.dot_general(kt[:, sl], qs[:, sl], (_NT, ((), ())), preferred_element_type=F32) + bt
                m_new = jnp.maximum(m, jnp.max(s, axis=0, keepdims=True))
                p = jnp.exp(s - m_new)
                a = jnp.exp(m - m_new)
                l = a * l + jnp.sum(p, axis=0, keepdims=True)
                acc = a * acc + lax.dot_general(vt[:, sl], p.astype(_MXU), (_TN, ((), ())), preferred_element_type=F32)
                out.append((m_new, l, acc))
            return tuple(out)

        init = tuple((jnp.full((1, T), -1e30, F32), jnp.zeros((1, T), F32), jnp.zeros((DIL_DIM, T), F32)) for _ in range(2))
        res = lax.fori_loop(0, i + 1, step, init)
        lse_ref[...] = jnp.zeros_like(lse_ref)
        for hh in range(2):
            m, l, acc = res[hh]
            o_ref[:, hh * DIL_DIM:(hh + 1) * DIL_DIM] = (acc / l).T
            lse_ref[hh:hh + 1, :] = m + jnp.log(l)

    return pl.pallas_call(
        body, name="attn_fwd", grid=(npair, nt),
        in_specs=[pl.BlockSpec((T, 128), lambda p, i: (i, qb0 + p)),
                  pl.BlockSpec((S, 128), lambda p, i: (0, kb0 + p)),
                  pl.BlockSpec((S, 128), lambda p, i: (0, vb0 + p)),
                  pl.BlockSpec((nt, T, T), lambda p, i: (0, 0, 0)), pl.BlockSpec(memory_space=pl.ANY)],
        out_specs=(pl.BlockSpec((T, 128), lambda p, i: (i, GDN_WIDTH // 128 + p)),
                   pl.BlockSpec((None, None, 8, T), lambda p, i: (p, i, 0, 0))),
        out_shape=(jax.ShapeDtypeStruct(mix.shape, F32), jax.ShapeDtypeStruct((npair, nt, 8, T), F32)),
        input_output_aliases={4: 0},
        compiler_params=_cparams(("parallel", "parallel")),
    )(proj, proj, proj, bias, mix)


def _attn_bwd(proj, mix, lse, d_mix):
    S = proj.shape[0]
    T = min(ATT_T, S)
    nt = S // T
    bias = _dil_bias(nt, T)
    scale = DIL_DIM ** -0.5
    npair = DIL_WIDTH // 128
    qb0, kb0, vb0 = P_QKVB // 128, (P_QKVB + DIL_WIDTH) // 128, (P_QKVB + 2 * DIL_WIDTH) // 128

    def body(q_ref, k_ref, v_ref, o_ref, lse_ref, do_ref, b_ref, dq_ref, dk_ref, dv_ref, dq_scr):
        j = pl.program_id(1)

        @pl.when(j == 0)
        def _():
            dq_scr[...] = jnp.zeros_like(dq_scr)

        kt = k_ref[...].astype(_MXU)
        vt = v_ref[...].astype(_MXU)
        ones = jnp.ones((8, DIL_DIM), F32)

        def step(i, carry):
            rows = pl.ds(pl.multiple_of(i * T, T), T)
            qs = (q_ref[rows, :] * scale).astype(_MXU)
            dov = do_ref[rows, :]
            prod = dov * o_ref[rows, :]
            lsev = lse_ref[i]
            dob = dov.astype(_MXU)
            bt = b_ref[i - j]
            out = []
            dqs = []
            for hh in range(2):
                dk, dv = carry[hh]
                sl = slice(hh * DIL_DIM, (hh + 1) * DIL_DIM)
                s = lax.dot_general(kt[:, sl], qs[:, sl], (_NT, ((), ())), preferred_element_type=F32) + bt
                p = jnp.exp(s - lsev[hh:hh + 1, :])
                delta = lax.dot_general(ones, prod[:, sl], (_NT, ((), ())), precision=_HI, preferred_element_type=F32)[0:1, :]
                dp = lax.dot_general(vt[:, sl], dob[:, sl], (_NT, ((), ())), preferred_element_type=F32)
                ds = (p * (dp - delta)).astype(_MXU)
                dv = dv + lax.dot_general(p.astype(_MXU), dob[:, sl], (_NN, ((), ())), preferred_element_type=F32)
                dk = dk + lax.dot_general(ds, qs[:, sl], (_NN, ((), ())), preferred_element_type=F32)
                dqs.append(lax.dot_general(ds, kt[:, sl], (_TN, ((), ())), preferred_element_type=F32) * scale)
                out.append((dk, dv))
            dq_scr[rows, :] += jnp.concatenate(dqs, axis=1)
            return tuple(out)

        init = tuple((jnp.zeros((T, DIL_DIM), F32), jnp.zeros((T, DIL_DIM), F32)) for _ in range(2))
        res = lax.fori_loop(j, nt, step, init)
        dk_ref[...] = jnp.concatenate([res[0][0], res[1][0]], axis=1).astype(dk_ref.dtype)
        dv_ref[...] = jnp.concatenate([res[0][1], res[1][1]], axis=1).astype(dv_ref.dtype)

        @pl.when(j == nt - 1)
        def _():
            dq_ref[...] = dq_scr[...].astype(dq_ref.dtype)

    full = lambda c0: pl.BlockSpec((S, 128), lambda p, j: (0, c0 + p))
    tile = lambda c0: pl.BlockSpec((T, 128), lambda p, j: (j, c0 + p))
    out3 = jax.ShapeDtypeStruct((S, DIL_WIDTH), _MXU)
    return pl.pallas_call(
        body, name="attn_bwd", grid=(npair, nt),
        in_specs=[full(qb0), tile(kb0), tile(vb0), full(GDN_WIDTH // 128),
                  pl.BlockSpec((None, nt, 8, T), lambda p, j: (p, 0, 0, 0)), full(GDN_WIDTH // 128),
                  pl.BlockSpec((nt, T, T), lambda p, j: (0, 0, 0))],
        out_specs=(full(0), tile(0), tile(0)),
        out_shape=(out3, out3, out3),
        scratch_shapes=[pltpu.VMEM((S, 128), F32)],
        compiler_params=_cparams(("parallel", "arbitrary")),
    )(proj, proj, proj, mix, lse, d_mix, bias)


def _ffn_act(up, cw):
    S, Cc = up.shape[0], up.shape[1] // 2
    T, tc = _pick_tile(S, 256), _pick_tile(Cc, 1536)
    r16 = T // 16
    nct = Cc // tc

    def body(g_ref, gp_ref, u_ref, up_ref, wg_ref, wu_ref, o_ref):
        keep = jnp.where(pl.program_id(1) == 0, 0.0, 1.0)
        cg = _conv_taps(jnp.concatenate([gp_ref[8:16, :].astype(F32) * keep, g_ref[...].astype(F32)], axis=0),
                        wg_ref[...], FFN_CONV, T)
        cu = _conv_taps(jnp.concatenate([up_ref[8:16, :].astype(F32) * keep, u_ref[...].astype(F32)], axis=0),
                        wu_ref[...], FFN_CONV, T)
        o_ref[...] = (_silu(cg) * cu).astype(o_ref.dtype)

    cur = lambda o: pl.BlockSpec((T, tc), lambda j, i: (i, j + o))
    prev = lambda o: pl.BlockSpec((16, tc), lambda j, i: (jnp.maximum(i * r16 - 1, 0), j + o))
    wsp = lambda o: pl.BlockSpec((FFN_CONV, tc), lambda j, i: (0, j + o))
    return pl.pallas_call(
        body, name="ffn_act", grid=(nct, S // T),
        in_specs=[cur(0), prev(0), cur(nct), prev(nct), wsp(0), wsp(nct)], out_specs=cur(0),
        out_shape=jax.ShapeDtypeStruct((S, Cc), _MXU),
        compiler_params=_cparams(("parallel", "parallel")),
    )(up, up, up, up, cw, cw)


def _ffn_act_bwd(d_act, up, cw):
    S, Cc = up.shape[0], up.shape[1] // 2
    T, tc = _pick_tile(S, 256), _pick_tile(Cc, 1536)
    r8, r16 = T // 8, T // 16
    nt = S // T
    nct = Cc // tc
    K = FFN_CONV

    def body(da_ref, dan_ref, g_ref, gp_ref, gn_ref, u_ref, up_ref, un_ref, wg_ref, wu_ref,
             dg_ref, du_ref, dwg_ref, dwu_ref):
        i = pl.program_id(1)
        keep_p = jnp.where(i == 0, 0.0, 1.0)
        keep_n = jnp.where(i == nt - 1, 0.0, 1.0)
        wg, wu = wg_ref[...], wu_ref[...]
        xg = jnp.concatenate([gp_ref[8:16, :].astype(F32) * keep_p, g_ref[...].astype(F32),
                              gn_ref[0:8, :].astype(F32) * keep_n], axis=0)
        xu = jnp.concatenate([up_ref[8:16, :].astype(F32) * keep_p, u_ref[...].astype(F32),
                              un_ref[0:8, :].astype(F32) * keep_n], axis=0)
        cg = _conv_taps(xg, wg, K, T + 8)
        cu = _conv_taps(xu, wu, K, T + 8)
        da = jnp.concatenate([da_ref[...], dan_ref[...] * keep_n], axis=0)
        sg = jax.nn.sigmoid(cg)
        d_cg = da * cu * (sg * (1.0 + cg * (1.0 - sg)))
        d_cu = da * (cg * sg)
        dg_ref[...] = _conv_taps_t(d_cg, wg, K, T).astype(dg_ref.dtype)
        du_ref[...] = _conv_taps_t(d_cu, wu, K, T).astype(du_ref.dtype)

        @pl.when(i == 0)
        def _():
            dwg_ref[...] = jnp.zeros_like(dwg_ref)
            dwu_ref[...] = jnp.zeros_like(dwu_ref)

        for k in range(K):
            dwg_ref[k:k + 1, :] += jnp.sum(d_cg[0:T, :] * _shifted(xg, (K - 1) - k, 8, T), axis=0, keepdims=True)
            dwu_ref[k:k + 1, :] += jnp.sum(d_cu[0:T, :] * _shifted(xu, (K - 1) - k, 8, T), axis=0, keepdims=True)

    cur = lambda o: pl.BlockSpec((T, tc), lambda j, i: (i, j + o))
    prev = lambda o: pl.BlockSpec((16, tc), lambda j, i: (jnp.maximum(i * r16 - 1, 0), j + o))
    nxt = lambda o: pl.BlockSpec((16, tc), lambda j, i: (jnp.minimum((i + 1) * r16, S // 16 - 1), j + o))
    nxt8 = pl.BlockSpec((8, tc), lambda j, i: (jnp.minimum((i + 1) * r8, S // 8 - 1), j))
    wsp = lambda o: pl.BlockSpec((K, tc), lambda j, i: (0, j + o))
    return pl.pallas_call(
        body, name="ffn_act_bwd", grid=(nct, nt),
        in_specs=[cur(0), nxt8, cur(0), prev(0), nxt(0), cur(nct), prev(nct), nxt(nct), wsp(0), wsp(nct)],
        out_specs=(cur(0), cur(0), wsp(0), wsp(0)),
        out_shape=(jax.ShapeDtypeStruct((S, Cc), _MXU), jax.ShapeDtypeStruct((S, Cc), _MXU),
                   jax.ShapeDtypeStruct((K, Cc), F32), jax.ShapeDtypeStruct((K, Cc), F32)),
        compiler_params=_cparams(("parallel", "arbitrary")),
    )(d_act, d_act, up, up, up, up, up, up, cw, cw)


def _local_step(x, tgt, h1, n1w, n2w, fnw, gp, gnw, wp, conv_w, fcw, rest_weights, early_grads):
    proj = _mm(h1, wp, "nn", name="proj")
    u_v, w_k, q_dec, k_end, attn, tinv, g_end = _gdn_pre(proj, conv_w, gp)
    mix, states = _gdn_scan(u_v, w_k, q_dec, k_end, attn, g_end, proj, gnw)
    mix, lse = _attn_fwd(proj, mix)
    w_out, w_up4, w_down = rest_weights([mix])
    x2 = _mm(mix, w_out, "nn", residual=x, name="outproj")
    h2 = _rmsnorm_fwd(x2, n2w, "norm2")
    up = _mm(h2, w_up4, "nn", b_blocks=True, out_dtype=_MXU, name="up")
    act = _ffn_act(up, fcw)
    x3 = _mm(act, w_down, "nn", residual=x2, name="down")
    loss, dx3, dx3n, d_fnw = _loss_head(x3, fnw, tgt, "loss_head")
    d_act = _mm(dx3n, w_down, "nt", name="d_act")
    d_wdown = _mm(act, dx3n, "tn", name="d_wdown")
    d_upg, d_upu, d_fcwg, d_fcwu = _ffn_act_bwd(d_act, up, fcw)
    d_wup = _mm(h2, d_upg, "tn", place=("blocks", N_CHIPS, 0), tn=w_up4.shape[2], name="d_wgate")
    d_wup = _mm(h2, d_upu, "tn", place=("blocks", N_CHIPS, N_CHIPS // 2), tn=w_up4.shape[2], into=d_wup, name="d_wup")
    token = early_grads[0](d_wup, d_wdown)
    d_h2 = _mm_nt_blocks([d_upg, d_upu], w_up4, "d_h2", after=[token])
    dx2, d_n2w = _rmsnorm_bwd(d_h2, x2, n2w, dx3, "norm2_bwd", after=[token])
    token = early_grads[1](dx2)
    d_mix = _mm(dx2, w_out, "nt", name="d_mix")
    d_wout = _mm(mix, dx2, "tn", name="d_wout")
    dq_b, dk_b, dv_b = _attn_bwd(proj, mix, lse, d_mix)
    d_uv, d_wk, d_qd, d_ke, d_at, d_ge, d_z, d_gnw = _gdn_scan_bwd(u_v, w_k, q_dec, k_end, attn, g_end, proj,
                                                                   gnw + token[0:1, 0:1], states, d_mix)
    d_pre, d_ba, d_gp = _gdn_post(proj, conv_w, gp, tinv, u_v, w_k, d_uv, d_wk, d_qd, d_ke, d_at, d_ge)
    d_qkva, d_convw = _conv_bwd(d_pre, proj, 0, conv_w, GDN_CONV, "gdn_conv_bwd", 512)
    d_proj = jnp.concatenate([d_qkva, d_z.astype(_MXU), dq_b, dk_b, dv_b, d_ba.astype(_MXU),
                              jnp.zeros((x.shape[0], P_COLS - P_BA - 128), _MXU)], axis=1)
    d_wp = _mm(h1, d_proj, "tn", name="d_wp")
    token = early_grads[2](d_wp, d_wout)
    d_h1 = _mm(d_proj, wp, "nt", name="d_h1", after=[token])
    dx, d_n1w = _rmsnorm_bwd(d_h1, x, n1w, dx2, "norm1_bwd", after=[token])
    grads = dict(wp=d_wp, conv_w=d_convw, w_out=d_wout, w_up=d_wup, fcw_g=d_fcwg, fcw_u=d_fcwu, w_down=d_wdown,
                 n1w=d_n1w, n2w=d_n2w, fnw=d_fnw, gp=d_gp, gnw=d_gnw)
    return loss, dx, grads


_HBM = pl.BlockSpec(memory_space=pltpu.HBM)


def _pos():
    return lax.axis_index("x"), lax.axis_index("y"), lax.axis_index("c")


def _other_chips(x, y):
    return [(1 - x, y), (x, 1 - y), (1 - x, 1 - y)]


def _halvable(shape):
    return shape[0] % 32 == 0


def _rows_of_half(shape, half):
    if not _halvable(shape):
        return pl.ds(0, shape[0])
    return pl.ds(pl.multiple_of(half * (shape[0] // 2), 16), shape[0] // 2)


_SEM = pl.BlockSpec(memory_space=pltpu.SEMAPHORE)
_ANY = pl.BlockSpec(memory_space=pl.ANY)
_DATAFLOW = pltpu.SideEffectType.DATAFLOW_SIDE_EFFECTING


def _in_hbm(a):
    return pltpu.with_memory_space_constraint(a, pltpu.HBM)


def _halves_copy(src_refs, land_refs, send_sems, recv_sems, shapes, a, j, block, x, y, c):
    px, py = _other_chips(x, y)[j]
    rows = _rows_of_half(shapes[a], c)
    return pltpu.make_async_remote_copy(
        src_ref=src_refs[a].at[rows, :], dst_ref=land_refs[a].at[block, rows, :], send_sem=send_sems.at[3 * a + j],
        recv_sem=recv_sems.at[3 * a + j], device_id=(px, py, c), device_id_type=MESH)


def _gather_halves_start(shards, after, name):
    n = len(shards)
    shapes = [s.shape for s in shards]

    def body(*refs):
        ins, lands = refs[:n], refs[n:2 * n]
        send_sems, recv_sems = refs[2 * n + 1], refs[2 * n + 2]
        token = refs[-1]
        x, y, c = _pos()
        q = 2 * x + y
        for a in range(n):
            for j in range(3):
                _halves_copy(ins, lands, send_sems, recv_sems, shapes, a, j, q, x, y, c).start()
        token[...] = jnp.zeros_like(token)

    land_shapes = [(N_CHIPS,) + s.shape for s in shards]
    return pl.pallas_call(
        body, name=name,
        out_shape=(pltpu.SemaphoreType.DMA((3 * n,)), pltpu.SemaphoreType.DMA((3 * n,)),
                   *[pltpu.HBM(s.shape, s.dtype) for s in shards],
                   *[pltpu.HBM(ls, s.dtype) for ls, s in zip(land_shapes, shards)],
                   jax.ShapeDtypeStruct((8, 128), F32)),
        in_specs=[_HBM] * (2 * n) + [_ANY],
        out_specs=(_SEM, _SEM, *[_HBM] * (2 * n), pl.BlockSpec(memory_space=pltpu.VMEM)),
        input_output_aliases={a: 2 + a for a in range(2 * n)},
        compiler_params=pltpu.CompilerParams(has_side_effects=_DATAFLOW),
    )(*[_in_hbm(s) for s in shards], *[_in_hbm(lax.empty(ls, s.dtype)) for ls, s in zip(land_shapes, shards)], after)


def _gather_halves_wait(started, after, name):
    send_sems, recv_sems, *thru = started
    n = len(thru) // 2
    shapes = [t.shape for t in thru[:n]]

    def body(*refs):
        ins, lands = refs[:n], refs[n:2 * n]
        send_sems, recv_sems = refs[2 * n], refs[2 * n + 1]
        x, y, c = _pos()
        q = 2 * x + y
        chips = _other_chips(x, y)
        for a in range(n):
            for j, (px, py) in enumerate(chips):
                _halves_copy(ins, lands, send_sems, recv_sems, shapes, a, j, q, x, y, c).wait_send()
                _halves_copy(ins, lands, send_sems, recv_sems, shapes, a, j, 2 * px + py, x, y, c).wait_recv()

    outs = pl.pallas_call(
        body, name=name, out_shape=[pltpu.HBM(t.shape, t.dtype) for t in thru],
        in_specs=[_HBM] * (2 * n) + [_SEM, _SEM] + [_ANY] * len(after), out_specs=[_HBM] * (2 * n),
        input_output_aliases={a: a for a in range(2 * n)},
        compiler_params=pltpu.CompilerParams(has_side_effects=_DATAFLOW),
    )(*thru, send_sems, recv_sems, *after)
    return outs[:n], outs[n:]


def _sibling_fill(gathered, name):
    big = [a for a, g in enumerate(gathered) if _halvable(g.shape[1:])]
    n = len(gathered)

    def body(*refs):
        ins, outs = refs[:n], refs[n:2 * n]
        send_sems, recv_sems = refs[2 * n:]
        x, y, c = _pos()
        chips = _other_chips(x, y)

        def copy(k, j, half):
            a = big[k]
            px, py = chips[j]
            rows = _rows_of_half(gathered[a].shape[1:], half)
            return pltpu.make_async_remote_copy(
                src_ref=ins[a].at[2 * px + py, rows, :], dst_ref=outs[a].at[2 * px + py, rows, :],
                send_sem=send_sems.at[3 * k + j], recv_sem=recv_sems.at[3 * k + j],
                device_id=(x, y, 1 - c), device_id_type=MESH)

        sends = [copy(k, j, c) for k in range(len(big)) for j in range(3)]
        for cp in sends:
            cp.start()
        for k in range(len(big)):
            for j in range(3):
                copy(k, j, 1 - c).wait_recv()
        for cp in sends:
            cp.wait_send()

    return pl.pallas_call(
        body, name=name, in_specs=[_HBM] * n, out_specs=[_HBM] * n,
        out_shape=[jax.ShapeDtypeStruct(g.shape, g.dtype) for g in gathered],
        input_output_aliases={a: a for a in range(n)},
        scratch_shapes=[pltpu.SemaphoreType.DMA((3 * len(big),)), pltpu.SemaphoreType.DMA((3 * len(big),))],
    )(*gathered)


def _place_own(shards, gathered, cq, name):
    n = len(shards)
    steps = 4

    def body(cq_ref, *refs):
        for a in range(n):
            refs[2 * n + a][...] = refs[a][...]

    def tile(shape):
        return shape[0] // steps if _halvable(shape) else shape[0]

    in_specs = [pl.BlockSpec((tile(s.shape), s.shape[1]), (lambda i, s_: (i, 0)) if _halvable(s.shape) else (lambda i, s_: (0, 0)))
                for s in shards]
    in_specs += [pl.BlockSpec(memory_space=pl.ANY)] * n
    out_specs = [pl.BlockSpec((None, tile(s.shape), s.shape[1]),
                              (lambda i, s_: (s_[1], i, 0)) if _halvable(s.shape) else (lambda i, s_: (s_[1], 0, 0)))
                 for s in shards]
    gs = pltpu.PrefetchScalarGridSpec(num_scalar_prefetch=1, grid=(steps,), in_specs=in_specs, out_specs=out_specs)
    return pl.pallas_call(
        body, name=name, grid_spec=gs, out_shape=[jax.ShapeDtypeStruct(g.shape, g.dtype) for g in gathered],
        input_output_aliases={1 + n + a: a for a in range(n)},
        compiler_params=_cparams(("arbitrary",)),
    )(cq, *shards, *gathered)


def _half_rows(ref, c, rh):
    return ref.at[:, pl.ds(pl.multiple_of(c * rh, 8), rh), :]


def _chips_copy(src_refs, land_refs, send_sems, recv_sems, a, j, x, y, c):
    px, py = _other_chips(x, y)[j]
    return pltpu.make_async_remote_copy(src_ref=src_refs[a].at[2 * px + py], dst_ref=land_refs[a].at[j],
                                        send_sem=send_sems.at[3 * a + j], recv_sem=recv_sems.at[3 * a + j],
                                        device_id=(px, py, c), device_id_type=MESH)


def _grad_chips_start(parts, name):
    n = len(parts)

    def body(*refs):
        ins, lands = refs[:n], refs[n:2 * n]
        send_sems, recv_sems = refs[2 * n], refs[2 * n + 1]
        token = refs[-1]
        x, y, c = _pos()
        for a in range(n):
            for j in range(3):
                _chips_copy(ins, lands, send_sems, recv_sems, a, j, x, y, c).start()
        token[...] = jnp.zeros_like(token)

    land_shapes = [(3,) + p.shape[1:] for p in parts]
    return pl.pallas_call(
        body, name=name,
        out_shape=(pltpu.SemaphoreType.DMA((3 * n,)), pltpu.SemaphoreType.DMA((3 * n,)),
                   *[pltpu.HBM(p.shape, p.dtype) for p in parts],
                   *[pltpu.HBM(ls, p.dtype) for ls, p in zip(land_shapes, parts)],
                   jax.ShapeDtypeStruct((8, 128), F32)),
        in_specs=[_HBM] * (2 * n),
        out_specs=(_SEM, _SEM, *[_HBM] * (2 * n), pl.BlockSpec(memory_space=pltpu.VMEM)),
        input_output_aliases={a: 2 + a for a in range(2 * n)},
        compiler_params=pltpu.CompilerParams(has_side_effects=_DATAFLOW),
    )(*[_in_hbm(p) for p in parts], *[_in_hbm(lax.empty(ls, p.dtype)) for ls, p in zip(land_shapes, parts)])


def _grad_chips_wait(started, after, name):
    send_sems, recv_sems, *thru = started
    n = len(thru) // 2

    def body(*refs):
        ins, lands = refs[:n], refs[n:2 * n]
        send_sems, recv_sems = refs[2 * n], refs[2 * n + 1]
        x, y, c = _pos()
        for a in range(n):
            for j in range(3):
                cp = _chips_copy(ins, lands, send_sems, recv_sems, a, j, x, y, c)
                cp.wait_send()
                cp.wait_recv()

    outs = pl.pallas_call(
        body, name=name, out_shape=[pltpu.HBM(t.shape, t.dtype) for t in thru],
        in_specs=[_HBM] * (2 * n) + [_SEM, _SEM] + [_ANY] * len(after), out_specs=[_HBM] * (2 * n),
        input_output_aliases={a: a for a in range(2 * n)},
        compiler_params=pltpu.CompilerParams(has_side_effects=_DATAFLOW),
    )(*thru, send_sems, recv_sems, *after)
    return outs[n:]


def _sibling_copy(src_refs, land_refs, send_sems, recv_sems, rhs, a, c, x, y):
    return pltpu.make_async_remote_copy(src_ref=_half_rows(src_refs[a], 1 - c, rhs[a]), dst_ref=land_refs[a],
                                        send_sem=send_sems.at[a], recv_sem=recv_sems.at[a],
                                        device_id=(x, y, 1 - c), device_id_type=MESH)


def _grad_sibling_start(fams, name):
    n = len(fams)
    rhs = [f.shape[1] // 2 for f in fams]

    def body(*refs):
        ins, lands = refs[:n], refs[n:2 * n]
        send_sems, recv_sems = refs[2 * n], refs[2 * n + 1]
        token = refs[-1]
        x, y, c = _pos()
        for a in range(n):
            _sibling_copy(ins, lands, send_sems, recv_sems, rhs, a, c, x, y).start()
        token[...] = jnp.zeros_like(token)

    land_shapes = [(f.shape[0], f.shape[1] // 2, f.shape[2]) for f in fams]
    return pl.pallas_call(
        body, name=name,
        out_shape=(pltpu.SemaphoreType.DMA((n,)), pltpu.SemaphoreType.DMA((n,)),
                   *[pltpu.HBM(f.shape, f.dtype) for f in fams],
                   *[pltpu.HBM(ls, f.dtype) for ls, f in zip(land_shapes, fams)],
                   jax.ShapeDtypeStruct((8, 128), F32)),
        in_specs=[_HBM] * (2 * n),
        out_specs=(_SEM, _SEM, *[_HBM] * (2 * n), pl.BlockSpec(memory_space=pltpu.VMEM)),
        input_output_aliases={a: 2 + a for a in range(2 * n)},
        compiler_params=pltpu.CompilerParams(has_side_effects=_DATAFLOW),
    )(*[_in_hbm(f) for f in fams], *[_in_hbm(lax.empty(ls, f.dtype)) for ls, f in zip(land_shapes, fams)])


def _grad_sibling_wait(started, after, name):
    send_sems, recv_sems, *thru = started
    n = len(thru) // 2
    rhs = [t.shape[1] // 2 for t in thru[:n]]

    def body(*refs):
        ins, lands = refs[:n], refs[n:2 * n]
        send_sems, recv_sems = refs[2 * n], refs[2 * n + 1]
        x, y, c = _pos()
        for a in range(n):
            cp = _sibling_copy(ins, lands, send_sems, recv_sems, rhs, a, c, x, y)
            cp.wait_send()
            cp.wait_recv()

    outs = pl.pallas_call(
        body, name=name, out_shape=[pltpu.HBM(t.shape, t.dtype) for t in thru],
        in_specs=[_HBM] * (2 * n) + [_SEM, _SEM] + [_ANY] * len(after), out_specs=[_HBM] * (2 * n),
        input_output_aliases={a: a for a in range(2 * n)},
        compiler_params=pltpu.CompilerParams(has_side_effects=_DATAFLOW),
    )(*thru, send_sems, recv_sems, *after)
    return outs[:n], outs[n:]


def _grad_share(fulls, name, small=None):
    n = len(fulls)
    ns = 0 if small is None else 1
    rhs = [f.shape[0] // 2 for f in fulls]

    def body(*refs):
        ins, outs = refs[:n], refs[n + ns:2 * n + ns]
        send_sems, recv_sems = refs[2 * (n + ns)], refs[2 * (n + ns) + 1]
        x, y, c = _pos()

        def copy(a, half):
            rows = pl.ds(pl.multiple_of(half * rhs[a], 8), rhs[a])
            return pltpu.make_async_remote_copy(src_ref=ins[a].at[rows, :], dst_ref=outs[a].at[rows, :],
                                                send_sem=send_sems.at[7 * ns + a], recv_sem=recv_sems.at[7 * ns + a],
                                                device_id=(x, y, 1 - c), device_id_type=MESH)

        sends = [copy(a, c) for a in range(n)]
        for cp in sends:
            cp.start()
        if ns:
            small_ref, all_ref = refs[n], refs[2 * n + 1]
            me = 4 * x + 2 * y + c

            def peer(r):
                dx, dy, dc = (r >> 2) & 1, (r >> 1) & 1, r & 1
                return (x if dx == 0 else 1 - x), (y if dy == 0 else 1 - y), (c if dc == 0 else 1 - c)

            def small_copy(r, slot):
                return pltpu.make_async_remote_copy(src_ref=small_ref, dst_ref=all_ref.at[slot], send_sem=send_sems.at[r - 1],
                                                    recv_sem=recv_sems.at[r - 1], device_id=peer(r), device_id_type=MESH)

            smalls = [small_copy(r, me) for r in range(1, 8)]
            for cp in smalls:
                cp.start()
            for r in range(1, 8):
                px, py, pc = peer(r)
                small_copy(r, 4 * px + 2 * py + pc).wait_recv()
            sends = sends + smalls
        for a in range(n):
            copy(a, 1 - c).wait_recv()
        for cp in sends:
            cp.wait_send()

    return pl.pallas_call(
        body, name=name, in_specs=[_HBM] * (n + ns), out_specs=[_HBM] * (n + ns),
        out_shape=[jax.ShapeDtypeStruct(f.shape, f.dtype) for f in fulls]
        + ([jax.ShapeDtypeStruct((8,) + small.shape, small.dtype)] if ns else []),
        input_output_aliases={a: a for a in range(n)},
        scratch_shapes=[pltpu.SemaphoreType.DMA((7 * ns + n,)), pltpu.SemaphoreType.DMA((7 * ns + n,))],
    )(*fulls, *([small] if ns else []))


def _add_sibling(own, recv, cq, name):
    nb, R, Cc = own.shape
    Rh = R // 2

    def body(cq_ref, a_ref, b_ref, o32_ref, o16_ref):
        s = a_ref[...] + b_ref[...]
        o32_ref[...] = s
        o16_ref[...] = s.astype(o16_ref.dtype)

    sp = pl.BlockSpec((1, Rh, Cc), lambda b, s: (b, 0, 0))
    gs = pltpu.PrefetchScalarGridSpec(
        num_scalar_prefetch=1, grid=(nb,),
        in_specs=[pl.BlockSpec((1, Rh, Cc), lambda b, s: (b, s[0], 0)), sp], out_specs=[sp, sp])
    return pl.pallas_call(
        body, name=name, grid_spec=gs,
        out_shape=[jax.ShapeDtypeStruct((nb, Rh, Cc), F32), jax.ShapeDtypeStruct((nb, Rh, Cc), _MXU)],
        compiler_params=_cparams(("parallel",)),
    )(cq, own, recv)


def _add_chips(part32, recv3, cq, name):
    nb, Rh, Cc = part32.shape

    def body(cq_ref, a_ref, b_ref, o_ref):
        acc = a_ref[0]
        for j in range(3):
            acc = acc + b_ref[j].astype(F32)
        o_ref[...] = acc

    gs = pltpu.PrefetchScalarGridSpec(
        num_scalar_prefetch=1, grid=(1,),
        in_specs=[pl.BlockSpec((1, Rh, Cc), lambda i, s: (s[1], 0, 0)), pl.BlockSpec((3, Rh, Cc), lambda i, s: (0, 0, 0))],
        out_specs=pl.BlockSpec((Rh, Cc), lambda i, s: (s[0], 0)))
    return pl.pallas_call(
        body, name=name, grid_spec=gs, out_shape=jax.ShapeDtypeStruct((2 * Rh, Cc), F32),
        compiler_params=_cparams(("arbitrary",)),
    )(cq, part32, recv3)


def _adamw(w, g, m, v, name):
    R, Cc = w.shape
    T = max([t for t in range(8, 257, 8) if R % t == 0], default=R)

    def body(w_ref, g_ref, m_ref, v_ref, d_ref, mo_ref, vo_ref):
        d_ref[...], mo_ref[...], vo_ref[...] = _adamw_math(w_ref[...], g_ref[...], m_ref[...], v_ref[...])

    sp = pl.BlockSpec((T, Cc), lambda i: (i, 0))
    sh = jax.ShapeDtypeStruct((R, Cc), F32)
    return pl.pallas_call(
        body, name=name, grid=(R // T,), in_specs=[sp] * 4, out_specs=(sp, sp, sp), out_shape=(sh, sh, sh),
        compiler_params=_cparams(("parallel",)),
    )(w, g, m, v)


SMALL_ROWS = 32
ROW_CONV, ROW_FCG, ROW_FCU = 5, 13, 22


def _adamw_math(w, g, m, v):
    mn = ADAM_B1 * m + (1.0 - ADAM_B1) * g
    vn = ADAM_B2 * v + (1.0 - ADAM_B2) * (g * g)
    c1 = 1.0 / (1.0 - ADAM_B1 ** ADAM_STEP)
    c2 = 1.0 / (1.0 - ADAM_B2 ** ADAM_STEP)
    return -ADAM_LR * ((mn * c1) / (jnp.sqrt(vn * c2) + ADAM_EPS) + ADAM_WD * w), mn, vn


def _pack_small(n1, n2, fn, gp, gn, conv, fcg, fcu, loss):
    W = D_MODEL

    def body(n1_ref, n2_ref, fn_ref, gp_ref, gn_ref, conv_ref, fcg_ref, fcu_ref, loss_ref, o_ref):
        o_ref[...] = jnp.zeros_like(o_ref)
        o_ref[0:1, :] = n1_ref[...]
        o_ref[1:2, :] = n2_ref[...]
        o_ref[2:3, :] = fn_ref[...]
        o_ref[3:4, 0:8] = gp_ref[0:1, 0:8]
        o_ref[3:4, 8:9] = loss_ref[0:1, 0:1]
        o_ref[4:5, 0:128] = gn_ref[...]
        for i in range(GDN_CONV):
            o_ref[ROW_CONV + 2 * i:ROW_CONV + 2 * i + 1, :] = conv_ref[i:i + 1, 0:W]
            o_ref[ROW_CONV + 2 * i + 1:ROW_CONV + 2 * i + 2, 0:3 * GDN_WIDTH - W] = conv_ref[i:i + 1, W:3 * GDN_WIDTH]
        for r0, ref in ((ROW_FCG, fcg_ref), (ROW_FCU, fcu_ref)):
            for i in range(FFN_CONV):
                for k in range(3):
                    n = min(W, D_FF - k * W)
                    o_ref[r0 + 3 * i + k:r0 + 3 * i + k + 1, 0:n] = ref[i:i + 1, k * W:k * W + n]

    return pl.pallas_call(body, name="pack_small", out_shape=jax.ShapeDtypeStruct((SMALL_ROWS, W), F32))(
        n1, n2, fn, gp, gn, conv, fcg, fcu, loss)


def _small_step(meq, small_all, small, ws, ms, vs):
    W = D_MODEL
    n = len(ws)
    cw, fw = ws[6].shape[1], ws[7].shape[1]

    def body(meq_ref, all_ref, own_ref, *refs):
        w_refs, m_refs, v_refs = refs[:n], refs[n:2 * n], refs[2 * n:3 * n]
        loss_ref = refs[3 * n]
        outs = refs[3 * n + 1:]
        me, q = meq_ref[0], meq_ref[1]
        red = None
        for d in range(8):
            term = jnp.where(me == d, own_ref[...], all_ref[d])
            red = term if red is None else red + term
        loss_ref[...] = jnp.broadcast_to(red[3:4, 8:9], loss_ref.shape)
        conv = [jnp.concatenate([red[ROW_CONV + 2 * i:ROW_CONV + 2 * i + 1, :],
                                 red[ROW_CONV + 2 * i + 1:ROW_CONV + 2 * i + 2, 0:3 * GDN_WIDTH - W]], axis=1)
                for i in range(GDN_CONV)]
        conv = jnp.concatenate(conv, axis=0)

        def fc_rows(r0):
            rows = [jnp.concatenate([red[r0 + 3 * i + k:r0 + 3 * i + k + 1, 0:min(W, D_FF - k * W)] for k in range(3)], axis=1)
                    for i in range(FFN_CONV)]
            return jnp.concatenate(rows, axis=0)

        fc = jnp.concatenate([fc_rows(ROW_FCG), fc_rows(ROW_FCU)], axis=1)

        def chip_block(full, width):
            out = None
            for j in range(N_CHIPS):
                term = jnp.where(q == j, full[:, width * j:width * (j + 1)], 0.0)
                out = term if out is None else out + term
            return out

        grads = [red[0:1, :], red[1:2, :], red[2:3, :], red[3:4, 0:4], red[3:4, 4:8], red[4:5, 0:128],
                 chip_block(conv, cw), chip_block(fc, fw)]
        for k in range(n):
            d_, m_, v_ = _adamw_math(w_refs[k][...], grads[k], m_refs[k][...], v_refs[k][...])
            outs[4 * k][...] = grads[k]
            outs[4 * k + 1][...] = d_
            outs[4 * k + 2][...] = m_
            outs[4 * k + 3][...] = v_

    full = lambda a: pl.BlockSpec(a.shape, lambda i, s_, nd=len(a.shape): (0,) * nd)
    arrays = [small_all, small, *ws, *ms, *vs]
    out_shapes = [jax.ShapeDtypeStruct((8, 128), F32)] + [jax.ShapeDtypeStruct(w.shape, F32) for w in ws for _ in range(4)]
    gs = pltpu.PrefetchScalarGridSpec(
        num_scalar_prefetch=1, grid=(1,), in_specs=[full(a) for a in arrays],
        out_specs=[pl.BlockSpec(o.shape, lambda i, s_, nd=len(o.shape): (0,) * nd) for o in out_shapes])
    return pl.pallas_call(body, name="small_step", grid_spec=gs, out_shape=out_shapes)(meq, *arrays)


def _pad_lanes(v, n=D_MODEL):
    return jnp.pad(v, ((0, 0), (0, n - v.shape[1])))


def kernel(x, norm1_w, w_in, conv_qkv_w, a_log, dt_bias, gdn_norm_w, w_out, norm2_w, w_up, ffn_conv_w, w_down, final_norm_w, loss_target, m_norm1_w, m_w_in, m_conv_qkv_w, m_a_log, m_dt_bias, m_gdn_norm_w, m_w_out, m_norm2_w, m_w_up, m_ffn_conv_w, m_w_down, m_final_norm_w, v_norm1_w, v_w_in, v_conv_qkv_w, v_a_log, v_dt_bias, v_gdn_norm_w, v_w_out, v_norm2_w, v_w_up, v_ffn_conv_w, v_w_down, v_final_norm_w):
    c = lax.axis_index("c")
    q = 2 * lax.axis_index("x") + lax.axis_index("y")
    S = x.shape[1]
    cq = jnp.stack([c, q]).astype(jnp.int32)

    *in_started, in_token = _gather_halves_start([w_in[0].astype(_MXU), conv_qkv_w[0], ffn_conv_w[0]], x, "gather_in_start")
    w_in_l, m_w_in_l, v_w_in_l = (a + in_token[0:1, 0:1] for a in (w_in, m_w_in, v_w_in))
    h1 = _rmsnorm_fwd(x[0], norm1_w, "norm1", after=[in_token])
    rest = [(a[0] + in_token[0:1, 0:1]).astype(_MXU) for a in (w_out, w_up, w_down)]
    in_shards, got_in = _gather_halves_wait(in_started, [w_in_l, m_w_in_l, v_w_in_l, h1, *rest], "gather_in_wait")
    g_in, g_conv, g_fconv = _place_own(in_shards, _sibling_fill(got_in, "fill_in"), cq, "place_in")
    *rest_started, token = _gather_halves_start(rest, g_conv, "gather_rest_start")

    def rest_weights(after):
        shards, got = _gather_halves_wait(rest_started, after, "gather_rest_wait")
        got = _sibling_fill(got, "fill_rest")
        g_out, g_up, g_down = _place_own(shards, got, cq, "place_rest")
        return g_out.reshape(D_MODEL, D_MODEL), g_up, g_down.reshape(D_FF, D_MODEL)
    wp = _wp_assemble(g_in, [token])
    conv_f = jnp.concatenate([g_conv[i] for i in range(N_CHIPS)], axis=1)
    fcw = jnp.concatenate([g_fconv[i] for i in range(N_CHIPS)], axis=1)
    gp = _pad_lanes(jnp.concatenate([a_log, dt_bias], axis=1), 128)
    fnw = final_norm_w[None, :]
    early = {}

    def early_sibling(d_wup, d_wdown):
        *early["sibling"], tok = _grad_sibling_start([d_wup, d_wdown.reshape(N_CHIPS, D_FF // N_CHIPS, D_MODEL)],
                                                     "grad_sibling_early_start")
        return tok

    def early_chips(dx2):
        fams_e, got_e = _grad_sibling_wait(early["sibling"], [dx2], "grad_sibling_early_wait")
        early["parts"] = [_add_sibling(f, r, cq, "add_sibling_" + nm) for f, r, nm in zip(fams_e, got_e, ("w_up", "w_down"))]
        *early["started"], tok = _grad_chips_start([p[1] for p in early["parts"]], "grad_chips_start")
        return tok

    def late_sibling(d_wp, d_wout):
        *early["late_sibling"], tok = _grad_sibling_start(
            [_win_split(d_wp), d_wout.reshape(N_CHIPS, D_MODEL // N_CHIPS, D_MODEL)], "grad_sibling_late_start")
        return tok

    early_grads = (early_sibling, early_chips, late_sibling)

    loss_l, dx, g = _local_step(x[0], loss_target[0], h1, norm1_w, norm2_w, fnw, gp, gdn_norm_w, wp,
                                conv_f, fcw, rest_weights, early_grads)
    small = _pack_small(g["n1w"], g["n2w"], g["fnw"], g["gp"], g["gnw"], g["conv_w"], g["fcw_g"], g["fcw_u"], loss_l)
    fams, got = _grad_sibling_wait(early["late_sibling"], [dx], "grad_sibling_late_wait")
    parts = [_add_sibling(f, r, cq, "add_sibling_" + nm) for f, r, nm in zip(fams, got, ("w_in", "w_out"))]
    *late_started, late_token = _grad_chips_start([p[1] for p in parts], "grad_chips_late_start")
    got3_e = _grad_chips_wait(early["started"], [dx, g["wp"], late_token], "grad_chips_wait")
    g_w_up, g_w_down = _grad_share(
        [_add_chips(p[0], r3, cq, "add_chips_" + nm) for p, r3, nm in zip(early["parts"], got3_e, ("w_up", "w_down"))],
        "grad_share_early")
    big = {}

    def adamw_big(nm, w, gg, m, v):
        d_, m_, v_ = _adamw(w[0], gg, m[0], v[0], "adamw_" + nm)
        big[nm] = (gg[None], d_[None], m_[None], v_[None])

    adamw_big("w_up", w_up, g_w_up, m_w_up, v_w_up)
    adamw_big("w_down", w_down, g_w_down, m_w_down, v_w_down)
    got3 = _grad_chips_wait(late_started, [big["w_up"][1], big["w_down"][1]], "grad_chips_late_wait")
    g_w_in, g_w_out, small_all = _grad_share(
        [_add_chips(p[0], r3, cq, "add_chips_" + nm) for p, r3, nm in zip(parts, got3, ("w_in", "w_out"))],
        "grad_share_late", small)
    adamw_big("w_in", w_in_l, g_w_in, m_w_in_l, v_w_in_l)
    adamw_big("w_out", w_out, g_w_out, m_w_out, v_w_out)
    small_names = ["norm1_w", "norm2_w", "final_norm_w", "a_log", "dt_bias", "gdn_norm_w", "conv_qkv_w", "ffn_conv_w"]
    loss_b, *small_out = _small_step(
        jnp.stack([2 * q + c, q]).astype(jnp.int32), small_all, small,
        [norm1_w, norm2_w, final_norm_w[None], a_log, dt_bias, gdn_norm_w, conv_qkv_w[0], ffn_conv_w[0]],
        [m_norm1_w, m_norm2_w, m_final_norm_w[None], m_a_log, m_dt_bias, m_gdn_norm_w, m_conv_qkv_w[0], m_ffn_conv_w[0]],
        [v_norm1_w, v_norm2_w, v_final_norm_w[None], v_a_log, v_dt_bias, v_gdn_norm_w, v_conv_qkv_w[0], v_ffn_conv_w[0]])
    like = dict(final_norm_w=lambda t: t[0], conv_qkv_w=lambda t: t[None], ffn_conv_w=lambda t: t[None])
    for k, nm in enumerate(small_names):
        big[nm] = tuple(like.get(nm, lambda t: t)(t) for t in small_out[4 * k:4 * k + 4])
    names = ["norm1_w", "w_in", "conv_qkv_w", "a_log", "dt_bias", "gdn_norm_w", "w_out", "norm2_w", "w_up",
             "ffn_conv_w", "w_down", "final_norm_w"]
    return (loss_b[0, 0], dx[None], *[big[n][0] for n in names], *[big[n][1] for n in names],
            *[big[n][2] for n in names], *[big[n][3] for n in names])
```

```python
import functools
import math

import numpy as np
import jax
import jax.numpy as jnp
from jax import lax
from jax.experimental import pallas as pl
from jax.experimental.pallas import tpu as pltpu

F32 = jnp.float32
BF16 = jnp.bfloat16
_MXU = jnp.bfloat16
_HI = lax.Precision.HIGHEST
EPS = 1e-6
V7X_VMEM_LIMIT = 56 * 1024 * 1024
MESH = pl.DeviceIdType.MESH

D_MODEL = 1024
GDN_HEADS, GDN_DIM, GDN_CHUNK, GDN_CONV = 4, 128, 64, 4
GDN_WIDTH = GDN_HEADS * GDN_DIM
DIL_HEADS, DIL_DIM = 8, 64
DIL_WIDTH = DIL_HEADS * DIL_DIM
D_FF, FFN_CONV = 2816, 3
IN_COLS = 3592
P_COLS = 3840
P_Z, P_QKVB, P_BA = 1536, 2048, 3584
ATT_T = 1024
ADAM_LR, ADAM_B1, ADAM_B2, ADAM_EPS, ADAM_WD, ADAM_STEP = 0.001, 0.9, 0.999, 1e-08, 0.01, 10
N_CHIPS = 4


def _cparams(sem=None, vmem=None):
    kw = {}
    if sem is not None:
        kw["dimension_semantics"] = sem
    if vmem is not None:
        kw["vmem_limit_bytes"] = vmem
    return pltpu.CompilerParams(**kw)


def _silu(x):
    return x * jax.nn.sigmoid(x)


def _pick_tile(n, cap):
    best = None
    for t in range(128, min(n, cap) + 1, 128):
        if n % t == 0:
            best = t
    return best or n


def _mm(a, b, mode, *, out_dtype=F32, residual=None, name, b_blocks=False, place=None, into=None, tn=None, after=()):
    if mode == "nn":
        M, K = a.shape
        N = b.shape[0] * b.shape[2] if b_blocks else b.shape[1]
    elif mode == "nt":
        (M, K), (N, _) = a.shape, b.shape
    else:
        (K, M), (_, N) = a.shape, b.shape
    tm = _pick_tile(M, 1024)
    tn = b.shape[2] if b_blocks else (tn or _pick_tile(N, 1536))

    def vmem(tm, tn):
        return 2 * (tm * K * a.dtype.itemsize + tn * K * b.dtype.itemsize
                    + tm * tn * (jnp.dtype(out_dtype).itemsize + (4 if residual is not None else 0))) + 3 * tm * tn * 4

    fixed_tn = b_blocks or (place is not None and place[0] == "blocks")
    while vmem(tm, tn) > 40 * 1024 * 1024:
        if (tm >= tn or fixed_tn) and tm % 256 == 0:
            tm //= 2
        elif tn % 256 == 0 and not fixed_tn:
            tn //= 2
        else:
            tm //= 2
    a_spec = pl.BlockSpec((K, tm), lambda j, i: (0, i)) if mode == "tn" else pl.BlockSpec((tm, K), lambda j, i: (i, 0))
    if b_blocks:
        b_spec = pl.BlockSpec((None, K, tn), lambda j, i: (j, 0, 0))
    else:
        b_spec = pl.BlockSpec((tn, K), lambda j, i: (j, 0)) if mode == "nt" else pl.BlockSpec((K, tn), lambda j, i: (0, j))
    r_spec = pl.BlockSpec((tm, tn), lambda j, i: (i, j))
    if place is None:
        o_spec, o_shape = r_spec, (M, N)
    elif place[0] == "rows":
        off = place[2] // tm
        o_spec, o_shape = pl.BlockSpec((tm, tn), lambda j, i: (i + off, j)), (place[1], N)
    else:
        off = place[2]
        o_spec, o_shape = pl.BlockSpec((None, tm, tn), lambda j, i: (j + off, i, 0)), (place[1], M, tn)
    dims = {"nn": (((1,), (0,)), ((), ())), "nt": (((1,), (1,)), ((), ())), "tn": (((0,), (0,)), ((), ()))}[mode]

    def body(*refs):
        a_ref, b_ref = refs[0], refs[1]
        o_ref = refs[-1]
        acc = lax.dot_general(a_ref[...].astype(_MXU), b_ref[...].astype(_MXU), dims, preferred_element_type=F32)
        if residual is not None:
            acc = acc + refs[2][...]
        o_ref[...] = acc.astype(out_dtype)

    ins, specs, alias = [a, b], [a_spec, b_spec], {}
    if residual is not None:
        ins.append(residual)
        specs.append(r_spec)
    if into is not None:
        alias = {len(ins): 0}
        ins.append(into)
        specs.append(pl.BlockSpec(memory_space=pl.ANY))
    ins += list(after)
    specs += [pl.BlockSpec(memory_space=pl.ANY)] * len(after)
    return pl.pallas_call(
        body, name=name, grid=(N // tn, M // tm), in_specs=specs, out_specs=o_spec,
        out_shape=jax.ShapeDtypeStruct(o_shape, out_dtype), input_output_aliases=alias,
        compiler_params=_cparams(("parallel", "parallel"), V7X_VMEM_LIMIT),
    )(*ins)


def _mm_nt_blocks(a_list, b4, name, after=()):
    M = a_list[0].shape[0]
    nb, N, Kb = b4.shape
    tm, tn = _pick_tile(M, 1024), _pick_tile(N, 512)

    def body(a0_ref, a1_ref, b_ref, *rest):
        o_ref = rest[-1]
        acc = None
        for blk in range(nb):
            a_ref = (a0_ref, a1_ref)[blk // 2]
            lo = (blk % 2) * Kb
            t = lax.dot_general(a_ref[:, lo:lo + Kb].astype(_MXU), b_ref[blk].astype(_MXU), (((1,), (1,)), ((), ())),
                                preferred_element_type=F32)
            acc = t if acc is None else acc + t
        o_ref[...] = acc

    a_spec = pl.BlockSpec((tm, 2 * Kb), lambda j, i: (i, 0))
    return pl.pallas_call(
        body, name=name, grid=(N // tn, M // tm),
        in_specs=[a_spec, a_spec, pl.BlockSpec((nb, tn, Kb), lambda j, i: (0, j, 0))]
        + [pl.BlockSpec(memory_space=pl.ANY)] * len(after),
        out_specs=pl.BlockSpec((tm, tn), lambda j, i: (i, j)), out_shape=jax.ShapeDtypeStruct((M, N), F32),
        compiler_params=_cparams(("parallel", "parallel"), V7X_VMEM_LIMIT),
    )(a_list[0], a_list[1], b4, *after)


def _wp_assemble(g_in, after=()):
    nb, Dm, Wb = g_in.shape
    T = 256
    n_lo = P_QKVB - 2 * Wb

    def body(g_ref, *rest):
        g2 = g_ref[2]
        rest[-1][...] = jnp.concatenate(
            [g_ref[0], g_ref[1], g2[:, :n_lo], g2[:, n_lo + 8:], g_ref[3], g2[:, n_lo:n_lo + 8],
             jnp.zeros((T, P_COLS - P_BA - 8), g_in.dtype)], axis=1)

    return pl.pallas_call(
        body, name="wp_assemble", grid=(Dm // T,),
        in_specs=[pl.BlockSpec((nb, T, Wb), lambda i: (0, i, 0))] + [pl.BlockSpec(memory_space=pl.ANY)] * len(after),
        out_specs=pl.BlockSpec((T, P_COLS), lambda i: (i, 0)), out_shape=jax.ShapeDtypeStruct((Dm, P_COLS), g_in.dtype),
        compiler_params=_cparams(("parallel",)),
    )(g_in, *after)


def _win_split(d_wp):
    Dm = d_wp.shape[0]
    Wb = IN_COLS // N_CHIPS
    T = 256

    def body(x_ref, o_ref):
        xv = x_ref[...]
        o_ref[0] = xv[:, 0:Wb]
        o_ref[1] = xv[:, Wb:2 * Wb]
        o_ref[2] = jnp.concatenate([xv[:, 2 * Wb:P_QKVB], xv[:, P_BA:P_BA + 8], xv[:, P_QKVB:3 * Wb - 8]], axis=1)
        o_ref[3] = xv[:, 3 * Wb - 8:P_BA]

    return pl.pallas_call(
        body, name="win_split", grid=(Dm // T,), in_specs=[pl.BlockSpec((T, P_COLS), lambda i: (i, 0))],
        out_specs=pl.BlockSpec((N_CHIPS, T, Wb), lambda i: (0, i, 0)),
        out_shape=jax.ShapeDtypeStruct((N_CHIPS, Dm, Wb), F32), compiler_params=_cparams(("parallel",)),
    )(d_wp)


def _rmsnorm_fwd(x, w, name, after=()):
    S, D = x.shape
    T = _pick_tile(S, 512)

    def body(x_ref, w_ref, *rest):
        xv = x_ref[...]
        rs = lax.rsqrt(jnp.mean(xv * xv, axis=-1, keepdims=True) + EPS)
        rest[-1][...] = (xv * rs * w_ref[...]).astype(rest[-1].dtype)

    return pl.pallas_call(
        body, name=name, grid=(S // T,),
        in_specs=[pl.BlockSpec((T, D), lambda i: (i, 0)), pl.BlockSpec((1, D), lambda i: (0, 0))] + [_ANY] * len(after),
        out_specs=pl.BlockSpec((T, D), lambda i: (i, 0)),
        out_shape=jax.ShapeDtypeStruct((S, D), _MXU),
        compiler_params=_cparams(("parallel",)),
    )(x, w, *after)


def _rmsnorm_bwd(dh, x, w, dres, name, after=()):
    S, D = x.shape
    T = _pick_tile(S, 512)

    def body(dh_ref, x_ref, w_ref, dres_ref, *rest):
        dx_ref, dw_ref = rest[-2:]
        xv = x_ref[...]
        rs = lax.rsqrt(jnp.mean(xv * xv, axis=-1, keepdims=True) + EPS)
        xn = xv * rs
        dhv = dh_ref[...]
        dxn = dhv * w_ref[...]
        dx_ref[...] = dres_ref[...] + rs * (dxn - xn * jnp.mean(dxn * xn, axis=-1, keepdims=True))

        @pl.when(pl.program_id(0) == 0)
        def _():
            dw_ref[...] = jnp.zeros_like(dw_ref)

        dw_ref[...] += jnp.sum(dhv * xn, axis=0, keepdims=True)

    row = pl.BlockSpec((T, D), lambda i: (i, 0))
    vec = pl.BlockSpec((1, D), lambda i: (0, 0))
    return pl.pallas_call(
        body, name=name, grid=(S // T,), in_specs=[row, row, vec, row] + [_ANY] * len(after), out_specs=(row, vec),
        out_shape=(jax.ShapeDtypeStruct((S, D), F32), jax.ShapeDtypeStruct((1, D), F32)),
        compiler_params=_cparams(("arbitrary",)),
    )(dh, x, w, dres, *after)


def _loss_head(x3, w, tgt, name):
    S, D = x3.shape
    T = _pick_tile(S, 512)

    def body(x_ref, w_ref, t_ref, loss_ref, dx_ref, dxn_ref, dw_ref):
        xv = x_ref[...]
        rs = lax.rsqrt(jnp.mean(xv * xv, axis=-1, keepdims=True) + EPS)
        xn = xv * rs
        err = xn * w_ref[...] - t_ref[...]
        dy = err * (1.0 / D)
        dxn = dy * w_ref[...]
        dxv = rs * (dxn - xn * jnp.mean(dxn * xn, axis=-1, keepdims=True))
        dx_ref[...] = dxv
        dxn_ref[...] = dxv.astype(dxn_ref.dtype)

        @pl.when(pl.program_id(0) == 0)
        def _():
            dw_ref[...] = jnp.zeros_like(dw_ref)
            loss_ref[...] = jnp.zeros_like(loss_ref)

        dw_ref[...] += jnp.sum(dy * xn, axis=0, keepdims=True)
        part = jnp.sum(jnp.sum(err * err, axis=-1, keepdims=True), axis=0, keepdims=True) * (0.5 / D)
        loss_ref[...] += jnp.broadcast_to(part, loss_ref.shape)

    row = pl.BlockSpec((T, D), lambda i: (i, 0))
    vec = pl.BlockSpec((1, D), lambda i: (0, 0))
    return pl.pallas_call(
        body, name=name, grid=(S // T,), in_specs=[row, vec, row],
        out_specs=(pl.BlockSpec((8, 128), lambda i: (0, 0)), row, row, vec),
        out_shape=(jax.ShapeDtypeStruct((8, 128), F32), jax.ShapeDtypeStruct((S, D), F32), jax.ShapeDtypeStruct((S, D), _MXU),
                   jax.ShapeDtypeStruct((1, D), F32)),
        compiler_params=_cparams(("arbitrary",)),
    )(x3, w, tgt)


def _shifted(ext, back, lo, n):
    if back == 0:
        return ext[lo:lo + n, :]
    return pltpu.roll(ext, back % ext.shape[0], 0)[lo:lo + n, :]


def _conv_windows(ext, K, T):
    return [_shifted(ext, (K - 1) - i, 8, T) for i in range(K)]


def _conv_taps(ext, w, K, T):
    out = None
    for i, win in enumerate(_conv_windows(ext, K, T)):
        term = win * w[i:i + 1, :]
        out = term if out is None else out + term
    return out


def _conv_taps_t(ext, w, K, T):
    out = None
    for i in range(K):
        term = _shifted(ext, i - (K - 1), 0, T) * w[i:i + 1, :]
        out = term if out is None else out + term
    return out


def _tri_masks(C):
    r = lax.broadcasted_iota(jnp.int32, (C, C), 0)
    c = lax.broadcasted_iota(jnp.int32, (C, C), 1)
    return r == c, r >= c, r > c, r <= c


_NN, _NT, _TN = ((1,), (0,)), ((1,), (1,)), ((0,), (0,))
_GDN_PASSES = dict(qk=1, inv=1, sol=1, scan=1, bwd=1)


def _bdot_raw(a, b, kind, passes):
    dims = ({"NN": ((2,), (1,)), "NT": ((2,), (2,)), "TN": ((1,), (1,))}[kind], ((0,), (0,)))
    if passes == 0:
        return lax.dot_general(a, b, dims, precision=_HI, preferred_element_type=F32)
    ah, bh = a.astype(BF16), b.astype(BF16)
    out = lax.dot_general(ah, bh, dims, preferred_element_type=F32)
    if passes == 3:
        al, bl = (a - ah.astype(F32)).astype(BF16), (b - bh.astype(F32)).astype(BF16)
        out = out + lax.dot_general(ah, bl, dims, preferred_element_type=F32) + lax.dot_general(al, bh, dims, preferred_element_type=F32)
    return out


@functools.partial(jax.custom_vjp, nondiff_argnums=(2, 3))
def _bdot(a, b, kind, passes):
    return _bdot_raw(a, b, kind, passes)


def _bdot_fwd(a, b, kind, passes):
    return _bdot_raw(a, b, kind, passes), (a, b)


def _bdot_bwd(kind, passes, res, ct):
    a, b = res
    if kind == "NN":
        return _bdot_raw(ct, b, "NT", passes), _bdot_raw(a, ct, "TN", passes)
    if kind == "NT":
        return _bdot_raw(ct, b, "NN", passes), _bdot_raw(ct, a, "TN", passes)
    return _bdot_raw(b, ct, "NT", passes), _bdot_raw(a, ct, "NN", passes)


_bdot.defvjp(_bdot_fwd, _bdot_bwd)


def _softplus(x):
    return jnp.maximum(x, 0.0) + jnp.log(1.0 + jnp.exp(-jnp.abs(x)))


def _gdn_stage1(cq, ck, cv, b_col, a_col, alog, dtb, dot=_bdot_raw):
    C = cq.shape[1]
    eye, incl, strict, incl_t = _tri_masks(C)
    qn = cq * lax.rsqrt(jnp.sum(cq * cq, axis=-1, keepdims=True) + EPS) * (GDN_DIM ** -0.5)
    kn = ck * lax.rsqrt(jnp.sum(ck * ck, axis=-1, keepdims=True) + EPS)
    beta = jax.nn.sigmoid(b_col)
    g = -jnp.exp(alog) * _softplus(a_col + dtb)
    g_row = jnp.sum(jnp.where(eye, g, 0.0), axis=1, keepdims=True)
    beta_row = jnp.sum(jnp.where(eye, beta, 0.0), axis=1, keepdims=True)
    gc_col = jnp.sum(jnp.where(incl, g_row, 0.0), axis=2, keepdims=True)
    gc_row = jnp.sum(jnp.where(incl_t, g, 0.0), axis=1, keepdims=True)
    dec = jnp.where(incl, jnp.exp(jnp.where(incl, gc_col - gc_row, 0.0)), 0.0)
    kk = dot(kn, kn, "NT", _GDN_PASSES["qk"])
    qk = dot(qn, kn, "NT", _GDN_PASSES["qk"])
    lmat = jnp.where(strict, dec * kk * beta_row, 0.0)
    attn = dec * qk * beta_row
    gam = jnp.exp(gc_col)
    gc_last = gc_col[:, C - 1:C, :]
    k_end = kn * (jnp.exp(gc_last - gc_col) * beta)
    return lmat, cv, gam * kn, gam * qn, attn, k_end, jnp.exp(gc_last)


def _tri_inv(lmat):
    C = lmat.shape[1]
    eye = _tri_masks(C)[0]
    ps = _GDN_PASSES["inv"]
    p = jnp.where(eye, 1.0, 0.0) - lmat
    lp = _bdot_raw(lmat, lmat, "NN", ps)
    n = int(math.log2(C))
    for s in range(1, n):
        p = p + _bdot_raw(p, lp, "NN", ps)
        if s < n - 1:
            lp = _bdot_raw(lp, lp, "NN", ps)
    return p


def _gated_norm(o, z, gnw):
    on = o * lax.rsqrt(jnp.mean(o * o, axis=-1, keepdims=True) + EPS) * gnw
    return on * _silu(z)


GDN_PG = 4
GDN_SG = 4


def _gdn_pairs(c, ba, gp, G):
    C, W, H = GDN_CHUNK, GDN_WIDTH, GDN_HEADS
    pairs = [(j, h) for j in range(G) for h in range(H)]
    cq, ck, cv = (jnp.stack([c[C * j:C * (j + 1), o + GDN_DIM * h:o + GDN_DIM * (h + 1)] for j, h in pairs]) for o in (0, W, 2 * W))
    b_col = jnp.stack([ba[C * j:C * (j + 1), h:h + 1] for j, h in pairs])
    a_col = jnp.stack([ba[C * j:C * (j + 1), H + h:H + h + 1] for j, h in pairs])
    alog = jnp.stack([gp[0:1, h:h + 1] for j, h in pairs])
    dtb = jnp.stack([gp[0:1, H + h:H + h + 1] for j, h in pairs])
    return pairs, (cq, ck, cv, b_col, a_col, alog, dtb)


def _gdn_pre_specs(S, G):
    C = GDN_CHUNK
    T = C * G
    return dict(
        cur=pl.BlockSpec((T, 3 * GDN_WIDTH), lambda i: (i, 0)),
        prev=pl.BlockSpec((8, 3 * GDN_WIDTH), lambda i: (jnp.maximum(i * (T // 8) - 1, 0), 0)),
        ba=pl.BlockSpec((T, 128), lambda i: (i, P_BA // 128)),
        cw=pl.BlockSpec((GDN_CONV, 3 * GDN_WIDTH), lambda i: (0, 0)),
        vec=pl.BlockSpec((1, 128), lambda i: (0, 0)),
        hd=pl.BlockSpec((GDN_HEADS, T, GDN_DIM), lambda i: (0, i, 0)),
        hc=pl.BlockSpec((GDN_HEADS, T, C), lambda i: (0, i, 0)),
        ge=pl.BlockSpec((G, GDN_HEADS, 8, 128), lambda i: (i, 0, 0, 0)),
    )


def _hd_shape(S, last=GDN_DIM):
    return jax.ShapeDtypeStruct((GDN_HEADS, S, last), F32)


def _gdn_pre(proj, conv_w, gp):
    S = proj.shape[0]
    C, G = GDN_CHUNK, GDN_PG
    nc = S // C
    sp = _gdn_pre_specs(S, G)

    def body(cur_ref, prev_ref, ba_ref, cw_ref, gp_ref, uv_ref, wk_ref, qd_ref, ke_ref, at_ref, ti_ref, ge_ref):
        prev = prev_ref[...] * jnp.where(pl.program_id(0) == 0, 0.0, 1.0)
        c = _silu(_conv_taps(jnp.concatenate([prev, cur_ref[...]], axis=0), cw_ref[...], GDN_CONV, C * G))
        pairs, args = _gdn_pairs(c, ba_ref[...], gp_ref[...], G)
        lmat, v, rk, q_dec, attn, k_end, g_end = _gdn_stage1(*args)
        t = _tri_inv(lmat)
        u_v = _bdot_raw(t, v, "NN", _GDN_PASSES["sol"])
        w_k = _bdot_raw(t, rk, "NN", _GDN_PASSES["sol"])
        for b, (j, h) in enumerate(pairs):
            rows = slice(C * j, C * (j + 1))
            uv_ref[h, rows, :] = u_v[b]
            wk_ref[h, rows, :] = w_k[b]
            qd_ref[h, rows, :] = q_dec[b]
            ke_ref[h, rows, :] = k_end[b]
            at_ref[h, rows, :] = attn[b]
            ti_ref[h, rows, :] = t[b]
            ge_ref[j, h] = jnp.broadcast_to(g_end[b], (8, 128))

    return pl.pallas_call(
        body, name="gdn_pre", grid=(nc // G,),
        in_specs=[sp["cur"], sp["prev"], sp["ba"], sp["cw"], sp["vec"]],
        out_specs=(sp["hd"], sp["hd"], sp["hd"], sp["hd"], sp["hc"], sp["hc"], sp["ge"]),
        out_shape=(_hd_shape(S), _hd_shape(S), _hd_shape(S), _hd_shape(S), _hd_shape(S, C), _hd_shape(S, C),
                   jax.ShapeDtypeStruct((nc, GDN_HEADS, 8, 128), F32)),
        compiler_params=_cparams(("parallel",)),
    )(proj, proj, proj, conv_w, gp)


def _gdn_scan_specs(S, G, rev):
    C = GDN_CHUNK
    T = C * G
    n = S // T
    ci = (lambda i: n - 1 - i) if rev else (lambda i: i)
    return dict(
        hd=pl.BlockSpec((GDN_HEADS, T, GDN_DIM), lambda i: (0, ci(i), 0)),
        hc=pl.BlockSpec((GDN_HEADS, T, C), lambda i: (0, ci(i), 0)),
        ge=pl.BlockSpec((G, GDN_HEADS, 8, 128), lambda i: (ci(i), 0, 0, 0)),
        z=pl.BlockSpec((T, GDN_WIDTH), lambda i: (ci(i), P_Z // GDN_WIDTH)),
        oa=pl.BlockSpec((T, GDN_WIDTH), lambda i: (ci(i), 0)),
        vec=pl.BlockSpec((1, 128), lambda i: (0, 0)),
        st=pl.BlockSpec((G, GDN_HEADS, GDN_DIM, GDN_DIM), lambda i: (ci(i), 0, 0, 0)),
    )


def _gdn_scan(u_v, w_k, q_dec, k_end, attn, g_end, proj, gnw):
    S = proj.shape[0]
    C, G = GDN_CHUNK, GDN_SG
    nc = S // C
    sp = _gdn_scan_specs(S, G, False)
    ps = _GDN_PASSES["scan"]

    def body(uv_ref, wk_ref, qd_ref, ke_ref, at_ref, ge_ref, z_ref, gnw_ref, oa_ref, st_ref, s_scr):
        @pl.when(pl.program_id(0) == 0)
        def _():
            s_scr[...] = jnp.zeros_like(s_scr)

        for j in range(G):
            rows = slice(C * j, C * (j + 1))
            st = s_scr[...]
            st_ref[j] = st
            u = uv_ref[:, rows, :] - _bdot_raw(wk_ref[:, rows, :], st, "NN", ps)
            o = _bdot_raw(qd_ref[:, rows, :], st, "NN", ps) + _bdot_raw(at_ref[:, rows, :], u, "NN", ps)
            s_scr[...] = ge_ref[j][:, 0:1, 0:1] * st + _bdot_raw(ke_ref[:, rows, :], u, "TN", ps)
            for h in range(GDN_HEADS):
                cols = slice(GDN_DIM * h, GDN_DIM * (h + 1))
                oa_ref[rows, cols] = _gated_norm(o[h], z_ref[rows, cols], gnw_ref[...])

    return pl.pallas_call(
        body, name="gdn_scan", grid=(nc // G,),
        in_specs=[sp["hd"], sp["hd"], sp["hd"], sp["hd"], sp["hc"], sp["ge"], sp["z"], sp["vec"]],
        out_specs=(sp["oa"], sp["st"]),
        out_shape=(jax.ShapeDtypeStruct((S, GDN_WIDTH + DIL_WIDTH), F32),
                   jax.ShapeDtypeStruct((nc, GDN_HEADS, GDN_DIM, GDN_DIM), F32)),
        scratch_shapes=[pltpu.VMEM((GDN_HEADS, GDN_DIM, GDN_DIM), F32)],
        compiler_params=_cparams(("arbitrary",)),
    )(u_v, w_k, q_dec, k_end, attn, g_end, proj, gnw)


def _gdn_scan_bwd(u_v, w_k, q_dec, k_end, attn, g_end, proj, gnw, states, d_oa):
    S = proj.shape[0]
    C, G = GDN_CHUNK, GDN_SG
    nc = S // C
    sp = _gdn_scan_specs(S, G, True)
    ps, pb = _GDN_PASSES["scan"], _GDN_PASSES["bwd"]

    def body(uv_ref, wk_ref, qd_ref, ke_ref, at_ref, ge_ref, z_ref, gnw_ref, st_ref, doa_ref,
             duv_ref, dwk_ref, dqd_ref, dke_ref, dat_ref, dge_ref, dz_ref, dgnw_ref, ds_scr):
        @pl.when(pl.program_id(0) == 0)
        def _():
            ds_scr[...] = jnp.zeros_like(ds_scr)
            dgnw_ref[...] = jnp.zeros_like(dgnw_ref)

        dgnw = jnp.zeros((1, 128), F32)
        for j in reversed(range(G)):
            rows = slice(C * j, C * (j + 1))
            st = st_ref[j]
            wk, qd, ke, at = wk_ref[:, rows, :], qd_ref[:, rows, :], ke_ref[:, rows, :], at_ref[:, rows, :]
            u = uv_ref[:, rows, :] - _bdot_raw(wk, st, "NN", ps)
            o = _bdot_raw(qd, st, "NN", ps) + _bdot_raw(at, u, "NN", ps)
            dos = []
            for h in range(GDN_HEADS):
                cols = slice(GDN_DIM * h, GDN_DIM * (h + 1))
                _, vjp2 = jax.vjp(_gated_norm, o[h], z_ref[rows, cols], gnw_ref[...])
                do_h, dz_h, dgn = vjp2(doa_ref[rows, cols])
                dz_ref[rows, cols] = dz_h
                dgnw = dgnw + dgn
                dos.append(do_h)
            do = jnp.stack(dos)
            ds_new = ds_scr[...]
            du = _bdot_raw(at, do, "TN", pb) + _bdot_raw(ke, ds_new, "NN", pb)
            duv_ref[:, rows, :] = du
            dat_ref[:, rows, :] = _bdot_raw(do, u, "NT", pb)
            dqd_ref[:, rows, :] = _bdot_raw(do, st, "NT", pb)
            dke_ref[:, rows, :] = _bdot_raw(u, ds_new, "NT", pb)
            dwk_ref[:, rows, :] = -_bdot_raw(du, st, "NT", pb)
            d_ge = jnp.sum(jnp.sum(st * ds_new, axis=2, keepdims=True), axis=1, keepdims=True)
            dge_ref[j] = jnp.broadcast_to(d_ge, (GDN_HEADS, 8, 128))
            ds_scr[...] = ge_ref[j][:, 0:1, 0:1] * ds_new + _bdot_raw(qd, do, "TN", pb) - _bdot_raw(wk, du, "TN", pb)
        dgnw_ref[...] += dgnw

    return pl.pallas_call(
        body, name="gdn_scan_bwd", grid=(nc // G,),
        in_specs=[sp["hd"], sp["hd"], sp["hd"], sp["hd"], sp["hc"], sp["ge"], sp["z"], sp["vec"], sp["st"], sp["oa"]],
        out_specs=(sp["hd"], sp["hd"], sp["hd"], sp["hd"], sp["hc"], sp["ge"], sp["oa"], sp["vec"]),
        out_shape=(_hd_shape(S), _hd_shape(S), _hd_shape(S), _hd_shape(S), _hd_shape(S, C),
                   jax.ShapeDtypeStruct((nc, GDN_HEADS, 8, 128), F32), jax.ShapeDtypeStruct((S, GDN_WIDTH), F32),
                   jax.ShapeDtypeStruct((1, 128), F32)),
        scratch_shapes=[pltpu.VMEM((GDN_HEADS, GDN_DIM, GDN_DIM), F32)],
        compiler_params=_cparams(("arbitrary",)),
    )(u_v, w_k, q_dec, k_end, attn, g_end, proj, gnw, states, d_oa)


def _gdn_post(proj, conv_w, gp, tinv, u_v, w_k, d_uv, d_wk, d_qd, d_ke, d_at, d_ge):
    S = proj.shape[0]
    C, G = GDN_CHUNK, GDN_PG
    nc = S // C
    sp = _gdn_pre_specs(S, G)
    pb = _GDN_PASSES["bwd"]

    def body(cur_ref, prev_ref, ba_ref, cw_ref, gp_ref, ti_ref, uv_ref, wk_ref, duv_ref, dwk_ref, dqd_ref, dke_ref,
             dat_ref, dge_ref, dpre_ref, dba_ref, dgp_ref):
        i = pl.program_id(0)

        @pl.when(i == 0)
        def _():
            dgp_ref[...] = jnp.zeros_like(dgp_ref)

        prev = prev_ref[...] * jnp.where(i == 0, 0.0, 1.0)
        pre = _conv_taps(jnp.concatenate([prev, cur_ref[...]], axis=0), cw_ref[...], GDN_CONV, C * G)
        sg = jax.nn.sigmoid(pre)
        dsilu = sg * (1.0 + pre * (1.0 - sg))
        pairs, args = _gdn_pairs(pre * sg, ba_ref[...], gp_ref[...], G)
        _, vjp1 = jax.vjp(functools.partial(_gdn_stage1, dot=_bdot), *args)

        def take(ref):
            return jnp.stack([ref[h, C * j:C * (j + 1), :] for j, h in pairs])

        t, u_v, w_k = take(ti_ref), take(uv_ref), take(wk_ref)
        d_v = _bdot_raw(t, take(duv_ref), "TN", pb)
        d_rk = _bdot_raw(t, take(dwk_ref), "TN", pb)
        d_l = -(_bdot_raw(d_v, u_v, "NT", pb) + _bdot_raw(d_rk, w_k, "NT", pb))
        d_ge = jnp.stack([dge_ref[j, h][0:1, 0:1] for j, h in pairs])
        dcq, dck, dcv, db, da, dalog, ddtb = vjp1((d_l, d_v, d_rk, take(dqd_ref), take(dat_ref), take(dke_ref), d_ge))
        lane = lax.broadcasted_iota(jnp.int32, (C, 128), 1)
        lane1 = lax.broadcasted_iota(jnp.int32, (1, 128), 1)
        dgp = jnp.zeros((1, 128), F32)
        for j in range(G):
            rows = slice(C * j, C * (j + 1))
            dba = jnp.zeros((C, 128), F32)
            for h in range(GDN_HEADS):
                b = GDN_HEADS * j + h
                for o_, dcx in ((0, dcq), (GDN_WIDTH, dck), (2 * GDN_WIDTH, dcv)):
                    cols = slice(o_ + GDN_DIM * h, o_ + GDN_DIM * (h + 1))
                    dpre_ref[rows, cols] = dcx[b] * dsilu[rows, cols]
                dba = dba + jnp.where(lane == h, db[b], 0.0) + jnp.where(lane == GDN_HEADS + h, da[b], 0.0)
                dgp = dgp + jnp.where(lane1 == h, dalog[b], 0.0) + jnp.where(lane1 == GDN_HEADS + h, ddtb[b], 0.0)
            dba_ref[rows, :] = dba
        dgp_ref[0:1, :] += dgp

    T = C * G
    return pl.pallas_call(
        body, name="gdn_post", grid=(nc // G,),
        in_specs=[sp["cur"], sp["prev"], sp["ba"], sp["cw"], sp["vec"], sp["hc"], sp["hd"], sp["hd"], sp["hd"], sp["hd"],
                  sp["hd"], sp["hd"], sp["hc"], sp["ge"]],
        out_specs=(sp["cur"], pl.BlockSpec((T, 128), lambda i: (i, 0)), pl.BlockSpec((8, 128), lambda i: (0, 0))),
        out_shape=(jax.ShapeDtypeStruct((S, 3 * GDN_WIDTH), F32), jax.ShapeDtypeStruct((S, 128), F32),
                   jax.ShapeDtypeStruct((8, 128), F32)),
        compiler_params=_cparams(("arbitrary",)),
    )(proj, proj, proj, conv_w, gp, tinv, u_v, w_k, d_uv, d_wk, d_qd, d_ke, d_at, d_ge)


def _conv_bwd(dpre, x, xcol0, w, K, name, tc):
    S, Cc = dpre.shape
    T = _pick_tile(S, 256)
    nt, ncol = S // T, Cc // tc
    xo = xcol0 // tc

    def body(d_ref, dn_ref, x_ref, xp_ref, w_ref, dx_ref, dw_ref):
        i = pl.program_id(1)
        dn = dn_ref[...] * jnp.where(i == nt - 1, 0.0, 1.0)
        dv = d_ref[...]
        ext_d = jnp.concatenate([dv, dn], axis=0)
        dx_ref[...] = _conv_taps_t(ext_d, w_ref[...], K, T).astype(dx_ref.dtype)
        xp = xp_ref[...] * jnp.where(i == 0, 0.0, 1.0)
        ext_x = jnp.concatenate([xp, x_ref[...]], axis=0)

        @pl.when(i == 0)
        def _():
            dw_ref[...] = jnp.zeros_like(dw_ref)

        for k in range(K):
            dw_ref[k:k + 1, :] += jnp.sum(dv * _shifted(ext_x, (K - 1) - k, 8, T), axis=0, keepdims=True)

    r8 = T // 8
    return pl.pallas_call(
        body, name=name, grid=(ncol, nt),
        in_specs=[pl.BlockSpec((T, tc), lambda j, i: (i, j)),
                  pl.BlockSpec((8, tc), lambda j, i: (jnp.minimum((i + 1) * r8, S // 8 - 1), j)),
                  pl.BlockSpec((T, tc), lambda j, i: (i, j + xo)),
                  pl.BlockSpec((8, tc), lambda j, i: (jnp.maximum(i * r8 - 1, 0), j + xo)),
                  pl.BlockSpec((K, tc), lambda j, i: (0, j))],
        out_specs=(pl.BlockSpec((T, tc), lambda j, i: (i, j)), pl.BlockSpec((K, tc), lambda j, i: (0, j))),
        out_shape=(jax.ShapeDtypeStruct((S, Cc), _MXU), jax.ShapeDtypeStruct((K, Cc), F32)),
        compiler_params=_cparams(("parallel", "arbitrary")),
    )(dpre, dpre, x, x, w)


def _dil_bias(nt, T):
    d = (np.arange(nt)[:, None, None] * T + np.arange(T)[None, None, :] - np.arange(T)[None, :, None])
    cnt = ((d >= 0) & (d <= 128)).astype(np.float64) + ((d >= 0) & (d % 4 == 0) & (d <= 512)) + ((d >= 0) & (d % 16 == 0))
    return jnp.asarray(np.where(cnt > 0, np.log(np.maximum(cnt, 1.0)), -1e30), dtype=F32)


def _attn_fwd(proj, mix, after=()):
    S = proj.shape[0]
    T = min(ATT_T, S)
    nt = S // T
    bias = _dil_bias(nt, T)
    scale = DIL_DIM ** -0.5
    npair = DIL_WIDTH // 128
    qb0, kb0, vb0 = P_QKVB // 128, (P_QKVB + DIL_WIDTH) // 128, (P_QKVB + 2 * DIL_WIDTH) // 128

    def body(q_ref, k_ref, v_ref, b_ref, *rest):
        o_ref, lse_ref = rest[-2:]
        i = pl.program_id(1)
        qs = (q_ref[...] * scale).astype(_MXU)

        def step(j, carry):
            kt = k_ref[pl.ds(pl.multiple_of(j * T, T), T), :].astype(_MXU)
            vt = v_ref[pl.ds(pl.multiple_of(j * T, T), T), :].astype(_MXU)
            bt = b_ref[i - j]
            out = []
            for hh in range(2):
                m, l, acc = carry[hh]
                sl = slice(hh * DIL_DIM, (hh + 1) * DIL_DIM)
                s = lax.dot_general(kt[:, sl], qs[:, sl], (_NT, ((), ())), preferred_element_type=F32) + bt
                m_new = jnp.maximum(m, jnp.max(s, axis=0, keepdims=True))
                p = jnp.exp(s - m_new)
                a = jnp.exp(m - m_new)
                l = a * l + jnp.sum(p, axis=0, keepdims=True)
                acc = a * acc + lax.dot_general(vt[:, sl], p.astype(_MXU), (_TN, ((), ())), preferred_element_type=F32)
                out.append((m_new, l, acc))
            return tuple(out)

        init = tuple((jnp.full((1, T), -1e30, F32), jnp.zeros((1, T), F32), jnp.zeros((DIL_DIM, T), F32)) for _ in range(2))
        res = lax.fori_loop(0, i + 1, step, init)
        lse_ref[...] = jnp.zeros_like(lse_ref)
        for hh in range(2):
            m, l, acc = res[hh]
            o_ref[:, hh * DIL_DIM:(hh + 1) * DIL_DIM] = (acc / l).T
            lse_ref[hh:hh + 1, :] = m + jnp.log(l)

    return pl.pallas_call(
        body, name="attn_fwd", grid=(npair, nt),
        in_specs=[pl.BlockSpec((T, 128), lambda p, i: (i, qb0 + p)),
                  pl.BlockSpec((S, 128), lambda p, i: (0, kb0 + p)),
                  pl.BlockSpec((S, 128), lambda p, i: (0, vb0 + p)),
                  pl.BlockSpec((nt, T, T), lambda p, i: (0, 0, 0))] + [_ANY] * (1 + len(after)),
        out_specs=(pl.BlockSpec((T, 128), lambda p, i: (i, GDN_WIDTH // 128 + p)),
                   pl.BlockSpec((None, None, 8, T), lambda p, i: (p, i, 0, 0))),
        out_shape=(jax.ShapeDtypeStruct(mix.shape, F32), jax.ShapeDtypeStruct((npair, nt, 8, T), F32)),
        input_output_aliases={4: 0},
        compiler_params=_cparams(("parallel", "parallel")),
    )(proj, proj, proj, bias, mix, *after)


def _attn_bwd(proj, mix, lse, d_mix):
    S = proj.shape[0]
    T = min(ATT_T, S)
    nt = S // T
    bias = _dil_bias(nt, T)
    scale = DIL_DIM ** -0.5
    npair = DIL_WIDTH // 128
    qb0, kb0, vb0 = P_QKVB // 128, (P_QKVB + DIL_WIDTH) // 128, (P_QKVB + 2 * DIL_WIDTH) // 128

    def body(q_ref, k_ref, v_ref, o_ref, lse_ref, do_ref, b_ref, dq_ref, dk_ref, dv_ref, dq_scr):
        j = pl.program_id(1)

        @pl.when(j == 0)
        def _():
            dq_scr[...] = jnp.zeros_like(dq_scr)

        kt = k_ref[...].astype(_MXU)
        vt = v_ref[...].astype(_MXU)
        ones = jnp.ones((8, DIL_DIM), F32)

        def step(i, carry):
            rows = pl.ds(pl.multiple_of(i * T, T), T)
            qs = (q_ref[rows, :] * scale).astype(_MXU)
            dov = do_ref[rows, :]
            prod = dov * o_ref[rows, :]
            lsev = lse_ref[i]
            dob = dov.astype(_MXU)
            bt = b_ref[i - j]
            out = []
            dqs = []
            for hh in range(2):
                dk, dv = carry[hh]
                sl = slice(hh * DIL_DIM, (hh + 1) * DIL_DIM)
                s = lax.dot_general(kt[:, sl], qs[:, sl], (_NT, ((), ())), preferred_element_type=F32) + bt
                p = jnp.exp(s - lsev[hh:hh + 1, :])
                delta = lax.dot_general(ones, prod[:, sl], (_NT, ((), ())), precision=_HI, preferred_element_type=F32)[0:1, :]
                dp = lax.dot_general(vt[:, sl], dob[:, sl], (_NT, ((), ())), preferred_element_type=F32)
                ds = (p * (dp - delta)).astype(_MXU)
                dv = dv + lax.dot_general(p.astype(_MXU), dob[:, sl], (_NN, ((), ())), preferred_element_type=F32)
                dk = dk + lax.dot_general(ds, qs[:, sl], (_NN, ((), ())), preferred_element_type=F32)
                dqs.append(lax.dot_general(ds, kt[:, sl], (_TN, ((), ())), preferred_element_type=F32) * scale)
                out.append((dk, dv))
            dq_scr[rows, :] += jnp.concatenate(dqs, axis=1)
            return tuple(out)

        init = tuple((jnp.zeros((T, DIL_DIM), F32), jnp.zeros((T, DIL_DIM), F32)) for _ in range(2))
        res = lax.fori_loop(j, nt, step, init)
        dk_ref[...] = jnp.concatenate([res[0][0], res[1][0]], axis=1).astype(dk_ref.dtype)
        dv_ref[...] = jnp.concatenate([res[0][1], res[1][1]], axis=1).astype(dv_ref.dtype)

        @pl.when(j == nt - 1)
        def _():
            dq_ref[...] = dq_scr[...].astype(dq_ref.dtype)

    full = lambda c0: pl.BlockSpec((S, 128), lambda p, j: (0, c0 + p))
    tile = lambda c0: pl.BlockSpec((T, 128), lambda p, j: (j, c0 + p))
    out3 = jax.ShapeDtypeStruct((S, DIL_WIDTH), _MXU)
    return pl.pallas_call(
        body, name="attn_bwd", grid=(npair, nt),
        in_specs=[full(qb0), tile(kb0), tile(vb0), full(GDN_WIDTH // 128),
                  pl.BlockSpec((None, nt, 8, T), lambda p, j: (p, 0, 0, 0)), full(GDN_WIDTH // 128),
                  pl.BlockSpec((nt, T, T), lambda p, j: (0, 0, 0))],
        out_specs=(full(0), tile(0), tile(0)),
        out_shape=(out3, out3, out3),
        scratch_shapes=[pltpu.VMEM((S, 128), F32)],
        compiler_params=_cparams(("parallel", "arbitrary")),
    )(proj, proj, proj, mix, lse, d_mix, bias)


def _ffn_act(up, cw):
    S, Cc = up.shape[0], up.shape[1] // 2
    T, tc = _pick_tile(S, 256), _pick_tile(Cc, 1536)
    r16 = T // 16
    nct = Cc // tc

    def body(g_ref, gp_ref, u_ref, up_ref, wg_ref, wu_ref, o_ref):
        keep = jnp.where(pl.program_id(1) == 0, 0.0, 1.0)
        cg = _conv_taps(jnp.concatenate([gp_ref[8:16, :].astype(F32) * keep, g_ref[...].astype(F32)], axis=0),
                        wg_ref[...], FFN_CONV, T)
        cu = _conv_taps(jnp.concatenate([up_ref[8:16, :].astype(F32) * keep, u_ref[...].astype(F32)], axis=0),
                        wu_ref[...], FFN_CONV, T)
        o_ref[...] = (_silu(cg) * cu).astype(o_ref.dtype)

    cur = lambda o: pl.BlockSpec((T, tc), lambda j, i: (i, j + o))
    prev = lambda o: pl.BlockSpec((16, tc), lambda j, i: (jnp.maximum(i * r16 - 1, 0), j + o))
    wsp = lambda o: pl.BlockSpec((FFN_CONV, tc), lambda j, i: (0, j + o))
    return pl.pallas_call(
        body, name="ffn_act", grid=(nct, S // T),
        in_specs=[cur(0), prev(0), cur(nct), prev(nct), wsp(0), wsp(nct)], out_specs=cur(0),
        out_shape=jax.ShapeDtypeStruct((S, Cc), _MXU),
        compiler_params=_cparams(("parallel", "parallel")),
    )(up, up, up, up, cw, cw)


def _ffn_act_bwd(d_act, up, cw):
    S, Cc = up.shape[0], up.shape[1] // 2
    T, tc = _pick_tile(S, 256), _pick_tile(Cc, 1536)
    r8, r16 = T // 8, T // 16
    nt = S // T
    nct = Cc // tc
    K = FFN_CONV

    def body(da_ref, dan_ref, g_ref, gp_ref, gn_ref, u_ref, up_ref, un_ref, wg_ref, wu_ref,
             dg_ref, du_ref, dwg_ref, dwu_ref):
        i = pl.program_id(1)
        keep_p = jnp.where(i == 0, 0.0, 1.0)
        keep_n = jnp.where(i == nt - 1, 0.0, 1.0)
        wg, wu = wg_ref[...], wu_ref[...]
        xg = jnp.concatenate([gp_ref[8:16, :].astype(F32) * keep_p, g_ref[...].astype(F32),
                              gn_ref[0:8, :].astype(F32) * keep_n], axis=0)
        xu = jnp.concatenate([up_ref[8:16, :].astype(F32) * keep_p, u_ref[...].astype(F32),
                              un_ref[0:8, :].astype(F32) * keep_n], axis=0)
        cg = _conv_taps(xg, wg, K, T + 8)
        cu = _conv_taps(xu, wu, K, T + 8)
        da = jnp.concatenate([da_ref[...], dan_ref[...] * keep_n], axis=0)
        sg = jax.nn.sigmoid(cg)
        d_cg = da * cu * (sg * (1.0 + cg * (1.0 - sg)))
        d_cu = da * (cg * sg)
        dg_ref[...] = _conv_taps_t(d_cg, wg, K, T).astype(dg_ref.dtype)
        du_ref[...] = _conv_taps_t(d_cu, wu, K, T).astype(du_ref.dtype)

        @pl.when(i == 0)
        def _():
            dwg_ref[...] = jnp.zeros_like(dwg_ref)
            dwu_ref[...] = jnp.zeros_like(dwu_ref)

        for k in range(K):
            dwg_ref[k:k + 1, :] += jnp.sum(d_cg[0:T, :] * _shifted(xg, (K - 1) - k, 8, T), axis=0, keepdims=True)
            dwu_ref[k:k + 1, :] += jnp.sum(d_cu[0:T, :] * _shifted(xu, (K - 1) - k, 8, T), axis=0, keepdims=True)

    cur = lambda o: pl.BlockSpec((T, tc), lambda j, i: (i, j + o))
    prev = lambda o: pl.BlockSpec((16, tc), lambda j, i: (jnp.maximum(i * r16 - 1, 0), j + o))
    nxt = lambda o: pl.BlockSpec((16, tc), lambda j, i: (jnp.minimum((i + 1) * r16, S // 16 - 1), j + o))
    nxt8 = pl.BlockSpec((8, tc), lambda j, i: (jnp.minimum((i + 1) * r8, S // 8 - 1), j))
    wsp = lambda o: pl.BlockSpec((K, tc), lambda j, i: (0, j + o))
    return pl.pallas_call(
        body, name="ffn_act_bwd", grid=(nct, nt),
        in_specs=[cur(0), nxt8, cur(0), prev(0), nxt(0), cur(nct), prev(nct), nxt(nct), wsp(0), wsp(nct)],
        out_specs=(cur(0), cur(0), wsp(0), wsp(0)),
        out_shape=(jax.ShapeDtypeStruct((S, Cc), _MXU), jax.ShapeDtypeStruct((S, Cc), _MXU),
                   jax.ShapeDtypeStruct((K, Cc), F32), jax.ShapeDtypeStruct((K, Cc), F32)),
        compiler_params=_cparams(("parallel", "arbitrary")),
    )(d_act, d_act, up, up, up, up, up, up, cw, cw)


def _local_step(x, tgt, h1, n1w, n2w, fnw, gp, gnw, wp, conv_w, fcw, rest_weights, early_grads):
    proj = _mm(h1, wp, "nn", name="proj")
    u_v, w_k, q_dec, k_end, attn, tinv, g_end = _gdn_pre(proj, conv_w, gp)
    mix, states = _gdn_scan(u_v, w_k, q_dec, k_end, attn, g_end, proj, gnw)
    mix, lse = _attn_fwd(proj, mix, after=[rest_weights[0]([mix])])
    w_out, w_up4, w_down = rest_weights[1]([mix])
    x2 = _mm(mix, w_out, "nn", residual=x, name="outproj")
    h2 = _rmsnorm_fwd(x2, n2w, "norm2")
    up = _mm(h2, w_up4, "nn", b_blocks=True, out_dtype=_MXU, name="up")
    act = _ffn_act(up, fcw)
    x3 = _mm(act, w_down, "nn", residual=x2, name="down")
    loss, dx3, dx3n, d_fnw = _loss_head(x3, fnw, tgt, "loss_head")
    d_act = _mm(dx3n, w_down, "nt", name="d_act")
    d_wdown = _mm(act, dx3n, "tn", name="d_wdown")
    d_upg, d_upu, d_fcwg, d_fcwu = _ffn_act_bwd(d_act, up, fcw)
    d_wup = _mm(h2, d_upg, "tn", place=("blocks", N_CHIPS, 0), tn=w_up4.shape[2], name="d_wgate")
    d_wup = _mm(h2, d_upu, "tn", place=("blocks", N_CHIPS, N_CHIPS // 2), tn=w_up4.shape[2], into=d_wup, name="d_wup")
    token = early_grads[0](d_wup, d_wdown)
    d_h2 = _mm_nt_blocks([d_upg, d_upu], w_up4, "d_h2", after=[token])
    dx2, d_n2w = _rmsnorm_bwd(d_h2, x2, n2w, dx3, "norm2_bwd", after=[token])
    token = early_grads[1](dx2)
    d_mix = _mm(dx2, w_out, "nt", name="d_mix")
    d_wout = _mm(mix, dx2, "tn", name="d_wout")
    dq_b, dk_b, dv_b = _attn_bwd(proj, mix, lse, d_mix)
    d_uv, d_wk, d_qd, d_ke, d_at, d_ge, d_z, d_gnw = _gdn_scan_bwd(u_v, w_k, q_dec, k_end, attn, g_end, proj,
                                                                   gnw + token[0:1, 0:1], states, d_mix)
    d_pre, d_ba, d_gp = _gdn_post(proj, conv_w, gp, tinv, u_v, w_k, d_uv, d_wk, d_qd, d_ke, d_at, d_ge)
    d_qkva, d_convw = _conv_bwd(d_pre, proj, 0, conv_w, GDN_CONV, "gdn_conv_bwd", 512)
    d_proj = jnp.concatenate([d_qkva, d_z.astype(_MXU), dq_b, dk_b, dv_b, d_ba.astype(_MXU),
                              jnp.zeros((x.shape[0], P_COLS - P_BA - 128), _MXU)], axis=1)
    d_wp = _mm(h1, d_proj, "tn", name="d_wp")
    token = early_grads[2](d_wp, d_wout)
    d_h1 = _mm(d_proj, wp, "nt", name="d_h1", after=[token])
    dx, d_n1w = _rmsnorm_bwd(d_h1, x, n1w, dx2, "norm1_bwd", after=[token])
    grads = dict(wp=d_wp, conv_w=d_convw, w_out=d_wout, w_up=d_wup, fcw_g=d_fcwg, fcw_u=d_fcwu, w_down=d_wdown,
                 n1w=d_n1w, n2w=d_n2w, fnw=d_fnw, gp=d_gp, gnw=d_gnw)
    return loss, dx, grads


_HBM = pl.BlockSpec(memory_space=pltpu.HBM)


def _pos():
    return lax.axis_index("x"), lax.axis_index("y"), lax.axis_index("c")


def _other_chips(x, y):
    return [(1 - x, y), (x, 1 - y), (1 - x, 1 - y)]


def _halvable(shape):
    return shape[0] % 32 == 0


def _rows_of_half(shape, half):
    if not _halvable(shape):
        return pl.ds(0, shape[0])
    return pl.ds(pl.multiple_of(half * (shape[0] // 2), 16), shape[0] // 2)


_SEM = pl.BlockSpec(memory_space=pltpu.SEMAPHORE)
_ANY = pl.BlockSpec(memory_space=pl.ANY)
_DATAFLOW = pltpu.SideEffectType.DATAFLOW_SIDE_EFFECTING


def _in_hbm(a):
    return pltpu.with_memory_space_constraint(a, pltpu.HBM)


def _halves_copy(src_refs, land_refs, send_sems, recv_sems, shapes, a, j, block, x, y, c):
    px, py = _other_chips(x, y)[j]
    rows = _rows_of_half(shapes[a], c)
    return pltpu.make_async_remote_copy(
        src_ref=src_refs[a].at[rows, :], dst_ref=land_refs[a].at[block, rows, :], send_sem=send_sems.at[3 * a + j],
        recv_sem=recv_sems.at[3 * a + j], device_id=(px, py, c), device_id_type=MESH)


def _gather_halves_start(shards, after, name):
    n = len(shards)
    shapes = [s.shape for s in shards]

    def body(*refs):
        ins, lands = refs[:n], refs[n:2 * n]
        send_sems, recv_sems = refs[2 * n + 1], refs[2 * n + 2]
        token = refs[-1]
        x, y, c = _pos()
        q = 2 * x + y
        for a in range(n):
            for j in range(3):
                _halves_copy(ins, lands, send_sems, recv_sems, shapes, a, j, q, x, y, c).start()
        token[...] = jnp.zeros_like(token)

    land_shapes = [(N_CHIPS,) + s.shape for s in shards]
    return pl.pallas_call(
        body, name=name,
        out_shape=(pltpu.SemaphoreType.DMA((3 * n,)), pltpu.SemaphoreType.DMA((3 * n,)),
                   *[pltpu.HBM(s.shape, s.dtype) for s in shards],
                   *[pltpu.HBM(ls, s.dtype) for ls, s in zip(land_shapes, shards)],
                   jax.ShapeDtypeStruct((8, 128), F32)),
        in_specs=[_HBM] * (2 * n) + [_ANY],
        out_specs=(_SEM, _SEM, *[_HBM] * (2 * n), pl.BlockSpec(memory_space=pltpu.VMEM)),
        input_output_aliases={a: 2 + a for a in range(2 * n)},
        compiler_params=pltpu.CompilerParams(has_side_effects=_DATAFLOW),
    )(*[_in_hbm(s) for s in shards], *[_in_hbm(lax.empty(ls, s.dtype)) for ls, s in zip(land_shapes, shards)], after)


def _gather_halves_wait(started, after, name):
    send_sems, recv_sems, *thru = started
    n = len(thru) // 2
    shapes = [t.shape for t in thru[:n]]

    def body(*refs):
        ins, lands = refs[:n], refs[n:2 * n]
        send_sems, recv_sems = refs[2 * n], refs[2 * n + 1]
        x, y, c = _pos()
        q = 2 * x + y
        chips = _other_chips(x, y)
        for a in range(n):
            for j, (px, py) in enumerate(chips):
                _halves_copy(ins, lands, send_sems, recv_sems, shapes, a, j, q, x, y, c).wait_send()
                _halves_copy(ins, lands, send_sems, recv_sems, shapes, a, j, 2 * px + py, x, y, c).wait_recv()

    outs = pl.pallas_call(
        body, name=name, out_shape=[pltpu.HBM(t.shape, t.dtype) for t in thru],
        in_specs=[_HBM] * (2 * n) + [_SEM, _SEM] + [_ANY] * len(after), out_specs=[_HBM] * (2 * n),
        input_output_aliases={a: a for a in range(2 * n)},
        compiler_params=pltpu.CompilerParams(has_side_effects=_DATAFLOW),
    )(*thru, send_sems, recv_sems, *after)
    return outs[:n], outs[n:]


def _sibling_fill(gathered, name):
    big = [a for a, g in enumerate(gathered) if _halvable(g.shape[1:])]
    n = len(gathered)

    def body(*refs):
        ins, outs = refs[:n], refs[n:2 * n]
        send_sems, recv_sems = refs[2 * n:]
        x, y, c = _pos()
        chips = _other_chips(x, y)

        def copy(k, j, half):
            a = big[k]
            px, py = chips[j]
            rows = _rows_of_half(gathered[a].shape[1:], half)
            return pltpu.make_async_remote_copy(
                src_ref=ins[a].at[2 * px + py, rows, :], dst_ref=outs[a].at[2 * px + py, rows, :],
                send_sem=send_sems.at[3 * k + j], recv_sem=recv_sems.at[3 * k + j],
                device_id=(x, y, 1 - c), device_id_type=MESH)

        sends = [copy(k, j, c) for k in range(len(big)) for j in range(3)]
        for cp in sends:
            cp.start()
        for k in range(len(big)):
            for j in range(3):
                copy(k, j, 1 - c).wait_recv()
        for cp in sends:
            cp.wait_send()

    return pl.pallas_call(
        body, name=name, in_specs=[_HBM] * n, out_specs=[_HBM] * n,
        out_shape=[jax.ShapeDtypeStruct(g.shape, g.dtype) for g in gathered],
        input_output_aliases={a: a for a in range(n)},
        scratch_shapes=[pltpu.SemaphoreType.DMA((3 * len(big),)), pltpu.SemaphoreType.DMA((3 * len(big),))],
    )(*gathered)


def _fill_copy(refs, send_sems, recv_sems, shapes, a, j, half, x, y, c):
    px, py = _other_chips(x, y)[j]
    rows = _rows_of_half(shapes[a], half)
    return pltpu.make_async_remote_copy(
        src_ref=refs[a].at[2 * px + py, rows, :], dst_ref=refs[a].at[2 * px + py, rows, :],
        send_sem=send_sems.at[3 * a + j], recv_sem=recv_sems.at[3 * a + j],
        device_id=(x, y, 1 - c), device_id_type=MESH)


def _sibling_fill_start(gathered, name):
    n = len(gathered)
    shapes = [g.shape[1:] for g in gathered]

    def body(*refs):
        ins = refs[:n]
        send_sems, recv_sems = refs[n], refs[n + 1]
        token = refs[-1]
        x, y, c = _pos()
        for a in range(n):
            for j in range(3):
                _fill_copy(ins, send_sems, recv_sems, shapes, a, j, c, x, y, c).start()
        token[...] = jnp.zeros_like(token)

    return pl.pallas_call(
        body, name=name,
        out_shape=(pltpu.SemaphoreType.DMA((3 * n,)), pltpu.SemaphoreType.DMA((3 * n,)),
                   *[pltpu.HBM(g.shape, g.dtype) for g in gathered], jax.ShapeDtypeStruct((8, 128), F32)),
        in_specs=[_HBM] * n,
        out_specs=(_SEM, _SEM, *[_HBM] * n, pl.BlockSpec(memory_space=pltpu.VMEM)),
        input_output_aliases={a: 2 + a for a in range(n)},
        compiler_params=pltpu.CompilerParams(has_side_effects=_DATAFLOW),
    )(*[_in_hbm(g) for g in gathered])


def _sibling_fill_wait(started, after, name):
    send_sems, recv_sems, *thru = started
    n = len(thru)
    shapes = [t.shape[1:] for t in thru]

    def body(*refs):
        ins = refs[:n]
        send_sems, recv_sems = refs[n], refs[n + 1]
        x, y, c = _pos()
        for a in range(n):
            for j in range(3):
                _fill_copy(ins, send_sems, recv_sems, shapes, a, j, c, x, y, c).wait_send()
                _fill_copy(ins, send_sems, recv_sems, shapes, a, j, 1 - c, x, y, c).wait_recv()

    return pl.pallas_call(
        body, name=name, out_shape=[pltpu.HBM(t.shape, t.dtype) for t in thru],
        in_specs=[_HBM] * n + [_SEM, _SEM] + [_ANY] * len(after), out_specs=[_HBM] * n,
        input_output_aliases={a: a for a in range(n)},
        compiler_params=pltpu.CompilerParams(has_side_effects=_DATAFLOW),
    )(*thru, send_sems, recv_sems, *after)


def _place_own(shards, gathered, cq, name):
    n = len(shards)
    steps = 4

    def body(cq_ref, *refs):
        for a in range(n):
            refs[2 * n + a][...] = refs[a][...]

    def tile(shape):
        return shape[0] // steps if _halvable(shape) else shape[0]

    in_specs = [pl.BlockSpec((tile(s.shape), s.shape[1]), (lambda i, s_: (i, 0)) if _halvable(s.shape) else (lambda i, s_: (0, 0)))
                for s in shards]
    in_specs += [pl.BlockSpec(memory_space=pl.ANY)] * n
    out_specs = [pl.BlockSpec((None, tile(s.shape), s.shape[1]),
                              (lambda i, s_: (s_[1], i, 0)) if _halvable(s.shape) else (lambda i, s_: (s_[1], 0, 0)))
                 for s in shards]
    gs = pltpu.PrefetchScalarGridSpec(num_scalar_prefetch=1, grid=(steps,), in_specs=in_specs, out_specs=out_specs)
    return pl.pallas_call(
        body, name=name, grid_spec=gs, out_shape=[jax.ShapeDtypeStruct(g.shape, g.dtype) for g in gathered],
        input_output_aliases={1 + n + a: a for a in range(n)},
        compiler_params=_cparams(("arbitrary",)),
    )(cq, *shards, *gathered)


def _half_rows(ref, c, rh):
    return ref.at[:, pl.ds(pl.multiple_of(c * rh, 8), rh), :]


def _chips_copy(src_refs, land_refs, send_sems, recv_sems, a, j, x, y, c):
    px, py = _other_chips(x, y)[j]
    return pltpu.make_async_remote_copy(src_ref=src_refs[a].at[2 * px + py], dst_ref=land_refs[a].at[j],
                                        send_sem=send_sems.at[3 * a + j], recv_sem=recv_sems.at[3 * a + j],
                                        device_id=(px, py, c), device_id_type=MESH)


def _grad_chips_start(parts, name):
    n = len(parts)

    def body(*refs):
        ins, lands = refs[:n], refs[n:2 * n]
        send_sems, recv_sems = refs[2 * n], refs[2 * n + 1]
        token = refs[-1]
        x, y, c = _pos()
        for a in range(n):
            for j in range(3):
                _chips_copy(ins, lands, send_sems, recv_sems, a, j, x, y, c).start()
        token[...] = jnp.zeros_like(token)

    land_shapes = [(3,) + p.shape[1:] for p in parts]
    return pl.pallas_call(
        body, name=name,
        out_shape=(pltpu.SemaphoreType.DMA((3 * n,)), pltpu.SemaphoreType.DMA((3 * n,)),
                   *[pltpu.HBM(p.shape, p.dtype) for p in parts],
                   *[pltpu.HBM(ls, p.dtype) for ls, p in zip(land_shapes, parts)],
                   jax.ShapeDtypeStruct((8, 128), F32)),
        in_specs=[_HBM] * (2 * n),
        out_specs=(_SEM, _SEM, *[_HBM] * (2 * n), pl.BlockSpec(memory_space=pltpu.VMEM)),
        input_output_aliases={a: 2 + a for a in range(2 * n)},
        compiler_params=pltpu.CompilerParams(has_side_effects=_DATAFLOW),
    )(*[_in_hbm(p) for p in parts], *[_in_hbm(lax.empty(ls, p.dtype)) for ls, p in zip(land_shapes, parts)])


def _grad_chips_wait(started, after, name):
    send_sems, recv_sems, *thru = started
    n = len(thru) // 2

    def body(*refs):
        ins, lands = refs[:n], refs[n:2 * n]
        send_sems, recv_sems = refs[2 * n], refs[2 * n + 1]
        x, y, c = _pos()
        for a in range(n):
            for j in range(3):
                cp = _chips_copy(ins, lands, send_sems, recv_sems, a, j, x, y, c)
                cp.wait_send()
                cp.wait_recv()

    outs = pl.pallas_call(
        body, name=name, out_shape=[pltpu.HBM(t.shape, t.dtype) for t in thru],
        in_specs=[_HBM] * (2 * n) + [_SEM, _SEM] + [_ANY] * len(after), out_specs=[_HBM] * (2 * n),
        input_output_aliases={a: a for a in range(2 * n)},
        compiler_params=pltpu.CompilerParams(has_side_effects=_DATAFLOW),
    )(*thru, send_sems, recv_sems, *after)
    return outs[n:]


def _sibling_copy(src_refs, land_refs, send_sems, recv_sems, rhs, a, c, x, y):
    return pltpu.make_async_remote_copy(src_ref=_half_rows(src_refs[a], 1 - c, rhs[a]), dst_ref=land_refs[a],
                                        send_sem=send_sems.at[a], recv_sem=recv_sems.at[a],
                                        device_id=(x, y, 1 - c), device_id_type=MESH)


def _grad_sibling_start(fams, name):
    n = len(fams)
    rhs = [f.shape[1] // 2 for f in fams]

    def body(*refs):
        ins, lands = refs[:n], refs[n:2 * n]
        send_sems, recv_sems = refs[2 * n], refs[2 * n + 1]
        token = refs[-1]
        x, y, c = _pos()
        for a in range(n):
            _sibling_copy(ins, lands, send_sems, recv_sems, rhs, a, c, x, y).start()
        token[...] = jnp.zeros_like(token)

    land_shapes = [(f.shape[0], f.shape[1] // 2, f.shape[2]) for f in fams]
    return pl.pallas_call(
        body, name=name,
        out_shape=(pltpu.SemaphoreType.DMA((n,)), pltpu.SemaphoreType.DMA((n,)),
                   *[pltpu.HBM(f.shape, f.dtype) for f in fams],
                   *[pltpu.HBM(ls, f.dtype) for ls, f in zip(land_shapes, fams)],
                   jax.ShapeDtypeStruct((8, 128), F32)),
        in_specs=[_HBM] * (2 * n),
        out_specs=(_SEM, _SEM, *[_HBM] * (2 * n), pl.BlockSpec(memory_space=pltpu.VMEM)),
        input_output_aliases={a: 2 + a for a in range(2 * n)},
        compiler_params=pltpu.CompilerParams(has_side_effects=_DATAFLOW),
    )(*[_in_hbm(f) for f in fams], *[_in_hbm(lax.empty(ls, f.dtype)) for ls, f in zip(land_shapes, fams)])


def _grad_sibling_wait(started, after, name):
    send_sems, recv_sems, *thru = started
    n = len(thru) // 2
    rhs = [t.shape[1] // 2 for t in thru[:n]]

    def body(*refs):
        ins, lands = refs[:n], refs[n:2 * n]
        send_sems, recv_sems = refs[2 * n], refs[2 * n + 1]
        x, y, c = _pos()
        for a in range(n):
            cp = _sibling_copy(ins, lands, send_sems, recv_sems, rhs, a, c, x, y)
            cp.wait_send()
            cp.wait_recv()

    outs = pl.pallas_call(
        body, name=name, out_shape=[pltpu.HBM(t.shape, t.dtype) for t in thru],
        in_specs=[_HBM] * (2 * n) + [_SEM, _SEM] + [_ANY] * len(after), out_specs=[_HBM] * (2 * n),
        input_output_aliases={a: a for a in range(2 * n)},
        compiler_params=pltpu.CompilerParams(has_side_effects=_DATAFLOW),
    )(*thru, send_sems, recv_sems, *after)
    return outs[:n], outs[n:]


def _grad_share(fulls, name, small=None):
    n = len(fulls)
    ns = 0 if small is None else 1
    rhs = [f.shape[0] // 2 for f in fulls]

    def body(*refs):
        ins, outs = refs[:n], refs[n + ns:2 * n + ns]
        send_sems, recv_sems = refs[2 * (n + ns)], refs[2 * (n + ns) + 1]
        x, y, c = _pos()

        def copy(a, half):
            rows = pl.ds(pl.multiple_of(half * rhs[a], 8), rhs[a])
            return pltpu.make_async_remote_copy(src_ref=ins[a].at[rows, :], dst_ref=outs[a].at[rows, :],
                                                send_sem=send_sems.at[7 * ns + a], recv_sem=recv_sems.at[7 * ns + a],
                                                device_id=(x, y, 1 - c), device_id_type=MESH)

        sends = [copy(a, c) for a in range(n)]
        for cp in sends:
            cp.start()
        if ns:
            small_ref, all_ref = refs[n], refs[2 * n + 1]
            me = 4 * x + 2 * y + c

            def peer(r):
                dx, dy, dc = (r >> 2) & 1, (r >> 1) & 1, r & 1
                return (x if dx == 0 else 1 - x), (y if dy == 0 else 1 - y), (c if dc == 0 else 1 - c)

            def small_copy(r, slot):
                return pltpu.make_async_remote_copy(src_ref=small_ref, dst_ref=all_ref.at[slot], send_sem=send_sems.at[r - 1],
                                                    recv_sem=recv_sems.at[r - 1], device_id=peer(r), device_id_type=MESH)

            smalls = [small_copy(r, me) for r in range(1, 8)]
            for cp in smalls:
                cp.start()
            for r in range(1, 8):
                px, py, pc = peer(r)
                small_copy(r, 4 * px + 2 * py + pc).wait_recv()
            sends = sends + smalls
        for a in range(n):
            copy(a, 1 - c).wait_recv()
        for cp in sends:
            cp.wait_send()

    return pl.pallas_call(
        body, name=name, in_specs=[_HBM] * (n + ns), out_specs=[_HBM] * (n + ns),
        out_shape=[jax.ShapeDtypeStruct(f.shape, f.dtype) for f in fulls]
        + ([jax.ShapeDtypeStruct((8,) + small.shape, small.dtype)] if ns else []),
        input_output_aliases={a: a for a in range(n)},
        scratch_shapes=[pltpu.SemaphoreType.DMA((7 * ns + n,)), pltpu.SemaphoreType.DMA((7 * ns + n,))],
    )(*fulls, *([small] if ns else []))


def _add_sibling(own, recv, cq, name):
    nb, R, Cc = own.shape
    Rh = R // 2

    def body(cq_ref, a_ref, b_ref, o32_ref, o16_ref):
        s = a_ref[0] + b_ref[0]
        mine = pl.program_id(0) == cq_ref[1]

        @pl.when(mine)
        def _():
            o32_ref[...] = s

        @pl.when(jnp.logical_not(mine))
        def _():
            o16_ref[0] = s.astype(o16_ref.dtype)

    sp = pl.BlockSpec((1, Rh, Cc), lambda b, s: (b, 0, 0))
    gs = pltpu.PrefetchScalarGridSpec(
        num_scalar_prefetch=1, grid=(nb,),
        in_specs=[pl.BlockSpec((1, Rh, Cc), lambda b, s: (b, s[0], 0)), sp],
        out_specs=[pl.BlockSpec((Rh, Cc), lambda b, s: (0, 0)), sp])
    return pl.pallas_call(
        body, name=name, grid_spec=gs,
        out_shape=[jax.ShapeDtypeStruct((Rh, Cc), F32), jax.ShapeDtypeStruct((nb, Rh, Cc), _MXU)],
        compiler_params=_cparams(("arbitrary",)),
    )(cq, own, recv)


def _add_chips(part32, recv3, cq, name):
    Rh, Cc = part32.shape

    def body(cq_ref, a_ref, b_ref, o_ref):
        acc = a_ref[...]
        for j in range(3):
            acc = acc + b_ref[j].astype(F32)
        o_ref[...] = acc

    gs = pltpu.PrefetchScalarGridSpec(
        num_scalar_prefetch=1, grid=(1,),
        in_specs=[pl.BlockSpec((Rh, Cc), lambda i, s: (0, 0)), pl.BlockSpec((3, Rh, Cc), lambda i, s: (0, 0, 0))],
        out_specs=pl.BlockSpec((Rh, Cc), lambda i, s: (s[0], 0)))
    return pl.pallas_call(
        body, name=name, grid_spec=gs, out_shape=jax.ShapeDtypeStruct((2 * Rh, Cc), F32),
        compiler_params=_cparams(("arbitrary",)),
    )(cq, part32, recv3)


def _adamw(w, g, m, v, name):
    R, Cc = w.shape
    T = max([t for t in range(8, 257, 8) if R % t == 0], default=R)

    def body(w_ref, g_ref, m_ref, v_ref, d_ref, mo_ref, vo_ref):
        d_ref[...], mo_ref[...], vo_ref[...] = _adamw_math(w_ref[...], g_ref[...], m_ref[...], v_ref[...])

    sp = pl.BlockSpec((T, Cc), lambda i: (i, 0))
    sh = jax.ShapeDtypeStruct((R, Cc), F32)
    return pl.pallas_call(
        body, name=name, grid=(R // T,), in_specs=[sp] * 4, out_specs=(sp, sp, sp), out_shape=(sh, sh, sh),
        compiler_params=_cparams(("parallel",)),
    )(w, g, m, v)


SMALL_ROWS = 32
ROW_CONV, ROW_FCG, ROW_FCU = 5, 13, 22


def _adamw_math(w, g, m, v):
    mn = ADAM_B1 * m + (1.0 - ADAM_B1) * g
    vn = ADAM_B2 * v + (1.0 - ADAM_B2) * (g * g)
    c1 = 1.0 / (1.0 - ADAM_B1 ** ADAM_STEP)
    c2 = 1.0 / (1.0 - ADAM_B2 ** ADAM_STEP)
    return -ADAM_LR * ((mn * c1) / (jnp.sqrt(vn * c2) + ADAM_EPS) + ADAM_WD * w), mn, vn


def _pack_small(n1, n2, fn, gp, gn, conv, fcg, fcu, loss):
    W = D_MODEL

    def body(n1_ref, n2_ref, fn_ref, gp_ref, gn_ref, conv_ref, fcg_ref, fcu_ref, loss_ref, o_ref):
        o_ref[...] = jnp.zeros_like(o_ref)
        o_ref[0:1, :] = n1_ref[...]
        o_ref[1:2, :] = n2_ref[...]
        o_ref[2:3, :] = fn_ref[...]
        o_ref[3:4, 0:8] = gp_ref[0:1, 0:8]
        o_ref[3:4, 8:9] = loss_ref[0:1, 0:1]
        o_ref[4:5, 0:128] = gn_ref[...]
        for i in range(GDN_CONV):
            o_ref[ROW_CONV + 2 * i:ROW_CONV + 2 * i + 1, :] = conv_ref[i:i + 1, 0:W]
            o_ref[ROW_CONV + 2 * i + 1:ROW_CONV + 2 * i + 2, 0:3 * GDN_WIDTH - W] = conv_ref[i:i + 1, W:3 * GDN_WIDTH]
        for r0, ref in ((ROW_FCG, fcg_ref), (ROW_FCU, fcu_ref)):
            for i in range(FFN_CONV):
                for k in range(3):
                    n = min(W, D_FF - k * W)
                    o_ref[r0 + 3 * i + k:r0 + 3 * i + k + 1, 0:n] = ref[i:i + 1, k * W:k * W + n]

    return pl.pallas_call(body, name="pack_small", out_shape=jax.ShapeDtypeStruct((SMALL_ROWS, W), F32))(
        n1, n2, fn, gp, gn, conv, fcg, fcu, loss)


def _small_step(meq, small_all, small, ws, ms, vs):
    W = D_MODEL
    n = len(ws)
    cw, fw = ws[6].shape[1], ws[7].shape[1]

    def body(meq_ref, all_ref, own_ref, *refs):
        w_refs, m_refs, v_refs = refs[:n], refs[n:2 * n], refs[2 * n:3 * n]
        loss_ref = refs[3 * n]
        outs = refs[3 * n + 1:]
        me, q = meq_ref[0], meq_ref[1]
        red = None
        for d in range(8):
            term = jnp.where(me == d, own_ref[...], all_ref[d])
            red = term if red is None else red + term
        loss_ref[...] = jnp.broadcast_to(red[3:4, 8:9], loss_ref.shape)
        conv = [jnp.concatenate([red[ROW_CONV + 2 * i:ROW_CONV + 2 * i + 1, :],
                                 red[ROW_CONV + 2 * i + 1:ROW_CONV + 2 * i + 2, 0:3 * GDN_WIDTH - W]], axis=1)
                for i in range(GDN_CONV)]
        conv = jnp.concatenate(conv, axis=0)

        def fc_rows(r0):
            rows = [jnp.concatenate([red[r0 + 3 * i + k:r0 + 3 * i + k + 1, 0:min(W, D_FF - k * W)] for k in range(3)], axis=1)
                    for i in range(FFN_CONV)]
            return jnp.concatenate(rows, axis=0)

        fc = jnp.concatenate([fc_rows(ROW_FCG), fc_rows(ROW_FCU)], axis=1)

        def chip_block(full, width):
            out = None
            for j in range(N_CHIPS):
                term = jnp.where(q == j, full[:, width * j:width * (j + 1)], 0.0)
                out = term if out is None else out + term
            return out

        grads = [red[0:1, :], red[1:2, :], red[2:3, :], red[3:4, 0:4], red[3:4, 4:8], red[4:5, 0:128],
                 chip_block(conv, cw), chip_block(fc, fw)]
        for k in range(n):
            d_, m_, v_ = _adamw_math(w_refs[k][...], grads[k], m_refs[k][...], v_refs[k][...])
            outs[4 * k][...] = grads[k]
            outs[4 * k + 1][...] = d_
            outs[4 * k + 2][...] = m_
            outs[4 * k + 3][...] = v_

    full = lambda a: pl.BlockSpec(a.shape, lambda i, s_, nd=len(a.shape): (0,) * nd)
    arrays = [small_all, small, *ws, *ms, *vs]
    out_shapes = [jax.ShapeDtypeStruct((8, 128), F32)] + [jax.ShapeDtypeStruct(w.shape, F32) for w in ws for _ in range(4)]
    gs = pltpu.PrefetchScalarGridSpec(
        num_scalar_prefetch=1, grid=(1,), in_specs=[full(a) for a in arrays],
        out_specs=[pl.BlockSpec(o.shape, lambda i, s_, nd=len(o.shape): (0,) * nd) for o in out_shapes])
    return pl.pallas_call(body, name="small_step", grid_spec=gs, out_shape=out_shapes)(meq, *arrays)


def _pad_lanes(v, n=D_MODEL):
    return jnp.pad(v, ((0, 0), (0, n - v.shape[1])))


def kernel(x, norm1_w, w_in, conv_qkv_w, a_log, dt_bias, gdn_norm_w, w_out, norm2_w, w_up, ffn_conv_w, w_down, final_norm_w, loss_target, m_norm1_w, m_w_in, m_conv_qkv_w, m_a_log, m_dt_bias, m_gdn_norm_w, m_w_out, m_norm2_w, m_w_up, m_ffn_conv_w, m_w_down, m_final_norm_w, v_norm1_w, v_w_in, v_conv_qkv_w, v_a_log, v_dt_bias, v_gdn_norm_w, v_w_out, v_norm2_w, v_w_up, v_ffn_conv_w, v_w_down, v_final_norm_w):
    c = lax.axis_index("c")
    q = 2 * lax.axis_index("x") + lax.axis_index("y")
    S = x.shape[1]
    cq = jnp.stack([c, q]).astype(jnp.int32)

    *in_started, in_token = _gather_halves_start([w_in[0].astype(_MXU), conv_qkv_w[0], ffn_conv_w[0]], x, "gather_in_start")
    w_in_l, m_w_in_l, v_w_in_l = (a + in_token[0:1, 0:1] for a in (w_in, m_w_in, v_w_in))
    h1 = _rmsnorm_fwd(x[0], norm1_w, "norm1", after=[in_token])
    rest = [(a[0] + in_token[0:1, 0:1]).astype(_MXU) for a in (w_out, w_up, w_down)]
    in_shards, got_in = _gather_halves_wait(in_started, [w_in_l, m_w_in_l, v_w_in_l, h1, *rest], "gather_in_wait")
    g_in, g_conv, g_fconv = _place_own(in_shards, _sibling_fill(got_in, "fill_in"), cq, "place_in")
    *rest_started, token = _gather_halves_start(rest, g_conv, "gather_rest_start")

    rest_state = {}

    def rest_arrived(after):
        rest_state["shards"], got = _gather_halves_wait(rest_started, after, "gather_rest_wait")
        *rest_state["fill"], tok = _sibling_fill_start(got, "fill_rest_start")
        return tok

    def rest_filled(after):
        got = _sibling_fill_wait(rest_state["fill"], after, "fill_rest_wait")
        g_out, g_up, g_down = _place_own(rest_state["shards"], got, cq, "place_rest")
        return g_out.reshape(D_MODEL, D_MODEL), g_up, g_down.reshape(D_FF, D_MODEL)

    rest_weights = (rest_arrived, rest_filled)
    wp = _wp_assemble(g_in, [token])
    conv_f = jnp.concatenate([g_conv[i] for i in range(N_CHIPS)], axis=1)
    fcw = jnp.concatenate([g_fconv[i] for i in range(N_CHIPS)], axis=1)
    gp = _pad_lanes(jnp.concatenate([a_log, dt_bias], axis=1), 128)
    fnw = final_norm_w[None, :]
    early = {}

    def early_sibling(d_wup, d_wdown):
        *early["sibling"], tok = _grad_sibling_start([d_wup, d_wdown.reshape(N_CHIPS, D_FF // N_CHIPS, D_MODEL)],
                                                     "grad_sibling_early_start")
        return tok

    def early_chips(dx2):
        fams_e, got_e = _grad_sibling_wait(early["sibling"], [dx2], "grad_sibling_early_wait")
        early["parts"] = [_add_sibling(f, r, cq, "add_sibling_" + nm) for f, r, nm in zip(fams_e, got_e, ("w_up", "w_down"))]
        *early["started"], tok = _grad_chips_start([p[1] for p in early["parts"]], "grad_chips_start")
        return tok

    def late_sibling(d_wp, d_wout):
        *early["late_sibling"], tok = _grad_sibling_start(
            [_win_split(d_wp), d_wout.reshape(N_CHIPS, D_MODEL // N_CHIPS, D_MODEL)], "grad_sibling_late_start")
        return tok

    early_grads = (early_sibling, early_chips, late_sibling)

    loss_l, dx, g = _local_step(x[0], loss_target[0], h1, norm1_w, norm2_w, fnw, gp, gdn_norm_w, wp,
                                conv_f, fcw, rest_weights, early_grads)
    small = _pack_small(g["n1w"], g["n2w"], g["fnw"], g["gp"], g["gnw"], g["conv_w"], g["fcw_g"], g["fcw_u"], loss_l)
    fams, got = _grad_sibling_wait(early["late_sibling"], [dx], "grad_sibling_late_wait")
    parts = [_add_sibling(f, r, cq, "add_sibling_" + nm) for f, r, nm in zip(fams, got, ("w_in", "w_out"))]
    *late_started, late_token = _grad_chips_start([p[1] for p in parts], "grad_chips_late_start")
    got3_e = _grad_chips_wait(early["started"], [dx, g["wp"], late_token], "grad_chips_wait")
    g_w_up, g_w_down = _grad_share(
        [_add_chips(p[0], r3, cq, "add_chips_" + nm) for p, r3, nm in zip(early["parts"], got3_e, ("w_up", "w_down"))],
        "grad_share_early")
    big = {}

    def adamw_big(nm, w, gg, m, v):
        d_, m_, v_ = _adamw(w[0], gg, m[0], v[0], "adamw_" + nm)
        big[nm] = (gg[None], d_[None], m_[None], v_[None])

    adamw_big("w_up", w_up, g_w_up, m_w_up, v_w_up)
    adamw_big("w_down", w_down, g_w_down, m_w_down, v_w_down)
    got3 = _grad_chips_wait(late_started, [big["w_up"][1], big["w_down"][1]], "grad_chips_late_wait")
    g_w_in, g_w_out, small_all = _grad_share(
        [_add_chips(p[0], r3, cq, "add_chips_" + nm) for p, r3, nm in zip(parts, got3, ("w_in", "w_out"))],
        "grad_share_late", small)
    adamw_big("w_in", w_in_l, g_w_in, m_w_in_l, v_w_in_l)
    adamw_big("w_out", w_out, g_w_out, m_w_out, v_w_out)
    small_names = ["norm1_w", "norm2_w", "final_norm_w", "a_log", "dt_bias", "gdn_norm_w", "conv_qkv_w", "ffn_conv_w"]
    loss_b, *small_out = _small_step(
        jnp.stack([2 * q + c, q]).astype(jnp.int32), small_all, small,
        [norm1_w, norm2_w, final_norm_w[None], a_log, dt_bias, gdn_norm_w, conv_qkv_w[0], ffn_conv_w[0]],
        [m_norm1_w, m_norm2_w, m_final_norm_w[None], m_a_log, m_dt_bias, m_gdn_norm_w, m_conv_qkv_w[0], m_ffn_conv_w[0]],
        [v_norm1_w, v_norm2_w, v_final_norm_w[None], v_a_log, v_dt_bias, v_gdn_norm_w, v_conv_qkv_w[0], v_ffn_conv_w[0]])
    like = dict(final_norm_w=lambda t: t[0], conv_qkv_w=lambda t: t[None], ffn_conv_w=lambda t: t[None])
    for k, nm in enumerate(small_names):
        big[nm] = tuple(like.get(nm, lambda t: t)(t) for t in small_out[4 * k:4 * k + 4])
    names = ["norm1_w", "w_in", "conv_qkv_w", "a_log", "dt_bias", "gdn_norm_w", "w_out", "norm2_w", "w_up",
             "ffn_conv_w", "w_down", "final_norm_w"]
    return (loss_b[0, 0], dx[None], *[big[n][0] for n in names], *[big[n][1] for n in names],
            *[big[n][2] for n in names], *[big[n][3] for n in names])
```

```python
import functools
import math

import numpy as np
import jax
import jax.numpy as jnp
from jax import lax
from jax.experimental import pallas as pl
from jax.experimental.pallas import tpu as pltpu

F32 = jnp.float32
BF16 = jnp.bfloat16
_MXU = jnp.bfloat16
_HI = lax.Precision.HIGHEST
EPS = 1e-6
V7X_VMEM_LIMIT = 56 * 1024 * 1024
MESH = pl.DeviceIdType.MESH

D_MODEL = 1024
GDN_HEADS, GDN_DIM, GDN_CHUNK, GDN_CONV = 4, 128, 64, 4
GDN_WIDTH = GDN_HEADS * GDN_DIM
DIL_HEADS, DIL_DIM = 8, 64
DIL_WIDTH = DIL_HEADS * DIL_DIM
D_FF, FFN_CONV = 2816, 3
IN_COLS = 3592
P_COLS = 3840
P_Z, P_QKVB, P_BA = 1536, 2048, 3584
ATT_T = 1024
ADAM_LR, ADAM_B1, ADAM_B2, ADAM_EPS, ADAM_WD, ADAM_STEP = 0.001, 0.9, 0.999, 1e-08, 0.01, 10
N_CHIPS = 4


def _cparams(sem=None, vmem=None):
    kw = {}
    if sem is not None:
        kw["dimension_semantics"] = sem
    if vmem is not None:
        kw["vmem_limit_bytes"] = vmem
    return pltpu.CompilerParams(**kw)


def _silu(x):
    return x * jax.nn.sigmoid(x)


def _pick_tile(n, cap):
    best = None
    for t in range(128, min(n, cap) + 1, 128):
        if n % t == 0:
            best = t
    return best or n


def _mm(a, b, mode, *, out_dtype=F32, residual=None, name, b_blocks=False, place=None, into=None, tn=None, after=()):
    if mode == "nn":
        M, K = a.shape
        N = b.shape[0] * b.shape[2] if b_blocks else b.shape[1]
    elif mode == "nt":
        (M, K), (N, _) = a.shape, b.shape
    else:
        (K, M), (_, N) = a.shape, b.shape
    tm = _pick_tile(M, 1024)
    tn = b.shape[2] if b_blocks else (tn or _pick_tile(N, 1536))

    def vmem(tm, tn):
        return 2 * (tm * K * a.dtype.itemsize + tn * K * b.dtype.itemsize
                    + tm * tn * (jnp.dtype(out_dtype).itemsize + (4 if residual is not None else 0))) + 3 * tm * tn * 4

    fixed_tn = b_blocks or (place is not None and place[0] == "blocks")
    while vmem(tm, tn) > 40 * 1024 * 1024:
        if (tm >= tn or fixed_tn) and tm % 256 == 0:
            tm //= 2
        elif tn % 256 == 0 and not fixed_tn:
            tn //= 2
        else:
            tm //= 2
    a_spec = pl.BlockSpec((K, tm), lambda j, i: (0, i)) if mode == "tn" else pl.BlockSpec((tm, K), lambda j, i: (i, 0))
    if b_blocks:
        b_spec = pl.BlockSpec((None, K, tn), lambda j, i: (j, 0, 0))
    else:
        b_spec = pl.BlockSpec((tn, K), lambda j, i: (j, 0)) if mode == "nt" else pl.BlockSpec((K, tn), lambda j, i: (0, j))
    r_spec = pl.BlockSpec((tm, tn), lambda j, i: (i, j))
    if place is None:
        o_spec, o_shape = r_spec, (M, N)
    elif place[0] == "rows":
        off = place[2] // tm
        o_spec, o_shape = pl.BlockSpec((tm, tn), lambda j, i: (i + off, j)), (place[1], N)
    else:
        off = place[2]
        o_spec, o_shape = pl.BlockSpec((None, tm, tn), lambda j, i: (j + off, i, 0)), (place[1], M, tn)
    dims = {"nn": (((1,), (0,)), ((), ())), "nt": (((1,), (1,)), ((), ())), "tn": (((0,), (0,)), ((), ()))}[mode]

    def body(*refs):
        a_ref, b_ref = refs[0], refs[1]
        o_ref = refs[-1]
        acc = lax.dot_general(a_ref[...].astype(_MXU), b_ref[...].astype(_MXU), dims, preferred_element_type=F32)
        if residual is not None:
            acc = acc + refs[2][...]
        o_ref[...] = acc.astype(out_dtype)

    ins, specs, alias = [a, b], [a_spec, b_spec], {}
    if residual is not None:
        ins.append(residual)
        specs.append(r_spec)
    if into is not None:
        alias = {len(ins): 0}
        ins.append(into)
        specs.append(pl.BlockSpec(memory_space=pl.ANY))
    ins += list(after)
    specs += [pl.BlockSpec(memory_space=pl.ANY)] * len(after)
    return pl.pallas_call(
        body, name=name, grid=(N // tn, M // tm), in_specs=specs, out_specs=o_spec,
        out_shape=jax.ShapeDtypeStruct(o_shape, out_dtype), input_output_aliases=alias,
        compiler_params=_cparams(("parallel", "parallel"), V7X_VMEM_LIMIT),
    )(*ins)


def _mm_nt_blocks(a_list, b4, name, after=()):
    M = a_list[0].shape[0]
    nb, N, Kb = b4.shape
    tm, tn = _pick_tile(M, 1024), _pick_tile(N, 512)

    def body(a0_ref, a1_ref, b_ref, *rest):
        o_ref = rest[-1]
        acc = None
        for blk in range(nb):
            a_ref = (a0_ref, a1_ref)[blk // 2]
            lo = (blk % 2) * Kb
            t = lax.dot_general(a_ref[:, lo:lo + Kb].astype(_MXU), b_ref[blk].astype(_MXU), (((1,), (1,)), ((), ())),
                                preferred_element_type=F32)
            acc = t if acc is None else acc + t
        o_ref[...] = acc

    a_spec = pl.BlockSpec((tm, 2 * Kb), lambda j, i: (i, 0))
    return pl.pallas_call(
        body, name=name, grid=(N // tn, M // tm),
        in_specs=[a_spec, a_spec, pl.BlockSpec((nb, tn, Kb), lambda j, i: (0, j, 0))]
        + [pl.BlockSpec(memory_space=pl.ANY)] * len(after),
        out_specs=pl.BlockSpec((tm, tn), lambda j, i: (i, j)), out_shape=jax.ShapeDtypeStruct((M, N), F32),
        compiler_params=_cparams(("parallel", "parallel"), V7X_VMEM_LIMIT),
    )(a_list[0], a_list[1], b4, *after)


def _wp_assemble(g_in, after=()):
    nb, Dm, Wb = g_in.shape
    T = 256
    n_lo = P_QKVB - 2 * Wb

    def body(g_ref, *rest):
        g2 = g_ref[2]
        rest[-1][...] = jnp.concatenate(
            [g_ref[0], g_ref[1], g2[:, :n_lo], g2[:, n_lo + 8:], g_ref[3], g2[:, n_lo:n_lo + 8],
             jnp.zeros((T, P_COLS - P_BA - 8), g_in.dtype)], axis=1)

    return pl.pallas_call(
        body, name="wp_assemble", grid=(Dm // T,),
        in_specs=[pl.BlockSpec((nb, T, Wb), lambda i: (0, i, 0))] + [pl.BlockSpec(memory_space=pl.ANY)] * len(after),
        out_specs=pl.BlockSpec((T, P_COLS), lambda i: (i, 0)), out_shape=jax.ShapeDtypeStruct((Dm, P_COLS), g_in.dtype),
        compiler_params=_cparams(("parallel",)),
    )(g_in, *after)


def _win_split(d_wp):
    Dm = d_wp.shape[0]
    Wb = IN_COLS // N_CHIPS
    T = 256

    def body(x_ref, o_ref):
        xv = x_ref[...]
        o_ref[0] = xv[:, 0:Wb]
        o_ref[1] = xv[:, Wb:2 * Wb]
        o_ref[2] = jnp.concatenate([xv[:, 2 * Wb:P_QKVB], xv[:, P_BA:P_BA + 8], xv[:, P_QKVB:3 * Wb - 8]], axis=1)
        o_ref[3] = xv[:, 3 * Wb - 8:P_BA]

    return pl.pallas_call(
        body, name="win_split", grid=(Dm // T,), in_specs=[pl.BlockSpec((T, P_COLS), lambda i: (i, 0))],
        out_specs=pl.BlockSpec((N_CHIPS, T, Wb), lambda i: (0, i, 0)),
        out_shape=jax.ShapeDtypeStruct((N_CHIPS, Dm, Wb), F32), compiler_params=_cparams(("parallel",)),
    )(d_wp)


def _rmsnorm_fwd(x, w, name, after=()):
    S, D = x.shape
    T = _pick_tile(S, 512)

    def body(x_ref, w_ref, *rest):
        xv = x_ref[...]
        rs = lax.rsqrt(jnp.mean(xv * xv, axis=-1, keepdims=True) + EPS)
        rest[-1][...] = (xv * rs * w_ref[...]).astype(rest[-1].dtype)

    return pl.pallas_call(
        body, name=name, grid=(S // T,),
        in_specs=[pl.BlockSpec((T, D), lambda i: (i, 0)), pl.BlockSpec((1, D), lambda i: (0, 0))] + [_ANY] * len(after),
        out_specs=pl.BlockSpec((T, D), lambda i: (i, 0)),
        out_shape=jax.ShapeDtypeStruct((S, D), _MXU),
        compiler_params=_cparams(("parallel",)),
    )(x, w, *after)


def _rmsnorm_bwd(dh, x, w, dres, name, after=()):
    S, D = x.shape
    T = _pick_tile(S, 512)

    def body(dh_ref, x_ref, w_ref, dres_ref, *rest):
        dx_ref, dw_ref = rest[-2:]
        xv = x_ref[...]
        rs = lax.rsqrt(jnp.mean(xv * xv, axis=-1, keepdims=True) + EPS)
        xn = xv * rs
        dhv = dh_ref[...]
        dxn = dhv * w_ref[...]
        dx_ref[...] = dres_ref[...] + rs * (dxn - xn * jnp.mean(dxn * xn, axis=-1, keepdims=True))

        @pl.when(pl.program_id(0) == 0)
        def _():
            dw_ref[...] = jnp.zeros_like(dw_ref)

        dw_ref[...] += jnp.sum(dhv * xn, axis=0, keepdims=True)

    row = pl.BlockSpec((T, D), lambda i: (i, 0))
    vec = pl.BlockSpec((1, D), lambda i: (0, 0))
    return pl.pallas_call(
        body, name=name, grid=(S // T,), in_specs=[row, row, vec, row] + [_ANY] * len(after), out_specs=(row, vec),
        out_shape=(jax.ShapeDtypeStruct((S, D), F32), jax.ShapeDtypeStruct((1, D), F32)),
        compiler_params=_cparams(("arbitrary",)),
    )(dh, x, w, dres, *after)


def _loss_head(x3, w, tgt, name):
    S, D = x3.shape
    T = _pick_tile(S, 512)

    def body(x_ref, w_ref, t_ref, loss_ref, dx_ref, dxn_ref, dw_ref):
        xv = x_ref[...]
        rs = lax.rsqrt(jnp.mean(xv * xv, axis=-1, keepdims=True) + EPS)
        xn = xv * rs
        err = xn * w_ref[...] - t_ref[...]
        dy = err * (1.0 / D)
        dxn = dy * w_ref[...]
        dxv = rs * (dxn - xn * jnp.mean(dxn * xn, axis=-1, keepdims=True))
        dx_ref[...] = dxv
        dxn_ref[...] = dxv.astype(dxn_ref.dtype)

        @pl.when(pl.program_id(0) == 0)
        def _():
            dw_ref[...] = jnp.zeros_like(dw_ref)
            loss_ref[...] = jnp.zeros_like(loss_ref)

        dw_ref[...] += jnp.sum(dy * xn, axis=0, keepdims=True)
        part = jnp.sum(jnp.sum(err * err, axis=-1, keepdims=True), axis=0, keepdims=True) * (0.5 / D)
        loss_ref[...] += jnp.broadcast_to(part, loss_ref.shape)

    row = pl.BlockSpec((T, D), lambda i: (i, 0))
    vec = pl.BlockSpec((1, D), lambda i: (0, 0))
    return pl.pallas_call(
        body, name=name, grid=(S // T,), in_specs=[row, vec, row],
        out_specs=(pl.BlockSpec((8, 128), lambda i: (0, 0)), row, row, vec),
        out_shape=(jax.ShapeDtypeStruct((8, 128), F32), jax.ShapeDtypeStruct((S, D), F32), jax.ShapeDtypeStruct((S, D), _MXU),
                   jax.ShapeDtypeStruct((1, D), F32)),
        compiler_params=_cparams(("arbitrary",)),
    )(x3, w, tgt)


def _shifted(ext, back, lo, n):
    if back == 0:
        return ext[lo:lo + n, :]
    return pltpu.roll(ext, back % ext.shape[0], 0)[lo:lo + n, :]


def _conv_windows(ext, K, T):
    return [_shifted(ext, (K - 1) - i, 8, T) for i in range(K)]


def _conv_taps(ext, w, K, T):
    out = None
    for i, win in enumerate(_conv_windows(ext, K, T)):
        term = win * w[i:i + 1, :]
        out = term if out is None else out + term
    return out


def _conv_taps_t(ext, w, K, T):
    out = None
    for i in range(K):
        term = _shifted(ext, i - (K - 1), 0, T) * w[i:i + 1, :]
        out = term if out is None else out + term
    return out


def _tri_masks(C):
    r = lax.broadcasted_iota(jnp.int32, (C, C), 0)
    c = lax.broadcasted_iota(jnp.int32, (C, C), 1)
    return r == c, r >= c, r > c, r <= c


_NN, _NT, _TN = ((1,), (0,)), ((1,), (1,)), ((0,), (0,))
_GDN_PASSES = dict(qk=1, inv=1, sol=1, scan=1, bwd=1)


def _bdot_raw(a, b, kind, passes):
    dims = ({"NN": ((2,), (1,)), "NT": ((2,), (2,)), "TN": ((1,), (1,))}[kind], ((0,), (0,)))
    if passes == 0:
        return lax.dot_general(a, b, dims, precision=_HI, preferred_element_type=F32)
    ah, bh = a.astype(BF16), b.astype(BF16)
    out = lax.dot_general(ah, bh, dims, preferred_element_type=F32)
    if passes == 3:
        al, bl = (a - ah.astype(F32)).astype(BF16), (b - bh.astype(F32)).astype(BF16)
        out = out + lax.dot_general(ah, bl, dims, preferred_element_type=F32) + lax.dot_general(al, bh, dims, preferred_element_type=F32)
    return out


@functools.partial(jax.custom_vjp, nondiff_argnums=(2, 3))
def _bdot(a, b, kind, passes):
    return _bdot_raw(a, b, kind, passes)


def _bdot_fwd(a, b, kind, passes):
    return _bdot_raw(a, b, kind, passes), (a, b)


def _bdot_bwd(kind, passes, res, ct):
    a, b = res
    if kind == "NN":
        return _bdot_raw(ct, b, "NT", passes), _bdot_raw(a, ct, "TN", passes)
    if kind == "NT":
        return _bdot_raw(ct, b, "NN", passes), _bdot_raw(ct, a, "TN", passes)
    return _bdot_raw(b, ct, "NT", passes), _bdot_raw(a, ct, "NN", passes)


_bdot.defvjp(_bdot_fwd, _bdot_bwd)


def _softplus(x):
    return jnp.maximum(x, 0.0) + jnp.log(1.0 + jnp.exp(-jnp.abs(x)))


def _gdn_stage1(cq, ck, cv, b_col, a_col, alog, dtb, dot=_bdot_raw):
    C = cq.shape[1]
    eye, incl, strict, incl_t = _tri_masks(C)
    qn = cq * lax.rsqrt(jnp.sum(cq * cq, axis=-1, keepdims=True) + EPS) * (GDN_DIM ** -0.5)
    kn = ck * lax.rsqrt(jnp.sum(ck * ck, axis=-1, keepdims=True) + EPS)
    beta = jax.nn.sigmoid(b_col)
    g = -jnp.exp(alog) * _softplus(a_col + dtb)
    g_row = jnp.sum(jnp.where(eye, g, 0.0), axis=1, keepdims=True)
    beta_row = jnp.sum(jnp.where(eye, beta, 0.0), axis=1, keepdims=True)
    gc_col = jnp.sum(jnp.where(incl, g_row, 0.0), axis=2, keepdims=True)
    gc_row = jnp.sum(jnp.where(incl_t, g, 0.0), axis=1, keepdims=True)
    dec = jnp.where(incl, jnp.exp(jnp.where(incl, gc_col - gc_row, 0.0)), 0.0)
    kk = dot(kn, kn, "NT", _GDN_PASSES["qk"])
    qk = dot(qn, kn, "NT", _GDN_PASSES["qk"])
    lmat = jnp.where(strict, dec * kk * beta_row, 0.0)
    attn = dec * qk * beta_row
    gam = jnp.exp(gc_col)
    gc_last = gc_col[:, C - 1:C, :]
    k_end = kn * (jnp.exp(gc_last - gc_col) * beta)
    return lmat, cv, gam * kn, gam * qn, attn, k_end, jnp.exp(gc_last)


def _tri_inv(lmat):
    C = lmat.shape[1]
    eye = _tri_masks(C)[0]
    ps = _GDN_PASSES["inv"]
    p = jnp.where(eye, 1.0, 0.0) - lmat
    lp = _bdot_raw(lmat, lmat, "NN", ps)
    n = int(math.log2(C))
    for s in range(1, n):
        p = p + _bdot_raw(p, lp, "NN", ps)
        if s < n - 1:
            lp = _bdot_raw(lp, lp, "NN", ps)
    return p


def _gated_norm(o, z, gnw):
    on = o * lax.rsqrt(jnp.mean(o * o, axis=-1, keepdims=True) + EPS) * gnw
    return on * _silu(z)


GDN_PG = 4
GDN_SG = 4


def _gdn_pairs(c, ba, gp, G):
    C, W, H = GDN_CHUNK, GDN_WIDTH, GDN_HEADS
    pairs = [(j, h) for j in range(G) for h in range(H)]
    cq, ck, cv = (jnp.stack([c[C * j:C * (j + 1), o + GDN_DIM * h:o + GDN_DIM * (h + 1)] for j, h in pairs]) for o in (0, W, 2 * W))
    b_col = jnp.stack([ba[C * j:C * (j + 1), h:h + 1] for j, h in pairs])
    a_col = jnp.stack([ba[C * j:C * (j + 1), H + h:H + h + 1] for j, h in pairs])
    alog = jnp.stack([gp[0:1, h:h + 1] for j, h in pairs])
    dtb = jnp.stack([gp[0:1, H + h:H + h + 1] for j, h in pairs])
    return pairs, (cq, ck, cv, b_col, a_col, alog, dtb)


def _gdn_pre_specs(S, G):
    C = GDN_CHUNK
    T = C * G
    return dict(
        cur=pl.BlockSpec((T, 3 * GDN_WIDTH), lambda i: (i, 0)),
        prev=pl.BlockSpec((8, 3 * GDN_WIDTH), lambda i: (jnp.maximum(i * (T // 8) - 1, 0), 0)),
        ba=pl.BlockSpec((T, 128), lambda i: (i, P_BA // 128)),
        cw=pl.BlockSpec((GDN_CONV, 3 * GDN_WIDTH), lambda i: (0, 0)),
        vec=pl.BlockSpec((1, 128), lambda i: (0, 0)),
        hd=pl.BlockSpec((GDN_HEADS, T, GDN_DIM), lambda i: (0, i, 0)),
        hc=pl.BlockSpec((GDN_HEADS, T, C), lambda i: (0, i, 0)),
        ge=pl.BlockSpec((G, GDN_HEADS, 8, 128), lambda i: (i, 0, 0, 0)),
    )


def _hd_shape(S, last=GDN_DIM):
    return jax.ShapeDtypeStruct((GDN_HEADS, S, last), F32)


def _gdn_pre(proj, conv_w, gp):
    S = proj.shape[0]
    C, G = GDN_CHUNK, GDN_PG
    nc = S // C
    sp = _gdn_pre_specs(S, G)

    def body(cur_ref, prev_ref, ba_ref, cw_ref, gp_ref, uv_ref, wk_ref, qd_ref, ke_ref, at_ref, ti_ref, ge_ref):
        prev = prev_ref[...] * jnp.where(pl.program_id(0) == 0, 0.0, 1.0)
        c = _silu(_conv_taps(jnp.concatenate([prev, cur_ref[...]], axis=0), cw_ref[...], GDN_CONV, C * G))
        pairs, args = _gdn_pairs(c, ba_ref[...], gp_ref[...], G)
        lmat, v, rk, q_dec, attn, k_end, g_end = _gdn_stage1(*args)
        t = _tri_inv(lmat)
        u_v = _bdot_raw(t, v, "NN", _GDN_PASSES["sol"])
        w_k = _bdot_raw(t, rk, "NN", _GDN_PASSES["sol"])
        for b, (j, h) in enumerate(pairs):
            rows = slice(C * j, C * (j + 1))
            uv_ref[h, rows, :] = u_v[b]
            wk_ref[h, rows, :] = w_k[b]
            qd_ref[h, rows, :] = q_dec[b]
            ke_ref[h, rows, :] = k_end[b]
            at_ref[h, rows, :] = attn[b]
            ti_ref[h, rows, :] = t[b]
            ge_ref[j, h] = jnp.broadcast_to(g_end[b], (8, 128))

    return pl.pallas_call(
        body, name="gdn_pre", grid=(nc // G,),
        in_specs=[sp["cur"], sp["prev"], sp["ba"], sp["cw"], sp["vec"]],
        out_specs=(sp["hd"], sp["hd"], sp["hd"], sp["hd"], sp["hc"], sp["hc"], sp["ge"]),
        out_shape=(_hd_shape(S), _hd_shape(S), _hd_shape(S), _hd_shape(S), _hd_shape(S, C), _hd_shape(S, C),
                   jax.ShapeDtypeStruct((nc, GDN_HEADS, 8, 128), F32)),
        compiler_params=_cparams(("parallel",)),
    )(proj, proj, proj, conv_w, gp)


def _gdn_scan_specs(S, G, rev):
    C = GDN_CHUNK
    T = C * G
    n = S // T
    ci = (lambda i: n - 1 - i) if rev else (lambda i: i)
    return dict(
        hd=pl.BlockSpec((GDN_HEADS, T, GDN_DIM), lambda i: (0, ci(i), 0)),
        hc=pl.BlockSpec((GDN_HEADS, T, C), lambda i: (0, ci(i), 0)),
        ge=pl.BlockSpec((G, GDN_HEADS, 8, 128), lambda i: (ci(i), 0, 0, 0)),
        z=pl.BlockSpec((T, GDN_WIDTH), lambda i: (ci(i), P_Z // GDN_WIDTH)),
        oa=pl.BlockSpec((T, GDN_WIDTH), lambda i: (ci(i), 0)),
        vec=pl.BlockSpec((1, 128), lambda i: (0, 0)),
        st=pl.BlockSpec((G, GDN_HEADS, GDN_DIM, GDN_DIM), lambda i: (ci(i), 0, 0, 0)),
    )


def _gdn_scan(u_v, w_k, q_dec, k_end, attn, g_end, proj, gnw, mix, after=()):
    S = proj.shape[0]
    C, G = GDN_CHUNK, GDN_SG
    nc = S // C
    sp = _gdn_scan_specs(S, G, False)
    ps = _GDN_PASSES["scan"]

    def body(uv_ref, wk_ref, qd_ref, ke_ref, at_ref, ge_ref, z_ref, gnw_ref, *rest):
        oa_ref, st_ref, s_scr = rest[-3:]

        @pl.when(pl.program_id(0) == 0)
        def _():
            s_scr[...] = jnp.zeros_like(s_scr)

        for j in range(G):
            rows = slice(C * j, C * (j + 1))
            st = s_scr[...]
            st_ref[j] = st
            u = uv_ref[:, rows, :] - _bdot_raw(wk_ref[:, rows, :], st, "NN", ps)
            o = _bdot_raw(qd_ref[:, rows, :], st, "NN", ps) + _bdot_raw(at_ref[:, rows, :], u, "NN", ps)
            s_scr[...] = ge_ref[j][:, 0:1, 0:1] * st + _bdot_raw(ke_ref[:, rows, :], u, "TN", ps)
            for h in range(GDN_HEADS):
                cols = slice(GDN_DIM * h, GDN_DIM * (h + 1))
                oa_ref[rows, cols] = _gated_norm(o[h], z_ref[rows, cols], gnw_ref[...])

    return pl.pallas_call(
        body, name="gdn_scan", grid=(nc // G,),
        in_specs=[sp["hd"], sp["hd"], sp["hd"], sp["hd"], sp["hc"], sp["ge"], sp["z"], sp["vec"]] + [_ANY] * (1 + len(after)),
        out_specs=(sp["oa"], sp["st"]),
        out_shape=(jax.ShapeDtypeStruct(mix.shape, F32),
                   jax.ShapeDtypeStruct((nc, GDN_HEADS, GDN_DIM, GDN_DIM), F32)),
        input_output_aliases={8: 0},
        scratch_shapes=[pltpu.VMEM((GDN_HEADS, GDN_DIM, GDN_DIM), F32)],
        compiler_params=_cparams(("arbitrary",)),
    )(u_v, w_k, q_dec, k_end, attn, g_end, proj, gnw, mix, *after)


def _gdn_scan_bwd(u_v, w_k, q_dec, k_end, attn, g_end, proj, gnw, states, d_oa):
    S = proj.shape[0]
    C, G = GDN_CHUNK, GDN_SG
    nc = S // C
    sp = _gdn_scan_specs(S, G, True)
    ps, pb = _GDN_PASSES["scan"], _GDN_PASSES["bwd"]

    def body(uv_ref, wk_ref, qd_ref, ke_ref, at_ref, ge_ref, z_ref, gnw_ref, st_ref, doa_ref,
             duv_ref, dwk_ref, dqd_ref, dke_ref, dat_ref, dge_ref, dz_ref, dgnw_ref, ds_scr):
        @pl.when(pl.program_id(0) == 0)
        def _():
            ds_scr[...] = jnp.zeros_like(ds_scr)
            dgnw_ref[...] = jnp.zeros_like(dgnw_ref)

        dgnw = jnp.zeros((1, 128), F32)
        for j in reversed(range(G)):
            rows = slice(C * j, C * (j + 1))
            st = st_ref[j]
            wk, qd, ke, at = wk_ref[:, rows, :], qd_ref[:, rows, :], ke_ref[:, rows, :], at_ref[:, rows, :]
            u = uv_ref[:, rows, :] - _bdot_raw(wk, st, "NN", ps)
            o = _bdot_raw(qd, st, "NN", ps) + _bdot_raw(at, u, "NN", ps)
            dos = []
            for h in range(GDN_HEADS):
                cols = slice(GDN_DIM * h, GDN_DIM * (h + 1))
                _, vjp2 = jax.vjp(_gated_norm, o[h], z_ref[rows, cols], gnw_ref[...])
                do_h, dz_h, dgn = vjp2(doa_ref[rows, cols])
                dz_ref[rows, cols] = dz_h
                dgnw = dgnw + dgn
                dos.append(do_h)
            do = jnp.stack(dos)
            ds_new = ds_scr[...]
            du = _bdot_raw(at, do, "TN", pb) + _bdot_raw(ke, ds_new, "NN", pb)
            duv_ref[:, rows, :] = du
            dat_ref[:, rows, :] = _bdot_raw(do, u, "NT", pb)
            dqd_ref[:, rows, :] = _bdot_raw(do, st, "NT", pb)
            dke_ref[:, rows, :] = _bdot_raw(u, ds_new, "NT", pb)
            dwk_ref[:, rows, :] = -_bdot_raw(du, st, "NT", pb)
            d_ge = jnp.sum(jnp.sum(st * ds_new, axis=2, keepdims=True), axis=1, keepdims=True)
            dge_ref[j] = jnp.broadcast_to(d_ge, (GDN_HEADS, 8, 128))
            ds_scr[...] = ge_ref[j][:, 0:1, 0:1] * ds_new + _bdot_raw(qd, do, "TN", pb) - _bdot_raw(wk, du, "TN", pb)
        dgnw_ref[...] += dgnw

    return pl.pallas_call(
        body, name="gdn_scan_bwd", grid=(nc // G,),
        in_specs=[sp["hd"], sp["hd"], sp["hd"], sp["hd"], sp["hc"], sp["ge"], sp["z"], sp["vec"], sp["st"], sp["oa"]],
        out_specs=(sp["hd"], sp["hd"], sp["hd"], sp["hd"], sp["hc"], sp["ge"], sp["oa"], sp["vec"]),
        out_shape=(_hd_shape(S), _hd_shape(S), _hd_shape(S), _hd_shape(S), _hd_shape(S, C),
                   jax.ShapeDtypeStruct((nc, GDN_HEADS, 8, 128), F32), jax.ShapeDtypeStruct((S, GDN_WIDTH), F32),
                   jax.ShapeDtypeStruct((1, 128), F32)),
        scratch_shapes=[pltpu.VMEM((GDN_HEADS, GDN_DIM, GDN_DIM), F32)],
        compiler_params=_cparams(("arbitrary",)),
    )(u_v, w_k, q_dec, k_end, attn, g_end, proj, gnw, states, d_oa)


def _gdn_post(proj, conv_w, gp, tinv, u_v, w_k, d_uv, d_wk, d_qd, d_ke, d_at, d_ge):
    S = proj.shape[0]
    C, G = GDN_CHUNK, GDN_PG
    nc = S // C
    sp = _gdn_pre_specs(S, G)
    pb = _GDN_PASSES["bwd"]

    def body(cur_ref, prev_ref, ba_ref, cw_ref, gp_ref, ti_ref, uv_ref, wk_ref, duv_ref, dwk_ref, dqd_ref, dke_ref,
             dat_ref, dge_ref, dpre_ref, dba_ref, dgp_ref):
        i = pl.program_id(0)

        @pl.when(i == 0)
        def _():
            dgp_ref[...] = jnp.zeros_like(dgp_ref)

        prev = prev_ref[...] * jnp.where(i == 0, 0.0, 1.0)
        pre = _conv_taps(jnp.concatenate([prev, cur_ref[...]], axis=0), cw_ref[...], GDN_CONV, C * G)
        sg = jax.nn.sigmoid(pre)
        dsilu = sg * (1.0 + pre * (1.0 - sg))
        pairs, args = _gdn_pairs(pre * sg, ba_ref[...], gp_ref[...], G)
        _, vjp1 = jax.vjp(functools.partial(_gdn_stage1, dot=_bdot), *args)

        def take(ref):
            return jnp.stack([ref[h, C * j:C * (j + 1), :] for j, h in pairs])

        t, u_v, w_k = take(ti_ref), take(uv_ref), take(wk_ref)
        d_v = _bdot_raw(t, take(duv_ref), "TN", pb)
        d_rk = _bdot_raw(t, take(dwk_ref), "TN", pb)
        d_l = -(_bdot_raw(d_v, u_v, "NT", pb) + _bdot_raw(d_rk, w_k, "NT", pb))
        d_ge = jnp.stack([dge_ref[j, h][0:1, 0:1] for j, h in pairs])
        dcq, dck, dcv, db, da, dalog, ddtb = vjp1((d_l, d_v, d_rk, take(dqd_ref), take(dat_ref), take(dke_ref), d_ge))
        lane = lax.broadcasted_iota(jnp.int32, (C, 128), 1)
        lane1 = lax.broadcasted_iota(jnp.int32, (1, 128), 1)
        dgp = jnp.zeros((1, 128), F32)
        for j in range(G):
            rows = slice(C * j, C * (j + 1))
            dba = jnp.zeros((C, 128), F32)
            for h in range(GDN_HEADS):
                b = GDN_HEADS * j + h
                for o_, dcx in ((0, dcq), (GDN_WIDTH, dck), (2 * GDN_WIDTH, dcv)):
                    cols = slice(o_ + GDN_DIM * h, o_ + GDN_DIM * (h + 1))
                    dpre_ref[rows, cols] = dcx[b] * dsilu[rows, cols]
                dba = dba + jnp.where(lane == h, db[b], 0.0) + jnp.where(lane == GDN_HEADS + h, da[b], 0.0)
                dgp = dgp + jnp.where(lane1 == h, dalog[b], 0.0) + jnp.where(lane1 == GDN_HEADS + h, ddtb[b], 0.0)
            dba_ref[rows, :] = dba
        dgp_ref[0:1, :] += dgp

    T = C * G
    return pl.pallas_call(
        body, name="gdn_post", grid=(nc // G,),
        in_specs=[sp["cur"], sp["prev"], sp["ba"], sp["cw"], sp["vec"], sp["hc"], sp["hd"], sp["hd"], sp["hd"], sp["hd"],
                  sp["hd"], sp["hd"], sp["hc"], sp["ge"]],
        out_specs=(sp["cur"], pl.BlockSpec((T, 128), lambda i: (i, 0)), pl.BlockSpec((8, 128), lambda i: (0, 0))),
        out_shape=(jax.ShapeDtypeStruct((S, 3 * GDN_WIDTH), F32), jax.ShapeDtypeStruct((S, 128), F32),
                   jax.ShapeDtypeStruct((8, 128), F32)),
        compiler_params=_cparams(("arbitrary",)),
    )(proj, proj, proj, conv_w, gp, tinv, u_v, w_k, d_uv, d_wk, d_qd, d_ke, d_at, d_ge)


def _conv_bwd(dpre, x, xcol0, w, K, name, tc):
    S, Cc = dpre.shape
    T = _pick_tile(S, 256)
    nt, ncol = S // T, Cc // tc
    xo = xcol0 // tc

    def body(d_ref, dn_ref, x_ref, xp_ref, w_ref, dx_ref, dw_ref):
        i = pl.program_id(1)
        dn = dn_ref[...] * jnp.where(i == nt - 1, 0.0, 1.0)
        dv = d_ref[...]
        ext_d = jnp.concatenate([dv, dn], axis=0)
        dx_ref[...] = _conv_taps_t(ext_d, w_ref[...], K, T).astype(dx_ref.dtype)
        xp = xp_ref[...] * jnp.where(i == 0, 0.0, 1.0)
        ext_x = jnp.concatenate([xp, x_ref[...]], axis=0)

        @pl.when(i == 0)
        def _():
            dw_ref[...] = jnp.zeros_like(dw_ref)

        for k in range(K):
            dw_ref[k:k + 1, :] += jnp.sum(dv * _shifted(ext_x, (K - 1) - k, 8, T), axis=0, keepdims=True)

    r8 = T // 8
    return pl.pallas_call(
        body, name=name, grid=(ncol, nt),
        in_specs=[pl.BlockSpec((T, tc), lambda j, i: (i, j)),
                  pl.BlockSpec((8, tc), lambda j, i: (jnp.minimum((i + 1) * r8, S // 8 - 1), j)),
                  pl.BlockSpec((T, tc), lambda j, i: (i, j + xo)),
                  pl.BlockSpec((8, tc), lambda j, i: (jnp.maximum(i * r8 - 1, 0), j + xo)),
                  pl.BlockSpec((K, tc), lambda j, i: (0, j))],
        out_specs=(pl.BlockSpec((T, tc), lambda j, i: (i, j)), pl.BlockSpec((K, tc), lambda j, i: (0, j))),
        out_shape=(jax.ShapeDtypeStruct((S, Cc), _MXU), jax.ShapeDtypeStruct((K, Cc), F32)),
        compiler_params=_cparams(("parallel", "arbitrary")),
    )(dpre, dpre, x, x, w)


def _dil_bias(nt, T):
    d = (np.arange(nt)[:, None, None] * T + np.arange(T)[None, None, :] - np.arange(T)[None, :, None])
    cnt = ((d >= 0) & (d <= 128)).astype(np.float64) + ((d >= 0) & (d % 4 == 0) & (d <= 512)) + ((d >= 0) & (d % 16 == 0))
    return jnp.asarray(np.where(cnt > 0, np.log(np.maximum(cnt, 1.0)), -1e30), dtype=F32)


def _attn_fwd(proj, after=()):
    S = proj.shape[0]
    T = min(ATT_T, S)
    nt = S // T
    bias = _dil_bias(nt, T)
    scale = DIL_DIM ** -0.5
    npair = DIL_WIDTH // 128
    qb0, kb0, vb0 = P_QKVB // 128, (P_QKVB + DIL_WIDTH) // 128, (P_QKVB + 2 * DIL_WIDTH) // 128

    def body(q_ref, k_ref, v_ref, b_ref, *rest):
        o_ref, lse_ref = rest[-2:]
        i = pl.program_id(1)
        qs = (q_ref[...] * scale).astype(_MXU)

        def step(j, carry):
            kt = k_ref[pl.ds(pl.multiple_of(j * T, T), T), :].astype(_MXU)
            vt = v_ref[pl.ds(pl.multiple_of(j * T, T), T), :].astype(_MXU)
            bt = b_ref[i - j]
            out = []
            for hh in range(2):
                m, l, acc = carry[hh]
                sl = slice(hh * DIL_DIM, (hh + 1) * DIL_DIM)
                s = lax.dot_general(kt[:, sl], qs[:, sl], (_NT, ((), ())), preferred_element_type=F32) + bt
                m_new = jnp.maximum(m, jnp.max(s, axis=0, keepdims=True))
                p = jnp.exp(s - m_new)
                a = jnp.exp(m - m_new)
                l = a * l + jnp.sum(p, axis=0, keepdims=True)
                acc = a * acc + lax.dot_general(vt[:, sl], p.astype(_MXU), (_TN, ((), ())), preferred_element_type=F32)
                out.append((m_new, l, acc))
            return tuple(out)

        init = tuple((jnp.full((1, T), -1e30, F32), jnp.zeros((1, T), F32), jnp.zeros((DIL_DIM, T), F32)) for _ in range(2))
        res = lax.fori_loop(0, i + 1, step, init)
        lse_ref[...] = jnp.zeros_like(lse_ref)
        for hh in range(2):
            m, l, acc = res[hh]
            o_ref[:, hh * DIL_DIM:(hh + 1) * DIL_DIM] = (acc / l).T
            lse_ref[hh:hh + 1, :] = m + jnp.log(l)

    return pl.pallas_call(
        body, name="attn_fwd", grid=(npair, nt),
        in_specs=[pl.BlockSpec((T, 128), lambda p, i: (i, qb0 + p)),
                  pl.BlockSpec((S, 128), lambda p, i: (0, kb0 + p)),
                  pl.BlockSpec((S, 128), lambda p, i: (0, vb0 + p)),
                  pl.BlockSpec((nt, T, T), lambda p, i: (0, 0, 0))] + [_ANY] * len(after),
        out_specs=(pl.BlockSpec((T, 128), lambda p, i: (i, GDN_WIDTH // 128 + p)),
                   pl.BlockSpec((None, None, 8, T), lambda p, i: (p, i, 0, 0))),
        out_shape=(jax.ShapeDtypeStruct((S, GDN_WIDTH + DIL_WIDTH), F32), jax.ShapeDtypeStruct((npair, nt, 8, T), F32)),
        compiler_params=_cparams(("parallel", "parallel")),
    )(proj, proj, proj, bias, *after)


def _attn_bwd(proj, mix, lse, d_mix):
    S = proj.shape[0]
    T = min(ATT_T, S)
    nt = S // T
    bias = _dil_bias(nt, T)
    scale = DIL_DIM ** -0.5
    npair = DIL_WIDTH // 128
    qb0, kb0, vb0 = P_QKVB // 128, (P_QKVB + DIL_WIDTH) // 128, (P_QKVB + 2 * DIL_WIDTH) // 128

    def body(q_ref, k_ref, v_ref, o_ref, lse_ref, do_ref, b_ref, dq_ref, dk_ref, dv_ref, dq_scr):
        j = pl.program_id(1)

        @pl.when(j == 0)
        def _():
            dq_scr[...] = jnp.zeros_like(dq_scr)

        kt = k_ref[...].astype(_MXU)
        vt = v_ref[...].astype(_MXU)
        ones = jnp.ones((8, DIL_DIM), F32)

        def step(i, carry):
            rows = pl.ds(pl.multiple_of(i * T, T), T)
            qs = (q_ref[rows, :] * scale).astype(_MXU)
            dov = do_ref[rows, :]
            prod = dov * o_ref[rows, :]
            lsev = lse_ref[i]
            dob = dov.astype(_MXU)
            bt = b_ref[i - j]
            out = []
            dqs = []
            for hh in range(2):
                dk, dv = carry[hh]
                sl = slice(hh * DIL_DIM, (hh + 1) * DIL_DIM)
                s = lax.dot_general(kt[:, sl], qs[:, sl], (_NT, ((), ())), preferred_element_type=F32) + bt
                p = jnp.exp(s - lsev[hh:hh + 1, :])
                delta = lax.dot_general(ones, prod[:, sl], (_NT, ((), ())), precision=_HI, preferred_element_type=F32)[0:1, :]
                dp = lax.dot_general(vt[:, sl], dob[:, sl], (_NT, ((), ())), preferred_element_type=F32)
                ds = (p * (dp - delta)).astype(_MXU)
                dv = dv + lax.dot_general(p.astype(_MXU), dob[:, sl], (_NN, ((), ())), preferred_element_type=F32)
                dk = dk + lax.dot_general(ds, qs[:, sl], (_NN, ((), ())), preferred_element_type=F32)
                dqs.append(lax.dot_general(ds, kt[:, sl], (_TN, ((), ())), preferred_element_type=F32) * scale)
                out.append((dk, dv))
            dq_scr[rows, :] += jnp.concatenate(dqs, axis=1)
            return tuple(out)

        init = tuple((jnp.zeros((T, DIL_DIM), F32), jnp.zeros((T, DIL_DIM), F32)) for _ in range(2))
        res = lax.fori_loop(j, nt, step, init)
        dk_ref[...] = jnp.concatenate([res[0][0], res[1][0]], axis=1).astype(dk_ref.dtype)
        dv_ref[...] = jnp.concatenate([res[0][1], res[1][1]], axis=1).astype(dv_ref.dtype)

        @pl.when(j == nt - 1)
        def _():
            dq_ref[...] = dq_scr[...].astype(dq_ref.dtype)

    full = lambda c0: pl.BlockSpec((S, 128), lambda p, j: (0, c0 + p))
    tile = lambda c0: pl.BlockSpec((T, 128), lambda p, j: (j, c0 + p))
    out3 = jax.ShapeDtypeStruct((S, DIL_WIDTH), _MXU)
    return pl.pallas_call(
        body, name="attn_bwd", grid=(npair, nt),
        in_specs=[full(qb0), tile(kb0), tile(vb0), full(GDN_WIDTH // 128),
                  pl.BlockSpec((None, nt, 8, T), lambda p, j: (p, 0, 0, 0)), full(GDN_WIDTH // 128),
                  pl.BlockSpec((nt, T, T), lambda p, j: (0, 0, 0))],
        out_specs=(full(0), tile(0), tile(0)),
        out_shape=(out3, out3, out3),
        scratch_shapes=[pltpu.VMEM((S, 128), F32)],
        compiler_params=_cparams(("parallel", "arbitrary")),
    )(proj, proj, proj, mix, lse, d_mix, bias)


def _ffn_act(up, cw):
    S, Cc = up.shape[0], up.shape[1] // 2
    T, tc = _pick_tile(S, 256), _pick_tile(Cc, 1536)
    r16 = T // 16
    nct = Cc // tc

    def body(g_ref, gp_ref, u_ref, up_ref, wg_ref, wu_ref, o_ref):
        keep = jnp.where(pl.program_id(1) == 0, 0.0, 1.0)
        cg = _conv_taps(jnp.concatenate([gp_ref[8:16, :].astype(F32) * keep, g_ref[...].astype(F32)], axis=0),
                        wg_ref[...], FFN_CONV, T)
        cu = _conv_taps(jnp.concatenate([up_ref[8:16, :].astype(F32) * keep, u_ref[...].astype(F32)], axis=0),
                        wu_ref[...], FFN_CONV, T)
        o_ref[...] = (_silu(cg) * cu).astype(o_ref.dtype)

    cur = lambda o: pl.BlockSpec((T, tc), lambda j, i: (i, j + o))
    prev = lambda o: pl.BlockSpec((16, tc), lambda j, i: (jnp.maximum(i * r16 - 1, 0), j + o))
    wsp = lambda o: pl.BlockSpec((FFN_CONV, tc), lambda j, i: (0, j + o))
    return pl.pallas_call(
        body, name="ffn_act", grid=(nct, S // T),
        in_specs=[cur(0), prev(0), cur(nct), prev(nct), wsp(0), wsp(nct)], out_specs=cur(0),
        out_shape=jax.ShapeDtypeStruct((S, Cc), _MXU),
        compiler_params=_cparams(("parallel", "parallel")),
    )(up, up, up, up, cw, cw)


def _ffn_act_bwd(d_act, up, cw):
    S, Cc = up.shape[0], up.shape[1] // 2
    T, tc = _pick_tile(S, 256), _pick_tile(Cc, 1536)
    r8, r16 = T // 8, T // 16
    nt = S // T
    nct = Cc // tc
    K = FFN_CONV

    def body(da_ref, dan_ref, g_ref, gp_ref, gn_ref, u_ref, up_ref, un_ref, wg_ref, wu_ref,
             dg_ref, du_ref, dwg_ref, dwu_ref):
        i = pl.program_id(1)
        keep_p = jnp.where(i == 0, 0.0, 1.0)
        keep_n = jnp.where(i == nt - 1, 0.0, 1.0)
        wg, wu = wg_ref[...], wu_ref[...]
        xg = jnp.concatenate([gp_ref[8:16, :].astype(F32) * keep_p, g_ref[...].astype(F32),
                              gn_ref[0:8, :].astype(F32) * keep_n], axis=0)
        xu = jnp.concatenate([up_ref[8:16, :].astype(F32) * keep_p, u_ref[...].astype(F32),
                              un_ref[0:8, :].astype(F32) * keep_n], axis=0)
        cg = _conv_taps(xg, wg, K, T + 8)
        cu = _conv_taps(xu, wu, K, T + 8)
        da = jnp.concatenate([da_ref[...], dan_ref[...] * keep_n], axis=0)
        sg = jax.nn.sigmoid(cg)
        d_cg = da * cu * (sg * (1.0 + cg * (1.0 - sg)))
        d_cu = da * (cg * sg)
        dg_ref[...] = _conv_taps_t(d_cg, wg, K, T).astype(dg_ref.dtype)
        du_ref[...] = _conv_taps_t(d_cu, wu, K, T).astype(du_ref.dtype)

        @pl.when(i == 0)
        def _():
            dwg_ref[...] = jnp.zeros_like(dwg_ref)
            dwu_ref[...] = jnp.zeros_like(dwu_ref)

        for k in range(K):
            dwg_ref[k:k + 1, :] += jnp.sum(d_cg[0:T, :] * _shifted(xg, (K - 1) - k, 8, T), axis=0, keepdims=True)
            dwu_ref[k:k + 1, :] += jnp.sum(d_cu[0:T, :] * _shifted(xu, (K - 1) - k, 8, T), axis=0, keepdims=True)

    cur = lambda o: pl.BlockSpec((T, tc), lambda j, i: (i, j + o))
    prev = lambda o: pl.BlockSpec((16, tc), lambda j, i: (jnp.maximum(i * r16 - 1, 0), j + o))
    nxt = lambda o: pl.BlockSpec((16, tc), lambda j, i: (jnp.minimum((i + 1) * r16, S // 16 - 1), j + o))
    nxt8 = pl.BlockSpec((8, tc), lambda j, i: (jnp.minimum((i + 1) * r8, S // 8 - 1), j))
    wsp = lambda o: pl.BlockSpec((K, tc), lambda j, i: (0, j + o))
    return pl.pallas_call(
        body, name="ffn_act_bwd", grid=(nct, nt),
        in_specs=[cur(0), nxt8, cur(0), prev(0), nxt(0), cur(nct), prev(nct), nxt(nct), wsp(0), wsp(nct)],
        out_specs=(cur(0), cur(0), wsp(0), wsp(0)),
        out_shape=(jax.ShapeDtypeStruct((S, Cc), _MXU), jax.ShapeDtypeStruct((S, Cc), _MXU),
                   jax.ShapeDtypeStruct((K, Cc), F32), jax.ShapeDtypeStruct((K, Cc), F32)),
        compiler_params=_cparams(("parallel", "arbitrary")),
    )(d_act, d_act, up, up, up, up, up, up, cw, cw)


def _local_step(x, tgt, h1, n1w, n2w, fnw, gp, gnw, wp, conv_w, fcw, rest_weights, early_grads):
    proj = _mm(h1, wp, "nn", name="proj")
    u_v, w_k, q_dec, k_end, attn, tinv, g_end = _gdn_pre(proj, conv_w, gp)
    mix, lse = _attn_fwd(proj)
    mix, states = _gdn_scan(u_v, w_k, q_dec, k_end, attn, g_end, proj, gnw, mix, after=[rest_weights[0]([mix])])
    w_out, w_up4, w_down = rest_weights[1]([mix])
    x2 = _mm(mix, w_out, "nn", residual=x, name="outproj")
    h2 = _rmsnorm_fwd(x2, n2w, "norm2")
    up = _mm(h2, w_up4, "nn", b_blocks=True, out_dtype=_MXU, name="up")
    act = _ffn_act(up, fcw)
    x3 = _mm(act, w_down, "nn", residual=x2, name="down")
    loss, dx3, dx3n, d_fnw = _loss_head(x3, fnw, tgt, "loss_head")
    d_act = _mm(dx3n, w_down, "nt", name="d_act")
    d_wdown = _mm(act, dx3n, "tn", name="d_wdown")
    d_upg, d_upu, d_fcwg, d_fcwu = _ffn_act_bwd(d_act, up, fcw)
    d_wup = _mm(h2, d_upg, "tn", place=("blocks", N_CHIPS, 0), tn=w_up4.shape[2], name="d_wgate")
    d_wup = _mm(h2, d_upu, "tn", place=("blocks", N_CHIPS, N_CHIPS // 2), tn=w_up4.shape[2], into=d_wup, name="d_wup")
    token = early_grads[0](d_wup, d_wdown)
    d_h2 = _mm_nt_blocks([d_upg, d_upu], w_up4, "d_h2", after=[token])
    dx2, d_n2w = _rmsnorm_bwd(d_h2, x2, n2w, dx3, "norm2_bwd", after=[token])
    token = early_grads[1](dx2)
    d_mix = _mm(dx2, w_out, "nt", name="d_mix")
    d_wout = _mm(mix, dx2, "tn", name="d_wout")
    dq_b, dk_b, dv_b = _attn_bwd(proj, mix, lse, d_mix)
    d_uv, d_wk, d_qd, d_ke, d_at, d_ge, d_z, d_gnw = _gdn_scan_bwd(u_v, w_k, q_dec, k_end, attn, g_end, proj,
                                                                   gnw + token[0:1, 0:1], states, d_mix)
    d_pre, d_ba, d_gp = _gdn_post(proj, conv_w, gp, tinv, u_v, w_k, d_uv, d_wk, d_qd, d_ke, d_at, d_ge)
    d_qkva, d_convw = _conv_bwd(d_pre, proj, 0, conv_w, GDN_CONV, "gdn_conv_bwd", 512)
    d_proj = jnp.concatenate([d_qkva, d_z.astype(_MXU), dq_b, dk_b, dv_b, d_ba.astype(_MXU),
                              jnp.zeros((x.shape[0], P_COLS - P_BA - 128), _MXU)], axis=1)
    d_wp = _mm(h1, d_proj, "tn", name="d_wp")
    token = early_grads[2](d_wp, d_wout)
    d_h1 = _mm(d_proj, wp, "nt", name="d_h1", after=[token])
    dx, d_n1w = _rmsnorm_bwd(d_h1, x, n1w, dx2, "norm1_bwd", after=[token])
    grads = dict(wp=d_wp, conv_w=d_convw, w_out=d_wout, w_up=d_wup, fcw_g=d_fcwg, fcw_u=d_fcwu, w_down=d_wdown,
                 n1w=d_n1w, n2w=d_n2w, fnw=d_fnw, gp=d_gp, gnw=d_gnw)
    return loss, dx, grads


_HBM = pl.BlockSpec(memory_space=pltpu.HBM)


def _pos():
    return lax.axis_index("x"), lax.axis_index("y"), lax.axis_index("c")


def _other_chips(x, y):
    return [(1 - x, y), (x, 1 - y), (1 - x, 1 - y)]


def _halvable(shape):
    return shape[0] % 32 == 0


def _rows_of_half(shape, half):
    if not _halvable(shape):
        return pl.ds(0, shape[0])
    return pl.ds(pl.multiple_of(half * (shape[0] // 2), 16), shape[0] // 2)


_SEM = pl.BlockSpec(memory_space=pltpu.SEMAPHORE)
_ANY = pl.BlockSpec(memory_space=pl.ANY)
_DATAFLOW = pltpu.SideEffectType.DATAFLOW_SIDE_EFFECTING


def _in_hbm(a):
    return pltpu.with_memory_space_constraint(a, pltpu.HBM)


def _halves_copy(src_refs, land_refs, send_sems, recv_sems, shapes, a, j, block, x, y, c):
    px, py = _other_chips(x, y)[j]
    rows = _rows_of_half(shapes[a], c)
    return pltpu.make_async_remote_copy(
        src_ref=src_refs[a].at[rows, :], dst_ref=land_refs[a].at[block, rows, :], send_sem=send_sems.at[3 * a + j],
        recv_sem=recv_sems.at[3 * a + j], device_id=(px, py, c), device_id_type=MESH)


def _gather_halves_start(shards, after, name):
    n = len(shards)
    shapes = [s.shape for s in shards]

    def body(*refs):
        ins, lands = refs[:n], refs[n:2 * n]
        send_sems, recv_sems = refs[2 * n + 1], refs[2 * n + 2]
        token = refs[-1]
        x, y, c = _pos()
        q = 2 * x + y
        for a in range(n):
            for j in range(3):
                _halves_copy(ins, lands, send_sems, recv_sems, shapes, a, j, q, x, y, c).start()
        token[...] = jnp.zeros_like(token)

    land_shapes = [(N_CHIPS,) + s.shape for s in shards]
    return pl.pallas_call(
        body, name=name,
        out_shape=(pltpu.SemaphoreType.DMA((3 * n,)), pltpu.SemaphoreType.DMA((3 * n,)),
                   *[pltpu.HBM(s.shape, s.dtype) for s in shards],
                   *[pltpu.HBM(ls, s.dtype) for ls, s in zip(land_shapes, shards)],
                   jax.ShapeDtypeStruct((8, 128), F32)),
        in_specs=[_HBM] * (2 * n) + [_ANY],
        out_specs=(_SEM, _SEM, *[_HBM] * (2 * n), pl.BlockSpec(memory_space=pltpu.VMEM)),
        input_output_aliases={a: 2 + a for a in range(2 * n)},
        compiler_params=pltpu.CompilerParams(has_side_effects=_DATAFLOW),
    )(*[_in_hbm(s) for s in shards], *[_in_hbm(lax.empty(ls, s.dtype)) for ls, s in zip(land_shapes, shards)], after)


def _gather_halves_wait(started, after, name):
    send_sems, recv_sems, *thru = started
    n = len(thru) // 2
    shapes = [t.shape for t in thru[:n]]

    def body(*refs):
        ins, lands = refs[:n], refs[n:2 * n]
        send_sems, recv_sems = refs[2 * n], refs[2 * n + 1]
        x, y, c = _pos()
        q = 2 * x + y
        chips = _other_chips(x, y)
        for a in range(n):
            for j, (px, py) in enumerate(chips):
                _halves_copy(ins, lands, send_sems, recv_sems, shapes, a, j, q, x, y, c).wait_send()
                _halves_copy(ins, lands, send_sems, recv_sems, shapes, a, j, 2 * px + py, x, y, c).wait_recv()

    outs = pl.pallas_call(
        body, name=name, out_shape=[pltpu.HBM(t.shape, t.dtype) for t in thru],
        in_specs=[_HBM] * (2 * n) + [_SEM, _SEM] + [_ANY] * len(after), out_specs=[_HBM] * (2 * n),
        input_output_aliases={a: a for a in range(2 * n)},
        compiler_params=pltpu.CompilerParams(has_side_effects=_DATAFLOW),
    )(*thru, send_sems, recv_sems, *after)
    return outs[:n], outs[n:]


def _sibling_fill(gathered, name):
    big = [a for a, g in enumerate(gathered) if _halvable(g.shape[1:])]
    n = len(gathered)

    def body(*refs):
        ins, outs = refs[:n], refs[n:2 * n]
        send_sems, recv_sems = refs[2 * n:]
        x, y, c = _pos()
        chips = _other_chips(x, y)

        def copy(k, j, half):
            a = big[k]
            px, py = chips[j]
            rows = _rows_of_half(gathered[a].shape[1:], half)
            return pltpu.make_async_remote_copy(
                src_ref=ins[a].at[2 * px + py, rows, :], dst_ref=outs[a].at[2 * px + py, rows, :],
                send_sem=send_sems.at[3 * k + j], recv_sem=recv_sems.at[3 * k + j],
                device_id=(x, y, 1 - c), device_id_type=MESH)

        sends = [copy(k, j, c) for k in range(len(big)) for j in range(3)]
        for cp in sends:
            cp.start()
        for k in range(len(big)):
            for j in range(3):
                copy(k, j, 1 - c).wait_recv()
        for cp in sends:
            cp.wait_send()

    return pl.pallas_call(
        body, name=name, in_specs=[_HBM] * n, out_specs=[_HBM] * n,
        out_shape=[jax.ShapeDtypeStruct(g.shape, g.dtype) for g in gathered],
        input_output_aliases={a: a for a in range(n)},
        scratch_shapes=[pltpu.SemaphoreType.DMA((3 * len(big),)), pltpu.SemaphoreType.DMA((3 * len(big),))],
    )(*gathered)


def _fill_copy(refs, send_sems, recv_sems, shapes, a, j, half, x, y, c):
    px, py = _other_chips(x, y)[j]
    rows = _rows_of_half(shapes[a], half)
    return pltpu.make_async_remote_copy(
        src_ref=refs[a].at[2 * px + py, rows, :], dst_ref=refs[a].at[2 * px + py, rows, :],
        send_sem=send_sems.at[3 * a + j], recv_sem=recv_sems.at[3 * a + j],
        device_id=(x, y, 1 - c), device_id_type=MESH)


def _sibling_fill_start(gathered, name):
    n = len(gathered)
    shapes = [g.shape[1:] for g in gathered]

    def body(*refs):
        ins = refs[:n]
        send_sems, recv_sems = refs[n], refs[n + 1]
        token = refs[-1]
        x, y, c = _pos()
        for a in range(n):
            for j in range(3):
                _fill_copy(ins, send_sems, recv_sems, shapes, a, j, c, x, y, c).start()
        token[...] = jnp.zeros_like(token)

    return pl.pallas_call(
        body, name=name,
        out_shape=(pltpu.SemaphoreType.DMA((3 * n,)), pltpu.SemaphoreType.DMA((3 * n,)),
                   *[pltpu.HBM(g.shape, g.dtype) for g in gathered], jax.ShapeDtypeStruct((8, 128), F32)),
        in_specs=[_HBM] * n,
        out_specs=(_SEM, _SEM, *[_HBM] * n, pl.BlockSpec(memory_space=pltpu.VMEM)),
        input_output_aliases={a: 2 + a for a in range(n)},
        compiler_params=pltpu.CompilerParams(has_side_effects=_DATAFLOW),
    )(*[_in_hbm(g) for g in gathered])


def _sibling_fill_wait(started, after, name):
    send_sems, recv_sems, *thru = started
    n = len(thru)
    shapes = [t.shape[1:] for t in thru]

    def body(*refs):
        ins = refs[:n]
        send_sems, recv_sems = refs[n], refs[n + 1]
        x, y, c = _pos()
        for a in range(n):
            for j in range(3):
                _fill_copy(ins, send_sems, recv_sems, shapes, a, j, c, x, y, c).wait_send()
                _fill_copy(ins, send_sems, recv_sems, shapes, a, j, 1 - c, x, y, c).wait_recv()

    return pl.pallas_call(
        body, name=name, out_shape=[pltpu.HBM(t.shape, t.dtype) for t in thru],
        in_specs=[_HBM] * n + [_SEM, _SEM] + [_ANY] * len(after), out_specs=[_HBM] * n,
        input_output_aliases={a: a for a in range(n)},
        compiler_params=pltpu.CompilerParams(has_side_effects=_DATAFLOW),
    )(*thru, send_sems, recv_sems, *after)


def _place_own(shards, gathered, cq, name):
    n = len(shards)
    steps = 4

    def body(cq_ref, *refs):
        for a in range(n):
            refs[2 * n + a][...] = refs[a][...]

    def tile(shape):
        return shape[0] // steps if _halvable(shape) else shape[0]

    in_specs = [pl.BlockSpec((tile(s.shape), s.shape[1]), (lambda i, s_: (i, 0)) if _halvable(s.shape) else (lambda i, s_: (0, 0)))
                for s in shards]
    in_specs += [pl.BlockSpec(memory_space=pl.ANY)] * n
    out_specs = [pl.BlockSpec((None, tile(s.shape), s.shape[1]),
                              (lambda i, s_: (s_[1], i, 0)) if _halvable(s.shape) else (lambda i, s_: (s_[1], 0, 0)))
                 for s in shards]
    gs = pltpu.PrefetchScalarGridSpec(num_scalar_prefetch=1, grid=(steps,), in_specs=in_specs, out_specs=out_specs)
    return pl.pallas_call(
        body, name=name, grid_spec=gs, out_shape=[jax.ShapeDtypeStruct(g.shape, g.dtype) for g in gathered],
        input_output_aliases={1 + n + a: a for a in range(n)},
        compiler_params=_cparams(("arbitrary",)),
    )(cq, *shards, *gathered)


def _half_rows(ref, c, rh):
    return ref.at[:, pl.ds(pl.multiple_of(c * rh, 8), rh), :]


def _chips_copy(src_refs, land_refs, send_sems, recv_sems, a, j, x, y, c):
    px, py = _other_chips(x, y)[j]
    return pltpu.make_async_remote_copy(src_ref=src_refs[a].at[2 * px + py], dst_ref=land_refs[a].at[j],
                                        send_sem=send_sems.at[3 * a + j], recv_sem=recv_sems.at[3 * a + j],
                                        device_id=(px, py, c), device_id_type=MESH)


def _grad_chips_start(parts, name):
    n = len(parts)

    def body(*refs):
        ins, lands = refs[:n], refs[n:2 * n]
        send_sems, recv_sems = refs[2 * n], refs[2 * n + 1]
        token = refs[-1]
        x, y, c = _pos()
        for a in range(n):
            for j in range(3):
                _chips_copy(ins, lands, send_sems, recv_sems, a, j, x, y, c).start()
        token[...] = jnp.zeros_like(token)

    land_shapes = [(3,) + p.shape[1:] for p in parts]
    return pl.pallas_call(
        body, name=name,
        out_shape=(pltpu.SemaphoreType.DMA((3 * n,)), pltpu.SemaphoreType.DMA((3 * n,)),
                   *[pltpu.HBM(p.shape, p.dtype) for p in parts],
                   *[pltpu.HBM(ls, p.dtype) for ls, p in zip(land_shapes, parts)],
                   jax.ShapeDtypeStruct((8, 128), F32)),
        in_specs=[_HBM] * (2 * n),
        out_specs=(_SEM, _SEM, *[_HBM] * (2 * n), pl.BlockSpec(memory_space=pltpu.VMEM)),
        input_output_aliases={a: 2 + a for a in range(2 * n)},
        compiler_params=pltpu.CompilerParams(has_side_effects=_DATAFLOW),
    )(*[_in_hbm(p) for p in parts], *[_in_hbm(lax.empty(ls, p.dtype)) for ls, p in zip(land_shapes, parts)])


def _grad_chips_wait(started, after, name):
    send_sems, recv_sems, *thru = started
    n = len(thru) // 2

    def body(*refs):
        ins, lands = refs[:n], refs[n:2 * n]
        send_sems, recv_sems = refs[2 * n], refs[2 * n + 1]
        x, y, c = _pos()
        for a in range(n):
            for j in range(3):
                cp = _chips_copy(ins, lands, send_sems, recv_sems, a, j, x, y, c)
                cp.wait_send()
                cp.wait_recv()

    outs = pl.pallas_call(
        body, name=name, out_shape=[pltpu.HBM(t.shape, t.dtype) for t in thru],
        in_specs=[_HBM] * (2 * n) + [_SEM, _SEM] + [_ANY] * len(after), out_specs=[_HBM] * (2 * n),
        input_output_aliases={a: a for a in range(2 * n)},
        compiler_params=pltpu.CompilerParams(has_side_effects=_DATAFLOW),
    )(*thru, send_sems, recv_sems, *after)
    return outs[n:]


def _sibling_copy(src_refs, land_refs, send_sems, recv_sems, rhs, a, c, x, y):
    return pltpu.make_async_remote_copy(src_ref=_half_rows(src_refs[a], 1 - c, rhs[a]), dst_ref=land_refs[a],
                                        send_sem=send_sems.at[a], recv_sem=recv_sems.at[a],
                                        device_id=(x, y, 1 - c), device_id_type=MESH)


def _grad_sibling_start(fams, name):
    n = len(fams)
    rhs = [f.shape[1] // 2 for f in fams]

    def body(*refs):
        ins, lands = refs[:n], refs[n:2 * n]
        send_sems, recv_sems = refs[2 * n], refs[2 * n + 1]
        token = refs[-1]
        x, y, c = _pos()
        for a in range(n):
            _sibling_copy(ins, lands, send_sems, recv_sems, rhs, a, c, x, y).start()
        token[...] = jnp.zeros_like(token)

    land_shapes = [(f.shape[0], f.shape[1] // 2, f.shape[2]) for f in fams]
    return pl.pallas_call(
        body, name=name,
        out_shape=(pltpu.SemaphoreType.DMA((n,)), pltpu.SemaphoreType.DMA((n,)),
                   *[pltpu.HBM(f.shape, f.dtype) for f in fams],
                   *[pltpu.HBM(ls, f.dtype) for ls, f in zip(land_shapes, fams)],
                   jax.ShapeDtypeStruct((8, 128), F32)),
        in_specs=[_HBM] * (2 * n),
        out_specs=(_SEM, _SEM, *[_HBM] * (2 * n), pl.BlockSpec(memory_space=pltpu.VMEM)),
        input_output_aliases={a: 2 + a for a in range(2 * n)},
        compiler_params=pltpu.CompilerParams(has_side_effects=_DATAFLOW),
    )(*[_in_hbm(f) for f in fams], *[_in_hbm(lax.empty(ls, f.dtype)) for ls, f in zip(land_shapes, fams)])


def _grad_sibling_wait(started, after, name):
    send_sems, recv_sems, *thru = started
    n = len(thru) // 2
    rhs = [t.shape[1] // 2 for t in thru[:n]]

    def body(*refs):
        ins, lands = refs[:n], refs[n:2 * n]
        send_sems, recv_sems = refs[2 * n], refs[2 * n + 1]
        x, y, c = _pos()
        for a in range(n):
            cp = _sibling_copy(ins, lands, send_sems, recv_sems, rhs, a, c, x, y)
            cp.wait_send()
            cp.wait_recv()

    outs = pl.pallas_call(
        body, name=name, out_shape=[pltpu.HBM(t.shape, t.dtype) for t in thru],
        in_specs=[_HBM] * (2 * n) + [_SEM, _SEM] + [_ANY] * len(after), out_specs=[_HBM] * (2 * n),
        input_output_aliases={a: a for a in range(2 * n)},
        compiler_params=pltpu.CompilerParams(has_side_effects=_DATAFLOW),
    )(*thru, send_sems, recv_sems, *after)
    return outs[:n], outs[n:]


def _grad_share(fulls, name, small=None):
    n = len(fulls)
    ns = 0 if small is None else 1
    rhs = [f.shape[0] // 2 for f in fulls]

    def body(*refs):
        ins, outs = refs[:n], refs[n + ns:2 * n + ns]
        send_sems, recv_sems = refs[2 * (n + ns)], refs[2 * (n + ns) + 1]
        x, y, c = _pos()

        def copy(a, half):
            rows = pl.ds(pl.multiple_of(half * rhs[a], 8), rhs[a])
            return pltpu.make_async_remote_copy(src_ref=ins[a].at[rows, :], dst_ref=outs[a].at[rows, :],
                                                send_sem=send_sems.at[7 * ns + a], recv_sem=recv_sems.at[7 * ns + a],
                                                device_id=(x, y, 1 - c), device_id_type=MESH)

        sends = [copy(a, c) for a in range(n)]
        for cp in sends:
            cp.start()
        if ns:
            small_ref, all_ref = refs[n], refs[2 * n + 1]
            me = 4 * x + 2 * y + c

            def peer(r):
                dx, dy, dc = (r >> 2) & 1, (r >> 1) & 1, r & 1
                return (x if dx == 0 else 1 - x), (y if dy == 0 else 1 - y), (c if dc == 0 else 1 - c)

            def small_copy(r, slot):
                return pltpu.make_async_remote_copy(src_ref=small_ref, dst_ref=all_ref.at[slot], send_sem=send_sems.at[r - 1],
                                                    recv_sem=recv_sems.at[r - 1], device_id=peer(r), device_id_type=MESH)

            smalls = [small_copy(r, me) for r in range(1, 8)]
            for cp in smalls:
                cp.start()
            for r in range(1, 8):
                px, py, pc = peer(r)
                small_copy(r, 4 * px + 2 * py + pc).wait_recv()
            sends = sends + smalls
        for a in range(n):
            copy(a, 1 - c).wait_recv()
        for cp in sends:
            cp.wait_send()

    return pl.pallas_call(
        body, name=name, in_specs=[_HBM] * (n + ns), out_specs=[_HBM] * (n + ns),
        out_shape=[jax.ShapeDtypeStruct(f.shape, f.dtype) for f in fulls]
        + ([jax.ShapeDtypeStruct((8,) + small.shape, small.dtype)] if ns else []),
        input_output_aliases={a: a for a in range(n)},
        scratch_shapes=[pltpu.SemaphoreType.DMA((7 * ns + n,)), pltpu.SemaphoreType.DMA((7 * ns + n,))],
    )(*fulls, *([small] if ns else []))


def _add_sibling(own, recv, cq, name):
    nb, R, Cc = own.shape
    Rh = R // 2

    def body(cq_ref, a_ref, b_ref, o32_ref, o16_ref):
        s = a_ref[0] + b_ref[0]
        mine = pl.program_id(0) == cq_ref[1]

        @pl.when(mine)
        def _():
            o32_ref[...] = s

        @pl.when(jnp.logical_not(mine))
        def _():
            o16_ref[0] = s.astype(o16_ref.dtype)

    sp = pl.BlockSpec((1, Rh, Cc), lambda b, s: (b, 0, 0))
    gs = pltpu.PrefetchScalarGridSpec(
        num_scalar_prefetch=1, grid=(nb,),
        in_specs=[pl.BlockSpec((1, Rh, Cc), lambda b, s: (b, s[0], 0)), sp],
        out_specs=[pl.BlockSpec((Rh, Cc), lambda b, s: (0, 0)), sp])
    return pl.pallas_call(
        body, name=name, grid_spec=gs,
        out_shape=[jax.ShapeDtypeStruct((Rh, Cc), F32), jax.ShapeDtypeStruct((nb, Rh, Cc), _MXU)],
        compiler_params=_cparams(("arbitrary",)),
    )(cq, own, recv)


def _add_chips(part32, recv3, cq, name):
    Rh, Cc = part32.shape

    def body(cq_ref, a_ref, b_ref, o_ref):
        acc = a_ref[...]
        for j in range(3):
            acc = acc + b_ref[j].astype(F32)
        o_ref[...] = acc

    gs = pltpu.PrefetchScalarGridSpec(
        num_scalar_prefetch=1, grid=(1,),
        in_specs=[pl.BlockSpec((Rh, Cc), lambda i, s: (0, 0)), pl.BlockSpec((3, Rh, Cc), lambda i, s: (0, 0, 0))],
        out_specs=pl.BlockSpec((Rh, Cc), lambda i, s: (s[0], 0)))
    return pl.pallas_call(
        body, name=name, grid_spec=gs, out_shape=jax.ShapeDtypeStruct((2 * Rh, Cc), F32),
        compiler_params=_cparams(("arbitrary",)),
    )(cq, part32, recv3)


def _adamw(w, g, m, v, name):
    R, Cc = w.shape
    T = max([t for t in range(8, 257, 8) if R % t == 0], default=R)

    def body(w_ref, g_ref, m_ref, v_ref, d_ref, mo_ref, vo_ref):
        d_ref[...], mo_ref[...], vo_ref[...] = _adamw_math(w_ref[...], g_ref[...], m_ref[...], v_ref[...])

    sp = pl.BlockSpec((T, Cc), lambda i: (i, 0))
    sh = jax.ShapeDtypeStruct((R, Cc), F32)
    return pl.pallas_call(
        body, name=name, grid=(R // T,), in_specs=[sp] * 4, out_specs=(sp, sp, sp), out_shape=(sh, sh, sh),
        compiler_params=_cparams(("parallel",)),
    )(w, g, m, v)


SMALL_ROWS = 32
ROW_CONV, ROW_FCG, ROW_FCU = 5, 13, 22


def _adamw_math(w, g, m, v):
    mn = ADAM_B1 * m + (1.0 - ADAM_B1) * g
    vn = ADAM_B2 * v + (1.0 - ADAM_B2) * (g * g)
    c1 = 1.0 / (1.0 - ADAM_B1 ** ADAM_STEP)
    c2 = 1.0 / (1.0 - ADAM_B2 ** ADAM_STEP)
    return -ADAM_LR * ((mn * c1) / (jnp.sqrt(vn * c2) + ADAM_EPS) + ADAM_WD * w), mn, vn


def _pack_small(n1, n2, fn, gp, gn, conv, fcg, fcu, loss):
    W = D_MODEL

    def body(n1_ref, n2_ref, fn_ref, gp_ref, gn_ref, conv_ref, fcg_ref, fcu_ref, loss_ref, o_ref):
        o_ref[...] = jnp.zeros_like(o_ref)
        o_ref[0:1, :] = n1_ref[...]
        o_ref[1:2, :] = n2_ref[...]
        o_ref[2:3, :] = fn_ref[...]
        o_ref[3:4, 0:8] = gp_ref[0:1, 0:8]
        o_ref[3:4, 8:9] = loss_ref[0:1, 0:1]
        o_ref[4:5, 0:128] = gn_ref[...]
        for i in range(GDN_CONV):
            o_ref[ROW_CONV + 2 * i:ROW_CONV + 2 * i + 1, :] = conv_ref[i:i + 1, 0:W]
            o_ref[ROW_CONV + 2 * i + 1:ROW_CONV + 2 * i + 2, 0:3 * GDN_WIDTH - W] = conv_ref[i:i + 1, W:3 * GDN_WIDTH]
        for r0, ref in ((ROW_FCG, fcg_ref), (ROW_FCU, fcu_ref)):
            for i in range(FFN_CONV):
                for k in range(3):
                    n = min(W, D_FF - k * W)
                    o_ref[r0 + 3 * i + k:r0 + 3 * i + k + 1, 0:n] = ref[i:i + 1, k * W:k * W + n]

    return pl.pallas_call(body, name="pack_small", out_shape=jax.ShapeDtypeStruct((SMALL_ROWS, W), F32))(
        n1, n2, fn, gp, gn, conv, fcg, fcu, loss)


def _small_step(meq, small_all, small, ws, ms, vs):
    W = D_MODEL
    n = len(ws)
    cw, fw = ws[6].shape[1], ws[7].shape[1]

    def body(meq_ref, all_ref, own_ref, *refs):
        w_refs, m_refs, v_refs = refs[:n], refs[n:2 * n], refs[2 * n:3 * n]
        loss_ref = refs[3 * n]
        outs = refs[3 * n + 1:]
        me, q = meq_ref[0], meq_ref[1]
        red = None
        for d in range(8):
            term = jnp.where(me == d, own_ref[...], all_ref[d])
            red = term if red is None else red + term
        loss_ref[...] = jnp.broadcast_to(red[3:4, 8:9], loss_ref.shape)
        conv = [jnp.concatenate([red[ROW_CONV + 2 * i:ROW_CONV + 2 * i + 1, :],
                                 red[ROW_CONV + 2 * i + 1:ROW_CONV + 2 * i + 2, 0:3 * GDN_WIDTH - W]], axis=1)
                for i in range(GDN_CONV)]
        conv = jnp.concatenate(conv, axis=0)

        def fc_rows(r0):
            rows = [jnp.concatenate([red[r0 + 3 * i + k:r0 + 3 * i + k + 1, 0:min(W, D_FF - k * W)] for k in range(3)], axis=1)
                    for i in range(FFN_CONV)]
            return jnp.concatenate(rows, axis=0)

        fc = jnp.concatenate([fc_rows(ROW_FCG), fc_rows(ROW_FCU)], axis=1)

        def chip_block(full, width):
            out = None
            for j in range(N_CHIPS):
                term = jnp.where(q == j, full[:, width * j:width * (j + 1)], 0.0)
                out = term if out is None else out + term
            return out

        grads = [red[0:1, :], red[1:2, :], red[2:3, :], red[3:4, 0:4], red[3:4, 4:8], red[4:5, 0:128],
                 chip_block(conv, cw), chip_block(fc, fw)]
        for k in range(n):
            d_, m_, v_ = _adamw_math(w_refs[k][...], grads[k], m_refs[k][...], v_refs[k][...])
            outs[4 * k][...] = grads[k]
            outs[4 * k + 1][...] = d_
            outs[4 * k + 2][...] = m_
            outs[4 * k + 3][...] = v_

    full = lambda a: pl.BlockSpec(a.shape, lambda i, s_, nd=len(a.shape): (0,) * nd)
    arrays = [small_all, small, *ws, *ms, *vs]
    out_shapes = [jax.ShapeDtypeStruct((8, 128), F32)] + [jax.ShapeDtypeStruct(w.shape, F32) for w in ws for _ in range(4)]
    gs = pltpu.PrefetchScalarGridSpec(
        num_scalar_prefetch=1, grid=(1,), in_specs=[full(a) for a in arrays],
        out_specs=[pl.BlockSpec(o.shape, lambda i, s_, nd=len(o.shape): (0,) * nd) for o in out_shapes])
    return pl.pallas_call(body, name="small_step", grid_spec=gs, out_shape=out_shapes)(meq, *arrays)


def _pad_lanes(v, n=D_MODEL):
    return jnp.pad(v, ((0, 0), (0, n - v.shape[1])))


def kernel(x, norm1_w, w_in, conv_qkv_w, a_log, dt_bias, gdn_norm_w, w_out, norm2_w, w_up, ffn_conv_w, w_down, final_norm_w, loss_target, m_norm1_w, m_w_in, m_conv_qkv_w, m_a_log, m_dt_bias, m_gdn_norm_w, m_w_out, m_norm2_w, m_w_up, m_ffn_conv_w, m_w_down, m_final_norm_w, v_norm1_w, v_w_in, v_conv_qkv_w, v_a_log, v_dt_bias, v_gdn_norm_w, v_w_out, v_norm2_w, v_w_up, v_ffn_conv_w, v_w_down, v_final_norm_w):
    c = lax.axis_index("c")
    q = 2 * lax.axis_index("x") + lax.axis_index("y")
    S = x.shape[1]
    cq = jnp.stack([c, q]).astype(jnp.int32)

    *in_started, in_token = _gather_halves_start([w_in[0].astype(_MXU), conv_qkv_w[0], ffn_conv_w[0]], x, "gather_in_start")
    w_in_l, m_w_in_l, v_w_in_l = (a + in_token[0:1, 0:1] for a in (w_in, m_w_in, v_w_in))
    h1 = _rmsnorm_fwd(x[0], norm1_w, "norm1", after=[in_token])
    rest = [(a[0] + in_token[0:1, 0:1]).astype(_MXU) for a in (w_out, w_up, w_down)]
    in_shards, got_in = _gather_halves_wait(in_started, [w_in_l, m_w_in_l, v_w_in_l, h1, *rest], "gather_in_wait")
    g_in, g_conv, g_fconv = _place_own(in_shards, _sibling_fill(got_in, "fill_in"), cq, "place_in")
    *rest_started, token = _gather_halves_start(rest, g_conv, "gather_rest_start")

    rest_state = {}

    def rest_arrived(after):
        rest_state["shards"], got = _gather_halves_wait(rest_started, after, "gather_rest_wait")
        *rest_state["fill"], tok = _sibling_fill_start(got, "fill_rest_start")
        return tok

    def rest_filled(after):
        got = _sibling_fill_wait(rest_state["fill"], after, "fill_rest_wait")
        g_out, g_up, g_down = _place_own(rest_state["shards"], got, cq, "place_rest")
        return g_out.reshape(D_MODEL, D_MODEL), g_up, g_down.reshape(D_FF, D_MODEL)

    rest_weights = (rest_arrived, rest_filled)
    wp = _wp_assemble(g_in, [token])
    conv_f = jnp.concatenate([g_conv[i] for i in range(N_CHIPS)], axis=1)
    fcw = jnp.concatenate([g_fconv[i] for i in range(N_CHIPS)], axis=1)
    gp = _pad_lanes(jnp.concatenate([a_log, dt_bias], axis=1), 128)
    fnw = final_norm_w[None, :]
    early = {}

    def early_sibling(d_wup, d_wdown):
        *early["sibling"], tok = _grad_sibling_start([d_wup, d_wdown.reshape(N_CHIPS, D_FF // N_CHIPS, D_MODEL)],
                                                     "grad_sibling_early_start")
        return tok

    def early_chips(dx2):
        fams_e, got_e = _grad_sibling_wait(early["sibling"], [dx2], "grad_sibling_early_wait")
        early["parts"] = [_add_sibling(f, r, cq, "add_sibling_" + nm) for f, r, nm in zip(fams_e, got_e, ("w_up", "w_down"))]
        *early["started"], tok = _grad_chips_start([p[1] for p in early["parts"]], "grad_chips_start")
        return tok

    def late_sibling(d_wp, d_wout):
        *early["late_sibling"], tok = _grad_sibling_start(
            [_win_split(d_wp), d_wout.reshape(N_CHIPS, D_MODEL // N_CHIPS, D_MODEL)], "grad_sibling_late_start")
        return tok

    early_grads = (early_sibling, early_chips, late_sibling)

    loss_l, dx, g = _local_step(x[0], loss_target[0], h1, norm1_w, norm2_w, fnw, gp, gdn_norm_w, wp,
                                conv_f, fcw, rest_weights, early_grads)
    small = _pack_small(g["n1w"], g["n2w"], g["fnw"], g["gp"], g["gnw"], g["conv_w"], g["fcw_g"], g["fcw_u"], loss_l)
    fams, got = _grad_sibling_wait(early["late_sibling"], [dx], "grad_sibling_late_wait")
    parts = [_add_sibling(f, r, cq, "add_sibling_" + nm) for f, r, nm in zip(fams, got, ("w_in", "w_out"))]
    *late_started, late_token = _grad_chips_start([p[1] for p in parts], "grad_chips_late_start")
    got3_e = _grad_chips_wait(early["started"], [dx, g["wp"], late_token], "grad_chips_wait")
    g_w_up, g_w_down = _grad_share(
        [_add_chips(p[0], r3, cq, "add_chips_" + nm) for p, r3, nm in zip(early["parts"], got3_e, ("w_up", "w_down"))],
        "grad_share_early")
    big = {}

    def adamw_big(nm, w, gg, m, v):
        d_, m_, v_ = _adamw(w[0], gg, m[0], v[0], "adamw_" + nm)
        big[nm] = (gg[None], d_[None], m_[None], v_[None])

    adamw_big("w_up", w_up, g_w_up, m_w_up, v_w_up)
    adamw_big("w_down", w_down, g_w_down, m_w_down, v_w_down)
    got3 = _grad_chips_wait(late_started, [big["w_up"][1], big["w_down"][1]], "grad_chips_late_wait")
    g_w_in, g_w_out, small_all = _grad_share(
        [_add_chips(p[0], r3, cq, "add_chips_" + nm) for p, r3, nm in zip(parts, got3, ("w_in", "w_out"))],
        "grad_share_late", small)
    adamw_big("w_in", w_in_l, g_w_in, m_w_in_l, v_w_in_l)
    adamw_big("w_out", w_out, g_w_out, m_w_out, v_w_out)
    small_names = ["norm1_w", "norm2_w", "final_norm_w", "a_log", "dt_bias", "gdn_norm_w", "conv_qkv_w", "ffn_conv_w"]
    loss_b, *small_out = _small_step(
        jnp.stack([2 * q + c, q]).astype(jnp.int32), small_all, small,
        [norm1_w, norm2_w, final_norm_w[None], a_log, dt_bias, gdn_norm_w, conv_qkv_w[0], ffn_conv_w[0]],
        [m_norm1_w, m_norm2_w, m_final_norm_w[None], m_a_log, m_dt_bias, m_gdn_norm_w, m_conv_qkv_w[0], m_ffn_conv_w[0]],
        [v_norm1_w, v_norm2_w, v_final_norm_w[None], v_a_log, v_dt_bias, v_gdn_norm_w, v_conv_qkv_w[0], v_ffn_conv_w[0]])
    like = dict(final_norm_w=lambda t: t[0], conv_qkv_w=lambda t: t[None], ffn_conv_w=lambda t: t[None])
    for k, nm in enumerate(small_names):
        big[nm] = tuple(like.get(nm, lambda t: t)(t) for t in small_out[4 * k:4 * k + 4])
    names = ["norm1_w", "w_in", "conv_qkv_w", "a_log", "dt_bias", "gdn_norm_w", "w_out", "norm2_w", "w_up",
             "ffn_conv_w", "w_down", "final_norm_w"]
    return (loss_b[0, 0], dx[None], *[big[n][0] for n in names], *[big[n][1] for n in names],
            *[big[n][2] for n in names], *[big[n][3] for n in names])
```

```python
import functools
import math

import numpy as np
import jax
import jax.numpy as jnp
from jax import lax
from jax.experimental import pallas as pl
from jax.experimental.pallas import tpu as pltpu

F32 = jnp.float32
BF16 = jnp.bfloat16
_MXU = jnp.bfloat16
_HI = lax.Precision.HIGHEST
EPS = 1e-6
V7X_VMEM_LIMIT = 56 * 1024 * 1024
MESH = pl.DeviceIdType.MESH

D_MODEL = 1024
GDN_HEADS, GDN_DIM, GDN_CHUNK, GDN_CONV = 4, 128, 64, 4
GDN_WIDTH = GDN_HEADS * GDN_DIM
DIL_HEADS, DIL_DIM = 8, 64
DIL_WIDTH = DIL_HEADS * DIL_DIM
D_FF, FFN_CONV = 2816, 3
IN_COLS = 3592
P_COLS = 3840
P_Z, P_QKVB, P_BA = 1536, 2048, 3584
ATT_T = 1024
ADAM_LR, ADAM_B1, ADAM_B2, ADAM_EPS, ADAM_WD, ADAM_STEP = 0.001, 0.9, 0.999, 1e-08, 0.01, 10
N_CHIPS = 4


def _cparams(sem=None, vmem=None):
    kw = {}
    if sem is not None:
        kw["dimension_semantics"] = sem
    if vmem is not None:
        kw["vmem_limit_bytes"] = vmem
    return pltpu.CompilerParams(**kw)


def _silu(x):
    return x * jax.nn.sigmoid(x)


def _pick_tile(n, cap):
    best = None
    for t in range(128, min(n, cap) + 1, 128):
        if n % t == 0:
            best = t
    return best or n


def _mm(a, b, mode, *, out_dtype=F32, residual=None, name, b_blocks=False, place=None, into=None, tn=None, after=()):
    if mode == "nn":
        M, K = a.shape
        N = b.shape[0] * b.shape[2] if b_blocks else b.shape[1]
    elif mode == "nt":
        (M, K), (N, _) = a.shape, b.shape
    else:
        (K, M), (_, N) = a.shape, b.shape
    tm = _pick_tile(M, 1024)
    tn = b.shape[2] if b_blocks else (tn or _pick_tile(N, 1536))

    def vmem(tm, tn):
        return 2 * (tm * K * a.dtype.itemsize + tn * K * b.dtype.itemsize
                    + tm * tn * (jnp.dtype(out_dtype).itemsize + (4 if residual is not None else 0))) + 3 * tm * tn * 4

    fixed_tn = b_blocks or (place is not None and place[0] == "blocks")
    while vmem(tm, tn) > 40 * 1024 * 1024:
        if (tm >= tn or fixed_tn) and tm % 256 == 0:
            tm //= 2
        elif tn % 256 == 0 and not fixed_tn:
            tn //= 2
        else:
            tm //= 2
    a_spec = pl.BlockSpec((K, tm), lambda j, i: (0, i)) if mode == "tn" else pl.BlockSpec((tm, K), lambda j, i: (i, 0))
    if b_blocks:
        b_spec = pl.BlockSpec((None, K, tn), lambda j, i: (j, 0, 0))
    else:
        b_spec = pl.BlockSpec((tn, K), lambda j, i: (j, 0)) if mode == "nt" else pl.BlockSpec((K, tn), lambda j, i: (0, j))
    r_spec = pl.BlockSpec((tm, tn), lambda j, i: (i, j))
    if place is None:
        o_spec, o_shape = r_spec, (M, N)
    elif place[0] == "rows":
        off = place[2] // tm
        o_spec, o_shape = pl.BlockSpec((tm, tn), lambda j, i: (i + off, j)), (place[1], N)
    else:
        off = place[2]
        o_spec, o_shape = pl.BlockSpec((None, tm, tn), lambda j, i: (j + off, i, 0)), (place[1], M, tn)
    dims = {"nn": (((1,), (0,)), ((), ())), "nt": (((1,), (1,)), ((), ())), "tn": (((0,), (0,)), ((), ()))}[mode]

    def body(*refs):
        a_ref, b_ref = refs[0], refs[1]
        o_ref = refs[-1]
        acc = lax.dot_general(a_ref[...].astype(_MXU), b_ref[...].astype(_MXU), dims, preferred_element_type=F32)
        if residual is not None:
            acc = acc + refs[2][...]
        o_ref[...] = acc.astype(out_dtype)

    ins, specs, alias = [a, b], [a_spec, b_spec], {}
    if residual is not None:
        ins.append(residual)
        specs.append(r_spec)
    if into is not None:
        alias = {len(ins): 0}
        ins.append(into)
        specs.append(pl.BlockSpec(memory_space=pl.ANY))
    ins += list(after)
    specs += [pl.BlockSpec(memory_space=pl.ANY)] * len(after)
    return pl.pallas_call(
        body, name=name, grid=(N // tn, M // tm), in_specs=specs, out_specs=o_spec,
        out_shape=jax.ShapeDtypeStruct(o_shape, out_dtype), input_output_aliases=alias,
        compiler_params=_cparams(("parallel", "parallel"), V7X_VMEM_LIMIT),
    )(*ins)


def _mm_nt_blocks(a_list, b4, name, after=()):
    M = a_list[0].shape[0]
    nb, N, Kb = b4.shape
    tm, tn = _pick_tile(M, 1024), _pick_tile(N, 512)

    def body(a0_ref, a1_ref, b_ref, *rest):
        o_ref = rest[-1]
        acc = None
        for blk in range(nb):
            a_ref = (a0_ref, a1_ref)[blk // 2]
            lo = (blk % 2) * Kb
            t = lax.dot_general(a_ref[:, lo:lo + Kb].astype(_MXU), b_ref[blk].astype(_MXU), (((1,), (1,)), ((), ())),
                                preferred_element_type=F32)
            acc = t if acc is None else acc + t
        o_ref[...] = acc

    a_spec = pl.BlockSpec((tm, 2 * Kb), lambda j, i: (i, 0))
    return pl.pallas_call(
        body, name=name, grid=(N // tn, M // tm),
        in_specs=[a_spec, a_spec, pl.BlockSpec((nb, tn, Kb), lambda j, i: (0, j, 0))]
        + [pl.BlockSpec(memory_space=pl.ANY)] * len(after),
        out_specs=pl.BlockSpec((tm, tn), lambda j, i: (i, j)), out_shape=jax.ShapeDtypeStruct((M, N), F32),
        compiler_params=_cparams(("parallel", "parallel"), V7X_VMEM_LIMIT),
    )(a_list[0], a_list[1], b4, *after)


def _wp_assemble(g_in, after=()):
    nb, Dm, Wb = g_in.shape
    T = 256
    n_lo = P_QKVB - 2 * Wb

    def body(g_ref, *rest):
        g2 = g_ref[2]
        rest[-1][...] = jnp.concatenate(
            [g_ref[0], g_ref[1], g2[:, :n_lo], g2[:, n_lo + 8:], g_ref[3], g2[:, n_lo:n_lo + 8],
             jnp.zeros((T, P_COLS - P_BA - 8), g_in.dtype)], axis=1)

    return pl.pallas_call(
        body, name="wp_assemble", grid=(Dm // T,),
        in_specs=[pl.BlockSpec((nb, T, Wb), lambda i: (0, i, 0))] + [pl.BlockSpec(memory_space=pl.ANY)] * len(after),
        out_specs=pl.BlockSpec((T, P_COLS), lambda i: (i, 0)), out_shape=jax.ShapeDtypeStruct((Dm, P_COLS), g_in.dtype),
        compiler_params=_cparams(("parallel",)),
    )(g_in, *after)


def _win_split(d_wp):
    Dm = d_wp.shape[0]
    Wb = IN_COLS // N_CHIPS
    T = 256

    def body(x_ref, o_ref):
        xv = x_ref[...]
        o_ref[0] = xv[:, 0:Wb]
        o_ref[1] = xv[:, Wb:2 * Wb]
        o_ref[2] = jnp.concatenate([xv[:, 2 * Wb:P_QKVB], xv[:, P_BA:P_BA + 8], xv[:, P_QKVB:3 * Wb - 8]], axis=1)
        o_ref[3] = xv[:, 3 * Wb - 8:P_BA]

    return pl.pallas_call(
        body, name="win_split", grid=(Dm // T,), in_specs=[pl.BlockSpec((T, P_COLS), lambda i: (i, 0))],
        out_specs=pl.BlockSpec((N_CHIPS, T, Wb), lambda i: (0, i, 0)),
        out_shape=jax.ShapeDtypeStruct((N_CHIPS, Dm, Wb), F32), compiler_params=_cparams(("parallel",)),
    )(d_wp)


def _rmsnorm_fwd(x, w, name, after=()):
    S, D = x.shape
    T = _pick_tile(S, 512)

    def body(x_ref, w_ref, *rest):
        xv = x_ref[...]
        rs = lax.rsqrt(jnp.mean(xv * xv, axis=-1, keepdims=True) + EPS)
        rest[-1][...] = (xv * rs * w_ref[...]).astype(rest[-1].dtype)

    return pl.pallas_call(
        body, name=name, grid=(S // T,),
        in_specs=[pl.BlockSpec((T, D), lambda i: (i, 0)), pl.BlockSpec((1, D), lambda i: (0, 0))] + [_ANY] * len(after),
        out_specs=pl.BlockSpec((T, D), lambda i: (i, 0)),
        out_shape=jax.ShapeDtypeStruct((S, D), _MXU),
        compiler_params=_cparams(("parallel",)),
    )(x, w, *after)


def _rmsnorm_bwd(dh, x, w, dres, name, after=()):
    S, D = x.shape
    T = _pick_tile(S, 512)

    def body(dh_ref, x_ref, w_ref, dres_ref, *rest):
        dx_ref, dw_ref = rest[-2:]
        xv = x_ref[...]
        rs = lax.rsqrt(jnp.mean(xv * xv, axis=-1, keepdims=True) + EPS)
        xn = xv * rs
        dhv = dh_ref[...]
        dxn = dhv * w_ref[...]
        dx_ref[...] = dres_ref[...] + rs * (dxn - xn * jnp.mean(dxn * xn, axis=-1, keepdims=True))

        @pl.when(pl.program_id(0) == 0)
        def _():
            dw_ref[...] = jnp.zeros_like(dw_ref)

        dw_ref[...] += jnp.sum(dhv * xn, axis=0, keepdims=True)

    row = pl.BlockSpec((T, D), lambda i: (i, 0))
    vec = pl.BlockSpec((1, D), lambda i: (0, 0))
    return pl.pallas_call(
        body, name=name, grid=(S // T,), in_specs=[row, row, vec, row] + [_ANY] * len(after), out_specs=(row, vec),
        out_shape=(jax.ShapeDtypeStruct((S, D), F32), jax.ShapeDtypeStruct((1, D), F32)),
        compiler_params=_cparams(("arbitrary",)),
    )(dh, x, w, dres, *after)


def _loss_head(x3, w, tgt, name):
    S, D = x3.shape
    T = _pick_tile(S, 512)

    def body(x_ref, w_ref, t_ref, loss_ref, dx_ref, dxn_ref, dw_ref):
        xv = x_ref[...]
        rs = lax.rsqrt(jnp.mean(xv * xv, axis=-1, keepdims=True) + EPS)
        xn = xv * rs
        err = xn * w_ref[...] - t_ref[...]
        dy = err * (1.0 / D)
        dxn = dy * w_ref[...]
        dxv = rs * (dxn - xn * jnp.mean(dxn * xn, axis=-1, keepdims=True))
        dx_ref[...] = dxv
        dxn_ref[...] = dxv.astype(dxn_ref.dtype)

        @pl.when(pl.program_id(0) == 0)
        def _():
            dw_ref[...] = jnp.zeros_like(dw_ref)
            loss_ref[...] = jnp.zeros_like(loss_ref)

        dw_ref[...] += jnp.sum(dy * xn, axis=0, keepdims=True)
        part = jnp.sum(jnp.sum(err * err, axis=-1, keepdims=True), axis=0, keepdims=True) * (0.5 / D)
        loss_ref[...] += jnp.broadcast_to(part, loss_ref.shape)

    row = pl.BlockSpec((T, D), lambda i: (i, 0))
    vec = pl.BlockSpec((1, D), lambda i: (0, 0))
    return pl.pallas_call(
        body, name=name, grid=(S // T,), in_specs=[row, vec, row],
        out_specs=(pl.BlockSpec((8, 128), lambda i: (0, 0)), row, row, vec),
        out_shape=(jax.ShapeDtypeStruct((8, 128), F32), jax.ShapeDtypeStruct((S, D), F32), jax.ShapeDtypeStruct((S, D), _MXU),
                   jax.ShapeDtypeStruct((1, D), F32)),
        compiler_params=_cparams(("arbitrary",)),
    )(x3, w, tgt)


def _shifted(ext, back, lo, n):
    if back == 0:
        return ext[lo:lo + n, :]
    return pltpu.roll(ext, back % ext.shape[0], 0)[lo:lo + n, :]


def _conv_windows(ext, K, T):
    return [_shifted(ext, (K - 1) - i, 8, T) for i in range(K)]


def _conv_taps(ext, w, K, T):
    out = None
    for i, win in enumerate(_conv_windows(ext, K, T)):
        term = win * w[i:i + 1, :]
        out = term if out is None else out + term
    return out


def _conv_taps_t(ext, w, K, T):
    out = None
    for i in range(K):
        term = _shifted(ext, i - (K - 1), 0, T) * w[i:i + 1, :]
        out = term if out is None else out + term
    return out


def _tri_masks(C):
    r = lax.broadcasted_iota(jnp.int32, (C, C), 0)
    c = lax.broadcasted_iota(jnp.int32, (C, C), 1)
    return r == c, r >= c, r > c, r <= c


_NN, _NT, _TN = ((1,), (0,)), ((1,), (1,)), ((0,), (0,))
_GDN_PASSES = dict(qk=1, inv=1, sol=1, scan=1, bwd=1)


def _bdot_raw(a, b, kind, passes):
    dims = ({"NN": ((2,), (1,)), "NT": ((2,), (2,)), "TN": ((1,), (1,))}[kind], ((0,), (0,)))
    if passes == 0:
        return lax.dot_general(a, b, dims, precision=_HI, preferred_element_type=F32)
    ah, bh = a.astype(BF16), b.astype(BF16)
    out = lax.dot_general(ah, bh, dims, preferred_element_type=F32)
    if passes == 3:
        al, bl = (a - ah.astype(F32)).astype(BF16), (b - bh.astype(F32)).astype(BF16)
        out = out + lax.dot_general(ah, bl, dims, preferred_element_type=F32) + lax.dot_general(al, bh, dims, preferred_element_type=F32)
    return out


@functools.partial(jax.custom_vjp, nondiff_argnums=(2, 3))
def _bdot(a, b, kind, passes):
    return _bdot_raw(a, b, kind, passes)


def _bdot_fwd(a, b, kind, passes):
    return _bdot_raw(a, b, kind, passes), (a, b)


def _bdot_bwd(kind, passes, res, ct):
    a, b = res
    if kind == "NN":
        return _bdot_raw(ct, b, "NT", passes), _bdot_raw(a, ct, "TN", passes)
    if kind == "NT":
        return _bdot_raw(ct, b, "NN", passes), _bdot_raw(ct, a, "TN", passes)
    return _bdot_raw(b, ct, "NT", passes), _bdot_raw(a, ct, "NN", passes)


_bdot.defvjp(_bdot_fwd, _bdot_bwd)


def _softplus(x):
    return jnp.maximum(x, 0.0) + jnp.log(1.0 + jnp.exp(-jnp.abs(x)))


def _gdn_stage1(cq, ck, cv, b_col, a_col, alog, dtb, dot=_bdot_raw):
    C = cq.shape[1]
    eye, incl, strict, incl_t = _tri_masks(C)
    qn = cq * lax.rsqrt(jnp.sum(cq * cq, axis=-1, keepdims=True) + EPS) * (GDN_DIM ** -0.5)
    kn = ck * lax.rsqrt(jnp.sum(ck * ck, axis=-1, keepdims=True) + EPS)
    beta = jax.nn.sigmoid(b_col)
    g = -jnp.exp(alog) * _softplus(a_col + dtb)
    g_row = jnp.sum(jnp.where(eye, g, 0.0), axis=1, keepdims=True)
    beta_row = jnp.sum(jnp.where(eye, beta, 0.0), axis=1, keepdims=True)
    gc_col = jnp.sum(jnp.where(incl, g_row, 0.0), axis=2, keepdims=True)
    gc_row = jnp.sum(jnp.where(incl_t, g, 0.0), axis=1, keepdims=True)
    dec = jnp.where(incl, jnp.exp(jnp.where(incl, gc_col - gc_row, 0.0)), 0.0)
    kk = dot(kn, kn, "NT", _GDN_PASSES["qk"])
    qk = dot(qn, kn, "NT", _GDN_PASSES["qk"])
    lmat = jnp.where(strict, dec * kk * beta_row, 0.0)
    attn = dec * qk * beta_row
    gam = jnp.exp(gc_col)
    gc_last = gc_col[:, C - 1:C, :]
    k_end = kn * (jnp.exp(gc_last - gc_col) * beta)
    return lmat, cv, gam * kn, gam * qn, attn, k_end, jnp.exp(gc_last)


def _tri_inv(lmat):
    C = lmat.shape[1]
    eye = _tri_masks(C)[0]
    ps = _GDN_PASSES["inv"]
    p = jnp.where(eye, 1.0, 0.0) - lmat
    lp = _bdot_raw(lmat, lmat, "NN", ps)
    n = int(math.log2(C))
    for s in range(1, n):
        p = p + _bdot_raw(p, lp, "NN", ps)
        if s < n - 1:
            lp = _bdot_raw(lp, lp, "NN", ps)
    return p


def _gated_norm(o, z, gnw):
    on = o * lax.rsqrt(jnp.mean(o * o, axis=-1, keepdims=True) + EPS) * gnw
    return on * _silu(z)


GDN_PG = 4
GDN_SG = 4


def _gdn_pairs(c, ba, gp, G):
    C, W, H = GDN_CHUNK, GDN_WIDTH, GDN_HEADS
    pairs = [(j, h) for j in range(G) for h in range(H)]
    cq, ck, cv = (jnp.stack([c[C * j:C * (j + 1), o + GDN_DIM * h:o + GDN_DIM * (h + 1)] for j, h in pairs]) for o in (0, W, 2 * W))
    b_col = jnp.stack([ba[C * j:C * (j + 1), h:h + 1] for j, h in pairs])
    a_col = jnp.stack([ba[C * j:C * (j + 1), H + h:H + h + 1] for j, h in pairs])
    alog = jnp.stack([gp[0:1, h:h + 1] for j, h in pairs])
    dtb = jnp.stack([gp[0:1, H + h:H + h + 1] for j, h in pairs])
    return pairs, (cq, ck, cv, b_col, a_col, alog, dtb)


def _gdn_pre_specs(S, G):
    C = GDN_CHUNK
    T = C * G
    return dict(
        cur=pl.BlockSpec((T, 3 * GDN_WIDTH), lambda i: (i, 0)),
        prev=pl.BlockSpec((8, 3 * GDN_WIDTH), lambda i: (jnp.maximum(i * (T // 8) - 1, 0), 0)),
        ba=pl.BlockSpec((T, 128), lambda i: (i, P_BA // 128)),
        cw=pl.BlockSpec((GDN_CONV, 3 * GDN_WIDTH), lambda i: (0, 0)),
        vec=pl.BlockSpec((1, 128), lambda i: (0, 0)),
        hd=pl.BlockSpec((GDN_HEADS, T, GDN_DIM), lambda i: (0, i, 0)),
        hc=pl.BlockSpec((GDN_HEADS, T, C), lambda i: (0, i, 0)),
        ge=pl.BlockSpec((G, GDN_HEADS, 8, 128), lambda i: (i, 0, 0, 0)),
    )


def _hd_shape(S, last=GDN_DIM):
    return jax.ShapeDtypeStruct((GDN_HEADS, S, last), F32)


def _gdn_pre(proj, conv_w, gp):
    S = proj.shape[0]
    C, G = GDN_CHUNK, GDN_PG
    nc = S // C
    sp = _gdn_pre_specs(S, G)

    def body(cur_ref, prev_ref, ba_ref, cw_ref, gp_ref, uv_ref, wk_ref, qd_ref, ke_ref, at_ref, ti_ref, ge_ref):
        prev = prev_ref[...] * jnp.where(pl.program_id(0) == 0, 0.0, 1.0)
        c = _silu(_conv_taps(jnp.concatenate([prev, cur_ref[...]], axis=0), cw_ref[...], GDN_CONV, C * G))
        pairs, args = _gdn_pairs(c, ba_ref[...], gp_ref[...], G)
        lmat, v, rk, q_dec, attn, k_end, g_end = _gdn_stage1(*args)
        t = _tri_inv(lmat)
        u_v = _bdot_raw(t, v, "NN", _GDN_PASSES["sol"])
        w_k = _bdot_raw(t, rk, "NN", _GDN_PASSES["sol"])
        for b, (j, h) in enumerate(pairs):
            rows = slice(C * j, C * (j + 1))
            uv_ref[h, rows, :] = u_v[b]
            wk_ref[h, rows, :] = w_k[b]
            qd_ref[h, rows, :] = q_dec[b]
            ke_ref[h, rows, :] = k_end[b]
            at_ref[h, rows, :] = attn[b]
            ti_ref[h, rows, :] = t[b]
            ge_ref[j, h] = jnp.broadcast_to(g_end[b], (8, 128))

    return pl.pallas_call(
        body, name="gdn_pre", grid=(nc // G,),
        in_specs=[sp["cur"], sp["prev"], sp["ba"], sp["cw"], sp["vec"]],
        out_specs=(sp["hd"], sp["hd"], sp["hd"], sp["hd"], sp["hc"], sp["hc"], sp["ge"]),
        out_shape=(_hd_shape(S), _hd_shape(S), _hd_shape(S), _hd_shape(S), _hd_shape(S, C), _hd_shape(S, C),
                   jax.ShapeDtypeStruct((nc, GDN_HEADS, 8, 128), F32)),
        compiler_params=_cparams(("parallel",)),
    )(proj, proj, proj, conv_w, gp)


def _gdn_scan_specs(S, G, rev):
    C = GDN_CHUNK
    T = C * G
    n = S // T
    ci = (lambda i: n - 1 - i) if rev else (lambda i: i)
    return dict(
        hd=pl.BlockSpec((GDN_HEADS, T, GDN_DIM), lambda i: (0, ci(i), 0)),
        hc=pl.BlockSpec((GDN_HEADS, T, C), lambda i: (0, ci(i), 0)),
        ge=pl.BlockSpec((G, GDN_HEADS, 8, 128), lambda i: (ci(i), 0, 0, 0)),
        z=pl.BlockSpec((T, GDN_WIDTH), lambda i: (ci(i), P_Z // GDN_WIDTH)),
        oa=pl.BlockSpec((T, GDN_WIDTH), lambda i: (ci(i), 0)),
        vec=pl.BlockSpec((1, 128), lambda i: (0, 0)),
        st=pl.BlockSpec((G, GDN_HEADS, GDN_DIM, GDN_DIM), lambda i: (ci(i), 0, 0, 0)),
    )


def _gdn_scan(u_v, w_k, q_dec, k_end, attn, g_end, proj, gnw, mix, after=()):
    S = proj.shape[0]
    C, G = GDN_CHUNK, GDN_SG
    nc = S // C
    sp = _gdn_scan_specs(S, G, False)
    ps = _GDN_PASSES["scan"]

    def body(uv_ref, wk_ref, qd_ref, ke_ref, at_ref, ge_ref, z_ref, gnw_ref, *rest):
        oa_ref, st_ref, s_scr = rest[-3:]

        @pl.when(pl.program_id(0) == 0)
        def _():
            s_scr[...] = jnp.zeros_like(s_scr)

        for j in range(G):
            rows = slice(C * j, C * (j + 1))
            st = s_scr[...]
            st_ref[j] = st
            u = uv_ref[:, rows, :] - _bdot_raw(wk_ref[:, rows, :], st, "NN", ps)
            o = _bdot_raw(qd_ref[:, rows, :], st, "NN", ps) + _bdot_raw(at_ref[:, rows, :], u, "NN", ps)
            s_scr[...] = ge_ref[j][:, 0:1, 0:1] * st + _bdot_raw(ke_ref[:, rows, :], u, "TN", ps)
            for h in range(GDN_HEADS):
                cols = slice(GDN_DIM * h, GDN_DIM * (h + 1))
                oa_ref[rows, cols] = _gated_norm(o[h], z_ref[rows, cols], gnw_ref[...])

    return pl.pallas_call(
        body, name="gdn_scan", grid=(nc // G,),
        in_specs=[sp["hd"], sp["hd"], sp["hd"], sp["hd"], sp["hc"], sp["ge"], sp["z"], sp["vec"]] + [_ANY] * (1 + len(after)),
        out_specs=(sp["oa"], sp["st"]),
        out_shape=(jax.ShapeDtypeStruct(mix.shape, F32),
                   jax.ShapeDtypeStruct((nc, GDN_HEADS, GDN_DIM, GDN_DIM), F32)),
        input_output_aliases={8: 0},
        scratch_shapes=[pltpu.VMEM((GDN_HEADS, GDN_DIM, GDN_DIM), F32)],
        compiler_params=_cparams(("arbitrary",)),
    )(u_v, w_k, q_dec, k_end, attn, g_end, proj, gnw, mix, *after)


def _gdn_scan_bwd(u_v, w_k, q_dec, k_end, attn, g_end, proj, gnw, states, d_oa):
    S = proj.shape[0]
    C, G = GDN_CHUNK, GDN_SG
    nc = S // C
    sp = _gdn_scan_specs(S, G, True)
    ps, pb = _GDN_PASSES["scan"], _GDN_PASSES["bwd"]

    def body(uv_ref, wk_ref, qd_ref, ke_ref, at_ref, ge_ref, z_ref, gnw_ref, st_ref, doa_ref,
             duv_ref, dwk_ref, dqd_ref, dke_ref, dat_ref, dge_ref, dz_ref, dgnw_ref, ds_scr):
        @pl.when(pl.program_id(0) == 0)
        def _():
            ds_scr[...] = jnp.zeros_like(ds_scr)
            dgnw_ref[...] = jnp.zeros_like(dgnw_ref)

        dgnw = jnp.zeros((1, 128), F32)
        for j in reversed(range(G)):
            rows = slice(C * j, C * (j + 1))
            st = st_ref[j]
            wk, qd, ke, at = wk_ref[:, rows, :], qd_ref[:, rows, :], ke_ref[:, rows, :], at_ref[:, rows, :]
            u = uv_ref[:, rows, :] - _bdot_raw(wk, st, "NN", ps)
            o = _bdot_raw(qd, st, "NN", ps) + _bdot_raw(at, u, "NN", ps)
            dos = []
            for h in range(GDN_HEADS):
                cols = slice(GDN_DIM * h, GDN_DIM * (h + 1))
                _, vjp2 = jax.vjp(_gated_norm, o[h], z_ref[rows, cols], gnw_ref[...])
                do_h, dz_h, dgn = vjp2(doa_ref[rows, cols])
                dz_ref[rows, cols] = dz_h
                dgnw = dgnw + dgn
                dos.append(do_h)
            do = jnp.stack(dos)
            ds_new = ds_scr[...]
            du = _bdot_raw(at, do, "TN", pb) + _bdot_raw(ke, ds_new, "NN", pb)
            duv_ref[:, rows, :] = du
            dat_ref[:, rows, :] = _bdot_raw(do, u, "NT", pb)
            dqd_ref[:, rows, :] = _bdot_raw(do, st, "NT", pb)
            dke_ref[:, rows, :] = _bdot_raw(u, ds_new, "NT", pb)
            dwk_ref[:, rows, :] = -_bdot_raw(du, st, "NT", pb)
            d_ge = jnp.sum(jnp.sum(st * ds_new, axis=2, keepdims=True), axis=1, keepdims=True)
            dge_ref[j] = jnp.broadcast_to(d_ge, (GDN_HEADS, 8, 128))
            ds_scr[...] = ge_ref[j][:, 0:1, 0:1] * ds_new + _bdot_raw(qd, do, "TN", pb) - _bdot_raw(wk, du, "TN", pb)
        dgnw_ref[...] += dgnw

    return pl.pallas_call(
        body, name="gdn_scan_bwd", grid=(nc // G,),
        in_specs=[sp["hd"], sp["hd"], sp["hd"], sp["hd"], sp["hc"], sp["ge"], sp["z"], sp["vec"], sp["st"], sp["oa"]],
        out_specs=(sp["hd"], sp["hd"], sp["hd"], sp["hd"], sp["hc"], sp["ge"], sp["oa"], sp["vec"]),
        out_shape=(_hd_shape(S), _hd_shape(S), _hd_shape(S), _hd_shape(S), _hd_shape(S, C),
                   jax.ShapeDtypeStruct((nc, GDN_HEADS, 8, 128), F32), jax.ShapeDtypeStruct((S, GDN_WIDTH), F32),
                   jax.ShapeDtypeStruct((1, 128), F32)),
        scratch_shapes=[pltpu.VMEM((GDN_HEADS, GDN_DIM, GDN_DIM), F32)],
        compiler_params=_cparams(("arbitrary",)),
    )(u_v, w_k, q_dec, k_end, attn, g_end, proj, gnw, states, d_oa)


def _gdn_post(proj, conv_w, gp, tinv, u_v, w_k, d_uv, d_wk, d_qd, d_ke, d_at, d_ge):
    S = proj.shape[0]
    C, G = GDN_CHUNK, GDN_PG
    nc = S // C
    sp = _gdn_pre_specs(S, G)
    pb = _GDN_PASSES["bwd"]

    def body(cur_ref, prev_ref, ba_ref, cw_ref, gp_ref, ti_ref, uv_ref, wk_ref, duv_ref, dwk_ref, dqd_ref, dke_ref,
             dat_ref, dge_ref, dpre_ref, dba_ref, dgp_ref):
        i = pl.program_id(0)

        @pl.when(i == 0)
        def _():
            dgp_ref[...] = jnp.zeros_like(dgp_ref)

        prev = prev_ref[...] * jnp.where(i == 0, 0.0, 1.0)
        pre = _conv_taps(jnp.concatenate([prev, cur_ref[...]], axis=0), cw_ref[...], GDN_CONV, C * G)
        sg = jax.nn.sigmoid(pre)
        dsilu = sg * (1.0 + pre * (1.0 - sg))
        pairs, args = _gdn_pairs(pre * sg, ba_ref[...], gp_ref[...], G)
        _, vjp1 = jax.vjp(functools.partial(_gdn_stage1, dot=_bdot), *args)

        def take(ref):
            return jnp.stack([ref[h, C * j:C * (j + 1), :] for j, h in pairs])

        t, u_v, w_k = take(ti_ref), take(uv_ref), take(wk_ref)
        d_v = _bdot_raw(t, take(duv_ref), "TN", pb)
        d_rk = _bdot_raw(t, take(dwk_ref), "TN", pb)
        d_l = -(_bdot_raw(d_v, u_v, "NT", pb) + _bdot_raw(d_rk, w_k, "NT", pb))
        d_ge = jnp.stack([dge_ref[j, h][0:1, 0:1] for j, h in pairs])
        dcq, dck, dcv, db, da, dalog, ddtb = vjp1((d_l, d_v, d_rk, take(dqd_ref), take(dat_ref), take(dke_ref), d_ge))
        lane = lax.broadcasted_iota(jnp.int32, (C, 128), 1)
        lane1 = lax.broadcasted_iota(jnp.int32, (1, 128), 1)
        dgp = jnp.zeros((1, 128), F32)
        for j in range(G):
            rows = slice(C * j, C * (j + 1))
            dba = jnp.zeros((C, 128), F32)
            for h in range(GDN_HEADS):
                b = GDN_HEADS * j + h
                for o_, dcx in ((0, dcq), (GDN_WIDTH, dck), (2 * GDN_WIDTH, dcv)):
                    cols = slice(o_ + GDN_DIM * h, o_ + GDN_DIM * (h + 1))
                    dpre_ref[rows, cols] = dcx[b] * dsilu[rows, cols]
                dba = dba + jnp.where(lane == h, db[b], 0.0) + jnp.where(lane == GDN_HEADS + h, da[b], 0.0)
                dgp = dgp + jnp.where(lane1 == h, dalog[b], 0.0) + jnp.where(lane1 == GDN_HEADS + h, ddtb[b], 0.0)
            dba_ref[rows, :] = dba
        dgp_ref[0:1, :] += dgp

    T = C * G
    return pl.pallas_call(
        body, name="gdn_post", grid=(nc // G,),
        in_specs=[sp["cur"], sp["prev"], sp["ba"], sp["cw"], sp["vec"], sp["hc"], sp["hd"], sp["hd"], sp["hd"], sp["hd"],
                  sp["hd"], sp["hd"], sp["hc"], sp["ge"]],
        out_specs=(sp["cur"], pl.BlockSpec((T, 128), lambda i: (i, 0)), pl.BlockSpec((8, 128), lambda i: (0, 0))),
        out_shape=(jax.ShapeDtypeStruct((S, 3 * GDN_WIDTH), F32), jax.ShapeDtypeStruct((S, 128), F32),
                   jax.ShapeDtypeStruct((8, 128), F32)),
        compiler_params=_cparams(("arbitrary",)),
    )(proj, proj, proj, conv_w, gp, tinv, u_v, w_k, d_uv, d_wk, d_qd, d_ke, d_at, d_ge)


def _conv_bwd(dpre, x, xcol0, w, K, name, tc):
    S, Cc = dpre.shape
    T = _pick_tile(S, 256)
    nt, ncol = S // T, Cc // tc
    xo = xcol0 // tc

    def body(d_ref, dn_ref, x_ref, xp_ref, w_ref, dx_ref, dw_ref):
        i = pl.program_id(1)
        dn = dn_ref[...] * jnp.where(i == nt - 1, 0.0, 1.0)
        dv = d_ref[...]
        ext_d = jnp.concatenate([dv, dn], axis=0)
        dx_ref[...] = _conv_taps_t(ext_d, w_ref[...], K, T).astype(dx_ref.dtype)
        xp = xp_ref[...] * jnp.where(i == 0, 0.0, 1.0)
        ext_x = jnp.concatenate([xp, x_ref[...]], axis=0)

        @pl.when(i == 0)
        def _():
            dw_ref[...] = jnp.zeros_like(dw_ref)

        for k in range(K):
            dw_ref[k:k + 1, :] += jnp.sum(dv * _shifted(ext_x, (K - 1) - k, 8, T), axis=0, keepdims=True)

    r8 = T // 8
    return pl.pallas_call(
        body, name=name, grid=(ncol, nt),
        in_specs=[pl.BlockSpec((T, tc), lambda j, i: (i, j)),
                  pl.BlockSpec((8, tc), lambda j, i: (jnp.minimum((i + 1) * r8, S // 8 - 1), j)),
                  pl.BlockSpec((T, tc), lambda j, i: (i, j + xo)),
                  pl.BlockSpec((8, tc), lambda j, i: (jnp.maximum(i * r8 - 1, 0), j + xo)),
                  pl.BlockSpec((K, tc), lambda j, i: (0, j))],
        out_specs=(pl.BlockSpec((T, tc), lambda j, i: (i, j)), pl.BlockSpec((K, tc), lambda j, i: (0, j))),
        out_shape=(jax.ShapeDtypeStruct((S, Cc), _MXU), jax.ShapeDtypeStruct((K, Cc), F32)),
        compiler_params=_cparams(("parallel", "arbitrary")),
    )(dpre, dpre, x, x, w)


def _dil_bias(nt, T):
    d = (np.arange(nt)[:, None, None] * T + np.arange(T)[None, None, :] - np.arange(T)[None, :, None])
    cnt = ((d >= 0) & (d <= 128)).astype(np.float64) + ((d >= 0) & (d % 4 == 0) & (d <= 512)) + ((d >= 0) & (d % 16 == 0))
    return jnp.asarray(np.where(cnt > 0, np.log(np.maximum(cnt, 1.0)), -1e30), dtype=F32)


def _attn_fwd(proj, after=()):
    S = proj.shape[0]
    T = min(ATT_T, S)
    nt = S // T
    bias = _dil_bias(nt, T)
    scale = DIL_DIM ** -0.5
    npair = DIL_WIDTH // 128
    qb0, kb0, vb0 = P_QKVB // 128, (P_QKVB + DIL_WIDTH) // 128, (P_QKVB + 2 * DIL_WIDTH) // 128

    def body(q_ref, k_ref, v_ref, b_ref, *rest):
        o_ref, lse_ref = rest[-2:]
        i = pl.program_id(1)
        qs = (q_ref[...] * scale).astype(_MXU)

        def step(j, carry):
            kt = k_ref[pl.ds(pl.multiple_of(j * T, T), T), :].astype(_MXU)
            vt = v_ref[pl.ds(pl.multiple_of(j * T, T), T), :].astype(_MXU)
            bt = b_ref[i - j]
            out = []
            for hh in range(2):
                m, l, acc = carry[hh]
                sl = slice(hh * DIL_DIM, (hh + 1) * DIL_DIM)
                s = lax.dot_general(kt[:, sl], qs[:, sl], (_NT, ((), ())), preferred_element_type=F32) + bt
                m_new = jnp.maximum(m, jnp.max(s, axis=0, keepdims=True))
                p = jnp.exp(s - m_new)
                a = jnp.exp(m - m_new)
                l = a * l + jnp.sum(p, axis=0, keepdims=True)
                acc = a * acc + lax.dot_general(vt[:, sl], p.astype(_MXU), (_TN, ((), ())), preferred_element_type=F32)
                out.append((m_new, l, acc))
            return tuple(out)

        init = tuple((jnp.full((1, T), -1e30, F32), jnp.zeros((1, T), F32), jnp.zeros((DIL_DIM, T), F32)) for _ in range(2))
        res = lax.fori_loop(0, i + 1, step, init)
        lse_ref[...] = jnp.zeros_like(lse_ref)
        for hh in range(2):
            m, l, acc = res[hh]
            o_ref[:, hh * DIL_DIM:(hh + 1) * DIL_DIM] = (acc / l).T
            lse_ref[hh:hh + 1, :] = m + jnp.log(l)

    return pl.pallas_call(
        body, name="attn_fwd", grid=(npair, nt),
        in_specs=[pl.BlockSpec((T, 128), lambda p, i: (i, qb0 + p)),
                  pl.BlockSpec((S, 128), lambda p, i: (0, kb0 + p)),
                  pl.BlockSpec((S, 128), lambda p, i: (0, vb0 + p)),
                  pl.BlockSpec((nt, T, T), lambda p, i: (0, 0, 0))] + [_ANY] * len(after),
        out_specs=(pl.BlockSpec((T, 128), lambda p, i: (i, GDN_WIDTH // 128 + p)),
                   pl.BlockSpec((None, None, 8, T), lambda p, i: (p, i, 0, 0))),
        out_shape=(jax.ShapeDtypeStruct((S, GDN_WIDTH + DIL_WIDTH), F32), jax.ShapeDtypeStruct((npair, nt, 8, T), F32)),
        compiler_params=_cparams(("parallel", "parallel")),
    )(proj, proj, proj, bias, *after)


def _attn_bwd(proj, mix, lse, d_mix):
    S = proj.shape[0]
    T = min(ATT_T, S)
    nt = S // T
    bias = _dil_bias(nt, T)
    scale = DIL_DIM ** -0.5
    npair = DIL_WIDTH // 128
    qb0, kb0, vb0 = P_QKVB // 128, (P_QKVB + DIL_WIDTH) // 128, (P_QKVB + 2 * DIL_WIDTH) // 128

    def body(q_ref, k_ref, v_ref, o_ref, lse_ref, do_ref, b_ref, dq_ref, dk_ref, dv_ref, dq_scr):
        j = pl.program_id(1)

        @pl.when(j == 0)
        def _():
            dq_scr[...] = jnp.zeros_like(dq_scr)

        kt = k_ref[...].astype(_MXU)
        vt = v_ref[...].astype(_MXU)
        ones = jnp.ones((8, DIL_DIM), F32)

        def step(i, carry):
            rows = pl.ds(pl.multiple_of(i * T, T), T)
            qs = (q_ref[rows, :] * scale).astype(_MXU)
            dov = do_ref[rows, :]
            prod = dov * o_ref[rows, :]
            lsev = lse_ref[i]
            dob = dov.astype(_MXU)
            bt = b_ref[i - j]
            out = []
            dqs = []
            for hh in range(2):
                dk, dv = carry[hh]
                sl = slice(hh * DIL_DIM, (hh + 1) * DIL_DIM)
                s = lax.dot_general(kt[:, sl], qs[:, sl], (_NT, ((), ())), preferred_element_type=F32) + bt
                p = jnp.exp(s - lsev[hh:hh + 1, :])
                delta = lax.dot_general(ones, prod[:, sl], (_NT, ((), ())), precision=_HI, preferred_element_type=F32)[0:1, :]
                dp = lax.dot_general(vt[:, sl], dob[:, sl], (_NT, ((), ())), preferred_element_type=F32)
                ds = (p * (dp - delta)).astype(_MXU)
                dv = dv + lax.dot_general(p.astype(_MXU), dob[:, sl], (_NN, ((), ())), preferred_element_type=F32)
                dk = dk + lax.dot_general(ds, qs[:, sl], (_NN, ((), ())), preferred_element_type=F32)
                dqs.append(lax.dot_general(ds, kt[:, sl], (_TN, ((), ())), preferred_element_type=F32) * scale)
                out.append((dk, dv))
            dq_scr[rows, :] += jnp.concatenate(dqs, axis=1)
            return tuple(out)

        init = tuple((jnp.zeros((T, DIL_DIM), F32), jnp.zeros((T, DIL_DIM), F32)) for _ in range(2))
        res = lax.fori_loop(j, nt, step, init)
        dk_ref[...] = jnp.concatenate([res[0][0], res[1][0]], axis=1).astype(dk_ref.dtype)
        dv_ref[...] = jnp.concatenate([res[0][1], res[1][1]], axis=1).astype(dv_ref.dtype)

        @pl.when(j == nt - 1)
        def _():
            dq_ref[...] = dq_scr[...].astype(dq_ref.dtype)

    full = lambda c0: pl.BlockSpec((S, 128), lambda p, j: (0, c0 + p))
    tile = lambda c0: pl.BlockSpec((T, 128), lambda p, j: (j, c0 + p))
    out3 = jax.ShapeDtypeStruct((S, DIL_WIDTH), _MXU)
    return pl.pallas_call(
        body, name="attn_bwd", grid=(npair, nt),
        in_specs=[full(qb0), tile(kb0), tile(vb0), full(GDN_WIDTH // 128),
                  pl.BlockSpec((None, nt, 8, T), lambda p, j: (p, 0, 0, 0)), full(GDN_WIDTH // 128),
                  pl.BlockSpec((nt, T, T), lambda p, j: (0, 0, 0))],
        out_specs=(full(0), tile(0), tile(0)),
        out_shape=(out3, out3, out3),
        scratch_shapes=[pltpu.VMEM((S, 128), F32)],
        compiler_params=_cparams(("parallel", "arbitrary")),
    )(proj, proj, proj, mix, lse, d_mix, bias)


def _ffn_act(up, cw):
    S, Cc = up.shape[0], up.shape[1] // 2
    T, tc = _pick_tile(S, 256), _pick_tile(Cc, 1536)
    r16 = T // 16
    nct = Cc // tc

    def body(g_ref, gp_ref, u_ref, up_ref, wg_ref, wu_ref, o_ref):
        keep = jnp.where(pl.program_id(1) == 0, 0.0, 1.0)
        cg = _conv_taps(jnp.concatenate([gp_ref[8:16, :].astype(F32) * keep, g_ref[...].astype(F32)], axis=0),
                        wg_ref[...], FFN_CONV, T)
        cu = _conv_taps(jnp.concatenate([up_ref[8:16, :].astype(F32) * keep, u_ref[...].astype(F32)], axis=0),
                        wu_ref[...], FFN_CONV, T)
        o_ref[...] = (_silu(cg) * cu).astype(o_ref.dtype)

    cur = lambda o: pl.BlockSpec((T, tc), lambda j, i: (i, j + o))
    prev = lambda o: pl.BlockSpec((16, tc), lambda j, i: (jnp.maximum(i * r16 - 1, 0), j + o))
    wsp = lambda o: pl.BlockSpec((FFN_CONV, tc), lambda j, i: (0, j + o))
    return pl.pallas_call(
        body, name="ffn_act", grid=(nct, S // T),
        in_specs=[cur(0), prev(0), cur(nct), prev(nct), wsp(0), wsp(nct)], out_specs=cur(0),
        out_shape=jax.ShapeDtypeStruct((S, Cc), _MXU),
        compiler_params=_cparams(("parallel", "parallel")),
    )(up, up, up, up, cw, cw)


def _ffn_act_bwd(d_act, up, cw):
    S, Cc = up.shape[0], up.shape[1] // 2
    T, tc = _pick_tile(S, 256), _pick_tile(Cc, 1536)
    r8, r16 = T // 8, T // 16
    nt = S // T
    nct = Cc // tc
    K = FFN_CONV

    def body(da_ref, dan_ref, g_ref, gp_ref, gn_ref, u_ref, up_ref, un_ref, wg_ref, wu_ref,
             dg_ref, du_ref, dwg_ref, dwu_ref):
        i = pl.program_id(1)
        keep_p = jnp.where(i == 0, 0.0, 1.0)
        keep_n = jnp.where(i == nt - 1, 0.0, 1.0)
        wg, wu = wg_ref[...], wu_ref[...]
        xg = jnp.concatenate([gp_ref[8:16, :].astype(F32) * keep_p, g_ref[...].astype(F32),
                              gn_ref[0:8, :].astype(F32) * keep_n], axis=0)
        xu = jnp.concatenate([up_ref[8:16, :].astype(F32) * keep_p, u_ref[...].astype(F32),
                              un_ref[0:8, :].astype(F32) * keep_n], axis=0)
        cg = _conv_taps(xg, wg, K, T + 8)
        cu = _conv_taps(xu, wu, K, T + 8)
        da = jnp.concatenate([da_ref[...], dan_ref[...] * keep_n], axis=0)
        sg = jax.nn.sigmoid(cg)
        d_cg = da * cu * (sg * (1.0 + cg * (1.0 - sg)))
        d_cu = da * (cg * sg)
        dg_ref[...] = _conv_taps_t(d_cg, wg, K, T).astype(dg_ref.dtype)
        du_ref[...] = _conv_taps_t(d_cu, wu, K, T).astype(du_ref.dtype)

        @pl.when(i == 0)
        def _():
            dwg_ref[...] = jnp.zeros_like(dwg_ref)
            dwu_ref[...] = jnp.zeros_like(dwu_ref)

        for k in range(K):
            dwg_ref[k:k + 1, :] += jnp.sum(d_cg[0:T, :] * _shifted(xg, (K - 1) - k, 8, T), axis=0, keepdims=True)
            dwu_ref[k:k + 1, :] += jnp.sum(d_cu[0:T, :] * _shifted(xu, (K - 1) - k, 8, T), axis=0, keepdims=True)

    cur = lambda o: pl.BlockSpec((T, tc), lambda j, i: (i, j + o))
    prev = lambda o: pl.BlockSpec((16, tc), lambda j, i: (jnp.maximum(i * r16 - 1, 0), j + o))
    nxt = lambda o: pl.BlockSpec((16, tc), lambda j, i: (jnp.minimum((i + 1) * r16, S // 16 - 1), j + o))
    nxt8 = pl.BlockSpec((8, tc), lambda j, i: (jnp.minimum((i + 1) * r8, S // 8 - 1), j))
    wsp = lambda o: pl.BlockSpec((K, tc), lambda j, i: (0, j + o))
    return pl.pallas_call(
        body, name="ffn_act_bwd", grid=(nct, nt),
        in_specs=[cur(0), nxt8, cur(0), prev(0), nxt(0), cur(nct), prev(nct), nxt(nct), wsp(0), wsp(nct)],
        out_specs=(cur(0), cur(0), wsp(0), wsp(0)),
        out_shape=(jax.ShapeDtypeStruct((S, Cc), _MXU), jax.ShapeDtypeStruct((S, Cc), _MXU),
                   jax.ShapeDtypeStruct((K, Cc), F32), jax.ShapeDtypeStruct((K, Cc), F32)),
        compiler_params=_cparams(("parallel", "arbitrary")),
    )(d_act, d_act, up, up, up, up, up, up, cw, cw)


def _local_step(x, tgt, h1, n1w, n2w, fnw, gp, gnw, wp, conv_w, fcw, rest_weights, early_grads):
    proj = _mm(h1, wp, "nn", name="proj")
    u_v, w_k, q_dec, k_end, attn, tinv, g_end = _gdn_pre(proj, conv_w, gp)
    mix, lse = _attn_fwd(proj)
    mix, states = _gdn_scan(u_v, w_k, q_dec, k_end, attn, g_end, proj, gnw, mix, after=[rest_weights[0]([mix])])
    w_out, w_up4, w_down = rest_weights[1]([mix])
    x2 = _mm(mix, w_out, "nn", residual=x, name="outproj")
    h2 = _rmsnorm_fwd(x2, n2w, "norm2")
    up = _mm(h2, w_up4, "nn", b_blocks=True, out_dtype=_MXU, name="up")
    act = _ffn_act(up, fcw)
    x3 = _mm(act, w_down, "nn", residual=x2, name="down")
    loss, dx3, dx3n, d_fnw = _loss_head(x3, fnw, tgt, "loss_head")
    d_act = _mm(dx3n, w_down, "nt", name="d_act")
    d_wdown = _mm(act, dx3n, "tn", name="d_wdown")
    d_upg, d_upu, d_fcwg, d_fcwu = _ffn_act_bwd(d_act, up, fcw)
    d_wup = _mm(h2, d_upg, "tn", place=("blocks", N_CHIPS, 0), tn=w_up4.shape[2], name="d_wgate")
    d_wup = _mm(h2, d_upu, "tn", place=("blocks", N_CHIPS, N_CHIPS // 2), tn=w_up4.shape[2], into=d_wup, name="d_wup")
    token = early_grads[0](d_wup, d_wdown)
    d_h2 = _mm_nt_blocks([d_upg, d_upu], w_up4, "d_h2", after=[token])
    dx2, d_n2w = _rmsnorm_bwd(d_h2, x2, n2w, dx3, "norm2_bwd", after=[token])
    token = early_grads[1](dx2)
    d_mix = _mm(dx2, w_out, "nt", name="d_mix")
    d_wout = _mm(mix, dx2, "tn", name="d_wout")
    dq_b, dk_b, dv_b = _attn_bwd(proj, mix, lse, d_mix)
    d_uv, d_wk, d_qd, d_ke, d_at, d_ge, d_z, d_gnw = _gdn_scan_bwd(u_v, w_k, q_dec, k_end, attn, g_end, proj,
                                                                   gnw + token[0:1, 0:1], states, d_mix)
    d_pre, d_ba, d_gp = _gdn_post(proj, conv_w, gp, tinv, u_v, w_k, d_uv, d_wk, d_qd, d_ke, d_at, d_ge)
    d_qkva, d_convw = _conv_bwd(d_pre, proj, 0, conv_w, GDN_CONV, "gdn_conv_bwd", 512)
    d_proj = jnp.concatenate([d_qkva, d_z.astype(_MXU), dq_b, dk_b, dv_b, d_ba.astype(_MXU),
                              jnp.zeros((x.shape[0], P_COLS - P_BA - 128), _MXU)], axis=1)
    d_wp = _mm(h1, d_proj, "tn", name="d_wp")
    token = early_grads[2](d_wp, d_wout)
    d_h1 = _mm(d_proj, wp, "nt", name="d_h1", after=[token])
    dx, d_n1w = _rmsnorm_bwd(d_h1, x, n1w, dx2, "norm1_bwd", after=[token])
    grads = dict(wp=d_wp, conv_w=d_convw, w_out=d_wout, w_up=d_wup, fcw_g=d_fcwg, fcw_u=d_fcwu, w_down=d_wdown,
                 n1w=d_n1w, n2w=d_n2w, fnw=d_fnw, gp=d_gp, gnw=d_gnw)
    return loss, dx, grads


_HBM = pl.BlockSpec(memory_space=pltpu.HBM)


def _pos():
    return lax.axis_index("x"), lax.axis_index("y"), lax.axis_index("c")


def _other_chips(x, y):
    return [(1 - x, y), (x, 1 - y), (1 - x, 1 - y)]


def _halvable(shape):
    return shape[0] % 32 == 0


def _rows_of_half(shape, half):
    if not _halvable(shape):
        return pl.ds(0, shape[0])
    return pl.ds(pl.multiple_of(half * (shape[0] // 2), 16), shape[0] // 2)


_SEM = pl.BlockSpec(memory_space=pltpu.SEMAPHORE)
_ANY = pl.BlockSpec(memory_space=pl.ANY)
_DATAFLOW = pltpu.SideEffectType.DATAFLOW_SIDE_EFFECTING


def _in_hbm(a):
    return pltpu.with_memory_space_constraint(a, pltpu.HBM)


def _halves_copy(src_refs, land_refs, send_sems, recv_sems, shapes, a, j, block, x, y, c):
    px, py = _other_chips(x, y)[j]
    rows = _rows_of_half(shapes[a], c)
    return pltpu.make_async_remote_copy(
        src_ref=src_refs[a].at[rows, :], dst_ref=land_refs[a].at[block, rows, :], send_sem=send_sems.at[3 * a + j],
        recv_sem=recv_sems.at[3 * a + j], device_id=(px, py, c), device_id_type=MESH)


def _gather_halves_start(shards, after, name):
    n = len(shards)
    shapes = [s.shape for s in shards]

    def body(*refs):
        ins, lands = refs[:n], refs[n:2 * n]
        send_sems, recv_sems = refs[2 * n + 1], refs[2 * n + 2]
        token = refs[-1]
        x, y, c = _pos()
        q = 2 * x + y
        for a in range(n):
            for j in range(3):
                _halves_copy(ins, lands, send_sems, recv_sems, shapes, a, j, q, x, y, c).start()
        token[...] = jnp.zeros_like(token)

    land_shapes = [(N_CHIPS,) + s.shape for s in shards]
    return pl.pallas_call(
        body, name=name,
        out_shape=(pltpu.SemaphoreType.DMA((3 * n,)), pltpu.SemaphoreType.DMA((3 * n,)),
                   *[pltpu.HBM(s.shape, s.dtype) for s in shards],
                   *[pltpu.HBM(ls, s.dtype) for ls, s in zip(land_shapes, shards)],
                   jax.ShapeDtypeStruct((8, 128), F32)),
        in_specs=[_HBM] * (2 * n) + [_ANY],
        out_specs=(_SEM, _SEM, *[_HBM] * (2 * n), pl.BlockSpec(memory_space=pltpu.VMEM)),
        input_output_aliases={a: 2 + a for a in range(2 * n)},
        compiler_params=pltpu.CompilerParams(has_side_effects=_DATAFLOW),
    )(*[_in_hbm(s) for s in shards], *[_in_hbm(lax.empty(ls, s.dtype)) for ls, s in zip(land_shapes, shards)], after)


def _gather_halves_wait(started, after, name):
    send_sems, recv_sems, *thru = started
    n = len(thru) // 2
    shapes = [t.shape for t in thru[:n]]

    def body(*refs):
        ins, lands = refs[:n], refs[n:2 * n]
        send_sems, recv_sems = refs[2 * n], refs[2 * n + 1]
        x, y, c = _pos()
        q = 2 * x + y
        chips = _other_chips(x, y)
        for a in range(n):
            for j, (px, py) in enumerate(chips):
                _halves_copy(ins, lands, send_sems, recv_sems, shapes, a, j, q, x, y, c).wait_send()
                _halves_copy(ins, lands, send_sems, recv_sems, shapes, a, j, 2 * px + py, x, y, c).wait_recv()

    outs = pl.pallas_call(
        body, name=name, out_shape=[pltpu.HBM(t.shape, t.dtype) for t in thru],
        in_specs=[_HBM] * (2 * n) + [_SEM, _SEM] + [_ANY] * len(after), out_specs=[_HBM] * (2 * n),
        input_output_aliases={a: a for a in range(2 * n)},
        compiler_params=pltpu.CompilerParams(has_side_effects=_DATAFLOW),
    )(*thru, send_sems, recv_sems, *after)
    return outs[:n], outs[n:]


def _sibling_fill(gathered, name):
    big = [a for a, g in enumerate(gathered) if _halvable(g.shape[1:])]
    n = len(gathered)

    def body(*refs):
        ins, outs = refs[:n], refs[n:2 * n]
        send_sems, recv_sems = refs[2 * n:]
        x, y, c = _pos()
        chips = _other_chips(x, y)

        def copy(k, j, half):
            a = big[k]
            px, py = chips[j]
            rows = _rows_of_half(gathered[a].shape[1:], half)
            return pltpu.make_async_remote_copy(
                src_ref=ins[a].at[2 * px + py, rows, :], dst_ref=outs[a].at[2 * px + py, rows, :],
                send_sem=send_sems.at[3 * k + j], recv_sem=recv_sems.at[3 * k + j],
                device_id=(x, y, 1 - c), device_id_type=MESH)

        sends = [copy(k, j, c) for k in range(len(big)) for j in range(3)]
        for cp in sends:
            cp.start()
        for k in range(len(big)):
            for j in range(3):
                copy(k, j, 1 - c).wait_recv()
        for cp in sends:
            cp.wait_send()

    return pl.pallas_call(
        body, name=name, in_specs=[_HBM] * n, out_specs=[_HBM] * n,
        out_shape=[jax.ShapeDtypeStruct(g.shape, g.dtype) for g in gathered],
        input_output_aliases={a: a for a in range(n)},
        scratch_shapes=[pltpu.SemaphoreType.DMA((3 * len(big),)), pltpu.SemaphoreType.DMA((3 * len(big),))],
    )(*gathered)


def _fill_copy(refs, send_sems, recv_sems, shapes, a, j, half, x, y, c):
    px, py = _other_chips(x, y)[j]
    rows = _rows_of_half(shapes[a], half)
    return pltpu.make_async_remote_copy(
        src_ref=refs[a].at[2 * px + py, rows, :], dst_ref=refs[a].at[2 * px + py, rows, :],
        send_sem=send_sems.at[3 * a + j], recv_sem=recv_sems.at[3 * a + j],
        device_id=(x, y, 1 - c), device_id_type=MESH)


def _sibling_fill_start(gathered, name):
    n = len(gathered)
    shapes = [g.shape[1:] for g in gathered]

    def body(*refs):
        ins = refs[:n]
        send_sems, recv_sems = refs[n], refs[n + 1]
        token = refs[-1]
        x, y, c = _pos()
        for a in range(n):
            for j in range(3):
                _fill_copy(ins, send_sems, recv_sems, shapes, a, j, c, x, y, c).start()
        token[...] = jnp.zeros_like(token)

    return pl.pallas_call(
        body, name=name,
        out_shape=(pltpu.SemaphoreType.DMA((3 * n,)), pltpu.SemaphoreType.DMA((3 * n,)),
                   *[pltpu.HBM(g.shape, g.dtype) for g in gathered], jax.ShapeDtypeStruct((8, 128), F32)),
        in_specs=[_HBM] * n,
        out_specs=(_SEM, _SEM, *[_HBM] * n, pl.BlockSpec(memory_space=pltpu.VMEM)),
        input_output_aliases={a: 2 + a for a in range(n)},
        compiler_params=pltpu.CompilerParams(has_side_effects=_DATAFLOW),
    )(*[_in_hbm(g) for g in gathered])


def _sibling_fill_wait(started, after, name):
    send_sems, recv_sems, *thru = started
    n = len(thru)
    shapes = [t.shape[1:] for t in thru]

    def body(*refs):
        ins = refs[:n]
        send_sems, recv_sems = refs[n], refs[n + 1]
        x, y, c = _pos()
        for a in range(n):
            for j in range(3):
                _fill_copy(ins, send_sems, recv_sems, shapes, a, j, c, x, y, c).wait_send()
                _fill_copy(ins, send_sems, recv_sems, shapes, a, j, 1 - c, x, y, c).wait_recv()

    return pl.pallas_call(
        body, name=name, out_shape=[pltpu.HBM(t.shape, t.dtype) for t in thru],
        in_specs=[_HBM] * n + [_SEM, _SEM] + [_ANY] * len(after), out_specs=[_HBM] * n,
        input_output_aliases={a: a for a in range(n)},
        compiler_params=pltpu.CompilerParams(has_side_effects=_DATAFLOW),
    )(*thru, send_sems, recv_sems, *after)


def _place_own(shards, gathered, cq, name, carry=()):
    n = len(shards)
    nc = len(carry)
    steps = 4

    def body(cq_ref, *refs):
        for a in range(n):
            refs[2 * n + nc + a][...] = refs[a][...]

    def tile(shape):
        return shape[0] // steps if _halvable(shape) else shape[0]

    in_specs = [pl.BlockSpec((tile(s.shape), s.shape[1]), (lambda i, s_: (i, 0)) if _halvable(s.shape) else (lambda i, s_: (0, 0)))
                for s in shards]
    in_specs += [pl.BlockSpec(memory_space=pl.ANY)] * (n + nc)
    out_specs = [pl.BlockSpec((None, tile(s.shape), s.shape[1]),
                              (lambda i, s_: (s_[1], i, 0)) if _halvable(s.shape) else (lambda i, s_: (s_[1], 0, 0)))
                 for s in shards]
    out_specs += [pl.BlockSpec(memory_space=pl.ANY)] * nc
    gs = pltpu.PrefetchScalarGridSpec(num_scalar_prefetch=1, grid=(steps,), in_specs=in_specs, out_specs=out_specs)
    outs = pl.pallas_call(
        body, name=name, grid_spec=gs,
        out_shape=[jax.ShapeDtypeStruct(g.shape, g.dtype) for g in gathered] + [jax.ShapeDtypeStruct(t.shape, t.dtype) for t in carry],
        input_output_aliases={1 + n + a: a for a in range(n + nc)},
        compiler_params=_cparams(("arbitrary",)),
    )(cq, *shards, *gathered, *carry)
    return (outs[:n], outs[n:]) if nc else outs


def _half_rows(ref, c, rh):
    return ref.at[:, pl.ds(pl.multiple_of(c * rh, 8), rh), :]


def _chips_copy(src_refs, land_refs, send_sems, recv_sems, a, j, x, y, c):
    px, py = _other_chips(x, y)[j]
    return pltpu.make_async_remote_copy(src_ref=src_refs[a].at[2 * px + py], dst_ref=land_refs[a].at[j],
                                        send_sem=send_sems.at[3 * a + j], recv_sem=recv_sems.at[3 * a + j],
                                        device_id=(px, py, c), device_id_type=MESH)


def _grad_chips_start(parts, name):
    n = len(parts)

    def body(*refs):
        ins, lands = refs[:n], refs[n:2 * n]
        send_sems, recv_sems = refs[2 * n], refs[2 * n + 1]
        token = refs[-1]
        x, y, c = _pos()
        for a in range(n):
            for j in range(3):
                _chips_copy(ins, lands, send_sems, recv_sems, a, j, x, y, c).start()
        token[...] = jnp.zeros_like(token)

    land_shapes = [(3,) + p.shape[1:] for p in parts]
    return pl.pallas_call(
        body, name=name,
        out_shape=(pltpu.SemaphoreType.DMA((3 * n,)), pltpu.SemaphoreType.DMA((3 * n,)),
                   *[pltpu.HBM(p.shape, p.dtype) for p in parts],
                   *[pltpu.HBM(ls, p.dtype) for ls, p in zip(land_shapes, parts)],
                   jax.ShapeDtypeStruct((8, 128), F32)),
        in_specs=[_HBM] * (2 * n),
        out_specs=(_SEM, _SEM, *[_HBM] * (2 * n), pl.BlockSpec(memory_space=pltpu.VMEM)),
        input_output_aliases={a: 2 + a for a in range(2 * n)},
        compiler_params=pltpu.CompilerParams(has_side_effects=_DATAFLOW),
    )(*[_in_hbm(p) for p in parts], *[_in_hbm(lax.empty(ls, p.dtype)) for ls, p in zip(land_shapes, parts)])


def _grad_chips_wait(started, after, name):
    send_sems, recv_sems, *thru = started
    n = len(thru) // 2

    def body(*refs):
        ins, lands = refs[:n], refs[n:2 * n]
        send_sems, recv_sems = refs[2 * n], refs[2 * n + 1]
        x, y, c = _pos()
        for a in range(n):
            for j in range(3):
                cp = _chips_copy(ins, lands, send_sems, recv_sems, a, j, x, y, c)
                cp.wait_send()
                cp.wait_recv()

    outs = pl.pallas_call(
        body, name=name, out_shape=[pltpu.HBM(t.shape, t.dtype) for t in thru],
        in_specs=[_HBM] * (2 * n) + [_SEM, _SEM] + [_ANY] * len(after), out_specs=[_HBM] * (2 * n),
        input_output_aliases={a: a for a in range(2 * n)},
        compiler_params=pltpu.CompilerParams(has_side_effects=_DATAFLOW),
    )(*thru, send_sems, recv_sems, *after)
    return outs[n:]


def _sibling_copy(src_refs, land_refs, send_sems, recv_sems, rhs, a, c, x, y):
    return pltpu.make_async_remote_copy(src_ref=_half_rows(src_refs[a], 1 - c, rhs[a]), dst_ref=land_refs[a],
                                        send_sem=send_sems.at[a], recv_sem=recv_sems.at[a],
                                        device_id=(x, y, 1 - c), device_id_type=MESH)


def _grad_sibling_start(fams, name):
    n = len(fams)
    rhs = [f.shape[1] // 2 for f in fams]

    def body(*refs):
        ins, lands = refs[:n], refs[n:2 * n]
        send_sems, recv_sems = refs[2 * n], refs[2 * n + 1]
        token = refs[-1]
        x, y, c = _pos()
        for a in range(n):
            _sibling_copy(ins, lands, send_sems, recv_sems, rhs, a, c, x, y).start()
        token[...] = jnp.zeros_like(token)

    land_shapes = [(f.shape[0], f.shape[1] // 2, f.shape[2]) for f in fams]
    return pl.pallas_call(
        body, name=name,
        out_shape=(pltpu.SemaphoreType.DMA((n,)), pltpu.SemaphoreType.DMA((n,)),
                   *[pltpu.HBM(f.shape, f.dtype) for f in fams],
                   *[pltpu.HBM(ls, f.dtype) for ls, f in zip(land_shapes, fams)],
                   jax.ShapeDtypeStruct((8, 128), F32)),
        in_specs=[_HBM] * (2 * n),
        out_specs=(_SEM, _SEM, *[_HBM] * (2 * n), pl.BlockSpec(memory_space=pltpu.VMEM)),
        input_output_aliases={a: 2 + a for a in range(2 * n)},
        compiler_params=pltpu.CompilerParams(has_side_effects=_DATAFLOW),
    )(*[_in_hbm(f) for f in fams], *[_in_hbm(lax.empty(ls, f.dtype)) for ls, f in zip(land_shapes, fams)])


def _grad_sibling_wait(started, after, name):
    send_sems, recv_sems, *thru = started
    n = len(thru) // 2
    rhs = [t.shape[1] // 2 for t in thru[:n]]

    def body(*refs):
        ins, lands = refs[:n], refs[n:2 * n]
        send_sems, recv_sems = refs[2 * n], refs[2 * n + 1]
        x, y, c = _pos()
        for a in range(n):
            cp = _sibling_copy(ins, lands, send_sems, recv_sems, rhs, a, c, x, y)
            cp.wait_send()
            cp.wait_recv()

    outs = pl.pallas_call(
        body, name=name, out_shape=[pltpu.HBM(t.shape, t.dtype) for t in thru],
        in_specs=[_HBM] * (2 * n) + [_SEM, _SEM] + [_ANY] * len(after), out_specs=[_HBM] * (2 * n),
        input_output_aliases={a: a for a in range(2 * n)},
        compiler_params=pltpu.CompilerParams(has_side_effects=_DATAFLOW),
    )(*thru, send_sems, recv_sems, *after)
    return outs[:n], outs[n:]


def _grad_share(fulls, name, small=None):
    n = len(fulls)
    ns = 0 if small is None else 1
    rhs = [f.shape[0] // 2 for f in fulls]

    def body(*refs):
        ins, outs = refs[:n], refs[n + ns:2 * n + ns]
        send_sems, recv_sems = refs[2 * (n + ns)], refs[2 * (n + ns) + 1]
        x, y, c = _pos()

        def copy(a, half):
            rows = pl.ds(pl.multiple_of(half * rhs[a], 8), rhs[a])
            return pltpu.make_async_remote_copy(src_ref=ins[a].at[rows, :], dst_ref=outs[a].at[rows, :],
                                                send_sem=send_sems.at[7 * ns + a], recv_sem=recv_sems.at[7 * ns + a],
                                                device_id=(x, y, 1 - c), device_id_type=MESH)

        sends = [copy(a, c) for a in range(n)]
        for cp in sends:
            cp.start()
        if ns:
            small_ref, all_ref = refs[n], refs[2 * n + 1]
            me = 4 * x + 2 * y + c

            def peer(r):
                dx, dy, dc = (r >> 2) & 1, (r >> 1) & 1, r & 1
                return (x if dx == 0 else 1 - x), (y if dy == 0 else 1 - y), (c if dc == 0 else 1 - c)

            def small_copy(r, slot):
                return pltpu.make_async_remote_copy(src_ref=small_ref, dst_ref=all_ref.at[slot], send_sem=send_sems.at[r - 1],
                                                    recv_sem=recv_sems.at[r - 1], device_id=peer(r), device_id_type=MESH)

            smalls = [small_copy(r, me) for r in range(1, 8)]
            for cp in smalls:
                cp.start()
            for r in range(1, 8):
                px, py, pc = peer(r)
                small_copy(r, 4 * px + 2 * py + pc).wait_recv()
            sends = sends + smalls
        for a in range(n):
            copy(a, 1 - c).wait_recv()
        for cp in sends:
            cp.wait_send()

    return pl.pallas_call(
        body, name=name, in_specs=[_HBM] * (n + ns), out_specs=[_HBM] * (n + ns),
        out_shape=[jax.ShapeDtypeStruct(f.shape, f.dtype) for f in fulls]
        + ([jax.ShapeDtypeStruct((8,) + small.shape, small.dtype)] if ns else []),
        input_output_aliases={a: a for a in range(n)},
        scratch_shapes=[pltpu.SemaphoreType.DMA((7 * ns + n,)), pltpu.SemaphoreType.DMA((7 * ns + n,))],
    )(*fulls, *([small] if ns else []))


def _add_sibling(own, recv, cq, name):
    nb, R, Cc = own.shape
    Rh = R // 2

    def body(cq_ref, a_ref, b_ref, o32_ref, o16_ref):
        s = a_ref[0] + b_ref[0]
        mine = pl.program_id(0) == cq_ref[1]

        @pl.when(mine)
        def _():
            o32_ref[...] = s

        @pl.when(jnp.logical_not(mine))
        def _():
            o16_ref[0] = s.astype(o16_ref.dtype)

    sp = pl.BlockSpec((1, Rh, Cc), lambda b, s: (b, 0, 0))
    gs = pltpu.PrefetchScalarGridSpec(
        num_scalar_prefetch=1, grid=(nb,),
        in_specs=[pl.BlockSpec((1, Rh, Cc), lambda b, s: (b, s[0], 0)), sp],
        out_specs=[pl.BlockSpec((Rh, Cc), lambda b, s: (0, 0)), sp])
    return pl.pallas_call(
        body, name=name, grid_spec=gs,
        out_shape=[jax.ShapeDtypeStruct((Rh, Cc), F32), jax.ShapeDtypeStruct((nb, Rh, Cc), _MXU)],
        compiler_params=_cparams(("arbitrary",)),
    )(cq, own, recv)


def _add_chips(part32, recv3, cq, name):
    Rh, Cc = part32.shape

    def body(cq_ref, a_ref, b_ref, o_ref):
        acc = a_ref[...]
        for j in range(3):
            acc = acc + b_ref[j].astype(F32)
        o_ref[...] = acc

    gs = pltpu.PrefetchScalarGridSpec(
        num_scalar_prefetch=1, grid=(1,),
        in_specs=[pl.BlockSpec((Rh, Cc), lambda i, s: (0, 0)), pl.BlockSpec((3, Rh, Cc), lambda i, s: (0, 0, 0))],
        out_specs=pl.BlockSpec((Rh, Cc), lambda i, s: (s[0], 0)))
    return pl.pallas_call(
        body, name=name, grid_spec=gs, out_shape=jax.ShapeDtypeStruct((2 * Rh, Cc), F32),
        compiler_params=_cparams(("arbitrary",)),
    )(cq, part32, recv3)


def _adamw(w, g, m, v, name):
    R, Cc = w.shape
    T = max([t for t in range(8, 257, 8) if R % t == 0], default=R)

    def body(w_ref, g_ref, m_ref, v_ref, d_ref, mo_ref, vo_ref):
        d_ref[...], mo_ref[...], vo_ref[...] = _adamw_math(w_ref[...], g_ref[...], m_ref[...], v_ref[...])

    sp = pl.BlockSpec((T, Cc), lambda i: (i, 0))
    sh = jax.ShapeDtypeStruct((R, Cc), F32)
    return pl.pallas_call(
        body, name=name, grid=(R // T,), in_specs=[sp] * 4, out_specs=(sp, sp, sp), out_shape=(sh, sh, sh),
        compiler_params=_cparams(("parallel",)),
    )(w, g, m, v)


SMALL_ROWS = 32
ROW_CONV, ROW_FCG, ROW_FCU = 5, 13, 22


def _adamw_math(w, g, m, v):
    mn = ADAM_B1 * m + (1.0 - ADAM_B1) * g
    vn = ADAM_B2 * v + (1.0 - ADAM_B2) * (g * g)
    c1 = 1.0 / (1.0 - ADAM_B1 ** ADAM_STEP)
    c2 = 1.0 / (1.0 - ADAM_B2 ** ADAM_STEP)
    return -ADAM_LR * ((mn * c1) / (jnp.sqrt(vn * c2) + ADAM_EPS) + ADAM_WD * w), mn, vn


def _pack_small(n1, n2, fn, gp, gn, conv, fcg, fcu, loss):
    W = D_MODEL

    def body(n1_ref, n2_ref, fn_ref, gp_ref, gn_ref, conv_ref, fcg_ref, fcu_ref, loss_ref, o_ref):
        o_ref[...] = jnp.zeros_like(o_ref)
        o_ref[0:1, :] = n1_ref[...]
        o_ref[1:2, :] = n2_ref[...]
        o_ref[2:3, :] = fn_ref[...]
        o_ref[3:4, 0:8] = gp_ref[0:1, 0:8]
        o_ref[3:4, 8:9] = loss_ref[0:1, 0:1]
        o_ref[4:5, 0:128] = gn_ref[...]
        for i in range(GDN_CONV):
            o_ref[ROW_CONV + 2 * i:ROW_CONV + 2 * i + 1, :] = conv_ref[i:i + 1, 0:W]
            o_ref[ROW_CONV + 2 * i + 1:ROW_CONV + 2 * i + 2, 0:3 * GDN_WIDTH - W] = conv_ref[i:i + 1, W:3 * GDN_WIDTH]
        for r0, ref in ((ROW_FCG, fcg_ref), (ROW_FCU, fcu_ref)):
            for i in range(FFN_CONV):
                for k in range(3):
                    n = min(W, D_FF - k * W)
                    o_ref[r0 + 3 * i + k:r0 + 3 * i + k + 1, 0:n] = ref[i:i + 1, k * W:k * W + n]

    return pl.pallas_call(body, name="pack_small", out_shape=jax.ShapeDtypeStruct((SMALL_ROWS, W), F32))(
        n1, n2, fn, gp, gn, conv, fcg, fcu, loss)


def _small_step(meq, small_all, small, ws, ms, vs):
    W = D_MODEL
    n = len(ws)
    cw, fw = ws[6].shape[1], ws[7].shape[1]

    def body(meq_ref, all_ref, own_ref, *refs):
        w_refs, m_refs, v_refs = refs[:n], refs[n:2 * n], refs[2 * n:3 * n]
        loss_ref = refs[3 * n]
        outs = refs[3 * n + 1:]
        me, q = meq_ref[0], meq_ref[1]
        red = None
        for d in range(8):
            term = jnp.where(me == d, own_ref[...], all_ref[d])
            red = term if red is None else red + term
        loss_ref[...] = jnp.broadcast_to(red[3:4, 8:9], loss_ref.shape)
        conv = [jnp.concatenate([red[ROW_CONV + 2 * i:ROW_CONV + 2 * i + 1, :],
                                 red[ROW_CONV + 2 * i + 1:ROW_CONV + 2 * i + 2, 0:3 * GDN_WIDTH - W]], axis=1)
                for i in range(GDN_CONV)]
        conv = jnp.concatenate(conv, axis=0)

        def fc_rows(r0):
            rows = [jnp.concatenate([red[r0 + 3 * i + k:r0 + 3 * i + k + 1, 0:min(W, D_FF - k * W)] for k in range(3)], axis=1)
                    for i in range(FFN_CONV)]
            return jnp.concatenate(rows, axis=0)

        fc = jnp.concatenate([fc_rows(ROW_FCG), fc_rows(ROW_FCU)], axis=1)

        def chip_block(full, width):
            out = None
            for j in range(N_CHIPS):
                term = jnp.where(q == j, full[:, width * j:width * (j + 1)], 0.0)
                out = term if out is None else out + term
            return out

        grads = [red[0:1, :], red[1:2, :], red[2:3, :], red[3:4, 0:4], red[3:4, 4:8], red[4:5, 0:128],
                 chip_block(conv, cw), chip_block(fc, fw)]
        for k in range(n):
            d_, m_, v_ = _adamw_math(w_refs[k][...], grads[k], m_refs[k][...], v_refs[k][...])
            outs[4 * k][...] = grads[k]
            outs[4 * k + 1][...] = d_
            outs[4 * k + 2][...] = m_
            outs[4 * k + 3][...] = v_

    full = lambda a: pl.BlockSpec(a.shape, lambda i, s_, nd=len(a.shape): (0,) * nd)
    arrays = [small_all, small, *ws, *ms, *vs]
    out_shapes = [jax.ShapeDtypeStruct((8, 128), F32)] + [jax.ShapeDtypeStruct(w.shape, F32) for w in ws for _ in range(4)]
    gs = pltpu.PrefetchScalarGridSpec(
        num_scalar_prefetch=1, grid=(1,), in_specs=[full(a) for a in arrays],
        out_specs=[pl.BlockSpec(o.shape, lambda i, s_, nd=len(o.shape): (0,) * nd) for o in out_shapes])
    return pl.pallas_call(body, name="small_step", grid_spec=gs, out_shape=out_shapes)(meq, *arrays)


def _pad_lanes(v, n=D_MODEL):
    return jnp.pad(v, ((0, 0), (0, n - v.shape[1])))


def kernel(x, norm1_w, w_in, conv_qkv_w, a_log, dt_bias, gdn_norm_w, w_out, norm2_w, w_up, ffn_conv_w, w_down, final_norm_w, loss_target, m_norm1_w, m_w_in, m_conv_qkv_w, m_a_log, m_dt_bias, m_gdn_norm_w, m_w_out, m_norm2_w, m_w_up, m_ffn_conv_w, m_w_down, m_final_norm_w, v_norm1_w, v_w_in, v_conv_qkv_w, v_a_log, v_dt_bias, v_gdn_norm_w, v_w_out, v_norm2_w, v_w_up, v_ffn_conv_w, v_w_down, v_final_norm_w):
    c = lax.axis_index("c")
    q = 2 * lax.axis_index("x") + lax.axis_index("y")
    S = x.shape[1]
    cq = jnp.stack([c, q]).astype(jnp.int32)

    *in_started, in_token = _gather_halves_start([w_in[0].astype(_MXU), conv_qkv_w[0], ffn_conv_w[0]], x, "gather_in_start")
    w_in_l, m_w_in_l, v_w_in_l = (a + in_token[0:1, 0:1] for a in (w_in, m_w_in, v_w_in))
    h1 = _rmsnorm_fwd(x[0], norm1_w, "norm1", after=[in_token])
    rest = [(a[0] + in_token[0:1, 0:1]).astype(_MXU) for a in (w_out, w_up, w_down)]
    in_shards, got_in = _gather_halves_wait(in_started, [w_in_l, m_w_in_l, v_w_in_l, h1, *rest], "gather_in_wait")
    (g_in, g_conv, g_fconv), (w_in_l, m_w_in_l, v_w_in_l) = _place_own(
        in_shards, _sibling_fill(got_in, "fill_in"), cq, "place_in", carry=[w_in_l, m_w_in_l, v_w_in_l])
    *rest_started, token = _gather_halves_start(rest, g_conv, "gather_rest_start")

    rest_state = {}

    def rest_arrived(after):
        rest_state["shards"], got = _gather_halves_wait(rest_started, after, "gather_rest_wait")
        *rest_state["fill"], tok = _sibling_fill_start(got, "fill_rest_start")
        return tok

    def rest_filled(after):
        got = _sibling_fill_wait(rest_state["fill"], after, "fill_rest_wait")
        g_out, g_up, g_down = _place_own(rest_state["shards"], got, cq, "place_rest")
        return g_out.reshape(D_MODEL, D_MODEL), g_up, g_down.reshape(D_FF, D_MODEL)

    rest_weights = (rest_arrived, rest_filled)
    wp = _wp_assemble(g_in, [token])
    conv_f = jnp.concatenate([g_conv[i] for i in range(N_CHIPS)], axis=1)
    fcw = jnp.concatenate([g_fconv[i] for i in range(N_CHIPS)], axis=1)
    gp = _pad_lanes(jnp.concatenate([a_log, dt_bias], axis=1), 128)
    fnw = final_norm_w[None, :]
    early = {}

    def early_sibling(d_wup, d_wdown):
        *early["sibling"], tok = _grad_sibling_start([d_wup, d_wdown.reshape(N_CHIPS, D_FF // N_CHIPS, D_MODEL)],
                                                     "grad_sibling_early_start")
        return tok

    def early_chips(dx2):
        fams_e, got_e = _grad_sibling_wait(early["sibling"], [dx2], "grad_sibling_early_wait")
        early["parts"] = [_add_sibling(f, r, cq, "add_sibling_" + nm) for f, r, nm in zip(fams_e, got_e, ("w_up", "w_down"))]
        *early["started"], tok = _grad_chips_start([p[1] for p in early["parts"]], "grad_chips_start")
        return tok

    def late_sibling(d_wp, d_wout):
        *early["late_sibling"], tok = _grad_sibling_start(
            [_win_split(d_wp), d_wout.reshape(N_CHIPS, D_MODEL // N_CHIPS, D_MODEL)], "grad_sibling_late_start")
        return tok

    early_grads = (early_sibling, early_chips, late_sibling)

    loss_l, dx, g = _local_step(x[0], loss_target[0], h1, norm1_w, norm2_w, fnw, gp, gdn_norm_w, wp,
                                conv_f, fcw, rest_weights, early_grads)
    small = _pack_small(g["n1w"], g["n2w"], g["fnw"], g["gp"], g["gnw"], g["conv_w"], g["fcw_g"], g["fcw_u"], loss_l)
    fams, got = _grad_sibling_wait(early["late_sibling"], [dx], "grad_sibling_late_wait")
    parts = [_add_sibling(f, r, cq, "add_sibling_" + nm) for f, r, nm in zip(fams, got, ("w_in", "w_out"))]
    *late_started, late_token = _grad_chips_start([p[1] for p in parts], "grad_chips_late_start")
    got3_e = _grad_chips_wait(early["started"], [dx, g["wp"], late_token], "grad_chips_wait")
    g_w_up, g_w_down = _grad_share(
        [_add_chips(p[0], r3, cq, "add_chips_" + nm) for p, r3, nm in zip(early["parts"], got3_e, ("w_up", "w_down"))],
        "grad_share_early")
    big = {}

    def adamw_big(nm, w, gg, m, v):
        d_, m_, v_ = _adamw(w[0], gg, m[0], v[0], "adamw_" + nm)
        big[nm] = (gg[None], d_[None], m_[None], v_[None])

    adamw_big("w_up", w_up, g_w_up, m_w_up, v_w_up)
    adamw_big("w_down", w_down, g_w_down, m_w_down, v_w_down)
    got3 = _grad_chips_wait(late_started, [big["w_up"][1], big["w_down"][1]], "grad_chips_late_wait")
    g_w_in, g_w_out, small_all = _grad_share(
        [_add_chips(p[0], r3, cq, "add_chips_" + nm) for p, r3, nm in zip(parts, got3, ("w_in", "w_out"))],
        "grad_share_late", small)
    adamw_big("w_in", w_in_l, g_w_in, m_w_in_l, v_w_in_l)
    adamw_big("w_out", w_out, g_w_out, m_w_out, v_w_out)
    small_names = ["norm1_w", "norm2_w", "final_norm_w", "a_log", "dt_bias", "gdn_norm_w", "conv_qkv_w", "ffn_conv_w"]
    loss_b, *small_out = _small_step(
        jnp.stack([2 * q + c, q]).astype(jnp.int32), small_all, small,
        [norm1_w, norm2_w, final_norm_w[None], a_log, dt_bias, gdn_norm_w, conv_qkv_w[0], ffn_conv_w[0]],
        [m_norm1_w, m_norm2_w, m_final_norm_w[None], m_a_log, m_dt_bias, m_gdn_norm_w, m_conv_qkv_w[0], m_ffn_conv_w[0]],
        [v_norm1_w, v_norm2_w, v_final_norm_w[None], v_a_log, v_dt_bias, v_gdn_norm_w, v_conv_qkv_w[0], v_ffn_conv_w[0]])
    like = dict(final_norm_w=lambda t: t[0], conv_qkv_w=lambda t: t[None], ffn_conv_w=lambda t: t[None])
    for k, nm in enumerate(small_names):
        big[nm] = tuple(like.get(nm, lambda t: t)(t) for t in small_out[4 * k:4 * k + 4])
    names = ["norm1_w", "w_in", "conv_qkv_w", "a_log", "dt_bias", "gdn_norm_w", "w_out", "norm2_w", "w_up",
             "ffn_conv_w", "w_down", "final_norm_w"]
    return (loss_b[0, 0], dx[None], *[big[n][0] for n in names], *[big[n][1] for n in names],
            *[big[n][2] for n in names], *[big[n][3] for n in names])
```

```python
import functools
import math

import numpy as np
import jax
import jax.numpy as jnp
from jax import lax
from jax.experimental import pallas as pl
from jax.experimental.pallas import tpu as pltpu

F32 = jnp.float32
BF16 = jnp.bfloat16
_MXU = jnp.bfloat16
_HI = lax.Precision.HIGHEST
EPS = 1e-6
V7X_VMEM_LIMIT = 56 * 1024 * 1024
MESH = pl.DeviceIdType.MESH

D_MODEL = 1024
GDN_HEADS, GDN_DIM, GDN_CHUNK, GDN_CONV = 4, 128, 64, 4
GDN_WIDTH = GDN_HEADS * GDN_DIM
DIL_HEADS, DIL_DIM = 8, 64
DIL_WIDTH = DIL_HEADS * DIL_DIM
D_FF, FFN_CONV = 2816, 3
IN_COLS = 3592
P_COLS = 3840
P_Z, P_QKVB, P_BA = 1536, 2048, 3584
ATT_T = 1024
ADAM_LR, ADAM_B1, ADAM_B2, ADAM_EPS, ADAM_WD, ADAM_STEP = 0.001, 0.9, 0.999, 1e-08, 0.01, 10
N_CHIPS = 4


def _cparams(sem=None, vmem=None):
    kw = {}
    if sem is not None:
        kw["dimension_semantics"] = sem
    if vmem is not None:
        kw["vmem_limit_bytes"] = vmem
    return pltpu.CompilerParams(**kw)


def _silu(x):
    return x * jax.nn.sigmoid(x)


def _pick_tile(n, cap):
    best = None
    for t in range(128, min(n, cap) + 1, 128):
        if n % t == 0:
            best = t
    return best or n


def _mm(a, b, mode, *, out_dtype=F32, residual=None, name, b_blocks=False, place=None, into=None, tn=None, after=()):
    if mode == "nn":
        M, K = a.shape
        N = b.shape[0] * b.shape[2] if b_blocks else b.shape[1]
    elif mode == "nt":
        (M, K), (N, _) = a.shape, b.shape
    else:
        (K, M), (_, N) = a.shape, b.shape
    tm = _pick_tile(M, 1024)
    tn = b.shape[2] if b_blocks else (tn or _pick_tile(N, 1536))

    def vmem(tm, tn):
        return 2 * (tm * K * a.dtype.itemsize + tn * K * b.dtype.itemsize
                    + tm * tn * (jnp.dtype(out_dtype).itemsize + (4 if residual is not None else 0))) + 3 * tm * tn * 4

    fixed_tn = b_blocks or (place is not None and place[0] == "blocks")
    while vmem(tm, tn) > 40 * 1024 * 1024:
        if (tm >= tn or fixed_tn) and tm % 256 == 0:
            tm //= 2
        elif tn % 256 == 0 and not fixed_tn:
            tn //= 2
        else:
            tm //= 2
    a_spec = pl.BlockSpec((K, tm), lambda j, i: (0, i)) if mode == "tn" else pl.BlockSpec((tm, K), lambda j, i: (i, 0))
    if b_blocks:
        b_spec = pl.BlockSpec((None, K, tn), lambda j, i: (j, 0, 0))
    else:
        b_spec = pl.BlockSpec((tn, K), lambda j, i: (j, 0)) if mode == "nt" else pl.BlockSpec((K, tn), lambda j, i: (0, j))
    r_spec = pl.BlockSpec((tm, tn), lambda j, i: (i, j))
    if place is None:
        o_spec, o_shape = r_spec, (M, N)
    elif place[0] == "rows":
        off = place[2] // tm
        o_spec, o_shape = pl.BlockSpec((tm, tn), lambda j, i: (i + off, j)), (place[1], N)
    else:
        off = place[2]
        o_spec, o_shape = pl.BlockSpec((None, tm, tn), lambda j, i: (j + off, i, 0)), (place[1], M, tn)
    dims = {"nn": (((1,), (0,)), ((), ())), "nt": (((1,), (1,)), ((), ())), "tn": (((0,), (0,)), ((), ()))}[mode]

    def body(*refs):
        a_ref, b_ref = refs[0], refs[1]
        o_ref = refs[-1]
        acc = lax.dot_general(a_ref[...].astype(_MXU), b_ref[...].astype(_MXU), dims, preferred_element_type=F32)
        if residual is not None:
            acc = acc + refs[2][...]
        o_ref[...] = acc.astype(out_dtype)

    ins, specs, alias = [a, b], [a_spec, b_spec], {}
    if residual is not None:
        ins.append(residual)
        specs.append(r_spec)
    if into is not None:
        alias = {len(ins): 0}
        ins.append(into)
        specs.append(pl.BlockSpec(memory_space=pl.ANY))
    ins += list(after)
    specs += [pl.BlockSpec(memory_space=pl.ANY)] * len(after)
    return pl.pallas_call(
        body, name=name, grid=(N // tn, M // tm), in_specs=specs, out_specs=o_spec,
        out_shape=jax.ShapeDtypeStruct(o_shape, out_dtype), input_output_aliases=alias,
        compiler_params=_cparams(("parallel", "parallel"), V7X_VMEM_LIMIT),
    )(*ins)


def _mm_nt_blocks(a_list, b4, name, after=()):
    M = a_list[0].shape[0]
    nb, N, Kb = b4.shape
    tm, tn = _pick_tile(M, 1024), _pick_tile(N, 512)

    def body(a0_ref, a1_ref, b_ref, *rest):
        o_ref = rest[-1]
        acc = None
        for blk in range(nb):
            a_ref = (a0_ref, a1_ref)[blk // 2]
            lo = (blk % 2) * Kb
            t = lax.dot_general(a_ref[:, lo:lo + Kb].astype(_MXU), b_ref[blk].astype(_MXU), (((1,), (1,)), ((), ())),
                                preferred_element_type=F32)
            acc = t if acc is None else acc + t
        o_ref[...] = acc

    a_spec = pl.BlockSpec((tm, 2 * Kb), lambda j, i: (i, 0))
    return pl.pallas_call(
        body, name=name, grid=(N // tn, M // tm),
        in_specs=[a_spec, a_spec, pl.BlockSpec((nb, tn, Kb), lambda j, i: (0, j, 0))]
        + [pl.BlockSpec(memory_space=pl.ANY)] * len(after),
        out_specs=pl.BlockSpec((tm, tn), lambda j, i: (i, j)), out_shape=jax.ShapeDtypeStruct((M, N), F32),
        compiler_params=_cparams(("parallel", "parallel"), V7X_VMEM_LIMIT),
    )(a_list[0], a_list[1], b4, *after)


def _wp_assemble(g_in, after=()):
    nb, Dm, Wb = g_in.shape
    T = 256
    n_lo = P_QKVB - 2 * Wb

    def body(g_ref, *rest):
        g2 = g_ref[2]
        rest[-1][...] = jnp.concatenate(
            [g_ref[0], g_ref[1], g2[:, :n_lo], g2[:, n_lo + 8:], g_ref[3], g2[:, n_lo:n_lo + 8],
             jnp.zeros((T, P_COLS - P_BA - 8), g_in.dtype)], axis=1)

    return pl.pallas_call(
        body, name="wp_assemble", grid=(Dm // T,),
        in_specs=[pl.BlockSpec((nb, T, Wb), lambda i: (0, i, 0))] + [pl.BlockSpec(memory_space=pl.ANY)] * len(after),
        out_specs=pl.BlockSpec((T, P_COLS), lambda i: (i, 0)), out_shape=jax.ShapeDtypeStruct((Dm, P_COLS), g_in.dtype),
        compiler_params=_cparams(("parallel",)),
    )(g_in, *after)


def _win_split(d_wp):
    Dm = d_wp.shape[0]
    Wb = IN_COLS // N_CHIPS
    T = 256

    def body(x_ref, o_ref):
        xv = x_ref[...]
        o_ref[0] = xv[:, 0:Wb]
        o_ref[1] = xv[:, Wb:2 * Wb]
        o_ref[2] = jnp.concatenate([xv[:, 2 * Wb:P_QKVB], xv[:, P_BA:P_BA + 8], xv[:, P_QKVB:3 * Wb - 8]], axis=1)
        o_ref[3] = xv[:, 3 * Wb - 8:P_BA]

    return pl.pallas_call(
        body, name="win_split", grid=(Dm // T,), in_specs=[pl.BlockSpec((T, P_COLS), lambda i: (i, 0))],
        out_specs=pl.BlockSpec((N_CHIPS, T, Wb), lambda i: (0, i, 0)),
        out_shape=jax.ShapeDtypeStruct((N_CHIPS, Dm, Wb), F32), compiler_params=_cparams(("parallel",)),
    )(d_wp)


def _rmsnorm_fwd(x, w, name, after=()):
    S, D = x.shape
    T = _pick_tile(S, 512)

    def body(x_ref, w_ref, *rest):
        xv = x_ref[...]
        rs = lax.rsqrt(jnp.mean(xv * xv, axis=-1, keepdims=True) + EPS)
        rest[-1][...] = (xv * rs * w_ref[...]).astype(rest[-1].dtype)

    return pl.pallas_call(
        body, name=name, grid=(S // T,),
        in_specs=[pl.BlockSpec((T, D), lambda i: (i, 0)), pl.BlockSpec((1, D), lambda i: (0, 0))] + [_ANY] * len(after),
        out_specs=pl.BlockSpec((T, D), lambda i: (i, 0)),
        out_shape=jax.ShapeDtypeStruct((S, D), _MXU),
        compiler_params=_cparams(("parallel",)),
    )(x, w, *after)


def _rmsnorm_bwd(dh, x, w, dres, name, after=()):
    S, D = x.shape
    T = _pick_tile(S, 512)

    def body(dh_ref, x_ref, w_ref, dres_ref, *rest):
        dx_ref, dw_ref = rest[-2:]
        xv = x_ref[...]
        rs = lax.rsqrt(jnp.mean(xv * xv, axis=-1, keepdims=True) + EPS)
        xn = xv * rs
        dhv = dh_ref[...]
        dxn = dhv * w_ref[...]
        dx_ref[...] = dres_ref[...] + rs * (dxn - xn * jnp.mean(dxn * xn, axis=-1, keepdims=True))

        @pl.when(pl.program_id(0) == 0)
        def _():
            dw_ref[...] = jnp.zeros_like(dw_ref)

        dw_ref[...] += jnp.sum(dhv * xn, axis=0, keepdims=True)

    row = pl.BlockSpec((T, D), lambda i: (i, 0))
    vec = pl.BlockSpec((1, D), lambda i: (0, 0))
    return pl.pallas_call(
        body, name=name, grid=(S // T,), in_specs=[row, row, vec, row] + [_ANY] * len(after), out_specs=(row, vec),
        out_shape=(jax.ShapeDtypeStruct((S, D), F32), jax.ShapeDtypeStruct((1, D), F32)),
        compiler_params=_cparams(("arbitrary",)),
    )(dh, x, w, dres, *after)


def _loss_head(x3, w, tgt, name):
    S, D = x3.shape
    T = _pick_tile(S, 512)

    def body(x_ref, w_ref, t_ref, loss_ref, dx_ref, dxn_ref, dw_ref):
        xv = x_ref[...]
        rs = lax.rsqrt(jnp.mean(xv * xv, axis=-1, keepdims=True) + EPS)
        xn = xv * rs
        err = xn * w_ref[...] - t_ref[...]
        dy = err * (1.0 / D)
        dxn = dy * w_ref[...]
        dxv = rs * (dxn - xn * jnp.mean(dxn * xn, axis=-1, keepdims=True))
        dx_ref[...] = dxv
        dxn_ref[...] = dxv.astype(dxn_ref.dtype)

        @pl.when(pl.program_id(0) == 0)
        def _():
            dw_ref[...] = jnp.zeros_like(dw_ref)
            loss_ref[...] = jnp.zeros_like(loss_ref)

        dw_ref[...] += jnp.sum(dy * xn, axis=0, keepdims=True)
        part = jnp.sum(jnp.sum(err * err, axis=-1, keepdims=True), axis=0, keepdims=True) * (0.5 / D)
        loss_ref[...] += jnp.broadcast_to(part, loss_ref.shape)

    row = pl.BlockSpec((T, D), lambda i: (i, 0))
    vec = pl.BlockSpec((1, D), lambda i: (0, 0))
    return pl.pallas_call(
        body, name=name, grid=(S // T,), in_specs=[row, vec, row],
        out_specs=(pl.BlockSpec((8, 128), lambda i: (0, 0)), row, row, vec),
        out_shape=(jax.ShapeDtypeStruct((8, 128), F32), jax.ShapeDtypeStruct((S, D), F32), jax.ShapeDtypeStruct((S, D), _MXU),
                   jax.ShapeDtypeStruct((1, D), F32)),
        compiler_params=_cparams(("arbitrary",)),
    )(x3, w, tgt)


def _shifted(ext, back, lo, n):
    if back == 0:
        return ext[lo:lo + n, :]
    return pltpu.roll(ext, back % ext.shape[0], 0)[lo:lo + n, :]


def _conv_windows(ext, K, T):
    return [_shifted(ext, (K - 1) - i, 8, T) for i in range(K)]


def _conv_taps(ext, w, K, T):
    out = None
    for i, win in enumerate(_conv_windows(ext, K, T)):
        term = win * w[i:i + 1, :]
        out = term if out is None else out + term
    return out


def _conv_taps_t(ext, w, K, T):
    out = None
    for i in range(K):
        term = _shifted(ext, i - (K - 1), 0, T) * w[i:i + 1, :]
        out = term if out is None else out + term
    return out


def _tri_masks(C):
    r = lax.broadcasted_iota(jnp.int32, (C, C), 0)
    c = lax.broadcasted_iota(jnp.int32, (C, C), 1)
    return r == c, r >= c, r > c, r <= c


_NN, _NT, _TN = ((1,), (0,)), ((1,), (1,)), ((0,), (0,))
_GDN_PASSES = dict(qk=1, inv=1, sol=1, scan=1, bwd=1)


def _bdot_raw(a, b, kind, passes):
    dims = ({"NN": ((2,), (1,)), "NT": ((2,), (2,)), "TN": ((1,), (1,))}[kind], ((0,), (0,)))
    if passes == 0:
        return lax.dot_general(a, b, dims, precision=_HI, preferred_element_type=F32)
    ah, bh = a.astype(BF16), b.astype(BF16)
    out = lax.dot_general(ah, bh, dims, preferred_element_type=F32)
    if passes == 3:
        al, bl = (a - ah.astype(F32)).astype(BF16), (b - bh.astype(F32)).astype(BF16)
        out = out + lax.dot_general(ah, bl, dims, preferred_element_type=F32) + lax.dot_general(al, bh, dims, preferred_element_type=F32)
    return out


@functools.partial(jax.custom_vjp, nondiff_argnums=(2, 3))
def _bdot(a, b, kind, passes):
    return _bdot_raw(a, b, kind, passes)


def _bdot_fwd(a, b, kind, passes):
    return _bdot_raw(a, b, kind, passes), (a, b)


def _bdot_bwd(kind, passes, res, ct):
    a, b = res
    if kind == "NN":
        return _bdot_raw(ct, b, "NT", passes), _bdot_raw(a, ct, "TN", passes)
    if kind == "NT":
        return _bdot_raw(ct, b, "NN", passes), _bdot_raw(ct, a, "TN", passes)
    return _bdot_raw(b, ct, "NT", passes), _bdot_raw(a, ct, "NN", passes)


_bdot.defvjp(_bdot_fwd, _bdot_bwd)


def _softplus(x):
    return jnp.maximum(x, 0.0) + jnp.log(1.0 + jnp.exp(-jnp.abs(x)))


def _gdn_stage1(cq, ck, cv, b_col, a_col, alog, dtb, dot=_bdot_raw):
    C = cq.shape[1]
    eye, incl, strict, incl_t = _tri_masks(C)
    qn = cq * lax.rsqrt(jnp.sum(cq * cq, axis=-1, keepdims=True) + EPS) * (GDN_DIM ** -0.5)
    kn = ck * lax.rsqrt(jnp.sum(ck * ck, axis=-1, keepdims=True) + EPS)
    beta = jax.nn.sigmoid(b_col)
    g = -jnp.exp(alog) * _softplus(a_col + dtb)
    g_row = jnp.sum(jnp.where(eye, g, 0.0), axis=1, keepdims=True)
    beta_row = jnp.sum(jnp.where(eye, beta, 0.0), axis=1, keepdims=True)
    gc_col = jnp.sum(jnp.where(incl, g_row, 0.0), axis=2, keepdims=True)
    gc_row = jnp.sum(jnp.where(incl_t, g, 0.0), axis=1, keepdims=True)
    dec = jnp.where(incl, jnp.exp(jnp.where(incl, gc_col - gc_row, 0.0)), 0.0)
    kk = dot(kn, kn, "NT", _GDN_PASSES["qk"])
    qk = dot(qn, kn, "NT", _GDN_PASSES["qk"])
    lmat = jnp.where(strict, dec * kk * beta_row, 0.0)
    attn = dec * qk * beta_row
    gam = jnp.exp(gc_col)
    gc_last = gc_col[:, C - 1:C, :]
    k_end = kn * (jnp.exp(gc_last - gc_col) * beta)
    return lmat, cv, gam * kn, gam * qn, attn, k_end, jnp.exp(gc_last)


def _tri_inv(lmat):
    C = lmat.shape[1]
    eye = _tri_masks(C)[0]
    ps = _GDN_PASSES["inv"]
    p = jnp.where(eye, 1.0, 0.0) - lmat
    lp = _bdot_raw(lmat, lmat, "NN", ps)
    n = int(math.log2(C))
    for s in range(1, n):
        p = p + _bdot_raw(p, lp, "NN", ps)
        if s < n - 1:
            lp = _bdot_raw(lp, lp, "NN", ps)
    return p


def _gated_norm(o, z, gnw):
    on = o * lax.rsqrt(jnp.mean(o * o, axis=-1, keepdims=True) + EPS) * gnw
    return on * _silu(z)


GDN_PG = 4
GDN_SG = 4


def _gdn_pairs(c, ba, gp, G):
    C, W, H = GDN_CHUNK, GDN_WIDTH, GDN_HEADS
    pairs = [(j, h) for j in range(G) for h in range(H)]
    cq, ck, cv = (jnp.stack([c[C * j:C * (j + 1), o + GDN_DIM * h:o + GDN_DIM * (h + 1)] for j, h in pairs]) for o in (0, W, 2 * W))
    b_col = jnp.stack([ba[C * j:C * (j + 1), h:h + 1] for j, h in pairs])
    a_col = jnp.stack([ba[C * j:C * (j + 1), H + h:H + h + 1] for j, h in pairs])
    alog = jnp.stack([gp[0:1, h:h + 1] for j, h in pairs])
    dtb = jnp.stack([gp[0:1, H + h:H + h + 1] for j, h in pairs])
    return pairs, (cq, ck, cv, b_col, a_col, alog, dtb)


def _gdn_pre_specs(S, G):
    C = GDN_CHUNK
    T = C * G
    return dict(
        cur=pl.BlockSpec((T, 3 * GDN_WIDTH), lambda i: (i, 0)),
        prev=pl.BlockSpec((8, 3 * GDN_WIDTH), lambda i: (jnp.maximum(i * (T // 8) - 1, 0), 0)),
        ba=pl.BlockSpec((T, 128), lambda i: (i, P_BA // 128)),
        cw=pl.BlockSpec((GDN_CONV, 3 * GDN_WIDTH), lambda i: (0, 0)),
        vec=pl.BlockSpec((1, 128), lambda i: (0, 0)),
        hd=pl.BlockSpec((GDN_HEADS, T, GDN_DIM), lambda i: (0, i, 0)),
        hc=pl.BlockSpec((GDN_HEADS, T, C), lambda i: (0, i, 0)),
        ge=pl.BlockSpec((G, GDN_HEADS, 8, 128), lambda i: (i, 0, 0, 0)),
    )


def _hd_shape(S, last=GDN_DIM):
    return jax.ShapeDtypeStruct((GDN_HEADS, S, last), F32)


def _gdn_pre(proj, conv_w, gp):
    S = proj.shape[0]
    C, G = GDN_CHUNK, GDN_PG
    nc = S // C
    sp = _gdn_pre_specs(S, G)

    def body(cur_ref, prev_ref, ba_ref, cw_ref, gp_ref, uv_ref, wk_ref, qd_ref, ke_ref, at_ref, ti_ref, ge_ref):
        prev = prev_ref[...] * jnp.where(pl.program_id(0) == 0, 0.0, 1.0)
        c = _silu(_conv_taps(jnp.concatenate([prev, cur_ref[...]], axis=0), cw_ref[...], GDN_CONV, C * G))
        pairs, args = _gdn_pairs(c, ba_ref[...], gp_ref[...], G)
        lmat, v, rk, q_dec, attn, k_end, g_end = _gdn_stage1(*args)
        t = _tri_inv(lmat)
        u_v = _bdot_raw(t, v, "NN", _GDN_PASSES["sol"])
        w_k = _bdot_raw(t, rk, "NN", _GDN_PASSES["sol"])
        for b, (j, h) in enumerate(pairs):
            rows = slice(C * j, C * (j + 1))
            uv_ref[h, rows, :] = u_v[b]
            wk_ref[h, rows, :] = w_k[b]
            qd_ref[h, rows, :] = q_dec[b]
            ke_ref[h, rows, :] = k_end[b]
            at_ref[h, rows, :] = attn[b]
            ti_ref[h, rows, :] = t[b]
            ge_ref[j, h] = jnp.broadcast_to(g_end[b], (8, 128))

    return pl.pallas_call(
        body, name="gdn_pre", grid=(nc // G,),
        in_specs=[sp["cur"], sp["prev"], sp["ba"], sp["cw"], sp["vec"]],
        out_specs=(sp["hd"], sp["hd"], sp["hd"], sp["hd"], sp["hc"], sp["hc"], sp["ge"]),
        out_shape=(_hd_shape(S), _hd_shape(S), _hd_shape(S), _hd_shape(S), _hd_shape(S, C), _hd_shape(S, C),
                   jax.ShapeDtypeStruct((nc, GDN_HEADS, 8, 128), F32)),
        compiler_params=_cparams(("parallel",)),
    )(proj, proj, proj, conv_w, gp)


def _gdn_scan_specs(S, G, rev):
    C = GDN_CHUNK
    T = C * G
    n = S // T
    ci = (lambda i: n - 1 - i) if rev else (lambda i: i)
    return dict(
        hd=pl.BlockSpec((GDN_HEADS, T, GDN_DIM), lambda i: (0, ci(i), 0)),
        hc=pl.BlockSpec((GDN_HEADS, T, C), lambda i: (0, ci(i), 0)),
        ge=pl.BlockSpec((G, GDN_HEADS, 8, 128), lambda i: (ci(i), 0, 0, 0)),
        z=pl.BlockSpec((T, GDN_WIDTH), lambda i: (ci(i), P_Z // GDN_WIDTH)),
        oa=pl.BlockSpec((T, GDN_WIDTH), lambda i: (ci(i), 0)),
        vec=pl.BlockSpec((1, 128), lambda i: (0, 0)),
        st=pl.BlockSpec((G, GDN_HEADS, GDN_DIM, GDN_DIM), lambda i: (ci(i), 0, 0, 0)),
    )


def _gdn_scan(u_v, w_k, q_dec, k_end, attn, g_end, proj, gnw, mix, after=()):
    S = proj.shape[0]
    C, G = GDN_CHUNK, GDN_SG
    nc = S // C
    sp = _gdn_scan_specs(S, G, False)
    ps = _GDN_PASSES["scan"]

    def body(uv_ref, wk_ref, qd_ref, ke_ref, at_ref, ge_ref, z_ref, gnw_ref, *rest):
        oa_ref, st_ref, s_scr = rest[-3:]

        @pl.when(pl.program_id(0) == 0)
        def _():
            s_scr[...] = jnp.zeros_like(s_scr)

        for j in range(G):
            rows = slice(C * j, C * (j + 1))
            st = s_scr[...]
            st_ref[j] = st
            u = uv_ref[:, rows, :] - _bdot_raw(wk_ref[:, rows, :], st, "NN", ps)
            o = _bdot_raw(qd_ref[:, rows, :], st, "NN", ps) + _bdot_raw(at_ref[:, rows, :], u, "NN", ps)
            s_scr[...] = ge_ref[j][:, 0:1, 0:1] * st + _bdot_raw(ke_ref[:, rows, :], u, "TN", ps)
            for h in range(GDN_HEADS):
                cols = slice(GDN_DIM * h, GDN_DIM * (h + 1))
                oa_ref[rows, cols] = _gated_norm(o[h], z_ref[rows, cols], gnw_ref[...])

    return pl.pallas_call(
        body, name="gdn_scan", grid=(nc // G,),
        in_specs=[sp["hd"], sp["hd"], sp["hd"], sp["hd"], sp["hc"], sp["ge"], sp["z"], sp["vec"]] + [_ANY] * (1 + len(after)),
        out_specs=(sp["oa"], sp["st"]),
        out_shape=(jax.ShapeDtypeStruct(mix.shape, F32),
                   jax.ShapeDtypeStruct((nc, GDN_HEADS, GDN_DIM, GDN_DIM), F32)),
        input_output_aliases={8: 0},
        scratch_shapes=[pltpu.VMEM((GDN_HEADS, GDN_DIM, GDN_DIM), F32)],
        compiler_params=_cparams(("arbitrary",)),
    )(u_v, w_k, q_dec, k_end, attn, g_end, proj, gnw, mix, *after)


def _gdn_scan_bwd(u_v, w_k, q_dec, k_end, attn, g_end, proj, gnw, states, d_oa):
    S = proj.shape[0]
    C, G = GDN_CHUNK, GDN_SG
    nc = S // C
    sp = _gdn_scan_specs(S, G, True)
    ps, pb = _GDN_PASSES["scan"], _GDN_PASSES["bwd"]

    def body(uv_ref, wk_ref, qd_ref, ke_ref, at_ref, ge_ref, z_ref, gnw_ref, st_ref, doa_ref,
             duv_ref, dwk_ref, dqd_ref, dke_ref, dat_ref, dge_ref, dz_ref, dgnw_ref, ds_scr):
        @pl.when(pl.program_id(0) == 0)
        def _():
            ds_scr[...] = jnp.zeros_like(ds_scr)
            dgnw_ref[...] = jnp.zeros_like(dgnw_ref)

        dgnw = jnp.zeros((1, 128), F32)
        for j in reversed(range(G)):
            rows = slice(C * j, C * (j + 1))
            st = st_ref[j]
            wk, qd, ke, at = wk_ref[:, rows, :], qd_ref[:, rows, :], ke_ref[:, rows, :], at_ref[:, rows, :]
            u = uv_ref[:, rows, :] - _bdot_raw(wk, st, "NN", ps)
            o = _bdot_raw(qd, st, "NN", ps) + _bdot_raw(at, u, "NN", ps)
            dos = []
            for h in range(GDN_HEADS):
                cols = slice(GDN_DIM * h, GDN_DIM * (h + 1))
                _, vjp2 = jax.vjp(_gated_norm, o[h], z_ref[rows, cols], gnw_ref[...])
                do_h, dz_h, dgn = vjp2(doa_ref[rows, cols])
                dz_ref[rows, cols] = dz_h
                dgnw = dgnw + dgn
                dos.append(do_h)
            do = jnp.stack(dos)
            ds_new = ds_scr[...]
            du = _bdot_raw(at, do, "TN", pb) + _bdot_raw(ke, ds_new, "NN", pb)
            duv_ref[:, rows, :] = du
            dat_ref[:, rows, :] = _bdot_raw(do, u, "NT", pb)
            dqd_ref[:, rows, :] = _bdot_raw(do, st, "NT", pb)
            dke_ref[:, rows, :] = _bdot_raw(u, ds_new, "NT", pb)
            dwk_ref[:, rows, :] = -_bdot_raw(du, st, "NT", pb)
            d_ge = jnp.sum(jnp.sum(st * ds_new, axis=2, keepdims=True), axis=1, keepdims=True)
            dge_ref[j] = jnp.broadcast_to(d_ge, (GDN_HEADS, 8, 128))
            ds_scr[...] = ge_ref[j][:, 0:1, 0:1] * ds_new + _bdot_raw(qd, do, "TN", pb) - _bdot_raw(wk, du, "TN", pb)
        dgnw_ref[...] += dgnw

    return pl.pallas_call(
        body, name="gdn_scan_bwd", grid=(nc // G,),
        in_specs=[sp["hd"], sp["hd"], sp["hd"], sp["hd"], sp["hc"], sp["ge"], sp["z"], sp["vec"], sp["st"], sp["oa"]],
        out_specs=(sp["hd"], sp["hd"], sp["hd"], sp["hd"], sp["hc"], sp["ge"], sp["oa"], sp["vec"]),
        out_shape=(_hd_shape(S), _hd_shape(S), _hd_shape(S), _hd_shape(S), _hd_shape(S, C),
                   jax.ShapeDtypeStruct((nc, GDN_HEADS, 8, 128), F32), jax.ShapeDtypeStruct((S, GDN_WIDTH), F32),
                   jax.ShapeDtypeStruct((1, 128), F32)),
        scratch_shapes=[pltpu.VMEM((GDN_HEADS, GDN_DIM, GDN_DIM), F32)],
        compiler_params=_cparams(("arbitrary",)),
    )(u_v, w_k, q_dec, k_end, attn, g_end, proj, gnw, states, d_oa)


def _gdn_post(proj, conv_w, gp, tinv, u_v, w_k, d_uv, d_wk, d_qd, d_ke, d_at, d_ge):
    S = proj.shape[0]
    C, G = GDN_CHUNK, GDN_PG
    nc = S // C
    sp = _gdn_pre_specs(S, G)
    pb = _GDN_PASSES["bwd"]

    def body(cur_ref, prev_ref, ba_ref, cw_ref, gp_ref, ti_ref, uv_ref, wk_ref, duv_ref, dwk_ref, dqd_ref, dke_ref,
             dat_ref, dge_ref, dpre_ref, dba_ref, dgp_ref):
        i = pl.program_id(0)

        @pl.when(i == 0)
        def _():
            dgp_ref[...] = jnp.zeros_like(dgp_ref)

        prev = prev_ref[...] * jnp.where(i == 0, 0.0, 1.0)
        pre = _conv_taps(jnp.concatenate([prev, cur_ref[...]], axis=0), cw_ref[...], GDN_CONV, C * G)
        sg = jax.nn.sigmoid(pre)
        dsilu = sg * (1.0 + pre * (1.0 - sg))
        pairs, args = _gdn_pairs(pre * sg, ba_ref[...], gp_ref[...], G)
        _, vjp1 = jax.vjp(functools.partial(_gdn_stage1, dot=_bdot), *args)

        def take(ref):
            return jnp.stack([ref[h, C * j:C * (j + 1), :] for j, h in pairs])

        t, u_v, w_k = take(ti_ref), take(uv_ref), take(wk_ref)
        d_v = _bdot_raw(t, take(duv_ref), "TN", pb)
        d_rk = _bdot_raw(t, take(dwk_ref), "TN", pb)
        d_l = -(_bdot_raw(d_v, u_v, "NT", pb) + _bdot_raw(d_rk, w_k, "NT", pb))
        d_ge = jnp.stack([dge_ref[j, h][0:1, 0:1] for j, h in pairs])
        dcq, dck, dcv, db, da, dalog, ddtb = vjp1((d_l, d_v, d_rk, take(dqd_ref), take(dat_ref), take(dke_ref), d_ge))
        lane = lax.broadcasted_iota(jnp.int32, (C, 128), 1)
        lane1 = lax.broadcasted_iota(jnp.int32, (1, 128), 1)
        dgp = jnp.zeros((1, 128), F32)
        for j in range(G):
            rows = slice(C * j, C * (j + 1))
            dba = jnp.zeros((C, 128), F32)
            for h in range(GDN_HEADS):
                b = GDN_HEADS * j + h
                for o_, dcx in ((0, dcq), (GDN_WIDTH, dck), (2 * GDN_WIDTH, dcv)):
                    cols = slice(o_ + GDN_DIM * h, o_ + GDN_DIM * (h + 1))
                    dpre_ref[rows, cols] = dcx[b] * dsilu[rows, cols]
                dba = dba + jnp.where(lane == h, db[b], 0.0) + jnp.where(lane == GDN_HEADS + h, da[b], 0.0)
                dgp = dgp + jnp.where(lane1 == h, dalog[b], 0.0) + jnp.where(lane1 == GDN_HEADS + h, ddtb[b], 0.0)
            dba_ref[rows, :] = dba
        dgp_ref[0:1, :] += dgp

    T = C * G
    return pl.pallas_call(
        body, name="gdn_post", grid=(nc // G,),
        in_specs=[sp["cur"], sp["prev"], sp["ba"], sp["cw"], sp["vec"], sp["hc"], sp["hd"], sp["hd"], sp["hd"], sp["hd"],
                  sp["hd"], sp["hd"], sp["hc"], sp["ge"]],
        out_specs=(sp["cur"], pl.BlockSpec((T, 128), lambda i: (i, 0)), pl.BlockSpec((8, 128), lambda i: (0, 0))),
        out_shape=(jax.ShapeDtypeStruct((S, 3 * GDN_WIDTH), F32), jax.ShapeDtypeStruct((S, 128), F32),
                   jax.ShapeDtypeStruct((8, 128), F32)),
        compiler_params=_cparams(("arbitrary",)),
    )(proj, proj, proj, conv_w, gp, tinv, u_v, w_k, d_uv, d_wk, d_qd, d_ke, d_at, d_ge)


def _conv_bwd(dpre, x, xcol0, w, K, name, tc):
    S, Cc = dpre.shape
    T = _pick_tile(S, 256)
    nt, ncol = S // T, Cc // tc
    xo = xcol0 // tc

    def body(d_ref, dn_ref, x_ref, xp_ref, w_ref, dx_ref, dw_ref):
        i = pl.program_id(1)
        dn = dn_ref[...] * jnp.where(i == nt - 1, 0.0, 1.0)
        dv = d_ref[...]
        ext_d = jnp.concatenate([dv, dn], axis=0)
        dx_ref[...] = _conv_taps_t(ext_d, w_ref[...], K, T).astype(dx_ref.dtype)
        xp = xp_ref[...] * jnp.where(i == 0, 0.0, 1.0)
        ext_x = jnp.concatenate([xp, x_ref[...]], axis=0)

        @pl.when(i == 0)
        def _():
            dw_ref[...] = jnp.zeros_like(dw_ref)

        for k in range(K):
            dw_ref[k:k + 1, :] += jnp.sum(dv * _shifted(ext_x, (K - 1) - k, 8, T), axis=0, keepdims=True)

    r8 = T // 8
    return pl.pallas_call(
        body, name=name, grid=(ncol, nt),
        in_specs=[pl.BlockSpec((T, tc), lambda j, i: (i, j)),
                  pl.BlockSpec((8, tc), lambda j, i: (jnp.minimum((i + 1) * r8, S // 8 - 1), j)),
                  pl.BlockSpec((T, tc), lambda j, i: (i, j + xo)),
                  pl.BlockSpec((8, tc), lambda j, i: (jnp.maximum(i * r8 - 1, 0), j + xo)),
                  pl.BlockSpec((K, tc), lambda j, i: (0, j))],
        out_specs=(pl.BlockSpec((T, tc), lambda j, i: (i, j)), pl.BlockSpec((K, tc), lambda j, i: (0, j))),
        out_shape=(jax.ShapeDtypeStruct((S, Cc), _MXU), jax.ShapeDtypeStruct((K, Cc), F32)),
        compiler_params=_cparams(("parallel", "arbitrary")),
    )(dpre, dpre, x, x, w)


def _dil_bias(nt, T):
    d = (np.arange(nt)[:, None, None] * T + np.arange(T)[None, None, :] - np.arange(T)[None, :, None])
    cnt = ((d >= 0) & (d <= 128)).astype(np.float64) + ((d >= 0) & (d % 4 == 0) & (d <= 512)) + ((d >= 0) & (d % 16 == 0))
    return jnp.asarray(np.where(cnt > 0, np.log(np.maximum(cnt, 1.0)), -1e30), dtype=F32)


def _attn_fwd(proj, after=()):
    S = proj.shape[0]
    T = min(ATT_T, S)
    nt = S // T
    bias = _dil_bias(nt, T)
    scale = DIL_DIM ** -0.5
    npair = DIL_WIDTH // 128
    qb0, kb0, vb0 = P_QKVB // 128, (P_QKVB + DIL_WIDTH) // 128, (P_QKVB + 2 * DIL_WIDTH) // 128

    def body(q_ref, k_ref, v_ref, b_ref, *rest):
        o_ref, lse_ref = rest[-2:]
        i = pl.program_id(1)
        qs = (q_ref[...] * scale).astype(_MXU)

        def step(j, carry):
            kt = k_ref[pl.ds(pl.multiple_of(j * T, T), T), :].astype(_MXU)
            vt = v_ref[pl.ds(pl.multiple_of(j * T, T), T), :].astype(_MXU)
            bt = b_ref[i - j]
            out = []
            for hh in range(2):
                m, l, acc = carry[hh]
                sl = slice(hh * DIL_DIM, (hh + 1) * DIL_DIM)
                s = lax.dot_general(kt[:, sl], qs[:, sl], (_NT, ((), ())), preferred_element_type=F32) + bt
                m_new = jnp.maximum(m, jnp.max(s, axis=0, keepdims=True))
                p = jnp.exp(s - m_new)
                a = jnp.exp(m - m_new)
                l = a * l + jnp.sum(p, axis=0, keepdims=True)
                acc = a * acc + lax.dot_general(vt[:, sl], p.astype(_MXU), (_TN, ((), ())), preferred_element_type=F32)
                out.append((m_new, l, acc))
            return tuple(out)

        init = tuple((jnp.full((1, T), -1e30, F32), jnp.zeros((1, T), F32), jnp.zeros((DIL_DIM, T), F32)) for _ in range(2))
        res = lax.fori_loop(0, i + 1, step, init)
        lse_ref[...] = jnp.zeros_like(lse_ref)
        for hh in range(2):
            m, l, acc = res[hh]
            o_ref[:, hh * DIL_DIM:(hh + 1) * DIL_DIM] = (acc / l).T
            lse_ref[hh:hh + 1, :] = m + jnp.log(l)

    return pl.pallas_call(
        body, name="attn_fwd", grid=(npair, nt),
        in_specs=[pl.BlockSpec((T, 128), lambda p, i: (i, qb0 + p)),
                  pl.BlockSpec((S, 128), lambda p, i: (0, kb0 + p)),
                  pl.BlockSpec((S, 128), lambda p, i: (0, vb0 + p)),
                  pl.BlockSpec((nt, T, T), lambda p, i: (0, 0, 0))] + [_ANY] * len(after),
        out_specs=(pl.BlockSpec((T, 128), lambda p, i: (i, GDN_WIDTH // 128 + p)),
                   pl.BlockSpec((None, None, 8, T), lambda p, i: (p, i, 0, 0))),
        out_shape=(jax.ShapeDtypeStruct((S, GDN_WIDTH + DIL_WIDTH), F32), jax.ShapeDtypeStruct((npair, nt, 8, T), F32)),
        compiler_params=_cparams(("parallel", "parallel")),
    )(proj, proj, proj, bias, *after)


def _attn_bwd(proj, mix, lse, d_mix):
    S = proj.shape[0]
    T = min(ATT_T, S)
    nt = S // T
    bias = _dil_bias(nt, T)
    scale = DIL_DIM ** -0.5
    npair = DIL_WIDTH // 128
    qb0, kb0, vb0 = P_QKVB // 128, (P_QKVB + DIL_WIDTH) // 128, (P_QKVB + 2 * DIL_WIDTH) // 128

    def body(q_ref, k_ref, v_ref, o_ref, lse_ref, do_ref, b_ref, dq_ref, dk_ref, dv_ref, dq_scr):
        j = pl.program_id(1)

        @pl.when(j == 0)
        def _():
            dq_scr[...] = jnp.zeros_like(dq_scr)

        kt = k_ref[...].astype(_MXU)
        vt = v_ref[...].astype(_MXU)
        ones = jnp.ones((8, DIL_DIM), F32)

        def step(i, carry):
            rows = pl.ds(pl.multiple_of(i * T, T), T)
            qs = (q_ref[rows, :] * scale).astype(_MXU)
            dov = do_ref[rows, :]
            prod = dov * o_ref[rows, :]
            lsev = lse_ref[i]
            dob = dov.astype(_MXU)
            bt = b_ref[i - j]
            out = []
            dqs = []
            for hh in range(2):
                dk, dv = carry[hh]
                sl = slice(hh * DIL_DIM, (hh + 1) * DIL_DIM)
                s = lax.dot_general(kt[:, sl], qs[:, sl], (_NT, ((), ())), preferred_element_type=F32) + bt
                p = jnp.exp(s - lsev[hh:hh + 1, :])
                delta = lax.dot_general(ones, prod[:, sl], (_NT, ((), ())), precision=_HI, preferred_element_type=F32)[0:1, :]
                dp = lax.dot_general(vt[:, sl], dob[:, sl], (_NT, ((), ())), preferred_element_type=F32)
                ds = (p * (dp - delta)).astype(_MXU)
                dv = dv + lax.dot_general(p.astype(_MXU), dob[:, sl], (_NN, ((), ())), preferred_element_type=F32)
                dk = dk + lax.dot_general(ds, qs[:, sl], (_NN, ((), ())), preferred_element_type=F32)
                dqs.append(lax.dot_general(ds, kt[:, sl], (_TN, ((), ())), preferred_element_type=F32) * scale)
                out.append((dk, dv))
            dq_scr[rows, :] += jnp.concatenate(dqs, axis=1)
            return tuple(out)

        init = tuple((jnp.zeros((T, DIL_DIM), F32), jnp.zeros((T, DIL_DIM), F32)) for _ in range(2))
        res = lax.fori_loop(j, nt, step, init)
        dk_ref[...] = jnp.concatenate([res[0][0], res[1][0]], axis=1).astype(dk_ref.dtype)
        dv_ref[...] = jnp.concatenate([res[0][1], res[1][1]], axis=1).astype(dv_ref.dtype)

        @pl.when(j == nt - 1)
        def _():
            dq_ref[...] = dq_scr[...].astype(dq_ref.dtype)

    full = lambda c0: pl.BlockSpec((S, 128), lambda p, j: (0, c0 + p))
    tile = lambda c0: pl.BlockSpec((T, 128), lambda p, j: (j, c0 + p))
    out3 = jax.ShapeDtypeStruct((S, DIL_WIDTH), _MXU)
    return pl.pallas_call(
        body, name="attn_bwd", grid=(npair, nt),
        in_specs=[full(qb0), tile(kb0), tile(vb0), full(GDN_WIDTH // 128),
                  pl.BlockSpec((None, nt, 8, T), lambda p, j: (p, 0, 0, 0)), full(GDN_WIDTH // 128),
                  pl.BlockSpec((nt, T, T), lambda p, j: (0, 0, 0))],
        out_specs=(full(0), tile(0), tile(0)),
        out_shape=(out3, out3, out3),
        scratch_shapes=[pltpu.VMEM((S, 128), F32)],
        compiler_params=_cparams(("parallel", "arbitrary")),
    )(proj, proj, proj, mix, lse, d_mix, bias)


def _ffn_act(up, cw):
    S, Cc = up.shape[0], up.shape[1] // 2
    T, tc = _pick_tile(S, 256), _pick_tile(Cc, 1536)
    r16 = T // 16
    nct = Cc // tc

    def body(g_ref, gp_ref, u_ref, up_ref, wg_ref, wu_ref, o_ref):
        keep = jnp.where(pl.program_id(1) == 0, 0.0, 1.0)
        cg = _conv_taps(jnp.concatenate([gp_ref[8:16, :].astype(F32) * keep, g_ref[...].astype(F32)], axis=0),
                        wg_ref[...], FFN_CONV, T)
        cu = _conv_taps(jnp.concatenate([up_ref[8:16, :].astype(F32) * keep, u_ref[...].astype(F32)], axis=0),
                        wu_ref[...], FFN_CONV, T)
        o_ref[...] = (_silu(cg) * cu).astype(o_ref.dtype)

    cur = lambda o: pl.BlockSpec((T, tc), lambda j, i: (i, j + o))
    prev = lambda o: pl.BlockSpec((16, tc), lambda j, i: (jnp.maximum(i * r16 - 1, 0), j + o))
    wsp = lambda o: pl.BlockSpec((FFN_CONV, tc), lambda j, i: (0, j + o))
    return pl.pallas_call(
        body, name="ffn_act", grid=(nct, S // T),
        in_specs=[cur(0), prev(0), cur(nct), prev(nct), wsp(0), wsp(nct)], out_specs=cur(0),
        out_shape=jax.ShapeDtypeStruct((S, Cc), _MXU),
        compiler_params=_cparams(("parallel", "parallel")),
    )(up, up, up, up, cw, cw)


def _ffn_act_bwd(d_act, up, cw):
    S, Cc = up.shape[0], up.shape[1] // 2
    T, tc = _pick_tile(S, 256), _pick_tile(Cc, 1536)
    r8, r16 = T // 8, T // 16
    nt = S // T
    nct = Cc // tc
    K = FFN_CONV

    def body(da_ref, dan_ref, g_ref, gp_ref, gn_ref, u_ref, up_ref, un_ref, wg_ref, wu_ref,
             dg_ref, du_ref, dwg_ref, dwu_ref):
        i = pl.program_id(1)
        keep_p = jnp.where(i == 0, 0.0, 1.0)
        keep_n = jnp.where(i == nt - 1, 0.0, 1.0)
        wg, wu = wg_ref[...], wu_ref[...]
        xg = jnp.concatenate([gp_ref[8:16, :].astype(F32) * keep_p, g_ref[...].astype(F32),
                              gn_ref[0:8, :].astype(F32) * keep_n], axis=0)
        xu = jnp.concatenate([up_ref[8:16, :].astype(F32) * keep_p, u_ref[...].astype(F32),
                              un_ref[0:8, :].astype(F32) * keep_n], axis=0)
        cg = _conv_taps(xg, wg, K, T + 8)
        cu = _conv_taps(xu, wu, K, T + 8)
        da = jnp.concatenate([da_ref[...], dan_ref[...] * keep_n], axis=0)
        sg = jax.nn.sigmoid(cg)
        d_cg = da * cu * (sg * (1.0 + cg * (1.0 - sg)))
        d_cu = da * (cg * sg)
        dg_ref[...] = _conv_taps_t(d_cg, wg, K, T).astype(dg_ref.dtype)
        du_ref[...] = _conv_taps_t(d_cu, wu, K, T).astype(du_ref.dtype)

        @pl.when(i == 0)
        def _():
            dwg_ref[...] = jnp.zeros_like(dwg_ref)
            dwu_ref[...] = jnp.zeros_like(dwu_ref)

        for k in range(K):
            dwg_ref[k:k + 1, :] += jnp.sum(d_cg[0:T, :] * _shifted(xg, (K - 1) - k, 8, T), axis=0, keepdims=True)
            dwu_ref[k:k + 1, :] += jnp.sum(d_cu[0:T, :] * _shifted(xu, (K - 1) - k, 8, T), axis=0, keepdims=True)

    cur = lambda o: pl.BlockSpec((T, tc), lambda j, i: (i, j + o))
    prev = lambda o: pl.BlockSpec((16, tc), lambda j, i: (jnp.maximum(i * r16 - 1, 0), j + o))
    nxt = lambda o: pl.BlockSpec((16, tc), lambda j, i: (jnp.minimum((i + 1) * r16, S // 16 - 1), j + o))
    nxt8 = pl.BlockSpec((8, tc), lambda j, i: (jnp.minimum((i + 1) * r8, S // 8 - 1), j))
    wsp = lambda o: pl.BlockSpec((K, tc), lambda j, i: (0, j + o))
    return pl.pallas_call(
        body, name="ffn_act_bwd", grid=(nct, nt),
        in_specs=[cur(0), nxt8, cur(0), prev(0), nxt(0), cur(nct), prev(nct), nxt(nct), wsp(0), wsp(nct)],
        out_specs=(cur(0), cur(0), wsp(0), wsp(0)),
        out_shape=(jax.ShapeDtypeStruct((S, Cc), _MXU), jax.ShapeDtypeStruct((S, Cc), _MXU),
                   jax.ShapeDtypeStruct((K, Cc), F32), jax.ShapeDtypeStruct((K, Cc), F32)),
        compiler_params=_cparams(("parallel", "arbitrary")),
    )(d_act, d_act, up, up, up, up, up, up, cw, cw)


def _local_step(x, tgt, h1, n1w, n2w, fnw, gp, gnw, wp, conv_w, fcw, rest_weights, early_grads):
    proj = _mm(h1, wp, "nn", name="proj")
    u_v, w_k, q_dec, k_end, attn, tinv, g_end = _gdn_pre(proj, conv_w, gp)
    mix, lse = _attn_fwd(proj)
    mix, states = _gdn_scan(u_v, w_k, q_dec, k_end, attn, g_end, proj, gnw, mix, after=[rest_weights[0]([mix])])
    w_out, w_up4, w_down = rest_weights[1]([mix])
    x2 = _mm(mix, w_out, "nn", residual=x, name="outproj")
    h2 = _rmsnorm_fwd(x2, n2w, "norm2")
    up = _mm(h2, w_up4, "nn", b_blocks=True, out_dtype=_MXU, name="up")
    act = _ffn_act(up, fcw)
    x3 = _mm(act, w_down, "nn", residual=x2, name="down")
    loss, dx3, dx3n, d_fnw = _loss_head(x3, fnw, tgt, "loss_head")
    d_act = _mm(dx3n, w_down, "nt", name="d_act")
    d_wdown = _mm(act, dx3n, "tn", name="d_wdown")
    d_upg, d_upu, d_fcwg, d_fcwu = _ffn_act_bwd(d_act, up, fcw)
    d_wup = _mm(h2, d_upg, "tn", place=("blocks", N_CHIPS, 0), tn=w_up4.shape[2], name="d_wgate")
    d_wup = _mm(h2, d_upu, "tn", place=("blocks", N_CHIPS, N_CHIPS // 2), tn=w_up4.shape[2], into=d_wup, name="d_wup")
    token = early_grads[0](d_wup, d_wdown)
    d_h2 = _mm_nt_blocks([d_upg, d_upu], w_up4, "d_h2", after=[token])
    dx2, d_n2w = _rmsnorm_bwd(d_h2, x2, n2w, dx3, "norm2_bwd", after=[token])
    token = early_grads[1](dx2)
    d_mix = _mm(dx2, w_out, "nt", name="d_mix")
    d_wout = _mm(mix, dx2, "tn", name="d_wout")
    dq_b, dk_b, dv_b = _attn_bwd(proj, mix, lse, d_mix)
    d_uv, d_wk, d_qd, d_ke, d_at, d_ge, d_z, d_gnw = _gdn_scan_bwd(u_v, w_k, q_dec, k_end, attn, g_end, proj,
                                                                   gnw + token[0:1, 0:1], states, d_mix)
    d_pre, d_ba, d_gp = _gdn_post(proj, conv_w, gp, tinv, u_v, w_k, d_uv, d_wk, d_qd, d_ke, d_at, d_ge)
    d_qkva, d_convw = _conv_bwd(d_pre, proj, 0, conv_w, GDN_CONV, "gdn_conv_bwd", 512)
    d_proj = jnp.concatenate([d_qkva, d_z.astype(_MXU), dq_b, dk_b, dv_b, d_ba.astype(_MXU),
                              jnp.zeros((x.shape[0], P_COLS - P_BA - 128), _MXU)], axis=1)
    d_wp = _mm(h1, d_proj, "tn", name="d_wp")
    token = early_grads[2](d_wp, d_wout)
    d_h1 = _mm(d_proj, wp, "nt", name="d_h1", after=[token])
    dx, d_n1w = _rmsnorm_bwd(d_h1, x, n1w, dx2, "norm1_bwd", after=[token])
    grads = dict(wp=d_wp, conv_w=d_convw, w_out=d_wout, w_up=d_wup, fcw_g=d_fcwg, fcw_u=d_fcwu, w_down=d_wdown,
                 n1w=d_n1w, n2w=d_n2w, fnw=d_fnw, gp=d_gp, gnw=d_gnw)
    return loss, dx, grads


_HBM = pl.BlockSpec(memory_space=pltpu.HBM)


def _pos():
    return lax.axis_index("x"), lax.axis_index("y"), lax.axis_index("c")


def _other_chips(x, y):
    return [(1 - x, y), (x, 1 - y), (1 - x, 1 - y)]


def _halvable(shape):
    return shape[0] % 32 == 0


def _rows_of_half(shape, half):
    if not _halvable(shape):
        return pl.ds(0, shape[0])
    return pl.ds(pl.multiple_of(half * (shape[0] // 2), 16), shape[0] // 2)


_SEM = pl.BlockSpec(memory_space=pltpu.SEMAPHORE)
_ANY = pl.BlockSpec(memory_space=pl.ANY)
_DATAFLOW = pltpu.SideEffectType.DATAFLOW_SIDE_EFFECTING


def _in_hbm(a):
    return pltpu.with_memory_space_constraint(a, pltpu.HBM)


def _halves_copy(src_refs, land_refs, send_sems, recv_sems, shapes, a, j, block, x, y, c):
    px, py = _other_chips(x, y)[j]
    rows = _rows_of_half(shapes[a], c)
    return pltpu.make_async_remote_copy(
        src_ref=src_refs[a].at[rows, :], dst_ref=land_refs[a].at[block, rows, :], send_sem=send_sems.at[3 * a + j],
        recv_sem=recv_sems.at[3 * a + j], device_id=(px, py, c), device_id_type=MESH)


def _gather_halves_start(shards, after, name):
    n = len(shards)
    shapes = [s.shape for s in shards]

    def body(*refs):
        ins, lands = refs[:n], refs[n:2 * n]
        send_sems, recv_sems = refs[2 * n + 1], refs[2 * n + 2]
        token = refs[-1]
        x, y, c = _pos()
        q = 2 * x + y
        for a in range(n):
            for j in range(3):
                _halves_copy(ins, lands, send_sems, recv_sems, shapes, a, j, q, x, y, c).start()
        token[...] = jnp.zeros_like(token)

    land_shapes = [(N_CHIPS,) + s.shape for s in shards]
    return pl.pallas_call(
        body, name=name,
        out_shape=(pltpu.SemaphoreType.DMA((3 * n,)), pltpu.SemaphoreType.DMA((3 * n,)),
                   *[pltpu.HBM(s.shape, s.dtype) for s in shards],
                   *[pltpu.HBM(ls, s.dtype) for ls, s in zip(land_shapes, shards)],
                   jax.ShapeDtypeStruct((8, 128), F32)),
        in_specs=[_HBM] * (2 * n) + [_ANY],
        out_specs=(_SEM, _SEM, *[_HBM] * (2 * n), pl.BlockSpec(memory_space=pltpu.VMEM)),
        input_output_aliases={a: 2 + a for a in range(2 * n)},
        compiler_params=pltpu.CompilerParams(has_side_effects=_DATAFLOW),
    )(*[_in_hbm(s) for s in shards], *[_in_hbm(lax.empty(ls, s.dtype)) for ls, s in zip(land_shapes, shards)], after)


def _gather_halves_wait(started, after, name):
    send_sems, recv_sems, *thru = started
    n = len(thru) // 2
    shapes = [t.shape for t in thru[:n]]

    def body(*refs):
        ins, lands = refs[:n], refs[n:2 * n]
        send_sems, recv_sems = refs[2 * n], refs[2 * n + 1]
        x, y, c = _pos()
        q = 2 * x + y
        chips = _other_chips(x, y)
        for a in range(n):
            for j, (px, py) in enumerate(chips):
                _halves_copy(ins, lands, send_sems, recv_sems, shapes, a, j, q, x, y, c).wait_send()
                _halves_copy(ins, lands, send_sems, recv_sems, shapes, a, j, 2 * px + py, x, y, c).wait_recv()

    outs = pl.pallas_call(
        body, name=name, out_shape=[pltpu.HBM(t.shape, t.dtype) for t in thru],
        in_specs=[_HBM] * (2 * n) + [_SEM, _SEM] + [_ANY] * len(after), out_specs=[_HBM] * (2 * n),
        input_output_aliases={a: a for a in range(2 * n)},
        compiler_params=pltpu.CompilerParams(has_side_effects=_DATAFLOW),
    )(*thru, send_sems, recv_sems, *after)
    return outs[:n], outs[n:]


def _sibling_fill(gathered, name):
    big = [a for a, g in enumerate(gathered) if _halvable(g.shape[1:])]
    n = len(gathered)

    def body(*refs):
        ins, outs = refs[:n], refs[n:2 * n]
        send_sems, recv_sems = refs[2 * n:]
        x, y, c = _pos()
        chips = _other_chips(x, y)

        def copy(k, j, half):
            a = big[k]
            px, py = chips[j]
            rows = _rows_of_half(gathered[a].shape[1:], half)
            return pltpu.make_async_remote_copy(
                src_ref=ins[a].at[2 * px + py, rows, :], dst_ref=outs[a].at[2 * px + py, rows, :],
                send_sem=send_sems.at[3 * k + j], recv_sem=recv_sems.at[3 * k + j],
                device_id=(x, y, 1 - c), device_id_type=MESH)

        sends = [copy(k, j, c) for k in range(len(big)) for j in range(3)]
        for cp in sends:
            cp.start()
        for k in range(len(big)):
            for j in range(3):
                copy(k, j, 1 - c).wait_recv()
        for cp in sends:
            cp.wait_send()

    return pl.pallas_call(
        body, name=name, in_specs=[_HBM] * n, out_specs=[_HBM] * n,
        out_shape=[jax.ShapeDtypeStruct(g.shape, g.dtype) for g in gathered],
        input_output_aliases={a: a for a in range(n)},
        scratch_shapes=[pltpu.SemaphoreType.DMA((3 * len(big),)), pltpu.SemaphoreType.DMA((3 * len(big),))],
    )(*gathered)


def _fill_copy(refs, send_sems, recv_sems, shapes, a, j, half, x, y, c):
    px, py = _other_chips(x, y)[j]
    rows = _rows_of_half(shapes[a], half)
    return pltpu.make_async_remote_copy(
        src_ref=refs[a].at[2 * px + py, rows, :], dst_ref=refs[a].at[2 * px + py, rows, :],
        send_sem=send_sems.at[3 * a + j], recv_sem=recv_sems.at[3 * a + j],
        device_id=(x, y, 1 - c), device_id_type=MESH)


def _sibling_fill_start(gathered, name):
    n = len(gathered)
    shapes = [g.shape[1:] for g in gathered]

    def body(*refs):
        ins = refs[:n]
        send_sems, recv_sems = refs[n], refs[n + 1]
        token = refs[-1]
        x, y, c = _pos()
        for a in range(n):
            for j in range(3):
                _fill_copy(ins, send_sems, recv_sems, shapes, a, j, c, x, y, c).start()
        token[...] = jnp.zeros_like(token)

    return pl.pallas_call(
        body, name=name,
        out_shape=(pltpu.SemaphoreType.DMA((3 * n,)), pltpu.SemaphoreType.DMA((3 * n,)),
                   *[pltpu.HBM(g.shape, g.dtype) for g in gathered], jax.ShapeDtypeStruct((8, 128), F32)),
        in_specs=[_HBM] * n,
        out_specs=(_SEM, _SEM, *[_HBM] * n, pl.BlockSpec(memory_space=pltpu.VMEM)),
        input_output_aliases={a: 2 + a for a in range(n)},
        compiler_params=pltpu.CompilerParams(has_side_effects=_DATAFLOW),
    )(*[_in_hbm(g) for g in gathered])


def _sibling_fill_wait(started, after, name):
    send_sems, recv_sems, *thru = started
    n = len(thru)
    shapes = [t.shape[1:] for t in thru]

    def body(*refs):
        ins = refs[:n]
        send_sems, recv_sems = refs[n], refs[n + 1]
        x, y, c = _pos()
        for a in range(n):
            for j in range(3):
                _fill_copy(ins, send_sems, recv_sems, shapes, a, j, c, x, y, c).wait_send()
                _fill_copy(ins, send_sems, recv_sems, shapes, a, j, 1 - c, x, y, c).wait_recv()

    return pl.pallas_call(
        body, name=name, out_shape=[pltpu.HBM(t.shape, t.dtype) for t in thru],
        in_specs=[_HBM] * n + [_SEM, _SEM] + [_ANY] * len(after), out_specs=[_HBM] * n,
        input_output_aliases={a: a for a in range(n)},
        compiler_params=pltpu.CompilerParams(has_side_effects=_DATAFLOW),
    )(*thru, send_sems, recv_sems, *after)


def _place_own(shards, gathered, cq, name, carry=()):
    n = len(shards)
    nc = len(carry)
    steps = 4

    def body(cq_ref, *refs):
        for a in range(n):
            refs[2 * n + nc + a][...] = refs[a][...]

    def tile(shape):
        return shape[0] // steps if _halvable(shape) else shape[0]

    in_specs = [pl.BlockSpec((tile(s.shape), s.shape[1]), (lambda i, s_: (i, 0)) if _halvable(s.shape) else (lambda i, s_: (0, 0)))
                for s in shards]
    in_specs += [pl.BlockSpec(memory_space=pl.ANY)] * (n + nc)
    out_specs = [pl.BlockSpec((None, tile(s.shape), s.shape[1]),
                              (lambda i, s_: (s_[1], i, 0)) if _halvable(s.shape) else (lambda i, s_: (s_[1], 0, 0)))
                 for s in shards]
    out_specs += [pl.BlockSpec(memory_space=pl.ANY)] * nc
    gs = pltpu.PrefetchScalarGridSpec(num_scalar_prefetch=1, grid=(steps,), in_specs=in_specs, out_specs=out_specs)
    outs = pl.pallas_call(
        body, name=name, grid_spec=gs,
        out_shape=[jax.ShapeDtypeStruct(g.shape, g.dtype) for g in gathered] + [jax.ShapeDtypeStruct(t.shape, t.dtype) for t in carry],
        input_output_aliases={1 + n + a: a for a in range(n + nc)},
        compiler_params=_cparams(("arbitrary",)),
    )(cq, *shards, *gathered, *carry)
    return (outs[:n], outs[n:]) if nc else outs


def _half_rows(ref, c, rh):
    return ref.at[:, pl.ds(pl.multiple_of(c * rh, 8), rh), :]


def _chips_copy(src_refs, land_refs, send_sems, recv_sems, a, j, x, y, c):
    px, py = _other_chips(x, y)[j]
    return pltpu.make_async_remote_copy(src_ref=src_refs[a].at[2 * px + py], dst_ref=land_refs[a].at[j],
                                        send_sem=send_sems.at[3 * a + j], recv_sem=recv_sems.at[3 * a + j],
                                        device_id=(px, py, c), device_id_type=MESH)


def _grad_chips_start(parts, name):
    n = len(parts)

    def body(*refs):
        ins, lands = refs[:n], refs[n:2 * n]
        send_sems, recv_sems = refs[2 * n], refs[2 * n + 1]
        token = refs[-1]
        x, y, c = _pos()
        for a in range(n):
            for j in range(3):
                _chips_copy(ins, lands, send_sems, recv_sems, a, j, x, y, c).start()
        token[...] = jnp.zeros_like(token)

    land_shapes = [(3,) + p.shape[1:] for p in parts]
    return pl.pallas_call(
        body, name=name,
        out_shape=(pltpu.SemaphoreType.DMA((3 * n,)), pltpu.SemaphoreType.DMA((3 * n,)),
                   *[pltpu.HBM(p.shape, p.dtype) for p in parts],
                   *[pltpu.HBM(ls, p.dtype) for ls, p in zip(land_shapes, parts)],
                   jax.ShapeDtypeStruct((8, 128), F32)),
        in_specs=[_HBM] * (2 * n),
        out_specs=(_SEM, _SEM, *[_HBM] * (2 * n), pl.BlockSpec(memory_space=pltpu.VMEM)),
        input_output_aliases={a: 2 + a for a in range(2 * n)},
        compiler_params=pltpu.CompilerParams(has_side_effects=_DATAFLOW),
    )(*[_in_hbm(p) for p in parts], *[_in_hbm(lax.empty(ls, p.dtype)) for ls, p in zip(land_shapes, parts)])


def _grad_chips_wait(started, after, name):
    send_sems, recv_sems, *thru = started
    n = len(thru) // 2

    def body(*refs):
        ins, lands = refs[:n], refs[n:2 * n]
        send_sems, recv_sems = refs[2 * n], refs[2 * n + 1]
        x, y, c = _pos()
        for a in range(n):
            for j in range(3):
                cp = _chips_copy(ins, lands, send_sems, recv_sems, a, j, x, y, c)
                cp.wait_send()
                cp.wait_recv()

    outs = pl.pallas_call(
        body, name=name, out_shape=[pltpu.HBM(t.shape, t.dtype) for t in thru],
        in_specs=[_HBM] * (2 * n) + [_SEM, _SEM] + [_ANY] * len(after), out_specs=[_HBM] * (2 * n),
        input_output_aliases={a: a for a in range(2 * n)},
        compiler_params=pltpu.CompilerParams(has_side_effects=_DATAFLOW),
    )(*thru, send_sems, recv_sems, *after)
    return outs[n:]


def _sibling_copy(src_refs, land_refs, send_sems, recv_sems, rhs, a, c, x, y):
    return pltpu.make_async_remote_copy(src_ref=_half_rows(src_refs[a], 1 - c, rhs[a]), dst_ref=land_refs[a],
                                        send_sem=send_sems.at[a], recv_sem=recv_sems.at[a],
                                        device_id=(x, y, 1 - c), device_id_type=MESH)


def _grad_sibling_start(fams, name):
    n = len(fams)
    rhs = [f.shape[1] // 2 for f in fams]

    def body(*refs):
        ins, lands = refs[:n], refs[n:2 * n]
        send_sems, recv_sems = refs[2 * n], refs[2 * n + 1]
        token = refs[-1]
        x, y, c = _pos()
        for a in range(n):
            _sibling_copy(ins, lands, send_sems, recv_sems, rhs, a, c, x, y).start()
        token[...] = jnp.zeros_like(token)

    land_shapes = [(f.shape[0], f.shape[1] // 2, f.shape[2]) for f in fams]
    return pl.pallas_call(
        body, name=name,
        out_shape=(pltpu.SemaphoreType.DMA((n,)), pltpu.SemaphoreType.DMA((n,)),
                   *[pltpu.HBM(f.shape, f.dtype) for f in fams],
                   *[pltpu.HBM(ls, f.dtype) for ls, f in zip(land_shapes, fams)],
                   jax.ShapeDtypeStruct((8, 128), F32)),
        in_specs=[_HBM] * (2 * n),
        out_specs=(_SEM, _SEM, *[_HBM] * (2 * n), pl.BlockSpec(memory_space=pltpu.VMEM)),
        input_output_aliases={a: 2 + a for a in range(2 * n)},
        compiler_params=pltpu.CompilerParams(has_side_effects=_DATAFLOW),
    )(*[_in_hbm(f) for f in fams], *[_in_hbm(lax.empty(ls, f.dtype)) for ls, f in zip(land_shapes, fams)])


def _grad_sibling_wait(started, after, name):
    send_sems, recv_sems, *thru = started
    n = len(thru) // 2
    rhs = [t.shape[1] // 2 for t in thru[:n]]

    def body(*refs):
        ins, lands = refs[:n], refs[n:2 * n]
        send_sems, recv_sems = refs[2 * n], refs[2 * n + 1]
        x, y, c = _pos()
        for a in range(n):
            cp = _sibling_copy(ins, lands, send_sems, recv_sems, rhs, a, c, x, y)
            cp.wait_send()
            cp.wait_recv()

    outs = pl.pallas_call(
        body, name=name, out_shape=[pltpu.HBM(t.shape, t.dtype) for t in thru],
        in_specs=[_HBM] * (2 * n) + [_SEM, _SEM] + [_ANY] * len(after), out_specs=[_HBM] * (2 * n),
        input_output_aliases={a: a for a in range(2 * n)},
        compiler_params=pltpu.CompilerParams(has_side_effects=_DATAFLOW),
    )(*thru, send_sems, recv_sems, *after)
    return outs[:n], outs[n:]


def _grad_share(fulls, name, small=None):
    n = len(fulls)
    ns = 0 if small is None else 1
    rhs = [f.shape[0] // 2 for f in fulls]

    def body(*refs):
        ins, outs = refs[:n], refs[n + ns:2 * n + ns]
        send_sems, recv_sems = refs[2 * (n + ns)], refs[2 * (n + ns) + 1]
        x, y, c = _pos()

        def copy(a, half):
            rows = pl.ds(pl.multiple_of(half * rhs[a], 8), rhs[a])
            return pltpu.make_async_remote_copy(src_ref=ins[a].at[rows, :], dst_ref=outs[a].at[rows, :],
                                                send_sem=send_sems.at[7 * ns + a], recv_sem=recv_sems.at[7 * ns + a],
                                                device_id=(x, y, 1 - c), device_id_type=MESH)

        sends = [copy(a, c) for a in range(n)]
        for cp in sends:
            cp.start()
        if ns:
            small_ref, all_ref = refs[n], refs[2 * n + 1]
            me = 4 * x + 2 * y + c

            def peer(r):
                dx, dy, dc = (r >> 2) & 1, (r >> 1) & 1, r & 1
                return (x if dx == 0 else 1 - x), (y if dy == 0 else 1 - y), (c if dc == 0 else 1 - c)

            def small_copy(r, slot):
                return pltpu.make_async_remote_copy(src_ref=small_ref, dst_ref=all_ref.at[slot], send_sem=send_sems.at[r - 1],
                                                    recv_sem=recv_sems.at[r - 1], device_id=peer(r), device_id_type=MESH)

            smalls = [small_copy(r, me) for r in range(1, 8)]
            for cp in smalls:
                cp.start()
            for r in range(1, 8):
                px, py, pc = peer(r)
                small_copy(r, 4 * px + 2 * py + pc).wait_recv()
            sends = sends + smalls
        for a in range(n):
            copy(a, 1 - c).wait_recv()
        for cp in sends:
            cp.wait_send()

    return pl.pallas_call(
        body, name=name, in_specs=[_HBM] * (n + ns), out_specs=[_HBM] * (n + ns),
        out_shape=[jax.ShapeDtypeStruct(f.shape, f.dtype) for f in fulls]
        + ([jax.ShapeDtypeStruct((8,) + small.shape, small.dtype)] if ns else []),
        input_output_aliases={a: a for a in range(n)},
        scratch_shapes=[pltpu.SemaphoreType.DMA((7 * ns + n,)), pltpu.SemaphoreType.DMA((7 * ns + n,))],
    )(*fulls, *([small] if ns else []))


def _add_sibling(own, recv, cq, name):
    nb, R, Cc = own.shape
    Rh = R // 2

    def body(cq_ref, a_ref, b_ref, o32_ref, o16_ref):
        s = a_ref[0] + b_ref[0]
        mine = pl.program_id(0) == cq_ref[1]

        @pl.when(mine)
        def _():
            o32_ref[...] = s

        @pl.when(jnp.logical_not(mine))
        def _():
            o16_ref[0] = s.astype(o16_ref.dtype)

    sp = pl.BlockSpec((1, Rh, Cc), lambda b, s: (b, 0, 0))
    gs = pltpu.PrefetchScalarGridSpec(
        num_scalar_prefetch=1, grid=(nb,),
        in_specs=[pl.BlockSpec((1, Rh, Cc), lambda b, s: (b, s[0], 0)), sp],
        out_specs=[pl.BlockSpec((Rh, Cc), lambda b, s: (0, 0)), sp])
    return pl.pallas_call(
        body, name=name, grid_spec=gs,
        out_shape=[jax.ShapeDtypeStruct((Rh, Cc), F32), jax.ShapeDtypeStruct((nb, Rh, Cc), _MXU)],
        compiler_params=_cparams(("arbitrary",)),
    )(cq, own, recv)


def _add_chips(part32, recv3, cq, name):
    Rh, Cc = part32.shape

    def body(cq_ref, a_ref, b_ref, o_ref):
        acc = a_ref[...]
        for j in range(3):
            acc = acc + b_ref[j].astype(F32)
        o_ref[...] = acc

    gs = pltpu.PrefetchScalarGridSpec(
        num_scalar_prefetch=1, grid=(1,),
        in_specs=[pl.BlockSpec((Rh, Cc), lambda i, s: (0, 0)), pl.BlockSpec((3, Rh, Cc), lambda i, s: (0, 0, 0))],
        out_specs=pl.BlockSpec((Rh, Cc), lambda i, s: (s[0], 0)))
    return pl.pallas_call(
        body, name=name, grid_spec=gs, out_shape=jax.ShapeDtypeStruct((2 * Rh, Cc), F32),
        compiler_params=_cparams(("arbitrary",)),
    )(cq, part32, recv3)


def _transposed(g):
    Dm, n = g.shape
    pad = -n % 128

    def body(g_ref, o_ref):
        xp = jnp.concatenate([g_ref[...], jnp.zeros((Dm, pad), F32)], axis=1)
        o_ref[...] = xp.T[:n, :]

    return pl.pallas_call(body, name="transposed", out_shape=jax.ShapeDtypeStruct((n, Dm), F32),
                          compiler_params=_cparams(vmem=V7X_VMEM_LIMIT))(g)


def _adamw(w, g, m, v, name):
    R, Cc = w.shape
    T = max([t for t in range(8, 257, 8) if R % t == 0], default=R)

    def body(w_ref, g_ref, m_ref, v_ref, d_ref, mo_ref, vo_ref):
        d_ref[...], mo_ref[...], vo_ref[...] = _adamw_math(w_ref[...], g_ref[...], m_ref[...], v_ref[...])

    sp = pl.BlockSpec((T, Cc), lambda i: (i, 0))
    sh = jax.ShapeDtypeStruct((R, Cc), F32)
    return pl.pallas_call(
        body, name=name, grid=(R // T,), in_specs=[sp] * 4, out_specs=(sp, sp, sp), out_shape=(sh, sh, sh),
        compiler_params=_cparams(("parallel",)),
    )(w, g, m, v)


SMALL_ROWS = 32
ROW_CONV, ROW_FCG, ROW_FCU = 5, 13, 22


def _adamw_math(w, g, m, v):
    mn = ADAM_B1 * m + (1.0 - ADAM_B1) * g
    vn = ADAM_B2 * v + (1.0 - ADAM_B2) * (g * g)
    c1 = 1.0 / (1.0 - ADAM_B1 ** ADAM_STEP)
    c2 = 1.0 / (1.0 - ADAM_B2 ** ADAM_STEP)
    return -ADAM_LR * ((mn * c1) / (jnp.sqrt(vn * c2) + ADAM_EPS) + ADAM_WD * w), mn, vn


def _pack_small(n1, n2, fn, gp, gn, conv, fcg, fcu, loss):
    W = D_MODEL

    def body(n1_ref, n2_ref, fn_ref, gp_ref, gn_ref, conv_ref, fcg_ref, fcu_ref, loss_ref, o_ref):
        o_ref[...] = jnp.zeros_like(o_ref)
        o_ref[0:1, :] = n1_ref[...]
        o_ref[1:2, :] = n2_ref[...]
        o_ref[2:3, :] = fn_ref[...]
        o_ref[3:4, 0:8] = gp_ref[0:1, 0:8]
        o_ref[3:4, 8:9] = loss_ref[0:1, 0:1]
        o_ref[4:5, 0:128] = gn_ref[...]
        for i in range(GDN_CONV):
            o_ref[ROW_CONV + 2 * i:ROW_CONV + 2 * i + 1, :] = conv_ref[i:i + 1, 0:W]
            o_ref[ROW_CONV + 2 * i + 1:ROW_CONV + 2 * i + 2, 0:3 * GDN_WIDTH - W] = conv_ref[i:i + 1, W:3 * GDN_WIDTH]
        for r0, ref in ((ROW_FCG, fcg_ref), (ROW_FCU, fcu_ref)):
            for i in range(FFN_CONV):
                for k in range(3):
                    n = min(W, D_FF - k * W)
                    o_ref[r0 + 3 * i + k:r0 + 3 * i + k + 1, 0:n] = ref[i:i + 1, k * W:k * W + n]

    return pl.pallas_call(body, name="pack_small", out_shape=jax.ShapeDtypeStruct((SMALL_ROWS, W), F32))(
        n1, n2, fn, gp, gn, conv, fcg, fcu, loss)


def _small_step(meq, small_all, small, ws, ms, vs):
    W = D_MODEL
    n = len(ws)
    cw, fw = ws[6].shape[1], ws[7].shape[1]

    def body(meq_ref, all_ref, own_ref, *refs):
        w_refs, m_refs, v_refs = refs[:n], refs[n:2 * n], refs[2 * n:3 * n]
        loss_ref = refs[3 * n]
        outs = refs[3 * n + 1:]
        me, q = meq_ref[0], meq_ref[1]
        red = None
        for d in range(8):
            term = jnp.where(me == d, own_ref[...], all_ref[d])
            red = term if red is None else red + term
        loss_ref[...] = jnp.broadcast_to(red[3:4, 8:9], loss_ref.shape)
        conv = [jnp.concatenate([red[ROW_CONV + 2 * i:ROW_CONV + 2 * i + 1, :],
                                 red[ROW_CONV + 2 * i + 1:ROW_CONV + 2 * i + 2, 0:3 * GDN_WIDTH - W]], axis=1)
                for i in range(GDN_CONV)]
        conv = jnp.concatenate(conv, axis=0)

        def fc_rows(r0):
            rows = [jnp.concatenate([red[r0 + 3 * i + k:r0 + 3 * i + k + 1, 0:min(W, D_FF - k * W)] for k in range(3)], axis=1)
                    for i in range(FFN_CONV)]
            return jnp.concatenate(rows, axis=0)

        fc = jnp.concatenate([fc_rows(ROW_FCG), fc_rows(ROW_FCU)], axis=1)

        def chip_block(full, width):
            out = None
            for j in range(N_CHIPS):
                term = jnp.where(q == j, full[:, width * j:width * (j + 1)], 0.0)
                out = term if out is None else out + term
            return out

        grads = [red[0:1, :], red[1:2, :], red[2:3, :], red[3:4, 0:4], red[3:4, 4:8], red[4:5, 0:128],
                 chip_block(conv, cw), chip_block(fc, fw)]
        for k in range(n):
            d_, m_, v_ = _adamw_math(w_refs[k][...], grads[k], m_refs[k][...], v_refs[k][...])
            outs[4 * k][...] = grads[k]
            outs[4 * k + 1][...] = d_
            outs[4 * k + 2][...] = m_
            outs[4 * k + 3][...] = v_

    full = lambda a: pl.BlockSpec(a.shape, lambda i, s_, nd=len(a.shape): (0,) * nd)
    arrays = [small_all, small, *ws, *ms, *vs]
    out_shapes = [jax.ShapeDtypeStruct((8, 128), F32)] + [jax.ShapeDtypeStruct(w.shape, F32) for w in ws for _ in range(4)]
    gs = pltpu.PrefetchScalarGridSpec(
        num_scalar_prefetch=1, grid=(1,), in_specs=[full(a) for a in arrays],
        out_specs=[pl.BlockSpec(o.shape, lambda i, s_, nd=len(o.shape): (0,) * nd) for o in out_shapes])
    return pl.pallas_call(body, name="small_step", grid_spec=gs, out_shape=out_shapes)(meq, *arrays)


def _pad_lanes(v, n=D_MODEL):
    return jnp.pad(v, ((0, 0), (0, n - v.shape[1])))


def kernel(x, norm1_w, w_in, conv_qkv_w, a_log, dt_bias, gdn_norm_w, w_out, norm2_w, w_up, ffn_conv_w, w_down, final_norm_w, loss_target, m_norm1_w, m_w_in, m_conv_qkv_w, m_a_log, m_dt_bias, m_gdn_norm_w, m_w_out, m_norm2_w, m_w_up, m_ffn_conv_w, m_w_down, m_final_norm_w, v_norm1_w, v_w_in, v_conv_qkv_w, v_a_log, v_dt_bias, v_gdn_norm_w, v_w_out, v_norm2_w, v_w_up, v_ffn_conv_w, v_w_down, v_final_norm_w):
    c = lax.axis_index("c")
    q = 2 * lax.axis_index("x") + lax.axis_index("y")
    S = x.shape[1]
    cq = jnp.stack([c, q]).astype(jnp.int32)

    *in_started, in_token = _gather_halves_start([w_in[0].astype(_MXU), conv_qkv_w[0], ffn_conv_w[0]], x, "gather_in_start")
    w_in_l, m_w_in_l, v_w_in_l = (jnp.swapaxes(a + in_token[0:1, 0:1], 1, 2)[0] for a in (w_in, m_w_in, v_w_in))
    h1 = _rmsnorm_fwd(x[0], norm1_w, "norm1", after=[in_token])
    rest = [(a[0] + in_token[0:1, 0:1]).astype(_MXU) for a in (w_out, w_up, w_down)]
    in_shards, got_in = _gather_halves_wait(in_started, [w_in_l, m_w_in_l, v_w_in_l, h1, *rest], "gather_in_wait")
    (g_in, g_conv, g_fconv), (w_in_l, m_w_in_l, v_w_in_l) = _place_own(
        in_shards, _sibling_fill(got_in, "fill_in"), cq, "place_in", carry=[w_in_l, m_w_in_l, v_w_in_l])
    *rest_started, token = _gather_halves_start(rest, g_conv, "gather_rest_start")

    rest_state = {}

    def rest_arrived(after):
        rest_state["shards"], got = _gather_halves_wait(rest_started, after, "gather_rest_wait")
        *rest_state["fill"], tok = _sibling_fill_start(got, "fill_rest_start")
        return tok

    def rest_filled(after):
        got = _sibling_fill_wait(rest_state["fill"], after, "fill_rest_wait")
        g_out, g_up, g_down = _place_own(rest_state["shards"], got, cq, "place_rest")
        return g_out.reshape(D_MODEL, D_MODEL), g_up, g_down.reshape(D_FF, D_MODEL)

    rest_weights = (rest_arrived, rest_filled)
    wp = _wp_assemble(g_in, [token])
    conv_f = jnp.concatenate([g_conv[i] for i in range(N_CHIPS)], axis=1)
    fcw = jnp.concatenate([g_fconv[i] for i in range(N_CHIPS)], axis=1)
    gp = _pad_lanes(jnp.concatenate([a_log, dt_bias], axis=1), 128)
    fnw = final_norm_w[None, :]
    early = {}

    def early_sibling(d_wup, d_wdown):
        *early["sibling"], tok = _grad_sibling_start([d_wup, d_wdown.reshape(N_CHIPS, D_FF // N_CHIPS, D_MODEL)],
                                                     "grad_sibling_early_start")
        return tok

    def early_chips(dx2):
        fams_e, got_e = _grad_sibling_wait(early["sibling"], [dx2], "grad_sibling_early_wait")
        early["parts"] = [_add_sibling(f, r, cq, "add_sibling_" + nm) for f, r, nm in zip(fams_e, got_e, ("w_up", "w_down"))]
        *early["started"], tok = _grad_chips_start([p[1] for p in early["parts"]], "grad_chips_start")
        return tok

    def late_sibling(d_wp, d_wout):
        *early["late_sibling"], tok = _grad_sibling_start(
            [_win_split(d_wp), d_wout.reshape(N_CHIPS, D_MODEL // N_CHIPS, D_MODEL)], "grad_sibling_late_start")
        return tok

    early_grads = (early_sibling, early_chips, late_sibling)

    loss_l, dx, g = _local_step(x[0], loss_target[0], h1, norm1_w, norm2_w, fnw, gp, gdn_norm_w, wp,
                                conv_f, fcw, rest_weights, early_grads)
    small = _pack_small(g["n1w"], g["n2w"], g["fnw"], g["gp"], g["gnw"], g["conv_w"], g["fcw_g"], g["fcw_u"], loss_l)
    fams, got = _grad_sibling_wait(early["late_sibling"], [dx], "grad_sibling_late_wait")
    parts = [_add_sibling(f, r, cq, "add_sibling_" + nm) for f, r, nm in zip(fams, got, ("w_in", "w_out"))]
    *late_started, late_token = _grad_chips_start([p[1] for p in parts], "grad_chips_late_start")
    got3_e = _grad_chips_wait(early["started"], [dx, g["wp"], late_token], "grad_chips_wait")
    g_w_up, g_w_down = _grad_share(
        [_add_chips(p[0], r3, cq, "add_chips_" + nm) for p, r3, nm in zip(early["parts"], got3_e, ("w_up", "w_down"))],
        "grad_share_early")
    big = {}

    def adamw_big(nm, w, gg, m, v):
        d_, m_, v_ = _adamw(w[0], gg, m[0], v[0], "adamw_" + nm)
        big[nm] = (gg[None], d_[None], m_[None], v_[None])

    adamw_big("w_up", w_up, g_w_up, m_w_up, v_w_up)
    adamw_big("w_down", w_down, g_w_down, m_w_down, v_w_down)
    got3 = _grad_chips_wait(late_started, [big["w_up"][1], big["w_down"][1]], "grad_chips_late_wait")
    g_w_in, g_w_out, small_all = _grad_share(
        [_add_chips(p[0], r3, cq, "add_chips_" + nm) for p, r3, nm in zip(parts, got3, ("w_in", "w_out"))],
        "grad_share_late", small)
    g_t = _transposed(g_w_in)
    d_t, m_t, v_t = _adamw(w_in_l, g_t, m_w_in_l, v_w_in_l, "adamw_w_in")
    big["w_in"] = tuple(jnp.swapaxes(t[None], 1, 2) for t in (g_t, d_t, m_t, v_t))
    adamw_big("w_out", w_out, g_w_out, m_w_out, v_w_out)
    small_names = ["norm1_w", "norm2_w", "final_norm_w", "a_log", "dt_bias", "gdn_norm_w", "conv_qkv_w", "ffn_conv_w"]
    loss_b, *small_out = _small_step(
        jnp.stack([2 * q + c, q]).astype(jnp.int32), small_all, small,
        [norm1_w, norm2_w, final_norm_w[None], a_log, dt_bias, gdn_norm_w, conv_qkv_w[0], ffn_conv_w[0]],
        [m_norm1_w, m_norm2_w, m_final_norm_w[None], m_a_log, m_dt_bias, m_gdn_norm_w, m_conv_qkv_w[0], m_ffn_conv_w[0]],
        [v_norm1_w, v_norm2_w, v_final_norm_w[None], v_a_log, v_dt_bias, v_gdn_norm_w, v_conv_qkv_w[0], v_ffn_conv_w[0]])
    like = dict(final_norm_w=lambda t: t[0], conv_qkv_w=lambda t: t[None], ffn_conv_w=lambda t: t[None])
    for k, nm in enumerate(small_names):
        big[nm] = tuple(like.get(nm, lambda t: t)(t) for t in small_out[4 * k:4 * k + 4])
    names = ["norm1_w", "w_in", "conv_qkv_w", "a_log", "dt_bias", "gdn_norm_w", "w_out", "norm2_w", "w_up",
             "ffn_conv_w", "w_down", "final_norm_w"]
    return (loss_b[0, 0], dx[None], *[big[n][0] for n in names], *[big[n][1] for n in names],
            *[big[n][2] for n in names], *[big[n][3] for n in names])
```

```python
import functools
import math

import numpy as np
import jax
import jax.numpy as jnp
from jax import lax
from jax.experimental import pallas as pl
from jax.experimental.pallas import tpu as pltpu

F32 = jnp.float32
BF16 = jnp.bfloat16
_MXU = jnp.bfloat16
_HI = lax.Precision.HIGHEST
EPS = 1e-6
V7X_VMEM_LIMIT = 56 * 1024 * 1024
MESH = pl.DeviceIdType.MESH

D_MODEL = 1024
GDN_HEADS, GDN_DIM, GDN_CHUNK, GDN_CONV = 4, 128, 64, 4
GDN_WIDTH = GDN_HEADS * GDN_DIM
DIL_HEADS, DIL_DIM = 8, 64
DIL_WIDTH = DIL_HEADS * DIL_DIM
D_FF, FFN_CONV = 2816, 3
IN_COLS = 3592
P_COLS = 3840
P_Z, P_QKVB, P_BA = 1536, 2048, 3584
ATT_T = 1024
ADAM_LR, ADAM_B1, ADAM_B2, ADAM_EPS, ADAM_WD, ADAM_STEP = 0.001, 0.9, 0.999, 1e-08, 0.01, 10
N_CHIPS = 4


def _cparams(sem=None, vmem=None):
    kw = {}
    if sem is not None:
        kw["dimension_semantics"] = sem
    if vmem is not None:
        kw["vmem_limit_bytes"] = vmem
    return pltpu.CompilerParams(**kw)


def _silu(x):
    return x * jax.nn.sigmoid(x)


def _pick_tile(n, cap):
    best = None
    for t in range(128, min(n, cap) + 1, 128):
        if n % t == 0:
            best = t
    return best or n


def _mm(a, b, mode, *, out_dtype=F32, residual=None, name, b_blocks=False, place=None, into=None, tn=None, after=()):
    if mode == "nn":
        M, K = a.shape
        N = b.shape[0] * b.shape[2] if b_blocks else b.shape[1]
    elif mode == "nt":
        (M, K), (N, _) = a.shape, b.shape
    else:
        (K, M), (_, N) = a.shape, b.shape
    tm = _pick_tile(M, 1024)
    tn = b.shape[2] if b_blocks else (tn or _pick_tile(N, 1536))

    def vmem(tm, tn):
        return 2 * (tm * K * a.dtype.itemsize + tn * K * b.dtype.itemsize
                    + tm * tn * (jnp.dtype(out_dtype).itemsize + (4 if residual is not None else 0))) + 3 * tm * tn * 4

    fixed_tn = b_blocks or (place is not None and place[0] == "blocks")
    while vmem(tm, tn) > 40 * 1024 * 1024:
        if (tm >= tn or fixed_tn) and tm % 256 == 0:
            tm //= 2
        elif tn % 256 == 0 and not fixed_tn:
            tn //= 2
        else:
            tm //= 2
    a_spec = pl.BlockSpec((K, tm), lambda j, i: (0, i)) if mode == "tn" else pl.BlockSpec((tm, K), lambda j, i: (i, 0))
    if b_blocks:
        b_spec = pl.BlockSpec((None, K, tn), lambda j, i: (j, 0, 0))
    else:
        b_spec = pl.BlockSpec((tn, K), lambda j, i: (j, 0)) if mode == "nt" else pl.BlockSpec((K, tn), lambda j, i: (0, j))
    r_spec = pl.BlockSpec((tm, tn), lambda j, i: (i, j))
    if place is None:
        o_spec, o_shape = r_spec, (M, N)
    elif place[0] == "rows":
        off = place[2] // tm
        o_spec, o_shape = pl.BlockSpec((tm, tn), lambda j, i: (i + off, j)), (place[1], N)
    else:
        off = place[2]
        o_spec, o_shape = pl.BlockSpec((None, tm, tn), lambda j, i: (j + off, i, 0)), (place[1], M, tn)
    dims = {"nn": (((1,), (0,)), ((), ())), "nt": (((1,), (1,)), ((), ())), "tn": (((0,), (0,)), ((), ()))}[mode]

    def body(*refs):
        a_ref, b_ref = refs[0], refs[1]
        o_ref = refs[-1]
        acc = lax.dot_general(a_ref[...].astype(_MXU), b_ref[...].astype(_MXU), dims, preferred_element_type=F32)
        if residual is not None:
            acc = acc + refs[2][...]
        o_ref[...] = acc.astype(out_dtype)

    ins, specs, alias = [a, b], [a_spec, b_spec], {}
    if residual is not None:
        ins.append(residual)
        specs.append(r_spec)
    if into is not None:
        alias = {len(ins): 0}
        ins.append(into)
        specs.append(pl.BlockSpec(memory_space=pl.ANY))
    ins += list(after)
    specs += [pl.BlockSpec(memory_space=pl.ANY)] * len(after)
    return pl.pallas_call(
        body, name=name, grid=(N // tn, M // tm), in_specs=specs, out_specs=o_spec,
        out_shape=jax.ShapeDtypeStruct(o_shape, out_dtype), input_output_aliases=alias,
        compiler_params=_cparams(("parallel", "parallel"), V7X_VMEM_LIMIT),
    )(*ins)


def _mm_nt_blocks(a_list, b4, name, after=()):
    M = a_list[0].shape[0]
    nb, N, Kb = b4.shape
    tm, tn = _pick_tile(M, 1024), _pick_tile(N, 512)

    def body(a0_ref, a1_ref, b_ref, *rest):
        o_ref = rest[-1]
        acc = None
        for blk in range(nb):
            a_ref = (a0_ref, a1_ref)[blk // 2]
            lo = (blk % 2) * Kb
            t = lax.dot_general(a_ref[:, lo:lo + Kb].astype(_MXU), b_ref[blk].astype(_MXU), (((1,), (1,)), ((), ())),
                                preferred_element_type=F32)
            acc = t if acc is None else acc + t
        o_ref[...] = acc

    a_spec = pl.BlockSpec((tm, 2 * Kb), lambda j, i: (i, 0))
    return pl.pallas_call(
        body, name=name, grid=(N // tn, M // tm),
        in_specs=[a_spec, a_spec, pl.BlockSpec((nb, tn, Kb), lambda j, i: (0, j, 0))]
        + [pl.BlockSpec(memory_space=pl.ANY)] * len(after),
        out_specs=pl.BlockSpec((tm, tn), lambda j, i: (i, j)), out_shape=jax.ShapeDtypeStruct((M, N), F32),
        compiler_params=_cparams(("parallel", "parallel"), V7X_VMEM_LIMIT),
    )(a_list[0], a_list[1], b4, *after)


def _wp_assemble(g_in, after=()):
    nb, Dm, Wb = g_in.shape
    T = 256
    n_lo = P_QKVB - 2 * Wb

    def body(g_ref, *rest):
        g2 = g_ref[2]
        rest[-1][...] = jnp.concatenate(
            [g_ref[0], g_ref[1], g2[:, :n_lo], g2[:, n_lo + 8:], g_ref[3], g2[:, n_lo:n_lo + 8],
             jnp.zeros((T, P_COLS - P_BA - 8), g_in.dtype)], axis=1)

    return pl.pallas_call(
        body, name="wp_assemble", grid=(Dm // T,),
        in_specs=[pl.BlockSpec((nb, T, Wb), lambda i: (0, i, 0))] + [pl.BlockSpec(memory_space=pl.ANY)] * len(after),
        out_specs=pl.BlockSpec((T, P_COLS), lambda i: (i, 0)), out_shape=jax.ShapeDtypeStruct((Dm, P_COLS), g_in.dtype),
        compiler_params=_cparams(("parallel",)),
    )(g_in, *after)


def _rmsnorm_fwd(x, w, name, after=()):
    S, D = x.shape
    T = _pick_tile(S, 512)

    def body(x_ref, w_ref, *rest):
        xv = x_ref[...]
        rs = lax.rsqrt(jnp.mean(xv * xv, axis=-1, keepdims=True) + EPS)
        rest[-1][...] = (xv * rs * w_ref[...]).astype(rest[-1].dtype)

    return pl.pallas_call(
        body, name=name, grid=(S // T,),
        in_specs=[pl.BlockSpec((T, D), lambda i: (i, 0)), pl.BlockSpec((1, D), lambda i: (0, 0))] + [_ANY] * len(after),
        out_specs=pl.BlockSpec((T, D), lambda i: (i, 0)),
        out_shape=jax.ShapeDtypeStruct((S, D), _MXU),
        compiler_params=_cparams(("parallel",)),
    )(x, w, *after)


def _rmsnorm_bwd(dh, x, w, dres, name, after=()):
    S, D = x.shape
    T = _pick_tile(S, 512)

    def body(dh_ref, x_ref, w_ref, dres_ref, *rest):
        dx_ref, dw_ref = rest[-2:]
        xv = x_ref[...]
        rs = lax.rsqrt(jnp.mean(xv * xv, axis=-1, keepdims=True) + EPS)
        xn = xv * rs
        dhv = dh_ref[...]
        dxn = dhv * w_ref[...]
        dx_ref[...] = dres_ref[...] + rs * (dxn - xn * jnp.mean(dxn * xn, axis=-1, keepdims=True))

        @pl.when(pl.program_id(0) == 0)
        def _():
            dw_ref[...] = jnp.zeros_like(dw_ref)

        dw_ref[...] += jnp.sum(dhv * xn, axis=0, keepdims=True)

    row = pl.BlockSpec((T, D), lambda i: (i, 0))
    vec = pl.BlockSpec((1, D), lambda i: (0, 0))
    return pl.pallas_call(
        body, name=name, grid=(S // T,), in_specs=[row, row, vec, row] + [_ANY] * len(after), out_specs=(row, vec),
        out_shape=(jax.ShapeDtypeStruct((S, D), F32), jax.ShapeDtypeStruct((1, D), F32)),
        compiler_params=_cparams(("arbitrary",)),
    )(dh, x, w, dres, *after)


def _loss_head(x3, w, tgt, name):
    S, D = x3.shape
    T = _pick_tile(S, 512)

    def body(x_ref, w_ref, t_ref, loss_ref, dx_ref, dxn_ref, dw_ref):
        xv = x_ref[...]
        rs = lax.rsqrt(jnp.mean(xv * xv, axis=-1, keepdims=True) + EPS)
        xn = xv * rs
        err = xn * w_ref[...] - t_ref[...]
        dy = err * (1.0 / D)
        dxn = dy * w_ref[...]
        dxv = rs * (dxn - xn * jnp.mean(dxn * xn, axis=-1, keepdims=True))
        dx_ref[...] = dxv
        dxn_ref[...] = dxv.astype(dxn_ref.dtype)

        @pl.when(pl.program_id(0) == 0)
        def _():
            dw_ref[...] = jnp.zeros_like(dw_ref)
            loss_ref[...] = jnp.zeros_like(loss_ref)

        dw_ref[...] += jnp.sum(dy * xn, axis=0, keepdims=True)
        part = jnp.sum(jnp.sum(err * err, axis=-1, keepdims=True), axis=0, keepdims=True) * (0.5 / D)
        loss_ref[...] += jnp.broadcast_to(part, loss_ref.shape)

    row = pl.BlockSpec((T, D), lambda i: (i, 0))
    vec = pl.BlockSpec((1, D), lambda i: (0, 0))
    return pl.pallas_call(
        body, name=name, grid=(S // T,), in_specs=[row, vec, row],
        out_specs=(pl.BlockSpec((8, 128), lambda i: (0, 0)), row, row, vec),
        out_shape=(jax.ShapeDtypeStruct((8, 128), F32), jax.ShapeDtypeStruct((S, D), F32), jax.ShapeDtypeStruct((S, D), _MXU),
                   jax.ShapeDtypeStruct((1, D), F32)),
        compiler_params=_cparams(("arbitrary",)),
    )(x3, w, tgt)


def _shifted(ext, back, lo, n):
    if back == 0:
        return ext[lo:lo + n, :]
    return pltpu.roll(ext, back % ext.shape[0], 0)[lo:lo + n, :]


def _conv_windows(ext, K, T):
    return [_shifted(ext, (K - 1) - i, 8, T) for i in range(K)]


def _conv_taps(ext, w, K, T):
    out = None
    for i, win in enumerate(_conv_windows(ext, K, T)):
        term = win * w[i:i + 1, :]
        out = term if out is None else out + term
    return out


def _conv_taps_t(ext, w, K, T):
    out = None
    for i in range(K):
        term = _shifted(ext, i - (K - 1), 0, T) * w[i:i + 1, :]
        out = term if out is None else out + term
    return out


def _tri_masks(C):
    r = lax.broadcasted_iota(jnp.int32, (C, C), 0)
    c = lax.broadcasted_iota(jnp.int32, (C, C), 1)
    return r == c, r >= c, r > c, r <= c


_NN, _NT, _TN = ((1,), (0,)), ((1,), (1,)), ((0,), (0,))
_GDN_PASSES = dict(qk=1, inv=1, sol=1, scan=1, bwd=1)


def _bdot_raw(a, b, kind, passes):
    dims = ({"NN": ((2,), (1,)), "NT": ((2,), (2,)), "TN": ((1,), (1,))}[kind], ((0,), (0,)))
    if passes == 0:
        return lax.dot_general(a, b, dims, precision=_HI, preferred_element_type=F32)
    ah, bh = a.astype(BF16), b.astype(BF16)
    out = lax.dot_general(ah, bh, dims, preferred_element_type=F32)
    if passes == 3:
        al, bl = (a - ah.astype(F32)).astype(BF16), (b - bh.astype(F32)).astype(BF16)
        out = out + lax.dot_general(ah, bl, dims, preferred_element_type=F32) + lax.dot_general(al, bh, dims, preferred_element_type=F32)
    return out


@functools.partial(jax.custom_vjp, nondiff_argnums=(2, 3))
def _bdot(a, b, kind, passes):
    return _bdot_raw(a, b, kind, passes)


def _bdot_fwd(a, b, kind, passes):
    return _bdot_raw(a, b, kind, passes), (a, b)


def _bdot_bwd(kind, passes, res, ct):
    a, b = res
    if kind == "NN":
        return _bdot_raw(ct, b, "NT", passes), _bdot_raw(a, ct, "TN", passes)
    if kind == "NT":
        return _bdot_raw(ct, b, "NN", passes), _bdot_raw(ct, a, "TN", passes)
    return _bdot_raw(b, ct, "NT", passes), _bdot_raw(a, ct, "NN", passes)


_bdot.defvjp(_bdot_fwd, _bdot_bwd)


def _softplus(x):
    return jnp.maximum(x, 0.0) + jnp.log(1.0 + jnp.exp(-jnp.abs(x)))


def _gdn_stage1(cq, ck, cv, b_col, a_col, alog, dtb, dot=_bdot_raw):
    C = cq.shape[1]
    eye, incl, strict, incl_t = _tri_masks(C)
    qn = cq * lax.rsqrt(jnp.sum(cq * cq, axis=-1, keepdims=True) + EPS) * (GDN_DIM ** -0.5)
    kn = ck * lax.rsqrt(jnp.sum(ck * ck, axis=-1, keepdims=True) + EPS)
    beta = jax.nn.sigmoid(b_col)
    g = -jnp.exp(alog) * _softplus(a_col + dtb)
    g_row = jnp.sum(jnp.where(eye, g, 0.0), axis=1, keepdims=True)
    beta_row = jnp.sum(jnp.where(eye, beta, 0.0), axis=1, keepdims=True)
    gc_col = jnp.sum(jnp.where(incl, g_row, 0.0), axis=2, keepdims=True)
    gc_row = jnp.sum(jnp.where(incl_t, g, 0.0), axis=1, keepdims=True)
    dec = jnp.where(incl, jnp.exp(jnp.where(incl, gc_col - gc_row, 0.0)), 0.0)
    kk = dot(kn, kn, "NT", _GDN_PASSES["qk"])
    qk = dot(qn, kn, "NT", _GDN_PASSES["qk"])
    lmat = jnp.where(strict, dec * kk * beta_row, 0.0)
    attn = dec * qk * beta_row
    gam = jnp.exp(gc_col)
    gc_last = gc_col[:, C - 1:C, :]
    k_end = kn * (jnp.exp(gc_last - gc_col) * beta)
    return lmat, cv, gam * kn, gam * qn, attn, k_end, jnp.exp(gc_last)


def _tri_inv(lmat):
    C = lmat.shape[1]
    eye = _tri_masks(C)[0]
    ps = _GDN_PASSES["inv"]
    p = jnp.where(eye, 1.0, 0.0) - lmat
    lp = _bdot_raw(lmat, lmat, "NN", ps)
    n = int(math.log2(C))
    for s in range(1, n):
        p = p + _bdot_raw(p, lp, "NN", ps)
        if s < n - 1:
            lp = _bdot_raw(lp, lp, "NN", ps)
    return p


def _gated_norm(o, z, gnw):
    on = o * lax.rsqrt(jnp.mean(o * o, axis=-1, keepdims=True) + EPS) * gnw
    return on * _silu(z)


GDN_PG = 4
GDN_SG = 4


def _gdn_pairs(c, ba, gp, G):
    C, W, H = GDN_CHUNK, GDN_WIDTH, GDN_HEADS
    pairs = [(j, h) for j in range(G) for h in range(H)]
    cq, ck, cv = (jnp.stack([c[C * j:C * (j + 1), o + GDN_DIM * h:o + GDN_DIM * (h + 1)] for j, h in pairs]) for o in (0, W, 2 * W))
    b_col = jnp.stack([ba[C * j:C * (j + 1), h:h + 1] for j, h in pairs])
    a_col = jnp.stack([ba[C * j:C * (j + 1), H + h:H + h + 1] for j, h in pairs])
    alog = jnp.stack([gp[0:1, h:h + 1] for j, h in pairs])
    dtb = jnp.stack([gp[0:1, H + h:H + h + 1] for j, h in pairs])
    return pairs, (cq, ck, cv, b_col, a_col, alog, dtb)


def _gdn_pre_specs(S, G):
    C = GDN_CHUNK
    T = C * G
    return dict(
        cur=pl.BlockSpec((T, 3 * GDN_WIDTH), lambda i: (i, 0)),
        prev=pl.BlockSpec((8, 3 * GDN_WIDTH), lambda i: (jnp.maximum(i * (T // 8) - 1, 0), 0)),
        ba=pl.BlockSpec((T, 128), lambda i: (i, P_BA // 128)),
        cw=pl.BlockSpec((GDN_CONV, 3 * GDN_WIDTH), lambda i: (0, 0)),
        vec=pl.BlockSpec((1, 128), lambda i: (0, 0)),
        hd=pl.BlockSpec((GDN_HEADS, T, GDN_DIM), lambda i: (0, i, 0)),
        hc=pl.BlockSpec((GDN_HEADS, T, C), lambda i: (0, i, 0)),
        ge=pl.BlockSpec((G, GDN_HEADS, 8, 128), lambda i: (i, 0, 0, 0)),
    )


def _hd_shape(S, last=GDN_DIM):
    return jax.ShapeDtypeStruct((GDN_HEADS, S, last), F32)


def _gdn_pre(proj, conv_w, gp):
    S = proj.shape[0]
    C, G = GDN_CHUNK, GDN_PG
    nc = S // C
    sp = _gdn_pre_specs(S, G)

    def body(cur_ref, prev_ref, ba_ref, cw_ref, gp_ref, uv_ref, wk_ref, qd_ref, ke_ref, at_ref, ti_ref, ge_ref):
        prev = prev_ref[...] * jnp.where(pl.program_id(0) == 0, 0.0, 1.0)
        c = _silu(_conv_taps(jnp.concatenate([prev, cur_ref[...]], axis=0), cw_ref[...], GDN_CONV, C * G))
        pairs, args = _gdn_pairs(c, ba_ref[...], gp_ref[...], G)
        lmat, v, rk, q_dec, attn, k_end, g_end = _gdn_stage1(*args)
        t = _tri_inv(lmat)
        u_v = _bdot_raw(t, v, "NN", _GDN_PASSES["sol"])
        w_k = _bdot_raw(t, rk, "NN", _GDN_PASSES["sol"])
        for b, (j, h) in enumerate(pairs):
            rows = slice(C * j, C * (j + 1))
            uv_ref[h, rows, :] = u_v[b]
            wk_ref[h, rows, :] = w_k[b]
            qd_ref[h, rows, :] = q_dec[b]
            ke_ref[h, rows, :] = k_end[b]
            at_ref[h, rows, :] = attn[b]
            ti_ref[h, rows, :] = t[b]
            ge_ref[j, h] = jnp.broadcast_to(g_end[b], (8, 128))

    return pl.pallas_call(
        body, name="gdn_pre", grid=(nc // G,),
        in_specs=[sp["cur"], sp["prev"], sp["ba"], sp["cw"], sp["vec"]],
        out_specs=(sp["hd"], sp["hd"], sp["hd"], sp["hd"], sp["hc"], sp["hc"], sp["ge"]),
        out_shape=(_hd_shape(S), _hd_shape(S), _hd_shape(S), _hd_shape(S), _hd_shape(S, C), _hd_shape(S, C),
                   jax.ShapeDtypeStruct((nc, GDN_HEADS, 8, 128), F32)),
        compiler_params=_cparams(("parallel",)),
    )(proj, proj, proj, conv_w, gp)


def _gdn_scan_specs(S, G, rev):
    C = GDN_CHUNK
    T = C * G
    n = S // T
    ci = (lambda i: n - 1 - i) if rev else (lambda i: i)
    return dict(
        hd=pl.BlockSpec((GDN_HEADS, T, GDN_DIM), lambda i: (0, ci(i), 0)),
        hc=pl.BlockSpec((GDN_HEADS, T, C), lambda i: (0, ci(i), 0)),
        ge=pl.BlockSpec((G, GDN_HEADS, 8, 128), lambda i: (ci(i), 0, 0, 0)),
        z=pl.BlockSpec((T, GDN_WIDTH), lambda i: (ci(i), P_Z // GDN_WIDTH)),
        oa=pl.BlockSpec((T, GDN_WIDTH), lambda i: (ci(i), 0)),
        vec=pl.BlockSpec((1, 128), lambda i: (0, 0)),
        st=pl.BlockSpec((G, GDN_HEADS, GDN_DIM, GDN_DIM), lambda i: (ci(i), 0, 0, 0)),
    )


def _gdn_scan(u_v, w_k, q_dec, k_end, attn, g_end, proj, gnw, mix, after=()):
    S = proj.shape[0]
    C, G = GDN_CHUNK, GDN_SG
    nc = S // C
    sp = _gdn_scan_specs(S, G, False)
    ps = _GDN_PASSES["scan"]

    def body(uv_ref, wk_ref, qd_ref, ke_ref, at_ref, ge_ref, z_ref, gnw_ref, *rest):
        oa_ref, st_ref, s_scr = rest[-3:]

        @pl.when(pl.program_id(0) == 0)
        def _():
            s_scr[...] = jnp.zeros_like(s_scr)

        for j in range(G):
            rows = slice(C * j, C * (j + 1))
            st = s_scr[...]
            st_ref[j] = st
            u = uv_ref[:, rows, :] - _bdot_raw(wk_ref[:, rows, :], st, "NN", ps)
            o = _bdot_raw(qd_ref[:, rows, :], st, "NN", ps) + _bdot_raw(at_ref[:, rows, :], u, "NN", ps)
            s_scr[...] = ge_ref[j][:, 0:1, 0:1] * st + _bdot_raw(ke_ref[:, rows, :], u, "TN", ps)
            for h in range(GDN_HEADS):
                cols = slice(GDN_DIM * h, GDN_DIM * (h + 1))
                oa_ref[rows, cols] = _gated_norm(o[h], z_ref[rows, cols], gnw_ref[...])

    return pl.pallas_call(
        body, name="gdn_scan", grid=(nc // G,),
        in_specs=[sp["hd"], sp["hd"], sp["hd"], sp["hd"], sp["hc"], sp["ge"], sp["z"], sp["vec"]] + [_ANY] * (1 + len(after)),
        out_specs=(sp["oa"], sp["st"]),
        out_shape=(jax.ShapeDtypeStruct(mix.shape, F32),
                   jax.ShapeDtypeStruct((nc, GDN_HEADS, GDN_DIM, GDN_DIM), F32)),
        input_output_aliases={8: 0},
        scratch_shapes=[pltpu.VMEM((GDN_HEADS, GDN_DIM, GDN_DIM), F32)],
        compiler_params=_cparams(("arbitrary",)),
    )(u_v, w_k, q_dec, k_end, attn, g_end, proj, gnw, mix, *after)


def _gdn_scan_bwd(u_v, w_k, q_dec, k_end, attn, g_end, proj, gnw, states, d_oa):
    S = proj.shape[0]
    C, G = GDN_CHUNK, GDN_SG
    nc = S // C
    sp = _gdn_scan_specs(S, G, True)
    ps, pb = _GDN_PASSES["scan"], _GDN_PASSES["bwd"]

    def body(uv_ref, wk_ref, qd_ref, ke_ref, at_ref, ge_ref, z_ref, gnw_ref, st_ref, doa_ref,
             duv_ref, dwk_ref, dqd_ref, dke_ref, dat_ref, dge_ref, dz_ref, dgnw_ref, ds_scr):
        @pl.when(pl.program_id(0) == 0)
        def _():
            ds_scr[...] = jnp.zeros_like(ds_scr)
            dgnw_ref[...] = jnp.zeros_like(dgnw_ref)

        dgnw = jnp.zeros((1, 128), F32)
        for j in reversed(range(G)):
            rows = slice(C * j, C * (j + 1))
            st = st_ref[j]
            wk, qd, ke, at = wk_ref[:, rows, :], qd_ref[:, rows, :], ke_ref[:, rows, :], at_ref[:, rows, :]
            u = uv_ref[:, rows, :] - _bdot_raw(wk, st, "NN", ps)
            o = _bdot_raw(qd, st, "NN", ps) + _bdot_raw(at, u, "NN", ps)
            dos = []
            for h in range(GDN_HEADS):
                cols = slice(GDN_DIM * h, GDN_DIM * (h + 1))
                _, vjp2 = jax.vjp(_gated_norm, o[h], z_ref[rows, cols], gnw_ref[...])
                do_h, dz_h, dgn = vjp2(doa_ref[rows, cols])
                dz_ref[rows, cols] = dz_h
                dgnw = dgnw + dgn
                dos.append(do_h)
            do = jnp.stack(dos)
            ds_new = ds_scr[...]
            du = _bdot_raw(at, do, "TN", pb) + _bdot_raw(ke, ds_new, "NN", pb)
            duv_ref[:, rows, :] = du
            dat_ref[:, rows, :] = _bdot_raw(do, u, "NT", pb)
            dqd_ref[:, rows, :] = _bdot_raw(do, st, "NT", pb)
            dke_ref[:, rows, :] = _bdot_raw(u, ds_new, "NT", pb)
            dwk_ref[:, rows, :] = -_bdot_raw(du, st, "NT", pb)
            d_ge = jnp.sum(jnp.sum(st * ds_new, axis=2, keepdims=True), axis=1, keepdims=True)
            dge_ref[j] = jnp.broadcast_to(d_ge, (GDN_HEADS, 8, 128))
            ds_scr[...] = ge_ref[j][:, 0:1, 0:1] * ds_new + _bdot_raw(qd, do, "TN", pb) - _bdot_raw(wk, du, "TN", pb)
        dgnw_ref[...] += dgnw

    return pl.pallas_call(
        body, name="gdn_scan_bwd", grid=(nc // G,),
        in_specs=[sp["hd"], sp["hd"], sp["hd"], sp["hd"], sp["hc"], sp["ge"], sp["z"], sp["vec"], sp["st"], sp["oa"]],
        out_specs=(sp["hd"], sp["hd"], sp["hd"], sp["hd"], sp["hc"], sp["ge"], sp["oa"], sp["vec"]),
        out_shape=(_hd_shape(S), _hd_shape(S), _hd_shape(S), _hd_shape(S), _hd_shape(S, C),
                   jax.ShapeDtypeStruct((nc, GDN_HEADS, 8, 128), F32), jax.ShapeDtypeStruct((S, GDN_WIDTH), F32),
                   jax.ShapeDtypeStruct((1, 128), F32)),
        scratch_shapes=[pltpu.VMEM((GDN_HEADS, GDN_DIM, GDN_DIM), F32)],
        compiler_params=_cparams(("arbitrary",)),
    )(u_v, w_k, q_dec, k_end, attn, g_end, proj, gnw, states, d_oa)


def _gdn_post(proj, conv_w, gp, tinv, u_v, w_k, d_uv, d_wk, d_qd, d_ke, d_at, d_ge):
    S = proj.shape[0]
    C, G = GDN_CHUNK, GDN_PG
    nc = S // C
    sp = _gdn_pre_specs(S, G)
    pb = _GDN_PASSES["bwd"]

    def body(cur_ref, prev_ref, ba_ref, cw_ref, gp_ref, ti_ref, uv_ref, wk_ref, duv_ref, dwk_ref, dqd_ref, dke_ref,
             dat_ref, dge_ref, dpre_ref, dba_ref, dgp_ref):
        i = pl.program_id(0)

        @pl.when(i == 0)
        def _():
            dgp_ref[...] = jnp.zeros_like(dgp_ref)

        prev = prev_ref[...] * jnp.where(i == 0, 0.0, 1.0)
        pre = _conv_taps(jnp.concatenate([prev, cur_ref[...]], axis=0), cw_ref[...], GDN_CONV, C * G)
        sg = jax.nn.sigmoid(pre)
        dsilu = sg * (1.0 + pre * (1.0 - sg))
        pairs, args = _gdn_pairs(pre * sg, ba_ref[...], gp_ref[...], G)
        _, vjp1 = jax.vjp(functools.partial(_gdn_stage1, dot=_bdot), *args)

        def take(ref):
            return jnp.stack([ref[h, C * j:C * (j + 1), :] for j, h in pairs])

        t, u_v, w_k = take(ti_ref), take(uv_ref), take(wk_ref)
        d_v = _bdot_raw(t, take(duv_ref), "TN", pb)
        d_rk = _bdot_raw(t, take(dwk_ref), "TN", pb)
        d_l = -(_bdot_raw(d_v, u_v, "NT", pb) + _bdot_raw(d_rk, w_k, "NT", pb))
        d_ge = jnp.stack([dge_ref[j, h][0:1, 0:1] for j, h in pairs])
        dcq, dck, dcv, db, da, dalog, ddtb = vjp1((d_l, d_v, d_rk, take(dqd_ref), take(dat_ref), take(dke_ref), d_ge))
        lane = lax.broadcasted_iota(jnp.int32, (C, 128), 1)
        lane1 = lax.broadcasted_iota(jnp.int32, (1, 128), 1)
        dgp = jnp.zeros((1, 128), F32)
        for j in range(G):
            rows = slice(C * j, C * (j + 1))
            dba = jnp.zeros((C, 128), F32)
            for h in range(GDN_HEADS):
                b = GDN_HEADS * j + h
                for o_, dcx in ((0, dcq), (GDN_WIDTH, dck), (2 * GDN_WIDTH, dcv)):
                    cols = slice(o_ + GDN_DIM * h, o_ + GDN_DIM * (h + 1))
                    dpre_ref[rows, cols] = dcx[b] * dsilu[rows, cols]
                dba = dba + jnp.where(lane == h, db[b], 0.0) + jnp.where(lane == GDN_HEADS + h, da[b], 0.0)
                dgp = dgp + jnp.where(lane1 == h, dalog[b], 0.0) + jnp.where(lane1 == GDN_HEADS + h, ddtb[b], 0.0)
            dba_ref[rows, :] = dba
        dgp_ref[0:1, :] += dgp

    T = C * G
    return pl.pallas_call(
        body, name="gdn_post", grid=(nc // G,),
        in_specs=[sp["cur"], sp["prev"], sp["ba"], sp["cw"], sp["vec"], sp["hc"], sp["hd"], sp["hd"], sp["hd"], sp["hd"],
                  sp["hd"], sp["hd"], sp["hc"], sp["ge"]],
        out_specs=(sp["cur"], pl.BlockSpec((T, 128), lambda i: (i, 0)), pl.BlockSpec((8, 128), lambda i: (0, 0))),
        out_shape=(jax.ShapeDtypeStruct((S, 3 * GDN_WIDTH), F32), jax.ShapeDtypeStruct((S, 128), F32),
                   jax.ShapeDtypeStruct((8, 128), F32)),
        compiler_params=_cparams(("arbitrary",)),
    )(proj, proj, proj, conv_w, gp, tinv, u_v, w_k, d_uv, d_wk, d_qd, d_ke, d_at, d_ge)


def _conv_bwd(dpre, x, xcol0, w, K, name, tc):
    S, Cc = dpre.shape
    T = _pick_tile(S, 256)
    nt, ncol = S // T, Cc // tc
    xo = xcol0 // tc

    def body(d_ref, dn_ref, x_ref, xp_ref, w_ref, dx_ref, dw_ref):
        i = pl.program_id(1)
        dn = dn_ref[...] * jnp.where(i == nt - 1, 0.0, 1.0)
        dv = d_ref[...]
        ext_d = jnp.concatenate([dv, dn], axis=0)
        dx_ref[...] = _conv_taps_t(ext_d, w_ref[...], K, T).astype(dx_ref.dtype)
        xp = xp_ref[...] * jnp.where(i == 0, 0.0, 1.0)
        ext_x = jnp.concatenate([xp, x_ref[...]], axis=0)

        @pl.when(i == 0)
        def _():
            dw_ref[...] = jnp.zeros_like(dw_ref)

        for k in range(K):
            dw_ref[k:k + 1, :] += jnp.sum(dv * _shifted(ext_x, (K - 1) - k, 8, T), axis=0, keepdims=True)

    r8 = T // 8
    return pl.pallas_call(
        body, name=name, grid=(ncol, nt),
        in_specs=[pl.BlockSpec((T, tc), lambda j, i: (i, j)),
                  pl.BlockSpec((8, tc), lambda j, i: (jnp.minimum((i + 1) * r8, S // 8 - 1), j)),
                  pl.BlockSpec((T, tc), lambda j, i: (i, j + xo)),
                  pl.BlockSpec((8, tc), lambda j, i: (jnp.maximum(i * r8 - 1, 0), j + xo)),
                  pl.BlockSpec((K, tc), lambda j, i: (0, j))],
        out_specs=(pl.BlockSpec((T, tc), lambda j, i: (i, j)), pl.BlockSpec((K, tc), lambda j, i: (0, j))),
        out_shape=(jax.ShapeDtypeStruct((S, Cc), _MXU), jax.ShapeDtypeStruct((K, Cc), F32)),
        compiler_params=_cparams(("parallel", "arbitrary")),
    )(dpre, dpre, x, x, w)


def _dil_bias(nt, T):
    d = (np.arange(nt)[:, None, None] * T + np.arange(T)[None, None, :] - np.arange(T)[None, :, None])
    cnt = ((d >= 0) & (d <= 128)).astype(np.float64) + ((d >= 0) & (d % 4 == 0) & (d <= 512)) + ((d >= 0) & (d % 16 == 0))
    return jnp.asarray(np.where(cnt > 0, np.log(np.maximum(cnt, 1.0)), -1e30), dtype=F32)


def _attn_fwd(proj, after=()):
    S = proj.shape[0]
    T = min(ATT_T, S)
    nt = S // T
    bias = _dil_bias(nt, T)
    scale = DIL_DIM ** -0.5
    npair = DIL_WIDTH // 128
    qb0, kb0, vb0 = P_QKVB // 128, (P_QKVB + DIL_WIDTH) // 128, (P_QKVB + 2 * DIL_WIDTH) // 128

    def body(q_ref, k_ref, v_ref, b_ref, *rest):
        o_ref, lse_ref = rest[-2:]
        i = pl.program_id(1)
        qs = (q_ref[...] * scale).astype(_MXU)

        def step(j, carry):
            kt = k_ref[pl.ds(pl.multiple_of(j * T, T), T), :].astype(_MXU)
            vt = v_ref[pl.ds(pl.multiple_of(j * T, T), T), :].astype(_MXU)
            bt = b_ref[i - j]
            out = []
            for hh in range(2):
                m, l, acc = carry[hh]
                sl = slice(hh * DIL_DIM, (hh + 1) * DIL_DIM)
                s = lax.dot_general(kt[:, sl], qs[:, sl], (_NT, ((), ())), preferred_element_type=F32) + bt
                m_new = jnp.maximum(m, jnp.max(s, axis=0, keepdims=True))
                p = jnp.exp(s - m_new)
                a = jnp.exp(m - m_new)
                l = a * l + jnp.sum(p, axis=0, keepdims=True)
                acc = a * acc + lax.dot_general(vt[:, sl], p.astype(_MXU), (_TN, ((), ())), preferred_element_type=F32)
                out.append((m_new, l, acc))
            return tuple(out)

        init = tuple((jnp.full((1, T), -1e30, F32), jnp.zeros((1, T), F32), jnp.zeros((DIL_DIM, T), F32)) for _ in range(2))
        res = lax.fori_loop(0, i + 1, step, init)
        lse_ref[...] = jnp.zeros_like(lse_ref)
        for hh in range(2):
            m, l, acc = res[hh]
            o_ref[:, hh * DIL_DIM:(hh + 1) * DIL_DIM] = (acc / l).T
            lse_ref[hh:hh + 1, :] = m + jnp.log(l)

    return pl.pallas_call(
        body, name="attn_fwd", grid=(npair, nt),
        in_specs=[pl.BlockSpec((T, 128), lambda p, i: (i, qb0 + p)),
                  pl.BlockSpec((S, 128), lambda p, i: (0, kb0 + p)),
                  pl.BlockSpec((S, 128), lambda p, i: (0, vb0 + p)),
                  pl.BlockSpec((nt, T, T), lambda p, i: (0, 0, 0))] + [_ANY] * len(after),
        out_specs=(pl.BlockSpec((T, 128), lambda p, i: (i, GDN_WIDTH // 128 + p)),
                   pl.BlockSpec((None, None, 8, T), lambda p, i: (p, i, 0, 0))),
        out_shape=(jax.ShapeDtypeStruct((S, GDN_WIDTH + DIL_WIDTH), F32), jax.ShapeDtypeStruct((npair, nt, 8, T), F32)),
        compiler_params=_cparams(("parallel", "parallel")),
    )(proj, proj, proj, bias, *after)


def _attn_bwd(proj, mix, lse, d_mix):
    S = proj.shape[0]
    T = min(ATT_T, S)
    nt = S // T
    bias = _dil_bias(nt, T)
    scale = DIL_DIM ** -0.5
    npair = DIL_WIDTH // 128
    qb0, kb0, vb0 = P_QKVB // 128, (P_QKVB + DIL_WIDTH) // 128, (P_QKVB + 2 * DIL_WIDTH) // 128

    def body(q_ref, k_ref, v_ref, o_ref, lse_ref, do_ref, b_ref, dq_ref, dk_ref, dv_ref, dq_scr):
        j = pl.program_id(1)

        @pl.when(j == 0)
        def _():
            dq_scr[...] = jnp.zeros_like(dq_scr)

        kt = k_ref[...].astype(_MXU)
        vt = v_ref[...].astype(_MXU)
        ones = jnp.ones((8, DIL_DIM), F32)

        def step(i, carry):
            rows = pl.ds(pl.multiple_of(i * T, T), T)
            qs = (q_ref[rows, :] * scale).astype(_MXU)
            dov = do_ref[rows, :]
            prod = dov * o_ref[rows, :]
            lsev = lse_ref[i]
            dob = dov.astype(_MXU)
            bt = b_ref[i - j]
            out = []
            dqs = []
            for hh in range(2):
                dk, dv = carry[hh]
                sl = slice(hh * DIL_DIM, (hh + 1) * DIL_DIM)
                s = lax.dot_general(kt[:, sl], qs[:, sl], (_NT, ((), ())), preferred_element_type=F32) + bt
                p = jnp.exp(s - lsev[hh:hh + 1, :])
                delta = lax.dot_general(ones, prod[:, sl], (_NT, ((), ())), precision=_HI, preferred_element_type=F32)[0:1, :]
                dp = lax.dot_general(vt[:, sl], dob[:, sl], (_NT, ((), ())), preferred_element_type=F32)
                ds = (p * (dp - delta)).astype(_MXU)
                dv = dv + lax.dot_general(p.astype(_MXU), dob[:, sl], (_NN, ((), ())), preferred_element_type=F32)
                dk = dk + lax.dot_general(ds, qs[:, sl], (_NN, ((), ())), preferred_element_type=F32)
                dqs.append(lax.dot_general(ds, kt[:, sl], (_TN, ((), ())), preferred_element_type=F32) * scale)
                out.append((dk, dv))
            dq_scr[rows, :] += jnp.concatenate(dqs, axis=1)
            return tuple(out)

        init = tuple((jnp.zeros((T, DIL_DIM), F32), jnp.zeros((T, DIL_DIM), F32)) for _ in range(2))
        res = lax.fori_loop(j, nt, step, init)
        dk_ref[...] = jnp.concatenate([res[0][0], res[1][0]], axis=1).astype(dk_ref.dtype)
        dv_ref[...] = jnp.concatenate([res[0][1], res[1][1]], axis=1).astype(dv_ref.dtype)

        @pl.when(j == nt - 1)
        def _():
            dq_ref[...] = dq_scr[...].astype(dq_ref.dtype)

    full = lambda c0: pl.BlockSpec((S, 128), lambda p, j: (0, c0 + p))
    tile = lambda c0: pl.BlockSpec((T, 128), lambda p, j: (j, c0 + p))
    out3 = jax.ShapeDtypeStruct((S, DIL_WIDTH), _MXU)
    return pl.pallas_call(
        body, name="attn_bwd", grid=(npair, nt),
        in_specs=[full(qb0), tile(kb0), tile(vb0), full(GDN_WIDTH // 128),
                  pl.BlockSpec((None, nt, 8, T), lambda p, j: (p, 0, 0, 0)), full(GDN_WIDTH // 128),
                  pl.BlockSpec((nt, T, T), lambda p, j: (0, 0, 0))],
        out_specs=(full(0), tile(0), tile(0)),
        out_shape=(out3, out3, out3),
        scratch_shapes=[pltpu.VMEM((S, 128), F32)],
        compiler_params=_cparams(("parallel", "arbitrary")),
    )(proj, proj, proj, mix, lse, d_mix, bias)


def _ffn_act(up, cw):
    S, Cc = up.shape[0], up.shape[1] // 2
    T, tc = _pick_tile(S, 256), _pick_tile(Cc, 1536)
    r16 = T // 16
    nct = Cc // tc

    def body(g_ref, gp_ref, u_ref, up_ref, wg_ref, wu_ref, o_ref):
        keep = jnp.where(pl.program_id(1) == 0, 0.0, 1.0)
        cg = _conv_taps(jnp.concatenate([gp_ref[8:16, :].astype(F32) * keep, g_ref[...].astype(F32)], axis=0),
                        wg_ref[...], FFN_CONV, T)
        cu = _conv_taps(jnp.concatenate([up_ref[8:16, :].astype(F32) * keep, u_ref[...].astype(F32)], axis=0),
                        wu_ref[...], FFN_CONV, T)
        o_ref[...] = (_silu(cg) * cu).astype(o_ref.dtype)

    cur = lambda o: pl.BlockSpec((T, tc), lambda j, i: (i, j + o))
    prev = lambda o: pl.BlockSpec((16, tc), lambda j, i: (jnp.maximum(i * r16 - 1, 0), j + o))
    wsp = lambda o: pl.BlockSpec((FFN_CONV, tc), lambda j, i: (0, j + o))
    return pl.pallas_call(
        body, name="ffn_act", grid=(nct, S // T),
        in_specs=[cur(0), prev(0), cur(nct), prev(nct), wsp(0), wsp(nct)], out_specs=cur(0),
        out_shape=jax.ShapeDtypeStruct((S, Cc), _MXU),
        compiler_params=_cparams(("parallel", "parallel")),
    )(up, up, up, up, cw, cw)


def _ffn_act_bwd(d_act, up, cw):
    S, Cc = up.shape[0], up.shape[1] // 2
    T, tc = _pick_tile(S, 256), _pick_tile(Cc, 1536)
    r8, r16 = T // 8, T // 16
    nt = S // T
    nct = Cc // tc
    K = FFN_CONV

    def body(da_ref, dan_ref, g_ref, gp_ref, gn_ref, u_ref, up_ref, un_ref, wg_ref, wu_ref,
             dg_ref, du_ref, dwg_ref, dwu_ref):
        i = pl.program_id(1)
        keep_p = jnp.where(i == 0, 0.0, 1.0)
        keep_n = jnp.where(i == nt - 1, 0.0, 1.0)
        wg, wu = wg_ref[...], wu_ref[...]
        xg = jnp.concatenate([gp_ref[8:16, :].astype(F32) * keep_p, g_ref[...].astype(F32),
                              gn_ref[0:8, :].astype(F32) * keep_n], axis=0)
        xu = jnp.concatenate([up_ref[8:16, :].astype(F32) * keep_p, u_ref[...].astype(F32),
                              un_ref[0:8, :].astype(F32) * keep_n], axis=0)
        cg = _conv_taps(xg, wg, K, T + 8)
        cu = _conv_taps(xu, wu, K, T + 8)
        da = jnp.concatenate([da_ref[...], dan_ref[...] * keep_n], axis=0)
        sg = jax.nn.sigmoid(cg)
        d_cg = da * cu * (sg * (1.0 + cg * (1.0 - sg)))
        d_cu = da * (cg * sg)
        dg_ref[...] = _conv_taps_t(d_cg, wg, K, T).astype(dg_ref.dtype)
        du_ref[...] = _conv_taps_t(d_cu, wu, K, T).astype(du_ref.dtype)

        @pl.when(i == 0)
        def _():
            dwg_ref[...] = jnp.zeros_like(dwg_ref)
            dwu_ref[...] = jnp.zeros_like(dwu_ref)

        for k in range(K):
            dwg_ref[k:k + 1, :] += jnp.sum(d_cg[0:T, :] * _shifted(xg, (K - 1) - k, 8, T), axis=0, keepdims=True)
            dwu_ref[k:k + 1, :] += jnp.sum(d_cu[0:T, :] * _shifted(xu, (K - 1) - k, 8, T), axis=0, keepdims=True)

    cur = lambda o: pl.BlockSpec((T, tc), lambda j, i: (i, j + o))
    prev = lambda o: pl.BlockSpec((16, tc), lambda j, i: (jnp.maximum(i * r16 - 1, 0), j + o))
    nxt = lambda o: pl.BlockSpec((16, tc), lambda j, i: (jnp.minimum((i + 1) * r16, S // 16 - 1), j + o))
    nxt8 = pl.BlockSpec((8, tc), lambda j, i: (jnp.minimum((i + 1) * r8, S // 8 - 1), j))
    wsp = lambda o: pl.BlockSpec((K, tc), lambda j, i: (0, j + o))
    return pl.pallas_call(
        body, name="ffn_act_bwd", grid=(nct, nt),
        in_specs=[cur(0), nxt8, cur(0), prev(0), nxt(0), cur(nct), prev(nct), nxt(nct), wsp(0), wsp(nct)],
        out_specs=(cur(0), cur(0), wsp(0), wsp(0)),
        out_shape=(jax.ShapeDtypeStruct((S, Cc), _MXU), jax.ShapeDtypeStruct((S, Cc), _MXU),
                   jax.ShapeDtypeStruct((K, Cc), F32), jax.ShapeDtypeStruct((K, Cc), F32)),
        compiler_params=_cparams(("parallel", "arbitrary")),
    )(d_act, d_act, up, up, up, up, up, up, cw, cw)


def _local_step(x, tgt, h1, n1w, n2w, fnw, gp, gnw, wp, conv_w, fcw, rest_weights, early_grads):
    proj = _mm(h1, wp, "nn", name="proj")
    u_v, w_k, q_dec, k_end, attn, tinv, g_end = _gdn_pre(proj, conv_w, gp)
    mix, lse = _attn_fwd(proj)
    mix, states = _gdn_scan(u_v, w_k, q_dec, k_end, attn, g_end, proj, gnw, mix, after=[rest_weights[0]([mix])])
    w_out, w_up4, w_down = rest_weights[1]([mix])
    x2 = _mm(mix, w_out, "nn", residual=x, name="outproj")
    h2 = _rmsnorm_fwd(x2, n2w, "norm2")
    up = _mm(h2, w_up4, "nn", b_blocks=True, out_dtype=_MXU, name="up")
    act = _ffn_act(up, fcw)
    x3 = _mm(act, w_down, "nn", residual=x2, name="down")
    loss, dx3, dx3n, d_fnw = _loss_head(x3, fnw, tgt, "loss_head")
    d_act = _mm(dx3n, w_down, "nt", name="d_act")
    d_wdown = _mm(act, dx3n, "tn", name="d_wdown")
    d_upg, d_upu, d_fcwg, d_fcwu = _ffn_act_bwd(d_act, up, fcw)
    d_wup = _mm(h2, d_upg, "tn", place=("blocks", N_CHIPS, 0), tn=w_up4.shape[2], name="d_wgate")
    d_wup = _mm(h2, d_upu, "tn", place=("blocks", N_CHIPS, N_CHIPS // 2), tn=w_up4.shape[2], into=d_wup, name="d_wup")
    token = early_grads[0](d_wup, d_wdown)
    d_h2 = _mm_nt_blocks([d_upg, d_upu], w_up4, "d_h2", after=[token])
    dx2, d_n2w = _rmsnorm_bwd(d_h2, x2, n2w, dx3, "norm2_bwd", after=[token])
    token = early_grads[1](dx2)
    d_mix = _mm(dx2, w_out, "nt", name="d_mix")
    d_wout = _mm(mix, dx2, "tn", name="d_wout")
    dq_b, dk_b, dv_b = _attn_bwd(proj, mix, lse, d_mix)
    d_uv, d_wk, d_qd, d_ke, d_at, d_ge, d_z, d_gnw = _gdn_scan_bwd(u_v, w_k, q_dec, k_end, attn, g_end, proj,
                                                                   gnw + token[0:1, 0:1], states, d_mix)
    d_pre, d_ba, d_gp = _gdn_post(proj, conv_w, gp, tinv, u_v, w_k, d_uv, d_wk, d_qd, d_ke, d_at, d_ge)
    d_qkva, d_convw = _conv_bwd(d_pre, proj, 0, conv_w, GDN_CONV, "gdn_conv_bwd", 512)
    d_proj = jnp.concatenate([d_qkva, d_z.astype(_MXU), dq_b, dk_b, dv_b, d_ba.astype(_MXU),
                              jnp.zeros((x.shape[0], P_COLS - P_BA - 128), _MXU)], axis=1)
    d_wp = _mm(h1, d_proj, "tn", name="d_wp")
    token = early_grads[2](d_wp, d_wout)
    d_h1 = _mm(d_proj, wp, "nt", name="d_h1", after=[token])
    dx, d_n1w = _rmsnorm_bwd(d_h1, x, n1w, dx2, "norm1_bwd", after=[token])
    grads = dict(wp=d_wp, conv_w=d_convw, w_out=d_wout, w_up=d_wup, fcw_g=d_fcwg, fcw_u=d_fcwu, w_down=d_wdown,
                 n1w=d_n1w, n2w=d_n2w, fnw=d_fnw, gp=d_gp, gnw=d_gnw)
    return loss, dx, grads


_HBM = pl.BlockSpec(memory_space=pltpu.HBM)


def _pos():
    return lax.axis_index("x"), lax.axis_index("y"), lax.axis_index("c")


def _other_chips(x, y):
    return [(1 - x, y), (x, 1 - y), (1 - x, 1 - y)]


def _halvable(shape):
    return shape[0] % 32 == 0


def _rows_of_half(shape, half):
    if not _halvable(shape):
        return pl.ds(0, shape[0])
    return pl.ds(pl.multiple_of(half * (shape[0] // 2), 16), shape[0] // 2)


_SEM = pl.BlockSpec(memory_space=pltpu.SEMAPHORE)
_ANY = pl.BlockSpec(memory_space=pl.ANY)
_DATAFLOW = pltpu.SideEffectType.DATAFLOW_SIDE_EFFECTING


def _in_hbm(a):
    return pltpu.with_memory_space_constraint(a, pltpu.HBM)


def _halves_copy(src_refs, land_refs, send_sems, recv_sems, shapes, a, j, block, x, y, c):
    px, py = _other_chips(x, y)[j]
    rows = _rows_of_half(shapes[a], c)
    return pltpu.make_async_remote_copy(
        src_ref=src_refs[a].at[rows, :], dst_ref=land_refs[a].at[block, rows, :], send_sem=send_sems.at[3 * a + j],
        recv_sem=recv_sems.at[3 * a + j], device_id=(px, py, c), device_id_type=MESH)


def _gather_halves_start(shards, after, name):
    n = len(shards)
    shapes = [s.shape for s in shards]

    def body(*refs):
        ins, lands = refs[:n], refs[n:2 * n]
        send_sems, recv_sems = refs[2 * n + 1], refs[2 * n + 2]
        token = refs[-1]
        x, y, c = _pos()
        q = 2 * x + y
        for a in range(n):
            for j in range(3):
                _halves_copy(ins, lands, send_sems, recv_sems, shapes, a, j, q, x, y, c).start()
        token[...] = jnp.zeros_like(token)

    land_shapes = [(N_CHIPS,) + s.shape for s in shards]
    return pl.pallas_call(
        body, name=name,
        out_shape=(pltpu.SemaphoreType.DMA((3 * n,)), pltpu.SemaphoreType.DMA((3 * n,)),
                   *[pltpu.HBM(s.shape, s.dtype) for s in shards],
                   *[pltpu.HBM(ls, s.dtype) for ls, s in zip(land_shapes, shards)],
                   jax.ShapeDtypeStruct((8, 128), F32)),
        in_specs=[_HBM] * (2 * n) + [_ANY],
        out_specs=(_SEM, _SEM, *[_HBM] * (2 * n), pl.BlockSpec(memory_space=pltpu.VMEM)),
        input_output_aliases={a: 2 + a for a in range(2 * n)},
        compiler_params=pltpu.CompilerParams(has_side_effects=_DATAFLOW),
    )(*[_in_hbm(s) for s in shards], *[_in_hbm(lax.empty(ls, s.dtype)) for ls, s in zip(land_shapes, shards)], after)


def _gather_halves_wait(started, after, name):
    send_sems, recv_sems, *thru = started
    n = len(thru) // 2
    shapes = [t.shape for t in thru[:n]]

    def body(*refs):
        ins, lands = refs[:n], refs[n:2 * n]
        send_sems, recv_sems = refs[2 * n], refs[2 * n + 1]
        x, y, c = _pos()
        q = 2 * x + y
        chips = _other_chips(x, y)
        for a in range(n):
            for j, (px, py) in enumerate(chips):
                _halves_copy(ins, lands, send_sems, recv_sems, shapes, a, j, q, x, y, c).wait_send()
                _halves_copy(ins, lands, send_sems, recv_sems, shapes, a, j, 2 * px + py, x, y, c).wait_recv()

    outs = pl.pallas_call(
        body, name=name, out_shape=[pltpu.HBM(t.shape, t.dtype) for t in thru],
        in_specs=[_HBM] * (2 * n) + [_SEM, _SEM] + [_ANY] * len(after), out_specs=[_HBM] * (2 * n),
        input_output_aliases={a: a for a in range(2 * n)},
        compiler_params=pltpu.CompilerParams(has_side_effects=_DATAFLOW),
    )(*thru, send_sems, recv_sems, *after)
    return outs[:n], outs[n:]


def _sibling_fill(gathered, name):
    big = [a for a, g in enumerate(gathered) if _halvable(g.shape[1:])]
    n = len(gathered)

    def body(*refs):
        ins, outs = refs[:n], refs[n:2 * n]
        send_sems, recv_sems = refs[2 * n:]
        x, y, c = _pos()
        chips = _other_chips(x, y)

        def copy(k, j, half):
            a = big[k]
            px, py = chips[j]
            rows = _rows_of_half(gathered[a].shape[1:], half)
            return pltpu.make_async_remote_copy(
                src_ref=ins[a].at[2 * px + py, rows, :], dst_ref=outs[a].at[2 * px + py, rows, :],
                send_sem=send_sems.at[3 * k + j], recv_sem=recv_sems.at[3 * k + j],
                device_id=(x, y, 1 - c), device_id_type=MESH)

        sends = [copy(k, j, c) for k in range(len(big)) for j in range(3)]
        for cp in sends:
            cp.start()
        for k in range(len(big)):
            for j in range(3):
                copy(k, j, 1 - c).wait_recv()
        for cp in sends:
            cp.wait_send()

    return pl.pallas_call(
        body, name=name, in_specs=[_HBM] * n, out_specs=[_HBM] * n,
        out_shape=[jax.ShapeDtypeStruct(g.shape, g.dtype) for g in gathered],
        input_output_aliases={a: a for a in range(n)},
        scratch_shapes=[pltpu.SemaphoreType.DMA((3 * len(big),)), pltpu.SemaphoreType.DMA((3 * len(big),))],
    )(*gathered)


def _fill_copy(refs, send_sems, recv_sems, shapes, a, j, half, x, y, c):
    px, py = _other_chips(x, y)[j]
    rows = _rows_of_half(shapes[a], half)
    return pltpu.make_async_remote_copy(
        src_ref=refs[a].at[2 * px + py, rows, :], dst_ref=refs[a].at[2 * px + py, rows, :],
        send_sem=send_sems.at[3 * a + j], recv_sem=recv_sems.at[3 * a + j],
        device_id=(x, y, 1 - c), device_id_type=MESH)


def _sibling_fill_start(gathered, name):
    n = len(gathered)
    shapes = [g.shape[1:] for g in gathered]

    def body(*refs):
        ins = refs[:n]
        send_sems, recv_sems = refs[n], refs[n + 1]
        token = refs[-1]
        x, y, c = _pos()
        for a in range(n):
            for j in range(3):
                _fill_copy(ins, send_sems, recv_sems, shapes, a, j, c, x, y, c).start()
        token[...] = jnp.zeros_like(token)

    return pl.pallas_call(
        body, name=name,
        out_shape=(pltpu.SemaphoreType.DMA((3 * n,)), pltpu.SemaphoreType.DMA((3 * n,)),
                   *[pltpu.HBM(g.shape, g.dtype) for g in gathered], jax.ShapeDtypeStruct((8, 128), F32)),
        in_specs=[_HBM] * n,
        out_specs=(_SEM, _SEM, *[_HBM] * n, pl.BlockSpec(memory_space=pltpu.VMEM)),
        input_output_aliases={a: 2 + a for a in range(n)},
        compiler_params=pltpu.CompilerParams(has_side_effects=_DATAFLOW),
    )(*[_in_hbm(g) for g in gathered])


def _sibling_fill_wait(started, after, name):
    send_sems, recv_sems, *thru = started
    n = len(thru)
    shapes = [t.shape[1:] for t in thru]

    def body(*refs):
        ins = refs[:n]
        send_sems, recv_sems = refs[n], refs[n + 1]
        x, y, c = _pos()
        for a in range(n):
            for j in range(3):
                _fill_copy(ins, send_sems, recv_sems, shapes, a, j, c, x, y, c).wait_send()
                _fill_copy(ins, send_sems, recv_sems, shapes, a, j, 1 - c, x, y, c).wait_recv()

    return pl.pallas_call(
        body, name=name, out_shape=[pltpu.HBM(t.shape, t.dtype) for t in thru],
        in_specs=[_HBM] * n + [_SEM, _SEM] + [_ANY] * len(after), out_specs=[_HBM] * n,
        input_output_aliases={a: a for a in range(n)},
        compiler_params=pltpu.CompilerParams(has_side_effects=_DATAFLOW),
    )(*thru, send_sems, recv_sems, *after)


def _place_own(shards, gathered, cq, name, carry=()):
    n = len(shards)
    nc = len(carry)
    steps = 4

    def body(cq_ref, *refs):
        for a in range(n):
            refs[2 * n + nc + a][...] = refs[a][...]

    def tile(shape):
        return shape[0] // steps if _halvable(shape) else shape[0]

    in_specs = [pl.BlockSpec((tile(s.shape), s.shape[1]), (lambda i, s_: (i, 0)) if _halvable(s.shape) else (lambda i, s_: (0, 0)))
                for s in shards]
    in_specs += [pl.BlockSpec(memory_space=pl.ANY)] * (n + nc)
    out_specs = [pl.BlockSpec((None, tile(s.shape), s.shape[1]),
                              (lambda i, s_: (s_[1], i, 0)) if _halvable(s.shape) else (lambda i, s_: (s_[1], 0, 0)))
                 for s in shards]
    out_specs += [pl.BlockSpec(memory_space=pl.ANY)] * nc
    gs = pltpu.PrefetchScalarGridSpec(num_scalar_prefetch=1, grid=(steps,), in_specs=in_specs, out_specs=out_specs)
    outs = pl.pallas_call(
        body, name=name, grid_spec=gs,
        out_shape=[jax.ShapeDtypeStruct(g.shape, g.dtype) for g in gathered] + [jax.ShapeDtypeStruct(t.shape, t.dtype) for t in carry],
        input_output_aliases={1 + n + a: a for a in range(n + nc)},
        compiler_params=_cparams(("arbitrary",)),
    )(cq, *shards, *gathered, *carry)
    return (outs[:n], outs[n:]) if nc else outs


def _half_rows(ref, c, rh):
    return ref.at[:, pl.ds(pl.multiple_of(c * rh, 8), rh), :]


def _chips_copy(src_refs, land_refs, send_sems, recv_sems, a, j, x, y, c):
    px, py = _other_chips(x, y)[j]
    return pltpu.make_async_remote_copy(src_ref=src_refs[a].at[2 * px + py], dst_ref=land_refs[a].at[j],
                                        send_sem=send_sems.at[3 * a + j], recv_sem=recv_sems.at[3 * a + j],
                                        device_id=(px, py, c), device_id_type=MESH)


def _grad_chips_start(parts, name):
    n = len(parts)

    def body(*refs):
        ins, lands = refs[:n], refs[n:2 * n]
        send_sems, recv_sems = refs[2 * n], refs[2 * n + 1]
        token = refs[-1]
        x, y, c = _pos()
        for a in range(n):
            for j in range(3):
                _chips_copy(ins, lands, send_sems, recv_sems, a, j, x, y, c).start()
        token[...] = jnp.zeros_like(token)

    land_shapes = [(3,) + p.shape[1:] for p in parts]
    return pl.pallas_call(
        body, name=name,
        out_shape=(pltpu.SemaphoreType.DMA((3 * n,)), pltpu.SemaphoreType.DMA((3 * n,)),
                   *[pltpu.HBM(p.shape, p.dtype) for p in parts],
                   *[pltpu.HBM(ls, p.dtype) for ls, p in zip(land_shapes, parts)],
                   jax.ShapeDtypeStruct((8, 128), F32)),
        in_specs=[_HBM] * (2 * n),
        out_specs=(_SEM, _SEM, *[_HBM] * (2 * n), pl.BlockSpec(memory_space=pltpu.VMEM)),
        input_output_aliases={a: 2 + a for a in range(2 * n)},
        compiler_params=pltpu.CompilerParams(has_side_effects=_DATAFLOW),
    )(*[_in_hbm(p) for p in parts], *[_in_hbm(lax.empty(ls, p.dtype)) for ls, p in zip(land_shapes, parts)])


def _grad_chips_wait(started, after, name):
    send_sems, recv_sems, *thru = started
    n = len(thru) // 2

    def body(*refs):
        ins, lands = refs[:n], refs[n:2 * n]
        send_sems, recv_sems = refs[2 * n], refs[2 * n + 1]
        x, y, c = _pos()
        for a in range(n):
            for j in range(3):
                cp = _chips_copy(ins, lands, send_sems, recv_sems, a, j, x, y, c)
                cp.wait_send()
                cp.wait_recv()

    outs = pl.pallas_call(
        body, name=name, out_shape=[pltpu.HBM(t.shape, t.dtype) for t in thru],
        in_specs=[_HBM] * (2 * n) + [_SEM, _SEM] + [_ANY] * len(after), out_specs=[_HBM] * (2 * n),
        input_output_aliases={a: a for a in range(2 * n)},
        compiler_params=pltpu.CompilerParams(has_side_effects=_DATAFLOW),
    )(*thru, send_sems, recv_sems, *after)
    return outs[n:]


def _sibling_copy(src_refs, land_refs, send_sems, recv_sems, rhs, a, c, x, y):
    return pltpu.make_async_remote_copy(src_ref=_half_rows(src_refs[a], 1 - c, rhs[a]), dst_ref=land_refs[a],
                                        send_sem=send_sems.at[a], recv_sem=recv_sems.at[a],
                                        device_id=(x, y, 1 - c), device_id_type=MESH)


def _grad_sibling_start(fams, name):
    n = len(fams)
    rhs = [f.shape[1] // 2 for f in fams]

    def body(*refs):
        ins, lands = refs[:n], refs[n:2 * n]
        send_sems, recv_sems = refs[2 * n], refs[2 * n + 1]
        token = refs[-1]
        x, y, c = _pos()
        for a in range(n):
            _sibling_copy(ins, lands, send_sems, recv_sems, rhs, a, c, x, y).start()
        token[...] = jnp.zeros_like(token)

    land_shapes = [(f.shape[0], f.shape[1] // 2, f.shape[2]) for f in fams]
    return pl.pallas_call(
        body, name=name,
        out_shape=(pltpu.SemaphoreType.DMA((n,)), pltpu.SemaphoreType.DMA((n,)),
                   *[pltpu.HBM(f.shape, f.dtype) for f in fams],
                   *[pltpu.HBM(ls, f.dtype) for ls, f in zip(land_shapes, fams)],
                   jax.ShapeDtypeStruct((8, 128), F32)),
        in_specs=[_HBM] * (2 * n),
        out_specs=(_SEM, _SEM, *[_HBM] * (2 * n), pl.BlockSpec(memory_space=pltpu.VMEM)),
        input_output_aliases={a: 2 + a for a in range(2 * n)},
        compiler_params=pltpu.CompilerParams(has_side_effects=_DATAFLOW),
    )(*[_in_hbm(f) for f in fams], *[_in_hbm(lax.empty(ls, f.dtype)) for ls, f in zip(land_shapes, fams)])


def _grad_sibling_wait(started, after, name):
    send_sems, recv_sems, *thru = started
    n = len(thru) // 2
    rhs = [t.shape[1] // 2 for t in thru[:n]]

    def body(*refs):
        ins, lands = refs[:n], refs[n:2 * n]
        send_sems, recv_sems = refs[2 * n], refs[2 * n + 1]
        x, y, c = _pos()
        for a in range(n):
            cp = _sibling_copy(ins, lands, send_sems, recv_sems, rhs, a, c, x, y)
            cp.wait_send()
            cp.wait_recv()

    outs = pl.pallas_call(
        body, name=name, out_shape=[pltpu.HBM(t.shape, t.dtype) for t in thru],
        in_specs=[_HBM] * (2 * n) + [_SEM, _SEM] + [_ANY] * len(after), out_specs=[_HBM] * (2 * n),
        input_output_aliases={a: a for a in range(2 * n)},
        compiler_params=pltpu.CompilerParams(has_side_effects=_DATAFLOW),
    )(*thru, send_sems, recv_sems, *after)
    return outs[:n], outs[n:]


def _grad_share(fulls, name, small=None):
    n = len(fulls)
    ns = 0 if small is None else 1
    rhs = [f.shape[0] // 2 for f in fulls]

    def body(*refs):
        ins, outs = refs[:n], refs[n + ns:2 * n + ns]
        send_sems, recv_sems = refs[2 * (n + ns)], refs[2 * (n + ns) + 1]
        x, y, c = _pos()

        def copy(a, half):
            rows = pl.ds(pl.multiple_of(half * rhs[a], 8), rhs[a])
            return pltpu.make_async_remote_copy(src_ref=ins[a].at[rows, :], dst_ref=outs[a].at[rows, :],
                                                send_sem=send_sems.at[7 * ns + a], recv_sem=recv_sems.at[7 * ns + a],
                                                device_id=(x, y, 1 - c), device_id_type=MESH)

        sends = [copy(a, c) for a in range(n)]
        for cp in sends:
            cp.start()
        if ns:
            small_ref, all_ref = refs[n], refs[2 * n + 1]
            me = 4 * x + 2 * y + c

            def peer(r):
                dx, dy, dc = (r >> 2) & 1, (r >> 1) & 1, r & 1
                return (x if dx == 0 else 1 - x), (y if dy == 0 else 1 - y), (c if dc == 0 else 1 - c)

            def small_copy(r, slot):
                return pltpu.make_async_remote_copy(src_ref=small_ref, dst_ref=all_ref.at[slot], send_sem=send_sems.at[r - 1],
                                                    recv_sem=recv_sems.at[r - 1], device_id=peer(r), device_id_type=MESH)

            smalls = [small_copy(r, me) for r in range(1, 8)]
            for cp in smalls:
                cp.start()
            for r in range(1, 8):
                px, py, pc = peer(r)
                small_copy(r, 4 * px + 2 * py + pc).wait_recv()
            sends = sends + smalls
        for a in range(n):
            copy(a, 1 - c).wait_recv()
        for cp in sends:
            cp.wait_send()

    return pl.pallas_call(
        body, name=name, in_specs=[_HBM] * (n + ns), out_specs=[_HBM] * (n + ns),
        out_shape=[jax.ShapeDtypeStruct(f.shape, f.dtype) for f in fulls]
        + ([jax.ShapeDtypeStruct((8,) + small.shape, small.dtype)] if ns else []),
        input_output_aliases={a: a for a in range(n)},
        scratch_shapes=[pltpu.SemaphoreType.DMA((7 * ns + n,)), pltpu.SemaphoreType.DMA((7 * ns + n,))],
    )(*fulls, *([small] if ns else []))


def _add_sibling(own, recv, cq, name):
    nb, R, Cc = own.shape
    Rh = R // 2

    def body(cq_ref, a_ref, b_ref, o32_ref, o16_ref):
        s = a_ref[0] + b_ref[0]
        mine = pl.program_id(0) == cq_ref[1]

        @pl.when(mine)
        def _():
            o32_ref[...] = s

        @pl.when(jnp.logical_not(mine))
        def _():
            o16_ref[0] = s.astype(o16_ref.dtype)

    sp = pl.BlockSpec((1, Rh, Cc), lambda b, s: (b, 0, 0))
    gs = pltpu.PrefetchScalarGridSpec(
        num_scalar_prefetch=1, grid=(nb,),
        in_specs=[pl.BlockSpec((1, Rh, Cc), lambda b, s: (b, s[0], 0)), sp],
        out_specs=[pl.BlockSpec((Rh, Cc), lambda b, s: (0, 0)), sp])
    return pl.pallas_call(
        body, name=name, grid_spec=gs,
        out_shape=[jax.ShapeDtypeStruct((Rh, Cc), F32), jax.ShapeDtypeStruct((nb, Rh, Cc), _MXU)],
        compiler_params=_cparams(("arbitrary",)),
    )(cq, own, recv)


def _add_sibling_split(d_wp, recv, cq, name):
    _, Dm, Pc = d_wp.shape
    Rh = Dm // 2
    Wb = IN_COLS // N_CHIPS
    T = 256

    def body(cq_ref, a_ref, b_ref, o32_ref, o16_ref):
        s = a_ref[0] + b_ref[0]
        blocks = [s[:, 0:Wb], s[:, Wb:2 * Wb],
                  jnp.concatenate([s[:, 2 * Wb:P_QKVB], s[:, P_BA:P_BA + 8], s[:, P_QKVB:3 * Wb - 8]], axis=1),
                  s[:, 3 * Wb - 8:P_BA]]
        q = cq_ref[1]
        own = None
        for j, blk in enumerate(blocks):
            term = jnp.where(q == j, blk, 0.0)
            own = term if own is None else own + term
            o16_ref[j] = blk.astype(o16_ref.dtype)
        o32_ref[...] = own

    gs = pltpu.PrefetchScalarGridSpec(
        num_scalar_prefetch=1, grid=(Rh // T,),
        in_specs=[pl.BlockSpec((1, T, Pc), lambda i, s: (0, s[0] * (Rh // T) + i, 0)), pl.BlockSpec((1, T, Pc), lambda i, s: (0, i, 0))],
        out_specs=[pl.BlockSpec((T, Wb), lambda i, s: (i, 0)), pl.BlockSpec((N_CHIPS, T, Wb), lambda i, s: (0, i, 0))])
    return pl.pallas_call(
        body, name=name, grid_spec=gs,
        out_shape=[jax.ShapeDtypeStruct((Rh, Wb), F32), jax.ShapeDtypeStruct((N_CHIPS, Rh, Wb), _MXU)],
        compiler_params=_cparams(("parallel",)),
    )(cq, d_wp, recv)


def _add_chips(part32, recv3, cq, name):
    Rh, Cc = part32.shape

    def body(cq_ref, a_ref, b_ref, o_ref):
        acc = a_ref[...]
        for j in range(3):
            acc = acc + b_ref[j].astype(F32)
        o_ref[...] = acc

    gs = pltpu.PrefetchScalarGridSpec(
        num_scalar_prefetch=1, grid=(1,),
        in_specs=[pl.BlockSpec((Rh, Cc), lambda i, s: (0, 0)), pl.BlockSpec((3, Rh, Cc), lambda i, s: (0, 0, 0))],
        out_specs=pl.BlockSpec((Rh, Cc), lambda i, s: (s[0], 0)))
    return pl.pallas_call(
        body, name=name, grid_spec=gs, out_shape=jax.ShapeDtypeStruct((2 * Rh, Cc), F32),
        compiler_params=_cparams(("arbitrary",)),
    )(cq, part32, recv3)


def _transposed(g):
    Dm, n = g.shape
    pad = -n % 128

    def body(g_ref, o_ref):
        xp = jnp.concatenate([g_ref[...], jnp.zeros((Dm, pad), F32)], axis=1)
        o_ref[...] = xp.T[:n, :]

    return pl.pallas_call(body, name="transposed", out_shape=jax.ShapeDtypeStruct((n, Dm), F32),
                          compiler_params=_cparams(vmem=V7X_VMEM_LIMIT))(g)


def _adamw(w, g, m, v, name):
    R, Cc = w.shape
    T = max([t for t in range(8, 257, 8) if R % t == 0], default=R)

    def body(w_ref, g_ref, m_ref, v_ref, d_ref, mo_ref, vo_ref):
        d_ref[...], mo_ref[...], vo_ref[...] = _adamw_math(w_ref[...], g_ref[...], m_ref[...], v_ref[...])

    sp = pl.BlockSpec((T, Cc), lambda i: (i, 0))
    sh = jax.ShapeDtypeStruct((R, Cc), F32)
    return pl.pallas_call(
        body, name=name, grid=(R // T,), in_specs=[sp] * 4, out_specs=(sp, sp, sp), out_shape=(sh, sh, sh),
        compiler_params=_cparams(("parallel",)),
    )(w, g, m, v)


SMALL_ROWS = 32
ROW_CONV, ROW_FCG, ROW_FCU = 5, 13, 22


def _adamw_math(w, g, m, v):
    mn = ADAM_B1 * m + (1.0 - ADAM_B1) * g
    vn = ADAM_B2 * v + (1.0 - ADAM_B2) * (g * g)
    c1 = 1.0 / (1.0 - ADAM_B1 ** ADAM_STEP)
    c2 = 1.0 / (1.0 - ADAM_B2 ** ADAM_STEP)
    return -ADAM_LR * ((mn * c1) / (jnp.sqrt(vn * c2) + ADAM_EPS) + ADAM_WD * w), mn, vn


def _pack_small(n1, n2, fn, gp, gn, conv, fcg, fcu, loss):
    W = D_MODEL

    def body(n1_ref, n2_ref, fn_ref, gp_ref, gn_ref, conv_ref, fcg_ref, fcu_ref, loss_ref, o_ref):
        o_ref[...] = jnp.zeros_like(o_ref)
        o_ref[0:1, :] = n1_ref[...]
        o_ref[1:2, :] = n2_ref[...]
        o_ref[2:3, :] = fn_ref[...]
        o_ref[3:4, 0:8] = gp_ref[0:1, 0:8]
        o_ref[3:4, 8:9] = loss_ref[0:1, 0:1]
        o_ref[4:5, 0:128] = gn_ref[...]
        for i in range(GDN_CONV):
            o_ref[ROW_CONV + 2 * i:ROW_CONV + 2 * i + 1, :] = conv_ref[i:i + 1, 0:W]
            o_ref[ROW_CONV + 2 * i + 1:ROW_CONV + 2 * i + 2, 0:3 * GDN_WIDTH - W] = conv_ref[i:i + 1, W:3 * GDN_WIDTH]
        for r0, ref in ((ROW_FCG, fcg_ref), (ROW_FCU, fcu_ref)):
            for i in range(FFN_CONV):
                for k in range(3):
                    n = min(W, D_FF - k * W)
                    o_ref[r0 + 3 * i + k:r0 + 3 * i + k + 1, 0:n] = ref[i:i + 1, k * W:k * W + n]

    return pl.pallas_call(body, name="pack_small", out_shape=jax.ShapeDtypeStruct((SMALL_ROWS, W), F32))(
        n1, n2, fn, gp, gn, conv, fcg, fcu, loss)


def _small_step(meq, small_all, small, ws, ms, vs):
    W = D_MODEL
    n = len(ws)
    cw, fw = ws[6].shape[1], ws[7].shape[1]

    def body(meq_ref, all_ref, own_ref, *refs):
        w_refs, m_refs, v_refs = refs[:n], refs[n:2 * n], refs[2 * n:3 * n]
        loss_ref = refs[3 * n]
        outs = refs[3 * n + 1:]
        me, q = meq_ref[0], meq_ref[1]
        red = None
        for d in range(8):
            term = jnp.where(me == d, own_ref[...], all_ref[d])
            red = term if red is None else red + term
        loss_ref[...] = jnp.broadcast_to(red[3:4, 8:9], loss_ref.shape)
        conv = [jnp.concatenate([red[ROW_CONV + 2 * i:ROW_CONV + 2 * i + 1, :],
                                 red[ROW_CONV + 2 * i + 1:ROW_CONV + 2 * i + 2, 0:3 * GDN_WIDTH - W]], axis=1)
                for i in range(GDN_CONV)]
        conv = jnp.concatenate(conv, axis=0)

        def fc_rows(r0):
            rows = [jnp.concatenate([red[r0 + 3 * i + k:r0 + 3 * i + k + 1, 0:min(W, D_FF - k * W)] for k in range(3)], axis=1)
                    for i in range(FFN_CONV)]
            return jnp.concatenate(rows, axis=0)

        fc = jnp.concatenate([fc_rows(ROW_FCG), fc_rows(ROW_FCU)], axis=1)

        def chip_block(full, width):
            out = None
            for j in range(N_CHIPS):
                term = jnp.where(q == j, full[:, width * j:width * (j + 1)], 0.0)
                out = term if out is None else out + term
            return out

        grads = [red[0:1, :], red[1:2, :], red[2:3, :], red[3:4, 0:4], red[3:4, 4:8], red[4:5, 0:128],
                 chip_block(conv, cw), chip_block(fc, fw)]
        for k in range(n):
            d_, m_, v_ = _adamw_math(w_refs[k][...], grads[k], m_refs[k][...], v_refs[k][...])
            outs[4 * k][...] = grads[k]
            outs[4 * k + 1][...] = d_
            outs[4 * k + 2][...] = m_
            outs[4 * k + 3][...] = v_

    full = lambda a: pl.BlockSpec(a.shape, lambda i, s_, nd=len(a.shape): (0,) * nd)
    arrays = [small_all, small, *ws, *ms, *vs]
    out_shapes = [jax.ShapeDtypeStruct((8, 128), F32)] + [jax.ShapeDtypeStruct(w.shape, F32) for w in ws for _ in range(4)]
    gs = pltpu.PrefetchScalarGridSpec(
        num_scalar_prefetch=1, grid=(1,), in_specs=[full(a) for a in arrays],
        out_specs=[pl.BlockSpec(o.shape, lambda i, s_, nd=len(o.shape): (0,) * nd) for o in out_shapes])
    return pl.pallas_call(body, name="small_step", grid_spec=gs, out_shape=out_shapes)(meq, *arrays)


def _pad_lanes(v, n=D_MODEL):
    return jnp.pad(v, ((0, 0), (0, n - v.shape[1])))


def kernel(x, norm1_w, w_in, conv_qkv_w, a_log, dt_bias, gdn_norm_w, w_out, norm2_w, w_up, ffn_conv_w, w_down, final_norm_w, loss_target, m_norm1_w, m_w_in, m_conv_qkv_w, m_a_log, m_dt_bias, m_gdn_norm_w, m_w_out, m_norm2_w, m_w_up, m_ffn_conv_w, m_w_down, m_final_norm_w, v_norm1_w, v_w_in, v_conv_qkv_w, v_a_log, v_dt_bias, v_gdn_norm_w, v_w_out, v_norm2_w, v_w_up, v_ffn_conv_w, v_w_down, v_final_norm_w):
    c = lax.axis_index("c")
    q = 2 * lax.axis_index("x") + lax.axis_index("y")
    S = x.shape[1]
    cq = jnp.stack([c, q]).astype(jnp.int32)

    *in_started, in_token = _gather_halves_start([w_in[0].astype(_MXU), conv_qkv_w[0], ffn_conv_w[0]], x, "gather_in_start")
    w_in_l, m_w_in_l, v_w_in_l = (jnp.swapaxes(a + in_token[0:1, 0:1], 1, 2)[0] for a in (w_in, m_w_in, v_w_in))
    h1 = _rmsnorm_fwd(x[0], norm1_w, "norm1", after=[in_token])
    rest = [(a[0] + in_token[0:1, 0:1]).astype(_MXU) for a in (w_out, w_up, w_down)]
    in_shards, got_in = _gather_halves_wait(in_started, [w_in_l, m_w_in_l, v_w_in_l, h1, *rest], "gather_in_wait")
    (g_in, g_conv, g_fconv), (w_in_l, m_w_in_l, v_w_in_l) = _place_own(
        in_shards, _sibling_fill(got_in, "fill_in"), cq, "place_in", carry=[w_in_l, m_w_in_l, v_w_in_l])
    *rest_started, token = _gather_halves_start(rest, g_conv, "gather_rest_start")

    rest_state = {}

    def rest_arrived(after):
        rest_state["shards"], got = _gather_halves_wait(rest_started, after, "gather_rest_wait")
        *rest_state["fill"], tok = _sibling_fill_start(got, "fill_rest_start")
        return tok

    def rest_filled(after):
        got = _sibling_fill_wait(rest_state["fill"], after, "fill_rest_wait")
        g_out, g_up, g_down = _place_own(rest_state["shards"], got, cq, "place_rest")
        return g_out.reshape(D_MODEL, D_MODEL), g_up, g_down.reshape(D_FF, D_MODEL)

    rest_weights = (rest_arrived, rest_filled)
    wp = _wp_assemble(g_in, [token])
    conv_f = jnp.concatenate([g_conv[i] for i in range(N_CHIPS)], axis=1)
    fcw = jnp.concatenate([g_fconv[i] for i in range(N_CHIPS)], axis=1)
    gp = _pad_lanes(jnp.concatenate([a_log, dt_bias], axis=1), 128)
    fnw = final_norm_w[None, :]
    early = {}

    def early_sibling(d_wup, d_wdown):
        *early["sibling"], tok = _grad_sibling_start([d_wup, d_wdown.reshape(N_CHIPS, D_FF // N_CHIPS, D_MODEL)],
                                                     "grad_sibling_early_start")
        return tok

    def early_chips(dx2):
        fams_e, got_e = _grad_sibling_wait(early["sibling"], [dx2], "grad_sibling_early_wait")
        early["parts"] = [_add_sibling(f, r, cq, "add_sibling_" + nm) for f, r, nm in zip(fams_e, got_e, ("w_up", "w_down"))]
        *early["started"], tok = _grad_chips_start([p[1] for p in early["parts"]], "grad_chips_start")
        return tok

    def late_sibling(d_wp, d_wout):
        *early["late_sibling"], tok = _grad_sibling_start(
            [d_wp[None], d_wout.reshape(N_CHIPS, D_MODEL // N_CHIPS, D_MODEL)], "grad_sibling_late_start")
        return tok

    early_grads = (early_sibling, early_chips, late_sibling)

    loss_l, dx, g = _local_step(x[0], loss_target[0], h1, norm1_w, norm2_w, fnw, gp, gdn_norm_w, wp,
                                conv_f, fcw, rest_weights, early_grads)
    small = _pack_small(g["n1w"], g["n2w"], g["fnw"], g["gp"], g["gnw"], g["conv_w"], g["fcw_g"], g["fcw_u"], loss_l)
    fams, got = _grad_sibling_wait(early["late_sibling"], [dx], "grad_sibling_late_wait")
    parts = [_add_sibling_split(fams[0], got[0], cq, "add_sibling_w_in"), _add_sibling(fams[1], got[1], cq, "add_sibling_w_out")]
    *late_started, late_token = _grad_chips_start([p[1] for p in parts], "grad_chips_late_start")
    got3_e = _grad_chips_wait(early["started"], [dx, g["wp"], late_token], "grad_chips_wait")
    g_w_up, g_w_down = _grad_share(
        [_add_chips(p[0], r3, cq, "add_chips_" + nm) for p, r3, nm in zip(early["parts"], got3_e, ("w_up", "w_down"))],
        "grad_share_early")
    big = {}

    def adamw_big(nm, w, gg, m, v):
        d_, m_, v_ = _adamw(w[0], gg, m[0], v[0], "adamw_" + nm)
        big[nm] = (gg[None], d_[None], m_[None], v_[None])

    adamw_big("w_up", w_up, g_w_up, m_w_up, v_w_up)
    adamw_big("w_down", w_down, g_w_down, m_w_down, v_w_down)
    got3 = _grad_chips_wait(late_started, [big["w_up"][1], big["w_down"][1]], "grad_chips_late_wait")
    g_w_in, g_w_out, small_all = _grad_share(
        [_add_chips(p[0], r3, cq, "add_chips_" + nm) for p, r3, nm in zip(parts, got3, ("w_in", "w_out"))],
        "grad_share_late", small)
    g_t = _transposed(g_w_in)
    d_t, m_t, v_t = _adamw(w_in_l, g_t, m_w_in_l, v_w_in_l, "adamw_w_in")
    big["w_in"] = tuple(jnp.swapaxes(t[None], 1, 2) for t in (g_t, d_t, m_t, v_t))
    adamw_big("w_out", w_out, g_w_out, m_w_out, v_w_out)
    small_names = ["norm1_w", "norm2_w", "final_norm_w", "a_log", "dt_bias", "gdn_norm_w", "conv_qkv_w", "ffn_conv_w"]
    loss_b, *small_out = _small_step(
        jnp.stack([2 * q + c, q]).astype(jnp.int32), small_all, small,
        [norm1_w, norm2_w, final_norm_w[None], a_log, dt_bias, gdn_norm_w, conv_qkv_w[0], ffn_conv_w[0]],
        [m_norm1_w, m_norm2_w, m_final_norm_w[None], m_a_log, m_dt_bias, m_gdn_norm_w, m_conv_qkv_w[0], m_ffn_conv_w[0]],
        [v_norm1_w, v_norm2_w, v_final_norm_w[None], v_a_log, v_dt_bias, v_gdn_norm_w, v_conv_qkv_w[0], v_ffn_conv_w[0]])
    like = dict(final_norm_w=lambda t: t[0], conv_qkv_w=lambda t: t[None], ffn_conv_w=lambda t: t[None])
    for k, nm in enumerate(small_names):
        big[nm] = tuple(like.get(nm, lambda t: t)(t) for t in small_out[4 * k:4 * k + 4])
    names = ["norm1_w", "w_in", "conv_qkv_w", "a_log", "dt_bias", "gdn_norm_w", "w_out", "norm2_w", "w_up",
             "ffn_conv_w", "w_down", "final_norm_w"]
    return (loss_b[0, 0], dx[None], *[big[n][0] for n in names], *[big[n][1] for n in names],
            *[big[n][2] for n in names], *[big[n][3] for n in names])
```

```python
import functools
import math

import numpy as np
import jax
import jax.numpy as jnp
from jax import lax
from jax.experimental import pallas as pl
from jax.experimental.pallas import tpu as pltpu

F32 = jnp.float32
BF16 = jnp.bfloat16
_MXU = jnp.bfloat16
_HI = lax.Precision.HIGHEST
EPS = 1e-6
V7X_VMEM_LIMIT = 56 * 1024 * 1024
MESH = pl.DeviceIdType.MESH

D_MODEL = 1024
GDN_HEADS, GDN_DIM, GDN_CHUNK, GDN_CONV = 4, 128, 64, 4
GDN_WIDTH = GDN_HEADS * GDN_DIM
DIL_HEADS, DIL_DIM = 8, 64
DIL_WIDTH = DIL_HEADS * DIL_DIM
D_FF, FFN_CONV = 2816, 3
IN_COLS = 3592
P_COLS = 3840
P_Z, P_QKVB, P_BA = 1536, 2048, 3584
ATT_T = 1024
ADAM_LR, ADAM_B1, ADAM_B2, ADAM_EPS, ADAM_WD, ADAM_STEP = 0.001, 0.9, 0.999, 1e-08, 0.01, 10
N_CHIPS = 4


def _cparams(sem=None, vmem=None):
    kw = {}
    if sem is not None:
        kw["dimension_semantics"] = sem
    if vmem is not None:
        kw["vmem_limit_bytes"] = vmem
    return pltpu.CompilerParams(**kw)


def _silu(x):
    return x * jax.nn.sigmoid(x)


def _pick_tile(n, cap):
    best = None
    for t in range(128, min(n, cap) + 1, 128):
        if n % t == 0:
            best = t
    return best or n


def _mm(a, b, mode, *, out_dtype=F32, residual=None, name, b_blocks=False, place=None, into=None, tn=None, after=()):
    if mode == "nn":
        M, K = a.shape
        N = b.shape[0] * b.shape[2] if b_blocks else b.shape[1]
    elif mode == "nt":
        (M, K), (N, _) = a.shape, b.shape
    else:
        (K, M), (_, N) = a.shape, b.shape
    tm = _pick_tile(M, 1024)
    tn = b.shape[2] if b_blocks else (tn or _pick_tile(N, 1536))

    def vmem(tm, tn):
        return 2 * (tm * K * a.dtype.itemsize + tn * K * b.dtype.itemsize
                    + tm * tn * (jnp.dtype(out_dtype).itemsize + (4 if residual is not None else 0))) + 3 * tm * tn * 4

    fixed_tn = b_blocks or (place is not None and place[0] == "blocks")
    while vmem(tm, tn) > 40 * 1024 * 1024:
        if (tm >= tn or fixed_tn) and tm % 256 == 0:
            tm //= 2
        elif tn % 256 == 0 and not fixed_tn:
            tn //= 2
        else:
            tm //= 2
    a_spec = pl.BlockSpec((K, tm), lambda j, i: (0, i)) if mode == "tn" else pl.BlockSpec((tm, K), lambda j, i: (i, 0))
    if b_blocks:
        b_spec = pl.BlockSpec((None, K, tn), lambda j, i: (j, 0, 0))
    else:
        b_spec = pl.BlockSpec((tn, K), lambda j, i: (j, 0)) if mode == "nt" else pl.BlockSpec((K, tn), lambda j, i: (0, j))
    r_spec = pl.BlockSpec((tm, tn), lambda j, i: (i, j))
    if place is None:
        o_spec, o_shape = r_spec, (M, N)
    elif place[0] == "rows":
        off = place[2] // tm
        o_spec, o_shape = pl.BlockSpec((tm, tn), lambda j, i: (i + off, j)), (place[1], N)
    else:
        off = place[2]
        o_spec, o_shape = pl.BlockSpec((None, tm, tn), lambda j, i: (j + off, i, 0)), (place[1], M, tn)
    dims = {"nn": (((1,), (0,)), ((), ())), "nt": (((1,), (1,)), ((), ())), "tn": (((0,), (0,)), ((), ()))}[mode]

    def body(*refs):
        a_ref, b_ref = refs[0], refs[1]
        o_ref = refs[-1]
        acc = lax.dot_general(a_ref[...].astype(_MXU), b_ref[...].astype(_MXU), dims, preferred_element_type=F32)
        if residual is not None:
            acc = acc + refs[2][...]
        o_ref[...] = acc.astype(out_dtype)

    ins, specs, alias = [a, b], [a_spec, b_spec], {}
    if residual is not None:
        ins.append(residual)
        specs.append(r_spec)
    if into is not None:
        alias = {len(ins): 0}
        ins.append(into)
        specs.append(pl.BlockSpec(memory_space=pl.ANY))
    ins += list(after)
    specs += [pl.BlockSpec(memory_space=pl.ANY)] * len(after)
    return pl.pallas_call(
        body, name=name, grid=(N // tn, M // tm), in_specs=specs, out_specs=o_spec,
        out_shape=jax.ShapeDtypeStruct(o_shape, out_dtype), input_output_aliases=alias,
        compiler_params=_cparams(("parallel", "parallel"), V7X_VMEM_LIMIT),
    )(*ins)


def _mm_nt_blocks(a_list, b4, name, after=()):
    M = a_list[0].shape[0]
    nb, N, Kb = b4.shape
    tm, tn = _pick_tile(M, 1024), _pick_tile(N, 512)

    def body(a0_ref, a1_ref, b_ref, *rest):
        o_ref = rest[-1]
        acc = None
        for blk in range(nb):
            a_ref = (a0_ref, a1_ref)[blk // 2]
            lo = (blk % 2) * Kb
            t = lax.dot_general(a_ref[:, lo:lo + Kb].astype(_MXU), b_ref[blk].astype(_MXU), (((1,), (1,)), ((), ())),
                                preferred_element_type=F32)
            acc = t if acc is None else acc + t
        o_ref[...] = acc

    a_spec = pl.BlockSpec((tm, 2 * Kb), lambda j, i: (i, 0))
    return pl.pallas_call(
        body, name=name, grid=(N // tn, M // tm),
        in_specs=[a_spec, a_spec, pl.BlockSpec((nb, tn, Kb), lambda j, i: (0, j, 0))]
        + [pl.BlockSpec(memory_space=pl.ANY)] * len(after),
        out_specs=pl.BlockSpec((tm, tn), lambda j, i: (i, j)), out_shape=jax.ShapeDtypeStruct((M, N), F32),
        compiler_params=_cparams(("parallel", "parallel"), V7X_VMEM_LIMIT),
    )(a_list[0], a_list[1], b4, *after)


def _wp_assemble(g_in, after=()):
    nb, Dm, Wb = g_in.shape
    T = 256
    n_lo = P_QKVB - 2 * Wb

    def body(g_ref, *rest):
        g2 = g_ref[2]
        rest[-1][...] = jnp.concatenate(
            [g_ref[0], g_ref[1], g2[:, :n_lo], g2[:, n_lo + 8:], g_ref[3], g2[:, n_lo:n_lo + 8],
             jnp.zeros((T, P_COLS - P_BA - 8), g_in.dtype)], axis=1)

    return pl.pallas_call(
        body, name="wp_assemble", grid=(Dm // T,),
        in_specs=[pl.BlockSpec((nb, T, Wb), lambda i: (0, i, 0))] + [pl.BlockSpec(memory_space=pl.ANY)] * len(after),
        out_specs=pl.BlockSpec((T, P_COLS), lambda i: (i, 0)), out_shape=jax.ShapeDtypeStruct((Dm, P_COLS), g_in.dtype),
        compiler_params=_cparams(("parallel",)),
    )(g_in, *after)


def _rmsnorm_fwd(x, w, name, after=()):
    S, D = x.shape
    T = _pick_tile(S, 512)

    def body(x_ref, w_ref, *rest):
        xv = x_ref[...]
        rs = lax.rsqrt(jnp.mean(xv * xv, axis=-1, keepdims=True) + EPS)
        rest[-1][...] = (xv * rs * w_ref[...]).astype(rest[-1].dtype)

    return pl.pallas_call(
        body, name=name, grid=(S // T,),
        in_specs=[pl.BlockSpec((T, D), lambda i: (i, 0)), pl.BlockSpec((1, D), lambda i: (0, 0))] + [_ANY] * len(after),
        out_specs=pl.BlockSpec((T, D), lambda i: (i, 0)),
        out_shape=jax.ShapeDtypeStruct((S, D), _MXU),
        compiler_params=_cparams(("parallel",)),
    )(x, w, *after)


def _rmsnorm_bwd(dh, x, w, dres, name, after=()):
    S, D = x.shape
    T = _pick_tile(S, 512)

    def body(dh_ref, x_ref, w_ref, dres_ref, *rest):
        dx_ref, dw_ref = rest[-2:]
        xv = x_ref[...]
        rs = lax.rsqrt(jnp.mean(xv * xv, axis=-1, keepdims=True) + EPS)
        xn = xv * rs
        dhv = dh_ref[...]
        dxn = dhv * w_ref[...]
        dx_ref[...] = dres_ref[...] + rs * (dxn - xn * jnp.mean(dxn * xn, axis=-1, keepdims=True))

        @pl.when(pl.program_id(0) == 0)
        def _():
            dw_ref[...] = jnp.zeros_like(dw_ref)

        dw_ref[...] += jnp.sum(dhv * xn, axis=0, keepdims=True)

    row = pl.BlockSpec((T, D), lambda i: (i, 0))
    vec = pl.BlockSpec((1, D), lambda i: (0, 0))
    return pl.pallas_call(
        body, name=name, grid=(S // T,), in_specs=[row, row, vec, row] + [_ANY] * len(after), out_specs=(row, vec),
        out_shape=(jax.ShapeDtypeStruct((S, D), F32), jax.ShapeDtypeStruct((1, D), F32)),
        compiler_params=_cparams(("arbitrary",)),
    )(dh, x, w, dres, *after)


def _loss_head(x3, w, tgt, name):
    S, D = x3.shape
    T = _pick_tile(S, 512)

    def body(x_ref, w_ref, t_ref, loss_ref, dx_ref, dxn_ref, dw_ref):
        xv = x_ref[...]
        rs = lax.rsqrt(jnp.mean(xv * xv, axis=-1, keepdims=True) + EPS)
        xn = xv * rs
        err = xn * w_ref[...] - t_ref[...]
        dy = err * (1.0 / D)
        dxn = dy * w_ref[...]
        dxv = rs * (dxn - xn * jnp.mean(dxn * xn, axis=-1, keepdims=True))
        dx_ref[...] = dxv
        dxn_ref[...] = dxv.astype(dxn_ref.dtype)

        @pl.when(pl.program_id(0) == 0)
        def _():
            dw_ref[...] = jnp.zeros_like(dw_ref)
            loss_ref[...] = jnp.zeros_like(loss_ref)

        dw_ref[...] += jnp.sum(dy * xn, axis=0, keepdims=True)
        part = jnp.sum(jnp.sum(err * err, axis=-1, keepdims=True), axis=0, keepdims=True) * (0.5 / D)
        loss_ref[...] += jnp.broadcast_to(part, loss_ref.shape)

    row = pl.BlockSpec((T, D), lambda i: (i, 0))
    vec = pl.BlockSpec((1, D), lambda i: (0, 0))
    return pl.pallas_call(
        body, name=name, grid=(S // T,), in_specs=[row, vec, row],
        out_specs=(pl.BlockSpec((8, 128), lambda i: (0, 0)), row, row, vec),
        out_shape=(jax.ShapeDtypeStruct((8, 128), F32), jax.ShapeDtypeStruct((S, D), F32), jax.ShapeDtypeStruct((S, D), _MXU),
                   jax.ShapeDtypeStruct((1, D), F32)),
        compiler_params=_cparams(("arbitrary",)),
    )(x3, w, tgt)


def _shifted(ext, back, lo, n):
    if back == 0:
        return ext[lo:lo + n, :]
    return pltpu.roll(ext, back % ext.shape[0], 0)[lo:lo + n, :]


def _conv_windows(ext, K, T):
    return [_shifted(ext, (K - 1) - i, 8, T) for i in range(K)]


def _conv_taps(ext, w, K, T):
    out = None
    for i, win in enumerate(_conv_windows(ext, K, T)):
        term = win * w[i:i + 1, :]
        out = term if out is None else out + term
    return out


def _conv_taps_t(ext, w, K, T):
    out = None
    for i in range(K):
        term = _shifted(ext, i - (K - 1), 0, T) * w[i:i + 1, :]
        out = term if out is None else out + term
    return out


def _tri_masks(C):
    r = lax.broadcasted_iota(jnp.int32, (C, C), 0)
    c = lax.broadcasted_iota(jnp.int32, (C, C), 1)
    return r == c, r >= c, r > c, r <= c


_NN, _NT, _TN = ((1,), (0,)), ((1,), (1,)), ((0,), (0,))
_GDN_PASSES = dict(qk=1, inv=1, sol=1, scan=1, bwd=1)


def _bdot_raw(a, b, kind, passes):
    dims = ({"NN": ((2,), (1,)), "NT": ((2,), (2,)), "TN": ((1,), (1,))}[kind], ((0,), (0,)))
    if passes == 0:
        return lax.dot_general(a, b, dims, precision=_HI, preferred_element_type=F32)
    ah, bh = a.astype(BF16), b.astype(BF16)
    out = lax.dot_general(ah, bh, dims, preferred_element_type=F32)
    if passes == 3:
        al, bl = (a - ah.astype(F32)).astype(BF16), (b - bh.astype(F32)).astype(BF16)
        out = out + lax.dot_general(ah, bl, dims, preferred_element_type=F32) + lax.dot_general(al, bh, dims, preferred_element_type=F32)
    return out


@functools.partial(jax.custom_vjp, nondiff_argnums=(2, 3))
def _bdot(a, b, kind, passes):
    return _bdot_raw(a, b, kind, passes)


def _bdot_fwd(a, b, kind, passes):
    return _bdot_raw(a, b, kind, passes), (a, b)


def _bdot_bwd(kind, passes, res, ct):
    a, b = res
    if kind == "NN":
        return _bdot_raw(ct, b, "NT", passes), _bdot_raw(a, ct, "TN", passes)
    if kind == "NT":
        return _bdot_raw(ct, b, "NN", passes), _bdot_raw(ct, a, "TN", passes)
    return _bdot_raw(b, ct, "NT", passes), _bdot_raw(a, ct, "NN", passes)


_bdot.defvjp(_bdot_fwd, _bdot_bwd)


def _softplus(x):
    return jnp.maximum(x, 0.0) + jnp.log(1.0 + jnp.exp(-jnp.abs(x)))


def _gdn_stage1(cq, ck, cv, b_col, a_col, alog, dtb, dot=_bdot_raw):
    C = cq.shape[1]
    eye, incl, strict, incl_t = _tri_masks(C)
    qn = cq * lax.rsqrt(jnp.sum(cq * cq, axis=-1, keepdims=True) + EPS) * (GDN_DIM ** -0.5)
    kn = ck * lax.rsqrt(jnp.sum(ck * ck, axis=-1, keepdims=True) + EPS)
    beta = jax.nn.sigmoid(b_col)
    g = -jnp.exp(alog) * _softplus(a_col + dtb)
    g_row = jnp.sum(jnp.where(eye, g, 0.0), axis=1, keepdims=True)
    beta_row = jnp.sum(jnp.where(eye, beta, 0.0), axis=1, keepdims=True)
    gc_col = jnp.sum(jnp.where(incl, g_row, 0.0), axis=2, keepdims=True)
    gc_row = jnp.sum(jnp.where(incl_t, g, 0.0), axis=1, keepdims=True)
    dec = jnp.where(incl, jnp.exp(jnp.where(incl, gc_col - gc_row, 0.0)), 0.0)
    kk = dot(kn, kn, "NT", _GDN_PASSES["qk"])
    qk = dot(qn, kn, "NT", _GDN_PASSES["qk"])
    lmat = jnp.where(strict, dec * kk * beta_row, 0.0)
    attn = dec * qk * beta_row
    gam = jnp.exp(gc_col)
    gc_last = gc_col[:, C - 1:C, :]
    k_end = kn * (jnp.exp(gc_last - gc_col) * beta)
    return lmat, cv, gam * kn, gam * qn, attn, k_end, jnp.exp(gc_last)


def _tri_inv(lmat):
    C = lmat.shape[1]
    eye = _tri_masks(C)[0]
    ps = _GDN_PASSES["inv"]
    p = jnp.where(eye, 1.0, 0.0) - lmat
    lp = _bdot_raw(lmat, lmat, "NN", ps)
    n = int(math.log2(C))
    for s in range(1, n):
        p = p + _bdot_raw(p, lp, "NN", ps)
        if s < n - 1:
            lp = _bdot_raw(lp, lp, "NN", ps)
    return p


def _gated_norm(o, z, gnw):
    on = o * lax.rsqrt(jnp.mean(o * o, axis=-1, keepdims=True) + EPS) * gnw
    return on * _silu(z)


GDN_PG = 4
GDN_SG = 4


def _gdn_pairs(c, ba, gp, G):
    C, W, H = GDN_CHUNK, GDN_WIDTH, GDN_HEADS
    pairs = [(j, h) for j in range(G) for h in range(H)]
    cq, ck, cv = (jnp.stack([c[C * j:C * (j + 1), o + GDN_DIM * h:o + GDN_DIM * (h + 1)] for j, h in pairs]) for o in (0, W, 2 * W))
    b_col = jnp.stack([ba[C * j:C * (j + 1), h:h + 1] for j, h in pairs])
    a_col = jnp.stack([ba[C * j:C * (j + 1), H + h:H + h + 1] for j, h in pairs])
    alog = jnp.stack([gp[0:1, h:h + 1] for j, h in pairs])
    dtb = jnp.stack([gp[0:1, H + h:H + h + 1] for j, h in pairs])
    return pairs, (cq, ck, cv, b_col, a_col, alog, dtb)


def _gdn_pre_specs(S, G):
    C = GDN_CHUNK
    T = C * G
    return dict(
        cur=pl.BlockSpec((T, 3 * GDN_WIDTH), lambda i: (i, 0)),
        prev=pl.BlockSpec((8, 3 * GDN_WIDTH), lambda i: (jnp.maximum(i * (T // 8) - 1, 0), 0)),
        ba=pl.BlockSpec((T, 128), lambda i: (i, P_BA // 128)),
        cw=pl.BlockSpec((GDN_CONV, 3 * GDN_WIDTH), lambda i: (0, 0)),
        vec=pl.BlockSpec((1, 128), lambda i: (0, 0)),
        hd=pl.BlockSpec((GDN_HEADS, T, GDN_DIM), lambda i: (0, i, 0)),
        hc=pl.BlockSpec((GDN_HEADS, T, C), lambda i: (0, i, 0)),
        ge=pl.BlockSpec((G, GDN_HEADS, 8, 128), lambda i: (i, 0, 0, 0)),
    )


def _hd_shape(S, last=GDN_DIM):
    return jax.ShapeDtypeStruct((GDN_HEADS, S, last), F32)


def _gdn_pre(proj, conv_w, gp):
    S = proj.shape[0]
    C, G = GDN_CHUNK, GDN_PG
    nc = S // C
    sp = _gdn_pre_specs(S, G)

    def body(cur_ref, prev_ref, ba_ref, cw_ref, gp_ref, uv_ref, wk_ref, qd_ref, ke_ref, at_ref, ti_ref, ge_ref):
        prev = prev_ref[...] * jnp.where(pl.program_id(0) == 0, 0.0, 1.0)
        c = _silu(_conv_taps(jnp.concatenate([prev, cur_ref[...]], axis=0), cw_ref[...], GDN_CONV, C * G))
        pairs, args = _gdn_pairs(c, ba_ref[...], gp_ref[...], G)
        lmat, v, rk, q_dec, attn, k_end, g_end = _gdn_stage1(*args)
        t = _tri_inv(lmat)
        u_v = _bdot_raw(t, v, "NN", _GDN_PASSES["sol"])
        w_k = _bdot_raw(t, rk, "NN", _GDN_PASSES["sol"])
        for b, (j, h) in enumerate(pairs):
            rows = slice(C * j, C * (j + 1))
            uv_ref[h, rows, :] = u_v[b]
            wk_ref[h, rows, :] = w_k[b]
            qd_ref[h, rows, :] = q_dec[b]
            ke_ref[h, rows, :] = k_end[b]
            at_ref[h, rows, :] = attn[b]
            ti_ref[h, rows, :] = t[b]
            ge_ref[j, h] = jnp.broadcast_to(g_end[b], (8, 128))

    return pl.pallas_call(
        body, name="gdn_pre", grid=(nc // G,),
        in_specs=[sp["cur"], sp["prev"], sp["ba"], sp["cw"], sp["vec"]],
        out_specs=(sp["hd"], sp["hd"], sp["hd"], sp["hd"], sp["hc"], sp["hc"], sp["ge"]),
        out_shape=(_hd_shape(S), _hd_shape(S), _hd_shape(S), _hd_shape(S), _hd_shape(S, C), _hd_shape(S, C),
                   jax.ShapeDtypeStruct((nc, GDN_HEADS, 8, 128), F32)),
        compiler_params=_cparams(("parallel",)),
    )(proj, proj, proj, conv_w, gp)


def _gdn_scan_specs(S, G, rev):
    C = GDN_CHUNK
    T = C * G
    n = S // T
    ci = (lambda i: n - 1 - i) if rev else (lambda i: i)
    return dict(
        hd=pl.BlockSpec((GDN_HEADS, T, GDN_DIM), lambda i: (0, ci(i), 0)),
        hc=pl.BlockSpec((GDN_HEADS, T, C), lambda i: (0, ci(i), 0)),
        ge=pl.BlockSpec((G, GDN_HEADS, 8, 128), lambda i: (ci(i), 0, 0, 0)),
        z=pl.BlockSpec((T, GDN_WIDTH), lambda i: (ci(i), P_Z // GDN_WIDTH)),
        oa=pl.BlockSpec((T, GDN_WIDTH), lambda i: (ci(i), 0)),
        vec=pl.BlockSpec((1, 128), lambda i: (0, 0)),
        st=pl.BlockSpec((G, GDN_HEADS, GDN_DIM, GDN_DIM), lambda i: (ci(i), 0, 0, 0)),
    )


def _gdn_scan(u_v, w_k, q_dec, k_end, attn, g_end, proj, gnw, mix, after=()):
    S = proj.shape[0]
    C, G = GDN_CHUNK, GDN_SG
    nc = S // C
    sp = _gdn_scan_specs(S, G, False)
    ps = _GDN_PASSES["scan"]

    def body(uv_ref, wk_ref, qd_ref, ke_ref, at_ref, ge_ref, z_ref, gnw_ref, *rest):
        oa_ref, st_ref, s_scr = rest[-3:]

        @pl.when(pl.program_id(0) == 0)
        def _():
            s_scr[...] = jnp.zeros_like(s_scr)

        for j in range(G):
            rows = slice(C * j, C * (j + 1))
            st = s_scr[...]
            st_ref[j] = st
            u = uv_ref[:, rows, :] - _bdot_raw(wk_ref[:, rows, :], st, "NN", ps)
            o = _bdot_raw(qd_ref[:, rows, :], st, "NN", ps) + _bdot_raw(at_ref[:, rows, :], u, "NN", ps)
            s_scr[...] = ge_ref[j][:, 0:1, 0:1] * st + _bdot_raw(ke_ref[:, rows, :], u, "TN", ps)
            for h in range(GDN_HEADS):
                cols = slice(GDN_DIM * h, GDN_DIM * (h + 1))
                oa_ref[rows, cols] = _gated_norm(o[h], z_ref[rows, cols], gnw_ref[...])

    return pl.pallas_call(
        body, name="gdn_scan", grid=(nc // G,),
        in_specs=[sp["hd"], sp["hd"], sp["hd"], sp["hd"], sp["hc"], sp["ge"], sp["z"], sp["vec"]] + [_ANY] * (1 + len(after)),
        out_specs=(sp["oa"], sp["st"]),
        out_shape=(jax.ShapeDtypeStruct(mix.shape, F32),
                   jax.ShapeDtypeStruct((nc, GDN_HEADS, GDN_DIM, GDN_DIM), F32)),
        input_output_aliases={8: 0},
        scratch_shapes=[pltpu.VMEM((GDN_HEADS, GDN_DIM, GDN_DIM), F32)],
        compiler_params=_cparams(("arbitrary",)),
    )(u_v, w_k, q_dec, k_end, attn, g_end, proj, gnw, mix, *after)


def _gdn_scan_bwd(u_v, w_k, q_dec, k_end, attn, g_end, proj, gnw, states, d_oa):
    S = proj.shape[0]
    C, G = GDN_CHUNK, GDN_SG
    nc = S // C
    sp = _gdn_scan_specs(S, G, True)
    ps, pb = _GDN_PASSES["scan"], _GDN_PASSES["bwd"]

    def body(uv_ref, wk_ref, qd_ref, ke_ref, at_ref, ge_ref, z_ref, gnw_ref, st_ref, doa_ref,
             duv_ref, dwk_ref, dqd_ref, dke_ref, dat_ref, dge_ref, dz_ref, dgnw_ref, ds_scr):
        @pl.when(pl.program_id(0) == 0)
        def _():
            ds_scr[...] = jnp.zeros_like(ds_scr)
            dgnw_ref[...] = jnp.zeros_like(dgnw_ref)

        dgnw = jnp.zeros((1, 128), F32)
        for j in reversed(range(G)):
            rows = slice(C * j, C * (j + 1))
            st = st_ref[j]
            wk, qd, ke, at = wk_ref[:, rows, :], qd_ref[:, rows, :], ke_ref[:, rows, :], at_ref[:, rows, :]
            u = uv_ref[:, rows, :] - _bdot_raw(wk, st, "NN", ps)
            o = _bdot_raw(qd, st, "NN", ps) + _bdot_raw(at, u, "NN", ps)
            dos = []
            for h in range(GDN_HEADS):
                cols = slice(GDN_DIM * h, GDN_DIM * (h + 1))
                _, vjp2 = jax.vjp(_gated_norm, o[h], z_ref[rows, cols], gnw_ref[...])
                do_h, dz_h, dgn = vjp2(doa_ref[rows, cols])
                dz_ref[rows, cols] = dz_h
                dgnw = dgnw + dgn
                dos.append(do_h)
            do = jnp.stack(dos)
            ds_new = ds_scr[...]
            du = _bdot_raw(at, do, "TN", pb) + _bdot_raw(ke, ds_new, "NN", pb)
            duv_ref[:, rows, :] = du
            dat_ref[:, rows, :] = _bdot_raw(do, u, "NT", pb)
            dqd_ref[:, rows, :] = _bdot_raw(do, st, "NT", pb)
            dke_ref[:, rows, :] = _bdot_raw(u, ds_new, "NT", pb)
            dwk_ref[:, rows, :] = -_bdot_raw(du, st, "NT", pb)
            d_ge = jnp.sum(jnp.sum(st * ds_new, axis=2, keepdims=True), axis=1, keepdims=True)
            dge_ref[j] = jnp.broadcast_to(d_ge, (GDN_HEADS, 8, 128))
            ds_scr[...] = ge_ref[j][:, 0:1, 0:1] * ds_new + _bdot_raw(qd, do, "TN", pb) - _bdot_raw(wk, du, "TN", pb)
        dgnw_ref[...] += dgnw

    return pl.pallas_call(
        body, name="gdn_scan_bwd", grid=(nc // G,),
        in_specs=[sp["hd"], sp["hd"], sp["hd"], sp["hd"], sp["hc"], sp["ge"], sp["z"], sp["vec"], sp["st"], sp["oa"]],
        out_specs=(sp["hd"], sp["hd"], sp["hd"], sp["hd"], sp["hc"], sp["ge"], sp["oa"], sp["vec"]),
        out_shape=(_hd_shape(S), _hd_shape(S), _hd_shape(S), _hd_shape(S), _hd_shape(S, C),
                   jax.ShapeDtypeStruct((nc, GDN_HEADS, 8, 128), F32), jax.ShapeDtypeStruct((S, GDN_WIDTH), F32),
                   jax.ShapeDtypeStruct((1, 128), F32)),
        scratch_shapes=[pltpu.VMEM((GDN_HEADS, GDN_DIM, GDN_DIM), F32)],
        compiler_params=_cparams(("arbitrary",)),
    )(u_v, w_k, q_dec, k_end, attn, g_end, proj, gnw, states, d_oa)


def _gdn_post(proj, conv_w, gp, tinv, u_v, w_k, d_uv, d_wk, d_qd, d_ke, d_at, d_ge):
    S = proj.shape[0]
    C, G = GDN_CHUNK, GDN_PG
    nc = S // C
    sp = _gdn_pre_specs(S, G)
    pb = _GDN_PASSES["bwd"]

    def body(cur_ref, prev_ref, ba_ref, cw_ref, gp_ref, ti_ref, uv_ref, wk_ref, duv_ref, dwk_ref, dqd_ref, dke_ref,
             dat_ref, dge_ref, dpre_ref, dba_ref, dgp_ref):
        i = pl.program_id(0)

        @pl.when(i == 0)
        def _():
            dgp_ref[...] = jnp.zeros_like(dgp_ref)

        prev = prev_ref[...] * jnp.where(i == 0, 0.0, 1.0)
        pre = _conv_taps(jnp.concatenate([prev, cur_ref[...]], axis=0), cw_ref[...], GDN_CONV, C * G)
        sg = jax.nn.sigmoid(pre)
        dsilu = sg * (1.0 + pre * (1.0 - sg))
        pairs, args = _gdn_pairs(pre * sg, ba_ref[...], gp_ref[...], G)
        _, vjp1 = jax.vjp(functools.partial(_gdn_stage1, dot=_bdot), *args)

        def take(ref):
            return jnp.stack([ref[h, C * j:C * (j + 1), :] for j, h in pairs])

        t, u_v, w_k = take(ti_ref), take(uv_ref), take(wk_ref)
        d_v = _bdot_raw(t, take(duv_ref), "TN", pb)
        d_rk = _bdot_raw(t, take(dwk_ref), "TN", pb)
        d_l = -(_bdot_raw(d_v, u_v, "NT", pb) + _bdot_raw(d_rk, w_k, "NT", pb))
        d_ge = jnp.stack([dge_ref[j, h][0:1, 0:1] for j, h in pairs])
        dcq, dck, dcv, db, da, dalog, ddtb = vjp1((d_l, d_v, d_rk, take(dqd_ref), take(dat_ref), take(dke_ref), d_ge))
        lane = lax.broadcasted_iota(jnp.int32, (C, 128), 1)
        lane1 = lax.broadcasted_iota(jnp.int32, (1, 128), 1)
        dgp = jnp.zeros((1, 128), F32)
        for j in range(G):
            rows = slice(C * j, C * (j + 1))
            dba = jnp.zeros((C, 128), F32)
            for h in range(GDN_HEADS):
                b = GDN_HEADS * j + h
                for o_, dcx in ((0, dcq), (GDN_WIDTH, dck), (2 * GDN_WIDTH, dcv)):
                    cols = slice(o_ + GDN_DIM * h, o_ + GDN_DIM * (h + 1))
                    dpre_ref[rows, cols] = dcx[b] * dsilu[rows, cols]
                dba = dba + jnp.where(lane == h, db[b], 0.0) + jnp.where(lane == GDN_HEADS + h, da[b], 0.0)
                dgp = dgp + jnp.where(lane1 == h, dalog[b], 0.0) + jnp.where(lane1 == GDN_HEADS + h, ddtb[b], 0.0)
            dba_ref[rows, :] = dba
        dgp_ref[0:1, :] += dgp

    T = C * G
    return pl.pallas_call(
        body, name="gdn_post", grid=(nc // G,),
        in_specs=[sp["cur"], sp["prev"], sp["ba"], sp["cw"], sp["vec"], sp["hc"], sp["hd"], sp["hd"], sp["hd"], sp["hd"],
                  sp["hd"], sp["hd"], sp["hc"], sp["ge"]],
        out_specs=(sp["cur"], pl.BlockSpec((T, 128), lambda i: (i, 0)), pl.BlockSpec((8, 128), lambda i: (0, 0))),
        out_shape=(jax.ShapeDtypeStruct((S, 3 * GDN_WIDTH), F32), jax.ShapeDtypeStruct((S, 128), F32),
                   jax.ShapeDtypeStruct((8, 128), F32)),
        compiler_params=_cparams(("arbitrary",)),
    )(proj, proj, proj, conv_w, gp, tinv, u_v, w_k, d_uv, d_wk, d_qd, d_ke, d_at, d_ge)


def _conv_bwd(dpre, x, xcol0, w, K, name, tc):
    S, Cc = dpre.shape
    T = _pick_tile(S, 256)
    nt, ncol = S // T, Cc // tc
    xo = xcol0 // tc

    def body(d_ref, dn_ref, x_ref, xp_ref, w_ref, dx_ref, dw_ref):
        i = pl.program_id(1)
        dn = dn_ref[...] * jnp.where(i == nt - 1, 0.0, 1.0)
        dv = d_ref[...]
        ext_d = jnp.concatenate([dv, dn], axis=0)
        dx_ref[...] = _conv_taps_t(ext_d, w_ref[...], K, T).astype(dx_ref.dtype)
        xp = xp_ref[...] * jnp.where(i == 0, 0.0, 1.0)
        ext_x = jnp.concatenate([xp, x_ref[...]], axis=0)

        @pl.when(i == 0)
        def _():
            dw_ref[...] = jnp.zeros_like(dw_ref)

        for k in range(K):
            dw_ref[k:k + 1, :] += jnp.sum(dv * _shifted(ext_x, (K - 1) - k, 8, T), axis=0, keepdims=True)

    r8 = T // 8
    return pl.pallas_call(
        body, name=name, grid=(ncol, nt),
        in_specs=[pl.BlockSpec((T, tc), lambda j, i: (i, j)),
                  pl.BlockSpec((8, tc), lambda j, i: (jnp.minimum((i + 1) * r8, S // 8 - 1), j)),
                  pl.BlockSpec((T, tc), lambda j, i: (i, j + xo)),
                  pl.BlockSpec((8, tc), lambda j, i: (jnp.maximum(i * r8 - 1, 0), j + xo)),
                  pl.BlockSpec((K, tc), lambda j, i: (0, j))],
        out_specs=(pl.BlockSpec((T, tc), lambda j, i: (i, j)), pl.BlockSpec((K, tc), lambda j, i: (0, j))),
        out_shape=(jax.ShapeDtypeStruct((S, Cc), _MXU), jax.ShapeDtypeStruct((K, Cc), F32)),
        compiler_params=_cparams(("parallel", "arbitrary")),
    )(dpre, dpre, x, x, w)


def _dil_bias(nt, T):
    d = (np.arange(nt)[:, None, None] * T + np.arange(T)[None, None, :] - np.arange(T)[None, :, None])
    cnt = ((d >= 0) & (d <= 128)).astype(np.float64) + ((d >= 0) & (d % 4 == 0) & (d <= 512)) + ((d >= 0) & (d % 16 == 0))
    return jnp.asarray(np.where(cnt > 0, np.log(np.maximum(cnt, 1.0)), -1e30), dtype=F32)


def _attn_fwd(proj, after=()):
    S = proj.shape[0]
    T = min(ATT_T, S)
    nt = S // T
    bias = _dil_bias(nt, T)
    scale = DIL_DIM ** -0.5
    npair = DIL_WIDTH // 128
    qb0, kb0, vb0 = P_QKVB // 128, (P_QKVB + DIL_WIDTH) // 128, (P_QKVB + 2 * DIL_WIDTH) // 128

    def body(q_ref, k_ref, v_ref, b_ref, *rest):
        o_ref, lse_ref = rest[-2:]
        i = pl.program_id(1)
        qs = (q_ref[...] * scale).astype(_MXU)

        def step(j, carry):
            kt = k_ref[pl.ds(pl.multiple_of(j * T, T), T), :].astype(_MXU)
            vt = v_ref[pl.ds(pl.multiple_of(j * T, T), T), :].astype(_MXU)
            bt = b_ref[i - j]
            out = []
            for hh in range(2):
                m, l, acc = carry[hh]
                sl = slice(hh * DIL_DIM, (hh + 1) * DIL_DIM)
                s = lax.dot_general(kt[:, sl], qs[:, sl], (_NT, ((), ())), preferred_element_type=F32) + bt
                m_new = jnp.maximum(m, jnp.max(s, axis=0, keepdims=True))
                p = jnp.exp(s - m_new)
                a = jnp.exp(m - m_new)
                l = a * l + jnp.sum(p, axis=0, keepdims=True)
                acc = a * acc + lax.dot_general(vt[:, sl], p.astype(_MXU), (_TN, ((), ())), preferred_element_type=F32)
                out.append((m_new, l, acc))
            return tuple(out)

        init = tuple((jnp.full((1, T), -1e30, F32), jnp.zeros((1, T), F32), jnp.zeros((DIL_DIM, T), F32)) for _ in range(2))
        res = lax.fori_loop(0, i + 1, step, init)
        lse_ref[...] = jnp.zeros_like(lse_ref)
        for hh in range(2):
            m, l, acc = res[hh]
            o_ref[:, hh * DIL_DIM:(hh + 1) * DIL_DIM] = (acc / l).T
            lse_ref[hh:hh + 1, :] = m + jnp.log(l)

    return pl.pallas_call(
        body, name="attn_fwd", grid=(npair, nt),
        in_specs=[pl.BlockSpec((T, 128), lambda p, i: (i, qb0 + p)),
                  pl.BlockSpec((S, 128), lambda p, i: (0, kb0 + p)),
                  pl.BlockSpec((S, 128), lambda p, i: (0, vb0 + p)),
                  pl.BlockSpec((nt, T, T), lambda p, i: (0, 0, 0))] + [_ANY] * len(after),
        out_specs=(pl.BlockSpec((T, 128), lambda p, i: (i, GDN_WIDTH // 128 + p)),
                   pl.BlockSpec((None, None, 8, T), lambda p, i: (p, i, 0, 0))),
        out_shape=(jax.ShapeDtypeStruct((S, GDN_WIDTH + DIL_WIDTH), F32), jax.ShapeDtypeStruct((npair, nt, 8, T), F32)),
        compiler_params=_cparams(("parallel", "parallel")),
    )(proj, proj, proj, bias, *after)


def _attn_bwd(proj, mix, lse, d_mix):
    S = proj.shape[0]
    T = min(ATT_T, S)
    nt = S // T
    bias = _dil_bias(nt, T)
    scale = DIL_DIM ** -0.5
    npair = DIL_WIDTH // 128
    qb0, kb0, vb0 = P_QKVB // 128, (P_QKVB + DIL_WIDTH) // 128, (P_QKVB + 2 * DIL_WIDTH) // 128

    def body(q_ref, k_ref, v_ref, o_ref, lse_ref, do_ref, b_ref, dq_ref, dk_ref, dv_ref, dq_scr):
        j = pl.program_id(1)

        @pl.when(j == 0)
        def _():
            dq_scr[...] = jnp.zeros_like(dq_scr)

        kt = k_ref[...].astype(_MXU)
        vt = v_ref[...].astype(_MXU)
        ones = jnp.ones((8, DIL_DIM), F32)

        def step(i, carry):
            rows = pl.ds(pl.multiple_of(i * T, T), T)
            qs = (q_ref[rows, :] * scale).astype(_MXU)
            dov = do_ref[rows, :]
            prod = dov * o_ref[rows, :]
            lsev = lse_ref[i]
            dob = dov.astype(_MXU)
            bt = b_ref[i - j]
            out = []
            dqs = []
            for hh in range(2):
                dk, dv = carry[hh]
                sl = slice(hh * DIL_DIM, (hh + 1) * DIL_DIM)
                s = lax.dot_general(kt[:, sl], qs[:, sl], (_NT, ((), ())), preferred_element_type=F32) + bt
                p = jnp.exp(s - lsev[hh:hh + 1, :])
                delta = lax.dot_general(ones, prod[:, sl], (_NT, ((), ())), precision=_HI, preferred_element_type=F32)[0:1, :]
                dp = lax.dot_general(vt[:, sl], dob[:, sl], (_NT, ((), ())), preferred_element_type=F32)
                ds = (p * (dp - delta)).astype(_MXU)
                dv = dv + lax.dot_general(p.astype(_MXU), dob[:, sl], (_NN, ((), ())), preferred_element_type=F32)
                dk = dk + lax.dot_general(ds, qs[:, sl], (_NN, ((), ())), preferred_element_type=F32)
                dqs.append(lax.dot_general(ds, kt[:, sl], (_TN, ((), ())), preferred_element_type=F32) * scale)
                out.append((dk, dv))
            dq_scr[rows, :] += jnp.concatenate(dqs, axis=1)
            return tuple(out)

        init = tuple((jnp.zeros((T, DIL_DIM), F32), jnp.zeros((T, DIL_DIM), F32)) for _ in range(2))
        res = lax.fori_loop(j, nt, step, init)
        dk_ref[...] = jnp.concatenate([res[0][0], res[1][0]], axis=1).astype(dk_ref.dtype)
        dv_ref[...] = jnp.concatenate([res[0][1], res[1][1]], axis=1).astype(dv_ref.dtype)

        @pl.when(j == nt - 1)
        def _():
            dq_ref[...] = dq_scr[...].astype(dq_ref.dtype)

    full = lambda c0: pl.BlockSpec((S, 128), lambda p, j: (0, c0 + p))
    tile = lambda c0: pl.BlockSpec((T, 128), lambda p, j: (j, c0 + p))
    out3 = jax.ShapeDtypeStruct((S, DIL_WIDTH), _MXU)
    return pl.pallas_call(
        body, name="attn_bwd", grid=(npair, nt),
        in_specs=[full(qb0), tile(kb0), tile(vb0), full(GDN_WIDTH // 128),
                  pl.BlockSpec((None, nt, 8, T), lambda p, j: (p, 0, 0, 0)), full(GDN_WIDTH // 128),
                  pl.BlockSpec((nt, T, T), lambda p, j: (0, 0, 0))],
        out_specs=(full(0), tile(0), tile(0)),
        out_shape=(out3, out3, out3),
        scratch_shapes=[pltpu.VMEM((S, 128), F32)],
        compiler_params=_cparams(("parallel", "arbitrary")),
    )(proj, proj, proj, mix, lse, d_mix, bias)


def _ffn_act(up, cw):
    S, Cc = up.shape[0], up.shape[1] // 2
    T, tc = _pick_tile(S, 256), _pick_tile(Cc, 1536)
    r16 = T // 16
    nct = Cc // tc

    def body(g_ref, gp_ref, u_ref, up_ref, wg_ref, wu_ref, o_ref):
        keep = jnp.where(pl.program_id(1) == 0, 0.0, 1.0)
        cg = _conv_taps(jnp.concatenate([gp_ref[8:16, :].astype(F32) * keep, g_ref[...].astype(F32)], axis=0),
                        wg_ref[...], FFN_CONV, T)
        cu = _conv_taps(jnp.concatenate([up_ref[8:16, :].astype(F32) * keep, u_ref[...].astype(F32)], axis=0),
                        wu_ref[...], FFN_CONV, T)
        o_ref[...] = (_silu(cg) * cu).astype(o_ref.dtype)

    cur = lambda o: pl.BlockSpec((T, tc), lambda j, i: (i, j + o))
    prev = lambda o: pl.BlockSpec((16, tc), lambda j, i: (jnp.maximum(i * r16 - 1, 0), j + o))
    wsp = lambda o: pl.BlockSpec((FFN_CONV, tc), lambda j, i: (0, j + o))
    return pl.pallas_call(
        body, name="ffn_act", grid=(nct, S // T),
        in_specs=[cur(0), prev(0), cur(nct), prev(nct), wsp(0), wsp(nct)], out_specs=cur(0),
        out_shape=jax.ShapeDtypeStruct((S, Cc), _MXU),
        compiler_params=_cparams(("parallel", "parallel")),
    )(up, up, up, up, cw, cw)


def _ffn_act_bwd(d_act, up, cw):
    S, Cc = up.shape[0], up.shape[1] // 2
    T, tc = _pick_tile(S, 256), _pick_tile(Cc, 1536)
    r8, r16 = T // 8, T // 16
    nt = S // T
    nct = Cc // tc
    K = FFN_CONV

    def body(da_ref, dan_ref, g_ref, gp_ref, gn_ref, u_ref, up_ref, un_ref, wg_ref, wu_ref,
             dg_ref, du_ref, dwg_ref, dwu_ref):
        i = pl.program_id(1)
        keep_p = jnp.where(i == 0, 0.0, 1.0)
        keep_n = jnp.where(i == nt - 1, 0.0, 1.0)
        wg, wu = wg_ref[...], wu_ref[...]
        xg = jnp.concatenate([gp_ref[8:16, :].astype(F32) * keep_p, g_ref[...].astype(F32),
                              gn_ref[0:8, :].astype(F32) * keep_n], axis=0)
        xu = jnp.concatenate([up_ref[8:16, :].astype(F32) * keep_p, u_ref[...].astype(F32),
                              un_ref[0:8, :].astype(F32) * keep_n], axis=0)
        cg = _conv_taps(xg, wg, K, T + 8)
        cu = _conv_taps(xu, wu, K, T + 8)
        da = jnp.concatenate([da_ref[...], dan_ref[...] * keep_n], axis=0)
        sg = jax.nn.sigmoid(cg)
        d_cg = da * cu * (sg * (1.0 + cg * (1.0 - sg)))
        d_cu = da * (cg * sg)
        dg_ref[...] = _conv_taps_t(d_cg, wg, K, T).astype(dg_ref.dtype)
        du_ref[...] = _conv_taps_t(d_cu, wu, K, T).astype(du_ref.dtype)

        @pl.when(i == 0)
        def _():
            dwg_ref[...] = jnp.zeros_like(dwg_ref)
            dwu_ref[...] = jnp.zeros_like(dwu_ref)

        for k in range(K):
            dwg_ref[k:k + 1, :] += jnp.sum(d_cg[0:T, :] * _shifted(xg, (K - 1) - k, 8, T), axis=0, keepdims=True)
            dwu_ref[k:k + 1, :] += jnp.sum(d_cu[0:T, :] * _shifted(xu, (K - 1) - k, 8, T), axis=0, keepdims=True)

    cur = lambda o: pl.BlockSpec((T, tc), lambda j, i: (i, j + o))
    prev = lambda o: pl.BlockSpec((16, tc), lambda j, i: (jnp.maximum(i * r16 - 1, 0), j + o))
    nxt = lambda o: pl.BlockSpec((16, tc), lambda j, i: (jnp.minimum((i + 1) * r16, S // 16 - 1), j + o))
    nxt8 = pl.BlockSpec((8, tc), lambda j, i: (jnp.minimum((i + 1) * r8, S // 8 - 1), j))
    wsp = lambda o: pl.BlockSpec((K, tc), lambda j, i: (0, j + o))
    return pl.pallas_call(
        body, name="ffn_act_bwd", grid=(nct, nt),
        in_specs=[cur(0), nxt8, cur(0), prev(0), nxt(0), cur(nct), prev(nct), nxt(nct), wsp(0), wsp(nct)],
        out_specs=(cur(0), cur(0), wsp(0), wsp(0)),
        out_shape=(jax.ShapeDtypeStruct((S, Cc), _MXU), jax.ShapeDtypeStruct((S, Cc), _MXU),
                   jax.ShapeDtypeStruct((K, Cc), F32), jax.ShapeDtypeStruct((K, Cc), F32)),
        compiler_params=_cparams(("parallel", "arbitrary")),
    )(d_act, d_act, up, up, up, up, up, up, cw, cw)


def _local_step(x, tgt, h1, n1w, n2w, fnw, gp, gnw, wp, conv_w, fcw, rest_weights, early_grads):
    proj = _mm(h1, wp, "nn", name="proj")
    u_v, w_k, q_dec, k_end, attn, tinv, g_end = _gdn_pre(proj, conv_w, gp)
    mix, lse = _attn_fwd(proj)
    mix, states = _gdn_scan(u_v, w_k, q_dec, k_end, attn, g_end, proj, gnw, mix, after=[rest_weights[0]([mix])])
    w_out, w_up4, w_down = rest_weights[1]([mix])
    x2 = _mm(mix, w_out, "nn", residual=x, name="outproj")
    h2 = _rmsnorm_fwd(x2, n2w, "norm2")
    up = _mm(h2, w_up4, "nn", b_blocks=True, out_dtype=_MXU, name="up")
    act = _ffn_act(up, fcw)
    x3 = _mm(act, w_down, "nn", residual=x2, name="down")
    loss, dx3, dx3n, d_fnw = _loss_head(x3, fnw, tgt, "loss_head")
    d_act = _mm(dx3n, w_down, "nt", name="d_act")
    d_wdown = _mm(act, dx3n, "tn", name="d_wdown")
    d_upg, d_upu, d_fcwg, d_fcwu = _ffn_act_bwd(d_act, up, fcw)
    d_wup = _mm(h2, d_upg, "tn", place=("blocks", N_CHIPS, 0), tn=w_up4.shape[2], name="d_wgate")
    d_wup = _mm(h2, d_upu, "tn", place=("blocks", N_CHIPS, N_CHIPS // 2), tn=w_up4.shape[2], into=d_wup, name="d_wup")
    token = early_grads[0](d_wup, d_wdown)
    d_h2 = _mm_nt_blocks([d_upg, d_upu], w_up4, "d_h2", after=[token])
    dx2, d_n2w = _rmsnorm_bwd(d_h2, x2, n2w, dx3, "norm2_bwd", after=[token])
    token = early_grads[1](dx2)
    d_mix = _mm(dx2, w_out, "nt", name="d_mix")
    d_wout = _mm(mix, dx2, "tn", name="d_wout")
    dq_b, dk_b, dv_b = _attn_bwd(proj, mix, lse, d_mix)
    d_uv, d_wk, d_qd, d_ke, d_at, d_ge, d_z, d_gnw = _gdn_scan_bwd(u_v, w_k, q_dec, k_end, attn, g_end, proj,
                                                                   gnw + token[0:1, 0:1], states, d_mix)
    d_pre, d_ba, d_gp = _gdn_post(proj, conv_w, gp, tinv, u_v, w_k, d_uv, d_wk, d_qd, d_ke, d_at, d_ge)
    d_qkva, d_convw = _conv_bwd(d_pre, proj, 0, conv_w, GDN_CONV, "gdn_conv_bwd", 512)
    d_proj = jnp.concatenate([d_qkva, d_z.astype(_MXU), dq_b, dk_b, dv_b, d_ba.astype(_MXU),
                              jnp.zeros((x.shape[0], P_COLS - P_BA - 128), _MXU)], axis=1)
    d_wp = _mm(h1, d_proj, "tn", name="d_wp")
    token = early_grads[2](d_wp, d_wout)
    d_h1 = _mm(d_proj, wp, "nt", name="d_h1", after=[token])
    dx, d_n1w = _rmsnorm_bwd(d_h1, x, n1w, dx2, "norm1_bwd", after=[token])
    grads = dict(wp=d_wp, conv_w=d_convw, w_out=d_wout, w_up=d_wup, fcw_g=d_fcwg, fcw_u=d_fcwu, w_down=d_wdown,
                 n1w=d_n1w, n2w=d_n2w, fnw=d_fnw, gp=d_gp, gnw=d_gnw)
    return loss, dx, grads


_HBM = pl.BlockSpec(memory_space=pltpu.HBM)


def _pos():
    return lax.axis_index("x"), lax.axis_index("y"), lax.axis_index("c")


def _other_chips(x, y):
    return [(1 - x, y), (x, 1 - y), (1 - x, 1 - y)]


def _halvable(shape):
    return shape[0] % 32 == 0


def _rows_of_half(shape, half):
    if not _halvable(shape):
        return pl.ds(0, shape[0])
    return pl.ds(pl.multiple_of(half * (shape[0] // 2), 16), shape[0] // 2)


_SEM = pl.BlockSpec(memory_space=pltpu.SEMAPHORE)
_ANY = pl.BlockSpec(memory_space=pl.ANY)
_DATAFLOW = pltpu.SideEffectType.DATAFLOW_SIDE_EFFECTING


def _in_hbm(a):
    return pltpu.with_memory_space_constraint(a, pltpu.HBM)


def _halves_copy(src_refs, land_refs, send_sems, recv_sems, shapes, a, j, block, x, y, c):
    px, py = _other_chips(x, y)[j]
    rows = _rows_of_half(shapes[a], c)
    return pltpu.make_async_remote_copy(
        src_ref=src_refs[a].at[rows, :], dst_ref=land_refs[a].at[block, rows, :], send_sem=send_sems.at[3 * a + j],
        recv_sem=recv_sems.at[3 * a + j], device_id=(px, py, c), device_id_type=MESH)


def _gather_halves_start(shards, after, name):
    n = len(shards)
    shapes = [s.shape for s in shards]

    def body(*refs):
        ins, lands = refs[:n], refs[n:2 * n]
        send_sems, recv_sems = refs[2 * n + 1], refs[2 * n + 2]
        token = refs[-1]
        x, y, c = _pos()
        q = 2 * x + y
        for a in range(n):
            for j in range(3):
                _halves_copy(ins, lands, send_sems, recv_sems, shapes, a, j, q, x, y, c).start()
        token[...] = jnp.zeros_like(token)

    land_shapes = [(N_CHIPS,) + s.shape for s in shards]
    return pl.pallas_call(
        body, name=name,
        out_shape=(pltpu.SemaphoreType.DMA((3 * n,)), pltpu.SemaphoreType.DMA((3 * n,)),
                   *[pltpu.HBM(s.shape, s.dtype) for s in shards],
                   *[pltpu.HBM(ls, s.dtype) for ls, s in zip(land_shapes, shards)],
                   jax.ShapeDtypeStruct((8, 128), F32)),
        in_specs=[_HBM] * (2 * n) + [_ANY],
        out_specs=(_SEM, _SEM, *[_HBM] * (2 * n), pl.BlockSpec(memory_space=pltpu.VMEM)),
        input_output_aliases={a: 2 + a for a in range(2 * n)},
        compiler_params=pltpu.CompilerParams(has_side_effects=_DATAFLOW),
    )(*[_in_hbm(s) for s in shards], *[_in_hbm(lax.empty(ls, s.dtype)) for ls, s in zip(land_shapes, shards)], after)


def _gather_halves_wait(started, after, name):
    send_sems, recv_sems, *thru = started
    n = len(thru) // 2
    shapes = [t.shape for t in thru[:n]]

    def body(*refs):
        ins, lands = refs[:n], refs[n:2 * n]
        send_sems, recv_sems = refs[2 * n], refs[2 * n + 1]
        x, y, c = _pos()
        q = 2 * x + y
        chips = _other_chips(x, y)
        for a in range(n):
            for j, (px, py) in enumerate(chips):
                _halves_copy(ins, lands, send_sems, recv_sems, shapes, a, j, q, x, y, c).wait_send()
                _halves_copy(ins, lands, send_sems, recv_sems, shapes, a, j, 2 * px + py, x, y, c).wait_recv()

    outs = pl.pallas_call(
        body, name=name, out_shape=[pltpu.HBM(t.shape, t.dtype) for t in thru],
        in_specs=[_HBM] * (2 * n) + [_SEM, _SEM] + [_ANY] * len(after), out_specs=[_HBM] * (2 * n),
        input_output_aliases={a: a for a in range(2 * n)},
        compiler_params=pltpu.CompilerParams(has_side_effects=_DATAFLOW),
    )(*thru, send_sems, recv_sems, *after)
    return outs[:n], outs[n:]


def _sibling_fill(gathered, name):
    big = [a for a, g in enumerate(gathered) if _halvable(g.shape[1:])]
    n = len(gathered)

    def body(*refs):
        ins, outs = refs[:n], refs[n:2 * n]
        send_sems, recv_sems = refs[2 * n:]
        x, y, c = _pos()
        chips = _other_chips(x, y)

        def copy(k, j, half):
            a = big[k]
            px, py = chips[j]
            rows = _rows_of_half(gathered[a].shape[1:], half)
            return pltpu.make_async_remote_copy(
                src_ref=ins[a].at[2 * px + py, rows, :], dst_ref=outs[a].at[2 * px + py, rows, :],
                send_sem=send_sems.at[3 * k + j], recv_sem=recv_sems.at[3 * k + j],
                device_id=(x, y, 1 - c), device_id_type=MESH)

        sends = [copy(k, j, c) for k in range(len(big)) for j in range(3)]
        for cp in sends:
            cp.start()
        for k in range(len(big)):
            for j in range(3):
                copy(k, j, 1 - c).wait_recv()
        for cp in sends:
            cp.wait_send()

    return pl.pallas_call(
        body, name=name, in_specs=[_HBM] * n, out_specs=[_HBM] * n,
        out_shape=[jax.ShapeDtypeStruct(g.shape, g.dtype) for g in gathered],
        input_output_aliases={a: a for a in range(n)},
        scratch_shapes=[pltpu.SemaphoreType.DMA((3 * len(big),)), pltpu.SemaphoreType.DMA((3 * len(big),))],
    )(*gathered)


def _fill_copy(refs, send_sems, recv_sems, shapes, a, j, half, x, y, c):
    px, py = _other_chips(x, y)[j]
    rows = _rows_of_half(shapes[a], half)
    return pltpu.make_async_remote_copy(
        src_ref=refs[a].at[2 * px + py, rows, :], dst_ref=refs[a].at[2 * px + py, rows, :],
        send_sem=send_sems.at[3 * a + j], recv_sem=recv_sems.at[3 * a + j],
        device_id=(x, y, 1 - c), device_id_type=MESH)


def _sibling_fill_start(gathered, name):
    n = len(gathered)
    shapes = [g.shape[1:] for g in gathered]

    def body(*refs):
        ins = refs[:n]
        send_sems, recv_sems = refs[n], refs[n + 1]
        token = refs[-1]
        x, y, c = _pos()
        for a in range(n):
            for j in range(3):
                _fill_copy(ins, send_sems, recv_sems, shapes, a, j, c, x, y, c).start()
        token[...] = jnp.zeros_like(token)

    return pl.pallas_call(
        body, name=name,
        out_shape=(pltpu.SemaphoreType.DMA((3 * n,)), pltpu.SemaphoreType.DMA((3 * n,)),
                   *[pltpu.HBM(g.shape, g.dtype) for g in gathered], jax.ShapeDtypeStruct((8, 128), F32)),
        in_specs=[_HBM] * n,
        out_specs=(_SEM, _SEM, *[_HBM] * n, pl.BlockSpec(memory_space=pltpu.VMEM)),
        input_output_aliases={a: 2 + a for a in range(n)},
        compiler_params=pltpu.CompilerParams(has_side_effects=_DATAFLOW),
    )(*[_in_hbm(g) for g in gathered])


def _sibling_fill_wait(started, after, name):
    send_sems, recv_sems, *thru = started
    n = len(thru)
    shapes = [t.shape[1:] for t in thru]

    def body(*refs):
        ins = refs[:n]
        send_sems, recv_sems = refs[n], refs[n + 1]
        x, y, c = _pos()
        for a in range(n):
            for j in range(3):
                _fill_copy(ins, send_sems, recv_sems, shapes, a, j, c, x, y, c).wait_send()
                _fill_copy(ins, send_sems, recv_sems, shapes, a, j, 1 - c, x, y, c).wait_recv()

    return pl.pallas_call(
        body, name=name, out_shape=[pltpu.HBM(t.shape, t.dtype) for t in thru],
        in_specs=[_HBM] * n + [_SEM, _SEM] + [_ANY] * len(after), out_specs=[_HBM] * n,
        input_output_aliases={a: a for a in range(n)},
        compiler_params=pltpu.CompilerParams(has_side_effects=_DATAFLOW),
    )(*thru, send_sems, recv_sems, *after)


def _place_own(shards, gathered, cq, name, carry=()):
    n = len(shards)
    nc = len(carry)
    steps = 4

    def body(cq_ref, *refs):
        for a in range(n):
            refs[2 * n + nc + a][...] = refs[a][...]

    def tile(shape):
        return shape[0] // steps if _halvable(shape) else shape[0]

    in_specs = [pl.BlockSpec((tile(s.shape), s.shape[1]), (lambda i, s_: (i, 0)) if _halvable(s.shape) else (lambda i, s_: (0, 0)))
                for s in shards]
    in_specs += [pl.BlockSpec(memory_space=pl.ANY)] * (n + nc)
    out_specs = [pl.BlockSpec((None, tile(s.shape), s.shape[1]),
                              (lambda i, s_: (s_[1], i, 0)) if _halvable(s.shape) else (lambda i, s_: (s_[1], 0, 0)))
                 for s in shards]
    out_specs += [pl.BlockSpec(memory_space=pl.ANY)] * nc
    gs = pltpu.PrefetchScalarGridSpec(num_scalar_prefetch=1, grid=(steps,), in_specs=in_specs, out_specs=out_specs)
    outs = pl.pallas_call(
        body, name=name, grid_spec=gs,
        out_shape=[jax.ShapeDtypeStruct(g.shape, g.dtype) for g in gathered] + [jax.ShapeDtypeStruct(t.shape, t.dtype) for t in carry],
        input_output_aliases={1 + n + a: a for a in range(n + nc)},
        compiler_params=_cparams(("arbitrary",)),
    )(cq, *shards, *gathered, *carry)
    return (outs[:n], outs[n:]) if nc else outs


def _half_rows(ref, c, rh):
    return ref.at[:, pl.ds(pl.multiple_of(c * rh, 8), rh), :]


def _chips_copy(src_refs, land_refs, send_sems, recv_sems, a, j, x, y, c):
    px, py = _other_chips(x, y)[j]
    return pltpu.make_async_remote_copy(src_ref=src_refs[a].at[2 * px + py], dst_ref=land_refs[a].at[j],
                                        send_sem=send_sems.at[3 * a + j], recv_sem=recv_sems.at[3 * a + j],
                                        device_id=(px, py, c), device_id_type=MESH)


def _grad_chips_start(parts, name):
    n = len(parts)

    def body(*refs):
        ins, lands = refs[:n], refs[n:2 * n]
        send_sems, recv_sems = refs[2 * n], refs[2 * n + 1]
        token = refs[-1]
        x, y, c = _pos()
        for a in range(n):
            for j in range(3):
                _chips_copy(ins, lands, send_sems, recv_sems, a, j, x, y, c).start()
        token[...] = jnp.zeros_like(token)

    land_shapes = [(3,) + p.shape[1:] for p in parts]
    return pl.pallas_call(
        body, name=name,
        out_shape=(pltpu.SemaphoreType.DMA((3 * n,)), pltpu.SemaphoreType.DMA((3 * n,)),
                   *[pltpu.HBM(p.shape, p.dtype) for p in parts],
                   *[pltpu.HBM(ls, p.dtype) for ls, p in zip(land_shapes, parts)],
                   jax.ShapeDtypeStruct((8, 128), F32)),
        in_specs=[_HBM] * (2 * n),
        out_specs=(_SEM, _SEM, *[_HBM] * (2 * n), pl.BlockSpec(memory_space=pltpu.VMEM)),
        input_output_aliases={a: 2 + a for a in range(2 * n)},
        compiler_params=pltpu.CompilerParams(has_side_effects=_DATAFLOW),
    )(*[_in_hbm(p) for p in parts], *[_in_hbm(lax.empty(ls, p.dtype)) for ls, p in zip(land_shapes, parts)])


def _grad_chips_wait(started, after, name):
    send_sems, recv_sems, *thru = started
    n = len(thru) // 2

    def body(*refs):
        ins, lands = refs[:n], refs[n:2 * n]
        send_sems, recv_sems = refs[2 * n], refs[2 * n + 1]
        x, y, c = _pos()
        for a in range(n):
            for j in range(3):
                cp = _chips_copy(ins, lands, send_sems, recv_sems, a, j, x, y, c)
                cp.wait_send()
                cp.wait_recv()

    outs = pl.pallas_call(
        body, name=name, out_shape=[pltpu.HBM(t.shape, t.dtype) for t in thru],
        in_specs=[_HBM] * (2 * n) + [_SEM, _SEM] + [_ANY] * len(after), out_specs=[_HBM] * (2 * n),
        input_output_aliases={a: a for a in range(2 * n)},
        compiler_params=pltpu.CompilerParams(has_side_effects=_DATAFLOW),
    )(*thru, send_sems, recv_sems, *after)
    return outs[n:]


def _sibling_copy(src_refs, land_refs, send_sems, recv_sems, rhs, a, c, x, y):
    return pltpu.make_async_remote_copy(src_ref=_half_rows(src_refs[a], 1 - c, rhs[a]), dst_ref=land_refs[a],
                                        send_sem=send_sems.at[a], recv_sem=recv_sems.at[a],
                                        device_id=(x, y, 1 - c), device_id_type=MESH)


def _grad_sibling_start(fams, name):
    n = len(fams)
    rhs = [f.shape[1] // 2 for f in fams]

    def body(*refs):
        ins, lands = refs[:n], refs[n:2 * n]
        send_sems, recv_sems = refs[2 * n], refs[2 * n + 1]
        token = refs[-1]
        x, y, c = _pos()
        for a in range(n):
            _sibling_copy(ins, lands, send_sems, recv_sems, rhs, a, c, x, y).start()
        token[...] = jnp.zeros_like(token)

    land_shapes = [(f.shape[0], f.shape[1] // 2, f.shape[2]) for f in fams]
    return pl.pallas_call(
        body, name=name,
        out_shape=(pltpu.SemaphoreType.DMA((n,)), pltpu.SemaphoreType.DMA((n,)),
                   *[pltpu.HBM(f.shape, f.dtype) for f in fams],
                   *[pltpu.HBM(ls, f.dtype) for ls, f in zip(land_shapes, fams)],
                   jax.ShapeDtypeStruct((8, 128), F32)),
        in_specs=[_HBM] * (2 * n),
        out_specs=(_SEM, _SEM, *[_HBM] * (2 * n), pl.BlockSpec(memory_space=pltpu.VMEM)),
        input_output_aliases={a: 2 + a for a in range(2 * n)},
        compiler_params=pltpu.CompilerParams(has_side_effects=_DATAFLOW),
    )(*[_in_hbm(f) for f in fams], *[_in_hbm(lax.empty(ls, f.dtype)) for ls, f in zip(land_shapes, fams)])


def _grad_sibling_wait(started, after, name):
    send_sems, recv_sems, *thru = started
    n = len(thru) // 2
    rhs = [t.shape[1] // 2 for t in thru[:n]]

    def body(*refs):
        ins, lands = refs[:n], refs[n:2 * n]
        send_sems, recv_sems = refs[2 * n], refs[2 * n + 1]
        x, y, c = _pos()
        for a in range(n):
            cp = _sibling_copy(ins, lands, send_sems, recv_sems, rhs, a, c, x, y)
            cp.wait_send()
            cp.wait_recv()

    outs = pl.pallas_call(
        body, name=name, out_shape=[pltpu.HBM(t.shape, t.dtype) for t in thru],
        in_specs=[_HBM] * (2 * n) + [_SEM, _SEM] + [_ANY] * len(after), out_specs=[_HBM] * (2 * n),
        input_output_aliases={a: a for a in range(2 * n)},
        compiler_params=pltpu.CompilerParams(has_side_effects=_DATAFLOW),
    )(*thru, send_sems, recv_sems, *after)
    return outs[:n], outs[n:]


def _grad_share(fulls, name, small=None):
    n = len(fulls)
    ns = 0 if small is None else 1
    rhs = [f.shape[0] // 2 for f in fulls]

    def body(*refs):
        ins, outs = refs[:n], refs[n + ns:2 * n + ns]
        send_sems, recv_sems = refs[2 * (n + ns)], refs[2 * (n + ns) + 1]
        x, y, c = _pos()

        def copy(a, half):
            rows = pl.ds(pl.multiple_of(half * rhs[a], 8), rhs[a])
            return pltpu.make_async_remote_copy(src_ref=ins[a].at[rows, :], dst_ref=outs[a].at[rows, :],
                                                send_sem=send_sems.at[7 * ns + a], recv_sem=recv_sems.at[7 * ns + a],
                                                device_id=(x, y, 1 - c), device_id_type=MESH)

        sends = [copy(a, c) for a in range(n)]
        for cp in sends:
            cp.start()
        if ns:
            small_ref, all_ref = refs[n], refs[2 * n + 1]
            me = 4 * x + 2 * y + c

            def peer(r):
                dx, dy, dc = (r >> 2) & 1, (r >> 1) & 1, r & 1
                return (x if dx == 0 else 1 - x), (y if dy == 0 else 1 - y), (c if dc == 0 else 1 - c)

            def small_copy(r, slot):
                return pltpu.make_async_remote_copy(src_ref=small_ref, dst_ref=all_ref.at[slot], send_sem=send_sems.at[r - 1],
                                                    recv_sem=recv_sems.at[r - 1], device_id=peer(r), device_id_type=MESH)

            smalls = [small_copy(r, me) for r in range(1, 8)]
            for cp in smalls:
                cp.start()
            for r in range(1, 8):
                px, py, pc = peer(r)
                small_copy(r, 4 * px + 2 * py + pc).wait_recv()
            sends = sends + smalls
        for a in range(n):
            copy(a, 1 - c).wait_recv()
        for cp in sends:
            cp.wait_send()

    return pl.pallas_call(
        body, name=name, in_specs=[_HBM] * (n + ns), out_specs=[_HBM] * (n + ns),
        out_shape=[jax.ShapeDtypeStruct(f.shape, f.dtype) for f in fulls]
        + ([jax.ShapeDtypeStruct((8,) + small.shape, small.dtype)] if ns else []),
        input_output_aliases={a: a for a in range(n)},
        scratch_shapes=[pltpu.SemaphoreType.DMA((7 * ns + n,)), pltpu.SemaphoreType.DMA((7 * ns + n,))],
    )(*fulls, *([small] if ns else []))


def _add_sibling(own, recv, cq, name):
    nb, R, Cc = own.shape
    Rh = R // 2

    def body(cq_ref, a_ref, b_ref, o32_ref, o16_ref):
        s = a_ref[0] + b_ref[0]
        mine = pl.program_id(0) == cq_ref[1]

        @pl.when(mine)
        def _():
            o32_ref[...] = s

        @pl.when(jnp.logical_not(mine))
        def _():
            o16_ref[0] = s.astype(o16_ref.dtype)

    sp = pl.BlockSpec((1, Rh, Cc), lambda b, s: (b, 0, 0))
    gs = pltpu.PrefetchScalarGridSpec(
        num_scalar_prefetch=1, grid=(nb,),
        in_specs=[pl.BlockSpec((1, Rh, Cc), lambda b, s: (b, s[0], 0)), sp],
        out_specs=[pl.BlockSpec((Rh, Cc), lambda b, s: (0, 0)), sp])
    return pl.pallas_call(
        body, name=name, grid_spec=gs,
        out_shape=[jax.ShapeDtypeStruct((Rh, Cc), F32), jax.ShapeDtypeStruct((nb, Rh, Cc), _MXU)],
        compiler_params=_cparams(("arbitrary",)),
    )(cq, own, recv)


def _add_sibling_split(d_wp, recv, cq, name):
    _, Dm, Pc = d_wp.shape
    Rh = Dm // 2
    Wb = IN_COLS // N_CHIPS
    T = 256

    def body(cq_ref, a_ref, b_ref, o32_ref, o16_ref):
        s = a_ref[0] + b_ref[0]
        blocks = [s[:, 0:Wb], s[:, Wb:2 * Wb],
                  jnp.concatenate([s[:, 2 * Wb:P_QKVB], s[:, P_BA:P_BA + 8], s[:, P_QKVB:3 * Wb - 8]], axis=1),
                  s[:, 3 * Wb - 8:P_BA]]
        q = cq_ref[1]
        own = None
        for j, blk in enumerate(blocks):
            term = jnp.where(q == j, blk, 0.0)
            own = term if own is None else own + term
            o16_ref[j] = blk.astype(o16_ref.dtype)
        o32_ref[...] = own

    gs = pltpu.PrefetchScalarGridSpec(
        num_scalar_prefetch=1, grid=(Rh // T,),
        in_specs=[pl.BlockSpec((1, T, Pc), lambda i, s: (0, s[0] * (Rh // T) + i, 0)), pl.BlockSpec((1, T, Pc), lambda i, s: (0, i, 0))],
        out_specs=[pl.BlockSpec((T, Wb), lambda i, s: (i, 0)), pl.BlockSpec((N_CHIPS, T, Wb), lambda i, s: (0, i, 0))])
    return pl.pallas_call(
        body, name=name, grid_spec=gs,
        out_shape=[jax.ShapeDtypeStruct((Rh, Wb), F32), jax.ShapeDtypeStruct((N_CHIPS, Rh, Wb), _MXU)],
        compiler_params=_cparams(("parallel",)),
    )(cq, d_wp, recv)


def _add_chips(part32, recv3, cq, name):
    Rh, Cc = part32.shape

    def body(cq_ref, a_ref, b_ref, o_ref):
        acc = a_ref[...]
        for j in range(3):
            acc = acc + b_ref[j].astype(F32)
        o_ref[...] = acc

    gs = pltpu.PrefetchScalarGridSpec(
        num_scalar_prefetch=1, grid=(1,),
        in_specs=[pl.BlockSpec((Rh, Cc), lambda i, s: (0, 0)), pl.BlockSpec((3, Rh, Cc), lambda i, s: (0, 0, 0))],
        out_specs=pl.BlockSpec((Rh, Cc), lambda i, s: (s[0], 0)))
    return pl.pallas_call(
        body, name=name, grid_spec=gs, out_shape=jax.ShapeDtypeStruct((2 * Rh, Cc), F32),
        compiler_params=_cparams(("arbitrary",)),
    )(cq, part32, recv3)


def _transposed(g):
    Dm, n = g.shape
    pad = -n % 128

    def body(g_ref, o_ref):
        xp = jnp.concatenate([g_ref[...], jnp.zeros((Dm, pad), F32)], axis=1)
        o_ref[...] = xp.T[:n, :]

    return pl.pallas_call(body, name="transposed", out_shape=jax.ShapeDtypeStruct((n, Dm), F32),
                          compiler_params=_cparams(vmem=V7X_VMEM_LIMIT))(g)


def _adamw(w, g, m, v, name):
    R, Cc = w.shape
    T = max([t for t in range(8, 257, 8) if R % t == 0], default=R)

    def body(w_ref, g_ref, m_ref, v_ref, d_ref, mo_ref, vo_ref):
        d_ref[...], mo_ref[...], vo_ref[...] = _adamw_math(w_ref[...], g_ref[...], m_ref[...], v_ref[...])

    sp = pl.BlockSpec((T, Cc), lambda i: (i, 0))
    sh = jax.ShapeDtypeStruct((R, Cc), F32)
    return pl.pallas_call(
        body, name=name, grid=(R // T,), in_specs=[sp] * 4, out_specs=(sp, sp, sp), out_shape=(sh, sh, sh),
        compiler_params=_cparams(("parallel",)),
    )(w, g, m, v)


SMALL_ROWS = 32
ROW_CONV, ROW_FCG, ROW_FCU = 5, 13, 22


def _adamw_math(w, g, m, v):
    mn = ADAM_B1 * m + (1.0 - ADAM_B1) * g
    vn = ADAM_B2 * v + (1.0 - ADAM_B2) * (g * g)
    c1 = 1.0 / (1.0 - ADAM_B1 ** ADAM_STEP)
    c2 = 1.0 / (1.0 - ADAM_B2 ** ADAM_STEP)
    return -ADAM_LR * ((mn * c1) / (jnp.sqrt(vn * c2) + ADAM_EPS) + ADAM_WD * w), mn, vn


def _pack_small(n1, n2, fn, gp, gn, conv, fcg, fcu, loss):
    W = D_MODEL

    def body(n1_ref, n2_ref, fn_ref, gp_ref, gn_ref, conv_ref, fcg_ref, fcu_ref, loss_ref, o_ref):
        o_ref[...] = jnp.zeros_like(o_ref)
        o_ref[0:1, :] = n1_ref[...]
        o_ref[1:2, :] = n2_ref[...]
        o_ref[2:3, :] = fn_ref[...]
        o_ref[3:4, 0:8] = gp_ref[0:1, 0:8]
        o_ref[3:4, 8:9] = loss_ref[0:1, 0:1]
        o_ref[4:5, 0:128] = gn_ref[...]
        for i in range(GDN_CONV):
            o_ref[ROW_CONV + 2 * i:ROW_CONV + 2 * i + 1, :] = conv_ref[i:i + 1, 0:W]
            o_ref[ROW_CONV + 2 * i + 1:ROW_CONV + 2 * i + 2, 0:3 * GDN_WIDTH - W] = conv_ref[i:i + 1, W:3 * GDN_WIDTH]
        for r0, ref in ((ROW_FCG, fcg_ref), (ROW_FCU, fcu_ref)):
            for i in range(FFN_CONV):
                for k in range(3):
                    n = min(W, D_FF - k * W)
                    o_ref[r0 + 3 * i + k:r0 + 3 * i + k + 1, 0:n] = ref[i:i + 1, k * W:k * W + n]

    return pl.pallas_call(body, name="pack_small", out_shape=jax.ShapeDtypeStruct((SMALL_ROWS, W), F32))(
        n1, n2, fn, gp, gn, conv, fcg, fcu, loss)


def _small_step(meq, small_all, small, ws, ms, vs):
    W = D_MODEL
    n = len(ws)
    cw, fw = ws[6].shape[1], ws[7].shape[1]

    def body(meq_ref, all_ref, own_ref, *refs):
        w_refs, m_refs, v_refs = refs[:n], refs[n:2 * n], refs[2 * n:3 * n]
        loss_ref = refs[3 * n]
        outs = refs[3 * n + 1:]
        me, q = meq_ref[0], meq_ref[1]
        red = None
        for d in range(8):
            term = jnp.where(me == d, own_ref[...], all_ref[d])
            red = term if red is None else red + term
        loss_ref[...] = jnp.broadcast_to(red[3:4, 8:9], loss_ref.shape)
        conv = [jnp.concatenate([red[ROW_CONV + 2 * i:ROW_CONV + 2 * i + 1, :],
                                 red[ROW_CONV + 2 * i + 1:ROW_CONV + 2 * i + 2, 0:3 * GDN_WIDTH - W]], axis=1)
                for i in range(GDN_CONV)]
        conv = jnp.concatenate(conv, axis=0)

        def fc_rows(r0):
            rows = [jnp.concatenate([red[r0 + 3 * i + k:r0 + 3 * i + k + 1, 0:min(W, D_FF - k * W)] for k in range(3)], axis=1)
                    for i in range(FFN_CONV)]
            return jnp.concatenate(rows, axis=0)

        fc = jnp.concatenate([fc_rows(ROW_FCG), fc_rows(ROW_FCU)], axis=1)

        def chip_block(full, width):
            out = None
            for j in range(N_CHIPS):
                term = jnp.where(q == j, full[:, width * j:width * (j + 1)], 0.0)
                out = term if out is None else out + term
            return out

        grads = [red[0:1, :], red[1:2, :], red[2:3, :], red[3:4, 0:4], red[3:4, 4:8], red[4:5, 0:128],
                 chip_block(conv, cw), chip_block(fc, fw)]
        for k in range(n):
            d_, m_, v_ = _adamw_math(w_refs[k][...], grads[k], m_refs[k][...], v_refs[k][...])
            outs[4 * k][...] = grads[k]
            outs[4 * k + 1][...] = d_
            outs[4 * k + 2][...] = m_
            outs[4 * k + 3][...] = v_

    full = lambda a: pl.BlockSpec(a.shape, lambda i, s_, nd=len(a.shape): (0,) * nd)
    arrays = [small_all, small, *ws, *ms, *vs]
    out_shapes = [jax.ShapeDtypeStruct((8, 128), F32)] + [jax.ShapeDtypeStruct(w.shape, F32) for w in ws for _ in range(4)]
    gs = pltpu.PrefetchScalarGridSpec(
        num_scalar_prefetch=1, grid=(1,), in_specs=[full(a) for a in arrays],
        out_specs=[pl.BlockSpec(o.shape, lambda i, s_, nd=len(o.shape): (0,) * nd) for o in out_shapes])
    return pl.pallas_call(body, name="small_step", grid_spec=gs, out_shape=out_shapes)(meq, *arrays)


def _pad_lanes(v, n=D_MODEL):
    return jnp.pad(v, ((0, 0), (0, n - v.shape[1])))


def kernel(x, norm1_w, w_in, conv_qkv_w, a_log, dt_bias, gdn_norm_w, w_out, norm2_w, w_up, ffn_conv_w, w_down, final_norm_w, loss_target, m_norm1_w, m_w_in, m_conv_qkv_w, m_a_log, m_dt_bias, m_gdn_norm_w, m_w_out, m_norm2_w, m_w_up, m_ffn_conv_w, m_w_down, m_final_norm_w, v_norm1_w, v_w_in, v_conv_qkv_w, v_a_log, v_dt_bias, v_gdn_norm_w, v_w_out, v_norm2_w, v_w_up, v_ffn_conv_w, v_w_down, v_final_norm_w):
    c = lax.axis_index("c")
    q = 2 * lax.axis_index("x") + lax.axis_index("y")
    S = x.shape[1]
    cq = jnp.stack([c, q]).astype(jnp.int32)

    *in_started, in_token = _gather_halves_start([w_in[0].astype(_MXU), conv_qkv_w[0], ffn_conv_w[0]], x, "gather_in_start")
    w_in_l, m_w_in_l, v_w_in_l = (jnp.swapaxes(a + in_token[0:1, 0:1], 1, 2)[0] for a in (w_in, m_w_in, v_w_in))
    h1 = _rmsnorm_fwd(x[0], norm1_w, "norm1", after=[in_token])
    rest = [(a[0] + in_token[0:1, 0:1]).astype(_MXU) for a in (w_out, w_up, w_down)]
    in_shards, got_in = _gather_halves_wait(in_started, [w_in_l, m_w_in_l, v_w_in_l, h1, *rest], "gather_in_wait")
    (g_in, g_conv, g_fconv), (w_in_l, m_w_in_l, v_w_in_l) = _place_own(
        in_shards, _sibling_fill(got_in, "fill_in"), cq, "place_in", carry=[w_in_l, m_w_in_l, v_w_in_l])
    *rest_started, token = _gather_halves_start(rest, g_conv, "gather_rest_start")

    rest_state = {}

    def rest_arrived(after):
        rest_state["shards"], got = _gather_halves_wait(rest_started, after, "gather_rest_wait")
        *rest_state["fill"], tok = _sibling_fill_start(got, "fill_rest_start")
        return tok

    def rest_filled(after):
        got = _sibling_fill_wait(rest_state["fill"], after, "fill_rest_wait")
        g_out, g_up, g_down = _place_own(rest_state["shards"], got, cq, "place_rest")
        return g_out.reshape(D_MODEL, D_MODEL), g_up, g_down.reshape(D_FF, D_MODEL)

    rest_weights = (rest_arrived, rest_filled)
    wp = _wp_assemble(g_in, [token])
    conv_f = jnp.concatenate([g_conv[i] for i in range(N_CHIPS)], axis=1)
    fcw = jnp.concatenate([g_fconv[i] for i in range(N_CHIPS)], axis=1)
    gp = _pad_lanes(jnp.concatenate([a_log, dt_bias], axis=1), 128)
    fnw = final_norm_w[None, :]
    early = {}

    def early_sibling(d_wup, d_wdown):
        *early["sibling"], tok = _grad_sibling_start([d_wup, d_wdown.reshape(N_CHIPS, D_FF // N_CHIPS, D_MODEL)],
                                                     "grad_sibling_early_start")
        return tok

    def early_chips(dx2):
        fams_e, got_e = _grad_sibling_wait(early["sibling"], [dx2], "grad_sibling_early_wait")
        early["parts"] = [_add_sibling(f, r, cq, "add_sibling_" + nm) for f, r, nm in zip(fams_e, got_e, ("w_up", "w_down"))]
        *early["started"], tok = _grad_chips_start([p[1] for p in early["parts"]], "grad_chips_start")
        return tok

    def late_sibling(d_wp, d_wout):
        *early["late_sibling"], tok = _grad_sibling_start(
            [d_wp[None], d_wout.reshape(N_CHIPS, D_MODEL // N_CHIPS, D_MODEL)], "grad_sibling_late_start")
        return tok

    early_grads = (early_sibling, early_chips, late_sibling)

    loss_l, dx, g = _local_step(x[0], loss_target[0], h1, norm1_w, norm2_w, fnw, gp, gdn_norm_w, wp,
                                conv_f, fcw, rest_weights, early_grads)
    small = _pack_small(g["n1w"], g["n2w"], g["fnw"], g["gp"], g["gnw"], g["conv_w"], g["fcw_g"], g["fcw_u"], loss_l)
    fams, got = _grad_sibling_wait(early["late_sibling"], [dx], "grad_sibling_late_wait")
    parts = [_add_sibling_split(fams[0], got[0], cq, "add_sibling_w_in"), _add_sibling(fams[1], got[1], cq, "add_sibling_w_out")]
    *late_started, late_token = _grad_chips_start([p[1] for p in parts], "grad_chips_late_start")
    got3_e = _grad_chips_wait(early["started"], [dx, late_token], "grad_chips_wait")
    g_w_up, g_w_down = _grad_share(
        [_add_chips(p[0], r3, cq, "add_chips_" + nm) for p, r3, nm in zip(early["parts"], got3_e, ("w_up", "w_down"))],
        "grad_share_early")
    big = {}

    def adamw_big(nm, w, gg, m, v):
        d_, m_, v_ = _adamw(w[0], gg, m[0], v[0], "adamw_" + nm)
        big[nm] = (gg[None], d_[None], m_[None], v_[None])

    adamw_big("w_up", w_up, g_w_up, m_w_up, v_w_up)
    adamw_big("w_down", w_down, g_w_down, m_w_down, v_w_down)
    got3 = _grad_chips_wait(late_started, [big["w_up"][1], big["w_down"][1]], "grad_chips_late_wait")
    g_w_in, g_w_out, small_all = _grad_share(
        [_add_chips(p[0], r3, cq, "add_chips_" + nm) for p, r3, nm in zip(parts, got3, ("w_in", "w_out"))],
        "grad_share_late", small)
    g_t = _transposed(g_w_in)
    d_t, m_t, v_t = _adamw(w_in_l, g_t, m_w_in_l, v_w_in_l, "adamw_w_in")
    big["w_in"] = tuple(jnp.swapaxes(t[None], 1, 2) for t in (g_t, d_t, m_t, v_t))
    adamw_big("w_out", w_out, g_w_out, m_w_out, v_w_out)
    small_names = ["norm1_w", "norm2_w", "final_norm_w", "a_log", "dt_bias", "gdn_norm_w", "conv_qkv_w", "ffn_conv_w"]
    loss_b, *small_out = _small_step(
        jnp.stack([2 * q + c, q]).astype(jnp.int32), small_all, small,
        [norm1_w, norm2_w, final_norm_w[None], a_log, dt_bias, gdn_norm_w, conv_qkv_w[0], ffn_conv_w[0]],
        [m_norm1_w, m_norm2_w, m_final_norm_w[None], m_a_log, m_dt_bias, m_gdn_norm_w, m_conv_qkv_w[0], m_ffn_conv_w[0]],
        [v_norm1_w, v_norm2_w, v_final_norm_w[None], v_a_log, v_dt_bias, v_gdn_norm_w, v_conv_qkv_w[0], v_ffn_conv_w[0]])
    like = dict(final_norm_w=lambda t: t[0], conv_qkv_w=lambda t: t[None], ffn_conv_w=lambda t: t[None])
    for k, nm in enumerate(small_names):
        big[nm] = tuple(like.get(nm, lambda t: t)(t) for t in small_out[4 * k:4 * k + 4])
    names = ["norm1_w", "w_in", "conv_qkv_w", "a_log", "dt_bias", "gdn_norm_w", "w_out", "norm2_w", "w_up",
             "ffn_conv_w", "w_down", "final_norm_w"]
    return (loss_b[0, 0], dx[None], *[big[n][0] for n in names], *[big[n][1] for n in names],
            *[big[n][2] for n in names], *[big[n][3] for n in names])
```

```python
import functools
import math

import numpy as np
import jax
import jax.numpy as jnp
from jax import lax
from jax.experimental import pallas as pl
from jax.experimental.pallas import tpu as pltpu

F32 = jnp.float32
BF16 = jnp.bfloat16
_MXU = jnp.bfloat16
_HI = lax.Precision.HIGHEST
EPS = 1e-6
V7X_VMEM_LIMIT = 56 * 1024 * 1024
MESH = pl.DeviceIdType.MESH

D_MODEL = 1024
GDN_HEADS, GDN_DIM, GDN_CHUNK, GDN_CONV = 4, 128, 64, 4
GDN_WIDTH = GDN_HEADS * GDN_DIM
DIL_HEADS, DIL_DIM = 8, 64
DIL_WIDTH = DIL_HEADS * DIL_DIM
D_FF, FFN_CONV = 2816, 3
IN_COLS = 3592
P_COLS = 3840
P_Z, P_QKVB, P_BA = 1536, 2048, 3584
ATT_T = 1024
ADAM_LR, ADAM_B1, ADAM_B2, ADAM_EPS, ADAM_WD, ADAM_STEP = 0.001, 0.9, 0.999, 1e-08, 0.01, 10
N_CHIPS = 4


def _cparams(sem=None, vmem=None):
    kw = {}
    if sem is not None:
        kw["dimension_semantics"] = sem
    if vmem is not None:
        kw["vmem_limit_bytes"] = vmem
    return pltpu.CompilerParams(**kw)


def _silu(x):
    return x * jax.nn.sigmoid(x)


def _pick_tile(n, cap):
    best = None
    for t in range(128, min(n, cap) + 1, 128):
        if n % t == 0:
            best = t
    return best or n


def _mm(a, b, mode, *, out_dtype=F32, residual=None, name, b_blocks=False, place=None, into=None, tn=None, after=()):
    if mode == "nn":
        M, K = a.shape
        N = b.shape[0] * b.shape[2] if b_blocks else b.shape[1]
    elif mode == "nt":
        (M, K), (N, _) = a.shape, b.shape
    else:
        (K, M), (_, N) = a.shape, b.shape
    tm = _pick_tile(M, 1024)
    tn = b.shape[2] if b_blocks else (tn or _pick_tile(N, 1536))

    def vmem(tm, tn):
        return 2 * (tm * K * a.dtype.itemsize + tn * K * b.dtype.itemsize
                    + tm * tn * (jnp.dtype(out_dtype).itemsize + (4 if residual is not None else 0))) + 3 * tm * tn * 4

    fixed_tn = b_blocks or (place is not None and place[0] == "blocks")
    while vmem(tm, tn) > 40 * 1024 * 1024:
        if (tm >= tn or fixed_tn) and tm % 256 == 0:
            tm //= 2
        elif tn % 256 == 0 and not fixed_tn:
            tn //= 2
        else:
            tm //= 2
    a_spec = pl.BlockSpec((K, tm), lambda j, i: (0, i)) if mode == "tn" else pl.BlockSpec((tm, K), lambda j, i: (i, 0))
    if b_blocks:
        b_spec = pl.BlockSpec((None, K, tn), lambda j, i: (j, 0, 0))
    else:
        b_spec = pl.BlockSpec((tn, K), lambda j, i: (j, 0)) if mode == "nt" else pl.BlockSpec((K, tn), lambda j, i: (0, j))
    r_spec = pl.BlockSpec((tm, tn), lambda j, i: (i, j))
    if place is None:
        o_spec, o_shape = r_spec, (M, N)
    elif place[0] == "rows":
        off = place[2] // tm
        o_spec, o_shape = pl.BlockSpec((tm, tn), lambda j, i: (i + off, j)), (place[1], N)
    else:
        off = place[2]
        o_spec, o_shape = pl.BlockSpec((None, tm, tn), lambda j, i: (j + off, i, 0)), (place[1], M, tn)
    dims = {"nn": (((1,), (0,)), ((), ())), "nt": (((1,), (1,)), ((), ())), "tn": (((0,), (0,)), ((), ()))}[mode]

    def body(*refs):
        a_ref, b_ref = refs[0], refs[1]
        o_ref = refs[-1]
        acc = lax.dot_general(a_ref[...].astype(_MXU), b_ref[...].astype(_MXU), dims, preferred_element_type=F32)
        if residual is not None:
            acc = acc + refs[2][...]
        o_ref[...] = acc.astype(out_dtype)

    ins, specs, alias = [a, b], [a_spec, b_spec], {}
    if residual is not None:
        ins.append(residual)
        specs.append(r_spec)
    if into is not None:
        alias = {len(ins): 0}
        ins.append(into)
        specs.append(pl.BlockSpec(memory_space=pl.ANY))
    ins += list(after)
    specs += [pl.BlockSpec(memory_space=pl.ANY)] * len(after)
    return pl.pallas_call(
        body, name=name, grid=(N // tn, M // tm), in_specs=specs, out_specs=o_spec,
        out_shape=jax.ShapeDtypeStruct(o_shape, out_dtype), input_output_aliases=alias,
        compiler_params=_cparams(("parallel", "parallel"), V7X_VMEM_LIMIT),
    )(*ins)


def _mm_nt_blocks(a_list, b4, name, after=()):
    M = a_list[0].shape[0]
    nb, N, Kb = b4.shape
    tm, tn = _pick_tile(M, 1024), _pick_tile(N, 512)

    def body(a0_ref, a1_ref, b_ref, *rest):
        o_ref = rest[-1]
        acc = None
        for blk in range(nb):
            a_ref = (a0_ref, a1_ref)[blk // 2]
            lo = (blk % 2) * Kb
            t = lax.dot_general(a_ref[:, lo:lo + Kb].astype(_MXU), b_ref[blk].astype(_MXU), (((1,), (1,)), ((), ())),
                                preferred_element_type=F32)
            acc = t if acc is None else acc + t
        o_ref[...] = acc

    a_spec = pl.BlockSpec((tm, 2 * Kb), lambda j, i: (i, 0))
    return pl.pallas_call(
        body, name=name, grid=(N // tn, M // tm),
        in_specs=[a_spec, a_spec, pl.BlockSpec((nb, tn, Kb), lambda j, i: (0, j, 0))]
        + [pl.BlockSpec(memory_space=pl.ANY)] * len(after),
        out_specs=pl.BlockSpec((tm, tn), lambda j, i: (i, j)), out_shape=jax.ShapeDtypeStruct((M, N), F32),
        compiler_params=_cparams(("parallel", "parallel"), V7X_VMEM_LIMIT),
    )(a_list[0], a_list[1], b4, *after)


def _wp_assemble(g_in, after=()):
    nb, Dm, Wb = g_in.shape
    T = 256
    n_lo = P_QKVB - 2 * Wb

    def body(g_ref, *rest):
        g2 = g_ref[2]
        rest[-1][...] = jnp.concatenate(
            [g_ref[0], g_ref[1], g2[:, :n_lo], g2[:, n_lo + 8:], g_ref[3], g2[:, n_lo:n_lo + 8],
             jnp.zeros((T, P_COLS - P_BA - 8), g_in.dtype)], axis=1)

    return pl.pallas_call(
        body, name="wp_assemble", grid=(Dm // T,),
        in_specs=[pl.BlockSpec((nb, T, Wb), lambda i: (0, i, 0))] + [pl.BlockSpec(memory_space=pl.ANY)] * len(after),
        out_specs=pl.BlockSpec((T, P_COLS), lambda i: (i, 0)), out_shape=jax.ShapeDtypeStruct((Dm, P_COLS), g_in.dtype),
        compiler_params=_cparams(("parallel",)),
    )(g_in, *after)


def _rmsnorm_fwd(x, w, name, after=()):
    S, D = x.shape
    T = _pick_tile(S, 512)

    def body(x_ref, w_ref, *rest):
        xv = x_ref[...]
        rs = lax.rsqrt(jnp.mean(xv * xv, axis=-1, keepdims=True) + EPS)
        rest[-1][...] = (xv * rs * w_ref[...]).astype(rest[-1].dtype)

    return pl.pallas_call(
        body, name=name, grid=(S // T,),
        in_specs=[pl.BlockSpec((T, D), lambda i: (i, 0)), pl.BlockSpec((1, D), lambda i: (0, 0))] + [_ANY] * len(after),
        out_specs=pl.BlockSpec((T, D), lambda i: (i, 0)),
        out_shape=jax.ShapeDtypeStruct((S, D), _MXU),
        compiler_params=_cparams(("parallel",)),
    )(x, w, *after)


def _rmsnorm_bwd(dh, x, w, dres, name, after=()):
    S, D = x.shape
    T = _pick_tile(S, 512)

    def body(dh_ref, x_ref, w_ref, dres_ref, *rest):
        dx_ref, dw_ref = rest[-2:]
        xv = x_ref[...]
        rs = lax.rsqrt(jnp.mean(xv * xv, axis=-1, keepdims=True) + EPS)
        xn = xv * rs
        dhv = dh_ref[...]
        dxn = dhv * w_ref[...]
        dx_ref[...] = dres_ref[...] + rs * (dxn - xn * jnp.mean(dxn * xn, axis=-1, keepdims=True))

        @pl.when(pl.program_id(0) == 0)
        def _():
            dw_ref[...] = jnp.zeros_like(dw_ref)

        dw_ref[...] += jnp.sum(dhv * xn, axis=0, keepdims=True)

    row = pl.BlockSpec((T, D), lambda i: (i, 0))
    vec = pl.BlockSpec((1, D), lambda i: (0, 0))
    return pl.pallas_call(
        body, name=name, grid=(S // T,), in_specs=[row, row, vec, row] + [_ANY] * len(after), out_specs=(row, vec),
        out_shape=(jax.ShapeDtypeStruct((S, D), F32), jax.ShapeDtypeStruct((1, D), F32)),
        compiler_params=_cparams(("arbitrary",)),
    )(dh, x, w, dres, *after)


def _loss_head(x3, w, tgt, name):
    S, D = x3.shape
    T = _pick_tile(S, 512)

    def body(x_ref, w_ref, t_ref, loss_ref, dx_ref, dxn_ref, dw_ref):
        xv = x_ref[...]
        rs = lax.rsqrt(jnp.mean(xv * xv, axis=-1, keepdims=True) + EPS)
        xn = xv * rs
        err = xn * w_ref[...] - t_ref[...]
        dy = err * (1.0 / D)
        dxn = dy * w_ref[...]
        dxv = rs * (dxn - xn * jnp.mean(dxn * xn, axis=-1, keepdims=True))
        dx_ref[...] = dxv
        dxn_ref[...] = dxv.astype(dxn_ref.dtype)

        @pl.when(pl.program_id(0) == 0)
        def _():
            dw_ref[...] = jnp.zeros_like(dw_ref)
            loss_ref[...] = jnp.zeros_like(loss_ref)

        dw_ref[...] += jnp.sum(dy * xn, axis=0, keepdims=True)
        part = jnp.sum(jnp.sum(err * err, axis=-1, keepdims=True), axis=0, keepdims=True) * (0.5 / D)
        loss_ref[...] += jnp.broadcast_to(part, loss_ref.shape)

    row = pl.BlockSpec((T, D), lambda i: (i, 0))
    vec = pl.BlockSpec((1, D), lambda i: (0, 0))
    return pl.pallas_call(
        body, name=name, grid=(S // T,), in_specs=[row, vec, row],
        out_specs=(pl.BlockSpec((8, 128), lambda i: (0, 0)), row, row, vec),
        out_shape=(jax.ShapeDtypeStruct((8, 128), F32), jax.ShapeDtypeStruct((S, D), F32), jax.ShapeDtypeStruct((S, D), _MXU),
                   jax.ShapeDtypeStruct((1, D), F32)),
        compiler_params=_cparams(("arbitrary",)),
    )(x3, w, tgt)


def _shifted(ext, back, lo, n):
    if back == 0:
        return ext[lo:lo + n, :]
    return pltpu.roll(ext, back % ext.shape[0], 0)[lo:lo + n, :]


def _conv_windows(ext, K, T):
    return [_shifted(ext, (K - 1) - i, 8, T) for i in range(K)]


def _conv_taps(ext, w, K, T):
    out = None
    for i, win in enumerate(_conv_windows(ext, K, T)):
        term = win * w[i:i + 1, :]
        out = term if out is None else out + term
    return out


def _conv_taps_t(ext, w, K, T):
    out = None
    for i in range(K):
        term = _shifted(ext, i - (K - 1), 0, T) * w[i:i + 1, :]
        out = term if out is None else out + term
    return out


def _tri_masks(C):
    r = lax.broadcasted_iota(jnp.int32, (C, C), 0)
    c = lax.broadcasted_iota(jnp.int32, (C, C), 1)
    return r == c, r >= c, r > c, r <= c


_NN, _NT, _TN = ((1,), (0,)), ((1,), (1,)), ((0,), (0,))
_GDN_PASSES = dict(qk=1, inv=1, sol=1, scan=1, bwd=1)


def _bdot_raw(a, b, kind, passes):
    dims = ({"NN": ((2,), (1,)), "NT": ((2,), (2,)), "TN": ((1,), (1,))}[kind], ((0,), (0,)))
    if passes == 0:
        return lax.dot_general(a, b, dims, precision=_HI, preferred_element_type=F32)
    ah, bh = a.astype(BF16), b.astype(BF16)
    out = lax.dot_general(ah, bh, dims, preferred_element_type=F32)
    if passes == 3:
        al, bl = (a - ah.astype(F32)).astype(BF16), (b - bh.astype(F32)).astype(BF16)
        out = out + lax.dot_general(ah, bl, dims, preferred_element_type=F32) + lax.dot_general(al, bh, dims, preferred_element_type=F32)
    return out


@functools.partial(jax.custom_vjp, nondiff_argnums=(2, 3))
def _bdot(a, b, kind, passes):
    return _bdot_raw(a, b, kind, passes)


def _bdot_fwd(a, b, kind, passes):
    return _bdot_raw(a, b, kind, passes), (a, b)


def _bdot_bwd(kind, passes, res, ct):
    a, b = res
    if kind == "NN":
        return _bdot_raw(ct, b, "NT", passes), _bdot_raw(a, ct, "TN", passes)
    if kind == "NT":
        return _bdot_raw(ct, b, "NN", passes), _bdot_raw(ct, a, "TN", passes)
    return _bdot_raw(b, ct, "NT", passes), _bdot_raw(a, ct, "NN", passes)


_bdot.defvjp(_bdot_fwd, _bdot_bwd)


def _softplus(x):
    return jnp.maximum(x, 0.0) + jnp.log(1.0 + jnp.exp(-jnp.abs(x)))


def _gdn_stage1(cq, ck, cv, b_col, a_col, alog, dtb, dot=_bdot_raw):
    C = cq.shape[1]
    eye, incl, strict, incl_t = _tri_masks(C)
    qn = cq * lax.rsqrt(jnp.sum(cq * cq, axis=-1, keepdims=True) + EPS) * (GDN_DIM ** -0.5)
    kn = ck * lax.rsqrt(jnp.sum(ck * ck, axis=-1, keepdims=True) + EPS)
    beta = jax.nn.sigmoid(b_col)
    g = -jnp.exp(alog) * _softplus(a_col + dtb)
    g_row = jnp.sum(jnp.where(eye, g, 0.0), axis=1, keepdims=True)
    beta_row = jnp.sum(jnp.where(eye, beta, 0.0), axis=1, keepdims=True)
    gc_col = jnp.sum(jnp.where(incl, g_row, 0.0), axis=2, keepdims=True)
    gc_row = jnp.sum(jnp.where(incl_t, g, 0.0), axis=1, keepdims=True)
    dec = jnp.where(incl, jnp.exp(jnp.where(incl, gc_col - gc_row, 0.0)), 0.0)
    kk = dot(kn, kn, "NT", _GDN_PASSES["qk"])
    qk = dot(qn, kn, "NT", _GDN_PASSES["qk"])
    lmat = jnp.where(strict, dec * kk * beta_row, 0.0)
    attn = dec * qk * beta_row
    gam = jnp.exp(gc_col)
    gc_last = gc_col[:, C - 1:C, :]
    k_end = kn * (jnp.exp(gc_last - gc_col) * beta)
    return lmat, cv, gam * kn, gam * qn, attn, k_end, jnp.exp(gc_last)


def _tri_inv(lmat):
    C = lmat.shape[1]
    eye = _tri_masks(C)[0]
    ps = _GDN_PASSES["inv"]
    p = jnp.where(eye, 1.0, 0.0) - lmat
    lp = _bdot_raw(lmat, lmat, "NN", ps)
    n = int(math.log2(C))
    for s in range(1, n):
        p = p + _bdot_raw(p, lp, "NN", ps)
        if s < n - 1:
            lp = _bdot_raw(lp, lp, "NN", ps)
    return p


def _gated_norm(o, z, gnw):
    on = o * lax.rsqrt(jnp.mean(o * o, axis=-1, keepdims=True) + EPS) * gnw
    return on * _silu(z)


GDN_PG = 4
GDN_SG = 4


def _gdn_pairs(c, ba, gp, G):
    C, W, H = GDN_CHUNK, GDN_WIDTH, GDN_HEADS
    pairs = [(j, h) for j in range(G) for h in range(H)]
    cq, ck, cv = (jnp.stack([c[C * j:C * (j + 1), o + GDN_DIM * h:o + GDN_DIM * (h + 1)] for j, h in pairs]) for o in (0, W, 2 * W))
    b_col = jnp.stack([ba[C * j:C * (j + 1), h:h + 1] for j, h in pairs])
    a_col = jnp.stack([ba[C * j:C * (j + 1), H + h:H + h + 1] for j, h in pairs])
    alog = jnp.stack([gp[0:1, h:h + 1] for j, h in pairs])
    dtb = jnp.stack([gp[0:1, H + h:H + h + 1] for j, h in pairs])
    return pairs, (cq, ck, cv, b_col, a_col, alog, dtb)


def _gdn_pre_specs(S, G):
    C = GDN_CHUNK
    T = C * G
    return dict(
        cur=pl.BlockSpec((T, 3 * GDN_WIDTH), lambda i: (i, 0)),
        prev=pl.BlockSpec((8, 3 * GDN_WIDTH), lambda i: (jnp.maximum(i * (T // 8) - 1, 0), 0)),
        ba=pl.BlockSpec((T, 128), lambda i: (i, P_BA // 128)),
        cw=pl.BlockSpec((GDN_CONV, 3 * GDN_WIDTH), lambda i: (0, 0)),
        vec=pl.BlockSpec((1, 128), lambda i: (0, 0)),
        hd=pl.BlockSpec((GDN_HEADS, T, GDN_DIM), lambda i: (0, i, 0)),
        hc=pl.BlockSpec((GDN_HEADS, T, C), lambda i: (0, i, 0)),
        ge=pl.BlockSpec((G, GDN_HEADS, 8, 128), lambda i: (i, 0, 0, 0)),
    )


def _hd_shape(S, last=GDN_DIM):
    return jax.ShapeDtypeStruct((GDN_HEADS, S, last), F32)


def _gdn_pre(proj, conv_w, gp):
    S = proj.shape[0]
    C, G = GDN_CHUNK, GDN_PG
    nc = S // C
    sp = _gdn_pre_specs(S, G)

    def body(cur_ref, prev_ref, ba_ref, cw_ref, gp_ref, uv_ref, wk_ref, qd_ref, ke_ref, at_ref, ti_ref, ge_ref):
        prev = prev_ref[...] * jnp.where(pl.program_id(0) == 0, 0.0, 1.0)
        c = _silu(_conv_taps(jnp.concatenate([prev, cur_ref[...]], axis=0), cw_ref[...], GDN_CONV, C * G))
        pairs, args = _gdn_pairs(c, ba_ref[...], gp_ref[...], G)
        lmat, v, rk, q_dec, attn, k_end, g_end = _gdn_stage1(*args)
        t = _tri_inv(lmat)
        u_v = _bdot_raw(t, v, "NN", _GDN_PASSES["sol"])
        w_k = _bdot_raw(t, rk, "NN", _GDN_PASSES["sol"])
        for b, (j, h) in enumerate(pairs):
            rows = slice(C * j, C * (j + 1))
            uv_ref[h, rows, :] = u_v[b]
            wk_ref[h, rows, :] = w_k[b]
            qd_ref[h, rows, :] = q_dec[b]
            ke_ref[h, rows, :] = k_end[b]
            at_ref[h, rows, :] = attn[b]
            ti_ref[h, rows, :] = t[b]
            ge_ref[j, h] = jnp.broadcast_to(g_end[b], (8, 128))

    return pl.pallas_call(
        body, name="gdn_pre", grid=(nc // G,),
        in_specs=[sp["cur"], sp["prev"], sp["ba"], sp["cw"], sp["vec"]],
        out_specs=(sp["hd"], sp["hd"], sp["hd"], sp["hd"], sp["hc"], sp["hc"], sp["ge"]),
        out_shape=(_hd_shape(S), _hd_shape(S), _hd_shape(S), _hd_shape(S), _hd_shape(S, C), _hd_shape(S, C),
                   jax.ShapeDtypeStruct((nc, GDN_HEADS, 8, 128), F32)),
        compiler_params=_cparams(("parallel",)),
    )(proj, proj, proj, conv_w, gp)


def _gdn_scan_specs(S, G, rev):
    C = GDN_CHUNK
    T = C * G
    n = S // T
    ci = (lambda i: n - 1 - i) if rev else (lambda i: i)
    return dict(
        hd=pl.BlockSpec((GDN_HEADS, T, GDN_DIM), lambda i: (0, ci(i), 0)),
        hc=pl.BlockSpec((GDN_HEADS, T, C), lambda i: (0, ci(i), 0)),
        ge=pl.BlockSpec((G, GDN_HEADS, 8, 128), lambda i: (ci(i), 0, 0, 0)),
        z=pl.BlockSpec((T, GDN_WIDTH), lambda i: (ci(i), P_Z // GDN_WIDTH)),
        oa=pl.BlockSpec((T, GDN_WIDTH), lambda i: (ci(i), 0)),
        vec=pl.BlockSpec((1, 128), lambda i: (0, 0)),
        st=pl.BlockSpec((G, GDN_HEADS, GDN_DIM, GDN_DIM), lambda i: (ci(i), 0, 0, 0)),
    )


def _gdn_scan(u_v, w_k, q_dec, k_end, attn, g_end, proj, gnw, mix, after=()):
    S = proj.shape[0]
    C, G = GDN_CHUNK, GDN_SG
    nc = S // C
    sp = _gdn_scan_specs(S, G, False)
    ps = _GDN_PASSES["scan"]

    def body(uv_ref, wk_ref, qd_ref, ke_ref, at_ref, ge_ref, z_ref, gnw_ref, *rest):
        oa_ref, st_ref, s_scr = rest[-3:]

        @pl.when(pl.program_id(0) == 0)
        def _():
            s_scr[...] = jnp.zeros_like(s_scr)

        for j in range(G):
            rows = slice(C * j, C * (j + 1))
            st = s_scr[...]
            st_ref[j] = st
            u = uv_ref[:, rows, :] - _bdot_raw(wk_ref[:, rows, :], st, "NN", ps)
            o = _bdot_raw(qd_ref[:, rows, :], st, "NN", ps) + _bdot_raw(at_ref[:, rows, :], u, "NN", ps)
            s_scr[...] = ge_ref[j][:, 0:1, 0:1] * st + _bdot_raw(ke_ref[:, rows, :], u, "TN", ps)
            for h in range(GDN_HEADS):
                cols = slice(GDN_DIM * h, GDN_DIM * (h + 1))
                oa_ref[rows, cols] = _gated_norm(o[h], z_ref[rows, cols], gnw_ref[...])

    return pl.pallas_call(
        body, name="gdn_scan", grid=(nc // G,),
        in_specs=[sp["hd"], sp["hd"], sp["hd"], sp["hd"], sp["hc"], sp["ge"], sp["z"], sp["vec"]] + [_ANY] * (1 + len(after)),
        out_specs=(sp["oa"], sp["st"]),
        out_shape=(jax.ShapeDtypeStruct(mix.shape, F32),
                   jax.ShapeDtypeStruct((nc, GDN_HEADS, GDN_DIM, GDN_DIM), F32)),
        input_output_aliases={8: 0},
        scratch_shapes=[pltpu.VMEM((GDN_HEADS, GDN_DIM, GDN_DIM), F32)],
        compiler_params=_cparams(("arbitrary",)),
    )(u_v, w_k, q_dec, k_end, attn, g_end, proj, gnw, mix, *after)


def _gdn_scan_bwd(u_v, w_k, q_dec, k_end, attn, g_end, proj, gnw, states, d_oa):
    S = proj.shape[0]
    C, G = GDN_CHUNK, GDN_SG
    nc = S // C
    sp = _gdn_scan_specs(S, G, True)
    ps, pb = _GDN_PASSES["scan"], _GDN_PASSES["bwd"]

    def body(uv_ref, wk_ref, qd_ref, ke_ref, at_ref, ge_ref, z_ref, gnw_ref, st_ref, doa_ref,
             duv_ref, dwk_ref, dqd_ref, dke_ref, dat_ref, dge_ref, dz_ref, dgnw_ref, ds_scr):
        @pl.when(pl.program_id(0) == 0)
        def _():
            ds_scr[...] = jnp.zeros_like(ds_scr)
            dgnw_ref[...] = jnp.zeros_like(dgnw_ref)

        dgnw = jnp.zeros((1, 128), F32)
        for j in reversed(range(G)):
            rows = slice(C * j, C * (j + 1))
            st = st_ref[j]
            wk, qd, ke, at = wk_ref[:, rows, :], qd_ref[:, rows, :], ke_ref[:, rows, :], at_ref[:, rows, :]
            u = uv_ref[:, rows, :] - _bdot_raw(wk, st, "NN", ps)
            o = _bdot_raw(qd, st, "NN", ps) + _bdot_raw(at, u, "NN", ps)
            dos = []
            for h in range(GDN_HEADS):
                cols = slice(GDN_DIM * h, GDN_DIM * (h + 1))
                _, vjp2 = jax.vjp(_gated_norm, o[h], z_ref[rows, cols], gnw_ref[...])
                do_h, dz_h, dgn = vjp2(doa_ref[rows, cols])
                dz_ref[rows, cols] = dz_h
                dgnw = dgnw + dgn
                dos.append(do_h)
            do = jnp.stack(dos)
            ds_new = ds_scr[...]
            du = _bdot_raw(at, do, "TN", pb) + _bdot_raw(ke, ds_new, "NN", pb)
            duv_ref[:, rows, :] = du
            dat_ref[:, rows, :] = _bdot_raw(do, u, "NT", pb)
            dqd_ref[:, rows, :] = _bdot_raw(do, st, "NT", pb)
            dke_ref[:, rows, :] = _bdot_raw(u, ds_new, "NT", pb)
            dwk_ref[:, rows, :] = -_bdot_raw(du, st, "NT", pb)
            d_ge = jnp.sum(jnp.sum(st * ds_new, axis=2, keepdims=True), axis=1, keepdims=True)
            dge_ref[j] = jnp.broadcast_to(d_ge, (GDN_HEADS, 8, 128))
            ds_scr[...] = ge_ref[j][:, 0:1, 0:1] * ds_new + _bdot_raw(qd, do, "TN", pb) - _bdot_raw(wk, du, "TN", pb)
        dgnw_ref[...] += dgnw

    return pl.pallas_call(
        body, name="gdn_scan_bwd", grid=(nc // G,),
        in_specs=[sp["hd"], sp["hd"], sp["hd"], sp["hd"], sp["hc"], sp["ge"], sp["z"], sp["vec"], sp["st"], sp["oa"]],
        out_specs=(sp["hd"], sp["hd"], sp["hd"], sp["hd"], sp["hc"], sp["ge"], sp["oa"], sp["vec"]),
        out_shape=(_hd_shape(S), _hd_shape(S), _hd_shape(S), _hd_shape(S), _hd_shape(S, C),
                   jax.ShapeDtypeStruct((nc, GDN_HEADS, 8, 128), F32), jax.ShapeDtypeStruct((S, GDN_WIDTH), F32),
                   jax.ShapeDtypeStruct((1, 128), F32)),
        scratch_shapes=[pltpu.VMEM((GDN_HEADS, GDN_DIM, GDN_DIM), F32)],
        compiler_params=_cparams(("arbitrary",)),
    )(u_v, w_k, q_dec, k_end, attn, g_end, proj, gnw, states, d_oa)


def _gdn_post(proj, conv_w, gp, tinv, u_v, w_k, d_uv, d_wk, d_qd, d_ke, d_at, d_ge):
    S = proj.shape[0]
    C, G = GDN_CHUNK, GDN_PG
    nc = S // C
    sp = _gdn_pre_specs(S, G)
    pb = _GDN_PASSES["bwd"]

    def body(cur_ref, prev_ref, ba_ref, cw_ref, gp_ref, ti_ref, uv_ref, wk_ref, duv_ref, dwk_ref, dqd_ref, dke_ref,
             dat_ref, dge_ref, dpre_ref, dba_ref, dgp_ref):
        i = pl.program_id(0)

        @pl.when(i == 0)
        def _():
            dgp_ref[...] = jnp.zeros_like(dgp_ref)

        prev = prev_ref[...] * jnp.where(i == 0, 0.0, 1.0)
        pre = _conv_taps(jnp.concatenate([prev, cur_ref[...]], axis=0), cw_ref[...], GDN_CONV, C * G)
        sg = jax.nn.sigmoid(pre)
        dsilu = sg * (1.0 + pre * (1.0 - sg))
        pairs, args = _gdn_pairs(pre * sg, ba_ref[...], gp_ref[...], G)
        _, vjp1 = jax.vjp(functools.partial(_gdn_stage1, dot=_bdot), *args)

        def take(ref):
            return jnp.stack([ref[h, C * j:C * (j + 1), :] for j, h in pairs])

        t, u_v, w_k = take(ti_ref), take(uv_ref), take(wk_ref)
        d_v = _bdot_raw(t, take(duv_ref), "TN", pb)
        d_rk = _bdot_raw(t, take(dwk_ref), "TN", pb)
        d_l = -(_bdot_raw(d_v, u_v, "NT", pb) + _bdot_raw(d_rk, w_k, "NT", pb))
        d_ge = jnp.stack([dge_ref[j, h][0:1, 0:1] for j, h in pairs])
        dcq, dck, dcv, db, da, dalog, ddtb = vjp1((d_l, d_v, d_rk, take(dqd_ref), take(dat_ref), take(dke_ref), d_ge))
        lane = lax.broadcasted_iota(jnp.int32, (C, 128), 1)
        lane1 = lax.broadcasted_iota(jnp.int32, (1, 128), 1)
        dgp = jnp.zeros((1, 128), F32)
        for j in range(G):
            rows = slice(C * j, C * (j + 1))
            dba = jnp.zeros((C, 128), F32)
            for h in range(GDN_HEADS):
                b = GDN_HEADS * j + h
                for o_, dcx in ((0, dcq), (GDN_WIDTH, dck), (2 * GDN_WIDTH, dcv)):
                    cols = slice(o_ + GDN_DIM * h, o_ + GDN_DIM * (h + 1))
                    dpre_ref[rows, cols] = dcx[b] * dsilu[rows, cols]
                dba = dba + jnp.where(lane == h, db[b], 0.0) + jnp.where(lane == GDN_HEADS + h, da[b], 0.0)
                dgp = dgp + jnp.where(lane1 == h, dalog[b], 0.0) + jnp.where(lane1 == GDN_HEADS + h, ddtb[b], 0.0)
            dba_ref[rows, :] = dba
        dgp_ref[0:1, :] += dgp

    T = C * G
    return pl.pallas_call(
        body, name="gdn_post", grid=(nc // G,),
        in_specs=[sp["cur"], sp["prev"], sp["ba"], sp["cw"], sp["vec"], sp["hc"], sp["hd"], sp["hd"], sp["hd"], sp["hd"],
                  sp["hd"], sp["hd"], sp["hc"], sp["ge"]],
        out_specs=(sp["cur"], pl.BlockSpec((T, 128), lambda i: (i, 0)), pl.BlockSpec((8, 128), lambda i: (0, 0))),
        out_shape=(jax.ShapeDtypeStruct((S, 3 * GDN_WIDTH), F32), jax.ShapeDtypeStruct((S, 128), F32),
                   jax.ShapeDtypeStruct((8, 128), F32)),
        compiler_params=_cparams(("arbitrary",)),
    )(proj, proj, proj, conv_w, gp, tinv, u_v, w_k, d_uv, d_wk, d_qd, d_ke, d_at, d_ge)


def _conv_bwd(dpre, x, xcol0, w, K, name, tc):
    S, Cc = dpre.shape
    T = _pick_tile(S, 256)
    nt, ncol = S // T, Cc // tc
    xo = xcol0 // tc

    def body(d_ref, dn_ref, x_ref, xp_ref, w_ref, dx_ref, dw_ref):
        i = pl.program_id(1)
        dn = dn_ref[...] * jnp.where(i == nt - 1, 0.0, 1.0)
        dv = d_ref[...]
        ext_d = jnp.concatenate([dv, dn], axis=0)
        dx_ref[...] = _conv_taps_t(ext_d, w_ref[...], K, T).astype(dx_ref.dtype)
        xp = xp_ref[...] * jnp.where(i == 0, 0.0, 1.0)
        ext_x = jnp.concatenate([xp, x_ref[...]], axis=0)

        @pl.when(i == 0)
        def _():
            dw_ref[...] = jnp.zeros_like(dw_ref)

        for k in range(K):
            dw_ref[k:k + 1, :] += jnp.sum(dv * _shifted(ext_x, (K - 1) - k, 8, T), axis=0, keepdims=True)

    r8 = T // 8
    return pl.pallas_call(
        body, name=name, grid=(ncol, nt),
        in_specs=[pl.BlockSpec((T, tc), lambda j, i: (i, j)),
                  pl.BlockSpec((8, tc), lambda j, i: (jnp.minimum((i + 1) * r8, S // 8 - 1), j)),
                  pl.BlockSpec((T, tc), lambda j, i: (i, j + xo)),
                  pl.BlockSpec((8, tc), lambda j, i: (jnp.maximum(i * r8 - 1, 0), j + xo)),
                  pl.BlockSpec((K, tc), lambda j, i: (0, j))],
        out_specs=(pl.BlockSpec((T, tc), lambda j, i: (i, j)), pl.BlockSpec((K, tc), lambda j, i: (0, j))),
        out_shape=(jax.ShapeDtypeStruct((S, Cc), _MXU), jax.ShapeDtypeStruct((K, Cc), F32)),
        compiler_params=_cparams(("parallel", "arbitrary")),
    )(dpre, dpre, x, x, w)


def _dil_bias(nt, T):
    d = (np.arange(nt)[:, None, None] * T + np.arange(T)[None, None, :] - np.arange(T)[None, :, None])
    cnt = ((d >= 0) & (d <= 128)).astype(np.float64) + ((d >= 0) & (d % 4 == 0) & (d <= 512)) + ((d >= 0) & (d % 16 == 0))
    return jnp.asarray(np.where(cnt > 0, np.log(np.maximum(cnt, 1.0)), -1e30), dtype=F32)


def _attn_fwd(proj, after=()):
    S = proj.shape[0]
    T = min(ATT_T, S)
    nt = S // T
    bias = _dil_bias(nt, T)
    scale = DIL_DIM ** -0.5
    npair = DIL_WIDTH // 128
    qb0, kb0, vb0 = P_QKVB // 128, (P_QKVB + DIL_WIDTH) // 128, (P_QKVB + 2 * DIL_WIDTH) // 128

    def body(q_ref, k_ref, v_ref, b_ref, *rest):
        o_ref, lse_ref = rest[-2:]
        i = pl.program_id(1)
        qs = (q_ref[...] * scale).astype(_MXU)

        def step(j, carry):
            kt = k_ref[pl.ds(pl.multiple_of(j * T, T), T), :].astype(_MXU)
            vt = v_ref[pl.ds(pl.multiple_of(j * T, T), T), :].astype(_MXU)
            bt = b_ref[i - j]
            out = []
            for hh in range(2):
                m, l, acc = carry[hh]
                sl = slice(hh * DIL_DIM, (hh + 1) * DIL_DIM)
                s = lax.dot_general(kt[:, sl], qs[:, sl], (_NT, ((), ())), preferred_element_type=F32) + bt
                m_new = jnp.maximum(m, jnp.max(s, axis=0, keepdims=True))
                p = jnp.exp(s - m_new)
                a = jnp.exp(m - m_new)
                l = a * l + jnp.sum(p, axis=0, keepdims=True)
                acc = a * acc + lax.dot_general(vt[:, sl], p.astype(_MXU), (_TN, ((), ())), preferred_element_type=F32)
                out.append((m_new, l, acc))
            return tuple(out)

        init = tuple((jnp.full((1, T), -1e30, F32), jnp.zeros((1, T), F32), jnp.zeros((DIL_DIM, T), F32)) for _ in range(2))
        res = lax.fori_loop(0, i + 1, step, init)
        lse_ref[...] = jnp.zeros_like(lse_ref)
        for hh in range(2):
            m, l, acc = res[hh]
            o_ref[:, hh * DIL_DIM:(hh + 1) * DIL_DIM] = (acc / l).T
            lse_ref[hh:hh + 1, :] = m + jnp.log(l)

    return pl.pallas_call(
        body, name="attn_fwd", grid=(npair, nt),
        in_specs=[pl.BlockSpec((T, 128), lambda p, i: (i, qb0 + p)),
                  pl.BlockSpec((S, 128), lambda p, i: (0, kb0 + p)),
                  pl.BlockSpec((S, 128), lambda p, i: (0, vb0 + p)),
                  pl.BlockSpec((nt, T, T), lambda p, i: (0, 0, 0))] + [_ANY] * len(after),
        out_specs=(pl.BlockSpec((T, 128), lambda p, i: (i, GDN_WIDTH // 128 + p)),
                   pl.BlockSpec((None, None, 8, T), lambda p, i: (p, i, 0, 0))),
        out_shape=(jax.ShapeDtypeStruct((S, GDN_WIDTH + DIL_WIDTH), F32), jax.ShapeDtypeStruct((npair, nt, 8, T), F32)),
        compiler_params=_cparams(("parallel", "parallel")),
    )(proj, proj, proj, bias, *after)


def _attn_bwd(proj, mix, lse, d_mix):
    S = proj.shape[0]
    T = min(ATT_T, S)
    nt = S // T
    bias = _dil_bias(nt, T)
    scale = DIL_DIM ** -0.5
    npair = DIL_WIDTH // 128
    qb0, kb0, vb0 = P_QKVB // 128, (P_QKVB + DIL_WIDTH) // 128, (P_QKVB + 2 * DIL_WIDTH) // 128

    def body(q_ref, k_ref, v_ref, o_ref, lse_ref, do_ref, b_ref, dq_ref, dk_ref, dv_ref, dq_scr):
        j = pl.program_id(1)

        @pl.when(j == 0)
        def _():
            dq_scr[...] = jnp.zeros_like(dq_scr)

        kt = k_ref[...].astype(_MXU)
        vt = v_ref[...].astype(_MXU)
        ones = jnp.ones((8, DIL_DIM), F32)

        def step(i, carry):
            rows = pl.ds(pl.multiple_of(i * T, T), T)
            qs = (q_ref[rows, :] * scale).astype(_MXU)
            dov = do_ref[rows, :]
            prod = dov * o_ref[rows, :]
            lsev = lse_ref[i]
            dob = dov.astype(_MXU)
            bt = b_ref[i - j]
            out = []
            dqs = []
            for hh in range(2):
                dk, dv = carry[hh]
                sl = slice(hh * DIL_DIM, (hh + 1) * DIL_DIM)
                s = lax.dot_general(kt[:, sl], qs[:, sl], (_NT, ((), ())), preferred_element_type=F32) + bt
                p = jnp.exp(s - lsev[hh:hh + 1, :])
                delta = lax.dot_general(ones, prod[:, sl], (_NT, ((), ())), precision=_HI, preferred_element_type=F32)[0:1, :]
                dp = lax.dot_general(vt[:, sl], dob[:, sl], (_NT, ((), ())), preferred_element_type=F32)
                ds = (p * (dp - delta)).astype(_MXU)
                dv = dv + lax.dot_general(p.astype(_MXU), dob[:, sl], (_NN, ((), ())), preferred_element_type=F32)
                dk = dk + lax.dot_general(ds, qs[:, sl], (_NN, ((), ())), preferred_element_type=F32)
                dqs.append(lax.dot_general(ds, kt[:, sl], (_TN, ((), ())), preferred_element_type=F32) * scale)
                out.append((dk, dv))
            dq_scr[rows, :] += jnp.concatenate(dqs, axis=1)
            return tuple(out)

        init = tuple((jnp.zeros((T, DIL_DIM), F32), jnp.zeros((T, DIL_DIM), F32)) for _ in range(2))
        res = lax.fori_loop(j, nt, step, init)
        dk_ref[...] = jnp.concatenate([res[0][0], res[1][0]], axis=1).astype(dk_ref.dtype)
        dv_ref[...] = jnp.concatenate([res[0][1], res[1][1]], axis=1).astype(dv_ref.dtype)

        @pl.when(j == nt - 1)
        def _():
            dq_ref[...] = dq_scr[...].astype(dq_ref.dtype)

    full = lambda c0: pl.BlockSpec((S, 128), lambda p, j: (0, c0 + p))
    tile = lambda c0: pl.BlockSpec((T, 128), lambda p, j: (j, c0 + p))
    out3 = jax.ShapeDtypeStruct((S, DIL_WIDTH), _MXU)
    return pl.pallas_call(
        body, name="attn_bwd", grid=(npair, nt),
        in_specs=[full(qb0), tile(kb0), tile(vb0), full(GDN_WIDTH // 128),
                  pl.BlockSpec((None, nt, 8, T), lambda p, j: (p, 0, 0, 0)), full(GDN_WIDTH // 128),
                  pl.BlockSpec((nt, T, T), lambda p, j: (0, 0, 0))],
        out_specs=(full(0), tile(0), tile(0)),
        out_shape=(out3, out3, out3),
        scratch_shapes=[pltpu.VMEM((S, 128), F32)],
        compiler_params=_cparams(("parallel", "arbitrary")),
    )(proj, proj, proj, mix, lse, d_mix, bias)


def _ffn_act(up, cw):
    S, Cc = up.shape[0], up.shape[1] // 2
    T, tc = _pick_tile(S, 256), _pick_tile(Cc, 1536)
    r16 = T // 16
    nct = Cc // tc

    def body(g_ref, gp_ref, u_ref, up_ref, wg_ref, wu_ref, o_ref):
        keep = jnp.where(pl.program_id(1) == 0, 0.0, 1.0)
        cg = _conv_taps(jnp.concatenate([gp_ref[8:16, :].astype(F32) * keep, g_ref[...].astype(F32)], axis=0),
                        wg_ref[...], FFN_CONV, T)
        cu = _conv_taps(jnp.concatenate([up_ref[8:16, :].astype(F32) * keep, u_ref[...].astype(F32)], axis=0),
                        wu_ref[...], FFN_CONV, T)
        o_ref[...] = (_silu(cg) * cu).astype(o_ref.dtype)

    cur = lambda o: pl.BlockSpec((T, tc), lambda j, i: (i, j + o))
    prev = lambda o: pl.BlockSpec((16, tc), lambda j, i: (jnp.maximum(i * r16 - 1, 0), j + o))
    wsp = lambda o: pl.BlockSpec((FFN_CONV, tc), lambda j, i: (0, j + o))
    return pl.pallas_call(
        body, name="ffn_act", grid=(nct, S // T),
        in_specs=[cur(0), prev(0), cur(nct), prev(nct), wsp(0), wsp(nct)], out_specs=cur(0),
        out_shape=jax.ShapeDtypeStruct((S, Cc), _MXU),
        compiler_params=_cparams(("parallel", "parallel")),
    )(up, up, up, up, cw, cw)


def _ffn_act_bwd(d_act, up, cw):
    S, Cc = up.shape[0], up.shape[1] // 2
    T, tc = _pick_tile(S, 256), _pick_tile(Cc, 1536)
    r8, r16 = T // 8, T // 16
    nt = S // T
    nct = Cc // tc
    K = FFN_CONV

    def body(da_ref, dan_ref, g_ref, gp_ref, gn_ref, u_ref, up_ref, un_ref, wg_ref, wu_ref,
             dg_ref, du_ref, dwg_ref, dwu_ref):
        i = pl.program_id(1)
        keep_p = jnp.where(i == 0, 0.0, 1.0)
        keep_n = jnp.where(i == nt - 1, 0.0, 1.0)
        wg, wu = wg_ref[...], wu_ref[...]
        xg = jnp.concatenate([gp_ref[8:16, :].astype(F32) * keep_p, g_ref[...].astype(F32),
                              gn_ref[0:8, :].astype(F32) * keep_n], axis=0)
        xu = jnp.concatenate([up_ref[8:16, :].astype(F32) * keep_p, u_ref[...].astype(F32),
                              un_ref[0:8, :].astype(F32) * keep_n], axis=0)
        cg = _conv_taps(xg, wg, K, T + 8)
        cu = _conv_taps(xu, wu, K, T + 8)
        da = jnp.concatenate([da_ref[...], dan_ref[...] * keep_n], axis=0)
        sg = jax.nn.sigmoid(cg)
        d_cg = da * cu * (sg * (1.0 + cg * (1.0 - sg)))
        d_cu = da * (cg * sg)
        dg_ref[...] = _conv_taps_t(d_cg, wg, K, T).astype(dg_ref.dtype)
        du_ref[...] = _conv_taps_t(d_cu, wu, K, T).astype(du_ref.dtype)

        @pl.when(i == 0)
        def _():
            dwg_ref[...] = jnp.zeros_like(dwg_ref)
            dwu_ref[...] = jnp.zeros_like(dwu_ref)

        for k in range(K):
            dwg_ref[k:k + 1, :] += jnp.sum(d_cg[0:T, :] * _shifted(xg, (K - 1) - k, 8, T), axis=0, keepdims=True)
            dwu_ref[k:k + 1, :] += jnp.sum(d_cu[0:T, :] * _shifted(xu, (K - 1) - k, 8, T), axis=0, keepdims=True)

    cur = lambda o: pl.BlockSpec((T, tc), lambda j, i: (i, j + o))
    prev = lambda o: pl.BlockSpec((16, tc), lambda j, i: (jnp.maximum(i * r16 - 1, 0), j + o))
    nxt = lambda o: pl.BlockSpec((16, tc), lambda j, i: (jnp.minimum((i + 1) * r16, S // 16 - 1), j + o))
    nxt8 = pl.BlockSpec((8, tc), lambda j, i: (jnp.minimum((i + 1) * r8, S // 8 - 1), j))
    wsp = lambda o: pl.BlockSpec((K, tc), lambda j, i: (0, j + o))
    return pl.pallas_call(
        body, name="ffn_act_bwd", grid=(nct, nt),
        in_specs=[cur(0), nxt8, cur(0), prev(0), nxt(0), cur(nct), prev(nct), nxt(nct), wsp(0), wsp(nct)],
        out_specs=(cur(0), cur(0), wsp(0), wsp(0)),
        out_shape=(jax.ShapeDtypeStruct((S, Cc), _MXU), jax.ShapeDtypeStruct((S, Cc), _MXU),
                   jax.ShapeDtypeStruct((K, Cc), F32), jax.ShapeDtypeStruct((K, Cc), F32)),
        compiler_params=_cparams(("parallel", "arbitrary")),
    )(d_act, d_act, up, up, up, up, up, up, cw, cw)


def _local_step(x, tgt, h1, n1w, n2w, fnw, gp, gnw, wp, conv_w, fcw, rest_weights, early_grads):
    proj = _mm(h1, wp, "nn", name="proj")
    u_v, w_k, q_dec, k_end, attn, tinv, g_end = _gdn_pre(proj, conv_w, gp)
    mix, lse = _attn_fwd(proj)
    mix, states = _gdn_scan(u_v, w_k, q_dec, k_end, attn, g_end, proj, gnw, mix, after=[rest_weights[0]([mix])])
    w_out, w_up4, w_down = rest_weights[1]([mix])
    x2 = _mm(mix, w_out, "nn", residual=x, name="outproj")
    h2 = _rmsnorm_fwd(x2, n2w, "norm2")
    up = _mm(h2, w_up4, "nn", b_blocks=True, out_dtype=_MXU, name="up")
    act = _ffn_act(up, fcw)
    x3 = _mm(act, w_down, "nn", residual=x2, name="down")
    loss, dx3, dx3n, d_fnw = _loss_head(x3, fnw, tgt, "loss_head")
    d_act = _mm(dx3n, w_down, "nt", name="d_act")
    d_wdown = _mm(act, dx3n, "tn", name="d_wdown")
    d_upg, d_upu, d_fcwg, d_fcwu = _ffn_act_bwd(d_act, up, fcw)
    d_wup = _mm(h2, d_upg, "tn", place=("blocks", N_CHIPS, 0), tn=w_up4.shape[2], name="d_wgate")
    d_wup = _mm(h2, d_upu, "tn", place=("blocks", N_CHIPS, N_CHIPS // 2), tn=w_up4.shape[2], into=d_wup, name="d_wup")
    token = early_grads[0](d_wup, d_wdown)
    d_h2 = _mm_nt_blocks([d_upg, d_upu], w_up4, "d_h2", after=[token])
    dx2, d_n2w = _rmsnorm_bwd(d_h2, x2, n2w, dx3, "norm2_bwd", after=[token])
    token = early_grads[1](dx2)
    d_mix = _mm(dx2, w_out, "nt", name="d_mix")
    d_wout = _mm(mix, dx2, "tn", name="d_wout")
    dq_b, dk_b, dv_b = _attn_bwd(proj, mix, lse, d_mix)
    d_uv, d_wk, d_qd, d_ke, d_at, d_ge, d_z, d_gnw = _gdn_scan_bwd(u_v, w_k, q_dec, k_end, attn, g_end, proj,
                                                                   gnw + token[0:1, 0:1], states, d_mix)
    d_pre, d_ba, d_gp = _gdn_post(proj, conv_w, gp, tinv, u_v, w_k, d_uv, d_wk, d_qd, d_ke, d_at, d_ge)
    d_qkva, d_convw = _conv_bwd(d_pre, proj, 0, conv_w, GDN_CONV, "gdn_conv_bwd", 512)
    d_proj = jnp.concatenate([d_qkva, d_z.astype(_MXU), dq_b, dk_b, dv_b, d_ba.astype(_MXU),
                              jnp.zeros((x.shape[0], P_COLS - P_BA - 128), _MXU)], axis=1)
    d_wp = _mm(h1, d_proj, "tn", name="d_wp")
    token = early_grads[2](d_wp, d_wout)
    d_h1 = _mm(d_proj, wp, "nt", name="d_h1", after=[token])
    dx, d_n1w = _rmsnorm_bwd(d_h1, x, n1w, dx2, "norm1_bwd", after=[token])
    grads = dict(wp=d_wp, conv_w=d_convw, w_out=d_wout, w_up=d_wup, fcw_g=d_fcwg, fcw_u=d_fcwu, w_down=d_wdown,
                 n1w=d_n1w, n2w=d_n2w, fnw=d_fnw, gp=d_gp, gnw=d_gnw)
    return loss, dx, grads


_HBM = pl.BlockSpec(memory_space=pltpu.HBM)


def _pos():
    return lax.axis_index("x"), lax.axis_index("y"), lax.axis_index("c")


def _other_chips(x, y):
    return [(1 - x, y), (x, 1 - y), (1 - x, 1 - y)]


def _halvable(shape):
    return shape[0] % 32 == 0


def _rows_of_half(shape, half):
    if not _halvable(shape):
        return pl.ds(0, shape[0])
    return pl.ds(pl.multiple_of(half * (shape[0] // 2), 16), shape[0] // 2)


_SEM = pl.BlockSpec(memory_space=pltpu.SEMAPHORE)
_ANY = pl.BlockSpec(memory_space=pl.ANY)
_DATAFLOW = pltpu.SideEffectType.DATAFLOW_SIDE_EFFECTING


def _in_hbm(a):
    return pltpu.with_memory_space_constraint(a, pltpu.HBM)


def _halves_copy(src_refs, land_refs, send_sems, recv_sems, shapes, a, j, block, x, y, c):
    px, py = _other_chips(x, y)[j]
    rows = _rows_of_half(shapes[a], c)
    return pltpu.make_async_remote_copy(
        src_ref=src_refs[a].at[rows, :], dst_ref=land_refs[a].at[block, rows, :], send_sem=send_sems.at[3 * a + j],
        recv_sem=recv_sems.at[3 * a + j], device_id=(px, py, c), device_id_type=MESH)


def _gather_halves_start(shards, after, name):
    n = len(shards)
    shapes = [s.shape for s in shards]

    def body(*refs):
        ins, lands = refs[:n], refs[n:2 * n]
        send_sems, recv_sems = refs[2 * n + 1], refs[2 * n + 2]
        token = refs[-1]
        x, y, c = _pos()
        q = 2 * x + y
        for a in range(n):
            for j in range(3):
                _halves_copy(ins, lands, send_sems, recv_sems, shapes, a, j, q, x, y, c).start()
        token[...] = jnp.zeros_like(token)

    land_shapes = [(N_CHIPS,) + s.shape for s in shards]
    return pl.pallas_call(
        body, name=name,
        out_shape=(pltpu.SemaphoreType.DMA((3 * n,)), pltpu.SemaphoreType.DMA((3 * n,)),
                   *[pltpu.HBM(s.shape, s.dtype) for s in shards],
                   *[pltpu.HBM(ls, s.dtype) for ls, s in zip(land_shapes, shards)],
                   jax.ShapeDtypeStruct((8, 128), F32)),
        in_specs=[_HBM] * (2 * n) + [_ANY],
        out_specs=(_SEM, _SEM, *[_HBM] * (2 * n), pl.BlockSpec(memory_space=pltpu.VMEM)),
        input_output_aliases={a: 2 + a for a in range(2 * n)},
        compiler_params=pltpu.CompilerParams(has_side_effects=_DATAFLOW),
    )(*[_in_hbm(s) for s in shards], *[_in_hbm(lax.empty(ls, s.dtype)) for ls, s in zip(land_shapes, shards)], after)


def _gather_halves_wait(started, after, name):
    send_sems, recv_sems, *thru = started
    n = len(thru) // 2
    shapes = [t.shape for t in thru[:n]]

    def body(*refs):
        ins, lands = refs[:n], refs[n:2 * n]
        send_sems, recv_sems = refs[2 * n], refs[2 * n + 1]
        x, y, c = _pos()
        q = 2 * x + y
        chips = _other_chips(x, y)
        for a in range(n):
            for j, (px, py) in enumerate(chips):
                _halves_copy(ins, lands, send_sems, recv_sems, shapes, a, j, q, x, y, c).wait_send()
                _halves_copy(ins, lands, send_sems, recv_sems, shapes, a, j, 2 * px + py, x, y, c).wait_recv()

    outs = pl.pallas_call(
        body, name=name, out_shape=[pltpu.HBM(t.shape, t.dtype) for t in thru],
        in_specs=[_HBM] * (2 * n) + [_SEM, _SEM] + [_ANY] * len(after), out_specs=[_HBM] * (2 * n),
        input_output_aliases={a: a for a in range(2 * n)},
        compiler_params=pltpu.CompilerParams(has_side_effects=_DATAFLOW),
    )(*thru, send_sems, recv_sems, *after)
    return outs[:n], outs[n:]


def _sibling_fill(gathered, name):
    big = [a for a, g in enumerate(gathered) if _halvable(g.shape[1:])]
    n = len(gathered)

    def body(*refs):
        ins, outs = refs[:n], refs[n:2 * n]
        send_sems, recv_sems = refs[2 * n:]
        x, y, c = _pos()
        chips = _other_chips(x, y)

        def copy(k, j, half):
            a = big[k]
            px, py = chips[j]
            rows = _rows_of_half(gathered[a].shape[1:], half)
            return pltpu.make_async_remote_copy(
                src_ref=ins[a].at[2 * px + py, rows, :], dst_ref=outs[a].at[2 * px + py, rows, :],
                send_sem=send_sems.at[3 * k + j], recv_sem=recv_sems.at[3 * k + j],
                device_id=(x, y, 1 - c), device_id_type=MESH)

        sends = [copy(k, j, c) for k in range(len(big)) for j in range(3)]
        for cp in sends:
            cp.start()
        for k in range(len(big)):
            for j in range(3):
                copy(k, j, 1 - c).wait_recv()
        for cp in sends:
            cp.wait_send()

    return pl.pallas_call(
        body, name=name, in_specs=[_HBM] * n, out_specs=[_HBM] * n,
        out_shape=[jax.ShapeDtypeStruct(g.shape, g.dtype) for g in gathered],
        input_output_aliases={a: a for a in range(n)},
        scratch_shapes=[pltpu.SemaphoreType.DMA((3 * len(big),)), pltpu.SemaphoreType.DMA((3 * len(big),))],
    )(*gathered)


def _fill_copy(refs, send_sems, recv_sems, shapes, a, j, half, x, y, c):
    px, py = _other_chips(x, y)[j]
    rows = _rows_of_half(shapes[a], half)
    return pltpu.make_async_remote_copy(
        src_ref=refs[a].at[2 * px + py, rows, :], dst_ref=refs[a].at[2 * px + py, rows, :],
        send_sem=send_sems.at[3 * a + j], recv_sem=recv_sems.at[3 * a + j],
        device_id=(x, y, 1 - c), device_id_type=MESH)


def _sibling_fill_start(gathered, name):
    n = len(gathered)
    shapes = [g.shape[1:] for g in gathered]

    def body(*refs):
        ins = refs[:n]
        send_sems, recv_sems = refs[n], refs[n + 1]
        token = refs[-1]
        x, y, c = _pos()
        for a in range(n):
            for j in range(3):
                _fill_copy(ins, send_sems, recv_sems, shapes, a, j, c, x, y, c).start()
        token[...] = jnp.zeros_like(token)

    return pl.pallas_call(
        body, name=name,
        out_shape=(pltpu.SemaphoreType.DMA((3 * n,)), pltpu.SemaphoreType.DMA((3 * n,)),
                   *[pltpu.HBM(g.shape, g.dtype) for g in gathered], jax.ShapeDtypeStruct((8, 128), F32)),
        in_specs=[_HBM] * n,
        out_specs=(_SEM, _SEM, *[_HBM] * n, pl.BlockSpec(memory_space=pltpu.VMEM)),
        input_output_aliases={a: 2 + a for a in range(n)},
        compiler_params=pltpu.CompilerParams(has_side_effects=_DATAFLOW),
    )(*[_in_hbm(g) for g in gathered])


def _sibling_fill_wait(started, after, name):
    send_sems, recv_sems, *thru = started
    n = len(thru)
    shapes = [t.shape[1:] for t in thru]

    def body(*refs):
        ins = refs[:n]
        send_sems, recv_sems = refs[n], refs[n + 1]
        x, y, c = _pos()
        for a in range(n):
            for j in range(3):
                _fill_copy(ins, send_sems, recv_sems, shapes, a, j, c, x, y, c).wait_send()
                _fill_copy(ins, send_sems, recv_sems, shapes, a, j, 1 - c, x, y, c).wait_recv()

    return pl.pallas_call(
        body, name=name, out_shape=[pltpu.HBM(t.shape, t.dtype) for t in thru],
        in_specs=[_HBM] * n + [_SEM, _SEM] + [_ANY] * len(after), out_specs=[_HBM] * n,
        input_output_aliases={a: a for a in range(n)},
        compiler_params=pltpu.CompilerParams(has_side_effects=_DATAFLOW),
    )(*thru, send_sems, recv_sems, *after)


def _place_own(shards, gathered, cq, name, carry=()):
    n = len(shards)
    nc = len(carry)
    steps = 4

    def body(cq_ref, *refs):
        for a in range(n):
            refs[2 * n + nc + a][...] = refs[a][...]

    def tile(shape):
        return shape[0] // steps if _halvable(shape) else shape[0]

    in_specs = [pl.BlockSpec((tile(s.shape), s.shape[1]), (lambda i, s_: (i, 0)) if _halvable(s.shape) else (lambda i, s_: (0, 0)))
                for s in shards]
    in_specs += [pl.BlockSpec(memory_space=pl.ANY)] * (n + nc)
    out_specs = [pl.BlockSpec((None, tile(s.shape), s.shape[1]),
                              (lambda i, s_: (s_[1], i, 0)) if _halvable(s.shape) else (lambda i, s_: (s_[1], 0, 0)))
                 for s in shards]
    out_specs += [pl.BlockSpec(memory_space=pl.ANY)] * nc
    gs = pltpu.PrefetchScalarGridSpec(num_scalar_prefetch=1, grid=(steps,), in_specs=in_specs, out_specs=out_specs)
    outs = pl.pallas_call(
        body, name=name, grid_spec=gs,
        out_shape=[jax.ShapeDtypeStruct(g.shape, g.dtype) for g in gathered] + [jax.ShapeDtypeStruct(t.shape, t.dtype) for t in carry],
        input_output_aliases={1 + n + a: a for a in range(n + nc)},
        compiler_params=_cparams(("arbitrary",)),
    )(cq, *shards, *gathered, *carry)
    return (outs[:n], outs[n:]) if nc else outs


def _half_rows(ref, c, rh):
    return ref.at[:, pl.ds(pl.multiple_of(c * rh, 8), rh), :]


def _chips_copy(src_refs, land_refs, send_sems, recv_sems, a, j, x, y, c):
    px, py = _other_chips(x, y)[j]
    return pltpu.make_async_remote_copy(src_ref=src_refs[a].at[2 * px + py], dst_ref=land_refs[a].at[j],
                                        send_sem=send_sems.at[3 * a + j], recv_sem=recv_sems.at[3 * a + j],
                                        device_id=(px, py, c), device_id_type=MESH)


def _peer(r, x, y, c):
    return (x if r & 4 == 0 else 1 - x), (y if r & 2 == 0 else 1 - y), (c if r & 1 == 0 else 1 - c)


def _small_copy(small_ref, all_ref, send_sems, recv_sems, base, r, slot, x, y, c):
    return pltpu.make_async_remote_copy(src_ref=small_ref, dst_ref=all_ref.at[slot], send_sem=send_sems.at[base + r - 1],
                                        recv_sem=recv_sems.at[base + r - 1], device_id=_peer(r, x, y, c), device_id_type=MESH)


def _grad_chips_start(parts, name, small=None):
    n = len(parts)
    srcs = list(parts) + ([] if small is None else [small])
    m = len(srcs)

    def body(*refs):
        ins, lands = refs[:m], refs[m:2 * m]
        send_sems, recv_sems = refs[2 * m], refs[2 * m + 1]
        token = refs[-1]
        x, y, c = _pos()
        for a in range(n):
            for j in range(3):
                _chips_copy(ins, lands, send_sems, recv_sems, a, j, x, y, c).start()
        if small is not None:
            for r in range(1, 8):
                _small_copy(ins[n], lands[n], send_sems, recv_sems, 3 * n, r, 4 * x + 2 * y + c, x, y, c).start()
        token[...] = jnp.zeros_like(token)

    land_shapes = [(3,) + p.shape[1:] for p in parts] + ([] if small is None else [(8,) + small.shape])
    nsem = 3 * n + (0 if small is None else 7)
    return pl.pallas_call(
        body, name=name,
        out_shape=(pltpu.SemaphoreType.DMA((nsem,)), pltpu.SemaphoreType.DMA((nsem,)),
                   *[pltpu.HBM(p.shape, p.dtype) for p in srcs],
                   *[pltpu.HBM(ls, p.dtype) for ls, p in zip(land_shapes, srcs)],
                   jax.ShapeDtypeStruct((8, 128), F32)),
        in_specs=[_HBM] * (2 * m),
        out_specs=(_SEM, _SEM, *[_HBM] * (2 * m), pl.BlockSpec(memory_space=pltpu.VMEM)),
        input_output_aliases={a: 2 + a for a in range(2 * m)},
        compiler_params=pltpu.CompilerParams(has_side_effects=_DATAFLOW),
    )(*[_in_hbm(p) for p in srcs], *[_in_hbm(lax.empty(ls, p.dtype)) for ls, p in zip(land_shapes, srcs)])


def _grad_chips_wait(started, after, name, with_small=False):
    send_sems, recv_sems, *thru = started
    m = len(thru) // 2
    n = m - (1 if with_small else 0)

    def body(*refs):
        ins, lands = refs[:m], refs[m:2 * m]
        send_sems, recv_sems = refs[2 * m], refs[2 * m + 1]
        x, y, c = _pos()
        for a in range(n):
            for j in range(3):
                cp = _chips_copy(ins, lands, send_sems, recv_sems, a, j, x, y, c)
                cp.wait_send()
                cp.wait_recv()
        if with_small:
            for r in range(1, 8):
                px, py, pc = _peer(r, x, y, c)
                _small_copy(ins[n], lands[n], send_sems, recv_sems, 3 * n, r, 4 * x + 2 * y + c, x, y, c).wait_send()
                _small_copy(ins[n], lands[n], send_sems, recv_sems, 3 * n, r, 4 * px + 2 * py + pc, x, y, c).wait_recv()

    outs = pl.pallas_call(
        body, name=name, out_shape=[pltpu.HBM(t.shape, t.dtype) for t in thru],
        in_specs=[_HBM] * (2 * m) + [_SEM, _SEM] + [_ANY] * len(after), out_specs=[_HBM] * (2 * m),
        input_output_aliases={a: a for a in range(2 * m)},
        compiler_params=pltpu.CompilerParams(has_side_effects=_DATAFLOW),
    )(*thru, send_sems, recv_sems, *after)
    return list(outs[m:]) + list(outs[n:m])


def _sibling_copy(src_refs, land_refs, send_sems, recv_sems, rhs, a, c, x, y):
    return pltpu.make_async_remote_copy(src_ref=_half_rows(src_refs[a], 1 - c, rhs[a]), dst_ref=land_refs[a],
                                        send_sem=send_sems.at[a], recv_sem=recv_sems.at[a],
                                        device_id=(x, y, 1 - c), device_id_type=MESH)


def _grad_sibling_start(fams, name):
    n = len(fams)
    rhs = [f.shape[1] // 2 for f in fams]

    def body(*refs):
        ins, lands = refs[:n], refs[n:2 * n]
        send_sems, recv_sems = refs[2 * n], refs[2 * n + 1]
        token = refs[-1]
        x, y, c = _pos()
        for a in range(n):
            _sibling_copy(ins, lands, send_sems, recv_sems, rhs, a, c, x, y).start()
        token[...] = jnp.zeros_like(token)

    land_shapes = [(f.shape[0], f.shape[1] // 2, f.shape[2]) for f in fams]
    return pl.pallas_call(
        body, name=name,
        out_shape=(pltpu.SemaphoreType.DMA((n,)), pltpu.SemaphoreType.DMA((n,)),
                   *[pltpu.HBM(f.shape, f.dtype) for f in fams],
                   *[pltpu.HBM(ls, f.dtype) for ls, f in zip(land_shapes, fams)],
                   jax.ShapeDtypeStruct((8, 128), F32)),
        in_specs=[_HBM] * (2 * n),
        out_specs=(_SEM, _SEM, *[_HBM] * (2 * n), pl.BlockSpec(memory_space=pltpu.VMEM)),
        input_output_aliases={a: 2 + a for a in range(2 * n)},
        compiler_params=pltpu.CompilerParams(has_side_effects=_DATAFLOW),
    )(*[_in_hbm(f) for f in fams], *[_in_hbm(lax.empty(ls, f.dtype)) for ls, f in zip(land_shapes, fams)])


def _grad_sibling_wait(started, after, name):
    send_sems, recv_sems, *thru = started
    n = len(thru) // 2
    rhs = [t.shape[1] // 2 for t in thru[:n]]

    def body(*refs):
        ins, lands = refs[:n], refs[n:2 * n]
        send_sems, recv_sems = refs[2 * n], refs[2 * n + 1]
        x, y, c = _pos()
        for a in range(n):
            cp = _sibling_copy(ins, lands, send_sems, recv_sems, rhs, a, c, x, y)
            cp.wait_send()
            cp.wait_recv()

    outs = pl.pallas_call(
        body, name=name, out_shape=[pltpu.HBM(t.shape, t.dtype) for t in thru],
        in_specs=[_HBM] * (2 * n) + [_SEM, _SEM] + [_ANY] * len(after), out_specs=[_HBM] * (2 * n),
        input_output_aliases={a: a for a in range(2 * n)},
        compiler_params=pltpu.CompilerParams(has_side_effects=_DATAFLOW),
    )(*thru, send_sems, recv_sems, *after)
    return outs[:n], outs[n:]


def _grad_share(fulls, name):
    n = len(fulls)
    rhs = [f.shape[0] // 2 for f in fulls]

    def body(*refs):
        ins, outs = refs[:n], refs[n:2 * n]
        send_sems, recv_sems = refs[2 * n], refs[2 * n + 1]
        x, y, c = _pos()

        def copy(a, half):
            rows = pl.ds(pl.multiple_of(half * rhs[a], 8), rhs[a])
            return pltpu.make_async_remote_copy(src_ref=ins[a].at[rows, :], dst_ref=outs[a].at[rows, :],
                                                send_sem=send_sems.at[a], recv_sem=recv_sems.at[a],
                                                device_id=(x, y, 1 - c), device_id_type=MESH)

        sends = [copy(a, c) for a in range(n)]
        for cp in sends:
            cp.start()
        for a in range(n):
            copy(a, 1 - c).wait_recv()
        for cp in sends:
            cp.wait_send()

    return pl.pallas_call(
        body, name=name, in_specs=[_HBM] * n, out_specs=[_HBM] * n,
        out_shape=[jax.ShapeDtypeStruct(f.shape, f.dtype) for f in fulls],
        input_output_aliases={a: a for a in range(n)},
        scratch_shapes=[pltpu.SemaphoreType.DMA((n,)), pltpu.SemaphoreType.DMA((n,))],
    )(*fulls)


def _add_sibling(own, recv, cq, name):
    nb, R, Cc = own.shape
    Rh = R // 2

    def body(cq_ref, a_ref, b_ref, o32_ref, o16_ref):
        s = a_ref[0] + b_ref[0]
        mine = pl.program_id(0) == cq_ref[1]

        @pl.when(mine)
        def _():
            o32_ref[...] = s

        @pl.when(jnp.logical_not(mine))
        def _():
            o16_ref[0] = s.astype(o16_ref.dtype)

    sp = pl.BlockSpec((1, Rh, Cc), lambda b, s: (b, 0, 0))
    gs = pltpu.PrefetchScalarGridSpec(
        num_scalar_prefetch=1, grid=(nb,),
        in_specs=[pl.BlockSpec((1, Rh, Cc), lambda b, s: (b, s[0], 0)), sp],
        out_specs=[pl.BlockSpec((Rh, Cc), lambda b, s: (0, 0)), sp])
    return pl.pallas_call(
        body, name=name, grid_spec=gs,
        out_shape=[jax.ShapeDtypeStruct((Rh, Cc), F32), jax.ShapeDtypeStruct((nb, Rh, Cc), _MXU)],
        compiler_params=_cparams(("arbitrary",)),
    )(cq, own, recv)


def _add_sibling_split(d_wp, recv, cq, name):
    _, Dm, Pc = d_wp.shape
    Rh = Dm // 2
    Wb = IN_COLS // N_CHIPS
    T = 256

    def body(cq_ref, a_ref, b_ref, o32_ref, o16_ref):
        s = a_ref[0] + b_ref[0]
        blocks = [s[:, 0:Wb], s[:, Wb:2 * Wb],
                  jnp.concatenate([s[:, 2 * Wb:P_QKVB], s[:, P_BA:P_BA + 8], s[:, P_QKVB:3 * Wb - 8]], axis=1),
                  s[:, 3 * Wb - 8:P_BA]]
        q = cq_ref[1]
        own = None
        for j, blk in enumerate(blocks):
            term = jnp.where(q == j, blk, 0.0)
            own = term if own is None else own + term
            o16_ref[j] = blk.astype(o16_ref.dtype)
        o32_ref[...] = own

    gs = pltpu.PrefetchScalarGridSpec(
        num_scalar_prefetch=1, grid=(Rh // T,),
        in_specs=[pl.BlockSpec((1, T, Pc), lambda i, s: (0, s[0] * (Rh // T) + i, 0)), pl.BlockSpec((1, T, Pc), lambda i, s: (0, i, 0))],
        out_specs=[pl.BlockSpec((T, Wb), lambda i, s: (i, 0)), pl.BlockSpec((N_CHIPS, T, Wb), lambda i, s: (0, i, 0))])
    return pl.pallas_call(
        body, name=name, grid_spec=gs,
        out_shape=[jax.ShapeDtypeStruct((Rh, Wb), F32), jax.ShapeDtypeStruct((N_CHIPS, Rh, Wb), _MXU)],
        compiler_params=_cparams(("parallel",)),
    )(cq, d_wp, recv)


def _add_chips(part32, recv3, cq, name):
    Rh, Cc = part32.shape

    def body(cq_ref, a_ref, b_ref, o_ref):
        acc = a_ref[...]
        for j in range(3):
            acc = acc + b_ref[j].astype(F32)
        o_ref[...] = acc

    gs = pltpu.PrefetchScalarGridSpec(
        num_scalar_prefetch=1, grid=(1,),
        in_specs=[pl.BlockSpec((Rh, Cc), lambda i, s: (0, 0)), pl.BlockSpec((3, Rh, Cc), lambda i, s: (0, 0, 0))],
        out_specs=pl.BlockSpec((Rh, Cc), lambda i, s: (s[0], 0)))
    return pl.pallas_call(
        body, name=name, grid_spec=gs, out_shape=jax.ShapeDtypeStruct((2 * Rh, Cc), F32),
        compiler_params=_cparams(("arbitrary",)),
    )(cq, part32, recv3)


def _transposed(g):
    Dm, n = g.shape
    pad = -n % 128

    def body(g_ref, o_ref):
        xp = jnp.concatenate([g_ref[...], jnp.zeros((Dm, pad), F32)], axis=1)
        o_ref[...] = xp.T[:n, :]

    return pl.pallas_call(body, name="transposed", out_shape=jax.ShapeDtypeStruct((n, Dm), F32),
                          compiler_params=_cparams(vmem=V7X_VMEM_LIMIT))(g)


def _adamw(w, g, m, v, name):
    R, Cc = w.shape
    T = max([t for t in range(8, 257, 8) if R % t == 0], default=R)

    def body(w_ref, g_ref, m_ref, v_ref, d_ref, mo_ref, vo_ref):
        d_ref[...], mo_ref[...], vo_ref[...] = _adamw_math(w_ref[...], g_ref[...], m_ref[...], v_ref[...])

    sp = pl.BlockSpec((T, Cc), lambda i: (i, 0))
    sh = jax.ShapeDtypeStruct((R, Cc), F32)
    return pl.pallas_call(
        body, name=name, grid=(R // T,), in_specs=[sp] * 4, out_specs=(sp, sp, sp), out_shape=(sh, sh, sh),
        compiler_params=_cparams(("parallel",)),
    )(w, g, m, v)


SMALL_ROWS = 32
ROW_CONV, ROW_FCG, ROW_FCU = 5, 13, 22


def _adamw_math(w, g, m, v):
    mn = ADAM_B1 * m + (1.0 - ADAM_B1) * g
    vn = ADAM_B2 * v + (1.0 - ADAM_B2) * (g * g)
    c1 = 1.0 / (1.0 - ADAM_B1 ** ADAM_STEP)
    c2 = 1.0 / (1.0 - ADAM_B2 ** ADAM_STEP)
    return -ADAM_LR * ((mn * c1) / (jnp.sqrt(vn * c2) + ADAM_EPS) + ADAM_WD * w), mn, vn


def _pack_small(n1, n2, fn, gp, gn, conv, fcg, fcu, loss):
    W = D_MODEL

    def body(n1_ref, n2_ref, fn_ref, gp_ref, gn_ref, conv_ref, fcg_ref, fcu_ref, loss_ref, o_ref):
        o_ref[...] = jnp.zeros_like(o_ref)
        o_ref[0:1, :] = n1_ref[...]
        o_ref[1:2, :] = n2_ref[...]
        o_ref[2:3, :] = fn_ref[...]
        o_ref[3:4, 0:8] = gp_ref[0:1, 0:8]
        o_ref[3:4, 8:9] = loss_ref[0:1, 0:1]
        o_ref[4:5, 0:128] = gn_ref[...]
        for i in range(GDN_CONV):
            o_ref[ROW_CONV + 2 * i:ROW_CONV + 2 * i + 1, :] = conv_ref[i:i + 1, 0:W]
            o_ref[ROW_CONV + 2 * i + 1:ROW_CONV + 2 * i + 2, 0:3 * GDN_WIDTH - W] = conv_ref[i:i + 1, W:3 * GDN_WIDTH]
        for r0, ref in ((ROW_FCG, fcg_ref), (ROW_FCU, fcu_ref)):
            for i in range(FFN_CONV):
                for k in range(3):
                    n = min(W, D_FF - k * W)
                    o_ref[r0 + 3 * i + k:r0 + 3 * i + k + 1, 0:n] = ref[i:i + 1, k * W:k * W + n]

    return pl.pallas_call(body, name="pack_small", out_shape=jax.ShapeDtypeStruct((SMALL_ROWS, W), F32))(
        n1, n2, fn, gp, gn, conv, fcg, fcu, loss)


def _small_step(meq, small_all, small, ws, ms, vs):
    W = D_MODEL
    n = len(ws)
    cw, fw = ws[6].shape[1], ws[7].shape[1]

    def body(meq_ref, all_ref, own_ref, *refs):
        w_refs, m_refs, v_refs = refs[:n], refs[n:2 * n], refs[2 * n:3 * n]
        loss_ref = refs[3 * n]
        outs = refs[3 * n + 1:]
        me, q = meq_ref[0], meq_ref[1]
        red = None
        for d in range(8):
            term = jnp.where(me == d, own_ref[...], all_ref[d])
            red = term if red is None else red + term
        loss_ref[...] = jnp.broadcast_to(red[3:4, 8:9], loss_ref.shape)
        conv = [jnp.concatenate([red[ROW_CONV + 2 * i:ROW_CONV + 2 * i + 1, :],
                                 red[ROW_CONV + 2 * i + 1:ROW_CONV + 2 * i + 2, 0:3 * GDN_WIDTH - W]], axis=1)
                for i in range(GDN_CONV)]
        conv = jnp.concatenate(conv, axis=0)

        def fc_rows(r0):
            rows = [jnp.concatenate([red[r0 + 3 * i + k:r0 + 3 * i + k + 1, 0:min(W, D_FF - k * W)] for k in range(3)], axis=1)
                    for i in range(FFN_CONV)]
            return jnp.concatenate(rows, axis=0)

        fc = jnp.concatenate([fc_rows(ROW_FCG), fc_rows(ROW_FCU)], axis=1)

        def chip_block(full, width):
            out = None
            for j in range(N_CHIPS):
                term = jnp.where(q == j, full[:, width * j:width * (j + 1)], 0.0)
                out = term if out is None else out + term
            return out

        grads = [red[0:1, :], red[1:2, :], red[2:3, :], red[3:4, 0:4], red[3:4, 4:8], red[4:5, 0:128],
                 chip_block(conv, cw), chip_block(fc, fw)]
        for k in range(n):
            d_, m_, v_ = _adamw_math(w_refs[k][...], grads[k], m_refs[k][...], v_refs[k][...])
            outs[4 * k][...] = grads[k]
            outs[4 * k + 1][...] = d_
            outs[4 * k + 2][...] = m_
            outs[4 * k + 3][...] = v_

    full = lambda a: pl.BlockSpec(a.shape, lambda i, s_, nd=len(a.shape): (0,) * nd)
    arrays = [small_all, small, *ws, *ms, *vs]
    out_shapes = [jax.ShapeDtypeStruct((8, 128), F32)] + [jax.ShapeDtypeStruct(w.shape, F32) for w in ws for _ in range(4)]
    gs = pltpu.PrefetchScalarGridSpec(
        num_scalar_prefetch=1, grid=(1,), in_specs=[full(a) for a in arrays],
        out_specs=[pl.BlockSpec(o.shape, lambda i, s_, nd=len(o.shape): (0,) * nd) for o in out_shapes])
    return pl.pallas_call(body, name="small_step", grid_spec=gs, out_shape=out_shapes)(meq, *arrays)


def _pad_lanes(v, n=D_MODEL):
    return jnp.pad(v, ((0, 0), (0, n - v.shape[1])))


def kernel(x, norm1_w, w_in, conv_qkv_w, a_log, dt_bias, gdn_norm_w, w_out, norm2_w, w_up, ffn_conv_w, w_down, final_norm_w, loss_target, m_norm1_w, m_w_in, m_conv_qkv_w, m_a_log, m_dt_bias, m_gdn_norm_w, m_w_out, m_norm2_w, m_w_up, m_ffn_conv_w, m_w_down, m_final_norm_w, v_norm1_w, v_w_in, v_conv_qkv_w, v_a_log, v_dt_bias, v_gdn_norm_w, v_w_out, v_norm2_w, v_w_up, v_ffn_conv_w, v_w_down, v_final_norm_w):
    c = lax.axis_index("c")
    q = 2 * lax.axis_index("x") + lax.axis_index("y")
    S = x.shape[1]
    cq = jnp.stack([c, q]).astype(jnp.int32)

    *in_started, in_token = _gather_halves_start([w_in[0].astype(_MXU), conv_qkv_w[0], ffn_conv_w[0]], x, "gather_in_start")
    w_in_l, m_w_in_l, v_w_in_l = (jnp.swapaxes(a + in_token[0:1, 0:1], 1, 2)[0] for a in (w_in, m_w_in, v_w_in))
    h1 = _rmsnorm_fwd(x[0], norm1_w, "norm1", after=[in_token])
    rest = [(a[0] + in_token[0:1, 0:1]).astype(_MXU) for a in (w_out, w_up, w_down)]
    in_shards, got_in = _gather_halves_wait(in_started, [w_in_l, m_w_in_l, v_w_in_l, h1, *rest], "gather_in_wait")
    (g_in, g_conv, g_fconv), (w_in_l, m_w_in_l, v_w_in_l) = _place_own(
        in_shards, _sibling_fill(got_in, "fill_in"), cq, "place_in", carry=[w_in_l, m_w_in_l, v_w_in_l])
    *rest_started, token = _gather_halves_start(rest, g_conv, "gather_rest_start")

    rest_state = {}

    def rest_arrived(after):
        rest_state["shards"], got = _gather_halves_wait(rest_started, after, "gather_rest_wait")
        *rest_state["fill"], tok = _sibling_fill_start(got, "fill_rest_start")
        return tok

    def rest_filled(after):
        got = _sibling_fill_wait(rest_state["fill"], after, "fill_rest_wait")
        g_out, g_up, g_down = _place_own(rest_state["shards"], got, cq, "place_rest")
        return g_out.reshape(D_MODEL, D_MODEL), g_up, g_down.reshape(D_FF, D_MODEL)

    rest_weights = (rest_arrived, rest_filled)
    wp = _wp_assemble(g_in, [token])
    conv_f = jnp.concatenate([g_conv[i] for i in range(N_CHIPS)], axis=1)
    fcw = jnp.concatenate([g_fconv[i] for i in range(N_CHIPS)], axis=1)
    gp = _pad_lanes(jnp.concatenate([a_log, dt_bias], axis=1), 128)
    fnw = final_norm_w[None, :]
    early = {}

    def early_sibling(d_wup, d_wdown):
        *early["sibling"], tok = _grad_sibling_start([d_wup, d_wdown.reshape(N_CHIPS, D_FF // N_CHIPS, D_MODEL)],
                                                     "grad_sibling_early_start")
        return tok

    def early_chips(dx2):
        fams_e, got_e = _grad_sibling_wait(early["sibling"], [dx2], "grad_sibling_early_wait")
        early["parts"] = [_add_sibling(f, r, cq, "add_sibling_" + nm) for f, r, nm in zip(fams_e, got_e, ("w_up", "w_down"))]
        *early["started"], tok = _grad_chips_start([p[1] for p in early["parts"]], "grad_chips_start")
        return tok

    def late_sibling(d_wp, d_wout):
        *early["late_sibling"], tok = _grad_sibling_start(
            [d_wp[None], d_wout.reshape(N_CHIPS, D_MODEL // N_CHIPS, D_MODEL)], "grad_sibling_late_start")
        return tok

    early_grads = (early_sibling, early_chips, late_sibling)

    loss_l, dx, g = _local_step(x[0], loss_target[0], h1, norm1_w, norm2_w, fnw, gp, gdn_norm_w, wp,
                                conv_f, fcw, rest_weights, early_grads)
    small = _pack_small(g["n1w"], g["n2w"], g["fnw"], g["gp"], g["gnw"], g["conv_w"], g["fcw_g"], g["fcw_u"], loss_l)
    fams, got = _grad_sibling_wait(early["late_sibling"], [dx], "grad_sibling_late_wait")
    parts = [_add_sibling_split(fams[0], got[0], cq, "add_sibling_w_in"), _add_sibling(fams[1], got[1], cq, "add_sibling_w_out")]
    *late_started, late_token = _grad_chips_start([p[1] for p in parts], "grad_chips_late_start", small)
    got3_e = _grad_chips_wait(early["started"], [dx, late_token], "grad_chips_wait")
    g_w_up, g_w_down = _grad_share(
        [_add_chips(p[0], r3, cq, "add_chips_" + nm) for p, r3, nm in zip(early["parts"], got3_e, ("w_up", "w_down"))],
        "grad_share_early")
    big = {}

    def adamw_big(nm, w, gg, m, v):
        d_, m_, v_ = _adamw(w[0], gg, m[0], v[0], "adamw_" + nm)
        big[nm] = (gg[None], d_[None], m_[None], v_[None])

    adamw_big("w_up", w_up, g_w_up, m_w_up, v_w_up)
    adamw_big("w_down", w_down, g_w_down, m_w_down, v_w_down)
    *got3, small_all, small = _grad_chips_wait(late_started, [big["w_up"][1], big["w_down"][1]], "grad_chips_late_wait",
                                               with_small=True)
    g_w_in, g_w_out = _grad_share(
        [_add_chips(p[0], r3, cq, "add_chips_" + nm) for p, r3, nm in zip(parts, got3, ("w_in", "w_out"))],
        "grad_share_late")
    g_t = _transposed(g_w_in)
    d_t, m_t, v_t = _adamw(w_in_l, g_t, m_w_in_l, v_w_in_l, "adamw_w_in")
    big["w_in"] = tuple(jnp.swapaxes(t[None], 1, 2) for t in (g_t, d_t, m_t, v_t))
    adamw_big("w_out", w_out, g_w_out, m_w_out, v_w_out)
    small_names = ["norm1_w", "norm2_w", "final_norm_w", "a_log", "dt_bias", "gdn_norm_w", "conv_qkv_w", "ffn_conv_w"]
    loss_b, *small_out = _small_step(
        jnp.stack([2 * q + c, q]).astype(jnp.int32), small_all, small,
        [norm1_w, norm2_w, final_norm_w[None], a_log, dt_bias, gdn_norm_w, conv_qkv_w[0], ffn_conv_w[0]],
        [m_norm1_w, m_norm2_w, m_final_norm_w[None], m_a_log, m_dt_bias, m_gdn_norm_w, m_conv_qkv_w[0], m_ffn_conv_w[0]],
        [v_norm1_w, v_norm2_w, v_final_norm_w[None], v_a_log, v_dt_bias, v_gdn_norm_w, v_conv_qkv_w[0], v_ffn_conv_w[0]])
    like = dict(final_norm_w=lambda t: t[0], conv_qkv_w=lambda t: t[None], ffn_conv_w=lambda t: t[None])
    for k, nm in enumerate(small_names):
        big[nm] = tuple(like.get(nm, lambda t: t)(t) for t in small_out[4 * k:4 * k + 4])
    names = ["norm1_w", "w_in", "conv_qkv_w", "a_log", "dt_bias", "gdn_norm_w", "w_out", "norm2_w", "w_up",
             "ffn_conv_w", "w_down", "final_norm_w"]
    return (loss_b[0, 0], dx[None], *[big[n][0] for n in names], *[big[n][1] for n in names],
            *[big[n][2] for n in names], *[big[n][3] for n in names])
```

```python
import functools
import math

import numpy as np
import jax
import jax.numpy as jnp
from jax import lax
from jax.experimental import pallas as pl
from jax.experimental.pallas import tpu as pltpu

F32 = jnp.float32
BF16 = jnp.bfloat16
_MXU = jnp.bfloat16
_HI = lax.Precision.HIGHEST
EPS = 1e-6
V7X_VMEM_LIMIT = 56 * 1024 * 1024
MESH = pl.DeviceIdType.MESH

D_MODEL = 1024
GDN_HEADS, GDN_DIM, GDN_CHUNK, GDN_CONV = 4, 128, 64, 4
GDN_WIDTH = GDN_HEADS * GDN_DIM
DIL_HEADS, DIL_DIM = 8, 64
DIL_WIDTH = DIL_HEADS * DIL_DIM
D_FF, FFN_CONV = 2816, 3
IN_COLS = 3592
P_COLS = 3840
P_Z, P_QKVB, P_BA = 1536, 2048, 3584
ATT_T = 1024
ADAM_LR, ADAM_B1, ADAM_B2, ADAM_EPS, ADAM_WD, ADAM_STEP = 0.001, 0.9, 0.999, 1e-08, 0.01, 10
N_CHIPS = 4


def _cparams(sem=None, vmem=None):
    kw = {}
    if sem is not None:
        kw["dimension_semantics"] = sem
    if vmem is not None:
        kw["vmem_limit_bytes"] = vmem
    return pltpu.CompilerParams(**kw)


def _silu(x):
    return x * jax.nn.sigmoid(x)


def _pick_tile(n, cap):
    best = None
    for t in range(128, min(n, cap) + 1, 128):
        if n % t == 0:
            best = t
    return best or n


def _mm(a, b, mode, *, out_dtype=F32, residual=None, name, b_blocks=False, place=None, into=None, tn=None, after=()):
    if mode == "nn":
        M, K = a.shape
        N = b.shape[0] * b.shape[2] if b_blocks else b.shape[1]
    elif mode == "nt":
        (M, K), (N, _) = a.shape, b.shape
    else:
        (K, M), (_, N) = a.shape, b.shape
    tm = _pick_tile(M, 1024)
    tn = b.shape[2] if b_blocks else (tn or _pick_tile(N, 1536))

    def vmem(tm, tn):
        return 2 * (tm * K * a.dtype.itemsize + tn * K * b.dtype.itemsize
                    + tm * tn * (jnp.dtype(out_dtype).itemsize + (4 if residual is not None else 0))) + 3 * tm * tn * 4

    fixed_tn = b_blocks or (place is not None and place[0] == "blocks")
    while vmem(tm, tn) > 40 * 1024 * 1024:
        if (tm >= tn or fixed_tn) and tm % 256 == 0:
            tm //= 2
        elif tn % 256 == 0 and not fixed_tn:
            tn //= 2
        else:
            tm //= 2
    a_spec = pl.BlockSpec((K, tm), lambda j, i: (0, i)) if mode == "tn" else pl.BlockSpec((tm, K), lambda j, i: (i, 0))
    if b_blocks:
        b_spec = pl.BlockSpec((None, K, tn), lambda j, i: (j, 0, 0))
    else:
        b_spec = pl.BlockSpec((tn, K), lambda j, i: (j, 0)) if mode == "nt" else pl.BlockSpec((K, tn), lambda j, i: (0, j))
    r_spec = pl.BlockSpec((tm, tn), lambda j, i: (i, j))
    if place is None:
        o_spec, o_shape = r_spec, (M, N)
    elif place[0] == "rows":
        off = place[2] // tm
        o_spec, o_shape = pl.BlockSpec((tm, tn), lambda j, i: (i + off, j)), (place[1], N)
    else:
        off = place[2]
        o_spec, o_shape = pl.BlockSpec((None, tm, tn), lambda j, i: (j + off, i, 0)), (place[1], M, tn)
    dims = {"nn": (((1,), (0,)), ((), ())), "nt": (((1,), (1,)), ((), ())), "tn": (((0,), (0,)), ((), ()))}[mode]

    def body(*refs):
        a_ref, b_ref = refs[0], refs[1]
        o_ref = refs[-1]
        acc = lax.dot_general(a_ref[...].astype(_MXU), b_ref[...].astype(_MXU), dims, preferred_element_type=F32)
        if residual is not None:
            acc = acc + refs[2][...]
        o_ref[...] = acc.astype(out_dtype)

    ins, specs, alias = [a, b], [a_spec, b_spec], {}
    if residual is not None:
        ins.append(residual)
        specs.append(r_spec)
    if into is not None:
        alias = {len(ins): 0}
        ins.append(into)
        specs.append(pl.BlockSpec(memory_space=pl.ANY))
    ins += list(after)
    specs += [pl.BlockSpec(memory_space=pl.ANY)] * len(after)
    return pl.pallas_call(
        body, name=name, grid=(N // tn, M // tm), in_specs=specs, out_specs=o_spec,
        out_shape=jax.ShapeDtypeStruct(o_shape, out_dtype), input_output_aliases=alias,
        compiler_params=_cparams(("parallel", "parallel"), V7X_VMEM_LIMIT),
    )(*ins)


def _mm_nt_blocks(a_list, b4, name, after=()):
    M = a_list[0].shape[0]
    nb, N, Kb = b4.shape
    tm, tn = _pick_tile(M, 1024), _pick_tile(N, 512)

    def body(a0_ref, a1_ref, b_ref, *rest):
        o_ref = rest[-1]
        acc = None
        for blk in range(nb):
            a_ref = (a0_ref, a1_ref)[blk // 2]
            lo = (blk % 2) * Kb
            t = lax.dot_general(a_ref[:, lo:lo + Kb].astype(_MXU), b_ref[blk].astype(_MXU), (((1,), (1,)), ((), ())),
                                preferred_element_type=F32)
            acc = t if acc is None else acc + t
        o_ref[...] = acc

    a_spec = pl.BlockSpec((tm, 2 * Kb), lambda j, i: (i, 0))
    return pl.pallas_call(
        body, name=name, grid=(N // tn, M // tm),
        in_specs=[a_spec, a_spec, pl.BlockSpec((nb, tn, Kb), lambda j, i: (0, j, 0))]
        + [pl.BlockSpec(memory_space=pl.ANY)] * len(after),
        out_specs=pl.BlockSpec((tm, tn), lambda j, i: (i, j)), out_shape=jax.ShapeDtypeStruct((M, N), F32),
        compiler_params=_cparams(("parallel", "parallel"), V7X_VMEM_LIMIT),
    )(a_list[0], a_list[1], b4, *after)


def _wp_assemble(g_in, after=()):
    nb, Dm, Wb = g_in.shape
    T = 256
    n_lo = P_QKVB - 2 * Wb

    def body(g_ref, *rest):
        g2 = g_ref[2]
        rest[-1][...] = jnp.concatenate(
            [g_ref[0], g_ref[1], g2[:, :n_lo], g2[:, n_lo + 8:], g_ref[3], g2[:, n_lo:n_lo + 8],
             jnp.zeros((T, P_COLS - P_BA - 8), g_in.dtype)], axis=1)

    return pl.pallas_call(
        body, name="wp_assemble", grid=(Dm // T,),
        in_specs=[pl.BlockSpec((nb, T, Wb), lambda i: (0, i, 0))] + [pl.BlockSpec(memory_space=pl.ANY)] * len(after),
        out_specs=pl.BlockSpec((T, P_COLS), lambda i: (i, 0)), out_shape=jax.ShapeDtypeStruct((Dm, P_COLS), g_in.dtype),
        compiler_params=_cparams(("parallel",)),
    )(g_in, *after)


def _rmsnorm_fwd(x, w, name, after=()):
    S, D = x.shape
    T = _pick_tile(S, 512)

    def body(x_ref, w_ref, *rest):
        xv = x_ref[...]
        rs = lax.rsqrt(jnp.mean(xv * xv, axis=-1, keepdims=True) + EPS)
        rest[-1][...] = (xv * rs * w_ref[...]).astype(rest[-1].dtype)

    return pl.pallas_call(
        body, name=name, grid=(S // T,),
        in_specs=[pl.BlockSpec((T, D), lambda i: (i, 0)), pl.BlockSpec((1, D), lambda i: (0, 0))] + [_ANY] * len(after),
        out_specs=pl.BlockSpec((T, D), lambda i: (i, 0)),
        out_shape=jax.ShapeDtypeStruct((S, D), _MXU),
        compiler_params=_cparams(("parallel",)),
    )(x, w, *after)


def _rmsnorm_bwd(dh, x, w, dres, name, after=()):
    S, D = x.shape
    T = _pick_tile(S, 512)

    def body(dh_ref, x_ref, w_ref, dres_ref, *rest):
        dx_ref, dw_ref = rest[-2:]
        xv = x_ref[...]
        rs = lax.rsqrt(jnp.mean(xv * xv, axis=-1, keepdims=True) + EPS)
        xn = xv * rs
        dhv = dh_ref[...]
        dxn = dhv * w_ref[...]
        dx_ref[...] = dres_ref[...] + rs * (dxn - xn * jnp.mean(dxn * xn, axis=-1, keepdims=True))

        @pl.when(pl.program_id(0) == 0)
        def _():
            dw_ref[...] = jnp.zeros_like(dw_ref)

        dw_ref[...] += jnp.sum(dhv * xn, axis=0, keepdims=True)

    row = pl.BlockSpec((T, D), lambda i: (i, 0))
    vec = pl.BlockSpec((1, D), lambda i: (0, 0))
    return pl.pallas_call(
        body, name=name, grid=(S // T,), in_specs=[row, row, vec, row] + [_ANY] * len(after), out_specs=(row, vec),
        out_shape=(jax.ShapeDtypeStruct((S, D), F32), jax.ShapeDtypeStruct((1, D), F32)),
        compiler_params=_cparams(("arbitrary",)),
    )(dh, x, w, dres, *after)


def _loss_head(x3, w, tgt, name):
    S, D = x3.shape
    T = _pick_tile(S, 512)

    def body(x_ref, w_ref, t_ref, loss_ref, dx_ref, dxn_ref, dw_ref):
        xv = x_ref[...]
        rs = lax.rsqrt(jnp.mean(xv * xv, axis=-1, keepdims=True) + EPS)
        xn = xv * rs
        err = xn * w_ref[...] - t_ref[...]
        dy = err * (1.0 / D)
        dxn = dy * w_ref[...]
        dxv = rs * (dxn - xn * jnp.mean(dxn * xn, axis=-1, keepdims=True))
        dx_ref[...] = dxv
        dxn_ref[...] = dxv.astype(dxn_ref.dtype)

        @pl.when(pl.program_id(0) == 0)
        def _():
            dw_ref[...] = jnp.zeros_like(dw_ref)
            loss_ref[...] = jnp.zeros_like(loss_ref)

        dw_ref[...] += jnp.sum(dy * xn, axis=0, keepdims=True)
        part = jnp.sum(jnp.sum(err * err, axis=-1, keepdims=True), axis=0, keepdims=True) * (0.5 / D)
        loss_ref[...] += jnp.broadcast_to(part, loss_ref.shape)

    row = pl.BlockSpec((T, D), lambda i: (i, 0))
    vec = pl.BlockSpec((1, D), lambda i: (0, 0))
    return pl.pallas_call(
        body, name=name, grid=(S // T,), in_specs=[row, vec, row],
        out_specs=(pl.BlockSpec((8, 128), lambda i: (0, 0)), row, row, vec),
        out_shape=(jax.ShapeDtypeStruct((8, 128), F32), jax.ShapeDtypeStruct((S, D), F32), jax.ShapeDtypeStruct((S, D), _MXU),
                   jax.ShapeDtypeStruct((1, D), F32)),
        compiler_params=_cparams(("arbitrary",)),
    )(x3, w, tgt)


def _shifted(ext, back, lo, n):
    if back == 0:
        return ext[lo:lo + n, :]
    return pltpu.roll(ext, back % ext.shape[0], 0)[lo:lo + n, :]


def _conv_windows(ext, K, T):
    return [_shifted(ext, (K - 1) - i, 8, T) for i in range(K)]


def _conv_taps(ext, w, K, T):
    out = None
    for i, win in enumerate(_conv_windows(ext, K, T)):
        term = win * w[i:i + 1, :]
        out = term if out is None else out + term
    return out


def _conv_taps_t(ext, w, K, T):
    out = None
    for i in range(K):
        term = _shifted(ext, i - (K - 1), 0, T) * w[i:i + 1, :]
        out = term if out is None else out + term
    return out


def _tri_masks(C):
    r = lax.broadcasted_iota(jnp.int32, (C, C), 0)
    c = lax.broadcasted_iota(jnp.int32, (C, C), 1)
    return r == c, r >= c, r > c, r <= c


_NN, _NT, _TN = ((1,), (0,)), ((1,), (1,)), ((0,), (0,))
_GDN_PASSES = dict(qk=1, inv=1, sol=1, scan=1, bwd=1)


def _bdot_raw(a, b, kind, passes):
    dims = ({"NN": ((2,), (1,)), "NT": ((2,), (2,)), "TN": ((1,), (1,))}[kind], ((0,), (0,)))
    if passes == 0:
        return lax.dot_general(a, b, dims, precision=_HI, preferred_element_type=F32)
    ah, bh = a.astype(BF16), b.astype(BF16)
    out = lax.dot_general(ah, bh, dims, preferred_element_type=F32)
    if passes == 3:
        al, bl = (a - ah.astype(F32)).astype(BF16), (b - bh.astype(F32)).astype(BF16)
        out = out + lax.dot_general(ah, bl, dims, preferred_element_type=F32) + lax.dot_general(al, bh, dims, preferred_element_type=F32)
    return out


@functools.partial(jax.custom_vjp, nondiff_argnums=(2, 3))
def _bdot(a, b, kind, passes):
    return _bdot_raw(a, b, kind, passes)


def _bdot_fwd(a, b, kind, passes):
    return _bdot_raw(a, b, kind, passes), (a, b)


def _bdot_bwd(kind, passes, res, ct):
    a, b = res
    if kind == "NN":
        return _bdot_raw(ct, b, "NT", passes), _bdot_raw(a, ct, "TN", passes)
    if kind == "NT":
        return _bdot_raw(ct, b, "NN", passes), _bdot_raw(ct, a, "TN", passes)
    return _bdot_raw(b, ct, "NT", passes), _bdot_raw(a, ct, "NN", passes)


_bdot.defvjp(_bdot_fwd, _bdot_bwd)


def _softplus(x):
    return jnp.maximum(x, 0.0) + jnp.log(1.0 + jnp.exp(-jnp.abs(x)))


def _gdn_stage1(cq, ck, cv, b_col, a_col, alog, dtb, dot=_bdot_raw):
    C = cq.shape[1]
    eye, incl, strict, incl_t = _tri_masks(C)
    qn = cq * lax.rsqrt(jnp.sum(cq * cq, axis=-1, keepdims=True) + EPS) * (GDN_DIM ** -0.5)
    kn = ck * lax.rsqrt(jnp.sum(ck * ck, axis=-1, keepdims=True) + EPS)
    beta = jax.nn.sigmoid(b_col)
    g = -jnp.exp(alog) * _softplus(a_col + dtb)
    g_row = jnp.sum(jnp.where(eye, g, 0.0), axis=1, keepdims=True)
    beta_row = jnp.sum(jnp.where(eye, beta, 0.0), axis=1, keepdims=True)
    gc_col = jnp.sum(jnp.where(incl, g_row, 0.0), axis=2, keepdims=True)
    gc_row = jnp.sum(jnp.where(incl_t, g, 0.0), axis=1, keepdims=True)
    dec = jnp.where(incl, jnp.exp(jnp.where(incl, gc_col - gc_row, 0.0)), 0.0)
    kk = dot(kn, kn, "NT", _GDN_PASSES["qk"])
    qk = dot(qn, kn, "NT", _GDN_PASSES["qk"])
    lmat = jnp.where(strict, dec * kk * beta_row, 0.0)
    attn = dec * qk * beta_row
    gam = jnp.exp(gc_col)
    gc_last = gc_col[:, C - 1:C, :]
    k_end = kn * (jnp.exp(gc_last - gc_col) * beta)
    return lmat, cv, gam * kn, gam * qn, attn, k_end, jnp.exp(gc_last)


def _tri_inv(lmat):
    C = lmat.shape[1]
    eye = _tri_masks(C)[0]
    ps = _GDN_PASSES["inv"]
    p = jnp.where(eye, 1.0, 0.0) - lmat
    lp = _bdot_raw(lmat, lmat, "NN", ps)
    n = int(math.log2(C))
    for s in range(1, n):
        p = p + _bdot_raw(p, lp, "NN", ps)
        if s < n - 1:
            lp = _bdot_raw(lp, lp, "NN", ps)
    return p


def _gated_norm(o, z, gnw):
    on = o * lax.rsqrt(jnp.mean(o * o, axis=-1, keepdims=True) + EPS) * gnw
    return on * _silu(z)


GDN_PG = 4
GDN_SG = 4


def _gdn_pairs(c, ba, gp, G):
    C, W, H = GDN_CHUNK, GDN_WIDTH, GDN_HEADS
    pairs = [(j, h) for j in range(G) for h in range(H)]
    cq, ck, cv = (jnp.stack([c[C * j:C * (j + 1), o + GDN_DIM * h:o + GDN_DIM * (h + 1)] for j, h in pairs]) for o in (0, W, 2 * W))
    b_col = jnp.stack([ba[C * j:C * (j + 1), h:h + 1] for j, h in pairs])
    a_col = jnp.stack([ba[C * j:C * (j + 1), H + h:H + h + 1] for j, h in pairs])
    alog = jnp.stack([gp[0:1, h:h + 1] for j, h in pairs])
    dtb = jnp.stack([gp[0:1, H + h:H + h + 1] for j, h in pairs])
    return pairs, (cq, ck, cv, b_col, a_col, alog, dtb)


def _gdn_pre_specs(S, G):
    C = GDN_CHUNK
    T = C * G
    return dict(
        cur=pl.BlockSpec((T, 3 * GDN_WIDTH), lambda i: (i, 0)),
        prev=pl.BlockSpec((8, 3 * GDN_WIDTH), lambda i: (jnp.maximum(i * (T // 8) - 1, 0), 0)),
        ba=pl.BlockSpec((T, 128), lambda i: (i, P_BA // 128)),
        cw=pl.BlockSpec((GDN_CONV, 3 * GDN_WIDTH), lambda i: (0, 0)),
        vec=pl.BlockSpec((1, 128), lambda i: (0, 0)),
        hd=pl.BlockSpec((GDN_HEADS, T, GDN_DIM), lambda i: (0, i, 0)),
        hc=pl.BlockSpec((GDN_HEADS, T, C), lambda i: (0, i, 0)),
        ge=pl.BlockSpec((G, GDN_HEADS, 8, 128), lambda i: (i, 0, 0, 0)),
    )


def _hd_shape(S, last=GDN_DIM):
    return jax.ShapeDtypeStruct((GDN_HEADS, S, last), F32)


def _gdn_pre(proj, conv_w, gp):
    S = proj.shape[0]
    C, G = GDN_CHUNK, GDN_PG
    nc = S // C
    sp = _gdn_pre_specs(S, G)

    def body(cur_ref, prev_ref, ba_ref, cw_ref, gp_ref, uv_ref, wk_ref, qd_ref, ke_ref, at_ref, ti_ref, ge_ref):
        prev = prev_ref[...] * jnp.where(pl.program_id(0) == 0, 0.0, 1.0)
        c = _silu(_conv_taps(jnp.concatenate([prev, cur_ref[...]], axis=0), cw_ref[...], GDN_CONV, C * G))
        pairs, args = _gdn_pairs(c, ba_ref[...], gp_ref[...], G)
        lmat, v, rk, q_dec, attn, k_end, g_end = _gdn_stage1(*args)
        t = _tri_inv(lmat)
        u_v = _bdot_raw(t, v, "NN", _GDN_PASSES["sol"])
        w_k = _bdot_raw(t, rk, "NN", _GDN_PASSES["sol"])
        for b, (j, h) in enumerate(pairs):
            rows = slice(C * j, C * (j + 1))
            uv_ref[h, rows, :] = u_v[b]
            wk_ref[h, rows, :] = w_k[b]
            qd_ref[h, rows, :] = q_dec[b]
            ke_ref[h, rows, :] = k_end[b]
            at_ref[h, rows, :] = attn[b]
            ti_ref[h, rows, :] = t[b]
            ge_ref[j, h] = jnp.broadcast_to(g_end[b], (8, 128))

    return pl.pallas_call(
        body, name="gdn_pre", grid=(nc // G,),
        in_specs=[sp["cur"], sp["prev"], sp["ba"], sp["cw"], sp["vec"]],
        out_specs=(sp["hd"], sp["hd"], sp["hd"], sp["hd"], sp["hc"], sp["hc"], sp["ge"]),
        out_shape=(_hd_shape(S), _hd_shape(S), _hd_shape(S), _hd_shape(S), _hd_shape(S, C), _hd_shape(S, C),
                   jax.ShapeDtypeStruct((nc, GDN_HEADS, 8, 128), F32)),
        compiler_params=_cparams(("parallel",)),
    )(proj, proj, proj, conv_w, gp)


def _gdn_scan_specs(S, G, rev):
    C = GDN_CHUNK
    T = C * G
    n = S // T
    ci = (lambda i: n - 1 - i) if rev else (lambda i: i)
    return dict(
        hd=pl.BlockSpec((GDN_HEADS, T, GDN_DIM), lambda i: (0, ci(i), 0)),
        hc=pl.BlockSpec((GDN_HEADS, T, C), lambda i: (0, ci(i), 0)),
        ge=pl.BlockSpec((G, GDN_HEADS, 8, 128), lambda i: (ci(i), 0, 0, 0)),
        z=pl.BlockSpec((T, GDN_WIDTH), lambda i: (ci(i), P_Z // GDN_WIDTH)),
        oa=pl.BlockSpec((T, GDN_WIDTH), lambda i: (ci(i), 0)),
        vec=pl.BlockSpec((1, 128), lambda i: (0, 0)),
        st=pl.BlockSpec((G, GDN_HEADS, GDN_DIM, GDN_DIM), lambda i: (ci(i), 0, 0, 0)),
    )


def _gdn_scan(u_v, w_k, q_dec, k_end, attn, g_end, proj, gnw, mix, after=()):
    S = proj.shape[0]
    C, G = GDN_CHUNK, GDN_SG
    nc = S // C
    sp = _gdn_scan_specs(S, G, False)
    ps = _GDN_PASSES["scan"]

    def body(uv_ref, wk_ref, qd_ref, ke_ref, at_ref, ge_ref, z_ref, gnw_ref, *rest):
        oa_ref, st_ref, s_scr = rest[-3:]

        @pl.when(pl.program_id(0) == 0)
        def _():
            s_scr[...] = jnp.zeros_like(s_scr)

        for j in range(G):
            rows = slice(C * j, C * (j + 1))
            st = s_scr[...]
            st_ref[j] = st
            u = uv_ref[:, rows, :] - _bdot_raw(wk_ref[:, rows, :], st, "NN", ps)
            o = _bdot_raw(qd_ref[:, rows, :], st, "NN", ps) + _bdot_raw(at_ref[:, rows, :], u, "NN", ps)
            s_scr[...] = ge_ref[j][:, 0:1, 0:1] * st + _bdot_raw(ke_ref[:, rows, :], u, "TN", ps)
            for h in range(GDN_HEADS):
                cols = slice(GDN_DIM * h, GDN_DIM * (h + 1))
                oa_ref[rows, cols] = _gated_norm(o[h], z_ref[rows, cols], gnw_ref[...])

    return pl.pallas_call(
        body, name="gdn_scan", grid=(nc // G,),
        in_specs=[sp["hd"], sp["hd"], sp["hd"], sp["hd"], sp["hc"], sp["ge"], sp["z"], sp["vec"]] + [_ANY] * (1 + len(after)),
        out_specs=(sp["oa"], sp["st"]),
        out_shape=(jax.ShapeDtypeStruct(mix.shape, F32),
                   jax.ShapeDtypeStruct((nc, GDN_HEADS, GDN_DIM, GDN_DIM), F32)),
        input_output_aliases={8: 0},
        scratch_shapes=[pltpu.VMEM((GDN_HEADS, GDN_DIM, GDN_DIM), F32)],
        compiler_params=_cparams(("arbitrary",)),
    )(u_v, w_k, q_dec, k_end, attn, g_end, proj, gnw, mix, *after)


def _gdn_scan_bwd(u_v, w_k, q_dec, k_end, attn, g_end, proj, gnw, states, d_oa):
    S = proj.shape[0]
    C, G = GDN_CHUNK, GDN_SG
    nc = S // C
    sp = _gdn_scan_specs(S, G, True)
    ps, pb = _GDN_PASSES["scan"], _GDN_PASSES["bwd"]

    def body(uv_ref, wk_ref, qd_ref, ke_ref, at_ref, ge_ref, z_ref, gnw_ref, st_ref, doa_ref,
             duv_ref, dwk_ref, dqd_ref, dke_ref, dat_ref, dge_ref, dz_ref, dgnw_ref, ds_scr):
        @pl.when(pl.program_id(0) == 0)
        def _():
            ds_scr[...] = jnp.zeros_like(ds_scr)
            dgnw_ref[...] = jnp.zeros_like(dgnw_ref)

        dgnw = jnp.zeros((1, 128), F32)
        for j in reversed(range(G)):
            rows = slice(C * j, C * (j + 1))
            st = st_ref[j]
            wk, qd, ke, at = wk_ref[:, rows, :], qd_ref[:, rows, :], ke_ref[:, rows, :], at_ref[:, rows, :]
            u = uv_ref[:, rows, :] - _bdot_raw(wk, st, "NN", ps)
            o = _bdot_raw(qd, st, "NN", ps) + _bdot_raw(at, u, "NN", ps)
            dos = []
            for h in range(GDN_HEADS):
                cols = slice(GDN_DIM * h, GDN_DIM * (h + 1))
                _, vjp2 = jax.vjp(_gated_norm, o[h], z_ref[rows, cols], gnw_ref[...])
                do_h, dz_h, dgn = vjp2(doa_ref[rows, cols])
                dz_ref[rows, cols] = dz_h
                dgnw = dgnw + dgn
                dos.append(do_h)
            do = jnp.stack(dos)
            ds_new = ds_scr[...]
            du = _bdot_raw(at, do, "TN", pb) + _bdot_raw(ke, ds_new, "NN", pb)
            duv_ref[:, rows, :] = du
            dat_ref[:, rows, :] = _bdot_raw(do, u, "NT", pb)
            dqd_ref[:, rows, :] = _bdot_raw(do, st, "NT", pb)
            dke_ref[:, rows, :] = _bdot_raw(u, ds_new, "NT", pb)
            dwk_ref[:, rows, :] = -_bdot_raw(du, st, "NT", pb)
            d_ge = jnp.sum(jnp.sum(st * ds_new, axis=2, keepdims=True), axis=1, keepdims=True)
            dge_ref[j] = jnp.broadcast_to(d_ge, (GDN_HEADS, 8, 128))
            ds_scr[...] = ge_ref[j][:, 0:1, 0:1] * ds_new + _bdot_raw(qd, do, "TN", pb) - _bdot_raw(wk, du, "TN", pb)
        dgnw_ref[...] += dgnw

    return pl.pallas_call(
        body, name="gdn_scan_bwd", grid=(nc // G,),
        in_specs=[sp["hd"], sp["hd"], sp["hd"], sp["hd"], sp["hc"], sp["ge"], sp["z"], sp["vec"], sp["st"], sp["oa"]],
        out_specs=(sp["hd"], sp["hd"], sp["hd"], sp["hd"], sp["hc"], sp["ge"], sp["oa"], sp["vec"]),
        out_shape=(_hd_shape(S), _hd_shape(S), _hd_shape(S), _hd_shape(S), _hd_shape(S, C),
                   jax.ShapeDtypeStruct((nc, GDN_HEADS, 8, 128), F32), jax.ShapeDtypeStruct((S, GDN_WIDTH), F32),
                   jax.ShapeDtypeStruct((1, 128), F32)),
        scratch_shapes=[pltpu.VMEM((GDN_HEADS, GDN_DIM, GDN_DIM), F32)],
        compiler_params=_cparams(("arbitrary",)),
    )(u_v, w_k, q_dec, k_end, attn, g_end, proj, gnw, states, d_oa)


def _gdn_post(proj, conv_w, gp, tinv, u_v, w_k, d_uv, d_wk, d_qd, d_ke, d_at, d_ge):
    S = proj.shape[0]
    C, G = GDN_CHUNK, GDN_PG
    nc = S // C
    sp = _gdn_pre_specs(S, G)
    pb = _GDN_PASSES["bwd"]

    def body(cur_ref, prev_ref, ba_ref, cw_ref, gp_ref, ti_ref, uv_ref, wk_ref, duv_ref, dwk_ref, dqd_ref, dke_ref,
             dat_ref, dge_ref, dpre_ref, dba_ref, dgp_ref):
        i = pl.program_id(0)

        @pl.when(i == 0)
        def _():
            dgp_ref[...] = jnp.zeros_like(dgp_ref)

        prev = prev_ref[...] * jnp.where(i == 0, 0.0, 1.0)
        pre = _conv_taps(jnp.concatenate([prev, cur_ref[...]], axis=0), cw_ref[...], GDN_CONV, C * G)
        sg = jax.nn.sigmoid(pre)
        dsilu = sg * (1.0 + pre * (1.0 - sg))
        pairs, args = _gdn_pairs(pre * sg, ba_ref[...], gp_ref[...], G)
        _, vjp1 = jax.vjp(functools.partial(_gdn_stage1, dot=_bdot), *args)

        def take(ref):
            return jnp.stack([ref[h, C * j:C * (j + 1), :] for j, h in pairs])

        t, u_v, w_k = take(ti_ref), take(uv_ref), take(wk_ref)
        d_v = _bdot_raw(t, take(duv_ref), "TN", pb)
        d_rk = _bdot_raw(t, take(dwk_ref), "TN", pb)
        d_l = -(_bdot_raw(d_v, u_v, "NT", pb) + _bdot_raw(d_rk, w_k, "NT", pb))
        d_ge = jnp.stack([dge_ref[j, h][0:1, 0:1] for j, h in pairs])
        dcq, dck, dcv, db, da, dalog, ddtb = vjp1((d_l, d_v, d_rk, take(dqd_ref), take(dat_ref), take(dke_ref), d_ge))
        lane = lax.broadcasted_iota(jnp.int32, (C, 128), 1)
        lane1 = lax.broadcasted_iota(jnp.int32, (1, 128), 1)
        dgp = jnp.zeros((1, 128), F32)
        for j in range(G):
            rows = slice(C * j, C * (j + 1))
            dba = jnp.zeros((C, 128), F32)
            for h in range(GDN_HEADS):
                b = GDN_HEADS * j + h
                for o_, dcx in ((0, dcq), (GDN_WIDTH, dck), (2 * GDN_WIDTH, dcv)):
                    cols = slice(o_ + GDN_DIM * h, o_ + GDN_DIM * (h + 1))
                    dpre_ref[rows, cols] = dcx[b] * dsilu[rows, cols]
                dba = dba + jnp.where(lane == h, db[b], 0.0) + jnp.where(lane == GDN_HEADS + h, da[b], 0.0)
                dgp = dgp + jnp.where(lane1 == h, dalog[b], 0.0) + jnp.where(lane1 == GDN_HEADS + h, ddtb[b], 0.0)
            dba_ref[rows, :] = dba
        dgp_ref[0:1, :] += dgp

    T = C * G
    return pl.pallas_call(
        body, name="gdn_post", grid=(nc // G,),
        in_specs=[sp["cur"], sp["prev"], sp["ba"], sp["cw"], sp["vec"], sp["hc"], sp["hd"], sp["hd"], sp["hd"], sp["hd"],
                  sp["hd"], sp["hd"], sp["hc"], sp["ge"]],
        out_specs=(sp["cur"], pl.BlockSpec((T, 128), lambda i: (i, 0)), pl.BlockSpec((8, 128), lambda i: (0, 0))),
        out_shape=(jax.ShapeDtypeStruct((S, 3 * GDN_WIDTH), F32), jax.ShapeDtypeStruct((S, 128), F32),
                   jax.ShapeDtypeStruct((8, 128), F32)),
        compiler_params=_cparams(("arbitrary",)),
    )(proj, proj, proj, conv_w, gp, tinv, u_v, w_k, d_uv, d_wk, d_qd, d_ke, d_at, d_ge)


def _conv_bwd(dpre, x, xcol0, w, K, name, tc):
    S, Cc = dpre.shape
    T = _pick_tile(S, 256)
    nt, ncol = S // T, Cc // tc
    xo = xcol0 // tc

    def body(d_ref, dn_ref, x_ref, xp_ref, w_ref, dx_ref, dw_ref):
        i = pl.program_id(1)
        dn = dn_ref[...] * jnp.where(i == nt - 1, 0.0, 1.0)
        dv = d_ref[...]
        ext_d = jnp.concatenate([dv, dn], axis=0)
        dx_ref[...] = _conv_taps_t(ext_d, w_ref[...], K, T).astype(dx_ref.dtype)
        xp = xp_ref[...] * jnp.where(i == 0, 0.0, 1.0)
        ext_x = jnp.concatenate([xp, x_ref[...]], axis=0)

        @pl.when(i == 0)
        def _():
            dw_ref[...] = jnp.zeros_like(dw_ref)

        for k in range(K):
            dw_ref[k:k + 1, :] += jnp.sum(dv * _shifted(ext_x, (K - 1) - k, 8, T), axis=0, keepdims=True)

    r8 = T // 8
    return pl.pallas_call(
        body, name=name, grid=(ncol, nt),
        in_specs=[pl.BlockSpec((T, tc), lambda j, i: (i, j)),
                  pl.BlockSpec((8, tc), lambda j, i: (jnp.minimum((i + 1) * r8, S // 8 - 1), j)),
                  pl.BlockSpec((T, tc), lambda j, i: (i, j + xo)),
                  pl.BlockSpec((8, tc), lambda j, i: (jnp.maximum(i * r8 - 1, 0), j + xo)),
                  pl.BlockSpec((K, tc), lambda j, i: (0, j))],
        out_specs=(pl.BlockSpec((T, tc), lambda j, i: (i, j)), pl.BlockSpec((K, tc), lambda j, i: (0, j))),
        out_shape=(jax.ShapeDtypeStruct((S, Cc), _MXU), jax.ShapeDtypeStruct((K, Cc), F32)),
        compiler_params=_cparams(("parallel", "arbitrary")),
    )(dpre, dpre, x, x, w)


def _dil_bias(nt, T):
    d = (np.arange(nt)[:, None, None] * T + np.arange(T)[None, None, :] - np.arange(T)[None, :, None])
    cnt = ((d >= 0) & (d <= 128)).astype(np.float64) + ((d >= 0) & (d % 4 == 0) & (d <= 512)) + ((d >= 0) & (d % 16 == 0))
    return jnp.asarray(np.where(cnt > 0, np.log(np.maximum(cnt, 1.0)), -1e30), dtype=F32)


def _attn_fwd(proj, after=()):
    S = proj.shape[0]
    T = min(ATT_T, S)
    nt = S // T
    bias = _dil_bias(nt, T)
    scale = DIL_DIM ** -0.5
    npair = DIL_WIDTH // 128
    qb0, kb0, vb0 = P_QKVB // 128, (P_QKVB + DIL_WIDTH) // 128, (P_QKVB + 2 * DIL_WIDTH) // 128

    def body(q_ref, k_ref, v_ref, b_ref, *rest):
        o_ref, lse_ref = rest[-2:]
        i = pl.program_id(1)
        qs = (q_ref[...] * scale).astype(_MXU)

        def step(j, carry):
            kt = k_ref[pl.ds(pl.multiple_of(j * T, T), T), :].astype(_MXU)
            vt = v_ref[pl.ds(pl.multiple_of(j * T, T), T), :].astype(_MXU)
            bt = b_ref[i - j]
            out = []
            for hh in range(2):
                m, l, acc = carry[hh]
                sl = slice(hh * DIL_DIM, (hh + 1) * DIL_DIM)
                s = lax.dot_general(kt[:, sl], qs[:, sl], (_NT, ((), ())), preferred_element_type=F32) + bt
                m_new = jnp.maximum(m, jnp.max(s, axis=0, keepdims=True))
                p = jnp.exp(s - m_new)
                a = jnp.exp(m - m_new)
                l = a * l + jnp.sum(p, axis=0, keepdims=True)
                acc = a * acc + lax.dot_general(vt[:, sl], p.astype(_MXU), (_TN, ((), ())), preferred_element_type=F32)
                out.append((m_new, l, acc))
            return tuple(out)

        init = tuple((jnp.full((1, T), -1e30, F32), jnp.zeros((1, T), F32), jnp.zeros((DIL_DIM, T), F32)) for _ in range(2))
        res = lax.fori_loop(0, i + 1, step, init)
        lse_ref[...] = jnp.zeros_like(lse_ref)
        for hh in range(2):
            m, l, acc = res[hh]
            o_ref[:, hh * DIL_DIM:(hh + 1) * DIL_DIM] = (acc / l).T
            lse_ref[hh:hh + 1, :] = m + jnp.log(l)

    return pl.pallas_call(
        body, name="attn_fwd", grid=(npair, nt),
        in_specs=[pl.BlockSpec((T, 128), lambda p, i: (i, qb0 + p)),
                  pl.BlockSpec((S, 128), lambda p, i: (0, kb0 + p)),
                  pl.BlockSpec((S, 128), lambda p, i: (0, vb0 + p)),
                  pl.BlockSpec((nt, T, T), lambda p, i: (0, 0, 0))] + [_ANY] * len(after),
        out_specs=(pl.BlockSpec((T, 128), lambda p, i: (i, GDN_WIDTH // 128 + p)),
                   pl.BlockSpec((None, None, 8, T), lambda p, i: (p, i, 0, 0))),
        out_shape=(jax.ShapeDtypeStruct((S, GDN_WIDTH + DIL_WIDTH), F32), jax.ShapeDtypeStruct((npair, nt, 8, T), F32)),
        compiler_params=_cparams(("parallel", "parallel")),
    )(proj, proj, proj, bias, *after)


def _attn_bwd(proj, mix, lse, d_mix):
    S = proj.shape[0]
    T = min(ATT_T, S)
    nt = S // T
    bias = _dil_bias(nt, T)
    scale = DIL_DIM ** -0.5
    npair = DIL_WIDTH // 128
    qb0, kb0, vb0 = P_QKVB // 128, (P_QKVB + DIL_WIDTH) // 128, (P_QKVB + 2 * DIL_WIDTH) // 128

    def body(q_ref, k_ref, v_ref, o_ref, lse_ref, do_ref, b_ref, dq_ref, dk_ref, dv_ref, dq_scr):
        j = pl.program_id(1)

        @pl.when(j == 0)
        def _():
            dq_scr[...] = jnp.zeros_like(dq_scr)

        kt = k_ref[...].astype(_MXU)
        vt = v_ref[...].astype(_MXU)
        ones = jnp.ones((8, DIL_DIM), F32)

        def step(i, carry):
            rows = pl.ds(pl.multiple_of(i * T, T), T)
            qs = (q_ref[rows, :] * scale).astype(_MXU)
            dov = do_ref[rows, :]
            prod = dov * o_ref[rows, :]
            lsev = lse_ref[i]
            dob = dov.astype(_MXU)
            bt = b_ref[i - j]
            out = []
            dqs = []
            for hh in range(2):
                dk, dv = carry[hh]
                sl = slice(hh * DIL_DIM, (hh + 1) * DIL_DIM)
                s = lax.dot_general(kt[:, sl], qs[:, sl], (_NT, ((), ())), preferred_element_type=F32) + bt
                p = jnp.exp(s - lsev[hh:hh + 1, :])
                delta = lax.dot_general(ones, prod[:, sl], (_NT, ((), ())), precision=_HI, preferred_element_type=F32)[0:1, :]
                dp = lax.dot_general(vt[:, sl], dob[:, sl], (_NT, ((), ())), preferred_element_type=F32)
                ds = (p * (dp - delta)).astype(_MXU)
                dv = dv + lax.dot_general(p.astype(_MXU), dob[:, sl], (_NN, ((), ())), preferred_element_type=F32)
                dk = dk + lax.dot_general(ds, qs[:, sl], (_NN, ((), ())), preferred_element_type=F32)
                dqs.append(lax.dot_general(ds, kt[:, sl], (_TN, ((), ())), preferred_element_type=F32) * scale)
                out.append((dk, dv))
            dq_scr[rows, :] += jnp.concatenate(dqs, axis=1)
            return tuple(out)

        init = tuple((jnp.zeros((T, DIL_DIM), F32), jnp.zeros((T, DIL_DIM), F32)) for _ in range(2))
        res = lax.fori_loop(j, nt, step, init)
        dk_ref[...] = jnp.concatenate([res[0][0], res[1][0]], axis=1).astype(dk_ref.dtype)
        dv_ref[...] = jnp.concatenate([res[0][1], res[1][1]], axis=1).astype(dv_ref.dtype)

        @pl.when(j == nt - 1)
        def _():
            dq_ref[...] = dq_scr[...].astype(dq_ref.dtype)

    full = lambda c0: pl.BlockSpec((S, 128), lambda p, j: (0, c0 + p))
    tile = lambda c0: pl.BlockSpec((T, 128), lambda p, j: (j, c0 + p))
    out3 = jax.ShapeDtypeStruct((S, DIL_WIDTH), _MXU)
    return pl.pallas_call(
        body, name="attn_bwd", grid=(npair, nt),
        in_specs=[full(qb0), tile(kb0), tile(vb0), full(GDN_WIDTH // 128),
                  pl.BlockSpec((None, nt, 8, T), lambda p, j: (p, 0, 0, 0)), full(GDN_WIDTH // 128),
                  pl.BlockSpec((nt, T, T), lambda p, j: (0, 0, 0))],
        out_specs=(full(0), tile(0), tile(0)),
        out_shape=(out3, out3, out3),
        scratch_shapes=[pltpu.VMEM((S, 128), F32)],
        compiler_params=_cparams(("parallel", "arbitrary")),
    )(proj, proj, proj, mix, lse, d_mix, bias)


def _ffn_act(up, cw):
    S, Cc = up.shape[0], up.shape[1] // 2
    T, tc = _pick_tile(S, 256), _pick_tile(Cc, 1536)
    r16 = T // 16
    nct = Cc // tc

    def body(g_ref, gp_ref, u_ref, up_ref, wg_ref, wu_ref, o_ref):
        keep = jnp.where(pl.program_id(1) == 0, 0.0, 1.0)
        cg = _conv_taps(jnp.concatenate([gp_ref[8:16, :].astype(F32) * keep, g_ref[...].astype(F32)], axis=0),
                        wg_ref[...], FFN_CONV, T)
        cu = _conv_taps(jnp.concatenate([up_ref[8:16, :].astype(F32) * keep, u_ref[...].astype(F32)], axis=0),
                        wu_ref[...], FFN_CONV, T)
        o_ref[...] = (_silu(cg) * cu).astype(o_ref.dtype)

    cur = lambda o: pl.BlockSpec((T, tc), lambda j, i: (i, j + o))
    prev = lambda o: pl.BlockSpec((16, tc), lambda j, i: (jnp.maximum(i * r16 - 1, 0), j + o))
    wsp = lambda o: pl.BlockSpec((FFN_CONV, tc), lambda j, i: (0, j + o))
    return pl.pallas_call(
        body, name="ffn_act", grid=(nct, S // T),
        in_specs=[cur(0), prev(0), cur(nct), prev(nct), wsp(0), wsp(nct)], out_specs=cur(0),
        out_shape=jax.ShapeDtypeStruct((S, Cc), _MXU),
        compiler_params=_cparams(("parallel", "parallel")),
    )(up, up, up, up, cw, cw)


def _ffn_act_bwd(d_act, up, cw):
    S, Cc = up.shape[0], up.shape[1] // 2
    T, tc = _pick_tile(S, 256), _pick_tile(Cc, 1536)
    r8, r16 = T // 8, T // 16
    nt = S // T
    nct = Cc // tc
    K = FFN_CONV

    def body(da_ref, dan_ref, g_ref, gp_ref, gn_ref, u_ref, up_ref, un_ref, wg_ref, wu_ref,
             dg_ref, du_ref, dwg_ref, dwu_ref):
        i = pl.program_id(1)
        keep_p = jnp.where(i == 0, 0.0, 1.0)
        keep_n = jnp.where(i == nt - 1, 0.0, 1.0)
        wg, wu = wg_ref[...], wu_ref[...]
        xg = jnp.concatenate([gp_ref[8:16, :].astype(F32) * keep_p, g_ref[...].astype(F32),
                              gn_ref[0:8, :].astype(F32) * keep_n], axis=0)
        xu = jnp.concatenate([up_ref[8:16, :].astype(F32) * keep_p, u_ref[...].astype(F32),
                              un_ref[0:8, :].astype(F32) * keep_n], axis=0)
        cg = _conv_taps(xg, wg, K, T + 8)
        cu = _conv_taps(xu, wu, K, T + 8)
        da = jnp.concatenate([da_ref[...], dan_ref[...] * keep_n], axis=0)
        sg = jax.nn.sigmoid(cg)
        d_cg = da * cu * (sg * (1.0 + cg * (1.0 - sg)))
        d_cu = da * (cg * sg)
        dg_ref[...] = _conv_taps_t(d_cg, wg, K, T).astype(dg_ref.dtype)
        du_ref[...] = _conv_taps_t(d_cu, wu, K, T).astype(du_ref.dtype)

        @pl.when(i == 0)
        def _():
            dwg_ref[...] = jnp.zeros_like(dwg_ref)
            dwu_ref[...] = jnp.zeros_like(dwu_ref)

        for k in range(K):
            dwg_ref[k:k + 1, :] += jnp.sum(d_cg[0:T, :] * _shifted(xg, (K - 1) - k, 8, T), axis=0, keepdims=True)
            dwu_ref[k:k + 1, :] += jnp.sum(d_cu[0:T, :] * _shifted(xu, (K - 1) - k, 8, T), axis=0, keepdims=True)

    cur = lambda o: pl.BlockSpec((T, tc), lambda j, i: (i, j + o))
    prev = lambda o: pl.BlockSpec((16, tc), lambda j, i: (jnp.maximum(i * r16 - 1, 0), j + o))
    nxt = lambda o: pl.BlockSpec((16, tc), lambda j, i: (jnp.minimum((i + 1) * r16, S // 16 - 1), j + o))
    nxt8 = pl.BlockSpec((8, tc), lambda j, i: (jnp.minimum((i + 1) * r8, S // 8 - 1), j))
    wsp = lambda o: pl.BlockSpec((K, tc), lambda j, i: (0, j + o))
    return pl.pallas_call(
        body, name="ffn_act_bwd", grid=(nct, nt),
        in_specs=[cur(0), nxt8, cur(0), prev(0), nxt(0), cur(nct), prev(nct), nxt(nct), wsp(0), wsp(nct)],
        out_specs=(cur(0), cur(0), wsp(0), wsp(0)),
        out_shape=(jax.ShapeDtypeStruct((S, Cc), _MXU), jax.ShapeDtypeStruct((S, Cc), _MXU),
                   jax.ShapeDtypeStruct((K, Cc), F32), jax.ShapeDtypeStruct((K, Cc), F32)),
        compiler_params=_cparams(("parallel", "arbitrary")),
    )(d_act, d_act, up, up, up, up, up, up, cw, cw)


def _local_step(x, tgt, h1, n1w, n2w, fnw, gp, gnw, wp, conv_w, fcw, rest_weights, early_grads):
    proj = _mm(h1, wp, "nn", name="proj")
    u_v, w_k, q_dec, k_end, attn, tinv, g_end = _gdn_pre(proj, conv_w, gp)
    mix, lse = _attn_fwd(proj)
    mix, states = _gdn_scan(u_v, w_k, q_dec, k_end, attn, g_end, proj, gnw, mix, after=[rest_weights[0]([mix])])
    w_out, w_up4, w_down = rest_weights[1]([mix])
    x2 = _mm(mix, w_out, "nn", residual=x, name="outproj")
    h2 = _rmsnorm_fwd(x2, n2w, "norm2")
    up = _mm(h2, w_up4, "nn", b_blocks=True, out_dtype=_MXU, name="up")
    act = _ffn_act(up, fcw)
    x3 = _mm(act, w_down, "nn", residual=x2, name="down")
    loss, dx3, dx3n, d_fnw = _loss_head(x3, fnw, tgt, "loss_head")
    d_act = _mm(dx3n, w_down, "nt", name="d_act")
    d_wdown = _mm(act, dx3n, "tn", name="d_wdown")
    d_upg, d_upu, d_fcwg, d_fcwu = _ffn_act_bwd(d_act, up, fcw)
    d_wup = _mm(h2, d_upg, "tn", place=("blocks", N_CHIPS, 0), tn=w_up4.shape[2], name="d_wgate")
    d_wup = _mm(h2, d_upu, "tn", place=("blocks", N_CHIPS, N_CHIPS // 2), tn=w_up4.shape[2], into=d_wup, name="d_wup")
    d_h2 = _mm_nt_blocks([d_upg, d_upu], w_up4, "d_h2")
    dx2, d_n2w = _rmsnorm_bwd(d_h2, x2, n2w, dx3, "norm2_bwd")
    d_wout = _mm(mix, dx2, "tn", name="d_wout")
    token = early_grads[0](d_wup, d_wdown, d_wout)
    d_mix = _mm(dx2, w_out, "nt", name="d_mix", after=[token])
    dq_b, dk_b, dv_b = _attn_bwd(proj, mix, lse, d_mix)
    d_uv, d_wk, d_qd, d_ke, d_at, d_ge, d_z, d_gnw = _gdn_scan_bwd(u_v, w_k, q_dec, k_end, attn, g_end, proj,
                                                                   gnw, states, d_mix)
    d_pre, d_ba, d_gp = _gdn_post(proj, conv_w, gp, tinv, u_v, w_k, d_uv, d_wk, d_qd, d_ke, d_at, d_ge)
    token = early_grads[1]([d_pre])
    d_qkva, d_convw = _conv_bwd(d_pre, proj, 0, conv_w + token[0:1, 0:1], GDN_CONV, "gdn_conv_bwd", 512)
    d_proj = jnp.concatenate([d_qkva, d_z.astype(_MXU), dq_b, dk_b, dv_b, d_ba.astype(_MXU),
                              jnp.zeros((x.shape[0], P_COLS - P_BA - 128), _MXU)], axis=1)
    d_wp = _mm(h1, d_proj, "tn", name="d_wp")
    token = early_grads[2](d_wp)
    d_h1 = _mm(d_proj, wp, "nt", name="d_h1", after=[token])
    token = early_grads[3]([d_h1])
    dx, d_n1w = _rmsnorm_bwd(d_h1, x, n1w, dx2, "norm1_bwd", after=[token])
    grads = dict(wp=d_wp, conv_w=d_convw, w_out=d_wout, w_up=d_wup, fcw_g=d_fcwg, fcw_u=d_fcwu, w_down=d_wdown,
                 n1w=d_n1w, n2w=d_n2w, fnw=d_fnw, gp=d_gp, gnw=d_gnw)
    return loss, dx, grads


_HBM = pl.BlockSpec(memory_space=pltpu.HBM)


def _pos():
    return lax.axis_index("x"), lax.axis_index("y"), lax.axis_index("c")


def _other_chips(x, y):
    return [(1 - x, y), (x, 1 - y), (1 - x, 1 - y)]


def _halvable(shape):
    return shape[0] % 32 == 0


def _rows_of_half(shape, half):
    if not _halvable(shape):
        return pl.ds(0, shape[0])
    return pl.ds(pl.multiple_of(half * (shape[0] // 2), 16), shape[0] // 2)


_SEM = pl.BlockSpec(memory_space=pltpu.SEMAPHORE)
_ANY = pl.BlockSpec(memory_space=pl.ANY)
_DATAFLOW = pltpu.SideEffectType.DATAFLOW_SIDE_EFFECTING


def _in_hbm(a):
    return pltpu.with_memory_space_constraint(a, pltpu.HBM)


def _halves_copy(src_refs, land_refs, send_sems, recv_sems, shapes, a, j, block, x, y, c):
    px, py = _other_chips(x, y)[j]
    rows = _rows_of_half(shapes[a], c)
    return pltpu.make_async_remote_copy(
        src_ref=src_refs[a].at[rows, :], dst_ref=land_refs[a].at[block, rows, :], send_sem=send_sems.at[3 * a + j],
        recv_sem=recv_sems.at[3 * a + j], device_id=(px, py, c), device_id_type=MESH)


def _gather_halves_start(shards, after, name):
    n = len(shards)
    shapes = [s.shape for s in shards]

    def body(*refs):
        ins, lands = refs[:n], refs[n:2 * n]
        send_sems, recv_sems = refs[2 * n + 1], refs[2 * n + 2]
        token = refs[-1]
        x, y, c = _pos()
        q = 2 * x + y
        for a in range(n):
            for j in range(3):
                _halves_copy(ins, lands, send_sems, recv_sems, shapes, a, j, q, x, y, c).start()
        token[...] = jnp.zeros_like(token)

    land_shapes = [(N_CHIPS,) + s.shape for s in shards]
    return pl.pallas_call(
        body, name=name,
        out_shape=(pltpu.SemaphoreType.DMA((3 * n,)), pltpu.SemaphoreType.DMA((3 * n,)),
                   *[pltpu.HBM(s.shape, s.dtype) for s in shards],
                   *[pltpu.HBM(ls, s.dtype) for ls, s in zip(land_shapes, shards)],
                   jax.ShapeDtypeStruct((8, 128), F32)),
        in_specs=[_HBM] * (2 * n) + [_ANY],
        out_specs=(_SEM, _SEM, *[_HBM] * (2 * n), pl.BlockSpec(memory_space=pltpu.VMEM)),
        input_output_aliases={a: 2 + a for a in range(2 * n)},
        compiler_params=pltpu.CompilerParams(has_side_effects=_DATAFLOW),
    )(*[_in_hbm(s) for s in shards], *[_in_hbm(lax.empty(ls, s.dtype)) for ls, s in zip(land_shapes, shards)], after)


def _gather_halves_wait(started, after, name):
    send_sems, recv_sems, *thru = started
    n = len(thru) // 2
    shapes = [t.shape for t in thru[:n]]

    def body(*refs):
        ins, lands = refs[:n], refs[n:2 * n]
        send_sems, recv_sems = refs[2 * n], refs[2 * n + 1]
        x, y, c = _pos()
        q = 2 * x + y
        chips = _other_chips(x, y)
        for a in range(n):
            for j, (px, py) in enumerate(chips):
                _halves_copy(ins, lands, send_sems, recv_sems, shapes, a, j, q, x, y, c).wait_send()
                _halves_copy(ins, lands, send_sems, recv_sems, shapes, a, j, 2 * px + py, x, y, c).wait_recv()

    outs = pl.pallas_call(
        body, name=name, out_shape=[pltpu.HBM(t.shape, t.dtype) for t in thru],
        in_specs=[_HBM] * (2 * n) + [_SEM, _SEM] + [_ANY] * len(after), out_specs=[_HBM] * (2 * n),
        input_output_aliases={a: a for a in range(2 * n)},
        compiler_params=pltpu.CompilerParams(has_side_effects=_DATAFLOW),
    )(*thru, send_sems, recv_sems, *after)
    return outs[:n], outs[n:]


def _sibling_fill(gathered, name):
    big = [a for a, g in enumerate(gathered) if _halvable(g.shape[1:])]
    n = len(gathered)

    def body(*refs):
        ins, outs = refs[:n], refs[n:2 * n]
        send_sems, recv_sems = refs[2 * n:]
        x, y, c = _pos()
        chips = _other_chips(x, y)

        def copy(k, j, half):
            a = big[k]
            px, py = chips[j]
            rows = _rows_of_half(gathered[a].shape[1:], half)
            return pltpu.make_async_remote_copy(
                src_ref=ins[a].at[2 * px + py, rows, :], dst_ref=outs[a].at[2 * px + py, rows, :],
                send_sem=send_sems.at[3 * k + j], recv_sem=recv_sems.at[3 * k + j],
                device_id=(x, y, 1 - c), device_id_type=MESH)

        sends = [copy(k, j, c) for k in range(len(big)) for j in range(3)]
        for cp in sends:
            cp.start()
        for k in range(len(big)):
            for j in range(3):
                copy(k, j, 1 - c).wait_recv()
        for cp in sends:
            cp.wait_send()

    return pl.pallas_call(
        body, name=name, in_specs=[_HBM] * n, out_specs=[_HBM] * n,
        out_shape=[jax.ShapeDtypeStruct(g.shape, g.dtype) for g in gathered],
        input_output_aliases={a: a for a in range(n)},
        scratch_shapes=[pltpu.SemaphoreType.DMA((3 * len(big),)), pltpu.SemaphoreType.DMA((3 * len(big),))],
    )(*gathered)


def _fill_copy(refs, send_sems, recv_sems, shapes, a, j, half, x, y, c):
    px, py = _other_chips(x, y)[j]
    rows = _rows_of_half(shapes[a], half)
    return pltpu.make_async_remote_copy(
        src_ref=refs[a].at[2 * px + py, rows, :], dst_ref=refs[a].at[2 * px + py, rows, :],
        send_sem=send_sems.at[3 * a + j], recv_sem=recv_sems.at[3 * a + j],
        device_id=(x, y, 1 - c), device_id_type=MESH)


def _sibling_fill_start(gathered, name):
    n = len(gathered)
    shapes = [g.shape[1:] for g in gathered]

    def body(*refs):
        ins = refs[:n]
        send_sems, recv_sems = refs[n], refs[n + 1]
        token = refs[-1]
        x, y, c = _pos()
        for a in range(n):
            for j in range(3):
                _fill_copy(ins, send_sems, recv_sems, shapes, a, j, c, x, y, c).start()
        token[...] = jnp.zeros_like(token)

    return pl.pallas_call(
        body, name=name,
        out_shape=(pltpu.SemaphoreType.DMA((3 * n,)), pltpu.SemaphoreType.DMA((3 * n,)),
                   *[pltpu.HBM(g.shape, g.dtype) for g in gathered], jax.ShapeDtypeStruct((8, 128), F32)),
        in_specs=[_HBM] * n,
        out_specs=(_SEM, _SEM, *[_HBM] * n, pl.BlockSpec(memory_space=pltpu.VMEM)),
        input_output_aliases={a: 2 + a for a in range(n)},
        compiler_params=pltpu.CompilerParams(has_side_effects=_DATAFLOW),
    )(*[_in_hbm(g) for g in gathered])


def _sibling_fill_wait(started, after, name):
    send_sems, recv_sems, *thru = started
    n = len(thru)
    shapes = [t.shape[1:] for t in thru]

    def body(*refs):
        ins = refs[:n]
        send_sems, recv_sems = refs[n], refs[n + 1]
        x, y, c = _pos()
        for a in range(n):
            for j in range(3):
                _fill_copy(ins, send_sems, recv_sems, shapes, a, j, c, x, y, c).wait_send()
                _fill_copy(ins, send_sems, recv_sems, shapes, a, j, 1 - c, x, y, c).wait_recv()

    return pl.pallas_call(
        body, name=name, out_shape=[pltpu.HBM(t.shape, t.dtype) for t in thru],
        in_specs=[_HBM] * n + [_SEM, _SEM] + [_ANY] * len(after), out_specs=[_HBM] * n,
        input_output_aliases={a: a for a in range(n)},
        compiler_params=pltpu.CompilerParams(has_side_effects=_DATAFLOW),
    )(*thru, send_sems, recv_sems, *after)


def _place_own(shards, gathered, cq, name, carry=()):
    n = len(shards)
    nc = len(carry)
    steps = 4

    def body(cq_ref, *refs):
        for a in range(n):
            refs[2 * n + nc + a][...] = refs[a][...]

    def tile(shape):
        return shape[0] // steps if _halvable(shape) else shape[0]

    in_specs = [pl.BlockSpec((tile(s.shape), s.shape[1]), (lambda i, s_: (i, 0)) if _halvable(s.shape) else (lambda i, s_: (0, 0)))
                for s in shards]
    in_specs += [pl.BlockSpec(memory_space=pl.ANY)] * (n + nc)
    out_specs = [pl.BlockSpec((None, tile(s.shape), s.shape[1]),
                              (lambda i, s_: (s_[1], i, 0)) if _halvable(s.shape) else (lambda i, s_: (s_[1], 0, 0)))
                 for s in shards]
    out_specs += [pl.BlockSpec(memory_space=pl.ANY)] * nc
    gs = pltpu.PrefetchScalarGridSpec(num_scalar_prefetch=1, grid=(steps,), in_specs=in_specs, out_specs=out_specs)
    outs = pl.pallas_call(
        body, name=name, grid_spec=gs,
        out_shape=[jax.ShapeDtypeStruct(g.shape, g.dtype) for g in gathered] + [jax.ShapeDtypeStruct(t.shape, t.dtype) for t in carry],
        input_output_aliases={1 + n + a: a for a in range(n + nc)},
        compiler_params=_cparams(("arbitrary",)),
    )(cq, *shards, *gathered, *carry)
    return (outs[:n], outs[n:]) if nc else outs


def _half_rows(ref, c, rh):
    return ref.at[:, pl.ds(pl.multiple_of(c * rh, 8), rh), :]


def _chips_copy(src_refs, land_refs, send_sems, recv_sems, a, j, x, y, c):
    px, py = _other_chips(x, y)[j]
    return pltpu.make_async_remote_copy(src_ref=src_refs[a].at[2 * px + py], dst_ref=land_refs[a].at[j],
                                        send_sem=send_sems.at[3 * a + j], recv_sem=recv_sems.at[3 * a + j],
                                        device_id=(px, py, c), device_id_type=MESH)


def _peer(r, x, y, c):
    return (x if r & 4 == 0 else 1 - x), (y if r & 2 == 0 else 1 - y), (c if r & 1 == 0 else 1 - c)


def _small_copy(small_ref, all_ref, send_sems, recv_sems, base, r, slot, x, y, c):
    return pltpu.make_async_remote_copy(src_ref=small_ref, dst_ref=all_ref.at[slot], send_sem=send_sems.at[base + r - 1],
                                        recv_sem=recv_sems.at[base + r - 1], device_id=_peer(r, x, y, c), device_id_type=MESH)


def _grad_chips_start(parts, name, small=None):
    n = len(parts)
    srcs = list(parts) + ([] if small is None else [small])
    m = len(srcs)

    def body(*refs):
        ins, lands = refs[:m], refs[m:2 * m]
        send_sems, recv_sems = refs[2 * m], refs[2 * m + 1]
        token = refs[-1]
        x, y, c = _pos()
        for a in range(n):
            for j in range(3):
                _chips_copy(ins, lands, send_sems, recv_sems, a, j, x, y, c).start()
        if small is not None:
            for r in range(1, 8):
                _small_copy(ins[n], lands[n], send_sems, recv_sems, 3 * n, r, 4 * x + 2 * y + c, x, y, c).start()
        token[...] = jnp.zeros_like(token)

    land_shapes = [(3,) + p.shape[1:] for p in parts] + ([] if small is None else [(8,) + small.shape])
    nsem = 3 * n + (0 if small is None else 7)
    return pl.pallas_call(
        body, name=name,
        out_shape=(pltpu.SemaphoreType.DMA((nsem,)), pltpu.SemaphoreType.DMA((nsem,)),
                   *[pltpu.HBM(p.shape, p.dtype) for p in srcs],
                   *[pltpu.HBM(ls, p.dtype) for ls, p in zip(land_shapes, srcs)],
                   jax.ShapeDtypeStruct((8, 128), F32)),
        in_specs=[_HBM] * (2 * m),
        out_specs=(_SEM, _SEM, *[_HBM] * (2 * m), pl.BlockSpec(memory_space=pltpu.VMEM)),
        input_output_aliases={a: 2 + a for a in range(2 * m)},
        compiler_params=pltpu.CompilerParams(has_side_effects=_DATAFLOW),
    )(*[_in_hbm(p) for p in srcs], *[_in_hbm(lax.empty(ls, p.dtype)) for ls, p in zip(land_shapes, srcs)])


def _grad_chips_wait(started, after, name, with_small=False):
    send_sems, recv_sems, *thru = started
    m = len(thru) // 2
    n = m - (1 if with_small else 0)

    def body(*refs):
        ins, lands = refs[:m], refs[m:2 * m]
        send_sems, recv_sems = refs[2 * m], refs[2 * m + 1]
        x, y, c = _pos()
        for a in range(n):
            for j in range(3):
                cp = _chips_copy(ins, lands, send_sems, recv_sems, a, j, x, y, c)
                cp.wait_send()
                cp.wait_recv()
        if with_small:
            for r in range(1, 8):
                px, py, pc = _peer(r, x, y, c)
                _small_copy(ins[n], lands[n], send_sems, recv_sems, 3 * n, r, 4 * x + 2 * y + c, x, y, c).wait_send()
                _small_copy(ins[n], lands[n], send_sems, recv_sems, 3 * n, r, 4 * px + 2 * py + pc, x, y, c).wait_recv()

    outs = pl.pallas_call(
        body, name=name, out_shape=[pltpu.HBM(t.shape, t.dtype) for t in thru],
        in_specs=[_HBM] * (2 * m) + [_SEM, _SEM] + [_ANY] * len(after), out_specs=[_HBM] * (2 * m),
        input_output_aliases={a: a for a in range(2 * m)},
        compiler_params=pltpu.CompilerParams(has_side_effects=_DATAFLOW),
    )(*thru, send_sems, recv_sems, *after)
    return list(outs[m:]) + list(outs[n:m])


def _sibling_copy(src_refs, land_refs, send_sems, recv_sems, rhs, a, c, x, y):
    return pltpu.make_async_remote_copy(src_ref=_half_rows(src_refs[a], 1 - c, rhs[a]), dst_ref=land_refs[a],
                                        send_sem=send_sems.at[a], recv_sem=recv_sems.at[a],
                                        device_id=(x, y, 1 - c), device_id_type=MESH)


def _grad_sibling_start(fams, name):
    n = len(fams)
    rhs = [f.shape[1] // 2 for f in fams]

    def body(*refs):
        ins, lands = refs[:n], refs[n:2 * n]
        send_sems, recv_sems = refs[2 * n], refs[2 * n + 1]
        token = refs[-1]
        x, y, c = _pos()
        for a in range(n):
            _sibling_copy(ins, lands, send_sems, recv_sems, rhs, a, c, x, y).start()
        token[...] = jnp.zeros_like(token)

    land_shapes = [(f.shape[0], f.shape[1] // 2, f.shape[2]) for f in fams]
    return pl.pallas_call(
        body, name=name,
        out_shape=(pltpu.SemaphoreType.DMA((n,)), pltpu.SemaphoreType.DMA((n,)),
                   *[pltpu.HBM(f.shape, f.dtype) for f in fams],
                   *[pltpu.HBM(ls, f.dtype) for ls, f in zip(land_shapes, fams)],
                   jax.ShapeDtypeStruct((8, 128), F32)),
        in_specs=[_HBM] * (2 * n),
        out_specs=(_SEM, _SEM, *[_HBM] * (2 * n), pl.BlockSpec(memory_space=pltpu.VMEM)),
        input_output_aliases={a: 2 + a for a in range(2 * n)},
        compiler_params=pltpu.CompilerParams(has_side_effects=_DATAFLOW),
    )(*[_in_hbm(f) for f in fams], *[_in_hbm(lax.empty(ls, f.dtype)) for ls, f in zip(land_shapes, fams)])


def _grad_sibling_wait(started, after, name):
    send_sems, recv_sems, *thru = started
    n = len(thru) // 2
    rhs = [t.shape[1] // 2 for t in thru[:n]]

    def body(*refs):
        ins, lands = refs[:n], refs[n:2 * n]
        send_sems, recv_sems = refs[2 * n], refs[2 * n + 1]
        x, y, c = _pos()
        for a in range(n):
            cp = _sibling_copy(ins, lands, send_sems, recv_sems, rhs, a, c, x, y)
            cp.wait_send()
            cp.wait_recv()

    outs = pl.pallas_call(
        body, name=name, out_shape=[pltpu.HBM(t.shape, t.dtype) for t in thru],
        in_specs=[_HBM] * (2 * n) + [_SEM, _SEM] + [_ANY] * len(after), out_specs=[_HBM] * (2 * n),
        input_output_aliases={a: a for a in range(2 * n)},
        compiler_params=pltpu.CompilerParams(has_side_effects=_DATAFLOW),
    )(*thru, send_sems, recv_sems, *after)
    return outs[:n], outs[n:]


def _grad_share(fulls, name):
    n = len(fulls)
    rhs = [f.shape[0] // 2 for f in fulls]

    def body(*refs):
        ins, outs = refs[:n], refs[n:2 * n]
        send_sems, recv_sems = refs[2 * n], refs[2 * n + 1]
        x, y, c = _pos()

        def copy(a, half):
            rows = pl.ds(pl.multiple_of(half * rhs[a], 8), rhs[a])
            return pltpu.make_async_remote_copy(src_ref=ins[a].at[rows, :], dst_ref=outs[a].at[rows, :],
                                                send_sem=send_sems.at[a], recv_sem=recv_sems.at[a],
                                                device_id=(x, y, 1 - c), device_id_type=MESH)

        sends = [copy(a, c) for a in range(n)]
        for cp in sends:
            cp.start()
        for a in range(n):
            copy(a, 1 - c).wait_recv()
        for cp in sends:
            cp.wait_send()

    return pl.pallas_call(
        body, name=name, in_specs=[_HBM] * n, out_specs=[_HBM] * n,
        out_shape=[jax.ShapeDtypeStruct(f.shape, f.dtype) for f in fulls],
        input_output_aliases={a: a for a in range(n)},
        scratch_shapes=[pltpu.SemaphoreType.DMA((n,)), pltpu.SemaphoreType.DMA((n,))],
    )(*fulls)


def _add_sibling(own, recv, cq, name):
    nb, R, Cc = own.shape
    Rh = R // 2

    def body(cq_ref, a_ref, b_ref, o32_ref, o16_ref):
        s = a_ref[0] + b_ref[0]
        mine = pl.program_id(0) == cq_ref[1]

        @pl.when(mine)
        def _():
            o32_ref[...] = s

        @pl.when(jnp.logical_not(mine))
        def _():
            o16_ref[0] = s.astype(o16_ref.dtype)

    sp = pl.BlockSpec((1, Rh, Cc), lambda b, s: (b, 0, 0))
    gs = pltpu.PrefetchScalarGridSpec(
        num_scalar_prefetch=1, grid=(nb,),
        in_specs=[pl.BlockSpec((1, Rh, Cc), lambda b, s: (b, s[0], 0)), sp],
        out_specs=[pl.BlockSpec((Rh, Cc), lambda b, s: (0, 0)), sp])
    return pl.pallas_call(
        body, name=name, grid_spec=gs,
        out_shape=[jax.ShapeDtypeStruct((Rh, Cc), F32), jax.ShapeDtypeStruct((nb, Rh, Cc), _MXU)],
        compiler_params=_cparams(("arbitrary",)),
    )(cq, own, recv)


def _add_sibling_split(d_wp, recv, cq, name):
    _, Dm, Pc = d_wp.shape
    Rh = Dm // 2
    Wb = IN_COLS // N_CHIPS
    T = 256

    def body(cq_ref, a_ref, b_ref, o32_ref, o16_ref):
        s = a_ref[0] + b_ref[0]
        blocks = [s[:, 0:Wb], s[:, Wb:2 * Wb],
                  jnp.concatenate([s[:, 2 * Wb:P_QKVB], s[:, P_BA:P_BA + 8], s[:, P_QKVB:3 * Wb - 8]], axis=1),
                  s[:, 3 * Wb - 8:P_BA]]
        q = cq_ref[1]
        own = None
        for j, blk in enumerate(blocks):
            term = jnp.where(q == j, blk, 0.0)
            own = term if own is None else own + term
            o16_ref[j] = blk.astype(o16_ref.dtype)
        o32_ref[...] = own

    gs = pltpu.PrefetchScalarGridSpec(
        num_scalar_prefetch=1, grid=(Rh // T,),
        in_specs=[pl.BlockSpec((1, T, Pc), lambda i, s: (0, s[0] * (Rh // T) + i, 0)), pl.BlockSpec((1, T, Pc), lambda i, s: (0, i, 0))],
        out_specs=[pl.BlockSpec((T, Wb), lambda i, s: (i, 0)), pl.BlockSpec((N_CHIPS, T, Wb), lambda i, s: (0, i, 0))])
    return pl.pallas_call(
        body, name=name, grid_spec=gs,
        out_shape=[jax.ShapeDtypeStruct((Rh, Wb), F32), jax.ShapeDtypeStruct((N_CHIPS, Rh, Wb), _MXU)],
        compiler_params=_cparams(("parallel",)),
    )(cq, d_wp, recv)


def _add_chips(part32, recv3, cq, name):
    Rh, Cc = part32.shape

    def body(cq_ref, a_ref, b_ref, o_ref):
        acc = a_ref[...]
        for j in range(3):
            acc = acc + b_ref[j].astype(F32)
        o_ref[...] = acc

    gs = pltpu.PrefetchScalarGridSpec(
        num_scalar_prefetch=1, grid=(1,),
        in_specs=[pl.BlockSpec((Rh, Cc), lambda i, s: (0, 0)), pl.BlockSpec((3, Rh, Cc), lambda i, s: (0, 0, 0))],
        out_specs=pl.BlockSpec((Rh, Cc), lambda i, s: (s[0], 0)))
    return pl.pallas_call(
        body, name=name, grid_spec=gs, out_shape=jax.ShapeDtypeStruct((2 * Rh, Cc), F32),
        compiler_params=_cparams(("arbitrary",)),
    )(cq, part32, recv3)


def _transposed(g):
    Dm, n = g.shape
    pad = -n % 128

    def body(g_ref, o_ref):
        xp = jnp.concatenate([g_ref[...], jnp.zeros((Dm, pad), F32)], axis=1)
        o_ref[...] = xp.T[:n, :]

    return pl.pallas_call(body, name="transposed", out_shape=jax.ShapeDtypeStruct((n, Dm), F32),
                          compiler_params=_cparams(vmem=V7X_VMEM_LIMIT))(g)


def _adamw(w, g, m, v, name):
    R, Cc = w.shape
    T = max([t for t in range(8, 257, 8) if R % t == 0], default=R)

    def body(w_ref, g_ref, m_ref, v_ref, d_ref, mo_ref, vo_ref):
        d_ref[...], mo_ref[...], vo_ref[...] = _adamw_math(w_ref[...], g_ref[...], m_ref[...], v_ref[...])

    sp = pl.BlockSpec((T, Cc), lambda i: (i, 0))
    sh = jax.ShapeDtypeStruct((R, Cc), F32)
    return pl.pallas_call(
        body, name=name, grid=(R // T,), in_specs=[sp] * 4, out_specs=(sp, sp, sp), out_shape=(sh, sh, sh),
        compiler_params=_cparams(("parallel",)),
    )(w, g, m, v)


SMALL_ROWS = 32
ROW_CONV, ROW_FCG, ROW_FCU = 5, 13, 22


def _adamw_math(w, g, m, v):
    mn = ADAM_B1 * m + (1.0 - ADAM_B1) * g
    vn = ADAM_B2 * v + (1.0 - ADAM_B2) * (g * g)
    c1 = 1.0 / (1.0 - ADAM_B1 ** ADAM_STEP)
    c2 = 1.0 / (1.0 - ADAM_B2 ** ADAM_STEP)
    return -ADAM_LR * ((mn * c1) / (jnp.sqrt(vn * c2) + ADAM_EPS) + ADAM_WD * w), mn, vn


def _pack_small(n1, n2, fn, gp, gn, conv, fcg, fcu, loss):
    W = D_MODEL

    def body(n1_ref, n2_ref, fn_ref, gp_ref, gn_ref, conv_ref, fcg_ref, fcu_ref, loss_ref, o_ref):
        o_ref[...] = jnp.zeros_like(o_ref)
        o_ref[0:1, :] = n1_ref[...]
        o_ref[1:2, :] = n2_ref[...]
        o_ref[2:3, :] = fn_ref[...]
        o_ref[3:4, 0:8] = gp_ref[0:1, 0:8]
        o_ref[3:4, 8:9] = loss_ref[0:1, 0:1]
        o_ref[4:5, 0:128] = gn_ref[...]
        for i in range(GDN_CONV):
            o_ref[ROW_CONV + 2 * i:ROW_CONV + 2 * i + 1, :] = conv_ref[i:i + 1, 0:W]
            o_ref[ROW_CONV + 2 * i + 1:ROW_CONV + 2 * i + 2, 0:3 * GDN_WIDTH - W] = conv_ref[i:i + 1, W:3 * GDN_WIDTH]
        for r0, ref in ((ROW_FCG, fcg_ref), (ROW_FCU, fcu_ref)):
            for i in range(FFN_CONV):
                for k in range(3):
                    n = min(W, D_FF - k * W)
                    o_ref[r0 + 3 * i + k:r0 + 3 * i + k + 1, 0:n] = ref[i:i + 1, k * W:k * W + n]

    return pl.pallas_call(body, name="pack_small", out_shape=jax.ShapeDtypeStruct((SMALL_ROWS, W), F32))(
        n1, n2, fn, gp, gn, conv, fcg, fcu, loss)


def _small_step(meq, small_all, small, ws, ms, vs):
    W = D_MODEL
    n = len(ws)
    cw, fw = ws[6].shape[1], ws[7].shape[1]

    def body(meq_ref, all_ref, own_ref, *refs):
        w_refs, m_refs, v_refs = refs[:n], refs[n:2 * n], refs[2 * n:3 * n]
        loss_ref = refs[3 * n]
        outs = refs[3 * n + 1:]
        me, q = meq_ref[0], meq_ref[1]
        red = None
        for d in range(8):
            term = jnp.where(me == d, own_ref[...], all_ref[d])
            red = term if red is None else red + term
        loss_ref[...] = jnp.broadcast_to(red[3:4, 8:9], loss_ref.shape)
        conv = [jnp.concatenate([red[ROW_CONV + 2 * i:ROW_CONV + 2 * i + 1, :],
                                 red[ROW_CONV + 2 * i + 1:ROW_CONV + 2 * i + 2, 0:3 * GDN_WIDTH - W]], axis=1)
                for i in range(GDN_CONV)]
        conv = jnp.concatenate(conv, axis=0)

        def fc_rows(r0):
            rows = [jnp.concatenate([red[r0 + 3 * i + k:r0 + 3 * i + k + 1, 0:min(W, D_FF - k * W)] for k in range(3)], axis=1)
                    for i in range(FFN_CONV)]
            return jnp.concatenate(rows, axis=0)

        fc = jnp.concatenate([fc_rows(ROW_FCG), fc_rows(ROW_FCU)], axis=1)

        def chip_block(full, width):
            out = None
            for j in range(N_CHIPS):
                term = jnp.where(q == j, full[:, width * j:width * (j + 1)], 0.0)
                out = term if out is None else out + term
            return out

        grads = [red[0:1, :], red[1:2, :], red[2:3, :], red[3:4, 0:4], red[3:4, 4:8], red[4:5, 0:128],
                 chip_block(conv, cw), chip_block(fc, fw)]
        for k in range(n):
            d_, m_, v_ = _adamw_math(w_refs[k][...], grads[k], m_refs[k][...], v_refs[k][...])
            outs[4 * k][...] = grads[k]
            outs[4 * k + 1][...] = d_
            outs[4 * k + 2][...] = m_
            outs[4 * k + 3][...] = v_

    full = lambda a: pl.BlockSpec(a.shape, lambda i, s_, nd=len(a.shape): (0,) * nd)
    arrays = [small_all, small, *ws, *ms, *vs]
    out_shapes = [jax.ShapeDtypeStruct((8, 128), F32)] + [jax.ShapeDtypeStruct(w.shape, F32) for w in ws for _ in range(4)]
    gs = pltpu.PrefetchScalarGridSpec(
        num_scalar_prefetch=1, grid=(1,), in_specs=[full(a) for a in arrays],
        out_specs=[pl.BlockSpec(o.shape, lambda i, s_, nd=len(o.shape): (0,) * nd) for o in out_shapes])
    return pl.pallas_call(body, name="small_step", grid_spec=gs, out_shape=out_shapes)(meq, *arrays)


def _pad_lanes(v, n=D_MODEL):
    return jnp.pad(v, ((0, 0), (0, n - v.shape[1])))


def kernel(x, norm1_w, w_in, conv_qkv_w, a_log, dt_bias, gdn_norm_w, w_out, norm2_w, w_up, ffn_conv_w, w_down, final_norm_w, loss_target, m_norm1_w, m_w_in, m_conv_qkv_w, m_a_log, m_dt_bias, m_gdn_norm_w, m_w_out, m_norm2_w, m_w_up, m_ffn_conv_w, m_w_down, m_final_norm_w, v_norm1_w, v_w_in, v_conv_qkv_w, v_a_log, v_dt_bias, v_gdn_norm_w, v_w_out, v_norm2_w, v_w_up, v_ffn_conv_w, v_w_down, v_final_norm_w):
    c = lax.axis_index("c")
    q = 2 * lax.axis_index("x") + lax.axis_index("y")
    S = x.shape[1]
    cq = jnp.stack([c, q]).astype(jnp.int32)

    *in_started, in_token = _gather_halves_start([w_in[0].astype(_MXU), conv_qkv_w[0], ffn_conv_w[0]], x, "gather_in_start")
    w_in_l, m_w_in_l, v_w_in_l = (jnp.swapaxes(a + in_token[0:1, 0:1], 1, 2)[0] for a in (w_in, m_w_in, v_w_in))
    h1 = _rmsnorm_fwd(x[0], norm1_w, "norm1", after=[in_token])
    rest = [(a[0] + in_token[0:1, 0:1]).astype(_MXU) for a in (w_out, w_up, w_down)]
    in_shards, got_in = _gather_halves_wait(in_started, [w_in_l, m_w_in_l, v_w_in_l, h1, *rest], "gather_in_wait")
    (g_in, g_conv, g_fconv), (w_in_l, m_w_in_l, v_w_in_l) = _place_own(
        in_shards, _sibling_fill(got_in, "fill_in"), cq, "place_in", carry=[w_in_l, m_w_in_l, v_w_in_l])
    *rest_started, token = _gather_halves_start(rest, g_conv, "gather_rest_start")

    rest_state = {}

    def rest_arrived(after):
        rest_state["shards"], got = _gather_halves_wait(rest_started, after, "gather_rest_wait")
        *rest_state["fill"], tok = _sibling_fill_start(got, "fill_rest_start")
        return tok

    def rest_filled(after):
        got = _sibling_fill_wait(rest_state["fill"], after, "fill_rest_wait")
        g_out, g_up, g_down = _place_own(rest_state["shards"], got, cq, "place_rest")
        return g_out.reshape(D_MODEL, D_MODEL), g_up, g_down.reshape(D_FF, D_MODEL)

    rest_weights = (rest_arrived, rest_filled)
    wp = _wp_assemble(g_in, [token])
    conv_f = jnp.concatenate([g_conv[i] for i in range(N_CHIPS)], axis=1)
    fcw = jnp.concatenate([g_fconv[i] for i in range(N_CHIPS)], axis=1)
    gp = _pad_lanes(jnp.concatenate([a_log, dt_bias], axis=1), 128)
    fnw = final_norm_w[None, :]
    early = {}

    early_names = ("w_up", "w_down", "w_out")

    def early_sibling(d_wup, d_wdown, d_wout):
        *early["sibling"], tok = _grad_sibling_start(
            [d_wup, d_wdown.reshape(N_CHIPS, D_FF // N_CHIPS, D_MODEL), d_wout.reshape(N_CHIPS, D_MODEL // N_CHIPS, D_MODEL)],
            "grad_sibling_early_start")
        return tok

    def early_chips(after):
        fams_e, got_e = _grad_sibling_wait(early["sibling"], after, "grad_sibling_early_wait")
        early["parts"] = [_add_sibling(f, r, cq, "add_sibling_" + nm) for f, r, nm in zip(fams_e, got_e, early_names)]
        *early["started"], tok = _grad_chips_start([p[1] for p in early["parts"]], "grad_chips_start")
        return tok

    def late_sibling(d_wp):
        *early["late_sibling"], tok = _grad_sibling_start([d_wp[None]], "grad_sibling_late_start")
        return tok

    def late_chips(after):
        fams, got = _grad_sibling_wait(early["late_sibling"], after, "grad_sibling_late_wait")
        early["late_part"] = _add_sibling_split(fams[0], got[0], cq, "add_sibling_w_in")
        *early["late_started"], tok = _grad_chips_start([early["late_part"][1]], "grad_chips_late_start")
        return tok

    early_grads = (early_sibling, early_chips, late_sibling, late_chips)

    loss_l, dx, g = _local_step(x[0], loss_target[0], h1, norm1_w, norm2_w, fnw, gp, gdn_norm_w, wp,
                                conv_f, fcw, rest_weights, early_grads)
    small = _pack_small(g["n1w"], g["n2w"], g["fnw"], g["gp"], g["gnw"], g["conv_w"], g["fcw_g"], g["fcw_u"], loss_l)
    *small_started, small_token = _grad_chips_start([], "small_gather_start", small)
    got3_e = _grad_chips_wait(early["started"], [dx, small_token], "grad_chips_wait")
    g_w_up, g_w_down, g_w_out = _grad_share(
        [_add_chips(p[0], r3, cq, "add_chips_" + nm) for p, r3, nm in zip(early["parts"], got3_e, early_names)],
        "grad_share_early")
    big = {}

    def adamw_big(nm, w, gg, m, v):
        d_, m_, v_ = _adamw(w[0], gg, m[0], v[0], "adamw_" + nm)
        big[nm] = (gg[None], d_[None], m_[None], v_[None])

    adamw_big("w_up", w_up, g_w_up, m_w_up, v_w_up)
    adamw_big("w_down", w_down, g_w_down, m_w_down, v_w_down)
    adamw_big("w_out", w_out, g_w_out, m_w_out, v_w_out)
    got3, = _grad_chips_wait(early["late_started"], [big[nm][1] for nm in early_names], "grad_chips_late_wait")
    g_w_in, = _grad_share([_add_chips(early["late_part"][0], got3, cq, "add_chips_w_in")], "grad_share_late")
    g_t = _transposed(g_w_in)
    d_t, m_t, v_t = _adamw(w_in_l, g_t, m_w_in_l, v_w_in_l, "adamw_w_in")
    big["w_in"] = tuple(jnp.swapaxes(t[None], 1, 2) for t in (g_t, d_t, m_t, v_t))
    small_all, small = _grad_chips_wait(small_started, [d_t], "small_gather_wait", with_small=True)
    small_names = ["norm1_w", "norm2_w", "final_norm_w", "a_log", "dt_bias", "gdn_norm_w", "conv_qkv_w", "ffn_conv_w"]
    loss_b, *small_out = _small_step(
        jnp.stack([2 * q + c, q]).astype(jnp.int32), small_all, small,
        [norm1_w, norm2_w, final_norm_w[None], a_log, dt_bias, gdn_norm_w, conv_qkv_w[0], ffn_conv_w[0]],
        [m_norm1_w, m_norm2_w, m_final_norm_w[None], m_a_log, m_dt_bias, m_gdn_norm_w, m_conv_qkv_w[0], m_ffn_conv_w[0]],
        [v_norm1_w, v_norm2_w, v_final_norm_w[None], v_a_log, v_dt_bias, v_gdn_norm_w, v_conv_qkv_w[0], v_ffn_conv_w[0]])
    like = dict(final_norm_w=lambda t: t[0], conv_qkv_w=lambda t: t[None], ffn_conv_w=lambda t: t[None])
    for k, nm in enumerate(small_names):
        big[nm] = tuple(like.get(nm, lambda t: t)(t) for t in small_out[4 * k:4 * k + 4])
    names = ["norm1_w", "w_in", "conv_qkv_w", "a_log", "dt_bias", "gdn_norm_w", "w_out", "norm2_w", "w_up",
             "ffn_conv_w", "w_down", "final_norm_w"]
    return (loss_b[0, 0], dx[None], *[big[n][0] for n in names], *[big[n][1] for n in names],
            *[big[n][2] for n in names], *[big[n][3] for n in names])
```

```python
import functools
import math

import numpy as np
import jax
import jax.numpy as jnp
from jax import lax
from jax.experimental import pallas as pl
from jax.experimental.pallas import tpu as pltpu

F32 = jnp.float32
BF16 = jnp.bfloat16
_MXU = jnp.bfloat16
_HI = lax.Precision.HIGHEST
EPS = 1e-6
V7X_VMEM_LIMIT = 56 * 1024 * 1024
MESH = pl.DeviceIdType.MESH

D_MODEL = 1024
GDN_HEADS, GDN_DIM, GDN_CHUNK, GDN_CONV = 4, 128, 64, 4
GDN_WIDTH = GDN_HEADS * GDN_DIM
DIL_HEADS, DIL_DIM = 8, 64
DIL_WIDTH = DIL_HEADS * DIL_DIM
D_FF, FFN_CONV = 2816, 3
IN_COLS = 3592
P_COLS = 3840
P_Z, P_QKVB, P_BA = 1536, 2048, 3584
ATT_T = 1024
ADAM_LR, ADAM_B1, ADAM_B2, ADAM_EPS, ADAM_WD, ADAM_STEP = 0.001, 0.9, 0.999, 1e-08, 0.01, 10
N_CHIPS = 4


def _cparams(sem=None, vmem=None):
    kw = {}
    if sem is not None:
        kw["dimension_semantics"] = sem
    if vmem is not None:
        kw["vmem_limit_bytes"] = vmem
    return pltpu.CompilerParams(**kw)


def _silu(x):
    return x * jax.nn.sigmoid(x)


def _pick_tile(n, cap):
    best = None
    for t in range(128, min(n, cap) + 1, 128):
        if n % t == 0:
            best = t
    return best or n


def _mm(a, b, mode, *, out_dtype=F32, residual=None, name, b_blocks=False, place=None, into=None, tn=None, after=()):
    if mode == "nn":
        M, K = a.shape
        N = b.shape[0] * b.shape[2] if b_blocks else b.shape[1]
    elif mode == "nt":
        (M, K), (N, _) = a.shape, b.shape
    else:
        (K, M), (_, N) = a.shape, b.shape
    tm = _pick_tile(M, 1024)
    tn = b.shape[2] if b_blocks else (tn or _pick_tile(N, 1536))

    def vmem(tm, tn):
        return 2 * (tm * K * a.dtype.itemsize + tn * K * b.dtype.itemsize
                    + tm * tn * (jnp.dtype(out_dtype).itemsize + (4 if residual is not None else 0))) + 3 * tm * tn * 4

    fixed_tn = b_blocks or (place is not None and place[0] == "blocks")
    while vmem(tm, tn) > 40 * 1024 * 1024:
        if (tm >= tn or fixed_tn) and tm % 256 == 0:
            tm //= 2
        elif tn % 256 == 0 and not fixed_tn:
            tn //= 2
        else:
            tm //= 2
    a_spec = pl.BlockSpec((K, tm), lambda j, i: (0, i)) if mode == "tn" else pl.BlockSpec((tm, K), lambda j, i: (i, 0))
    if b_blocks:
        b_spec = pl.BlockSpec((None, K, tn), lambda j, i: (j, 0, 0))
    else:
        b_spec = pl.BlockSpec((tn, K), lambda j, i: (j, 0)) if mode == "nt" else pl.BlockSpec((K, tn), lambda j, i: (0, j))
    r_spec = pl.BlockSpec((tm, tn), lambda j, i: (i, j))
    if place is None:
        o_spec, o_shape = r_spec, (M, N)
    elif place[0] == "rows":
        off = place[2] // tm
        o_spec, o_shape = pl.BlockSpec((tm, tn), lambda j, i: (i + off, j)), (place[1], N)
    else:
        off = place[2]
        o_spec, o_shape = pl.BlockSpec((None, tm, tn), lambda j, i: (j + off, i, 0)), (place[1], M, tn)
    dims = {"nn": (((1,), (0,)), ((), ())), "nt": (((1,), (1,)), ((), ())), "tn": (((0,), (0,)), ((), ()))}[mode]

    def body(*refs):
        a_ref, b_ref = refs[0], refs[1]
        o_ref = refs[-1]
        acc = lax.dot_general(a_ref[...].astype(_MXU), b_ref[...].astype(_MXU), dims, preferred_element_type=F32)
        if residual is not None:
            acc = acc + refs[2][...]
        o_ref[...] = acc.astype(out_dtype)

    ins, specs, alias = [a, b], [a_spec, b_spec], {}
    if residual is not None:
        ins.append(residual)
        specs.append(r_spec)
    if into is not None:
        alias = {len(ins): 0}
        ins.append(into)
        specs.append(pl.BlockSpec(memory_space=pl.ANY))
    ins += list(after)
    specs += [pl.BlockSpec(memory_space=pl.ANY)] * len(after)
    return pl.pallas_call(
        body, name=name, grid=(N // tn, M // tm), in_specs=specs, out_specs=o_spec,
        out_shape=jax.ShapeDtypeStruct(o_shape, out_dtype), input_output_aliases=alias,
        compiler_params=_cparams(("parallel", "parallel"), V7X_VMEM_LIMIT),
    )(*ins)


def _mm_nt_blocks(a_list, b4, name, after=()):
    M = a_list[0].shape[0]
    nb, N, Kb = b4.shape
    tm, tn = _pick_tile(M, 1024), _pick_tile(N, 512)

    def body(a0_ref, a1_ref, b_ref, *rest):
        o_ref = rest[-1]
        acc = None
        for blk in range(nb):
            a_ref = (a0_ref, a1_ref)[blk // 2]
            lo = (blk % 2) * Kb
            t = lax.dot_general(a_ref[:, lo:lo + Kb].astype(_MXU), b_ref[blk].astype(_MXU), (((1,), (1,)), ((), ())),
                                preferred_element_type=F32)
            acc = t if acc is None else acc + t
        o_ref[...] = acc

    a_spec = pl.BlockSpec((tm, 2 * Kb), lambda j, i: (i, 0))
    return pl.pallas_call(
        body, name=name, grid=(N // tn, M // tm),
        in_specs=[a_spec, a_spec, pl.BlockSpec((nb, tn, Kb), lambda j, i: (0, j, 0))]
        + [pl.BlockSpec(memory_space=pl.ANY)] * len(after),
        out_specs=pl.BlockSpec((tm, tn), lambda j, i: (i, j)), out_shape=jax.ShapeDtypeStruct((M, N), F32),
        compiler_params=_cparams(("parallel", "parallel"), V7X_VMEM_LIMIT),
    )(a_list[0], a_list[1], b4, *after)


def _wp_assemble(g_in, after=()):
    nb, Dm, Wb = g_in.shape
    T = 256
    n_lo = P_QKVB - 2 * Wb

    def body(g_ref, *rest):
        g2 = g_ref[2]
        rest[-1][...] = jnp.concatenate(
            [g_ref[0], g_ref[1], g2[:, :n_lo], g2[:, n_lo + 8:], g_ref[3], g2[:, n_lo:n_lo + 8],
             jnp.zeros((T, P_COLS - P_BA - 8), g_in.dtype)], axis=1)

    return pl.pallas_call(
        body, name="wp_assemble", grid=(Dm // T,),
        in_specs=[pl.BlockSpec((nb, T, Wb), lambda i: (0, i, 0))] + [pl.BlockSpec(memory_space=pl.ANY)] * len(after),
        out_specs=pl.BlockSpec((T, P_COLS), lambda i: (i, 0)), out_shape=jax.ShapeDtypeStruct((Dm, P_COLS), g_in.dtype),
        compiler_params=_cparams(("parallel",)),
    )(g_in, *after)


def _rmsnorm_fwd(x, w, name, after=()):
    S, D = x.shape
    T = _pick_tile(S, 512)

    def body(x_ref, w_ref, *rest):
        xv = x_ref[...]
        rs = lax.rsqrt(jnp.mean(xv * xv, axis=-1, keepdims=True) + EPS)
        rest[-1][...] = (xv * rs * w_ref[...]).astype(rest[-1].dtype)

    return pl.pallas_call(
        body, name=name, grid=(S // T,),
        in_specs=[pl.BlockSpec((T, D), lambda i: (i, 0)), pl.BlockSpec((1, D), lambda i: (0, 0))] + [_ANY] * len(after),
        out_specs=pl.BlockSpec((T, D), lambda i: (i, 0)),
        out_shape=jax.ShapeDtypeStruct((S, D), _MXU),
        compiler_params=_cparams(("parallel",)),
    )(x, w, *after)


def _rmsnorm_bwd(dh, x, w, dres, name, after=()):
    S, D = x.shape
    T = _pick_tile(S, 512)

    def body(dh_ref, x_ref, w_ref, dres_ref, *rest):
        dx_ref, dw_ref = rest[-2:]
        xv = x_ref[...]
        rs = lax.rsqrt(jnp.mean(xv * xv, axis=-1, keepdims=True) + EPS)
        xn = xv * rs
        dhv = dh_ref[...]
        dxn = dhv * w_ref[...]
        dx_ref[...] = dres_ref[...] + rs * (dxn - xn * jnp.mean(dxn * xn, axis=-1, keepdims=True))

        @pl.when(pl.program_id(0) == 0)
        def _():
            dw_ref[...] = jnp.zeros_like(dw_ref)

        dw_ref[...] += jnp.sum(dhv * xn, axis=0, keepdims=True)

    row = pl.BlockSpec((T, D), lambda i: (i, 0))
    vec = pl.BlockSpec((1, D), lambda i: (0, 0))
    return pl.pallas_call(
        body, name=name, grid=(S // T,), in_specs=[row, row, vec, row] + [_ANY] * len(after), out_specs=(row, vec),
        out_shape=(jax.ShapeDtypeStruct((S, D), F32), jax.ShapeDtypeStruct((1, D), F32)),
        compiler_params=_cparams(("arbitrary",)),
    )(dh, x, w, dres, *after)


def _loss_head(x3, w, tgt, name):
    S, D = x3.shape
    T = _pick_tile(S, 512)

    def body(x_ref, w_ref, t_ref, loss_ref, dx_ref, dxn_ref, dw_ref):
        xv = x_ref[...]
        rs = lax.rsqrt(jnp.mean(xv * xv, axis=-1, keepdims=True) + EPS)
        xn = xv * rs
        err = xn * w_ref[...] - t_ref[...]
        dy = err * (1.0 / D)
        dxn = dy * w_ref[...]
        dxv = rs * (dxn - xn * jnp.mean(dxn * xn, axis=-1, keepdims=True))
        dx_ref[...] = dxv
        dxn_ref[...] = dxv.astype(dxn_ref.dtype)

        @pl.when(pl.program_id(0) == 0)
        def _():
            dw_ref[...] = jnp.zeros_like(dw_ref)
            loss_ref[...] = jnp.zeros_like(loss_ref)

        dw_ref[...] += jnp.sum(dy * xn, axis=0, keepdims=True)
        part = jnp.sum(jnp.sum(err * err, axis=-1, keepdims=True), axis=0, keepdims=True) * (0.5 / D)
        loss_ref[...] += jnp.broadcast_to(part, loss_ref.shape)

    row = pl.BlockSpec((T, D), lambda i: (i, 0))
    vec = pl.BlockSpec((1, D), lambda i: (0, 0))
    return pl.pallas_call(
        body, name=name, grid=(S // T,), in_specs=[row, vec, row],
        out_specs=(pl.BlockSpec((8, 128), lambda i: (0, 0)), row, row, vec),
        out_shape=(jax.ShapeDtypeStruct((8, 128), F32), jax.ShapeDtypeStruct((S, D), F32), jax.ShapeDtypeStruct((S, D), _MXU),
                   jax.ShapeDtypeStruct((1, D), F32)),
        compiler_params=_cparams(("arbitrary",)),
    )(x3, w, tgt)


def _shifted(ext, back, lo, n):
    if back == 0:
        return ext[lo:lo + n, :]
    return pltpu.roll(ext, back % ext.shape[0], 0)[lo:lo + n, :]


def _conv_windows(ext, K, T):
    return [_shifted(ext, (K - 1) - i, 8, T) for i in range(K)]


def _conv_taps(ext, w, K, T):
    out = None
    for i, win in enumerate(_conv_windows(ext, K, T)):
        term = win * w[i:i + 1, :]
        out = term if out is None else out + term
    return out


def _conv_taps_t(ext, w, K, T):
    out = None
    for i in range(K):
        term = _shifted(ext, i - (K - 1), 0, T) * w[i:i + 1, :]
        out = term if out is None else out + term
    return out


def _tri_masks(C):
    r = lax.broadcasted_iota(jnp.int32, (C, C), 0)
    c = lax.broadcasted_iota(jnp.int32, (C, C), 1)
    return r == c, r >= c, r > c, r <= c


_NN, _NT, _TN = ((1,), (0,)), ((1,), (1,)), ((0,), (0,))
_GDN_PASSES = dict(qk=1, inv=1, sol=1, scan=1, bwd=1)


def _bdot_raw(a, b, kind, passes):
    dims = ({"NN": ((2,), (1,)), "NT": ((2,), (2,)), "TN": ((1,), (1,))}[kind], ((0,), (0,)))
    if passes == 0:
        return lax.dot_general(a, b, dims, precision=_HI, preferred_element_type=F32)
    ah, bh = a.astype(BF16), b.astype(BF16)
    out = lax.dot_general(ah, bh, dims, preferred_element_type=F32)
    if passes == 3:
        al, bl = (a - ah.astype(F32)).astype(BF16), (b - bh.astype(F32)).astype(BF16)
        out = out + lax.dot_general(ah, bl, dims, preferred_element_type=F32) + lax.dot_general(al, bh, dims, preferred_element_type=F32)
    return out


@functools.partial(jax.custom_vjp, nondiff_argnums=(2, 3))
def _bdot(a, b, kind, passes):
    return _bdot_raw(a, b, kind, passes)


def _bdot_fwd(a, b, kind, passes):
    return _bdot_raw(a, b, kind, passes), (a, b)


def _bdot_bwd(kind, passes, res, ct):
    a, b = res
    if kind == "NN":
        return _bdot_raw(ct, b, "NT", passes), _bdot_raw(a, ct, "TN", passes)
    if kind == "NT":
        return _bdot_raw(ct, b, "NN", passes), _bdot_raw(ct, a, "TN", passes)
    return _bdot_raw(b, ct, "NT", passes), _bdot_raw(a, ct, "NN", passes)


_bdot.defvjp(_bdot_fwd, _bdot_bwd)


def _softplus(x):
    return jnp.maximum(x, 0.0) + jnp.log(1.0 + jnp.exp(-jnp.abs(x)))


def _gdn_stage1(cq, ck, cv, b_col, a_col, alog, dtb, dot=_bdot_raw):
    C = cq.shape[1]
    eye, incl, strict, incl_t = _tri_masks(C)
    qn = cq * lax.rsqrt(jnp.sum(cq * cq, axis=-1, keepdims=True) + EPS) * (GDN_DIM ** -0.5)
    kn = ck * lax.rsqrt(jnp.sum(ck * ck, axis=-1, keepdims=True) + EPS)
    beta = jax.nn.sigmoid(b_col)
    g = -jnp.exp(alog) * _softplus(a_col + dtb)
    g_row = jnp.sum(jnp.where(eye, g, 0.0), axis=1, keepdims=True)
    beta_row = jnp.sum(jnp.where(eye, beta, 0.0), axis=1, keepdims=True)
    gc_col = jnp.sum(jnp.where(incl, g_row, 0.0), axis=2, keepdims=True)
    gc_row = jnp.sum(jnp.where(incl_t, g, 0.0), axis=1, keepdims=True)
    dec = jnp.where(incl, jnp.exp(jnp.where(incl, gc_col - gc_row, 0.0)), 0.0)
    kk = dot(kn, kn, "NT", _GDN_PASSES["qk"])
    qk = dot(qn, kn, "NT", _GDN_PASSES["qk"])
    lmat = jnp.where(strict, dec * kk * beta_row, 0.0)
    attn = dec * qk * beta_row
    gam = jnp.exp(gc_col)
    gc_last = gc_col[:, C - 1:C, :]
    k_end = kn * (jnp.exp(gc_last - gc_col) * beta)
    return lmat, cv, gam * kn, gam * qn, attn, k_end, jnp.exp(gc_last)


def _tri_inv(lmat):
    C = lmat.shape[1]
    eye = _tri_masks(C)[0]
    ps = _GDN_PASSES["inv"]
    p = jnp.where(eye, 1.0, 0.0) - lmat
    lp = _bdot_raw(lmat, lmat, "NN", ps)
    n = int(math.log2(C))
    for s in range(1, n):
        p = p + _bdot_raw(p, lp, "NN", ps)
        if s < n - 1:
            lp = _bdot_raw(lp, lp, "NN", ps)
    return p


def _gated_norm(o, z, gnw):
    on = o * lax.rsqrt(jnp.mean(o * o, axis=-1, keepdims=True) + EPS) * gnw
    return on * _silu(z)


GDN_PG = 4
GDN_SG = 4


def _gdn_pairs(c, ba, gp, G):
    C, W, H = GDN_CHUNK, GDN_WIDTH, GDN_HEADS
    pairs = [(j, h) for j in range(G) for h in range(H)]
    cq, ck, cv = (jnp.stack([c[C * j:C * (j + 1), o + GDN_DIM * h:o + GDN_DIM * (h + 1)] for j, h in pairs]) for o in (0, W, 2 * W))
    b_col = jnp.stack([ba[C * j:C * (j + 1), h:h + 1] for j, h in pairs])
    a_col = jnp.stack([ba[C * j:C * (j + 1), H + h:H + h + 1] for j, h in pairs])
    alog = jnp.stack([gp[0:1, h:h + 1] for j, h in pairs])
    dtb = jnp.stack([gp[0:1, H + h:H + h + 1] for j, h in pairs])
    return pairs, (cq, ck, cv, b_col, a_col, alog, dtb)


def _gdn_pre_specs(S, G):
    C = GDN_CHUNK
    T = C * G
    return dict(
        cur=pl.BlockSpec((T, 3 * GDN_WIDTH), lambda i: (i, 0)),
        prev=pl.BlockSpec((8, 3 * GDN_WIDTH), lambda i: (jnp.maximum(i * (T // 8) - 1, 0), 0)),
        ba=pl.BlockSpec((T, 128), lambda i: (i, P_BA // 128)),
        cw=pl.BlockSpec((GDN_CONV, 3 * GDN_WIDTH), lambda i: (0, 0)),
        vec=pl.BlockSpec((1, 128), lambda i: (0, 0)),
        hd=pl.BlockSpec((GDN_HEADS, T, GDN_DIM), lambda i: (0, i, 0)),
        hc=pl.BlockSpec((GDN_HEADS, T, C), lambda i: (0, i, 0)),
        ge=pl.BlockSpec((G, GDN_HEADS, 8, 128), lambda i: (i, 0, 0, 0)),
    )


def _hd_shape(S, last=GDN_DIM):
    return jax.ShapeDtypeStruct((GDN_HEADS, S, last), F32)


def _gdn_pre(proj, conv_w, gp):
    S = proj.shape[0]
    C, G = GDN_CHUNK, GDN_PG
    nc = S // C
    sp = _gdn_pre_specs(S, G)

    def body(cur_ref, prev_ref, ba_ref, cw_ref, gp_ref, uv_ref, wk_ref, qd_ref, ke_ref, at_ref, ti_ref, ge_ref):
        prev = prev_ref[...] * jnp.where(pl.program_id(0) == 0, 0.0, 1.0)
        c = _silu(_conv_taps(jnp.concatenate([prev, cur_ref[...]], axis=0), cw_ref[...], GDN_CONV, C * G))
        pairs, args = _gdn_pairs(c, ba_ref[...], gp_ref[...], G)
        lmat, v, rk, q_dec, attn, k_end, g_end = _gdn_stage1(*args)
        t = _tri_inv(lmat)
        u_v = _bdot_raw(t, v, "NN", _GDN_PASSES["sol"])
        w_k = _bdot_raw(t, rk, "NN", _GDN_PASSES["sol"])
        for b, (j, h) in enumerate(pairs):
            rows = slice(C * j, C * (j + 1))
            uv_ref[h, rows, :] = u_v[b]
            wk_ref[h, rows, :] = w_k[b]
            qd_ref[h, rows, :] = q_dec[b]
            ke_ref[h, rows, :] = k_end[b]
            at_ref[h, rows, :] = attn[b]
            ti_ref[h, rows, :] = t[b]
            ge_ref[j, h] = jnp.broadcast_to(g_end[b], (8, 128))

    return pl.pallas_call(
        body, name="gdn_pre", grid=(nc // G,),
        in_specs=[sp["cur"], sp["prev"], sp["ba"], sp["cw"], sp["vec"]],
        out_specs=(sp["hd"], sp["hd"], sp["hd"], sp["hd"], sp["hc"], sp["hc"], sp["ge"]),
        out_shape=(_hd_shape(S), _hd_shape(S), _hd_shape(S), _hd_shape(S), _hd_shape(S, C), _hd_shape(S, C),
                   jax.ShapeDtypeStruct((nc, GDN_HEADS, 8, 128), F32)),
        compiler_params=_cparams(("parallel",)),
    )(proj, proj, proj, conv_w, gp)


def _gdn_scan_specs(S, G, rev):
    C = GDN_CHUNK
    T = C * G
    n = S // T
    ci = (lambda i: n - 1 - i) if rev else (lambda i: i)
    return dict(
        hd=pl.BlockSpec((GDN_HEADS, T, GDN_DIM), lambda i: (0, ci(i), 0)),
        hc=pl.BlockSpec((GDN_HEADS, T, C), lambda i: (0, ci(i), 0)),
        ge=pl.BlockSpec((G, GDN_HEADS, 8, 128), lambda i: (ci(i), 0, 0, 0)),
        z=pl.BlockSpec((T, GDN_WIDTH), lambda i: (ci(i), P_Z // GDN_WIDTH)),
        oa=pl.BlockSpec((T, GDN_WIDTH), lambda i: (ci(i), 0)),
        vec=pl.BlockSpec((1, 128), lambda i: (0, 0)),
        st=pl.BlockSpec((G, GDN_HEADS, GDN_DIM, GDN_DIM), lambda i: (ci(i), 0, 0, 0)),
    )


def _gdn_scan(u_v, w_k, q_dec, k_end, attn, g_end, proj, gnw, mix, after=()):
    S = proj.shape[0]
    C, G = GDN_CHUNK, GDN_SG
    nc = S // C
    sp = _gdn_scan_specs(S, G, False)
    ps = _GDN_PASSES["scan"]

    def body(uv_ref, wk_ref, qd_ref, ke_ref, at_ref, ge_ref, z_ref, gnw_ref, *rest):
        oa_ref, st_ref, s_scr = rest[-3:]

        @pl.when(pl.program_id(0) == 0)
        def _():
            s_scr[...] = jnp.zeros_like(s_scr)

        for j in range(G):
            rows = slice(C * j, C * (j + 1))
            st = s_scr[...]
            st_ref[j] = st
            u = uv_ref[:, rows, :] - _bdot_raw(wk_ref[:, rows, :], st, "NN", ps)
            o = _bdot_raw(qd_ref[:, rows, :], st, "NN", ps) + _bdot_raw(at_ref[:, rows, :], u, "NN", ps)
            s_scr[...] = ge_ref[j][:, 0:1, 0:1] * st + _bdot_raw(ke_ref[:, rows, :], u, "TN", ps)
            for h in range(GDN_HEADS):
                cols = slice(GDN_DIM * h, GDN_DIM * (h + 1))
                oa_ref[rows, cols] = _gated_norm(o[h], z_ref[rows, cols], gnw_ref[...])

    return pl.pallas_call(
        body, name="gdn_scan", grid=(nc // G,),
        in_specs=[sp["hd"], sp["hd"], sp["hd"], sp["hd"], sp["hc"], sp["ge"], sp["z"], sp["vec"]] + [_ANY] * (1 + len(after)),
        out_specs=(sp["oa"], sp["st"]),
        out_shape=(jax.ShapeDtypeStruct(mix.shape, F32),
                   jax.ShapeDtypeStruct((nc, GDN_HEADS, GDN_DIM, GDN_DIM), F32)),
        input_output_aliases={8: 0},
        scratch_shapes=[pltpu.VMEM((GDN_HEADS, GDN_DIM, GDN_DIM), F32)],
        compiler_params=_cparams(("arbitrary",)),
    )(u_v, w_k, q_dec, k_end, attn, g_end, proj, gnw, mix, *after)


def _gdn_scan_bwd(u_v, w_k, q_dec, k_end, attn, g_end, proj, gnw, states, d_oa):
    S = proj.shape[0]
    C, G = GDN_CHUNK, GDN_SG
    nc = S // C
    sp = _gdn_scan_specs(S, G, True)
    ps, pb = _GDN_PASSES["scan"], _GDN_PASSES["bwd"]

    def body(uv_ref, wk_ref, qd_ref, ke_ref, at_ref, ge_ref, z_ref, gnw_ref, st_ref, doa_ref,
             duv_ref, dwk_ref, dqd_ref, dke_ref, dat_ref, dge_ref, dz_ref, dgnw_ref, ds_scr):
        @pl.when(pl.program_id(0) == 0)
        def _():
            ds_scr[...] = jnp.zeros_like(ds_scr)
            dgnw_ref[...] = jnp.zeros_like(dgnw_ref)

        dgnw = jnp.zeros((1, 128), F32)
        for j in reversed(range(G)):
            rows = slice(C * j, C * (j + 1))
            st = st_ref[j]
            wk, qd, ke, at = wk_ref[:, rows, :], qd_ref[:, rows, :], ke_ref[:, rows, :], at_ref[:, rows, :]
            u = uv_ref[:, rows, :] - _bdot_raw(wk, st, "NN", ps)
            o = _bdot_raw(qd, st, "NN", ps) + _bdot_raw(at, u, "NN", ps)
            dos = []
            for h in range(GDN_HEADS):
                cols = slice(GDN_DIM * h, GDN_DIM * (h + 1))
                _, vjp2 = jax.vjp(_gated_norm, o[h], z_ref[rows, cols], gnw_ref[...])
                do_h, dz_h, dgn = vjp2(doa_ref[rows, cols])
                dz_ref[rows, cols] = dz_h
                dgnw = dgnw + dgn
                dos.append(do_h)
            do = jnp.stack(dos)
            ds_new = ds_scr[...]
            du = _bdot_raw(at, do, "TN", pb) + _bdot_raw(ke, ds_new, "NN", pb)
            duv_ref[:, rows, :] = du
            dat_ref[:, rows, :] = _bdot_raw(do, u, "NT", pb)
            dqd_ref[:, rows, :] = _bdot_raw(do, st, "NT", pb)
            dke_ref[:, rows, :] = _bdot_raw(u, ds_new, "NT", pb)
            dwk_ref[:, rows, :] = -_bdot_raw(du, st, "NT", pb)
            d_ge = jnp.sum(jnp.sum(st * ds_new, axis=2, keepdims=True), axis=1, keepdims=True)
            dge_ref[j] = jnp.broadcast_to(d_ge, (GDN_HEADS, 8, 128))
            ds_scr[...] = ge_ref[j][:, 0:1, 0:1] * ds_new + _bdot_raw(qd, do, "TN", pb) - _bdot_raw(wk, du, "TN", pb)
        dgnw_ref[...] += dgnw

    return pl.pallas_call(
        body, name="gdn_scan_bwd", grid=(nc // G,),
        in_specs=[sp["hd"], sp["hd"], sp["hd"], sp["hd"], sp["hc"], sp["ge"], sp["z"], sp["vec"], sp["st"], sp["oa"]],
        out_specs=(sp["hd"], sp["hd"], sp["hd"], sp["hd"], sp["hc"], sp["ge"], sp["oa"], sp["vec"]),
        out_shape=(_hd_shape(S), _hd_shape(S), _hd_shape(S), _hd_shape(S), _hd_shape(S, C),
                   jax.ShapeDtypeStruct((nc, GDN_HEADS, 8, 128), F32), jax.ShapeDtypeStruct((S, GDN_WIDTH), F32),
                   jax.ShapeDtypeStruct((1, 128), F32)),
        scratch_shapes=[pltpu.VMEM((GDN_HEADS, GDN_DIM, GDN_DIM), F32)],
        compiler_params=_cparams(("arbitrary",)),
    )(u_v, w_k, q_dec, k_end, attn, g_end, proj, gnw, states, d_oa)


def _gdn_post(proj, conv_w, gp, tinv, u_v, w_k, d_uv, d_wk, d_qd, d_ke, d_at, d_ge):
    S = proj.shape[0]
    C, G = GDN_CHUNK, GDN_PG
    nc = S // C
    sp = _gdn_pre_specs(S, G)
    pb = _GDN_PASSES["bwd"]

    def body(cur_ref, prev_ref, ba_ref, cw_ref, gp_ref, ti_ref, uv_ref, wk_ref, duv_ref, dwk_ref, dqd_ref, dke_ref,
             dat_ref, dge_ref, dpre_ref, dba_ref, dgp_ref):
        i = pl.program_id(0)

        @pl.when(i == 0)
        def _():
            dgp_ref[...] = jnp.zeros_like(dgp_ref)

        prev = prev_ref[...] * jnp.where(i == 0, 0.0, 1.0)
        pre = _conv_taps(jnp.concatenate([prev, cur_ref[...]], axis=0), cw_ref[...], GDN_CONV, C * G)
        sg = jax.nn.sigmoid(pre)
        dsilu = sg * (1.0 + pre * (1.0 - sg))
        pairs, args = _gdn_pairs(pre * sg, ba_ref[...], gp_ref[...], G)
        _, vjp1 = jax.vjp(functools.partial(_gdn_stage1, dot=_bdot), *args)

        def take(ref):
            return jnp.stack([ref[h, C * j:C * (j + 1), :] for j, h in pairs])

        t, u_v, w_k = take(ti_ref), take(uv_ref), take(wk_ref)
        d_v = _bdot_raw(t, take(duv_ref), "TN", pb)
        d_rk = _bdot_raw(t, take(dwk_ref), "TN", pb)
        d_l = -(_bdot_raw(d_v, u_v, "NT", pb) + _bdot_raw(d_rk, w_k, "NT", pb))
        d_ge = jnp.stack([dge_ref[j, h][0:1, 0:1] for j, h in pairs])
        dcq, dck, dcv, db, da, dalog, ddtb = vjp1((d_l, d_v, d_rk, take(dqd_ref), take(dat_ref), take(dke_ref), d_ge))
        lane = lax.broadcasted_iota(jnp.int32, (C, 128), 1)
        lane1 = lax.broadcasted_iota(jnp.int32, (1, 128), 1)
        dgp = jnp.zeros((1, 128), F32)
        for j in range(G):
            rows = slice(C * j, C * (j + 1))
            dba = jnp.zeros((C, 128), F32)
            for h in range(GDN_HEADS):
                b = GDN_HEADS * j + h
                for o_, dcx in ((0, dcq), (GDN_WIDTH, dck), (2 * GDN_WIDTH, dcv)):
                    cols = slice(o_ + GDN_DIM * h, o_ + GDN_DIM * (h + 1))
                    dpre_ref[rows, cols] = dcx[b] * dsilu[rows, cols]
                dba = dba + jnp.where(lane == h, db[b], 0.0) + jnp.where(lane == GDN_HEADS + h, da[b], 0.0)
                dgp = dgp + jnp.where(lane1 == h, dalog[b], 0.0) + jnp.where(lane1 == GDN_HEADS + h, ddtb[b], 0.0)
            dba_ref[rows, :] = dba
        dgp_ref[0:1, :] += dgp

    T = C * G
    return pl.pallas_call(
        body, name="gdn_post", grid=(nc // G,),
        in_specs=[sp["cur"], sp["prev"], sp["ba"], sp["cw"], sp["vec"], sp["hc"], sp["hd"], sp["hd"], sp["hd"], sp["hd"],
                  sp["hd"], sp["hd"], sp["hc"], sp["ge"]],
        out_specs=(sp["cur"], pl.BlockSpec((T, 128), lambda i: (i, 0)), pl.BlockSpec((8, 128), lambda i: (0, 0))),
        out_shape=(jax.ShapeDtypeStruct((S, 3 * GDN_WIDTH), F32), jax.ShapeDtypeStruct((S, 128), F32),
                   jax.ShapeDtypeStruct((8, 128), F32)),
        compiler_params=_cparams(("arbitrary",)),
    )(proj, proj, proj, conv_w, gp, tinv, u_v, w_k, d_uv, d_wk, d_qd, d_ke, d_at, d_ge)


def _conv_bwd(dpre, x, xcol0, w, K, name, tc):
    S, Cc = dpre.shape
    T = _pick_tile(S, 256)
    nt, ncol = S // T, Cc // tc
    xo = xcol0 // tc

    def body(d_ref, dn_ref, x_ref, xp_ref, w_ref, dx_ref, dw_ref):
        i = pl.program_id(1)
        dn = dn_ref[...] * jnp.where(i == nt - 1, 0.0, 1.0)
        dv = d_ref[...]
        ext_d = jnp.concatenate([dv, dn], axis=0)
        dx_ref[...] = _conv_taps_t(ext_d, w_ref[...], K, T).astype(dx_ref.dtype)
        xp = xp_ref[...] * jnp.where(i == 0, 0.0, 1.0)
        ext_x = jnp.concatenate([xp, x_ref[...]], axis=0)

        @pl.when(i == 0)
        def _():
            dw_ref[...] = jnp.zeros_like(dw_ref)

        for k in range(K):
            dw_ref[k:k + 1, :] += jnp.sum(dv * _shifted(ext_x, (K - 1) - k, 8, T), axis=0, keepdims=True)

    r8 = T // 8
    return pl.pallas_call(
        body, name=name, grid=(ncol, nt),
        in_specs=[pl.BlockSpec((T, tc), lambda j, i: (i, j)),
                  pl.BlockSpec((8, tc), lambda j, i: (jnp.minimum((i + 1) * r8, S // 8 - 1), j)),
                  pl.BlockSpec((T, tc), lambda j, i: (i, j + xo)),
                  pl.BlockSpec((8, tc), lambda j, i: (jnp.maximum(i * r8 - 1, 0), j + xo)),
                  pl.BlockSpec((K, tc), lambda j, i: (0, j))],
        out_specs=(pl.BlockSpec((T, tc), lambda j, i: (i, j)), pl.BlockSpec((K, tc), lambda j, i: (0, j))),
        out_shape=(jax.ShapeDtypeStruct((S, Cc), _MXU), jax.ShapeDtypeStruct((K, Cc), F32)),
        compiler_params=_cparams(("parallel", "arbitrary")),
    )(dpre, dpre, x, x, w)


def _dil_bias(nt, T):
    d = (np.arange(nt)[:, None, None] * T + np.arange(T)[None, None, :] - np.arange(T)[None, :, None])
    cnt = ((d >= 0) & (d <= 128)).astype(np.float64) + ((d >= 0) & (d % 4 == 0) & (d <= 512)) + ((d >= 0) & (d % 16 == 0))
    return jnp.asarray(np.where(cnt > 0, np.log(np.maximum(cnt, 1.0)), -1e30), dtype=F32)


def _attn_fwd(proj, after=()):
    S = proj.shape[0]
    T = min(ATT_T, S)
    nt, H = S // T, T // 2
    bias = _dil_bias(nt, T)
    scale = DIL_DIM ** -0.5
    npair = DIL_WIDTH // 128
    qb0, kb0, vb0 = P_QKVB // 128, (P_QKVB + DIL_WIDTH) // 128, (P_QKVB + 2 * DIL_WIDTH) // 128

    def body(q_ref, k_ref, v_ref, b_ref, *rest):
        o_ref, lse_ref = rest[-2:]
        i = pl.program_id(1)
        qs = (q_ref[...] * scale).astype(_MXU)

        def update(carry, kt, vt, qt, bt):
            out = []
            for hh in range(2):
                m, l, acc = carry[hh]
                sl = slice(hh * DIL_DIM, (hh + 1) * DIL_DIM)
                s = lax.dot_general(kt[:, sl], qt[:, sl], (_NT, ((), ())), preferred_element_type=F32) + bt
                m_new = jnp.maximum(m, jnp.max(s, axis=0, keepdims=True))
                p = jnp.exp(s - m_new)
                a = jnp.exp(m - m_new)
                l = a * l + jnp.sum(p, axis=0, keepdims=True)
                acc = a * acc + lax.dot_general(vt[:, sl], p.astype(_MXU), (_TN, ((), ())), preferred_element_type=F32)
                out.append((m_new, l, acc))
            return tuple(out)

        def keys(j):
            rows = pl.ds(pl.multiple_of(j * T, T), T)
            return k_ref[rows, :].astype(_MXU), v_ref[rows, :].astype(_MXU)

        init = tuple((jnp.full((1, T), -1e30, F32), jnp.zeros((1, T), F32), jnp.zeros((DIL_DIM, T), F32)) for _ in range(2))
        res = lax.fori_loop(0, i, lambda j, carry: update(carry, *keys(j), qs, b_ref[i - j]), init)
        kd, vd = keys(i)
        res = update(res, kd[:H], vd[:H], qs, b_ref[0, :H, :])
        late = update(tuple(tuple(t[:, H:] for t in r) for r in res), kd[H:], vd[H:], qs[H:], b_ref[0, H:, H:])
        res = tuple(tuple(jnp.concatenate([t[:, :H], u], axis=1) for t, u in zip(r, r2)) for r, r2 in zip(res, late))
        lse_ref[...] = jnp.zeros_like(lse_ref)
        for hh in range(2):
            m, l, acc = res[hh]
            o_ref[:, hh * DIL_DIM:(hh + 1) * DIL_DIM] = (acc / l).T
            lse_ref[hh:hh + 1, :] = m + jnp.log(l)

    return pl.pallas_call(
        body, name="attn_fwd", grid=(npair, nt),
        in_specs=[pl.BlockSpec((T, 128), lambda p, i: (i, qb0 + p)),
                  pl.BlockSpec((S, 128), lambda p, i: (0, kb0 + p)),
                  pl.BlockSpec((S, 128), lambda p, i: (0, vb0 + p)),
                  pl.BlockSpec((nt, T, T), lambda p, i: (0, 0, 0))] + [_ANY] * len(after),
        out_specs=(pl.BlockSpec((T, 128), lambda p, i: (i, GDN_WIDTH // 128 + p)),
                   pl.BlockSpec((None, None, 8, T), lambda p, i: (p, i, 0, 0))),
        out_shape=(jax.ShapeDtypeStruct((S, GDN_WIDTH + DIL_WIDTH), F32), jax.ShapeDtypeStruct((npair, nt, 8, T), F32)),
        compiler_params=_cparams(("parallel", "parallel")),
    )(proj, proj, proj, bias, *after)


def _attn_bwd(proj, mix, lse, d_mix):
    S = proj.shape[0]
    T = min(ATT_T, S)
    nt, H = S // T, T // 2
    bias = _dil_bias(nt, T)
    scale = DIL_DIM ** -0.5
    npair = DIL_WIDTH // 128
    qb0, kb0, vb0 = P_QKVB // 128, (P_QKVB + DIL_WIDTH) // 128, (P_QKVB + 2 * DIL_WIDTH) // 128

    def body(q_ref, k_ref, v_ref, o_ref, lse_ref, do_ref, b_ref, dq_ref, dk_ref, dv_ref, dq_scr):
        j = pl.program_id(1)

        @pl.when(j == 0)
        def _():
            dq_scr[...] = jnp.zeros_like(dq_scr)

        kt = k_ref[...].astype(_MXU)
        vt = v_ref[...].astype(_MXU)
        ones = jnp.ones((8, DIL_DIM), F32)

        def block(carry, kt, vt, rows, lsev, bt):
            qs = (q_ref[rows, :] * scale).astype(_MXU)
            dov = do_ref[rows, :]
            prod = dov * o_ref[rows, :]
            dob = dov.astype(_MXU)
            out = []
            dqs = []
            for hh in range(2):
                dk, dv = carry[hh]
                sl = slice(hh * DIL_DIM, (hh + 1) * DIL_DIM)
                s = lax.dot_general(kt[:, sl], qs[:, sl], (_NT, ((), ())), preferred_element_type=F32) + bt
                p = jnp.exp(s - lsev[hh:hh + 1, :])
                delta = lax.dot_general(ones, prod[:, sl], (_NT, ((), ())), precision=_HI, preferred_element_type=F32)[0:1, :]
                dp = lax.dot_general(vt[:, sl], dob[:, sl], (_NT, ((), ())), preferred_element_type=F32)
                ds = (p * (dp - delta)).astype(_MXU)
                dv = dv + lax.dot_general(p.astype(_MXU), dob[:, sl], (_NN, ((), ())), preferred_element_type=F32)
                dk = dk + lax.dot_general(ds, qs[:, sl], (_NN, ((), ())), preferred_element_type=F32)
                dqs.append(lax.dot_general(ds, kt[:, sl], (_TN, ((), ())), preferred_element_type=F32) * scale)
                out.append((dk, dv))
            dq_scr[rows, :] += jnp.concatenate(dqs, axis=1)
            return tuple(out)

        def step(i, carry):
            return block(carry, kt, vt, pl.ds(pl.multiple_of(i * T, T), T), lse_ref[i], b_ref[i - j])

        zeros = tuple((jnp.zeros((H, DIL_DIM), F32), jnp.zeros((H, DIL_DIM), F32)) for _ in range(2))
        lsed = lse_ref[j]
        early = block(zeros, kt[:H], vt[:H], pl.ds(pl.multiple_of(j * T, T), T), lsed, b_ref[0, :H, :])
        late = block(zeros, kt[H:], vt[H:], pl.ds(pl.multiple_of(j * T + H, H), H), lsed[:, H:], b_ref[0, H:, H:])
        init = tuple(tuple(jnp.concatenate([t, u], axis=0) for t, u in zip(r, r2)) for r, r2 in zip(early, late))
        res = lax.fori_loop(j + 1, nt, step, init)
        dk_ref[...] = jnp.concatenate([res[0][0], res[1][0]], axis=1).astype(dk_ref.dtype)
        dv_ref[...] = jnp.concatenate([res[0][1], res[1][1]], axis=1).astype(dv_ref.dtype)

        @pl.when(j == nt - 1)
        def _():
            dq_ref[...] = dq_scr[...].astype(dq_ref.dtype)

    full = lambda c0: pl.BlockSpec((S, 128), lambda p, j: (0, c0 + p))
    tile = lambda c0: pl.BlockSpec((T, 128), lambda p, j: (j, c0 + p))
    out3 = jax.ShapeDtypeStruct((S, DIL_WIDTH), _MXU)
    return pl.pallas_call(
        body, name="attn_bwd", grid=(npair, nt),
        in_specs=[full(qb0), tile(kb0), tile(vb0), full(GDN_WIDTH // 128),
                  pl.BlockSpec((None, nt, 8, T), lambda p, j: (p, 0, 0, 0)), full(GDN_WIDTH // 128),
                  pl.BlockSpec((nt, T, T), lambda p, j: (0, 0, 0))],
        out_specs=(full(0), tile(0), tile(0)),
        out_shape=(out3, out3, out3),
        scratch_shapes=[pltpu.VMEM((S, 128), F32)],
        compiler_params=_cparams(("parallel", "arbitrary")),
    )(proj, proj, proj, mix, lse, d_mix, bias)


def _ffn_act(up, cw):
    S, Cc = up.shape[0], up.shape[1] // 2
    T, tc = _pick_tile(S, 256), _pick_tile(Cc, 1536)
    r16 = T // 16
    nct = Cc // tc

    def body(g_ref, gp_ref, u_ref, up_ref, wg_ref, wu_ref, o_ref):
        keep = jnp.where(pl.program_id(1) == 0, 0.0, 1.0)
        cg = _conv_taps(jnp.concatenate([gp_ref[8:16, :].astype(F32) * keep, g_ref[...].astype(F32)], axis=0),
                        wg_ref[...], FFN_CONV, T)
        cu = _conv_taps(jnp.concatenate([up_ref[8:16, :].astype(F32) * keep, u_ref[...].astype(F32)], axis=0),
                        wu_ref[...], FFN_CONV, T)
        o_ref[...] = (_silu(cg) * cu).astype(o_ref.dtype)

    cur = lambda o: pl.BlockSpec((T, tc), lambda j, i: (i, j + o))
    prev = lambda o: pl.BlockSpec((16, tc), lambda j, i: (jnp.maximum(i * r16 - 1, 0), j + o))
    wsp = lambda o: pl.BlockSpec((FFN_CONV, tc), lambda j, i: (0, j + o))
    return pl.pallas_call(
        body, name="ffn_act", grid=(nct, S // T),
        in_specs=[cur(0), prev(0), cur(nct), prev(nct), wsp(0), wsp(nct)], out_specs=cur(0),
        out_shape=jax.ShapeDtypeStruct((S, Cc), _MXU),
        compiler_params=_cparams(("parallel", "parallel")),
    )(up, up, up, up, cw, cw)


def _ffn_act_bwd(d_act, up, cw):
    S, Cc = up.shape[0], up.shape[1] // 2
    T, tc = _pick_tile(S, 256), _pick_tile(Cc, 1536)
    r8, r16 = T // 8, T // 16
    nt = S // T
    nct = Cc // tc
    K = FFN_CONV

    def body(da_ref, dan_ref, g_ref, gp_ref, gn_ref, u_ref, up_ref, un_ref, wg_ref, wu_ref,
             dg_ref, du_ref, dwg_ref, dwu_ref):
        i = pl.program_id(1)
        keep_p = jnp.where(i == 0, 0.0, 1.0)
        keep_n = jnp.where(i == nt - 1, 0.0, 1.0)
        wg, wu = wg_ref[...], wu_ref[...]
        xg = jnp.concatenate([gp_ref[8:16, :].astype(F32) * keep_p, g_ref[...].astype(F32),
                              gn_ref[0:8, :].astype(F32) * keep_n], axis=0)
        xu = jnp.concatenate([up_ref[8:16, :].astype(F32) * keep_p, u_ref[...].astype(F32),
                              un_ref[0:8, :].astype(F32) * keep_n], axis=0)
        cg = _conv_taps(xg, wg, K, T + 8)
        cu = _conv_taps(xu, wu, K, T + 8)
        da = jnp.concatenate([da_ref[...], dan_ref[...] * keep_n], axis=0)
        sg = jax.nn.sigmoid(cg)
        d_cg = da * cu * (sg * (1.0 + cg * (1.0 - sg)))
        d_cu = da * (cg * sg)
        dg_ref[...] = _conv_taps_t(d_cg, wg, K, T).astype(dg_ref.dtype)
        du_ref[...] = _conv_taps_t(d_cu, wu, K, T).astype(du_ref.dtype)

        @pl.when(i == 0)
        def _():
            dwg_ref[...] = jnp.zeros_like(dwg_ref)
            dwu_ref[...] = jnp.zeros_like(dwu_ref)

        for k in range(K):
            dwg_ref[k:k + 1, :] += jnp.sum(d_cg[0:T, :] * _shifted(xg, (K - 1) - k, 8, T), axis=0, keepdims=True)
            dwu_ref[k:k + 1, :] += jnp.sum(d_cu[0:T, :] * _shifted(xu, (K - 1) - k, 8, T), axis=0, keepdims=True)

    cur = lambda o: pl.BlockSpec((T, tc), lambda j, i: (i, j + o))
    prev = lambda o: pl.BlockSpec((16, tc), lambda j, i: (jnp.maximum(i * r16 - 1, 0), j + o))
    nxt = lambda o: pl.BlockSpec((16, tc), lambda j, i: (jnp.minimum((i + 1) * r16, S // 16 - 1), j + o))
    nxt8 = pl.BlockSpec((8, tc), lambda j, i: (jnp.minimum((i + 1) * r8, S // 8 - 1), j))
    wsp = lambda o: pl.BlockSpec((K, tc), lambda j, i: (0, j + o))
    return pl.pallas_call(
        body, name="ffn_act_bwd", grid=(nct, nt),
        in_specs=[cur(0), nxt8, cur(0), prev(0), nxt(0), cur(nct), prev(nct), nxt(nct), wsp(0), wsp(nct)],
        out_specs=(cur(0), cur(0), wsp(0), wsp(0)),
        out_shape=(jax.ShapeDtypeStruct((S, Cc), _MXU), jax.ShapeDtypeStruct((S, Cc), _MXU),
                   jax.ShapeDtypeStruct((K, Cc), F32), jax.ShapeDtypeStruct((K, Cc), F32)),
        compiler_params=_cparams(("parallel", "arbitrary")),
    )(d_act, d_act, up, up, up, up, up, up, cw, cw)


def _local_step(x, tgt, h1, n1w, n2w, fnw, gp, gnw, wp, conv_w, fcw, rest_weights, early_grads):
    proj = _mm(h1, wp, "nn", name="proj")
    u_v, w_k, q_dec, k_end, attn, tinv, g_end = _gdn_pre(proj, conv_w, gp)
    mix, lse = _attn_fwd(proj)
    mix, states = _gdn_scan(u_v, w_k, q_dec, k_end, attn, g_end, proj, gnw, mix, after=[rest_weights[0]([mix])])
    w_out, w_up4, w_down = rest_weights[1]([mix])
    x2 = _mm(mix, w_out, "nn", residual=x, name="outproj")
    h2 = _rmsnorm_fwd(x2, n2w, "norm2")
    up = _mm(h2, w_up4, "nn", b_blocks=True, out_dtype=_MXU, name="up")
    act = _ffn_act(up, fcw)
    x3 = _mm(act, w_down, "nn", residual=x2, name="down")
    loss, dx3, dx3n, d_fnw = _loss_head(x3, fnw, tgt, "loss_head")
    d_act = _mm(dx3n, w_down, "nt", name="d_act")
    d_wdown = _mm(act, dx3n, "tn", name="d_wdown")
    d_upg, d_upu, d_fcwg, d_fcwu = _ffn_act_bwd(d_act, up, fcw)
    d_wup = _mm(h2, d_upg, "tn", place=("blocks", N_CHIPS, 0), tn=w_up4.shape[2], name="d_wgate")
    d_wup = _mm(h2, d_upu, "tn", place=("blocks", N_CHIPS, N_CHIPS // 2), tn=w_up4.shape[2], into=d_wup, name="d_wup")
    d_h2 = _mm_nt_blocks([d_upg, d_upu], w_up4, "d_h2")
    dx2, d_n2w = _rmsnorm_bwd(d_h2, x2, n2w, dx3, "norm2_bwd")
    d_wout = _mm(mix, dx2, "tn", name="d_wout")
    token = early_grads[0](d_wup, d_wdown, d_wout)
    d_mix = _mm(dx2, w_out, "nt", name="d_mix", after=[token])
    dq_b, dk_b, dv_b = _attn_bwd(proj, mix, lse, d_mix)
    d_uv, d_wk, d_qd, d_ke, d_at, d_ge, d_z, d_gnw = _gdn_scan_bwd(u_v, w_k, q_dec, k_end, attn, g_end, proj,
                                                                   gnw, states, d_mix)
    d_pre, d_ba, d_gp = _gdn_post(proj, conv_w, gp, tinv, u_v, w_k, d_uv, d_wk, d_qd, d_ke, d_at, d_ge)
    token = early_grads[1]([d_pre])
    d_qkva, d_convw = _conv_bwd(d_pre, proj, 0, conv_w + token[0:1, 0:1], GDN_CONV, "gdn_conv_bwd", 512)
    d_proj = jnp.concatenate([d_qkva, d_z.astype(_MXU), dq_b, dk_b, dv_b, d_ba.astype(_MXU),
                              jnp.zeros((x.shape[0], P_COLS - P_BA - 128), _MXU)], axis=1)
    d_wp = _mm(h1, d_proj, "tn", name="d_wp")
    token = early_grads[2](d_wp)
    d_h1 = _mm(d_proj, wp, "nt", name="d_h1", after=[token])
    token = early_grads[3]([d_h1])
    dx, d_n1w = _rmsnorm_bwd(d_h1, x, n1w, dx2, "norm1_bwd", after=[token])
    grads = dict(wp=d_wp, conv_w=d_convw, w_out=d_wout, w_up=d_wup, fcw_g=d_fcwg, fcw_u=d_fcwu, w_down=d_wdown,
                 n1w=d_n1w, n2w=d_n2w, fnw=d_fnw, gp=d_gp, gnw=d_gnw)
    return loss, dx, grads


_HBM = pl.BlockSpec(memory_space=pltpu.HBM)


def _pos():
    return lax.axis_index("x"), lax.axis_index("y"), lax.axis_index("c")


def _other_chips(x, y):
    return [(1 - x, y), (x, 1 - y), (1 - x, 1 - y)]


def _halvable(shape):
    return shape[0] % 32 == 0


def _rows_of_half(shape, half):
    if not _halvable(shape):
        return pl.ds(0, shape[0])
    return pl.ds(pl.multiple_of(half * (shape[0] // 2), 16), shape[0] // 2)


_SEM = pl.BlockSpec(memory_space=pltpu.SEMAPHORE)
_ANY = pl.BlockSpec(memory_space=pl.ANY)
_DATAFLOW = pltpu.SideEffectType.DATAFLOW_SIDE_EFFECTING


def _in_hbm(a):
    return pltpu.with_memory_space_constraint(a, pltpu.HBM)


def _halves_copy(src_refs, land_refs, send_sems, recv_sems, shapes, a, j, block, x, y, c):
    px, py = _other_chips(x, y)[j]
    rows = _rows_of_half(shapes[a], c)
    return pltpu.make_async_remote_copy(
        src_ref=src_refs[a].at[rows, :], dst_ref=land_refs[a].at[block, rows, :], send_sem=send_sems.at[3 * a + j],
        recv_sem=recv_sems.at[3 * a + j], device_id=(px, py, c), device_id_type=MESH)


def _gather_halves_start(shards, after, name):
    n = len(shards)
    shapes = [s.shape for s in shards]

    def body(*refs):
        ins, lands = refs[:n], refs[n:2 * n]
        send_sems, recv_sems = refs[2 * n + 1], refs[2 * n + 2]
        token = refs[-1]
        x, y, c = _pos()
        q = 2 * x + y
        for a in range(n):
            for j in range(3):
                _halves_copy(ins, lands, send_sems, recv_sems, shapes, a, j, q, x, y, c).start()
        token[...] = jnp.zeros_like(token)

    land_shapes = [(N_CHIPS,) + s.shape for s in shards]
    return pl.pallas_call(
        body, name=name,
        out_shape=(pltpu.SemaphoreType.DMA((3 * n,)), pltpu.SemaphoreType.DMA((3 * n,)),
                   *[pltpu.HBM(s.shape, s.dtype) for s in shards],
                   *[pltpu.HBM(ls, s.dtype) for ls, s in zip(land_shapes, shards)],
                   jax.ShapeDtypeStruct((8, 128), F32)),
        in_specs=[_HBM] * (2 * n) + [_ANY],
        out_specs=(_SEM, _SEM, *[_HBM] * (2 * n), pl.BlockSpec(memory_space=pltpu.VMEM)),
        input_output_aliases={a: 2 + a for a in range(2 * n)},
        compiler_params=pltpu.CompilerParams(has_side_effects=_DATAFLOW),
    )(*[_in_hbm(s) for s in shards], *[_in_hbm(lax.empty(ls, s.dtype)) for ls, s in zip(land_shapes, shards)], after)


def _gather_halves_wait(started, after, name):
    send_sems, recv_sems, *thru = started
    n = len(thru) // 2
    shapes = [t.shape for t in thru[:n]]

    def body(*refs):
        ins, lands = refs[:n], refs[n:2 * n]
        send_sems, recv_sems = refs[2 * n], refs[2 * n + 1]
        x, y, c = _pos()
        q = 2 * x + y
        chips = _other_chips(x, y)
        for a in range(n):
            for j, (px, py) in enumerate(chips):
                _halves_copy(ins, lands, send_sems, recv_sems, shapes, a, j, q, x, y, c).wait_send()
                _halves_copy(ins, lands, send_sems, recv_sems, shapes, a, j, 2 * px + py, x, y, c).wait_recv()

    outs = pl.pallas_call(
        body, name=name, out_shape=[pltpu.HBM(t.shape, t.dtype) for t in thru],
        in_specs=[_HBM] * (2 * n) + [_SEM, _SEM] + [_ANY] * len(after), out_specs=[_HBM] * (2 * n),
        input_output_aliases={a: a for a in range(2 * n)},
        compiler_params=pltpu.CompilerParams(has_side_effects=_DATAFLOW),
    )(*thru, send_sems, recv_sems, *after)
    return outs[:n], outs[n:]


def _sibling_fill(gathered, name):
    big = [a for a, g in enumerate(gathered) if _halvable(g.shape[1:])]
    n = len(gathered)

    def body(*refs):
        ins, outs = refs[:n], refs[n:2 * n]
        send_sems, recv_sems = refs[2 * n:]
        x, y, c = _pos()
        chips = _other_chips(x, y)

        def copy(k, j, half):
            a = big[k]
            px, py = chips[j]
            rows = _rows_of_half(gathered[a].shape[1:], half)
            return pltpu.make_async_remote_copy(
                src_ref=ins[a].at[2 * px + py, rows, :], dst_ref=outs[a].at[2 * px + py, rows, :],
                send_sem=send_sems.at[3 * k + j], recv_sem=recv_sems.at[3 * k + j],
                device_id=(x, y, 1 - c), device_id_type=MESH)

        sends = [copy(k, j, c) for k in range(len(big)) for j in range(3)]
        for cp in sends:
            cp.start()
        for k in range(len(big)):
            for j in range(3):
                copy(k, j, 1 - c).wait_recv()
        for cp in sends:
            cp.wait_send()

    return pl.pallas_call(
        body, name=name, in_specs=[_HBM] * n, out_specs=[_HBM] * n,
        out_shape=[jax.ShapeDtypeStruct(g.shape, g.dtype) for g in gathered],
        input_output_aliases={a: a for a in range(n)},
        scratch_shapes=[pltpu.SemaphoreType.DMA((3 * len(big),)), pltpu.SemaphoreType.DMA((3 * len(big),))],
    )(*gathered)


def _fill_copy(refs, send_sems, recv_sems, shapes, a, j, half, x, y, c):
    px, py = _other_chips(x, y)[j]
    rows = _rows_of_half(shapes[a], half)
    return pltpu.make_async_remote_copy(
        src_ref=refs[a].at[2 * px + py, rows, :], dst_ref=refs[a].at[2 * px + py, rows, :],
        send_sem=send_sems.at[3 * a + j], recv_sem=recv_sems.at[3 * a + j],
        device_id=(x, y, 1 - c), device_id_type=MESH)


def _sibling_fill_start(gathered, name):
    n = len(gathered)
    shapes = [g.shape[1:] for g in gathered]

    def body(*refs):
        ins = refs[:n]
        send_sems, recv_sems = refs[n], refs[n + 1]
        token = refs[-1]
        x, y, c = _pos()
        for a in range(n):
            for j in range(3):
                _fill_copy(ins, send_sems, recv_sems, shapes, a, j, c, x, y, c).start()
        token[...] = jnp.zeros_like(token)

    return pl.pallas_call(
        body, name=name,
        out_shape=(pltpu.SemaphoreType.DMA((3 * n,)), pltpu.SemaphoreType.DMA((3 * n,)),
                   *[pltpu.HBM(g.shape, g.dtype) for g in gathered], jax.ShapeDtypeStruct((8, 128), F32)),
        in_specs=[_HBM] * n,
        out_specs=(_SEM, _SEM, *[_HBM] * n, pl.BlockSpec(memory_space=pltpu.VMEM)),
        input_output_aliases={a: 2 + a for a in range(n)},
        compiler_params=pltpu.CompilerParams(has_side_effects=_DATAFLOW),
    )(*[_in_hbm(g) for g in gathered])


def _sibling_fill_wait(started, after, name):
    send_sems, recv_sems, *thru = started
    n = len(thru)
    shapes = [t.shape[1:] for t in thru]

    def body(*refs):
        ins = refs[:n]
        send_sems, recv_sems = refs[n], refs[n + 1]
        x, y, c = _pos()
        for a in range(n):
            for j in range(3):
                _fill_copy(ins, send_sems, recv_sems, shapes, a, j, c, x, y, c).wait_send()
                _fill_copy(ins, send_sems, recv_sems, shapes, a, j, 1 - c, x, y, c).wait_recv()

    return pl.pallas_call(
        body, name=name, out_shape=[pltpu.HBM(t.shape, t.dtype) for t in thru],
        in_specs=[_HBM] * n + [_SEM, _SEM] + [_ANY] * len(after), out_specs=[_HBM] * n,
        input_output_aliases={a: a for a in range(n)},
        compiler_params=pltpu.CompilerParams(has_side_effects=_DATAFLOW),
    )(*thru, send_sems, recv_sems, *after)


def _place_own(shards, gathered, cq, name, carry=()):
    n = len(shards)
    nc = len(carry)
    steps = 4

    def body(cq_ref, *refs):
        for a in range(n):
            refs[2 * n + nc + a][...] = refs[a][...]

    def tile(shape):
        return shape[0] // steps if _halvable(shape) else shape[0]

    in_specs = [pl.BlockSpec((tile(s.shape), s.shape[1]), (lambda i, s_: (i, 0)) if _halvable(s.shape) else (lambda i, s_: (0, 0)))
                for s in shards]
    in_specs += [pl.BlockSpec(memory_space=pl.ANY)] * (n + nc)
    out_specs = [pl.BlockSpec((None, tile(s.shape), s.shape[1]),
                              (lambda i, s_: (s_[1], i, 0)) if _halvable(s.shape) else (lambda i, s_: (s_[1], 0, 0)))
                 for s in shards]
    out_specs += [pl.BlockSpec(memory_space=pl.ANY)] * nc
    gs = pltpu.PrefetchScalarGridSpec(num_scalar_prefetch=1, grid=(steps,), in_specs=in_specs, out_specs=out_specs)
    outs = pl.pallas_call(
        body, name=name, grid_spec=gs,
        out_shape=[jax.ShapeDtypeStruct(g.shape, g.dtype) for g in gathered] + [jax.ShapeDtypeStruct(t.shape, t.dtype) for t in carry],
        input_output_aliases={1 + n + a: a for a in range(n + nc)},
        compiler_params=_cparams(("arbitrary",)),
    )(cq, *shards, *gathered, *carry)
    return (outs[:n], outs[n:]) if nc else outs


def _half_rows(ref, c, rh):
    return ref.at[:, pl.ds(pl.multiple_of(c * rh, 8), rh), :]


def _chips_copy(src_refs, land_refs, send_sems, recv_sems, a, j, x, y, c):
    px, py = _other_chips(x, y)[j]
    return pltpu.make_async_remote_copy(src_ref=src_refs[a].at[2 * px + py], dst_ref=land_refs[a].at[j],
                                        send_sem=send_sems.at[3 * a + j], recv_sem=recv_sems.at[3 * a + j],
                                        device_id=(px, py, c), device_id_type=MESH)


def _peer(r, x, y, c):
    return (x if r & 4 == 0 else 1 - x), (y if r & 2 == 0 else 1 - y), (c if r & 1 == 0 else 1 - c)


def _small_copy(small_ref, all_ref, send_sems, recv_sems, base, r, slot, x, y, c):
    return pltpu.make_async_remote_copy(src_ref=small_ref, dst_ref=all_ref.at[slot], send_sem=send_sems.at[base + r - 1],
                                        recv_sem=recv_sems.at[base + r - 1], device_id=_peer(r, x, y, c), device_id_type=MESH)


def _grad_chips_start(parts, name, small=None):
    n = len(parts)
    srcs = list(parts) + ([] if small is None else [small])
    m = len(srcs)

    def body(*refs):
        ins, lands = refs[:m], refs[m:2 * m]
        send_sems, recv_sems = refs[2 * m], refs[2 * m + 1]
        token = refs[-1]
        x, y, c = _pos()
        for a in range(n):
            for j in range(3):
                _chips_copy(ins, lands, send_sems, recv_sems, a, j, x, y, c).start()
        if small is not None:
            for r in range(1, 8):
                _small_copy(ins[n], lands[n], send_sems, recv_sems, 3 * n, r, 4 * x + 2 * y + c, x, y, c).start()
        token[...] = jnp.zeros_like(token)

    land_shapes = [(3,) + p.shape[1:] for p in parts] + ([] if small is None else [(8,) + small.shape])
    nsem = 3 * n + (0 if small is None else 7)
    return pl.pallas_call(
        body, name=name,
        out_shape=(pltpu.SemaphoreType.DMA((nsem,)), pltpu.SemaphoreType.DMA((nsem,)),
                   *[pltpu.HBM(p.shape, p.dtype) for p in srcs],
                   *[pltpu.HBM(ls, p.dtype) for ls, p in zip(land_shapes, srcs)],
                   jax.ShapeDtypeStruct((8, 128), F32)),
        in_specs=[_HBM] * (2 * m),
        out_specs=(_SEM, _SEM, *[_HBM] * (2 * m), pl.BlockSpec(memory_space=pltpu.VMEM)),
        input_output_aliases={a: 2 + a for a in range(2 * m)},
        compiler_params=pltpu.CompilerParams(has_side_effects=_DATAFLOW),
    )(*[_in_hbm(p) for p in srcs], *[_in_hbm(lax.empty(ls, p.dtype)) for ls, p in zip(land_shapes, srcs)])


def _grad_chips_wait(started, after, name, with_small=False):
    send_sems, recv_sems, *thru = started
    m = len(thru) // 2
    n = m - (1 if with_small else 0)

    def body(*refs):
        ins, lands = refs[:m], refs[m:2 * m]
        send_sems, recv_sems = refs[2 * m], refs[2 * m + 1]
        x, y, c = _pos()
        for a in range(n):
            for j in range(3):
                cp = _chips_copy(ins, lands, send_sems, recv_sems, a, j, x, y, c)
                cp.wait_send()
                cp.wait_recv()
        if with_small:
            for r in range(1, 8):
                px, py, pc = _peer(r, x, y, c)
                _small_copy(ins[n], lands[n], send_sems, recv_sems, 3 * n, r, 4 * x + 2 * y + c, x, y, c).wait_send()
                _small_copy(ins[n], lands[n], send_sems, recv_sems, 3 * n, r, 4 * px + 2 * py + pc, x, y, c).wait_recv()

    outs = pl.pallas_call(
        body, name=name, out_shape=[pltpu.HBM(t.shape, t.dtype) for t in thru],
        in_specs=[_HBM] * (2 * m) + [_SEM, _SEM] + [_ANY] * len(after), out_specs=[_HBM] * (2 * m),
        input_output_aliases={a: a for a in range(2 * m)},
        compiler_params=pltpu.CompilerParams(has_side_effects=_DATAFLOW),
    )(*thru, send_sems, recv_sems, *after)
    return list(outs[m:]) + list(outs[n:m])


def _sibling_copy(src_refs, land_refs, send_sems, recv_sems, rhs, a, c, x, y):
    return pltpu.make_async_remote_copy(src_ref=_half_rows(src_refs[a], 1 - c, rhs[a]), dst_ref=land_refs[a],
                                        send_sem=send_sems.at[a], recv_sem=recv_sems.at[a],
                                        device_id=(x, y, 1 - c), device_id_type=MESH)


def _grad_sibling_start(fams, name):
    n = len(fams)
    rhs = [f.shape[1] // 2 for f in fams]

    def body(*refs):
        ins, lands = refs[:n], refs[n:2 * n]
        send_sems, recv_sems = refs[2 * n], refs[2 * n + 1]
        token = refs[-1]
        x, y, c = _pos()
        for a in range(n):
            _sibling_copy(ins, lands, send_sems, recv_sems, rhs, a, c, x, y).start()
        token[...] = jnp.zeros_like(token)

    land_shapes = [(f.shape[0], f.shape[1] // 2, f.shape[2]) for f in fams]
    return pl.pallas_call(
        body, name=name,
        out_shape=(pltpu.SemaphoreType.DMA((n,)), pltpu.SemaphoreType.DMA((n,)),
                   *[pltpu.HBM(f.shape, f.dtype) for f in fams],
                   *[pltpu.HBM(ls, f.dtype) for ls, f in zip(land_shapes, fams)],
                   jax.ShapeDtypeStruct((8, 128), F32)),
        in_specs=[_HBM] * (2 * n),
        out_specs=(_SEM, _SEM, *[_HBM] * (2 * n), pl.BlockSpec(memory_space=pltpu.VMEM)),
        input_output_aliases={a: 2 + a for a in range(2 * n)},
        compiler_params=pltpu.CompilerParams(has_side_effects=_DATAFLOW),
    )(*[_in_hbm(f) for f in fams], *[_in_hbm(lax.empty(ls, f.dtype)) for ls, f in zip(land_shapes, fams)])


def _grad_sibling_wait(started, after, name):
    send_sems, recv_sems, *thru = started
    n = len(thru) // 2
    rhs = [t.shape[1] // 2 for t in thru[:n]]

    def body(*refs):
        ins, lands = refs[:n], refs[n:2 * n]
        send_sems, recv_sems = refs[2 * n], refs[2 * n + 1]
        x, y, c = _pos()
        for a in range(n):
            cp = _sibling_copy(ins, lands, send_sems, recv_sems, rhs, a, c, x, y)
            cp.wait_send()
            cp.wait_recv()

    outs = pl.pallas_call(
        body, name=name, out_shape=[pltpu.HBM(t.shape, t.dtype) for t in thru],
        in_specs=[_HBM] * (2 * n) + [_SEM, _SEM] + [_ANY] * len(after), out_specs=[_HBM] * (2 * n),
        input_output_aliases={a: a for a in range(2 * n)},
        compiler_params=pltpu.CompilerParams(has_side_effects=_DATAFLOW),
    )(*thru, send_sems, recv_sems, *after)
    return outs[:n], outs[n:]


def _grad_share(fulls, name):
    n = len(fulls)
    rhs = [f.shape[0] // 2 for f in fulls]

    def body(*refs):
        ins, outs = refs[:n], refs[n:2 * n]
        send_sems, recv_sems = refs[2 * n], refs[2 * n + 1]
        x, y, c = _pos()

        def copy(a, half):
            rows = pl.ds(pl.multiple_of(half * rhs[a], 8), rhs[a])
            return pltpu.make_async_remote_copy(src_ref=ins[a].at[rows, :], dst_ref=outs[a].at[rows, :],
                                                send_sem=send_sems.at[a], recv_sem=recv_sems.at[a],
                                                device_id=(x, y, 1 - c), device_id_type=MESH)

        sends = [copy(a, c) for a in range(n)]
        for cp in sends:
            cp.start()
        for a in range(n):
            copy(a, 1 - c).wait_recv()
        for cp in sends:
            cp.wait_send()

    return pl.pallas_call(
        body, name=name, in_specs=[_HBM] * n, out_specs=[_HBM] * n,
        out_shape=[jax.ShapeDtypeStruct(f.shape, f.dtype) for f in fulls],
        input_output_aliases={a: a for a in range(n)},
        scratch_shapes=[pltpu.SemaphoreType.DMA((n,)), pltpu.SemaphoreType.DMA((n,))],
    )(*fulls)


def _add_sibling(own, recv, cq, name):
    nb, R, Cc = own.shape
    Rh = R // 2

    def body(cq_ref, a_ref, b_ref, o32_ref, o16_ref):
        s = a_ref[0] + b_ref[0]
        mine = pl.program_id(0) == cq_ref[1]

        @pl.when(mine)
        def _():
            o32_ref[...] = s

        @pl.when(jnp.logical_not(mine))
        def _():
            o16_ref[0] = s.astype(o16_ref.dtype)

    sp = pl.BlockSpec((1, Rh, Cc), lambda b, s: (b, 0, 0))
    gs = pltpu.PrefetchScalarGridSpec(
        num_scalar_prefetch=1, grid=(nb,),
        in_specs=[pl.BlockSpec((1, Rh, Cc), lambda b, s: (b, s[0], 0)), sp],
        out_specs=[pl.BlockSpec((Rh, Cc), lambda b, s: (0, 0)), sp])
    return pl.pallas_call(
        body, name=name, grid_spec=gs,
        out_shape=[jax.ShapeDtypeStruct((Rh, Cc), F32), jax.ShapeDtypeStruct((nb, Rh, Cc), _MXU)],
        compiler_params=_cparams(("arbitrary",)),
    )(cq, own, recv)


def _add_sibling_split(d_wp, recv, cq, name):
    _, Dm, Pc = d_wp.shape
    Rh = Dm // 2
    Wb = IN_COLS // N_CHIPS
    T = 256

    def body(cq_ref, a_ref, b_ref, o32_ref, o16_ref):
        s = a_ref[0] + b_ref[0]
        blocks = [s[:, 0:Wb], s[:, Wb:2 * Wb],
                  jnp.concatenate([s[:, 2 * Wb:P_QKVB], s[:, P_BA:P_BA + 8], s[:, P_QKVB:3 * Wb - 8]], axis=1),
                  s[:, 3 * Wb - 8:P_BA]]
        q = cq_ref[1]
        own = None
        for j, blk in enumerate(blocks):
            term = jnp.where(q == j, blk, 0.0)
            own = term if own is None else own + term
            o16_ref[j] = blk.astype(o16_ref.dtype)
        o32_ref[...] = own

    gs = pltpu.PrefetchScalarGridSpec(
        num_scalar_prefetch=1, grid=(Rh // T,),
        in_specs=[pl.BlockSpec((1, T, Pc), lambda i, s: (0, s[0] * (Rh // T) + i, 0)), pl.BlockSpec((1, T, Pc), lambda i, s: (0, i, 0))],
        out_specs=[pl.BlockSpec((T, Wb), lambda i, s: (i, 0)), pl.BlockSpec((N_CHIPS, T, Wb), lambda i, s: (0, i, 0))])
    return pl.pallas_call(
        body, name=name, grid_spec=gs,
        out_shape=[jax.ShapeDtypeStruct((Rh, Wb), F32), jax.ShapeDtypeStruct((N_CHIPS, Rh, Wb), _MXU)],
        compiler_params=_cparams(("parallel",)),
    )(cq, d_wp, recv)


def _add_chips(part32, recv3, cq, name):
    Rh, Cc = part32.shape

    def body(cq_ref, a_ref, b_ref, o_ref):
        acc = a_ref[...]
        for j in range(3):
            acc = acc + b_ref[j].astype(F32)
        o_ref[...] = acc

    gs = pltpu.PrefetchScalarGridSpec(
        num_scalar_prefetch=1, grid=(1,),
        in_specs=[pl.BlockSpec((Rh, Cc), lambda i, s: (0, 0)), pl.BlockSpec((3, Rh, Cc), lambda i, s: (0, 0, 0))],
        out_specs=pl.BlockSpec((Rh, Cc), lambda i, s: (s[0], 0)))
    return pl.pallas_call(
        body, name=name, grid_spec=gs, out_shape=jax.ShapeDtypeStruct((2 * Rh, Cc), F32),
        compiler_params=_cparams(("arbitrary",)),
    )(cq, part32, recv3)


def _transposed(g):
    Dm, n = g.shape
    pad = -n % 128

    def body(g_ref, o_ref):
        xp = jnp.concatenate([g_ref[...], jnp.zeros((Dm, pad), F32)], axis=1)
        o_ref[...] = xp.T[:n, :]

    return pl.pallas_call(body, name="transposed", out_shape=jax.ShapeDtypeStruct((n, Dm), F32),
                          compiler_params=_cparams(vmem=V7X_VMEM_LIMIT))(g)


def _adamw(w, g, m, v, name):
    R, Cc = w.shape
    T = max([t for t in range(8, 257, 8) if R % t == 0], default=R)

    def body(w_ref, g_ref, m_ref, v_ref, d_ref, mo_ref, vo_ref):
        d_ref[...], mo_ref[...], vo_ref[...] = _adamw_math(w_ref[...], g_ref[...], m_ref[...], v_ref[...])

    sp = pl.BlockSpec((T, Cc), lambda i: (i, 0))
    sh = jax.ShapeDtypeStruct((R, Cc), F32)
    return pl.pallas_call(
        body, name=name, grid=(R // T,), in_specs=[sp] * 4, out_specs=(sp, sp, sp), out_shape=(sh, sh, sh),
        compiler_params=_cparams(("parallel",)),
    )(w, g, m, v)


SMALL_ROWS = 32
ROW_CONV, ROW_FCG, ROW_FCU = 5, 13, 22


def _adamw_math(w, g, m, v):
    mn = ADAM_B1 * m + (1.0 - ADAM_B1) * g
    vn = ADAM_B2 * v + (1.0 - ADAM_B2) * (g * g)
    c1 = 1.0 / (1.0 - ADAM_B1 ** ADAM_STEP)
    c2 = 1.0 / (1.0 - ADAM_B2 ** ADAM_STEP)
    return -ADAM_LR * ((mn * c1) / (jnp.sqrt(vn * c2) + ADAM_EPS) + ADAM_WD * w), mn, vn


def _pack_small(n1, n2, fn, gp, gn, conv, fcg, fcu, loss):
    W = D_MODEL

    def body(n1_ref, n2_ref, fn_ref, gp_ref, gn_ref, conv_ref, fcg_ref, fcu_ref, loss_ref, o_ref):
        o_ref[...] = jnp.zeros_like(o_ref)
        o_ref[0:1, :] = n1_ref[...]
        o_ref[1:2, :] = n2_ref[...]
        o_ref[2:3, :] = fn_ref[...]
        o_ref[3:4, 0:8] = gp_ref[0:1, 0:8]
        o_ref[3:4, 8:9] = loss_ref[0:1, 0:1]
        o_ref[4:5, 0:128] = gn_ref[...]
        for i in range(GDN_CONV):
            o_ref[ROW_CONV + 2 * i:ROW_CONV + 2 * i + 1, :] = conv_ref[i:i + 1, 0:W]
            o_ref[ROW_CONV + 2 * i + 1:ROW_CONV + 2 * i + 2, 0:3 * GDN_WIDTH - W] = conv_ref[i:i + 1, W:3 * GDN_WIDTH]
        for r0, ref in ((ROW_FCG, fcg_ref), (ROW_FCU, fcu_ref)):
            for i in range(FFN_CONV):
                for k in range(3):
                    n = min(W, D_FF - k * W)
                    o_ref[r0 + 3 * i + k:r0 + 3 * i + k + 1, 0:n] = ref[i:i + 1, k * W:k * W + n]

    return pl.pallas_call(body, name="pack_small", out_shape=jax.ShapeDtypeStruct((SMALL_ROWS, W), F32))(
        n1, n2, fn, gp, gn, conv, fcg, fcu, loss)


def _small_step(meq, small_all, small, ws, ms, vs):
    W = D_MODEL
    n = len(ws)
    cw, fw = ws[6].shape[1], ws[7].shape[1]

    def body(meq_ref, all_ref, own_ref, *refs):
        w_refs, m_refs, v_refs = refs[:n], refs[n:2 * n], refs[2 * n:3 * n]
        loss_ref = refs[3 * n]
        outs = refs[3 * n + 1:]
        me, q = meq_ref[0], meq_ref[1]
        red = None
        for d in range(8):
            term = jnp.where(me == d, own_ref[...], all_ref[d])
            red = term if red is None else red + term
        loss_ref[...] = jnp.broadcast_to(red[3:4, 8:9], loss_ref.shape)
        conv = [jnp.concatenate([red[ROW_CONV + 2 * i:ROW_CONV + 2 * i + 1, :],
                                 red[ROW_CONV + 2 * i + 1:ROW_CONV + 2 * i + 2, 0:3 * GDN_WIDTH - W]], axis=1)
                for i in range(GDN_CONV)]
        conv = jnp.concatenate(conv, axis=0)

        def fc_rows(r0):
            rows = [jnp.concatenate([red[r0 + 3 * i + k:r0 + 3 * i + k + 1, 0:min(W, D_FF - k * W)] for k in range(3)], axis=1)
                    for i in range(FFN_CONV)]
            return jnp.concatenate(rows, axis=0)

        fc = jnp.concatenate([fc_rows(ROW_FCG), fc_rows(ROW_FCU)], axis=1)

        def chip_block(full, width):
            out = None
            for j in range(N_CHIPS):
                term = jnp.where(q == j, full[:, width * j:width * (j + 1)], 0.0)
                out = term if out is None else out + term
            return out

        grads = [red[0:1, :], red[1:2, :], red[2:3, :], red[3:4, 0:4], red[3:4, 4:8], red[4:5, 0:128],
                 chip_block(conv, cw), chip_block(fc, fw)]
        for k in range(n):
            d_, m_, v_ = _adamw_math(w_refs[k][...], grads[k], m_refs[k][...], v_refs[k][...])
            outs[4 * k][...] = grads[k]
            outs[4 * k + 1][...] = d_
            outs[4 * k + 2][...] = m_
            outs[4 * k + 3][...] = v_

    full = lambda a: pl.BlockSpec(a.shape, lambda i, s_, nd=len(a.shape): (0,) * nd)
    arrays = [small_all, small, *ws, *ms, *vs]
    out_shapes = [jax.ShapeDtypeStruct((8, 128), F32)] + [jax.ShapeDtypeStruct(w.shape, F32) for w in ws for _ in range(4)]
    gs = pltpu.PrefetchScalarGridSpec(
        num_scalar_prefetch=1, grid=(1,), in_specs=[full(a) for a in arrays],
        out_specs=[pl.BlockSpec(o.shape, lambda i, s_, nd=len(o.shape): (0,) * nd) for o in out_shapes])
    return pl.pallas_call(body, name="small_step", grid_spec=gs, out_shape=out_shapes)(meq, *arrays)


def _pad_lanes(v, n=D_MODEL):
    return jnp.pad(v, ((0, 0), (0, n - v.shape[1])))


def kernel(x, norm1_w, w_in, conv_qkv_w, a_log, dt_bias, gdn_norm_w, w_out, norm2_w, w_up, ffn_conv_w, w_down, final_norm_w, loss_target, m_norm1_w, m_w_in, m_conv_qkv_w, m_a_log, m_dt_bias, m_gdn_norm_w, m_w_out, m_norm2_w, m_w_up, m_ffn_conv_w, m_w_down, m_final_norm_w, v_norm1_w, v_w_in, v_conv_qkv_w, v_a_log, v_dt_bias, v_gdn_norm_w, v_w_out, v_norm2_w, v_w_up, v_ffn_conv_w, v_w_down, v_final_norm_w):
    c = lax.axis_index("c")
    q = 2 * lax.axis_index("x") + lax.axis_index("y")
    S = x.shape[1]
    cq = jnp.stack([c, q]).astype(jnp.int32)

    *in_started, in_token = _gather_halves_start([w_in[0].astype(_MXU), conv_qkv_w[0], ffn_conv_w[0]], x, "gather_in_start")
    w_in_l, m_w_in_l, v_w_in_l = (jnp.swapaxes(a + in_token[0:1, 0:1], 1, 2)[0] for a in (w_in, m_w_in, v_w_in))
    h1 = _rmsnorm_fwd(x[0], norm1_w, "norm1", after=[in_token])
    rest = [(a[0] + in_token[0:1, 0:1]).astype(_MXU) for a in (w_out, w_up, w_down)]
    in_shards, got_in = _gather_halves_wait(in_started, [w_in_l, m_w_in_l, v_w_in_l, h1, *rest], "gather_in_wait")
    (g_in, g_conv, g_fconv), (w_in_l, m_w_in_l, v_w_in_l) = _place_own(
        in_shards, _sibling_fill(got_in, "fill_in"), cq, "place_in", carry=[w_in_l, m_w_in_l, v_w_in_l])
    *rest_started, token = _gather_halves_start(rest, g_conv, "gather_rest_start")

    rest_state = {}

    def rest_arrived(after):
        rest_state["shards"], got = _gather_halves_wait(rest_started, after, "gather_rest_wait")
        *rest_state["fill"], tok = _sibling_fill_start(got, "fill_rest_start")
        return tok

    def rest_filled(after):
        got = _sibling_fill_wait(rest_state["fill"], after, "fill_rest_wait")
        g_out, g_up, g_down = _place_own(rest_state["shards"], got, cq, "place_rest")
        return g_out.reshape(D_MODEL, D_MODEL), g_up, g_down.reshape(D_FF, D_MODEL)

    rest_weights = (rest_arrived, rest_filled)
    wp = _wp_assemble(g_in, [token])
    conv_f = jnp.concatenate([g_conv[i] for i in range(N_CHIPS)], axis=1)
    fcw = jnp.concatenate([g_fconv[i] for i in range(N_CHIPS)], axis=1)
    gp = _pad_lanes(jnp.concatenate([a_log, dt_bias], axis=1), 128)
    fnw = final_norm_w[None, :]
    early = {}

    early_names = ("w_up", "w_down", "w_out")

    def early_sibling(d_wup, d_wdown, d_wout):
        *early["sibling"], tok = _grad_sibling_start(
            [d_wup, d_wdown.reshape(N_CHIPS, D_FF // N_CHIPS, D_MODEL), d_wout.reshape(N_CHIPS, D_MODEL // N_CHIPS, D_MODEL)],
            "grad_sibling_early_start")
        return tok

    def early_chips(after):
        fams_e, got_e = _grad_sibling_wait(early["sibling"], after, "grad_sibling_early_wait")
        early["parts"] = [_add_sibling(f, r, cq, "add_sibling_" + nm) for f, r, nm in zip(fams_e, got_e, early_names)]
        *early["started"], tok = _grad_chips_start([p[1] for p in early["parts"]], "grad_chips_start")
        return tok

    def late_sibling(d_wp):
        *early["late_sibling"], tok = _grad_sibling_start([d_wp[None]], "grad_sibling_late_start")
        return tok

    def late_chips(after):
        fams, got = _grad_sibling_wait(early["late_sibling"], after, "grad_sibling_late_wait")
        early["late_part"] = _add_sibling_split(fams[0], got[0], cq, "add_sibling_w_in")
        *early["late_started"], tok = _grad_chips_start([early["late_part"][1]], "grad_chips_late_start")
        return tok

    early_grads = (early_sibling, early_chips, late_sibling, late_chips)

    loss_l, dx, g = _local_step(x[0], loss_target[0], h1, norm1_w, norm2_w, fnw, gp, gdn_norm_w, wp,
                                conv_f, fcw, rest_weights, early_grads)
    small = _pack_small(g["n1w"], g["n2w"], g["fnw"], g["gp"], g["gnw"], g["conv_w"], g["fcw_g"], g["fcw_u"], loss_l)
    *small_started, small_token = _grad_chips_start([], "small_gather_start", small)
    got3_e = _grad_chips_wait(early["started"], [dx, small_token], "grad_chips_wait")
    g_w_up, g_w_down, g_w_out = _grad_share(
        [_add_chips(p[0], r3, cq, "add_chips_" + nm) for p, r3, nm in zip(early["parts"], got3_e, early_names)],
        "grad_share_early")
    big = {}

    def adamw_big(nm, w, gg, m, v):
        d_, m_, v_ = _adamw(w[0], gg, m[0], v[0], "adamw_" + nm)
        big[nm] = (gg[None], d_[None], m_[None], v_[None])

    adamw_big("w_up", w_up, g_w_up, m_w_up, v_w_up)
    adamw_big("w_down", w_down, g_w_down, m_w_down, v_w_down)
    adamw_big("w_out", w_out, g_w_out, m_w_out, v_w_out)
    got3, = _grad_chips_wait(early["late_started"], [big[nm][1] for nm in early_names], "grad_chips_late_wait")
    g_w_in, = _grad_share([_add_chips(early["late_part"][0], got3, cq, "add_chips_w_in")], "grad_share_late")
    g_t = _transposed(g_w_in)
    d_t, m_t, v_t = _adamw(w_in_l, g_t, m_w_in_l, v_w_in_l, "adamw_w_in")
    big["w_in"] = tuple(jnp.swapaxes(t[None], 1, 2) for t in (g_t, d_t, m_t, v_t))
    small_all, small = _grad_chips_wait(small_started, [d_t], "small_gather_wait", with_small=True)
    small_names = ["norm1_w", "norm2_w", "final_norm_w", "a_log", "dt_bias", "gdn_norm_w", "conv_qkv_w", "ffn_conv_w"]
    loss_b, *small_out = _small_step(
        jnp.stack([2 * q + c, q]).astype(jnp.int32), small_all, small,
        [norm1_w, norm2_w, final_norm_w[None], a_log, dt_bias, gdn_norm_w, conv_qkv_w[0], ffn_conv_w[0]],
        [m_norm1_w, m_norm2_w, m_final_norm_w[None], m_a_log, m_dt_bias, m_gdn_norm_w, m_conv_qkv_w[0], m_ffn_conv_w[0]],
        [v_norm1_w, v_norm2_w, v_final_norm_w[None], v_a_log, v_dt_bias, v_gdn_norm_w, v_conv_qkv_w[0], v_ffn_conv_w[0]])
    like = dict(final_norm_w=lambda t: t[0], conv_qkv_w=lambda t: t[None], ffn_conv_w=lambda t: t[None])
    for k, nm in enumerate(small_names):
        big[nm] = tuple(like.get(nm, lambda t: t)(t) for t in small_out[4 * k:4 * k + 4])
    names = ["norm1_w", "w_in", "conv_qkv_w", "a_log", "dt_bias", "gdn_norm_w", "w_out", "norm2_w", "w_up",
             "ffn_conv_w", "w_down", "final_norm_w"]
    return (loss_b[0, 0], dx[None], *[big[n][0] for n in names], *[big[n][1] for n in names],
            *[big[n][2] for n in names], *[big[n][3] for n in names])
```

```python
import functools
import math

import numpy as np
import jax
import jax.numpy as jnp
from jax import lax
from jax.experimental import pallas as pl
from jax.experimental.pallas import tpu as pltpu

F32 = jnp.float32
BF16 = jnp.bfloat16
_MXU = jnp.bfloat16
_HI = lax.Precision.HIGHEST
EPS = 1e-6
V7X_VMEM_LIMIT = 56 * 1024 * 1024
MESH = pl.DeviceIdType.MESH

D_MODEL = 1024
GDN_HEADS, GDN_DIM, GDN_CHUNK, GDN_CONV = 4, 128, 64, 4
GDN_WIDTH = GDN_HEADS * GDN_DIM
DIL_HEADS, DIL_DIM = 8, 64
DIL_WIDTH = DIL_HEADS * DIL_DIM
D_FF, FFN_CONV = 2816, 3
IN_COLS = 3592
P_COLS = 3840
P_Z, P_QKVB, P_BA = 1536, 2048, 3584
ATT_T = 1024
ATT_GROUPS = 4
ADAM_LR, ADAM_B1, ADAM_B2, ADAM_EPS, ADAM_WD, ADAM_STEP = 0.001, 0.9, 0.999, 1e-08, 0.01, 10
N_CHIPS = 4


def _cparams(sem=None, vmem=None):
    kw = {}
    if sem is not None:
        kw["dimension_semantics"] = sem
    if vmem is not None:
        kw["vmem_limit_bytes"] = vmem
    return pltpu.CompilerParams(**kw)


def _silu(x):
    return x * jax.nn.sigmoid(x)


def _pick_tile(n, cap):
    best = None
    for t in range(128, min(n, cap) + 1, 128):
        if n % t == 0:
            best = t
    return best or n


def _mm(a, b, mode, *, out_dtype=F32, residual=None, name, b_blocks=False, place=None, into=None, tn=None, after=()):
    if mode == "nn":
        M, K = a.shape
        N = b.shape[0] * b.shape[2] if b_blocks else b.shape[1]
    elif mode == "nt":
        (M, K), (N, _) = a.shape, b.shape
    else:
        (K, M), (_, N) = a.shape, b.shape
    tm = _pick_tile(M, 1024)
    tn = b.shape[2] if b_blocks else (tn or _pick_tile(N, 1536))

    def vmem(tm, tn):
        return 2 * (tm * K * a.dtype.itemsize + tn * K * b.dtype.itemsize
                    + tm * tn * (jnp.dtype(out_dtype).itemsize + (4 if residual is not None else 0))) + 3 * tm * tn * 4

    fixed_tn = b_blocks or (place is not None and place[0] == "blocks")
    while vmem(tm, tn) > 40 * 1024 * 1024:
        if (tm >= tn or fixed_tn) and tm % 256 == 0:
            tm //= 2
        elif tn % 256 == 0 and not fixed_tn:
            tn //= 2
        else:
            tm //= 2
    a_spec = pl.BlockSpec((K, tm), lambda j, i: (0, i)) if mode == "tn" else pl.BlockSpec((tm, K), lambda j, i: (i, 0))
    if b_blocks:
        b_spec = pl.BlockSpec((None, K, tn), lambda j, i: (j, 0, 0))
    else:
        b_spec = pl.BlockSpec((tn, K), lambda j, i: (j, 0)) if mode == "nt" else pl.BlockSpec((K, tn), lambda j, i: (0, j))
    r_spec = pl.BlockSpec((tm, tn), lambda j, i: (i, j))
    if place is None:
        o_spec, o_shape = r_spec, (M, N)
    elif place[0] == "rows":
        off = place[2] // tm
        o_spec, o_shape = pl.BlockSpec((tm, tn), lambda j, i: (i + off, j)), (place[1], N)
    else:
        off = place[2]
        o_spec, o_shape = pl.BlockSpec((None, tm, tn), lambda j, i: (j + off, i, 0)), (place[1], M, tn)
    dims = {"nn": (((1,), (0,)), ((), ())), "nt": (((1,), (1,)), ((), ())), "tn": (((0,), (0,)), ((), ()))}[mode]

    def body(*refs):
        a_ref, b_ref = refs[0], refs[1]
        o_ref = refs[-1]
        acc = lax.dot_general(a_ref[...].astype(_MXU), b_ref[...].astype(_MXU), dims, preferred_element_type=F32)
        if residual is not None:
            acc = acc + refs[2][...]
        o_ref[...] = acc.astype(out_dtype)

    ins, specs, alias = [a, b], [a_spec, b_spec], {}
    if residual is not None:
        ins.append(residual)
        specs.append(r_spec)
    if into is not None:
        alias = {len(ins): 0}
        ins.append(into)
        specs.append(pl.BlockSpec(memory_space=pl.ANY))
    ins += list(after)
    specs += [pl.BlockSpec(memory_space=pl.ANY)] * len(after)
    return pl.pallas_call(
        body, name=name, grid=(N // tn, M // tm), in_specs=specs, out_specs=o_spec,
        out_shape=jax.ShapeDtypeStruct(o_shape, out_dtype), input_output_aliases=alias,
        compiler_params=_cparams(("parallel", "parallel"), V7X_VMEM_LIMIT),
    )(*ins)


def _mm_nt_blocks(a_list, b4, name, after=()):
    M = a_list[0].shape[0]
    nb, N, Kb = b4.shape
    tm, tn = _pick_tile(M, 1024), _pick_tile(N, 512)

    def body(a0_ref, a1_ref, b_ref, *rest):
        o_ref = rest[-1]
        acc = None
        for blk in range(nb):
            a_ref = (a0_ref, a1_ref)[blk // 2]
            lo = (blk % 2) * Kb
            t = lax.dot_general(a_ref[:, lo:lo + Kb].astype(_MXU), b_ref[blk].astype(_MXU), (((1,), (1,)), ((), ())),
                                preferred_element_type=F32)
            acc = t if acc is None else acc + t
        o_ref[...] = acc

    a_spec = pl.BlockSpec((tm, 2 * Kb), lambda j, i: (i, 0))
    return pl.pallas_call(
        body, name=name, grid=(N // tn, M // tm),
        in_specs=[a_spec, a_spec, pl.BlockSpec((nb, tn, Kb), lambda j, i: (0, j, 0))]
        + [pl.BlockSpec(memory_space=pl.ANY)] * len(after),
        out_specs=pl.BlockSpec((tm, tn), lambda j, i: (i, j)), out_shape=jax.ShapeDtypeStruct((M, N), F32),
        compiler_params=_cparams(("parallel", "parallel"), V7X_VMEM_LIMIT),
    )(a_list[0], a_list[1], b4, *after)


def _wp_assemble(g_in, after=()):
    nb, Dm, Wb = g_in.shape
    T = 256
    n_lo = P_QKVB - 2 * Wb

    def body(g_ref, *rest):
        g2 = g_ref[2]
        rest[-1][...] = jnp.concatenate(
            [g_ref[0], g_ref[1], g2[:, :n_lo], g2[:, n_lo + 8:], g_ref[3], g2[:, n_lo:n_lo + 8],
             jnp.zeros((T, P_COLS - P_BA - 8), g_in.dtype)], axis=1)

    return pl.pallas_call(
        body, name="wp_assemble", grid=(Dm // T,),
        in_specs=[pl.BlockSpec((nb, T, Wb), lambda i: (0, i, 0))] + [pl.BlockSpec(memory_space=pl.ANY)] * len(after),
        out_specs=pl.BlockSpec((T, P_COLS), lambda i: (i, 0)), out_shape=jax.ShapeDtypeStruct((Dm, P_COLS), g_in.dtype),
        compiler_params=_cparams(("parallel",)),
    )(g_in, *after)


def _rmsnorm_fwd(x, w, name, after=()):
    S, D = x.shape
    T = _pick_tile(S, 512)

    def body(x_ref, w_ref, *rest):
        xv = x_ref[...]
        rs = lax.rsqrt(jnp.mean(xv * xv, axis=-1, keepdims=True) + EPS)
        rest[-1][...] = (xv * rs * w_ref[...]).astype(rest[-1].dtype)

    return pl.pallas_call(
        body, name=name, grid=(S // T,),
        in_specs=[pl.BlockSpec((T, D), lambda i: (i, 0)), pl.BlockSpec((1, D), lambda i: (0, 0))] + [_ANY] * len(after),
        out_specs=pl.BlockSpec((T, D), lambda i: (i, 0)),
        out_shape=jax.ShapeDtypeStruct((S, D), _MXU),
        compiler_params=_cparams(("parallel",)),
    )(x, w, *after)


def _rmsnorm_bwd(dh, x, w, dres, name, after=()):
    S, D = x.shape
    T = _pick_tile(S, 512)

    def body(dh_ref, x_ref, w_ref, dres_ref, *rest):
        dx_ref, dw_ref = rest[-2:]
        xv = x_ref[...]
        rs = lax.rsqrt(jnp.mean(xv * xv, axis=-1, keepdims=True) + EPS)
        xn = xv * rs
        dhv = dh_ref[...]
        dxn = dhv * w_ref[...]
        dx_ref[...] = dres_ref[...] + rs * (dxn - xn * jnp.mean(dxn * xn, axis=-1, keepdims=True))

        @pl.when(pl.program_id(0) == 0)
        def _():
            dw_ref[...] = jnp.zeros_like(dw_ref)

        dw_ref[...] += jnp.sum(dhv * xn, axis=0, keepdims=True)

    row = pl.BlockSpec((T, D), lambda i: (i, 0))
    vec = pl.BlockSpec((1, D), lambda i: (0, 0))
    return pl.pallas_call(
        body, name=name, grid=(S // T,), in_specs=[row, row, vec, row] + [_ANY] * len(after), out_specs=(row, vec),
        out_shape=(jax.ShapeDtypeStruct((S, D), F32), jax.ShapeDtypeStruct((1, D), F32)),
        compiler_params=_cparams(("arbitrary",)),
    )(dh, x, w, dres, *after)


def _loss_head(x3, w, tgt, name):
    S, D = x3.shape
    T = _pick_tile(S, 512)

    def body(x_ref, w_ref, t_ref, loss_ref, dx_ref, dxn_ref, dw_ref):
        xv = x_ref[...]
        rs = lax.rsqrt(jnp.mean(xv * xv, axis=-1, keepdims=True) + EPS)
        xn = xv * rs
        err = xn * w_ref[...] - t_ref[...]
        dy = err * (1.0 / D)
        dxn = dy * w_ref[...]
        dxv = rs * (dxn - xn * jnp.mean(dxn * xn, axis=-1, keepdims=True))
        dx_ref[...] = dxv
        dxn_ref[...] = dxv.astype(dxn_ref.dtype)

        @pl.when(pl.program_id(0) == 0)
        def _():
            dw_ref[...] = jnp.zeros_like(dw_ref)
            loss_ref[...] = jnp.zeros_like(loss_ref)

        dw_ref[...] += jnp.sum(dy * xn, axis=0, keepdims=True)
        part = jnp.sum(jnp.sum(err * err, axis=-1, keepdims=True), axis=0, keepdims=True) * (0.5 / D)
        loss_ref[...] += jnp.broadcast_to(part, loss_ref.shape)

    row = pl.BlockSpec((T, D), lambda i: (i, 0))
    vec = pl.BlockSpec((1, D), lambda i: (0, 0))
    return pl.pallas_call(
        body, name=name, grid=(S // T,), in_specs=[row, vec, row],
        out_specs=(pl.BlockSpec((8, 128), lambda i: (0, 0)), row, row, vec),
        out_shape=(jax.ShapeDtypeStruct((8, 128), F32), jax.ShapeDtypeStruct((S, D), F32), jax.ShapeDtypeStruct((S, D), _MXU),
                   jax.ShapeDtypeStruct((1, D), F32)),
        compiler_params=_cparams(("arbitrary",)),
    )(x3, w, tgt)


def _shifted(ext, back, lo, n):
    if back == 0:
        return ext[lo:lo + n, :]
    return pltpu.roll(ext, back % ext.shape[0], 0)[lo:lo + n, :]


def _conv_windows(ext, K, T):
    return [_shifted(ext, (K - 1) - i, 8, T) for i in range(K)]


def _conv_taps(ext, w, K, T):
    out = None
    for i, win in enumerate(_conv_windows(ext, K, T)):
        term = win * w[i:i + 1, :]
        out = term if out is None else out + term
    return out


def _conv_taps_t(ext, w, K, T):
    out = None
    for i in range(K):
        term = _shifted(ext, i - (K - 1), 0, T) * w[i:i + 1, :]
        out = term if out is None else out + term
    return out


def _tri_masks(C):
    r = lax.broadcasted_iota(jnp.int32, (C, C), 0)
    c = lax.broadcasted_iota(jnp.int32, (C, C), 1)
    return r == c, r >= c, r > c, r <= c


_NN, _NT, _TN = ((1,), (0,)), ((1,), (1,)), ((0,), (0,))
_GDN_PASSES = dict(qk=1, inv=1, sol=1, scan=1, bwd=1)


def _bdot_raw(a, b, kind, passes):
    dims = ({"NN": ((2,), (1,)), "NT": ((2,), (2,)), "TN": ((1,), (1,))}[kind], ((0,), (0,)))
    if passes == 0:
        return lax.dot_general(a, b, dims, precision=_HI, preferred_element_type=F32)
    ah, bh = a.astype(BF16), b.astype(BF16)
    out = lax.dot_general(ah, bh, dims, preferred_element_type=F32)
    if passes == 3:
        al, bl = (a - ah.astype(F32)).astype(BF16), (b - bh.astype(F32)).astype(BF16)
        out = out + lax.dot_general(ah, bl, dims, preferred_element_type=F32) + lax.dot_general(al, bh, dims, preferred_element_type=F32)
    return out


@functools.partial(jax.custom_vjp, nondiff_argnums=(2, 3))
def _bdot(a, b, kind, passes):
    return _bdot_raw(a, b, kind, passes)


def _bdot_fwd(a, b, kind, passes):
    return _bdot_raw(a, b, kind, passes), (a, b)


def _bdot_bwd(kind, passes, res, ct):
    a, b = res
    if kind == "NN":
        return _bdot_raw(ct, b, "NT", passes), _bdot_raw(a, ct, "TN", passes)
    if kind == "NT":
        return _bdot_raw(ct, b, "NN", passes), _bdot_raw(ct, a, "TN", passes)
    return _bdot_raw(b, ct, "NT", passes), _bdot_raw(a, ct, "NN", passes)


_bdot.defvjp(_bdot_fwd, _bdot_bwd)


def _softplus(x):
    return jnp.maximum(x, 0.0) + jnp.log(1.0 + jnp.exp(-jnp.abs(x)))


def _gdn_stage1(cq, ck, cv, b_col, a_col, alog, dtb, dot=_bdot_raw):
    C = cq.shape[1]
    eye, incl, strict, incl_t = _tri_masks(C)
    qn = cq * lax.rsqrt(jnp.sum(cq * cq, axis=-1, keepdims=True) + EPS) * (GDN_DIM ** -0.5)
    kn = ck * lax.rsqrt(jnp.sum(ck * ck, axis=-1, keepdims=True) + EPS)
    beta = jax.nn.sigmoid(b_col)
    g = -jnp.exp(alog) * _softplus(a_col + dtb)
    g_row = jnp.sum(jnp.where(eye, g, 0.0), axis=1, keepdims=True)
    beta_row = jnp.sum(jnp.where(eye, beta, 0.0), axis=1, keepdims=True)
    gc_col = jnp.sum(jnp.where(incl, g_row, 0.0), axis=2, keepdims=True)
    gc_row = jnp.sum(jnp.where(incl_t, g, 0.0), axis=1, keepdims=True)
    dec = jnp.where(incl, jnp.exp(jnp.where(incl, gc_col - gc_row, 0.0)), 0.0)
    kk = dot(kn, kn, "NT", _GDN_PASSES["qk"])
    qk = dot(qn, kn, "NT", _GDN_PASSES["qk"])
    lmat = jnp.where(strict, dec * kk * beta_row, 0.0)
    attn = dec * qk * beta_row
    gam = jnp.exp(gc_col)
    gc_last = gc_col[:, C - 1:C, :]
    k_end = kn * (jnp.exp(gc_last - gc_col) * beta)
    return lmat, cv, gam * kn, gam * qn, attn, k_end, jnp.exp(gc_last)


def _tri_inv(lmat):
    C = lmat.shape[1]
    eye = _tri_masks(C)[0]
    ps = _GDN_PASSES["inv"]
    p = jnp.where(eye, 1.0, 0.0) - lmat
    lp = _bdot_raw(lmat, lmat, "NN", ps)
    n = int(math.log2(C))
    for s in range(1, n):
        p = p + _bdot_raw(p, lp, "NN", ps)
        if s < n - 1:
            lp = _bdot_raw(lp, lp, "NN", ps)
    return p


def _gated_norm(o, z, gnw):
    on = o * lax.rsqrt(jnp.mean(o * o, axis=-1, keepdims=True) + EPS) * gnw
    return on * _silu(z)


GDN_PG = 4
GDN_SG = 4


def _gdn_pairs(c, ba, gp, G):
    C, W, H = GDN_CHUNK, GDN_WIDTH, GDN_HEADS
    pairs = [(j, h) for j in range(G) for h in range(H)]
    cq, ck, cv = (jnp.stack([c[C * j:C * (j + 1), o + GDN_DIM * h:o + GDN_DIM * (h + 1)] for j, h in pairs]) for o in (0, W, 2 * W))
    b_col = jnp.stack([ba[C * j:C * (j + 1), h:h + 1] for j, h in pairs])
    a_col = jnp.stack([ba[C * j:C * (j + 1), H + h:H + h + 1] for j, h in pairs])
    alog = jnp.stack([gp[0:1, h:h + 1] for j, h in pairs])
    dtb = jnp.stack([gp[0:1, H + h:H + h + 1] for j, h in pairs])
    return pairs, (cq, ck, cv, b_col, a_col, alog, dtb)


def _gdn_pre_specs(S, G):
    C = GDN_CHUNK
    T = C * G
    return dict(
        cur=pl.BlockSpec((T, 3 * GDN_WIDTH), lambda i: (i, 0)),
        prev=pl.BlockSpec((8, 3 * GDN_WIDTH), lambda i: (jnp.maximum(i * (T // 8) - 1, 0), 0)),
        ba=pl.BlockSpec((T, 128), lambda i: (i, P_BA // 128)),
        cw=pl.BlockSpec((GDN_CONV, 3 * GDN_WIDTH), lambda i: (0, 0)),
        vec=pl.BlockSpec((1, 128), lambda i: (0, 0)),
        hd=pl.BlockSpec((GDN_HEADS, T, GDN_DIM), lambda i: (0, i, 0)),
        hc=pl.BlockSpec((GDN_HEADS, T, C), lambda i: (0, i, 0)),
        ge=pl.BlockSpec((G, GDN_HEADS, 8, 128), lambda i: (i, 0, 0, 0)),
    )


def _hd_shape(S, last=GDN_DIM):
    return jax.ShapeDtypeStruct((GDN_HEADS, S, last), F32)


def _gdn_pre(proj, conv_w, gp):
    S = proj.shape[0]
    C, G = GDN_CHUNK, GDN_PG
    nc = S // C
    sp = _gdn_pre_specs(S, G)

    def body(cur_ref, prev_ref, ba_ref, cw_ref, gp_ref, uv_ref, wk_ref, qd_ref, ke_ref, at_ref, ti_ref, ge_ref):
        prev = prev_ref[...] * jnp.where(pl.program_id(0) == 0, 0.0, 1.0)
        c = _silu(_conv_taps(jnp.concatenate([prev, cur_ref[...]], axis=0), cw_ref[...], GDN_CONV, C * G))
        pairs, args = _gdn_pairs(c, ba_ref[...], gp_ref[...], G)
        lmat, v, rk, q_dec, attn, k_end, g_end = _gdn_stage1(*args)
        t = _tri_inv(lmat)
        u_v = _bdot_raw(t, v, "NN", _GDN_PASSES["sol"])
        w_k = _bdot_raw(t, rk, "NN", _GDN_PASSES["sol"])
        for b, (j, h) in enumerate(pairs):
            rows = slice(C * j, C * (j + 1))
            uv_ref[h, rows, :] = u_v[b]
            wk_ref[h, rows, :] = w_k[b]
            qd_ref[h, rows, :] = q_dec[b]
            ke_ref[h, rows, :] = k_end[b]
            at_ref[h, rows, :] = attn[b]
            ti_ref[h, rows, :] = t[b]
            ge_ref[j, h] = jnp.broadcast_to(g_end[b], (8, 128))

    return pl.pallas_call(
        body, name="gdn_pre", grid=(nc // G,),
        in_specs=[sp["cur"], sp["prev"], sp["ba"], sp["cw"], sp["vec"]],
        out_specs=(sp["hd"], sp["hd"], sp["hd"], sp["hd"], sp["hc"], sp["hc"], sp["ge"]),
        out_shape=(_hd_shape(S), _hd_shape(S), _hd_shape(S), _hd_shape(S), _hd_shape(S, C), _hd_shape(S, C),
                   jax.ShapeDtypeStruct((nc, GDN_HEADS, 8, 128), F32)),
        compiler_params=_cparams(("parallel",)),
    )(proj, proj, proj, conv_w, gp)


def _gdn_scan_specs(S, G, rev):
    C = GDN_CHUNK
    T = C * G
    n = S // T
    ci = (lambda i: n - 1 - i) if rev else (lambda i: i)
    return dict(
        hd=pl.BlockSpec((GDN_HEADS, T, GDN_DIM), lambda i: (0, ci(i), 0)),
        hc=pl.BlockSpec((GDN_HEADS, T, C), lambda i: (0, ci(i), 0)),
        ge=pl.BlockSpec((G, GDN_HEADS, 8, 128), lambda i: (ci(i), 0, 0, 0)),
        z=pl.BlockSpec((T, GDN_WIDTH), lambda i: (ci(i), P_Z // GDN_WIDTH)),
        oa=pl.BlockSpec((T, GDN_WIDTH), lambda i: (ci(i), 0)),
        vec=pl.BlockSpec((1, 128), lambda i: (0, 0)),
        st=pl.BlockSpec((G, GDN_HEADS, GDN_DIM, GDN_DIM), lambda i: (ci(i), 0, 0, 0)),
    )


def _gdn_scan(u_v, w_k, q_dec, k_end, attn, g_end, proj, gnw, mix, after=()):
    S = proj.shape[0]
    C, G = GDN_CHUNK, GDN_SG
    nc = S // C
    sp = _gdn_scan_specs(S, G, False)
    ps = _GDN_PASSES["scan"]

    def body(uv_ref, wk_ref, qd_ref, ke_ref, at_ref, ge_ref, z_ref, gnw_ref, *rest):
        oa_ref, st_ref, s_scr = rest[-3:]

        @pl.when(pl.program_id(0) == 0)
        def _():
            s_scr[...] = jnp.zeros_like(s_scr)

        for j in range(G):
            rows = slice(C * j, C * (j + 1))
            st = s_scr[...]
            st_ref[j] = st
            u = uv_ref[:, rows, :] - _bdot_raw(wk_ref[:, rows, :], st, "NN", ps)
            o = _bdot_raw(qd_ref[:, rows, :], st, "NN", ps) + _bdot_raw(at_ref[:, rows, :], u, "NN", ps)
            s_scr[...] = ge_ref[j][:, 0:1, 0:1] * st + _bdot_raw(ke_ref[:, rows, :], u, "TN", ps)
            for h in range(GDN_HEADS):
                cols = slice(GDN_DIM * h, GDN_DIM * (h + 1))
                oa_ref[rows, cols] = _gated_norm(o[h], z_ref[rows, cols], gnw_ref[...])

    return pl.pallas_call(
        body, name="gdn_scan", grid=(nc // G,),
        in_specs=[sp["hd"], sp["hd"], sp["hd"], sp["hd"], sp["hc"], sp["ge"], sp["z"], sp["vec"]] + [_ANY] * (1 + len(after)),
        out_specs=(sp["oa"], sp["st"]),
        out_shape=(jax.ShapeDtypeStruct(mix.shape, F32),
                   jax.ShapeDtypeStruct((nc, GDN_HEADS, GDN_DIM, GDN_DIM), F32)),
        input_output_aliases={8: 0},
        scratch_shapes=[pltpu.VMEM((GDN_HEADS, GDN_DIM, GDN_DIM), F32)],
        compiler_params=_cparams(("arbitrary",)),
    )(u_v, w_k, q_dec, k_end, attn, g_end, proj, gnw, mix, *after)


def _gdn_scan_bwd(u_v, w_k, q_dec, k_end, attn, g_end, proj, gnw, states, d_oa):
    S = proj.shape[0]
    C, G = GDN_CHUNK, GDN_SG
    nc = S // C
    sp = _gdn_scan_specs(S, G, True)
    ps, pb = _GDN_PASSES["scan"], _GDN_PASSES["bwd"]

    def body(uv_ref, wk_ref, qd_ref, ke_ref, at_ref, ge_ref, z_ref, gnw_ref, st_ref, doa_ref,
             duv_ref, dwk_ref, dqd_ref, dke_ref, dat_ref, dge_ref, dz_ref, dgnw_ref, ds_scr):
        @pl.when(pl.program_id(0) == 0)
        def _():
            ds_scr[...] = jnp.zeros_like(ds_scr)
            dgnw_ref[...] = jnp.zeros_like(dgnw_ref)

        dgnw = jnp.zeros((1, 128), F32)
        for j in reversed(range(G)):
            rows = slice(C * j, C * (j + 1))
            st = st_ref[j]
            wk, qd, ke, at = wk_ref[:, rows, :], qd_ref[:, rows, :], ke_ref[:, rows, :], at_ref[:, rows, :]
            u = uv_ref[:, rows, :] - _bdot_raw(wk, st, "NN", ps)
            o = _bdot_raw(qd, st, "NN", ps) + _bdot_raw(at, u, "NN", ps)
            dos = []
            for h in range(GDN_HEADS):
                cols = slice(GDN_DIM * h, GDN_DIM * (h + 1))
                _, vjp2 = jax.vjp(_gated_norm, o[h], z_ref[rows, cols], gnw_ref[...])
                do_h, dz_h, dgn = vjp2(doa_ref[rows, cols])
                dz_ref[rows, cols] = dz_h
                dgnw = dgnw + dgn
                dos.append(do_h)
            do = jnp.stack(dos)
            ds_new = ds_scr[...]
            du = _bdot_raw(at, do, "TN", pb) + _bdot_raw(ke, ds_new, "NN", pb)
            duv_ref[:, rows, :] = du
            dat_ref[:, rows, :] = _bdot_raw(do, u, "NT", pb)
            dqd_ref[:, rows, :] = _bdot_raw(do, st, "NT", pb)
            dke_ref[:, rows, :] = _bdot_raw(u, ds_new, "NT", pb)
            dwk_ref[:, rows, :] = -_bdot_raw(du, st, "NT", pb)
            d_ge = jnp.sum(jnp.sum(st * ds_new, axis=2, keepdims=True), axis=1, keepdims=True)
            dge_ref[j] = jnp.broadcast_to(d_ge, (GDN_HEADS, 8, 128))
            ds_scr[...] = ge_ref[j][:, 0:1, 0:1] * ds_new + _bdot_raw(qd, do, "TN", pb) - _bdot_raw(wk, du, "TN", pb)
        dgnw_ref[...] += dgnw

    return pl.pallas_call(
        body, name="gdn_scan_bwd", grid=(nc // G,),
        in_specs=[sp["hd"], sp["hd"], sp["hd"], sp["hd"], sp["hc"], sp["ge"], sp["z"], sp["vec"], sp["st"], sp["oa"]],
        out_specs=(sp["hd"], sp["hd"], sp["hd"], sp["hd"], sp["hc"], sp["ge"], sp["oa"], sp["vec"]),
        out_shape=(_hd_shape(S), _hd_shape(S), _hd_shape(S), _hd_shape(S), _hd_shape(S, C),
                   jax.ShapeDtypeStruct((nc, GDN_HEADS, 8, 128), F32), jax.ShapeDtypeStruct((S, GDN_WIDTH), F32),
                   jax.ShapeDtypeStruct((1, 128), F32)),
        scratch_shapes=[pltpu.VMEM((GDN_HEADS, GDN_DIM, GDN_DIM), F32)],
        compiler_params=_cparams(("arbitrary",)),
    )(u_v, w_k, q_dec, k_end, attn, g_end, proj, gnw, states, d_oa)


def _gdn_post(proj, conv_w, gp, tinv, u_v, w_k, d_uv, d_wk, d_qd, d_ke, d_at, d_ge):
    S = proj.shape[0]
    C, G = GDN_CHUNK, GDN_PG
    nc = S // C
    sp = _gdn_pre_specs(S, G)
    pb = _GDN_PASSES["bwd"]

    def body(cur_ref, prev_ref, ba_ref, cw_ref, gp_ref, ti_ref, uv_ref, wk_ref, duv_ref, dwk_ref, dqd_ref, dke_ref,
             dat_ref, dge_ref, dpre_ref, dba_ref, dgp_ref):
        i = pl.program_id(0)

        @pl.when(i == 0)
        def _():
            dgp_ref[...] = jnp.zeros_like(dgp_ref)

        prev = prev_ref[...] * jnp.where(i == 0, 0.0, 1.0)
        pre = _conv_taps(jnp.concatenate([prev, cur_ref[...]], axis=0), cw_ref[...], GDN_CONV, C * G)
        sg = jax.nn.sigmoid(pre)
        dsilu = sg * (1.0 + pre * (1.0 - sg))
        pairs, args = _gdn_pairs(pre * sg, ba_ref[...], gp_ref[...], G)
        _, vjp1 = jax.vjp(functools.partial(_gdn_stage1, dot=_bdot), *args)

        def take(ref):
            return jnp.stack([ref[h, C * j:C * (j + 1), :] for j, h in pairs])

        t, u_v, w_k = take(ti_ref), take(uv_ref), take(wk_ref)
        d_v = _bdot_raw(t, take(duv_ref), "TN", pb)
        d_rk = _bdot_raw(t, take(dwk_ref), "TN", pb)
        d_l = -(_bdot_raw(d_v, u_v, "NT", pb) + _bdot_raw(d_rk, w_k, "NT", pb))
        d_ge = jnp.stack([dge_ref[j, h][0:1, 0:1] for j, h in pairs])
        dcq, dck, dcv, db, da, dalog, ddtb = vjp1((d_l, d_v, d_rk, take(dqd_ref), take(dat_ref), take(dke_ref), d_ge))
        lane = lax.broadcasted_iota(jnp.int32, (C, 128), 1)
        lane1 = lax.broadcasted_iota(jnp.int32, (1, 128), 1)
        dgp = jnp.zeros((1, 128), F32)
        for j in range(G):
            rows = slice(C * j, C * (j + 1))
            dba = jnp.zeros((C, 128), F32)
            for h in range(GDN_HEADS):
                b = GDN_HEADS * j + h
                for o_, dcx in ((0, dcq), (GDN_WIDTH, dck), (2 * GDN_WIDTH, dcv)):
                    cols = slice(o_ + GDN_DIM * h, o_ + GDN_DIM * (h + 1))
                    dpre_ref[rows, cols] = dcx[b] * dsilu[rows, cols]
                dba = dba + jnp.where(lane == h, db[b], 0.0) + jnp.where(lane == GDN_HEADS + h, da[b], 0.0)
                dgp = dgp + jnp.where(lane1 == h, dalog[b], 0.0) + jnp.where(lane1 == GDN_HEADS + h, ddtb[b], 0.0)
            dba_ref[rows, :] = dba
        dgp_ref[0:1, :] += dgp

    T = C * G
    return pl.pallas_call(
        body, name="gdn_post", grid=(nc // G,),
        in_specs=[sp["cur"], sp["prev"], sp["ba"], sp["cw"], sp["vec"], sp["hc"], sp["hd"], sp["hd"], sp["hd"], sp["hd"],
                  sp["hd"], sp["hd"], sp["hc"], sp["ge"]],
        out_specs=(sp["cur"], pl.BlockSpec((T, 128), lambda i: (i, 0)), pl.BlockSpec((8, 128), lambda i: (0, 0))),
        out_shape=(jax.ShapeDtypeStruct((S, 3 * GDN_WIDTH), F32), jax.ShapeDtypeStruct((S, 128), F32),
                   jax.ShapeDtypeStruct((8, 128), F32)),
        compiler_params=_cparams(("arbitrary",)),
    )(proj, proj, proj, conv_w, gp, tinv, u_v, w_k, d_uv, d_wk, d_qd, d_ke, d_at, d_ge)


def _conv_bwd(dpre, x, xcol0, w, K, name, tc):
    S, Cc = dpre.shape
    T = _pick_tile(S, 256)
    nt, ncol = S // T, Cc // tc
    xo = xcol0 // tc

    def body(d_ref, dn_ref, x_ref, xp_ref, w_ref, dx_ref, dw_ref):
        i = pl.program_id(1)
        dn = dn_ref[...] * jnp.where(i == nt - 1, 0.0, 1.0)
        dv = d_ref[...]
        ext_d = jnp.concatenate([dv, dn], axis=0)
        dx_ref[...] = _conv_taps_t(ext_d, w_ref[...], K, T).astype(dx_ref.dtype)
        xp = xp_ref[...] * jnp.where(i == 0, 0.0, 1.0)
        ext_x = jnp.concatenate([xp, x_ref[...]], axis=0)

        @pl.when(i == 0)
        def _():
            dw_ref[...] = jnp.zeros_like(dw_ref)

        for k in range(K):
            dw_ref[k:k + 1, :] += jnp.sum(dv * _shifted(ext_x, (K - 1) - k, 8, T), axis=0, keepdims=True)

    r8 = T // 8
    return pl.pallas_call(
        body, name=name, grid=(ncol, nt),
        in_specs=[pl.BlockSpec((T, tc), lambda j, i: (i, j)),
                  pl.BlockSpec((8, tc), lambda j, i: (jnp.minimum((i + 1) * r8, S // 8 - 1), j)),
                  pl.BlockSpec((T, tc), lambda j, i: (i, j + xo)),
                  pl.BlockSpec((8, tc), lambda j, i: (jnp.maximum(i * r8 - 1, 0), j + xo)),
                  pl.BlockSpec((K, tc), lambda j, i: (0, j))],
        out_specs=(pl.BlockSpec((T, tc), lambda j, i: (i, j)), pl.BlockSpec((K, tc), lambda j, i: (0, j))),
        out_shape=(jax.ShapeDtypeStruct((S, Cc), _MXU), jax.ShapeDtypeStruct((K, Cc), F32)),
        compiler_params=_cparams(("parallel", "arbitrary")),
    )(dpre, dpre, x, x, w)


def _dil_bias(nt, T):
    d = (np.arange(nt)[:, None, None] * T + np.arange(T)[None, None, :] - np.arange(T)[None, :, None])
    cnt = ((d >= 0) & (d <= 128)).astype(np.float64) + ((d >= 0) & (d % 4 == 0) & (d <= 512)) + ((d >= 0) & (d % 16 == 0))
    return jnp.asarray(np.where(cnt > 0, np.log(np.maximum(cnt, 1.0)), -1e30), dtype=F32)


def _attn_fwd(proj, after=()):
    S = proj.shape[0]
    T = min(ATT_T, S)
    nt, H = S // T, T // ATT_GROUPS
    bias = _dil_bias(nt, T)
    scale = DIL_DIM ** -0.5
    npair = DIL_WIDTH // 128
    qb0, kb0, vb0 = P_QKVB // 128, (P_QKVB + DIL_WIDTH) // 128, (P_QKVB + 2 * DIL_WIDTH) // 128

    def body(q_ref, k_ref, v_ref, b_ref, *rest):
        o_ref, lse_ref = rest[-2:]
        i = pl.program_id(1)
        qs = (q_ref[...] * scale).astype(_MXU)

        def update(carry, kt, vt, qt, bt):
            out = []
            for hh in range(2):
                m, l, acc = carry[hh]
                sl = slice(hh * DIL_DIM, (hh + 1) * DIL_DIM)
                s = lax.dot_general(kt[:, sl], qt[:, sl], (_NT, ((), ())), preferred_element_type=F32) + bt
                m_new = jnp.maximum(m, jnp.max(s, axis=0, keepdims=True))
                p = jnp.exp(s - m_new)
                a = jnp.exp(m - m_new)
                l = a * l + jnp.sum(p, axis=0, keepdims=True)
                acc = a * acc + lax.dot_general(vt[:, sl], p.astype(_MXU), (_TN, ((), ())), preferred_element_type=F32)
                out.append((m_new, l, acc))
            return tuple(out)

        def keys(j):
            rows = pl.ds(pl.multiple_of(j * T, T), T)
            return k_ref[rows, :].astype(_MXU), v_ref[rows, :].astype(_MXU)

        init = tuple((jnp.full((1, T), -1e30, F32), jnp.zeros((1, T), F32), jnp.zeros((DIL_DIM, T), F32)) for _ in range(2))
        res = lax.fori_loop(0, i, lambda j, carry: update(carry, *keys(j), qs, b_ref[i - j]), init)
        kd, vd = keys(i)
        for lo in range(0, T, H):
            part = update(tuple(tuple(t[:, lo:] for t in r) for r in res), kd[lo:lo + H], vd[lo:lo + H], qs[lo:],
                          b_ref[0, lo:lo + H, lo:])
            res = tuple(tuple(jnp.concatenate([t[:, :lo], u], axis=1) if lo else u for t, u in zip(r, r2))
                        for r, r2 in zip(res, part))
        lse_ref[...] = jnp.zeros_like(lse_ref)
        for hh in range(2):
            m, l, acc = res[hh]
            o_ref[:, hh * DIL_DIM:(hh + 1) * DIL_DIM] = (acc / l).T
            lse_ref[hh:hh + 1, :] = m + jnp.log(l)

    return pl.pallas_call(
        body, name="attn_fwd", grid=(npair, nt),
        in_specs=[pl.BlockSpec((T, 128), lambda p, i: (i, qb0 + p)),
                  pl.BlockSpec((S, 128), lambda p, i: (0, kb0 + p)),
                  pl.BlockSpec((S, 128), lambda p, i: (0, vb0 + p)),
                  pl.BlockSpec((nt, T, T), lambda p, i: (0, 0, 0))] + [_ANY] * len(after),
        out_specs=(pl.BlockSpec((T, 128), lambda p, i: (i, GDN_WIDTH // 128 + p)),
                   pl.BlockSpec((None, None, 8, T), lambda p, i: (p, i, 0, 0))),
        out_shape=(jax.ShapeDtypeStruct((S, GDN_WIDTH + DIL_WIDTH), F32), jax.ShapeDtypeStruct((npair, nt, 8, T), F32)),
        compiler_params=_cparams(("parallel", "parallel")),
    )(proj, proj, proj, bias, *after)


def _attn_bwd(proj, mix, lse, d_mix):
    S = proj.shape[0]
    T = min(ATT_T, S)
    nt, H = S // T, T // ATT_GROUPS
    bias = _dil_bias(nt, T)
    scale = DIL_DIM ** -0.5
    npair = DIL_WIDTH // 128
    qb0, kb0, vb0 = P_QKVB // 128, (P_QKVB + DIL_WIDTH) // 128, (P_QKVB + 2 * DIL_WIDTH) // 128

    def body(q_ref, k_ref, v_ref, o_ref, lse_ref, do_ref, b_ref, dq_ref, dk_ref, dv_ref, dq_scr):
        j = pl.program_id(1)

        @pl.when(j == 0)
        def _():
            dq_scr[...] = jnp.zeros_like(dq_scr)

        kt = k_ref[...].astype(_MXU)
        vt = v_ref[...].astype(_MXU)
        ones = jnp.ones((8, DIL_DIM), F32)

        def block(carry, kt, vt, rows, lsev, bt):
            qs = (q_ref[rows, :] * scale).astype(_MXU)
            dov = do_ref[rows, :]
            prod = dov * o_ref[rows, :]
            dob = dov.astype(_MXU)
            out = []
            dqs = []
            for hh in range(2):
                dk, dv = carry[hh]
                sl = slice(hh * DIL_DIM, (hh + 1) * DIL_DIM)
                s = lax.dot_general(kt[:, sl], qs[:, sl], (_NT, ((), ())), preferred_element_type=F32) + bt
                p = jnp.exp(s - lsev[hh:hh + 1, :])
                delta = lax.dot_general(ones, prod[:, sl], (_NT, ((), ())), precision=_HI, preferred_element_type=F32)[0:1, :]
                dp = lax.dot_general(vt[:, sl], dob[:, sl], (_NT, ((), ())), preferred_element_type=F32)
                ds = (p * (dp - delta)).astype(_MXU)
                dv = dv + lax.dot_general(p.astype(_MXU), dob[:, sl], (_NN, ((), ())), preferred_element_type=F32)
                dk = dk + lax.dot_general(ds, qs[:, sl], (_NN, ((), ())), preferred_element_type=F32)
                dqs.append(lax.dot_general(ds, kt[:, sl], (_TN, ((), ())), preferred_element_type=F32) * scale)
                out.append((dk, dv))
            dq_scr[rows, :] += jnp.concatenate(dqs, axis=1)
            return tuple(out)

        def step(i, carry):
            return block(carry, kt, vt, pl.ds(pl.multiple_of(i * T, T), T), lse_ref[i], b_ref[i - j])

        zeros = tuple((jnp.zeros((H, DIL_DIM), F32), jnp.zeros((H, DIL_DIM), F32)) for _ in range(2))
        lsed = lse_ref[j]
        parts = [block(zeros, kt[lo:lo + H], vt[lo:lo + H], pl.ds(pl.multiple_of(j * T + lo, H), T - lo), lsed[:, lo:],
                       b_ref[0, lo:lo + H, lo:]) for lo in range(0, T, H)]
        init = tuple(tuple(jnp.concatenate([part[hh][w] for part in parts], axis=0) for w in range(2)) for hh in range(2))
        res = lax.fori_loop(j + 1, nt, step, init)
        dk_ref[...] = jnp.concatenate([res[0][0], res[1][0]], axis=1).astype(dk_ref.dtype)
        dv_ref[...] = jnp.concatenate([res[0][1], res[1][1]], axis=1).astype(dv_ref.dtype)

        @pl.when(j == nt - 1)
        def _():
            dq_ref[...] = dq_scr[...].astype(dq_ref.dtype)

    full = lambda c0: pl.BlockSpec((S, 128), lambda p, j: (0, c0 + p))
    tile = lambda c0: pl.BlockSpec((T, 128), lambda p, j: (j, c0 + p))
    out3 = jax.ShapeDtypeStruct((S, DIL_WIDTH), _MXU)
    return pl.pallas_call(
        body, name="attn_bwd", grid=(npair, nt),
        in_specs=[full(qb0), tile(kb0), tile(vb0), full(GDN_WIDTH // 128),
                  pl.BlockSpec((None, nt, 8, T), lambda p, j: (p, 0, 0, 0)), full(GDN_WIDTH // 128),
                  pl.BlockSpec((nt, T, T), lambda p, j: (0, 0, 0))],
        out_specs=(full(0), tile(0), tile(0)),
        out_shape=(out3, out3, out3),
        scratch_shapes=[pltpu.VMEM((S, 128), F32)],
        compiler_params=_cparams(("parallel", "arbitrary")),
    )(proj, proj, proj, mix, lse, d_mix, bias)


def _ffn_act(up, cw):
    S, Cc = up.shape[0], up.shape[1] // 2
    T, tc = _pick_tile(S, 256), _pick_tile(Cc, 1536)
    r16 = T // 16
    nct = Cc // tc

    def body(g_ref, gp_ref, u_ref, up_ref, wg_ref, wu_ref, o_ref):
        keep = jnp.where(pl.program_id(1) == 0, 0.0, 1.0)
        cg = _conv_taps(jnp.concatenate([gp_ref[8:16, :].astype(F32) * keep, g_ref[...].astype(F32)], axis=0),
                        wg_ref[...], FFN_CONV, T)
        cu = _conv_taps(jnp.concatenate([up_ref[8:16, :].astype(F32) * keep, u_ref[...].astype(F32)], axis=0),
                        wu_ref[...], FFN_CONV, T)
        o_ref[...] = (_silu(cg) * cu).astype(o_ref.dtype)

    cur = lambda o: pl.BlockSpec((T, tc), lambda j, i: (i, j + o))
    prev = lambda o: pl.BlockSpec((16, tc), lambda j, i: (jnp.maximum(i * r16 - 1, 0), j + o))
    wsp = lambda o: pl.BlockSpec((FFN_CONV, tc), lambda j, i: (0, j + o))
    return pl.pallas_call(
        body, name="ffn_act", grid=(nct, S // T),
        in_specs=[cur(0), prev(0), cur(nct), prev(nct), wsp(0), wsp(nct)], out_specs=cur(0),
        out_shape=jax.ShapeDtypeStruct((S, Cc), _MXU),
        compiler_params=_cparams(("parallel", "parallel")),
    )(up, up, up, up, cw, cw)


def _ffn_act_bwd(d_act, up, cw):
    S, Cc = up.shape[0], up.shape[1] // 2
    T, tc = _pick_tile(S, 256), _pick_tile(Cc, 1536)
    r8, r16 = T // 8, T // 16
    nt = S // T
    nct = Cc // tc
    K = FFN_CONV

    def body(da_ref, dan_ref, g_ref, gp_ref, gn_ref, u_ref, up_ref, un_ref, wg_ref, wu_ref,
             dg_ref, du_ref, dwg_ref, dwu_ref):
        i = pl.program_id(1)
        keep_p = jnp.where(i == 0, 0.0, 1.0)
        keep_n = jnp.where(i == nt - 1, 0.0, 1.0)
        wg, wu = wg_ref[...], wu_ref[...]
        xg = jnp.concatenate([gp_ref[8:16, :].astype(F32) * keep_p, g_ref[...].astype(F32),
                              gn_ref[0:8, :].astype(F32) * keep_n], axis=0)
        xu = jnp.concatenate([up_ref[8:16, :].astype(F32) * keep_p, u_ref[...].astype(F32),
                              un_ref[0:8, :].astype(F32) * keep_n], axis=0)
        cg = _conv_taps(xg, wg, K, T + 8)
        cu = _conv_taps(xu, wu, K, T + 8)
        da = jnp.concatenate([da_ref[...], dan_ref[...] * keep_n], axis=0)
        sg = jax.nn.sigmoid(cg)
        d_cg = da * cu * (sg * (1.0 + cg * (1.0 - sg)))
        d_cu = da * (cg * sg)
        dg_ref[...] = _conv_taps_t(d_cg, wg, K, T).astype(dg_ref.dtype)
        du_ref[...] = _conv_taps_t(d_cu, wu, K, T).astype(du_ref.dtype)

        @pl.when(i == 0)
        def _():
            dwg_ref[...] = jnp.zeros_like(dwg_ref)
            dwu_ref[...] = jnp.zeros_like(dwu_ref)

        for k in range(K):
            dwg_ref[k:k + 1, :] += jnp.sum(d_cg[0:T, :] * _shifted(xg, (K - 1) - k, 8, T), axis=0, keepdims=True)
            dwu_ref[k:k + 1, :] += jnp.sum(d_cu[0:T, :] * _shifted(xu, (K - 1) - k, 8, T), axis=0, keepdims=True)

    cur = lambda o: pl.BlockSpec((T, tc), lambda j, i: (i, j + o))
    prev = lambda o: pl.BlockSpec((16, tc), lambda j, i: (jnp.maximum(i * r16 - 1, 0), j + o))
    nxt = lambda o: pl.BlockSpec((16, tc), lambda j, i: (jnp.minimum((i + 1) * r16, S // 16 - 1), j + o))
    nxt8 = pl.BlockSpec((8, tc), lambda j, i: (jnp.minimum((i + 1) * r8, S // 8 - 1), j))
    wsp = lambda o: pl.BlockSpec((K, tc), lambda j, i: (0, j + o))
    return pl.pallas_call(
        body, name="ffn_act_bwd", grid=(nct, nt),
        in_specs=[cur(0), nxt8, cur(0), prev(0), nxt(0), cur(nct), prev(nct), nxt(nct), wsp(0), wsp(nct)],
        out_specs=(cur(0), cur(0), wsp(0), wsp(0)),
        out_shape=(jax.ShapeDtypeStruct((S, Cc), _MXU), jax.ShapeDtypeStruct((S, Cc), _MXU),
                   jax.ShapeDtypeStruct((K, Cc), F32), jax.ShapeDtypeStruct((K, Cc), F32)),
        compiler_params=_cparams(("parallel", "arbitrary")),
    )(d_act, d_act, up, up, up, up, up, up, cw, cw)


def _local_step(x, tgt, h1, n1w, n2w, fnw, gp, gnw, wp, conv_w, fcw, rest_weights, early_grads):
    proj = _mm(h1, wp, "nn", name="proj")
    u_v, w_k, q_dec, k_end, attn, tinv, g_end = _gdn_pre(proj, conv_w, gp)
    mix, lse = _attn_fwd(proj)
    mix, states = _gdn_scan(u_v, w_k, q_dec, k_end, attn, g_end, proj, gnw, mix, after=[rest_weights[0]([mix])])
    w_out, w_up4, w_down = rest_weights[1]([mix])
    x2 = _mm(mix, w_out, "nn", residual=x, name="outproj")
    h2 = _rmsnorm_fwd(x2, n2w, "norm2")
    up = _mm(h2, w_up4, "nn", b_blocks=True, out_dtype=_MXU, name="up")
    act = _ffn_act(up, fcw)
    x3 = _mm(act, w_down, "nn", residual=x2, name="down")
    loss, dx3, dx3n, d_fnw = _loss_head(x3, fnw, tgt, "loss_head")
    d_act = _mm(dx3n, w_down, "nt", name="d_act")
    d_wdown = _mm(act, dx3n, "tn", name="d_wdown")
    d_upg, d_upu, d_fcwg, d_fcwu = _ffn_act_bwd(d_act, up, fcw)
    d_wup = _mm(h2, d_upg, "tn", place=("blocks", N_CHIPS, 0), tn=w_up4.shape[2], name="d_wgate")
    d_wup = _mm(h2, d_upu, "tn", place=("blocks", N_CHIPS, N_CHIPS // 2), tn=w_up4.shape[2], into=d_wup, name="d_wup")
    d_h2 = _mm_nt_blocks([d_upg, d_upu], w_up4, "d_h2")
    dx2, d_n2w = _rmsnorm_bwd(d_h2, x2, n2w, dx3, "norm2_bwd")
    d_wout = _mm(mix, dx2, "tn", name="d_wout")
    token = early_grads[0](d_wup, d_wdown, d_wout)
    d_mix = _mm(dx2, w_out, "nt", name="d_mix", after=[token])
    dq_b, dk_b, dv_b = _attn_bwd(proj, mix, lse, d_mix)
    d_uv, d_wk, d_qd, d_ke, d_at, d_ge, d_z, d_gnw = _gdn_scan_bwd(u_v, w_k, q_dec, k_end, attn, g_end, proj,
                                                                   gnw, states, d_mix)
    d_pre, d_ba, d_gp = _gdn_post(proj, conv_w, gp, tinv, u_v, w_k, d_uv, d_wk, d_qd, d_ke, d_at, d_ge)
    token = early_grads[1]([d_pre])
    d_qkva, d_convw = _conv_bwd(d_pre, proj, 0, conv_w + token[0:1, 0:1], GDN_CONV, "gdn_conv_bwd", 512)
    d_proj = jnp.concatenate([d_qkva, d_z.astype(_MXU), dq_b, dk_b, dv_b, d_ba.astype(_MXU),
                              jnp.zeros((x.shape[0], P_COLS - P_BA - 128), _MXU)], axis=1)
    d_wp = _mm(h1, d_proj, "tn", name="d_wp")
    token = early_grads[2](d_wp)
    d_h1 = _mm(d_proj, wp, "nt", name="d_h1", after=[token])
    token = early_grads[3]([d_h1])
    dx, d_n1w = _rmsnorm_bwd(d_h1, x, n1w, dx2, "norm1_bwd", after=[token])
    grads = dict(wp=d_wp, conv_w=d_convw, w_out=d_wout, w_up=d_wup, fcw_g=d_fcwg, fcw_u=d_fcwu, w_down=d_wdown,
                 n1w=d_n1w, n2w=d_n2w, fnw=d_fnw, gp=d_gp, gnw=d_gnw)
    return loss, dx, grads


_HBM = pl.BlockSpec(memory_space=pltpu.HBM)


def _pos():
    return lax.axis_index("x"), lax.axis_index("y"), lax.axis_index("c")


def _other_chips(x, y):
    return [(1 - x, y), (x, 1 - y), (1 - x, 1 - y)]


def _halvable(shape):
    return shape[0] % 32 == 0


def _rows_of_half(shape, half):
    if not _halvable(shape):
        return pl.ds(0, shape[0])
    return pl.ds(pl.multiple_of(half * (shape[0] // 2), 16), shape[0] // 2)


_SEM = pl.BlockSpec(memory_space=pltpu.SEMAPHORE)
_ANY = pl.BlockSpec(memory_space=pl.ANY)
_DATAFLOW = pltpu.SideEffectType.DATAFLOW_SIDE_EFFECTING


def _in_hbm(a):
    return pltpu.with_memory_space_constraint(a, pltpu.HBM)


def _halves_copy(src_refs, land_refs, send_sems, recv_sems, shapes, a, j, block, x, y, c):
    px, py = _other_chips(x, y)[j]
    rows = _rows_of_half(shapes[a], c)
    return pltpu.make_async_remote_copy(
        src_ref=src_refs[a].at[rows, :], dst_ref=land_refs[a].at[block, rows, :], send_sem=send_sems.at[3 * a + j],
        recv_sem=recv_sems.at[3 * a + j], device_id=(px, py, c), device_id_type=MESH)


def _gather_halves_start(shards, after, name):
    n = len(shards)
    shapes = [s.shape for s in shards]

    def body(*refs):
        ins, lands = refs[:n], refs[n:2 * n]
        send_sems, recv_sems = refs[2 * n + 1], refs[2 * n + 2]
        token = refs[-1]
        x, y, c = _pos()
        q = 2 * x + y
        for a in range(n):
            for j in range(3):
                _halves_copy(ins, lands, send_sems, recv_sems, shapes, a, j, q, x, y, c).start()
        token[...] = jnp.zeros_like(token)

    land_shapes = [(N_CHIPS,) + s.shape for s in shards]
    return pl.pallas_call(
        body, name=name,
        out_shape=(pltpu.SemaphoreType.DMA((3 * n,)), pltpu.SemaphoreType.DMA((3 * n,)),
                   *[pltpu.HBM(s.shape, s.dtype) for s in shards],
                   *[pltpu.HBM(ls, s.dtype) for ls, s in zip(land_shapes, shards)],
                   jax.ShapeDtypeStruct((8, 128), F32)),
        in_specs=[_HBM] * (2 * n) + [_ANY],
        out_specs=(_SEM, _SEM, *[_HBM] * (2 * n), pl.BlockSpec(memory_space=pltpu.VMEM)),
        input_output_aliases={a: 2 + a for a in range(2 * n)},
        compiler_params=pltpu.CompilerParams(has_side_effects=_DATAFLOW),
    )(*[_in_hbm(s) for s in shards], *[_in_hbm(lax.empty(ls, s.dtype)) for ls, s in zip(land_shapes, shards)], after)


def _gather_halves_wait(started, after, name):
    send_sems, recv_sems, *thru = started
    n = len(thru) // 2
    shapes = [t.shape for t in thru[:n]]

    def body(*refs):
        ins, lands = refs[:n], refs[n:2 * n]
        send_sems, recv_sems = refs[2 * n], refs[2 * n + 1]
        x, y, c = _pos()
        q = 2 * x + y
        chips = _other_chips(x, y)
        for a in range(n):
            for j, (px, py) in enumerate(chips):
                _halves_copy(ins, lands, send_sems, recv_sems, shapes, a, j, q, x, y, c).wait_send()
                _halves_copy(ins, lands, send_sems, recv_sems, shapes, a, j, 2 * px + py, x, y, c).wait_recv()

    outs = pl.pallas_call(
        body, name=name, out_shape=[pltpu.HBM(t.shape, t.dtype) for t in thru],
        in_specs=[_HBM] * (2 * n) + [_SEM, _SEM] + [_ANY] * len(after), out_specs=[_HBM] * (2 * n),
        input_output_aliases={a: a for a in range(2 * n)},
        compiler_params=pltpu.CompilerParams(has_side_effects=_DATAFLOW),
    )(*thru, send_sems, recv_sems, *after)
    return outs[:n], outs[n:]


def _sibling_fill(gathered, name):
    big = [a for a, g in enumerate(gathered) if _halvable(g.shape[1:])]
    n = len(gathered)

    def body(*refs):
        ins, outs = refs[:n], refs[n:2 * n]
        send_sems, recv_sems = refs[2 * n:]
        x, y, c = _pos()
        chips = _other_chips(x, y)

        def copy(k, j, half):
            a = big[k]
            px, py = chips[j]
            rows = _rows_of_half(gathered[a].shape[1:], half)
            return pltpu.make_async_remote_copy(
                src_ref=ins[a].at[2 * px + py, rows, :], dst_ref=outs[a].at[2 * px + py, rows, :],
                send_sem=send_sems.at[3 * k + j], recv_sem=recv_sems.at[3 * k + j],
                device_id=(x, y, 1 - c), device_id_type=MESH)

        sends = [copy(k, j, c) for k in range(len(big)) for j in range(3)]
        for cp in sends:
            cp.start()
        for k in range(len(big)):
            for j in range(3):
                copy(k, j, 1 - c).wait_recv()
        for cp in sends:
            cp.wait_send()

    return pl.pallas_call(
        body, name=name, in_specs=[_HBM] * n, out_specs=[_HBM] * n,
        out_shape=[jax.ShapeDtypeStruct(g.shape, g.dtype) for g in gathered],
        input_output_aliases={a: a for a in range(n)},
        scratch_shapes=[pltpu.SemaphoreType.DMA((3 * len(big),)), pltpu.SemaphoreType.DMA((3 * len(big),))],
    )(*gathered)


def _fill_copy(refs, send_sems, recv_sems, shapes, a, j, half, x, y, c):
    px, py = _other_chips(x, y)[j]
    rows = _rows_of_half(shapes[a], half)
    return pltpu.make_async_remote_copy(
        src_ref=refs[a].at[2 * px + py, rows, :], dst_ref=refs[a].at[2 * px + py, rows, :],
        send_sem=send_sems.at[3 * a + j], recv_sem=recv_sems.at[3 * a + j],
        device_id=(x, y, 1 - c), device_id_type=MESH)


def _sibling_fill_start(gathered, name):
    n = len(gathered)
    shapes = [g.shape[1:] for g in gathered]

    def body(*refs):
        ins = refs[:n]
        send_sems, recv_sems = refs[n], refs[n + 1]
        token = refs[-1]
        x, y, c = _pos()
        for a in range(n):
            for j in range(3):
                _fill_copy(ins, send_sems, recv_sems, shapes, a, j, c, x, y, c).start()
        token[...] = jnp.zeros_like(token)

    return pl.pallas_call(
        body, name=name,
        out_shape=(pltpu.SemaphoreType.DMA((3 * n,)), pltpu.SemaphoreType.DMA((3 * n,)),
                   *[pltpu.HBM(g.shape, g.dtype) for g in gathered], jax.ShapeDtypeStruct((8, 128), F32)),
        in_specs=[_HBM] * n,
        out_specs=(_SEM, _SEM, *[_HBM] * n, pl.BlockSpec(memory_space=pltpu.VMEM)),
        input_output_aliases={a: 2 + a for a in range(n)},
        compiler_params=pltpu.CompilerParams(has_side_effects=_DATAFLOW),
    )(*[_in_hbm(g) for g in gathered])


def _sibling_fill_wait(started, after, name):
    send_sems, recv_sems, *thru = started
    n = len(thru)
    shapes = [t.shape[1:] for t in thru]

    def body(*refs):
        ins = refs[:n]
        send_sems, recv_sems = refs[n], refs[n + 1]
        x, y, c = _pos()
        for a in range(n):
            for j in range(3):
                _fill_copy(ins, send_sems, recv_sems, shapes, a, j, c, x, y, c).wait_send()
                _fill_copy(ins, send_sems, recv_sems, shapes, a, j, 1 - c, x, y, c).wait_recv()

    return pl.pallas_call(
        body, name=name, out_shape=[pltpu.HBM(t.shape, t.dtype) for t in thru],
        in_specs=[_HBM] * n + [_SEM, _SEM] + [_ANY] * len(after), out_specs=[_HBM] * n,
        input_output_aliases={a: a for a in range(n)},
        compiler_params=pltpu.CompilerParams(has_side_effects=_DATAFLOW),
    )(*thru, send_sems, recv_sems, *after)


def _place_own(shards, gathered, cq, name, carry=()):
    n = len(shards)
    nc = len(carry)
    steps = 4

    def body(cq_ref, *refs):
        for a in range(n):
            refs[2 * n + nc + a][...] = refs[a][...]

    def tile(shape):
        return shape[0] // steps if _halvable(shape) else shape[0]

    in_specs = [pl.BlockSpec((tile(s.shape), s.shape[1]), (lambda i, s_: (i, 0)) if _halvable(s.shape) else (lambda i, s_: (0, 0)))
                for s in shards]
    in_specs += [pl.BlockSpec(memory_space=pl.ANY)] * (n + nc)
    out_specs = [pl.BlockSpec((None, tile(s.shape), s.shape[1]),
                              (lambda i, s_: (s_[1], i, 0)) if _halvable(s.shape) else (lambda i, s_: (s_[1], 0, 0)))
                 for s in shards]
    out_specs += [pl.BlockSpec(memory_space=pl.ANY)] * nc
    gs = pltpu.PrefetchScalarGridSpec(num_scalar_prefetch=1, grid=(steps,), in_specs=in_specs, out_specs=out_specs)
    outs = pl.pallas_call(
        body, name=name, grid_spec=gs,
        out_shape=[jax.ShapeDtypeStruct(g.shape, g.dtype) for g in gathered] + [jax.ShapeDtypeStruct(t.shape, t.dtype) for t in carry],
        input_output_aliases={1 + n + a: a for a in range(n + nc)},
        compiler_params=_cparams(("arbitrary",)),
    )(cq, *shards, *gathered, *carry)
    return (outs[:n], outs[n:]) if nc else outs


def _half_rows(ref, c, rh):
    return ref.at[:, pl.ds(pl.multiple_of(c * rh, 8), rh), :]


def _chips_copy(src_refs, land_refs, send_sems, recv_sems, a, j, x, y, c):
    px, py = _other_chips(x, y)[j]
    return pltpu.make_async_remote_copy(src_ref=src_refs[a].at[2 * px + py], dst_ref=land_refs[a].at[j],
                                        send_sem=send_sems.at[3 * a + j], recv_sem=recv_sems.at[3 * a + j],
                                        device_id=(px, py, c), device_id_type=MESH)


def _peer(r, x, y, c):
    return (x if r & 4 == 0 else 1 - x), (y if r & 2 == 0 else 1 - y), (c if r & 1 == 0 else 1 - c)


def _small_copy(small_ref, all_ref, send_sems, recv_sems, base, r, slot, x, y, c):
    return pltpu.make_async_remote_copy(src_ref=small_ref, dst_ref=all_ref.at[slot], send_sem=send_sems.at[base + r - 1],
                                        recv_sem=recv_sems.at[base + r - 1], device_id=_peer(r, x, y, c), device_id_type=MESH)


def _grad_chips_start(parts, name, small=None):
    n = len(parts)
    srcs = list(parts) + ([] if small is None else [small])
    m = len(srcs)

    def body(*refs):
        ins, lands = refs[:m], refs[m:2 * m]
        send_sems, recv_sems = refs[2 * m], refs[2 * m + 1]
        token = refs[-1]
        x, y, c = _pos()
        for a in range(n):
            for j in range(3):
                _chips_copy(ins, lands, send_sems, recv_sems, a, j, x, y, c).start()
        if small is not None:
            for r in range(1, 8):
                _small_copy(ins[n], lands[n], send_sems, recv_sems, 3 * n, r, 4 * x + 2 * y + c, x, y, c).start()
        token[...] = jnp.zeros_like(token)

    land_shapes = [(3,) + p.shape[1:] for p in parts] + ([] if small is None else [(8,) + small.shape])
    nsem = 3 * n + (0 if small is None else 7)
    return pl.pallas_call(
        body, name=name,
        out_shape=(pltpu.SemaphoreType.DMA((nsem,)), pltpu.SemaphoreType.DMA((nsem,)),
                   *[pltpu.HBM(p.shape, p.dtype) for p in srcs],
                   *[pltpu.HBM(ls, p.dtype) for ls, p in zip(land_shapes, srcs)],
                   jax.ShapeDtypeStruct((8, 128), F32)),
        in_specs=[_HBM] * (2 * m),
        out_specs=(_SEM, _SEM, *[_HBM] * (2 * m), pl.BlockSpec(memory_space=pltpu.VMEM)),
        input_output_aliases={a: 2 + a for a in range(2 * m)},
        compiler_params=pltpu.CompilerParams(has_side_effects=_DATAFLOW),
    )(*[_in_hbm(p) for p in srcs], *[_in_hbm(lax.empty(ls, p.dtype)) for ls, p in zip(land_shapes, srcs)])


def _grad_chips_wait(started, after, name, with_small=False):
    send_sems, recv_sems, *thru = started
    m = len(thru) // 2
    n = m - (1 if with_small else 0)

    def body(*refs):
        ins, lands = refs[:m], refs[m:2 * m]
        send_sems, recv_sems = refs[2 * m], refs[2 * m + 1]
        x, y, c = _pos()
        for a in range(n):
            for j in range(3):
                cp = _chips_copy(ins, lands, send_sems, recv_sems, a, j, x, y, c)
                cp.wait_send()
                cp.wait_recv()
        if with_small:
            for r in range(1, 8):
                px, py, pc = _peer(r, x, y, c)
                _small_copy(ins[n], lands[n], send_sems, recv_sems, 3 * n, r, 4 * x + 2 * y + c, x, y, c).wait_send()
                _small_copy(ins[n], lands[n], send_sems, recv_sems, 3 * n, r, 4 * px + 2 * py + pc, x, y, c).wait_recv()

    outs = pl.pallas_call(
        body, name=name, out_shape=[pltpu.HBM(t.shape, t.dtype) for t in thru],
        in_specs=[_HBM] * (2 * m) + [_SEM, _SEM] + [_ANY] * len(after), out_specs=[_HBM] * (2 * m),
        input_output_aliases={a: a for a in range(2 * m)},
        compiler_params=pltpu.CompilerParams(has_side_effects=_DATAFLOW),
    )(*thru, send_sems, recv_sems, *after)
    return list(outs[m:]) + list(outs[n:m])


def _sibling_copy(src_refs, land_refs, send_sems, recv_sems, rhs, a, c, x, y):
    return pltpu.make_async_remote_copy(src_ref=_half_rows(src_refs[a], 1 - c, rhs[a]), dst_ref=land_refs[a],
                                        send_sem=send_sems.at[a], recv_sem=recv_sems.at[a],
                                        device_id=(x, y, 1 - c), device_id_type=MESH)


def _grad_sibling_start(fams, name):
    n = len(fams)
    rhs = [f.shape[1] // 2 for f in fams]

    def body(*refs):
        ins, lands = refs[:n], refs[n:2 * n]
        send_sems, recv_sems = refs[2 * n], refs[2 * n + 1]
        token = refs[-1]
        x, y, c = _pos()
        for a in range(n):
            _sibling_copy(ins, lands, send_sems, recv_sems, rhs, a, c, x, y).start()
        token[...] = jnp.zeros_like(token)

    land_shapes = [(f.shape[0], f.shape[1] // 2, f.shape[2]) for f in fams]
    return pl.pallas_call(
        body, name=name,
        out_shape=(pltpu.SemaphoreType.DMA((n,)), pltpu.SemaphoreType.DMA((n,)),
                   *[pltpu.HBM(f.shape, f.dtype) for f in fams],
                   *[pltpu.HBM(ls, f.dtype) for ls, f in zip(land_shapes, fams)],
                   jax.ShapeDtypeStruct((8, 128), F32)),
        in_specs=[_HBM] * (2 * n),
        out_specs=(_SEM, _SEM, *[_HBM] * (2 * n), pl.BlockSpec(memory_space=pltpu.VMEM)),
        input_output_aliases={a: 2 + a for a in range(2 * n)},
        compiler_params=pltpu.CompilerParams(has_side_effects=_DATAFLOW),
    )(*[_in_hbm(f) for f in fams], *[_in_hbm(lax.empty(ls, f.dtype)) for ls, f in zip(land_shapes, fams)])


def _grad_sibling_wait(started, after, name):
    send_sems, recv_sems, *thru = started
    n = len(thru) // 2
    rhs = [t.shape[1] // 2 for t in thru[:n]]

    def body(*refs):
        ins, lands = refs[:n], refs[n:2 * n]
        send_sems, recv_sems = refs[2 * n], refs[2 * n + 1]
        x, y, c = _pos()
        for a in range(n):
            cp = _sibling_copy(ins, lands, send_sems, recv_sems, rhs, a, c, x, y)
            cp.wait_send()
            cp.wait_recv()

    outs = pl.pallas_call(
        body, name=name, out_shape=[pltpu.HBM(t.shape, t.dtype) for t in thru],
        in_specs=[_HBM] * (2 * n) + [_SEM, _SEM] + [_ANY] * len(after), out_specs=[_HBM] * (2 * n),
        input_output_aliases={a: a for a in range(2 * n)},
        compiler_params=pltpu.CompilerParams(has_side_effects=_DATAFLOW),
    )(*thru, send_sems, recv_sems, *after)
    return outs[:n], outs[n:]


def _grad_share(fulls, name):
    n = len(fulls)
    rhs = [f.shape[0] // 2 for f in fulls]

    def body(*refs):
        ins, outs = refs[:n], refs[n:2 * n]
        send_sems, recv_sems = refs[2 * n], refs[2 * n + 1]
        x, y, c = _pos()

        def copy(a, half):
            rows = pl.ds(pl.multiple_of(half * rhs[a], 8), rhs[a])
            return pltpu.make_async_remote_copy(src_ref=ins[a].at[rows, :], dst_ref=outs[a].at[rows, :],
                                                send_sem=send_sems.at[a], recv_sem=recv_sems.at[a],
                                                device_id=(x, y, 1 - c), device_id_type=MESH)

        sends = [copy(a, c) for a in range(n)]
        for cp in sends:
            cp.start()
        for a in range(n):
            copy(a, 1 - c).wait_recv()
        for cp in sends:
            cp.wait_send()

    return pl.pallas_call(
        body, name=name, in_specs=[_HBM] * n, out_specs=[_HBM] * n,
        out_shape=[jax.ShapeDtypeStruct(f.shape, f.dtype) for f in fulls],
        input_output_aliases={a: a for a in range(n)},
        scratch_shapes=[pltpu.SemaphoreType.DMA((n,)), pltpu.SemaphoreType.DMA((n,))],
    )(*fulls)


def _add_sibling(own, recv, cq, name):
    nb, R, Cc = own.shape
    Rh = R // 2

    def body(cq_ref, a_ref, b_ref, o32_ref, o16_ref):
        s = a_ref[0] + b_ref[0]
        mine = pl.program_id(0) == cq_ref[1]

        @pl.when(mine)
        def _():
            o32_ref[...] = s

        @pl.when(jnp.logical_not(mine))
        def _():
            o16_ref[0] = s.astype(o16_ref.dtype)

    sp = pl.BlockSpec((1, Rh, Cc), lambda b, s: (b, 0, 0))
    gs = pltpu.PrefetchScalarGridSpec(
        num_scalar_prefetch=1, grid=(nb,),
        in_specs=[pl.BlockSpec((1, Rh, Cc), lambda b, s: (b, s[0], 0)), sp],
        out_specs=[pl.BlockSpec((Rh, Cc), lambda b, s: (0, 0)), sp])
    return pl.pallas_call(
        body, name=name, grid_spec=gs,
        out_shape=[jax.ShapeDtypeStruct((Rh, Cc), F32), jax.ShapeDtypeStruct((nb, Rh, Cc), _MXU)],
        compiler_params=_cparams(("arbitrary",)),
    )(cq, own, recv)


def _add_sibling_split(d_wp, recv, cq, name):
    _, Dm, Pc = d_wp.shape
    Rh = Dm // 2
    Wb = IN_COLS // N_CHIPS
    T = 256

    def body(cq_ref, a_ref, b_ref, o32_ref, o16_ref):
        s = a_ref[0] + b_ref[0]
        blocks = [s[:, 0:Wb], s[:, Wb:2 * Wb],
                  jnp.concatenate([s[:, 2 * Wb:P_QKVB], s[:, P_BA:P_BA + 8], s[:, P_QKVB:3 * Wb - 8]], axis=1),
                  s[:, 3 * Wb - 8:P_BA]]
        q = cq_ref[1]
        own = None
        for j, blk in enumerate(blocks):
            term = jnp.where(q == j, blk, 0.0)
            own = term if own is None else own + term
            o16_ref[j] = blk.astype(o16_ref.dtype)
        o32_ref[...] = own

    gs = pltpu.PrefetchScalarGridSpec(
        num_scalar_prefetch=1, grid=(Rh // T,),
        in_specs=[pl.BlockSpec((1, T, Pc), lambda i, s: (0, s[0] * (Rh // T) + i, 0)), pl.BlockSpec((1, T, Pc), lambda i, s: (0, i, 0))],
        out_specs=[pl.BlockSpec((T, Wb), lambda i, s: (i, 0)), pl.BlockSpec((N_CHIPS, T, Wb), lambda i, s: (0, i, 0))])
    return pl.pallas_call(
        body, name=name, grid_spec=gs,
        out_shape=[jax.ShapeDtypeStruct((Rh, Wb), F32), jax.ShapeDtypeStruct((N_CHIPS, Rh, Wb), _MXU)],
        compiler_params=_cparams(("parallel",)),
    )(cq, d_wp, recv)


def _add_chips(part32, recv3, cq, name):
    Rh, Cc = part32.shape

    def body(cq_ref, a_ref, b_ref, o_ref):
        acc = a_ref[...]
        for j in range(3):
            acc = acc + b_ref[j].astype(F32)
        o_ref[...] = acc

    gs = pltpu.PrefetchScalarGridSpec(
        num_scalar_prefetch=1, grid=(1,),
        in_specs=[pl.BlockSpec((Rh, Cc), lambda i, s: (0, 0)), pl.BlockSpec((3, Rh, Cc), lambda i, s: (0, 0, 0))],
        out_specs=pl.BlockSpec((Rh, Cc), lambda i, s: (s[0], 0)))
    return pl.pallas_call(
        body, name=name, grid_spec=gs, out_shape=jax.ShapeDtypeStruct((2 * Rh, Cc), F32),
        compiler_params=_cparams(("arbitrary",)),
    )(cq, part32, recv3)


def _transposed(g):
    Dm, n = g.shape
    pad = -n % 128

    def body(g_ref, o_ref):
        xp = jnp.concatenate([g_ref[...], jnp.zeros((Dm, pad), F32)], axis=1)
        o_ref[...] = xp.T[:n, :]

    return pl.pallas_call(body, name="transposed", out_shape=jax.ShapeDtypeStruct((n, Dm), F32),
                          compiler_params=_cparams(vmem=V7X_VMEM_LIMIT))(g)


def _adamw(w, g, m, v, name):
    R, Cc = w.shape
    T = max([t for t in range(8, 257, 8) if R % t == 0], default=R)

    def body(w_ref, g_ref, m_ref, v_ref, d_ref, mo_ref, vo_ref):
        d_ref[...], mo_ref[...], vo_ref[...] = _adamw_math(w_ref[...], g_ref[...], m_ref[...], v_ref[...])

    sp = pl.BlockSpec((T, Cc), lambda i: (i, 0))
    sh = jax.ShapeDtypeStruct((R, Cc), F32)
    return pl.pallas_call(
        body, name=name, grid=(R // T,), in_specs=[sp] * 4, out_specs=(sp, sp, sp), out_shape=(sh, sh, sh),
        compiler_params=_cparams(("parallel",)),
    )(w, g, m, v)


SMALL_ROWS = 32
ROW_CONV, ROW_FCG, ROW_FCU = 5, 13, 22


def _adamw_math(w, g, m, v):
    mn = ADAM_B1 * m + (1.0 - ADAM_B1) * g
    vn = ADAM_B2 * v + (1.0 - ADAM_B2) * (g * g)
    c1 = 1.0 / (1.0 - ADAM_B1 ** ADAM_STEP)
    c2 = 1.0 / (1.0 - ADAM_B2 ** ADAM_STEP)
    return -ADAM_LR * ((mn * c1) / (jnp.sqrt(vn * c2) + ADAM_EPS) + ADAM_WD * w), mn, vn


def _pack_small(n1, n2, fn, gp, gn, conv, fcg, fcu, loss):
    W = D_MODEL

    def body(n1_ref, n2_ref, fn_ref, gp_ref, gn_ref, conv_ref, fcg_ref, fcu_ref, loss_ref, o_ref):
        o_ref[...] = jnp.zeros_like(o_ref)
        o_ref[0:1, :] = n1_ref[...]
        o_ref[1:2, :] = n2_ref[...]
        o_ref[2:3, :] = fn_ref[...]
        o_ref[3:4, 0:8] = gp_ref[0:1, 0:8]
        o_ref[3:4, 8:9] = loss_ref[0:1, 0:1]
        o_ref[4:5, 0:128] = gn_ref[...]
        for i in range(GDN_CONV):
            o_ref[ROW_CONV + 2 * i:ROW_CONV + 2 * i + 1, :] = conv_ref[i:i + 1, 0:W]
            o_ref[ROW_CONV + 2 * i + 1:ROW_CONV + 2 * i + 2, 0:3 * GDN_WIDTH - W] = conv_ref[i:i + 1, W:3 * GDN_WIDTH]
        for r0, ref in ((ROW_FCG, fcg_ref), (ROW_FCU, fcu_ref)):
            for i in range(FFN_CONV):
                for k in range(3):
                    n = min(W, D_FF - k * W)
                    o_ref[r0 + 3 * i + k:r0 + 3 * i + k + 1, 0:n] = ref[i:i + 1, k * W:k * W + n]

    return pl.pallas_call(body, name="pack_small", out_shape=jax.ShapeDtypeStruct((SMALL_ROWS, W), F32))(
        n1, n2, fn, gp, gn, conv, fcg, fcu, loss)


def _small_step(meq, small_all, small, ws, ms, vs):
    W = D_MODEL
    n = len(ws)
    cw, fw = ws[6].shape[1], ws[7].shape[1]

    def body(meq_ref, all_ref, own_ref, *refs):
        w_refs, m_refs, v_refs = refs[:n], refs[n:2 * n], refs[2 * n:3 * n]
        loss_ref = refs[3 * n]
        outs = refs[3 * n + 1:]
        me, q = meq_ref[0], meq_ref[1]
        red = None
        for d in range(8):
            term = jnp.where(me == d, own_ref[...], all_ref[d])
            red = term if red is None else red + term
        loss_ref[...] = jnp.broadcast_to(red[3:4, 8:9], loss_ref.shape)
        conv = [jnp.concatenate([red[ROW_CONV + 2 * i:ROW_CONV + 2 * i + 1, :],
                                 red[ROW_CONV + 2 * i + 1:ROW_CONV + 2 * i + 2, 0:3 * GDN_WIDTH - W]], axis=1)
                for i in range(GDN_CONV)]
        conv = jnp.concatenate(conv, axis=0)

        def fc_rows(r0):
            rows = [jnp.concatenate([red[r0 + 3 * i + k:r0 + 3 * i + k + 1, 0:min(W, D_FF - k * W)] for k in range(3)], axis=1)
                    for i in range(FFN_CONV)]
            return jnp.concatenate(rows, axis=0)

        fc = jnp.concatenate([fc_rows(ROW_FCG), fc_rows(ROW_FCU)], axis=1)

        def chip_block(full, width):
            out = None
            for j in range(N_CHIPS):
                term = jnp.where(q == j, full[:, width * j:width * (j + 1)], 0.0)
                out = term if out is None else out + term
            return out

        grads = [red[0:1, :], red[1:2, :], red[2:3, :], red[3:4, 0:4], red[3:4, 4:8], red[4:5, 0:128],
                 chip_block(conv, cw), chip_block(fc, fw)]
        for k in range(n):
            d_, m_, v_ = _adamw_math(w_refs[k][...], grads[k], m_refs[k][...], v_refs[k][...])
            outs[4 * k][...] = grads[k]
            outs[4 * k + 1][...] = d_
            outs[4 * k + 2][...] = m_
            outs[4 * k + 3][...] = v_

    full = lambda a: pl.BlockSpec(a.shape, lambda i, s_, nd=len(a.shape): (0,) * nd)
    arrays = [small_all, small, *ws, *ms, *vs]
    out_shapes = [jax.ShapeDtypeStruct((8, 128), F32)] + [jax.ShapeDtypeStruct(w.shape, F32) for w in ws for _ in range(4)]
    gs = pltpu.PrefetchScalarGridSpec(
        num_scalar_prefetch=1, grid=(1,), in_specs=[full(a) for a in arrays],
        out_specs=[pl.BlockSpec(o.shape, lambda i, s_, nd=len(o.shape): (0,) * nd) for o in out_shapes])
    return pl.pallas_call(body, name="small_step", grid_spec=gs, out_shape=out_shapes)(meq, *arrays)


def _pad_lanes(v, n=D_MODEL):
    return jnp.pad(v, ((0, 0), (0, n - v.shape[1])))


def kernel(x, norm1_w, w_in, conv_qkv_w, a_log, dt_bias, gdn_norm_w, w_out, norm2_w, w_up, ffn_conv_w, w_down, final_norm_w, loss_target, m_norm1_w, m_w_in, m_conv_qkv_w, m_a_log, m_dt_bias, m_gdn_norm_w, m_w_out, m_norm2_w, m_w_up, m_ffn_conv_w, m_w_down, m_final_norm_w, v_norm1_w, v_w_in, v_conv_qkv_w, v_a_log, v_dt_bias, v_gdn_norm_w, v_w_out, v_norm2_w, v_w_up, v_ffn_conv_w, v_w_down, v_final_norm_w):
    c = lax.axis_index("c")
    q = 2 * lax.axis_index("x") + lax.axis_index("y")
    S = x.shape[1]
    cq = jnp.stack([c, q]).astype(jnp.int32)

    *in_started, in_token = _gather_halves_start([w_in[0].astype(_MXU), conv_qkv_w[0], ffn_conv_w[0]], x, "gather_in_start")
    w_in_l, m_w_in_l, v_w_in_l = (jnp.swapaxes(a + in_token[0:1, 0:1], 1, 2)[0] for a in (w_in, m_w_in, v_w_in))
    h1 = _rmsnorm_fwd(x[0], norm1_w, "norm1", after=[in_token])
    rest = [(a[0] + in_token[0:1, 0:1]).astype(_MXU) for a in (w_out, w_up, w_down)]
    in_shards, got_in = _gather_halves_wait(in_started, [w_in_l, m_w_in_l, v_w_in_l, h1, *rest], "gather_in_wait")
    (g_in, g_conv, g_fconv), (w_in_l, m_w_in_l, v_w_in_l) = _place_own(
        in_shards, _sibling_fill(got_in, "fill_in"), cq, "place_in", carry=[w_in_l, m_w_in_l, v_w_in_l])
    *rest_started, token = _gather_halves_start(rest, g_conv, "gather_rest_start")

    rest_state = {}

    def rest_arrived(after):
        rest_state["shards"], got = _gather_halves_wait(rest_started, after, "gather_rest_wait")
        *rest_state["fill"], tok = _sibling_fill_start(got, "fill_rest_start")
        return tok

    def rest_filled(after):
        got = _sibling_fill_wait(rest_state["fill"], after, "fill_rest_wait")
        g_out, g_up, g_down = _place_own(rest_state["shards"], got, cq, "place_rest")
        return g_out.reshape(D_MODEL, D_MODEL), g_up, g_down.reshape(D_FF, D_MODEL)

    rest_weights = (rest_arrived, rest_filled)
    wp = _wp_assemble(g_in, [token])
    conv_f = jnp.concatenate([g_conv[i] for i in range(N_CHIPS)], axis=1)
    fcw = jnp.concatenate([g_fconv[i] for i in range(N_CHIPS)], axis=1)
    gp = _pad_lanes(jnp.concatenate([a_log, dt_bias], axis=1), 128)
    fnw = final_norm_w[None, :]
    early = {}

    early_names = ("w_up", "w_down", "w_out")

    def early_sibling(d_wup, d_wdown, d_wout):
        *early["sibling"], tok = _grad_sibling_start(
            [d_wup, d_wdown.reshape(N_CHIPS, D_FF // N_CHIPS, D_MODEL), d_wout.reshape(N_CHIPS, D_MODEL // N_CHIPS, D_MODEL)],
            "grad_sibling_early_start")
        return tok

    def early_chips(after):
        fams_e, got_e = _grad_sibling_wait(early["sibling"], after, "grad_sibling_early_wait")
        early["parts"] = [_add_sibling(f, r, cq, "add_sibling_" + nm) for f, r, nm in zip(fams_e, got_e, early_names)]
        *early["started"], tok = _grad_chips_start([p[1] for p in early["parts"]], "grad_chips_start")
        return tok

    def late_sibling(d_wp):
        *early["late_sibling"], tok = _grad_sibling_start([d_wp[None]], "grad_sibling_late_start")
        return tok

    def late_chips(after):
        fams, got = _grad_sibling_wait(early["late_sibling"], after, "grad_sibling_late_wait")
        early["late_part"] = _add_sibling_split(fams[0], got[0], cq, "add_sibling_w_in")
        *early["late_started"], tok = _grad_chips_start([early["late_part"][1]], "grad_chips_late_start")
        return tok

    early_grads = (early_sibling, early_chips, late_sibling, late_chips)

    loss_l, dx, g = _local_step(x[0], loss_target[0], h1, norm1_w, norm2_w, fnw, gp, gdn_norm_w, wp,
                                conv_f, fcw, rest_weights, early_grads)
    small = _pack_small(g["n1w"], g["n2w"], g["fnw"], g["gp"], g["gnw"], g["conv_w"], g["fcw_g"], g["fcw_u"], loss_l)
    *small_started, small_token = _grad_chips_start([], "small_gather_start", small)
    got3_e = _grad_chips_wait(early["started"], [dx, small_token], "grad_chips_wait")
    g_w_up, g_w_down, g_w_out = _grad_share(
        [_add_chips(p[0], r3, cq, "add_chips_" + nm) for p, r3, nm in zip(early["parts"], got3_e, early_names)],
        "grad_share_early")
    big = {}

    def adamw_big(nm, w, gg, m, v):
        d_, m_, v_ = _adamw(w[0], gg, m[0], v[0], "adamw_" + nm)
        big[nm] = (gg[None], d_[None], m_[None], v_[None])

    adamw_big("w_up", w_up, g_w_up, m_w_up, v_w_up)
    adamw_big("w_down", w_down, g_w_down, m_w_down, v_w_down)
    adamw_big("w_out", w_out, g_w_out, m_w_out, v_w_out)
    got3, = _grad_chips_wait(early["late_started"], [big[nm][1] for nm in early_names], "grad_chips_late_wait")
    g_w_in, = _grad_share([_add_chips(early["late_part"][0], got3, cq, "add_chips_w_in")], "grad_share_late")
    g_t = _transposed(g_w_in)
    d_t, m_t, v_t = _adamw(w_in_l, g_t, m_w_in_l, v_w_in_l, "adamw_w_in")
    big["w_in"] = tuple(jnp.swapaxes(t[None], 1, 2) for t in (g_t, d_t, m_t, v_t))
    small_all, small = _grad_chips_wait(small_started, [d_t], "small_gather_wait", with_small=True)
    small_names = ["norm1_w", "norm2_w", "final_norm_w", "a_log", "dt_bias", "gdn_norm_w", "conv_qkv_w", "ffn_conv_w"]
    loss_b, *small_out = _small_step(
        jnp.stack([2 * q + c, q]).astype(jnp.int32), small_all, small,
        [norm1_w, norm2_w, final_norm_w[None], a_log, dt_bias, gdn_norm_w, conv_qkv_w[0], ffn_conv_w[0]],
        [m_norm1_w, m_norm2_w, m_final_norm_w[None], m_a_log, m_dt_bias, m_gdn_norm_w, m_conv_qkv_w[0], m_ffn_conv_w[0]],
        [v_norm1_w, v_norm2_w, v_final_norm_w[None], v_a_log, v_dt_bias, v_gdn_norm_w, v_conv_qkv_w[0], v_ffn_conv_w[0]])
    like = dict(final_norm_w=lambda t: t[0], conv_qkv_w=lambda t: t[None], ffn_conv_w=lambda t: t[None])
    for k, nm in enumerate(small_names):
        big[nm] = tuple(like.get(nm, lambda t: t)(t) for t in small_out[4 * k:4 * k + 4])
    names = ["norm1_w", "w_in", "conv_qkv_w", "a_log", "dt_bias", "gdn_norm_w", "w_out", "norm2_w", "w_up",
             "ffn_conv_w", "w_down", "final_norm_w"]
    return (loss_b[0, 0], dx[None], *[big[n][0] for n in names], *[big[n][1] for n in names],
            *[big[n][2] for n in names], *[big[n][3] for n in names])
```

```python
import functools
import math

import numpy as np
import jax
import jax.numpy as jnp
from jax import lax
from jax.experimental import pallas as pl
from jax.experimental.pallas import tpu as pltpu

F32 = jnp.float32
BF16 = jnp.bfloat16
_MXU = jnp.bfloat16
_HI = lax.Precision.HIGHEST
EPS = 1e-6
V7X_VMEM_LIMIT = 56 * 1024 * 1024
MESH = pl.DeviceIdType.MESH

D_MODEL = 1024
GDN_HEADS, GDN_DIM, GDN_CHUNK, GDN_CONV = 4, 128, 64, 4
GDN_WIDTH = GDN_HEADS * GDN_DIM
DIL_HEADS, DIL_DIM = 8, 64
DIL_WIDTH = DIL_HEADS * DIL_DIM
D_FF, FFN_CONV = 2816, 3
IN_COLS = 3592
P_COLS = 3840
P_Z, P_QKVB, P_BA = 1536, 2048, 3584
ATT_T = 1024
ADAM_LR, ADAM_B1, ADAM_B2, ADAM_EPS, ADAM_WD, ADAM_STEP = 0.001, 0.9, 0.999, 1e-08, 0.01, 10
N_CHIPS = 4


def _cparams(sem=None, vmem=None):
    kw = {}
    if sem is not None:
        kw["dimension_semantics"] = sem
    if vmem is not None:
        kw["vmem_limit_bytes"] = vmem
    return pltpu.CompilerParams(**kw)


def _silu(x):
    return x * jax.nn.sigmoid(x)


def _pick_tile(n, cap):
    best = None
    for t in range(128, min(n, cap) + 1, 128):
        if n % t == 0:
            best = t
    return best or n


def _mm(a, b, mode, *, out_dtype=F32, residual=None, name, b_blocks=False, place=None, into=None, tn=None, after=()):
    if mode == "nn":
        M, K = a.shape
        N = b.shape[0] * b.shape[2] if b_blocks else b.shape[1]
    elif mode == "nt":
        (M, K), (N, _) = a.shape, b.shape
    else:
        (K, M), (_, N) = a.shape, b.shape
    tm = _pick_tile(M, 1024)
    tn = b.shape[2] if b_blocks else (tn or _pick_tile(N, 1536))

    def vmem(tm, tn):
        return 2 * (tm * K * a.dtype.itemsize + tn * K * b.dtype.itemsize
                    + tm * tn * (jnp.dtype(out_dtype).itemsize + (4 if residual is not None else 0))) + 3 * tm * tn * 4

    fixed_tn = b_blocks or (place is not None and place[0] == "blocks")
    while vmem(tm, tn) > 40 * 1024 * 1024:
        if (tm >= tn or fixed_tn) and tm % 256 == 0:
            tm //= 2
        elif tn % 256 == 0 and not fixed_tn:
            tn //= 2
        else:
            tm //= 2
    a_spec = pl.BlockSpec((K, tm), lambda j, i: (0, i)) if mode == "tn" else pl.BlockSpec((tm, K), lambda j, i: (i, 0))
    if b_blocks:
        b_spec = pl.BlockSpec((None, K, tn), lambda j, i: (j, 0, 0))
    else:
        b_spec = pl.BlockSpec((tn, K), lambda j, i: (j, 0)) if mode == "nt" else pl.BlockSpec((K, tn), lambda j, i: (0, j))
    r_spec = pl.BlockSpec((tm, tn), lambda j, i: (i, j))
    if place is None:
        o_spec, o_shape = r_spec, (M, N)
    elif place[0] == "rows":
        off = place[2] // tm
        o_spec, o_shape = pl.BlockSpec((tm, tn), lambda j, i: (i + off, j)), (place[1], N)
    else:
        off = place[2]
        o_spec, o_shape = pl.BlockSpec((None, tm, tn), lambda j, i: (j + off, i, 0)), (place[1], M, tn)
    dims = {"nn": (((1,), (0,)), ((), ())), "nt": (((1,), (1,)), ((), ())), "tn": (((0,), (0,)), ((), ()))}[mode]

    def body(*refs):
        a_ref, b_ref = refs[0], refs[1]
        o_ref = refs[-1]
        acc = lax.dot_general(a_ref[...].astype(_MXU), b_ref[...].astype(_MXU), dims, preferred_element_type=F32)
        if residual is not None:
            acc = acc + refs[2][...]
        o_ref[...] = acc.astype(out_dtype)

    ins, specs, alias = [a, b], [a_spec, b_spec], {}
    if residual is not None:
        ins.append(residual)
        specs.append(r_spec)
    if into is not None:
        alias = {len(ins): 0}
        ins.append(into)
        specs.append(pl.BlockSpec(memory_space=pl.ANY))
    ins += list(after)
    specs += [pl.BlockSpec(memory_space=pl.ANY)] * len(after)
    return pl.pallas_call(
        body, name=name, grid=(N // tn, M // tm), in_specs=specs, out_specs=o_spec,
        out_shape=jax.ShapeDtypeStruct(o_shape, out_dtype), input_output_aliases=alias,
        compiler_params=_cparams(("parallel", "parallel"), V7X_VMEM_LIMIT),
    )(*ins)


def _mm_nt_blocks(a_list, b4, name, after=()):
    M = a_list[0].shape[0]
    nb, N, Kb = b4.shape
    tm, tn = _pick_tile(M, 1024), _pick_tile(N, 512)

    def body(a0_ref, a1_ref, b_ref, *rest):
        o_ref = rest[-1]
        acc = None
        for blk in range(nb):
            a_ref = (a0_ref, a1_ref)[blk // 2]
            lo = (blk % 2) * Kb
            t = lax.dot_general(a_ref[:, lo:lo + Kb].astype(_MXU), b_ref[blk].astype(_MXU), (((1,), (1,)), ((), ())),
                                preferred_element_type=F32)
            acc = t if acc is None else acc + t
        o_ref[...] = acc

    a_spec = pl.BlockSpec((tm, 2 * Kb), lambda j, i: (i, 0))
    return pl.pallas_call(
        body, name=name, grid=(N // tn, M // tm),
        in_specs=[a_spec, a_spec, pl.BlockSpec((nb, tn, Kb), lambda j, i: (0, j, 0))]
        + [pl.BlockSpec(memory_space=pl.ANY)] * len(after),
        out_specs=pl.BlockSpec((tm, tn), lambda j, i: (i, j)), out_shape=jax.ShapeDtypeStruct((M, N), F32),
        compiler_params=_cparams(("parallel", "parallel"), V7X_VMEM_LIMIT),
    )(a_list[0], a_list[1], b4, *after)


def _wp_assemble(g_in, after=()):
    nb, Dm, Wb = g_in.shape
    T = 256
    n_lo = P_QKVB - 2 * Wb

    def body(g_ref, *rest):
        g2 = g_ref[2]
        rest[-1][...] = jnp.concatenate(
            [g_ref[0], g_ref[1], g2[:, :n_lo], g2[:, n_lo + 8:], g_ref[3], g2[:, n_lo:n_lo + 8],
             jnp.zeros((T, P_COLS - P_BA - 8), g_in.dtype)], axis=1)

    return pl.pallas_call(
        body, name="wp_assemble", grid=(Dm // T,),
        in_specs=[pl.BlockSpec((nb, T, Wb), lambda i: (0, i, 0))] + [pl.BlockSpec(memory_space=pl.ANY)] * len(after),
        out_specs=pl.BlockSpec((T, P_COLS), lambda i: (i, 0)), out_shape=jax.ShapeDtypeStruct((Dm, P_COLS), g_in.dtype),
        compiler_params=_cparams(("parallel",)),
    )(g_in, *after)


def _rmsnorm_fwd(x, w, name, after=()):
    S, D = x.shape
    T = _pick_tile(S, 512)

    def body(x_ref, w_ref, *rest):
        xv = x_ref[...]
        rs = lax.rsqrt(jnp.mean(xv * xv, axis=-1, keepdims=True) + EPS)
        rest[-1][...] = (xv * rs * w_ref[...]).astype(rest[-1].dtype)

    return pl.pallas_call(
        body, name=name, grid=(S // T,),
        in_specs=[pl.BlockSpec((T, D), lambda i: (i, 0)), pl.BlockSpec((1, D), lambda i: (0, 0))] + [_ANY] * len(after),
        out_specs=pl.BlockSpec((T, D), lambda i: (i, 0)),
        out_shape=jax.ShapeDtypeStruct((S, D), _MXU),
        compiler_params=_cparams(("parallel",)),
    )(x, w, *after)


def _rmsnorm_bwd(dh, x, w, dres, name, after=()):
    S, D = x.shape
    pair = isinstance(dh, tuple)
    T = _pick_tile(S, 256 if pair else 512)
    dhs = list(dh) if pair else [dh]

    def body(*refs):
        x_ref, w_ref, dres_ref = refs[len(dhs):len(dhs) + 3]
        dx_ref, dw_ref = refs[-2:]
        xv = x_ref[...]
        rs = lax.rsqrt(jnp.mean(xv * xv, axis=-1, keepdims=True) + EPS)
        xn = xv * rs
        if pair:
            dhv = lax.dot_general(refs[0][...].astype(_MXU), refs[1][...].astype(_MXU), (_NT, ((), ())),
                                  preferred_element_type=F32)
        else:
            dhv = refs[0][...]
        dxn = dhv * w_ref[...]
        dx_ref[...] = dres_ref[...] + rs * (dxn - xn * jnp.mean(dxn * xn, axis=-1, keepdims=True))

        @pl.when(pl.program_id(0) == 0)
        def _():
            dw_ref[...] = jnp.zeros_like(dw_ref)

        dw_ref[...] += jnp.sum(dhv * xn, axis=0, keepdims=True)

    row = pl.BlockSpec((T, D), lambda i: (i, 0))
    vec = pl.BlockSpec((1, D), lambda i: (0, 0))
    dh_specs = [pl.BlockSpec((T, dhs[0].shape[1]), lambda i: (i, 0)), pl.BlockSpec(dhs[1].shape, lambda i: (0, 0))] if pair else [row]
    return pl.pallas_call(
        body, name=name, grid=(S // T,), in_specs=dh_specs + [row, vec, row] + [_ANY] * len(after), out_specs=(row, vec),
        out_shape=(jax.ShapeDtypeStruct((S, D), F32), jax.ShapeDtypeStruct((1, D), F32)),
        compiler_params=_cparams(("arbitrary",), V7X_VMEM_LIMIT if pair else None),
    )(*dhs, x, w, dres, *after)


def _loss_head(x3, w, tgt, name):
    S, D = x3.shape
    T = _pick_tile(S, 512)

    def body(x_ref, w_ref, t_ref, loss_ref, dx_ref, dxn_ref, dw_ref):
        xv = x_ref[...]
        rs = lax.rsqrt(jnp.mean(xv * xv, axis=-1, keepdims=True) + EPS)
        xn = xv * rs
        err = xn * w_ref[...] - t_ref[...]
        dy = err * (1.0 / D)
        dxn = dy * w_ref[...]
        dxv = rs * (dxn - xn * jnp.mean(dxn * xn, axis=-1, keepdims=True))
        dx_ref[...] = dxv
        dxn_ref[...] = dxv.astype(dxn_ref.dtype)

        @pl.when(pl.program_id(0) == 0)
        def _():
            dw_ref[...] = jnp.zeros_like(dw_ref)
            loss_ref[...] = jnp.zeros_like(loss_ref)

        dw_ref[...] += jnp.sum(dy * xn, axis=0, keepdims=True)
        part = jnp.sum(jnp.sum(err * err, axis=-1, keepdims=True), axis=0, keepdims=True) * (0.5 / D)
        loss_ref[...] += jnp.broadcast_to(part, loss_ref.shape)

    row = pl.BlockSpec((T, D), lambda i: (i, 0))
    vec = pl.BlockSpec((1, D), lambda i: (0, 0))
    return pl.pallas_call(
        body, name=name, grid=(S // T,), in_specs=[row, vec, row],
        out_specs=(pl.BlockSpec((8, 128), lambda i: (0, 0)), row, row, vec),
        out_shape=(jax.ShapeDtypeStruct((8, 128), F32), jax.ShapeDtypeStruct((S, D), F32), jax.ShapeDtypeStruct((S, D), _MXU),
                   jax.ShapeDtypeStruct((1, D), F32)),
        compiler_params=_cparams(("arbitrary",)),
    )(x3, w, tgt)


def _shifted(ext, back, lo, n):
    if back == 0:
        return ext[lo:lo + n, :]
    return pltpu.roll(ext, back % ext.shape[0], 0)[lo:lo + n, :]


def _conv_windows(ext, K, T):
    return [_shifted(ext, (K - 1) - i, 8, T) for i in range(K)]


def _conv_taps(ext, w, K, T):
    out = None
    for i, win in enumerate(_conv_windows(ext, K, T)):
        term = win * w[i:i + 1, :]
        out = term if out is None else out + term
    return out


def _conv_taps_t(ext, w, K, T):
    out = None
    for i in range(K):
        term = _shifted(ext, i - (K - 1), 0, T) * w[i:i + 1, :]
        out = term if out is None else out + term
    return out


def _tri_masks(C):
    r = lax.broadcasted_iota(jnp.int32, (C, C), 0)
    c = lax.broadcasted_iota(jnp.int32, (C, C), 1)
    return r == c, r >= c, r > c, r <= c


_NN, _NT, _TN = ((1,), (0,)), ((1,), (1,)), ((0,), (0,))
_GDN_PASSES = dict(qk=1, inv=1, sol=1, scan=1, bwd=1)


def _bdot_raw(a, b, kind, passes):
    dims = ({"NN": ((2,), (1,)), "NT": ((2,), (2,)), "TN": ((1,), (1,))}[kind], ((0,), (0,)))
    if passes == 0:
        return lax.dot_general(a, b, dims, precision=_HI, preferred_element_type=F32)
    ah, bh = a.astype(BF16), b.astype(BF16)
    out = lax.dot_general(ah, bh, dims, preferred_element_type=F32)
    if passes == 3:
        al, bl = (a - ah.astype(F32)).astype(BF16), (b - bh.astype(F32)).astype(BF16)
        out = out + lax.dot_general(ah, bl, dims, preferred_element_type=F32) + lax.dot_general(al, bh, dims, preferred_element_type=F32)
    return out


@functools.partial(jax.custom_vjp, nondiff_argnums=(2, 3))
def _bdot(a, b, kind, passes):
    return _bdot_raw(a, b, kind, passes)


def _bdot_fwd(a, b, kind, passes):
    return _bdot_raw(a, b, kind, passes), (a, b)


def _bdot_bwd(kind, passes, res, ct):
    a, b = res
    if kind == "NN":
        return _bdot_raw(ct, b, "NT", passes), _bdot_raw(a, ct, "TN", passes)
    if kind == "NT":
        return _bdot_raw(ct, b, "NN", passes), _bdot_raw(ct, a, "TN", passes)
    return _bdot_raw(b, ct, "NT", passes), _bdot_raw(a, ct, "NN", passes)


_bdot.defvjp(_bdot_fwd, _bdot_bwd)


def _softplus(x):
    return jnp.maximum(x, 0.0) + jnp.log(1.0 + jnp.exp(-jnp.abs(x)))


def _gdn_stage1(cq, ck, cv, b_col, a_col, alog, dtb, dot=_bdot_raw):
    C = cq.shape[1]
    eye, incl, strict, incl_t = _tri_masks(C)
    qn = cq * lax.rsqrt(jnp.sum(cq * cq, axis=-1, keepdims=True) + EPS) * (GDN_DIM ** -0.5)
    kn = ck * lax.rsqrt(jnp.sum(ck * ck, axis=-1, keepdims=True) + EPS)
    beta = jax.nn.sigmoid(b_col)
    g = -jnp.exp(alog) * _softplus(a_col + dtb)
    g_row = jnp.sum(jnp.where(eye, g, 0.0), axis=1, keepdims=True)
    beta_row = jnp.sum(jnp.where(eye, beta, 0.0), axis=1, keepdims=True)
    gc_col = jnp.sum(jnp.where(incl, g_row, 0.0), axis=2, keepdims=True)
    gc_row = jnp.sum(jnp.where(incl_t, g, 0.0), axis=1, keepdims=True)
    dec = jnp.where(incl, jnp.exp(jnp.where(incl, gc_col - gc_row, 0.0)), 0.0)
    kk = dot(kn, kn, "NT", _GDN_PASSES["qk"])
    qk = dot(qn, kn, "NT", _GDN_PASSES["qk"])
    lmat = jnp.where(strict, dec * kk * beta_row, 0.0)
    attn = dec * qk * beta_row
    gam = jnp.exp(gc_col)
    gc_last = gc_col[:, C - 1:C, :]
    k_end = kn * (jnp.exp(gc_last - gc_col) * beta)
    return lmat, cv, gam * kn, gam * qn, attn, k_end, jnp.exp(gc_last)


def _tri_inv(lmat):
    C = lmat.shape[1]
    eye = _tri_masks(C)[0]
    ps = _GDN_PASSES["inv"]
    p = jnp.where(eye, 1.0, 0.0) - lmat
    lp = _bdot_raw(lmat, lmat, "NN", ps)
    n = int(math.log2(C))
    for s in range(1, n):
        p = p + _bdot_raw(p, lp, "NN", ps)
        if s < n - 1:
            lp = _bdot_raw(lp, lp, "NN", ps)
    return p


def _gated_norm(o, z, gnw):
    on = o * lax.rsqrt(jnp.mean(o * o, axis=-1, keepdims=True) + EPS) * gnw
    return on * _silu(z)


GDN_PG = 4
GDN_SG = 4


def _gdn_pairs(c, ba, gp, G):
    C, W, H = GDN_CHUNK, GDN_WIDTH, GDN_HEADS
    pairs = [(j, h) for j in range(G) for h in range(H)]
    cq, ck, cv = (jnp.stack([c[C * j:C * (j + 1), o + GDN_DIM * h:o + GDN_DIM * (h + 1)] for j, h in pairs]) for o in (0, W, 2 * W))
    b_col = jnp.stack([ba[C * j:C * (j + 1), h:h + 1] for j, h in pairs])
    a_col = jnp.stack([ba[C * j:C * (j + 1), H + h:H + h + 1] for j, h in pairs])
    alog = jnp.stack([gp[0:1, h:h + 1] for j, h in pairs])
    dtb = jnp.stack([gp[0:1, H + h:H + h + 1] for j, h in pairs])
    return pairs, (cq, ck, cv, b_col, a_col, alog, dtb)


def _gdn_pre_specs(S, G):
    C = GDN_CHUNK
    T = C * G
    return dict(
        cur=pl.BlockSpec((T, 3 * GDN_WIDTH), lambda i: (i, 0)),
        prev=pl.BlockSpec((8, 3 * GDN_WIDTH), lambda i: (jnp.maximum(i * (T // 8) - 1, 0), 0)),
        ba=pl.BlockSpec((T, 128), lambda i: (i, P_BA // 128)),
        cw=pl.BlockSpec((GDN_CONV, 3 * GDN_WIDTH), lambda i: (0, 0)),
        vec=pl.BlockSpec((1, 128), lambda i: (0, 0)),
        hd=pl.BlockSpec((GDN_HEADS, T, GDN_DIM), lambda i: (0, i, 0)),
        hc=pl.BlockSpec((GDN_HEADS, T, C), lambda i: (0, i, 0)),
        ge=pl.BlockSpec((G, GDN_HEADS, 8, 128), lambda i: (i, 0, 0, 0)),
    )


def _hd_shape(S, last=GDN_DIM):
    return jax.ShapeDtypeStruct((GDN_HEADS, S, last), F32)


def _gdn_pre(proj, conv_w, gp):
    S = proj.shape[0]
    C, G = GDN_CHUNK, GDN_PG
    nc = S // C
    sp = _gdn_pre_specs(S, G)

    def body(cur_ref, prev_ref, ba_ref, cw_ref, gp_ref, uv_ref, wk_ref, qd_ref, ke_ref, at_ref, ti_ref, ge_ref):
        prev = prev_ref[...] * jnp.where(pl.program_id(0) == 0, 0.0, 1.0)
        c = _silu(_conv_taps(jnp.concatenate([prev, cur_ref[...]], axis=0), cw_ref[...], GDN_CONV, C * G))
        pairs, args = _gdn_pairs(c, ba_ref[...], gp_ref[...], G)
        lmat, v, rk, q_dec, attn, k_end, g_end = _gdn_stage1(*args)
        t = _tri_inv(lmat)
        u_v = _bdot_raw(t, v, "NN", _GDN_PASSES["sol"])
        w_k = _bdot_raw(t, rk, "NN", _GDN_PASSES["sol"])
        for b, (j, h) in enumerate(pairs):
            rows = slice(C * j, C * (j + 1))
            uv_ref[h, rows, :] = u_v[b]
            wk_ref[h, rows, :] = w_k[b]
            qd_ref[h, rows, :] = q_dec[b]
            ke_ref[h, rows, :] = k_end[b]
            at_ref[h, rows, :] = attn[b]
            ti_ref[h, rows, :] = t[b]
            ge_ref[j, h] = jnp.broadcast_to(g_end[b], (8, 128))

    return pl.pallas_call(
        body, name="gdn_pre", grid=(nc // G,),
        in_specs=[sp["cur"], sp["prev"], sp["ba"], sp["cw"], sp["vec"]],
        out_specs=(sp["hd"], sp["hd"], sp["hd"], sp["hd"], sp["hc"], sp["hc"], sp["ge"]),
        out_shape=(_hd_shape(S), _hd_shape(S), _hd_shape(S), _hd_shape(S), _hd_shape(S, C), _hd_shape(S, C),
                   jax.ShapeDtypeStruct((nc, GDN_HEADS, 8, 128), F32)),
        compiler_params=_cparams(("parallel",)),
    )(proj, proj, proj, conv_w, gp)


def _gdn_scan_specs(S, G, rev):
    C = GDN_CHUNK
    T = C * G
    n = S // T
    ci = (lambda i: n - 1 - i) if rev else (lambda i: i)
    return dict(
        hd=pl.BlockSpec((GDN_HEADS, T, GDN_DIM), lambda i: (0, ci(i), 0)),
        hc=pl.BlockSpec((GDN_HEADS, T, C), lambda i: (0, ci(i), 0)),
        ge=pl.BlockSpec((G, GDN_HEADS, 8, 128), lambda i: (ci(i), 0, 0, 0)),
        z=pl.BlockSpec((T, GDN_WIDTH), lambda i: (ci(i), P_Z // GDN_WIDTH)),
        oa=pl.BlockSpec((T, GDN_WIDTH), lambda i: (ci(i), 0)),
        vec=pl.BlockSpec((1, 128), lambda i: (0, 0)),
        st=pl.BlockSpec((G, GDN_HEADS, GDN_DIM, GDN_DIM), lambda i: (ci(i), 0, 0, 0)),
    )


def _gdn_scan(u_v, w_k, q_dec, k_end, attn, g_end, proj, gnw, mix, after=()):
    S = proj.shape[0]
    C, G = GDN_CHUNK, GDN_SG
    nc = S // C
    sp = _gdn_scan_specs(S, G, False)
    ps = _GDN_PASSES["scan"]

    def body(uv_ref, wk_ref, qd_ref, ke_ref, at_ref, ge_ref, z_ref, gnw_ref, *rest):
        oa_ref, st_ref, s_scr = rest[-3:]

        @pl.when(pl.program_id(0) == 0)
        def _():
            s_scr[...] = jnp.zeros_like(s_scr)

        for j in range(G):
            rows = slice(C * j, C * (j + 1))
            st = s_scr[...]
            st_ref[j] = st
            u = uv_ref[:, rows, :] - _bdot_raw(wk_ref[:, rows, :], st, "NN", ps)
            o = _bdot_raw(qd_ref[:, rows, :], st, "NN", ps) + _bdot_raw(at_ref[:, rows, :], u, "NN", ps)
            s_scr[...] = ge_ref[j][:, 0:1, 0:1] * st + _bdot_raw(ke_ref[:, rows, :], u, "TN", ps)
            for h in range(GDN_HEADS):
                cols = slice(GDN_DIM * h, GDN_DIM * (h + 1))
                oa_ref[rows, cols] = _gated_norm(o[h], z_ref[rows, cols], gnw_ref[...])

    return pl.pallas_call(
        body, name="gdn_scan", grid=(nc // G,),
        in_specs=[sp["hd"], sp["hd"], sp["hd"], sp["hd"], sp["hc"], sp["ge"], sp["z"], sp["vec"]] + [_ANY] * (1 + len(after)),
        out_specs=(sp["oa"], sp["st"]),
        out_shape=(jax.ShapeDtypeStruct(mix.shape, F32),
                   jax.ShapeDtypeStruct((nc, GDN_HEADS, GDN_DIM, GDN_DIM), F32)),
        input_output_aliases={8: 0},
        scratch_shapes=[pltpu.VMEM((GDN_HEADS, GDN_DIM, GDN_DIM), F32)],
        compiler_params=_cparams(("arbitrary",)),
    )(u_v, w_k, q_dec, k_end, attn, g_end, proj, gnw, mix, *after)


def _gdn_scan_bwd(u_v, w_k, q_dec, k_end, attn, g_end, proj, gnw, states, d_oa):
    S = proj.shape[0]
    C, G = GDN_CHUNK, GDN_SG
    nc = S // C
    sp = _gdn_scan_specs(S, G, True)
    ps, pb = _GDN_PASSES["scan"], _GDN_PASSES["bwd"]

    def body(uv_ref, wk_ref, qd_ref, ke_ref, at_ref, ge_ref, z_ref, gnw_ref, st_ref, doa_ref,
             duv_ref, dwk_ref, dqd_ref, dke_ref, dat_ref, dge_ref, dz_ref, dgnw_ref, ds_scr):
        @pl.when(pl.program_id(0) == 0)
        def _():
            ds_scr[...] = jnp.zeros_like(ds_scr)
            dgnw_ref[...] = jnp.zeros_like(dgnw_ref)

        dgnw = jnp.zeros((1, 128), F32)
        for j in reversed(range(G)):
            rows = slice(C * j, C * (j + 1))
            st = st_ref[j]
            wk, qd, ke, at = wk_ref[:, rows, :], qd_ref[:, rows, :], ke_ref[:, rows, :], at_ref[:, rows, :]
            u = uv_ref[:, rows, :] - _bdot_raw(wk, st, "NN", ps)
            o = _bdot_raw(qd, st, "NN", ps) + _bdot_raw(at, u, "NN", ps)
            dos = []
            for h in range(GDN_HEADS):
                cols = slice(GDN_DIM * h, GDN_DIM * (h + 1))
                _, vjp2 = jax.vjp(_gated_norm, o[h], z_ref[rows, cols], gnw_ref[...])
                do_h, dz_h, dgn = vjp2(doa_ref[rows, cols])
                dz_ref[rows, cols] = dz_h
                dgnw = dgnw + dgn
                dos.append(do_h)
            do = jnp.stack(dos)
            ds_new = ds_scr[...]
            du = _bdot_raw(at, do, "TN", pb) + _bdot_raw(ke, ds_new, "NN", pb)
            duv_ref[:, rows, :] = du
            dat_ref[:, rows, :] = _bdot_raw(do, u, "NT", pb)
            dqd_ref[:, rows, :] = _bdot_raw(do, st, "NT", pb)
            dke_ref[:, rows, :] = _bdot_raw(u, ds_new, "NT", pb)
            dwk_ref[:, rows, :] = -_bdot_raw(du, st, "NT", pb)
            d_ge = jnp.sum(jnp.sum(st * ds_new, axis=2, keepdims=True), axis=1, keepdims=True)
            dge_ref[j] = jnp.broadcast_to(d_ge, (GDN_HEADS, 8, 128))
            ds_scr[...] = ge_ref[j][:, 0:1, 0:1] * ds_new + _bdot_raw(qd, do, "TN", pb) - _bdot_raw(wk, du, "TN", pb)
        dgnw_ref[...] += dgnw

    return pl.pallas_call(
        body, name="gdn_scan_bwd", grid=(nc // G,),
        in_specs=[sp["hd"], sp["hd"], sp["hd"], sp["hd"], sp["hc"], sp["ge"], sp["z"], sp["vec"], sp["st"], sp["oa"]],
        out_specs=(sp["hd"], sp["hd"], sp["hd"], sp["hd"], sp["hc"], sp["ge"], sp["oa"], sp["vec"]),
        out_shape=(_hd_shape(S), _hd_shape(S), _hd_shape(S), _hd_shape(S), _hd_shape(S, C),
                   jax.ShapeDtypeStruct((nc, GDN_HEADS, 8, 128), F32), jax.ShapeDtypeStruct((S, GDN_WIDTH), F32),
                   jax.ShapeDtypeStruct((1, 128), F32)),
        scratch_shapes=[pltpu.VMEM((GDN_HEADS, GDN_DIM, GDN_DIM), F32)],
        compiler_params=_cparams(("arbitrary",)),
    )(u_v, w_k, q_dec, k_end, attn, g_end, proj, gnw, states, d_oa)


def _gdn_post(proj, conv_w, gp, tinv, u_v, w_k, d_uv, d_wk, d_qd, d_ke, d_at, d_ge):
    S = proj.shape[0]
    C, G = GDN_CHUNK, GDN_PG
    nc = S // C
    sp = _gdn_pre_specs(S, G)
    pb = _GDN_PASSES["bwd"]

    def body(cur_ref, prev_ref, ba_ref, cw_ref, gp_ref, ti_ref, uv_ref, wk_ref, duv_ref, dwk_ref, dqd_ref, dke_ref,
             dat_ref, dge_ref, dpre_ref, dba_ref, dgp_ref):
        i = pl.program_id(0)

        @pl.when(i == 0)
        def _():
            dgp_ref[...] = jnp.zeros_like(dgp_ref)

        prev = prev_ref[...] * jnp.where(i == 0, 0.0, 1.0)
        pre = _conv_taps(jnp.concatenate([prev, cur_ref[...]], axis=0), cw_ref[...], GDN_CONV, C * G)
        sg = jax.nn.sigmoid(pre)
        dsilu = sg * (1.0 + pre * (1.0 - sg))
        pairs, args = _gdn_pairs(pre * sg, ba_ref[...], gp_ref[...], G)
        _, vjp1 = jax.vjp(functools.partial(_gdn_stage1, dot=_bdot), *args)

        def take(ref):
            return jnp.stack([ref[h, C * j:C * (j + 1), :] for j, h in pairs])

        t, u_v, w_k = take(ti_ref), take(uv_ref), take(wk_ref)
        d_v = _bdot_raw(t, take(duv_ref), "TN", pb)
        d_rk = _bdot_raw(t, take(dwk_ref), "TN", pb)
        d_l = -(_bdot_raw(d_v, u_v, "NT", pb) + _bdot_raw(d_rk, w_k, "NT", pb))
        d_ge = jnp.stack([dge_ref[j, h][0:1, 0:1] for j, h in pairs])
        dcq, dck, dcv, db, da, dalog, ddtb = vjp1((d_l, d_v, d_rk, take(dqd_ref), take(dat_ref), take(dke_ref), d_ge))
        lane = lax.broadcasted_iota(jnp.int32, (C, 128), 1)
        lane1 = lax.broadcasted_iota(jnp.int32, (1, 128), 1)
        dgp = jnp.zeros((1, 128), F32)
        for j in range(G):
            rows = slice(C * j, C * (j + 1))
            dba = jnp.zeros((C, 128), F32)
            for h in range(GDN_HEADS):
                b = GDN_HEADS * j + h
                for o_, dcx in ((0, dcq), (GDN_WIDTH, dck), (2 * GDN_WIDTH, dcv)):
                    cols = slice(o_ + GDN_DIM * h, o_ + GDN_DIM * (h + 1))
                    dpre_ref[rows, cols] = dcx[b] * dsilu[rows, cols]
                dba = dba + jnp.where(lane == h, db[b], 0.0) + jnp.where(lane == GDN_HEADS + h, da[b], 0.0)
                dgp = dgp + jnp.where(lane1 == h, dalog[b], 0.0) + jnp.where(lane1 == GDN_HEADS + h, ddtb[b], 0.0)
            dba_ref[rows, :] = dba
        dgp_ref[0:1, :] += dgp

    T = C * G
    return pl.pallas_call(
        body, name="gdn_post", grid=(nc // G,),
        in_specs=[sp["cur"], sp["prev"], sp["ba"], sp["cw"], sp["vec"], sp["hc"], sp["hd"], sp["hd"], sp["hd"], sp["hd"],
                  sp["hd"], sp["hd"], sp["hc"], sp["ge"]],
        out_specs=(sp["cur"], pl.BlockSpec((T, 128), lambda i: (i, 0)), pl.BlockSpec((8, 128), lambda i: (0, 0))),
        out_shape=(jax.ShapeDtypeStruct((S, 3 * GDN_WIDTH), F32), jax.ShapeDtypeStruct((S, 128), F32),
                   jax.ShapeDtypeStruct((8, 128), F32)),
        compiler_params=_cparams(("arbitrary",)),
    )(proj, proj, proj, conv_w, gp, tinv, u_v, w_k, d_uv, d_wk, d_qd, d_ke, d_at, d_ge)


def _conv_bwd(dpre, x, xcol0, w, K, name, tc):
    S, Cc = dpre.shape
    T = _pick_tile(S, 256)
    nt, ncol = S // T, Cc // tc
    xo = xcol0 // tc

    def body(d_ref, dn_ref, x_ref, xp_ref, w_ref, dx_ref, dw_ref):
        i = pl.program_id(1)
        dn = dn_ref[...] * jnp.where(i == nt - 1, 0.0, 1.0)
        dv = d_ref[...]
        ext_d = jnp.concatenate([dv, dn], axis=0)
        dx_ref[...] = _conv_taps_t(ext_d, w_ref[...], K, T).astype(dx_ref.dtype)
        xp = xp_ref[...] * jnp.where(i == 0, 0.0, 1.0)
        ext_x = jnp.concatenate([xp, x_ref[...]], axis=0)

        @pl.when(i == 0)
        def _():
            dw_ref[...] = jnp.zeros_like(dw_ref)

        for k in range(K):
            dw_ref[k:k + 1, :] += jnp.sum(dv * _shifted(ext_x, (K - 1) - k, 8, T), axis=0, keepdims=True)

    r8 = T // 8
    return pl.pallas_call(
        body, name=name, grid=(ncol, nt),
        in_specs=[pl.BlockSpec((T, tc), lambda j, i: (i, j)),
                  pl.BlockSpec((8, tc), lambda j, i: (jnp.minimum((i + 1) * r8, S // 8 - 1), j)),
                  pl.BlockSpec((T, tc), lambda j, i: (i, j + xo)),
                  pl.BlockSpec((8, tc), lambda j, i: (jnp.maximum(i * r8 - 1, 0), j + xo)),
                  pl.BlockSpec((K, tc), lambda j, i: (0, j))],
        out_specs=(pl.BlockSpec((T, tc), lambda j, i: (i, j)), pl.BlockSpec((K, tc), lambda j, i: (0, j))),
        out_shape=(jax.ShapeDtypeStruct((S, Cc), _MXU), jax.ShapeDtypeStruct((K, Cc), F32)),
        compiler_params=_cparams(("parallel", "arbitrary")),
    )(dpre, dpre, x, x, w)


def _dil_bias(nt, T):
    d = (np.arange(nt)[:, None, None] * T + np.arange(T)[None, None, :] - np.arange(T)[None, :, None])
    cnt = ((d >= 0) & (d <= 128)).astype(np.float64) + ((d >= 0) & (d % 4 == 0) & (d <= 512)) + ((d >= 0) & (d % 16 == 0))
    return jnp.asarray(np.where(cnt > 0, np.log(np.maximum(cnt, 1.0)), -1e30), dtype=F32)


def _attn_fwd(proj, after=()):
    S = proj.shape[0]
    T = min(ATT_T, S)
    nt, H = S // T, T // 2
    bias = _dil_bias(nt, T)
    scale = DIL_DIM ** -0.5
    npair = DIL_WIDTH // 128
    qb0, kb0, vb0 = P_QKVB // 128, (P_QKVB + DIL_WIDTH) // 128, (P_QKVB + 2 * DIL_WIDTH) // 128

    def body(q_ref, k_ref, v_ref, b_ref, *rest):
        o_ref, lse_ref = rest[-2:]
        i = pl.program_id(1)
        qs = (q_ref[...] * scale).astype(_MXU)

        def update(carry, kt, vt, qt, bt):
            out = []
            for hh in range(2):
                m, l, acc = carry[hh]
                sl = slice(hh * DIL_DIM, (hh + 1) * DIL_DIM)
                s = lax.dot_general(kt[:, sl], qt[:, sl], (_NT, ((), ())), preferred_element_type=F32) + bt
                m_new = jnp.maximum(m, jnp.max(s, axis=0, keepdims=True))
                p = jnp.exp(s - m_new)
                a = jnp.exp(m - m_new)
                l = a * l + jnp.sum(p, axis=0, keepdims=True)
                acc = a * acc + lax.dot_general(vt[:, sl], p.astype(_MXU), (_TN, ((), ())), preferred_element_type=F32)
                out.append((m_new, l, acc))
            return tuple(out)

        def keys(j):
            rows = pl.ds(pl.multiple_of(j * T, T), T)
            return k_ref[rows, :].astype(_MXU), v_ref[rows, :].astype(_MXU)

        init = tuple((jnp.full((1, T), -1e30, F32), jnp.zeros((1, T), F32), jnp.zeros((DIL_DIM, T), F32)) for _ in range(2))
        res = lax.fori_loop(0, i, lambda j, carry: update(carry, *keys(j), qs, b_ref[i - j]), init)
        kd, vd = keys(i)
        res = update(res, kd[:H], vd[:H], qs, b_ref[0, :H, :])
        late = update(tuple(tuple(t[:, H:] for t in r) for r in res), kd[H:], vd[H:], qs[H:], b_ref[0, H:, H:])
        res = tuple(tuple(jnp.concatenate([t[:, :H], u], axis=1) for t, u in zip(r, r2)) for r, r2 in zip(res, late))
        lse_ref[...] = jnp.zeros_like(lse_ref)
        for hh in range(2):
            m, l, acc = res[hh]
            o_ref[:, hh * DIL_DIM:(hh + 1) * DIL_DIM] = (acc / l).T
            lse_ref[hh:hh + 1, :] = m + jnp.log(l)

    return pl.pallas_call(
        body, name="attn_fwd", grid=(npair, nt),
        in_specs=[pl.BlockSpec((T, 128), lambda p, i: (i, qb0 + p)),
                  pl.BlockSpec((S, 128), lambda p, i: (0, kb0 + p)),
                  pl.BlockSpec((S, 128), lambda p, i: (0, vb0 + p)),
                  pl.BlockSpec((nt, T, T), lambda p, i: (0, 0, 0))] + [_ANY] * len(after),
        out_specs=(pl.BlockSpec((T, 128), lambda p, i: (i, GDN_WIDTH // 128 + p)),
                   pl.BlockSpec((None, None, 8, T), lambda p, i: (p, i, 0, 0))),
        out_shape=(jax.ShapeDtypeStruct((S, GDN_WIDTH + DIL_WIDTH), F32), jax.ShapeDtypeStruct((npair, nt, 8, T), F32)),
        compiler_params=_cparams(("parallel", "parallel")),
    )(proj, proj, proj, bias, *after)


def _attn_bwd(proj, mix, lse, d_mix):
    S = proj.shape[0]
    T = min(ATT_T, S)
    nt, H = S // T, T // 2
    bias = _dil_bias(nt, T)
    scale = DIL_DIM ** -0.5
    npair = DIL_WIDTH // 128
    qb0, kb0, vb0 = P_QKVB // 128, (P_QKVB + DIL_WIDTH) // 128, (P_QKVB + 2 * DIL_WIDTH) // 128

    def body(q_ref, k_ref, v_ref, o_ref, lse_ref, do_ref, b_ref, dq_ref, dk_ref, dv_ref, dq_scr):
        j = pl.program_id(1)

        @pl.when(j == 0)
        def _():
            dq_scr[...] = jnp.zeros_like(dq_scr)

        kt = k_ref[...].astype(_MXU)
        vt = v_ref[...].astype(_MXU)
        ones = jnp.ones((8, DIL_DIM), F32)

        def block(carry, kt, vt, rows, lsev, bt):
            qs = (q_ref[rows, :] * scale).astype(_MXU)
            dov = do_ref[rows, :]
            prod = dov * o_ref[rows, :]
            dob = dov.astype(_MXU)
            out = []
            dqs = []
            for hh in range(2):
                dk, dv = carry[hh]
                sl = slice(hh * DIL_DIM, (hh + 1) * DIL_DIM)
                s = lax.dot_general(kt[:, sl], qs[:, sl], (_NT, ((), ())), preferred_element_type=F32) + bt
                p = jnp.exp(s - lsev[hh:hh + 1, :])
                delta = lax.dot_general(ones, prod[:, sl], (_NT, ((), ())), precision=_HI, preferred_element_type=F32)[0:1, :]
                dp = lax.dot_general(vt[:, sl], dob[:, sl], (_NT, ((), ())), preferred_element_type=F32)
                ds = (p * (dp - delta)).astype(_MXU)
                dv = dv + lax.dot_general(p.astype(_MXU), dob[:, sl], (_NN, ((), ())), preferred_element_type=F32)
                dk = dk + lax.dot_general(ds, qs[:, sl], (_NN, ((), ())), preferred_element_type=F32)
                dqs.append(lax.dot_general(ds, kt[:, sl], (_TN, ((), ())), preferred_element_type=F32) * scale)
                out.append((dk, dv))
            dq_scr[rows, :] += jnp.concatenate(dqs, axis=1)
            return tuple(out)

        def step(i, carry):
            return block(carry, kt, vt, pl.ds(pl.multiple_of(i * T, T), T), lse_ref[i], b_ref[i - j])

        zeros = tuple((jnp.zeros((H, DIL_DIM), F32), jnp.zeros((H, DIL_DIM), F32)) for _ in range(2))
        lsed = lse_ref[j]
        early = block(zeros, kt[:H], vt[:H], pl.ds(pl.multiple_of(j * T, T), T), lsed, b_ref[0, :H, :])
        late = block(zeros, kt[H:], vt[H:], pl.ds(pl.multiple_of(j * T + H, H), H), lsed[:, H:], b_ref[0, H:, H:])
        init = tuple(tuple(jnp.concatenate([t, u], axis=0) for t, u in zip(r, r2)) for r, r2 in zip(early, late))
        res = lax.fori_loop(j + 1, nt, step, init)
        dk_ref[...] = jnp.concatenate([res[0][0], res[1][0]], axis=1).astype(dk_ref.dtype)
        dv_ref[...] = jnp.concatenate([res[0][1], res[1][1]], axis=1).astype(dv_ref.dtype)

        @pl.when(j == nt - 1)
        def _():
            dq_ref[...] = dq_scr[...].astype(dq_ref.dtype)

    full = lambda c0: pl.BlockSpec((S, 128), lambda p, j: (0, c0 + p))
    tile = lambda c0: pl.BlockSpec((T, 128), lambda p, j: (j, c0 + p))
    out3 = jax.ShapeDtypeStruct((S, DIL_WIDTH), _MXU)
    return pl.pallas_call(
        body, name="attn_bwd", grid=(npair, nt),
        in_specs=[full(qb0), tile(kb0), tile(vb0), full(GDN_WIDTH // 128),
                  pl.BlockSpec((None, nt, 8, T), lambda p, j: (p, 0, 0, 0)), full(GDN_WIDTH // 128),
                  pl.BlockSpec((nt, T, T), lambda p, j: (0, 0, 0))],
        out_specs=(full(0), tile(0), tile(0)),
        out_shape=(out3, out3, out3),
        scratch_shapes=[pltpu.VMEM((S, 128), F32)],
        compiler_params=_cparams(("parallel", "arbitrary")),
    )(proj, proj, proj, mix, lse, d_mix, bias)


def _ffn_act(up, cw):
    S, Cc = up.shape[0], up.shape[1] // 2
    T, tc = _pick_tile(S, 256), _pick_tile(Cc, 1536)
    r16 = T // 16
    nct = Cc // tc

    def body(g_ref, gp_ref, u_ref, up_ref, wg_ref, wu_ref, o_ref):
        keep = jnp.where(pl.program_id(1) == 0, 0.0, 1.0)
        cg = _conv_taps(jnp.concatenate([gp_ref[8:16, :].astype(F32) * keep, g_ref[...].astype(F32)], axis=0),
                        wg_ref[...], FFN_CONV, T)
        cu = _conv_taps(jnp.concatenate([up_ref[8:16, :].astype(F32) * keep, u_ref[...].astype(F32)], axis=0),
                        wu_ref[...], FFN_CONV, T)
        o_ref[...] = (_silu(cg) * cu).astype(o_ref.dtype)

    cur = lambda o: pl.BlockSpec((T, tc), lambda j, i: (i, j + o))
    prev = lambda o: pl.BlockSpec((16, tc), lambda j, i: (jnp.maximum(i * r16 - 1, 0), j + o))
    wsp = lambda o: pl.BlockSpec((FFN_CONV, tc), lambda j, i: (0, j + o))
    return pl.pallas_call(
        body, name="ffn_act", grid=(nct, S // T),
        in_specs=[cur(0), prev(0), cur(nct), prev(nct), wsp(0), wsp(nct)], out_specs=cur(0),
        out_shape=jax.ShapeDtypeStruct((S, Cc), _MXU),
        compiler_params=_cparams(("parallel", "parallel")),
    )(up, up, up, up, cw, cw)


def _ffn_act_bwd(d_act, up, cw):
    S, Cc = up.shape[0], up.shape[1] // 2
    T, tc = _pick_tile(S, 256), _pick_tile(Cc, 1536)
    r8, r16 = T // 8, T // 16
    nt = S // T
    nct = Cc // tc
    K = FFN_CONV

    def body(da_ref, dan_ref, g_ref, gp_ref, gn_ref, u_ref, up_ref, un_ref, wg_ref, wu_ref,
             dg_ref, du_ref, dwg_ref, dwu_ref):
        i = pl.program_id(1)
        keep_p = jnp.where(i == 0, 0.0, 1.0)
        keep_n = jnp.where(i == nt - 1, 0.0, 1.0)
        wg, wu = wg_ref[...], wu_ref[...]
        xg = jnp.concatenate([gp_ref[8:16, :].astype(F32) * keep_p, g_ref[...].astype(F32),
                              gn_ref[0:8, :].astype(F32) * keep_n], axis=0)
        xu = jnp.concatenate([up_ref[8:16, :].astype(F32) * keep_p, u_ref[...].astype(F32),
                              un_ref[0:8, :].astype(F32) * keep_n], axis=0)
        cg = _conv_taps(xg, wg, K, T + 8)
        cu = _conv_taps(xu, wu, K, T + 8)
        da = jnp.concatenate([da_ref[...], dan_ref[...] * keep_n], axis=0)
        sg = jax.nn.sigmoid(cg)
        d_cg = da * cu * (sg * (1.0 + cg * (1.0 - sg)))
        d_cu = da * (cg * sg)
        dg_ref[...] = _conv_taps_t(d_cg, wg, K, T).astype(dg_ref.dtype)
        du_ref[...] = _conv_taps_t(d_cu, wu, K, T).astype(du_ref.dtype)

        @pl.when(i == 0)
        def _():
            dwg_ref[...] = jnp.zeros_like(dwg_ref)
            dwu_ref[...] = jnp.zeros_like(dwu_ref)

        for k in range(K):
            dwg_ref[k:k + 1, :] += jnp.sum(d_cg[0:T, :] * _shifted(xg, (K - 1) - k, 8, T), axis=0, keepdims=True)
            dwu_ref[k:k + 1, :] += jnp.sum(d_cu[0:T, :] * _shifted(xu, (K - 1) - k, 8, T), axis=0, keepdims=True)

    cur = lambda o: pl.BlockSpec((T, tc), lambda j, i: (i, j + o))
    prev = lambda o: pl.BlockSpec((16, tc), lambda j, i: (jnp.maximum(i * r16 - 1, 0), j + o))
    nxt = lambda o: pl.BlockSpec((16, tc), lambda j, i: (jnp.minimum((i + 1) * r16, S // 16 - 1), j + o))
    nxt8 = pl.BlockSpec((8, tc), lambda j, i: (jnp.minimum((i + 1) * r8, S // 8 - 1), j))
    wsp = lambda o: pl.BlockSpec((K, tc), lambda j, i: (0, j + o))
    return pl.pallas_call(
        body, name="ffn_act_bwd", grid=(nct, nt),
        in_specs=[cur(0), nxt8, cur(0), prev(0), nxt(0), cur(nct), prev(nct), nxt(nct), wsp(0), wsp(nct)],
        out_specs=(cur(0), cur(0), wsp(0), wsp(0)),
        out_shape=(jax.ShapeDtypeStruct((S, Cc), _MXU), jax.ShapeDtypeStruct((S, Cc), _MXU),
                   jax.ShapeDtypeStruct((K, Cc), F32), jax.ShapeDtypeStruct((K, Cc), F32)),
        compiler_params=_cparams(("parallel", "arbitrary")),
    )(d_act, d_act, up, up, up, up, up, up, cw, cw)


def _local_step(x, tgt, h1, n1w, n2w, fnw, gp, gnw, wp, conv_w, fcw, rest_weights, early_grads):
    proj = _mm(h1, wp, "nn", name="proj")
    u_v, w_k, q_dec, k_end, attn, tinv, g_end = _gdn_pre(proj, conv_w, gp)
    mix, lse = _attn_fwd(proj)
    mix, states = _gdn_scan(u_v, w_k, q_dec, k_end, attn, g_end, proj, gnw, mix, after=[rest_weights[0]([mix])])
    w_out, w_up4, w_down = rest_weights[1]([mix])
    x2 = _mm(mix, w_out, "nn", residual=x, name="outproj")
    h2 = _rmsnorm_fwd(x2, n2w, "norm2")
    up = _mm(h2, w_up4, "nn", b_blocks=True, out_dtype=_MXU, name="up")
    act = _ffn_act(up, fcw)
    x3 = _mm(act, w_down, "nn", residual=x2, name="down")
    loss, dx3, dx3n, d_fnw = _loss_head(x3, fnw, tgt, "loss_head")
    d_act = _mm(dx3n, w_down, "nt", name="d_act")
    d_wdown = _mm(act, dx3n, "tn", name="d_wdown")
    d_upg, d_upu, d_fcwg, d_fcwu = _ffn_act_bwd(d_act, up, fcw)
    d_wup = _mm(h2, d_upg, "tn", place=("blocks", N_CHIPS, 0), tn=w_up4.shape[2], name="d_wgate")
    d_wup = _mm(h2, d_upu, "tn", place=("blocks", N_CHIPS, N_CHIPS // 2), tn=w_up4.shape[2], into=d_wup, name="d_wup")
    d_h2 = _mm_nt_blocks([d_upg, d_upu], w_up4, "d_h2")
    dx2, d_n2w = _rmsnorm_bwd(d_h2, x2, n2w, dx3, "norm2_bwd")
    d_wout = _mm(mix, dx2, "tn", name="d_wout")
    token = early_grads[0](d_wup, d_wdown, d_wout)
    d_mix = _mm(dx2, w_out, "nt", name="d_mix", after=[token])
    dq_b, dk_b, dv_b = _attn_bwd(proj, mix, lse, d_mix)
    d_uv, d_wk, d_qd, d_ke, d_at, d_ge, d_z, d_gnw = _gdn_scan_bwd(u_v, w_k, q_dec, k_end, attn, g_end, proj,
                                                                   gnw, states, d_mix)
    d_pre, d_ba, d_gp = _gdn_post(proj, conv_w, gp, tinv, u_v, w_k, d_uv, d_wk, d_qd, d_ke, d_at, d_ge)
    token = early_grads[1]([d_pre])
    d_qkva, d_convw = _conv_bwd(d_pre, proj, 0, conv_w + token[0:1, 0:1], GDN_CONV, "gdn_conv_bwd", 512)
    d_proj = jnp.concatenate([d_qkva, d_z.astype(_MXU), dq_b, dk_b, dv_b, d_ba.astype(_MXU),
                              jnp.zeros((x.shape[0], P_COLS - P_BA - 128), _MXU)], axis=1)
    d_wp = _mm(h1, d_proj, "tn", name="d_wp")
    token = early_grads[2](d_wp)
    dx, d_n1w = _rmsnorm_bwd((d_proj, wp), x, n1w, dx2, "norm1_bwd", after=[token])
    grads = dict(wp=d_wp, conv_w=d_convw, w_out=d_wout, w_up=d_wup, fcw_g=d_fcwg, fcw_u=d_fcwu, w_down=d_wdown,
                 n1w=d_n1w, n2w=d_n2w, fnw=d_fnw, gp=d_gp, gnw=d_gnw)
    return loss, dx, grads


_HBM = pl.BlockSpec(memory_space=pltpu.HBM)


def _pos():
    return lax.axis_index("x"), lax.axis_index("y"), lax.axis_index("c")


def _other_chips(x, y):
    return [(1 - x, y), (x, 1 - y), (1 - x, 1 - y)]


def _halvable(shape):
    return shape[0] % 32 == 0


def _rows_of_half(shape, half):
    if not _halvable(shape):
        return pl.ds(0, shape[0])
    return pl.ds(pl.multiple_of(half * (shape[0] // 2), 16), shape[0] // 2)


_SEM = pl.BlockSpec(memory_space=pltpu.SEMAPHORE)
_ANY = pl.BlockSpec(memory_space=pl.ANY)
_DATAFLOW = pltpu.SideEffectType.DATAFLOW_SIDE_EFFECTING


def _in_hbm(a):
    return pltpu.with_memory_space_constraint(a, pltpu.HBM)


def _halves_copy(src_refs, land_refs, send_sems, recv_sems, shapes, a, j, block, x, y, c):
    px, py = _other_chips(x, y)[j]
    rows = _rows_of_half(shapes[a], c)
    return pltpu.make_async_remote_copy(
        src_ref=src_refs[a].at[rows, :], dst_ref=land_refs[a].at[block, rows, :], send_sem=send_sems.at[3 * a + j],
        recv_sem=recv_sems.at[3 * a + j], device_id=(px, py, c), device_id_type=MESH)


def _gather_halves_start(shards, after, name):
    n = len(shards)
    shapes = [s.shape for s in shards]

    def body(*refs):
        ins, lands = refs[:n], refs[n:2 * n]
        send_sems, recv_sems = refs[2 * n + 1], refs[2 * n + 2]
        token = refs[-1]
        x, y, c = _pos()
        q = 2 * x + y
        for a in range(n):
            for j in range(3):
                _halves_copy(ins, lands, send_sems, recv_sems, shapes, a, j, q, x, y, c).start()
        token[...] = jnp.zeros_like(token)

    land_shapes = [(N_CHIPS,) + s.shape for s in shards]
    return pl.pallas_call(
        body, name=name,
        out_shape=(pltpu.SemaphoreType.DMA((3 * n,)), pltpu.SemaphoreType.DMA((3 * n,)),
                   *[pltpu.HBM(s.shape, s.dtype) for s in shards],
                   *[pltpu.HBM(ls, s.dtype) for ls, s in zip(land_shapes, shards)],
                   jax.ShapeDtypeStruct((8, 128), F32)),
        in_specs=[_HBM] * (2 * n) + [_ANY],
        out_specs=(_SEM, _SEM, *[_HBM] * (2 * n), pl.BlockSpec(memory_space=pltpu.VMEM)),
        input_output_aliases={a: 2 + a for a in range(2 * n)},
        compiler_params=pltpu.CompilerParams(has_side_effects=_DATAFLOW),
    )(*[_in_hbm(s) for s in shards], *[_in_hbm(lax.empty(ls, s.dtype)) for ls, s in zip(land_shapes, shards)], after)


def _gather_halves_wait(started, after, name):
    send_sems, recv_sems, *thru = started
    n = len(thru) // 2
    shapes = [t.shape for t in thru[:n]]

    def body(*refs):
        ins, lands = refs[:n], refs[n:2 * n]
        send_sems, recv_sems = refs[2 * n], refs[2 * n + 1]
        x, y, c = _pos()
        q = 2 * x + y
        chips = _other_chips(x, y)
        for a in range(n):
            for j, (px, py) in enumerate(chips):
                _halves_copy(ins, lands, send_sems, recv_sems, shapes, a, j, q, x, y, c).wait_send()
                _halves_copy(ins, lands, send_sems, recv_sems, shapes, a, j, 2 * px + py, x, y, c).wait_recv()

    outs = pl.pallas_call(
        body, name=name, out_shape=[pltpu.HBM(t.shape, t.dtype) for t in thru],
        in_specs=[_HBM] * (2 * n) + [_SEM, _SEM] + [_ANY] * len(after), out_specs=[_HBM] * (2 * n),
        input_output_aliases={a: a for a in range(2 * n)},
        compiler_params=pltpu.CompilerParams(has_side_effects=_DATAFLOW),
    )(*thru, send_sems, recv_sems, *after)
    return outs[:n], outs[n:]


def _sibling_fill(gathered, name):
    big = [a for a, g in enumerate(gathered) if _halvable(g.shape[1:])]
    n = len(gathered)

    def body(*refs):
        ins, outs = refs[:n], refs[n:2 * n]
        send_sems, recv_sems = refs[2 * n:]
        x, y, c = _pos()
        chips = _other_chips(x, y)

        def copy(k, j, half):
            a = big[k]
            px, py = chips[j]
            rows = _rows_of_half(gathered[a].shape[1:], half)
            return pltpu.make_async_remote_copy(
                src_ref=ins[a].at[2 * px + py, rows, :], dst_ref=outs[a].at[2 * px + py, rows, :],
                send_sem=send_sems.at[3 * k + j], recv_sem=recv_sems.at[3 * k + j],
                device_id=(x, y, 1 - c), device_id_type=MESH)

        sends = [copy(k, j, c) for k in range(len(big)) for j in range(3)]
        for cp in sends:
            cp.start()
        for k in range(len(big)):
            for j in range(3):
                copy(k, j, 1 - c).wait_recv()
        for cp in sends:
            cp.wait_send()

    return pl.pallas_call(
        body, name=name, in_specs=[_HBM] * n, out_specs=[_HBM] * n,
        out_shape=[jax.ShapeDtypeStruct(g.shape, g.dtype) for g in gathered],
        input_output_aliases={a: a for a in range(n)},
        scratch_shapes=[pltpu.SemaphoreType.DMA((3 * len(big),)), pltpu.SemaphoreType.DMA((3 * len(big),))],
    )(*gathered)


def _fill_copy(refs, send_sems, recv_sems, shapes, a, j, half, x, y, c):
    px, py = _other_chips(x, y)[j]
    rows = _rows_of_half(shapes[a], half)
    return pltpu.make_async_remote_copy(
        src_ref=refs[a].at[2 * px + py, rows, :], dst_ref=refs[a].at[2 * px + py, rows, :],
        send_sem=send_sems.at[3 * a + j], recv_sem=recv_sems.at[3 * a + j],
        device_id=(x, y, 1 - c), device_id_type=MESH)


def _sibling_fill_start(gathered, name):
    n = len(gathered)
    shapes = [g.shape[1:] for g in gathered]

    def body(*refs):
        ins = refs[:n]
        send_sems, recv_sems = refs[n], refs[n + 1]
        token = refs[-1]
        x, y, c = _pos()
        for a in range(n):
            for j in range(3):
                _fill_copy(ins, send_sems, recv_sems, shapes, a, j, c, x, y, c).start()
        token[...] = jnp.zeros_like(token)

    return pl.pallas_call(
        body, name=name,
        out_shape=(pltpu.SemaphoreType.DMA((3 * n,)), pltpu.SemaphoreType.DMA((3 * n,)),
                   *[pltpu.HBM(g.shape, g.dtype) for g in gathered], jax.ShapeDtypeStruct((8, 128), F32)),
        in_specs=[_HBM] * n,
        out_specs=(_SEM, _SEM, *[_HBM] * n, pl.BlockSpec(memory_space=pltpu.VMEM)),
        input_output_aliases={a: 2 + a for a in range(n)},
        compiler_params=pltpu.CompilerParams(has_side_effects=_DATAFLOW),
    )(*[_in_hbm(g) for g in gathered])


def _sibling_fill_wait(started, after, name):
    send_sems, recv_sems, *thru = started
    n = len(thru)
    shapes = [t.shape[1:] for t in thru]

    def body(*refs):
        ins = refs[:n]
        send_sems, recv_sems = refs[n], refs[n + 1]
        x, y, c = _pos()
        for a in range(n):
            for j in range(3):
                _fill_copy(ins, send_sems, recv_sems, shapes, a, j, c, x, y, c).wait_send()
                _fill_copy(ins, send_sems, recv_sems, shapes, a, j, 1 - c, x, y, c).wait_recv()

    return pl.pallas_call(
        body, name=name, out_shape=[pltpu.HBM(t.shape, t.dtype) for t in thru],
        in_specs=[_HBM] * n + [_SEM, _SEM] + [_ANY] * len(after), out_specs=[_HBM] * n,
        input_output_aliases={a: a for a in range(n)},
        compiler_params=pltpu.CompilerParams(has_side_effects=_DATAFLOW),
    )(*thru, send_sems, recv_sems, *after)


def _place_own(shards, gathered, cq, name, carry=()):
    n = len(shards)
    nc = len(carry)
    steps = 4

    def body(cq_ref, *refs):
        for a in range(n):
            refs[2 * n + nc + a][...] = refs[a][...]

    def tile(shape):
        return shape[0] // steps if _halvable(shape) else shape[0]

    in_specs = [pl.BlockSpec((tile(s.shape), s.shape[1]), (lambda i, s_: (i, 0)) if _halvable(s.shape) else (lambda i, s_: (0, 0)))
                for s in shards]
    in_specs += [pl.BlockSpec(memory_space=pl.ANY)] * (n + nc)
    out_specs = [pl.BlockSpec((None, tile(s.shape), s.shape[1]),
                              (lambda i, s_: (s_[1], i, 0)) if _halvable(s.shape) else (lambda i, s_: (s_[1], 0, 0)))
                 for s in shards]
    out_specs += [pl.BlockSpec(memory_space=pl.ANY)] * nc
    gs = pltpu.PrefetchScalarGridSpec(num_scalar_prefetch=1, grid=(steps,), in_specs=in_specs, out_specs=out_specs)
    outs = pl.pallas_call(
        body, name=name, grid_spec=gs,
        out_shape=[jax.ShapeDtypeStruct(g.shape, g.dtype) for g in gathered] + [jax.ShapeDtypeStruct(t.shape, t.dtype) for t in carry],
        input_output_aliases={1 + n + a: a for a in range(n + nc)},
        compiler_params=_cparams(("arbitrary",)),
    )(cq, *shards, *gathered, *carry)
    return (outs[:n], outs[n:]) if nc else outs


def _half_rows(ref, c, rh):
    return ref.at[:, pl.ds(pl.multiple_of(c * rh, 8), rh), :]


def _chips_copy(src_refs, land_refs, send_sems, recv_sems, a, j, x, y, c):
    px, py = _other_chips(x, y)[j]
    return pltpu.make_async_remote_copy(src_ref=src_refs[a].at[2 * px + py], dst_ref=land_refs[a].at[j],
                                        send_sem=send_sems.at[3 * a + j], recv_sem=recv_sems.at[3 * a + j],
                                        device_id=(px, py, c), device_id_type=MESH)


def _peer(r, x, y, c):
    return (x if r & 4 == 0 else 1 - x), (y if r & 2 == 0 else 1 - y), (c if r & 1 == 0 else 1 - c)


def _small_copy(small_ref, all_ref, send_sems, recv_sems, base, r, slot, x, y, c):
    return pltpu.make_async_remote_copy(src_ref=small_ref, dst_ref=all_ref.at[slot], send_sem=send_sems.at[base + r - 1],
                                        recv_sem=recv_sems.at[base + r - 1], device_id=_peer(r, x, y, c), device_id_type=MESH)


def _grad_chips_start(parts, name, small=None):
    n = len(parts)
    srcs = list(parts) + ([] if small is None else [small])
    m = len(srcs)

    def body(*refs):
        ins, lands = refs[:m], refs[m:2 * m]
        send_sems, recv_sems = refs[2 * m], refs[2 * m + 1]
        token = refs[-1]
        x, y, c = _pos()
        for a in range(n):
            for j in range(3):
                _chips_copy(ins, lands, send_sems, recv_sems, a, j, x, y, c).start()
        if small is not None:
            for r in range(1, 8):
                _small_copy(ins[n], lands[n], send_sems, recv_sems, 3 * n, r, 4 * x + 2 * y + c, x, y, c).start()
        token[...] = jnp.zeros_like(token)

    land_shapes = [(3,) + p.shape[1:] for p in parts] + ([] if small is None else [(8,) + small.shape])
    nsem = 3 * n + (0 if small is None else 7)
    return pl.pallas_call(
        body, name=name,
        out_shape=(pltpu.SemaphoreType.DMA((nsem,)), pltpu.SemaphoreType.DMA((nsem,)),
                   *[pltpu.HBM(p.shape, p.dtype) for p in srcs],
                   *[pltpu.HBM(ls, p.dtype) for ls, p in zip(land_shapes, srcs)],
                   jax.ShapeDtypeStruct((8, 128), F32)),
        in_specs=[_HBM] * (2 * m),
        out_specs=(_SEM, _SEM, *[_HBM] * (2 * m), pl.BlockSpec(memory_space=pltpu.VMEM)),
        input_output_aliases={a: 2 + a for a in range(2 * m)},
        compiler_params=pltpu.CompilerParams(has_side_effects=_DATAFLOW),
    )(*[_in_hbm(p) for p in srcs], *[_in_hbm(lax.empty(ls, p.dtype)) for ls, p in zip(land_shapes, srcs)])


def _grad_chips_wait(started, after, name, with_small=False):
    send_sems, recv_sems, *thru = started
    m = len(thru) // 2
    n = m - (1 if with_small else 0)

    def body(*refs):
        ins, lands = refs[:m], refs[m:2 * m]
        send_sems, recv_sems = refs[2 * m], refs[2 * m + 1]
        x, y, c = _pos()
        for a in range(n):
            for j in range(3):
                cp = _chips_copy(ins, lands, send_sems, recv_sems, a, j, x, y, c)
                cp.wait_send()
                cp.wait_recv()
        if with_small:
            for r in range(1, 8):
                px, py, pc = _peer(r, x, y, c)
                _small_copy(ins[n], lands[n], send_sems, recv_sems, 3 * n, r, 4 * x + 2 * y + c, x, y, c).wait_send()
                _small_copy(ins[n], lands[n], send_sems, recv_sems, 3 * n, r, 4 * px + 2 * py + pc, x, y, c).wait_recv()

    outs = pl.pallas_call(
        body, name=name, out_shape=[pltpu.HBM(t.shape, t.dtype) for t in thru],
        in_specs=[_HBM] * (2 * m) + [_SEM, _SEM] + [_ANY] * len(after), out_specs=[_HBM] * (2 * m),
        input_output_aliases={a: a for a in range(2 * m)},
        compiler_params=pltpu.CompilerParams(has_side_effects=_DATAFLOW),
    )(*thru, send_sems, recv_sems, *after)
    return list(outs[m:]) + list(outs[n:m])


def _sibling_copy(src_refs, land_refs, send_sems, recv_sems, rhs, a, c, x, y):
    return pltpu.make_async_remote_copy(src_ref=_half_rows(src_refs[a], 1 - c, rhs[a]), dst_ref=land_refs[a],
                                        send_sem=send_sems.at[a], recv_sem=recv_sems.at[a],
                                        device_id=(x, y, 1 - c), device_id_type=MESH)


def _grad_sibling_start(fams, name):
    n = len(fams)
    rhs = [f.shape[1] // 2 for f in fams]

    def body(*refs):
        ins, lands = refs[:n], refs[n:2 * n]
        send_sems, recv_sems = refs[2 * n], refs[2 * n + 1]
        token = refs[-1]
        x, y, c = _pos()
        for a in range(n):
            _sibling_copy(ins, lands, send_sems, recv_sems, rhs, a, c, x, y).start()
        token[...] = jnp.zeros_like(token)

    land_shapes = [(f.shape[0], f.shape[1] // 2, f.shape[2]) for f in fams]
    return pl.pallas_call(
        body, name=name,
        out_shape=(pltpu.SemaphoreType.DMA((n,)), pltpu.SemaphoreType.DMA((n,)),
                   *[pltpu.HBM(f.shape, f.dtype) for f in fams],
                   *[pltpu.HBM(ls, f.dtype) for ls, f in zip(land_shapes, fams)],
                   jax.ShapeDtypeStruct((8, 128), F32)),
        in_specs=[_HBM] * (2 * n),
        out_specs=(_SEM, _SEM, *[_HBM] * (2 * n), pl.BlockSpec(memory_space=pltpu.VMEM)),
        input_output_aliases={a: 2 + a for a in range(2 * n)},
        compiler_params=pltpu.CompilerParams(has_side_effects=_DATAFLOW),
    )(*[_in_hbm(f) for f in fams], *[_in_hbm(lax.empty(ls, f.dtype)) for ls, f in zip(land_shapes, fams)])


def _grad_sibling_wait(started, after, name):
    send_sems, recv_sems, *thru = started
    n = len(thru) // 2
    rhs = [t.shape[1] // 2 for t in thru[:n]]

    def body(*refs):
        ins, lands = refs[:n], refs[n:2 * n]
        send_sems, recv_sems = refs[2 * n], refs[2 * n + 1]
        x, y, c = _pos()
        for a in range(n):
            cp = _sibling_copy(ins, lands, send_sems, recv_sems, rhs, a, c, x, y)
            cp.wait_send()
            cp.wait_recv()

    outs = pl.pallas_call(
        body, name=name, out_shape=[pltpu.HBM(t.shape, t.dtype) for t in thru],
        in_specs=[_HBM] * (2 * n) + [_SEM, _SEM] + [_ANY] * len(after), out_specs=[_HBM] * (2 * n),
        input_output_aliases={a: a for a in range(2 * n)},
        compiler_params=pltpu.CompilerParams(has_side_effects=_DATAFLOW),
    )(*thru, send_sems, recv_sems, *after)
    return outs[:n], outs[n:]


def _grad_share(fulls, name):
    n = len(fulls)
    rhs = [f.shape[0] // 2 for f in fulls]

    def body(*refs):
        ins, outs = refs[:n], refs[n:2 * n]
        send_sems, recv_sems = refs[2 * n], refs[2 * n + 1]
        x, y, c = _pos()

        def copy(a, half):
            rows = pl.ds(pl.multiple_of(half * rhs[a], 8), rhs[a])
            return pltpu.make_async_remote_copy(src_ref=ins[a].at[rows, :], dst_ref=outs[a].at[rows, :],
                                                send_sem=send_sems.at[a], recv_sem=recv_sems.at[a],
                                                device_id=(x, y, 1 - c), device_id_type=MESH)

        sends = [copy(a, c) for a in range(n)]
        for cp in sends:
            cp.start()
        for a in range(n):
            copy(a, 1 - c).wait_recv()
        for cp in sends:
            cp.wait_send()

    return pl.pallas_call(
        body, name=name, in_specs=[_HBM] * n, out_specs=[_HBM] * n,
        out_shape=[jax.ShapeDtypeStruct(f.shape, f.dtype) for f in fulls],
        input_output_aliases={a: a for a in range(n)},
        scratch_shapes=[pltpu.SemaphoreType.DMA((n,)), pltpu.SemaphoreType.DMA((n,))],
    )(*fulls)


def _add_sibling(own, recv, cq, name):
    nb, R, Cc = own.shape
    Rh = R // 2

    def body(cq_ref, a_ref, b_ref, o32_ref, o16_ref):
        s = a_ref[0] + b_ref[0]
        mine = pl.program_id(0) == cq_ref[1]

        @pl.when(mine)
        def _():
            o32_ref[...] = s

        @pl.when(jnp.logical_not(mine))
        def _():
            o16_ref[0] = s.astype(o16_ref.dtype)

    sp = pl.BlockSpec((1, Rh, Cc), lambda b, s: (b, 0, 0))
    gs = pltpu.PrefetchScalarGridSpec(
        num_scalar_prefetch=1, grid=(nb,),
        in_specs=[pl.BlockSpec((1, Rh, Cc), lambda b, s: (b, s[0], 0)), sp],
        out_specs=[pl.BlockSpec((Rh, Cc), lambda b, s: (0, 0)), sp])
    return pl.pallas_call(
        body, name=name, grid_spec=gs,
        out_shape=[jax.ShapeDtypeStruct((Rh, Cc), F32), jax.ShapeDtypeStruct((nb, Rh, Cc), _MXU)],
        compiler_params=_cparams(("arbitrary",)),
    )(cq, own, recv)


def _add_sibling_split(d_wp, recv, cq, name):
    _, Dm, Pc = d_wp.shape
    Rh = Dm // 2
    Wb = IN_COLS // N_CHIPS
    T = 256

    def body(cq_ref, a_ref, b_ref, o32_ref, o16_ref):
        s = a_ref[0] + b_ref[0]
        blocks = [s[:, 0:Wb], s[:, Wb:2 * Wb],
                  jnp.concatenate([s[:, 2 * Wb:P_QKVB], s[:, P_BA:P_BA + 8], s[:, P_QKVB:3 * Wb - 8]], axis=1),
                  s[:, 3 * Wb - 8:P_BA]]
        q = cq_ref[1]
        own = None
        for j, blk in enumerate(blocks):
            term = jnp.where(q == j, blk, 0.0)
            own = term if own is None else own + term
            o16_ref[j] = blk.astype(o16_ref.dtype)
        o32_ref[...] = own

    gs = pltpu.PrefetchScalarGridSpec(
        num_scalar_prefetch=1, grid=(Rh // T,),
        in_specs=[pl.BlockSpec((1, T, Pc), lambda i, s: (0, s[0] * (Rh // T) + i, 0)), pl.BlockSpec((1, T, Pc), lambda i, s: (0, i, 0))],
        out_specs=[pl.BlockSpec((T, Wb), lambda i, s: (i, 0)), pl.BlockSpec((N_CHIPS, T, Wb), lambda i, s: (0, i, 0))])
    return pl.pallas_call(
        body, name=name, grid_spec=gs,
        out_shape=[jax.ShapeDtypeStruct((Rh, Wb), F32), jax.ShapeDtypeStruct((N_CHIPS, Rh, Wb), _MXU)],
        compiler_params=_cparams(("parallel",)),
    )(cq, d_wp, recv)


def _add_chips(part32, recv3, cq, name):
    Rh, Cc = part32.shape

    def body(cq_ref, a_ref, b_ref, o_ref):
        acc = a_ref[...]
        for j in range(3):
            acc = acc + b_ref[j].astype(F32)
        o_ref[...] = acc

    gs = pltpu.PrefetchScalarGridSpec(
        num_scalar_prefetch=1, grid=(1,),
        in_specs=[pl.BlockSpec((Rh, Cc), lambda i, s: (0, 0)), pl.BlockSpec((3, Rh, Cc), lambda i, s: (0, 0, 0))],
        out_specs=pl.BlockSpec((Rh, Cc), lambda i, s: (s[0], 0)))
    return pl.pallas_call(
        body, name=name, grid_spec=gs, out_shape=jax.ShapeDtypeStruct((2 * Rh, Cc), F32),
        compiler_params=_cparams(("arbitrary",)),
    )(cq, part32, recv3)


def _transposed(g):
    Dm, n = g.shape
    pad = -n % 128

    def body(g_ref, o_ref):
        xp = jnp.concatenate([g_ref[...], jnp.zeros((Dm, pad), F32)], axis=1)
        o_ref[...] = xp.T[:n, :]

    return pl.pallas_call(body, name="transposed", out_shape=jax.ShapeDtypeStruct((n, Dm), F32),
                          compiler_params=_cparams(vmem=V7X_VMEM_LIMIT))(g)


def _adamw(w, g, m, v, name):
    R, Cc = w.shape
    T = max([t for t in range(8, 257, 8) if R % t == 0], default=R)

    def body(w_ref, g_ref, m_ref, v_ref, d_ref, mo_ref, vo_ref):
        d_ref[...], mo_ref[...], vo_ref[...] = _adamw_math(w_ref[...], g_ref[...], m_ref[...], v_ref[...])

    sp = pl.BlockSpec((T, Cc), lambda i: (i, 0))
    sh = jax.ShapeDtypeStruct((R, Cc), F32)
    return pl.pallas_call(
        body, name=name, grid=(R // T,), in_specs=[sp] * 4, out_specs=(sp, sp, sp), out_shape=(sh, sh, sh),
        compiler_params=_cparams(("parallel",)),
    )(w, g, m, v)


SMALL_ROWS = 32
ROW_CONV, ROW_FCG, ROW_FCU = 5, 13, 22


def _adamw_math(w, g, m, v):
    mn = ADAM_B1 * m + (1.0 - ADAM_B1) * g
    vn = ADAM_B2 * v + (1.0 - ADAM_B2) * (g * g)
    c1 = 1.0 / (1.0 - ADAM_B1 ** ADAM_STEP)
    c2 = 1.0 / (1.0 - ADAM_B2 ** ADAM_STEP)
    return -ADAM_LR * ((mn * c1) / (jnp.sqrt(vn * c2) + ADAM_EPS) + ADAM_WD * w), mn, vn


def _pack_small(n1, n2, fn, gp, gn, conv, fcg, fcu, loss):
    W = D_MODEL

    def body(n1_ref, n2_ref, fn_ref, gp_ref, gn_ref, conv_ref, fcg_ref, fcu_ref, loss_ref, o_ref):
        o_ref[...] = jnp.zeros_like(o_ref)
        o_ref[0:1, :] = n1_ref[...]
        o_ref[1:2, :] = n2_ref[...]
        o_ref[2:3, :] = fn_ref[...]
        o_ref[3:4, 0:8] = gp_ref[0:1, 0:8]
        o_ref[3:4, 8:9] = loss_ref[0:1, 0:1]
        o_ref[4:5, 0:128] = gn_ref[...]
        for i in range(GDN_CONV):
            o_ref[ROW_CONV + 2 * i:ROW_CONV + 2 * i + 1, :] = conv_ref[i:i + 1, 0:W]
            o_ref[ROW_CONV + 2 * i + 1:ROW_CONV + 2 * i + 2, 0:3 * GDN_WIDTH - W] = conv_ref[i:i + 1, W:3 * GDN_WIDTH]
        for r0, ref in ((ROW_FCG, fcg_ref), (ROW_FCU, fcu_ref)):
            for i in range(FFN_CONV):
                for k in range(3):
                    n = min(W, D_FF - k * W)
                    o_ref[r0 + 3 * i + k:r0 + 3 * i + k + 1, 0:n] = ref[i:i + 1, k * W:k * W + n]

    return pl.pallas_call(body, name="pack_small", out_shape=jax.ShapeDtypeStruct((SMALL_ROWS, W), F32))(
        n1, n2, fn, gp, gn, conv, fcg, fcu, loss)


def _small_step(meq, small_all, small, ws, ms, vs):
    W = D_MODEL
    n = len(ws)
    cw, fw = ws[6].shape[1], ws[7].shape[1]

    def body(meq_ref, all_ref, own_ref, *refs):
        w_refs, m_refs, v_refs = refs[:n], refs[n:2 * n], refs[2 * n:3 * n]
        loss_ref = refs[3 * n]
        outs = refs[3 * n + 1:]
        me, q = meq_ref[0], meq_ref[1]
        red = None
        for d in range(8):
            term = jnp.where(me == d, own_ref[...], all_ref[d])
            red = term if red is None else red + term
        loss_ref[...] = jnp.broadcast_to(red[3:4, 8:9], loss_ref.shape)
        conv = [jnp.concatenate([red[ROW_CONV + 2 * i:ROW_CONV + 2 * i + 1, :],
                                 red[ROW_CONV + 2 * i + 1:ROW_CONV + 2 * i + 2, 0:3 * GDN_WIDTH - W]], axis=1)
                for i in range(GDN_CONV)]
        conv = jnp.concatenate(conv, axis=0)

        def fc_rows(r0):
            rows = [jnp.concatenate([red[r0 + 3 * i + k:r0 + 3 * i + k + 1, 0:min(W, D_FF - k * W)] for k in range(3)], axis=1)
                    for i in range(FFN_CONV)]
            return jnp.concatenate(rows, axis=0)

        fc = jnp.concatenate([fc_rows(ROW_FCG), fc_rows(ROW_FCU)], axis=1)

        def chip_block(full, width):
            out = None
            for j in range(N_CHIPS):
                term = jnp.where(q == j, full[:, width * j:width * (j + 1)], 0.0)
                out = term if out is None else out + term
            return out

        grads = [red[0:1, :], red[1:2, :], red[2:3, :], red[3:4, 0:4], red[3:4, 4:8], red[4:5, 0:128],
                 chip_block(conv, cw), chip_block(fc, fw)]
        for k in range(n):
            d_, m_, v_ = _adamw_math(w_refs[k][...], grads[k], m_refs[k][...], v_refs[k][...])
            outs[4 * k][...] = grads[k]
            outs[4 * k + 1][...] = d_
            outs[4 * k + 2][...] = m_
            outs[4 * k + 3][...] = v_

    full = lambda a: pl.BlockSpec(a.shape, lambda i, s_, nd=len(a.shape): (0,) * nd)
    arrays = [small_all, small, *ws, *ms, *vs]
    out_shapes = [jax.ShapeDtypeStruct((8, 128), F32)] + [jax.ShapeDtypeStruct(w.shape, F32) for w in ws for _ in range(4)]
    gs = pltpu.PrefetchScalarGridSpec(
        num_scalar_prefetch=1, grid=(1,), in_specs=[full(a) for a in arrays],
        out_specs=[pl.BlockSpec(o.shape, lambda i, s_, nd=len(o.shape): (0,) * nd) for o in out_shapes])
    return pl.pallas_call(body, name="small_step", grid_spec=gs, out_shape=out_shapes)(meq, *arrays)


def _pad_lanes(v, n=D_MODEL):
    return jnp.pad(v, ((0, 0), (0, n - v.shape[1])))


def kernel(x, norm1_w, w_in, conv_qkv_w, a_log, dt_bias, gdn_norm_w, w_out, norm2_w, w_up, ffn_conv_w, w_down, final_norm_w, loss_target, m_norm1_w, m_w_in, m_conv_qkv_w, m_a_log, m_dt_bias, m_gdn_norm_w, m_w_out, m_norm2_w, m_w_up, m_ffn_conv_w, m_w_down, m_final_norm_w, v_norm1_w, v_w_in, v_conv_qkv_w, v_a_log, v_dt_bias, v_gdn_norm_w, v_w_out, v_norm2_w, v_w_up, v_ffn_conv_w, v_w_down, v_final_norm_w):
    c = lax.axis_index("c")
    q = 2 * lax.axis_index("x") + lax.axis_index("y")
    S = x.shape[1]
    cq = jnp.stack([c, q]).astype(jnp.int32)

    *in_started, in_token = _gather_halves_start([w_in[0].astype(_MXU), conv_qkv_w[0], ffn_conv_w[0]], x, "gather_in_start")
    w_in_l, m_w_in_l, v_w_in_l = (jnp.swapaxes(a + in_token[0:1, 0:1], 1, 2)[0] for a in (w_in, m_w_in, v_w_in))
    h1 = _rmsnorm_fwd(x[0], norm1_w, "norm1", after=[in_token])
    rest = [(a[0] + in_token[0:1, 0:1]).astype(_MXU) for a in (w_out, w_up, w_down)]
    in_shards, got_in = _gather_halves_wait(in_started, [w_in_l, m_w_in_l, v_w_in_l, h1, *rest], "gather_in_wait")
    (g_in, g_conv, g_fconv), (w_in_l, m_w_in_l, v_w_in_l) = _place_own(
        in_shards, _sibling_fill(got_in, "fill_in"), cq, "place_in", carry=[w_in_l, m_w_in_l, v_w_in_l])
    *rest_started, token = _gather_halves_start(rest, g_conv, "gather_rest_start")

    rest_state = {}

    def rest_arrived(after):
        rest_state["shards"], got = _gather_halves_wait(rest_started, after, "gather_rest_wait")
        *rest_state["fill"], tok = _sibling_fill_start(got, "fill_rest_start")
        return tok

    def rest_filled(after):
        got = _sibling_fill_wait(rest_state["fill"], after, "fill_rest_wait")
        g_out, g_up, g_down = _place_own(rest_state["shards"], got, cq, "place_rest")
        return g_out.reshape(D_MODEL, D_MODEL), g_up, g_down.reshape(D_FF, D_MODEL)

    rest_weights = (rest_arrived, rest_filled)
    wp = _wp_assemble(g_in, [token])
    conv_f = jnp.concatenate([g_conv[i] for i in range(N_CHIPS)], axis=1)
    fcw = jnp.concatenate([g_fconv[i] for i in range(N_CHIPS)], axis=1)
    gp = _pad_lanes(jnp.concatenate([a_log, dt_bias], axis=1), 128)
    fnw = final_norm_w[None, :]
    early = {}

    early_names = ("w_up", "w_down", "w_out")

    def early_sibling(d_wup, d_wdown, d_wout):
        *early["sibling"], tok = _grad_sibling_start(
            [d_wup, d_wdown.reshape(N_CHIPS, D_FF // N_CHIPS, D_MODEL), d_wout.reshape(N_CHIPS, D_MODEL // N_CHIPS, D_MODEL)],
            "grad_sibling_early_start")
        return tok

    def early_chips(after):
        fams_e, got_e = _grad_sibling_wait(early["sibling"], after, "grad_sibling_early_wait")
        early["parts"] = [_add_sibling(f, r, cq, "add_sibling_" + nm) for f, r, nm in zip(fams_e, got_e, early_names)]
        *early["started"], tok = _grad_chips_start([p[1] for p in early["parts"]], "grad_chips_start")
        return tok

    def late_sibling(d_wp):
        *early["late_sibling"], tok = _grad_sibling_start([d_wp[None]], "grad_sibling_late_start")
        return tok

    loss_l, dx, g = _local_step(x[0], loss_target[0], h1, norm1_w, norm2_w, fnw, gp, gdn_norm_w, wp,
                                conv_f, fcw, rest_weights, (early_sibling, early_chips, late_sibling))
    fams, got = _grad_sibling_wait(early["late_sibling"], [dx], "grad_sibling_late_wait")
    late_part = _add_sibling_split(fams[0], got[0], cq, "add_sibling_w_in")
    *late_started, late_token = _grad_chips_start([late_part[1]], "grad_chips_late_start")
    small = _pack_small(g["n1w"], g["n2w"], g["fnw"], g["gp"], g["gnw"], g["conv_w"], g["fcw_g"], g["fcw_u"], loss_l)
    *small_started, small_token = _grad_chips_start([], "small_gather_start", small)
    got3_e = _grad_chips_wait(early["started"], [late_token, small_token], "grad_chips_wait")
    g_w_up, g_w_down, g_w_out = _grad_share(
        [_add_chips(p[0], r3, cq, "add_chips_" + nm) for p, r3, nm in zip(early["parts"], got3_e, early_names)],
        "grad_share_early")
    big = {}

    def adamw_big(nm, w, gg, m, v):
        d_, m_, v_ = _adamw(w[0], gg, m[0], v[0], "adamw_" + nm)
        big[nm] = (gg[None], d_[None], m_[None], v_[None])

    adamw_big("w_up", w_up, g_w_up, m_w_up, v_w_up)
    adamw_big("w_down", w_down, g_w_down, m_w_down, v_w_down)
    adamw_big("w_out", w_out, g_w_out, m_w_out, v_w_out)
    got3, = _grad_chips_wait(late_started, [big[nm][1] for nm in early_names], "grad_chips_late_wait")
    g_w_in, = _grad_share([_add_chips(late_part[0], got3, cq, "add_chips_w_in")], "grad_share_late")
    g_t = _transposed(g_w_in)
    d_t, m_t, v_t = _adamw(w_in_l, g_t, m_w_in_l, v_w_in_l, "adamw_w_in")
    big["w_in"] = tuple(jnp.swapaxes(t[None], 1, 2) for t in (g_t, d_t, m_t, v_t))
    small_all, small = _grad_chips_wait(small_started, [d_t], "small_gather_wait", with_small=True)
    small_names = ["norm1_w", "norm2_w", "final_norm_w", "a_log", "dt_bias", "gdn_norm_w", "conv_qkv_w", "ffn_conv_w"]
    loss_b, *small_out = _small_step(
        jnp.stack([2 * q + c, q]).astype(jnp.int32), small_all, small,
        [norm1_w, norm2_w, final_norm_w[None], a_log, dt_bias, gdn_norm_w, conv_qkv_w[0], ffn_conv_w[0]],
        [m_norm1_w, m_norm2_w, m_final_norm_w[None], m_a_log, m_dt_bias, m_gdn_norm_w, m_conv_qkv_w[0], m_ffn_conv_w[0]],
        [v_norm1_w, v_norm2_w, v_final_norm_w[None], v_a_log, v_dt_bias, v_gdn_norm_w, v_conv_qkv_w[0], v_ffn_conv_w[0]])
    like = dict(final_norm_w=lambda t: t[0], conv_qkv_w=lambda t: t[None], ffn_conv_w=lambda t: t[None])
    for k, nm in enumerate(small_names):
        big[nm] = tuple(like.get(nm, lambda t: t)(t) for t in small_out[4 * k:4 * k + 4])
    names = ["norm1_w", "w_in", "conv_qkv_w", "a_log", "dt_bias", "gdn_norm_w", "w_out", "norm2_w", "w_up",
             "ffn_conv_w", "w_down", "final_norm_w"]
    return (loss_b[0, 0], dx[None], *[big[n][0] for n in names], *[big[n][1] for n in names],
            *[big[n][2] for n in names], *[big[n][3] for n in names])
```

```python
import functools
import math

import numpy as np
import jax
import jax.numpy as jnp
from jax import lax
from jax.experimental import pallas as pl
from jax.experimental.pallas import tpu as pltpu

F32 = jnp.float32
BF16 = jnp.bfloat16
_MXU = jnp.bfloat16
_HI = lax.Precision.HIGHEST
EPS = 1e-6
V7X_VMEM_LIMIT = 56 * 1024 * 1024
MESH = pl.DeviceIdType.MESH

D_MODEL = 1024
GDN_HEADS, GDN_DIM, GDN_CHUNK, GDN_CONV = 4, 128, 64, 4
GDN_WIDTH = GDN_HEADS * GDN_DIM
DIL_HEADS, DIL_DIM = 8, 64
DIL_WIDTH = DIL_HEADS * DIL_DIM
D_FF, FFN_CONV = 2816, 3
IN_COLS = 3592
P_COLS = 3840
P_Z, P_QKVB, P_BA = 1536, 2048, 3584
ATT_T = 1024
ADAM_LR, ADAM_B1, ADAM_B2, ADAM_EPS, ADAM_WD, ADAM_STEP = 0.001, 0.9, 0.999, 1e-08, 0.01, 10
N_CHIPS = 4


def _cparams(sem=None, vmem=None):
    kw = {}
    if sem is not None:
        kw["dimension_semantics"] = sem
    if vmem is not None:
        kw["vmem_limit_bytes"] = vmem
    return pltpu.CompilerParams(**kw)


def _silu(x):
    return x * jax.nn.sigmoid(x)


def _pick_tile(n, cap):
    best = None
    for t in range(128, min(n, cap) + 1, 128):
        if n % t == 0:
            best = t
    return best or n


def _mm(a, b, mode, *, out_dtype=F32, residual=None, name, b_blocks=False, place=None, into=None, tn=None, after=()):
    if mode == "nn":
        M, K = a.shape
        N = b.shape[0] * b.shape[2] if b_blocks else b.shape[1]
    elif mode == "nt":
        (M, K), (N, _) = a.shape, b.shape
    else:
        (K, M), (_, N) = a.shape, b.shape
    tm = _pick_tile(M, 1024)
    tn = b.shape[2] if b_blocks else (tn or _pick_tile(N, 1536))

    def vmem(tm, tn):
        return 2 * (tm * K * a.dtype.itemsize + tn * K * b.dtype.itemsize
                    + tm * tn * (jnp.dtype(out_dtype).itemsize + (4 if residual is not None else 0))) + 3 * tm * tn * 4

    fixed_tn = b_blocks or (place is not None and place[0] == "blocks")
    while vmem(tm, tn) > 40 * 1024 * 1024:
        if (tm >= tn or fixed_tn) and tm % 256 == 0:
            tm //= 2
        elif tn % 256 == 0 and not fixed_tn:
            tn //= 2
        else:
            tm //= 2
    a_spec = pl.BlockSpec((K, tm), lambda j, i: (0, i)) if mode == "tn" else pl.BlockSpec((tm, K), lambda j, i: (i, 0))
    if b_blocks:
        b_spec = pl.BlockSpec((None, K, tn), lambda j, i: (j, 0, 0))
    else:
        b_spec = pl.BlockSpec((tn, K), lambda j, i: (j, 0)) if mode == "nt" else pl.BlockSpec((K, tn), lambda j, i: (0, j))
    r_spec = pl.BlockSpec((tm, tn), lambda j, i: (i, j))
    if place is None:
        o_spec, o_shape = r_spec, (M, N)
    elif place[0] == "rows":
        off = place[2] // tm
        o_spec, o_shape = pl.BlockSpec((tm, tn), lambda j, i: (i + off, j)), (place[1], N)
    else:
        off = place[2]
        o_spec, o_shape = pl.BlockSpec((None, tm, tn), lambda j, i: (j + off, i, 0)), (place[1], M, tn)
    dims = {"nn": (((1,), (0,)), ((), ())), "nt": (((1,), (1,)), ((), ())), "tn": (((0,), (0,)), ((), ()))}[mode]

    def body(*refs):
        a_ref, b_ref = refs[0], refs[1]
        o_ref = refs[-1]
        acc = lax.dot_general(a_ref[...].astype(_MXU), b_ref[...].astype(_MXU), dims, preferred_element_type=F32)
        if residual is not None:
            acc = acc + refs[2][...]
        o_ref[...] = acc.astype(out_dtype)

    ins, specs, alias = [a, b], [a_spec, b_spec], {}
    if residual is not None:
        ins.append(residual)
        specs.append(r_spec)
    if into is not None:
        alias = {len(ins): 0}
        ins.append(into)
        specs.append(pl.BlockSpec(memory_space=pl.ANY))
    ins += list(after)
    specs += [pl.BlockSpec(memory_space=pl.ANY)] * len(after)
    return pl.pallas_call(
        body, name=name, grid=(N // tn, M // tm), in_specs=specs, out_specs=o_spec,
        out_shape=jax.ShapeDtypeStruct(o_shape, out_dtype), input_output_aliases=alias,
        compiler_params=_cparams(("parallel", "parallel"), V7X_VMEM_LIMIT),
    )(*ins)


def _wp_assemble(g_in, after=()):
    nb, Dm, Wb = g_in.shape
    T = 256
    n_lo = P_QKVB - 2 * Wb

    def body(g_ref, *rest):
        g2 = g_ref[2]
        rest[-1][...] = jnp.concatenate(
            [g_ref[0], g_ref[1], g2[:, :n_lo], g2[:, n_lo + 8:], g_ref[3], g2[:, n_lo:n_lo + 8],
             jnp.zeros((T, P_COLS - P_BA - 8), g_in.dtype)], axis=1)

    return pl.pallas_call(
        body, name="wp_assemble", grid=(Dm // T,),
        in_specs=[pl.BlockSpec((nb, T, Wb), lambda i: (0, i, 0))] + [pl.BlockSpec(memory_space=pl.ANY)] * len(after),
        out_specs=pl.BlockSpec((T, P_COLS), lambda i: (i, 0)), out_shape=jax.ShapeDtypeStruct((Dm, P_COLS), g_in.dtype),
        compiler_params=_cparams(("parallel",)),
    )(g_in, *after)


def _rmsnorm_fwd(x, w, name, after=()):
    S, D = x.shape
    T = _pick_tile(S, 512)

    def body(x_ref, w_ref, *rest):
        xv = x_ref[...]
        rs = lax.rsqrt(jnp.mean(xv * xv, axis=-1, keepdims=True) + EPS)
        rest[-1][...] = (xv * rs * w_ref[...]).astype(rest[-1].dtype)

    return pl.pallas_call(
        body, name=name, grid=(S // T,),
        in_specs=[pl.BlockSpec((T, D), lambda i: (i, 0)), pl.BlockSpec((1, D), lambda i: (0, 0))] + [_ANY] * len(after),
        out_specs=pl.BlockSpec((T, D), lambda i: (i, 0)),
        out_shape=jax.ShapeDtypeStruct((S, D), _MXU),
        compiler_params=_cparams(("parallel",)),
    )(x, w, *after)


def _rmsnorm_bwd(dh, x, w, dres, name, after=()):
    S, D = x.shape
    pair = isinstance(dh, tuple)
    T = _pick_tile(S, 256 if pair else 512)
    dhs = [*dh[0], dh[1]] if pair else [dh]

    def body(*refs):
        x_ref, w_ref, dres_ref = refs[len(dhs):len(dhs) + 3]
        dx_ref, dw_ref = refs[-2:]
        xv = x_ref[...]
        rs = lax.rsqrt(jnp.mean(xv * xv, axis=-1, keepdims=True) + EPS)
        xn = xv * rs
        if pair:
            b_ref = refs[len(dhs) - 1]
            nb, _, Kb = b_ref.shape
            per = nb // (len(dhs) - 1)
            dhv = None
            for blk in range(nb):
                lo = (blk % per) * Kb
                t = lax.dot_general(refs[blk // per][:, lo:lo + Kb].astype(_MXU), b_ref[blk].astype(_MXU), (_NT, ((), ())),
                                    preferred_element_type=F32)
                dhv = t if dhv is None else dhv + t
        else:
            dhv = refs[0][...]
        dxn = dhv * w_ref[...]
        dx_ref[...] = dres_ref[...] + rs * (dxn - xn * jnp.mean(dxn * xn, axis=-1, keepdims=True))

        @pl.when(pl.program_id(0) == 0)
        def _():
            dw_ref[...] = jnp.zeros_like(dw_ref)

        dw_ref[...] += jnp.sum(dhv * xn, axis=0, keepdims=True)

    row = pl.BlockSpec((T, D), lambda i: (i, 0))
    vec = pl.BlockSpec((1, D), lambda i: (0, 0))
    dh_specs = [row] if not pair else (
        [pl.BlockSpec((T, a.shape[1]), lambda i: (i, 0)) for a in dh[0]] + [pl.BlockSpec(dh[1].shape, lambda i: (0, 0, 0))])
    return pl.pallas_call(
        body, name=name, grid=(S // T,), in_specs=dh_specs + [row, vec, row] + [_ANY] * len(after), out_specs=(row, vec),
        out_shape=(jax.ShapeDtypeStruct((S, D), F32), jax.ShapeDtypeStruct((1, D), F32)),
        compiler_params=_cparams(("arbitrary",), V7X_VMEM_LIMIT if pair else None),
    )(*dhs, x, w, dres, *after)


def _loss_head(x3, w, tgt, name):
    S, D = x3.shape
    T = _pick_tile(S, 512)

    def body(x_ref, w_ref, t_ref, loss_ref, dx_ref, dxn_ref, dw_ref):
        xv = x_ref[...]
        rs = lax.rsqrt(jnp.mean(xv * xv, axis=-1, keepdims=True) + EPS)
        xn = xv * rs
        err = xn * w_ref[...] - t_ref[...]
        dy = err * (1.0 / D)
        dxn = dy * w_ref[...]
        dxv = rs * (dxn - xn * jnp.mean(dxn * xn, axis=-1, keepdims=True))
        dx_ref[...] = dxv
        dxn_ref[...] = dxv.astype(dxn_ref.dtype)

        @pl.when(pl.program_id(0) == 0)
        def _():
            dw_ref[...] = jnp.zeros_like(dw_ref)
            loss_ref[...] = jnp.zeros_like(loss_ref)

        dw_ref[...] += jnp.sum(dy * xn, axis=0, keepdims=True)
        part = jnp.sum(jnp.sum(err * err, axis=-1, keepdims=True), axis=0, keepdims=True) * (0.5 / D)
        loss_ref[...] += jnp.broadcast_to(part, loss_ref.shape)

    row = pl.BlockSpec((T, D), lambda i: (i, 0))
    vec = pl.BlockSpec((1, D), lambda i: (0, 0))
    return pl.pallas_call(
        body, name=name, grid=(S // T,), in_specs=[row, vec, row],
        out_specs=(pl.BlockSpec((8, 128), lambda i: (0, 0)), row, row, vec),
        out_shape=(jax.ShapeDtypeStruct((8, 128), F32), jax.ShapeDtypeStruct((S, D), F32), jax.ShapeDtypeStruct((S, D), _MXU),
                   jax.ShapeDtypeStruct((1, D), F32)),
        compiler_params=_cparams(("arbitrary",)),
    )(x3, w, tgt)


def _shifted(ext, back, lo, n):
    if back == 0:
        return ext[lo:lo + n, :]
    return pltpu.roll(ext, back % ext.shape[0], 0)[lo:lo + n, :]


def _conv_windows(ext, K, T):
    return [_shifted(ext, (K - 1) - i, 8, T) for i in range(K)]


def _conv_taps(ext, w, K, T):
    out = None
    for i, win in enumerate(_conv_windows(ext, K, T)):
        term = win * w[i:i + 1, :]
        out = term if out is None else out + term
    return out


def _conv_taps_t(ext, w, K, T):
    out = None
    for i in range(K):
        term = _shifted(ext, i - (K - 1), 0, T) * w[i:i + 1, :]
        out = term if out is None else out + term
    return out


def _tri_masks(C):
    r = lax.broadcasted_iota(jnp.int32, (C, C), 0)
    c = lax.broadcasted_iota(jnp.int32, (C, C), 1)
    return r == c, r >= c, r > c, r <= c


_NN, _NT, _TN = ((1,), (0,)), ((1,), (1,)), ((0,), (0,))
_GDN_PASSES = dict(qk=1, inv=1, sol=1, scan=1, bwd=1)


def _bdot_raw(a, b, kind, passes):
    dims = ({"NN": ((2,), (1,)), "NT": ((2,), (2,)), "TN": ((1,), (1,))}[kind], ((0,), (0,)))
    if passes == 0:
        return lax.dot_general(a, b, dims, precision=_HI, preferred_element_type=F32)
    ah, bh = a.astype(BF16), b.astype(BF16)
    out = lax.dot_general(ah, bh, dims, preferred_element_type=F32)
    if passes == 3:
        al, bl = (a - ah.astype(F32)).astype(BF16), (b - bh.astype(F32)).astype(BF16)
        out = out + lax.dot_general(ah, bl, dims, preferred_element_type=F32) + lax.dot_general(al, bh, dims, preferred_element_type=F32)
    return out


@functools.partial(jax.custom_vjp, nondiff_argnums=(2, 3))
def _bdot(a, b, kind, passes):
    return _bdot_raw(a, b, kind, passes)


def _bdot_fwd(a, b, kind, passes):
    return _bdot_raw(a, b, kind, passes), (a, b)


def _bdot_bwd(kind, passes, res, ct):
    a, b = res
    if kind == "NN":
        return _bdot_raw(ct, b, "NT", passes), _bdot_raw(a, ct, "TN", passes)
    if kind == "NT":
        return _bdot_raw(ct, b, "NN", passes), _bdot_raw(ct, a, "TN", passes)
    return _bdot_raw(b, ct, "NT", passes), _bdot_raw(a, ct, "NN", passes)


_bdot.defvjp(_bdot_fwd, _bdot_bwd)


def _softplus(x):
    return jnp.maximum(x, 0.0) + jnp.log(1.0 + jnp.exp(-jnp.abs(x)))


def _gdn_stage1(cq, ck, cv, b_col, a_col, alog, dtb, dot=_bdot_raw):
    C = cq.shape[1]
    eye, incl, strict, incl_t = _tri_masks(C)
    qn = cq * lax.rsqrt(jnp.sum(cq * cq, axis=-1, keepdims=True) + EPS) * (GDN_DIM ** -0.5)
    kn = ck * lax.rsqrt(jnp.sum(ck * ck, axis=-1, keepdims=True) + EPS)
    beta = jax.nn.sigmoid(b_col)
    g = -jnp.exp(alog) * _softplus(a_col + dtb)
    g_row = jnp.sum(jnp.where(eye, g, 0.0), axis=1, keepdims=True)
    beta_row = jnp.sum(jnp.where(eye, beta, 0.0), axis=1, keepdims=True)
    gc_col = jnp.sum(jnp.where(incl, g_row, 0.0), axis=2, keepdims=True)
    gc_row = jnp.sum(jnp.where(incl_t, g, 0.0), axis=1, keepdims=True)
    dec = jnp.where(incl, jnp.exp(jnp.where(incl, gc_col - gc_row, 0.0)), 0.0)
    kk = dot(kn, kn, "NT", _GDN_PASSES["qk"])
    qk = dot(qn, kn, "NT", _GDN_PASSES["qk"])
    lmat = jnp.where(strict, dec * kk * beta_row, 0.0)
    attn = dec * qk * beta_row
    gam = jnp.exp(gc_col)
    gc_last = gc_col[:, C - 1:C, :]
    k_end = kn * (jnp.exp(gc_last - gc_col) * beta)
    return lmat, cv, gam * kn, gam * qn, attn, k_end, jnp.exp(gc_last)


def _tri_inv(lmat):
    C = lmat.shape[1]
    eye = _tri_masks(C)[0]
    ps = _GDN_PASSES["inv"]
    p = jnp.where(eye, 1.0, 0.0) - lmat
    lp = _bdot_raw(lmat, lmat, "NN", ps)
    n = int(math.log2(C))
    for s in range(1, n):
        p = p + _bdot_raw(p, lp, "NN", ps)
        if s < n - 1:
            lp = _bdot_raw(lp, lp, "NN", ps)
    return p


def _gated_norm(o, z, gnw):
    on = o * lax.rsqrt(jnp.mean(o * o, axis=-1, keepdims=True) + EPS) * gnw
    return on * _silu(z)


GDN_PG = 4
GDN_SG = 4


def _gdn_pairs(c, ba, gp, G):
    C, W, H = GDN_CHUNK, GDN_WIDTH, GDN_HEADS
    pairs = [(j, h) for j in range(G) for h in range(H)]
    cq, ck, cv = (jnp.stack([c[C * j:C * (j + 1), o + GDN_DIM * h:o + GDN_DIM * (h + 1)] for j, h in pairs]) for o in (0, W, 2 * W))
    b_col = jnp.stack([ba[C * j:C * (j + 1), h:h + 1] for j, h in pairs])
    a_col = jnp.stack([ba[C * j:C * (j + 1), H + h:H + h + 1] for j, h in pairs])
    alog = jnp.stack([gp[0:1, h:h + 1] for j, h in pairs])
    dtb = jnp.stack([gp[0:1, H + h:H + h + 1] for j, h in pairs])
    return pairs, (cq, ck, cv, b_col, a_col, alog, dtb)


def _gdn_pre_specs(S, G):
    C = GDN_CHUNK
    T = C * G
    return dict(
        cur=pl.BlockSpec((T, 3 * GDN_WIDTH), lambda i: (i, 0)),
        prev=pl.BlockSpec((8, 3 * GDN_WIDTH), lambda i: (jnp.maximum(i * (T // 8) - 1, 0), 0)),
        ba=pl.BlockSpec((T, 128), lambda i: (i, P_BA // 128)),
        cw=pl.BlockSpec((GDN_CONV, 3 * GDN_WIDTH), lambda i: (0, 0)),
        vec=pl.BlockSpec((1, 128), lambda i: (0, 0)),
        hd=pl.BlockSpec((GDN_HEADS, T, GDN_DIM), lambda i: (0, i, 0)),
        hc=pl.BlockSpec((GDN_HEADS, T, C), lambda i: (0, i, 0)),
        ge=pl.BlockSpec((G, GDN_HEADS, 8, 128), lambda i: (i, 0, 0, 0)),
    )


def _hd_shape(S, last=GDN_DIM):
    return jax.ShapeDtypeStruct((GDN_HEADS, S, last), F32)


def _gdn_pre(proj, conv_w, gp):
    S = proj.shape[0]
    C, G = GDN_CHUNK, GDN_PG
    nc = S // C
    sp = _gdn_pre_specs(S, G)

    def body(cur_ref, prev_ref, ba_ref, cw_ref, gp_ref, uv_ref, wk_ref, qd_ref, ke_ref, at_ref, ti_ref, ge_ref):
        prev = prev_ref[...] * jnp.where(pl.program_id(0) == 0, 0.0, 1.0)
        c = _silu(_conv_taps(jnp.concatenate([prev, cur_ref[...]], axis=0), cw_ref[...], GDN_CONV, C * G))
        pairs, args = _gdn_pairs(c, ba_ref[...], gp_ref[...], G)
        lmat, v, rk, q_dec, attn, k_end, g_end = _gdn_stage1(*args)
        t = _tri_inv(lmat)
        u_v = _bdot_raw(t, v, "NN", _GDN_PASSES["sol"])
        w_k = _bdot_raw(t, rk, "NN", _GDN_PASSES["sol"])
        for b, (j, h) in enumerate(pairs):
            rows = slice(C * j, C * (j + 1))
            uv_ref[h, rows, :] = u_v[b]
            wk_ref[h, rows, :] = w_k[b]
            qd_ref[h, rows, :] = q_dec[b]
            ke_ref[h, rows, :] = k_end[b]
            at_ref[h, rows, :] = attn[b]
            ti_ref[h, rows, :] = t[b]
            ge_ref[j, h] = jnp.broadcast_to(g_end[b], (8, 128))

    return pl.pallas_call(
        body, name="gdn_pre", grid=(nc // G,),
        in_specs=[sp["cur"], sp["prev"], sp["ba"], sp["cw"], sp["vec"]],
        out_specs=(sp["hd"], sp["hd"], sp["hd"], sp["hd"], sp["hc"], sp["hc"], sp["ge"]),
        out_shape=(_hd_shape(S), _hd_shape(S), _hd_shape(S), _hd_shape(S), _hd_shape(S, C), _hd_shape(S, C),
                   jax.ShapeDtypeStruct((nc, GDN_HEADS, 8, 128), F32)),
        compiler_params=_cparams(("parallel",)),
    )(proj, proj, proj, conv_w, gp)


def _gdn_scan_specs(S, G, rev):
    C = GDN_CHUNK
    T = C * G
    n = S // T
    ci = (lambda i: n - 1 - i) if rev else (lambda i: i)
    return dict(
        hd=pl.BlockSpec((GDN_HEADS, T, GDN_DIM), lambda i: (0, ci(i), 0)),
        hc=pl.BlockSpec((GDN_HEADS, T, C), lambda i: (0, ci(i), 0)),
        ge=pl.BlockSpec((G, GDN_HEADS, 8, 128), lambda i: (ci(i), 0, 0, 0)),
        z=pl.BlockSpec((T, GDN_WIDTH), lambda i: (ci(i), P_Z // GDN_WIDTH)),
        oa=pl.BlockSpec((T, GDN_WIDTH), lambda i: (ci(i), 0)),
        vec=pl.BlockSpec((1, 128), lambda i: (0, 0)),
        st=pl.BlockSpec((G, GDN_HEADS, GDN_DIM, GDN_DIM), lambda i: (ci(i), 0, 0, 0)),
    )


def _gdn_scan(u_v, w_k, q_dec, k_end, attn, g_end, proj, gnw, mix, after=()):
    S = proj.shape[0]
    C, G = GDN_CHUNK, GDN_SG
    nc = S // C
    sp = _gdn_scan_specs(S, G, False)
    ps = _GDN_PASSES["scan"]

    def body(uv_ref, wk_ref, qd_ref, ke_ref, at_ref, ge_ref, z_ref, gnw_ref, *rest):
        oa_ref, st_ref, s_scr = rest[-3:]

        @pl.when(pl.program_id(0) == 0)
        def _():
            s_scr[...] = jnp.zeros_like(s_scr)

        for j in range(G):
            rows = slice(C * j, C * (j + 1))
            st = s_scr[...]
            st_ref[j] = st
            u = uv_ref[:, rows, :] - _bdot_raw(wk_ref[:, rows, :], st, "NN", ps)
            o = _bdot_raw(qd_ref[:, rows, :], st, "NN", ps) + _bdot_raw(at_ref[:, rows, :], u, "NN", ps)
            s_scr[...] = ge_ref[j][:, 0:1, 0:1] * st + _bdot_raw(ke_ref[:, rows, :], u, "TN", ps)
            for h in range(GDN_HEADS):
                cols = slice(GDN_DIM * h, GDN_DIM * (h + 1))
                oa_ref[rows, cols] = _gated_norm(o[h], z_ref[rows, cols], gnw_ref[...])

    return pl.pallas_call(
        body, name="gdn_scan", grid=(nc // G,),
        in_specs=[sp["hd"], sp["hd"], sp["hd"], sp["hd"], sp["hc"], sp["ge"], sp["z"], sp["vec"]] + [_ANY] * (1 + len(after)),
        out_specs=(sp["oa"], sp["st"]),
        out_shape=(jax.ShapeDtypeStruct(mix.shape, F32),
                   jax.ShapeDtypeStruct((nc, GDN_HEADS, GDN_DIM, GDN_DIM), F32)),
        input_output_aliases={8: 0},
        scratch_shapes=[pltpu.VMEM((GDN_HEADS, GDN_DIM, GDN_DIM), F32)],
        compiler_params=_cparams(("arbitrary",)),
    )(u_v, w_k, q_dec, k_end, attn, g_end, proj, gnw, mix, *after)


def _gdn_scan_bwd(u_v, w_k, q_dec, k_end, attn, g_end, proj, gnw, states, d_oa):
    S = proj.shape[0]
    C, G = GDN_CHUNK, GDN_SG
    nc = S // C
    sp = _gdn_scan_specs(S, G, True)
    ps, pb = _GDN_PASSES["scan"], _GDN_PASSES["bwd"]

    def body(uv_ref, wk_ref, qd_ref, ke_ref, at_ref, ge_ref, z_ref, gnw_ref, st_ref, doa_ref,
             duv_ref, dwk_ref, dqd_ref, dke_ref, dat_ref, dge_ref, dz_ref, dgnw_ref, ds_scr):
        @pl.when(pl.program_id(0) == 0)
        def _():
            ds_scr[...] = jnp.zeros_like(ds_scr)
            dgnw_ref[...] = jnp.zeros_like(dgnw_ref)

        dgnw = jnp.zeros((1, 128), F32)
        for j in reversed(range(G)):
            rows = slice(C * j, C * (j + 1))
            st = st_ref[j]
            wk, qd, ke, at = wk_ref[:, rows, :], qd_ref[:, rows, :], ke_ref[:, rows, :], at_ref[:, rows, :]
            u = uv_ref[:, rows, :] - _bdot_raw(wk, st, "NN", ps)
            o = _bdot_raw(qd, st, "NN", ps) + _bdot_raw(at, u, "NN", ps)
            dos = []
            for h in range(GDN_HEADS):
                cols = slice(GDN_DIM * h, GDN_DIM * (h + 1))
                _, vjp2 = jax.vjp(_gated_norm, o[h], z_ref[rows, cols], gnw_ref[...])
                do_h, dz_h, dgn = vjp2(doa_ref[rows, cols])
                dz_ref[rows, cols] = dz_h
                dgnw = dgnw + dgn
                dos.append(do_h)
            do = jnp.stack(dos)
            ds_new = ds_scr[...]
            du = _bdot_raw(at, do, "TN", pb) + _bdot_raw(ke, ds_new, "NN", pb)
            duv_ref[:, rows, :] = du
            dat_ref[:, rows, :] = _bdot_raw(do, u, "NT", pb)
            dqd_ref[:, rows, :] = _bdot_raw(do, st, "NT", pb)
            dke_ref[:, rows, :] = _bdot_raw(u, ds_new, "NT", pb)
            dwk_ref[:, rows, :] = -_bdot_raw(du, st, "NT", pb)
            d_ge = jnp.sum(jnp.sum(st * ds_new, axis=2, keepdims=True), axis=1, keepdims=True)
            dge_ref[j] = jnp.broadcast_to(d_ge, (GDN_HEADS, 8, 128))
            ds_scr[...] = ge_ref[j][:, 0:1, 0:1] * ds_new + _bdot_raw(qd, do, "TN", pb) - _bdot_raw(wk, du, "TN", pb)
        dgnw_ref[...] += dgnw

    return pl.pallas_call(
        body, name="gdn_scan_bwd", grid=(nc // G,),
        in_specs=[sp["hd"], sp["hd"], sp["hd"], sp["hd"], sp["hc"], sp["ge"], sp["z"], sp["vec"], sp["st"], sp["oa"]],
        out_specs=(sp["hd"], sp["hd"], sp["hd"], sp["hd"], sp["hc"], sp["ge"], sp["oa"], sp["vec"]),
        out_shape=(_hd_shape(S), _hd_shape(S), _hd_shape(S), _hd_shape(S), _hd_shape(S, C),
                   jax.ShapeDtypeStruct((nc, GDN_HEADS, 8, 128), F32), jax.ShapeDtypeStruct((S, GDN_WIDTH), F32),
                   jax.ShapeDtypeStruct((1, 128), F32)),
        scratch_shapes=[pltpu.VMEM((GDN_HEADS, GDN_DIM, GDN_DIM), F32)],
        compiler_params=_cparams(("arbitrary",)),
    )(u_v, w_k, q_dec, k_end, attn, g_end, proj, gnw, states, d_oa)


def _gdn_post(proj, conv_w, gp, tinv, u_v, w_k, d_uv, d_wk, d_qd, d_ke, d_at, d_ge):
    S = proj.shape[0]
    C, G = GDN_CHUNK, GDN_PG
    nc = S // C
    sp = _gdn_pre_specs(S, G)
    pb = _GDN_PASSES["bwd"]

    def body(cur_ref, prev_ref, ba_ref, cw_ref, gp_ref, ti_ref, uv_ref, wk_ref, duv_ref, dwk_ref, dqd_ref, dke_ref,
             dat_ref, dge_ref, dpre_ref, dba_ref, dgp_ref):
        i = pl.program_id(0)

        @pl.when(i == 0)
        def _():
            dgp_ref[...] = jnp.zeros_like(dgp_ref)

        prev = prev_ref[...] * jnp.where(i == 0, 0.0, 1.0)
        pre = _conv_taps(jnp.concatenate([prev, cur_ref[...]], axis=0), cw_ref[...], GDN_CONV, C * G)
        sg = jax.nn.sigmoid(pre)
        dsilu = sg * (1.0 + pre * (1.0 - sg))
        pairs, args = _gdn_pairs(pre * sg, ba_ref[...], gp_ref[...], G)
        _, vjp1 = jax.vjp(functools.partial(_gdn_stage1, dot=_bdot), *args)

        def take(ref):
            return jnp.stack([ref[h, C * j:C * (j + 1), :] for j, h in pairs])

        t, u_v, w_k = take(ti_ref), take(uv_ref), take(wk_ref)
        d_v = _bdot_raw(t, take(duv_ref), "TN", pb)
        d_rk = _bdot_raw(t, take(dwk_ref), "TN", pb)
        d_l = -(_bdot_raw(d_v, u_v, "NT", pb) + _bdot_raw(d_rk, w_k, "NT", pb))
        d_ge = jnp.stack([dge_ref[j, h][0:1, 0:1] for j, h in pairs])
        dcq, dck, dcv, db, da, dalog, ddtb = vjp1((d_l, d_v, d_rk, take(dqd_ref), take(dat_ref), take(dke_ref), d_ge))
        lane = lax.broadcasted_iota(jnp.int32, (C, 128), 1)
        lane1 = lax.broadcasted_iota(jnp.int32, (1, 128), 1)
        dgp = jnp.zeros((1, 128), F32)
        for j in range(G):
            rows = slice(C * j, C * (j + 1))
            dba = jnp.zeros((C, 128), F32)
            for h in range(GDN_HEADS):
                b = GDN_HEADS * j + h
                for o_, dcx in ((0, dcq), (GDN_WIDTH, dck), (2 * GDN_WIDTH, dcv)):
                    cols = slice(o_ + GDN_DIM * h, o_ + GDN_DIM * (h + 1))
                    dpre_ref[rows, cols] = dcx[b] * dsilu[rows, cols]
                dba = dba + jnp.where(lane == h, db[b], 0.0) + jnp.where(lane == GDN_HEADS + h, da[b], 0.0)
                dgp = dgp + jnp.where(lane1 == h, dalog[b], 0.0) + jnp.where(lane1 == GDN_HEADS + h, ddtb[b], 0.0)
            dba_ref[rows, :] = dba
        dgp_ref[0:1, :] += dgp

    T = C * G
    return pl.pallas_call(
        body, name="gdn_post", grid=(nc // G,),
        in_specs=[sp["cur"], sp["prev"], sp["ba"], sp["cw"], sp["vec"], sp["hc"], sp["hd"], sp["hd"], sp["hd"], sp["hd"],
                  sp["hd"], sp["hd"], sp["hc"], sp["ge"]],
        out_specs=(sp["cur"], pl.BlockSpec((T, 128), lambda i: (i, 0)), pl.BlockSpec((8, 128), lambda i: (0, 0))),
        out_shape=(jax.ShapeDtypeStruct((S, 3 * GDN_WIDTH), F32), jax.ShapeDtypeStruct((S, 128), F32),
                   jax.ShapeDtypeStruct((8, 128), F32)),
        compiler_params=_cparams(("arbitrary",)),
    )(proj, proj, proj, conv_w, gp, tinv, u_v, w_k, d_uv, d_wk, d_qd, d_ke, d_at, d_ge)


def _conv_bwd(dpre, x, xcol0, w, K, name, tc):
    S, Cc = dpre.shape
    T = _pick_tile(S, 256)
    nt, ncol = S // T, Cc // tc
    xo = xcol0 // tc

    def body(d_ref, dn_ref, x_ref, xp_ref, w_ref, dx_ref, dw_ref):
        i = pl.program_id(1)
        dn = dn_ref[...] * jnp.where(i == nt - 1, 0.0, 1.0)
        dv = d_ref[...]
        ext_d = jnp.concatenate([dv, dn], axis=0)
        dx_ref[...] = _conv_taps_t(ext_d, w_ref[...], K, T).astype(dx_ref.dtype)
        xp = xp_ref[...] * jnp.where(i == 0, 0.0, 1.0)
        ext_x = jnp.concatenate([xp, x_ref[...]], axis=0)

        @pl.when(i == 0)
        def _():
            dw_ref[...] = jnp.zeros_like(dw_ref)

        for k in range(K):
            dw_ref[k:k + 1, :] += jnp.sum(dv * _shifted(ext_x, (K - 1) - k, 8, T), axis=0, keepdims=True)

    r8 = T // 8
    return pl.pallas_call(
        body, name=name, grid=(ncol, nt),
        in_specs=[pl.BlockSpec((T, tc), lambda j, i: (i, j)),
                  pl.BlockSpec((8, tc), lambda j, i: (jnp.minimum((i + 1) * r8, S // 8 - 1), j)),
                  pl.BlockSpec((T, tc), lambda j, i: (i, j + xo)),
                  pl.BlockSpec((8, tc), lambda j, i: (jnp.maximum(i * r8 - 1, 0), j + xo)),
                  pl.BlockSpec((K, tc), lambda j, i: (0, j))],
        out_specs=(pl.BlockSpec((T, tc), lambda j, i: (i, j)), pl.BlockSpec((K, tc), lambda j, i: (0, j))),
        out_shape=(jax.ShapeDtypeStruct((S, Cc), _MXU), jax.ShapeDtypeStruct((K, Cc), F32)),
        compiler_params=_cparams(("parallel", "arbitrary")),
    )(dpre, dpre, x, x, w)


def _dil_bias(nt, T):
    d = (np.arange(nt)[:, None, None] * T + np.arange(T)[None, None, :] - np.arange(T)[None, :, None])
    cnt = ((d >= 0) & (d <= 128)).astype(np.float64) + ((d >= 0) & (d % 4 == 0) & (d <= 512)) + ((d >= 0) & (d % 16 == 0))
    return jnp.asarray(np.where(cnt > 0, np.log(np.maximum(cnt, 1.0)), -1e30), dtype=F32)


def _attn_fwd(proj, after=()):
    S = proj.shape[0]
    T = min(ATT_T, S)
    nt, H = S // T, T // 2
    bias = _dil_bias(nt, T)
    scale = DIL_DIM ** -0.5
    npair = DIL_WIDTH // 128
    qb0, kb0, vb0 = P_QKVB // 128, (P_QKVB + DIL_WIDTH) // 128, (P_QKVB + 2 * DIL_WIDTH) // 128

    def body(q_ref, k_ref, v_ref, b_ref, *rest):
        o_ref, lse_ref = rest[-2:]
        i = pl.program_id(1)
        qs = (q_ref[...] * scale).astype(_MXU)

        def update(carry, kt, vt, qt, bt):
            out = []
            for hh in range(2):
                m, l, acc = carry[hh]
                sl = slice(hh * DIL_DIM, (hh + 1) * DIL_DIM)
                s = lax.dot_general(kt[:, sl], qt[:, sl], (_NT, ((), ())), preferred_element_type=F32) + bt
                m_new = jnp.maximum(m, jnp.max(s, axis=0, keepdims=True))
                p = jnp.exp(s - m_new)
                a = jnp.exp(m - m_new)
                l = a * l + jnp.sum(p, axis=0, keepdims=True)
                acc = a * acc + lax.dot_general(vt[:, sl], p.astype(_MXU), (_TN, ((), ())), preferred_element_type=F32)
                out.append((m_new, l, acc))
            return tuple(out)

        def keys(j):
            rows = pl.ds(pl.multiple_of(j * T, T), T)
            return k_ref[rows, :].astype(_MXU), v_ref[rows, :].astype(_MXU)

        init = tuple((jnp.full((1, T), -1e30, F32), jnp.zeros((1, T), F32), jnp.zeros((DIL_DIM, T), F32)) for _ in range(2))
        res = lax.fori_loop(0, i, lambda j, carry: update(carry, *keys(j), qs, b_ref[i - j]), init)
        kd, vd = keys(i)
        res = update(res, kd[:H], vd[:H], qs, b_ref[0, :H, :])
        late = update(tuple(tuple(t[:, H:] for t in r) for r in res), kd[H:], vd[H:], qs[H:], b_ref[0, H:, H:])
        res = tuple(tuple(jnp.concatenate([t[:, :H], u], axis=1) for t, u in zip(r, r2)) for r, r2 in zip(res, late))
        lse_ref[...] = jnp.zeros_like(lse_ref)
        for hh in range(2):
            m, l, acc = res[hh]
            o_ref[:, hh * DIL_DIM:(hh + 1) * DIL_DIM] = (acc / l).T
            lse_ref[hh:hh + 1, :] = m + jnp.log(l)

    return pl.pallas_call(
        body, name="attn_fwd", grid=(npair, nt),
        in_specs=[pl.BlockSpec((T, 128), lambda p, i: (i, qb0 + p)),
                  pl.BlockSpec((S, 128), lambda p, i: (0, kb0 + p)),
                  pl.BlockSpec((S, 128), lambda p, i: (0, vb0 + p)),
                  pl.BlockSpec((nt, T, T), lambda p, i: (0, 0, 0))] + [_ANY] * len(after),
        out_specs=(pl.BlockSpec((T, 128), lambda p, i: (i, GDN_WIDTH // 128 + p)),
                   pl.BlockSpec((None, None, 8, T), lambda p, i: (p, i, 0, 0))),
        out_shape=(jax.ShapeDtypeStruct((S, GDN_WIDTH + DIL_WIDTH), F32), jax.ShapeDtypeStruct((npair, nt, 8, T), F32)),
        compiler_params=_cparams(("parallel", "parallel")),
    )(proj, proj, proj, bias, *after)


def _attn_bwd(proj, mix, lse, d_mix):
    S = proj.shape[0]
    T = min(ATT_T, S)
    nt, H = S // T, T // 2
    bias = _dil_bias(nt, T)
    scale = DIL_DIM ** -0.5
    npair = DIL_WIDTH // 128
    qb0, kb0, vb0 = P_QKVB // 128, (P_QKVB + DIL_WIDTH) // 128, (P_QKVB + 2 * DIL_WIDTH) // 128

    def body(q_ref, k_ref, v_ref, o_ref, lse_ref, do_ref, b_ref, dq_ref, dk_ref, dv_ref, dq_scr):
        j = pl.program_id(1)

        @pl.when(j == 0)
        def _():
            dq_scr[...] = jnp.zeros_like(dq_scr)

        kt = k_ref[...].astype(_MXU)
        vt = v_ref[...].astype(_MXU)
        ones = jnp.ones((8, DIL_DIM), F32)

        def block(carry, kt, vt, rows, lsev, bt):
            qs = (q_ref[rows, :] * scale).astype(_MXU)
            dov = do_ref[rows, :]
            prod = dov * o_ref[rows, :]
            dob = dov.astype(_MXU)
            out = []
            dqs = []
            for hh in range(2):
                dk, dv = carry[hh]
                sl = slice(hh * DIL_DIM, (hh + 1) * DIL_DIM)
                s = lax.dot_general(kt[:, sl], qs[:, sl], (_NT, ((), ())), preferred_element_type=F32) + bt
                p = jnp.exp(s - lsev[hh:hh + 1, :])
                delta = lax.dot_general(ones, prod[:, sl], (_NT, ((), ())), precision=_HI, preferred_element_type=F32)[0:1, :]
                dp = lax.dot_general(vt[:, sl], dob[:, sl], (_NT, ((), ())), preferred_element_type=F32)
                ds = (p * (dp - delta)).astype(_MXU)
                dv = dv + lax.dot_general(p.astype(_MXU), dob[:, sl], (_NN, ((), ())), preferred_element_type=F32)
                dk = dk + lax.dot_general(ds, qs[:, sl], (_NN, ((), ())), preferred_element_type=F32)
                dqs.append(lax.dot_general(ds, kt[:, sl], (_TN, ((), ())), preferred_element_type=F32) * scale)
                out.append((dk, dv))
            dq_scr[rows, :] += jnp.concatenate(dqs, axis=1)
            return tuple(out)

        def step(i, carry):
            return block(carry, kt, vt, pl.ds(pl.multiple_of(i * T, T), T), lse_ref[i], b_ref[i - j])

        zeros = tuple((jnp.zeros((H, DIL_DIM), F32), jnp.zeros((H, DIL_DIM), F32)) for _ in range(2))
        lsed = lse_ref[j]
        early = block(zeros, kt[:H], vt[:H], pl.ds(pl.multiple_of(j * T, T), T), lsed, b_ref[0, :H, :])
        late = block(zeros, kt[H:], vt[H:], pl.ds(pl.multiple_of(j * T + H, H), H), lsed[:, H:], b_ref[0, H:, H:])
        init = tuple(tuple(jnp.concatenate([t, u], axis=0) for t, u in zip(r, r2)) for r, r2 in zip(early, late))
        res = lax.fori_loop(j + 1, nt, step, init)
        dk_ref[...] = jnp.concatenate([res[0][0], res[1][0]], axis=1).astype(dk_ref.dtype)
        dv_ref[...] = jnp.concatenate([res[0][1], res[1][1]], axis=1).astype(dv_ref.dtype)

        @pl.when(j == nt - 1)
        def _():
            dq_ref[...] = dq_scr[...].astype(dq_ref.dtype)

    full = lambda c0: pl.BlockSpec((S, 128), lambda p, j: (0, c0 + p))
    tile = lambda c0: pl.BlockSpec((T, 128), lambda p, j: (j, c0 + p))
    out3 = jax.ShapeDtypeStruct((S, DIL_WIDTH), _MXU)
    return pl.pallas_call(
        body, name="attn_bwd", grid=(npair, nt),
        in_specs=[full(qb0), tile(kb0), tile(vb0), full(GDN_WIDTH // 128),
                  pl.BlockSpec((None, nt, 8, T), lambda p, j: (p, 0, 0, 0)), full(GDN_WIDTH // 128),
                  pl.BlockSpec((nt, T, T), lambda p, j: (0, 0, 0))],
        out_specs=(full(0), tile(0), tile(0)),
        out_shape=(out3, out3, out3),
        scratch_shapes=[pltpu.VMEM((S, 128), F32)],
        compiler_params=_cparams(("parallel", "arbitrary")),
    )(proj, proj, proj, mix, lse, d_mix, bias)


def _ffn_act(up, cw):
    S, Cc = up.shape[0], up.shape[1] // 2
    T, tc = _pick_tile(S, 256), _pick_tile(Cc, 1536)
    r16 = T // 16
    nct = Cc // tc

    def body(g_ref, gp_ref, u_ref, up_ref, wg_ref, wu_ref, o_ref):
        keep = jnp.where(pl.program_id(1) == 0, 0.0, 1.0)
        cg = _conv_taps(jnp.concatenate([gp_ref[8:16, :].astype(F32) * keep, g_ref[...].astype(F32)], axis=0),
                        wg_ref[...], FFN_CONV, T)
        cu = _conv_taps(jnp.concatenate([up_ref[8:16, :].astype(F32) * keep, u_ref[...].astype(F32)], axis=0),
                        wu_ref[...], FFN_CONV, T)
        o_ref[...] = (_silu(cg) * cu).astype(o_ref.dtype)

    cur = lambda o: pl.BlockSpec((T, tc), lambda j, i: (i, j + o))
    prev = lambda o: pl.BlockSpec((16, tc), lambda j, i: (jnp.maximum(i * r16 - 1, 0), j + o))
    wsp = lambda o: pl.BlockSpec((FFN_CONV, tc), lambda j, i: (0, j + o))
    return pl.pallas_call(
        body, name="ffn_act", grid=(nct, S // T),
        in_specs=[cur(0), prev(0), cur(nct), prev(nct), wsp(0), wsp(nct)], out_specs=cur(0),
        out_shape=jax.ShapeDtypeStruct((S, Cc), _MXU),
        compiler_params=_cparams(("parallel", "parallel")),
    )(up, up, up, up, cw, cw)


def _ffn_act_bwd(d_act, up, cw):
    S, Cc = up.shape[0], up.shape[1] // 2
    T, tc = _pick_tile(S, 256), _pick_tile(Cc, 1536)
    r8, r16 = T // 8, T // 16
    nt = S // T
    nct = Cc // tc
    K = FFN_CONV

    def body(da_ref, dan_ref, g_ref, gp_ref, gn_ref, u_ref, up_ref, un_ref, wg_ref, wu_ref,
             dg_ref, du_ref, dwg_ref, dwu_ref):
        i = pl.program_id(1)
        keep_p = jnp.where(i == 0, 0.0, 1.0)
        keep_n = jnp.where(i == nt - 1, 0.0, 1.0)
        wg, wu = wg_ref[...], wu_ref[...]
        xg = jnp.concatenate([gp_ref[8:16, :].astype(F32) * keep_p, g_ref[...].astype(F32),
                              gn_ref[0:8, :].astype(F32) * keep_n], axis=0)
        xu = jnp.concatenate([up_ref[8:16, :].astype(F32) * keep_p, u_ref[...].astype(F32),
                              un_ref[0:8, :].astype(F32) * keep_n], axis=0)
        cg = _conv_taps(xg, wg, K, T + 8)
        cu = _conv_taps(xu, wu, K, T + 8)
        da = jnp.concatenate([da_ref[...], dan_ref[...] * keep_n], axis=0)
        sg = jax.nn.sigmoid(cg)
        d_cg = da * cu * (sg * (1.0 + cg * (1.0 - sg)))
        d_cu = da * (cg * sg)
        dg_ref[...] = _conv_taps_t(d_cg, wg, K, T).astype(dg_ref.dtype)
        du_ref[...] = _conv_taps_t(d_cu, wu, K, T).astype(du_ref.dtype)

        @pl.when(i == 0)
        def _():
            dwg_ref[...] = jnp.zeros_like(dwg_ref)
            dwu_ref[...] = jnp.zeros_like(dwu_ref)

        for k in range(K):
            dwg_ref[k:k + 1, :] += jnp.sum(d_cg[0:T, :] * _shifted(xg, (K - 1) - k, 8, T), axis=0, keepdims=True)
            dwu_ref[k:k + 1, :] += jnp.sum(d_cu[0:T, :] * _shifted(xu, (K - 1) - k, 8, T), axis=0, keepdims=True)

    cur = lambda o: pl.BlockSpec((T, tc), lambda j, i: (i, j + o))
    prev = lambda o: pl.BlockSpec((16, tc), lambda j, i: (jnp.maximum(i * r16 - 1, 0), j + o))
    nxt = lambda o: pl.BlockSpec((16, tc), lambda j, i: (jnp.minimum((i + 1) * r16, S // 16 - 1), j + o))
    nxt8 = pl.BlockSpec((8, tc), lambda j, i: (jnp.minimum((i + 1) * r8, S // 8 - 1), j))
    wsp = lambda o: pl.BlockSpec((K, tc), lambda j, i: (0, j + o))
    return pl.pallas_call(
        body, name="ffn_act_bwd", grid=(nct, nt),
        in_specs=[cur(0), nxt8, cur(0), prev(0), nxt(0), cur(nct), prev(nct), nxt(nct), wsp(0), wsp(nct)],
        out_specs=(cur(0), cur(0), wsp(0), wsp(0)),
        out_shape=(jax.ShapeDtypeStruct((S, Cc), _MXU), jax.ShapeDtypeStruct((S, Cc), _MXU),
                   jax.ShapeDtypeStruct((K, Cc), F32), jax.ShapeDtypeStruct((K, Cc), F32)),
        compiler_params=_cparams(("parallel", "arbitrary")),
    )(d_act, d_act, up, up, up, up, up, up, cw, cw)


def _local_step(x, tgt, h1, n1w, n2w, fnw, gp, gnw, wp, conv_w, fcw, rest_weights, early_grads):
    proj = _mm(h1, wp, "nn", name="proj")
    u_v, w_k, q_dec, k_end, attn, tinv, g_end = _gdn_pre(proj, conv_w, gp)
    mix, lse = _attn_fwd(proj)
    mix, states = _gdn_scan(u_v, w_k, q_dec, k_end, attn, g_end, proj, gnw, mix, after=[rest_weights[0]([mix])])
    w_out, w_up4, w_down = rest_weights[1]([mix])
    x2 = _mm(mix, w_out, "nn", residual=x, name="outproj")
    h2 = _rmsnorm_fwd(x2, n2w, "norm2")
    up = _mm(h2, w_up4, "nn", b_blocks=True, out_dtype=_MXU, name="up")
    act = _ffn_act(up, fcw)
    x3 = _mm(act, w_down, "nn", residual=x2, name="down")
    loss, dx3, dx3n, d_fnw = _loss_head(x3, fnw, tgt, "loss_head")
    d_act = _mm(dx3n, w_down, "nt", name="d_act")
    d_wdown = _mm(act, dx3n, "tn", name="d_wdown")
    d_upg, d_upu, d_fcwg, d_fcwu = _ffn_act_bwd(d_act, up, fcw)
    d_wup = _mm(h2, d_upg, "tn", place=("blocks", N_CHIPS, 0), tn=w_up4.shape[2], name="d_wgate")
    d_wup = _mm(h2, d_upu, "tn", place=("blocks", N_CHIPS, N_CHIPS // 2), tn=w_up4.shape[2], into=d_wup, name="d_wup")
    dx2, d_n2w = _rmsnorm_bwd(([d_upg, d_upu], w_up4), x2, n2w, dx3, "norm2_bwd")
    d_wout = _mm(mix, dx2, "tn", name="d_wout")
    token = early_grads[0](d_wup, d_wdown, d_wout)
    d_mix = _mm(dx2, w_out, "nt", name="d_mix", after=[token])
    dq_b, dk_b, dv_b = _attn_bwd(proj, mix, lse, d_mix)
    d_uv, d_wk, d_qd, d_ke, d_at, d_ge, d_z, d_gnw = _gdn_scan_bwd(u_v, w_k, q_dec, k_end, attn, g_end, proj,
                                                                   gnw, states, d_mix)
    d_pre, d_ba, d_gp = _gdn_post(proj, conv_w, gp, tinv, u_v, w_k, d_uv, d_wk, d_qd, d_ke, d_at, d_ge)
    token = early_grads[1]([d_pre])
    d_qkva, d_convw = _conv_bwd(d_pre, proj, 0, conv_w + token[0:1, 0:1], GDN_CONV, "gdn_conv_bwd", 512)
    d_proj = jnp.concatenate([d_qkva, d_z.astype(_MXU), dq_b, dk_b, dv_b, d_ba.astype(_MXU),
                              jnp.zeros((x.shape[0], P_COLS - P_BA - 128), _MXU)], axis=1)
    d_wp = _mm(h1, d_proj, "tn", name="d_wp")
    token = early_grads[2](d_wp)
    dx, d_n1w = _rmsnorm_bwd(([d_proj], wp[None]), x, n1w, dx2, "norm1_bwd", after=[token])
    grads = dict(wp=d_wp, conv_w=d_convw, w_out=d_wout, w_up=d_wup, fcw_g=d_fcwg, fcw_u=d_fcwu, w_down=d_wdown,
                 n1w=d_n1w, n2w=d_n2w, fnw=d_fnw, gp=d_gp, gnw=d_gnw)
    return loss, dx, grads


_HBM = pl.BlockSpec(memory_space=pltpu.HBM)


def _pos():
    return lax.axis_index("x"), lax.axis_index("y"), lax.axis_index("c")


def _other_chips(x, y):
    return [(1 - x, y), (x, 1 - y), (1 - x, 1 - y)]


def _halvable(shape):
    return shape[0] % 32 == 0


def _rows_of_half(shape, half):
    if not _halvable(shape):
        return pl.ds(0, shape[0])
    return pl.ds(pl.multiple_of(half * (shape[0] // 2), 16), shape[0] // 2)


_SEM = pl.BlockSpec(memory_space=pltpu.SEMAPHORE)
_ANY = pl.BlockSpec(memory_space=pl.ANY)
_DATAFLOW = pltpu.SideEffectType.DATAFLOW_SIDE_EFFECTING


def _in_hbm(a):
    return pltpu.with_memory_space_constraint(a, pltpu.HBM)


def _halves_copy(src_refs, land_refs, send_sems, recv_sems, shapes, a, j, block, x, y, c):
    px, py = _other_chips(x, y)[j]
    rows = _rows_of_half(shapes[a], c)
    return pltpu.make_async_remote_copy(
        src_ref=src_refs[a].at[rows, :], dst_ref=land_refs[a].at[block, rows, :], send_sem=send_sems.at[3 * a + j],
        recv_sem=recv_sems.at[3 * a + j], device_id=(px, py, c), device_id_type=MESH)


def _gather_halves_start(shards, after, name):
    n = len(shards)
    shapes = [s.shape for s in shards]

    def body(*refs):
        ins, lands = refs[:n], refs[n:2 * n]
        send_sems, recv_sems = refs[2 * n + 1], refs[2 * n + 2]
        token = refs[-1]
        x, y, c = _pos()
        q = 2 * x + y
        for a in range(n):
            for j in range(3):
                _halves_copy(ins, lands, send_sems, recv_sems, shapes, a, j, q, x, y, c).start()
        token[...] = jnp.zeros_like(token)

    land_shapes = [(N_CHIPS,) + s.shape for s in shards]
    return pl.pallas_call(
        body, name=name,
        out_shape=(pltpu.SemaphoreType.DMA((3 * n,)), pltpu.SemaphoreType.DMA((3 * n,)),
                   *[pltpu.HBM(s.shape, s.dtype) for s in shards],
                   *[pltpu.HBM(ls, s.dtype) for ls, s in zip(land_shapes, shards)],
                   jax.ShapeDtypeStruct((8, 128), F32)),
        in_specs=[_HBM] * (2 * n) + [_ANY],
        out_specs=(_SEM, _SEM, *[_HBM] * (2 * n), pl.BlockSpec(memory_space=pltpu.VMEM)),
        input_output_aliases={a: 2 + a for a in range(2 * n)},
        compiler_params=pltpu.CompilerParams(has_side_effects=_DATAFLOW),
    )(*[_in_hbm(s) for s in shards], *[_in_hbm(lax.empty(ls, s.dtype)) for ls, s in zip(land_shapes, shards)], after)


def _gather_halves_wait(started, after, name):
    send_sems, recv_sems, *thru = started
    n = len(thru) // 2
    shapes = [t.shape for t in thru[:n]]

    def body(*refs):
        ins, lands = refs[:n], refs[n:2 * n]
        send_sems, recv_sems = refs[2 * n], refs[2 * n + 1]
        x, y, c = _pos()
        q = 2 * x + y
        chips = _other_chips(x, y)
        for a in range(n):
            for j, (px, py) in enumerate(chips):
                _halves_copy(ins, lands, send_sems, recv_sems, shapes, a, j, q, x, y, c).wait_send()
                _halves_copy(ins, lands, send_sems, recv_sems, shapes, a, j, 2 * px + py, x, y, c).wait_recv()

    outs = pl.pallas_call(
        body, name=name, out_shape=[pltpu.HBM(t.shape, t.dtype) for t in thru],
        in_specs=[_HBM] * (2 * n) + [_SEM, _SEM] + [_ANY] * len(after), out_specs=[_HBM] * (2 * n),
        input_output_aliases={a: a for a in range(2 * n)},
        compiler_params=pltpu.CompilerParams(has_side_effects=_DATAFLOW),
    )(*thru, send_sems, recv_sems, *after)
    return outs[:n], outs[n:]


def _sibling_fill(gathered, name):
    big = [a for a, g in enumerate(gathered) if _halvable(g.shape[1:])]
    n = len(gathered)

    def body(*refs):
        ins, outs = refs[:n], refs[n:2 * n]
        send_sems, recv_sems = refs[2 * n:]
        x, y, c = _pos()
        chips = _other_chips(x, y)

        def copy(k, j, half):
            a = big[k]
            px, py = chips[j]
            rows = _rows_of_half(gathered[a].shape[1:], half)
            return pltpu.make_async_remote_copy(
                src_ref=ins[a].at[2 * px + py, rows, :], dst_ref=outs[a].at[2 * px + py, rows, :],
                send_sem=send_sems.at[3 * k + j], recv_sem=recv_sems.at[3 * k + j],
                device_id=(x, y, 1 - c), device_id_type=MESH)

        sends = [copy(k, j, c) for k in range(len(big)) for j in range(3)]
        for cp in sends:
            cp.start()
        for k in range(len(big)):
            for j in range(3):
                copy(k, j, 1 - c).wait_recv()
        for cp in sends:
            cp.wait_send()

    return pl.pallas_call(
        body, name=name, in_specs=[_HBM] * n, out_specs=[_HBM] * n,
        out_shape=[jax.ShapeDtypeStruct(g.shape, g.dtype) for g in gathered],
        input_output_aliases={a: a for a in range(n)},
        scratch_shapes=[pltpu.SemaphoreType.DMA((3 * len(big),)), pltpu.SemaphoreType.DMA((3 * len(big),))],
    )(*gathered)


def _fill_copy(refs, send_sems, recv_sems, shapes, a, j, half, x, y, c):
    px, py = _other_chips(x, y)[j]
    rows = _rows_of_half(shapes[a], half)
    return pltpu.make_async_remote_copy(
        src_ref=refs[a].at[2 * px + py, rows, :], dst_ref=refs[a].at[2 * px + py, rows, :],
        send_sem=send_sems.at[3 * a + j], recv_sem=recv_sems.at[3 * a + j],
        device_id=(x, y, 1 - c), device_id_type=MESH)


def _sibling_fill_start(gathered, name):
    n = len(gathered)
    shapes = [g.shape[1:] for g in gathered]

    def body(*refs):
        ins = refs[:n]
        send_sems, recv_sems = refs[n], refs[n + 1]
        token = refs[-1]
        x, y, c = _pos()
        for a in range(n):
            for j in range(3):
                _fill_copy(ins, send_sems, recv_sems, shapes, a, j, c, x, y, c).start()
        token[...] = jnp.zeros_like(token)

    return pl.pallas_call(
        body, name=name,
        out_shape=(pltpu.SemaphoreType.DMA((3 * n,)), pltpu.SemaphoreType.DMA((3 * n,)),
                   *[pltpu.HBM(g.shape, g.dtype) for g in gathered], jax.ShapeDtypeStruct((8, 128), F32)),
        in_specs=[_HBM] * n,
        out_specs=(_SEM, _SEM, *[_HBM] * n, pl.BlockSpec(memory_space=pltpu.VMEM)),
        input_output_aliases={a: 2 + a for a in range(n)},
        compiler_params=pltpu.CompilerParams(has_side_effects=_DATAFLOW),
    )(*[_in_hbm(g) for g in gathered])


def _sibling_fill_wait(started, after, name):
    send_sems, recv_sems, *thru = started
    n = len(thru)
    shapes = [t.shape[1:] for t in thru]

    def body(*refs):
        ins = refs[:n]
        send_sems, recv_sems = refs[n], refs[n + 1]
        x, y, c = _pos()
        for a in range(n):
            for j in range(3):
                _fill_copy(ins, send_sems, recv_sems, shapes, a, j, c, x, y, c).wait_send()
                _fill_copy(ins, send_sems, recv_sems, shapes, a, j, 1 - c, x, y, c).wait_recv()

    return pl.pallas_call(
        body, name=name, out_shape=[pltpu.HBM(t.shape, t.dtype) for t in thru],
        in_specs=[_HBM] * n + [_SEM, _SEM] + [_ANY] * len(after), out_specs=[_HBM] * n,
        input_output_aliases={a: a for a in range(n)},
        compiler_params=pltpu.CompilerParams(has_side_effects=_DATAFLOW),
    )(*thru, send_sems, recv_sems, *after)


def _place_own(shards, gathered, cq, name, carry=()):
    n = len(shards)
    nc = len(carry)
    steps = 4

    def body(cq_ref, *refs):
        for a in range(n):
            refs[2 * n + nc + a][...] = refs[a][...]

    def tile(shape):
        return shape[0] // steps if _halvable(shape) else shape[0]

    in_specs = [pl.BlockSpec((tile(s.shape), s.shape[1]), (lambda i, s_: (i, 0)) if _halvable(s.shape) else (lambda i, s_: (0, 0)))
                for s in shards]
    in_specs += [pl.BlockSpec(memory_space=pl.ANY)] * (n + nc)
    out_specs = [pl.BlockSpec((None, tile(s.shape), s.shape[1]),
                              (lambda i, s_: (s_[1], i, 0)) if _halvable(s.shape) else (lambda i, s_: (s_[1], 0, 0)))
                 for s in shards]
    out_specs += [pl.BlockSpec(memory_space=pl.ANY)] * nc
    gs = pltpu.PrefetchScalarGridSpec(num_scalar_prefetch=1, grid=(steps,), in_specs=in_specs, out_specs=out_specs)
    outs = pl.pallas_call(
        body, name=name, grid_spec=gs,
        out_shape=[jax.ShapeDtypeStruct(g.shape, g.dtype) for g in gathered] + [jax.ShapeDtypeStruct(t.shape, t.dtype) for t in carry],
        input_output_aliases={1 + n + a: a for a in range(n + nc)},
        compiler_params=_cparams(("arbitrary",)),
    )(cq, *shards, *gathered, *carry)
    return (outs[:n], outs[n:]) if nc else outs


def _half_rows(ref, c, rh):
    return ref.at[:, pl.ds(pl.multiple_of(c * rh, 8), rh), :]


def _chips_copy(src_refs, land_refs, send_sems, recv_sems, a, j, x, y, c):
    px, py = _other_chips(x, y)[j]
    return pltpu.make_async_remote_copy(src_ref=src_refs[a].at[2 * px + py], dst_ref=land_refs[a].at[j],
                                        send_sem=send_sems.at[3 * a + j], recv_sem=recv_sems.at[3 * a + j],
                                        device_id=(px, py, c), device_id_type=MESH)


def _peer(r, x, y, c):
    return (x if r & 4 == 0 else 1 - x), (y if r & 2 == 0 else 1 - y), (c if r & 1 == 0 else 1 - c)


def _small_copy(small_ref, all_ref, send_sems, recv_sems, base, r, slot, x, y, c):
    return pltpu.make_async_remote_copy(src_ref=small_ref, dst_ref=all_ref.at[slot], send_sem=send_sems.at[base + r - 1],
                                        recv_sem=recv_sems.at[base + r - 1], device_id=_peer(r, x, y, c), device_id_type=MESH)


def _grad_chips_start(parts, name, small=None):
    n = len(parts)
    srcs = list(parts) + ([] if small is None else [small])
    m = len(srcs)

    def body(*refs):
        ins, lands = refs[:m], refs[m:2 * m]
        send_sems, recv_sems = refs[2 * m], refs[2 * m + 1]
        token = refs[-1]
        x, y, c = _pos()
        for a in range(n):
            for j in range(3):
                _chips_copy(ins, lands, send_sems, recv_sems, a, j, x, y, c).start()
        if small is not None:
            for r in range(1, 8):
                _small_copy(ins[n], lands[n], send_sems, recv_sems, 3 * n, r, 4 * x + 2 * y + c, x, y, c).start()
        token[...] = jnp.zeros_like(token)

    land_shapes = [(3,) + p.shape[1:] for p in parts] + ([] if small is None else [(8,) + small.shape])
    nsem = 3 * n + (0 if small is None else 7)
    return pl.pallas_call(
        body, name=name,
        out_shape=(pltpu.SemaphoreType.DMA((nsem,)), pltpu.SemaphoreType.DMA((nsem,)),
                   *[pltpu.HBM(p.shape, p.dtype) for p in srcs],
                   *[pltpu.HBM(ls, p.dtype) for ls, p in zip(land_shapes, srcs)],
                   jax.ShapeDtypeStruct((8, 128), F32)),
        in_specs=[_HBM] * (2 * m),
        out_specs=(_SEM, _SEM, *[_HBM] * (2 * m), pl.BlockSpec(memory_space=pltpu.VMEM)),
        input_output_aliases={a: 2 + a for a in range(2 * m)},
        compiler_params=pltpu.CompilerParams(has_side_effects=_DATAFLOW),
    )(*[_in_hbm(p) for p in srcs], *[_in_hbm(lax.empty(ls, p.dtype)) for ls, p in zip(land_shapes, srcs)])


def _grad_chips_wait(started, after, name, with_small=False):
    send_sems, recv_sems, *thru = started
    m = len(thru) // 2
    n = m - (1 if with_small else 0)

    def body(*refs):
        ins, lands = refs[:m], refs[m:2 * m]
        send_sems, recv_sems = refs[2 * m], refs[2 * m + 1]
        x, y, c = _pos()
        for a in range(n):
            for j in range(3):
                cp = _chips_copy(ins, lands, send_sems, recv_sems, a, j, x, y, c)
                cp.wait_send()
                cp.wait_recv()
        if with_small:
            for r in range(1, 8):
                px, py, pc = _peer(r, x, y, c)
                _small_copy(ins[n], lands[n], send_sems, recv_sems, 3 * n, r, 4 * x + 2 * y + c, x, y, c).wait_send()
                _small_copy(ins[n], lands[n], send_sems, recv_sems, 3 * n, r, 4 * px + 2 * py + pc, x, y, c).wait_recv()

    outs = pl.pallas_call(
        body, name=name, out_shape=[pltpu.HBM(t.shape, t.dtype) for t in thru],
        in_specs=[_HBM] * (2 * m) + [_SEM, _SEM] + [_ANY] * len(after), out_specs=[_HBM] * (2 * m),
        input_output_aliases={a: a for a in range(2 * m)},
        compiler_params=pltpu.CompilerParams(has_side_effects=_DATAFLOW),
    )(*thru, send_sems, recv_sems, *after)
    return list(outs[m:]) + list(outs[n:m])


def _sibling_copy(src_refs, land_refs, send_sems, recv_sems, rhs, a, c, x, y):
    return pltpu.make_async_remote_copy(src_ref=_half_rows(src_refs[a], 1 - c, rhs[a]), dst_ref=land_refs[a],
                                        send_sem=send_sems.at[a], recv_sem=recv_sems.at[a],
                                        device_id=(x, y, 1 - c), device_id_type=MESH)


def _grad_sibling_start(fams, name):
    n = len(fams)
    rhs = [f.shape[1] // 2 for f in fams]

    def body(*refs):
        ins, lands = refs[:n], refs[n:2 * n]
        send_sems, recv_sems = refs[2 * n], refs[2 * n + 1]
        token = refs[-1]
        x, y, c = _pos()
        for a in range(n):
            _sibling_copy(ins, lands, send_sems, recv_sems, rhs, a, c, x, y).start()
        token[...] = jnp.zeros_like(token)

    land_shapes = [(f.shape[0], f.shape[1] // 2, f.shape[2]) for f in fams]
    return pl.pallas_call(
        body, name=name,
        out_shape=(pltpu.SemaphoreType.DMA((n,)), pltpu.SemaphoreType.DMA((n,)),
                   *[pltpu.HBM(f.shape, f.dtype) for f in fams],
                   *[pltpu.HBM(ls, f.dtype) for ls, f in zip(land_shapes, fams)],
                   jax.ShapeDtypeStruct((8, 128), F32)),
        in_specs=[_HBM] * (2 * n),
        out_specs=(_SEM, _SEM, *[_HBM] * (2 * n), pl.BlockSpec(memory_space=pltpu.VMEM)),
        input_output_aliases={a: 2 + a for a in range(2 * n)},
        compiler_params=pltpu.CompilerParams(has_side_effects=_DATAFLOW),
    )(*[_in_hbm(f) for f in fams], *[_in_hbm(lax.empty(ls, f.dtype)) for ls, f in zip(land_shapes, fams)])


def _grad_sibling_wait(started, after, name):
    send_sems, recv_sems, *thru = started
    n = len(thru) // 2
    rhs = [t.shape[1] // 2 for t in thru[:n]]

    def body(*refs):
        ins, lands = refs[:n], refs[n:2 * n]
        send_sems, recv_sems = refs[2 * n], refs[2 * n + 1]
        x, y, c = _pos()
        for a in range(n):
            cp = _sibling_copy(ins, lands, send_sems, recv_sems, rhs, a, c, x, y)
            cp.wait_send()
            cp.wait_recv()

    outs = pl.pallas_call(
        body, name=name, out_shape=[pltpu.HBM(t.shape, t.dtype) for t in thru],
        in_specs=[_HBM] * (2 * n) + [_SEM, _SEM] + [_ANY] * len(after), out_specs=[_HBM] * (2 * n),
        input_output_aliases={a: a for a in range(2 * n)},
        compiler_params=pltpu.CompilerParams(has_side_effects=_DATAFLOW),
    )(*thru, send_sems, recv_sems, *after)
    return outs[:n], outs[n:]


def _grad_share(fulls, name):
    n = len(fulls)
    rhs = [f.shape[0] // 2 for f in fulls]

    def body(*refs):
        ins, outs = refs[:n], refs[n:2 * n]
        send_sems, recv_sems = refs[2 * n], refs[2 * n + 1]
        x, y, c = _pos()

        def copy(a, half):
            rows = pl.ds(pl.multiple_of(half * rhs[a], 8), rhs[a])
            return pltpu.make_async_remote_copy(src_ref=ins[a].at[rows, :], dst_ref=outs[a].at[rows, :],
                                                send_sem=send_sems.at[a], recv_sem=recv_sems.at[a],
                                                device_id=(x, y, 1 - c), device_id_type=MESH)

        sends = [copy(a, c) for a in range(n)]
        for cp in sends:
            cp.start()
        for a in range(n):
            copy(a, 1 - c).wait_recv()
        for cp in sends:
            cp.wait_send()

    return pl.pallas_call(
        body, name=name, in_specs=[_HBM] * n, out_specs=[_HBM] * n,
        out_shape=[jax.ShapeDtypeStruct(f.shape, f.dtype) for f in fulls],
        input_output_aliases={a: a for a in range(n)},
        scratch_shapes=[pltpu.SemaphoreType.DMA((n,)), pltpu.SemaphoreType.DMA((n,))],
    )(*fulls)


def _add_sibling(own, recv, cq, name):
    nb, R, Cc = own.shape
    Rh = R // 2

    def body(cq_ref, a_ref, b_ref, o32_ref, o16_ref):
        s = a_ref[0] + b_ref[0]
        mine = pl.program_id(0) == cq_ref[1]

        @pl.when(mine)
        def _():
            o32_ref[...] = s

        @pl.when(jnp.logical_not(mine))
        def _():
            o16_ref[0] = s.astype(o16_ref.dtype)

    sp = pl.BlockSpec((1, Rh, Cc), lambda b, s: (b, 0, 0))
    gs = pltpu.PrefetchScalarGridSpec(
        num_scalar_prefetch=1, grid=(nb,),
        in_specs=[pl.BlockSpec((1, Rh, Cc), lambda b, s: (b, s[0], 0)), sp],
        out_specs=[pl.BlockSpec((Rh, Cc), lambda b, s: (0, 0)), sp])
    return pl.pallas_call(
        body, name=name, grid_spec=gs,
        out_shape=[jax.ShapeDtypeStruct((Rh, Cc), F32), jax.ShapeDtypeStruct((nb, Rh, Cc), _MXU)],
        compiler_params=_cparams(("arbitrary",)),
    )(cq, own, recv)


def _add_sibling_split(d_wp, recv, cq, name):
    _, Dm, Pc = d_wp.shape
    Rh = Dm // 2
    Wb = IN_COLS // N_CHIPS
    T = 256

    def body(cq_ref, a_ref, b_ref, o32_ref, o16_ref):
        s = a_ref[0] + b_ref[0]
        blocks = [s[:, 0:Wb], s[:, Wb:2 * Wb],
                  jnp.concatenate([s[:, 2 * Wb:P_QKVB], s[:, P_BA:P_BA + 8], s[:, P_QKVB:3 * Wb - 8]], axis=1),
                  s[:, 3 * Wb - 8:P_BA]]
        q = cq_ref[1]
        own = None
        for j, blk in enumerate(blocks):
            term = jnp.where(q == j, blk, 0.0)
            own = term if own is None else own + term
            o16_ref[j] = blk.astype(o16_ref.dtype)
        o32_ref[...] = own

    gs = pltpu.PrefetchScalarGridSpec(
        num_scalar_prefetch=1, grid=(Rh // T,),
        in_specs=[pl.BlockSpec((1, T, Pc), lambda i, s: (0, s[0] * (Rh // T) + i, 0)), pl.BlockSpec((1, T, Pc), lambda i, s: (0, i, 0))],
        out_specs=[pl.BlockSpec((T, Wb), lambda i, s: (i, 0)), pl.BlockSpec((N_CHIPS, T, Wb), lambda i, s: (0, i, 0))])
    return pl.pallas_call(
        body, name=name, grid_spec=gs,
        out_shape=[jax.ShapeDtypeStruct((Rh, Wb), F32), jax.ShapeDtypeStruct((N_CHIPS, Rh, Wb), _MXU)],
        compiler_params=_cparams(("parallel",)),
    )(cq, d_wp, recv)


def _add_chips(part32, recv3, cq, name):
    Rh, Cc = part32.shape

    def body(cq_ref, a_ref, b_ref, o_ref):
        acc = a_ref[...]
        for j in range(3):
            acc = acc + b_ref[j].astype(F32)
        o_ref[...] = acc

    gs = pltpu.PrefetchScalarGridSpec(
        num_scalar_prefetch=1, grid=(1,),
        in_specs=[pl.BlockSpec((Rh, Cc), lambda i, s: (0, 0)), pl.BlockSpec((3, Rh, Cc), lambda i, s: (0, 0, 0))],
        out_specs=pl.BlockSpec((Rh, Cc), lambda i, s: (s[0], 0)))
    return pl.pallas_call(
        body, name=name, grid_spec=gs, out_shape=jax.ShapeDtypeStruct((2 * Rh, Cc), F32),
        compiler_params=_cparams(("arbitrary",)),
    )(cq, part32, recv3)


def _transposed(g):
    Dm, n = g.shape
    pad = -n % 128

    def body(g_ref, o_ref):
        xp = jnp.concatenate([g_ref[...], jnp.zeros((Dm, pad), F32)], axis=1)
        o_ref[...] = xp.T[:n, :]

    return pl.pallas_call(body, name="transposed", out_shape=jax.ShapeDtypeStruct((n, Dm), F32),
                          compiler_params=_cparams(vmem=V7X_VMEM_LIMIT))(g)


def _adamw(w, g, m, v, name):
    R, Cc = w.shape
    T = max([t for t in range(8, 257, 8) if R % t == 0], default=R)

    def body(w_ref, g_ref, m_ref, v_ref, d_ref, mo_ref, vo_ref):
        d_ref[...], mo_ref[...], vo_ref[...] = _adamw_math(w_ref[...], g_ref[...], m_ref[...], v_ref[...])

    sp = pl.BlockSpec((T, Cc), lambda i: (i, 0))
    sh = jax.ShapeDtypeStruct((R, Cc), F32)
    return pl.pallas_call(
        body, name=name, grid=(R // T,), in_specs=[sp] * 4, out_specs=(sp, sp, sp), out_shape=(sh, sh, sh),
        compiler_params=_cparams(("parallel",)),
    )(w, g, m, v)


SMALL_ROWS = 32
ROW_CONV, ROW_FCG, ROW_FCU = 5, 13, 22


def _adamw_math(w, g, m, v):
    mn = ADAM_B1 * m + (1.0 - ADAM_B1) * g
    vn = ADAM_B2 * v + (1.0 - ADAM_B2) * (g * g)
    c1 = 1.0 / (1.0 - ADAM_B1 ** ADAM_STEP)
    c2 = 1.0 / (1.0 - ADAM_B2 ** ADAM_STEP)
    return -ADAM_LR * ((mn * c1) / (jnp.sqrt(vn * c2) + ADAM_EPS) + ADAM_WD * w), mn, vn


def _pack_small(n1, n2, fn, gp, gn, conv, fcg, fcu, loss):
    W = D_MODEL

    def body(n1_ref, n2_ref, fn_ref, gp_ref, gn_ref, conv_ref, fcg_ref, fcu_ref, loss_ref, o_ref):
        o_ref[...] = jnp.zeros_like(o_ref)
        o_ref[0:1, :] = n1_ref[...]
        o_ref[1:2, :] = n2_ref[...]
        o_ref[2:3, :] = fn_ref[...]
        o_ref[3:4, 0:8] = gp_ref[0:1, 0:8]
        o_ref[3:4, 8:9] = loss_ref[0:1, 0:1]
        o_ref[4:5, 0:128] = gn_ref[...]
        for i in range(GDN_CONV):
            o_ref[ROW_CONV + 2 * i:ROW_CONV + 2 * i + 1, :] = conv_ref[i:i + 1, 0:W]
            o_ref[ROW_CONV + 2 * i + 1:ROW_CONV + 2 * i + 2, 0:3 * GDN_WIDTH - W] = conv_ref[i:i + 1, W:3 * GDN_WIDTH]
        for r0, ref in ((ROW_FCG, fcg_ref), (ROW_FCU, fcu_ref)):
            for i in range(FFN_CONV):
                for k in range(3):
                    n = min(W, D_FF - k * W)
                    o_ref[r0 + 3 * i + k:r0 + 3 * i + k + 1, 0:n] = ref[i:i + 1, k * W:k * W + n]

    return pl.pallas_call(body, name="pack_small", out_shape=jax.ShapeDtypeStruct((SMALL_ROWS, W), F32))(
        n1, n2, fn, gp, gn, conv, fcg, fcu, loss)


def _small_step(meq, small_all, small, ws, ms, vs):
    W = D_MODEL
    n = len(ws)
    cw, fw = ws[6].shape[1], ws[7].shape[1]

    def body(meq_ref, all_ref, own_ref, *refs):
        w_refs, m_refs, v_refs = refs[:n], refs[n:2 * n], refs[2 * n:3 * n]
        loss_ref = refs[3 * n]
        outs = refs[3 * n + 1:]
        me, q = meq_ref[0], meq_ref[1]
        red = None
        for d in range(8):
            term = jnp.where(me == d, own_ref[...], all_ref[d])
            red = term if red is None else red + term
        loss_ref[...] = jnp.broadcast_to(red[3:4, 8:9], loss_ref.shape)
        conv = [jnp.concatenate([red[ROW_CONV + 2 * i:ROW_CONV + 2 * i + 1, :],
                                 red[ROW_CONV + 2 * i + 1:ROW_CONV + 2 * i + 2, 0:3 * GDN_WIDTH - W]], axis=1)
                for i in range(GDN_CONV)]
        conv = jnp.concatenate(conv, axis=0)

        def fc_rows(r0):
            rows = [jnp.concatenate([red[r0 + 3 * i + k:r0 + 3 * i + k + 1, 0:min(W, D_FF - k * W)] for k in range(3)], axis=1)
                    for i in range(FFN_CONV)]
            return jnp.concatenate(rows, axis=0)

        fc = jnp.concatenate([fc_rows(ROW_FCG), fc_rows(ROW_FCU)], axis=1)

        def chip_block(full, width):
            out = None
            for j in range(N_CHIPS):
                term = jnp.where(q == j, full[:, width * j:width * (j + 1)], 0.0)
                out = term if out is None else out + term
            return out

        grads = [red[0:1, :], red[1:2, :], red[2:3, :], red[3:4, 0:4], red[3:4, 4:8], red[4:5, 0:128],
                 chip_block(conv, cw), chip_block(fc, fw)]
        for k in range(n):
            d_, m_, v_ = _adamw_math(w_refs[k][...], grads[k], m_refs[k][...], v_refs[k][...])
            outs[4 * k][...] = grads[k]
            outs[4 * k + 1][...] = d_
            outs[4 * k + 2][...] = m_
            outs[4 * k + 3][...] = v_

    full = lambda a: pl.BlockSpec(a.shape, lambda i, s_, nd=len(a.shape): (0,) * nd)
    arrays = [small_all, small, *ws, *ms, *vs]
    out_shapes = [jax.ShapeDtypeStruct((8, 128), F32)] + [jax.ShapeDtypeStruct(w.shape, F32) for w in ws for _ in range(4)]
    gs = pltpu.PrefetchScalarGridSpec(
        num_scalar_prefetch=1, grid=(1,), in_specs=[full(a) for a in arrays],
        out_specs=[pl.BlockSpec(o.shape, lambda i, s_, nd=len(o.shape): (0,) * nd) for o in out_shapes])
    return pl.pallas_call(body, name="small_step", grid_spec=gs, out_shape=out_shapes)(meq, *arrays)


def _pad_lanes(v, n=D_MODEL):
    return jnp.pad(v, ((0, 0), (0, n - v.shape[1])))


def kernel(x, norm1_w, w_in, conv_qkv_w, a_log, dt_bias, gdn_norm_w, w_out, norm2_w, w_up, ffn_conv_w, w_down, final_norm_w, loss_target, m_norm1_w, m_w_in, m_conv_qkv_w, m_a_log, m_dt_bias, m_gdn_norm_w, m_w_out, m_norm2_w, m_w_up, m_ffn_conv_w, m_w_down, m_final_norm_w, v_norm1_w, v_w_in, v_conv_qkv_w, v_a_log, v_dt_bias, v_gdn_norm_w, v_w_out, v_norm2_w, v_w_up, v_ffn_conv_w, v_w_down, v_final_norm_w):
    c = lax.axis_index("c")
    q = 2 * lax.axis_index("x") + lax.axis_index("y")
    S = x.shape[1]
    cq = jnp.stack([c, q]).astype(jnp.int32)

    *in_started, in_token = _gather_halves_start([w_in[0].astype(_MXU), conv_qkv_w[0], ffn_conv_w[0]], x, "gather_in_start")
    w_in_l, m_w_in_l, v_w_in_l = (jnp.swapaxes(a + in_token[0:1, 0:1], 1, 2)[0] for a in (w_in, m_w_in, v_w_in))
    h1 = _rmsnorm_fwd(x[0], norm1_w, "norm1", after=[in_token])
    rest = [(a[0] + in_token[0:1, 0:1]).astype(_MXU) for a in (w_out, w_up, w_down)]
    in_shards, got_in = _gather_halves_wait(in_started, [w_in_l, m_w_in_l, v_w_in_l, h1, *rest], "gather_in_wait")
    (g_in, g_conv, g_fconv), (w_in_l, m_w_in_l, v_w_in_l) = _place_own(
        in_shards, _sibling_fill(got_in, "fill_in"), cq, "place_in", carry=[w_in_l, m_w_in_l, v_w_in_l])
    *rest_started, token = _gather_halves_start(rest, g_conv, "gather_rest_start")

    rest_state = {}

    def rest_arrived(after):
        rest_state["shards"], got = _gather_halves_wait(rest_started, after, "gather_rest_wait")
        *rest_state["fill"], tok = _sibling_fill_start(got, "fill_rest_start")
        return tok

    def rest_filled(after):
        got = _sibling_fill_wait(rest_state["fill"], after, "fill_rest_wait")
        g_out, g_up, g_down = _place_own(rest_state["shards"], got, cq, "place_rest")
        return g_out.reshape(D_MODEL, D_MODEL), g_up, g_down.reshape(D_FF, D_MODEL)

    rest_weights = (rest_arrived, rest_filled)
    wp = _wp_assemble(g_in, [token])
    conv_f = jnp.concatenate([g_conv[i] for i in range(N_CHIPS)], axis=1)
    fcw = jnp.concatenate([g_fconv[i] for i in range(N_CHIPS)], axis=1)
    gp = _pad_lanes(jnp.concatenate([a_log, dt_bias], axis=1), 128)
    fnw = final_norm_w[None, :]
    early = {}

    early_names = ("w_up", "w_down", "w_out")

    def early_sibling(d_wup, d_wdown, d_wout):
        *early["sibling"], tok = _grad_sibling_start(
            [d_wup, d_wdown.reshape(N_CHIPS, D_FF // N_CHIPS, D_MODEL), d_wout.reshape(N_CHIPS, D_MODEL // N_CHIPS, D_MODEL)],
            "grad_sibling_early_start")
        return tok

    def early_chips(after):
        fams_e, got_e = _grad_sibling_wait(early["sibling"], after, "grad_sibling_early_wait")
        early["parts"] = [_add_sibling(f, r, cq, "add_sibling_" + nm) for f, r, nm in zip(fams_e, got_e, early_names)]
        *early["started"], tok = _grad_chips_start([p[1] for p in early["parts"]], "grad_chips_start")
        return tok

    def late_sibling(d_wp):
        *early["late_sibling"], tok = _grad_sibling_start([d_wp[None]], "grad_sibling_late_start")
        return tok

    loss_l, dx, g = _local_step(x[0], loss_target[0], h1, norm1_w, norm2_w, fnw, gp, gdn_norm_w, wp,
                                conv_f, fcw, rest_weights, (early_sibling, early_chips, late_sibling))
    fams, got = _grad_sibling_wait(early["late_sibling"], [dx], "grad_sibling_late_wait")
    late_part = _add_sibling_split(fams[0], got[0], cq, "add_sibling_w_in")
    *late_started, late_token = _grad_chips_start([late_part[1]], "grad_chips_late_start")
    small = _pack_small(g["n1w"], g["n2w"], g["fnw"], g["gp"], g["gnw"], g["conv_w"], g["fcw_g"], g["fcw_u"], loss_l)
    *small_started, small_token = _grad_chips_start([], "small_gather_start", small)
    got3_e = _grad_chips_wait(early["started"], [late_token, small_token], "grad_chips_wait")
    g_w_up, g_w_down, g_w_out = _grad_share(
        [_add_chips(p[0], r3, cq, "add_chips_" + nm) for p, r3, nm in zip(early["parts"], got3_e, early_names)],
        "grad_share_early")
    big = {}

    def adamw_big(nm, w, gg, m, v):
        d_, m_, v_ = _adamw(w[0], gg, m[0], v[0], "adamw_" + nm)
        big[nm] = (gg[None], d_[None], m_[None], v_[None])

    adamw_big("w_up", w_up, g_w_up, m_w_up, v_w_up)
    adamw_big("w_down", w_down, g_w_down, m_w_down, v_w_down)
    adamw_big("w_out", w_out, g_w_out, m_w_out, v_w_out)
    got3, = _grad_chips_wait(late_started, [big[nm][1] for nm in early_names], "grad_chips_late_wait")
    g_w_in, = _grad_share([_add_chips(late_part[0], got3, cq, "add_chips_w_in")], "grad_share_late")
    g_t = _transposed(g_w_in)
    d_t, m_t, v_t = _adamw(w_in_l, g_t, m_w_in_l, v_w_in_l, "adamw_w_in")
    big["w_in"] = tuple(jnp.swapaxes(t[None], 1, 2) for t in (g_t, d_t, m_t, v_t))
    small_all, small = _grad_chips_wait(small_started, [d_t], "small_gather_wait", with_small=True)
    small_names = ["norm1_w", "norm2_w", "final_norm_w", "a_log", "dt_bias", "gdn_norm_w", "conv_qkv_w", "ffn_conv_w"]
    loss_b, *small_out = _small_step(
        jnp.stack([2 * q + c, q]).astype(jnp.int32), small_all, small,
        [norm1_w, norm2_w, final_norm_w[None], a_log, dt_bias, gdn_norm_w, conv_qkv_w[0], ffn_conv_w[0]],
        [m_norm1_w, m_norm2_w, m_final_norm_w[None], m_a_log, m_dt_bias, m_gdn_norm_w, m_conv_qkv_w[0], m_ffn_conv_w[0]],
        [v_norm1_w, v_norm2_w, v_final_norm_w[None], v_a_log, v_dt_bias, v_gdn_norm_w, v_conv_qkv_w[0], v_ffn_conv_w[0]])
    like = dict(final_norm_w=lambda t: t[0], conv_qkv_w=lambda t: t[None], ffn_conv_w=lambda t: t[None])
    for k, nm in enumerate(small_names):
        big[nm] = tuple(like.get(nm, lambda t: t)(t) for t in small_out[4 * k:4 * k + 4])
    names = ["norm1_w", "w_in", "conv_qkv_w", "a_log", "dt_bias", "gdn_norm_w", "w_out", "norm2_w", "w_up",
             "ffn_conv_w", "w_down", "final_norm_w"]
    return (loss_b[0, 0], dx[None], *[big[n][0] for n in names], *[big[n][1] for n in names],
            *[big[n][2] for n in names], *[big[n][3] for n in names])
```

```python
import functools
import math

import numpy as np
import jax
import jax.numpy as jnp
from jax import lax
from jax.experimental import pallas as pl
from jax.experimental.pallas import tpu as pltpu

F32 = jnp.float32
BF16 = jnp.bfloat16
_MXU = jnp.bfloat16
_HI = lax.Precision.HIGHEST
EPS = 1e-6
V7X_VMEM_LIMIT = 56 * 1024 * 1024
MESH = pl.DeviceIdType.MESH

D_MODEL = 1024
GDN_HEADS, GDN_DIM, GDN_CHUNK, GDN_CONV = 4, 128, 64, 4
GDN_WIDTH = GDN_HEADS * GDN_DIM
DIL_HEADS, DIL_DIM = 8, 64
DIL_WIDTH = DIL_HEADS * DIL_DIM
D_FF, FFN_CONV = 2816, 3
IN_COLS = 3592
P_COLS = 3840
P_Z, P_QKVB, P_BA = 1536, 2048, 3584
ATT_T = 1024
ADAM_LR, ADAM_B1, ADAM_B2, ADAM_EPS, ADAM_WD, ADAM_STEP = 0.001, 0.9, 0.999, 1e-08, 0.01, 10
N_CHIPS = 4


def _cparams(sem=None, vmem=None):
    kw = {}
    if sem is not None:
        kw["dimension_semantics"] = sem
    if vmem is not None:
        kw["vmem_limit_bytes"] = vmem
    return pltpu.CompilerParams(**kw)


def _silu(x):
    return x * jax.nn.sigmoid(x)


def _pick_tile(n, cap):
    best = None
    for t in range(128, min(n, cap) + 1, 128):
        if n % t == 0:
            best = t
    return best or n


def _mm(a, b, mode, *, out_dtype=F32, residual=None, name, b_blocks=False, place=None, into=None, tn=None, after=(),
        normed_by=None):
    if mode == "nn":
        M, K = a.shape
        N = b.shape[0] * b.shape[2] if b_blocks else b.shape[1]
    elif mode == "nt":
        (M, K), (N, _) = a.shape, b.shape
    else:
        (K, M), (_, N) = a.shape, b.shape
    tm = _pick_tile(M, 1024)
    tn = b.shape[2] if b_blocks else (tn or _pick_tile(N, 1536))

    def vmem(tm, tn):
        return 2 * (tm * K * a.dtype.itemsize + tn * K * b.dtype.itemsize
                    + tm * tn * (jnp.dtype(out_dtype).itemsize + (4 if residual is not None else 0))) + 3 * tm * tn * 4

    fixed_tn = b_blocks or (place is not None and place[0] == "blocks")
    while vmem(tm, tn) > 40 * 1024 * 1024:
        if (tm >= tn or fixed_tn) and tm % 256 == 0:
            tm //= 2
        elif tn % 256 == 0 and not fixed_tn:
            tn //= 2
        else:
            tm //= 2
    a_spec = pl.BlockSpec((K, tm), lambda j, i: (0, i)) if mode == "tn" else pl.BlockSpec((tm, K), lambda j, i: (i, 0))
    if b_blocks:
        b_spec = pl.BlockSpec((None, K, tn), lambda j, i: (j, 0, 0))
    else:
        b_spec = pl.BlockSpec((tn, K), lambda j, i: (j, 0)) if mode == "nt" else pl.BlockSpec((K, tn), lambda j, i: (0, j))
    r_spec = pl.BlockSpec((tm, tn), lambda j, i: (i, j))
    if place is None:
        o_spec, o_shape = r_spec, (M, N)
    elif place[0] == "rows":
        off = place[2] // tm
        o_spec, o_shape = pl.BlockSpec((tm, tn), lambda j, i: (i + off, j)), (place[1], N)
    else:
        off = place[2]
        o_spec, o_shape = pl.BlockSpec((None, tm, tn), lambda j, i: (j + off, i, 0)), (place[1], M, tn)
    dims = {"nn": (((1,), (0,)), ((), ())), "nt": (((1,), (1,)), ((), ())), "tn": (((0,), (0,)), ((), ()))}[mode]

    def body(*refs):
        a_ref, b_ref = refs[0], refs[1]
        o_ref = refs[-1] if normed_by is None else refs[-2]
        acc = lax.dot_general(a_ref[...].astype(_MXU), b_ref[...].astype(_MXU), dims, preferred_element_type=F32)
        if residual is not None:
            acc = acc + refs[2][...]
        o_ref[...] = acc.astype(out_dtype)
        if normed_by is not None:
            rs = lax.rsqrt(jnp.mean(acc * acc, axis=-1, keepdims=True) + EPS)
            refs[-1][...] = (acc * rs * refs[len(ins0)][...]).astype(_MXU)

    ins, specs, alias = [a, b], [a_spec, b_spec], {}
    if residual is not None:
        ins.append(residual)
        specs.append(r_spec)
    ins0 = list(ins)
    if normed_by is not None:
        assert tn == N and place is None and into is None
        ins.append(normed_by)
        specs.append(pl.BlockSpec((1, N), lambda j, i: (0, 0)))
    if into is not None:
        alias = {len(ins): 0}
        ins.append(into)
        specs.append(pl.BlockSpec(memory_space=pl.ANY))
    ins += list(after)
    specs += [pl.BlockSpec(memory_space=pl.ANY)] * len(after)
    o_shape = jax.ShapeDtypeStruct(o_shape, out_dtype)
    if normed_by is not None:
        o_spec, o_shape = (o_spec, r_spec), (o_shape, jax.ShapeDtypeStruct((M, N), _MXU))
    return pl.pallas_call(
        body, name=name, grid=(N // tn, M // tm), in_specs=specs, out_specs=o_spec,
        out_shape=o_shape, input_output_aliases=alias,
        compiler_params=_cparams(("parallel", "parallel"), V7X_VMEM_LIMIT),
    )(*ins)


def _wp_assemble(g_in, after=()):
    nb, Dm, Wb = g_in.shape
    T = 256
    n_lo = P_QKVB - 2 * Wb

    def body(g_ref, *rest):
        g2 = g_ref[2]
        rest[-1][...] = jnp.concatenate(
            [g_ref[0], g_ref[1], g2[:, :n_lo], g2[:, n_lo + 8:], g_ref[3], g2[:, n_lo:n_lo + 8],
             jnp.zeros((T, P_COLS - P_BA - 8), g_in.dtype)], axis=1)

    return pl.pallas_call(
        body, name="wp_assemble", grid=(Dm // T,),
        in_specs=[pl.BlockSpec((nb, T, Wb), lambda i: (0, i, 0))] + [pl.BlockSpec(memory_space=pl.ANY)] * len(after),
        out_specs=pl.BlockSpec((T, P_COLS), lambda i: (i, 0)), out_shape=jax.ShapeDtypeStruct((Dm, P_COLS), g_in.dtype),
        compiler_params=_cparams(("parallel",)),
    )(g_in, *after)


def _rmsnorm_fwd(x, w, name, after=()):
    S, D = x.shape
    T = _pick_tile(S, 512)

    def body(x_ref, w_ref, *rest):
        xv = x_ref[...]
        rs = lax.rsqrt(jnp.mean(xv * xv, axis=-1, keepdims=True) + EPS)
        rest[-1][...] = (xv * rs * w_ref[...]).astype(rest[-1].dtype)

    return pl.pallas_call(
        body, name=name, grid=(S // T,),
        in_specs=[pl.BlockSpec((T, D), lambda i: (i, 0)), pl.BlockSpec((1, D), lambda i: (0, 0))] + [_ANY] * len(after),
        out_specs=pl.BlockSpec((T, D), lambda i: (i, 0)),
        out_shape=jax.ShapeDtypeStruct((S, D), _MXU),
        compiler_params=_cparams(("parallel",)),
    )(x, w, *after)


def _rmsnorm_bwd(dh, x, w, dres, name, after=()):
    S, D = x.shape
    pair = isinstance(dh, tuple)
    T = _pick_tile(S, 256 if pair else 512)
    dhs = [*dh[0], dh[1]] if pair else [dh]

    def body(*refs):
        x_ref, w_ref, dres_ref = refs[len(dhs):len(dhs) + 3]
        dx_ref, dw_ref = refs[-2:]
        xv = x_ref[...]
        rs = lax.rsqrt(jnp.mean(xv * xv, axis=-1, keepdims=True) + EPS)
        xn = xv * rs
        if pair:
            b_ref = refs[len(dhs) - 1]
            nb, _, Kb = b_ref.shape
            per = nb // (len(dhs) - 1)
            dhv = None
            for blk in range(nb):
                lo = (blk % per) * Kb
                t = lax.dot_general(refs[blk // per][:, lo:lo + Kb].astype(_MXU), b_ref[blk].astype(_MXU), (_NT, ((), ())),
                                    preferred_element_type=F32)
                dhv = t if dhv is None else dhv + t
        else:
            dhv = refs[0][...]
        dxn = dhv * w_ref[...]
        dx_ref[...] = dres_ref[...] + rs * (dxn - xn * jnp.mean(dxn * xn, axis=-1, keepdims=True))

        @pl.when(pl.program_id(0) == 0)
        def _():
            dw_ref[...] = jnp.zeros_like(dw_ref)

        dw_ref[...] += jnp.sum(dhv * xn, axis=0, keepdims=True)

    row = pl.BlockSpec((T, D), lambda i: (i, 0))
    vec = pl.BlockSpec((1, D), lambda i: (0, 0))
    dh_specs = [row] if not pair else (
        [pl.BlockSpec((T, a.shape[1]), lambda i: (i, 0)) for a in dh[0]] + [pl.BlockSpec(dh[1].shape, lambda i: (0, 0, 0))])
    return pl.pallas_call(
        body, name=name, grid=(S // T,), in_specs=dh_specs + [row, vec, row] + [_ANY] * len(after), out_specs=(row, vec),
        out_shape=(jax.ShapeDtypeStruct((S, D), F32), jax.ShapeDtypeStruct((1, D), F32)),
        compiler_params=_cparams(("arbitrary",), V7X_VMEM_LIMIT if pair else None),
    )(*dhs, x, w, dres, *after)


def _loss_head(x3, w, tgt, name):
    S, D = x3.shape
    T = _pick_tile(S, 512)

    def body(x_ref, w_ref, t_ref, loss_ref, dx_ref, dxn_ref, dw_ref):
        xv = x_ref[...]
        rs = lax.rsqrt(jnp.mean(xv * xv, axis=-1, keepdims=True) + EPS)
        xn = xv * rs
        err = xn * w_ref[...] - t_ref[...]
        dy = err * (1.0 / D)
        dxn = dy * w_ref[...]
        dxv = rs * (dxn - xn * jnp.mean(dxn * xn, axis=-1, keepdims=True))
        dx_ref[...] = dxv
        dxn_ref[...] = dxv.astype(dxn_ref.dtype)

        @pl.when(pl.program_id(0) == 0)
        def _():
            dw_ref[...] = jnp.zeros_like(dw_ref)
            loss_ref[...] = jnp.zeros_like(loss_ref)

        dw_ref[...] += jnp.sum(dy * xn, axis=0, keepdims=True)
        part = jnp.sum(jnp.sum(err * err, axis=-1, keepdims=True), axis=0, keepdims=True) * (0.5 / D)
        loss_ref[...] += jnp.broadcast_to(part, loss_ref.shape)

    row = pl.BlockSpec((T, D), lambda i: (i, 0))
    vec = pl.BlockSpec((1, D), lambda i: (0, 0))
    return pl.pallas_call(
        body, name=name, grid=(S // T,), in_specs=[row, vec, row],
        out_specs=(pl.BlockSpec((8, 128), lambda i: (0, 0)), row, row, vec),
        out_shape=(jax.ShapeDtypeStruct((8, 128), F32), jax.ShapeDtypeStruct((S, D), F32), jax.ShapeDtypeStruct((S, D), _MXU),
                   jax.ShapeDtypeStruct((1, D), F32)),
        compiler_params=_cparams(("arbitrary",)),
    )(x3, w, tgt)


def _shifted(ext, back, lo, n):
    if back == 0:
        return ext[lo:lo + n, :]
    return pltpu.roll(ext, back % ext.shape[0], 0)[lo:lo + n, :]


def _conv_windows(ext, K, T):
    return [_shifted(ext, (K - 1) - i, 8, T) for i in range(K)]


def _conv_taps(ext, w, K, T):
    out = None
    for i, win in enumerate(_conv_windows(ext, K, T)):
        term = win * w[i:i + 1, :]
        out = term if out is None else out + term
    return out


def _conv_taps_t(ext, w, K, T):
    out = None
    for i in range(K):
        term = _shifted(ext, i - (K - 1), 0, T) * w[i:i + 1, :]
        out = term if out is None else out + term
    return out


def _tri_masks(C):
    r = lax.broadcasted_iota(jnp.int32, (C, C), 0)
    c = lax.broadcasted_iota(jnp.int32, (C, C), 1)
    return r == c, r >= c, r > c, r <= c


_NN, _NT, _TN = ((1,), (0,)), ((1,), (1,)), ((0,), (0,))
_GDN_PASSES = dict(qk=1, inv=1, sol=1, scan=1, bwd=1)


def _bdot_raw(a, b, kind, passes):
    dims = ({"NN": ((2,), (1,)), "NT": ((2,), (2,)), "TN": ((1,), (1,))}[kind], ((0,), (0,)))
    if passes == 0:
        return lax.dot_general(a, b, dims, precision=_HI, preferred_element_type=F32)
    ah, bh = a.astype(BF16), b.astype(BF16)
    out = lax.dot_general(ah, bh, dims, preferred_element_type=F32)
    if passes == 3:
        al, bl = (a - ah.astype(F32)).astype(BF16), (b - bh.astype(F32)).astype(BF16)
        out = out + lax.dot_general(ah, bl, dims, preferred_element_type=F32) + lax.dot_general(al, bh, dims, preferred_element_type=F32)
    return out


@functools.partial(jax.custom_vjp, nondiff_argnums=(2, 3))
def _bdot(a, b, kind, passes):
    return _bdot_raw(a, b, kind, passes)


def _bdot_fwd(a, b, kind, passes):
    return _bdot_raw(a, b, kind, passes), (a, b)


def _bdot_bwd(kind, passes, res, ct):
    a, b = res
    if kind == "NN":
        return _bdot_raw(ct, b, "NT", passes), _bdot_raw(a, ct, "TN", passes)
    if kind == "NT":
        return _bdot_raw(ct, b, "NN", passes), _bdot_raw(ct, a, "TN", passes)
    return _bdot_raw(b, ct, "NT", passes), _bdot_raw(a, ct, "NN", passes)


_bdot.defvjp(_bdot_fwd, _bdot_bwd)


def _softplus(x):
    return jnp.maximum(x, 0.0) + jnp.log(1.0 + jnp.exp(-jnp.abs(x)))


def _gdn_stage1(cq, ck, cv, b_col, a_col, alog, dtb, dot=_bdot_raw):
    C = cq.shape[1]
    eye, incl, strict, incl_t = _tri_masks(C)
    qn = cq * lax.rsqrt(jnp.sum(cq * cq, axis=-1, keepdims=True) + EPS) * (GDN_DIM ** -0.5)
    kn = ck * lax.rsqrt(jnp.sum(ck * ck, axis=-1, keepdims=True) + EPS)
    beta = jax.nn.sigmoid(b_col)
    g = -jnp.exp(alog) * _softplus(a_col + dtb)
    g_row = jnp.sum(jnp.where(eye, g, 0.0), axis=1, keepdims=True)
    beta_row = jnp.sum(jnp.where(eye, beta, 0.0), axis=1, keepdims=True)
    gc_col = jnp.sum(jnp.where(incl, g_row, 0.0), axis=2, keepdims=True)
    gc_row = jnp.sum(jnp.where(incl_t, g, 0.0), axis=1, keepdims=True)
    dec = jnp.where(incl, jnp.exp(jnp.where(incl, gc_col - gc_row, 0.0)), 0.0)
    kk = dot(kn, kn, "NT", _GDN_PASSES["qk"])
    qk = dot(qn, kn, "NT", _GDN_PASSES["qk"])
    lmat = jnp.where(strict, dec * kk * beta_row, 0.0)
    attn = dec * qk * beta_row
    gam = jnp.exp(gc_col)
    gc_last = gc_col[:, C - 1:C, :]
    k_end = kn * (jnp.exp(gc_last - gc_col) * beta)
    return lmat, cv, gam * kn, gam * qn, attn, k_end, jnp.exp(gc_last)


def _tri_inv(lmat):
    C = lmat.shape[1]
    eye = _tri_masks(C)[0]
    ps = _GDN_PASSES["inv"]
    p = jnp.where(eye, 1.0, 0.0) - lmat
    lp = _bdot_raw(lmat, lmat, "NN", ps)
    n = int(math.log2(C))
    for s in range(1, n):
        p = p + _bdot_raw(p, lp, "NN", ps)
        if s < n - 1:
            lp = _bdot_raw(lp, lp, "NN", ps)
    return p


def _gated_norm(o, z, gnw):
    on = o * lax.rsqrt(jnp.mean(o * o, axis=-1, keepdims=True) + EPS) * gnw
    return on * _silu(z)


GDN_PG = 4
GDN_SG = 4


def _gdn_pairs(c, ba, gp, G):
    C, W, H = GDN_CHUNK, GDN_WIDTH, GDN_HEADS
    pairs = [(j, h) for j in range(G) for h in range(H)]
    cq, ck, cv = (jnp.stack([c[C * j:C * (j + 1), o + GDN_DIM * h:o + GDN_DIM * (h + 1)] for j, h in pairs]) for o in (0, W, 2 * W))
    b_col = jnp.stack([ba[C * j:C * (j + 1), h:h + 1] for j, h in pairs])
    a_col = jnp.stack([ba[C * j:C * (j + 1), H + h:H + h + 1] for j, h in pairs])
    alog = jnp.stack([gp[0:1, h:h + 1] for j, h in pairs])
    dtb = jnp.stack([gp[0:1, H + h:H + h + 1] for j, h in pairs])
    return pairs, (cq, ck, cv, b_col, a_col, alog, dtb)


def _gdn_pre_specs(S, G):
    C = GDN_CHUNK
    T = C * G
    return dict(
        cur=pl.BlockSpec((T, 3 * GDN_WIDTH), lambda i: (i, 0)),
        prev=pl.BlockSpec((8, 3 * GDN_WIDTH), lambda i: (jnp.maximum(i * (T // 8) - 1, 0), 0)),
        ba=pl.BlockSpec((T, 128), lambda i: (i, P_BA // 128)),
        cw=pl.BlockSpec((GDN_CONV, 3 * GDN_WIDTH), lambda i: (0, 0)),
        vec=pl.BlockSpec((1, 128), lambda i: (0, 0)),
        hd=pl.BlockSpec((GDN_HEADS, T, GDN_DIM), lambda i: (0, i, 0)),
        hc=pl.BlockSpec((GDN_HEADS, T, C), lambda i: (0, i, 0)),
        ge=pl.BlockSpec((G, GDN_HEADS, 8, 128), lambda i: (i, 0, 0, 0)),
    )


def _hd_shape(S, last=GDN_DIM):
    return jax.ShapeDtypeStruct((GDN_HEADS, S, last), F32)


def _gdn_pre(proj, conv_w, gp):
    S = proj.shape[0]
    C, G = GDN_CHUNK, GDN_PG
    nc = S // C
    sp = _gdn_pre_specs(S, G)

    def body(cur_ref, prev_ref, ba_ref, cw_ref, gp_ref, uv_ref, wk_ref, qd_ref, ke_ref, at_ref, ti_ref, ge_ref):
        prev = prev_ref[...] * jnp.where(pl.program_id(0) == 0, 0.0, 1.0)
        c = _silu(_conv_taps(jnp.concatenate([prev, cur_ref[...]], axis=0), cw_ref[...], GDN_CONV, C * G))
        pairs, args = _gdn_pairs(c, ba_ref[...], gp_ref[...], G)
        lmat, v, rk, q_dec, attn, k_end, g_end = _gdn_stage1(*args)
        t = _tri_inv(lmat)
        u_v = _bdot_raw(t, v, "NN", _GDN_PASSES["sol"])
        w_k = _bdot_raw(t, rk, "NN", _GDN_PASSES["sol"])
        for b, (j, h) in enumerate(pairs):
            rows = slice(C * j, C * (j + 1))
            uv_ref[h, rows, :] = u_v[b]
            wk_ref[h, rows, :] = w_k[b]
            qd_ref[h, rows, :] = q_dec[b]
            ke_ref[h, rows, :] = k_end[b]
            at_ref[h, rows, :] = attn[b]
            ti_ref[h, rows, :] = t[b]
            ge_ref[j, h] = jnp.broadcast_to(g_end[b], (8, 128))

    return pl.pallas_call(
        body, name="gdn_pre", grid=(nc // G,),
        in_specs=[sp["cur"], sp["prev"], sp["ba"], sp["cw"], sp["vec"]],
        out_specs=(sp["hd"], sp["hd"], sp["hd"], sp["hd"], sp["hc"], sp["hc"], sp["ge"]),
        out_shape=(_hd_shape(S), _hd_shape(S), _hd_shape(S), _hd_shape(S), _hd_shape(S, C), _hd_shape(S, C),
                   jax.ShapeDtypeStruct((nc, GDN_HEADS, 8, 128), F32)),
        compiler_params=_cparams(("parallel",)),
    )(proj, proj, proj, conv_w, gp)


def _gdn_scan_specs(S, G, rev):
    C = GDN_CHUNK
    T = C * G
    n = S // T
    ci = (lambda i: n - 1 - i) if rev else (lambda i: i)
    return dict(
        hd=pl.BlockSpec((GDN_HEADS, T, GDN_DIM), lambda i: (0, ci(i), 0)),
        hc=pl.BlockSpec((GDN_HEADS, T, C), lambda i: (0, ci(i), 0)),
        ge=pl.BlockSpec((G, GDN_HEADS, 8, 128), lambda i: (ci(i), 0, 0, 0)),
        z=pl.BlockSpec((T, GDN_WIDTH), lambda i: (ci(i), P_Z // GDN_WIDTH)),
        oa=pl.BlockSpec((T, GDN_WIDTH), lambda i: (ci(i), 0)),
        vec=pl.BlockSpec((1, 128), lambda i: (0, 0)),
        st=pl.BlockSpec((G, GDN_HEADS, GDN_DIM, GDN_DIM), lambda i: (ci(i), 0, 0, 0)),
    )


def _gdn_scan(u_v, w_k, q_dec, k_end, attn, g_end, proj, gnw, mix, after=()):
    S = proj.shape[0]
    C, G = GDN_CHUNK, GDN_SG
    nc = S // C
    sp = _gdn_scan_specs(S, G, False)
    ps = _GDN_PASSES["scan"]

    def body(uv_ref, wk_ref, qd_ref, ke_ref, at_ref, ge_ref, z_ref, gnw_ref, *rest):
        oa_ref, st_ref, s_scr = rest[-3:]

        @pl.when(pl.program_id(0) == 0)
        def _():
            s_scr[...] = jnp.zeros_like(s_scr)

        for j in range(G):
            rows = slice(C * j, C * (j + 1))
            st = s_scr[...]
            st_ref[j] = st
            u = uv_ref[:, rows, :] - _bdot_raw(wk_ref[:, rows, :], st, "NN", ps)
            o = _bdot_raw(qd_ref[:, rows, :], st, "NN", ps) + _bdot_raw(at_ref[:, rows, :], u, "NN", ps)
            s_scr[...] = ge_ref[j][:, 0:1, 0:1] * st + _bdot_raw(ke_ref[:, rows, :], u, "TN", ps)
            for h in range(GDN_HEADS):
                cols = slice(GDN_DIM * h, GDN_DIM * (h + 1))
                oa_ref[rows, cols] = _gated_norm(o[h], z_ref[rows, cols], gnw_ref[...])

    return pl.pallas_call(
        body, name="gdn_scan", grid=(nc // G,),
        in_specs=[sp["hd"], sp["hd"], sp["hd"], sp["hd"], sp["hc"], sp["ge"], sp["z"], sp["vec"]] + [_ANY] * (1 + len(after)),
        out_specs=(sp["oa"], sp["st"]),
        out_shape=(jax.ShapeDtypeStruct(mix.shape, F32),
                   jax.ShapeDtypeStruct((nc, GDN_HEADS, GDN_DIM, GDN_DIM), F32)),
        input_output_aliases={8: 0},
        scratch_shapes=[pltpu.VMEM((GDN_HEADS, GDN_DIM, GDN_DIM), F32)],
        compiler_params=_cparams(("arbitrary",)),
    )(u_v, w_k, q_dec, k_end, attn, g_end, proj, gnw, mix, *after)


def _gdn_scan_bwd(u_v, w_k, q_dec, k_end, attn, g_end, proj, gnw, states, d_oa):
    S = proj.shape[0]
    C, G = GDN_CHUNK, GDN_SG
    nc = S // C
    sp = _gdn_scan_specs(S, G, True)
    ps, pb = _GDN_PASSES["scan"], _GDN_PASSES["bwd"]

    def body(uv_ref, wk_ref, qd_ref, ke_ref, at_ref, ge_ref, z_ref, gnw_ref, st_ref, doa_ref,
             duv_ref, dwk_ref, dqd_ref, dke_ref, dat_ref, dge_ref, dz_ref, dgnw_ref, ds_scr):
        @pl.when(pl.program_id(0) == 0)
        def _():
            ds_scr[...] = jnp.zeros_like(ds_scr)
            dgnw_ref[...] = jnp.zeros_like(dgnw_ref)

        dgnw = jnp.zeros((1, 128), F32)
        for j in reversed(range(G)):
            rows = slice(C * j, C * (j + 1))
            st = st_ref[j]
            wk, qd, ke, at = wk_ref[:, rows, :], qd_ref[:, rows, :], ke_ref[:, rows, :], at_ref[:, rows, :]
            u = uv_ref[:, rows, :] - _bdot_raw(wk, st, "NN", ps)
            o = _bdot_raw(qd, st, "NN", ps) + _bdot_raw(at, u, "NN", ps)
            dos = []
            for h in range(GDN_HEADS):
                cols = slice(GDN_DIM * h, GDN_DIM * (h + 1))
                _, vjp2 = jax.vjp(_gated_norm, o[h], z_ref[rows, cols], gnw_ref[...])
                do_h, dz_h, dgn = vjp2(doa_ref[rows, cols])
                dz_ref[rows, cols] = dz_h
                dgnw = dgnw + dgn
                dos.append(do_h)
            do = jnp.stack(dos)
            ds_new = ds_scr[...]
            du = _bdot_raw(at, do, "TN", pb) + _bdot_raw(ke, ds_new, "NN", pb)
            duv_ref[:, rows, :] = du
            dat_ref[:, rows, :] = _bdot_raw(do, u, "NT", pb)
            dqd_ref[:, rows, :] = _bdot_raw(do, st, "NT", pb)
            dke_ref[:, rows, :] = _bdot_raw(u, ds_new, "NT", pb)
            dwk_ref[:, rows, :] = -_bdot_raw(du, st, "NT", pb)
            d_ge = jnp.sum(jnp.sum(st * ds_new, axis=2, keepdims=True), axis=1, keepdims=True)
            dge_ref[j] = jnp.broadcast_to(d_ge, (GDN_HEADS, 8, 128))
            ds_scr[...] = ge_ref[j][:, 0:1, 0:1] * ds_new + _bdot_raw(qd, do, "TN", pb) - _bdot_raw(wk, du, "TN", pb)
        dgnw_ref[...] += dgnw

    return pl.pallas_call(
        body, name="gdn_scan_bwd", grid=(nc // G,),
        in_specs=[sp["hd"], sp["hd"], sp["hd"], sp["hd"], sp["hc"], sp["ge"], sp["z"], sp["vec"], sp["st"], sp["oa"]],
        out_specs=(sp["hd"], sp["hd"], sp["hd"], sp["hd"], sp["hc"], sp["ge"], sp["oa"], sp["vec"]),
        out_shape=(_hd_shape(S), _hd_shape(S), _hd_shape(S), _hd_shape(S), _hd_shape(S, C),
                   jax.ShapeDtypeStruct((nc, GDN_HEADS, 8, 128), F32), jax.ShapeDtypeStruct((S, GDN_WIDTH), F32),
                   jax.ShapeDtypeStruct((1, 128), F32)),
        scratch_shapes=[pltpu.VMEM((GDN_HEADS, GDN_DIM, GDN_DIM), F32)],
        compiler_params=_cparams(("arbitrary",)),
    )(u_v, w_k, q_dec, k_end, attn, g_end, proj, gnw, states, d_oa)


def _gdn_post(proj, conv_w, gp, tinv, u_v, w_k, d_uv, d_wk, d_qd, d_ke, d_at, d_ge):
    S = proj.shape[0]
    C, G = GDN_CHUNK, GDN_PG
    nc = S // C
    sp = _gdn_pre_specs(S, G)
    pb = _GDN_PASSES["bwd"]

    def body(cur_ref, prev_ref, ba_ref, cw_ref, gp_ref, ti_ref, uv_ref, wk_ref, duv_ref, dwk_ref, dqd_ref, dke_ref,
             dat_ref, dge_ref, dpre_ref, dba_ref, dgp_ref):
        i = pl.program_id(0)

        @pl.when(i == 0)
        def _():
            dgp_ref[...] = jnp.zeros_like(dgp_ref)

        prev = prev_ref[...] * jnp.where(i == 0, 0.0, 1.0)
        pre = _conv_taps(jnp.concatenate([prev, cur_ref[...]], axis=0), cw_ref[...], GDN_CONV, C * G)
        sg = jax.nn.sigmoid(pre)
        dsilu = sg * (1.0 + pre * (1.0 - sg))
        pairs, args = _gdn_pairs(pre * sg, ba_ref[...], gp_ref[...], G)
        _, vjp1 = jax.vjp(functools.partial(_gdn_stage1, dot=_bdot), *args)

        def take(ref):
            return jnp.stack([ref[h, C * j:C * (j + 1), :] for j, h in pairs])

        t, u_v, w_k = take(ti_ref), take(uv_ref), take(wk_ref)
        d_v = _bdot_raw(t, take(duv_ref), "TN", pb)
        d_rk = _bdot_raw(t, take(dwk_ref), "TN", pb)
        d_l = -(_bdot_raw(d_v, u_v, "NT", pb) + _bdot_raw(d_rk, w_k, "NT", pb))
        d_ge = jnp.stack([dge_ref[j, h][0:1, 0:1] for j, h in pairs])
        dcq, dck, dcv, db, da, dalog, ddtb = vjp1((d_l, d_v, d_rk, take(dqd_ref), take(dat_ref), take(dke_ref), d_ge))
        lane = lax.broadcasted_iota(jnp.int32, (C, 128), 1)
        lane1 = lax.broadcasted_iota(jnp.int32, (1, 128), 1)
        dgp = jnp.zeros((1, 128), F32)
        for j in range(G):
            rows = slice(C * j, C * (j + 1))
            dba = jnp.zeros((C, 128), F32)
            for h in range(GDN_HEADS):
                b = GDN_HEADS * j + h
                for o_, dcx in ((0, dcq), (GDN_WIDTH, dck), (2 * GDN_WIDTH, dcv)):
                    cols = slice(o_ + GDN_DIM * h, o_ + GDN_DIM * (h + 1))
                    dpre_ref[rows, cols] = dcx[b] * dsilu[rows, cols]
                dba = dba + jnp.where(lane == h, db[b], 0.0) + jnp.where(lane == GDN_HEADS + h, da[b], 0.0)
                dgp = dgp + jnp.where(lane1 == h, dalog[b], 0.0) + jnp.where(lane1 == GDN_HEADS + h, ddtb[b], 0.0)
            dba_ref[rows, :] = dba
        dgp_ref[0:1, :] += dgp

    T = C * G
    return pl.pallas_call(
        body, name="gdn_post", grid=(nc // G,),
        in_specs=[sp["cur"], sp["prev"], sp["ba"], sp["cw"], sp["vec"], sp["hc"], sp["hd"], sp["hd"], sp["hd"], sp["hd"],
                  sp["hd"], sp["hd"], sp["hc"], sp["ge"]],
        out_specs=(sp["cur"], pl.BlockSpec((T, 128), lambda i: (i, 0)), pl.BlockSpec((8, 128), lambda i: (0, 0))),
        out_shape=(jax.ShapeDtypeStruct((S, 3 * GDN_WIDTH), F32), jax.ShapeDtypeStruct((S, 128), F32),
                   jax.ShapeDtypeStruct((8, 128), F32)),
        compiler_params=_cparams(("arbitrary",)),
    )(proj, proj, proj, conv_w, gp, tinv, u_v, w_k, d_uv, d_wk, d_qd, d_ke, d_at, d_ge)


def _conv_bwd(dpre, x, xcol0, w, K, name, tc):
    S, Cc = dpre.shape
    T = _pick_tile(S, 256)
    nt, ncol = S // T, Cc // tc
    xo = xcol0 // tc

    def body(d_ref, dn_ref, x_ref, xp_ref, w_ref, dx_ref, dw_ref):
        i = pl.program_id(1)
        dn = dn_ref[...] * jnp.where(i == nt - 1, 0.0, 1.0)
        dv = d_ref[...]
        ext_d = jnp.concatenate([dv, dn], axis=0)
        dx_ref[...] = _conv_taps_t(ext_d, w_ref[...], K, T).astype(dx_ref.dtype)
        xp = xp_ref[...] * jnp.where(i == 0, 0.0, 1.0)
        ext_x = jnp.concatenate([xp, x_ref[...]], axis=0)

        @pl.when(i == 0)
        def _():
            dw_ref[...] = jnp.zeros_like(dw_ref)

        for k in range(K):
            dw_ref[k:k + 1, :] += jnp.sum(dv * _shifted(ext_x, (K - 1) - k, 8, T), axis=0, keepdims=True)

    r8 = T // 8
    return pl.pallas_call(
        body, name=name, grid=(ncol, nt),
        in_specs=[pl.BlockSpec((T, tc), lambda j, i: (i, j)),
                  pl.BlockSpec((8, tc), lambda j, i: (jnp.minimum((i + 1) * r8, S // 8 - 1), j)),
                  pl.BlockSpec((T, tc), lambda j, i: (i, j + xo)),
                  pl.BlockSpec((8, tc), lambda j, i: (jnp.maximum(i * r8 - 1, 0), j + xo)),
                  pl.BlockSpec((K, tc), lambda j, i: (0, j))],
        out_specs=(pl.BlockSpec((T, tc), lambda j, i: (i, j)), pl.BlockSpec((K, tc), lambda j, i: (0, j))),
        out_shape=(jax.ShapeDtypeStruct((S, Cc), _MXU), jax.ShapeDtypeStruct((K, Cc), F32)),
        compiler_params=_cparams(("parallel", "arbitrary")),
    )(dpre, dpre, x, x, w)


def _dil_bias(nt, T):
    d = (np.arange(nt)[:, None, None] * T + np.arange(T)[None, None, :] - np.arange(T)[None, :, None])
    cnt = ((d >= 0) & (d <= 128)).astype(np.float64) + ((d >= 0) & (d % 4 == 0) & (d <= 512)) + ((d >= 0) & (d % 16 == 0))
    return jnp.asarray(np.where(cnt > 0, np.log(np.maximum(cnt, 1.0)), -1e30), dtype=F32)


def _attn_fwd(proj, after=()):
    S = proj.shape[0]
    T = min(ATT_T, S)
    nt, H = S // T, T // 2
    bias = _dil_bias(nt, T)
    scale = DIL_DIM ** -0.5
    npair = DIL_WIDTH // 128
    qb0, kb0, vb0 = P_QKVB // 128, (P_QKVB + DIL_WIDTH) // 128, (P_QKVB + 2 * DIL_WIDTH) // 128

    def body(q_ref, k_ref, v_ref, b_ref, *rest):
        o_ref, lse_ref = rest[-2:]
        i = pl.program_id(1)
        qs = (q_ref[...] * scale).astype(_MXU)

        def update(carry, kt, vt, qt, bt):
            out = []
            for hh in range(2):
                m, l, acc = carry[hh]
                sl = slice(hh * DIL_DIM, (hh + 1) * DIL_DIM)
                s = lax.dot_general(kt[:, sl], qt[:, sl], (_NT, ((), ())), preferred_element_type=F32) + bt
                m_new = jnp.maximum(m, jnp.max(s, axis=0, keepdims=True))
                p = jnp.exp(s - m_new)
                a = jnp.exp(m - m_new)
                l = a * l + jnp.sum(p, axis=0, keepdims=True)
                acc = a * acc + lax.dot_general(vt[:, sl], p.astype(_MXU), (_TN, ((), ())), preferred_element_type=F32)
                out.append((m_new, l, acc))
            return tuple(out)

        def keys(j):
            rows = pl.ds(pl.multiple_of(j * T, T), T)
            return k_ref[rows, :].astype(_MXU), v_ref[rows, :].astype(_MXU)

        init = tuple((jnp.full((1, T), -1e30, F32), jnp.zeros((1, T), F32), jnp.zeros((DIL_DIM, T), F32)) for _ in range(2))
        res = lax.fori_loop(0, i, lambda j, carry: update(carry, *keys(j), qs, b_ref[i - j]), init)
        kd, vd = keys(i)
        res = update(res, kd[:H], vd[:H], qs, b_ref[0, :H, :])
        late = update(tuple(tuple(t[:, H:] for t in r) for r in res), kd[H:], vd[H:], qs[H:], b_ref[0, H:, H:])
        res = tuple(tuple(jnp.concatenate([t[:, :H], u], axis=1) for t, u in zip(r, r2)) for r, r2 in zip(res, late))
        lse_ref[...] = jnp.zeros_like(lse_ref)
        for hh in range(2):
            m, l, acc = res[hh]
            o_ref[:, hh * DIL_DIM:(hh + 1) * DIL_DIM] = (acc / l).T
            lse_ref[hh:hh + 1, :] = m + jnp.log(l)

    return pl.pallas_call(
        body, name="attn_fwd", grid=(npair, nt),
        in_specs=[pl.BlockSpec((T, 128), lambda p, i: (i, qb0 + p)),
                  pl.BlockSpec((S, 128), lambda p, i: (0, kb0 + p)),
                  pl.BlockSpec((S, 128), lambda p, i: (0, vb0 + p)),
                  pl.BlockSpec((nt, T, T), lambda p, i: (0, 0, 0))] + [_ANY] * len(after),
        out_specs=(pl.BlockSpec((T, 128), lambda p, i: (i, GDN_WIDTH // 128 + p)),
                   pl.BlockSpec((None, None, 8, T), lambda p, i: (p, i, 0, 0))),
        out_shape=(jax.ShapeDtypeStruct((S, GDN_WIDTH + DIL_WIDTH), F32), jax.ShapeDtypeStruct((npair, nt, 8, T), F32)),
        compiler_params=_cparams(("parallel", "parallel")),
    )(proj, proj, proj, bias, *after)


def _attn_bwd(proj, mix, lse, d_mix):
    S = proj.shape[0]
    T = min(ATT_T, S)
    nt, H = S // T, T // 2
    bias = _dil_bias(nt, T)
    scale = DIL_DIM ** -0.5
    npair = DIL_WIDTH // 128
    qb0, kb0, vb0 = P_QKVB // 128, (P_QKVB + DIL_WIDTH) // 128, (P_QKVB + 2 * DIL_WIDTH) // 128

    def body(q_ref, k_ref, v_ref, o_ref, lse_ref, do_ref, b_ref, dq_ref, dk_ref, dv_ref, dq_scr):
        j = pl.program_id(1)

        @pl.when(j == 0)
        def _():
            dq_scr[...] = jnp.zeros_like(dq_scr)

        kt = k_ref[...].astype(_MXU)
        vt = v_ref[...].astype(_MXU)
        ones = jnp.ones((8, DIL_DIM), F32)

        def block(carry, kt, vt, rows, lsev, bt):
            qs = (q_ref[rows, :] * scale).astype(_MXU)
            dov = do_ref[rows, :]
            prod = dov * o_ref[rows, :]
            dob = dov.astype(_MXU)
            out = []
            dqs = []
            for hh in range(2):
                dk, dv = carry[hh]
                sl = slice(hh * DIL_DIM, (hh + 1) * DIL_DIM)
                s = lax.dot_general(kt[:, sl], qs[:, sl], (_NT, ((), ())), preferred_element_type=F32) + bt
                p = jnp.exp(s - lsev[hh:hh + 1, :])
                delta = lax.dot_general(ones, prod[:, sl], (_NT, ((), ())), precision=_HI, preferred_element_type=F32)[0:1, :]
                dp = lax.dot_general(vt[:, sl], dob[:, sl], (_NT, ((), ())), preferred_element_type=F32)
                ds = (p * (dp - delta)).astype(_MXU)
                dv = dv + lax.dot_general(p.astype(_MXU), dob[:, sl], (_NN, ((), ())), preferred_element_type=F32)
                dk = dk + lax.dot_general(ds, qs[:, sl], (_NN, ((), ())), preferred_element_type=F32)
                dqs.append(lax.dot_general(ds, kt[:, sl], (_TN, ((), ())), preferred_element_type=F32) * scale)
                out.append((dk, dv))
            dq_scr[rows, :] += jnp.concatenate(dqs, axis=1)
            return tuple(out)

        def step(i, carry):
            return block(carry, kt, vt, pl.ds(pl.multiple_of(i * T, T), T), lse_ref[i], b_ref[i - j])

        zeros = tuple((jnp.zeros((H, DIL_DIM), F32), jnp.zeros((H, DIL_DIM), F32)) for _ in range(2))
        lsed = lse_ref[j]
        early = block(zeros, kt[:H], vt[:H], pl.ds(pl.multiple_of(j * T, T), T), lsed, b_ref[0, :H, :])
        late = block(zeros, kt[H:], vt[H:], pl.ds(pl.multiple_of(j * T + H, H), H), lsed[:, H:], b_ref[0, H:, H:])
        init = tuple(tuple(jnp.concatenate([t, u], axis=0) for t, u in zip(r, r2)) for r, r2 in zip(early, late))
        res = lax.fori_loop(j + 1, nt, step, init)
        dk_ref[...] = jnp.concatenate([res[0][0], res[1][0]], axis=1).astype(dk_ref.dtype)
        dv_ref[...] = jnp.concatenate([res[0][1], res[1][1]], axis=1).astype(dv_ref.dtype)

        @pl.when(j == nt - 1)
        def _():
            dq_ref[...] = dq_scr[...].astype(dq_ref.dtype)

    full = lambda c0: pl.BlockSpec((S, 128), lambda p, j: (0, c0 + p))
    tile = lambda c0: pl.BlockSpec((T, 128), lambda p, j: (j, c0 + p))
    out3 = jax.ShapeDtypeStruct((S, DIL_WIDTH), _MXU)
    return pl.pallas_call(
        body, name="attn_bwd", grid=(npair, nt),
        in_specs=[full(qb0), tile(kb0), tile(vb0), full(GDN_WIDTH // 128),
                  pl.BlockSpec((None, nt, 8, T), lambda p, j: (p, 0, 0, 0)), full(GDN_WIDTH // 128),
                  pl.BlockSpec((nt, T, T), lambda p, j: (0, 0, 0))],
        out_specs=(full(0), tile(0), tile(0)),
        out_shape=(out3, out3, out3),
        scratch_shapes=[pltpu.VMEM((S, 128), F32)],
        compiler_params=_cparams(("parallel", "arbitrary")),
    )(proj, proj, proj, mix, lse, d_mix, bias)


def _ffn_act(up, cw):
    S, Cc = up.shape[0], up.shape[1] // 2
    T, tc = _pick_tile(S, 256), _pick_tile(Cc, 1536)
    r16 = T // 16
    nct = Cc // tc

    def body(g_ref, gp_ref, u_ref, up_ref, wg_ref, wu_ref, o_ref):
        keep = jnp.where(pl.program_id(1) == 0, 0.0, 1.0)
        cg = _conv_taps(jnp.concatenate([gp_ref[8:16, :].astype(F32) * keep, g_ref[...].astype(F32)], axis=0),
                        wg_ref[...], FFN_CONV, T)
        cu = _conv_taps(jnp.concatenate([up_ref[8:16, :].astype(F32) * keep, u_ref[...].astype(F32)], axis=0),
                        wu_ref[...], FFN_CONV, T)
        o_ref[...] = (_silu(cg) * cu).astype(o_ref.dtype)

    cur = lambda o: pl.BlockSpec((T, tc), lambda j, i: (i, j + o))
    prev = lambda o: pl.BlockSpec((16, tc), lambda j, i: (jnp.maximum(i * r16 - 1, 0), j + o))
    wsp = lambda o: pl.BlockSpec((FFN_CONV, tc), lambda j, i: (0, j + o))
    return pl.pallas_call(
        body, name="ffn_act", grid=(nct, S // T),
        in_specs=[cur(0), prev(0), cur(nct), prev(nct), wsp(0), wsp(nct)], out_specs=cur(0),
        out_shape=jax.ShapeDtypeStruct((S, Cc), _MXU),
        compiler_params=_cparams(("parallel", "parallel")),
    )(up, up, up, up, cw, cw)


def _ffn_act_bwd(d_act, up, cw):
    S, Cc = up.shape[0], up.shape[1] // 2
    T, tc = _pick_tile(S, 256), _pick_tile(Cc, 1536)
    r8, r16 = T // 8, T // 16
    nt = S // T
    nct = Cc // tc
    K = FFN_CONV

    def body(da_ref, dan_ref, g_ref, gp_ref, gn_ref, u_ref, up_ref, un_ref, wg_ref, wu_ref,
             dg_ref, du_ref, dwg_ref, dwu_ref):
        i = pl.program_id(1)
        keep_p = jnp.where(i == 0, 0.0, 1.0)
        keep_n = jnp.where(i == nt - 1, 0.0, 1.0)
        wg, wu = wg_ref[...], wu_ref[...]
        xg = jnp.concatenate([gp_ref[8:16, :].astype(F32) * keep_p, g_ref[...].astype(F32),
                              gn_ref[0:8, :].astype(F32) * keep_n], axis=0)
        xu = jnp.concatenate([up_ref[8:16, :].astype(F32) * keep_p, u_ref[...].astype(F32),
                              un_ref[0:8, :].astype(F32) * keep_n], axis=0)
        cg = _conv_taps(xg, wg, K, T + 8)
        cu = _conv_taps(xu, wu, K, T + 8)
        da = jnp.concatenate([da_ref[...], dan_ref[...] * keep_n], axis=0)
        sg = jax.nn.sigmoid(cg)
        d_cg = da * cu * (sg * (1.0 + cg * (1.0 - sg)))
        d_cu = da * (cg * sg)
        dg_ref[...] = _conv_taps_t(d_cg, wg, K, T).astype(dg_ref.dtype)
        du_ref[...] = _conv_taps_t(d_cu, wu, K, T).astype(du_ref.dtype)

        @pl.when(i == 0)
        def _():
            dwg_ref[...] = jnp.zeros_like(dwg_ref)
            dwu_ref[...] = jnp.zeros_like(dwu_ref)

        for k in range(K):
            dwg_ref[k:k + 1, :] += jnp.sum(d_cg[0:T, :] * _shifted(xg, (K - 1) - k, 8, T), axis=0, keepdims=True)
            dwu_ref[k:k + 1, :] += jnp.sum(d_cu[0:T, :] * _shifted(xu, (K - 1) - k, 8, T), axis=0, keepdims=True)

    cur = lambda o: pl.BlockSpec((T, tc), lambda j, i: (i, j + o))
    prev = lambda o: pl.BlockSpec((16, tc), lambda j, i: (jnp.maximum(i * r16 - 1, 0), j + o))
    nxt = lambda o: pl.BlockSpec((16, tc), lambda j, i: (jnp.minimum((i + 1) * r16, S // 16 - 1), j + o))
    nxt8 = pl.BlockSpec((8, tc), lambda j, i: (jnp.minimum((i + 1) * r8, S // 8 - 1), j))
    wsp = lambda o: pl.BlockSpec((K, tc), lambda j, i: (0, j + o))
    return pl.pallas_call(
        body, name="ffn_act_bwd", grid=(nct, nt),
        in_specs=[cur(0), nxt8, cur(0), prev(0), nxt(0), cur(nct), prev(nct), nxt(nct), wsp(0), wsp(nct)],
        out_specs=(cur(0), cur(0), wsp(0), wsp(0)),
        out_shape=(jax.ShapeDtypeStruct((S, Cc), _MXU), jax.ShapeDtypeStruct((S, Cc), _MXU),
                   jax.ShapeDtypeStruct((K, Cc), F32), jax.ShapeDtypeStruct((K, Cc), F32)),
        compiler_params=_cparams(("parallel", "arbitrary")),
    )(d_act, d_act, up, up, up, up, up, up, cw, cw)


def _local_step(x, tgt, h1, n1w, n2w, fnw, gp, gnw, wp, conv_w, fcw, rest_weights, early_grads):
    proj = _mm(h1, wp, "nn", name="proj")
    u_v, w_k, q_dec, k_end, attn, tinv, g_end = _gdn_pre(proj, conv_w, gp)
    mix, lse = _attn_fwd(proj)
    mix, states = _gdn_scan(u_v, w_k, q_dec, k_end, attn, g_end, proj, gnw, mix, after=[rest_weights[0]([mix])])
    w_out, w_up4, w_down = rest_weights[1]([mix])
    x2, h2 = _mm(mix, w_out, "nn", residual=x, name="outproj", normed_by=n2w)
    up = _mm(h2, w_up4, "nn", b_blocks=True, out_dtype=_MXU, name="up")
    act = _ffn_act(up, fcw)
    x3 = _mm(act, w_down, "nn", residual=x2, name="down")
    loss, dx3, dx3n, d_fnw = _loss_head(x3, fnw, tgt, "loss_head")
    d_act = _mm(dx3n, w_down, "nt", name="d_act")
    d_wdown = _mm(act, dx3n, "tn", name="d_wdown")
    d_upg, d_upu, d_fcwg, d_fcwu = _ffn_act_bwd(d_act, up, fcw)
    d_wup = _mm(h2, d_upg, "tn", place=("blocks", N_CHIPS, 0), tn=w_up4.shape[2], name="d_wgate")
    d_wup = _mm(h2, d_upu, "tn", place=("blocks", N_CHIPS, N_CHIPS // 2), tn=w_up4.shape[2], into=d_wup, name="d_wup")
    dx2, d_n2w = _rmsnorm_bwd(([d_upg, d_upu], w_up4), x2, n2w, dx3, "norm2_bwd")
    d_wout = _mm(mix, dx2, "tn", name="d_wout")
    token = early_grads[0](d_wup, d_wdown, d_wout)
    d_mix = _mm(dx2, w_out, "nt", name="d_mix", after=[token])
    dq_b, dk_b, dv_b = _attn_bwd(proj, mix, lse, d_mix)
    d_uv, d_wk, d_qd, d_ke, d_at, d_ge, d_z, d_gnw = _gdn_scan_bwd(u_v, w_k, q_dec, k_end, attn, g_end, proj,
                                                                   gnw, states, d_mix)
    d_pre, d_ba, d_gp = _gdn_post(proj, conv_w, gp, tinv, u_v, w_k, d_uv, d_wk, d_qd, d_ke, d_at, d_ge)
    token = early_grads[1]([d_pre])
    d_qkva, d_convw = _conv_bwd(d_pre, proj, 0, conv_w + token[0:1, 0:1], GDN_CONV, "gdn_conv_bwd", 512)
    d_proj = jnp.concatenate([d_qkva, d_z.astype(_MXU), dq_b, dk_b, dv_b, d_ba.astype(_MXU),
                              jnp.zeros((x.shape[0], P_COLS - P_BA - 128), _MXU)], axis=1)
    d_wp = _mm(h1, d_proj, "tn", name="d_wp")
    token = early_grads[2](d_wp)
    dx, d_n1w = _rmsnorm_bwd(([d_proj], wp[None]), x, n1w, dx2, "norm1_bwd", after=[token])
    grads = dict(wp=d_wp, conv_w=d_convw, w_out=d_wout, w_up=d_wup, fcw_g=d_fcwg, fcw_u=d_fcwu, w_down=d_wdown,
                 n1w=d_n1w, n2w=d_n2w, fnw=d_fnw, gp=d_gp, gnw=d_gnw)
    return loss, dx, grads


_HBM = pl.BlockSpec(memory_space=pltpu.HBM)


def _pos():
    return lax.axis_index("x"), lax.axis_index("y"), lax.axis_index("c")


def _other_chips(x, y):
    return [(1 - x, y), (x, 1 - y), (1 - x, 1 - y)]


def _halvable(shape):
    return shape[0] % 32 == 0


def _rows_of_half(shape, half):
    if not _halvable(shape):
        return pl.ds(0, shape[0])
    return pl.ds(pl.multiple_of(half * (shape[0] // 2), 16), shape[0] // 2)


_SEM = pl.BlockSpec(memory_space=pltpu.SEMAPHORE)
_ANY = pl.BlockSpec(memory_space=pl.ANY)
_DATAFLOW = pltpu.SideEffectType.DATAFLOW_SIDE_EFFECTING


def _in_hbm(a):
    return pltpu.with_memory_space_constraint(a, pltpu.HBM)


def _halves_copy(src_refs, land_refs, send_sems, recv_sems, shapes, a, j, block, x, y, c):
    px, py = _other_chips(x, y)[j]
    rows = _rows_of_half(shapes[a], c)
    return pltpu.make_async_remote_copy(
        src_ref=src_refs[a].at[rows, :], dst_ref=land_refs[a].at[block, rows, :], send_sem=send_sems.at[3 * a + j],
        recv_sem=recv_sems.at[3 * a + j], device_id=(px, py, c), device_id_type=MESH)


def _gather_halves_start(shards, after, name):
    n = len(shards)
    shapes = [s.shape for s in shards]

    def body(*refs):
        ins, lands = refs[:n], refs[n:2 * n]
        send_sems, recv_sems = refs[2 * n + 1], refs[2 * n + 2]
        token = refs[-1]
        x, y, c = _pos()
        q = 2 * x + y
        for a in range(n):
            for j in range(3):
                _halves_copy(ins, lands, send_sems, recv_sems, shapes, a, j, q, x, y, c).start()
        token[...] = jnp.zeros_like(token)

    land_shapes = [(N_CHIPS,) + s.shape for s in shards]
    return pl.pallas_call(
        body, name=name,
        out_shape=(pltpu.SemaphoreType.DMA((3 * n,)), pltpu.SemaphoreType.DMA((3 * n,)),
                   *[pltpu.HBM(s.shape, s.dtype) for s in shards],
                   *[pltpu.HBM(ls, s.dtype) for ls, s in zip(land_shapes, shards)],
                   jax.ShapeDtypeStruct((8, 128), F32)),
        in_specs=[_HBM] * (2 * n) + [_ANY],
        out_specs=(_SEM, _SEM, *[_HBM] * (2 * n), pl.BlockSpec(memory_space=pltpu.VMEM)),
        input_output_aliases={a: 2 + a for a in range(2 * n)},
        compiler_params=pltpu.CompilerParams(has_side_effects=_DATAFLOW),
    )(*[_in_hbm(s) for s in shards], *[_in_hbm(lax.empty(ls, s.dtype)) for ls, s in zip(land_shapes, shards)], after)


def _gather_halves_wait(started, after, name):
    send_sems, recv_sems, *thru = started
    n = len(thru) // 2
    shapes = [t.shape for t in thru[:n]]

    def body(*refs):
        ins, lands = refs[:n], refs[n:2 * n]
        send_sems, recv_sems = refs[2 * n], refs[2 * n + 1]
        x, y, c = _pos()
        q = 2 * x + y
        chips = _other_chips(x, y)
        for a in range(n):
            for j, (px, py) in enumerate(chips):
                _halves_copy(ins, lands, send_sems, recv_sems, shapes, a, j, q, x, y, c).wait_send()
                _halves_copy(ins, lands, send_sems, recv_sems, shapes, a, j, 2 * px + py, x, y, c).wait_recv()

    outs = pl.pallas_call(
        body, name=name, out_shape=[pltpu.HBM(t.shape, t.dtype) for t in thru],
        in_specs=[_HBM] * (2 * n) + [_SEM, _SEM] + [_ANY] * len(after), out_specs=[_HBM] * (2 * n),
        input_output_aliases={a: a for a in range(2 * n)},
        compiler_params=pltpu.CompilerParams(has_side_effects=_DATAFLOW),
    )(*thru, send_sems, recv_sems, *after)
    return outs[:n], outs[n:]


def _sibling_fill(gathered, name):
    big = [a for a, g in enumerate(gathered) if _halvable(g.shape[1:])]
    n = len(gathered)

    def body(*refs):
        ins, outs = refs[:n], refs[n:2 * n]
        send_sems, recv_sems = refs[2 * n:]
        x, y, c = _pos()
        chips = _other_chips(x, y)

        def copy(k, j, half):
            a = big[k]
            px, py = chips[j]
            rows = _rows_of_half(gathered[a].shape[1:], half)
            return pltpu.make_async_remote_copy(
                src_ref=ins[a].at[2 * px + py, rows, :], dst_ref=outs[a].at[2 * px + py, rows, :],
                send_sem=send_sems.at[3 * k + j], recv_sem=recv_sems.at[3 * k + j],
                device_id=(x, y, 1 - c), device_id_type=MESH)

        sends = [copy(k, j, c) for k in range(len(big)) for j in range(3)]
        for cp in sends:
            cp.start()
        for k in range(len(big)):
            for j in range(3):
                copy(k, j, 1 - c).wait_recv()
        for cp in sends:
            cp.wait_send()

    return pl.pallas_call(
        body, name=name, in_specs=[_HBM] * n, out_specs=[_HBM] * n,
        out_shape=[jax.ShapeDtypeStruct(g.shape, g.dtype) for g in gathered],
        input_output_aliases={a: a for a in range(n)},
        scratch_shapes=[pltpu.SemaphoreType.DMA((3 * len(big),)), pltpu.SemaphoreType.DMA((3 * len(big),))],
    )(*gathered)


def _fill_copy(refs, send_sems, recv_sems, shapes, a, j, half, x, y, c):
    px, py = _other_chips(x, y)[j]
    rows = _rows_of_half(shapes[a], half)
    return pltpu.make_async_remote_copy(
        src_ref=refs[a].at[2 * px + py, rows, :], dst_ref=refs[a].at[2 * px + py, rows, :],
        send_sem=send_sems.at[3 * a + j], recv_sem=recv_sems.at[3 * a + j],
        device_id=(x, y, 1 - c), device_id_type=MESH)


def _sibling_fill_start(gathered, name):
    n = len(gathered)
    shapes = [g.shape[1:] for g in gathered]

    def body(*refs):
        ins = refs[:n]
        send_sems, recv_sems = refs[n], refs[n + 1]
        token = refs[-1]
        x, y, c = _pos()
        for a in range(n):
            for j in range(3):
                _fill_copy(ins, send_sems, recv_sems, shapes, a, j, c, x, y, c).start()
        token[...] = jnp.zeros_like(token)

    return pl.pallas_call(
        body, name=name,
        out_shape=(pltpu.SemaphoreType.DMA((3 * n,)), pltpu.SemaphoreType.DMA((3 * n,)),
                   *[pltpu.HBM(g.shape, g.dtype) for g in gathered], jax.ShapeDtypeStruct((8, 128), F32)),
        in_specs=[_HBM] * n,
        out_specs=(_SEM, _SEM, *[_HBM] * n, pl.BlockSpec(memory_space=pltpu.VMEM)),
        input_output_aliases={a: 2 + a for a in range(n)},
        compiler_params=pltpu.CompilerParams(has_side_effects=_DATAFLOW),
    )(*[_in_hbm(g) for g in gathered])


def _sibling_fill_wait(started, after, name):
    send_sems, recv_sems, *thru = started
    n = len(thru)
    shapes = [t.shape[1:] for t in thru]

    def body(*refs):
        ins = refs[:n]
        send_sems, recv_sems = refs[n], refs[n + 1]
        x, y, c = _pos()
        for a in range(n):
            for j in range(3):
                _fill_copy(ins, send_sems, recv_sems, shapes, a, j, c, x, y, c).wait_send()
                _fill_copy(ins, send_sems, recv_sems, shapes, a, j, 1 - c, x, y, c).wait_recv()

    return pl.pallas_call(
        body, name=name, out_shape=[pltpu.HBM(t.shape, t.dtype) for t in thru],
        in_specs=[_HBM] * n + [_SEM, _SEM] + [_ANY] * len(after), out_specs=[_HBM] * n,
        input_output_aliases={a: a for a in range(n)},
        compiler_params=pltpu.CompilerParams(has_side_effects=_DATAFLOW),
    )(*thru, send_sems, recv_sems, *after)


def _place_own(shards, gathered, cq, name, carry=()):
    n = len(shards)
    nc = len(carry)
    steps = 4

    def body(cq_ref, *refs):
        for a in range(n):
            refs[2 * n + nc + a][...] = refs[a][...]

    def tile(shape):
        return shape[0] // steps if _halvable(shape) else shape[0]

    in_specs = [pl.BlockSpec((tile(s.shape), s.shape[1]), (lambda i, s_: (i, 0)) if _halvable(s.shape) else (lambda i, s_: (0, 0)))
                for s in shards]
    in_specs += [pl.BlockSpec(memory_space=pl.ANY)] * (n + nc)
    out_specs = [pl.BlockSpec((None, tile(s.shape), s.shape[1]),
                              (lambda i, s_: (s_[1], i, 0)) if _halvable(s.shape) else (lambda i, s_: (s_[1], 0, 0)))
                 for s in shards]
    out_specs += [pl.BlockSpec(memory_space=pl.ANY)] * nc
    gs = pltpu.PrefetchScalarGridSpec(num_scalar_prefetch=1, grid=(steps,), in_specs=in_specs, out_specs=out_specs)
    outs = pl.pallas_call(
        body, name=name, grid_spec=gs,
        out_shape=[jax.ShapeDtypeStruct(g.shape, g.dtype) for g in gathered] + [jax.ShapeDtypeStruct(t.shape, t.dtype) for t in carry],
        input_output_aliases={1 + n + a: a for a in range(n + nc)},
        compiler_params=_cparams(("arbitrary",)),
    )(cq, *shards, *gathered, *carry)
    return (outs[:n], outs[n:]) if nc else outs


def _half_rows(ref, c, rh):
    return ref.at[:, pl.ds(pl.multiple_of(c * rh, 8), rh), :]


def _chips_copy(src_refs, land_refs, send_sems, recv_sems, a, j, x, y, c):
    px, py = _other_chips(x, y)[j]
    return pltpu.make_async_remote_copy(src_ref=src_refs[a].at[2 * px + py], dst_ref=land_refs[a].at[j],
                                        send_sem=send_sems.at[3 * a + j], recv_sem=recv_sems.at[3 * a + j],
                                        device_id=(px, py, c), device_id_type=MESH)


def _peer(r, x, y, c):
    return (x if r & 4 == 0 else 1 - x), (y if r & 2 == 0 else 1 - y), (c if r & 1 == 0 else 1 - c)


def _small_copy(small_ref, all_ref, send_sems, recv_sems, base, r, slot, x, y, c):
    return pltpu.make_async_remote_copy(src_ref=small_ref, dst_ref=all_ref.at[slot], send_sem=send_sems.at[base + r - 1],
                                        recv_sem=recv_sems.at[base + r - 1], device_id=_peer(r, x, y, c), device_id_type=MESH)


def _grad_chips_start(parts, name, small=None):
    n = len(parts)
    srcs = list(parts) + ([] if small is None else [small])
    m = len(srcs)

    def body(*refs):
        ins, lands = refs[:m], refs[m:2 * m]
        send_sems, recv_sems = refs[2 * m], refs[2 * m + 1]
        token = refs[-1]
        x, y, c = _pos()
        for a in range(n):
            for j in range(3):
                _chips_copy(ins, lands, send_sems, recv_sems, a, j, x, y, c).start()
        if small is not None:
            for r in range(1, 8):
                _small_copy(ins[n], lands[n], send_sems, recv_sems, 3 * n, r, 4 * x + 2 * y + c, x, y, c).start()
        token[...] = jnp.zeros_like(token)

    land_shapes = [(3,) + p.shape[1:] for p in parts] + ([] if small is None else [(8,) + small.shape])
    nsem = 3 * n + (0 if small is None else 7)
    return pl.pallas_call(
        body, name=name,
        out_shape=(pltpu.SemaphoreType.DMA((nsem,)), pltpu.SemaphoreType.DMA((nsem,)),
                   *[pltpu.HBM(p.shape, p.dtype) for p in srcs],
                   *[pltpu.HBM(ls, p.dtype) for ls, p in zip(land_shapes, srcs)],
                   jax.ShapeDtypeStruct((8, 128), F32)),
        in_specs=[_HBM] * (2 * m),
        out_specs=(_SEM, _SEM, *[_HBM] * (2 * m), pl.BlockSpec(memory_space=pltpu.VMEM)),
        input_output_aliases={a: 2 + a for a in range(2 * m)},
        compiler_params=pltpu.CompilerParams(has_side_effects=_DATAFLOW),
    )(*[_in_hbm(p) for p in srcs], *[_in_hbm(lax.empty(ls, p.dtype)) for ls, p in zip(land_shapes, srcs)])


def _grad_chips_wait(started, after, name, with_small=False):
    send_sems, recv_sems, *thru = started
    m = len(thru) // 2
    n = m - (1 if with_small else 0)

    def body(*refs):
        ins, lands = refs[:m], refs[m:2 * m]
        send_sems, recv_sems = refs[2 * m], refs[2 * m + 1]
        x, y, c = _pos()
        for a in range(n):
            for j in range(3):
                cp = _chips_copy(ins, lands, send_sems, recv_sems, a, j, x, y, c)
                cp.wait_send()
                cp.wait_recv()
        if with_small:
            for r in range(1, 8):
                px, py, pc = _peer(r, x, y, c)
                _small_copy(ins[n], lands[n], send_sems, recv_sems, 3 * n, r, 4 * x + 2 * y + c, x, y, c).wait_send()
                _small_copy(ins[n], lands[n], send_sems, recv_sems, 3 * n, r, 4 * px + 2 * py + pc, x, y, c).wait_recv()

    outs = pl.pallas_call(
        body, name=name, out_shape=[pltpu.HBM(t.shape, t.dtype) for t in thru],
        in_specs=[_HBM] * (2 * m) + [_SEM, _SEM] + [_ANY] * len(after), out_specs=[_HBM] * (2 * m),
        input_output_aliases={a: a for a in range(2 * m)},
        compiler_params=pltpu.CompilerParams(has_side_effects=_DATAFLOW),
    )(*thru, send_sems, recv_sems, *after)
    return list(outs[m:]) + list(outs[n:m])


def _sibling_copy(src_refs, land_refs, send_sems, recv_sems, rhs, a, c, x, y):
    return pltpu.make_async_remote_copy(src_ref=_half_rows(src_refs[a], 1 - c, rhs[a]), dst_ref=land_refs[a],
                                        send_sem=send_sems.at[a], recv_sem=recv_sems.at[a],
                                        device_id=(x, y, 1 - c), device_id_type=MESH)


def _grad_sibling_start(fams, name):
    n = len(fams)
    rhs = [f.shape[1] // 2 for f in fams]

    def body(*refs):
        ins, lands = refs[:n], refs[n:2 * n]
        send_sems, recv_sems = refs[2 * n], refs[2 * n + 1]
        token = refs[-1]
        x, y, c = _pos()
        for a in range(n):
            _sibling_copy(ins, lands, send_sems, recv_sems, rhs, a, c, x, y).start()
        token[...] = jnp.zeros_like(token)

    land_shapes = [(f.shape[0], f.shape[1] // 2, f.shape[2]) for f in fams]
    return pl.pallas_call(
        body, name=name,
        out_shape=(pltpu.SemaphoreType.DMA((n,)), pltpu.SemaphoreType.DMA((n,)),
                   *[pltpu.HBM(f.shape, f.dtype) for f in fams],
                   *[pltpu.HBM(ls, f.dtype) for ls, f in zip(land_shapes, fams)],
                   jax.ShapeDtypeStruct((8, 128), F32)),
        in_specs=[_HBM] * (2 * n),
        out_specs=(_SEM, _SEM, *[_HBM] * (2 * n), pl.BlockSpec(memory_space=pltpu.VMEM)),
        input_output_aliases={a: 2 + a for a in range(2 * n)},
        compiler_params=pltpu.CompilerParams(has_side_effects=_DATAFLOW),
    )(*[_in_hbm(f) for f in fams], *[_in_hbm(lax.empty(ls, f.dtype)) for ls, f in zip(land_shapes, fams)])


def _grad_sibling_wait(started, after, name):
    send_sems, recv_sems, *thru = started
    n = len(thru) // 2
    rhs = [t.shape[1] // 2 for t in thru[:n]]

    def body(*refs):
        ins, lands = refs[:n], refs[n:2 * n]
        send_sems, recv_sems = refs[2 * n], refs[2 * n + 1]
        x, y, c = _pos()
        for a in range(n):
            cp = _sibling_copy(ins, lands, send_sems, recv_sems, rhs, a, c, x, y)
            cp.wait_send()
            cp.wait_recv()

    outs = pl.pallas_call(
        body, name=name, out_shape=[pltpu.HBM(t.shape, t.dtype) for t in thru],
        in_specs=[_HBM] * (2 * n) + [_SEM, _SEM] + [_ANY] * len(after), out_specs=[_HBM] * (2 * n),
        input_output_aliases={a: a for a in range(2 * n)},
        compiler_params=pltpu.CompilerParams(has_side_effects=_DATAFLOW),
    )(*thru, send_sems, recv_sems, *after)
    return outs[:n], outs[n:]


def _grad_share(fulls, name):
    n = len(fulls)
    rhs = [f.shape[0] // 2 for f in fulls]

    def body(*refs):
        ins, outs = refs[:n], refs[n:2 * n]
        send_sems, recv_sems = refs[2 * n], refs[2 * n + 1]
        x, y, c = _pos()

        def copy(a, half):
            rows = pl.ds(pl.multiple_of(half * rhs[a], 8), rhs[a])
            return pltpu.make_async_remote_copy(src_ref=ins[a].at[rows, :], dst_ref=outs[a].at[rows, :],
                                                send_sem=send_sems.at[a], recv_sem=recv_sems.at[a],
                                                device_id=(x, y, 1 - c), device_id_type=MESH)

        sends = [copy(a, c) for a in range(n)]
        for cp in sends:
            cp.start()
        for a in range(n):
            copy(a, 1 - c).wait_recv()
        for cp in sends:
            cp.wait_send()

    return pl.pallas_call(
        body, name=name, in_specs=[_HBM] * n, out_specs=[_HBM] * n,
        out_shape=[jax.ShapeDtypeStruct(f.shape, f.dtype) for f in fulls],
        input_output_aliases={a: a for a in range(n)},
        scratch_shapes=[pltpu.SemaphoreType.DMA((n,)), pltpu.SemaphoreType.DMA((n,))],
    )(*fulls)


def _add_sibling(own, recv, cq, name):
    nb, R, Cc = own.shape
    Rh = R // 2

    def body(cq_ref, a_ref, b_ref, o32_ref, o16_ref):
        s = a_ref[0] + b_ref[0]
        mine = pl.program_id(0) == cq_ref[1]

        @pl.when(mine)
        def _():
            o32_ref[...] = s

        @pl.when(jnp.logical_not(mine))
        def _():
            o16_ref[0] = s.astype(o16_ref.dtype)

    sp = pl.BlockSpec((1, Rh, Cc), lambda b, s: (b, 0, 0))
    gs = pltpu.PrefetchScalarGridSpec(
        num_scalar_prefetch=1, grid=(nb,),
        in_specs=[pl.BlockSpec((1, Rh, Cc), lambda b, s: (b, s[0], 0)), sp],
        out_specs=[pl.BlockSpec((Rh, Cc), lambda b, s: (0, 0)), sp])
    return pl.pallas_call(
        body, name=name, grid_spec=gs,
        out_shape=[jax.ShapeDtypeStruct((Rh, Cc), F32), jax.ShapeDtypeStruct((nb, Rh, Cc), _MXU)],
        compiler_params=_cparams(("arbitrary",)),
    )(cq, own, recv)


def _add_sibling_split(d_wp, recv, cq, name):
    _, Dm, Pc = d_wp.shape
    Rh = Dm // 2
    Wb = IN_COLS // N_CHIPS
    T = 256

    def body(cq_ref, a_ref, b_ref, o32_ref, o16_ref):
        s = a_ref[0] + b_ref[0]
        blocks = [s[:, 0:Wb], s[:, Wb:2 * Wb],
                  jnp.concatenate([s[:, 2 * Wb:P_QKVB], s[:, P_BA:P_BA + 8], s[:, P_QKVB:3 * Wb - 8]], axis=1),
                  s[:, 3 * Wb - 8:P_BA]]
        q = cq_ref[1]
        own = None
        for j, blk in enumerate(blocks):
            term = jnp.where(q == j, blk, 0.0)
            own = term if own is None else own + term
            o16_ref[j] = blk.astype(o16_ref.dtype)
        o32_ref[...] = own

    gs = pltpu.PrefetchScalarGridSpec(
        num_scalar_prefetch=1, grid=(Rh // T,),
        in_specs=[pl.BlockSpec((1, T, Pc), lambda i, s: (0, s[0] * (Rh // T) + i, 0)), pl.BlockSpec((1, T, Pc), lambda i, s: (0, i, 0))],
        out_specs=[pl.BlockSpec((T, Wb), lambda i, s: (i, 0)), pl.BlockSpec((N_CHIPS, T, Wb), lambda i, s: (0, i, 0))])
    return pl.pallas_call(
        body, name=name, grid_spec=gs,
        out_shape=[jax.ShapeDtypeStruct((Rh, Wb), F32), jax.ShapeDtypeStruct((N_CHIPS, Rh, Wb), _MXU)],
        compiler_params=_cparams(("parallel",)),
    )(cq, d_wp, recv)


def _add_chips(part32, recv3, cq, name):
    Rh, Cc = part32.shape

    def body(cq_ref, a_ref, b_ref, o_ref):
        acc = a_ref[...]
        for j in range(3):
            acc = acc + b_ref[j].astype(F32)
        o_ref[...] = acc

    gs = pltpu.PrefetchScalarGridSpec(
        num_scalar_prefetch=1, grid=(1,),
        in_specs=[pl.BlockSpec((Rh, Cc), lambda i, s: (0, 0)), pl.BlockSpec((3, Rh, Cc), lambda i, s: (0, 0, 0))],
        out_specs=pl.BlockSpec((Rh, Cc), lambda i, s: (s[0], 0)))
    return pl.pallas_call(
        body, name=name, grid_spec=gs, out_shape=jax.ShapeDtypeStruct((2 * Rh, Cc), F32),
        compiler_params=_cparams(("arbitrary",)),
    )(cq, part32, recv3)


def _transposed(g):
    Dm, n = g.shape
    pad = -n % 128

    def body(g_ref, o_ref):
        xp = jnp.concatenate([g_ref[...], jnp.zeros((Dm, pad), F32)], axis=1)
        o_ref[...] = xp.T[:n, :]

    return pl.pallas_call(body, name="transposed", out_shape=jax.ShapeDtypeStruct((n, Dm), F32),
                          compiler_params=_cparams(vmem=V7X_VMEM_LIMIT))(g)


def _adamw(w, g, m, v, name):
    R, Cc = w.shape
    T = max([t for t in range(8, 257, 8) if R % t == 0], default=R)

    def body(w_ref, g_ref, m_ref, v_ref, d_ref, mo_ref, vo_ref):
        d_ref[...], mo_ref[...], vo_ref[...] = _adamw_math(w_ref[...], g_ref[...], m_ref[...], v_ref[...])

    sp = pl.BlockSpec((T, Cc), lambda i: (i, 0))
    sh = jax.ShapeDtypeStruct((R, Cc), F32)
    return pl.pallas_call(
        body, name=name, grid=(R // T,), in_specs=[sp] * 4, out_specs=(sp, sp, sp), out_shape=(sh, sh, sh),
        compiler_params=_cparams(("parallel",)),
    )(w, g, m, v)


SMALL_ROWS = 32
ROW_CONV, ROW_FCG, ROW_FCU = 5, 13, 22


def _adamw_math(w, g, m, v):
    mn = ADAM_B1 * m + (1.0 - ADAM_B1) * g
    vn = ADAM_B2 * v + (1.0 - ADAM_B2) * (g * g)
    c1 = 1.0 / (1.0 - ADAM_B1 ** ADAM_STEP)
    c2 = 1.0 / (1.0 - ADAM_B2 ** ADAM_STEP)
    return -ADAM_LR * ((mn * c1) / (jnp.sqrt(vn * c2) + ADAM_EPS) + ADAM_WD * w), mn, vn


def _pack_small(n1, n2, fn, gp, gn, conv, fcg, fcu, loss):
    W = D_MODEL

    def body(n1_ref, n2_ref, fn_ref, gp_ref, gn_ref, conv_ref, fcg_ref, fcu_ref, loss_ref, o_ref):
        o_ref[...] = jnp.zeros_like(o_ref)
        o_ref[0:1, :] = n1_ref[...]
        o_ref[1:2, :] = n2_ref[...]
        o_ref[2:3, :] = fn_ref[...]
        o_ref[3:4, 0:8] = gp_ref[0:1, 0:8]
        o_ref[3:4, 8:9] = loss_ref[0:1, 0:1]
        o_ref[4:5, 0:128] = gn_ref[...]
        for i in range(GDN_CONV):
            o_ref[ROW_CONV + 2 * i:ROW_CONV + 2 * i + 1, :] = conv_ref[i:i + 1, 0:W]
            o_ref[ROW_CONV + 2 * i + 1:ROW_CONV + 2 * i + 2, 0:3 * GDN_WIDTH - W] = conv_ref[i:i + 1, W:3 * GDN_WIDTH]
        for r0, ref in ((ROW_FCG, fcg_ref), (ROW_FCU, fcu_ref)):
            for i in range(FFN_CONV):
                for k in range(3):
                    n = min(W, D_FF - k * W)
                    o_ref[r0 + 3 * i + k:r0 + 3 * i + k + 1, 0:n] = ref[i:i + 1, k * W:k * W + n]

    return pl.pallas_call(body, name="pack_small", out_shape=jax.ShapeDtypeStruct((SMALL_ROWS, W), F32))(
        n1, n2, fn, gp, gn, conv, fcg, fcu, loss)


def _small_step(meq, small_all, small, ws, ms, vs):
    W = D_MODEL
    n = len(ws)
    cw, fw = ws[6].shape[1], ws[7].shape[1]

    def body(meq_ref, all_ref, own_ref, *refs):
        w_refs, m_refs, v_refs = refs[:n], refs[n:2 * n], refs[2 * n:3 * n]
        loss_ref = refs[3 * n]
        outs = refs[3 * n + 1:]
        me, q = meq_ref[0], meq_ref[1]
        red = None
        for d in range(8):
            term = jnp.where(me == d, own_ref[...], all_ref[d])
            red = term if red is None else red + term
        loss_ref[...] = jnp.broadcast_to(red[3:4, 8:9], loss_ref.shape)
        conv = [jnp.concatenate([red[ROW_CONV + 2 * i:ROW_CONV + 2 * i + 1, :],
                                 red[ROW_CONV + 2 * i + 1:ROW_CONV + 2 * i + 2, 0:3 * GDN_WIDTH - W]], axis=1)
                for i in range(GDN_CONV)]
        conv = jnp.concatenate(conv, axis=0)

        def fc_rows(r0):
            rows = [jnp.concatenate([red[r0 + 3 * i + k:r0 + 3 * i + k + 1, 0:min(W, D_FF - k * W)] for k in range(3)], axis=1)
                    for i in range(FFN_CONV)]
            return jnp.concatenate(rows, axis=0)

        fc = jnp.concatenate([fc_rows(ROW_FCG), fc_rows(ROW_FCU)], axis=1)

        def chip_block(full, width):
            out = None
            for j in range(N_CHIPS):
                term = jnp.where(q == j, full[:, width * j:width * (j + 1)], 0.0)
                out = term if out is None else out + term
            return out

        grads = [red[0:1, :], red[1:2, :], red[2:3, :], red[3:4, 0:4], red[3:4, 4:8], red[4:5, 0:128],
                 chip_block(conv, cw), chip_block(fc, fw)]
        for k in range(n):
            d_, m_, v_ = _adamw_math(w_refs[k][...], grads[k], m_refs[k][...], v_refs[k][...])
            outs[4 * k][...] = grads[k]
            outs[4 * k + 1][...] = d_
            outs[4 * k + 2][...] = m_
            outs[4 * k + 3][...] = v_

    full = lambda a: pl.BlockSpec(a.shape, lambda i, s_, nd=len(a.shape): (0,) * nd)
    arrays = [small_all, small, *ws, *ms, *vs]
    out_shapes = [jax.ShapeDtypeStruct((8, 128), F32)] + [jax.ShapeDtypeStruct(w.shape, F32) for w in ws for _ in range(4)]
    gs = pltpu.PrefetchScalarGridSpec(
        num_scalar_prefetch=1, grid=(1,), in_specs=[full(a) for a in arrays],
        out_specs=[pl.BlockSpec(o.shape, lambda i, s_, nd=len(o.shape): (0,) * nd) for o in out_shapes])
    return pl.pallas_call(body, name="small_step", grid_spec=gs, out_shape=out_shapes)(meq, *arrays)


def _pad_lanes(v, n=D_MODEL):
    return jnp.pad(v, ((0, 0), (0, n - v.shape[1])))


def kernel(x, norm1_w, w_in, conv_qkv_w, a_log, dt_bias, gdn_norm_w, w_out, norm2_w, w_up, ffn_conv_w, w_down, final_norm_w, loss_target, m_norm1_w, m_w_in, m_conv_qkv_w, m_a_log, m_dt_bias, m_gdn_norm_w, m_w_out, m_norm2_w, m_w_up, m_ffn_conv_w, m_w_down, m_final_norm_w, v_norm1_w, v_w_in, v_conv_qkv_w, v_a_log, v_dt_bias, v_gdn_norm_w, v_w_out, v_norm2_w, v_w_up, v_ffn_conv_w, v_w_down, v_final_norm_w):
    c = lax.axis_index("c")
    q = 2 * lax.axis_index("x") + lax.axis_index("y")
    S = x.shape[1]
    cq = jnp.stack([c, q]).astype(jnp.int32)

    *in_started, in_token = _gather_halves_start([w_in[0].astype(_MXU), conv_qkv_w[0], ffn_conv_w[0]], x, "gather_in_start")
    w_in_l, m_w_in_l, v_w_in_l = (jnp.swapaxes(a + in_token[0:1, 0:1], 1, 2)[0] for a in (w_in, m_w_in, v_w_in))
    h1 = _rmsnorm_fwd(x[0], norm1_w, "norm1", after=[in_token])
    rest = [(a[0] + in_token[0:1, 0:1]).astype(_MXU) for a in (w_out, w_up, w_down)]
    in_shards, got_in = _gather_halves_wait(in_started, [w_in_l, m_w_in_l, v_w_in_l, h1, *rest], "gather_in_wait")
    (g_in, g_conv, g_fconv), (w_in_l, m_w_in_l, v_w_in_l) = _place_own(
        in_shards, _sibling_fill(got_in, "fill_in"), cq, "place_in", carry=[w_in_l, m_w_in_l, v_w_in_l])
    *rest_started, token = _gather_halves_start(rest, g_conv, "gather_rest_start")

    rest_state = {}

    def rest_arrived(after):
        rest_state["shards"], got = _gather_halves_wait(rest_started, after, "gather_rest_wait")
        *rest_state["fill"], tok = _sibling_fill_start(got, "fill_rest_start")
        return tok

    def rest_filled(after):
        got = _sibling_fill_wait(rest_state["fill"], after, "fill_rest_wait")
        g_out, g_up, g_down = _place_own(rest_state["shards"], got, cq, "place_rest")
        return g_out.reshape(D_MODEL, D_MODEL), g_up, g_down.reshape(D_FF, D_MODEL)

    rest_weights = (rest_arrived, rest_filled)
    wp = _wp_assemble(g_in, [token])
    conv_f = jnp.concatenate([g_conv[i] for i in range(N_CHIPS)], axis=1)
    fcw = jnp.concatenate([g_fconv[i] for i in range(N_CHIPS)], axis=1)
    gp = _pad_lanes(jnp.concatenate([a_log, dt_bias], axis=1), 128)
    fnw = final_norm_w[None, :]
    early = {}

    early_names = ("w_up", "w_down", "w_out")

    def early_sibling(d_wup, d_wdown, d_wout):
        *early["sibling"], tok = _grad_sibling_start(
            [d_wup, d_wdown.reshape(N_CHIPS, D_FF // N_CHIPS, D_MODEL), d_wout.reshape(N_CHIPS, D_MODEL // N_CHIPS, D_MODEL)],
            "grad_sibling_early_start")
        return tok

    def early_chips(after):
        fams_e, got_e = _grad_sibling_wait(early["sibling"], after, "grad_sibling_early_wait")
        early["parts"] = [_add_sibling(f, r, cq, "add_sibling_" + nm) for f, r, nm in zip(fams_e, got_e, early_names)]
        *early["started"], tok = _grad_chips_start([p[1] for p in early["parts"]], "grad_chips_start")
        return tok

    def late_sibling(d_wp):
        *early["late_sibling"], tok = _grad_sibling_start([d_wp[None]], "grad_sibling_late_start")
        return tok

    loss_l, dx, g = _local_step(x[0], loss_target[0], h1, norm1_w, norm2_w, fnw, gp, gdn_norm_w, wp,
                                conv_f, fcw, rest_weights, (early_sibling, early_chips, late_sibling))
    fams, got = _grad_sibling_wait(early["late_sibling"], [dx], "grad_sibling_late_wait")
    late_part = _add_sibling_split(fams[0], got[0], cq, "add_sibling_w_in")
    *late_started, late_token = _grad_chips_start([late_part[1]], "grad_chips_late_start")
    small = _pack_small(g["n1w"], g["n2w"], g["fnw"], g["gp"], g["gnw"], g["conv_w"], g["fcw_g"], g["fcw_u"], loss_l)
    *small_started, small_token = _grad_chips_start([], "small_gather_start", small)
    got3_e = _grad_chips_wait(early["started"], [late_token, small_token], "grad_chips_wait")
    g_w_up, g_w_down, g_w_out = _grad_share(
        [_add_chips(p[0], r3, cq, "add_chips_" + nm) for p, r3, nm in zip(early["parts"], got3_e, early_names)],
        "grad_share_early")
    big = {}

    def adamw_big(nm, w, gg, m, v):
        d_, m_, v_ = _adamw(w[0], gg, m[0], v[0], "adamw_" + nm)
        big[nm] = (gg[None], d_[None], m_[None], v_[None])

    adamw_big("w_up", w_up, g_w_up, m_w_up, v_w_up)
    adamw_big("w_down", w_down, g_w_down, m_w_down, v_w_down)
    adamw_big("w_out", w_out, g_w_out, m_w_out, v_w_out)
    got3, = _grad_chips_wait(late_started, [big[nm][1] for nm in early_names], "grad_chips_late_wait")
    g_w_in, = _grad_share([_add_chips(late_part[0], got3, cq, "add_chips_w_in")], "grad_share_late")
    g_t = _transposed(g_w_in)
    d_t, m_t, v_t = _adamw(w_in_l, g_t, m_w_in_l, v_w_in_l, "adamw_w_in")
    big["w_in"] = tuple(jnp.swapaxes(t[None], 1, 2) for t in (g_t, d_t, m_t, v_t))
    small_all, small = _grad_chips_wait(small_started, [d_t], "small_gather_wait", with_small=True)
    small_names = ["norm1_w", "norm2_w", "final_norm_w", "a_log", "dt_bias", "gdn_norm_w", "conv_qkv_w", "ffn_conv_w"]
    loss_b, *small_out = _small_step(
        jnp.stack([2 * q + c, q]).astype(jnp.int32), small_all, small,
        [norm1_w, norm2_w, final_norm_w[None], a_log, dt_bias, gdn_norm_w, conv_qkv_w[0], ffn_conv_w[0]],
        [m_norm1_w, m_norm2_w, m_final_norm_w[None], m_a_log, m_dt_bias, m_gdn_norm_w, m_conv_qkv_w[0], m_ffn_conv_w[0]],
        [v_norm1_w, v_norm2_w, v_final_norm_w[None], v_a_log, v_dt_bias, v_gdn_norm_w, v_conv_qkv_w[0], v_ffn_conv_w[0]])
    like = dict(final_norm_w=lambda t: t[0], conv_qkv_w=lambda t: t[None], ffn_conv_w=lambda t: t[None])
    for k, nm in enumerate(small_names):
        big[nm] = tuple(like.get(nm, lambda t: t)(t) for t in small_out[4 * k:4 * k + 4])
    names = ["norm1_w", "w_in", "conv_qkv_w", "a_log", "dt_bias", "gdn_norm_w", "w_out", "norm2_w", "w_up",
             "ffn_conv_w", "w_down", "final_norm_w"]
    return (loss_b[0, 0], dx[None], *[big[n][0] for n in names], *[big[n][1] for n in names],
            *[big[n][2] for n in names], *[big[n][3] for n in names])
```

```python
import functools
import math

import numpy as np
import jax
import jax.numpy as jnp
from jax import lax
from jax.experimental import pallas as pl
from jax.experimental.pallas import tpu as pltpu

F32 = jnp.float32
BF16 = jnp.bfloat16
_MXU = jnp.bfloat16
_HI = lax.Precision.HIGHEST
EPS = 1e-6
V7X_VMEM_LIMIT = 56 * 1024 * 1024
MESH = pl.DeviceIdType.MESH

D_MODEL = 1024
GDN_HEADS, GDN_DIM, GDN_CHUNK, GDN_CONV = 4, 128, 64, 4
GDN_WIDTH = GDN_HEADS * GDN_DIM
DIL_HEADS, DIL_DIM = 8, 64
DIL_WIDTH = DIL_HEADS * DIL_DIM
D_FF, FFN_CONV = 2816, 3
IN_COLS = 3592
P_COLS = 3840
P_Z, P_QKVB, P_BA = 1536, 2048, 3584
ATT_T = 1024
ADAM_LR, ADAM_B1, ADAM_B2, ADAM_EPS, ADAM_WD, ADAM_STEP = 0.001, 0.9, 0.999, 1e-08, 0.01, 10
N_CHIPS = 4


def _cparams(sem=None, vmem=None):
    kw = {}
    if sem is not None:
        kw["dimension_semantics"] = sem
    if vmem is not None:
        kw["vmem_limit_bytes"] = vmem
    return pltpu.CompilerParams(**kw)


def _silu(x):
    return x * jax.nn.sigmoid(x)


def _pick_tile(n, cap):
    best = None
    for t in range(128, min(n, cap) + 1, 128):
        if n % t == 0:
            best = t
    return best or n


def _mm(a, b, mode, *, out_dtype=F32, residual=None, name, b_blocks=False, place=None, into=None, tn=None, after=(),
        normed_by=None):
    if mode == "nn":
        M, K = a.shape
        N = b.shape[0] * b.shape[2] if b_blocks else b.shape[1]
    elif mode == "nt":
        (M, K), (N, _) = a.shape, b.shape
    else:
        (K, M), (_, N) = a.shape, b.shape
    tm = _pick_tile(M, 1024)
    tn = b.shape[2] if b_blocks else (tn or _pick_tile(N, 1536))

    def vmem(tm, tn):
        return 2 * (tm * K * a.dtype.itemsize + tn * K * b.dtype.itemsize
                    + tm * tn * (jnp.dtype(out_dtype).itemsize + (4 if residual is not None else 0))) + 3 * tm * tn * 4

    fixed_tn = b_blocks or (place is not None and place[0] == "blocks")
    while vmem(tm, tn) > 40 * 1024 * 1024:
        if (tm >= tn or fixed_tn) and tm % 256 == 0:
            tm //= 2
        elif tn % 256 == 0 and not fixed_tn:
            tn //= 2
        else:
            tm //= 2
    a_spec = pl.BlockSpec((K, tm), lambda j, i: (0, i)) if mode == "tn" else pl.BlockSpec((tm, K), lambda j, i: (i, 0))
    if b_blocks:
        b_spec = pl.BlockSpec((None, K, tn), lambda j, i: (j, 0, 0))
    else:
        b_spec = pl.BlockSpec((tn, K), lambda j, i: (j, 0)) if mode == "nt" else pl.BlockSpec((K, tn), lambda j, i: (0, j))
    r_spec = pl.BlockSpec((tm, tn), lambda j, i: (i, j))
    if place is None:
        o_spec, o_shape = r_spec, (M, N)
    elif place[0] == "rows":
        off = place[2] // tm
        o_spec, o_shape = pl.BlockSpec((tm, tn), lambda j, i: (i + off, j)), (place[1], N)
    else:
        off = place[2]
        o_spec, o_shape = pl.BlockSpec((None, tm, tn), lambda j, i: (j + off, i, 0)), (place[1], M, tn)
    dims = {"nn": (((1,), (0,)), ((), ())), "nt": (((1,), (1,)), ((), ())), "tn": (((0,), (0,)), ((), ()))}[mode]

    def body(*refs):
        a_ref, b_ref = refs[0], refs[1]
        o_ref = refs[-1] if normed_by is None else refs[-2]
        acc = lax.dot_general(a_ref[...].astype(_MXU), b_ref[...].astype(_MXU), dims, preferred_element_type=F32)
        if residual is not None:
            acc = acc + refs[2][...]
        o_ref[...] = acc.astype(out_dtype)
        if normed_by is not None:
            rs = lax.rsqrt(jnp.mean(acc * acc, axis=-1, keepdims=True) + EPS)
            refs[-1][...] = (acc * rs * refs[len(ins0)][...]).astype(_MXU)

    ins, specs, alias = [a, b], [a_spec, b_spec], {}
    if residual is not None:
        ins.append(residual)
        specs.append(r_spec)
    ins0 = list(ins)
    if normed_by is not None:
        assert tn == N and place is None and into is None
        ins.append(normed_by)
        specs.append(pl.BlockSpec((1, N), lambda j, i: (0, 0)))
    if into is not None:
        alias = {len(ins): 0}
        ins.append(into)
        specs.append(pl.BlockSpec(memory_space=pl.ANY))
    ins += list(after)
    specs += [pl.BlockSpec(memory_space=pl.ANY)] * len(after)
    o_shape = jax.ShapeDtypeStruct(o_shape, out_dtype)
    if normed_by is not None:
        o_spec, o_shape = (o_spec, r_spec), (o_shape, jax.ShapeDtypeStruct((M, N), _MXU))
    return pl.pallas_call(
        body, name=name, grid=(N // tn, M // tm), in_specs=specs, out_specs=o_spec,
        out_shape=o_shape, input_output_aliases=alias,
        compiler_params=_cparams(("parallel", "parallel"), V7X_VMEM_LIMIT),
    )(*ins)


def _wp_assemble(g_in, after=()):
    nb, Dm, Wb = g_in.shape
    T = 256
    n_lo = P_QKVB - 2 * Wb

    def body(g_ref, *rest):
        g2 = g_ref[2]
        rest[-1][...] = jnp.concatenate(
            [g_ref[0], g_ref[1], g2[:, :n_lo], g2[:, n_lo + 8:], g_ref[3], g2[:, n_lo:n_lo + 8],
             jnp.zeros((T, P_COLS - P_BA - 8), g_in.dtype)], axis=1)

    return pl.pallas_call(
        body, name="wp_assemble", grid=(Dm // T,),
        in_specs=[pl.BlockSpec((nb, T, Wb), lambda i: (0, i, 0))] + [pl.BlockSpec(memory_space=pl.ANY)] * len(after),
        out_specs=pl.BlockSpec((T, P_COLS), lambda i: (i, 0)), out_shape=jax.ShapeDtypeStruct((Dm, P_COLS), g_in.dtype),
        compiler_params=_cparams(("parallel",)),
    )(g_in, *after)


def _rmsnorm_fwd(x, w, name, after=()):
    S, D = x.shape
    T = _pick_tile(S, 512)

    def body(x_ref, w_ref, *rest):
        xv = x_ref[...]
        rs = lax.rsqrt(jnp.mean(xv * xv, axis=-1, keepdims=True) + EPS)
        rest[-1][...] = (xv * rs * w_ref[...]).astype(rest[-1].dtype)

    return pl.pallas_call(
        body, name=name, grid=(S // T,),
        in_specs=[pl.BlockSpec((T, D), lambda i: (i, 0)), pl.BlockSpec((1, D), lambda i: (0, 0))] + [_ANY] * len(after),
        out_specs=pl.BlockSpec((T, D), lambda i: (i, 0)),
        out_shape=jax.ShapeDtypeStruct((S, D), _MXU),
        compiler_params=_cparams(("parallel",)),
    )(x, w, *after)


def _rmsnorm_bwd(dh, x, w, dres, name, after=()):
    S, D = x.shape
    pair = isinstance(dh, tuple)
    T = _pick_tile(S, 256 if pair else 512)
    dhs = [*dh[0], dh[1]] if pair else [dh]

    def body(*refs):
        x_ref, w_ref, dres_ref = refs[len(dhs):len(dhs) + 3]
        dx_ref, dw_ref = refs[-2:]
        xv = x_ref[...]
        rs = lax.rsqrt(jnp.mean(xv * xv, axis=-1, keepdims=True) + EPS)
        xn = xv * rs
        if pair:
            b_ref = refs[len(dhs) - 1]
            nb, _, Kb = b_ref.shape
            per = nb // (len(dhs) - 1)
            dhv = None
            for blk in range(nb):
                lo = (blk % per) * Kb
                t = lax.dot_general(refs[blk // per][:, lo:lo + Kb].astype(_MXU), b_ref[blk].astype(_MXU), (_NT, ((), ())),
                                    preferred_element_type=F32)
                dhv = t if dhv is None else dhv + t
        else:
            dhv = refs[0][...]
        dxn = dhv * w_ref[...]
        dx_ref[...] = dres_ref[...] + rs * (dxn - xn * jnp.mean(dxn * xn, axis=-1, keepdims=True))

        @pl.when(pl.program_id(0) == 0)
        def _():
            dw_ref[...] = jnp.zeros_like(dw_ref)

        dw_ref[...] += jnp.sum(dhv * xn, axis=0, keepdims=True)

    row = pl.BlockSpec((T, D), lambda i: (i, 0))
    vec = pl.BlockSpec((1, D), lambda i: (0, 0))
    dh_specs = [row] if not pair else (
        [pl.BlockSpec((T, a.shape[1]), lambda i: (i, 0)) for a in dh[0]] + [pl.BlockSpec(dh[1].shape, lambda i: (0, 0, 0))])
    return pl.pallas_call(
        body, name=name, grid=(S // T,), in_specs=dh_specs + [row, vec, row] + [_ANY] * len(after), out_specs=(row, vec),
        out_shape=(jax.ShapeDtypeStruct((S, D), F32), jax.ShapeDtypeStruct((1, D), F32)),
        compiler_params=_cparams(("arbitrary",), V7X_VMEM_LIMIT if pair else None),
    )(*dhs, x, w, dres, *after)


def _loss_head(x3, w, tgt, name):
    S, D = tgt.shape
    fused = isinstance(x3, tuple)
    T = _pick_tile(S, 256 if fused else 512)
    xs = list(x3) if fused else [x3]

    def body(*refs):
        w_ref, t_ref = refs[len(xs):len(xs) + 2]
        loss_ref, dx_ref, dxn_ref, dw_ref = refs[-4:]
        xv = refs[0][...]
        if fused:
            xv = lax.dot_general(xv.astype(_MXU), refs[1][...].astype(_MXU), (_NN, ((), ())),
                                 preferred_element_type=F32) + refs[2][...]
        rs = lax.rsqrt(jnp.mean(xv * xv, axis=-1, keepdims=True) + EPS)
        xn = xv * rs
        err = xn * w_ref[...] - t_ref[...]
        dy = err * (1.0 / D)
        dxn = dy * w_ref[...]
        dxv = rs * (dxn - xn * jnp.mean(dxn * xn, axis=-1, keepdims=True))
        dx_ref[...] = dxv
        dxn_ref[...] = dxv.astype(dxn_ref.dtype)

        @pl.when(pl.program_id(0) == 0)
        def _():
            dw_ref[...] = jnp.zeros_like(dw_ref)
            loss_ref[...] = jnp.zeros_like(loss_ref)

        dw_ref[...] += jnp.sum(dy * xn, axis=0, keepdims=True)
        part = jnp.sum(jnp.sum(err * err, axis=-1, keepdims=True), axis=0, keepdims=True) * (0.5 / D)
        loss_ref[...] += jnp.broadcast_to(part, loss_ref.shape)

    row = pl.BlockSpec((T, D), lambda i: (i, 0))
    vec = pl.BlockSpec((1, D), lambda i: (0, 0))
    x_specs = [row] if not fused else [pl.BlockSpec((T, xs[0].shape[1]), lambda i: (i, 0)),
                                       pl.BlockSpec(xs[1].shape, lambda i: (0, 0)), row]
    return pl.pallas_call(
        body, name=name, grid=(S // T,), in_specs=x_specs + [vec, row],
        out_specs=(pl.BlockSpec((8, 128), lambda i: (0, 0)), row, row, vec),
        out_shape=(jax.ShapeDtypeStruct((8, 128), F32), jax.ShapeDtypeStruct((S, D), F32), jax.ShapeDtypeStruct((S, D), _MXU),
                   jax.ShapeDtypeStruct((1, D), F32)),
        compiler_params=_cparams(("arbitrary",), V7X_VMEM_LIMIT if fused else None),
    )(*xs, w, tgt)


def _shifted(ext, back, lo, n):
    if back == 0:
        return ext[lo:lo + n, :]
    return pltpu.roll(ext, back % ext.shape[0], 0)[lo:lo + n, :]


def _conv_windows(ext, K, T):
    return [_shifted(ext, (K - 1) - i, 8, T) for i in range(K)]


def _conv_taps(ext, w, K, T):
    out = None
    for i, win in enumerate(_conv_windows(ext, K, T)):
        term = win * w[i:i + 1, :]
        out = term if out is None else out + term
    return out


def _conv_taps_t(ext, w, K, T):
    out = None
    for i in range(K):
        term = _shifted(ext, i - (K - 1), 0, T) * w[i:i + 1, :]
        out = term if out is None else out + term
    return out


def _tri_masks(C):
    r = lax.broadcasted_iota(jnp.int32, (C, C), 0)
    c = lax.broadcasted_iota(jnp.int32, (C, C), 1)
    return r == c, r >= c, r > c, r <= c


_NN, _NT, _TN = ((1,), (0,)), ((1,), (1,)), ((0,), (0,))
_GDN_PASSES = dict(qk=1, inv=1, sol=1, scan=1, bwd=1)


def _bdot_raw(a, b, kind, passes):
    dims = ({"NN": ((2,), (1,)), "NT": ((2,), (2,)), "TN": ((1,), (1,))}[kind], ((0,), (0,)))
    if passes == 0:
        return lax.dot_general(a, b, dims, precision=_HI, preferred_element_type=F32)
    ah, bh = a.astype(BF16), b.astype(BF16)
    out = lax.dot_general(ah, bh, dims, preferred_element_type=F32)
    if passes == 3:
        al, bl = (a - ah.astype(F32)).astype(BF16), (b - bh.astype(F32)).astype(BF16)
        out = out + lax.dot_general(ah, bl, dims, preferred_element_type=F32) + lax.dot_general(al, bh, dims, preferred_element_type=F32)
    return out


@functools.partial(jax.custom_vjp, nondiff_argnums=(2, 3))
def _bdot(a, b, kind, passes):
    return _bdot_raw(a, b, kind, passes)


def _bdot_fwd(a, b, kind, passes):
    return _bdot_raw(a, b, kind, passes), (a, b)


def _bdot_bwd(kind, passes, res, ct):
    a, b = res
    if kind == "NN":
        return _bdot_raw(ct, b, "NT", passes), _bdot_raw(a, ct, "TN", passes)
    if kind == "NT":
        return _bdot_raw(ct, b, "NN", passes), _bdot_raw(ct, a, "TN", passes)
    return _bdot_raw(b, ct, "NT", passes), _bdot_raw(a, ct, "NN", passes)


_bdot.defvjp(_bdot_fwd, _bdot_bwd)


def _softplus(x):
    return jnp.maximum(x, 0.0) + jnp.log(1.0 + jnp.exp(-jnp.abs(x)))


def _gdn_stage1(cq, ck, cv, b_col, a_col, alog, dtb, dot=_bdot_raw):
    C = cq.shape[1]
    eye, incl, strict, incl_t = _tri_masks(C)
    qn = cq * lax.rsqrt(jnp.sum(cq * cq, axis=-1, keepdims=True) + EPS) * (GDN_DIM ** -0.5)
    kn = ck * lax.rsqrt(jnp.sum(ck * ck, axis=-1, keepdims=True) + EPS)
    beta = jax.nn.sigmoid(b_col)
    g = -jnp.exp(alog) * _softplus(a_col + dtb)
    g_row = jnp.sum(jnp.where(eye, g, 0.0), axis=1, keepdims=True)
    beta_row = jnp.sum(jnp.where(eye, beta, 0.0), axis=1, keepdims=True)
    gc_col = jnp.sum(jnp.where(incl, g_row, 0.0), axis=2, keepdims=True)
    gc_row = jnp.sum(jnp.where(incl_t, g, 0.0), axis=1, keepdims=True)
    dec = jnp.where(incl, jnp.exp(jnp.where(incl, gc_col - gc_row, 0.0)), 0.0)
    kk = dot(kn, kn, "NT", _GDN_PASSES["qk"])
    qk = dot(qn, kn, "NT", _GDN_PASSES["qk"])
    lmat = jnp.where(strict, dec * kk * beta_row, 0.0)
    attn = dec * qk * beta_row
    gam = jnp.exp(gc_col)
    gc_last = gc_col[:, C - 1:C, :]
    k_end = kn * (jnp.exp(gc_last - gc_col) * beta)
    return lmat, cv, gam * kn, gam * qn, attn, k_end, jnp.exp(gc_last)


def _tri_inv(lmat):
    C = lmat.shape[1]
    eye = _tri_masks(C)[0]
    ps = _GDN_PASSES["inv"]
    p = jnp.where(eye, 1.0, 0.0) - lmat
    lp = _bdot_raw(lmat, lmat, "NN", ps)
    n = int(math.log2(C))
    for s in range(1, n):
        p = p + _bdot_raw(p, lp, "NN", ps)
        if s < n - 1:
            lp = _bdot_raw(lp, lp, "NN", ps)
    return p


def _gated_norm(o, z, gnw):
    on = o * lax.rsqrt(jnp.mean(o * o, axis=-1, keepdims=True) + EPS) * gnw
    return on * _silu(z)


GDN_PG = 4
GDN_SG = 4


def _gdn_pairs(c, ba, gp, G):
    C, W, H = GDN_CHUNK, GDN_WIDTH, GDN_HEADS
    pairs = [(j, h) for j in range(G) for h in range(H)]
    cq, ck, cv = (jnp.stack([c[C * j:C * (j + 1), o + GDN_DIM * h:o + GDN_DIM * (h + 1)] for j, h in pairs]) for o in (0, W, 2 * W))
    b_col = jnp.stack([ba[C * j:C * (j + 1), h:h + 1] for j, h in pairs])
    a_col = jnp.stack([ba[C * j:C * (j + 1), H + h:H + h + 1] for j, h in pairs])
    alog = jnp.stack([gp[0:1, h:h + 1] for j, h in pairs])
    dtb = jnp.stack([gp[0:1, H + h:H + h + 1] for j, h in pairs])
    return pairs, (cq, ck, cv, b_col, a_col, alog, dtb)


def _gdn_pre_specs(S, G):
    C = GDN_CHUNK
    T = C * G
    return dict(
        cur=pl.BlockSpec((T, 3 * GDN_WIDTH), lambda i: (i, 0)),
        prev=pl.BlockSpec((8, 3 * GDN_WIDTH), lambda i: (jnp.maximum(i * (T // 8) - 1, 0), 0)),
        ba=pl.BlockSpec((T, 128), lambda i: (i, P_BA // 128)),
        cw=pl.BlockSpec((GDN_CONV, 3 * GDN_WIDTH), lambda i: (0, 0)),
        vec=pl.BlockSpec((1, 128), lambda i: (0, 0)),
        hd=pl.BlockSpec((GDN_HEADS, T, GDN_DIM), lambda i: (0, i, 0)),
        hc=pl.BlockSpec((GDN_HEADS, T, C), lambda i: (0, i, 0)),
        ge=pl.BlockSpec((G, GDN_HEADS, 8, 128), lambda i: (i, 0, 0, 0)),
    )


def _hd_shape(S, last=GDN_DIM):
    return jax.ShapeDtypeStruct((GDN_HEADS, S, last), F32)


def _gdn_pre(proj, conv_w, gp):
    S = proj.shape[0]
    C, G = GDN_CHUNK, GDN_PG
    nc = S // C
    sp = _gdn_pre_specs(S, G)

    def body(cur_ref, prev_ref, ba_ref, cw_ref, gp_ref, uv_ref, wk_ref, qd_ref, ke_ref, at_ref, ti_ref, ge_ref):
        prev = prev_ref[...] * jnp.where(pl.program_id(0) == 0, 0.0, 1.0)
        c = _silu(_conv_taps(jnp.concatenate([prev, cur_ref[...]], axis=0), cw_ref[...], GDN_CONV, C * G))
        pairs, args = _gdn_pairs(c, ba_ref[...], gp_ref[...], G)
        lmat, v, rk, q_dec, attn, k_end, g_end = _gdn_stage1(*args)
        t = _tri_inv(lmat)
        u_v = _bdot_raw(t, v, "NN", _GDN_PASSES["sol"])
        w_k = _bdot_raw(t, rk, "NN", _GDN_PASSES["sol"])
        for b, (j, h) in enumerate(pairs):
            rows = slice(C * j, C * (j + 1))
            uv_ref[h, rows, :] = u_v[b]
            wk_ref[h, rows, :] = w_k[b]
            qd_ref[h, rows, :] = q_dec[b]
            ke_ref[h, rows, :] = k_end[b]
            at_ref[h, rows, :] = attn[b]
            ti_ref[h, rows, :] = t[b]
            ge_ref[j, h] = jnp.broadcast_to(g_end[b], (8, 128))

    return pl.pallas_call(
        body, name="gdn_pre", grid=(nc // G,),
        in_specs=[sp["cur"], sp["prev"], sp["ba"], sp["cw"], sp["vec"]],
        out_specs=(sp["hd"], sp["hd"], sp["hd"], sp["hd"], sp["hc"], sp["hc"], sp["ge"]),
        out_shape=(_hd_shape(S), _hd_shape(S), _hd_shape(S), _hd_shape(S), _hd_shape(S, C), _hd_shape(S, C),
                   jax.ShapeDtypeStruct((nc, GDN_HEADS, 8, 128), F32)),
        compiler_params=_cparams(("parallel",)),
    )(proj, proj, proj, conv_w, gp)


def _gdn_scan_specs(S, G, rev):
    C = GDN_CHUNK
    T = C * G
    n = S // T
    ci = (lambda i: n - 1 - i) if rev else (lambda i: i)
    return dict(
        hd=pl.BlockSpec((GDN_HEADS, T, GDN_DIM), lambda i: (0, ci(i), 0)),
        hc=pl.BlockSpec((GDN_HEADS, T, C), lambda i: (0, ci(i), 0)),
        ge=pl.BlockSpec((G, GDN_HEADS, 8, 128), lambda i: (ci(i), 0, 0, 0)),
        z=pl.BlockSpec((T, GDN_WIDTH), lambda i: (ci(i), P_Z // GDN_WIDTH)),
        oa=pl.BlockSpec((T, GDN_WIDTH), lambda i: (ci(i), 0)),
        vec=pl.BlockSpec((1, 128), lambda i: (0, 0)),
        st=pl.BlockSpec((G, GDN_HEADS, GDN_DIM, GDN_DIM), lambda i: (ci(i), 0, 0, 0)),
    )


def _gdn_scan(u_v, w_k, q_dec, k_end, attn, g_end, proj, gnw, mix, after=()):
    S = proj.shape[0]
    C, G = GDN_CHUNK, GDN_SG
    nc = S // C
    sp = _gdn_scan_specs(S, G, False)
    ps = _GDN_PASSES["scan"]

    def body(uv_ref, wk_ref, qd_ref, ke_ref, at_ref, ge_ref, z_ref, gnw_ref, *rest):
        oa_ref, st_ref, s_scr = rest[-3:]

        @pl.when(pl.program_id(0) == 0)
        def _():
            s_scr[...] = jnp.zeros_like(s_scr)

        for j in range(G):
            rows = slice(C * j, C * (j + 1))
            st = s_scr[...]
            st_ref[j] = st
            u = uv_ref[:, rows, :] - _bdot_raw(wk_ref[:, rows, :], st, "NN", ps)
            o = _bdot_raw(qd_ref[:, rows, :], st, "NN", ps) + _bdot_raw(at_ref[:, rows, :], u, "NN", ps)
            s_scr[...] = ge_ref[j][:, 0:1, 0:1] * st + _bdot_raw(ke_ref[:, rows, :], u, "TN", ps)
            for h in range(GDN_HEADS):
                cols = slice(GDN_DIM * h, GDN_DIM * (h + 1))
                oa_ref[rows, cols] = _gated_norm(o[h], z_ref[rows, cols], gnw_ref[...])

    return pl.pallas_call(
        body, name="gdn_scan", grid=(nc // G,),
        in_specs=[sp["hd"], sp["hd"], sp["hd"], sp["hd"], sp["hc"], sp["ge"], sp["z"], sp["vec"]] + [_ANY] * (1 + len(after)),
        out_specs=(sp["oa"], sp["st"]),
        out_shape=(jax.ShapeDtypeStruct(mix.shape, F32),
                   jax.ShapeDtypeStruct((nc, GDN_HEADS, GDN_DIM, GDN_DIM), F32)),
        input_output_aliases={8: 0},
        scratch_shapes=[pltpu.VMEM((GDN_HEADS, GDN_DIM, GDN_DIM), F32)],
        compiler_params=_cparams(("arbitrary",)),
    )(u_v, w_k, q_dec, k_end, attn, g_end, proj, gnw, mix, *after)


def _gdn_scan_bwd(u_v, w_k, q_dec, k_end, attn, g_end, proj, gnw, states, d_oa):
    S = proj.shape[0]
    C, G = GDN_CHUNK, GDN_SG
    nc = S // C
    sp = _gdn_scan_specs(S, G, True)
    ps, pb = _GDN_PASSES["scan"], _GDN_PASSES["bwd"]

    def body(uv_ref, wk_ref, qd_ref, ke_ref, at_ref, ge_ref, z_ref, gnw_ref, st_ref, doa_ref,
             duv_ref, dwk_ref, dqd_ref, dke_ref, dat_ref, dge_ref, dz_ref, dgnw_ref, ds_scr):
        @pl.when(pl.program_id(0) == 0)
        def _():
            ds_scr[...] = jnp.zeros_like(ds_scr)
            dgnw_ref[...] = jnp.zeros_like(dgnw_ref)

        dgnw = jnp.zeros((1, 128), F32)
        for j in reversed(range(G)):
            rows = slice(C * j, C * (j + 1))
            st = st_ref[j]
            wk, qd, ke, at = wk_ref[:, rows, :], qd_ref[:, rows, :], ke_ref[:, rows, :], at_ref[:, rows, :]
            u = uv_ref[:, rows, :] - _bdot_raw(wk, st, "NN", ps)
            o = _bdot_raw(qd, st, "NN", ps) + _bdot_raw(at, u, "NN", ps)
            dos = []
            for h in range(GDN_HEADS):
                cols = slice(GDN_DIM * h, GDN_DIM * (h + 1))
                _, vjp2 = jax.vjp(_gated_norm, o[h], z_ref[rows, cols], gnw_ref[...])
                do_h, dz_h, dgn = vjp2(doa_ref[rows, cols])
                dz_ref[rows, cols] = dz_h
                dgnw = dgnw + dgn
                dos.append(do_h)
            do = jnp.stack(dos)
            ds_new = ds_scr[...]
            du = _bdot_raw(at, do, "TN", pb) + _bdot_raw(ke, ds_new, "NN", pb)
            duv_ref[:, rows, :] = du
            dat_ref[:, rows, :] = _bdot_raw(do, u, "NT", pb)
            dqd_ref[:, rows, :] = _bdot_raw(do, st, "NT", pb)
            dke_ref[:, rows, :] = _bdot_raw(u, ds_new, "NT", pb)
            dwk_ref[:, rows, :] = -_bdot_raw(du, st, "NT", pb)
            d_ge = jnp.sum(jnp.sum(st * ds_new, axis=2, keepdims=True), axis=1, keepdims=True)
            dge_ref[j] = jnp.broadcast_to(d_ge, (GDN_HEADS, 8, 128))
            ds_scr[...] = ge_ref[j][:, 0:1, 0:1] * ds_new + _bdot_raw(qd, do, "TN", pb) - _bdot_raw(wk, du, "TN", pb)
        dgnw_ref[...] += dgnw

    return pl.pallas_call(
        body, name="gdn_scan_bwd", grid=(nc // G,),
        in_specs=[sp["hd"], sp["hd"], sp["hd"], sp["hd"], sp["hc"], sp["ge"], sp["z"], sp["vec"], sp["st"], sp["oa"]],
        out_specs=(sp["hd"], sp["hd"], sp["hd"], sp["hd"], sp["hc"], sp["ge"], sp["oa"], sp["vec"]),
        out_shape=(_hd_shape(S), _hd_shape(S), _hd_shape(S), _hd_shape(S), _hd_shape(S, C),
                   jax.ShapeDtypeStruct((nc, GDN_HEADS, 8, 128), F32), jax.ShapeDtypeStruct((S, GDN_WIDTH), F32),
                   jax.ShapeDtypeStruct((1, 128), F32)),
        scratch_shapes=[pltpu.VMEM((GDN_HEADS, GDN_DIM, GDN_DIM), F32)],
        compiler_params=_cparams(("arbitrary",)),
    )(u_v, w_k, q_dec, k_end, attn, g_end, proj, gnw, states, d_oa)


def _gdn_post(proj, conv_w, gp, tinv, u_v, w_k, d_uv, d_wk, d_qd, d_ke, d_at, d_ge):
    S = proj.shape[0]
    C, G = GDN_CHUNK, GDN_PG
    nc = S // C
    sp = _gdn_pre_specs(S, G)
    pb = _GDN_PASSES["bwd"]

    def body(cur_ref, prev_ref, ba_ref, cw_ref, gp_ref, ti_ref, uv_ref, wk_ref, duv_ref, dwk_ref, dqd_ref, dke_ref,
             dat_ref, dge_ref, dpre_ref, dba_ref, dgp_ref):
        i = pl.program_id(0)

        @pl.when(i == 0)
        def _():
            dgp_ref[...] = jnp.zeros_like(dgp_ref)

        prev = prev_ref[...] * jnp.where(i == 0, 0.0, 1.0)
        pre = _conv_taps(jnp.concatenate([prev, cur_ref[...]], axis=0), cw_ref[...], GDN_CONV, C * G)
        sg = jax.nn.sigmoid(pre)
        dsilu = sg * (1.0 + pre * (1.0 - sg))
        pairs, args = _gdn_pairs(pre * sg, ba_ref[...], gp_ref[...], G)
        _, vjp1 = jax.vjp(functools.partial(_gdn_stage1, dot=_bdot), *args)

        def take(ref):
            return jnp.stack([ref[h, C * j:C * (j + 1), :] for j, h in pairs])

        t, u_v, w_k = take(ti_ref), take(uv_ref), take(wk_ref)
        d_v = _bdot_raw(t, take(duv_ref), "TN", pb)
        d_rk = _bdot_raw(t, take(dwk_ref), "TN", pb)
        d_l = -(_bdot_raw(d_v, u_v, "NT", pb) + _bdot_raw(d_rk, w_k, "NT", pb))
        d_ge = jnp.stack([dge_ref[j, h][0:1, 0:1] for j, h in pairs])
        dcq, dck, dcv, db, da, dalog, ddtb = vjp1((d_l, d_v, d_rk, take(dqd_ref), take(dat_ref), take(dke_ref), d_ge))
        lane = lax.broadcasted_iota(jnp.int32, (C, 128), 1)
        lane1 = lax.broadcasted_iota(jnp.int32, (1, 128), 1)
        dgp = jnp.zeros((1, 128), F32)
        for j in range(G):
            rows = slice(C * j, C * (j + 1))
            dba = jnp.zeros((C, 128), F32)
            for h in range(GDN_HEADS):
                b = GDN_HEADS * j + h
                for o_, dcx in ((0, dcq), (GDN_WIDTH, dck), (2 * GDN_WIDTH, dcv)):
                    cols = slice(o_ + GDN_DIM * h, o_ + GDN_DIM * (h + 1))
                    dpre_ref[rows, cols] = dcx[b] * dsilu[rows, cols]
                dba = dba + jnp.where(lane == h, db[b], 0.0) + jnp.where(lane == GDN_HEADS + h, da[b], 0.0)
                dgp = dgp + jnp.where(lane1 == h, dalog[b], 0.0) + jnp.where(lane1 == GDN_HEADS + h, ddtb[b], 0.0)
            dba_ref[rows, :] = dba
        dgp_ref[0:1, :] += dgp

    T = C * G
    return pl.pallas_call(
        body, name="gdn_post", grid=(nc // G,),
        in_specs=[sp["cur"], sp["prev"], sp["ba"], sp["cw"], sp["vec"], sp["hc"], sp["hd"], sp["hd"], sp["hd"], sp["hd"],
                  sp["hd"], sp["hd"], sp["hc"], sp["ge"]],
        out_specs=(sp["cur"], pl.BlockSpec((T, 128), lambda i: (i, 0)), pl.BlockSpec((8, 128), lambda i: (0, 0))),
        out_shape=(jax.ShapeDtypeStruct((S, 3 * GDN_WIDTH), F32), jax.ShapeDtypeStruct((S, 128), F32),
                   jax.ShapeDtypeStruct((8, 128), F32)),
        compiler_params=_cparams(("arbitrary",)),
    )(proj, proj, proj, conv_w, gp, tinv, u_v, w_k, d_uv, d_wk, d_qd, d_ke, d_at, d_ge)


def _conv_bwd(dpre, x, xcol0, w, K, name, tc):
    S, Cc = dpre.shape
    T = _pick_tile(S, 256)
    nt, ncol = S // T, Cc // tc
    xo = xcol0 // tc

    def body(d_ref, dn_ref, x_ref, xp_ref, w_ref, dx_ref, dw_ref):
        i = pl.program_id(1)
        dn = dn_ref[...] * jnp.where(i == nt - 1, 0.0, 1.0)
        dv = d_ref[...]
        ext_d = jnp.concatenate([dv, dn], axis=0)
        dx_ref[...] = _conv_taps_t(ext_d, w_ref[...], K, T).astype(dx_ref.dtype)
        xp = xp_ref[...] * jnp.where(i == 0, 0.0, 1.0)
        ext_x = jnp.concatenate([xp, x_ref[...]], axis=0)

        @pl.when(i == 0)
        def _():
            dw_ref[...] = jnp.zeros_like(dw_ref)

        for k in range(K):
            dw_ref[k:k + 1, :] += jnp.sum(dv * _shifted(ext_x, (K - 1) - k, 8, T), axis=0, keepdims=True)

    r8 = T // 8
    return pl.pallas_call(
        body, name=name, grid=(ncol, nt),
        in_specs=[pl.BlockSpec((T, tc), lambda j, i: (i, j)),
                  pl.BlockSpec((8, tc), lambda j, i: (jnp.minimum((i + 1) * r8, S // 8 - 1), j)),
                  pl.BlockSpec((T, tc), lambda j, i: (i, j + xo)),
                  pl.BlockSpec((8, tc), lambda j, i: (jnp.maximum(i * r8 - 1, 0), j + xo)),
                  pl.BlockSpec((K, tc), lambda j, i: (0, j))],
        out_specs=(pl.BlockSpec((T, tc), lambda j, i: (i, j)), pl.BlockSpec((K, tc), lambda j, i: (0, j))),
        out_shape=(jax.ShapeDtypeStruct((S, Cc), _MXU), jax.ShapeDtypeStruct((K, Cc), F32)),
        compiler_params=_cparams(("parallel", "arbitrary")),
    )(dpre, dpre, x, x, w)


def _dil_bias(nt, T):
    d = (np.arange(nt)[:, None, None] * T + np.arange(T)[None, None, :] - np.arange(T)[None, :, None])
    cnt = ((d >= 0) & (d <= 128)).astype(np.float64) + ((d >= 0) & (d % 4 == 0) & (d <= 512)) + ((d >= 0) & (d % 16 == 0))
    return jnp.asarray(np.where(cnt > 0, np.log(np.maximum(cnt, 1.0)), -1e30), dtype=F32)


def _attn_fwd(proj, after=()):
    S = proj.shape[0]
    T = min(ATT_T, S)
    nt, H = S // T, T // 2
    bias = _dil_bias(nt, T)
    scale = DIL_DIM ** -0.5
    npair = DIL_WIDTH // 128
    qb0, kb0, vb0 = P_QKVB // 128, (P_QKVB + DIL_WIDTH) // 128, (P_QKVB + 2 * DIL_WIDTH) // 128

    def body(q_ref, k_ref, v_ref, b_ref, *rest):
        o_ref, lse_ref = rest[-2:]
        i = pl.program_id(1)
        qs = (q_ref[...] * scale).astype(_MXU)

        def update(carry, kt, vt, qt, bt):
            out = []
            for hh in range(2):
                m, l, acc = carry[hh]
                sl = slice(hh * DIL_DIM, (hh + 1) * DIL_DIM)
                s = lax.dot_general(kt[:, sl], qt[:, sl], (_NT, ((), ())), preferred_element_type=F32) + bt
                m_new = jnp.maximum(m, jnp.max(s, axis=0, keepdims=True))
                p = jnp.exp(s - m_new)
                a = jnp.exp(m - m_new)
                l = a * l + jnp.sum(p, axis=0, keepdims=True)
                acc = a * acc + lax.dot_general(vt[:, sl], p.astype(_MXU), (_TN, ((), ())), preferred_element_type=F32)
                out.append((m_new, l, acc))
            return tuple(out)

        def keys(j):
            rows = pl.ds(pl.multiple_of(j * T, T), T)
            return k_ref[rows, :].astype(_MXU), v_ref[rows, :].astype(_MXU)

        init = tuple((jnp.full((1, T), -1e30, F32), jnp.zeros((1, T), F32), jnp.zeros((DIL_DIM, T), F32)) for _ in range(2))
        res = lax.fori_loop(0, i, lambda j, carry: update(carry, *keys(j), qs, b_ref[i - j]), init)
        kd, vd = keys(i)
        res = update(res, kd[:H], vd[:H], qs, b_ref[0, :H, :])
        late = update(tuple(tuple(t[:, H:] for t in r) for r in res), kd[H:], vd[H:], qs[H:], b_ref[0, H:, H:])
        res = tuple(tuple(jnp.concatenate([t[:, :H], u], axis=1) for t, u in zip(r, r2)) for r, r2 in zip(res, late))
        lse_ref[...] = jnp.zeros_like(lse_ref)
        for hh in range(2):
            m, l, acc = res[hh]
            o_ref[:, hh * DIL_DIM:(hh + 1) * DIL_DIM] = (acc / l).T
            lse_ref[hh:hh + 1, :] = m + jnp.log(l)

    return pl.pallas_call(
        body, name="attn_fwd", grid=(npair, nt),
        in_specs=[pl.BlockSpec((T, 128), lambda p, i: (i, qb0 + p)),
                  pl.BlockSpec((S, 128), lambda p, i: (0, kb0 + p)),
                  pl.BlockSpec((S, 128), lambda p, i: (0, vb0 + p)),
                  pl.BlockSpec((nt, T, T), lambda p, i: (0, 0, 0))] + [_ANY] * len(after),
        out_specs=(pl.BlockSpec((T, 128), lambda p, i: (i, GDN_WIDTH // 128 + p)),
                   pl.BlockSpec((None, None, 8, T), lambda p, i: (p, i, 0, 0))),
        out_shape=(jax.ShapeDtypeStruct((S, GDN_WIDTH + DIL_WIDTH), F32), jax.ShapeDtypeStruct((npair, nt, 8, T), F32)),
        compiler_params=_cparams(("parallel", "parallel")),
    )(proj, proj, proj, bias, *after)


def _attn_bwd(proj, mix, lse, d_mix):
    S = proj.shape[0]
    T = min(ATT_T, S)
    nt, H = S // T, T // 2
    bias = _dil_bias(nt, T)
    scale = DIL_DIM ** -0.5
    npair = DIL_WIDTH // 128
    qb0, kb0, vb0 = P_QKVB // 128, (P_QKVB + DIL_WIDTH) // 128, (P_QKVB + 2 * DIL_WIDTH) // 128

    def body(q_ref, k_ref, v_ref, o_ref, lse_ref, do_ref, b_ref, dq_ref, dk_ref, dv_ref, dq_scr):
        j = pl.program_id(1)

        @pl.when(j == 0)
        def _():
            dq_scr[...] = jnp.zeros_like(dq_scr)

        kt = k_ref[...].astype(_MXU)
        vt = v_ref[...].astype(_MXU)
        ones = jnp.ones((8, DIL_DIM), F32)

        def block(carry, kt, vt, rows, lsev, bt):
            qs = (q_ref[rows, :] * scale).astype(_MXU)
            dov = do_ref[rows, :]
            prod = dov * o_ref[rows, :]
            dob = dov.astype(_MXU)
            out = []
            dqs = []
            for hh in range(2):
                dk, dv = carry[hh]
                sl = slice(hh * DIL_DIM, (hh + 1) * DIL_DIM)
                s = lax.dot_general(kt[:, sl], qs[:, sl], (_NT, ((), ())), preferred_element_type=F32) + bt
                p = jnp.exp(s - lsev[hh:hh + 1, :])
                delta = lax.dot_general(ones, prod[:, sl], (_NT, ((), ())), precision=_HI, preferred_element_type=F32)[0:1, :]
                dp = lax.dot_general(vt[:, sl], dob[:, sl], (_NT, ((), ())), preferred_element_type=F32)
                ds = (p * (dp - delta)).astype(_MXU)
                dv = dv + lax.dot_general(p.astype(_MXU), dob[:, sl], (_NN, ((), ())), preferred_element_type=F32)
                dk = dk + lax.dot_general(ds, qs[:, sl], (_NN, ((), ())), preferred_element_type=F32)
                dqs.append(lax.dot_general(ds, kt[:, sl], (_TN, ((), ())), preferred_element_type=F32) * scale)
                out.append((dk, dv))
            dq_scr[rows, :] += jnp.concatenate(dqs, axis=1)
            return tuple(out)

        def step(i, carry):
            return block(carry, kt, vt, pl.ds(pl.multiple_of(i * T, T), T), lse_ref[i], b_ref[i - j])

        zeros = tuple((jnp.zeros((H, DIL_DIM), F32), jnp.zeros((H, DIL_DIM), F32)) for _ in range(2))
        lsed = lse_ref[j]
        early = block(zeros, kt[:H], vt[:H], pl.ds(pl.multiple_of(j * T, T), T), lsed, b_ref[0, :H, :])
        late = block(zeros, kt[H:], vt[H:], pl.ds(pl.multiple_of(j * T + H, H), H), lsed[:, H:], b_ref[0, H:, H:])
        init = tuple(tuple(jnp.concatenate([t, u], axis=0) for t, u in zip(r, r2)) for r, r2 in zip(early, late))
        res = lax.fori_loop(j + 1, nt, step, init)
        dk_ref[...] = jnp.concatenate([res[0][0], res[1][0]], axis=1).astype(dk_ref.dtype)
        dv_ref[...] = jnp.concatenate([res[0][1], res[1][1]], axis=1).astype(dv_ref.dtype)

        @pl.when(j == nt - 1)
        def _():
            dq_ref[...] = dq_scr[...].astype(dq_ref.dtype)

    full = lambda c0: pl.BlockSpec((S, 128), lambda p, j: (0, c0 + p))
    tile = lambda c0: pl.BlockSpec((T, 128), lambda p, j: (j, c0 + p))
    out3 = jax.ShapeDtypeStruct((S, DIL_WIDTH), _MXU)
    return pl.pallas_call(
        body, name="attn_bwd", grid=(npair, nt),
        in_specs=[full(qb0), tile(kb0), tile(vb0), full(GDN_WIDTH // 128),
                  pl.BlockSpec((None, nt, 8, T), lambda p, j: (p, 0, 0, 0)), full(GDN_WIDTH // 128),
                  pl.BlockSpec((nt, T, T), lambda p, j: (0, 0, 0))],
        out_specs=(full(0), tile(0), tile(0)),
        out_shape=(out3, out3, out3),
        scratch_shapes=[pltpu.VMEM((S, 128), F32)],
        compiler_params=_cparams(("parallel", "arbitrary")),
    )(proj, proj, proj, mix, lse, d_mix, bias)


def _ffn_act(up, cw):
    S, Cc = up.shape[0], up.shape[1] // 2
    T, tc = _pick_tile(S, 256), _pick_tile(Cc, 1536)
    r16 = T // 16
    nct = Cc // tc

    def body(g_ref, gp_ref, u_ref, up_ref, wg_ref, wu_ref, o_ref):
        keep = jnp.where(pl.program_id(1) == 0, 0.0, 1.0)
        cg = _conv_taps(jnp.concatenate([gp_ref[8:16, :].astype(F32) * keep, g_ref[...].astype(F32)], axis=0),
                        wg_ref[...], FFN_CONV, T)
        cu = _conv_taps(jnp.concatenate([up_ref[8:16, :].astype(F32) * keep, u_ref[...].astype(F32)], axis=0),
                        wu_ref[...], FFN_CONV, T)
        o_ref[...] = (_silu(cg) * cu).astype(o_ref.dtype)

    cur = lambda o: pl.BlockSpec((T, tc), lambda j, i: (i, j + o))
    prev = lambda o: pl.BlockSpec((16, tc), lambda j, i: (jnp.maximum(i * r16 - 1, 0), j + o))
    wsp = lambda o: pl.BlockSpec((FFN_CONV, tc), lambda j, i: (0, j + o))
    return pl.pallas_call(
        body, name="ffn_act", grid=(nct, S // T),
        in_specs=[cur(0), prev(0), cur(nct), prev(nct), wsp(0), wsp(nct)], out_specs=cur(0),
        out_shape=jax.ShapeDtypeStruct((S, Cc), _MXU),
        compiler_params=_cparams(("parallel", "parallel")),
    )(up, up, up, up, cw, cw)


def _ffn_act_bwd(d_act, up, cw):
    S, Cc = up.shape[0], up.shape[1] // 2
    T, tc = _pick_tile(S, 256), _pick_tile(Cc, 1536)
    r8, r16 = T // 8, T // 16
    nt = S // T
    nct = Cc // tc
    K = FFN_CONV

    def body(da_ref, dan_ref, g_ref, gp_ref, gn_ref, u_ref, up_ref, un_ref, wg_ref, wu_ref,
             dg_ref, du_ref, dwg_ref, dwu_ref):
        i = pl.program_id(1)
        keep_p = jnp.where(i == 0, 0.0, 1.0)
        keep_n = jnp.where(i == nt - 1, 0.0, 1.0)
        wg, wu = wg_ref[...], wu_ref[...]
        xg = jnp.concatenate([gp_ref[8:16, :].astype(F32) * keep_p, g_ref[...].astype(F32),
                              gn_ref[0:8, :].astype(F32) * keep_n], axis=0)
        xu = jnp.concatenate([up_ref[8:16, :].astype(F32) * keep_p, u_ref[...].astype(F32),
                              un_ref[0:8, :].astype(F32) * keep_n], axis=0)
        cg = _conv_taps(xg, wg, K, T + 8)
        cu = _conv_taps(xu, wu, K, T + 8)
        da = jnp.concatenate([da_ref[...], dan_ref[...] * keep_n], axis=0)
        sg = jax.nn.sigmoid(cg)
        d_cg = da * cu * (sg * (1.0 + cg * (1.0 - sg)))
        d_cu = da * (cg * sg)
        dg_ref[...] = _conv_taps_t(d_cg, wg, K, T).astype(dg_ref.dtype)
        du_ref[...] = _conv_taps_t(d_cu, wu, K, T).astype(du_ref.dtype)

        @pl.when(i == 0)
        def _():
            dwg_ref[...] = jnp.zeros_like(dwg_ref)
            dwu_ref[...] = jnp.zeros_like(dwu_ref)

        for k in range(K):
            dwg_ref[k:k + 1, :] += jnp.sum(d_cg[0:T, :] * _shifted(xg, (K - 1) - k, 8, T), axis=0, keepdims=True)
            dwu_ref[k:k + 1, :] += jnp.sum(d_cu[0:T, :] * _shifted(xu, (K - 1) - k, 8, T), axis=0, keepdims=True)

    cur = lambda o: pl.BlockSpec((T, tc), lambda j, i: (i, j + o))
    prev = lambda o: pl.BlockSpec((16, tc), lambda j, i: (jnp.maximum(i * r16 - 1, 0), j + o))
    nxt = lambda o: pl.BlockSpec((16, tc), lambda j, i: (jnp.minimum((i + 1) * r16, S // 16 - 1), j + o))
    nxt8 = pl.BlockSpec((8, tc), lambda j, i: (jnp.minimum((i + 1) * r8, S // 8 - 1), j))
    wsp = lambda o: pl.BlockSpec((K, tc), lambda j, i: (0, j + o))
    return pl.pallas_call(
        body, name="ffn_act_bwd", grid=(nct, nt),
        in_specs=[cur(0), nxt8, cur(0), prev(0), nxt(0), cur(nct), prev(nct), nxt(nct), wsp(0), wsp(nct)],
        out_specs=(cur(0), cur(0), wsp(0), wsp(0)),
        out_shape=(jax.ShapeDtypeStruct((S, Cc), _MXU), jax.ShapeDtypeStruct((S, Cc), _MXU),
                   jax.ShapeDtypeStruct((K, Cc), F32), jax.ShapeDtypeStruct((K, Cc), F32)),
        compiler_params=_cparams(("parallel", "arbitrary")),
    )(d_act, d_act, up, up, up, up, up, up, cw, cw)


def _local_step(x, tgt, h1, n1w, n2w, fnw, gp, gnw, wp, conv_w, fcw, rest_weights, early_grads):
    proj = _mm(h1, wp, "nn", name="proj")
    u_v, w_k, q_dec, k_end, attn, tinv, g_end = _gdn_pre(proj, conv_w, gp)
    mix, lse = _attn_fwd(proj)
    mix, states = _gdn_scan(u_v, w_k, q_dec, k_end, attn, g_end, proj, gnw, mix, after=[rest_weights[0]([mix])])
    w_out, w_up4, w_down = rest_weights[1]([mix])
    x2, h2 = _mm(mix, w_out, "nn", residual=x, name="outproj", normed_by=n2w)
    up = _mm(h2, w_up4, "nn", b_blocks=True, out_dtype=_MXU, name="up")
    act = _ffn_act(up, fcw)
    loss, dx3, dx3n, d_fnw = _loss_head((act, w_down, x2), fnw, tgt, "loss_head")
    d_act = _mm(dx3n, w_down, "nt", name="d_act")
    d_wdown = _mm(act, dx3n, "tn", name="d_wdown")
    d_upg, d_upu, d_fcwg, d_fcwu = _ffn_act_bwd(d_act, up, fcw)
    d_wup = _mm(h2, d_upg, "tn", place=("blocks", N_CHIPS, 0), tn=w_up4.shape[2], name="d_wgate")
    d_wup = _mm(h2, d_upu, "tn", place=("blocks", N_CHIPS, N_CHIPS // 2), tn=w_up4.shape[2], into=d_wup, name="d_wup")
    dx2, d_n2w = _rmsnorm_bwd(([d_upg, d_upu], w_up4), x2, n2w, dx3, "norm2_bwd")
    d_wout = _mm(mix, dx2, "tn", name="d_wout")
    token = early_grads[0](d_wup, d_wdown, d_wout)
    d_mix = _mm(dx2, w_out, "nt", name="d_mix", after=[token])
    dq_b, dk_b, dv_b = _attn_bwd(proj, mix, lse, d_mix)
    d_uv, d_wk, d_qd, d_ke, d_at, d_ge, d_z, d_gnw = _gdn_scan_bwd(u_v, w_k, q_dec, k_end, attn, g_end, proj,
                                                                   gnw, states, d_mix)
    d_pre, d_ba, d_gp = _gdn_post(proj, conv_w, gp, tinv, u_v, w_k, d_uv, d_wk, d_qd, d_ke, d_at, d_ge)
    token = early_grads[1]([d_pre])
    d_qkva, d_convw = _conv_bwd(d_pre, proj, 0, conv_w + token[0:1, 0:1], GDN_CONV, "gdn_conv_bwd", 512)
    d_proj = jnp.concatenate([d_qkva, d_z.astype(_MXU), dq_b, dk_b, dv_b, d_ba.astype(_MXU),
                              jnp.zeros((x.shape[0], P_COLS - P_BA - 128), _MXU)], axis=1)
    d_wp = _mm(h1, d_proj, "tn", name="d_wp")
    token = early_grads[2](d_wp)
    dx, d_n1w = _rmsnorm_bwd(([d_proj], wp[None]), x, n1w, dx2, "norm1_bwd", after=[token])
    grads = dict(wp=d_wp, conv_w=d_convw, w_out=d_wout, w_up=d_wup, fcw_g=d_fcwg, fcw_u=d_fcwu, w_down=d_wdown,
                 n1w=d_n1w, n2w=d_n2w, fnw=d_fnw, gp=d_gp, gnw=d_gnw)
    return loss, dx, grads


_HBM = pl.BlockSpec(memory_space=pltpu.HBM)


def _pos():
    return lax.axis_index("x"), lax.axis_index("y"), lax.axis_index("c")


def _other_chips(x, y):
    return [(1 - x, y), (x, 1 - y), (1 - x, 1 - y)]


def _halvable(shape):
    return shape[0] % 32 == 0


def _rows_of_half(shape, half):
    if not _halvable(shape):
        return pl.ds(0, shape[0])
    return pl.ds(pl.multiple_of(half * (shape[0] // 2), 16), shape[0] // 2)


_SEM = pl.BlockSpec(memory_space=pltpu.SEMAPHORE)
_ANY = pl.BlockSpec(memory_space=pl.ANY)
_DATAFLOW = pltpu.SideEffectType.DATAFLOW_SIDE_EFFECTING


def _in_hbm(a):
    return pltpu.with_memory_space_constraint(a, pltpu.HBM)


def _halves_copy(src_refs, land_refs, send_sems, recv_sems, shapes, a, j, block, x, y, c):
    px, py = _other_chips(x, y)[j]
    rows = _rows_of_half(shapes[a], c)
    return pltpu.make_async_remote_copy(
        src_ref=src_refs[a].at[rows, :], dst_ref=land_refs[a].at[block, rows, :], send_sem=send_sems.at[3 * a + j],
        recv_sem=recv_sems.at[3 * a + j], device_id=(px, py, c), device_id_type=MESH)


def _gather_halves_start(shards, after, name):
    n = len(shards)
    shapes = [s.shape for s in shards]

    def body(*refs):
        ins, lands = refs[:n], refs[n:2 * n]
        send_sems, recv_sems = refs[2 * n + 1], refs[2 * n + 2]
        token = refs[-1]
        x, y, c = _pos()
        q = 2 * x + y
        for a in range(n):
            for j in range(3):
                _halves_copy(ins, lands, send_sems, recv_sems, shapes, a, j, q, x, y, c).start()
        token[...] = jnp.zeros_like(token)

    land_shapes = [(N_CHIPS,) + s.shape for s in shards]
    return pl.pallas_call(
        body, name=name,
        out_shape=(pltpu.SemaphoreType.DMA((3 * n,)), pltpu.SemaphoreType.DMA((3 * n,)),
                   *[pltpu.HBM(s.shape, s.dtype) for s in shards],
                   *[pltpu.HBM(ls, s.dtype) for ls, s in zip(land_shapes, shards)],
                   jax.ShapeDtypeStruct((8, 128), F32)),
        in_specs=[_HBM] * (2 * n) + [_ANY],
        out_specs=(_SEM, _SEM, *[_HBM] * (2 * n), pl.BlockSpec(memory_space=pltpu.VMEM)),
        input_output_aliases={a: 2 + a for a in range(2 * n)},
        compiler_params=pltpu.CompilerParams(has_side_effects=_DATAFLOW),
    )(*[_in_hbm(s) for s in shards], *[_in_hbm(lax.empty(ls, s.dtype)) for ls, s in zip(land_shapes, shards)], after)


def _gather_halves_wait(started, after, name):
    send_sems, recv_sems, *thru = started
    n = len(thru) // 2
    shapes = [t.shape for t in thru[:n]]

    def body(*refs):
        ins, lands = refs[:n], refs[n:2 * n]
        send_sems, recv_sems = refs[2 * n], refs[2 * n + 1]
        x, y, c = _pos()
        q = 2 * x + y
        chips = _other_chips(x, y)
        for a in range(n):
            for j, (px, py) in enumerate(chips):
                _halves_copy(ins, lands, send_sems, recv_sems, shapes, a, j, q, x, y, c).wait_send()
                _halves_copy(ins, lands, send_sems, recv_sems, shapes, a, j, 2 * px + py, x, y, c).wait_recv()

    outs = pl.pallas_call(
        body, name=name, out_shape=[pltpu.HBM(t.shape, t.dtype) for t in thru],
        in_specs=[_HBM] * (2 * n) + [_SEM, _SEM] + [_ANY] * len(after), out_specs=[_HBM] * (2 * n),
        input_output_aliases={a: a for a in range(2 * n)},
        compiler_params=pltpu.CompilerParams(has_side_effects=_DATAFLOW),
    )(*thru, send_sems, recv_sems, *after)
    return outs[:n], outs[n:]


def _sibling_fill(gathered, name):
    big = [a for a, g in enumerate(gathered) if _halvable(g.shape[1:])]
    n = len(gathered)

    def body(*refs):
        ins, outs = refs[:n], refs[n:2 * n]
        send_sems, recv_sems = refs[2 * n:]
        x, y, c = _pos()
        chips = _other_chips(x, y)

        def copy(k, j, half):
            a = big[k]
            px, py = chips[j]
            rows = _rows_of_half(gathered[a].shape[1:], half)
            return pltpu.make_async_remote_copy(
                src_ref=ins[a].at[2 * px + py, rows, :], dst_ref=outs[a].at[2 * px + py, rows, :],
                send_sem=send_sems.at[3 * k + j], recv_sem=recv_sems.at[3 * k + j],
                device_id=(x, y, 1 - c), device_id_type=MESH)

        sends = [copy(k, j, c) for k in range(len(big)) for j in range(3)]
        for cp in sends:
            cp.start()
        for k in range(len(big)):
            for j in range(3):
                copy(k, j, 1 - c).wait_recv()
        for cp in sends:
            cp.wait_send()

    return pl.pallas_call(
        body, name=name, in_specs=[_HBM] * n, out_specs=[_HBM] * n,
        out_shape=[jax.ShapeDtypeStruct(g.shape, g.dtype) for g in gathered],
        input_output_aliases={a: a for a in range(n)},
        scratch_shapes=[pltpu.SemaphoreType.DMA((3 * len(big),)), pltpu.SemaphoreType.DMA((3 * len(big),))],
    )(*gathered)


def _fill_copy(refs, send_sems, recv_sems, shapes, a, j, half, x, y, c):
    px, py = _other_chips(x, y)[j]
    rows = _rows_of_half(shapes[a], half)
    return pltpu.make_async_remote_copy(
        src_ref=refs[a].at[2 * px + py, rows, :], dst_ref=refs[a].at[2 * px + py, rows, :],
        send_sem=send_sems.at[3 * a + j], recv_sem=recv_sems.at[3 * a + j],
        device_id=(x, y, 1 - c), device_id_type=MESH)


def _sibling_fill_start(gathered, name):
    n = len(gathered)
    shapes = [g.shape[1:] for g in gathered]

    def body(*refs):
        ins = refs[:n]
        send_sems, recv_sems = refs[n], refs[n + 1]
        token = refs[-1]
        x, y, c = _pos()
        for a in range(n):
            for j in range(3):
                _fill_copy(ins, send_sems, recv_sems, shapes, a, j, c, x, y, c).start()
        token[...] = jnp.zeros_like(token)

    return pl.pallas_call(
        body, name=name,
        out_shape=(pltpu.SemaphoreType.DMA((3 * n,)), pltpu.SemaphoreType.DMA((3 * n,)),
                   *[pltpu.HBM(g.shape, g.dtype) for g in gathered], jax.ShapeDtypeStruct((8, 128), F32)),
        in_specs=[_HBM] * n,
        out_specs=(_SEM, _SEM, *[_HBM] * n, pl.BlockSpec(memory_space=pltpu.VMEM)),
        input_output_aliases={a: 2 + a for a in range(n)},
        compiler_params=pltpu.CompilerParams(has_side_effects=_DATAFLOW),
    )(*[_in_hbm(g) for g in gathered])


def _sibling_fill_wait(started, after, name):
    send_sems, recv_sems, *thru = started
    n = len(thru)
    shapes = [t.shape[1:] for t in thru]

    def body(*refs):
        ins = refs[:n]
        send_sems, recv_sems = refs[n], refs[n + 1]
        x, y, c = _pos()
        for a in range(n):
            for j in range(3):
                _fill_copy(ins, send_sems, recv_sems, shapes, a, j, c, x, y, c).wait_send()
                _fill_copy(ins, send_sems, recv_sems, shapes, a, j, 1 - c, x, y, c).wait_recv()

    return pl.pallas_call(
        body, name=name, out_shape=[pltpu.HBM(t.shape, t.dtype) for t in thru],
        in_specs=[_HBM] * n + [_SEM, _SEM] + [_ANY] * len(after), out_specs=[_HBM] * n,
        input_output_aliases={a: a for a in range(n)},
        compiler_params=pltpu.CompilerParams(has_side_effects=_DATAFLOW),
    )(*thru, send_sems, recv_sems, *after)


def _place_own(shards, gathered, cq, name, carry=()):
    n = len(shards)
    nc = len(carry)
    steps = 4

    def body(cq_ref, *refs):
        for a in range(n):
            refs[2 * n + nc + a][...] = refs[a][...]

    def tile(shape):
        return shape[0] // steps if _halvable(shape) else shape[0]

    in_specs = [pl.BlockSpec((tile(s.shape), s.shape[1]), (lambda i, s_: (i, 0)) if _halvable(s.shape) else (lambda i, s_: (0, 0)))
                for s in shards]
    in_specs += [pl.BlockSpec(memory_space=pl.ANY)] * (n + nc)
    out_specs = [pl.BlockSpec((None, tile(s.shape), s.shape[1]),
                              (lambda i, s_: (s_[1], i, 0)) if _halvable(s.shape) else (lambda i, s_: (s_[1], 0, 0)))
                 for s in shards]
    out_specs += [pl.BlockSpec(memory_space=pl.ANY)] * nc
    gs = pltpu.PrefetchScalarGridSpec(num_scalar_prefetch=1, grid=(steps,), in_specs=in_specs, out_specs=out_specs)
    outs = pl.pallas_call(
        body, name=name, grid_spec=gs,
        out_shape=[jax.ShapeDtypeStruct(g.shape, g.dtype) for g in gathered] + [jax.ShapeDtypeStruct(t.shape, t.dtype) for t in carry],
        input_output_aliases={1 + n + a: a for a in range(n + nc)},
        compiler_params=_cparams(("arbitrary",)),
    )(cq, *shards, *gathered, *carry)
    return (outs[:n], outs[n:]) if nc else outs


def _half_rows(ref, c, rh):
    return ref.at[:, pl.ds(pl.multiple_of(c * rh, 8), rh), :]


def _chips_copy(src_refs, land_refs, send_sems, recv_sems, a, j, x, y, c):
    px, py = _other_chips(x, y)[j]
    return pltpu.make_async_remote_copy(src_ref=src_refs[a].at[2 * px + py], dst_ref=land_refs[a].at[j],
                                        send_sem=send_sems.at[3 * a + j], recv_sem=recv_sems.at[3 * a + j],
                                        device_id=(px, py, c), device_id_type=MESH)


def _peer(r, x, y, c):
    return (x if r & 4 == 0 else 1 - x), (y if r & 2 == 0 else 1 - y), (c if r & 1 == 0 else 1 - c)


def _small_copy(small_ref, all_ref, send_sems, recv_sems, base, r, slot, x, y, c):
    return pltpu.make_async_remote_copy(src_ref=small_ref, dst_ref=all_ref.at[slot], send_sem=send_sems.at[base + r - 1],
                                        recv_sem=recv_sems.at[base + r - 1], device_id=_peer(r, x, y, c), device_id_type=MESH)


def _grad_chips_start(parts, name, small=None):
    n = len(parts)
    srcs = list(parts) + ([] if small is None else [small])
    m = len(srcs)

    def body(*refs):
        ins, lands = refs[:m], refs[m:2 * m]
        send_sems, recv_sems = refs[2 * m], refs[2 * m + 1]
        token = refs[-1]
        x, y, c = _pos()
        for a in range(n):
            for j in range(3):
                _chips_copy(ins, lands, send_sems, recv_sems, a, j, x, y, c).start()
        if small is not None:
            for r in range(1, 8):
                _small_copy(ins[n], lands[n], send_sems, recv_sems, 3 * n, r, 4 * x + 2 * y + c, x, y, c).start()
        token[...] = jnp.zeros_like(token)

    land_shapes = [(3,) + p.shape[1:] for p in parts] + ([] if small is None else [(8,) + small.shape])
    nsem = 3 * n + (0 if small is None else 7)
    return pl.pallas_call(
        body, name=name,
        out_shape=(pltpu.SemaphoreType.DMA((nsem,)), pltpu.SemaphoreType.DMA((nsem,)),
                   *[pltpu.HBM(p.shape, p.dtype) for p in srcs],
                   *[pltpu.HBM(ls, p.dtype) for ls, p in zip(land_shapes, srcs)],
                   jax.ShapeDtypeStruct((8, 128), F32)),
        in_specs=[_HBM] * (2 * m),
        out_specs=(_SEM, _SEM, *[_HBM] * (2 * m), pl.BlockSpec(memory_space=pltpu.VMEM)),
        input_output_aliases={a: 2 + a for a in range(2 * m)},
        compiler_params=pltpu.CompilerParams(has_side_effects=_DATAFLOW),
    )(*[_in_hbm(p) for p in srcs], *[_in_hbm(lax.empty(ls, p.dtype)) for ls, p in zip(land_shapes, srcs)])


def _grad_chips_wait(started, after, name, with_small=False):
    send_sems, recv_sems, *thru = started
    m = len(thru) // 2
    n = m - (1 if with_small else 0)

    def body(*refs):
        ins, lands = refs[:m], refs[m:2 * m]
        send_sems, recv_sems = refs[2 * m], refs[2 * m + 1]
        x, y, c = _pos()
        for a in range(n):
            for j in range(3):
                cp = _chips_copy(ins, lands, send_sems, recv_sems, a, j, x, y, c)
                cp.wait_send()
                cp.wait_recv()
        if with_small:
            for r in range(1, 8):
                px, py, pc = _peer(r, x, y, c)
                _small_copy(ins[n], lands[n], send_sems, recv_sems, 3 * n, r, 4 * x + 2 * y + c, x, y, c).wait_send()
                _small_copy(ins[n], lands[n], send_sems, recv_sems, 3 * n, r, 4 * px + 2 * py + pc, x, y, c).wait_recv()

    outs = pl.pallas_call(
        body, name=name, out_shape=[pltpu.HBM(t.shape, t.dtype) for t in thru],
        in_specs=[_HBM] * (2 * m) + [_SEM, _SEM] + [_ANY] * len(after), out_specs=[_HBM] * (2 * m),
        input_output_aliases={a: a for a in range(2 * m)},
        compiler_params=pltpu.CompilerParams(has_side_effects=_DATAFLOW),
    )(*thru, send_sems, recv_sems, *after)
    return list(outs[m:]) + list(outs[n:m])


def _sibling_copy(src_refs, land_refs, send_sems, recv_sems, rhs, a, c, x, y):
    return pltpu.make_async_remote_copy(src_ref=_half_rows(src_refs[a], 1 - c, rhs[a]), dst_ref=land_refs[a],
                                        send_sem=send_sems.at[a], recv_sem=recv_sems.at[a],
                                        device_id=(x, y, 1 - c), device_id_type=MESH)


def _grad_sibling_start(fams, name):
    n = len(fams)
    rhs = [f.shape[1] // 2 for f in fams]

    def body(*refs):
        ins, lands = refs[:n], refs[n:2 * n]
        send_sems, recv_sems = refs[2 * n], refs[2 * n + 1]
        token = refs[-1]
        x, y, c = _pos()
        for a in range(n):
            _sibling_copy(ins, lands, send_sems, recv_sems, rhs, a, c, x, y).start()
        token[...] = jnp.zeros_like(token)

    land_shapes = [(f.shape[0], f.shape[1] // 2, f.shape[2]) for f in fams]
    return pl.pallas_call(
        body, name=name,
        out_shape=(pltpu.SemaphoreType.DMA((n,)), pltpu.SemaphoreType.DMA((n,)),
                   *[pltpu.HBM(f.shape, f.dtype) for f in fams],
                   *[pltpu.HBM(ls, f.dtype) for ls, f in zip(land_shapes, fams)],
                   jax.ShapeDtypeStruct((8, 128), F32)),
        in_specs=[_HBM] * (2 * n),
        out_specs=(_SEM, _SEM, *[_HBM] * (2 * n), pl.BlockSpec(memory_space=pltpu.VMEM)),
        input_output_aliases={a: 2 + a for a in range(2 * n)},
        compiler_params=pltpu.CompilerParams(has_side_effects=_DATAFLOW),
    )(*[_in_hbm(f) for f in fams], *[_in_hbm(lax.empty(ls, f.dtype)) for ls, f in zip(land_shapes, fams)])


def _grad_sibling_wait(started, after, name):
    send_sems, recv_sems, *thru = started
    n = len(thru) // 2
    rhs = [t.shape[1] // 2 for t in thru[:n]]

    def body(*refs):
        ins, lands = refs[:n], refs[n:2 * n]
        send_sems, recv_sems = refs[2 * n], refs[2 * n + 1]
        x, y, c = _pos()
        for a in range(n):
            cp = _sibling_copy(ins, lands, send_sems, recv_sems, rhs, a, c, x, y)
            cp.wait_send()
            cp.wait_recv()

    outs = pl.pallas_call(
        body, name=name, out_shape=[pltpu.HBM(t.shape, t.dtype) for t in thru],
        in_specs=[_HBM] * (2 * n) + [_SEM, _SEM] + [_ANY] * len(after), out_specs=[_HBM] * (2 * n),
        input_output_aliases={a: a for a in range(2 * n)},
        compiler_params=pltpu.CompilerParams(has_side_effects=_DATAFLOW),
    )(*thru, send_sems, recv_sems, *after)
    return outs[:n], outs[n:]


def _grad_share(fulls, name):
    n = len(fulls)
    rhs = [f.shape[0] // 2 for f in fulls]

    def body(*refs):
        ins, outs = refs[:n], refs[n:2 * n]
        send_sems, recv_sems = refs[2 * n], refs[2 * n + 1]
        x, y, c = _pos()

        def copy(a, half):
            rows = pl.ds(pl.multiple_of(half * rhs[a], 8), rhs[a])
            return pltpu.make_async_remote_copy(src_ref=ins[a].at[rows, :], dst_ref=outs[a].at[rows, :],
                                                send_sem=send_sems.at[a], recv_sem=recv_sems.at[a],
                                                device_id=(x, y, 1 - c), device_id_type=MESH)

        sends = [copy(a, c) for a in range(n)]
        for cp in sends:
            cp.start()
        for a in range(n):
            copy(a, 1 - c).wait_recv()
        for cp in sends:
            cp.wait_send()

    return pl.pallas_call(
        body, name=name, in_specs=[_HBM] * n, out_specs=[_HBM] * n,
        out_shape=[jax.ShapeDtypeStruct(f.shape, f.dtype) for f in fulls],
        input_output_aliases={a: a for a in range(n)},
        scratch_shapes=[pltpu.SemaphoreType.DMA((n,)), pltpu.SemaphoreType.DMA((n,))],
    )(*fulls)


def _add_sibling(own, recv, cq, name):
    nb, R, Cc = own.shape
    Rh = R // 2

    def body(cq_ref, a_ref, b_ref, o32_ref, o16_ref):
        s = a_ref[0] + b_ref[0]
        mine = pl.program_id(0) == cq_ref[1]

        @pl.when(mine)
        def _():
            o32_ref[...] = s

        @pl.when(jnp.logical_not(mine))
        def _():
            o16_ref[0] = s.astype(o16_ref.dtype)

    sp = pl.BlockSpec((1, Rh, Cc), lambda b, s: (b, 0, 0))
    gs = pltpu.PrefetchScalarGridSpec(
        num_scalar_prefetch=1, grid=(nb,),
        in_specs=[pl.BlockSpec((1, Rh, Cc), lambda b, s: (b, s[0], 0)), sp],
        out_specs=[pl.BlockSpec((Rh, Cc), lambda b, s: (0, 0)), sp])
    return pl.pallas_call(
        body, name=name, grid_spec=gs,
        out_shape=[jax.ShapeDtypeStruct((Rh, Cc), F32), jax.ShapeDtypeStruct((nb, Rh, Cc), _MXU)],
        compiler_params=_cparams(("arbitrary",)),
    )(cq, own, recv)


def _add_sibling_split(d_wp, recv, cq, name):
    _, Dm, Pc = d_wp.shape
    Rh = Dm // 2
    Wb = IN_COLS // N_CHIPS
    T = 256

    def body(cq_ref, a_ref, b_ref, o32_ref, o16_ref):
        s = a_ref[0] + b_ref[0]
        blocks = [s[:, 0:Wb], s[:, Wb:2 * Wb],
                  jnp.concatenate([s[:, 2 * Wb:P_QKVB], s[:, P_BA:P_BA + 8], s[:, P_QKVB:3 * Wb - 8]], axis=1),
                  s[:, 3 * Wb - 8:P_BA]]
        q = cq_ref[1]
        own = None
        for j, blk in enumerate(blocks):
            term = jnp.where(q == j, blk, 0.0)
            own = term if own is None else own + term
            o16_ref[j] = blk.astype(o16_ref.dtype)
        o32_ref[...] = own

    gs = pltpu.PrefetchScalarGridSpec(
        num_scalar_prefetch=1, grid=(Rh // T,),
        in_specs=[pl.BlockSpec((1, T, Pc), lambda i, s: (0, s[0] * (Rh // T) + i, 0)), pl.BlockSpec((1, T, Pc), lambda i, s: (0, i, 0))],
        out_specs=[pl.BlockSpec((T, Wb), lambda i, s: (i, 0)), pl.BlockSpec((N_CHIPS, T, Wb), lambda i, s: (0, i, 0))])
    return pl.pallas_call(
        body, name=name, grid_spec=gs,
        out_shape=[jax.ShapeDtypeStruct((Rh, Wb), F32), jax.ShapeDtypeStruct((N_CHIPS, Rh, Wb), _MXU)],
        compiler_params=_cparams(("parallel",)),
    )(cq, d_wp, recv)


def _add_chips(part32, recv3, cq, name):
    Rh, Cc = part32.shape

    def body(cq_ref, a_ref, b_ref, o_ref):
        acc = a_ref[...]
        for j in range(3):
            acc = acc + b_ref[j].astype(F32)
        o_ref[...] = acc

    gs = pltpu.PrefetchScalarGridSpec(
        num_scalar_prefetch=1, grid=(1,),
        in_specs=[pl.BlockSpec((Rh, Cc), lambda i, s: (0, 0)), pl.BlockSpec((3, Rh, Cc), lambda i, s: (0, 0, 0))],
        out_specs=pl.BlockSpec((Rh, Cc), lambda i, s: (s[0], 0)))
    return pl.pallas_call(
        body, name=name, grid_spec=gs, out_shape=jax.ShapeDtypeStruct((2 * Rh, Cc), F32),
        compiler_params=_cparams(("arbitrary",)),
    )(cq, part32, recv3)


def _transposed(g):
    Dm, n = g.shape
    pad = -n % 128

    def body(g_ref, o_ref):
        xp = jnp.concatenate([g_ref[...], jnp.zeros((Dm, pad), F32)], axis=1)
        o_ref[...] = xp.T[:n, :]

    return pl.pallas_call(body, name="transposed", out_shape=jax.ShapeDtypeStruct((n, Dm), F32),
                          compiler_params=_cparams(vmem=V7X_VMEM_LIMIT))(g)


def _adamw(w, g, m, v, name):
    R, Cc = w.shape
    T = max([t for t in range(8, 257, 8) if R % t == 0], default=R)

    def body(w_ref, g_ref, m_ref, v_ref, d_ref, mo_ref, vo_ref):
        d_ref[...], mo_ref[...], vo_ref[...] = _adamw_math(w_ref[...], g_ref[...], m_ref[...], v_ref[...])

    sp = pl.BlockSpec((T, Cc), lambda i: (i, 0))
    sh = jax.ShapeDtypeStruct((R, Cc), F32)
    return pl.pallas_call(
        body, name=name, grid=(R // T,), in_specs=[sp] * 4, out_specs=(sp, sp, sp), out_shape=(sh, sh, sh),
        compiler_params=_cparams(("parallel",)),
    )(w, g, m, v)


SMALL_ROWS = 32
ROW_CONV, ROW_FCG, ROW_FCU = 5, 13, 22


def _adamw_math(w, g, m, v):
    mn = ADAM_B1 * m + (1.0 - ADAM_B1) * g
    vn = ADAM_B2 * v + (1.0 - ADAM_B2) * (g * g)
    c1 = 1.0 / (1.0 - ADAM_B1 ** ADAM_STEP)
    c2 = 1.0 / (1.0 - ADAM_B2 ** ADAM_STEP)
    return -ADAM_LR * ((mn * c1) / (jnp.sqrt(vn * c2) + ADAM_EPS) + ADAM_WD * w), mn, vn


def _pack_small(n1, n2, fn, gp, gn, conv, fcg, fcu, loss):
    W = D_MODEL

    def body(n1_ref, n2_ref, fn_ref, gp_ref, gn_ref, conv_ref, fcg_ref, fcu_ref, loss_ref, o_ref):
        o_ref[...] = jnp.zeros_like(o_ref)
        o_ref[0:1, :] = n1_ref[...]
        o_ref[1:2, :] = n2_ref[...]
        o_ref[2:3, :] = fn_ref[...]
        o_ref[3:4, 0:8] = gp_ref[0:1, 0:8]
        o_ref[3:4, 8:9] = loss_ref[0:1, 0:1]
        o_ref[4:5, 0:128] = gn_ref[...]
        for i in range(GDN_CONV):
            o_ref[ROW_CONV + 2 * i:ROW_CONV + 2 * i + 1, :] = conv_ref[i:i + 1, 0:W]
            o_ref[ROW_CONV + 2 * i + 1:ROW_CONV + 2 * i + 2, 0:3 * GDN_WIDTH - W] = conv_ref[i:i + 1, W:3 * GDN_WIDTH]
        for r0, ref in ((ROW_FCG, fcg_ref), (ROW_FCU, fcu_ref)):
            for i in range(FFN_CONV):
                for k in range(3):
                    n = min(W, D_FF - k * W)
                    o_ref[r0 + 3 * i + k:r0 + 3 * i + k + 1, 0:n] = ref[i:i + 1, k * W:k * W + n]

    return pl.pallas_call(body, name="pack_small", out_shape=jax.ShapeDtypeStruct((SMALL_ROWS, W), F32))(
        n1, n2, fn, gp, gn, conv, fcg, fcu, loss)


def _small_step(meq, small_all, small, ws, ms, vs):
    W = D_MODEL
    n = len(ws)
    cw, fw = ws[6].shape[1], ws[7].shape[1]

    def body(meq_ref, all_ref, own_ref, *refs):
        w_refs, m_refs, v_refs = refs[:n], refs[n:2 * n], refs[2 * n:3 * n]
        loss_ref = refs[3 * n]
        outs = refs[3 * n + 1:]
        me, q = meq_ref[0], meq_ref[1]
        red = None
        for d in range(8):
            term = jnp.where(me == d, own_ref[...], all_ref[d])
            red = term if red is None else red + term
        loss_ref[...] = jnp.broadcast_to(red[3:4, 8:9], loss_ref.shape)
        conv = [jnp.concatenate([red[ROW_CONV + 2 * i:ROW_CONV + 2 * i + 1, :],
                                 red[ROW_CONV + 2 * i + 1:ROW_CONV + 2 * i + 2, 0:3 * GDN_WIDTH - W]], axis=1)
                for i in range(GDN_CONV)]
        conv = jnp.concatenate(conv, axis=0)

        def fc_rows(r0):
            rows = [jnp.concatenate([red[r0 + 3 * i + k:r0 + 3 * i + k + 1, 0:min(W, D_FF - k * W)] for k in range(3)], axis=1)
                    for i in range(FFN_CONV)]
            return jnp.concatenate(rows, axis=0)

        fc = jnp.concatenate([fc_rows(ROW_FCG), fc_rows(ROW_FCU)], axis=1)

        def chip_block(full, width):
            out = None
            for j in range(N_CHIPS):
                term = jnp.where(q == j, full[:, width * j:width * (j + 1)], 0.0)
                out = term if out is None else out + term
            return out

        grads = [red[0:1, :], red[1:2, :], red[2:3, :], red[3:4, 0:4], red[3:4, 4:8], red[4:5, 0:128],
                 chip_block(conv, cw), chip_block(fc, fw)]
        for k in range(n):
            d_, m_, v_ = _adamw_math(w_refs[k][...], grads[k], m_refs[k][...], v_refs[k][...])
            outs[4 * k][...] = grads[k]
            outs[4 * k + 1][...] = d_
            outs[4 * k + 2][...] = m_
            outs[4 * k + 3][...] = v_

    full = lambda a: pl.BlockSpec(a.shape, lambda i, s_, nd=len(a.shape): (0,) * nd)
    arrays = [small_all, small, *ws, *ms, *vs]
    out_shapes = [jax.ShapeDtypeStruct((8, 128), F32)] + [jax.ShapeDtypeStruct(w.shape, F32) for w in ws for _ in range(4)]
    gs = pltpu.PrefetchScalarGridSpec(
        num_scalar_prefetch=1, grid=(1,), in_specs=[full(a) for a in arrays],
        out_specs=[pl.BlockSpec(o.shape, lambda i, s_, nd=len(o.shape): (0,) * nd) for o in out_shapes])
    return pl.pallas_call(body, name="small_step", grid_spec=gs, out_shape=out_shapes)(meq, *arrays)


def _pad_lanes(v, n=D_MODEL):
    return jnp.pad(v, ((0, 0), (0, n - v.shape[1])))


def kernel(x, norm1_w, w_in, conv_qkv_w, a_log, dt_bias, gdn_norm_w, w_out, norm2_w, w_up, ffn_conv_w, w_down, final_norm_w, loss_target, m_norm1_w, m_w_in, m_conv_qkv_w, m_a_log, m_dt_bias, m_gdn_norm_w, m_w_out, m_norm2_w, m_w_up, m_ffn_conv_w, m_w_down, m_final_norm_w, v_norm1_w, v_w_in, v_conv_qkv_w, v_a_log, v_dt_bias, v_gdn_norm_w, v_w_out, v_norm2_w, v_w_up, v_ffn_conv_w, v_w_down, v_final_norm_w):
    c = lax.axis_index("c")
    q = 2 * lax.axis_index("x") + lax.axis_index("y")
    S = x.shape[1]
    cq = jnp.stack([c, q]).astype(jnp.int32)

    *in_started, in_token = _gather_halves_start([w_in[0].astype(_MXU), conv_qkv_w[0], ffn_conv_w[0]], x, "gather_in_start")
    w_in_l, m_w_in_l, v_w_in_l = (jnp.swapaxes(a + in_token[0:1, 0:1], 1, 2)[0] for a in (w_in, m_w_in, v_w_in))
    h1 = _rmsnorm_fwd(x[0], norm1_w, "norm1", after=[in_token])
    rest = [(a[0] + in_token[0:1, 0:1]).astype(_MXU) for a in (w_out, w_up, w_down)]
    in_shards, got_in = _gather_halves_wait(in_started, [w_in_l, m_w_in_l, v_w_in_l, h1, *rest], "gather_in_wait")
    (g_in, g_conv, g_fconv), (w_in_l, m_w_in_l, v_w_in_l) = _place_own(
        in_shards, _sibling_fill(got_in, "fill_in"), cq, "place_in", carry=[w_in_l, m_w_in_l, v_w_in_l])
    *rest_started, token = _gather_halves_start(rest, g_conv, "gather_rest_start")

    rest_state = {}

    def rest_arrived(after):
        rest_state["shards"], got = _gather_halves_wait(rest_started, after, "gather_rest_wait")
        *rest_state["fill"], tok = _sibling_fill_start(got, "fill_rest_start")
        return tok

    def rest_filled(after):
        got = _sibling_fill_wait(rest_state["fill"], after, "fill_rest_wait")
        g_out, g_up, g_down = _place_own(rest_state["shards"], got, cq, "place_rest")
        return g_out.reshape(D_MODEL, D_MODEL), g_up, g_down.reshape(D_FF, D_MODEL)

    rest_weights = (rest_arrived, rest_filled)
    wp = _wp_assemble(g_in, [token])
    conv_f = jnp.concatenate([g_conv[i] for i in range(N_CHIPS)], axis=1)
    fcw = jnp.concatenate([g_fconv[i] for i in range(N_CHIPS)], axis=1)
    gp = _pad_lanes(jnp.concatenate([a_log, dt_bias], axis=1), 128)
    fnw = final_norm_w[None, :]
    early = {}

    early_names = ("w_up", "w_down", "w_out")

    def early_sibling(d_wup, d_wdown, d_wout):
        *early["sibling"], tok = _grad_sibling_start(
            [d_wup, d_wdown.reshape(N_CHIPS, D_FF // N_CHIPS, D_MODEL), d_wout.reshape(N_CHIPS, D_MODEL // N_CHIPS, D_MODEL)],
            "grad_sibling_early_start")
        return tok

    def early_chips(after):
        fams_e, got_e = _grad_sibling_wait(early["sibling"], after, "grad_sibling_early_wait")
        early["parts"] = [_add_sibling(f, r, cq, "add_sibling_" + nm) for f, r, nm in zip(fams_e, got_e, early_names)]
        *early["started"], tok = _grad_chips_start([p[1] for p in early["parts"]], "grad_chips_start")
        return tok

    def late_sibling(d_wp):
        *early["late_sibling"], tok = _grad_sibling_start([d_wp[None]], "grad_sibling_late_start")
        return tok

    loss_l, dx, g = _local_step(x[0], loss_target[0], h1, norm1_w, norm2_w, fnw, gp, gdn_norm_w, wp,
                                conv_f, fcw, rest_weights, (early_sibling, early_chips, late_sibling))
    fams, got = _grad_sibling_wait(early["late_sibling"], [dx], "grad_sibling_late_wait")
    late_part = _add_sibling_split(fams[0], got[0], cq, "add_sibling_w_in")
    *late_started, late_token = _grad_chips_start([late_part[1]], "grad_chips_late_start")
    small = _pack_small(g["n1w"], g["n2w"], g["fnw"], g["gp"], g["gnw"], g["conv_w"], g["fcw_g"], g["fcw_u"], loss_l)
    *small_started, small_token = _grad_chips_start([], "small_gather_start", small)
    got3_e = _grad_chips_wait(early["started"], [late_token, small_token], "grad_chips_wait")
    g_w_up, g_w_down, g_w_out = _grad_share(
        [_add_chips(p[0], r3, cq, "add_chips_" + nm) for p, r3, nm in zip(early["parts"], got3_e, early_names)],
        "grad_share_early")
    big = {}

    def adamw_big(nm, w, gg, m, v):
        d_, m_, v_ = _adamw(w[0], gg, m[0], v[0], "adamw_" + nm)
        big[nm] = (gg[None], d_[None], m_[None], v_[None])

    adamw_big("w_up", w_up, g_w_up, m_w_up, v_w_up)
    adamw_big("w_down", w_down, g_w_down, m_w_down, v_w_down)
    adamw_big("w_out", w_out, g_w_out, m_w_out, v_w_out)
    got3, = _grad_chips_wait(late_started, [big[nm][1] for nm in early_names], "grad_chips_late_wait")
    g_w_in, = _grad_share([_add_chips(late_part[0], got3, cq, "add_chips_w_in")], "grad_share_late")
    g_t = _transposed(g_w_in)
    d_t, m_t, v_t = _adamw(w_in_l, g_t, m_w_in_l, v_w_in_l, "adamw_w_in")
    big["w_in"] = tuple(jnp.swapaxes(t[None], 1, 2) for t in (g_t, d_t, m_t, v_t))
    small_all, small = _grad_chips_wait(small_started, [d_t], "small_gather_wait", with_small=True)
    small_names = ["norm1_w", "norm2_w", "final_norm_w", "a_log", "dt_bias", "gdn_norm_w", "conv_qkv_w", "ffn_conv_w"]
    loss_b, *small_out = _small_step(
        jnp.stack([2 * q + c, q]).astype(jnp.int32), small_all, small,
        [norm1_w, norm2_w, final_norm_w[None], a_log, dt_bias, gdn_norm_w, conv_qkv_w[0], ffn_conv_w[0]],
        [m_norm1_w, m_norm2_w, m_final_norm_w[None], m_a_log, m_dt_bias, m_gdn_norm_w, m_conv_qkv_w[0], m_ffn_conv_w[0]],
        [v_norm1_w, v_norm2_w, v_final_norm_w[None], v_a_log, v_dt_bias, v_gdn_norm_w, v_conv_qkv_w[0], v_ffn_conv_w[0]])
    like = dict(final_norm_w=lambda t: t[0], conv_qkv_w=lambda t: t[None], ffn_conv_w=lambda t: t[None])
    for k, nm in enumerate(small_names):
        big[nm] = tuple(like.get(nm, lambda t: t)(t) for t in small_out[4 * k:4 * k + 4])
    names = ["norm1_w", "w_in", "conv_qkv_w", "a_log", "dt_bias", "gdn_norm_w", "w_out", "norm2_w", "w_up",
             "ffn_conv_w", "w_down", "final_norm_w"]
    return (loss_b[0, 0], dx[None], *[big[n][0] for n in names], *[big[n][1] for n in names],
            *[big[n][2] for n in names], *[big[n][3] for n in names])
```

```python
import functools
import math

import numpy as np
import jax
import jax.numpy as jnp
from jax import lax
from jax.experimental import pallas as pl
from jax.experimental.pallas import tpu as pltpu

F32 = jnp.float32
BF16 = jnp.bfloat16
_MXU = jnp.bfloat16
_HI = lax.Precision.HIGHEST
EPS = 1e-6
V7X_VMEM_LIMIT = 56 * 1024 * 1024
MESH = pl.DeviceIdType.MESH

D_MODEL = 1024
GDN_HEADS, GDN_DIM, GDN_CHUNK, GDN_CONV = 4, 128, 64, 4
GDN_WIDTH = GDN_HEADS * GDN_DIM
DIL_HEADS, DIL_DIM = 8, 64
DIL_WIDTH = DIL_HEADS * DIL_DIM
D_FF, FFN_CONV = 2816, 3
IN_COLS = 3592
P_COLS = 3840
P_Z, P_QKVB, P_BA = 1536, 2048, 3584
ATT_T = 1024
ADAM_LR, ADAM_B1, ADAM_B2, ADAM_EPS, ADAM_WD, ADAM_STEP = 0.001, 0.9, 0.999, 1e-08, 0.01, 10
N_CHIPS = 4


def _cparams(sem=None, vmem=None):
    kw = {}
    if sem is not None:
        kw["dimension_semantics"] = sem
    if vmem is not None:
        kw["vmem_limit_bytes"] = vmem
    return pltpu.CompilerParams(**kw)


def _silu(x):
    return x * jax.nn.sigmoid(x)


def _pick_tile(n, cap):
    best = None
    for t in range(128, min(n, cap) + 1, 128):
        if n % t == 0:
            best = t
    return best or n


def _mm(a, b, mode, *, out_dtype=F32, residual=None, name, b_blocks=False, place=None, into=None, tn=None, after=(),
        normed_by=None):
    if mode == "nn":
        M, K = a.shape
        N = b.shape[0] * b.shape[2] if b_blocks else b.shape[1]
    elif mode == "nt":
        (M, K), (N, _) = a.shape, b.shape
    else:
        (K, M), (_, N) = a.shape, b.shape
    tm = _pick_tile(M, 1024)
    tn = b.shape[2] if b_blocks else (tn or _pick_tile(N, 1536))

    def vmem(tm, tn):
        return 2 * (tm * K * a.dtype.itemsize + tn * K * b.dtype.itemsize
                    + tm * tn * (jnp.dtype(out_dtype).itemsize + (4 if residual is not None else 0))) + 3 * tm * tn * 4

    fixed_tn = b_blocks or (place is not None and place[0] == "blocks")
    while vmem(tm, tn) > 40 * 1024 * 1024:
        if (tm >= tn or fixed_tn) and tm % 256 == 0:
            tm //= 2
        elif tn % 256 == 0 and not fixed_tn:
            tn //= 2
        else:
            tm //= 2
    a_spec = pl.BlockSpec((K, tm), lambda j, i: (0, i)) if mode == "tn" else pl.BlockSpec((tm, K), lambda j, i: (i, 0))
    if b_blocks:
        b_spec = pl.BlockSpec((None, K, tn), lambda j, i: (j, 0, 0))
    else:
        b_spec = pl.BlockSpec((tn, K), lambda j, i: (j, 0)) if mode == "nt" else pl.BlockSpec((K, tn), lambda j, i: (0, j))
    r_spec = pl.BlockSpec((tm, tn), lambda j, i: (i, j))
    if place is None:
        o_spec, o_shape = r_spec, (M, N)
    elif place[0] == "rows":
        off = place[2] // tm
        o_spec, o_shape = pl.BlockSpec((tm, tn), lambda j, i: (i + off, j)), (place[1], N)
    else:
        off = place[2]
        o_spec, o_shape = pl.BlockSpec((None, tm, tn), lambda j, i: (j + off, i, 0)), (place[1], M, tn)
    dims = {"nn": (((1,), (0,)), ((), ())), "nt": (((1,), (1,)), ((), ())), "tn": (((0,), (0,)), ((), ()))}[mode]

    def body(*refs):
        a_ref, b_ref = refs[0], refs[1]
        o_ref = refs[-1] if normed_by is None else refs[-2]
        acc = lax.dot_general(a_ref[...].astype(_MXU), b_ref[...].astype(_MXU), dims, preferred_element_type=F32)
        if residual is not None:
            acc = acc + refs[2][...]
        o_ref[...] = acc.astype(out_dtype)
        if normed_by is not None:
            rs = lax.rsqrt(jnp.mean(acc * acc, axis=-1, keepdims=True) + EPS)
            refs[-1][...] = (acc * rs * refs[len(ins0)][...]).astype(_MXU)

    ins, specs, alias = [a, b], [a_spec, b_spec], {}
    if residual is not None:
        ins.append(residual)
        specs.append(r_spec)
    ins0 = list(ins)
    if normed_by is not None:
        assert tn == N and place is None and into is None
        ins.append(normed_by)
        specs.append(pl.BlockSpec((1, N), lambda j, i: (0, 0)))
    if into is not None:
        alias = {len(ins): 0}
        ins.append(into)
        specs.append(pl.BlockSpec(memory_space=pl.ANY))
    ins += list(after)
    specs += [pl.BlockSpec(memory_space=pl.ANY)] * len(after)
    o_shape = jax.ShapeDtypeStruct(o_shape, out_dtype)
    if normed_by is not None:
        o_spec, o_shape = (o_spec, r_spec), (o_shape, jax.ShapeDtypeStruct((M, N), _MXU))
    return pl.pallas_call(
        body, name=name, grid=(N // tn, M // tm), in_specs=specs, out_specs=o_spec,
        out_shape=o_shape, input_output_aliases=alias,
        compiler_params=_cparams(("parallel", "parallel"), V7X_VMEM_LIMIT),
    )(*ins)


def _wp_assemble(g_in, after=()):
    nb, Dm, Wb = g_in.shape
    T = 256
    n_lo = P_QKVB - 2 * Wb

    def body(g_ref, *rest):
        g2 = g_ref[2]
        rest[-1][...] = jnp.concatenate(
            [g_ref[0], g_ref[1], g2[:, :n_lo], g2[:, n_lo + 8:], g_ref[3], g2[:, n_lo:n_lo + 8],
             jnp.zeros((T, P_COLS - P_BA - 8), g_in.dtype)], axis=1)

    return pl.pallas_call(
        body, name="wp_assemble", grid=(Dm // T,),
        in_specs=[pl.BlockSpec((nb, T, Wb), lambda i: (0, i, 0))] + [pl.BlockSpec(memory_space=pl.ANY)] * len(after),
        out_specs=pl.BlockSpec((T, P_COLS), lambda i: (i, 0)), out_shape=jax.ShapeDtypeStruct((Dm, P_COLS), g_in.dtype),
        compiler_params=_cparams(("parallel",)),
    )(g_in, *after)


def _rmsnorm_fwd(x, w, name, after=()):
    S, D = x.shape
    T = _pick_tile(S, 512)

    def body(x_ref, w_ref, *rest):
        xv = x_ref[...]
        rs = lax.rsqrt(jnp.mean(xv * xv, axis=-1, keepdims=True) + EPS)
        rest[-1][...] = (xv * rs * w_ref[...]).astype(rest[-1].dtype)

    return pl.pallas_call(
        body, name=name, grid=(S // T,),
        in_specs=[pl.BlockSpec((T, D), lambda i: (i, 0)), pl.BlockSpec((1, D), lambda i: (0, 0))] + [_ANY] * len(after),
        out_specs=pl.BlockSpec((T, D), lambda i: (i, 0)),
        out_shape=jax.ShapeDtypeStruct((S, D), _MXU),
        compiler_params=_cparams(("parallel",)),
    )(x, w, *after)


def _rmsnorm_bwd(dh, x, w, dres, name, after=()):
    S, D = x.shape
    pair = isinstance(dh, tuple)
    T = _pick_tile(S, 256 if pair else 512)
    dhs = [*dh[0], dh[1]] if pair else [dh]

    def body(*refs):
        x_ref, w_ref, dres_ref = refs[len(dhs):len(dhs) + 3]
        dx_ref, dw_ref = refs[-2:]
        xv = x_ref[...]
        rs = lax.rsqrt(jnp.mean(xv * xv, axis=-1, keepdims=True) + EPS)
        xn = xv * rs
        if pair:
            b_ref = refs[len(dhs) - 1]
            nb, _, Kb = b_ref.shape
            per = nb // (len(dhs) - 1)
            dhv = None
            for blk in range(nb):
                lo = (blk % per) * Kb
                t = lax.dot_general(refs[blk // per][:, lo:lo + Kb].astype(_MXU), b_ref[blk].astype(_MXU), (_NT, ((), ())),
                                    preferred_element_type=F32)
                dhv = t if dhv is None else dhv + t
        else:
            dhv = refs[0][...]
        dxn = dhv * w_ref[...]
        dx_ref[...] = dres_ref[...] + rs * (dxn - xn * jnp.mean(dxn * xn, axis=-1, keepdims=True))

        @pl.when(pl.program_id(0) == 0)
        def _():
            dw_ref[...] = jnp.zeros_like(dw_ref)

        dw_ref[...] += jnp.sum(dhv * xn, axis=0, keepdims=True)

    row = pl.BlockSpec((T, D), lambda i: (i, 0))
    vec = pl.BlockSpec((1, D), lambda i: (0, 0))
    dh_specs = [row] if not pair else (
        [pl.BlockSpec((T, a.shape[1]), lambda i: (i, 0)) for a in dh[0]] + [pl.BlockSpec(dh[1].shape, lambda i: (0, 0, 0))])
    return pl.pallas_call(
        body, name=name, grid=(S // T,), in_specs=dh_specs + [row, vec, row] + [_ANY] * len(after), out_specs=(row, vec),
        out_shape=(jax.ShapeDtypeStruct((S, D), F32), jax.ShapeDtypeStruct((1, D), F32)),
        compiler_params=_cparams(("arbitrary",), V7X_VMEM_LIMIT if pair else None),
    )(*dhs, x, w, dres, *after)


def _loss_head(x3, w, tgt, name):
    S, D = tgt.shape
    fused = isinstance(x3, tuple)
    T = _pick_tile(S, 256 if fused else 512)
    xs = list(x3) if fused else [x3]

    def body(*refs):
        w_ref, t_ref = refs[len(xs):len(xs) + 2]
        loss_ref, dx_ref, dxn_ref, dw_ref = refs[-4:]
        xv = refs[0][...]
        if fused:
            xv = lax.dot_general(xv.astype(_MXU), refs[1][...].astype(_MXU), (_NN, ((), ())),
                                 preferred_element_type=F32) + refs[2][...]
        rs = lax.rsqrt(jnp.mean(xv * xv, axis=-1, keepdims=True) + EPS)
        xn = xv * rs
        err = xn * w_ref[...] - t_ref[...]
        dy = err * (1.0 / D)
        dxn = dy * w_ref[...]
        dxv = rs * (dxn - xn * jnp.mean(dxn * xn, axis=-1, keepdims=True))
        dx_ref[...] = dxv
        dxn_ref[...] = dxv.astype(dxn_ref.dtype)

        @pl.when(pl.program_id(0) == 0)
        def _():
            dw_ref[...] = jnp.zeros_like(dw_ref)
            loss_ref[...] = jnp.zeros_like(loss_ref)

        dw_ref[...] += jnp.sum(dy * xn, axis=0, keepdims=True)
        part = jnp.sum(jnp.sum(err * err, axis=-1, keepdims=True), axis=0, keepdims=True) * (0.5 / D)
        loss_ref[...] += jnp.broadcast_to(part, loss_ref.shape)

    row = pl.BlockSpec((T, D), lambda i: (i, 0))
    vec = pl.BlockSpec((1, D), lambda i: (0, 0))
    x_specs = [row] if not fused else [pl.BlockSpec((T, xs[0].shape[1]), lambda i: (i, 0)),
                                       pl.BlockSpec(xs[1].shape, lambda i: (0, 0)), row]
    return pl.pallas_call(
        body, name=name, grid=(S // T,), in_specs=x_specs + [vec, row],
        out_specs=(pl.BlockSpec((8, 128), lambda i: (0, 0)), row, row, vec),
        out_shape=(jax.ShapeDtypeStruct((8, 128), F32), jax.ShapeDtypeStruct((S, D), F32), jax.ShapeDtypeStruct((S, D), _MXU),
                   jax.ShapeDtypeStruct((1, D), F32)),
        compiler_params=_cparams(("arbitrary",), V7X_VMEM_LIMIT if fused else None),
    )(*xs, w, tgt)


def _shifted(ext, back, lo, n):
    if back == 0:
        return ext[lo:lo + n, :]
    return pltpu.roll(ext, back % ext.shape[0], 0)[lo:lo + n, :]


def _conv_windows(ext, K, T):
    return [_shifted(ext, (K - 1) - i, 8, T) for i in range(K)]


def _conv_taps(ext, w, K, T):
    out = None
    for i, win in enumerate(_conv_windows(ext, K, T)):
        term = win * w[i:i + 1, :]
        out = term if out is None else out + term
    return out


def _conv_taps_t(ext, w, K, T):
    out = None
    for i in range(K):
        term = _shifted(ext, i - (K - 1), 0, T) * w[i:i + 1, :]
        out = term if out is None else out + term
    return out


def _tri_masks(C):
    r = lax.broadcasted_iota(jnp.int32, (C, C), 0)
    c = lax.broadcasted_iota(jnp.int32, (C, C), 1)
    return r == c, r >= c, r > c, r <= c


_NN, _NT, _TN = ((1,), (0,)), ((1,), (1,)), ((0,), (0,))
_GDN_PASSES = dict(qk=1, inv=1, sol=1, scan=1, bwd=1)


def _bdot_raw(a, b, kind, passes):
    dims = ({"NN": ((2,), (1,)), "NT": ((2,), (2,)), "TN": ((1,), (1,))}[kind], ((0,), (0,)))
    if passes == 0:
        return lax.dot_general(a, b, dims, precision=_HI, preferred_element_type=F32)
    ah, bh = a.astype(BF16), b.astype(BF16)
    out = lax.dot_general(ah, bh, dims, preferred_element_type=F32)
    if passes == 3:
        al, bl = (a - ah.astype(F32)).astype(BF16), (b - bh.astype(F32)).astype(BF16)
        out = out + lax.dot_general(ah, bl, dims, preferred_element_type=F32) + lax.dot_general(al, bh, dims, preferred_element_type=F32)
    return out


@functools.partial(jax.custom_vjp, nondiff_argnums=(2, 3))
def _bdot(a, b, kind, passes):
    return _bdot_raw(a, b, kind, passes)


def _bdot_fwd(a, b, kind, passes):
    return _bdot_raw(a, b, kind, passes), (a, b)


def _bdot_bwd(kind, passes, res, ct):
    a, b = res
    if kind == "NN":
        return _bdot_raw(ct, b, "NT", passes), _bdot_raw(a, ct, "TN", passes)
    if kind == "NT":
        return _bdot_raw(ct, b, "NN", passes), _bdot_raw(ct, a, "TN", passes)
    return _bdot_raw(b, ct, "NT", passes), _bdot_raw(a, ct, "NN", passes)


_bdot.defvjp(_bdot_fwd, _bdot_bwd)


def _softplus(x):
    return jnp.maximum(x, 0.0) + jnp.log(1.0 + jnp.exp(-jnp.abs(x)))


def _gdn_stage1(cq, ck, cv, b_col, a_col, alog, dtb, dot=_bdot_raw):
    C = cq.shape[1]
    eye, incl, strict, incl_t = _tri_masks(C)
    qn = cq * lax.rsqrt(jnp.sum(cq * cq, axis=-1, keepdims=True) + EPS) * (GDN_DIM ** -0.5)
    kn = ck * lax.rsqrt(jnp.sum(ck * ck, axis=-1, keepdims=True) + EPS)
    beta = jax.nn.sigmoid(b_col)
    g = -jnp.exp(alog) * _softplus(a_col + dtb)
    g_row = jnp.sum(jnp.where(eye, g, 0.0), axis=1, keepdims=True)
    beta_row = jnp.sum(jnp.where(eye, beta, 0.0), axis=1, keepdims=True)
    gc_col = jnp.sum(jnp.where(incl, g_row, 0.0), axis=2, keepdims=True)
    gc_row = jnp.sum(jnp.where(incl_t, g, 0.0), axis=1, keepdims=True)
    dec = jnp.where(incl, jnp.exp(jnp.where(incl, gc_col - gc_row, 0.0)), 0.0)
    kk = dot(kn, kn, "NT", _GDN_PASSES["qk"])
    qk = dot(qn, kn, "NT", _GDN_PASSES["qk"])
    lmat = jnp.where(strict, dec * kk * beta_row, 0.0)
    attn = dec * qk * beta_row
    gam = jnp.exp(gc_col)
    gc_last = gc_col[:, C - 1:C, :]
    k_end = kn * (jnp.exp(gc_last - gc_col) * beta)
    return lmat, cv, gam * kn, gam * qn, attn, k_end, jnp.exp(gc_last)


def _tri_inv(lmat):
    C = lmat.shape[1]
    eye = _tri_masks(C)[0]
    ps = _GDN_PASSES["inv"]
    p = jnp.where(eye, 1.0, 0.0) - lmat
    lp = _bdot_raw(lmat, lmat, "NN", ps)
    n = int(math.log2(C))
    for s in range(1, n):
        p = p + _bdot_raw(p, lp, "NN", ps)
        if s < n - 1:
            lp = _bdot_raw(lp, lp, "NN", ps)
    return p


def _gated_norm(o, z, gnw):
    on = o * lax.rsqrt(jnp.mean(o * o, axis=-1, keepdims=True) + EPS) * gnw
    return on * _silu(z)


GDN_PG = 4
GDN_SG = 4


def _gdn_pairs(c, ba, gp, G):
    C, W, H = GDN_CHUNK, GDN_WIDTH, GDN_HEADS
    pairs = [(j, h) for j in range(G) for h in range(H)]
    cq, ck, cv = (jnp.stack([c[C * j:C * (j + 1), o + GDN_DIM * h:o + GDN_DIM * (h + 1)] for j, h in pairs]) for o in (0, W, 2 * W))
    b_col = jnp.stack([ba[C * j:C * (j + 1), h:h + 1] for j, h in pairs])
    a_col = jnp.stack([ba[C * j:C * (j + 1), H + h:H + h + 1] for j, h in pairs])
    alog = jnp.stack([gp[0:1, h:h + 1] for j, h in pairs])
    dtb = jnp.stack([gp[0:1, H + h:H + h + 1] for j, h in pairs])
    return pairs, (cq, ck, cv, b_col, a_col, alog, dtb)


def _gdn_pre_specs(S, G):
    C = GDN_CHUNK
    T = C * G
    return dict(
        cur=pl.BlockSpec((T, 3 * GDN_WIDTH), lambda i: (i, 0)),
        prev=pl.BlockSpec((8, 3 * GDN_WIDTH), lambda i: (jnp.maximum(i * (T // 8) - 1, 0), 0)),
        ba=pl.BlockSpec((T, 128), lambda i: (i, P_BA // 128)),
        cw=pl.BlockSpec((GDN_CONV, 3 * GDN_WIDTH), lambda i: (0, 0)),
        vec=pl.BlockSpec((1, 128), lambda i: (0, 0)),
        hd=pl.BlockSpec((GDN_HEADS, T, GDN_DIM), lambda i: (0, i, 0)),
        hc=pl.BlockSpec((GDN_HEADS, T, C), lambda i: (0, i, 0)),
        ge=pl.BlockSpec((G, GDN_HEADS, 8, 128), lambda i: (i, 0, 0, 0)),
    )


def _hd_shape(S, last=GDN_DIM):
    return jax.ShapeDtypeStruct((GDN_HEADS, S, last), F32)


def _gdn_pre(proj, conv_w, gp):
    S = proj.shape[0]
    C, G = GDN_CHUNK, GDN_PG
    nc = S // C
    sp = _gdn_pre_specs(S, G)

    def body(cur_ref, prev_ref, ba_ref, cw_ref, gp_ref, uv_ref, wk_ref, qd_ref, ke_ref, at_ref, ti_ref, ge_ref):
        prev = prev_ref[...] * jnp.where(pl.program_id(0) == 0, 0.0, 1.0)
        c = _silu(_conv_taps(jnp.concatenate([prev, cur_ref[...]], axis=0), cw_ref[...], GDN_CONV, C * G))
        pairs, args = _gdn_pairs(c, ba_ref[...], gp_ref[...], G)
        lmat, v, rk, q_dec, attn, k_end, g_end = _gdn_stage1(*args)
        t = _tri_inv(lmat)
        u_v = _bdot_raw(t, v, "NN", _GDN_PASSES["sol"])
        w_k = _bdot_raw(t, rk, "NN", _GDN_PASSES["sol"])
        for b, (j, h) in enumerate(pairs):
            rows = slice(C * j, C * (j + 1))
            uv_ref[h, rows, :] = u_v[b]
            wk_ref[h, rows, :] = w_k[b]
            qd_ref[h, rows, :] = q_dec[b]
            ke_ref[h, rows, :] = k_end[b]
            at_ref[h, rows, :] = attn[b]
            ti_ref[h, rows, :] = t[b]
            ge_ref[j, h] = jnp.broadcast_to(g_end[b], (8, 128))

    return pl.pallas_call(
        body, name="gdn_pre", grid=(nc // G,),
        in_specs=[sp["cur"], sp["prev"], sp["ba"], sp["cw"], sp["vec"]],
        out_specs=(sp["hd"], sp["hd"], sp["hd"], sp["hd"], sp["hc"], sp["hc"], sp["ge"]),
        out_shape=(_hd_shape(S), _hd_shape(S), _hd_shape(S), _hd_shape(S), _hd_shape(S, C), _hd_shape(S, C),
                   jax.ShapeDtypeStruct((nc, GDN_HEADS, 8, 128), F32)),
        compiler_params=_cparams(("parallel",)),
    )(proj, proj, proj, conv_w, gp)


def _gdn_scan_specs(S, G, rev):
    C = GDN_CHUNK
    T = C * G
    n = S // T
    ci = (lambda i: n - 1 - i) if rev else (lambda i: i)
    return dict(
        hd=pl.BlockSpec((GDN_HEADS, T, GDN_DIM), lambda i: (0, ci(i), 0)),
        hc=pl.BlockSpec((GDN_HEADS, T, C), lambda i: (0, ci(i), 0)),
        ge=pl.BlockSpec((G, GDN_HEADS, 8, 128), lambda i: (ci(i), 0, 0, 0)),
        z=pl.BlockSpec((T, GDN_WIDTH), lambda i: (ci(i), P_Z // GDN_WIDTH)),
        oa=pl.BlockSpec((T, GDN_WIDTH), lambda i: (ci(i), 0)),
        vec=pl.BlockSpec((1, 128), lambda i: (0, 0)),
        st=pl.BlockSpec((G, GDN_HEADS, GDN_DIM, GDN_DIM), lambda i: (ci(i), 0, 0, 0)),
    )


def _gdn_scan(u_v, w_k, q_dec, k_end, attn, g_end, proj, gnw, mix, after=()):
    S = proj.shape[0]
    C, G = GDN_CHUNK, GDN_SG
    nc = S // C
    sp = _gdn_scan_specs(S, G, False)
    ps = _GDN_PASSES["scan"]

    def body(uv_ref, wk_ref, qd_ref, ke_ref, at_ref, ge_ref, z_ref, gnw_ref, *rest):
        oa_ref, st_ref, s_scr = rest[-3:]

        @pl.when(pl.program_id(0) == 0)
        def _():
            s_scr[...] = jnp.zeros_like(s_scr)

        for j in range(G):
            rows = slice(C * j, C * (j + 1))
            st = s_scr[...]
            st_ref[j] = st
            u = uv_ref[:, rows, :] - _bdot_raw(wk_ref[:, rows, :], st, "NN", ps)
            o = _bdot_raw(qd_ref[:, rows, :], st, "NN", ps) + _bdot_raw(at_ref[:, rows, :], u, "NN", ps)
            s_scr[...] = ge_ref[j][:, 0:1, 0:1] * st + _bdot_raw(ke_ref[:, rows, :], u, "TN", ps)
            for h in range(GDN_HEADS):
                cols = slice(GDN_DIM * h, GDN_DIM * (h + 1))
                oa_ref[rows, cols] = _gated_norm(o[h], z_ref[rows, cols], gnw_ref[...])

    return pl.pallas_call(
        body, name="gdn_scan", grid=(nc // G,),
        in_specs=[sp["hd"], sp["hd"], sp["hd"], sp["hd"], sp["hc"], sp["ge"], sp["z"], sp["vec"]] + [_ANY] * (1 + len(after)),
        out_specs=(sp["oa"], sp["st"]),
        out_shape=(jax.ShapeDtypeStruct(mix.shape, F32),
                   jax.ShapeDtypeStruct((nc, GDN_HEADS, GDN_DIM, GDN_DIM), F32)),
        input_output_aliases={8: 0},
        scratch_shapes=[pltpu.VMEM((GDN_HEADS, GDN_DIM, GDN_DIM), F32)],
        compiler_params=_cparams(("arbitrary",)),
    )(u_v, w_k, q_dec, k_end, attn, g_end, proj, gnw, mix, *after)


def _gdn_scan_bwd(u_v, w_k, q_dec, k_end, attn, g_end, proj, gnw, states, d_oa):
    S = proj.shape[0]
    C, G = GDN_CHUNK, GDN_SG
    nc = S // C
    sp = _gdn_scan_specs(S, G, True)
    ps, pb = _GDN_PASSES["scan"], _GDN_PASSES["bwd"]

    def body(uv_ref, wk_ref, qd_ref, ke_ref, at_ref, ge_ref, z_ref, gnw_ref, st_ref, doa_ref,
             duv_ref, dwk_ref, dqd_ref, dke_ref, dat_ref, dge_ref, dz_ref, dgnw_ref, ds_scr):
        @pl.when(pl.program_id(0) == 0)
        def _():
            ds_scr[...] = jnp.zeros_like(ds_scr)
            dgnw_ref[...] = jnp.zeros_like(dgnw_ref)

        dgnw = jnp.zeros((1, 128), F32)
        for j in reversed(range(G)):
            rows = slice(C * j, C * (j + 1))
            st = st_ref[j]
            wk, qd, ke, at = wk_ref[:, rows, :], qd_ref[:, rows, :], ke_ref[:, rows, :], at_ref[:, rows, :]
            u = uv_ref[:, rows, :] - _bdot_raw(wk, st, "NN", ps)
            o = _bdot_raw(qd, st, "NN", ps) + _bdot_raw(at, u, "NN", ps)
            dos = []
            for h in range(GDN_HEADS):
                cols = slice(GDN_DIM * h, GDN_DIM * (h + 1))
                _, vjp2 = jax.vjp(_gated_norm, o[h], z_ref[rows, cols], gnw_ref[...])
                do_h, dz_h, dgn = vjp2(doa_ref[rows, cols])
                dz_ref[rows, cols] = dz_h
                dgnw = dgnw + dgn
                dos.append(do_h)
            do = jnp.stack(dos)
            ds_new = ds_scr[...]
            du = _bdot_raw(at, do, "TN", pb) + _bdot_raw(ke, ds_new, "NN", pb)
            duv_ref[:, rows, :] = du
            dat_ref[:, rows, :] = _bdot_raw(do, u, "NT", pb)
            dqd_ref[:, rows, :] = _bdot_raw(do, st, "NT", pb)
            dke_ref[:, rows, :] = _bdot_raw(u, ds_new, "NT", pb)
            dwk_ref[:, rows, :] = -_bdot_raw(du, st, "NT", pb)
            d_ge = jnp.sum(jnp.sum(st * ds_new, axis=2, keepdims=True), axis=1, keepdims=True)
            dge_ref[j] = jnp.broadcast_to(d_ge, (GDN_HEADS, 8, 128))
            ds_scr[...] = ge_ref[j][:, 0:1, 0:1] * ds_new + _bdot_raw(qd, do, "TN", pb) - _bdot_raw(wk, du, "TN", pb)
        dgnw_ref[...] += dgnw

    return pl.pallas_call(
        body, name="gdn_scan_bwd", grid=(nc // G,),
        in_specs=[sp["hd"], sp["hd"], sp["hd"], sp["hd"], sp["hc"], sp["ge"], sp["z"], sp["vec"], sp["st"], sp["oa"]],
        out_specs=(sp["hd"], sp["hd"], sp["hd"], sp["hd"], sp["hc"], sp["ge"], sp["oa"], sp["vec"]),
        out_shape=(_hd_shape(S), _hd_shape(S), _hd_shape(S), _hd_shape(S), _hd_shape(S, C),
                   jax.ShapeDtypeStruct((nc, GDN_HEADS, 8, 128), F32), jax.ShapeDtypeStruct((S, GDN_WIDTH), F32),
                   jax.ShapeDtypeStruct((1, 128), F32)),
        scratch_shapes=[pltpu.VMEM((GDN_HEADS, GDN_DIM, GDN_DIM), F32)],
        compiler_params=_cparams(("arbitrary",)),
    )(u_v, w_k, q_dec, k_end, attn, g_end, proj, gnw, states, d_oa)


def _gdn_post(proj, conv_w, gp, tinv, u_v, w_k, d_uv, d_wk, d_qd, d_ke, d_at, d_ge):
    S = proj.shape[0]
    C, G = GDN_CHUNK, GDN_PG
    nc = S // C
    sp = _gdn_pre_specs(S, G)
    pb = _GDN_PASSES["bwd"]

    def body(cur_ref, prev_ref, ba_ref, cw_ref, gp_ref, ti_ref, uv_ref, wk_ref, duv_ref, dwk_ref, dqd_ref, dke_ref,
             dat_ref, dge_ref, dpre_ref, dba_ref, dgp_ref):
        i = pl.program_id(0)

        @pl.when(i == 0)
        def _():
            dgp_ref[...] = jnp.zeros_like(dgp_ref)

        prev = prev_ref[...] * jnp.where(i == 0, 0.0, 1.0)
        pre = _conv_taps(jnp.concatenate([prev, cur_ref[...]], axis=0), cw_ref[...], GDN_CONV, C * G)
        sg = jax.nn.sigmoid(pre)
        dsilu = sg * (1.0 + pre * (1.0 - sg))
        pairs, args = _gdn_pairs(pre * sg, ba_ref[...], gp_ref[...], G)
        _, vjp1 = jax.vjp(functools.partial(_gdn_stage1, dot=_bdot), *args)

        def take(ref):
            return jnp.stack([ref[h, C * j:C * (j + 1), :] for j, h in pairs])

        t, u_v, w_k = take(ti_ref), take(uv_ref), take(wk_ref)
        d_v = _bdot_raw(t, take(duv_ref), "TN", pb)
        d_rk = _bdot_raw(t, take(dwk_ref), "TN", pb)
        d_l = -(_bdot_raw(d_v, u_v, "NT", pb) + _bdot_raw(d_rk, w_k, "NT", pb))
        d_ge = jnp.stack([dge_ref[j, h][0:1, 0:1] for j, h in pairs])
        dcq, dck, dcv, db, da, dalog, ddtb = vjp1((d_l, d_v, d_rk, take(dqd_ref), take(dat_ref), take(dke_ref), d_ge))
        lane = lax.broadcasted_iota(jnp.int32, (C, 128), 1)
        lane1 = lax.broadcasted_iota(jnp.int32, (1, 128), 1)
        dgp = jnp.zeros((1, 128), F32)
        for j in range(G):
            rows = slice(C * j, C * (j + 1))
            dba = jnp.zeros((C, 128), F32)
            for h in range(GDN_HEADS):
                b = GDN_HEADS * j + h
                for o_, dcx in ((0, dcq), (GDN_WIDTH, dck), (2 * GDN_WIDTH, dcv)):
                    cols = slice(o_ + GDN_DIM * h, o_ + GDN_DIM * (h + 1))
                    dpre_ref[rows, cols] = dcx[b] * dsilu[rows, cols]
                dba = dba + jnp.where(lane == h, db[b], 0.0) + jnp.where(lane == GDN_HEADS + h, da[b], 0.0)
                dgp = dgp + jnp.where(lane1 == h, dalog[b], 0.0) + jnp.where(lane1 == GDN_HEADS + h, ddtb[b], 0.0)
            dba_ref[rows, :] = dba
        dgp_ref[0:1, :] += dgp

    T = C * G
    return pl.pallas_call(
        body, name="gdn_post", grid=(nc // G,),
        in_specs=[sp["cur"], sp["prev"], sp["ba"], sp["cw"], sp["vec"], sp["hc"], sp["hd"], sp["hd"], sp["hd"], sp["hd"],
                  sp["hd"], sp["hd"], sp["hc"], sp["ge"]],
        out_specs=(sp["cur"], pl.BlockSpec((T, 128), lambda i: (i, 0)), pl.BlockSpec((8, 128), lambda i: (0, 0))),
        out_shape=(jax.ShapeDtypeStruct((S, 3 * GDN_WIDTH), F32), jax.ShapeDtypeStruct((S, 128), F32),
                   jax.ShapeDtypeStruct((8, 128), F32)),
        compiler_params=_cparams(("arbitrary",)),
    )(proj, proj, proj, conv_w, gp, tinv, u_v, w_k, d_uv, d_wk, d_qd, d_ke, d_at, d_ge)


def _conv_bwd(dpre, x, xcol0, w, K, name, tc):
    S, Cc = dpre.shape
    T = _pick_tile(S, 256)
    nt, ncol = S // T, Cc // tc
    xo = xcol0 // tc

    def body(d_ref, dn_ref, x_ref, xp_ref, w_ref, dx_ref, dw_ref):
        i = pl.program_id(1)
        dn = dn_ref[...] * jnp.where(i == nt - 1, 0.0, 1.0)
        dv = d_ref[...]
        ext_d = jnp.concatenate([dv, dn], axis=0)
        dx_ref[...] = _conv_taps_t(ext_d, w_ref[...], K, T).astype(dx_ref.dtype)
        xp = xp_ref[...] * jnp.where(i == 0, 0.0, 1.0)
        ext_x = jnp.concatenate([xp, x_ref[...]], axis=0)

        @pl.when(i == 0)
        def _():
            dw_ref[...] = jnp.zeros_like(dw_ref)

        for k in range(K):
            dw_ref[k:k + 1, :] += jnp.sum(dv * _shifted(ext_x, (K - 1) - k, 8, T), axis=0, keepdims=True)

    r8 = T // 8
    return pl.pallas_call(
        body, name=name, grid=(ncol, nt),
        in_specs=[pl.BlockSpec((T, tc), lambda j, i: (i, j)),
                  pl.BlockSpec((8, tc), lambda j, i: (jnp.minimum((i + 1) * r8, S // 8 - 1), j)),
                  pl.BlockSpec((T, tc), lambda j, i: (i, j + xo)),
                  pl.BlockSpec((8, tc), lambda j, i: (jnp.maximum(i * r8 - 1, 0), j + xo)),
                  pl.BlockSpec((K, tc), lambda j, i: (0, j))],
        out_specs=(pl.BlockSpec((T, tc), lambda j, i: (i, j)), pl.BlockSpec((K, tc), lambda j, i: (0, j))),
        out_shape=(jax.ShapeDtypeStruct((S, Cc), _MXU), jax.ShapeDtypeStruct((K, Cc), F32)),
        compiler_params=_cparams(("parallel", "arbitrary")),
    )(dpre, dpre, x, x, w)


def _dil_bias(nt, T):
    d = (np.arange(nt)[:, None, None] * T + np.arange(T)[None, None, :] - np.arange(T)[None, :, None])
    cnt = ((d >= 0) & (d <= 128)).astype(np.float64) + ((d >= 0) & (d % 4 == 0) & (d <= 512)) + ((d >= 0) & (d % 16 == 0))
    return jnp.asarray(np.where(cnt > 0, np.log(np.maximum(cnt, 1.0)), -1e30), dtype=F32)


def _attn_fwd(proj, after=()):
    S = proj.shape[0]
    T = min(ATT_T, S)
    nt, H = S // T, T // 2
    bias = _dil_bias(nt, T)
    scale = DIL_DIM ** -0.5
    npair = DIL_WIDTH // 128
    qb0, kb0, vb0 = P_QKVB // 128, (P_QKVB + DIL_WIDTH) // 128, (P_QKVB + 2 * DIL_WIDTH) // 128

    def body(q_ref, k_ref, v_ref, b_ref, *rest):
        o_ref, lse_ref = rest[-2:]
        i = pl.program_id(1)
        qs = (q_ref[...] * scale).astype(_MXU)

        def update(carry, kt, vt, qt, bt):
            out = []
            for hh in range(2):
                m, l, acc = carry[hh]
                sl = slice(hh * DIL_DIM, (hh + 1) * DIL_DIM)
                s = lax.dot_general(kt[:, sl], qt[:, sl], (_NT, ((), ())), preferred_element_type=F32) + bt
                m_new = jnp.maximum(m, jnp.max(s, axis=0, keepdims=True))
                p = jnp.exp(s - m_new)
                a = jnp.exp(m - m_new)
                l = a * l + jnp.sum(p, axis=0, keepdims=True)
                acc = a * acc + lax.dot_general(vt[:, sl], p.astype(_MXU), (_TN, ((), ())), preferred_element_type=F32)
                out.append((m_new, l, acc))
            return tuple(out)

        def keys(j):
            rows = pl.ds(pl.multiple_of(j * T, T), T)
            return k_ref[rows, :].astype(_MXU), v_ref[rows, :].astype(_MXU)

        init = tuple((jnp.full((1, T), -1e30, F32), jnp.zeros((1, T), F32), jnp.zeros((DIL_DIM, T), F32)) for _ in range(2))
        res = lax.fori_loop(0, i, lambda j, carry: update(carry, *keys(j), qs, b_ref[i - j]), init)
        kd, vd = keys(i)
        res = update(res, kd[:H], vd[:H], qs, b_ref[0, :H, :])
        late = update(tuple(tuple(t[:, H:] for t in r) for r in res), kd[H:], vd[H:], qs[H:], b_ref[0, H:, H:])
        res = tuple(tuple(jnp.concatenate([t[:, :H], u], axis=1) for t, u in zip(r, r2)) for r, r2 in zip(res, late))
        lse_ref[...] = jnp.zeros_like(lse_ref)
        for hh in range(2):
            m, l, acc = res[hh]
            o_ref[:, hh * DIL_DIM:(hh + 1) * DIL_DIM] = (acc / l).T
            lse_ref[hh:hh + 1, :] = m + jnp.log(l)

    return pl.pallas_call(
        body, name="attn_fwd", grid=(npair, nt),
        in_specs=[pl.BlockSpec((T, 128), lambda p, i: (i, qb0 + p)),
                  pl.BlockSpec((S, 128), lambda p, i: (0, kb0 + p)),
                  pl.BlockSpec((S, 128), lambda p, i: (0, vb0 + p)),
                  pl.BlockSpec((nt, T, T), lambda p, i: (0, 0, 0))] + [_ANY] * len(after),
        out_specs=(pl.BlockSpec((T, 128), lambda p, i: (i, GDN_WIDTH // 128 + p)),
                   pl.BlockSpec((None, None, 8, T), lambda p, i: (p, i, 0, 0))),
        out_shape=(jax.ShapeDtypeStruct((S, GDN_WIDTH + DIL_WIDTH), F32), jax.ShapeDtypeStruct((npair, nt, 8, T), F32)),
        compiler_params=_cparams(("parallel", "parallel")),
    )(proj, proj, proj, bias, *after)


def _attn_bwd(proj, mix, lse, d_mix):
    S = proj.shape[0]
    T = min(ATT_T, S)
    nt, H = S // T, T // 2
    bias = _dil_bias(nt, T)
    scale = DIL_DIM ** -0.5
    npair = DIL_WIDTH // 128
    qb0, kb0, vb0 = P_QKVB // 128, (P_QKVB + DIL_WIDTH) // 128, (P_QKVB + 2 * DIL_WIDTH) // 128

    def body(q_ref, k_ref, v_ref, o_ref, lse_ref, do_ref, b_ref, dq_ref, dk_ref, dv_ref, dq_scr):
        j = pl.program_id(1)

        @pl.when(j == 0)
        def _():
            dq_scr[...] = jnp.zeros_like(dq_scr)

        kt = k_ref[...].astype(_MXU)
        vt = v_ref[...].astype(_MXU)
        ones = jnp.ones((8, DIL_DIM), F32)

        def block(carry, kt, vt, rows, lsev, bt):
            qs = (q_ref[rows, :] * scale).astype(_MXU)
            dov = do_ref[rows, :]
            prod = dov * o_ref[rows, :]
            dob = dov.astype(_MXU)
            out = []
            dqs = []
            for hh in range(2):
                dk, dv = carry[hh]
                sl = slice(hh * DIL_DIM, (hh + 1) * DIL_DIM)
                s = lax.dot_general(kt[:, sl], qs[:, sl], (_NT, ((), ())), preferred_element_type=F32) + bt
                p = jnp.exp(s - lsev[hh:hh + 1, :])
                delta = lax.dot_general(ones, prod[:, sl], (_NT, ((), ())), precision=_HI, preferred_element_type=F32)[0:1, :]
                dp = lax.dot_general(vt[:, sl], dob[:, sl], (_NT, ((), ())), preferred_element_type=F32)
                ds = (p * (dp - delta)).astype(_MXU)
                dv = dv + lax.dot_general(p.astype(_MXU), dob[:, sl], (_NN, ((), ())), preferred_element_type=F32)
                dk = dk + lax.dot_general(ds, qs[:, sl], (_NN, ((), ())), preferred_element_type=F32)
                dqs.append(lax.dot_general(ds, kt[:, sl], (_TN, ((), ())), preferred_element_type=F32) * scale)
                out.append((dk, dv))
            dq_scr[rows, :] += jnp.concatenate(dqs, axis=1)
            return tuple(out)

        def step(i, carry):
            return block(carry, kt, vt, pl.ds(pl.multiple_of(i * T, T), T), lse_ref[i], b_ref[i - j])

        zeros = tuple((jnp.zeros((H, DIL_DIM), F32), jnp.zeros((H, DIL_DIM), F32)) for _ in range(2))
        lsed = lse_ref[j]
        early = block(zeros, kt[:H], vt[:H], pl.ds(pl.multiple_of(j * T, T), T), lsed, b_ref[0, :H, :])
        late = block(zeros, kt[H:], vt[H:], pl.ds(pl.multiple_of(j * T + H, H), H), lsed[:, H:], b_ref[0, H:, H:])
        init = tuple(tuple(jnp.concatenate([t, u], axis=0) for t, u in zip(r, r2)) for r, r2 in zip(early, late))
        res = lax.fori_loop(j + 1, nt, step, init)
        dk_ref[...] = jnp.concatenate([res[0][0], res[1][0]], axis=1).astype(dk_ref.dtype)
        dv_ref[...] = jnp.concatenate([res[0][1], res[1][1]], axis=1).astype(dv_ref.dtype)

        @pl.when(j == nt - 1)
        def _():
            dq_ref[...] = dq_scr[...].astype(dq_ref.dtype)

    full = lambda c0: pl.BlockSpec((S, 128), lambda p, j: (0, c0 + p))
    tile = lambda c0: pl.BlockSpec((T, 128), lambda p, j: (j, c0 + p))
    out3 = jax.ShapeDtypeStruct((S, DIL_WIDTH), _MXU)
    return pl.pallas_call(
        body, name="attn_bwd", grid=(npair, nt),
        in_specs=[full(qb0), tile(kb0), tile(vb0), full(GDN_WIDTH // 128),
                  pl.BlockSpec((None, nt, 8, T), lambda p, j: (p, 0, 0, 0)), full(GDN_WIDTH // 128),
                  pl.BlockSpec((nt, T, T), lambda p, j: (0, 0, 0))],
        out_specs=(full(0), tile(0), tile(0)),
        out_shape=(out3, out3, out3),
        scratch_shapes=[pltpu.VMEM((S, 128), F32)],
        compiler_params=_cparams(("parallel", "arbitrary")),
    )(proj, proj, proj, mix, lse, d_mix, bias)


def _ffn_act(up, cw):
    S, Cc = up.shape[0], up.shape[1] // 2
    T, tc = _pick_tile(S, 256), _pick_tile(Cc, 1536)
    r16 = T // 16
    nct = Cc // tc

    def body(g_ref, gp_ref, u_ref, up_ref, wg_ref, wu_ref, o_ref):
        keep = jnp.where(pl.program_id(1) == 0, 0.0, 1.0)
        cg = _conv_taps(jnp.concatenate([gp_ref[8:16, :].astype(F32) * keep, g_ref[...].astype(F32)], axis=0),
                        wg_ref[...], FFN_CONV, T)
        cu = _conv_taps(jnp.concatenate([up_ref[8:16, :].astype(F32) * keep, u_ref[...].astype(F32)], axis=0),
                        wu_ref[...], FFN_CONV, T)
        o_ref[...] = (_silu(cg) * cu).astype(o_ref.dtype)

    cur = lambda o: pl.BlockSpec((T, tc), lambda j, i: (i, j + o))
    prev = lambda o: pl.BlockSpec((16, tc), lambda j, i: (jnp.maximum(i * r16 - 1, 0), j + o))
    wsp = lambda o: pl.BlockSpec((FFN_CONV, tc), lambda j, i: (0, j + o))
    return pl.pallas_call(
        body, name="ffn_act", grid=(nct, S // T),
        in_specs=[cur(0), prev(0), cur(nct), prev(nct), wsp(0), wsp(nct)], out_specs=cur(0),
        out_shape=jax.ShapeDtypeStruct((S, Cc), _MXU),
        compiler_params=_cparams(("parallel", "parallel")),
    )(up, up, up, up, cw, cw)


def _ffn_act_bwd(d_act, up, cw):
    S, Cc = up.shape[0], up.shape[1] // 2
    T, tc = _pick_tile(S, 256), _pick_tile(Cc, 1536)
    r8, r16 = T // 8, T // 16
    nt = S // T
    nct = Cc // tc
    K = FFN_CONV

    def body(da_ref, dan_ref, g_ref, gp_ref, gn_ref, u_ref, up_ref, un_ref, wg_ref, wu_ref,
             dg_ref, du_ref, dwg_ref, dwu_ref):
        i = pl.program_id(1)
        keep_p = jnp.where(i == 0, 0.0, 1.0)
        keep_n = jnp.where(i == nt - 1, 0.0, 1.0)
        wg, wu = wg_ref[...], wu_ref[...]
        xg = jnp.concatenate([gp_ref[8:16, :].astype(F32) * keep_p, g_ref[...].astype(F32),
                              gn_ref[0:8, :].astype(F32) * keep_n], axis=0)
        xu = jnp.concatenate([up_ref[8:16, :].astype(F32) * keep_p, u_ref[...].astype(F32),
                              un_ref[0:8, :].astype(F32) * keep_n], axis=0)
        cg = _conv_taps(xg, wg, K, T + 8)
        cu = _conv_taps(xu, wu, K, T + 8)
        da = jnp.concatenate([da_ref[...], dan_ref[...] * keep_n], axis=0)
        sg = jax.nn.sigmoid(cg)
        d_cg = da * cu * (sg * (1.0 + cg * (1.0 - sg)))
        d_cu = da * (cg * sg)
        dg_ref[...] = _conv_taps_t(d_cg, wg, K, T).astype(dg_ref.dtype)
        du_ref[...] = _conv_taps_t(d_cu, wu, K, T).astype(du_ref.dtype)

        @pl.when(i == 0)
        def _():
            dwg_ref[...] = jnp.zeros_like(dwg_ref)
            dwu_ref[...] = jnp.zeros_like(dwu_ref)

        for k in range(K):
            dwg_ref[k:k + 1, :] += jnp.sum(d_cg[0:T, :] * _shifted(xg, (K - 1) - k, 8, T), axis=0, keepdims=True)
            dwu_ref[k:k + 1, :] += jnp.sum(d_cu[0:T, :] * _shifted(xu, (K - 1) - k, 8, T), axis=0, keepdims=True)

    cur = lambda o: pl.BlockSpec((T, tc), lambda j, i: (i, j + o))
    prev = lambda o: pl.BlockSpec((16, tc), lambda j, i: (jnp.maximum(i * r16 - 1, 0), j + o))
    nxt = lambda o: pl.BlockSpec((16, tc), lambda j, i: (jnp.minimum((i + 1) * r16, S // 16 - 1), j + o))
    nxt8 = pl.BlockSpec((8, tc), lambda j, i: (jnp.minimum((i + 1) * r8, S // 8 - 1), j))
    wsp = lambda o: pl.BlockSpec((K, tc), lambda j, i: (0, j + o))
    return pl.pallas_call(
        body, name="ffn_act_bwd", grid=(nct, nt),
        in_specs=[cur(0), nxt8, cur(0), prev(0), nxt(0), cur(nct), prev(nct), nxt(nct), wsp(0), wsp(nct)],
        out_specs=(cur(0), cur(0), wsp(0), wsp(0)),
        out_shape=(jax.ShapeDtypeStruct((S, Cc), _MXU), jax.ShapeDtypeStruct((S, Cc), _MXU),
                   jax.ShapeDtypeStruct((K, Cc), F32), jax.ShapeDtypeStruct((K, Cc), F32)),
        compiler_params=_cparams(("parallel", "arbitrary")),
    )(d_act, d_act, up, up, up, up, up, up, cw, cw)


def _local_step(x, tgt, h1, n1w, n2w, fnw, gp, gnw, wp, conv_w, fcw, rest_weights, early_grads):
    proj = _mm(h1, wp, "nn", name="proj")
    u_v, w_k, q_dec, k_end, attn, tinv, g_end = _gdn_pre(proj, conv_w, gp)
    mix, lse = _attn_fwd(proj)
    mix, states = _gdn_scan(u_v, w_k, q_dec, k_end, attn, g_end, proj, gnw, mix, after=[rest_weights[0]([mix])])
    w_out, w_up4, w_down = rest_weights[1]([mix])
    x2, h2 = _mm(mix, w_out, "nn", residual=x, name="outproj", normed_by=n2w)
    up = _mm(h2, w_up4, "nn", b_blocks=True, out_dtype=_MXU, name="up")
    act = _ffn_act(up, fcw)
    loss, dx3, dx3n, d_fnw = _loss_head((act, w_down, x2), fnw, tgt, "loss_head")
    d_act = _mm(dx3n, w_down, "nt", name="d_act")
    d_wdown = _mm(act, dx3n, "tn", name="d_wdown")
    d_upg, d_upu, d_fcwg, d_fcwu = _ffn_act_bwd(d_act, up, fcw)
    d_wup = _mm(h2, d_upg, "tn", place=("blocks", N_CHIPS, 0), tn=w_up4.shape[2], name="d_wgate")
    d_wup = _mm(h2, d_upu, "tn", place=("blocks", N_CHIPS, N_CHIPS // 2), tn=w_up4.shape[2], into=d_wup, name="d_wup")
    dx2, d_n2w = _rmsnorm_bwd(([d_upg, d_upu], w_up4), x2, n2w, dx3, "norm2_bwd")
    d_wout = _mm(mix, dx2, "tn", name="d_wout")
    token = early_grads[0](d_wup, d_wdown, d_wout)
    d_mix = _mm(dx2, w_out, "nt", name="d_mix", after=[token])
    dq_b, dk_b, dv_b = _attn_bwd(proj, mix, lse, d_mix)
    d_uv, d_wk, d_qd, d_ke, d_at, d_ge, d_z, d_gnw = _gdn_scan_bwd(u_v, w_k, q_dec, k_end, attn, g_end, proj,
                                                                   gnw, states, d_mix)
    d_pre, d_ba, d_gp = _gdn_post(proj, conv_w, gp, tinv, u_v, w_k, d_uv, d_wk, d_qd, d_ke, d_at, d_ge)
    token = early_grads[1]([d_pre])
    d_qkva, d_convw = _conv_bwd(d_pre, proj, 0, conv_w + token[0:1, 0:1], GDN_CONV, "gdn_conv_bwd", 512)
    d_proj = jnp.concatenate([d_qkva, d_z.astype(_MXU), dq_b, dk_b, dv_b, d_ba.astype(_MXU),
                              jnp.zeros((x.shape[0], P_COLS - P_BA - 128), _MXU)], axis=1)
    d_wp = _mm(h1, d_proj, "tn", name="d_wp")
    token = early_grads[2](d_wp)
    dx, d_n1w = _rmsnorm_bwd(([d_proj], wp[None]), x, n1w, dx2, "norm1_bwd", after=[token])
    grads = dict(wp=d_wp, conv_w=d_convw, w_out=d_wout, w_up=d_wup, fcw_g=d_fcwg, fcw_u=d_fcwu, w_down=d_wdown,
                 n1w=d_n1w, n2w=d_n2w, fnw=d_fnw, gp=d_gp, gnw=d_gnw)
    return loss, dx, grads


_HBM = pl.BlockSpec(memory_space=pltpu.HBM)


def _pos():
    return lax.axis_index("x"), lax.axis_index("y"), lax.axis_index("c")


def _other_chips(x, y):
    return [(1 - x, y), (x, 1 - y), (1 - x, 1 - y)]


def _halvable(shape):
    return shape[0] % 32 == 0


def _rows_of_half(shape, half):
    if not _halvable(shape):
        return pl.ds(0, shape[0])
    return pl.ds(pl.multiple_of(half * (shape[0] // 2), 16), shape[0] // 2)


_SEM = pl.BlockSpec(memory_space=pltpu.SEMAPHORE)
_ANY = pl.BlockSpec(memory_space=pl.ANY)
_DATAFLOW = pltpu.SideEffectType.DATAFLOW_SIDE_EFFECTING


def _in_hbm(a):
    return pltpu.with_memory_space_constraint(a, pltpu.HBM)


def _halves_copy(src_refs, land_refs, send_sems, recv_sems, shapes, a, j, block, x, y, c):
    px, py = _other_chips(x, y)[j]
    rows = _rows_of_half(shapes[a], c)
    return pltpu.make_async_remote_copy(
        src_ref=src_refs[a].at[rows, :], dst_ref=land_refs[a].at[block, rows, :], send_sem=send_sems.at[3 * a + j],
        recv_sem=recv_sems.at[3 * a + j], device_id=(px, py, c), device_id_type=MESH)


def _gather_halves_start(shards, after, name):
    n = len(shards)
    shapes = [s.shape for s in shards]

    def body(*refs):
        ins, lands = refs[:n], refs[n:2 * n]
        send_sems, recv_sems = refs[2 * n + 1], refs[2 * n + 2]
        token = refs[-1]
        x, y, c = _pos()
        q = 2 * x + y
        for a in range(n):
            for j in range(3):
                _halves_copy(ins, lands, send_sems, recv_sems, shapes, a, j, q, x, y, c).start()
        token[...] = jnp.zeros_like(token)

    land_shapes = [(N_CHIPS,) + s.shape for s in shards]
    return pl.pallas_call(
        body, name=name,
        out_shape=(pltpu.SemaphoreType.DMA((3 * n,)), pltpu.SemaphoreType.DMA((3 * n,)),
                   *[pltpu.HBM(s.shape, s.dtype) for s in shards],
                   *[pltpu.HBM(ls, s.dtype) for ls, s in zip(land_shapes, shards)],
                   jax.ShapeDtypeStruct((8, 128), F32)),
        in_specs=[_HBM] * (2 * n) + [_ANY],
        out_specs=(_SEM, _SEM, *[_HBM] * (2 * n), pl.BlockSpec(memory_space=pltpu.VMEM)),
        input_output_aliases={a: 2 + a for a in range(2 * n)},
        compiler_params=pltpu.CompilerParams(has_side_effects=_DATAFLOW),
    )(*[_in_hbm(s) for s in shards], *[_in_hbm(lax.empty(ls, s.dtype)) for ls, s in zip(land_shapes, shards)], after)


def _gather_halves_wait(started, after, name):
    send_sems, recv_sems, *thru = started
    n = len(thru) // 2
    shapes = [t.shape for t in thru[:n]]

    def body(*refs):
        ins, lands = refs[:n], refs[n:2 * n]
        send_sems, recv_sems = refs[2 * n], refs[2 * n + 1]
        x, y, c = _pos()
        q = 2 * x + y
        chips = _other_chips(x, y)
        for a in range(n):
            for j, (px, py) in enumerate(chips):
                _halves_copy(ins, lands, send_sems, recv_sems, shapes, a, j, q, x, y, c).wait_send()
                _halves_copy(ins, lands, send_sems, recv_sems, shapes, a, j, 2 * px + py, x, y, c).wait_recv()

    outs = pl.pallas_call(
        body, name=name, out_shape=[pltpu.HBM(t.shape, t.dtype) for t in thru],
        in_specs=[_HBM] * (2 * n) + [_SEM, _SEM] + [_ANY] * len(after), out_specs=[_HBM] * (2 * n),
        input_output_aliases={a: a for a in range(2 * n)},
        compiler_params=pltpu.CompilerParams(has_side_effects=_DATAFLOW),
    )(*thru, send_sems, recv_sems, *after)
    return outs[:n], outs[n:]


def _sibling_fill(gathered, name):
    big = [a for a, g in enumerate(gathered) if _halvable(g.shape[1:])]
    n = len(gathered)

    def body(*refs):
        ins, outs = refs[:n], refs[n:2 * n]
        send_sems, recv_sems = refs[2 * n:]
        x, y, c = _pos()
        chips = _other_chips(x, y)

        def copy(k, j, half):
            a = big[k]
            px, py = chips[j]
            rows = _rows_of_half(gathered[a].shape[1:], half)
            return pltpu.make_async_remote_copy(
                src_ref=ins[a].at[2 * px + py, rows, :], dst_ref=outs[a].at[2 * px + py, rows, :],
                send_sem=send_sems.at[3 * k + j], recv_sem=recv_sems.at[3 * k + j],
                device_id=(x, y, 1 - c), device_id_type=MESH)

        sends = [copy(k, j, c) for k in range(len(big)) for j in range(3)]
        for cp in sends:
            cp.start()
        for k in range(len(big)):
            for j in range(3):
                copy(k, j, 1 - c).wait_recv()
        for cp in sends:
            cp.wait_send()

    return pl.pallas_call(
        body, name=name, in_specs=[_HBM] * n, out_specs=[_HBM] * n,
        out_shape=[jax.ShapeDtypeStruct(g.shape, g.dtype) for g in gathered],
        input_output_aliases={a: a for a in range(n)},
        scratch_shapes=[pltpu.SemaphoreType.DMA((3 * len(big),)), pltpu.SemaphoreType.DMA((3 * len(big),))],
    )(*gathered)


def _fill_copy(refs, send_sems, recv_sems, shapes, a, j, half, x, y, c):
    px, py = _other_chips(x, y)[j]
    rows = _rows_of_half(shapes[a], half)
    return pltpu.make_async_remote_copy(
        src_ref=refs[a].at[2 * px + py, rows, :], dst_ref=refs[a].at[2 * px + py, rows, :],
        send_sem=send_sems.at[3 * a + j], recv_sem=recv_sems.at[3 * a + j],
        device_id=(x, y, 1 - c), device_id_type=MESH)


def _sibling_fill_start(gathered, name):
    n = len(gathered)
    shapes = [g.shape[1:] for g in gathered]

    def body(*refs):
        ins = refs[:n]
        send_sems, recv_sems = refs[n], refs[n + 1]
        token = refs[-1]
        x, y, c = _pos()
        for a in range(n):
            for j in range(3):
                _fill_copy(ins, send_sems, recv_sems, shapes, a, j, c, x, y, c).start()
        token[...] = jnp.zeros_like(token)

    return pl.pallas_call(
        body, name=name,
        out_shape=(pltpu.SemaphoreType.DMA((3 * n,)), pltpu.SemaphoreType.DMA((3 * n,)),
                   *[pltpu.HBM(g.shape, g.dtype) for g in gathered], jax.ShapeDtypeStruct((8, 128), F32)),
        in_specs=[_HBM] * n,
        out_specs=(_SEM, _SEM, *[_HBM] * n, pl.BlockSpec(memory_space=pltpu.VMEM)),
        input_output_aliases={a: 2 + a for a in range(n)},
        compiler_params=pltpu.CompilerParams(has_side_effects=_DATAFLOW),
    )(*[_in_hbm(g) for g in gathered])


def _sibling_fill_wait(started, after, name):
    send_sems, recv_sems, *thru = started
    n = len(thru)
    shapes = [t.shape[1:] for t in thru]

    def body(*refs):
        ins = refs[:n]
        send_sems, recv_sems = refs[n], refs[n + 1]
        x, y, c = _pos()
        for a in range(n):
            for j in range(3):
                _fill_copy(ins, send_sems, recv_sems, shapes, a, j, c, x, y, c).wait_send()
                _fill_copy(ins, send_sems, recv_sems, shapes, a, j, 1 - c, x, y, c).wait_recv()

    return pl.pallas_call(
        body, name=name, out_shape=[pltpu.HBM(t.shape, t.dtype) for t in thru],
        in_specs=[_HBM] * n + [_SEM, _SEM] + [_ANY] * len(after), out_specs=[_HBM] * n,
        input_output_aliases={a: a for a in range(n)},
        compiler_params=pltpu.CompilerParams(has_side_effects=_DATAFLOW),
    )(*thru, send_sems, recv_sems, *after)


def _place_own(shards, gathered, cq, name, carry=()):
    n = len(shards)
    nc = len(carry)
    steps = 4

    def body(cq_ref, *refs):
        for a in range(n):
            refs[2 * n + nc + a][...] = refs[a][...]

    def tile(shape):
        return shape[0] // steps if _halvable(shape) else shape[0]

    in_specs = [pl.BlockSpec((tile(s.shape), s.shape[1]), (lambda i, s_: (i, 0)) if _halvable(s.shape) else (lambda i, s_: (0, 0)))
                for s in shards]
    in_specs += [pl.BlockSpec(memory_space=pl.ANY)] * (n + nc)
    out_specs = [pl.BlockSpec((None, tile(s.shape), s.shape[1]),
                              (lambda i, s_: (s_[1], i, 0)) if _halvable(s.shape) else (lambda i, s_: (s_[1], 0, 0)))
                 for s in shards]
    out_specs += [pl.BlockSpec(memory_space=pl.ANY)] * nc
    gs = pltpu.PrefetchScalarGridSpec(num_scalar_prefetch=1, grid=(steps,), in_specs=in_specs, out_specs=out_specs)
    outs = pl.pallas_call(
        body, name=name, grid_spec=gs,
        out_shape=[jax.ShapeDtypeStruct(g.shape, g.dtype) for g in gathered] + [jax.ShapeDtypeStruct(t.shape, t.dtype) for t in carry],
        input_output_aliases={1 + n + a: a for a in range(n + nc)},
        compiler_params=_cparams(("arbitrary",)),
    )(cq, *shards, *gathered, *carry)
    return (outs[:n], outs[n:]) if nc else outs


def _half_rows(ref, c, rh):
    return ref.at[:, pl.ds(pl.multiple_of(c * rh, 8), rh), :]


def _chips_copy(src_refs, land_refs, send_sems, recv_sems, a, j, x, y, c):
    px, py = _other_chips(x, y)[j]
    return pltpu.make_async_remote_copy(src_ref=src_refs[a].at[2 * px + py], dst_ref=land_refs[a].at[j],
                                        send_sem=send_sems.at[3 * a + j], recv_sem=recv_sems.at[3 * a + j],
                                        device_id=(px, py, c), device_id_type=MESH)


def _peer(r, x, y, c):
    return (x if r & 4 == 0 else 1 - x), (y if r & 2 == 0 else 1 - y), (c if r & 1 == 0 else 1 - c)


def _small_copy(small_ref, all_ref, send_sems, recv_sems, base, r, slot, x, y, c):
    return pltpu.make_async_remote_copy(src_ref=small_ref, dst_ref=all_ref.at[slot], send_sem=send_sems.at[base + r - 1],
                                        recv_sem=recv_sems.at[base + r - 1], device_id=_peer(r, x, y, c), device_id_type=MESH)


def _grad_chips_start(parts, name, small=None):
    n = len(parts)
    srcs = list(parts) + ([] if small is None else [small])
    m = len(srcs)

    def body(*refs):
        ins, lands = refs[:m], refs[m:2 * m]
        send_sems, recv_sems = refs[2 * m], refs[2 * m + 1]
        token = refs[-1]
        x, y, c = _pos()
        for a in range(n):
            for j in range(3):
                _chips_copy(ins, lands, send_sems, recv_sems, a, j, x, y, c).start()
        if small is not None:
            for r in range(1, 8):
                _small_copy(ins[n], lands[n], send_sems, recv_sems, 3 * n, r, 4 * x + 2 * y + c, x, y, c).start()
        token[...] = jnp.zeros_like(token)

    land_shapes = [(3,) + p.shape[1:] for p in parts] + ([] if small is None else [(8,) + small.shape])
    nsem = 3 * n + (0 if small is None else 7)
    return pl.pallas_call(
        body, name=name,
        out_shape=(pltpu.SemaphoreType.DMA((nsem,)), pltpu.SemaphoreType.DMA((nsem,)),
                   *[pltpu.HBM(p.shape, p.dtype) for p in srcs],
                   *[pltpu.HBM(ls, p.dtype) for ls, p in zip(land_shapes, srcs)],
                   jax.ShapeDtypeStruct((8, 128), F32)),
        in_specs=[_HBM] * (2 * m),
        out_specs=(_SEM, _SEM, *[_HBM] * (2 * m), pl.BlockSpec(memory_space=pltpu.VMEM)),
        input_output_aliases={a: 2 + a for a in range(2 * m)},
        compiler_params=pltpu.CompilerParams(has_side_effects=_DATAFLOW),
    )(*[_in_hbm(p) for p in srcs], *[_in_hbm(lax.empty(ls, p.dtype)) for ls, p in zip(land_shapes, srcs)])


def _grad_chips_wait(started, after, name, with_small=False):
    send_sems, recv_sems, *thru = started
    m = len(thru) // 2
    n = m - (1 if with_small else 0)

    def body(*refs):
        ins, lands = refs[:m], refs[m:2 * m]
        send_sems, recv_sems = refs[2 * m], refs[2 * m + 1]
        x, y, c = _pos()
        for a in range(n):
            for j in range(3):
                cp = _chips_copy(ins, lands, send_sems, recv_sems, a, j, x, y, c)
                cp.wait_send()
                cp.wait_recv()
        if with_small:
            for r in range(1, 8):
                px, py, pc = _peer(r, x, y, c)
                _small_copy(ins[n], lands[n], send_sems, recv_sems, 3 * n, r, 4 * x + 2 * y + c, x, y, c).wait_send()
                _small_copy(ins[n], lands[n], send_sems, recv_sems, 3 * n, r, 4 * px + 2 * py + pc, x, y, c).wait_recv()

    outs = pl.pallas_call(
        body, name=name, out_shape=[pltpu.HBM(t.shape, t.dtype) for t in thru],
        in_specs=[_HBM] * (2 * m) + [_SEM, _SEM] + [_ANY] * len(after), out_specs=[_HBM] * (2 * m),
        input_output_aliases={a: a for a in range(2 * m)},
        compiler_params=pltpu.CompilerParams(has_side_effects=_DATAFLOW),
    )(*thru, send_sems, recv_sems, *after)
    return list(outs[m:]) + list(outs[n:m])


def _sibling_copy(src_refs, land_refs, send_sems, recv_sems, rhs, a, c, x, y):
    return pltpu.make_async_remote_copy(src_ref=_half_rows(src_refs[a], 1 - c, rhs[a]), dst_ref=land_refs[a],
                                        send_sem=send_sems.at[a], recv_sem=recv_sems.at[a],
                                        device_id=(x, y, 1 - c), device_id_type=MESH)


def _grad_sibling_start(fams, name):
    n = len(fams)
    rhs = [f.shape[1] // 2 for f in fams]

    def body(*refs):
        ins, lands = refs[:n], refs[n:2 * n]
        send_sems, recv_sems = refs[2 * n], refs[2 * n + 1]
        token = refs[-1]
        x, y, c = _pos()
        for a in range(n):
            _sibling_copy(ins, lands, send_sems, recv_sems, rhs, a, c, x, y).start()
        token[...] = jnp.zeros_like(token)

    land_shapes = [(f.shape[0], f.shape[1] // 2, f.shape[2]) for f in fams]
    return pl.pallas_call(
        body, name=name,
        out_shape=(pltpu.SemaphoreType.DMA((n,)), pltpu.SemaphoreType.DMA((n,)),
                   *[pltpu.HBM(f.shape, f.dtype) for f in fams],
                   *[pltpu.HBM(ls, f.dtype) for ls, f in zip(land_shapes, fams)],
                   jax.ShapeDtypeStruct((8, 128), F32)),
        in_specs=[_HBM] * (2 * n),
        out_specs=(_SEM, _SEM, *[_HBM] * (2 * n), pl.BlockSpec(memory_space=pltpu.VMEM)),
        input_output_aliases={a: 2 + a for a in range(2 * n)},
        compiler_params=pltpu.CompilerParams(has_side_effects=_DATAFLOW),
    )(*[_in_hbm(f) for f in fams], *[_in_hbm(lax.empty(ls, f.dtype)) for ls, f in zip(land_shapes, fams)])


def _grad_sibling_wait(started, after, name):
    send_sems, recv_sems, *thru = started
    n = len(thru) // 2
    rhs = [t.shape[1] // 2 for t in thru[:n]]

    def body(*refs):
        ins, lands = refs[:n], refs[n:2 * n]
        send_sems, recv_sems = refs[2 * n], refs[2 * n + 1]
        x, y, c = _pos()
        for a in range(n):
            cp = _sibling_copy(ins, lands, send_sems, recv_sems, rhs, a, c, x, y)
            cp.wait_send()
            cp.wait_recv()

    outs = pl.pallas_call(
        body, name=name, out_shape=[pltpu.HBM(t.shape, t.dtype) for t in thru],
        in_specs=[_HBM] * (2 * n) + [_SEM, _SEM] + [_ANY] * len(after), out_specs=[_HBM] * (2 * n),
        input_output_aliases={a: a for a in range(2 * n)},
        compiler_params=pltpu.CompilerParams(has_side_effects=_DATAFLOW),
    )(*thru, send_sems, recv_sems, *after)
    return outs[:n], outs[n:]


def _grad_share(fulls, name):
    n = len(fulls)
    rhs = [f.shape[0] // 2 for f in fulls]

    def body(*refs):
        ins, outs = refs[:n], refs[n:2 * n]
        send_sems, recv_sems = refs[2 * n], refs[2 * n + 1]
        x, y, c = _pos()

        def copy(a, half):
            rows = pl.ds(pl.multiple_of(half * rhs[a], 8), rhs[a])
            return pltpu.make_async_remote_copy(src_ref=ins[a].at[rows, :], dst_ref=outs[a].at[rows, :],
                                                send_sem=send_sems.at[a], recv_sem=recv_sems.at[a],
                                                device_id=(x, y, 1 - c), device_id_type=MESH)

        sends = [copy(a, c) for a in range(n)]
        for cp in sends:
            cp.start()
        for a in range(n):
            copy(a, 1 - c).wait_recv()
        for cp in sends:
            cp.wait_send()

    return pl.pallas_call(
        body, name=name, in_specs=[_HBM] * n, out_specs=[_HBM] * n,
        out_shape=[jax.ShapeDtypeStruct(f.shape, f.dtype) for f in fulls],
        input_output_aliases={a: a for a in range(n)},
        scratch_shapes=[pltpu.SemaphoreType.DMA((n,)), pltpu.SemaphoreType.DMA((n,))],
    )(*fulls)


def _add_sibling(own, recv, cq, name):
    nb, R, Cc = own.shape
    Rh = R // 2

    def body(cq_ref, a_ref, b_ref, o32_ref, o16_ref):
        s = a_ref[0] + b_ref[0]
        mine = pl.program_id(0) == cq_ref[1]

        @pl.when(mine)
        def _():
            o32_ref[...] = s

        @pl.when(jnp.logical_not(mine))
        def _():
            o16_ref[0] = s.astype(o16_ref.dtype)

    sp = pl.BlockSpec((1, Rh, Cc), lambda b, s: (b, 0, 0))
    gs = pltpu.PrefetchScalarGridSpec(
        num_scalar_prefetch=1, grid=(nb,),
        in_specs=[pl.BlockSpec((1, Rh, Cc), lambda b, s: (b, s[0], 0)), sp],
        out_specs=[pl.BlockSpec((Rh, Cc), lambda b, s: (0, 0)), sp])
    return pl.pallas_call(
        body, name=name, grid_spec=gs,
        out_shape=[jax.ShapeDtypeStruct((Rh, Cc), F32), jax.ShapeDtypeStruct((nb, Rh, Cc), _MXU)],
        compiler_params=_cparams(("arbitrary",)),
    )(cq, own, recv)


def _add_sibling_split(d_wp, recv, cq, name):
    _, Dm, Pc = d_wp.shape
    Rh = Dm // 2
    Wb = IN_COLS // N_CHIPS
    T = 256

    def body(cq_ref, a_ref, b_ref, o32_ref, o16_ref):
        s = a_ref[0] + b_ref[0]
        blocks = [s[:, 0:Wb], s[:, Wb:2 * Wb],
                  jnp.concatenate([s[:, 2 * Wb:P_QKVB], s[:, P_BA:P_BA + 8], s[:, P_QKVB:3 * Wb - 8]], axis=1),
                  s[:, 3 * Wb - 8:P_BA]]
        q = cq_ref[1]
        own = None
        for j, blk in enumerate(blocks):
            term = jnp.where(q == j, blk, 0.0)
            own = term if own is None else own + term
            o16_ref[j] = blk.astype(o16_ref.dtype)
        o32_ref[...] = own

    gs = pltpu.PrefetchScalarGridSpec(
        num_scalar_prefetch=1, grid=(Rh // T,),
        in_specs=[pl.BlockSpec((1, T, Pc), lambda i, s: (0, s[0] * (Rh // T) + i, 0)), pl.BlockSpec((1, T, Pc), lambda i, s: (0, i, 0))],
        out_specs=[pl.BlockSpec((T, Wb), lambda i, s: (i, 0)), pl.BlockSpec((N_CHIPS, T, Wb), lambda i, s: (0, i, 0))])
    return pl.pallas_call(
        body, name=name, grid_spec=gs,
        out_shape=[jax.ShapeDtypeStruct((Rh, Wb), F32), jax.ShapeDtypeStruct((N_CHIPS, Rh, Wb), _MXU)],
        compiler_params=_cparams(("parallel",)),
    )(cq, d_wp, recv)


def _add_chips(part32, recv3, cq, name):
    Rh, Cc = part32.shape

    def body(cq_ref, a_ref, b_ref, o_ref):
        acc = a_ref[...]
        for j in range(3):
            acc = acc + b_ref[j].astype(F32)
        o_ref[...] = acc

    gs = pltpu.PrefetchScalarGridSpec(
        num_scalar_prefetch=1, grid=(1,),
        in_specs=[pl.BlockSpec((Rh, Cc), lambda i, s: (0, 0)), pl.BlockSpec((3, Rh, Cc), lambda i, s: (0, 0, 0))],
        out_specs=pl.BlockSpec((Rh, Cc), lambda i, s: (s[0], 0)))
    return pl.pallas_call(
        body, name=name, grid_spec=gs, out_shape=jax.ShapeDtypeStruct((2 * Rh, Cc), F32),
        compiler_params=_cparams(("arbitrary",)),
    )(cq, part32, recv3)


def _adamw_transposed(w, g, m, v, name):
    n, Dm = w.shape
    T = 128

    def body(w_ref, g_ref, m_ref, v_ref, gt_ref, d_ref, mo_ref, vo_ref):
        gt = g_ref[...].T
        gt_ref[...] = gt
        d_ref[...], mo_ref[...], vo_ref[...] = _adamw_math(w_ref[...], gt, m_ref[...], v_ref[...])

    row = pl.BlockSpec((T, Dm), lambda i: (i, 0))
    sh = jax.ShapeDtypeStruct((n, Dm), F32)
    return pl.pallas_call(
        body, name=name, grid=(pl.cdiv(n, T),), in_specs=[row, pl.BlockSpec((Dm, T), lambda i: (0, i)), row, row],
        out_specs=(row,) * 4, out_shape=(sh,) * 4, compiler_params=_cparams(("parallel",)),
    )(w, g, m, v)


def _adamw(w, g, m, v, name):
    R, Cc = w.shape
    T = max([t for t in range(8, 257, 8) if R % t == 0], default=R)

    def body(w_ref, g_ref, m_ref, v_ref, d_ref, mo_ref, vo_ref):
        d_ref[...], mo_ref[...], vo_ref[...] = _adamw_math(w_ref[...], g_ref[...], m_ref[...], v_ref[...])

    sp = pl.BlockSpec((T, Cc), lambda i: (i, 0))
    sh = jax.ShapeDtypeStruct((R, Cc), F32)
    return pl.pallas_call(
        body, name=name, grid=(R // T,), in_specs=[sp] * 4, out_specs=(sp, sp, sp), out_shape=(sh, sh, sh),
        compiler_params=_cparams(("parallel",)),
    )(w, g, m, v)


SMALL_ROWS = 32
ROW_CONV, ROW_FCG, ROW_FCU = 5, 13, 22


def _adamw_math(w, g, m, v):
    mn = ADAM_B1 * m + (1.0 - ADAM_B1) * g
    vn = ADAM_B2 * v + (1.0 - ADAM_B2) * (g * g)
    c1 = 1.0 / (1.0 - ADAM_B1 ** ADAM_STEP)
    c2 = 1.0 / (1.0 - ADAM_B2 ** ADAM_STEP)
    return -ADAM_LR * ((mn * c1) / (jnp.sqrt(vn * c2) + ADAM_EPS) + ADAM_WD * w), mn, vn


def _pack_small(n1, n2, fn, gp, gn, conv, fcg, fcu, loss):
    W = D_MODEL

    def body(n1_ref, n2_ref, fn_ref, gp_ref, gn_ref, conv_ref, fcg_ref, fcu_ref, loss_ref, o_ref):
        o_ref[...] = jnp.zeros_like(o_ref)
        o_ref[0:1, :] = n1_ref[...]
        o_ref[1:2, :] = n2_ref[...]
        o_ref[2:3, :] = fn_ref[...]
        o_ref[3:4, 0:8] = gp_ref[0:1, 0:8]
        o_ref[3:4, 8:9] = loss_ref[0:1, 0:1]
        o_ref[4:5, 0:128] = gn_ref[...]
        for i in range(GDN_CONV):
            o_ref[ROW_CONV + 2 * i:ROW_CONV + 2 * i + 1, :] = conv_ref[i:i + 1, 0:W]
            o_ref[ROW_CONV + 2 * i + 1:ROW_CONV + 2 * i + 2, 0:3 * GDN_WIDTH - W] = conv_ref[i:i + 1, W:3 * GDN_WIDTH]
        for r0, ref in ((ROW_FCG, fcg_ref), (ROW_FCU, fcu_ref)):
            for i in range(FFN_CONV):
                for k in range(3):
                    n = min(W, D_FF - k * W)
                    o_ref[r0 + 3 * i + k:r0 + 3 * i + k + 1, 0:n] = ref[i:i + 1, k * W:k * W + n]

    return pl.pallas_call(body, name="pack_small", out_shape=jax.ShapeDtypeStruct((SMALL_ROWS, W), F32))(
        n1, n2, fn, gp, gn, conv, fcg, fcu, loss)


def _small_step(meq, small_all, small, ws, ms, vs):
    W = D_MODEL
    n = len(ws)
    cw, fw = ws[6].shape[1], ws[7].shape[1]

    def body(meq_ref, all_ref, own_ref, *refs):
        w_refs, m_refs, v_refs = refs[:n], refs[n:2 * n], refs[2 * n:3 * n]
        loss_ref = refs[3 * n]
        outs = refs[3 * n + 1:]
        me, q = meq_ref[0], meq_ref[1]
        red = None
        for d in range(8):
            term = jnp.where(me == d, own_ref[...], all_ref[d])
            red = term if red is None else red + term
        loss_ref[...] = jnp.broadcast_to(red[3:4, 8:9], loss_ref.shape)
        conv = [jnp.concatenate([red[ROW_CONV + 2 * i:ROW_CONV + 2 * i + 1, :],
                                 red[ROW_CONV + 2 * i + 1:ROW_CONV + 2 * i + 2, 0:3 * GDN_WIDTH - W]], axis=1)
                for i in range(GDN_CONV)]
        conv = jnp.concatenate(conv, axis=0)

        def fc_rows(r0):
            rows = [jnp.concatenate([red[r0 + 3 * i + k:r0 + 3 * i + k + 1, 0:min(W, D_FF - k * W)] for k in range(3)], axis=1)
                    for i in range(FFN_CONV)]
            return jnp.concatenate(rows, axis=0)

        fc = jnp.concatenate([fc_rows(ROW_FCG), fc_rows(ROW_FCU)], axis=1)

        def chip_block(full, width):
            out = None
            for j in range(N_CHIPS):
                term = jnp.where(q == j, full[:, width * j:width * (j + 1)], 0.0)
                out = term if out is None else out + term
            return out

        grads = [red[0:1, :], red[1:2, :], red[2:3, :], red[3:4, 0:4], red[3:4, 4:8], red[4:5, 0:128],
                 chip_block(conv, cw), chip_block(fc, fw)]
        for k in range(n):
            d_, m_, v_ = _adamw_math(w_refs[k][...], grads[k], m_refs[k][...], v_refs[k][...])
            outs[4 * k][...] = grads[k]
            outs[4 * k + 1][...] = d_
            outs[4 * k + 2][...] = m_
            outs[4 * k + 3][...] = v_

    full = lambda a: pl.BlockSpec(a.shape, lambda i, s_, nd=len(a.shape): (0,) * nd)
    arrays = [small_all, small, *ws, *ms, *vs]
    out_shapes = [jax.ShapeDtypeStruct((8, 128), F32)] + [jax.ShapeDtypeStruct(w.shape, F32) for w in ws for _ in range(4)]
    gs = pltpu.PrefetchScalarGridSpec(
        num_scalar_prefetch=1, grid=(1,), in_specs=[full(a) for a in arrays],
        out_specs=[pl.BlockSpec(o.shape, lambda i, s_, nd=len(o.shape): (0,) * nd) for o in out_shapes])
    return pl.pallas_call(body, name="small_step", grid_spec=gs, out_shape=out_shapes)(meq, *arrays)


def _pad_lanes(v, n=D_MODEL):
    return jnp.pad(v, ((0, 0), (0, n - v.shape[1])))


def kernel(x, norm1_w, w_in, conv_qkv_w, a_log, dt_bias, gdn_norm_w, w_out, norm2_w, w_up, ffn_conv_w, w_down, final_norm_w, loss_target, m_norm1_w, m_w_in, m_conv_qkv_w, m_a_log, m_dt_bias, m_gdn_norm_w, m_w_out, m_norm2_w, m_w_up, m_ffn_conv_w, m_w_down, m_final_norm_w, v_norm1_w, v_w_in, v_conv_qkv_w, v_a_log, v_dt_bias, v_gdn_norm_w, v_w_out, v_norm2_w, v_w_up, v_ffn_conv_w, v_w_down, v_final_norm_w):
    c = lax.axis_index("c")
    q = 2 * lax.axis_index("x") + lax.axis_index("y")
    S = x.shape[1]
    cq = jnp.stack([c, q]).astype(jnp.int32)

    *in_started, in_token = _gather_halves_start([w_in[0].astype(_MXU), conv_qkv_w[0], ffn_conv_w[0]], x, "gather_in_start")
    w_in_l, m_w_in_l, v_w_in_l = (jnp.swapaxes(a + in_token[0:1, 0:1], 1, 2)[0] for a in (w_in, m_w_in, v_w_in))
    h1 = _rmsnorm_fwd(x[0], norm1_w, "norm1", after=[in_token])
    rest = [(a[0] + in_token[0:1, 0:1]).astype(_MXU) for a in (w_out, w_up, w_down)]
    in_shards, got_in = _gather_halves_wait(in_started, [w_in_l, m_w_in_l, v_w_in_l, h1, *rest], "gather_in_wait")
    (g_in, g_conv, g_fconv), (w_in_l, m_w_in_l, v_w_in_l) = _place_own(
        in_shards, _sibling_fill(got_in, "fill_in"), cq, "place_in", carry=[w_in_l, m_w_in_l, v_w_in_l])
    *rest_started, token = _gather_halves_start(rest, g_conv, "gather_rest_start")

    rest_state = {}

    def rest_arrived(after):
        rest_state["shards"], got = _gather_halves_wait(rest_started, after, "gather_rest_wait")
        *rest_state["fill"], tok = _sibling_fill_start(got, "fill_rest_start")
        return tok

    def rest_filled(after):
        got = _sibling_fill_wait(rest_state["fill"], after, "fill_rest_wait")
        g_out, g_up, g_down = _place_own(rest_state["shards"], got, cq, "place_rest")
        return g_out.reshape(D_MODEL, D_MODEL), g_up, g_down.reshape(D_FF, D_MODEL)

    rest_weights = (rest_arrived, rest_filled)
    wp = _wp_assemble(g_in, [token])
    conv_f = jnp.concatenate([g_conv[i] for i in range(N_CHIPS)], axis=1)
    fcw = jnp.concatenate([g_fconv[i] for i in range(N_CHIPS)], axis=1)
    gp = _pad_lanes(jnp.concatenate([a_log, dt_bias], axis=1), 128)
    fnw = final_norm_w[None, :]
    early = {}

    early_names = ("w_up", "w_down", "w_out")

    def early_sibling(d_wup, d_wdown, d_wout):
        *early["sibling"], tok = _grad_sibling_start(
            [d_wup, d_wdown.reshape(N_CHIPS, D_FF // N_CHIPS, D_MODEL), d_wout.reshape(N_CHIPS, D_MODEL // N_CHIPS, D_MODEL)],
            "grad_sibling_early_start")
        return tok

    def early_chips(after):
        fams_e, got_e = _grad_sibling_wait(early["sibling"], after, "grad_sibling_early_wait")
        early["parts"] = [_add_sibling(f, r, cq, "add_sibling_" + nm) for f, r, nm in zip(fams_e, got_e, early_names)]
        *early["started"], tok = _grad_chips_start([p[1] for p in early["parts"]], "grad_chips_start")
        return tok

    def late_sibling(d_wp):
        *early["late_sibling"], tok = _grad_sibling_start([d_wp[None]], "grad_sibling_late_start")
        return tok

    loss_l, dx, g = _local_step(x[0], loss_target[0], h1, norm1_w, norm2_w, fnw, gp, gdn_norm_w, wp,
                                conv_f, fcw, rest_weights, (early_sibling, early_chips, late_sibling))
    fams, got = _grad_sibling_wait(early["late_sibling"], [dx], "grad_sibling_late_wait")
    late_part = _add_sibling_split(fams[0], got[0], cq, "add_sibling_w_in")
    *late_started, late_token = _grad_chips_start([late_part[1]], "grad_chips_late_start")
    small = _pack_small(g["n1w"], g["n2w"], g["fnw"], g["gp"], g["gnw"], g["conv_w"], g["fcw_g"], g["fcw_u"], loss_l)
    *small_started, small_token = _grad_chips_start([], "small_gather_start", small)
    got3_e = _grad_chips_wait(early["started"], [late_token, small_token], "grad_chips_wait")
    g_w_up, g_w_down, g_w_out = _grad_share(
        [_add_chips(p[0], r3, cq, "add_chips_" + nm) for p, r3, nm in zip(early["parts"], got3_e, early_names)],
        "grad_share_early")
    big = {}

    def adamw_big(nm, w, gg, m, v):
        d_, m_, v_ = _adamw(w[0], gg, m[0], v[0], "adamw_" + nm)
        big[nm] = (gg[None], d_[None], m_[None], v_[None])

    adamw_big("w_up", w_up, g_w_up, m_w_up, v_w_up)
    adamw_big("w_down", w_down, g_w_down, m_w_down, v_w_down)
    adamw_big("w_out", w_out, g_w_out, m_w_out, v_w_out)
    got3, = _grad_chips_wait(late_started, [big[nm][1] for nm in early_names], "grad_chips_late_wait")
    g_w_in, = _grad_share([_add_chips(late_part[0], got3, cq, "add_chips_w_in")], "grad_share_late")
    g_t, d_t, m_t, v_t = _adamw_transposed(w_in_l, g_w_in, m_w_in_l, v_w_in_l, "adamw_w_in")
    big["w_in"] = tuple(jnp.swapaxes(t[None], 1, 2) for t in (g_t, d_t, m_t, v_t))
    small_all, small = _grad_chips_wait(small_started, [d_t], "small_gather_wait", with_small=True)
    small_names = ["norm1_w", "norm2_w", "final_norm_w", "a_log", "dt_bias", "gdn_norm_w", "conv_qkv_w", "ffn_conv_w"]
    loss_b, *small_out = _small_step(
        jnp.stack([2 * q + c, q]).astype(jnp.int32), small_all, small,
        [norm1_w, norm2_w, final_norm_w[None], a_log, dt_bias, gdn_norm_w, conv_qkv_w[0], ffn_conv_w[0]],
        [m_norm1_w, m_norm2_w, m_final_norm_w[None], m_a_log, m_dt_bias, m_gdn_norm_w, m_conv_qkv_w[0], m_ffn_conv_w[0]],
        [v_norm1_w, v_norm2_w, v_final_norm_w[None], v_a_log, v_dt_bias, v_gdn_norm_w, v_conv_qkv_w[0], v_ffn_conv_w[0]])
    like = dict(final_norm_w=lambda t: t[0], conv_qkv_w=lambda t: t[None], ffn_conv_w=lambda t: t[None])
    for k, nm in enumerate(small_names):
        big[nm] = tuple(like.get(nm, lambda t: t)(t) for t in small_out[4 * k:4 * k + 4])
    names = ["norm1_w", "w_in", "conv_qkv_w", "a_log", "dt_bias", "gdn_norm_w", "w_out", "norm2_w", "w_up",
             "ffn_conv_w", "w_down", "final_norm_w"]
    return (loss_b[0, 0], dx[None], *[big[n][0] for n in names], *[big[n][1] for n in names],
            *[big[n][2] for n in names], *[big[n][3] for n in names])
```

```python
import functools
import math

import numpy as np
import jax
import jax.numpy as jnp
from jax import lax
from jax.experimental import pallas as pl
from jax.experimental.pallas import tpu as pltpu

F32 = jnp.float32
BF16 = jnp.bfloat16
_MXU = jnp.bfloat16
_HI = lax.Precision.HIGHEST
EPS = 1e-6
V7X_VMEM_LIMIT = 56 * 1024 * 1024
MESH = pl.DeviceIdType.MESH

D_MODEL = 1024
GDN_HEADS, GDN_DIM, GDN_CHUNK, GDN_CONV = 4, 128, 64, 4
GDN_WIDTH = GDN_HEADS * GDN_DIM
DIL_HEADS, DIL_DIM = 8, 64
DIL_WIDTH = DIL_HEADS * DIL_DIM
D_FF, FFN_CONV = 2816, 3
IN_COLS = 3592
P_COLS = 3840
P_Z, P_QKVB, P_BA = 1536, 2048, 3584
ATT_T = 1024
ADAM_LR, ADAM_B1, ADAM_B2, ADAM_EPS, ADAM_WD, ADAM_STEP = 0.001, 0.9, 0.999, 1e-08, 0.01, 10
N_CHIPS = 4


def _cparams(sem=None, vmem=None):
    kw = {}
    if sem is not None:
        kw["dimension_semantics"] = sem
    if vmem is not None:
        kw["vmem_limit_bytes"] = vmem
    return pltpu.CompilerParams(**kw)


def _silu(x):
    return x * jax.nn.sigmoid(x)


def _pick_tile(n, cap):
    best = None
    for t in range(128, min(n, cap) + 1, 128):
        if n % t == 0:
            best = t
    return best or n


MM_MIN_STEPS = 4

def _mm(a, b, mode, *, out_dtype=F32, residual=None, name, b_blocks=False, place=None, into=None, tn=None, after=(),
        normed_by=None):
    if mode == "nn":
        M, K = a.shape
        N = b.shape[0] * b.shape[2] if b_blocks else b.shape[1]
    elif mode == "nt":
        (M, K), (N, _) = a.shape, b.shape
    else:
        (K, M), (_, N) = a.shape, b.shape
    tm = _pick_tile(M, 1024)
    tn = b.shape[2] if b_blocks else (tn or _pick_tile(N, 1536))

    def vmem(tm, tn):
        return 2 * (tm * K * a.dtype.itemsize + tn * K * b.dtype.itemsize
                    + tm * tn * (jnp.dtype(out_dtype).itemsize + (4 if residual is not None else 0))) + 3 * tm * tn * 4

    fixed_tn = b_blocks or (place is not None and place[0] == "blocks")
    while vmem(tm, tn) > 40 * 1024 * 1024:
        if (tm >= tn or fixed_tn) and tm % 256 == 0:
            tm //= 2
        elif tn % 256 == 0 and not fixed_tn:
            tn //= 2
        else:
            tm //= 2
    while (N // tn) * (M // tm) < MM_MIN_STEPS and tm % 256 == 0:
        tm //= 2
    a_spec = pl.BlockSpec((K, tm), lambda j, i: (0, i)) if mode == "tn" else pl.BlockSpec((tm, K), lambda j, i: (i, 0))
    if b_blocks:
        b_spec = pl.BlockSpec((None, K, tn), lambda j, i: (j, 0, 0))
    else:
        b_spec = pl.BlockSpec((tn, K), lambda j, i: (j, 0)) if mode == "nt" else pl.BlockSpec((K, tn), lambda j, i: (0, j))
    r_spec = pl.BlockSpec((tm, tn), lambda j, i: (i, j))
    if place is None:
        o_spec, o_shape = r_spec, (M, N)
    elif place[0] == "rows":
        off = place[2] // tm
        o_spec, o_shape = pl.BlockSpec((tm, tn), lambda j, i: (i + off, j)), (place[1], N)
    else:
        off = place[2]
        o_spec, o_shape = pl.BlockSpec((None, tm, tn), lambda j, i: (j + off, i, 0)), (place[1], M, tn)
    dims = {"nn": (((1,), (0,)), ((), ())), "nt": (((1,), (1,)), ((), ())), "tn": (((0,), (0,)), ((), ()))}[mode]

    def body(*refs):
        a_ref, b_ref = refs[0], refs[1]
        o_ref = refs[-1] if normed_by is None else refs[-2]
        acc = lax.dot_general(a_ref[...].astype(_MXU), b_ref[...].astype(_MXU), dims, preferred_element_type=F32)
        if residual is not None:
            acc = acc + refs[2][...]
        o_ref[...] = acc.astype(out_dtype)
        if normed_by is not None:
            rs = lax.rsqrt(jnp.mean(acc * acc, axis=-1, keepdims=True) + EPS)
            refs[-1][...] = (acc * rs * refs[len(ins0)][...]).astype(_MXU)

    ins, specs, alias = [a, b], [a_spec, b_spec], {}
    if residual is not None:
        ins.append(residual)
        specs.append(r_spec)
    ins0 = list(ins)
    if normed_by is not None:
        assert tn == N and place is None and into is None
        ins.append(normed_by)
        specs.append(pl.BlockSpec((1, N), lambda j, i: (0, 0)))
    if into is not None:
        alias = {len(ins): 0}
        ins.append(into)
        specs.append(pl.BlockSpec(memory_space=pl.ANY))
    ins += list(after)
    specs += [pl.BlockSpec(memory_space=pl.ANY)] * len(after)
    o_shape = jax.ShapeDtypeStruct(o_shape, out_dtype)
    if normed_by is not None:
        o_spec, o_shape = (o_spec, r_spec), (o_shape, jax.ShapeDtypeStruct((M, N), _MXU))
    return pl.pallas_call(
        body, name=name, grid=(N // tn, M // tm), in_specs=specs, out_specs=o_spec,
        out_shape=o_shape, input_output_aliases=alias,
        compiler_params=_cparams(("parallel", "parallel"), V7X_VMEM_LIMIT),
    )(*ins)


def _wp_assemble(g_in, after=()):
    nb, Dm, Wb = g_in.shape
    T = 256
    n_lo = P_QKVB - 2 * Wb

    def body(g_ref, *rest):
        g2 = g_ref[2]
        rest[-1][...] = jnp.concatenate(
            [g_ref[0], g_ref[1], g2[:, :n_lo], g2[:, n_lo + 8:], g_ref[3], g2[:, n_lo:n_lo + 8],
             jnp.zeros((T, P_COLS - P_BA - 8), g_in.dtype)], axis=1)

    return pl.pallas_call(
        body, name="wp_assemble", grid=(Dm // T,),
        in_specs=[pl.BlockSpec((nb, T, Wb), lambda i: (0, i, 0))] + [pl.BlockSpec(memory_space=pl.ANY)] * len(after),
        out_specs=pl.BlockSpec((T, P_COLS), lambda i: (i, 0)), out_shape=jax.ShapeDtypeStruct((Dm, P_COLS), g_in.dtype),
        compiler_params=_cparams(("parallel",)),
    )(g_in, *after)


def _rmsnorm_fwd(x, w, name, after=()):
    S, D = x.shape
    T = _pick_tile(S, 512)

    def body(x_ref, w_ref, *rest):
        xv = x_ref[...]
        rs = lax.rsqrt(jnp.mean(xv * xv, axis=-1, keepdims=True) + EPS)
        rest[-1][...] = (xv * rs * w_ref[...]).astype(rest[-1].dtype)

    return pl.pallas_call(
        body, name=name, grid=(S // T,),
        in_specs=[pl.BlockSpec((T, D), lambda i: (i, 0)), pl.BlockSpec((1, D), lambda i: (0, 0))] + [_ANY] * len(after),
        out_specs=pl.BlockSpec((T, D), lambda i: (i, 0)),
        out_shape=jax.ShapeDtypeStruct((S, D), _MXU),
        compiler_params=_cparams(("parallel",)),
    )(x, w, *after)


def _rmsnorm_bwd(dh, x, w, dres, name, after=()):
    S, D = x.shape
    pair = isinstance(dh, tuple)
    T = _pick_tile(S, 256 if pair else 512)
    dhs = [*dh[0], dh[1]] if pair else [dh]

    def body(*refs):
        x_ref, w_ref, dres_ref = refs[len(dhs):len(dhs) + 3]
        dx_ref, dw_ref = refs[-2:]
        xv = x_ref[...]
        rs = lax.rsqrt(jnp.mean(xv * xv, axis=-1, keepdims=True) + EPS)
        xn = xv * rs
        if pair:
            b_ref = refs[len(dhs) - 1]
            nb, _, Kb = b_ref.shape
            per = nb // (len(dhs) - 1)
            dhv = None
            for blk in range(nb):
                lo = (blk % per) * Kb
                t = lax.dot_general(refs[blk // per][:, lo:lo + Kb].astype(_MXU), b_ref[blk].astype(_MXU), (_NT, ((), ())),
                                    preferred_element_type=F32)
                dhv = t if dhv is None else dhv + t
        else:
            dhv = refs[0][...]
        dxn = dhv * w_ref[...]
        dx_ref[...] = dres_ref[...] + rs * (dxn - xn * jnp.mean(dxn * xn, axis=-1, keepdims=True))

        @pl.when(pl.program_id(0) == 0)
        def _():
            dw_ref[...] = jnp.zeros_like(dw_ref)

        dw_ref[...] += jnp.sum(dhv * xn, axis=0, keepdims=True)

    row = pl.BlockSpec((T, D), lambda i: (i, 0))
    vec = pl.BlockSpec((1, D), lambda i: (0, 0))
    dh_specs = [row] if not pair else (
        [pl.BlockSpec((T, a.shape[1]), lambda i: (i, 0)) for a in dh[0]] + [pl.BlockSpec(dh[1].shape, lambda i: (0, 0, 0))])
    return pl.pallas_call(
        body, name=name, grid=(S // T,), in_specs=dh_specs + [row, vec, row] + [_ANY] * len(after), out_specs=(row, vec),
        out_shape=(jax.ShapeDtypeStruct((S, D), F32), jax.ShapeDtypeStruct((1, D), F32)),
        compiler_params=_cparams(("arbitrary",), V7X_VMEM_LIMIT if pair else None),
    )(*dhs, x, w, dres, *after)


def _loss_head(x3, w, tgt, name):
    S, D = tgt.shape
    fused = isinstance(x3, tuple)
    T = _pick_tile(S, 256 if fused else 512)
    xs = list(x3) if fused else [x3]

    def body(*refs):
        w_ref, t_ref = refs[len(xs):len(xs) + 2]
        loss_ref, dx_ref, dxn_ref, dw_ref = refs[-4:]
        xv = refs[0][...]
        if fused:
            xv = lax.dot_general(xv.astype(_MXU), refs[1][...].astype(_MXU), (_NN, ((), ())),
                                 preferred_element_type=F32) + refs[2][...]
        rs = lax.rsqrt(jnp.mean(xv * xv, axis=-1, keepdims=True) + EPS)
        xn = xv * rs
        err = xn * w_ref[...] - t_ref[...]
        dy = err * (1.0 / D)
        dxn = dy * w_ref[...]
        dxv = rs * (dxn - xn * jnp.mean(dxn * xn, axis=-1, keepdims=True))
        dx_ref[...] = dxv
        dxn_ref[...] = dxv.astype(dxn_ref.dtype)

        @pl.when(pl.program_id(0) == 0)
        def _():
            dw_ref[...] = jnp.zeros_like(dw_ref)
            loss_ref[...] = jnp.zeros_like(loss_ref)

        dw_ref[...] += jnp.sum(dy * xn, axis=0, keepdims=True)
        part = jnp.sum(jnp.sum(err * err, axis=-1, keepdims=True), axis=0, keepdims=True) * (0.5 / D)
        loss_ref[...] += jnp.broadcast_to(part, loss_ref.shape)

    row = pl.BlockSpec((T, D), lambda i: (i, 0))
    vec = pl.BlockSpec((1, D), lambda i: (0, 0))
    x_specs = [row] if not fused else [pl.BlockSpec((T, xs[0].shape[1]), lambda i: (i, 0)),
                                       pl.BlockSpec(xs[1].shape, lambda i: (0, 0)), row]
    return pl.pallas_call(
        body, name=name, grid=(S // T,), in_specs=x_specs + [vec, row],
        out_specs=(pl.BlockSpec((8, 128), lambda i: (0, 0)), row, row, vec),
        out_shape=(jax.ShapeDtypeStruct((8, 128), F32), jax.ShapeDtypeStruct((S, D), F32), jax.ShapeDtypeStruct((S, D), _MXU),
                   jax.ShapeDtypeStruct((1, D), F32)),
        compiler_params=_cparams(("arbitrary",), V7X_VMEM_LIMIT if fused else None),
    )(*xs, w, tgt)


def _shifted(ext, back, lo, n):
    if back == 0:
        return ext[lo:lo + n, :]
    return pltpu.roll(ext, back % ext.shape[0], 0)[lo:lo + n, :]


def _conv_windows(ext, K, T):
    return [_shifted(ext, (K - 1) - i, 8, T) for i in range(K)]


def _conv_taps(ext, w, K, T):
    out = None
    for i, win in enumerate(_conv_windows(ext, K, T)):
        term = win * w[i:i + 1, :]
        out = term if out is None else out + term
    return out


def _conv_taps_t(ext, w, K, T):
    out = None
    for i in range(K):
        term = _shifted(ext, i - (K - 1), 0, T) * w[i:i + 1, :]
        out = term if out is None else out + term
    return out


def _tri_masks(C):
    r = lax.broadcasted_iota(jnp.int32, (C, C), 0)
    c = lax.broadcasted_iota(jnp.int32, (C, C), 1)
    return r == c, r >= c, r > c, r <= c


_NN, _NT, _TN = ((1,), (0,)), ((1,), (1,)), ((0,), (0,))
_GDN_PASSES = dict(qk=1, inv=1, sol=1, scan=1, bwd=1)


def _bdot_raw(a, b, kind, passes):
    dims = ({"NN": ((2,), (1,)), "NT": ((2,), (2,)), "TN": ((1,), (1,))}[kind], ((0,), (0,)))
    if passes == 0:
        return lax.dot_general(a, b, dims, precision=_HI, preferred_element_type=F32)
    ah, bh = a.astype(BF16), b.astype(BF16)
    out = lax.dot_general(ah, bh, dims, preferred_element_type=F32)
    if passes == 3:
        al, bl = (a - ah.astype(F32)).astype(BF16), (b - bh.astype(F32)).astype(BF16)
        out = out + lax.dot_general(ah, bl, dims, preferred_element_type=F32) + lax.dot_general(al, bh, dims, preferred_element_type=F32)
    return out


@functools.partial(jax.custom_vjp, nondiff_argnums=(2, 3))
def _bdot(a, b, kind, passes):
    return _bdot_raw(a, b, kind, passes)


def _bdot_fwd(a, b, kind, passes):
    return _bdot_raw(a, b, kind, passes), (a, b)


def _bdot_bwd(kind, passes, res, ct):
    a, b = res
    if kind == "NN":
        return _bdot_raw(ct, b, "NT", passes), _bdot_raw(a, ct, "TN", passes)
    if kind == "NT":
        return _bdot_raw(ct, b, "NN", passes), _bdot_raw(ct, a, "TN", passes)
    return _bdot_raw(b, ct, "NT", passes), _bdot_raw(a, ct, "NN", passes)


_bdot.defvjp(_bdot_fwd, _bdot_bwd)


def _softplus(x):
    return jnp.maximum(x, 0.0) + jnp.log(1.0 + jnp.exp(-jnp.abs(x)))


def _gdn_stage1(cq, ck, cv, b_col, a_col, alog, dtb, dot=_bdot_raw):
    C = cq.shape[1]
    eye, incl, strict, incl_t = _tri_masks(C)
    qn = cq * lax.rsqrt(jnp.sum(cq * cq, axis=-1, keepdims=True) + EPS) * (GDN_DIM ** -0.5)
    kn = ck * lax.rsqrt(jnp.sum(ck * ck, axis=-1, keepdims=True) + EPS)
    beta = jax.nn.sigmoid(b_col)
    g = -jnp.exp(alog) * _softplus(a_col + dtb)
    g_row = jnp.sum(jnp.where(eye, g, 0.0), axis=1, keepdims=True)
    beta_row = jnp.sum(jnp.where(eye, beta, 0.0), axis=1, keepdims=True)
    gc_col = jnp.sum(jnp.where(incl, g_row, 0.0), axis=2, keepdims=True)
    gc_row = jnp.sum(jnp.where(incl_t, g, 0.0), axis=1, keepdims=True)
    dec = jnp.where(incl, jnp.exp(jnp.where(incl, gc_col - gc_row, 0.0)), 0.0)
    kk = dot(kn, kn, "NT", _GDN_PASSES["qk"])
    qk = dot(qn, kn, "NT", _GDN_PASSES["qk"])
    lmat = jnp.where(strict, dec * kk * beta_row, 0.0)
    attn = dec * qk * beta_row
    gam = jnp.exp(gc_col)
    gc_last = gc_col[:, C - 1:C, :]
    k_end = kn * (jnp.exp(gc_last - gc_col) * beta)
    return lmat, cv, gam * kn, gam * qn, attn, k_end, jnp.exp(gc_last)


def _tri_inv(lmat):
    C = lmat.shape[1]
    eye = _tri_masks(C)[0]
    ps = _GDN_PASSES["inv"]
    p = jnp.where(eye, 1.0, 0.0) - lmat
    lp = _bdot_raw(lmat, lmat, "NN", ps)
    n = int(math.log2(C))
    for s in range(1, n):
        p = p + _bdot_raw(p, lp, "NN", ps)
        if s < n - 1:
            lp = _bdot_raw(lp, lp, "NN", ps)
    return p


def _gated_norm(o, z, gnw):
    on = o * lax.rsqrt(jnp.mean(o * o, axis=-1, keepdims=True) + EPS) * gnw
    return on * _silu(z)


GDN_PG = 4
GDN_SG = 4


def _gdn_pairs(c, ba, gp, G):
    C, W, H = GDN_CHUNK, GDN_WIDTH, GDN_HEADS
    pairs = [(j, h) for j in range(G) for h in range(H)]
    cq, ck, cv = (jnp.stack([c[C * j:C * (j + 1), o + GDN_DIM * h:o + GDN_DIM * (h + 1)] for j, h in pairs]) for o in (0, W, 2 * W))
    b_col = jnp.stack([ba[C * j:C * (j + 1), h:h + 1] for j, h in pairs])
    a_col = jnp.stack([ba[C * j:C * (j + 1), H + h:H + h + 1] for j, h in pairs])
    alog = jnp.stack([gp[0:1, h:h + 1] for j, h in pairs])
    dtb = jnp.stack([gp[0:1, H + h:H + h + 1] for j, h in pairs])
    return pairs, (cq, ck, cv, b_col, a_col, alog, dtb)


def _gdn_pre_specs(S, G):
    C = GDN_CHUNK
    T = C * G
    return dict(
        cur=pl.BlockSpec((T, 3 * GDN_WIDTH), lambda i: (i, 0)),
        prev=pl.BlockSpec((8, 3 * GDN_WIDTH), lambda i: (jnp.maximum(i * (T // 8) - 1, 0), 0)),
        ba=pl.BlockSpec((T, 128), lambda i: (i, P_BA // 128)),
        cw=pl.BlockSpec((GDN_CONV, 3 * GDN_WIDTH), lambda i: (0, 0)),
        vec=pl.BlockSpec((1, 128), lambda i: (0, 0)),
        hd=pl.BlockSpec((GDN_HEADS, T, GDN_DIM), lambda i: (0, i, 0)),
        hc=pl.BlockSpec((GDN_HEADS, T, C), lambda i: (0, i, 0)),
        ge=pl.BlockSpec((G, GDN_HEADS, 8, 128), lambda i: (i, 0, 0, 0)),
    )


def _hd_shape(S, last=GDN_DIM):
    return jax.ShapeDtypeStruct((GDN_HEADS, S, last), F32)


def _gdn_pre(proj, conv_w, gp):
    S = proj.shape[0]
    C, G = GDN_CHUNK, GDN_PG
    nc = S // C
    sp = _gdn_pre_specs(S, G)

    def body(cur_ref, prev_ref, ba_ref, cw_ref, gp_ref, uv_ref, wk_ref, qd_ref, ke_ref, at_ref, ti_ref, ge_ref):
        prev = prev_ref[...] * jnp.where(pl.program_id(0) == 0, 0.0, 1.0)
        c = _silu(_conv_taps(jnp.concatenate([prev, cur_ref[...]], axis=0), cw_ref[...], GDN_CONV, C * G))
        pairs, args = _gdn_pairs(c, ba_ref[...], gp_ref[...], G)
        lmat, v, rk, q_dec, attn, k_end, g_end = _gdn_stage1(*args)
        t = _tri_inv(lmat)
        u_v = _bdot_raw(t, v, "NN", _GDN_PASSES["sol"])
        w_k = _bdot_raw(t, rk, "NN", _GDN_PASSES["sol"])
        for b, (j, h) in enumerate(pairs):
            rows = slice(C * j, C * (j + 1))
            uv_ref[h, rows, :] = u_v[b]
            wk_ref[h, rows, :] = w_k[b]
            qd_ref[h, rows, :] = q_dec[b]
            ke_ref[h, rows, :] = k_end[b]
            at_ref[h, rows, :] = attn[b]
            ti_ref[h, rows, :] = t[b]
            ge_ref[j, h] = jnp.broadcast_to(g_end[b], (8, 128))

    return pl.pallas_call(
        body, name="gdn_pre", grid=(nc // G,),
        in_specs=[sp["cur"], sp["prev"], sp["ba"], sp["cw"], sp["vec"]],
        out_specs=(sp["hd"], sp["hd"], sp["hd"], sp["hd"], sp["hc"], sp["hc"], sp["ge"]),
        out_shape=(_hd_shape(S), _hd_shape(S), _hd_shape(S), _hd_shape(S), _hd_shape(S, C), _hd_shape(S, C),
                   jax.ShapeDtypeStruct((nc, GDN_HEADS, 8, 128), F32)),
        compiler_params=_cparams(("parallel",)),
    )(proj, proj, proj, conv_w, gp)


def _gdn_scan_specs(S, G, rev):
    C = GDN_CHUNK
    T = C * G
    n = S // T
    ci = (lambda i: n - 1 - i) if rev else (lambda i: i)
    return dict(
        hd=pl.BlockSpec((GDN_HEADS, T, GDN_DIM), lambda i: (0, ci(i), 0)),
        hc=pl.BlockSpec((GDN_HEADS, T, C), lambda i: (0, ci(i), 0)),
        ge=pl.BlockSpec((G, GDN_HEADS, 8, 128), lambda i: (ci(i), 0, 0, 0)),
        z=pl.BlockSpec((T, GDN_WIDTH), lambda i: (ci(i), P_Z // GDN_WIDTH)),
        oa=pl.BlockSpec((T, GDN_WIDTH), lambda i: (ci(i), 0)),
        vec=pl.BlockSpec((1, 128), lambda i: (0, 0)),
        st=pl.BlockSpec((G, GDN_HEADS, GDN_DIM, GDN_DIM), lambda i: (ci(i), 0, 0, 0)),
    )


def _gdn_scan(u_v, w_k, q_dec, k_end, attn, g_end, proj, gnw, mix, after=()):
    S = proj.shape[0]
    C, G = GDN_CHUNK, GDN_SG
    nc = S // C
    sp = _gdn_scan_specs(S, G, False)
    ps = _GDN_PASSES["scan"]

    def body(uv_ref, wk_ref, qd_ref, ke_ref, at_ref, ge_ref, z_ref, gnw_ref, *rest):
        oa_ref, st_ref, s_scr = rest[-3:]

        @pl.when(pl.program_id(0) == 0)
        def _():
            s_scr[...] = jnp.zeros_like(s_scr)

        for j in range(G):
            rows = slice(C * j, C * (j + 1))
            st = s_scr[...]
            st_ref[j] = st
            u = uv_ref[:, rows, :] - _bdot_raw(wk_ref[:, rows, :], st, "NN", ps)
            o = _bdot_raw(qd_ref[:, rows, :], st, "NN", ps) + _bdot_raw(at_ref[:, rows, :], u, "NN", ps)
            s_scr[...] = ge_ref[j][:, 0:1, 0:1] * st + _bdot_raw(ke_ref[:, rows, :], u, "TN", ps)
            for h in range(GDN_HEADS):
                cols = slice(GDN_DIM * h, GDN_DIM * (h + 1))
                oa_ref[rows, cols] = _gated_norm(o[h], z_ref[rows, cols], gnw_ref[...])

    return pl.pallas_call(
        body, name="gdn_scan", grid=(nc // G,),
        in_specs=[sp["hd"], sp["hd"], sp["hd"], sp["hd"], sp["hc"], sp["ge"], sp["z"], sp["vec"]] + [_ANY] * (1 + len(after)),
        out_specs=(sp["oa"], sp["st"]),
        out_shape=(jax.ShapeDtypeStruct(mix.shape, F32),
                   jax.ShapeDtypeStruct((nc, GDN_HEADS, GDN_DIM, GDN_DIM), F32)),
        input_output_aliases={8: 0},
        scratch_shapes=[pltpu.VMEM((GDN_HEADS, GDN_DIM, GDN_DIM), F32)],
        compiler_params=_cparams(("arbitrary",)),
    )(u_v, w_k, q_dec, k_end, attn, g_end, proj, gnw, mix, *after)


def _gdn_scan_bwd(u_v, w_k, q_dec, k_end, attn, g_end, proj, gnw, states, d_oa):
    S = proj.shape[0]
    C, G = GDN_CHUNK, GDN_SG
    nc = S // C
    sp = _gdn_scan_specs(S, G, True)
    ps, pb = _GDN_PASSES["scan"], _GDN_PASSES["bwd"]

    def body(uv_ref, wk_ref, qd_ref, ke_ref, at_ref, ge_ref, z_ref, gnw_ref, st_ref, doa_ref,
             duv_ref, dwk_ref, dqd_ref, dke_ref, dat_ref, dge_ref, dz_ref, dgnw_ref, ds_scr):
        @pl.when(pl.program_id(0) == 0)
        def _():
            ds_scr[...] = jnp.zeros_like(ds_scr)
            dgnw_ref[...] = jnp.zeros_like(dgnw_ref)

        dgnw = jnp.zeros((1, 128), F32)
        for j in reversed(range(G)):
            rows = slice(C * j, C * (j + 1))
            st = st_ref[j]
            wk, qd, ke, at = wk_ref[:, rows, :], qd_ref[:, rows, :], ke_ref[:, rows, :], at_ref[:, rows, :]
            u = uv_ref[:, rows, :] - _bdot_raw(wk, st, "NN", ps)
            o = _bdot_raw(qd, st, "NN", ps) + _bdot_raw(at, u, "NN", ps)
            dos = []
            for h in range(GDN_HEADS):
                cols = slice(GDN_DIM * h, GDN_DIM * (h + 1))
                _, vjp2 = jax.vjp(_gated_norm, o[h], z_ref[rows, cols], gnw_ref[...])
                do_h, dz_h, dgn = vjp2(doa_ref[rows, cols])
                dz_ref[rows, cols] = dz_h
                dgnw = dgnw + dgn
                dos.append(do_h)
            do = jnp.stack(dos)
            ds_new = ds_scr[...]
            du = _bdot_raw(at, do, "TN", pb) + _bdot_raw(ke, ds_new, "NN", pb)
            duv_ref[:, rows, :] = du
            dat_ref[:, rows, :] = _bdot_raw(do, u, "NT", pb)
            dqd_ref[:, rows, :] = _bdot_raw(do, st, "NT", pb)
            dke_ref[:, rows, :] = _bdot_raw(u, ds_new, "NT", pb)
            dwk_ref[:, rows, :] = -_bdot_raw(du, st, "NT", pb)
            d_ge = jnp.sum(jnp.sum(st * ds_new, axis=2, keepdims=True), axis=1, keepdims=True)
            dge_ref[j] = jnp.broadcast_to(d_ge, (GDN_HEADS, 8, 128))
            ds_scr[...] = ge_ref[j][:, 0:1, 0:1] * ds_new + _bdot_raw(qd, do, "TN", pb) - _bdot_raw(wk, du, "TN", pb)
        dgnw_ref[...] += dgnw

    return pl.pallas_call(
        body, name="gdn_scan_bwd", grid=(nc // G,),
        in_specs=[sp["hd"], sp["hd"], sp["hd"], sp["hd"], sp["hc"], sp["ge"], sp["z"], sp["vec"], sp["st"], sp["oa"]],
        out_specs=(sp["hd"], sp["hd"], sp["hd"], sp["hd"], sp["hc"], sp["ge"], sp["oa"], sp["vec"]),
        out_shape=(_hd_shape(S), _hd_shape(S), _hd_shape(S), _hd_shape(S), _hd_shape(S, C),
                   jax.ShapeDtypeStruct((nc, GDN_HEADS, 8, 128), F32), jax.ShapeDtypeStruct((S, GDN_WIDTH), F32),
                   jax.ShapeDtypeStruct((1, 128), F32)),
        scratch_shapes=[pltpu.VMEM((GDN_HEADS, GDN_DIM, GDN_DIM), F32)],
        compiler_params=_cparams(("arbitrary",)),
    )(u_v, w_k, q_dec, k_end, attn, g_end, proj, gnw, states, d_oa)


def _gdn_post(proj, conv_w, gp, tinv, u_v, w_k, d_uv, d_wk, d_qd, d_ke, d_at, d_ge):
    S = proj.shape[0]
    C, G = GDN_CHUNK, GDN_PG
    nc = S // C
    sp = _gdn_pre_specs(S, G)
    pb = _GDN_PASSES["bwd"]

    def body(cur_ref, prev_ref, ba_ref, cw_ref, gp_ref, ti_ref, uv_ref, wk_ref, duv_ref, dwk_ref, dqd_ref, dke_ref,
             dat_ref, dge_ref, dpre_ref, dba_ref, dgp_ref):
        i = pl.program_id(0)

        @pl.when(i == 0)
        def _():
            dgp_ref[...] = jnp.zeros_like(dgp_ref)

        prev = prev_ref[...] * jnp.where(i == 0, 0.0, 1.0)
        pre = _conv_taps(jnp.concatenate([prev, cur_ref[...]], axis=0), cw_ref[...], GDN_CONV, C * G)
        sg = jax.nn.sigmoid(pre)
        dsilu = sg * (1.0 + pre * (1.0 - sg))
        pairs, args = _gdn_pairs(pre * sg, ba_ref[...], gp_ref[...], G)
        _, vjp1 = jax.vjp(functools.partial(_gdn_stage1, dot=_bdot), *args)

        def take(ref):
            return jnp.stack([ref[h, C * j:C * (j + 1), :] for j, h in pairs])

        t, u_v, w_k = take(ti_ref), take(uv_ref), take(wk_ref)
        d_v = _bdot_raw(t, take(duv_ref), "TN", pb)
        d_rk = _bdot_raw(t, take(dwk_ref), "TN", pb)
        d_l = -(_bdot_raw(d_v, u_v, "NT", pb) + _bdot_raw(d_rk, w_k, "NT", pb))
        d_ge = jnp.stack([dge_ref[j, h][0:1, 0:1] for j, h in pairs])
        dcq, dck, dcv, db, da, dalog, ddtb = vjp1((d_l, d_v, d_rk, take(dqd_ref), take(dat_ref), take(dke_ref), d_ge))
        lane = lax.broadcasted_iota(jnp.int32, (C, 128), 1)
        lane1 = lax.broadcasted_iota(jnp.int32, (1, 128), 1)
        dgp = jnp.zeros((1, 128), F32)
        for j in range(G):
            rows = slice(C * j, C * (j + 1))
            dba = jnp.zeros((C, 128), F32)
            for h in range(GDN_HEADS):
                b = GDN_HEADS * j + h
                for o_, dcx in ((0, dcq), (GDN_WIDTH, dck), (2 * GDN_WIDTH, dcv)):
                    cols = slice(o_ + GDN_DIM * h, o_ + GDN_DIM * (h + 1))
                    dpre_ref[rows, cols] = dcx[b] * dsilu[rows, cols]
                dba = dba + jnp.where(lane == h, db[b], 0.0) + jnp.where(lane == GDN_HEADS + h, da[b], 0.0)
                dgp = dgp + jnp.where(lane1 == h, dalog[b], 0.0) + jnp.where(lane1 == GDN_HEADS + h, ddtb[b], 0.0)
            dba_ref[rows, :] = dba
        dgp_ref[0:1, :] += dgp

    T = C * G
    return pl.pallas_call(
        body, name="gdn_post", grid=(nc // G,),
        in_specs=[sp["cur"], sp["prev"], sp["ba"], sp["cw"], sp["vec"], sp["hc"], sp["hd"], sp["hd"], sp["hd"], sp["hd"],
                  sp["hd"], sp["hd"], sp["hc"], sp["ge"]],
        out_specs=(sp["cur"], pl.BlockSpec((T, 128), lambda i: (i, 0)), pl.BlockSpec((8, 128), lambda i: (0, 0))),
        out_shape=(jax.ShapeDtypeStruct((S, 3 * GDN_WIDTH), F32), jax.ShapeDtypeStruct((S, 128), F32),
                   jax.ShapeDtypeStruct((8, 128), F32)),
        compiler_params=_cparams(("arbitrary",)),
    )(proj, proj, proj, conv_w, gp, tinv, u_v, w_k, d_uv, d_wk, d_qd, d_ke, d_at, d_ge)


def _conv_bwd(dpre, x, xcol0, w, K, name, tc):
    S, Cc = dpre.shape
    T = _pick_tile(S, 256)
    nt, ncol = S // T, Cc // tc
    xo = xcol0 // tc

    def body(d_ref, dn_ref, x_ref, xp_ref, w_ref, dx_ref, dw_ref):
        i = pl.program_id(1)
        dn = dn_ref[...] * jnp.where(i == nt - 1, 0.0, 1.0)
        dv = d_ref[...]
        ext_d = jnp.concatenate([dv, dn], axis=0)
        dx_ref[...] = _conv_taps_t(ext_d, w_ref[...], K, T).astype(dx_ref.dtype)
        xp = xp_ref[...] * jnp.where(i == 0, 0.0, 1.0)
        ext_x = jnp.concatenate([xp, x_ref[...]], axis=0)

        @pl.when(i == 0)
        def _():
            dw_ref[...] = jnp.zeros_like(dw_ref)

        for k in range(K):
            dw_ref[k:k + 1, :] += jnp.sum(dv * _shifted(ext_x, (K - 1) - k, 8, T), axis=0, keepdims=True)

    r8 = T // 8
    return pl.pallas_call(
        body, name=name, grid=(ncol, nt),
        in_specs=[pl.BlockSpec((T, tc), lambda j, i: (i, j)),
                  pl.BlockSpec((8, tc), lambda j, i: (jnp.minimum((i + 1) * r8, S // 8 - 1), j)),
                  pl.BlockSpec((T, tc), lambda j, i: (i, j + xo)),
                  pl.BlockSpec((8, tc), lambda j, i: (jnp.maximum(i * r8 - 1, 0), j + xo)),
                  pl.BlockSpec((K, tc), lambda j, i: (0, j))],
        out_specs=(pl.BlockSpec((T, tc), lambda j, i: (i, j)), pl.BlockSpec((K, tc), lambda j, i: (0, j))),
        out_shape=(jax.ShapeDtypeStruct((S, Cc), _MXU), jax.ShapeDtypeStruct((K, Cc), F32)),
        compiler_params=_cparams(("parallel", "arbitrary")),
    )(dpre, dpre, x, x, w)


def _dil_bias(nt, T):
    d = (np.arange(nt)[:, None, None] * T + np.arange(T)[None, None, :] - np.arange(T)[None, :, None])
    cnt = ((d >= 0) & (d <= 128)).astype(np.float64) + ((d >= 0) & (d % 4 == 0) & (d <= 512)) + ((d >= 0) & (d % 16 == 0))
    return jnp.asarray(np.where(cnt > 0, np.log(np.maximum(cnt, 1.0)), -1e30), dtype=F32)


def _attn_fwd(proj, after=()):
    S = proj.shape[0]
    T = min(ATT_T, S)
    nt, H = S // T, T // 2
    bias = _dil_bias(nt, T)
    scale = DIL_DIM ** -0.5
    npair = DIL_WIDTH // 128
    qb0, kb0, vb0 = P_QKVB // 128, (P_QKVB + DIL_WIDTH) // 128, (P_QKVB + 2 * DIL_WIDTH) // 128

    def body(q_ref, k_ref, v_ref, b_ref, *rest):
        o_ref, lse_ref = rest[-2:]
        i = pl.program_id(1)
        qs = (q_ref[...] * scale).astype(_MXU)

        def update(carry, kt, vt, qt, bt):
            out = []
            for hh in range(2):
                m, l, acc = carry[hh]
                sl = slice(hh * DIL_DIM, (hh + 1) * DIL_DIM)
                s = lax.dot_general(kt[:, sl], qt[:, sl], (_NT, ((), ())), preferred_element_type=F32) + bt
                m_new = jnp.maximum(m, jnp.max(s, axis=0, keepdims=True))
                p = jnp.exp(s - m_new)
                a = jnp.exp(m - m_new)
                l = a * l + jnp.sum(p, axis=0, keepdims=True)
                acc = a * acc + lax.dot_general(vt[:, sl], p.astype(_MXU), (_TN, ((), ())), preferred_element_type=F32)
                out.append((m_new, l, acc))
            return tuple(out)

        def keys(j):
            rows = pl.ds(pl.multiple_of(j * T, T), T)
            return k_ref[rows, :].astype(_MXU), v_ref[rows, :].astype(_MXU)

        init = tuple((jnp.full((1, T), -1e30, F32), jnp.zeros((1, T), F32), jnp.zeros((DIL_DIM, T), F32)) for _ in range(2))
        res = lax.fori_loop(0, i, lambda j, carry: update(carry, *keys(j), qs, b_ref[i - j]), init)
        kd, vd = keys(i)
        res = update(res, kd[:H], vd[:H], qs, b_ref[0, :H, :])
        late = update(tuple(tuple(t[:, H:] for t in r) for r in res), kd[H:], vd[H:], qs[H:], b_ref[0, H:, H:])
        res = tuple(tuple(jnp.concatenate([t[:, :H], u], axis=1) for t, u in zip(r, r2)) for r, r2 in zip(res, late))
        lse_ref[...] = jnp.zeros_like(lse_ref)
        for hh in range(2):
            m, l, acc = res[hh]
            o_ref[:, hh * DIL_DIM:(hh + 1) * DIL_DIM] = (acc / l).T
            lse_ref[hh:hh + 1, :] = m + jnp.log(l)

    return pl.pallas_call(
        body, name="attn_fwd", grid=(npair, nt),
        in_specs=[pl.BlockSpec((T, 128), lambda p, i: (i, qb0 + p)),
                  pl.BlockSpec((S, 128), lambda p, i: (0, kb0 + p)),
                  pl.BlockSpec((S, 128), lambda p, i: (0, vb0 + p)),
                  pl.BlockSpec((nt, T, T), lambda p, i: (0, 0, 0))] + [_ANY] * len(after),
        out_specs=(pl.BlockSpec((T, 128), lambda p, i: (i, GDN_WIDTH // 128 + p)),
                   pl.BlockSpec((None, None, 8, T), lambda p, i: (p, i, 0, 0))),
        out_shape=(jax.ShapeDtypeStruct((S, GDN_WIDTH + DIL_WIDTH), F32), jax.ShapeDtypeStruct((npair, nt, 8, T), F32)),
        compiler_params=_cparams(("parallel", "parallel")),
    )(proj, proj, proj, bias, *after)


def _attn_bwd(proj, mix, lse, d_mix):
    S = proj.shape[0]
    T = min(ATT_T, S)
    nt, H = S // T, T // 2
    bias = _dil_bias(nt, T)
    scale = DIL_DIM ** -0.5
    npair = DIL_WIDTH // 128
    qb0, kb0, vb0 = P_QKVB // 128, (P_QKVB + DIL_WIDTH) // 128, (P_QKVB + 2 * DIL_WIDTH) // 128

    def body(q_ref, k_ref, v_ref, o_ref, lse_ref, do_ref, b_ref, dq_ref, dk_ref, dv_ref, dq_scr):
        j = pl.program_id(1)

        @pl.when(j == 0)
        def _():
            dq_scr[...] = jnp.zeros_like(dq_scr)

        kt = k_ref[...].astype(_MXU)
        vt = v_ref[...].astype(_MXU)
        ones = jnp.ones((8, DIL_DIM), F32)

        def block(carry, kt, vt, rows, lsev, bt):
            qs = (q_ref[rows, :] * scale).astype(_MXU)
            dov = do_ref[rows, :]
            prod = dov * o_ref[rows, :]
            dob = dov.astype(_MXU)
            out = []
            dqs = []
            for hh in range(2):
                dk, dv = carry[hh]
                sl = slice(hh * DIL_DIM, (hh + 1) * DIL_DIM)
                s = lax.dot_general(kt[:, sl], qs[:, sl], (_NT, ((), ())), preferred_element_type=F32) + bt
                p = jnp.exp(s - lsev[hh:hh + 1, :])
                delta = lax.dot_general(ones, prod[:, sl], (_NT, ((), ())), precision=_HI, preferred_element_type=F32)[0:1, :]
                dp = lax.dot_general(vt[:, sl], dob[:, sl], (_NT, ((), ())), preferred_element_type=F32)
                ds = (p * (dp - delta)).astype(_MXU)
                dv = dv + lax.dot_general(p.astype(_MXU), dob[:, sl], (_NN, ((), ())), preferred_element_type=F32)
                dk = dk + lax.dot_general(ds, qs[:, sl], (_NN, ((), ())), preferred_element_type=F32)
                dqs.append(lax.dot_general(ds, kt[:, sl], (_TN, ((), ())), preferred_element_type=F32) * scale)
                out.append((dk, dv))
            dq_scr[rows, :] += jnp.concatenate(dqs, axis=1)
            return tuple(out)

        def step(i, carry):
            return block(carry, kt, vt, pl.ds(pl.multiple_of(i * T, T), T), lse_ref[i], b_ref[i - j])

        zeros = tuple((jnp.zeros((H, DIL_DIM), F32), jnp.zeros((H, DIL_DIM), F32)) for _ in range(2))
        lsed = lse_ref[j]
        early = block(zeros, kt[:H], vt[:H], pl.ds(pl.multiple_of(j * T, T), T), lsed, b_ref[0, :H, :])
        late = block(zeros, kt[H:], vt[H:], pl.ds(pl.multiple_of(j * T + H, H), H), lsed[:, H:], b_ref[0, H:, H:])
        init = tuple(tuple(jnp.concatenate([t, u], axis=0) for t, u in zip(r, r2)) for r, r2 in zip(early, late))
        res = lax.fori_loop(j + 1, nt, step, init)
        dk_ref[...] = jnp.concatenate([res[0][0], res[1][0]], axis=1).astype(dk_ref.dtype)
        dv_ref[...] = jnp.concatenate([res[0][1], res[1][1]], axis=1).astype(dv_ref.dtype)

        @pl.when(j == nt - 1)
        def _():
            dq_ref[...] = dq_scr[...].astype(dq_ref.dtype)

    full = lambda c0: pl.BlockSpec((S, 128), lambda p, j: (0, c0 + p))
    tile = lambda c0: pl.BlockSpec((T, 128), lambda p, j: (j, c0 + p))
    out3 = jax.ShapeDtypeStruct((S, DIL_WIDTH), _MXU)
    return pl.pallas_call(
        body, name="attn_bwd", grid=(npair, nt),
        in_specs=[full(qb0), tile(kb0), tile(vb0), full(GDN_WIDTH // 128),
                  pl.BlockSpec((None, nt, 8, T), lambda p, j: (p, 0, 0, 0)), full(GDN_WIDTH // 128),
                  pl.BlockSpec((nt, T, T), lambda p, j: (0, 0, 0))],
        out_specs=(full(0), tile(0), tile(0)),
        out_shape=(out3, out3, out3),
        scratch_shapes=[pltpu.VMEM((S, 128), F32)],
        compiler_params=_cparams(("parallel", "arbitrary")),
    )(proj, proj, proj, mix, lse, d_mix, bias)


def _ffn_act(up, cw):
    S, Cc = up.shape[0], up.shape[1] // 2
    T, tc = _pick_tile(S, 256), _pick_tile(Cc, 1536)
    r16 = T // 16
    nct = Cc // tc

    def body(g_ref, gp_ref, u_ref, up_ref, wg_ref, wu_ref, o_ref):
        keep = jnp.where(pl.program_id(1) == 0, 0.0, 1.0)
        cg = _conv_taps(jnp.concatenate([gp_ref[8:16, :].astype(F32) * keep, g_ref[...].astype(F32)], axis=0),
                        wg_ref[...], FFN_CONV, T)
        cu = _conv_taps(jnp.concatenate([up_ref[8:16, :].astype(F32) * keep, u_ref[...].astype(F32)], axis=0),
                        wu_ref[...], FFN_CONV, T)
        o_ref[...] = (_silu(cg) * cu).astype(o_ref.dtype)

    cur = lambda o: pl.BlockSpec((T, tc), lambda j, i: (i, j + o))
    prev = lambda o: pl.BlockSpec((16, tc), lambda j, i: (jnp.maximum(i * r16 - 1, 0), j + o))
    wsp = lambda o: pl.BlockSpec((FFN_CONV, tc), lambda j, i: (0, j + o))
    return pl.pallas_call(
        body, name="ffn_act", grid=(nct, S // T),
        in_specs=[cur(0), prev(0), cur(nct), prev(nct), wsp(0), wsp(nct)], out_specs=cur(0),
        out_shape=jax.ShapeDtypeStruct((S, Cc), _MXU),
        compiler_params=_cparams(("parallel", "parallel")),
    )(up, up, up, up, cw, cw)


def _ffn_act_bwd(d_act, up, cw):
    S, Cc = up.shape[0], up.shape[1] // 2
    T, tc = _pick_tile(S, 256), _pick_tile(Cc, 1536)
    r8, r16 = T // 8, T // 16
    nt = S // T
    nct = Cc // tc
    K = FFN_CONV

    def body(da_ref, dan_ref, g_ref, gp_ref, gn_ref, u_ref, up_ref, un_ref, wg_ref, wu_ref,
             dg_ref, du_ref, dwg_ref, dwu_ref):
        i = pl.program_id(1)
        keep_p = jnp.where(i == 0, 0.0, 1.0)
        keep_n = jnp.where(i == nt - 1, 0.0, 1.0)
        wg, wu = wg_ref[...], wu_ref[...]
        xg = jnp.concatenate([gp_ref[8:16, :].astype(F32) * keep_p, g_ref[...].astype(F32),
                              gn_ref[0:8, :].astype(F32) * keep_n], axis=0)
        xu = jnp.concatenate([up_ref[8:16, :].astype(F32) * keep_p, u_ref[...].astype(F32),
                              un_ref[0:8, :].astype(F32) * keep_n], axis=0)
        cg = _conv_taps(xg, wg, K, T + 8)
        cu = _conv_taps(xu, wu, K, T + 8)
        da = jnp.concatenate([da_ref[...], dan_ref[...] * keep_n], axis=0)
        sg = jax.nn.sigmoid(cg)
        d_cg = da * cu * (sg * (1.0 + cg * (1.0 - sg)))
        d_cu = da * (cg * sg)
        dg_ref[...] = _conv_taps_t(d_cg, wg, K, T).astype(dg_ref.dtype)
        du_ref[...] = _conv_taps_t(d_cu, wu, K, T).astype(du_ref.dtype)

        @pl.when(i == 0)
        def _():
            dwg_ref[...] = jnp.zeros_like(dwg_ref)
            dwu_ref[...] = jnp.zeros_like(dwu_ref)

        for k in range(K):
            dwg_ref[k:k + 1, :] += jnp.sum(d_cg[0:T, :] * _shifted(xg, (K - 1) - k, 8, T), axis=0, keepdims=True)
            dwu_ref[k:k + 1, :] += jnp.sum(d_cu[0:T, :] * _shifted(xu, (K - 1) - k, 8, T), axis=0, keepdims=True)

    cur = lambda o: pl.BlockSpec((T, tc), lambda j, i: (i, j + o))
    prev = lambda o: pl.BlockSpec((16, tc), lambda j, i: (jnp.maximum(i * r16 - 1, 0), j + o))
    nxt = lambda o: pl.BlockSpec((16, tc), lambda j, i: (jnp.minimum((i + 1) * r16, S // 16 - 1), j + o))
    nxt8 = pl.BlockSpec((8, tc), lambda j, i: (jnp.minimum((i + 1) * r8, S // 8 - 1), j))
    wsp = lambda o: pl.BlockSpec((K, tc), lambda j, i: (0, j + o))
    return pl.pallas_call(
        body, name="ffn_act_bwd", grid=(nct, nt),
        in_specs=[cur(0), nxt8, cur(0), prev(0), nxt(0), cur(nct), prev(nct), nxt(nct), wsp(0), wsp(nct)],
        out_specs=(cur(0), cur(0), wsp(0), wsp(0)),
        out_shape=(jax.ShapeDtypeStruct((S, Cc), _MXU), jax.ShapeDtypeStruct((S, Cc), _MXU),
                   jax.ShapeDtypeStruct((K, Cc), F32), jax.ShapeDtypeStruct((K, Cc), F32)),
        compiler_params=_cparams(("parallel", "arbitrary")),
    )(d_act, d_act, up, up, up, up, up, up, cw, cw)


def _local_step(x, tgt, h1, n1w, n2w, fnw, gp, gnw, wp, conv_w, fcw, rest_weights, early_grads):
    proj = _mm(h1, wp, "nn", name="proj")
    u_v, w_k, q_dec, k_end, attn, tinv, g_end = _gdn_pre(proj, conv_w, gp)
    mix, lse = _attn_fwd(proj)
    mix, states = _gdn_scan(u_v, w_k, q_dec, k_end, attn, g_end, proj, gnw, mix, after=[rest_weights[0]([mix])])
    w_out, w_up4, w_down = rest_weights[1]([mix])
    x2, h2 = _mm(mix, w_out, "nn", residual=x, name="outproj", normed_by=n2w)
    up = _mm(h2, w_up4, "nn", b_blocks=True, out_dtype=_MXU, name="up")
    act = _ffn_act(up, fcw)
    loss, dx3, dx3n, d_fnw = _loss_head((act, w_down, x2), fnw, tgt, "loss_head")
    d_act = _mm(dx3n, w_down, "nt", name="d_act")
    d_wdown = _mm(act, dx3n, "tn", name="d_wdown")
    d_upg, d_upu, d_fcwg, d_fcwu = _ffn_act_bwd(d_act, up, fcw)
    d_wup = _mm(h2, d_upg, "tn", place=("blocks", N_CHIPS, 0), tn=w_up4.shape[2], name="d_wgate")
    d_wup = _mm(h2, d_upu, "tn", place=("blocks", N_CHIPS, N_CHIPS // 2), tn=w_up4.shape[2], into=d_wup, name="d_wup")
    dx2, d_n2w = _rmsnorm_bwd(([d_upg, d_upu], w_up4), x2, n2w, dx3, "norm2_bwd")
    d_wout = _mm(mix, dx2, "tn", name="d_wout")
    token = early_grads[0](d_wup, d_wdown, d_wout)
    d_mix = _mm(dx2, w_out, "nt", name="d_mix", after=[token])
    dq_b, dk_b, dv_b = _attn_bwd(proj, mix, lse, d_mix)
    d_uv, d_wk, d_qd, d_ke, d_at, d_ge, d_z, d_gnw = _gdn_scan_bwd(u_v, w_k, q_dec, k_end, attn, g_end, proj,
                                                                   gnw, states, d_mix)
    d_pre, d_ba, d_gp = _gdn_post(proj, conv_w, gp, tinv, u_v, w_k, d_uv, d_wk, d_qd, d_ke, d_at, d_ge)
    token = early_grads[1]([d_pre])
    d_qkva, d_convw = _conv_bwd(d_pre, proj, 0, conv_w + token[0:1, 0:1], GDN_CONV, "gdn_conv_bwd", 512)
    d_proj = jnp.concatenate([d_qkva, d_z.astype(_MXU), dq_b, dk_b, dv_b, d_ba.astype(_MXU),
                              jnp.zeros((x.shape[0], P_COLS - P_BA - 128), _MXU)], axis=1)
    d_wp = _mm(h1, d_proj, "tn", name="d_wp")
    token = early_grads[2](d_wp)
    dx, d_n1w = _rmsnorm_bwd(([d_proj], wp[None]), x, n1w, dx2, "norm1_bwd", after=[token])
    grads = dict(wp=d_wp, conv_w=d_convw, w_out=d_wout, w_up=d_wup, fcw_g=d_fcwg, fcw_u=d_fcwu, w_down=d_wdown,
                 n1w=d_n1w, n2w=d_n2w, fnw=d_fnw, gp=d_gp, gnw=d_gnw)
    return loss, dx, grads


_HBM = pl.BlockSpec(memory_space=pltpu.HBM)


def _pos():
    return lax.axis_index("x"), lax.axis_index("y"), lax.axis_index("c")


def _other_chips(x, y):
    return [(1 - x, y), (x, 1 - y), (1 - x, 1 - y)]


def _halvable(shape):
    return shape[0] % 32 == 0


def _rows_of_half(shape, half):
    if not _halvable(shape):
        return pl.ds(0, shape[0])
    return pl.ds(pl.multiple_of(half * (shape[0] // 2), 16), shape[0] // 2)


_SEM = pl.BlockSpec(memory_space=pltpu.SEMAPHORE)
_ANY = pl.BlockSpec(memory_space=pl.ANY)
_DATAFLOW = pltpu.SideEffectType.DATAFLOW_SIDE_EFFECTING


def _in_hbm(a):
    return pltpu.with_memory_space_constraint(a, pltpu.HBM)


def _halves_copy(src_refs, land_refs, send_sems, recv_sems, shapes, a, j, block, x, y, c):
    px, py = _other_chips(x, y)[j]
    rows = _rows_of_half(shapes[a], c)
    return pltpu.make_async_remote_copy(
        src_ref=src_refs[a].at[rows, :], dst_ref=land_refs[a].at[block, rows, :], send_sem=send_sems.at[3 * a + j],
        recv_sem=recv_sems.at[3 * a + j], device_id=(px, py, c), device_id_type=MESH)


def _gather_halves_start(shards, after, name):
    n = len(shards)
    shapes = [s.shape for s in shards]

    def body(*refs):
        ins, lands = refs[:n], refs[n:2 * n]
        send_sems, recv_sems = refs[2 * n + 1], refs[2 * n + 2]
        token = refs[-1]
        x, y, c = _pos()
        q = 2 * x + y
        for a in range(n):
            for j in range(3):
                _halves_copy(ins, lands, send_sems, recv_sems, shapes, a, j, q, x, y, c).start()
        token[...] = jnp.zeros_like(token)

    land_shapes = [(N_CHIPS,) + s.shape for s in shards]
    return pl.pallas_call(
        body, name=name,
        out_shape=(pltpu.SemaphoreType.DMA((3 * n,)), pltpu.SemaphoreType.DMA((3 * n,)),
                   *[pltpu.HBM(s.shape, s.dtype) for s in shards],
                   *[pltpu.HBM(ls, s.dtype) for ls, s in zip(land_shapes, shards)],
                   jax.ShapeDtypeStruct((8, 128), F32)),
        in_specs=[_HBM] * (2 * n) + [_ANY],
        out_specs=(_SEM, _SEM, *[_HBM] * (2 * n), pl.BlockSpec(memory_space=pltpu.VMEM)),
        input_output_aliases={a: 2 + a for a in range(2 * n)},
        compiler_params=pltpu.CompilerParams(has_side_effects=_DATAFLOW),
    )(*[_in_hbm(s) for s in shards], *[_in_hbm(lax.empty(ls, s.dtype)) for ls, s in zip(land_shapes, shards)], after)


def _gather_halves_wait(started, after, name):
    send_sems, recv_sems, *thru = started
    n = len(thru) // 2
    shapes = [t.shape for t in thru[:n]]

    def body(*refs):
        ins, lands = refs[:n], refs[n:2 * n]
        send_sems, recv_sems = refs[2 * n], refs[2 * n + 1]
        x, y, c = _pos()
        q = 2 * x + y
        chips = _other_chips(x, y)
        for a in range(n):
            for j, (px, py) in enumerate(chips):
                _halves_copy(ins, lands, send_sems, recv_sems, shapes, a, j, q, x, y, c).wait_send()
                _halves_copy(ins, lands, send_sems, recv_sems, shapes, a, j, 2 * px + py, x, y, c).wait_recv()

    outs = pl.pallas_call(
        body, name=name, out_shape=[pltpu.HBM(t.shape, t.dtype) for t in thru],
        in_specs=[_HBM] * (2 * n) + [_SEM, _SEM] + [_ANY] * len(after), out_specs=[_HBM] * (2 * n),
        input_output_aliases={a: a for a in range(2 * n)},
        compiler_params=pltpu.CompilerParams(has_side_effects=_DATAFLOW),
    )(*thru, send_sems, recv_sems, *after)
    return outs[:n], outs[n:]


def _sibling_fill(gathered, name):
    big = [a for a, g in enumerate(gathered) if _halvable(g.shape[1:])]
    n = len(gathered)

    def body(*refs):
        ins, outs = refs[:n], refs[n:2 * n]
        send_sems, recv_sems = refs[2 * n:]
        x, y, c = _pos()
        chips = _other_chips(x, y)

        def copy(k, j, half):
            a = big[k]
            px, py = chips[j]
            rows = _rows_of_half(gathered[a].shape[1:], half)
            return pltpu.make_async_remote_copy(
                src_ref=ins[a].at[2 * px + py, rows, :], dst_ref=outs[a].at[2 * px + py, rows, :],
                send_sem=send_sems.at[3 * k + j], recv_sem=recv_sems.at[3 * k + j],
                device_id=(x, y, 1 - c), device_id_type=MESH)

        sends = [copy(k, j, c) for k in range(len(big)) for j in range(3)]
        for cp in sends:
            cp.start()
        for k in range(len(big)):
            for j in range(3):
                copy(k, j, 1 - c).wait_recv()
        for cp in sends:
            cp.wait_send()

    return pl.pallas_call(
        body, name=name, in_specs=[_HBM] * n, out_specs=[_HBM] * n,
        out_shape=[jax.ShapeDtypeStruct(g.shape, g.dtype) for g in gathered],
        input_output_aliases={a: a for a in range(n)},
        scratch_shapes=[pltpu.SemaphoreType.DMA((3 * len(big),)), pltpu.SemaphoreType.DMA((3 * len(big),))],
    )(*gathered)


def _fill_copy(refs, send_sems, recv_sems, shapes, a, j, half, x, y, c):
    px, py = _other_chips(x, y)[j]
    rows = _rows_of_half(shapes[a], half)
    return pltpu.make_async_remote_copy(
        src_ref=refs[a].at[2 * px + py, rows, :], dst_ref=refs[a].at[2 * px + py, rows, :],
        send_sem=send_sems.at[3 * a + j], recv_sem=recv_sems.at[3 * a + j],
        device_id=(x, y, 1 - c), device_id_type=MESH)


def _sibling_fill_start(gathered, name):
    n = len(gathered)
    shapes = [g.shape[1:] for g in gathered]

    def body(*refs):
        ins = refs[:n]
        send_sems, recv_sems = refs[n], refs[n + 1]
        token = refs[-1]
        x, y, c = _pos()
        for a in range(n):
            for j in range(3):
                _fill_copy(ins, send_sems, recv_sems, shapes, a, j, c, x, y, c).start()
        token[...] = jnp.zeros_like(token)

    return pl.pallas_call(
        body, name=name,
        out_shape=(pltpu.SemaphoreType.DMA((3 * n,)), pltpu.SemaphoreType.DMA((3 * n,)),
                   *[pltpu.HBM(g.shape, g.dtype) for g in gathered], jax.ShapeDtypeStruct((8, 128), F32)),
        in_specs=[_HBM] * n,
        out_specs=(_SEM, _SEM, *[_HBM] * n, pl.BlockSpec(memory_space=pltpu.VMEM)),
        input_output_aliases={a: 2 + a for a in range(n)},
        compiler_params=pltpu.CompilerParams(has_side_effects=_DATAFLOW),
    )(*[_in_hbm(g) for g in gathered])


def _sibling_fill_wait(started, after, name):
    send_sems, recv_sems, *thru = started
    n = len(thru)
    shapes = [t.shape[1:] for t in thru]

    def body(*refs):
        ins = refs[:n]
        send_sems, recv_sems = refs[n], refs[n + 1]
        x, y, c = _pos()
        for a in range(n):
            for j in range(3):
                _fill_copy(ins, send_sems, recv_sems, shapes, a, j, c, x, y, c).wait_send()
                _fill_copy(ins, send_sems, recv_sems, shapes, a, j, 1 - c, x, y, c).wait_recv()

    return pl.pallas_call(
        body, name=name, out_shape=[pltpu.HBM(t.shape, t.dtype) for t in thru],
        in_specs=[_HBM] * n + [_SEM, _SEM] + [_ANY] * len(after), out_specs=[_HBM] * n,
        input_output_aliases={a: a for a in range(n)},
        compiler_params=pltpu.CompilerParams(has_side_effects=_DATAFLOW),
    )(*thru, send_sems, recv_sems, *after)


def _place_own(shards, gathered, cq, name, carry=()):
    n = len(shards)
    nc = len(carry)
    steps = 4

    def body(cq_ref, *refs):
        for a in range(n):
            refs[2 * n + nc + a][...] = refs[a][...]

    def tile(shape):
        return shape[0] // steps if _halvable(shape) else shape[0]

    in_specs = [pl.BlockSpec((tile(s.shape), s.shape[1]), (lambda i, s_: (i, 0)) if _halvable(s.shape) else (lambda i, s_: (0, 0)))
                for s in shards]
    in_specs += [pl.BlockSpec(memory_space=pl.ANY)] * (n + nc)
    out_specs = [pl.BlockSpec((None, tile(s.shape), s.shape[1]),
                              (lambda i, s_: (s_[1], i, 0)) if _halvable(s.shape) else (lambda i, s_: (s_[1], 0, 0)))
                 for s in shards]
    out_specs += [pl.BlockSpec(memory_space=pl.ANY)] * nc
    gs = pltpu.PrefetchScalarGridSpec(num_scalar_prefetch=1, grid=(steps,), in_specs=in_specs, out_specs=out_specs)
    outs = pl.pallas_call(
        body, name=name, grid_spec=gs,
        out_shape=[jax.ShapeDtypeStruct(g.shape, g.dtype) for g in gathered] + [jax.ShapeDtypeStruct(t.shape, t.dtype) for t in carry],
        input_output_aliases={1 + n + a: a for a in range(n + nc)},
        compiler_params=_cparams(("arbitrary",)),
    )(cq, *shards, *gathered, *carry)
    return (outs[:n], outs[n:]) if nc else outs


def _half_rows(ref, c, rh):
    return ref.at[:, pl.ds(pl.multiple_of(c * rh, 8), rh), :]


def _chips_copy(src_refs, land_refs, send_sems, recv_sems, a, j, x, y, c):
    px, py = _other_chips(x, y)[j]
    return pltpu.make_async_remote_copy(src_ref=src_refs[a].at[2 * px + py], dst_ref=land_refs[a].at[j],
                                        send_sem=send_sems.at[3 * a + j], recv_sem=recv_sems.at[3 * a + j],
                                        device_id=(px, py, c), device_id_type=MESH)


def _peer(r, x, y, c):
    return (x if r & 4 == 0 else 1 - x), (y if r & 2 == 0 else 1 - y), (c if r & 1 == 0 else 1 - c)


def _small_copy(small_ref, all_ref, send_sems, recv_sems, base, r, slot, x, y, c):
    return pltpu.make_async_remote_copy(src_ref=small_ref, dst_ref=all_ref.at[slot], send_sem=send_sems.at[base + r - 1],
                                        recv_sem=recv_sems.at[base + r - 1], device_id=_peer(r, x, y, c), device_id_type=MESH)


def _grad_chips_start(parts, name, small=None):
    n = len(parts)
    srcs = list(parts) + ([] if small is None else [small])
    m = len(srcs)

    def body(*refs):
        ins, lands = refs[:m], refs[m:2 * m]
        send_sems, recv_sems = refs[2 * m], refs[2 * m + 1]
        token = refs[-1]
        x, y, c = _pos()
        for a in range(n):
            for j in range(3):
                _chips_copy(ins, lands, send_sems, recv_sems, a, j, x, y, c).start()
        if small is not None:
            for r in range(1, 8):
                _small_copy(ins[n], lands[n], send_sems, recv_sems, 3 * n, r, 4 * x + 2 * y + c, x, y, c).start()
        token[...] = jnp.zeros_like(token)

    land_shapes = [(3,) + p.shape[1:] for p in parts] + ([] if small is None else [(8,) + small.shape])
    nsem = 3 * n + (0 if small is None else 7)
    return pl.pallas_call(
        body, name=name,
        out_shape=(pltpu.SemaphoreType.DMA((nsem,)), pltpu.SemaphoreType.DMA((nsem,)),
                   *[pltpu.HBM(p.shape, p.dtype) for p in srcs],
                   *[pltpu.HBM(ls, p.dtype) for ls, p in zip(land_shapes, srcs)],
                   jax.ShapeDtypeStruct((8, 128), F32)),
        in_specs=[_HBM] * (2 * m),
        out_specs=(_SEM, _SEM, *[_HBM] * (2 * m), pl.BlockSpec(memory_space=pltpu.VMEM)),
        input_output_aliases={a: 2 + a for a in range(2 * m)},
        compiler_params=pltpu.CompilerParams(has_side_effects=_DATAFLOW),
    )(*[_in_hbm(p) for p in srcs], *[_in_hbm(lax.empty(ls, p.dtype)) for ls, p in zip(land_shapes, srcs)])


def _grad_chips_wait(started, after, name, with_small=False):
    send_sems, recv_sems, *thru = started
    m = len(thru) // 2
    n = m - (1 if with_small else 0)

    def body(*refs):
        ins, lands = refs[:m], refs[m:2 * m]
        send_sems, recv_sems = refs[2 * m], refs[2 * m + 1]
        x, y, c = _pos()
        for a in range(n):
            for j in range(3):
                cp = _chips_copy(ins, lands, send_sems, recv_sems, a, j, x, y, c)
                cp.wait_send()
                cp.wait_recv()
        if with_small:
            for r in range(1, 8):
                px, py, pc = _peer(r, x, y, c)
                _small_copy(ins[n], lands[n], send_sems, recv_sems, 3 * n, r, 4 * x + 2 * y + c, x, y, c).wait_send()
                _small_copy(ins[n], lands[n], send_sems, recv_sems, 3 * n, r, 4 * px + 2 * py + pc, x, y, c).wait_recv()

    outs = pl.pallas_call(
        body, name=name, out_shape=[pltpu.HBM(t.shape, t.dtype) for t in thru],
        in_specs=[_HBM] * (2 * m) + [_SEM, _SEM] + [_ANY] * len(after), out_specs=[_HBM] * (2 * m),
        input_output_aliases={a: a for a in range(2 * m)},
        compiler_params=pltpu.CompilerParams(has_side_effects=_DATAFLOW),
    )(*thru, send_sems, recv_sems, *after)
    return list(outs[m:]) + list(outs[n:m])


def _sibling_copy(src_refs, land_refs, send_sems, recv_sems, rhs, a, c, x, y):
    return pltpu.make_async_remote_copy(src_ref=_half_rows(src_refs[a], 1 - c, rhs[a]), dst_ref=land_refs[a],
                                        send_sem=send_sems.at[a], recv_sem=recv_sems.at[a],
                                        device_id=(x, y, 1 - c), device_id_type=MESH)


def _grad_sibling_start(fams, name):
    n = len(fams)
    rhs = [f.shape[1] // 2 for f in fams]

    def body(*refs):
        ins, lands = refs[:n], refs[n:2 * n]
        send_sems, recv_sems = refs[2 * n], refs[2 * n + 1]
        token = refs[-1]
        x, y, c = _pos()
        for a in range(n):
            _sibling_copy(ins, lands, send_sems, recv_sems, rhs, a, c, x, y).start()
        token[...] = jnp.zeros_like(token)

    land_shapes = [(f.shape[0], f.shape[1] // 2, f.shape[2]) for f in fams]
    return pl.pallas_call(
        body, name=name,
        out_shape=(pltpu.SemaphoreType.DMA((n,)), pltpu.SemaphoreType.DMA((n,)),
                   *[pltpu.HBM(f.shape, f.dtype) for f in fams],
                   *[pltpu.HBM(ls, f.dtype) for ls, f in zip(land_shapes, fams)],
                   jax.ShapeDtypeStruct((8, 128), F32)),
        in_specs=[_HBM] * (2 * n),
        out_specs=(_SEM, _SEM, *[_HBM] * (2 * n), pl.BlockSpec(memory_space=pltpu.VMEM)),
        input_output_aliases={a: 2 + a for a in range(2 * n)},
        compiler_params=pltpu.CompilerParams(has_side_effects=_DATAFLOW),
    )(*[_in_hbm(f) for f in fams], *[_in_hbm(lax.empty(ls, f.dtype)) for ls, f in zip(land_shapes, fams)])


def _grad_sibling_wait(started, after, name):
    send_sems, recv_sems, *thru = started
    n = len(thru) // 2
    rhs = [t.shape[1] // 2 for t in thru[:n]]

    def body(*refs):
        ins, lands = refs[:n], refs[n:2 * n]
        send_sems, recv_sems = refs[2 * n], refs[2 * n + 1]
        x, y, c = _pos()
        for a in range(n):
            cp = _sibling_copy(ins, lands, send_sems, recv_sems, rhs, a, c, x, y)
            cp.wait_send()
            cp.wait_recv()

    outs = pl.pallas_call(
        body, name=name, out_shape=[pltpu.HBM(t.shape, t.dtype) for t in thru],
        in_specs=[_HBM] * (2 * n) + [_SEM, _SEM] + [_ANY] * len(after), out_specs=[_HBM] * (2 * n),
        input_output_aliases={a: a for a in range(2 * n)},
        compiler_params=pltpu.CompilerParams(has_side_effects=_DATAFLOW),
    )(*thru, send_sems, recv_sems, *after)
    return outs[:n], outs[n:]


def _grad_share(fulls, name):
    n = len(fulls)
    rhs = [f.shape[0] // 2 for f in fulls]

    def body(*refs):
        ins, outs = refs[:n], refs[n:2 * n]
        send_sems, recv_sems = refs[2 * n], refs[2 * n + 1]
        x, y, c = _pos()

        def copy(a, half):
            rows = pl.ds(pl.multiple_of(half * rhs[a], 8), rhs[a])
            return pltpu.make_async_remote_copy(src_ref=ins[a].at[rows, :], dst_ref=outs[a].at[rows, :],
                                                send_sem=send_sems.at[a], recv_sem=recv_sems.at[a],
                                                device_id=(x, y, 1 - c), device_id_type=MESH)

        sends = [copy(a, c) for a in range(n)]
        for cp in sends:
            cp.start()
        for a in range(n):
            copy(a, 1 - c).wait_recv()
        for cp in sends:
            cp.wait_send()

    return pl.pallas_call(
        body, name=name, in_specs=[_HBM] * n, out_specs=[_HBM] * n,
        out_shape=[jax.ShapeDtypeStruct(f.shape, f.dtype) for f in fulls],
        input_output_aliases={a: a for a in range(n)},
        scratch_shapes=[pltpu.SemaphoreType.DMA((n,)), pltpu.SemaphoreType.DMA((n,))],
    )(*fulls)


def _add_sibling(own, recv, cq, name):
    nb, R, Cc = own.shape
    Rh = R // 2

    def body(cq_ref, a_ref, b_ref, o32_ref, o16_ref):
        s = a_ref[0] + b_ref[0]
        mine = pl.program_id(0) == cq_ref[1]

        @pl.when(mine)
        def _():
            o32_ref[...] = s

        @pl.when(jnp.logical_not(mine))
        def _():
            o16_ref[0] = s.astype(o16_ref.dtype)

    sp = pl.BlockSpec((1, Rh, Cc), lambda b, s: (b, 0, 0))
    gs = pltpu.PrefetchScalarGridSpec(
        num_scalar_prefetch=1, grid=(nb,),
        in_specs=[pl.BlockSpec((1, Rh, Cc), lambda b, s: (b, s[0], 0)), sp],
        out_specs=[pl.BlockSpec((Rh, Cc), lambda b, s: (0, 0)), sp])
    return pl.pallas_call(
        body, name=name, grid_spec=gs,
        out_shape=[jax.ShapeDtypeStruct((Rh, Cc), F32), jax.ShapeDtypeStruct((nb, Rh, Cc), _MXU)],
        compiler_params=_cparams(("arbitrary",)),
    )(cq, own, recv)


def _add_sibling_split(d_wp, recv, cq, name):
    _, Dm, Pc = d_wp.shape
    Rh = Dm // 2
    Wb = IN_COLS // N_CHIPS
    T = 256

    def body(cq_ref, a_ref, b_ref, o32_ref, o16_ref):
        s = a_ref[0] + b_ref[0]
        blocks = [s[:, 0:Wb], s[:, Wb:2 * Wb],
                  jnp.concatenate([s[:, 2 * Wb:P_QKVB], s[:, P_BA:P_BA + 8], s[:, P_QKVB:3 * Wb - 8]], axis=1),
                  s[:, 3 * Wb - 8:P_BA]]
        q = cq_ref[1]
        own = None
        for j, blk in enumerate(blocks):
            term = jnp.where(q == j, blk, 0.0)
            own = term if own is None else own + term
            o16_ref[j] = blk.astype(o16_ref.dtype)
        o32_ref[...] = own

    gs = pltpu.PrefetchScalarGridSpec(
        num_scalar_prefetch=1, grid=(Rh // T,),
        in_specs=[pl.BlockSpec((1, T, Pc), lambda i, s: (0, s[0] * (Rh // T) + i, 0)), pl.BlockSpec((1, T, Pc), lambda i, s: (0, i, 0))],
        out_specs=[pl.BlockSpec((T, Wb), lambda i, s: (i, 0)), pl.BlockSpec((N_CHIPS, T, Wb), lambda i, s: (0, i, 0))])
    return pl.pallas_call(
        body, name=name, grid_spec=gs,
        out_shape=[jax.ShapeDtypeStruct((Rh, Wb), F32), jax.ShapeDtypeStruct((N_CHIPS, Rh, Wb), _MXU)],
        compiler_params=_cparams(("parallel",)),
    )(cq, d_wp, recv)


def _add_chips(part32, recv3, cq, name):
    Rh, Cc = part32.shape

    def body(cq_ref, a_ref, b_ref, o_ref):
        acc = a_ref[...]
        for j in range(3):
            acc = acc + b_ref[j].astype(F32)
        o_ref[...] = acc

    gs = pltpu.PrefetchScalarGridSpec(
        num_scalar_prefetch=1, grid=(1,),
        in_specs=[pl.BlockSpec((Rh, Cc), lambda i, s: (0, 0)), pl.BlockSpec((3, Rh, Cc), lambda i, s: (0, 0, 0))],
        out_specs=pl.BlockSpec((Rh, Cc), lambda i, s: (s[0], 0)))
    return pl.pallas_call(
        body, name=name, grid_spec=gs, out_shape=jax.ShapeDtypeStruct((2 * Rh, Cc), F32),
        compiler_params=_cparams(("arbitrary",)),
    )(cq, part32, recv3)


def _adamw_transposed(w, g, m, v, name):
    n, Dm = w.shape
    T = 128

    def body(w_ref, g_ref, m_ref, v_ref, gt_ref, d_ref, mo_ref, vo_ref):
        gt = g_ref[...].T
        gt_ref[...] = gt
        d_ref[...], mo_ref[...], vo_ref[...] = _adamw_math(w_ref[...], gt, m_ref[...], v_ref[...])

    row = pl.BlockSpec((T, Dm), lambda i: (i, 0))
    sh = jax.ShapeDtypeStruct((n, Dm), F32)
    return pl.pallas_call(
        body, name=name, grid=(pl.cdiv(n, T),), in_specs=[row, pl.BlockSpec((Dm, T), lambda i: (0, i)), row, row],
        out_specs=(row,) * 4, out_shape=(sh,) * 4, compiler_params=_cparams(("parallel",)),
    )(w, g, m, v)


def _adamw(w, g, m, v, name):
    R, Cc = w.shape
    T = max([t for t in range(8, 257, 8) if R % t == 0], default=R)

    def body(w_ref, g_ref, m_ref, v_ref, d_ref, mo_ref, vo_ref):
        d_ref[...], mo_ref[...], vo_ref[...] = _adamw_math(w_ref[...], g_ref[...], m_ref[...], v_ref[...])

    sp = pl.BlockSpec((T, Cc), lambda i: (i, 0))
    sh = jax.ShapeDtypeStruct((R, Cc), F32)
    return pl.pallas_call(
        body, name=name, grid=(R // T,), in_specs=[sp] * 4, out_specs=(sp, sp, sp), out_shape=(sh, sh, sh),
        compiler_params=_cparams(("parallel",)),
    )(w, g, m, v)


SMALL_ROWS = 32
ROW_CONV, ROW_FCG, ROW_FCU = 5, 13, 22


def _adamw_math(w, g, m, v):
    mn = ADAM_B1 * m + (1.0 - ADAM_B1) * g
    vn = ADAM_B2 * v + (1.0 - ADAM_B2) * (g * g)
    c1 = 1.0 / (1.0 - ADAM_B1 ** ADAM_STEP)
    c2 = 1.0 / (1.0 - ADAM_B2 ** ADAM_STEP)
    return -ADAM_LR * ((mn * c1) / (jnp.sqrt(vn * c2) + ADAM_EPS) + ADAM_WD * w), mn, vn


def _pack_small(n1, n2, fn, gp, gn, conv, fcg, fcu, loss):
    W = D_MODEL

    def body(n1_ref, n2_ref, fn_ref, gp_ref, gn_ref, conv_ref, fcg_ref, fcu_ref, loss_ref, o_ref):
        o_ref[...] = jnp.zeros_like(o_ref)
        o_ref[0:1, :] = n1_ref[...]
        o_ref[1:2, :] = n2_ref[...]
        o_ref[2:3, :] = fn_ref[...]
        o_ref[3:4, 0:8] = gp_ref[0:1, 0:8]
        o_ref[3:4, 8:9] = loss_ref[0:1, 0:1]
        o_ref[4:5, 0:128] = gn_ref[...]
        for i in range(GDN_CONV):
            o_ref[ROW_CONV + 2 * i:ROW_CONV + 2 * i + 1, :] = conv_ref[i:i + 1, 0:W]
            o_ref[ROW_CONV + 2 * i + 1:ROW_CONV + 2 * i + 2, 0:3 * GDN_WIDTH - W] = conv_ref[i:i + 1, W:3 * GDN_WIDTH]
        for r0, ref in ((ROW_FCG, fcg_ref), (ROW_FCU, fcu_ref)):
            for i in range(FFN_CONV):
                for k in range(3):
                    n = min(W, D_FF - k * W)
                    o_ref[r0 + 3 * i + k:r0 + 3 * i + k + 1, 0:n] = ref[i:i + 1, k * W:k * W + n]

    return pl.pallas_call(body, name="pack_small", out_shape=jax.ShapeDtypeStruct((SMALL_ROWS, W), F32))(
        n1, n2, fn, gp, gn, conv, fcg, fcu, loss)


def _small_step(meq, small_all, small, ws, ms, vs):
    W = D_MODEL
    n = len(ws)
    cw, fw = ws[6].shape[1], ws[7].shape[1]

    def body(meq_ref, all_ref, own_ref, *refs):
        w_refs, m_refs, v_refs = refs[:n], refs[n:2 * n], refs[2 * n:3 * n]
        loss_ref = refs[3 * n]
        outs = refs[3 * n + 1:]
        me, q = meq_ref[0], meq_ref[1]
        red = None
        for d in range(8):
            term = jnp.where(me == d, own_ref[...], all_ref[d])
            red = term if red is None else red + term
        loss_ref[...] = jnp.broadcast_to(red[3:4, 8:9], loss_ref.shape)
        conv = [jnp.concatenate([red[ROW_CONV + 2 * i:ROW_CONV + 2 * i + 1, :],
                                 red[ROW_CONV + 2 * i + 1:ROW_CONV + 2 * i + 2, 0:3 * GDN_WIDTH - W]], axis=1)
                for i in range(GDN_CONV)]
        conv = jnp.concatenate(conv, axis=0)

        def fc_rows(r0):
            rows = [jnp.concatenate([red[r0 + 3 * i + k:r0 + 3 * i + k + 1, 0:min(W, D_FF - k * W)] for k in range(3)], axis=1)
                    for i in range(FFN_CONV)]
            return jnp.concatenate(rows, axis=0)

        fc = jnp.concatenate([fc_rows(ROW_FCG), fc_rows(ROW_FCU)], axis=1)

        def chip_block(full, width):
            out = None
            for j in range(N_CHIPS):
                term = jnp.where(q == j, full[:, width * j:width * (j + 1)], 0.0)
                out = term if out is None else out + term
            return out

        grads = [red[0:1, :], red[1:2, :], red[2:3, :], red[3:4, 0:4], red[3:4, 4:8], red[4:5, 0:128],
                 chip_block(conv, cw), chip_block(fc, fw)]
        for k in range(n):
            d_, m_, v_ = _adamw_math(w_refs[k][...], grads[k], m_refs[k][...], v_refs[k][...])
            outs[4 * k][...] = grads[k]
            outs[4 * k + 1][...] = d_
            outs[4 * k + 2][...] = m_
            outs[4 * k + 3][...] = v_

    full = lambda a: pl.BlockSpec(a.shape, lambda i, s_, nd=len(a.shape): (0,) * nd)
    arrays = [small_all, small, *ws, *ms, *vs]
    out_shapes = [jax.ShapeDtypeStruct((8, 128), F32)] + [jax.ShapeDtypeStruct(w.shape, F32) for w in ws for _ in range(4)]
    gs = pltpu.PrefetchScalarGridSpec(
        num_scalar_prefetch=1, grid=(1,), in_specs=[full(a) for a in arrays],
        out_specs=[pl.BlockSpec(o.shape, lambda i, s_, nd=len(o.shape): (0,) * nd) for o in out_shapes])
    return pl.pallas_call(body, name="small_step", grid_spec=gs, out_shape=out_shapes)(meq, *arrays)


def _pad_lanes(v, n=D_MODEL):
    return jnp.pad(v, ((0, 0), (0, n - v.shape[1])))


def kernel(x, norm1_w, w_in, conv_qkv_w, a_log, dt_bias, gdn_norm_w, w_out, norm2_w, w_up, ffn_conv_w, w_down, final_norm_w, loss_target, m_norm1_w, m_w_in, m_conv_qkv_w, m_a_log, m_dt_bias, m_gdn_norm_w, m_w_out, m_norm2_w, m_w_up, m_ffn_conv_w, m_w_down, m_final_norm_w, v_norm1_w, v_w_in, v_conv_qkv_w, v_a_log, v_dt_bias, v_gdn_norm_w, v_w_out, v_norm2_w, v_w_up, v_ffn_conv_w, v_w_down, v_final_norm_w):
    c = lax.axis_index("c")
    q = 2 * lax.axis_index("x") + lax.axis_index("y")
    S = x.shape[1]
    cq = jnp.stack([c, q]).astype(jnp.int32)

    *in_started, in_token = _gather_halves_start([w_in[0].astype(_MXU), conv_qkv_w[0], ffn_conv_w[0]], x, "gather_in_start")
    w_in_l, m_w_in_l, v_w_in_l = (jnp.swapaxes(a + in_token[0:1, 0:1], 1, 2)[0] for a in (w_in, m_w_in, v_w_in))
    h1 = _rmsnorm_fwd(x[0], norm1_w, "norm1", after=[in_token])
    rest = [(a[0] + in_token[0:1, 0:1]).astype(_MXU) for a in (w_out, w_up, w_down)]
    in_shards, got_in = _gather_halves_wait(in_started, [w_in_l, m_w_in_l, v_w_in_l, h1, *rest], "gather_in_wait")
    (g_in, g_conv, g_fconv), (w_in_l, m_w_in_l, v_w_in_l) = _place_own(
        in_shards, _sibling_fill(got_in, "fill_in"), cq, "place_in", carry=[w_in_l, m_w_in_l, v_w_in_l])
    *rest_started, token = _gather_halves_start(rest, g_conv, "gather_rest_start")

    rest_state = {}

    def rest_arrived(after):
        rest_state["shards"], got = _gather_halves_wait(rest_started, after, "gather_rest_wait")
        *rest_state["fill"], tok = _sibling_fill_start(got, "fill_rest_start")
        return tok

    def rest_filled(after):
        got = _sibling_fill_wait(rest_state["fill"], after, "fill_rest_wait")
        g_out, g_up, g_down = _place_own(rest_state["shards"], got, cq, "place_rest")
        return g_out.reshape(D_MODEL, D_MODEL), g_up, g_down.reshape(D_FF, D_MODEL)

    rest_weights = (rest_arrived, rest_filled)
    wp = _wp_assemble(g_in, [token])
    conv_f = jnp.concatenate([g_conv[i] for i in range(N_CHIPS)], axis=1)
    fcw = jnp.concatenate([g_fconv[i] for i in range(N_CHIPS)], axis=1)
    gp = _pad_lanes(jnp.concatenate([a_log, dt_bias], axis=1), 128)
    fnw = final_norm_w[None, :]
    early = {}

    early_names = ("w_up", "w_down", "w_out")

    def early_sibling(d_wup, d_wdown, d_wout):
        *early["sibling"], tok = _grad_sibling_start(
            [d_wup, d_wdown.reshape(N_CHIPS, D_FF // N_CHIPS, D_MODEL), d_wout.reshape(N_CHIPS, D_MODEL // N_CHIPS, D_MODEL)],
            "grad_sibling_early_start")
        return tok

    def early_chips(after):
        fams_e, got_e = _grad_sibling_wait(early["sibling"], after, "grad_sibling_early_wait")
        early["parts"] = [_add_sibling(f, r, cq, "add_sibling_" + nm) for f, r, nm in zip(fams_e, got_e, early_names)]
        *early["started"], tok = _grad_chips_start([p[1] for p in early["parts"]], "grad_chips_start")
        return tok

    def late_sibling(d_wp):
        *early["late_sibling"], tok = _grad_sibling_start([d_wp[None]], "grad_sibling_late_start")
        return tok

    loss_l, dx, g = _local_step(x[0], loss_target[0], h1, norm1_w, norm2_w, fnw, gp, gdn_norm_w, wp,
                                conv_f, fcw, rest_weights, (early_sibling, early_chips, late_sibling))
    fams, got = _grad_sibling_wait(early["late_sibling"], [dx], "grad_sibling_late_wait")
    late_part = _add_sibling_split(fams[0], got[0], cq, "add_sibling_w_in")
    *late_started, late_token = _grad_chips_start([late_part[1]], "grad_chips_late_start")
    small = _pack_small(g["n1w"], g["n2w"], g["fnw"], g["gp"], g["gnw"], g["conv_w"], g["fcw_g"], g["fcw_u"], loss_l)
    *small_started, small_token = _grad_chips_start([], "small_gather_start", small)
    got3_e = _grad_chips_wait(early["started"], [late_token, small_token], "grad_chips_wait")
    g_w_up, g_w_down, g_w_out = _grad_share(
        [_add_chips(p[0], r3, cq, "add_chips_" + nm) for p, r3, nm in zip(early["parts"], got3_e, early_names)],
        "grad_share_early")
    big = {}

    def adamw_big(nm, w, gg, m, v):
        d_, m_, v_ = _adamw(w[0], gg, m[0], v[0], "adamw_" + nm)
        big[nm] = (gg[None], d_[None], m_[None], v_[None])

    adamw_big("w_up", w_up, g_w_up, m_w_up, v_w_up)
    adamw_big("w_down", w_down, g_w_down, m_w_down, v_w_down)
    adamw_big("w_out", w_out, g_w_out, m_w_out, v_w_out)
    got3, = _grad_chips_wait(late_started, [big[nm][1] for nm in early_names], "grad_chips_late_wait")
    g_w_in, = _grad_share([_add_chips(late_part[0], got3, cq, "add_chips_w_in")], "grad_share_late")
    g_t, d_t, m_t, v_t = _adamw_transposed(w_in_l, g_w_in, m_w_in_l, v_w_in_l, "adamw_w_in")
    big["w_in"] = tuple(jnp.swapaxes(t[None], 1, 2) for t in (g_t, d_t, m_t, v_t))
    small_all, small = _grad_chips_wait(small_started, [d_t], "small_gather_wait", with_small=True)
    small_names = ["norm1_w", "norm2_w", "final_norm_w", "a_log", "dt_bias", "gdn_norm_w", "conv_qkv_w", "ffn_conv_w"]
    loss_b, *small_out = _small_step(
        jnp.stack([2 * q + c, q]).astype(jnp.int32), small_all, small,
        [norm1_w, norm2_w, final_norm_w[None], a_log, dt_bias, gdn_norm_w, conv_qkv_w[0], ffn_conv_w[0]],
        [m_norm1_w, m_norm2_w, m_final_norm_w[None], m_a_log, m_dt_bias, m_gdn_norm_w, m_conv_qkv_w[0], m_ffn_conv_w[0]],
        [v_norm1_w, v_norm2_w, v_final_norm_w[None], v_a_log, v_dt_bias, v_gdn_norm_w, v_conv_qkv_w[0], v_ffn_conv_w[0]])
    like = dict(final_norm_w=lambda t: t[0], conv_qkv_w=lambda t: t[None], ffn_conv_w=lambda t: t[None])
    for k, nm in enumerate(small_names):
        big[nm] = tuple(like.get(nm, lambda t: t)(t) for t in small_out[4 * k:4 * k + 4])
    names = ["norm1_w", "w_in", "conv_qkv_w", "a_log", "dt_bias", "gdn_norm_w", "w_out", "norm2_w", "w_up",
             "ffn_conv_w", "w_down", "final_norm_w"]
    return (loss_b[0, 0], dx[None], *[big[n][0] for n in names], *[big[n][1] for n in names],
            *[big[n][2] for n in names], *[big[n][3] for n in names])
```

```python
import functools
import math

import numpy as np
import jax
import jax.numpy as jnp
from jax import lax
from jax.experimental import pallas as pl
from jax.experimental.pallas import tpu as pltpu

F32 = jnp.float32
BF16 = jnp.bfloat16
_MXU = jnp.bfloat16
_HI = lax.Precision.HIGHEST
EPS = 1e-6
V7X_VMEM_LIMIT = 56 * 1024 * 1024
MESH = pl.DeviceIdType.MESH

D_MODEL = 1024
GDN_HEADS, GDN_DIM, GDN_CHUNK, GDN_CONV = 4, 128, 64, 4
GDN_WIDTH = GDN_HEADS * GDN_DIM
DIL_HEADS, DIL_DIM = 8, 64
DIL_WIDTH = DIL_HEADS * DIL_DIM
D_FF, FFN_CONV = 2816, 3
IN_COLS = 3592
P_COLS = 3840
P_Z, P_QKVB, P_BA = 1536, 2048, 3584
ATT_T = 1024
ADAM_LR, ADAM_B1, ADAM_B2, ADAM_EPS, ADAM_WD, ADAM_STEP = 0.001, 0.9, 0.999, 1e-08, 0.01, 10
N_CHIPS = 4


def _cparams(sem=None, vmem=None):
    kw = {}
    if sem is not None:
        kw["dimension_semantics"] = sem
    if vmem is not None:
        kw["vmem_limit_bytes"] = vmem
    return pltpu.CompilerParams(**kw)


def _silu(x):
    return x * jax.nn.sigmoid(x)


def _pick_tile(n, cap):
    best = None
    for t in range(128, min(n, cap) + 1, 128):
        if n % t == 0:
            best = t
    return best or n


def _mm(a, b, mode, *, out_dtype=F32, residual=None, name, b_blocks=False, place=None, into=None, tn=None, after=(),
        normed_by=None):
    if mode == "nn":
        M, K = a.shape
        N = b.shape[0] * b.shape[2] if b_blocks else b.shape[1]
    elif mode == "nt":
        (M, K), (N, _) = a.shape, b.shape
    else:
        (K, M), (_, N) = a.shape, b.shape
    tm = _pick_tile(M, 1024)
    tn = b.shape[2] if b_blocks else (tn or _pick_tile(N, 1536))

    def vmem(tm, tn):
        return 2 * (tm * K * a.dtype.itemsize + tn * K * b.dtype.itemsize
                    + tm * tn * (jnp.dtype(out_dtype).itemsize + (4 if residual is not None else 0))) + 3 * tm * tn * 4

    fixed_tn = b_blocks or (place is not None and place[0] == "blocks")
    while vmem(tm, tn) > 40 * 1024 * 1024:
        if (tm >= tn or fixed_tn) and tm % 256 == 0:
            tm //= 2
        elif tn % 256 == 0 and not fixed_tn:
            tn //= 2
        else:
            tm //= 2
    a_spec = pl.BlockSpec((K, tm), lambda j, i: (0, i)) if mode == "tn" else pl.BlockSpec((tm, K), lambda j, i: (i, 0))
    if b_blocks:
        b_spec = pl.BlockSpec((None, K, tn), lambda j, i: (j, 0, 0))
    else:
        b_spec = pl.BlockSpec((tn, K), lambda j, i: (j, 0)) if mode == "nt" else pl.BlockSpec((K, tn), lambda j, i: (0, j))
    r_spec = pl.BlockSpec((tm, tn), lambda j, i: (i, j))
    if place is None:
        o_spec, o_shape = r_spec, (M, N)
    elif place[0] == "rows":
        off = place[2] // tm
        o_spec, o_shape = pl.BlockSpec((tm, tn), lambda j, i: (i + off, j)), (place[1], N)
    else:
        off = place[2]
        o_spec, o_shape = pl.BlockSpec((None, tm, tn), lambda j, i: (j + off, i, 0)), (place[1], M, tn)
    dims = {"nn": (((1,), (0,)), ((), ())), "nt": (((1,), (1,)), ((), ())), "tn": (((0,), (0,)), ((), ()))}[mode]

    def body(*refs):
        a_ref, b_ref = refs[0], refs[1]
        o_ref = refs[-1] if normed_by is None else refs[-2]
        acc = lax.dot_general(a_ref[...].astype(_MXU), b_ref[...].astype(_MXU), dims, preferred_element_type=F32)
        if residual is not None:
            acc = acc + refs[2][...]
        o_ref[...] = acc.astype(out_dtype)
        if normed_by is not None:
            rs = lax.rsqrt(jnp.mean(acc * acc, axis=-1, keepdims=True) + EPS)
            refs[-1][...] = (acc * rs * refs[len(ins0)][...]).astype(_MXU)

    ins, specs, alias = [a, b], [a_spec, b_spec], {}
    if residual is not None:
        ins.append(residual)
        specs.append(r_spec)
    ins0 = list(ins)
    if normed_by is not None:
        assert tn == N and place is None and into is None
        ins.append(normed_by)
        specs.append(pl.BlockSpec((1, N), lambda j, i: (0, 0)))
    if into is not None:
        alias = {len(ins): 0}
        ins.append(into)
        specs.append(pl.BlockSpec(memory_space=pl.ANY))
    ins += list(after)
    specs += [pl.BlockSpec(memory_space=pl.ANY)] * len(after)
    o_shape = jax.ShapeDtypeStruct(o_shape, out_dtype)
    if normed_by is not None:
        o_spec, o_shape = (o_spec, r_spec), (o_shape, jax.ShapeDtypeStruct((M, N), _MXU))
    return pl.pallas_call(
        body, name=name, grid=(N // tn, M // tm), in_specs=specs, out_specs=o_spec,
        out_shape=o_shape, input_output_aliases=alias,
        compiler_params=_cparams(("parallel", "parallel"), V7X_VMEM_LIMIT),
    )(*ins)


def _wp_assemble(g_in, after=()):
    nb, Dm, Wb = g_in.shape
    T = 256
    n_lo = P_QKVB - 2 * Wb

    def body(g_ref, *rest):
        g2 = g_ref[2]
        rest[-1][...] = jnp.concatenate(
            [g_ref[0], g_ref[1], g2[:, :n_lo], g2[:, n_lo + 8:], g_ref[3], g2[:, n_lo:n_lo + 8],
             jnp.zeros((T, P_COLS - P_BA - 8), g_in.dtype)], axis=1)

    return pl.pallas_call(
        body, name="wp_assemble", grid=(Dm // T,),
        in_specs=[pl.BlockSpec((nb, T, Wb), lambda i: (0, i, 0))] + [pl.BlockSpec(memory_space=pl.ANY)] * len(after),
        out_specs=pl.BlockSpec((T, P_COLS), lambda i: (i, 0)), out_shape=jax.ShapeDtypeStruct((Dm, P_COLS), g_in.dtype),
        compiler_params=_cparams(("parallel",)),
    )(g_in, *after)


def _rmsnorm_fwd(x, w, name, after=()):
    S, D = x.shape
    T = _pick_tile(S, 512)

    def body(x_ref, w_ref, *rest):
        xv = x_ref[...]
        rs = lax.rsqrt(jnp.mean(xv * xv, axis=-1, keepdims=True) + EPS)
        rest[-1][...] = (xv * rs * w_ref[...]).astype(rest[-1].dtype)

    return pl.pallas_call(
        body, name=name, grid=(S // T,),
        in_specs=[pl.BlockSpec((T, D), lambda i: (i, 0)), pl.BlockSpec((1, D), lambda i: (0, 0))] + [_ANY] * len(after),
        out_specs=pl.BlockSpec((T, D), lambda i: (i, 0)),
        out_shape=jax.ShapeDtypeStruct((S, D), _MXU),
        compiler_params=_cparams(("parallel",)),
    )(x, w, *after)


def _rmsnorm_bwd(dh, x, w, dres, name, after=()):
    S, D = x.shape
    pair = isinstance(dh, tuple)
    T = _pick_tile(S, 256 if pair else 512)
    dhs = [*dh[0], dh[1]] if pair else [dh]

    def body(*refs):
        x_ref, w_ref, dres_ref = refs[len(dhs):len(dhs) + 3]
        dx_ref, dw_ref = refs[-2:]
        xv = x_ref[...]
        rs = lax.rsqrt(jnp.mean(xv * xv, axis=-1, keepdims=True) + EPS)
        xn = xv * rs
        if pair:
            b_ref = refs[len(dhs) - 1]
            nb, _, Kb = b_ref.shape
            per = nb // (len(dhs) - 1)
            dhv = None
            for blk in range(nb):
                lo = (blk % per) * Kb
                t = lax.dot_general(refs[blk // per][:, lo:lo + Kb].astype(_MXU), b_ref[blk].astype(_MXU), (_NT, ((), ())),
                                    preferred_element_type=F32)
                dhv = t if dhv is None else dhv + t
        else:
            dhv = refs[0][...]
        dxn = dhv * w_ref[...]
        dx_ref[...] = dres_ref[...] + rs * (dxn - xn * jnp.mean(dxn * xn, axis=-1, keepdims=True))

        @pl.when(pl.program_id(0) == 0)
        def _():
            dw_ref[...] = jnp.zeros_like(dw_ref)

        dw_ref[...] += jnp.sum(dhv * xn, axis=0, keepdims=True)

    row = pl.BlockSpec((T, D), lambda i: (i, 0))
    vec = pl.BlockSpec((1, D), lambda i: (0, 0))
    dh_specs = [row] if not pair else (
        [pl.BlockSpec((T, a.shape[1]), lambda i: (i, 0)) for a in dh[0]] + [pl.BlockSpec(dh[1].shape, lambda i: (0, 0, 0))])
    return pl.pallas_call(
        body, name=name, grid=(S // T,), in_specs=dh_specs + [row, vec, row] + [_ANY] * len(after), out_specs=(row, vec),
        out_shape=(jax.ShapeDtypeStruct((S, D), F32), jax.ShapeDtypeStruct((1, D), F32)),
        compiler_params=_cparams(("arbitrary",), V7X_VMEM_LIMIT if pair else None),
    )(*dhs, x, w, dres, *after)


def _loss_head(x3, w, tgt, name):
    S, D = tgt.shape
    fused = isinstance(x3, tuple)
    T = _pick_tile(S, 256 if fused else 512)
    xs = list(x3) if fused else [x3]

    def body(*refs):
        w_ref, t_ref = refs[len(xs):len(xs) + 2]
        loss_ref, dx_ref, dxn_ref, dw_ref = refs[-4:]
        xv = refs[0][...]
        if fused:
            xv = lax.dot_general(xv.astype(_MXU), refs[1][...].astype(_MXU), (_NN, ((), ())),
                                 preferred_element_type=F32) + refs[2][...]
        rs = lax.rsqrt(jnp.mean(xv * xv, axis=-1, keepdims=True) + EPS)
        xn = xv * rs
        err = xn * w_ref[...] - t_ref[...]
        dy = err * (1.0 / D)
        dxn = dy * w_ref[...]
        dxv = rs * (dxn - xn * jnp.mean(dxn * xn, axis=-1, keepdims=True))
        dx_ref[...] = dxv
        dxn_ref[...] = dxv.astype(dxn_ref.dtype)

        @pl.when(pl.program_id(0) == 0)
        def _():
            dw_ref[...] = jnp.zeros_like(dw_ref)
            loss_ref[...] = jnp.zeros_like(loss_ref)

        dw_ref[...] += jnp.sum(dy * xn, axis=0, keepdims=True)
        part = jnp.sum(jnp.sum(err * err, axis=-1, keepdims=True), axis=0, keepdims=True) * (0.5 / D)
        loss_ref[...] += jnp.broadcast_to(part, loss_ref.shape)

    row = pl.BlockSpec((T, D), lambda i: (i, 0))
    vec = pl.BlockSpec((1, D), lambda i: (0, 0))
    x_specs = [row] if not fused else [pl.BlockSpec((T, xs[0].shape[1]), lambda i: (i, 0)),
                                       pl.BlockSpec(xs[1].shape, lambda i: (0, 0)), row]
    return pl.pallas_call(
        body, name=name, grid=(S // T,), in_specs=x_specs + [vec, row],
        out_specs=(pl.BlockSpec((8, 128), lambda i: (0, 0)), row, row, vec),
        out_shape=(jax.ShapeDtypeStruct((8, 128), F32), jax.ShapeDtypeStruct((S, D), F32), jax.ShapeDtypeStruct((S, D), _MXU),
                   jax.ShapeDtypeStruct((1, D), F32)),
        compiler_params=_cparams(("arbitrary",), V7X_VMEM_LIMIT if fused else None),
    )(*xs, w, tgt)


def _shifted(ext, back, lo, n):
    if back == 0:
        return ext[lo:lo + n, :]
    return pltpu.roll(ext, back % ext.shape[0], 0)[lo:lo + n, :]


def _conv_windows(ext, K, T):
    return [_shifted(ext, (K - 1) - i, 8, T) for i in range(K)]


def _conv_taps(ext, w, K, T):
    out = None
    for i, win in enumerate(_conv_windows(ext, K, T)):
        term = win * w[i:i + 1, :]
        out = term if out is None else out + term
    return out


def _conv_taps_t(ext, w, K, T):
    out = None
    for i in range(K):
        term = _shifted(ext, i - (K - 1), 0, T) * w[i:i + 1, :]
        out = term if out is None else out + term
    return out


def _tri_masks(C):
    r = lax.broadcasted_iota(jnp.int32, (C, C), 0)
    c = lax.broadcasted_iota(jnp.int32, (C, C), 1)
    return r == c, r >= c, r > c, r <= c


_NN, _NT, _TN = ((1,), (0,)), ((1,), (1,)), ((0,), (0,))
_GDN_PASSES = dict(qk=1, inv=1, sol=1, scan=1, bwd=1)


def _bdot_raw(a, b, kind, passes):
    dims = ({"NN": ((2,), (1,)), "NT": ((2,), (2,)), "TN": ((1,), (1,))}[kind], ((0,), (0,)))
    if passes == 0:
        return lax.dot_general(a, b, dims, precision=_HI, preferred_element_type=F32)
    ah, bh = a.astype(BF16), b.astype(BF16)
    out = lax.dot_general(ah, bh, dims, preferred_element_type=F32)
    if passes == 3:
        al, bl = (a - ah.astype(F32)).astype(BF16), (b - bh.astype(F32)).astype(BF16)
        out = out + lax.dot_general(ah, bl, dims, preferred_element_type=F32) + lax.dot_general(al, bh, dims, preferred_element_type=F32)
    return out


@functools.partial(jax.custom_vjp, nondiff_argnums=(2, 3))
def _bdot(a, b, kind, passes):
    return _bdot_raw(a, b, kind, passes)


def _bdot_fwd(a, b, kind, passes):
    return _bdot_raw(a, b, kind, passes), (a, b)


def _bdot_bwd(kind, passes, res, ct):
    a, b = res
    if kind == "NN":
        return _bdot_raw(ct, b, "NT", passes), _bdot_raw(a, ct, "TN", passes)
    if kind == "NT":
        return _bdot_raw(ct, b, "NN", passes), _bdot_raw(ct, a, "TN", passes)
    return _bdot_raw(b, ct, "NT", passes), _bdot_raw(a, ct, "NN", passes)


_bdot.defvjp(_bdot_fwd, _bdot_bwd)


def _softplus(x):
    return jnp.maximum(x, 0.0) + jnp.log(1.0 + jnp.exp(-jnp.abs(x)))


def _gdn_stage1(cq, ck, cv, b_col, a_col, alog, dtb, dot=_bdot_raw):
    C = cq.shape[1]
    eye, incl, strict, incl_t = _tri_masks(C)
    qn = cq * lax.rsqrt(jnp.sum(cq * cq, axis=-1, keepdims=True) + EPS) * (GDN_DIM ** -0.5)
    kn = ck * lax.rsqrt(jnp.sum(ck * ck, axis=-1, keepdims=True) + EPS)
    beta = jax.nn.sigmoid(b_col)
    g = -jnp.exp(alog) * _softplus(a_col + dtb)
    g_row = jnp.sum(jnp.where(eye, g, 0.0), axis=1, keepdims=True)
    beta_row = jnp.sum(jnp.where(eye, beta, 0.0), axis=1, keepdims=True)
    gc_col = jnp.sum(jnp.where(incl, g_row, 0.0), axis=2, keepdims=True)
    gc_row = jnp.sum(jnp.where(incl_t, g, 0.0), axis=1, keepdims=True)
    dec = jnp.where(incl, jnp.exp(jnp.where(incl, gc_col - gc_row, 0.0)), 0.0)
    kk = dot(kn, kn, "NT", _GDN_PASSES["qk"])
    qk = dot(qn, kn, "NT", _GDN_PASSES["qk"])
    lmat = jnp.where(strict, dec * kk * beta_row, 0.0)
    attn = dec * qk * beta_row
    gam = jnp.exp(gc_col)
    gc_last = gc_col[:, C - 1:C, :]
    k_end = kn * (jnp.exp(gc_last - gc_col) * beta)
    return lmat, cv, gam * kn, gam * qn, attn, k_end, jnp.exp(gc_last)


def _tri_inv(lmat):
    C = lmat.shape[1]
    eye = _tri_masks(C)[0]
    ps = _GDN_PASSES["inv"]
    p = jnp.where(eye, 1.0, 0.0) - lmat
    lp = _bdot_raw(lmat, lmat, "NN", ps)
    n = int(math.log2(C))
    for s in range(1, n):
        p = p + _bdot_raw(p, lp, "NN", ps)
        if s < n - 1:
            lp = _bdot_raw(lp, lp, "NN", ps)
    return p


def _gated_norm(o, z, gnw):
    on = o * lax.rsqrt(jnp.mean(o * o, axis=-1, keepdims=True) + EPS) * gnw
    return on * _silu(z)


GDN_PG = 4
GDN_SG = 4


def _gdn_pairs(c, ba, gp, G):
    C, W, H = GDN_CHUNK, GDN_WIDTH, GDN_HEADS
    pairs = [(j, h) for j in range(G) for h in range(H)]
    cq, ck, cv = (jnp.stack([c[C * j:C * (j + 1), o + GDN_DIM * h:o + GDN_DIM * (h + 1)] for j, h in pairs]) for o in (0, W, 2 * W))
    b_col = jnp.stack([ba[C * j:C * (j + 1), h:h + 1] for j, h in pairs])
    a_col = jnp.stack([ba[C * j:C * (j + 1), H + h:H + h + 1] for j, h in pairs])
    alog = jnp.stack([gp[0:1, h:h + 1] for j, h in pairs])
    dtb = jnp.stack([gp[0:1, H + h:H + h + 1] for j, h in pairs])
    return pairs, (cq, ck, cv, b_col, a_col, alog, dtb)


def _gdn_pre_specs(S, G):
    C = GDN_CHUNK
    T = C * G
    return dict(
        cur=pl.BlockSpec((T, 3 * GDN_WIDTH), lambda i: (i, 0)),
        prev=pl.BlockSpec((8, 3 * GDN_WIDTH), lambda i: (jnp.maximum(i * (T // 8) - 1, 0), 0)),
        ba=pl.BlockSpec((T, 128), lambda i: (i, P_BA // 128)),
        cw=pl.BlockSpec((GDN_CONV, 3 * GDN_WIDTH), lambda i: (0, 0)),
        vec=pl.BlockSpec((1, 128), lambda i: (0, 0)),
        hd=pl.BlockSpec((GDN_HEADS, T, GDN_DIM), lambda i: (0, i, 0)),
        hc=pl.BlockSpec((GDN_HEADS, T, C), lambda i: (0, i, 0)),
        ge=pl.BlockSpec((G, GDN_HEADS, 8, 128), lambda i: (i, 0, 0, 0)),
    )


def _hd_shape(S, last=GDN_DIM):
    return jax.ShapeDtypeStruct((GDN_HEADS, S, last), F32)


def _gdn_pre(proj, conv_w, gp):
    S = proj.shape[0]
    C, G = GDN_CHUNK, GDN_PG
    nc = S // C
    sp = _gdn_pre_specs(S, G)

    def body(cur_ref, prev_ref, ba_ref, cw_ref, gp_ref, uv_ref, wk_ref, qd_ref, ke_ref, at_ref, ti_ref, ge_ref):
        prev = prev_ref[...] * jnp.where(pl.program_id(0) == 0, 0.0, 1.0)
        c = _silu(_conv_taps(jnp.concatenate([prev, cur_ref[...]], axis=0), cw_ref[...], GDN_CONV, C * G))
        pairs, args = _gdn_pairs(c, ba_ref[...], gp_ref[...], G)
        lmat, v, rk, q_dec, attn, k_end, g_end = _gdn_stage1(*args)
        t = _tri_inv(lmat)
        u_v = _bdot_raw(t, v, "NN", _GDN_PASSES["sol"])
        w_k = _bdot_raw(t, rk, "NN", _GDN_PASSES["sol"])
        for b, (j, h) in enumerate(pairs):
            rows = slice(C * j, C * (j + 1))
            uv_ref[h, rows, :] = u_v[b]
            wk_ref[h, rows, :] = w_k[b]
            qd_ref[h, rows, :] = q_dec[b]
            ke_ref[h, rows, :] = k_end[b]
            at_ref[h, rows, :] = attn[b]
            ti_ref[h, rows, :] = t[b]
            ge_ref[j, h] = jnp.broadcast_to(g_end[b], (8, 128))

    return pl.pallas_call(
        body, name="gdn_pre", grid=(nc // G,),
        in_specs=[sp["cur"], sp["prev"], sp["ba"], sp["cw"], sp["vec"]],
        out_specs=(sp["hd"], sp["hd"], sp["hd"], sp["hd"], sp["hc"], sp["hc"], sp["ge"]),
        out_shape=(_hd_shape(S), _hd_shape(S), _hd_shape(S), _hd_shape(S), _hd_shape(S, C), _hd_shape(S, C),
                   jax.ShapeDtypeStruct((nc, GDN_HEADS, 8, 128), F32)),
        compiler_params=_cparams(("parallel",)),
    )(proj, proj, proj, conv_w, gp)


def _gdn_scan_specs(S, G, rev):
    C = GDN_CHUNK
    T = C * G
    n = S // T
    ci = (lambda i: n - 1 - i) if rev else (lambda i: i)
    return dict(
        hd=pl.BlockSpec((GDN_HEADS, T, GDN_DIM), lambda i: (0, ci(i), 0)),
        hc=pl.BlockSpec((GDN_HEADS, T, C), lambda i: (0, ci(i), 0)),
        ge=pl.BlockSpec((G, GDN_HEADS, 8, 128), lambda i: (ci(i), 0, 0, 0)),
        z=pl.BlockSpec((T, GDN_WIDTH), lambda i: (ci(i), P_Z // GDN_WIDTH)),
        oa=pl.BlockSpec((T, GDN_WIDTH), lambda i: (ci(i), 0)),
        vec=pl.BlockSpec((1, 128), lambda i: (0, 0)),
        st=pl.BlockSpec((G, GDN_HEADS, GDN_DIM, GDN_DIM), lambda i: (ci(i), 0, 0, 0)),
    )


def _gdn_scan(u_v, w_k, q_dec, k_end, attn, g_end, proj, gnw, mix, after=()):
    S = proj.shape[0]
    C, G = GDN_CHUNK, GDN_SG
    nc = S // C
    sp = _gdn_scan_specs(S, G, False)
    ps = _GDN_PASSES["scan"]

    def body(uv_ref, wk_ref, qd_ref, ke_ref, at_ref, ge_ref, z_ref, gnw_ref, *rest):
        oa_ref, st_ref, s_scr = rest[-3:]

        @pl.when(pl.program_id(0) == 0)
        def _():
            s_scr[...] = jnp.zeros_like(s_scr)

        for j in range(G):
            rows = slice(C * j, C * (j + 1))
            st = s_scr[...]
            st_ref[j] = st
            u = uv_ref[:, rows, :] - _bdot_raw(wk_ref[:, rows, :], st, "NN", ps)
            o = _bdot_raw(qd_ref[:, rows, :], st, "NN", ps) + _bdot_raw(at_ref[:, rows, :], u, "NN", ps)
            s_scr[...] = ge_ref[j][:, 0:1, 0:1] * st + _bdot_raw(ke_ref[:, rows, :], u, "TN", ps)
            for h in range(GDN_HEADS):
                cols = slice(GDN_DIM * h, GDN_DIM * (h + 1))
                oa_ref[rows, cols] = _gated_norm(o[h], z_ref[rows, cols], gnw_ref[...])

    return pl.pallas_call(
        body, name="gdn_scan", grid=(nc // G,),
        in_specs=[sp["hd"], sp["hd"], sp["hd"], sp["hd"], sp["hc"], sp["ge"], sp["z"], sp["vec"]] + [_ANY] * (1 + len(after)),
        out_specs=(sp["oa"], sp["st"]),
        out_shape=(jax.ShapeDtypeStruct(mix.shape, F32),
                   jax.ShapeDtypeStruct((nc, GDN_HEADS, GDN_DIM, GDN_DIM), F32)),
        input_output_aliases={8: 0},
        scratch_shapes=[pltpu.VMEM((GDN_HEADS, GDN_DIM, GDN_DIM), F32)],
        compiler_params=_cparams(("arbitrary",)),
    )(u_v, w_k, q_dec, k_end, attn, g_end, proj, gnw, mix, *after)


def _gdn_scan_bwd(u_v, w_k, q_dec, k_end, attn, g_end, proj, gnw, states, d_oa):
    S = proj.shape[0]
    C, G = GDN_CHUNK, GDN_SG
    nc = S // C
    sp = _gdn_scan_specs(S, G, True)
    ps, pb = _GDN_PASSES["scan"], _GDN_PASSES["bwd"]

    def body(uv_ref, wk_ref, qd_ref, ke_ref, at_ref, ge_ref, z_ref, gnw_ref, st_ref, doa_ref,
             duv_ref, dwk_ref, dqd_ref, dke_ref, dat_ref, dge_ref, dz_ref, dgnw_ref, ds_scr):
        @pl.when(pl.program_id(0) == 0)
        def _():
            ds_scr[...] = jnp.zeros_like(ds_scr)
            dgnw_ref[...] = jnp.zeros_like(dgnw_ref)

        dgnw = jnp.zeros((1, 128), F32)
        for j in reversed(range(G)):
            rows = slice(C * j, C * (j + 1))
            st = st_ref[j]
            wk, qd, ke, at = wk_ref[:, rows, :], qd_ref[:, rows, :], ke_ref[:, rows, :], at_ref[:, rows, :]
            u = uv_ref[:, rows, :] - _bdot_raw(wk, st, "NN", ps)
            o = _bdot_raw(qd, st, "NN", ps) + _bdot_raw(at, u, "NN", ps)
            dos = []
            for h in range(GDN_HEADS):
                cols = slice(GDN_DIM * h, GDN_DIM * (h + 1))
                _, vjp2 = jax.vjp(_gated_norm, o[h], z_ref[rows, cols], gnw_ref[...])
                do_h, dz_h, dgn = vjp2(doa_ref[rows, cols])
                dz_ref[rows, cols] = dz_h
                dgnw = dgnw + dgn
                dos.append(do_h)
            do = jnp.stack(dos)
            ds_new = ds_scr[...]
            du = _bdot_raw(at, do, "TN", pb) + _bdot_raw(ke, ds_new, "NN", pb)
            duv_ref[:, rows, :] = du
            dat_ref[:, rows, :] = _bdot_raw(do, u, "NT", pb)
            dqd_ref[:, rows, :] = _bdot_raw(do, st, "NT", pb)
            dke_ref[:, rows, :] = _bdot_raw(u, ds_new, "NT", pb)
            dwk_ref[:, rows, :] = -_bdot_raw(du, st, "NT", pb)
            d_ge = jnp.sum(jnp.sum(st * ds_new, axis=2, keepdims=True), axis=1, keepdims=True)
            dge_ref[j] = jnp.broadcast_to(d_ge, (GDN_HEADS, 8, 128))
            ds_scr[...] = ge_ref[j][:, 0:1, 0:1] * ds_new + _bdot_raw(qd, do, "TN", pb) - _bdot_raw(wk, du, "TN", pb)
        dgnw_ref[...] += dgnw

    return pl.pallas_call(
        body, name="gdn_scan_bwd", grid=(nc // G,),
        in_specs=[sp["hd"], sp["hd"], sp["hd"], sp["hd"], sp["hc"], sp["ge"], sp["z"], sp["vec"], sp["st"], sp["oa"]],
        out_specs=(sp["hd"], sp["hd"], sp["hd"], sp["hd"], sp["hc"], sp["ge"], sp["oa"], sp["vec"]),
        out_shape=(_hd_shape(S), _hd_shape(S), _hd_shape(S), _hd_shape(S), _hd_shape(S, C),
                   jax.ShapeDtypeStruct((nc, GDN_HEADS, 8, 128), F32), jax.ShapeDtypeStruct((S, GDN_WIDTH), F32),
                   jax.ShapeDtypeStruct((1, 128), F32)),
        scratch_shapes=[pltpu.VMEM((GDN_HEADS, GDN_DIM, GDN_DIM), F32)],
        compiler_params=_cparams(("arbitrary",)),
    )(u_v, w_k, q_dec, k_end, attn, g_end, proj, gnw, states, d_oa)


def _gdn_post(proj, conv_w, gp, tinv, u_v, w_k, d_uv, d_wk, d_qd, d_ke, d_at, d_ge):
    S = proj.shape[0]
    C, G = GDN_CHUNK, GDN_PG
    nc = S // C
    sp = _gdn_pre_specs(S, G)
    pb = _GDN_PASSES["bwd"]

    def body(cur_ref, prev_ref, ba_ref, cw_ref, gp_ref, ti_ref, uv_ref, wk_ref, duv_ref, dwk_ref, dqd_ref, dke_ref,
             dat_ref, dge_ref, dpre_ref, dba_ref, dgp_ref):
        i = pl.program_id(0)

        @pl.when(i == 0)
        def _():
            dgp_ref[...] = jnp.zeros_like(dgp_ref)

        prev = prev_ref[...] * jnp.where(i == 0, 0.0, 1.0)
        pre = _conv_taps(jnp.concatenate([prev, cur_ref[...]], axis=0), cw_ref[...], GDN_CONV, C * G)
        sg = jax.nn.sigmoid(pre)
        dsilu = sg * (1.0 + pre * (1.0 - sg))
        pairs, args = _gdn_pairs(pre * sg, ba_ref[...], gp_ref[...], G)
        _, vjp1 = jax.vjp(functools.partial(_gdn_stage1, dot=_bdot), *args)

        def take(ref):
            return jnp.stack([ref[h, C * j:C * (j + 1), :] for j, h in pairs])

        t, u_v, w_k = take(ti_ref), take(uv_ref), take(wk_ref)
        d_v = _bdot_raw(t, take(duv_ref), "TN", pb)
        d_rk = _bdot_raw(t, take(dwk_ref), "TN", pb)
        d_l = -(_bdot_raw(d_v, u_v, "NT", pb) + _bdot_raw(d_rk, w_k, "NT", pb))
        d_ge = jnp.stack([dge_ref[j, h][0:1, 0:1] for j, h in pairs])
        dcq, dck, dcv, db, da, dalog, ddtb = vjp1((d_l, d_v, d_rk, take(dqd_ref), take(dat_ref), take(dke_ref), d_ge))
        lane = lax.broadcasted_iota(jnp.int32, (C, 128), 1)
        lane1 = lax.broadcasted_iota(jnp.int32, (1, 128), 1)
        dgp = jnp.zeros((1, 128), F32)
        for j in range(G):
            rows = slice(C * j, C * (j + 1))
            dba = jnp.zeros((C, 128), F32)
            for h in range(GDN_HEADS):
                b = GDN_HEADS * j + h
                for o_, dcx in ((0, dcq), (GDN_WIDTH, dck), (2 * GDN_WIDTH, dcv)):
                    cols = slice(o_ + GDN_DIM * h, o_ + GDN_DIM * (h + 1))
                    dpre_ref[rows, cols] = dcx[b] * dsilu[rows, cols]
                dba = dba + jnp.where(lane == h, db[b], 0.0) + jnp.where(lane == GDN_HEADS + h, da[b], 0.0)
                dgp = dgp + jnp.where(lane1 == h, dalog[b], 0.0) + jnp.where(lane1 == GDN_HEADS + h, ddtb[b], 0.0)
            dba_ref[rows, :] = dba
        dgp_ref[0:1, :] += dgp

    T = C * G
    return pl.pallas_call(
        body, name="gdn_post", grid=(nc // G,),
        in_specs=[sp["cur"], sp["prev"], sp["ba"], sp["cw"], sp["vec"], sp["hc"], sp["hd"], sp["hd"], sp["hd"], sp["hd"],
                  sp["hd"], sp["hd"], sp["hc"], sp["ge"]],
        out_specs=(sp["cur"], pl.BlockSpec((T, 128), lambda i: (i, 0)), pl.BlockSpec((8, 128), lambda i: (0, 0))),
        out_shape=(jax.ShapeDtypeStruct((S, 3 * GDN_WIDTH), F32), jax.ShapeDtypeStruct((S, 128), F32),
                   jax.ShapeDtypeStruct((8, 128), F32)),
        compiler_params=_cparams(("arbitrary",)),
    )(proj, proj, proj, conv_w, gp, tinv, u_v, w_k, d_uv, d_wk, d_qd, d_ke, d_at, d_ge)


def _conv_bwd(dpre, x, xcol0, w, K, name, tc):
    S, Cc = dpre.shape
    T = _pick_tile(S, 256)
    nt, ncol = S // T, Cc // tc
    xo = xcol0 // tc

    def body(d_ref, dn_ref, x_ref, xp_ref, w_ref, dx_ref, dw_ref):
        i = pl.program_id(1)
        dn = dn_ref[...] * jnp.where(i == nt - 1, 0.0, 1.0)
        dv = d_ref[...]
        ext_d = jnp.concatenate([dv, dn], axis=0)
        dx_ref[...] = _conv_taps_t(ext_d, w_ref[...], K, T).astype(dx_ref.dtype)
        xp = xp_ref[...] * jnp.where(i == 0, 0.0, 1.0)
        ext_x = jnp.concatenate([xp, x_ref[...]], axis=0)

        @pl.when(i == 0)
        def _():
            dw_ref[...] = jnp.zeros_like(dw_ref)

        for k in range(K):
            dw_ref[k:k + 1, :] += jnp.sum(dv * _shifted(ext_x, (K - 1) - k, 8, T), axis=0, keepdims=True)

    r8 = T // 8
    return pl.pallas_call(
        body, name=name, grid=(ncol, nt),
        in_specs=[pl.BlockSpec((T, tc), lambda j, i: (i, j)),
                  pl.BlockSpec((8, tc), lambda j, i: (jnp.minimum((i + 1) * r8, S // 8 - 1), j)),
                  pl.BlockSpec((T, tc), lambda j, i: (i, j + xo)),
                  pl.BlockSpec((8, tc), lambda j, i: (jnp.maximum(i * r8 - 1, 0), j + xo)),
                  pl.BlockSpec((K, tc), lambda j, i: (0, j))],
        out_specs=(pl.BlockSpec((T, tc), lambda j, i: (i, j)), pl.BlockSpec((K, tc), lambda j, i: (0, j))),
        out_shape=(jax.ShapeDtypeStruct((S, Cc), _MXU), jax.ShapeDtypeStruct((K, Cc), F32)),
        compiler_params=_cparams(("parallel", "arbitrary")),
    )(dpre, dpre, x, x, w)


def _dil_bias(nt, T):
    d = (np.arange(nt)[:, None, None] * T + np.arange(T)[None, None, :] - np.arange(T)[None, :, None])
    cnt = ((d >= 0) & (d <= 128)).astype(np.float64) + ((d >= 0) & (d % 4 == 0) & (d <= 512)) + ((d >= 0) & (d % 16 == 0))
    return jnp.asarray(np.where(cnt > 0, np.log(np.maximum(cnt, 1.0)), -1e30), dtype=F32)


def _attn_fwd(proj, after=()):
    S = proj.shape[0]
    T = min(ATT_T, S)
    nt, H = S // T, T // 2
    bias = _dil_bias(nt, T)
    scale = DIL_DIM ** -0.5
    npair = DIL_WIDTH // 128
    qb0, kb0, vb0 = P_QKVB // 128, (P_QKVB + DIL_WIDTH) // 128, (P_QKVB + 2 * DIL_WIDTH) // 128

    def body(q_ref, k_ref, v_ref, b_ref, *rest):
        o_ref, lse_ref = rest[-2:]
        i = pl.program_id(1)
        qs = (q_ref[...] * scale).astype(_MXU)

        def update(carry, kt, vt, qt, bt):
            out = []
            for hh in range(2):
                m, l, acc = carry[hh]
                sl = slice(hh * DIL_DIM, (hh + 1) * DIL_DIM)
                s = lax.dot_general(kt[:, sl], qt[:, sl], (_NT, ((), ())), preferred_element_type=F32) + bt
                m_new = jnp.maximum(m, jnp.max(s, axis=0, keepdims=True))
                p = jnp.exp(s - m_new)
                a = jnp.exp(m - m_new)
                l = a * l + jnp.sum(p, axis=0, keepdims=True)
                acc = a * acc + lax.dot_general(vt[:, sl], p.astype(_MXU), (_TN, ((), ())), preferred_element_type=F32)
                out.append((m_new, l, acc))
            return tuple(out)

        def keys(j):
            rows = pl.ds(pl.multiple_of(j * T, T), T)
            return k_ref[rows, :].astype(_MXU), v_ref[rows, :].astype(_MXU)

        init = tuple((jnp.full((1, T), -1e30, F32), jnp.zeros((1, T), F32), jnp.zeros((DIL_DIM, T), F32)) for _ in range(2))
        res = lax.fori_loop(0, i, lambda j, carry: update(carry, *keys(j), qs, b_ref[i - j]), init)
        kd, vd = keys(i)
        res = update(res, kd[:H], vd[:H], qs, b_ref[0, :H, :])
        late = update(tuple(tuple(t[:, H:] for t in r) for r in res), kd[H:], vd[H:], qs[H:], b_ref[0, H:, H:])
        res = tuple(tuple(jnp.concatenate([t[:, :H], u], axis=1) for t, u in zip(r, r2)) for r, r2 in zip(res, late))
        lse_ref[...] = jnp.zeros_like(lse_ref)
        for hh in range(2):
            m, l, acc = res[hh]
            o_ref[:, hh * DIL_DIM:(hh + 1) * DIL_DIM] = (acc / l).T
            lse_ref[hh:hh + 1, :] = m + jnp.log(l)

    return pl.pallas_call(
        body, name="attn_fwd", grid=(npair, nt),
        in_specs=[pl.BlockSpec((T, 128), lambda p, i: (i, qb0 + p)),
                  pl.BlockSpec((S, 128), lambda p, i: (0, kb0 + p)),
                  pl.BlockSpec((S, 128), lambda p, i: (0, vb0 + p)),
                  pl.BlockSpec((nt, T, T), lambda p, i: (0, 0, 0))] + [_ANY] * len(after),
        out_specs=(pl.BlockSpec((T, 128), lambda p, i: (i, GDN_WIDTH // 128 + p)),
                   pl.BlockSpec((None, None, 8, T), lambda p, i: (p, i, 0, 0))),
        out_shape=(jax.ShapeDtypeStruct((S, GDN_WIDTH + DIL_WIDTH), F32), jax.ShapeDtypeStruct((npair, nt, 8, T), F32)),
        compiler_params=_cparams(("parallel", "parallel")),
    )(proj, proj, proj, bias, *after)


def _attn_bwd(proj, mix, lse, d_mix):
    S = proj.shape[0]
    T = min(ATT_T, S)
    nt, H = S // T, T // 2
    bias = _dil_bias(nt, T)
    scale = DIL_DIM ** -0.5
    npair = DIL_WIDTH // 128
    qb0, kb0, vb0 = P_QKVB // 128, (P_QKVB + DIL_WIDTH) // 128, (P_QKVB + 2 * DIL_WIDTH) // 128

    def body(q_ref, k_ref, v_ref, o_ref, lse_ref, do_ref, b_ref, dq_ref, dk_ref, dv_ref, dq_scr):
        j = pl.program_id(1)

        @pl.when(j == 0)
        def _():
            dq_scr[...] = jnp.zeros_like(dq_scr)

        kt = k_ref[...].astype(_MXU)
        vt = v_ref[...].astype(_MXU)
        ones = jnp.ones((8, DIL_DIM), F32)

        def block(carry, kt, vt, rows, lsev, bt):
            qs = (q_ref[rows, :] * scale).astype(_MXU)
            dov = do_ref[rows, :]
            prod = dov * o_ref[rows, :]
            dob = dov.astype(_MXU)
            out = []
            dqs = []
            for hh in range(2):
                dk, dv = carry[hh]
                sl = slice(hh * DIL_DIM, (hh + 1) * DIL_DIM)
                s = lax.dot_general(kt[:, sl], qs[:, sl], (_NT, ((), ())), preferred_element_type=F32) + bt
                p = jnp.exp(s - lsev[hh:hh + 1, :])
                delta = lax.dot_general(ones, prod[:, sl], (_NT, ((), ())), precision=_HI, preferred_element_type=F32)[0:1, :]
                dp = lax.dot_general(vt[:, sl], dob[:, sl], (_NT, ((), ())), preferred_element_type=F32)
                ds = (p * (dp - delta)).astype(_MXU)
                dv = dv + lax.dot_general(p.astype(_MXU), dob[:, sl], (_NN, ((), ())), preferred_element_type=F32)
                dk = dk + lax.dot_general(ds, qs[:, sl], (_NN, ((), ())), preferred_element_type=F32)
                dqs.append(lax.dot_general(ds, kt[:, sl], (_TN, ((), ())), preferred_element_type=F32) * scale)
                out.append((dk, dv))
            dq_scr[rows, :] += jnp.concatenate(dqs, axis=1)
            return tuple(out)

        def step(i, carry):
            return block(carry, kt, vt, pl.ds(pl.multiple_of(i * T, T), T), lse_ref[i], b_ref[i - j])

        zeros = tuple((jnp.zeros((H, DIL_DIM), F32), jnp.zeros((H, DIL_DIM), F32)) for _ in range(2))
        lsed = lse_ref[j]
        early = block(zeros, kt[:H], vt[:H], pl.ds(pl.multiple_of(j * T, T), T), lsed, b_ref[0, :H, :])
        late = block(zeros, kt[H:], vt[H:], pl.ds(pl.multiple_of(j * T + H, H), H), lsed[:, H:], b_ref[0, H:, H:])
        init = tuple(tuple(jnp.concatenate([t, u], axis=0) for t, u in zip(r, r2)) for r, r2 in zip(early, late))
        res = lax.fori_loop(j + 1, nt, step, init)
        dk_ref[...] = jnp.concatenate([res[0][0], res[1][0]], axis=1).astype(dk_ref.dtype)
        dv_ref[...] = jnp.concatenate([res[0][1], res[1][1]], axis=1).astype(dv_ref.dtype)

        @pl.when(j == nt - 1)
        def _():
            dq_ref[...] = dq_scr[...].astype(dq_ref.dtype)

    full = lambda c0: pl.BlockSpec((S, 128), lambda p, j: (0, c0 + p))
    tile = lambda c0: pl.BlockSpec((T, 128), lambda p, j: (j, c0 + p))
    out3 = jax.ShapeDtypeStruct((S, DIL_WIDTH), _MXU)
    return pl.pallas_call(
        body, name="attn_bwd", grid=(npair, nt),
        in_specs=[full(qb0), tile(kb0), tile(vb0), full(GDN_WIDTH // 128),
                  pl.BlockSpec((None, nt, 8, T), lambda p, j: (p, 0, 0, 0)), full(GDN_WIDTH // 128),
                  pl.BlockSpec((nt, T, T), lambda p, j: (0, 0, 0))],
        out_specs=(full(0), tile(0), tile(0)),
        out_shape=(out3, out3, out3),
        scratch_shapes=[pltpu.VMEM((S, 128), F32)],
        compiler_params=_cparams(("parallel", "arbitrary")),
    )(proj, proj, proj, mix, lse, d_mix, bias)


def _ffn_act(up, cw):
    S, Cc = up.shape[0], up.shape[1] // 2
    T, tc = _pick_tile(S, 256), _pick_tile(Cc, 1536)
    r16 = T // 16
    nct = Cc // tc

    def body(g_ref, gp_ref, u_ref, up_ref, wg_ref, wu_ref, o_ref):
        keep = jnp.where(pl.program_id(1) == 0, 0.0, 1.0)
        cg = _conv_taps(jnp.concatenate([gp_ref[8:16, :].astype(F32) * keep, g_ref[...].astype(F32)], axis=0),
                        wg_ref[...], FFN_CONV, T)
        cu = _conv_taps(jnp.concatenate([up_ref[8:16, :].astype(F32) * keep, u_ref[...].astype(F32)], axis=0),
                        wu_ref[...], FFN_CONV, T)
        o_ref[...] = (_silu(cg) * cu).astype(o_ref.dtype)

    cur = lambda o: pl.BlockSpec((T, tc), lambda j, i: (i, j + o))
    prev = lambda o: pl.BlockSpec((16, tc), lambda j, i: (jnp.maximum(i * r16 - 1, 0), j + o))
    wsp = lambda o: pl.BlockSpec((FFN_CONV, tc), lambda j, i: (0, j + o))
    return pl.pallas_call(
        body, name="ffn_act", grid=(nct, S // T),
        in_specs=[cur(0), prev(0), cur(nct), prev(nct), wsp(0), wsp(nct)], out_specs=cur(0),
        out_shape=jax.ShapeDtypeStruct((S, Cc), _MXU),
        compiler_params=_cparams(("parallel", "parallel")),
    )(up, up, up, up, cw, cw)


def _ffn_act_bwd(d_act, up, cw):
    S, Cc = up.shape[0], up.shape[1] // 2
    T, tc = _pick_tile(S, 256), _pick_tile(Cc, 1536)
    r8, r16 = T // 8, T // 16
    nt = S // T
    nct = Cc // tc
    K = FFN_CONV

    def body(da_ref, dan_ref, g_ref, gp_ref, gn_ref, u_ref, up_ref, un_ref, wg_ref, wu_ref,
             dg_ref, du_ref, dwg_ref, dwu_ref):
        i = pl.program_id(1)
        keep_p = jnp.where(i == 0, 0.0, 1.0)
        keep_n = jnp.where(i == nt - 1, 0.0, 1.0)
        wg, wu = wg_ref[...], wu_ref[...]
        xg = jnp.concatenate([gp_ref[8:16, :].astype(F32) * keep_p, g_ref[...].astype(F32),
                              gn_ref[0:8, :].astype(F32) * keep_n], axis=0)
        xu = jnp.concatenate([up_ref[8:16, :].astype(F32) * keep_p, u_ref[...].astype(F32),
                              un_ref[0:8, :].astype(F32) * keep_n], axis=0)
        cg = _conv_taps(xg, wg, K, T + 8)
        cu = _conv_taps(xu, wu, K, T + 8)
        da = jnp.concatenate([da_ref[...], dan_ref[...] * keep_n], axis=0)
        sg = jax.nn.sigmoid(cg)
        d_cg = da * cu * (sg * (1.0 + cg * (1.0 - sg)))
        d_cu = da * (cg * sg)
        dg_ref[...] = _conv_taps_t(d_cg, wg, K, T).astype(dg_ref.dtype)
        du_ref[...] = _conv_taps_t(d_cu, wu, K, T).astype(du_ref.dtype)

        @pl.when(i == 0)
        def _():
            dwg_ref[...] = jnp.zeros_like(dwg_ref)
            dwu_ref[...] = jnp.zeros_like(dwu_ref)

        for k in range(K):
            dwg_ref[k:k + 1, :] += jnp.sum(d_cg[0:T, :] * _shifted(xg, (K - 1) - k, 8, T), axis=0, keepdims=True)
            dwu_ref[k:k + 1, :] += jnp.sum(d_cu[0:T, :] * _shifted(xu, (K - 1) - k, 8, T), axis=0, keepdims=True)

    cur = lambda o: pl.BlockSpec((T, tc), lambda j, i: (i, j + o))
    prev = lambda o: pl.BlockSpec((16, tc), lambda j, i: (jnp.maximum(i * r16 - 1, 0), j + o))
    nxt = lambda o: pl.BlockSpec((16, tc), lambda j, i: (jnp.minimum((i + 1) * r16, S // 16 - 1), j + o))
    nxt8 = pl.BlockSpec((8, tc), lambda j, i: (jnp.minimum((i + 1) * r8, S // 8 - 1), j))
    wsp = lambda o: pl.BlockSpec((K, tc), lambda j, i: (0, j + o))
    return pl.pallas_call(
        body, name="ffn_act_bwd", grid=(nct, nt),
        in_specs=[cur(0), nxt8, cur(0), prev(0), nxt(0), cur(nct), prev(nct), nxt(nct), wsp(0), wsp(nct)],
        out_specs=(cur(0), cur(0), wsp(0), wsp(0)),
        out_shape=(jax.ShapeDtypeStruct((S, Cc), _MXU), jax.ShapeDtypeStruct((S, Cc), _MXU),
                   jax.ShapeDtypeStruct((K, Cc), F32), jax.ShapeDtypeStruct((K, Cc), F32)),
        compiler_params=_cparams(("parallel", "arbitrary")),
    )(d_act, d_act, up, up, up, up, up, up, cw, cw)


def _local_step(x, tgt, h1, n1w, n2w, fnw, gp, gnw, wp, conv_w, fcw, rest_weights, early_grads):
    proj = _mm(h1, wp, "nn", name="proj")
    u_v, w_k, q_dec, k_end, attn, tinv, g_end = _gdn_pre(proj, conv_w, gp)
    mix, lse = _attn_fwd(proj)
    mix, states = _gdn_scan(u_v, w_k, q_dec, k_end, attn, g_end, proj, gnw, mix, after=[rest_weights[0]([mix])])
    w_out, w_up4, w_down = rest_weights[1]([mix])
    x2, h2 = _mm(mix, w_out, "nn", residual=x, name="outproj", normed_by=n2w)
    up = _mm(h2, w_up4, "nn", b_blocks=True, out_dtype=_MXU, name="up")
    act = _ffn_act(up, fcw)
    loss, dx3, dx3n, d_fnw = _loss_head((act, w_down, x2), fnw, tgt, "loss_head")
    d_act = _mm(dx3n, w_down, "nt", name="d_act")
    d_wdown = _mm(act, dx3n, "tn", name="d_wdown")
    d_upg, d_upu, d_fcwg, d_fcwu = _ffn_act_bwd(d_act, up, fcw)
    d_wup = _mm(h2, d_upg, "tn", place=("blocks", N_CHIPS, 0), tn=w_up4.shape[2], name="d_wgate")
    d_wup = _mm(h2, d_upu, "tn", place=("blocks", N_CHIPS, N_CHIPS // 2), tn=w_up4.shape[2], into=d_wup, name="d_wup")
    dx2, d_n2w = _rmsnorm_bwd(([d_upg, d_upu], w_up4), x2, n2w, dx3, "norm2_bwd")
    d_wout = _mm(mix, dx2, "tn", name="d_wout")
    token = early_grads[0](d_wup, d_wdown, d_wout)
    d_mix = _mm(dx2, w_out, "nt", name="d_mix", after=[token])
    dq_b, dk_b, dv_b = _attn_bwd(proj, mix, lse, d_mix)
    d_uv, d_wk, d_qd, d_ke, d_at, d_ge, d_z, d_gnw = _gdn_scan_bwd(u_v, w_k, q_dec, k_end, attn, g_end, proj,
                                                                   gnw, states, d_mix)
    d_pre, d_ba, d_gp = _gdn_post(proj, conv_w, gp, tinv, u_v, w_k, d_uv, d_wk, d_qd, d_ke, d_at, d_ge)
    token = early_grads[1]([d_pre])
    d_qkva, d_convw = _conv_bwd(d_pre, proj, 0, conv_w + token[0:1, 0:1], GDN_CONV, "gdn_conv_bwd", 512)
    d_proj = jnp.concatenate([d_qkva, d_z.astype(_MXU), dq_b, dk_b, dv_b, d_ba.astype(_MXU),
                              jnp.zeros((x.shape[0], P_COLS - P_BA - 128), _MXU)], axis=1)
    d_wp = _mm(h1, d_proj, "tn", name="d_wp")
    token = early_grads[2](d_wp)
    dx, d_n1w = _rmsnorm_bwd(([d_proj], wp[None]), x, n1w, dx2, "norm1_bwd", after=[token])
    grads = dict(wp=d_wp, conv_w=d_convw, w_out=d_wout, w_up=d_wup, fcw_g=d_fcwg, fcw_u=d_fcwu, w_down=d_wdown,
                 n1w=d_n1w, n2w=d_n2w, fnw=d_fnw, gp=d_gp, gnw=d_gnw)
    return loss, dx, grads


_HBM = pl.BlockSpec(memory_space=pltpu.HBM)


def _pos():
    return lax.axis_index("x"), lax.axis_index("y"), lax.axis_index("c")


def _other_chips(x, y):
    return [(1 - x, y), (x, 1 - y), (1 - x, 1 - y)]


def _halvable(shape):
    return shape[0] % 32 == 0


def _rows_of_half(shape, half):
    if not _halvable(shape):
        return pl.ds(0, shape[0])
    return pl.ds(pl.multiple_of(half * (shape[0] // 2), 16), shape[0] // 2)


_SEM = pl.BlockSpec(memory_space=pltpu.SEMAPHORE)
_ANY = pl.BlockSpec(memory_space=pl.ANY)
_DATAFLOW = pltpu.SideEffectType.DATAFLOW_SIDE_EFFECTING


def _in_hbm(a):
    return pltpu.with_memory_space_constraint(a, pltpu.HBM)


def _halves_copy(src_refs, land_refs, send_sems, recv_sems, shapes, a, j, block, x, y, c):
    px, py = _other_chips(x, y)[j]
    rows = _rows_of_half(shapes[a], c)
    return pltpu.make_async_remote_copy(
        src_ref=src_refs[a].at[rows, :], dst_ref=land_refs[a].at[block, rows, :], send_sem=send_sems.at[3 * a + j],
        recv_sem=recv_sems.at[3 * a + j], device_id=(px, py, c), device_id_type=MESH)


def _gather_halves_start(shards, after, name):
    n = len(shards)
    shapes = [s.shape for s in shards]

    def body(*refs):
        ins, lands = refs[:n], refs[n:2 * n]
        send_sems, recv_sems = refs[2 * n + 1], refs[2 * n + 2]
        token = refs[-1]
        x, y, c = _pos()
        q = 2 * x + y
        for a in range(n):
            for j in range(3):
                _halves_copy(ins, lands, send_sems, recv_sems, shapes, a, j, q, x, y, c).start()
        token[...] = jnp.zeros_like(token)

    land_shapes = [(N_CHIPS,) + s.shape for s in shards]
    return pl.pallas_call(
        body, name=name,
        out_shape=(pltpu.SemaphoreType.DMA((3 * n,)), pltpu.SemaphoreType.DMA((3 * n,)),
                   *[pltpu.HBM(s.shape, s.dtype) for s in shards],
                   *[pltpu.HBM(ls, s.dtype) for ls, s in zip(land_shapes, shards)],
                   jax.ShapeDtypeStruct((8, 128), F32)),
        in_specs=[_HBM] * (2 * n) + [_ANY],
        out_specs=(_SEM, _SEM, *[_HBM] * (2 * n), pl.BlockSpec(memory_space=pltpu.VMEM)),
        input_output_aliases={a: 2 + a for a in range(2 * n)},
        compiler_params=pltpu.CompilerParams(has_side_effects=_DATAFLOW),
    )(*[_in_hbm(s) for s in shards], *[_in_hbm(lax.empty(ls, s.dtype)) for ls, s in zip(land_shapes, shards)], after)


def _gather_halves_wait(started, after, name):
    send_sems, recv_sems, *thru = started
    n = len(thru) // 2
    shapes = [t.shape for t in thru[:n]]

    def body(*refs):
        ins, lands = refs[:n], refs[n:2 * n]
        send_sems, recv_sems = refs[2 * n], refs[2 * n + 1]
        x, y, c = _pos()
        q = 2 * x + y
        chips = _other_chips(x, y)
        for a in range(n):
            for j, (px, py) in enumerate(chips):
                _halves_copy(ins, lands, send_sems, recv_sems, shapes, a, j, q, x, y, c).wait_send()
                _halves_copy(ins, lands, send_sems, recv_sems, shapes, a, j, 2 * px + py, x, y, c).wait_recv()

    outs = pl.pallas_call(
        body, name=name, out_shape=[pltpu.HBM(t.shape, t.dtype) for t in thru],
        in_specs=[_HBM] * (2 * n) + [_SEM, _SEM] + [_ANY] * len(after), out_specs=[_HBM] * (2 * n),
        input_output_aliases={a: a for a in range(2 * n)},
        compiler_params=pltpu.CompilerParams(has_side_effects=_DATAFLOW),
    )(*thru, send_sems, recv_sems, *after)
    return outs[:n], outs[n:]


def _sibling_fill(gathered, name):
    big = [a for a, g in enumerate(gathered) if _halvable(g.shape[1:])]
    n = len(gathered)

    def body(*refs):
        ins, outs = refs[:n], refs[n:2 * n]
        send_sems, recv_sems = refs[2 * n:]
        x, y, c = _pos()
        chips = _other_chips(x, y)

        def copy(k, j, half):
            a = big[k]
            px, py = chips[j]
            rows = _rows_of_half(gathered[a].shape[1:], half)
            return pltpu.make_async_remote_copy(
                src_ref=ins[a].at[2 * px + py, rows, :], dst_ref=outs[a].at[2 * px + py, rows, :],
                send_sem=send_sems.at[3 * k + j], recv_sem=recv_sems.at[3 * k + j],
                device_id=(x, y, 1 - c), device_id_type=MESH)

        sends = [copy(k, j, c) for k in range(len(big)) for j in range(3)]
        for cp in sends:
            cp.start()
        for k in range(len(big)):
            for j in range(3):
                copy(k, j, 1 - c).wait_recv()
        for cp in sends:
            cp.wait_send()

    return pl.pallas_call(
        body, name=name, in_specs=[_HBM] * n, out_specs=[_HBM] * n,
        out_shape=[jax.ShapeDtypeStruct(g.shape, g.dtype) for g in gathered],
        input_output_aliases={a: a for a in range(n)},
        scratch_shapes=[pltpu.SemaphoreType.DMA((3 * len(big),)), pltpu.SemaphoreType.DMA((3 * len(big),))],
    )(*gathered)


def _fill_copy(refs, send_sems, recv_sems, shapes, a, j, half, x, y, c):
    px, py = _other_chips(x, y)[j]
    rows = _rows_of_half(shapes[a], half)
    return pltpu.make_async_remote_copy(
        src_ref=refs[a].at[2 * px + py, rows, :], dst_ref=refs[a].at[2 * px + py, rows, :],
        send_sem=send_sems.at[3 * a + j], recv_sem=recv_sems.at[3 * a + j],
        device_id=(x, y, 1 - c), device_id_type=MESH)


def _sibling_fill_start(gathered, name):
    n = len(gathered)
    shapes = [g.shape[1:] for g in gathered]

    def body(*refs):
        ins = refs[:n]
        send_sems, recv_sems = refs[n], refs[n + 1]
        token = refs[-1]
        x, y, c = _pos()
        for a in range(n):
            for j in range(3):
                _fill_copy(ins, send_sems, recv_sems, shapes, a, j, c, x, y, c).start()
        token[...] = jnp.zeros_like(token)

    return pl.pallas_call(
        body, name=name,
        out_shape=(pltpu.SemaphoreType.DMA((3 * n,)), pltpu.SemaphoreType.DMA((3 * n,)),
                   *[pltpu.HBM(g.shape, g.dtype) for g in gathered], jax.ShapeDtypeStruct((8, 128), F32)),
        in_specs=[_HBM] * n,
        out_specs=(_SEM, _SEM, *[_HBM] * n, pl.BlockSpec(memory_space=pltpu.VMEM)),
        input_output_aliases={a: 2 + a for a in range(n)},
        compiler_params=pltpu.CompilerParams(has_side_effects=_DATAFLOW),
    )(*[_in_hbm(g) for g in gathered])


def _sibling_fill_wait(started, after, name):
    send_sems, recv_sems, *thru = started
    n = len(thru)
    shapes = [t.shape[1:] for t in thru]

    def body(*refs):
        ins = refs[:n]
        send_sems, recv_sems = refs[n], refs[n + 1]
        x, y, c = _pos()
        for a in range(n):
            for j in range(3):
                _fill_copy(ins, send_sems, recv_sems, shapes, a, j, c, x, y, c).wait_send()
                _fill_copy(ins, send_sems, recv_sems, shapes, a, j, 1 - c, x, y, c).wait_recv()

    return pl.pallas_call(
        body, name=name, out_shape=[pltpu.HBM(t.shape, t.dtype) for t in thru],
        in_specs=[_HBM] * n + [_SEM, _SEM] + [_ANY] * len(after), out_specs=[_HBM] * n,
        input_output_aliases={a: a for a in range(n)},
        compiler_params=pltpu.CompilerParams(has_side_effects=_DATAFLOW),
    )(*thru, send_sems, recv_sems, *after)


def _place_own(shards, gathered, cq, name, carry=()):
    n = len(shards)
    nc = len(carry)
    steps = 4

    def body(cq_ref, *refs):
        for a in range(n):
            refs[2 * n + nc + a][...] = refs[a][...]

    def tile(shape):
        return shape[0] // steps if _halvable(shape) else shape[0]

    in_specs = [pl.BlockSpec((tile(s.shape), s.shape[1]), (lambda i, s_: (i, 0)) if _halvable(s.shape) else (lambda i, s_: (0, 0)))
                for s in shards]
    in_specs += [pl.BlockSpec(memory_space=pl.ANY)] * (n + nc)
    out_specs = [pl.BlockSpec((None, tile(s.shape), s.shape[1]),
                              (lambda i, s_: (s_[1], i, 0)) if _halvable(s.shape) else (lambda i, s_: (s_[1], 0, 0)))
                 for s in shards]
    out_specs += [pl.BlockSpec(memory_space=pl.ANY)] * nc
    gs = pltpu.PrefetchScalarGridSpec(num_scalar_prefetch=1, grid=(steps,), in_specs=in_specs, out_specs=out_specs)
    outs = pl.pallas_call(
        body, name=name, grid_spec=gs,
        out_shape=[jax.ShapeDtypeStruct(g.shape, g.dtype) for g in gathered] + [jax.ShapeDtypeStruct(t.shape, t.dtype) for t in carry],
        input_output_aliases={1 + n + a: a for a in range(n + nc)},
        compiler_params=_cparams(("arbitrary",)),
    )(cq, *shards, *gathered, *carry)
    return (outs[:n], outs[n:]) if nc else outs


def _half_rows(ref, c, rh):
    return ref.at[:, pl.ds(pl.multiple_of(c * rh, 8), rh), :]


def _chips_copy(src_refs, land_refs, send_sems, recv_sems, a, j, x, y, c):
    px, py = _other_chips(x, y)[j]
    return pltpu.make_async_remote_copy(src_ref=src_refs[a].at[2 * px + py], dst_ref=land_refs[a].at[j],
                                        send_sem=send_sems.at[3 * a + j], recv_sem=recv_sems.at[3 * a + j],
                                        device_id=(px, py, c), device_id_type=MESH)


def _peer(r, x, y, c):
    return (x if r & 4 == 0 else 1 - x), (y if r & 2 == 0 else 1 - y), (c if r & 1 == 0 else 1 - c)


def _small_copy(small_ref, all_ref, send_sems, recv_sems, base, r, slot, x, y, c):
    return pltpu.make_async_remote_copy(src_ref=small_ref, dst_ref=all_ref.at[slot], send_sem=send_sems.at[base + r - 1],
                                        recv_sem=recv_sems.at[base + r - 1], device_id=_peer(r, x, y, c), device_id_type=MESH)


def _grad_chips_start(parts, name, small=None):
    n = len(parts)
    srcs = list(parts) + ([] if small is None else [small])
    m = len(srcs)

    def body(*refs):
        ins, lands = refs[:m], refs[m:2 * m]
        send_sems, recv_sems = refs[2 * m], refs[2 * m + 1]
        token = refs[-1]
        x, y, c = _pos()
        for a in range(n):
            for j in range(3):
                _chips_copy(ins, lands, send_sems, recv_sems, a, j, x, y, c).start()
        if small is not None:
            for r in range(1, 8):
                _small_copy(ins[n], lands[n], send_sems, recv_sems, 3 * n, r, 4 * x + 2 * y + c, x, y, c).start()
        token[...] = jnp.zeros_like(token)

    land_shapes = [(3,) + p.shape[1:] for p in parts] + ([] if small is None else [(8,) + small.shape])
    nsem = 3 * n + (0 if small is None else 7)
    return pl.pallas_call(
        body, name=name,
        out_shape=(pltpu.SemaphoreType.DMA((nsem,)), pltpu.SemaphoreType.DMA((nsem,)),
                   *[pltpu.HBM(p.shape, p.dtype) for p in srcs],
                   *[pltpu.HBM(ls, p.dtype) for ls, p in zip(land_shapes, srcs)],
                   jax.ShapeDtypeStruct((8, 128), F32)),
        in_specs=[_HBM] * (2 * m),
        out_specs=(_SEM, _SEM, *[_HBM] * (2 * m), pl.BlockSpec(memory_space=pltpu.VMEM)),
        input_output_aliases={a: 2 + a for a in range(2 * m)},
        compiler_params=pltpu.CompilerParams(has_side_effects=_DATAFLOW),
    )(*[_in_hbm(p) for p in srcs], *[_in_hbm(lax.empty(ls, p.dtype)) for ls, p in zip(land_shapes, srcs)])


def _grad_chips_wait(started, after, name, with_small=False):
    send_sems, recv_sems, *thru = started
    m = len(thru) // 2
    n = m - (1 if with_small else 0)

    def body(*refs):
        ins, lands = refs[:m], refs[m:2 * m]
        send_sems, recv_sems = refs[2 * m], refs[2 * m + 1]
        x, y, c = _pos()
        for a in range(n):
            for j in range(3):
                cp = _chips_copy(ins, lands, send_sems, recv_sems, a, j, x, y, c)
                cp.wait_send()
                cp.wait_recv()
        if with_small:
            for r in range(1, 8):
                px, py, pc = _peer(r, x, y, c)
                _small_copy(ins[n], lands[n], send_sems, recv_sems, 3 * n, r, 4 * x + 2 * y + c, x, y, c).wait_send()
                _small_copy(ins[n], lands[n], send_sems, recv_sems, 3 * n, r, 4 * px + 2 * py + pc, x, y, c).wait_recv()

    outs = pl.pallas_call(
        body, name=name, out_shape=[pltpu.HBM(t.shape, t.dtype) for t in thru],
        in_specs=[_HBM] * (2 * m) + [_SEM, _SEM] + [_ANY] * len(after), out_specs=[_HBM] * (2 * m),
        input_output_aliases={a: a for a in range(2 * m)},
        compiler_params=pltpu.CompilerParams(has_side_effects=_DATAFLOW),
    )(*thru, send_sems, recv_sems, *after)
    return list(outs[m:]) + list(outs[n:m])


def _sibling_copy(src_refs, land_refs, send_sems, recv_sems, rhs, a, c, x, y):
    return pltpu.make_async_remote_copy(src_ref=_half_rows(src_refs[a], 1 - c, rhs[a]), dst_ref=land_refs[a],
                                        send_sem=send_sems.at[a], recv_sem=recv_sems.at[a],
                                        device_id=(x, y, 1 - c), device_id_type=MESH)


def _grad_sibling_start(fams, name):
    n = len(fams)
    rhs = [f.shape[1] // 2 for f in fams]

    def body(*refs):
        ins, lands = refs[:n], refs[n:2 * n]
        send_sems, recv_sems = refs[2 * n], refs[2 * n + 1]
        token = refs[-1]
        x, y, c = _pos()
        for a in range(n):
            _sibling_copy(ins, lands, send_sems, recv_sems, rhs, a, c, x, y).start()
        token[...] = jnp.zeros_like(token)

    land_shapes = [(f.shape[0], f.shape[1] // 2, f.shape[2]) for f in fams]
    return pl.pallas_call(
        body, name=name,
        out_shape=(pltpu.SemaphoreType.DMA((n,)), pltpu.SemaphoreType.DMA((n,)),
                   *[pltpu.HBM(f.shape, f.dtype) for f in fams],
                   *[pltpu.HBM(ls, f.dtype) for ls, f in zip(land_shapes, fams)],
                   jax.ShapeDtypeStruct((8, 128), F32)),
        in_specs=[_HBM] * (2 * n),
        out_specs=(_SEM, _SEM, *[_HBM] * (2 * n), pl.BlockSpec(memory_space=pltpu.VMEM)),
        input_output_aliases={a: 2 + a for a in range(2 * n)},
        compiler_params=pltpu.CompilerParams(has_side_effects=_DATAFLOW),
    )(*[_in_hbm(f) for f in fams], *[_in_hbm(lax.empty(ls, f.dtype)) for ls, f in zip(land_shapes, fams)])


def _grad_sibling_wait(started, after, name):
    send_sems, recv_sems, *thru = started
    n = len(thru) // 2
    rhs = [t.shape[1] // 2 for t in thru[:n]]

    def body(*refs):
        ins, lands = refs[:n], refs[n:2 * n]
        send_sems, recv_sems = refs[2 * n], refs[2 * n + 1]
        x, y, c = _pos()
        for a in range(n):
            cp = _sibling_copy(ins, lands, send_sems, recv_sems, rhs, a, c, x, y)
            cp.wait_send()
            cp.wait_recv()

    outs = pl.pallas_call(
        body, name=name, out_shape=[pltpu.HBM(t.shape, t.dtype) for t in thru],
        in_specs=[_HBM] * (2 * n) + [_SEM, _SEM] + [_ANY] * len(after), out_specs=[_HBM] * (2 * n),
        input_output_aliases={a: a for a in range(2 * n)},
        compiler_params=pltpu.CompilerParams(has_side_effects=_DATAFLOW),
    )(*thru, send_sems, recv_sems, *after)
    return outs[:n], outs[n:]


def _grad_share(fulls, name):
    n = len(fulls)
    rhs = [f.shape[0] // 2 for f in fulls]

    def body(*refs):
        ins, outs = refs[:n], refs[n:2 * n]
        send_sems, recv_sems = refs[2 * n], refs[2 * n + 1]
        x, y, c = _pos()

        def copy(a, half):
            rows = pl.ds(pl.multiple_of(half * rhs[a], 8), rhs[a])
            return pltpu.make_async_remote_copy(src_ref=ins[a].at[rows, :], dst_ref=outs[a].at[rows, :],
                                                send_sem=send_sems.at[a], recv_sem=recv_sems.at[a],
                                                device_id=(x, y, 1 - c), device_id_type=MESH)

        sends = [copy(a, c) for a in range(n)]
        for cp in sends:
            cp.start()
        for a in range(n):
            copy(a, 1 - c).wait_recv()
        for cp in sends:
            cp.wait_send()

    return pl.pallas_call(
        body, name=name, in_specs=[_HBM] * n, out_specs=[_HBM] * n,
        out_shape=[jax.ShapeDtypeStruct(f.shape, f.dtype) for f in fulls],
        input_output_aliases={a: a for a in range(n)},
        scratch_shapes=[pltpu.SemaphoreType.DMA((n,)), pltpu.SemaphoreType.DMA((n,))],
    )(*fulls)


def _add_sibling(own, recv, cq, name):
    nb, R, Cc = own.shape
    Rh = R // 2

    def body(cq_ref, a_ref, b_ref, o32_ref, o16_ref):
        s = a_ref[0] + b_ref[0]
        mine = pl.program_id(0) == cq_ref[1]

        @pl.when(mine)
        def _():
            o32_ref[...] = s

        @pl.when(jnp.logical_not(mine))
        def _():
            o16_ref[0] = s.astype(o16_ref.dtype)

    sp = pl.BlockSpec((1, Rh, Cc), lambda b, s: (b, 0, 0))
    gs = pltpu.PrefetchScalarGridSpec(
        num_scalar_prefetch=1, grid=(nb,),
        in_specs=[pl.BlockSpec((1, Rh, Cc), lambda b, s: (b, s[0], 0)), sp],
        out_specs=[pl.BlockSpec((Rh, Cc), lambda b, s: (0, 0)), sp])
    return pl.pallas_call(
        body, name=name, grid_spec=gs,
        out_shape=[jax.ShapeDtypeStruct((Rh, Cc), F32), jax.ShapeDtypeStruct((nb, Rh, Cc), _MXU)],
        compiler_params=_cparams(("arbitrary",)),
    )(cq, own, recv)


def _add_sibling_split(d_wp, recv, cq, name):
    _, Dm, Pc = d_wp.shape
    Rh = Dm // 2
    Wb = IN_COLS // N_CHIPS
    T = 256

    def body(cq_ref, a_ref, b_ref, o32_ref, o16_ref):
        s = a_ref[0] + b_ref[0]
        blocks = [s[:, 0:Wb], s[:, Wb:2 * Wb],
                  jnp.concatenate([s[:, 2 * Wb:P_QKVB], s[:, P_BA:P_BA + 8], s[:, P_QKVB:3 * Wb - 8]], axis=1),
                  s[:, 3 * Wb - 8:P_BA]]
        q = cq_ref[1]
        own = None
        for j, blk in enumerate(blocks):
            term = jnp.where(q == j, blk, 0.0)
            own = term if own is None else own + term
            o16_ref[j] = blk.astype(o16_ref.dtype)
        o32_ref[...] = own

    gs = pltpu.PrefetchScalarGridSpec(
        num_scalar_prefetch=1, grid=(Rh // T,),
        in_specs=[pl.BlockSpec((1, T, Pc), lambda i, s: (0, s[0] * (Rh // T) + i, 0)), pl.BlockSpec((1, T, Pc), lambda i, s: (0, i, 0))],
        out_specs=[pl.BlockSpec((T, Wb), lambda i, s: (i, 0)), pl.BlockSpec((N_CHIPS, T, Wb), lambda i, s: (0, i, 0))])
    return pl.pallas_call(
        body, name=name, grid_spec=gs,
        out_shape=[jax.ShapeDtypeStruct((Rh, Wb), F32), jax.ShapeDtypeStruct((N_CHIPS, Rh, Wb), _MXU)],
        compiler_params=_cparams(("parallel",)),
    )(cq, d_wp, recv)


def _add_chips(part32, recv3, cq, name):
    Rh, Cc = part32.shape

    def body(cq_ref, a_ref, b_ref, o_ref):
        acc = a_ref[...]
        for j in range(3):
            acc = acc + b_ref[j].astype(F32)
        o_ref[...] = acc

    gs = pltpu.PrefetchScalarGridSpec(
        num_scalar_prefetch=1, grid=(1,),
        in_specs=[pl.BlockSpec((Rh, Cc), lambda i, s: (0, 0)), pl.BlockSpec((3, Rh, Cc), lambda i, s: (0, 0, 0))],
        out_specs=pl.BlockSpec((Rh, Cc), lambda i, s: (s[0], 0)))
    return pl.pallas_call(
        body, name=name, grid_spec=gs, out_shape=jax.ShapeDtypeStruct((2 * Rh, Cc), F32),
        compiler_params=_cparams(("arbitrary",)),
    )(cq, part32, recv3)


def _narrowed_shard(w):
    _, Dm, n = w.shape
    nj = Dm // 128
    pad = -n % 128

    def body(x_ref, o_ref):
        for j in range(nj):
            cols = jnp.concatenate([x_ref[pl.ds(j, n, stride=nj), :], jnp.zeros((pad, 128), F32)], axis=0)
            o_ref[128 * j:128 * (j + 1), :] = cols.T[:, :n].astype(_MXU)

    return pl.pallas_call(body, name="narrowed_shard", out_shape=jax.ShapeDtypeStruct((Dm, n), _MXU),
                          compiler_params=_cparams(vmem=V7X_VMEM_LIMIT))(jnp.swapaxes(w, 1, 2).reshape(n * nj, 128))


def _adamw_transposed(w, g, m, v, name):
    n, Dm = w.shape
    T = 128

    def body(w_ref, g_ref, m_ref, v_ref, gt_ref, d_ref, mo_ref, vo_ref):
        gt = g_ref[...].T
        gt_ref[...] = gt
        d_ref[...], mo_ref[...], vo_ref[...] = _adamw_math(w_ref[...], gt, m_ref[...], v_ref[...])

    row = pl.BlockSpec((T, Dm), lambda i: (i, 0))
    sh = jax.ShapeDtypeStruct((n, Dm), F32)
    return pl.pallas_call(
        body, name=name, grid=(pl.cdiv(n, T),), in_specs=[row, pl.BlockSpec((Dm, T), lambda i: (0, i)), row, row],
        out_specs=(row,) * 4, out_shape=(sh,) * 4, compiler_params=_cparams(("parallel",)),
    )(w, g, m, v)


def _adamw(w, g, m, v, name):
    R, Cc = w.shape
    T = max([t for t in range(8, 257, 8) if R % t == 0], default=R)

    def body(w_ref, g_ref, m_ref, v_ref, d_ref, mo_ref, vo_ref):
        d_ref[...], mo_ref[...], vo_ref[...] = _adamw_math(w_ref[...], g_ref[...], m_ref[...], v_ref[...])

    sp = pl.BlockSpec((T, Cc), lambda i: (i, 0))
    sh = jax.ShapeDtypeStruct((R, Cc), F32)
    return pl.pallas_call(
        body, name=name, grid=(R // T,), in_specs=[sp] * 4, out_specs=(sp, sp, sp), out_shape=(sh, sh, sh),
        compiler_params=_cparams(("parallel",)),
    )(w, g, m, v)


SMALL_ROWS = 32
ROW_CONV, ROW_FCG, ROW_FCU = 5, 13, 22


def _adamw_math(w, g, m, v):
    mn = ADAM_B1 * m + (1.0 - ADAM_B1) * g
    vn = ADAM_B2 * v + (1.0 - ADAM_B2) * (g * g)
    c1 = 1.0 / (1.0 - ADAM_B1 ** ADAM_STEP)
    c2 = 1.0 / (1.0 - ADAM_B2 ** ADAM_STEP)
    return -ADAM_LR * ((mn * c1) / (jnp.sqrt(vn * c2) + ADAM_EPS) + ADAM_WD * w), mn, vn


def _pack_small(n1, n2, fn, gp, gn, conv, fcg, fcu, loss):
    W = D_MODEL

    def body(n1_ref, n2_ref, fn_ref, gp_ref, gn_ref, conv_ref, fcg_ref, fcu_ref, loss_ref, o_ref):
        o_ref[...] = jnp.zeros_like(o_ref)
        o_ref[0:1, :] = n1_ref[...]
        o_ref[1:2, :] = n2_ref[...]
        o_ref[2:3, :] = fn_ref[...]
        o_ref[3:4, 0:8] = gp_ref[0:1, 0:8]
        o_ref[3:4, 8:9] = loss_ref[0:1, 0:1]
        o_ref[4:5, 0:128] = gn_ref[...]
        for i in range(GDN_CONV):
            o_ref[ROW_CONV + 2 * i:ROW_CONV + 2 * i + 1, :] = conv_ref[i:i + 1, 0:W]
            o_ref[ROW_CONV + 2 * i + 1:ROW_CONV + 2 * i + 2, 0:3 * GDN_WIDTH - W] = conv_ref[i:i + 1, W:3 * GDN_WIDTH]
        for r0, ref in ((ROW_FCG, fcg_ref), (ROW_FCU, fcu_ref)):
            for i in range(FFN_CONV):
                for k in range(3):
                    n = min(W, D_FF - k * W)
                    o_ref[r0 + 3 * i + k:r0 + 3 * i + k + 1, 0:n] = ref[i:i + 1, k * W:k * W + n]

    return pl.pallas_call(body, name="pack_small", out_shape=jax.ShapeDtypeStruct((SMALL_ROWS, W), F32))(
        n1, n2, fn, gp, gn, conv, fcg, fcu, loss)


def _small_step(meq, small_all, small, ws, ms, vs):
    W = D_MODEL
    n = len(ws)
    cw, fw = ws[6].shape[1], ws[7].shape[1]

    def body(meq_ref, all_ref, own_ref, *refs):
        w_refs, m_refs, v_refs = refs[:n], refs[n:2 * n], refs[2 * n:3 * n]
        loss_ref = refs[3 * n]
        outs = refs[3 * n + 1:]
        me, q = meq_ref[0], meq_ref[1]
        red = None
        for d in range(8):
            term = jnp.where(me == d, own_ref[...], all_ref[d])
            red = term if red is None else red + term
        loss_ref[...] = jnp.broadcast_to(red[3:4, 8:9], loss_ref.shape)
        conv = [jnp.concatenate([red[ROW_CONV + 2 * i:ROW_CONV + 2 * i + 1, :],
                                 red[ROW_CONV + 2 * i + 1:ROW_CONV + 2 * i + 2, 0:3 * GDN_WIDTH - W]], axis=1)
                for i in range(GDN_CONV)]
        conv = jnp.concatenate(conv, axis=0)

        def fc_rows(r0):
            rows = [jnp.concatenate([red[r0 + 3 * i + k:r0 + 3 * i + k + 1, 0:min(W, D_FF - k * W)] for k in range(3)], axis=1)
                    for i in range(FFN_CONV)]
            return jnp.concatenate(rows, axis=0)

        fc = jnp.concatenate([fc_rows(ROW_FCG), fc_rows(ROW_FCU)], axis=1)

        def chip_block(full, width):
            out = None
            for j in range(N_CHIPS):
                term = jnp.where(q == j, full[:, width * j:width * (j + 1)], 0.0)
                out = term if out is None else out + term
            return out

        grads = [red[0:1, :], red[1:2, :], red[2:3, :], red[3:4, 0:4], red[3:4, 4:8], red[4:5, 0:128],
                 chip_block(conv, cw), chip_block(fc, fw)]
        for k in range(n):
            d_, m_, v_ = _adamw_math(w_refs[k][...], grads[k], m_refs[k][...], v_refs[k][...])
            outs[4 * k][...] = grads[k]
            outs[4 * k + 1][...] = d_
            outs[4 * k + 2][...] = m_
            outs[4 * k + 3][...] = v_

    full = lambda a: pl.BlockSpec(a.shape, lambda i, s_, nd=len(a.shape): (0,) * nd)
    arrays = [small_all, small, *ws, *ms, *vs]
    out_shapes = [jax.ShapeDtypeStruct((8, 128), F32)] + [jax.ShapeDtypeStruct(w.shape, F32) for w in ws for _ in range(4)]
    gs = pltpu.PrefetchScalarGridSpec(
        num_scalar_prefetch=1, grid=(1,), in_specs=[full(a) for a in arrays],
        out_specs=[pl.BlockSpec(o.shape, lambda i, s_, nd=len(o.shape): (0,) * nd) for o in out_shapes])
    return pl.pallas_call(body, name="small_step", grid_spec=gs, out_shape=out_shapes)(meq, *arrays)


def _pad_lanes(v, n=D_MODEL):
    return jnp.pad(v, ((0, 0), (0, n - v.shape[1])))


def kernel(x, norm1_w, w_in, conv_qkv_w, a_log, dt_bias, gdn_norm_w, w_out, norm2_w, w_up, ffn_conv_w, w_down, final_norm_w, loss_target, m_norm1_w, m_w_in, m_conv_qkv_w, m_a_log, m_dt_bias, m_gdn_norm_w, m_w_out, m_norm2_w, m_w_up, m_ffn_conv_w, m_w_down, m_final_norm_w, v_norm1_w, v_w_in, v_conv_qkv_w, v_a_log, v_dt_bias, v_gdn_norm_w, v_w_out, v_norm2_w, v_w_up, v_ffn_conv_w, v_w_down, v_final_norm_w):
    c = lax.axis_index("c")
    q = 2 * lax.axis_index("x") + lax.axis_index("y")
    S = x.shape[1]
    cq = jnp.stack([c, q]).astype(jnp.int32)

    *in_started, in_token = _gather_halves_start([_narrowed_shard(w_in), conv_qkv_w[0], ffn_conv_w[0]], x, "gather_in_start")
    w_in_l, m_w_in_l, v_w_in_l = (jnp.swapaxes(a + in_token[0:1, 0:1], 1, 2)[0] for a in (w_in, m_w_in, v_w_in))
    h1 = _rmsnorm_fwd(x[0], norm1_w, "norm1", after=[in_token])
    rest = [(a[0] + in_token[0:1, 0:1]).astype(_MXU) for a in (w_out, w_up, w_down)]
    in_shards, got_in = _gather_halves_wait(in_started, [w_in_l, m_w_in_l, v_w_in_l, h1, *rest], "gather_in_wait")
    (g_in, g_conv, g_fconv), (w_in_l, m_w_in_l, v_w_in_l) = _place_own(
        in_shards, _sibling_fill(got_in, "fill_in"), cq, "place_in", carry=[w_in_l, m_w_in_l, v_w_in_l])
    *rest_started, token = _gather_halves_start(rest, g_conv, "gather_rest_start")

    rest_state = {}

    def rest_arrived(after):
        rest_state["shards"], got = _gather_halves_wait(rest_started, after, "gather_rest_wait")
        *rest_state["fill"], tok = _sibling_fill_start(got, "fill_rest_start")
        return tok

    def rest_filled(after):
        got = _sibling_fill_wait(rest_state["fill"], after, "fill_rest_wait")
        g_out, g_up, g_down = _place_own(rest_state["shards"], got, cq, "place_rest")
        return g_out.reshape(D_MODEL, D_MODEL), g_up, g_down.reshape(D_FF, D_MODEL)

    rest_weights = (rest_arrived, rest_filled)
    wp = _wp_assemble(g_in, [token])
    conv_f = jnp.concatenate([g_conv[i] for i in range(N_CHIPS)], axis=1)
    fcw = jnp.concatenate([g_fconv[i] for i in range(N_CHIPS)], axis=1)
    gp = _pad_lanes(jnp.concatenate([a_log, dt_bias], axis=1), 128)
    fnw = final_norm_w[None, :]
    early = {}

    early_names = ("w_up", "w_down", "w_out")

    def early_sibling(d_wup, d_wdown, d_wout):
        *early["sibling"], tok = _grad_sibling_start(
            [d_wup, d_wdown.reshape(N_CHIPS, D_FF // N_CHIPS, D_MODEL), d_wout.reshape(N_CHIPS, D_MODEL // N_CHIPS, D_MODEL)],
            "grad_sibling_early_start")
        return tok

    def early_chips(after):
        fams_e, got_e = _grad_sibling_wait(early["sibling"], after, "grad_sibling_early_wait")
        early["parts"] = [_add_sibling(f, r, cq, "add_sibling_" + nm) for f, r, nm in zip(fams_e, got_e, early_names)]
        *early["started"], tok = _grad_chips_start([p[1] for p in early["parts"]], "grad_chips_start")
        return tok

    def late_sibling(d_wp):
        *early["late_sibling"], tok = _grad_sibling_start([d_wp[None]], "grad_sibling_late_start")
        return tok

    loss_l, dx, g = _local_step(x[0], loss_target[0], h1, norm1_w, norm2_w, fnw, gp, gdn_norm_w, wp,
                                conv_f, fcw, rest_weights, (early_sibling, early_chips, late_sibling))
    fams, got = _grad_sibling_wait(early["late_sibling"], [dx], "grad_sibling_late_wait")
    late_part = _add_sibling_split(fams[0], got[0], cq, "add_sibling_w_in")
    *late_started, late_token = _grad_chips_start([late_part[1]], "grad_chips_late_start")
    small = _pack_small(g["n1w"], g["n2w"], g["fnw"], g["gp"], g["gnw"], g["conv_w"], g["fcw_g"], g["fcw_u"], loss_l)
    *small_started, small_token = _grad_chips_start([], "small_gather_start", small)
    got3_e = _grad_chips_wait(early["started"], [late_token, small_token], "grad_chips_wait")
    g_w_up, g_w_down, g_w_out = _grad_share(
        [_add_chips(p[0], r3, cq, "add_chips_" + nm) for p, r3, nm in zip(early["parts"], got3_e, early_names)],
        "grad_share_early")
    big = {}

    def adamw_big(nm, w, gg, m, v):
        d_, m_, v_ = _adamw(w[0], gg, m[0], v[0], "adamw_" + nm)
        big[nm] = (gg[None], d_[None], m_[None], v_[None])

    adamw_big("w_up", w_up, g_w_up, m_w_up, v_w_up)
    adamw_big("w_down", w_down, g_w_down, m_w_down, v_w_down)
    adamw_big("w_out", w_out, g_w_out, m_w_out, v_w_out)
    got3, = _grad_chips_wait(late_started, [big[nm][1] for nm in early_names], "grad_chips_late_wait")
    g_w_in, = _grad_share([_add_chips(late_part[0], got3, cq, "add_chips_w_in")], "grad_share_late")
    g_t, d_t, m_t, v_t = _adamw_transposed(w_in_l, g_w_in, m_w_in_l, v_w_in_l, "adamw_w_in")
    big["w_in"] = tuple(jnp.swapaxes(t[None], 1, 2) for t in (g_t, d_t, m_t, v_t))
    small_all, small = _grad_chips_wait(small_started, [d_t], "small_gather_wait", with_small=True)
    small_names = ["norm1_w", "norm2_w", "final_norm_w", "a_log", "dt_bias", "gdn_norm_w", "conv_qkv_w", "ffn_conv_w"]
    loss_b, *small_out = _small_step(
        jnp.stack([2 * q + c, q]).astype(jnp.int32), small_all, small,
        [norm1_w, norm2_w, final_norm_w[None], a_log, dt_bias, gdn_norm_w, conv_qkv_w[0], ffn_conv_w[0]],
        [m_norm1_w, m_norm2_w, m_final_norm_w[None], m_a_log, m_dt_bias, m_gdn_norm_w, m_conv_qkv_w[0], m_ffn_conv_w[0]],
        [v_norm1_w, v_norm2_w, v_final_norm_w[None], v_a_log, v_dt_bias, v_gdn_norm_w, v_conv_qkv_w[0], v_ffn_conv_w[0]])
    like = dict(final_norm_w=lambda t: t[0], conv_qkv_w=lambda t: t[None], ffn_conv_w=lambda t: t[None])
    for k, nm in enumerate(small_names):
        big[nm] = tuple(like.get(nm, lambda t: t)(t) for t in small_out[4 * k:4 * k + 4])
    names = ["norm1_w", "w_in", "conv_qkv_w", "a_log", "dt_bias", "gdn_norm_w", "w_out", "norm2_w", "w_up",
             "ffn_conv_w", "w_down", "final_norm_w"]
    return (loss_b[0, 0], dx[None], *[big[n][0] for n in names], *[big[n][1] for n in names],
            *[big[n][2] for n in names], *[big[n][3] for n in names])
```

```python
import functools
import math

import numpy as np
import jax
import jax.numpy as jnp
from jax import lax
from jax.experimental import pallas as pl
from jax.experimental.pallas import tpu as pltpu

F32 = jnp.float32
BF16 = jnp.bfloat16
_MXU = jnp.bfloat16
_HI = lax.Precision.HIGHEST
EPS = 1e-6
V7X_VMEM_LIMIT = 56 * 1024 * 1024
MESH = pl.DeviceIdType.MESH

D_MODEL = 1024
GDN_HEADS, GDN_DIM, GDN_CHUNK, GDN_CONV = 4, 128, 64, 4
GDN_WIDTH = GDN_HEADS * GDN_DIM
DIL_HEADS, DIL_DIM = 8, 64
DIL_WIDTH = DIL_HEADS * DIL_DIM
D_FF, FFN_CONV = 2816, 3
FFN_STRIP = 128
IN_COLS = 3592
P_COLS = 3840
P_Z, P_QKVB, P_BA = 1536, 2048, 3584
ATT_T = 1024
ADAM_LR, ADAM_B1, ADAM_B2, ADAM_EPS, ADAM_WD, ADAM_STEP = 0.001, 0.9, 0.999, 1e-08, 0.01, 10
N_CHIPS = 4


def _cparams(sem=None, vmem=None):
    kw = {}
    if sem is not None:
        kw["dimension_semantics"] = sem
    if vmem is not None:
        kw["vmem_limit_bytes"] = vmem
    return pltpu.CompilerParams(**kw)


def _silu(x):
    return x * jax.nn.sigmoid(x)


def _pick_tile(n, cap):
    best = None
    for t in range(128, min(n, cap) + 1, 128):
        if n % t == 0:
            best = t
    return best or n


def _mm(a, b, mode, *, out_dtype=F32, residual=None, name, b_blocks=False, place=None, into=None, tn=None, after=(),
        normed_by=None):
    if mode == "nn":
        M, K = a.shape
        N = b.shape[0] * b.shape[2] if b_blocks else b.shape[1]
    elif mode == "nt":
        (M, K), (N, _) = a.shape, b.shape
    else:
        (K, M), (_, N) = a.shape, b.shape
    tm = _pick_tile(M, 1024)
    tn = b.shape[2] if b_blocks else (tn or _pick_tile(N, 1536))

    def vmem(tm, tn):
        return 2 * (tm * K * a.dtype.itemsize + tn * K * b.dtype.itemsize
                    + tm * tn * (jnp.dtype(out_dtype).itemsize + (4 if residual is not None else 0))) + 3 * tm * tn * 4

    fixed_tn = b_blocks or (place is not None and place[0] == "blocks")
    while vmem(tm, tn) > 40 * 1024 * 1024:
        if (tm >= tn or fixed_tn) and tm % 256 == 0:
            tm //= 2
        elif tn % 256 == 0 and not fixed_tn:
            tn //= 2
        else:
            tm //= 2
    a_spec = pl.BlockSpec((K, tm), lambda j, i: (0, i)) if mode == "tn" else pl.BlockSpec((tm, K), lambda j, i: (i, 0))
    if b_blocks:
        b_spec = pl.BlockSpec((None, K, tn), lambda j, i: (j, 0, 0))
    else:
        b_spec = pl.BlockSpec((tn, K), lambda j, i: (j, 0)) if mode == "nt" else pl.BlockSpec((K, tn), lambda j, i: (0, j))
    r_spec = pl.BlockSpec((tm, tn), lambda j, i: (i, j))
    if place is None:
        o_spec, o_shape = r_spec, (M, N)
    elif place[0] == "rows":
        off = place[2] // tm
        o_spec, o_shape = pl.BlockSpec((tm, tn), lambda j, i: (i + off, j)), (place[1], N)
    else:
        off = place[2]
        o_spec, o_shape = pl.BlockSpec((None, tm, tn), lambda j, i: (j + off, i, 0)), (place[1], M, tn)
    dims = {"nn": (((1,), (0,)), ((), ())), "nt": (((1,), (1,)), ((), ())), "tn": (((0,), (0,)), ((), ()))}[mode]

    def body(*refs):
        a_ref, b_ref = refs[0], refs[1]
        o_ref = refs[-1] if normed_by is None else refs[-2]
        acc = lax.dot_general(a_ref[...].astype(_MXU), b_ref[...].astype(_MXU), dims, preferred_element_type=F32)
        if residual is not None:
            acc = acc + refs[2][...]
        o_ref[...] = acc.astype(out_dtype)
        if normed_by is not None:
            rs = lax.rsqrt(jnp.mean(acc * acc, axis=-1, keepdims=True) + EPS)
            refs[-1][...] = (acc * rs * refs[len(ins0)][...]).astype(_MXU)

    ins, specs, alias = [a, b], [a_spec, b_spec], {}
    if residual is not None:
        ins.append(residual)
        specs.append(r_spec)
    ins0 = list(ins)
    if normed_by is not None:
        assert tn == N and place is None and into is None
        ins.append(normed_by)
        specs.append(pl.BlockSpec((1, N), lambda j, i: (0, 0)))
    if into is not None:
        alias = {len(ins): 0}
        ins.append(into)
        specs.append(pl.BlockSpec(memory_space=pl.ANY))
    ins += list(after)
    specs += [pl.BlockSpec(memory_space=pl.ANY)] * len(after)
    o_shape = jax.ShapeDtypeStruct(o_shape, out_dtype)
    if normed_by is not None:
        o_spec, o_shape = (o_spec, r_spec), (o_shape, jax.ShapeDtypeStruct((M, N), _MXU))
    return pl.pallas_call(
        body, name=name, grid=(N // tn, M // tm), in_specs=specs, out_specs=o_spec,
        out_shape=o_shape, input_output_aliases=alias,
        compiler_params=_cparams(("parallel", "parallel"), V7X_VMEM_LIMIT),
    )(*ins)


def _wp_assemble(g_in, after=()):
    nb, Dm, Wb = g_in.shape
    T = 256
    n_lo = P_QKVB - 2 * Wb

    def body(g_ref, *rest):
        g2 = g_ref[2]
        rest[-1][...] = jnp.concatenate(
            [g_ref[0], g_ref[1], g2[:, :n_lo], g2[:, n_lo + 8:], g_ref[3], g2[:, n_lo:n_lo + 8],
             jnp.zeros((T, P_COLS - P_BA - 8), g_in.dtype)], axis=1)

    return pl.pallas_call(
        body, name="wp_assemble", grid=(Dm // T,),
        in_specs=[pl.BlockSpec((nb, T, Wb), lambda i: (0, i, 0))] + [pl.BlockSpec(memory_space=pl.ANY)] * len(after),
        out_specs=pl.BlockSpec((T, P_COLS), lambda i: (i, 0)), out_shape=jax.ShapeDtypeStruct((Dm, P_COLS), g_in.dtype),
        compiler_params=_cparams(("parallel",)),
    )(g_in, *after)


def _rmsnorm_fwd(x, w, name, after=()):
    S, D = x.shape
    T = _pick_tile(S, 512)

    def body(x_ref, w_ref, *rest):
        xv = x_ref[...]
        rs = lax.rsqrt(jnp.mean(xv * xv, axis=-1, keepdims=True) + EPS)
        rest[-1][...] = (xv * rs * w_ref[...]).astype(rest[-1].dtype)

    return pl.pallas_call(
        body, name=name, grid=(S // T,),
        in_specs=[pl.BlockSpec((T, D), lambda i: (i, 0)), pl.BlockSpec((1, D), lambda i: (0, 0))] + [_ANY] * len(after),
        out_specs=pl.BlockSpec((T, D), lambda i: (i, 0)),
        out_shape=jax.ShapeDtypeStruct((S, D), _MXU),
        compiler_params=_cparams(("parallel",)),
    )(x, w, *after)


def _rmsnorm_bwd(dh, x, w, dres, name, after=()):
    S, D = x.shape
    pair = isinstance(dh, tuple)
    T = _pick_tile(S, 256 if pair else 512)
    dhs = [*dh[0], dh[1]] if pair else [dh]

    def body(*refs):
        x_ref, w_ref, dres_ref = refs[len(dhs):len(dhs) + 3]
        dx_ref, dw_ref = refs[-2:]
        xv = x_ref[...]
        rs = lax.rsqrt(jnp.mean(xv * xv, axis=-1, keepdims=True) + EPS)
        xn = xv * rs
        if pair:
            b_ref = refs[len(dhs) - 1]
            nb, _, Kb = b_ref.shape
            per = nb // (len(dhs) - 1)
            dhv = None
            for blk in range(nb):
                lo = (blk % per) * Kb
                t = lax.dot_general(refs[blk // per][:, lo:lo + Kb].astype(_MXU), b_ref[blk].astype(_MXU), (_NT, ((), ())),
                                    preferred_element_type=F32)
                dhv = t if dhv is None else dhv + t
        else:
            dhv = refs[0][...]
        dxn = dhv * w_ref[...]
        dx_ref[...] = dres_ref[...] + rs * (dxn - xn * jnp.mean(dxn * xn, axis=-1, keepdims=True))

        @pl.when(pl.program_id(0) == 0)
        def _():
            dw_ref[...] = jnp.zeros_like(dw_ref)

        dw_ref[...] += jnp.sum(dhv * xn, axis=0, keepdims=True)

    row = pl.BlockSpec((T, D), lambda i: (i, 0))
    vec = pl.BlockSpec((1, D), lambda i: (0, 0))
    dh_specs = [row] if not pair else (
        [pl.BlockSpec((T, a.shape[1]), lambda i: (i, 0)) for a in dh[0]] + [pl.BlockSpec(dh[1].shape, lambda i: (0, 0, 0))])
    return pl.pallas_call(
        body, name=name, grid=(S // T,), in_specs=dh_specs + [row, vec, row] + [_ANY] * len(after), out_specs=(row, vec),
        out_shape=(jax.ShapeDtypeStruct((S, D), F32), jax.ShapeDtypeStruct((1, D), F32)),
        compiler_params=_cparams(("arbitrary",), V7X_VMEM_LIMIT if pair else None),
    )(*dhs, x, w, dres, *after)


def _loss_head(x3, w, tgt, name):
    S, D = tgt.shape
    fused = isinstance(x3, tuple)
    T = _pick_tile(S, 256 if fused else 512)
    xs = list(x3) if fused else [x3]

    def body(*refs):
        w_ref, t_ref = refs[len(xs):len(xs) + 2]
        loss_ref, dx_ref, dxn_ref, dw_ref = refs[-4:]
        xv = refs[0][...]
        if fused:
            xv = lax.dot_general(xv.astype(_MXU), refs[1][...].astype(_MXU), (_NN, ((), ())),
                                 preferred_element_type=F32) + refs[2][...]
        rs = lax.rsqrt(jnp.mean(xv * xv, axis=-1, keepdims=True) + EPS)
        xn = xv * rs
        err = xn * w_ref[...] - t_ref[...]
        dy = err * (1.0 / D)
        dxn = dy * w_ref[...]
        dxv = rs * (dxn - xn * jnp.mean(dxn * xn, axis=-1, keepdims=True))
        dx_ref[...] = dxv
        dxn_ref[...] = dxv.astype(dxn_ref.dtype)

        @pl.when(pl.program_id(0) == 0)
        def _():
            dw_ref[...] = jnp.zeros_like(dw_ref)
            loss_ref[...] = jnp.zeros_like(loss_ref)

        dw_ref[...] += jnp.sum(dy * xn, axis=0, keepdims=True)
        part = jnp.sum(jnp.sum(err * err, axis=-1, keepdims=True), axis=0, keepdims=True) * (0.5 / D)
        loss_ref[...] += jnp.broadcast_to(part, loss_ref.shape)

    row = pl.BlockSpec((T, D), lambda i: (i, 0))
    vec = pl.BlockSpec((1, D), lambda i: (0, 0))
    x_specs = [row] if not fused else [pl.BlockSpec((T, xs[0].shape[1]), lambda i: (i, 0)),
                                       pl.BlockSpec(xs[1].shape, lambda i: (0, 0)), row]
    return pl.pallas_call(
        body, name=name, grid=(S // T,), in_specs=x_specs + [vec, row],
        out_specs=(pl.BlockSpec((8, 128), lambda i: (0, 0)), row, row, vec),
        out_shape=(jax.ShapeDtypeStruct((8, 128), F32), jax.ShapeDtypeStruct((S, D), F32), jax.ShapeDtypeStruct((S, D), _MXU),
                   jax.ShapeDtypeStruct((1, D), F32)),
        compiler_params=_cparams(("arbitrary",), V7X_VMEM_LIMIT if fused else None),
    )(*xs, w, tgt)


def _shifted(ext, back, lo, n):
    if back == 0:
        return ext[lo:lo + n, :]
    return pltpu.roll(ext, back % ext.shape[0], 0)[lo:lo + n, :]


def _conv_windows(ext, K, T):
    return [_shifted(ext, (K - 1) - i, 8, T) for i in range(K)]


def _conv_taps(ext, w, K, T):
    out = None
    for i, win in enumerate(_conv_windows(ext, K, T)):
        term = win * w[i:i + 1, :]
        out = term if out is None else out + term
    return out


def _conv_taps_t(ext, w, K, T):
    out = None
    for i in range(K):
        term = _shifted(ext, i - (K - 1), 0, T) * w[i:i + 1, :]
        out = term if out is None else out + term
    return out


def _tri_masks(C):
    r = lax.broadcasted_iota(jnp.int32, (C, C), 0)
    c = lax.broadcasted_iota(jnp.int32, (C, C), 1)
    return r == c, r >= c, r > c, r <= c


_NN, _NT, _TN = ((1,), (0,)), ((1,), (1,)), ((0,), (0,))
_GDN_PASSES = dict(qk=1, inv=1, sol=1, scan=1, bwd=1)


def _bdot_raw(a, b, kind, passes):
    dims = ({"NN": ((2,), (1,)), "NT": ((2,), (2,)), "TN": ((1,), (1,))}[kind], ((0,), (0,)))
    if passes == 0:
        return lax.dot_general(a, b, dims, precision=_HI, preferred_element_type=F32)
    ah, bh = a.astype(BF16), b.astype(BF16)
    out = lax.dot_general(ah, bh, dims, preferred_element_type=F32)
    if passes == 3:
        al, bl = (a - ah.astype(F32)).astype(BF16), (b - bh.astype(F32)).astype(BF16)
        out = out + lax.dot_general(ah, bl, dims, preferred_element_type=F32) + lax.dot_general(al, bh, dims, preferred_element_type=F32)
    return out


@functools.partial(jax.custom_vjp, nondiff_argnums=(2, 3))
def _bdot(a, b, kind, passes):
    return _bdot_raw(a, b, kind, passes)


def _bdot_fwd(a, b, kind, passes):
    return _bdot_raw(a, b, kind, passes), (a, b)


def _bdot_bwd(kind, passes, res, ct):
    a, b = res
    if kind == "NN":
        return _bdot_raw(ct, b, "NT", passes), _bdot_raw(a, ct, "TN", passes)
    if kind == "NT":
        return _bdot_raw(ct, b, "NN", passes), _bdot_raw(ct, a, "TN", passes)
    return _bdot_raw(b, ct, "NT", passes), _bdot_raw(a, ct, "NN", passes)


_bdot.defvjp(_bdot_fwd, _bdot_bwd)


def _softplus(x):
    return jnp.maximum(x, 0.0) + jnp.log(1.0 + jnp.exp(-jnp.abs(x)))


def _gdn_stage1(cq, ck, cv, b_col, a_col, alog, dtb, dot=_bdot_raw):
    C = cq.shape[1]
    eye, incl, strict, incl_t = _tri_masks(C)
    qn = cq * lax.rsqrt(jnp.sum(cq * cq, axis=-1, keepdims=True) + EPS) * (GDN_DIM ** -0.5)
    kn = ck * lax.rsqrt(jnp.sum(ck * ck, axis=-1, keepdims=True) + EPS)
    beta = jax.nn.sigmoid(b_col)
    g = -jnp.exp(alog) * _softplus(a_col + dtb)
    g_row = jnp.sum(jnp.where(eye, g, 0.0), axis=1, keepdims=True)
    beta_row = jnp.sum(jnp.where(eye, beta, 0.0), axis=1, keepdims=True)
    gc_col = jnp.sum(jnp.where(incl, g_row, 0.0), axis=2, keepdims=True)
    gc_row = jnp.sum(jnp.where(incl_t, g, 0.0), axis=1, keepdims=True)
    dec = jnp.where(incl, jnp.exp(jnp.where(incl, gc_col - gc_row, 0.0)), 0.0)
    kk = dot(kn, kn, "NT", _GDN_PASSES["qk"])
    qk = dot(qn, kn, "NT", _GDN_PASSES["qk"])
    lmat = jnp.where(strict, dec * kk * beta_row, 0.0)
    attn = dec * qk * beta_row
    gam = jnp.exp(gc_col)
    gc_last = gc_col[:, C - 1:C, :]
    k_end = kn * (jnp.exp(gc_last - gc_col) * beta)
    return lmat, cv, gam * kn, gam * qn, attn, k_end, jnp.exp(gc_last)


def _tri_inv(lmat):
    C = lmat.shape[1]
    eye = _tri_masks(C)[0]
    ps = _GDN_PASSES["inv"]
    p = jnp.where(eye, 1.0, 0.0) - lmat
    lp = _bdot_raw(lmat, lmat, "NN", ps)
    n = int(math.log2(C))
    for s in range(1, n):
        p = p + _bdot_raw(p, lp, "NN", ps)
        if s < n - 1:
            lp = _bdot_raw(lp, lp, "NN", ps)
    return p


def _gated_norm(o, z, gnw):
    on = o * lax.rsqrt(jnp.mean(o * o, axis=-1, keepdims=True) + EPS) * gnw
    return on * _silu(z)


GDN_PG = 4
GDN_SG = 4


def _gdn_pairs(c, ba, gp, G):
    C, W, H = GDN_CHUNK, GDN_WIDTH, GDN_HEADS
    pairs = [(j, h) for j in range(G) for h in range(H)]
    cq, ck, cv = (jnp.stack([c[C * j:C * (j + 1), o + GDN_DIM * h:o + GDN_DIM * (h + 1)] for j, h in pairs]) for o in (0, W, 2 * W))
    b_col = jnp.stack([ba[C * j:C * (j + 1), h:h + 1] for j, h in pairs])
    a_col = jnp.stack([ba[C * j:C * (j + 1), H + h:H + h + 1] for j, h in pairs])
    alog = jnp.stack([gp[0:1, h:h + 1] for j, h in pairs])
    dtb = jnp.stack([gp[0:1, H + h:H + h + 1] for j, h in pairs])
    return pairs, (cq, ck, cv, b_col, a_col, alog, dtb)


def _gdn_pre_specs(S, G):
    C = GDN_CHUNK
    T = C * G
    return dict(
        cur=pl.BlockSpec((T, 3 * GDN_WIDTH), lambda i: (i, 0)),
        prev=pl.BlockSpec((8, 3 * GDN_WIDTH), lambda i: (jnp.maximum(i * (T // 8) - 1, 0), 0)),
        ba=pl.BlockSpec((T, 128), lambda i: (i, P_BA // 128)),
        cw=pl.BlockSpec((GDN_CONV, 3 * GDN_WIDTH), lambda i: (0, 0)),
        vec=pl.BlockSpec((1, 128), lambda i: (0, 0)),
        hd=pl.BlockSpec((GDN_HEADS, T, GDN_DIM), lambda i: (0, i, 0)),
        hc=pl.BlockSpec((GDN_HEADS, T, C), lambda i: (0, i, 0)),
        ge=pl.BlockSpec((G, GDN_HEADS, 8, 128), lambda i: (i, 0, 0, 0)),
    )


def _hd_shape(S, last=GDN_DIM):
    return jax.ShapeDtypeStruct((GDN_HEADS, S, last), F32)


def _gdn_pre(proj, conv_w, gp):
    S = proj.shape[0]
    C, G = GDN_CHUNK, GDN_PG
    nc = S // C
    sp = _gdn_pre_specs(S, G)

    def body(cur_ref, prev_ref, ba_ref, cw_ref, gp_ref, uv_ref, wk_ref, qd_ref, ke_ref, at_ref, ti_ref, ge_ref):
        prev = prev_ref[...] * jnp.where(pl.program_id(0) == 0, 0.0, 1.0)
        c = _silu(_conv_taps(jnp.concatenate([prev, cur_ref[...]], axis=0), cw_ref[...], GDN_CONV, C * G))
        pairs, args = _gdn_pairs(c, ba_ref[...], gp_ref[...], G)
        lmat, v, rk, q_dec, attn, k_end, g_end = _gdn_stage1(*args)
        t = _tri_inv(lmat)
        u_v = _bdot_raw(t, v, "NN", _GDN_PASSES["sol"])
        w_k = _bdot_raw(t, rk, "NN", _GDN_PASSES["sol"])
        for b, (j, h) in enumerate(pairs):
            rows = slice(C * j, C * (j + 1))
            uv_ref[h, rows, :] = u_v[b]
            wk_ref[h, rows, :] = w_k[b]
            qd_ref[h, rows, :] = q_dec[b]
            ke_ref[h, rows, :] = k_end[b]
            at_ref[h, rows, :] = attn[b]
            ti_ref[h, rows, :] = t[b]
            ge_ref[j, h] = jnp.broadcast_to(g_end[b], (8, 128))

    return pl.pallas_call(
        body, name="gdn_pre", grid=(nc // G,),
        in_specs=[sp["cur"], sp["prev"], sp["ba"], sp["cw"], sp["vec"]],
        out_specs=(sp["hd"], sp["hd"], sp["hd"], sp["hd"], sp["hc"], sp["hc"], sp["ge"]),
        out_shape=(_hd_shape(S), _hd_shape(S), _hd_shape(S), _hd_shape(S), _hd_shape(S, C), _hd_shape(S, C),
                   jax.ShapeDtypeStruct((nc, GDN_HEADS, 8, 128), F32)),
        compiler_params=_cparams(("parallel",)),
    )(proj, proj, proj, conv_w, gp)


def _gdn_scan_specs(S, G, rev):
    C = GDN_CHUNK
    T = C * G
    n = S // T
    ci = (lambda i: n - 1 - i) if rev else (lambda i: i)
    return dict(
        hd=pl.BlockSpec((GDN_HEADS, T, GDN_DIM), lambda i: (0, ci(i), 0)),
        hc=pl.BlockSpec((GDN_HEADS, T, C), lambda i: (0, ci(i), 0)),
        ge=pl.BlockSpec((G, GDN_HEADS, 8, 128), lambda i: (ci(i), 0, 0, 0)),
        z=pl.BlockSpec((T, GDN_WIDTH), lambda i: (ci(i), P_Z // GDN_WIDTH)),
        oa=pl.BlockSpec((T, GDN_WIDTH), lambda i: (ci(i), 0)),
        vec=pl.BlockSpec((1, 128), lambda i: (0, 0)),
        st=pl.BlockSpec((G, GDN_HEADS, GDN_DIM, GDN_DIM), lambda i: (ci(i), 0, 0, 0)),
    )


def _gdn_scan(u_v, w_k, q_dec, k_end, attn, g_end, proj, gnw, mix, after=()):
    S = proj.shape[0]
    C, G = GDN_CHUNK, GDN_SG
    nc = S // C
    sp = _gdn_scan_specs(S, G, False)
    ps = _GDN_PASSES["scan"]

    def body(uv_ref, wk_ref, qd_ref, ke_ref, at_ref, ge_ref, z_ref, gnw_ref, *rest):
        oa_ref, st_ref, s_scr = rest[-3:]

        @pl.when(pl.program_id(0) == 0)
        def _():
            s_scr[...] = jnp.zeros_like(s_scr)

        for j in range(G):
            rows = slice(C * j, C * (j + 1))
            st = s_scr[...]
            st_ref[j] = st
            u = uv_ref[:, rows, :] - _bdot_raw(wk_ref[:, rows, :], st, "NN", ps)
            o = _bdot_raw(qd_ref[:, rows, :], st, "NN", ps) + _bdot_raw(at_ref[:, rows, :], u, "NN", ps)
            s_scr[...] = ge_ref[j][:, 0:1, 0:1] * st + _bdot_raw(ke_ref[:, rows, :], u, "TN", ps)
            for h in range(GDN_HEADS):
                cols = slice(GDN_DIM * h, GDN_DIM * (h + 1))
                oa_ref[rows, cols] = _gated_norm(o[h], z_ref[rows, cols], gnw_ref[...])

    return pl.pallas_call(
        body, name="gdn_scan", grid=(nc // G,),
        in_specs=[sp["hd"], sp["hd"], sp["hd"], sp["hd"], sp["hc"], sp["ge"], sp["z"], sp["vec"]] + [_ANY] * (1 + len(after)),
        out_specs=(sp["oa"], sp["st"]),
        out_shape=(jax.ShapeDtypeStruct(mix.shape, F32),
                   jax.ShapeDtypeStruct((nc, GDN_HEADS, GDN_DIM, GDN_DIM), F32)),
        input_output_aliases={8: 0},
        scratch_shapes=[pltpu.VMEM((GDN_HEADS, GDN_DIM, GDN_DIM), F32)],
        compiler_params=_cparams(("arbitrary",)),
    )(u_v, w_k, q_dec, k_end, attn, g_end, proj, gnw, mix, *after)


def _gdn_scan_bwd(u_v, w_k, q_dec, k_end, attn, g_end, proj, gnw, states, d_oa):
    S = proj.shape[0]
    C, G = GDN_CHUNK, GDN_SG
    nc = S // C
    sp = _gdn_scan_specs(S, G, True)
    ps, pb = _GDN_PASSES["scan"], _GDN_PASSES["bwd"]

    def body(uv_ref, wk_ref, qd_ref, ke_ref, at_ref, ge_ref, z_ref, gnw_ref, st_ref, doa_ref,
             duv_ref, dwk_ref, dqd_ref, dke_ref, dat_ref, dge_ref, dz_ref, dgnw_ref, ds_scr):
        @pl.when(pl.program_id(0) == 0)
        def _():
            ds_scr[...] = jnp.zeros_like(ds_scr)
            dgnw_ref[...] = jnp.zeros_like(dgnw_ref)

        dgnw = jnp.zeros((1, 128), F32)
        for j in reversed(range(G)):
            rows = slice(C * j, C * (j + 1))
            st = st_ref[j]
            wk, qd, ke, at = wk_ref[:, rows, :], qd_ref[:, rows, :], ke_ref[:, rows, :], at_ref[:, rows, :]
            u = uv_ref[:, rows, :] - _bdot_raw(wk, st, "NN", ps)
            o = _bdot_raw(qd, st, "NN", ps) + _bdot_raw(at, u, "NN", ps)
            dos = []
            for h in range(GDN_HEADS):
                cols = slice(GDN_DIM * h, GDN_DIM * (h + 1))
                _, vjp2 = jax.vjp(_gated_norm, o[h], z_ref[rows, cols], gnw_ref[...])
                do_h, dz_h, dgn = vjp2(doa_ref[rows, cols])
                dz_ref[rows, cols] = dz_h
                dgnw = dgnw + dgn
                dos.append(do_h)
            do = jnp.stack(dos)
            ds_new = ds_scr[...]
            du = _bdot_raw(at, do, "TN", pb) + _bdot_raw(ke, ds_new, "NN", pb)
            duv_ref[:, rows, :] = du
            dat_ref[:, rows, :] = _bdot_raw(do, u, "NT", pb)
            dqd_ref[:, rows, :] = _bdot_raw(do, st, "NT", pb)
            dke_ref[:, rows, :] = _bdot_raw(u, ds_new, "NT", pb)
            dwk_ref[:, rows, :] = -_bdot_raw(du, st, "NT", pb)
            d_ge = jnp.sum(jnp.sum(st * ds_new, axis=2, keepdims=True), axis=1, keepdims=True)
            dge_ref[j] = jnp.broadcast_to(d_ge, (GDN_HEADS, 8, 128))
            ds_scr[...] = ge_ref[j][:, 0:1, 0:1] * ds_new + _bdot_raw(qd, do, "TN", pb) - _bdot_raw(wk, du, "TN", pb)
        dgnw_ref[...] += dgnw

    return pl.pallas_call(
        body, name="gdn_scan_bwd", grid=(nc // G,),
        in_specs=[sp["hd"], sp["hd"], sp["hd"], sp["hd"], sp["hc"], sp["ge"], sp["z"], sp["vec"], sp["st"], sp["oa"]],
        out_specs=(sp["hd"], sp["hd"], sp["hd"], sp["hd"], sp["hc"], sp["ge"], sp["oa"], sp["vec"]),
        out_shape=(_hd_shape(S), _hd_shape(S), _hd_shape(S), _hd_shape(S), _hd_shape(S, C),
                   jax.ShapeDtypeStruct((nc, GDN_HEADS, 8, 128), F32), jax.ShapeDtypeStruct((S, GDN_WIDTH), F32),
                   jax.ShapeDtypeStruct((1, 128), F32)),
        scratch_shapes=[pltpu.VMEM((GDN_HEADS, GDN_DIM, GDN_DIM), F32)],
        compiler_params=_cparams(("arbitrary",)),
    )(u_v, w_k, q_dec, k_end, attn, g_end, proj, gnw, states, d_oa)


def _gdn_post(proj, conv_w, gp, tinv, u_v, w_k, d_uv, d_wk, d_qd, d_ke, d_at, d_ge):
    S = proj.shape[0]
    C, G = GDN_CHUNK, GDN_PG
    nc = S // C
    sp = _gdn_pre_specs(S, G)
    pb = _GDN_PASSES["bwd"]

    def body(cur_ref, prev_ref, ba_ref, cw_ref, gp_ref, ti_ref, uv_ref, wk_ref, duv_ref, dwk_ref, dqd_ref, dke_ref,
             dat_ref, dge_ref, dpre_ref, dba_ref, dgp_ref):
        i = pl.program_id(0)

        @pl.when(i == 0)
        def _():
            dgp_ref[...] = jnp.zeros_like(dgp_ref)

        prev = prev_ref[...] * jnp.where(i == 0, 0.0, 1.0)
        pre = _conv_taps(jnp.concatenate([prev, cur_ref[...]], axis=0), cw_ref[...], GDN_CONV, C * G)
        sg = jax.nn.sigmoid(pre)
        dsilu = sg * (1.0 + pre * (1.0 - sg))
        pairs, args = _gdn_pairs(pre * sg, ba_ref[...], gp_ref[...], G)
        _, vjp1 = jax.vjp(functools.partial(_gdn_stage1, dot=_bdot), *args)

        def take(ref):
            return jnp.stack([ref[h, C * j:C * (j + 1), :] for j, h in pairs])

        t, u_v, w_k = take(ti_ref), take(uv_ref), take(wk_ref)
        d_v = _bdot_raw(t, take(duv_ref), "TN", pb)
        d_rk = _bdot_raw(t, take(dwk_ref), "TN", pb)
        d_l = -(_bdot_raw(d_v, u_v, "NT", pb) + _bdot_raw(d_rk, w_k, "NT", pb))
        d_ge = jnp.stack([dge_ref[j, h][0:1, 0:1] for j, h in pairs])
        dcq, dck, dcv, db, da, dalog, ddtb = vjp1((d_l, d_v, d_rk, take(dqd_ref), take(dat_ref), take(dke_ref), d_ge))
        lane = lax.broadcasted_iota(jnp.int32, (C, 128), 1)
        lane1 = lax.broadcasted_iota(jnp.int32, (1, 128), 1)
        dgp = jnp.zeros((1, 128), F32)
        for j in range(G):
            rows = slice(C * j, C * (j + 1))
            dba = jnp.zeros((C, 128), F32)
            for h in range(GDN_HEADS):
                b = GDN_HEADS * j + h
                for o_, dcx in ((0, dcq), (GDN_WIDTH, dck), (2 * GDN_WIDTH, dcv)):
                    cols = slice(o_ + GDN_DIM * h, o_ + GDN_DIM * (h + 1))
                    dpre_ref[rows, cols] = dcx[b] * dsilu[rows, cols]
                dba = dba + jnp.where(lane == h, db[b], 0.0) + jnp.where(lane == GDN_HEADS + h, da[b], 0.0)
                dgp = dgp + jnp.where(lane1 == h, dalog[b], 0.0) + jnp.where(lane1 == GDN_HEADS + h, ddtb[b], 0.0)
            dba_ref[rows, :] = dba
        dgp_ref[0:1, :] += dgp

    T = C * G
    return pl.pallas_call(
        body, name="gdn_post", grid=(nc // G,),
        in_specs=[sp["cur"], sp["prev"], sp["ba"], sp["cw"], sp["vec"], sp["hc"], sp["hd"], sp["hd"], sp["hd"], sp["hd"],
                  sp["hd"], sp["hd"], sp["hc"], sp["ge"]],
        out_specs=(sp["cur"], pl.BlockSpec((T, 128), lambda i: (i, 0)), pl.BlockSpec((8, 128), lambda i: (0, 0))),
        out_shape=(jax.ShapeDtypeStruct((S, 3 * GDN_WIDTH), F32), jax.ShapeDtypeStruct((S, 128), F32),
                   jax.ShapeDtypeStruct((8, 128), F32)),
        compiler_params=_cparams(("arbitrary",)),
    )(proj, proj, proj, conv_w, gp, tinv, u_v, w_k, d_uv, d_wk, d_qd, d_ke, d_at, d_ge)


def _conv_bwd(dpre, x, xcol0, w, K, name, tc):
    S, Cc = dpre.shape
    T = _pick_tile(S, 256)
    nt, ncol = S // T, Cc // tc
    xo = xcol0 // tc

    def body(d_ref, dn_ref, x_ref, xp_ref, w_ref, dx_ref, dw_ref):
        i = pl.program_id(1)
        dn = dn_ref[...] * jnp.where(i == nt - 1, 0.0, 1.0)
        dv = d_ref[...]
        ext_d = jnp.concatenate([dv, dn], axis=0)
        dx_ref[...] = _conv_taps_t(ext_d, w_ref[...], K, T).astype(dx_ref.dtype)
        xp = xp_ref[...] * jnp.where(i == 0, 0.0, 1.0)
        ext_x = jnp.concatenate([xp, x_ref[...]], axis=0)

        @pl.when(i == 0)
        def _():
            dw_ref[...] = jnp.zeros_like(dw_ref)

        for k in range(K):
            dw_ref[k:k + 1, :] += jnp.sum(dv * _shifted(ext_x, (K - 1) - k, 8, T), axis=0, keepdims=True)

    r8 = T // 8
    return pl.pallas_call(
        body, name=name, grid=(ncol, nt),
        in_specs=[pl.BlockSpec((T, tc), lambda j, i: (i, j)),
                  pl.BlockSpec((8, tc), lambda j, i: (jnp.minimum((i + 1) * r8, S // 8 - 1), j)),
                  pl.BlockSpec((T, tc), lambda j, i: (i, j + xo)),
                  pl.BlockSpec((8, tc), lambda j, i: (jnp.maximum(i * r8 - 1, 0), j + xo)),
                  pl.BlockSpec((K, tc), lambda j, i: (0, j))],
        out_specs=(pl.BlockSpec((T, tc), lambda j, i: (i, j)), pl.BlockSpec((K, tc), lambda j, i: (0, j))),
        out_shape=(jax.ShapeDtypeStruct((S, Cc), _MXU), jax.ShapeDtypeStruct((K, Cc), F32)),
        compiler_params=_cparams(("parallel", "arbitrary")),
    )(dpre, dpre, x, x, w)


def _dil_bias(nt, T):
    d = (np.arange(nt)[:, None, None] * T + np.arange(T)[None, None, :] - np.arange(T)[None, :, None])
    cnt = ((d >= 0) & (d <= 128)).astype(np.float64) + ((d >= 0) & (d % 4 == 0) & (d <= 512)) + ((d >= 0) & (d % 16 == 0))
    return jnp.asarray(np.where(cnt > 0, np.log(np.maximum(cnt, 1.0)), -1e30), dtype=F32)


def _attn_fwd(proj, after=()):
    S = proj.shape[0]
    T = min(ATT_T, S)
    nt, H = S // T, T // 2
    bias = _dil_bias(nt, T)
    scale = DIL_DIM ** -0.5
    npair = DIL_WIDTH // 128
    qb0, kb0, vb0 = P_QKVB // 128, (P_QKVB + DIL_WIDTH) // 128, (P_QKVB + 2 * DIL_WIDTH) // 128

    def body(q_ref, k_ref, v_ref, b_ref, *rest):
        o_ref, lse_ref = rest[-2:]
        i = pl.program_id(1)
        qs = (q_ref[...] * scale).astype(_MXU)

        def update(carry, kt, vt, qt, bt):
            out = []
            for hh in range(2):
                m, l, acc = carry[hh]
                sl = slice(hh * DIL_DIM, (hh + 1) * DIL_DIM)
                s = lax.dot_general(kt[:, sl], qt[:, sl], (_NT, ((), ())), preferred_element_type=F32) + bt
                m_new = jnp.maximum(m, jnp.max(s, axis=0, keepdims=True))
                p = jnp.exp(s - m_new)
                a = jnp.exp(m - m_new)
                l = a * l + jnp.sum(p, axis=0, keepdims=True)
                acc = a * acc + lax.dot_general(vt[:, sl], p.astype(_MXU), (_TN, ((), ())), preferred_element_type=F32)
                out.append((m_new, l, acc))
            return tuple(out)

        def keys(j):
            rows = pl.ds(pl.multiple_of(j * T, T), T)
            return k_ref[rows, :].astype(_MXU), v_ref[rows, :].astype(_MXU)

        init = tuple((jnp.full((1, T), -1e30, F32), jnp.zeros((1, T), F32), jnp.zeros((DIL_DIM, T), F32)) for _ in range(2))
        res = lax.fori_loop(0, i, lambda j, carry: update(carry, *keys(j), qs, b_ref[i - j]), init)
        kd, vd = keys(i)
        res = update(res, kd[:H], vd[:H], qs, b_ref[0, :H, :])
        late = update(tuple(tuple(t[:, H:] for t in r) for r in res), kd[H:], vd[H:], qs[H:], b_ref[0, H:, H:])
        res = tuple(tuple(jnp.concatenate([t[:, :H], u], axis=1) for t, u in zip(r, r2)) for r, r2 in zip(res, late))
        lse_ref[...] = jnp.zeros_like(lse_ref)
        for hh in range(2):
            m, l, acc = res[hh]
            o_ref[:, hh * DIL_DIM:(hh + 1) * DIL_DIM] = (acc / l).T
            lse_ref[hh:hh + 1, :] = m + jnp.log(l)

    return pl.pallas_call(
        body, name="attn_fwd", grid=(npair, nt),
        in_specs=[pl.BlockSpec((T, 128), lambda p, i: (i, qb0 + p)),
                  pl.BlockSpec((S, 128), lambda p, i: (0, kb0 + p)),
                  pl.BlockSpec((S, 128), lambda p, i: (0, vb0 + p)),
                  pl.BlockSpec((nt, T, T), lambda p, i: (0, 0, 0))] + [_ANY] * len(after),
        out_specs=(pl.BlockSpec((T, 128), lambda p, i: (i, GDN_WIDTH // 128 + p)),
                   pl.BlockSpec((None, None, 8, T), lambda p, i: (p, i, 0, 0))),
        out_shape=(jax.ShapeDtypeStruct((S, GDN_WIDTH + DIL_WIDTH), F32), jax.ShapeDtypeStruct((npair, nt, 8, T), F32)),
        compiler_params=_cparams(("parallel", "parallel")),
    )(proj, proj, proj, bias, *after)


def _attn_bwd(proj, mix, lse, d_mix):
    S = proj.shape[0]
    T = min(ATT_T, S)
    nt, H = S // T, T // 2
    bias = _dil_bias(nt, T)
    scale = DIL_DIM ** -0.5
    npair = DIL_WIDTH // 128
    qb0, kb0, vb0 = P_QKVB // 128, (P_QKVB + DIL_WIDTH) // 128, (P_QKVB + 2 * DIL_WIDTH) // 128

    def body(q_ref, k_ref, v_ref, o_ref, lse_ref, do_ref, b_ref, dq_ref, dk_ref, dv_ref, dq_scr):
        j = pl.program_id(1)

        @pl.when(j == 0)
        def _():
            dq_scr[...] = jnp.zeros_like(dq_scr)

        kt = k_ref[...].astype(_MXU)
        vt = v_ref[...].astype(_MXU)
        ones = jnp.ones((8, DIL_DIM), F32)

        def block(carry, kt, vt, rows, lsev, bt):
            qs = (q_ref[rows, :] * scale).astype(_MXU)
            dov = do_ref[rows, :]
            prod = dov * o_ref[rows, :]
            dob = dov.astype(_MXU)
            out = []
            dqs = []
            for hh in range(2):
                dk, dv = carry[hh]
                sl = slice(hh * DIL_DIM, (hh + 1) * DIL_DIM)
                s = lax.dot_general(kt[:, sl], qs[:, sl], (_NT, ((), ())), preferred_element_type=F32) + bt
                p = jnp.exp(s - lsev[hh:hh + 1, :])
                delta = lax.dot_general(ones, prod[:, sl], (_NT, ((), ())), precision=_HI, preferred_element_type=F32)[0:1, :]
                dp = lax.dot_general(vt[:, sl], dob[:, sl], (_NT, ((), ())), preferred_element_type=F32)
                ds = (p * (dp - delta)).astype(_MXU)
                dv = dv + lax.dot_general(p.astype(_MXU), dob[:, sl], (_NN, ((), ())), preferred_element_type=F32)
                dk = dk + lax.dot_general(ds, qs[:, sl], (_NN, ((), ())), preferred_element_type=F32)
                dqs.append(lax.dot_general(ds, kt[:, sl], (_TN, ((), ())), preferred_element_type=F32) * scale)
                out.append((dk, dv))
            dq_scr[rows, :] += jnp.concatenate(dqs, axis=1)
            return tuple(out)

        def step(i, carry):
            return block(carry, kt, vt, pl.ds(pl.multiple_of(i * T, T), T), lse_ref[i], b_ref[i - j])

        zeros = tuple((jnp.zeros((H, DIL_DIM), F32), jnp.zeros((H, DIL_DIM), F32)) for _ in range(2))
        lsed = lse_ref[j]
        early = block(zeros, kt[:H], vt[:H], pl.ds(pl.multiple_of(j * T, T), T), lsed, b_ref[0, :H, :])
        late = block(zeros, kt[H:], vt[H:], pl.ds(pl.multiple_of(j * T + H, H), H), lsed[:, H:], b_ref[0, H:, H:])
        init = tuple(tuple(jnp.concatenate([t, u], axis=0) for t, u in zip(r, r2)) for r, r2 in zip(early, late))
        res = lax.fori_loop(j + 1, nt, step, init)
        dk_ref[...] = jnp.concatenate([res[0][0], res[1][0]], axis=1).astype(dk_ref.dtype)
        dv_ref[...] = jnp.concatenate([res[0][1], res[1][1]], axis=1).astype(dv_ref.dtype)

        @pl.when(j == nt - 1)
        def _():
            dq_ref[...] = dq_scr[...].astype(dq_ref.dtype)

    full = lambda c0: pl.BlockSpec((S, 128), lambda p, j: (0, c0 + p))
    tile = lambda c0: pl.BlockSpec((T, 128), lambda p, j: (j, c0 + p))
    out3 = jax.ShapeDtypeStruct((S, DIL_WIDTH), _MXU)
    return pl.pallas_call(
        body, name="attn_bwd", grid=(npair, nt),
        in_specs=[full(qb0), tile(kb0), tile(vb0), full(GDN_WIDTH // 128),
                  pl.BlockSpec((None, nt, 8, T), lambda p, j: (p, 0, 0, 0)), full(GDN_WIDTH // 128),
                  pl.BlockSpec((nt, T, T), lambda p, j: (0, 0, 0))],
        out_specs=(full(0), tile(0), tile(0)),
        out_shape=(out3, out3, out3),
        scratch_shapes=[pltpu.VMEM((S, 128), F32)],
        compiler_params=_cparams(("parallel", "arbitrary")),
    )(proj, proj, proj, mix, lse, d_mix, bias)


def _ffn_act(up, cw):
    S, Cc = up.shape[0], up.shape[1] // 2
    T, tc = _pick_tile(S, 256), _pick_tile(Cc, 1536)
    r16 = T // 16
    nct = Cc // tc

    def body(g_ref, gp_ref, u_ref, up_ref, wg_ref, wu_ref, o_ref):
        keep = jnp.where(pl.program_id(1) == 0, 0.0, 1.0)
        cg = _conv_taps(jnp.concatenate([gp_ref[8:16, :].astype(F32) * keep, g_ref[...].astype(F32)], axis=0),
                        wg_ref[...], FFN_CONV, T)
        cu = _conv_taps(jnp.concatenate([up_ref[8:16, :].astype(F32) * keep, u_ref[...].astype(F32)], axis=0),
                        wu_ref[...], FFN_CONV, T)
        o_ref[...] = (_silu(cg) * cu).astype(o_ref.dtype)

    cur = lambda o: pl.BlockSpec((T, tc), lambda j, i: (i, j + o))
    prev = lambda o: pl.BlockSpec((16, tc), lambda j, i: (jnp.maximum(i * r16 - 1, 0), j + o))
    wsp = lambda o: pl.BlockSpec((FFN_CONV, tc), lambda j, i: (0, j + o))
    return pl.pallas_call(
        body, name="ffn_act", grid=(nct, S // T),
        in_specs=[cur(0), prev(0), cur(nct), prev(nct), wsp(0), wsp(nct)], out_specs=cur(0),
        out_shape=jax.ShapeDtypeStruct((S, Cc), _MXU),
        compiler_params=_cparams(("parallel", "parallel")),
    )(up, up, up, up, cw, cw)


def _ffn_act_bwd(d_act, up, cw):
    S, Cc = up.shape[0], up.shape[1] // 2
    T, tc = _pick_tile(S, 256), _pick_tile(Cc, 1536)
    r8, r16 = T // 8, T // 16
    nt = S // T
    nct = Cc // tc
    K = FFN_CONV

    def body(da_ref, dan_ref, g_ref, gp_ref, gn_ref, u_ref, up_ref, un_ref, wg_ref, wu_ref,
             dg_ref, du_ref, dwg_ref, dwu_ref):
        i = pl.program_id(1)
        keep_p = jnp.where(i == 0, 0.0, 1.0)
        keep_n = jnp.where(i == nt - 1, 0.0, 1.0)

        @pl.when(i == 0)
        def _():
            dwg_ref[...] = jnp.zeros_like(dwg_ref)
            dwu_ref[...] = jnp.zeros_like(dwu_ref)

        def strip(c, carry):
            cs = pl.ds(pl.multiple_of(c * FFN_STRIP, FFN_STRIP), FFN_STRIP)
            wg, wu = wg_ref[:, cs], wu_ref[:, cs]
            xg = jnp.concatenate([gp_ref[8:16, cs].astype(F32) * keep_p, g_ref[:, cs].astype(F32),
                                  gn_ref[0:8, cs].astype(F32) * keep_n], axis=0)
            xu = jnp.concatenate([up_ref[8:16, cs].astype(F32) * keep_p, u_ref[:, cs].astype(F32),
                                  un_ref[0:8, cs].astype(F32) * keep_n], axis=0)
            cg = _conv_taps(xg, wg, K, T + 8)
            cu = _conv_taps(xu, wu, K, T + 8)
            da = jnp.concatenate([da_ref[:, cs], dan_ref[:, cs] * keep_n], axis=0)
            sg = jax.nn.sigmoid(cg)
            d_cg = da * cu * (sg * (1.0 + cg * (1.0 - sg)))
            d_cu = da * (cg * sg)
            dg_ref[:, cs] = _conv_taps_t(d_cg, wg, K, T).astype(dg_ref.dtype)
            du_ref[:, cs] = _conv_taps_t(d_cu, wu, K, T).astype(du_ref.dtype)
            for k in range(K):
                dwg_ref[k:k + 1, cs] += jnp.sum(d_cg[0:T, :] * _shifted(xg, (K - 1) - k, 8, T), axis=0, keepdims=True)
                dwu_ref[k:k + 1, cs] += jnp.sum(d_cu[0:T, :] * _shifted(xu, (K - 1) - k, 8, T), axis=0, keepdims=True)
            return carry

        lax.fori_loop(0, tc // FFN_STRIP, strip, 0)

    cur = lambda o: pl.BlockSpec((T, tc), lambda j, i: (i, j + o))
    prev = lambda o: pl.BlockSpec((16, tc), lambda j, i: (jnp.maximum(i * r16 - 1, 0), j + o))
    nxt = lambda o: pl.BlockSpec((16, tc), lambda j, i: (jnp.minimum((i + 1) * r16, S // 16 - 1), j + o))
    nxt8 = pl.BlockSpec((8, tc), lambda j, i: (jnp.minimum((i + 1) * r8, S // 8 - 1), j))
    wsp = lambda o: pl.BlockSpec((K, tc), lambda j, i: (0, j + o))
    return pl.pallas_call(
        body, name="ffn_act_bwd", grid=(nct, nt),
        in_specs=[cur(0), nxt8, cur(0), prev(0), nxt(0), cur(nct), prev(nct), nxt(nct), wsp(0), wsp(nct)],
        out_specs=(cur(0), cur(0), wsp(0), wsp(0)),
        out_shape=(jax.ShapeDtypeStruct((S, Cc), _MXU), jax.ShapeDtypeStruct((S, Cc), _MXU),
                   jax.ShapeDtypeStruct((K, Cc), F32), jax.ShapeDtypeStruct((K, Cc), F32)),
        compiler_params=_cparams(("parallel", "arbitrary")),
    )(d_act, d_act, up, up, up, up, up, up, cw, cw)


def _local_step(x, tgt, h1, n1w, n2w, fnw, gp, gnw, wp, conv_w, fcw, rest_weights, early_grads):
    proj = _mm(h1, wp, "nn", name="proj")
    u_v, w_k, q_dec, k_end, attn, tinv, g_end = _gdn_pre(proj, conv_w, gp)
    mix, lse = _attn_fwd(proj)
    mix, states = _gdn_scan(u_v, w_k, q_dec, k_end, attn, g_end, proj, gnw, mix, after=[rest_weights[0]([mix])])
    w_out, w_up4, w_down = rest_weights[1]([mix])
    x2, h2 = _mm(mix, w_out, "nn", residual=x, name="outproj", normed_by=n2w)
    up = _mm(h2, w_up4, "nn", b_blocks=True, out_dtype=_MXU, name="up")
    act = _ffn_act(up, fcw)
    loss, dx3, dx3n, d_fnw = _loss_head((act, w_down, x2), fnw, tgt, "loss_head")
    d_act = _mm(dx3n, w_down, "nt", name="d_act")
    d_wdown = _mm(act, dx3n, "tn", name="d_wdown")
    d_upg, d_upu, d_fcwg, d_fcwu = _ffn_act_bwd(d_act, up, fcw)
    d_wup = _mm(h2, d_upg, "tn", place=("blocks", N_CHIPS, 0), tn=w_up4.shape[2], name="d_wgate")
    d_wup = _mm(h2, d_upu, "tn", place=("blocks", N_CHIPS, N_CHIPS // 2), tn=w_up4.shape[2], into=d_wup, name="d_wup")
    dx2, d_n2w = _rmsnorm_bwd(([d_upg, d_upu], w_up4), x2, n2w, dx3, "norm2_bwd")
    d_wout = _mm(mix, dx2, "tn", name="d_wout")
    token = early_grads[0](d_wup, d_wdown, d_wout)
    d_mix = _mm(dx2, w_out, "nt", name="d_mix", after=[token])
    dq_b, dk_b, dv_b = _attn_bwd(proj, mix, lse, d_mix)
    d_uv, d_wk, d_qd, d_ke, d_at, d_ge, d_z, d_gnw = _gdn_scan_bwd(u_v, w_k, q_dec, k_end, attn, g_end, proj,
                                                                   gnw, states, d_mix)
    d_pre, d_ba, d_gp = _gdn_post(proj, conv_w, gp, tinv, u_v, w_k, d_uv, d_wk, d_qd, d_ke, d_at, d_ge)
    token = early_grads[1]([d_pre])
    d_qkva, d_convw = _conv_bwd(d_pre, proj, 0, conv_w + token[0:1, 0:1], GDN_CONV, "gdn_conv_bwd", 512)
    d_proj = jnp.concatenate([d_qkva, d_z.astype(_MXU), dq_b, dk_b, dv_b, d_ba.astype(_MXU),
                              jnp.zeros((x.shape[0], P_COLS - P_BA - 128), _MXU)], axis=1)
    d_wp = _mm(h1, d_proj, "tn", name="d_wp")
    token = early_grads[2](d_wp)
    dx, d_n1w = _rmsnorm_bwd(([d_proj], wp[None]), x, n1w, dx2, "norm1_bwd", after=[token])
    grads = dict(wp=d_wp, conv_w=d_convw, w_out=d_wout, w_up=d_wup, fcw_g=d_fcwg, fcw_u=d_fcwu, w_down=d_wdown,
                 n1w=d_n1w, n2w=d_n2w, fnw=d_fnw, gp=d_gp, gnw=d_gnw)
    return loss, dx, grads


_HBM = pl.BlockSpec(memory_space=pltpu.HBM)


def _pos():
    return lax.axis_index("x"), lax.axis_index("y"), lax.axis_index("c")


def _other_chips(x, y):
    return [(1 - x, y), (x, 1 - y), (1 - x, 1 - y)]


def _halvable(shape):
    return shape[0] % 32 == 0


def _rows_of_half(shape, half):
    if not _halvable(shape):
        return pl.ds(0, shape[0])
    return pl.ds(pl.multiple_of(half * (shape[0] // 2), 16), shape[0] // 2)


_SEM = pl.BlockSpec(memory_space=pltpu.SEMAPHORE)
_ANY = pl.BlockSpec(memory_space=pl.ANY)
_DATAFLOW = pltpu.SideEffectType.DATAFLOW_SIDE_EFFECTING


def _in_hbm(a):
    return pltpu.with_memory_space_constraint(a, pltpu.HBM)


def _halves_copy(src_refs, land_refs, send_sems, recv_sems, shapes, a, j, block, x, y, c):
    px, py = _other_chips(x, y)[j]
    rows = _rows_of_half(shapes[a], c)
    return pltpu.make_async_remote_copy(
        src_ref=src_refs[a].at[rows, :], dst_ref=land_refs[a].at[block, rows, :], send_sem=send_sems.at[3 * a + j],
        recv_sem=recv_sems.at[3 * a + j], device_id=(px, py, c), device_id_type=MESH)


def _gather_halves_start(shards, after, name):
    n = len(shards)
    shapes = [s.shape for s in shards]

    def body(*refs):
        ins, lands = refs[:n], refs[n:2 * n]
        send_sems, recv_sems = refs[2 * n + 1], refs[2 * n + 2]
        token = refs[-1]
        x, y, c = _pos()
        q = 2 * x + y
        for a in range(n):
            for j in range(3):
                _halves_copy(ins, lands, send_sems, recv_sems, shapes, a, j, q, x, y, c).start()
        token[...] = jnp.zeros_like(token)

    land_shapes = [(N_CHIPS,) + s.shape for s in shards]
    return pl.pallas_call(
        body, name=name,
        out_shape=(pltpu.SemaphoreType.DMA((3 * n,)), pltpu.SemaphoreType.DMA((3 * n,)),
                   *[pltpu.HBM(s.shape, s.dtype) for s in shards],
                   *[pltpu.HBM(ls, s.dtype) for ls, s in zip(land_shapes, shards)],
                   jax.ShapeDtypeStruct((8, 128), F32)),
        in_specs=[_HBM] * (2 * n) + [_ANY],
        out_specs=(_SEM, _SEM, *[_HBM] * (2 * n), pl.BlockSpec(memory_space=pltpu.VMEM)),
        input_output_aliases={a: 2 + a for a in range(2 * n)},
        compiler_params=pltpu.CompilerParams(has_side_effects=_DATAFLOW),
    )(*[_in_hbm(s) for s in shards], *[_in_hbm(lax.empty(ls, s.dtype)) for ls, s in zip(land_shapes, shards)], after)


def _gather_halves_wait(started, after, name):
    send_sems, recv_sems, *thru = started
    n = len(thru) // 2
    shapes = [t.shape for t in thru[:n]]

    def body(*refs):
        ins, lands = refs[:n], refs[n:2 * n]
        send_sems, recv_sems = refs[2 * n], refs[2 * n + 1]
        x, y, c = _pos()
        q = 2 * x + y
        chips = _other_chips(x, y)
        for a in range(n):
            for j, (px, py) in enumerate(chips):
                _halves_copy(ins, lands, send_sems, recv_sems, shapes, a, j, q, x, y, c).wait_send()
                _halves_copy(ins, lands, send_sems, recv_sems, shapes, a, j, 2 * px + py, x, y, c).wait_recv()

    outs = pl.pallas_call(
        body, name=name, out_shape=[pltpu.HBM(t.shape, t.dtype) for t in thru],
        in_specs=[_HBM] * (2 * n) + [_SEM, _SEM] + [_ANY] * len(after), out_specs=[_HBM] * (2 * n),
        input_output_aliases={a: a for a in range(2 * n)},
        compiler_params=pltpu.CompilerParams(has_side_effects=_DATAFLOW),
    )(*thru, send_sems, recv_sems, *after)
    return outs[:n], outs[n:]


def _sibling_fill(gathered, name):
    big = [a for a, g in enumerate(gathered) if _halvable(g.shape[1:])]
    n = len(gathered)

    def body(*refs):
        ins, outs = refs[:n], refs[n:2 * n]
        send_sems, recv_sems = refs[2 * n:]
        x, y, c = _pos()
        chips = _other_chips(x, y)

        def copy(k, j, half):
            a = big[k]
            px, py = chips[j]
            rows = _rows_of_half(gathered[a].shape[1:], half)
            return pltpu.make_async_remote_copy(
                src_ref=ins[a].at[2 * px + py, rows, :], dst_ref=outs[a].at[2 * px + py, rows, :],
                send_sem=send_sems.at[3 * k + j], recv_sem=recv_sems.at[3 * k + j],
                device_id=(x, y, 1 - c), device_id_type=MESH)

        sends = [copy(k, j, c) for k in range(len(big)) for j in range(3)]
        for cp in sends:
            cp.start()
        for k in range(len(big)):
            for j in range(3):
                copy(k, j, 1 - c).wait_recv()
        for cp in sends:
            cp.wait_send()

    return pl.pallas_call(
        body, name=name, in_specs=[_HBM] * n, out_specs=[_HBM] * n,
        out_shape=[jax.ShapeDtypeStruct(g.shape, g.dtype) for g in gathered],
        input_output_aliases={a: a for a in range(n)},
        scratch_shapes=[pltpu.SemaphoreType.DMA((3 * len(big),)), pltpu.SemaphoreType.DMA((3 * len(big),))],
    )(*gathered)


def _fill_copy(refs, send_sems, recv_sems, shapes, a, j, half, x, y, c):
    px, py = _other_chips(x, y)[j]
    rows = _rows_of_half(shapes[a], half)
    return pltpu.make_async_remote_copy(
        src_ref=refs[a].at[2 * px + py, rows, :], dst_ref=refs[a].at[2 * px + py, rows, :],
        send_sem=send_sems.at[3 * a + j], recv_sem=recv_sems.at[3 * a + j],
        device_id=(x, y, 1 - c), device_id_type=MESH)


def _sibling_fill_start(gathered, name):
    n = len(gathered)
    shapes = [g.shape[1:] for g in gathered]

    def body(*refs):
        ins = refs[:n]
        send_sems, recv_sems = refs[n], refs[n + 1]
        token = refs[-1]
        x, y, c = _pos()
        for a in range(n):
            for j in range(3):
                _fill_copy(ins, send_sems, recv_sems, shapes, a, j, c, x, y, c).start()
        token[...] = jnp.zeros_like(token)

    return pl.pallas_call(
        body, name=name,
        out_shape=(pltpu.SemaphoreType.DMA((3 * n,)), pltpu.SemaphoreType.DMA((3 * n,)),
                   *[pltpu.HBM(g.shape, g.dtype) for g in gathered], jax.ShapeDtypeStruct((8, 128), F32)),
        in_specs=[_HBM] * n,
        out_specs=(_SEM, _SEM, *[_HBM] * n, pl.BlockSpec(memory_space=pltpu.VMEM)),
        input_output_aliases={a: 2 + a for a in range(n)},
        compiler_params=pltpu.CompilerParams(has_side_effects=_DATAFLOW),
    )(*[_in_hbm(g) for g in gathered])


def _sibling_fill_wait(started, after, name):
    send_sems, recv_sems, *thru = started
    n = len(thru)
    shapes = [t.shape[1:] for t in thru]

    def body(*refs):
        ins = refs[:n]
        send_sems, recv_sems = refs[n], refs[n + 1]
        x, y, c = _pos()
        for a in range(n):
            for j in range(3):
                _fill_copy(ins, send_sems, recv_sems, shapes, a, j, c, x, y, c).wait_send()
                _fill_copy(ins, send_sems, recv_sems, shapes, a, j, 1 - c, x, y, c).wait_recv()

    return pl.pallas_call(
        body, name=name, out_shape=[pltpu.HBM(t.shape, t.dtype) for t in thru],
        in_specs=[_HBM] * n + [_SEM, _SEM] + [_ANY] * len(after), out_specs=[_HBM] * n,
        input_output_aliases={a: a for a in range(n)},
        compiler_params=pltpu.CompilerParams(has_side_effects=_DATAFLOW),
    )(*thru, send_sems, recv_sems, *after)


def _place_own(shards, gathered, cq, name, carry=()):
    n = len(shards)
    nc = len(carry)
    steps = 4

    def body(cq_ref, *refs):
        for a in range(n):
            refs[2 * n + nc + a][...] = refs[a][...]

    def tile(shape):
        return shape[0] // steps if _halvable(shape) else shape[0]

    in_specs = [pl.BlockSpec((tile(s.shape), s.shape[1]), (lambda i, s_: (i, 0)) if _halvable(s.shape) else (lambda i, s_: (0, 0)))
                for s in shards]
    in_specs += [pl.BlockSpec(memory_space=pl.ANY)] * (n + nc)
    out_specs = [pl.BlockSpec((None, tile(s.shape), s.shape[1]),
                              (lambda i, s_: (s_[1], i, 0)) if _halvable(s.shape) else (lambda i, s_: (s_[1], 0, 0)))
                 for s in shards]
    out_specs += [pl.BlockSpec(memory_space=pl.ANY)] * nc
    gs = pltpu.PrefetchScalarGridSpec(num_scalar_prefetch=1, grid=(steps,), in_specs=in_specs, out_specs=out_specs)
    outs = pl.pallas_call(
        body, name=name, grid_spec=gs,
        out_shape=[jax.ShapeDtypeStruct(g.shape, g.dtype) for g in gathered] + [jax.ShapeDtypeStruct(t.shape, t.dtype) for t in carry],
        input_output_aliases={1 + n + a: a for a in range(n + nc)},
        compiler_params=_cparams(("arbitrary",)),
    )(cq, *shards, *gathered, *carry)
    return (outs[:n], outs[n:]) if nc else outs


def _half_rows(ref, c, rh):
    return ref.at[:, pl.ds(pl.multiple_of(c * rh, 8), rh), :]


def _chips_copy(src_refs, land_refs, send_sems, recv_sems, a, j, x, y, c):
    px, py = _other_chips(x, y)[j]
    return pltpu.make_async_remote_copy(src_ref=src_refs[a].at[2 * px + py], dst_ref=land_refs[a].at[j],
                                        send_sem=send_sems.at[3 * a + j], recv_sem=recv_sems.at[3 * a + j],
                                        device_id=(px, py, c), device_id_type=MESH)


def _peer(r, x, y, c):
    return (x if r & 4 == 0 else 1 - x), (y if r & 2 == 0 else 1 - y), (c if r & 1 == 0 else 1 - c)


def _small_copy(small_ref, all_ref, send_sems, recv_sems, base, r, slot, x, y, c):
    return pltpu.make_async_remote_copy(src_ref=small_ref, dst_ref=all_ref.at[slot], send_sem=send_sems.at[base + r - 1],
                                        recv_sem=recv_sems.at[base + r - 1], device_id=_peer(r, x, y, c), device_id_type=MESH)


def _grad_chips_start(parts, name, small=None):
    n = len(parts)
    srcs = list(parts) + ([] if small is None else [small])
    m = len(srcs)

    def body(*refs):
        ins, lands = refs[:m], refs[m:2 * m]
        send_sems, recv_sems = refs[2 * m], refs[2 * m + 1]
        token = refs[-1]
        x, y, c = _pos()
        for a in range(n):
            for j in range(3):
                _chips_copy(ins, lands, send_sems, recv_sems, a, j, x, y, c).start()
        if small is not None:
            for r in range(1, 8):
                _small_copy(ins[n], lands[n], send_sems, recv_sems, 3 * n, r, 4 * x + 2 * y + c, x, y, c).start()
        token[...] = jnp.zeros_like(token)

    land_shapes = [(3,) + p.shape[1:] for p in parts] + ([] if small is None else [(8,) + small.shape])
    nsem = 3 * n + (0 if small is None else 7)
    return pl.pallas_call(
        body, name=name,
        out_shape=(pltpu.SemaphoreType.DMA((nsem,)), pltpu.SemaphoreType.DMA((nsem,)),
                   *[pltpu.HBM(p.shape, p.dtype) for p in srcs],
                   *[pltpu.HBM(ls, p.dtype) for ls, p in zip(land_shapes, srcs)],
                   jax.ShapeDtypeStruct((8, 128), F32)),
        in_specs=[_HBM] * (2 * m),
        out_specs=(_SEM, _SEM, *[_HBM] * (2 * m), pl.BlockSpec(memory_space=pltpu.VMEM)),
        input_output_aliases={a: 2 + a for a in range(2 * m)},
        compiler_params=pltpu.CompilerParams(has_side_effects=_DATAFLOW),
    )(*[_in_hbm(p) for p in srcs], *[_in_hbm(lax.empty(ls, p.dtype)) for ls, p in zip(land_shapes, srcs)])


def _grad_chips_wait(started, after, name, with_small=False):
    send_sems, recv_sems, *thru = started
    m = len(thru) // 2
    n = m - (1 if with_small else 0)

    def body(*refs):
        ins, lands = refs[:m], refs[m:2 * m]
        send_sems, recv_sems = refs[2 * m], refs[2 * m + 1]
        x, y, c = _pos()
        for a in range(n):
            for j in range(3):
                cp = _chips_copy(ins, lands, send_sems, recv_sems, a, j, x, y, c)
                cp.wait_send()
                cp.wait_recv()
        if with_small:
            for r in range(1, 8):
                px, py, pc = _peer(r, x, y, c)
                _small_copy(ins[n], lands[n], send_sems, recv_sems, 3 * n, r, 4 * x + 2 * y + c, x, y, c).wait_send()
                _small_copy(ins[n], lands[n], send_sems, recv_sems, 3 * n, r, 4 * px + 2 * py + pc, x, y, c).wait_recv()

    outs = pl.pallas_call(
        body, name=name, out_shape=[pltpu.HBM(t.shape, t.dtype) for t in thru],
        in_specs=[_HBM] * (2 * m) + [_SEM, _SEM] + [_ANY] * len(after), out_specs=[_HBM] * (2 * m),
        input_output_aliases={a: a for a in range(2 * m)},
        compiler_params=pltpu.CompilerParams(has_side_effects=_DATAFLOW),
    )(*thru, send_sems, recv_sems, *after)
    return list(outs[m:]) + list(outs[n:m])


def _sibling_copy(src_refs, land_refs, send_sems, recv_sems, rhs, a, c, x, y):
    return pltpu.make_async_remote_copy(src_ref=_half_rows(src_refs[a], 1 - c, rhs[a]), dst_ref=land_refs[a],
                                        send_sem=send_sems.at[a], recv_sem=recv_sems.at[a],
                                        device_id=(x, y, 1 - c), device_id_type=MESH)


def _grad_sibling_start(fams, name):
    n = len(fams)
    rhs = [f.shape[1] // 2 for f in fams]

    def body(*refs):
        ins, lands = refs[:n], refs[n:2 * n]
        send_sems, recv_sems = refs[2 * n], refs[2 * n + 1]
        token = refs[-1]
        x, y, c = _pos()
        for a in range(n):
            _sibling_copy(ins, lands, send_sems, recv_sems, rhs, a, c, x, y).start()
        token[...] = jnp.zeros_like(token)

    land_shapes = [(f.shape[0], f.shape[1] // 2, f.shape[2]) for f in fams]
    return pl.pallas_call(
        body, name=name,
        out_shape=(pltpu.SemaphoreType.DMA((n,)), pltpu.SemaphoreType.DMA((n,)),
                   *[pltpu.HBM(f.shape, f.dtype) for f in fams],
                   *[pltpu.HBM(ls, f.dtype) for ls, f in zip(land_shapes, fams)],
                   jax.ShapeDtypeStruct((8, 128), F32)),
        in_specs=[_HBM] * (2 * n),
        out_specs=(_SEM, _SEM, *[_HBM] * (2 * n), pl.BlockSpec(memory_space=pltpu.VMEM)),
        input_output_aliases={a: 2 + a for a in range(2 * n)},
        compiler_params=pltpu.CompilerParams(has_side_effects=_DATAFLOW),
    )(*[_in_hbm(f) for f in fams], *[_in_hbm(lax.empty(ls, f.dtype)) for ls, f in zip(land_shapes, fams)])


def _grad_sibling_wait(started, after, name):
    send_sems, recv_sems, *thru = started
    n = len(thru) // 2
    rhs = [t.shape[1] // 2 for t in thru[:n]]

    def body(*refs):
        ins, lands = refs[:n], refs[n:2 * n]
        send_sems, recv_sems = refs[2 * n], refs[2 * n + 1]
        x, y, c = _pos()
        for a in range(n):
            cp = _sibling_copy(ins, lands, send_sems, recv_sems, rhs, a, c, x, y)
            cp.wait_send()
            cp.wait_recv()

    outs = pl.pallas_call(
        body, name=name, out_shape=[pltpu.HBM(t.shape, t.dtype) for t in thru],
        in_specs=[_HBM] * (2 * n) + [_SEM, _SEM] + [_ANY] * len(after), out_specs=[_HBM] * (2 * n),
        input_output_aliases={a: a for a in range(2 * n)},
        compiler_params=pltpu.CompilerParams(has_side_effects=_DATAFLOW),
    )(*thru, send_sems, recv_sems, *after)
    return outs[:n], outs[n:]


def _grad_share(fulls, name):
    n = len(fulls)
    rhs = [f.shape[0] // 2 for f in fulls]

    def body(*refs):
        ins, outs = refs[:n], refs[n:2 * n]
        send_sems, recv_sems = refs[2 * n], refs[2 * n + 1]
        x, y, c = _pos()

        def copy(a, half):
            rows = pl.ds(pl.multiple_of(half * rhs[a], 8), rhs[a])
            return pltpu.make_async_remote_copy(src_ref=ins[a].at[rows, :], dst_ref=outs[a].at[rows, :],
                                                send_sem=send_sems.at[a], recv_sem=recv_sems.at[a],
                                                device_id=(x, y, 1 - c), device_id_type=MESH)

        sends = [copy(a, c) for a in range(n)]
        for cp in sends:
            cp.start()
        for a in range(n):
            copy(a, 1 - c).wait_recv()
        for cp in sends:
            cp.wait_send()

    return pl.pallas_call(
        body, name=name, in_specs=[_HBM] * n, out_specs=[_HBM] * n,
        out_shape=[jax.ShapeDtypeStruct(f.shape, f.dtype) for f in fulls],
        input_output_aliases={a: a for a in range(n)},
        scratch_shapes=[pltpu.SemaphoreType.DMA((n,)), pltpu.SemaphoreType.DMA((n,))],
    )(*fulls)


def _add_sibling(own, recv, cq, name):
    nb, R, Cc = own.shape
    Rh = R // 2

    def body(cq_ref, a_ref, b_ref, o32_ref, o16_ref):
        s = a_ref[0] + b_ref[0]
        mine = pl.program_id(0) == cq_ref[1]

        @pl.when(mine)
        def _():
            o32_ref[...] = s

        @pl.when(jnp.logical_not(mine))
        def _():
            o16_ref[0] = s.astype(o16_ref.dtype)

    sp = pl.BlockSpec((1, Rh, Cc), lambda b, s: (b, 0, 0))
    gs = pltpu.PrefetchScalarGridSpec(
        num_scalar_prefetch=1, grid=(nb,),
        in_specs=[pl.BlockSpec((1, Rh, Cc), lambda b, s: (b, s[0], 0)), sp],
        out_specs=[pl.BlockSpec((Rh, Cc), lambda b, s: (0, 0)), sp])
    return pl.pallas_call(
        body, name=name, grid_spec=gs,
        out_shape=[jax.ShapeDtypeStruct((Rh, Cc), F32), jax.ShapeDtypeStruct((nb, Rh, Cc), _MXU)],
        compiler_params=_cparams(("arbitrary",)),
    )(cq, own, recv)


def _add_sibling_split(d_wp, recv, cq, name):
    _, Dm, Pc = d_wp.shape
    Rh = Dm // 2
    Wb = IN_COLS // N_CHIPS
    T = 256

    def body(cq_ref, a_ref, b_ref, o32_ref, o16_ref):
        s = a_ref[0] + b_ref[0]
        blocks = [s[:, 0:Wb], s[:, Wb:2 * Wb],
                  jnp.concatenate([s[:, 2 * Wb:P_QKVB], s[:, P_BA:P_BA + 8], s[:, P_QKVB:3 * Wb - 8]], axis=1),
                  s[:, 3 * Wb - 8:P_BA]]
        q = cq_ref[1]
        own = None
        for j, blk in enumerate(blocks):
            term = jnp.where(q == j, blk, 0.0)
            own = term if own is None else own + term
            o16_ref[j] = blk.astype(o16_ref.dtype)
        o32_ref[...] = own

    gs = pltpu.PrefetchScalarGridSpec(
        num_scalar_prefetch=1, grid=(Rh // T,),
        in_specs=[pl.BlockSpec((1, T, Pc), lambda i, s: (0, s[0] * (Rh // T) + i, 0)), pl.BlockSpec((1, T, Pc), lambda i, s: (0, i, 0))],
        out_specs=[pl.BlockSpec((T, Wb), lambda i, s: (i, 0)), pl.BlockSpec((N_CHIPS, T, Wb), lambda i, s: (0, i, 0))])
    return pl.pallas_call(
        body, name=name, grid_spec=gs,
        out_shape=[jax.ShapeDtypeStruct((Rh, Wb), F32), jax.ShapeDtypeStruct((N_CHIPS, Rh, Wb), _MXU)],
        compiler_params=_cparams(("parallel",)),
    )(cq, d_wp, recv)


def _add_chips(part32, recv3, cq, name):
    Rh, Cc = part32.shape

    def body(cq_ref, a_ref, b_ref, o_ref):
        acc = a_ref[...]
        for j in range(3):
            acc = acc + b_ref[j].astype(F32)
        o_ref[...] = acc

    gs = pltpu.PrefetchScalarGridSpec(
        num_scalar_prefetch=1, grid=(1,),
        in_specs=[pl.BlockSpec((Rh, Cc), lambda i, s: (0, 0)), pl.BlockSpec((3, Rh, Cc), lambda i, s: (0, 0, 0))],
        out_specs=pl.BlockSpec((Rh, Cc), lambda i, s: (s[0], 0)))
    return pl.pallas_call(
        body, name=name, grid_spec=gs, out_shape=jax.ShapeDtypeStruct((2 * Rh, Cc), F32),
        compiler_params=_cparams(("arbitrary",)),
    )(cq, part32, recv3)


def _narrowed_shard(w):
    _, Dm, n = w.shape
    nj = Dm // 128
    pad = -n % 128

    def body(x_ref, o_ref):
        for j in range(nj):
            cols = jnp.concatenate([x_ref[pl.ds(j, n, stride=nj), :], jnp.zeros((pad, 128), F32)], axis=0)
            o_ref[128 * j:128 * (j + 1), :] = cols.T[:, :n].astype(_MXU)

    return pl.pallas_call(body, name="narrowed_shard", out_shape=jax.ShapeDtypeStruct((Dm, n), _MXU),
                          compiler_params=_cparams(vmem=V7X_VMEM_LIMIT))(jnp.swapaxes(w, 1, 2).reshape(n * nj, 128))


def _adamw_transposed(w, g, m, v, name):
    n, Dm = w.shape
    T = 128

    def body(w_ref, g_ref, m_ref, v_ref, gt_ref, d_ref, mo_ref, vo_ref):
        gt = g_ref[...].T
        gt_ref[...] = gt
        d_ref[...], mo_ref[...], vo_ref[...] = _adamw_math(w_ref[...], gt, m_ref[...], v_ref[...])

    row = pl.BlockSpec((T, Dm), lambda i: (i, 0))
    sh = jax.ShapeDtypeStruct((n, Dm), F32)
    return pl.pallas_call(
        body, name=name, grid=(pl.cdiv(n, T),), in_specs=[row, pl.BlockSpec((Dm, T), lambda i: (0, i)), row, row],
        out_specs=(row,) * 4, out_shape=(sh,) * 4, compiler_params=_cparams(("parallel",)),
    )(w, g, m, v)


def _adamw(w, g, m, v, name):
    R, Cc = w.shape
    T = max([t for t in range(8, 257, 8) if R % t == 0], default=R)

    def body(w_ref, g_ref, m_ref, v_ref, d_ref, mo_ref, vo_ref):
        d_ref[...], mo_ref[...], vo_ref[...] = _adamw_math(w_ref[...], g_ref[...], m_ref[...], v_ref[...])

    sp = pl.BlockSpec((T, Cc), lambda i: (i, 0))
    sh = jax.ShapeDtypeStruct((R, Cc), F32)
    return pl.pallas_call(
        body, name=name, grid=(R // T,), in_specs=[sp] * 4, out_specs=(sp, sp, sp), out_shape=(sh, sh, sh),
        compiler_params=_cparams(("parallel",)),
    )(w, g, m, v)


SMALL_ROWS = 32
ROW_CONV, ROW_FCG, ROW_FCU = 5, 13, 22


def _adamw_math(w, g, m, v):
    mn = ADAM_B1 * m + (1.0 - ADAM_B1) * g
    vn = ADAM_B2 * v + (1.0 - ADAM_B2) * (g * g)
    c1 = 1.0 / (1.0 - ADAM_B1 ** ADAM_STEP)
    c2 = 1.0 / (1.0 - ADAM_B2 ** ADAM_STEP)
    return -ADAM_LR * ((mn * c1) / (jnp.sqrt(vn * c2) + ADAM_EPS) + ADAM_WD * w), mn, vn


def _pack_small(n1, n2, fn, gp, gn, conv, fcg, fcu, loss):
    W = D_MODEL

    def body(n1_ref, n2_ref, fn_ref, gp_ref, gn_ref, conv_ref, fcg_ref, fcu_ref, loss_ref, o_ref):
        o_ref[...] = jnp.zeros_like(o_ref)
        o_ref[0:1, :] = n1_ref[...]
        o_ref[1:2, :] = n2_ref[...]
        o_ref[2:3, :] = fn_ref[...]
        o_ref[3:4, 0:8] = gp_ref[0:1, 0:8]
        o_ref[3:4, 8:9] = loss_ref[0:1, 0:1]
        o_ref[4:5, 0:128] = gn_ref[...]
        for i in range(GDN_CONV):
            o_ref[ROW_CONV + 2 * i:ROW_CONV + 2 * i + 1, :] = conv_ref[i:i + 1, 0:W]
            o_ref[ROW_CONV + 2 * i + 1:ROW_CONV + 2 * i + 2, 0:3 * GDN_WIDTH - W] = conv_ref[i:i + 1, W:3 * GDN_WIDTH]
        for r0, ref in ((ROW_FCG, fcg_ref), (ROW_FCU, fcu_ref)):
            for i in range(FFN_CONV):
                for k in range(3):
                    n = min(W, D_FF - k * W)
                    o_ref[r0 + 3 * i + k:r0 + 3 * i + k + 1, 0:n] = ref[i:i + 1, k * W:k * W + n]

    return pl.pallas_call(body, name="pack_small", out_shape=jax.ShapeDtypeStruct((SMALL_ROWS, W), F32))(
        n1, n2, fn, gp, gn, conv, fcg, fcu, loss)


def _small_step(meq, small_all, small, ws, ms, vs):
    W = D_MODEL
    n = len(ws)
    cw, fw = ws[6].shape[1], ws[7].shape[1]

    def body(meq_ref, all_ref, own_ref, *refs):
        w_refs, m_refs, v_refs = refs[:n], refs[n:2 * n], refs[2 * n:3 * n]
        loss_ref = refs[3 * n]
        outs = refs[3 * n + 1:]
        me, q = meq_ref[0], meq_ref[1]
        red = None
        for d in range(8):
            term = jnp.where(me == d, own_ref[...], all_ref[d])
            red = term if red is None else red + term
        loss_ref[...] = jnp.broadcast_to(red[3:4, 8:9], loss_ref.shape)
        conv = [jnp.concatenate([red[ROW_CONV + 2 * i:ROW_CONV + 2 * i + 1, :],
                                 red[ROW_CONV + 2 * i + 1:ROW_CONV + 2 * i + 2, 0:3 * GDN_WIDTH - W]], axis=1)
                for i in range(GDN_CONV)]
        conv = jnp.concatenate(conv, axis=0)

        def fc_rows(r0):
            rows = [jnp.concatenate([red[r0 + 3 * i + k:r0 + 3 * i + k + 1, 0:min(W, D_FF - k * W)] for k in range(3)], axis=1)
                    for i in range(FFN_CONV)]
            return jnp.concatenate(rows, axis=0)

        fc = jnp.concatenate([fc_rows(ROW_FCG), fc_rows(ROW_FCU)], axis=1)

        def chip_block(full, width):
            out = None
            for j in range(N_CHIPS):
                term = jnp.where(q == j, full[:, width * j:width * (j + 1)], 0.0)
                out = term if out is None else out + term
            return out

        grads = [red[0:1, :], red[1:2, :], red[2:3, :], red[3:4, 0:4], red[3:4, 4:8], red[4:5, 0:128],
                 chip_block(conv, cw), chip_block(fc, fw)]
        for k in range(n):
            d_, m_, v_ = _adamw_math(w_refs[k][...], grads[k], m_refs[k][...], v_refs[k][...])
            outs[4 * k][...] = grads[k]
            outs[4 * k + 1][...] = d_
            outs[4 * k + 2][...] = m_
            outs[4 * k + 3][...] = v_

    full = lambda a: pl.BlockSpec(a.shape, lambda i, s_, nd=len(a.shape): (0,) * nd)
    arrays = [small_all, small, *ws, *ms, *vs]
    out_shapes = [jax.ShapeDtypeStruct((8, 128), F32)] + [jax.ShapeDtypeStruct(w.shape, F32) for w in ws for _ in range(4)]
    gs = pltpu.PrefetchScalarGridSpec(
        num_scalar_prefetch=1, grid=(1,), in_specs=[full(a) for a in arrays],
        out_specs=[pl.BlockSpec(o.shape, lambda i, s_, nd=len(o.shape): (0,) * nd) for o in out_shapes])
    return pl.pallas_call(body, name="small_step", grid_spec=gs, out_shape=out_shapes)(meq, *arrays)


def _pad_lanes(v, n=D_MODEL):
    return jnp.pad(v, ((0, 0), (0, n - v.shape[1])))


def kernel(x, norm1_w, w_in, conv_qkv_w, a_log, dt_bias, gdn_norm_w, w_out, norm2_w, w_up, ffn_conv_w, w_down, final_norm_w, loss_target, m_norm1_w, m_w_in, m_conv_qkv_w, m_a_log, m_dt_bias, m_gdn_norm_w, m_w_out, m_norm2_w, m_w_up, m_ffn_conv_w, m_w_down, m_final_norm_w, v_norm1_w, v_w_in, v_conv_qkv_w, v_a_log, v_dt_bias, v_gdn_norm_w, v_w_out, v_norm2_w, v_w_up, v_ffn_conv_w, v_w_down, v_final_norm_w):
    c = lax.axis_index("c")
    q = 2 * lax.axis_index("x") + lax.axis_index("y")
    S = x.shape[1]
    cq = jnp.stack([c, q]).astype(jnp.int32)

    *in_started, in_token = _gather_halves_start([_narrowed_shard(w_in), conv_qkv_w[0], ffn_conv_w[0]], x, "gather_in_start")
    w_in_l, m_w_in_l, v_w_in_l = (jnp.swapaxes(a + in_token[0:1, 0:1], 1, 2)[0] for a in (w_in, m_w_in, v_w_in))
    h1 = _rmsnorm_fwd(x[0], norm1_w, "norm1", after=[in_token])
    rest = [(a[0] + in_token[0:1, 0:1]).astype(_MXU) for a in (w_out, w_up, w_down)]
    in_shards, got_in = _gather_halves_wait(in_started, [w_in_l, m_w_in_l, v_w_in_l, h1, *rest], "gather_in_wait")
    (g_in, g_conv, g_fconv), (w_in_l, m_w_in_l, v_w_in_l) = _place_own(
        in_shards, _sibling_fill(got_in, "fill_in"), cq, "place_in", carry=[w_in_l, m_w_in_l, v_w_in_l])
    *rest_started, token = _gather_halves_start(rest, g_conv, "gather_rest_start")

    rest_state = {}

    def rest_arrived(after):
        rest_state["shards"], got = _gather_halves_wait(rest_started, after, "gather_rest_wait")
        *rest_state["fill"], tok = _sibling_fill_start(got, "fill_rest_start")
        return tok

    def rest_filled(after):
        got = _sibling_fill_wait(rest_state["fill"], after, "fill_rest_wait")
        g_out, g_up, g_down = _place_own(rest_state["shards"], got, cq, "place_rest")
        return g_out.reshape(D_MODEL, D_MODEL), g_up, g_down.reshape(D_FF, D_MODEL)

    rest_weights = (rest_arrived, rest_filled)
    wp = _wp_assemble(g_in, [token])
    conv_f = jnp.concatenate([g_conv[i] for i in range(N_CHIPS)], axis=1)
    fcw = jnp.concatenate([g_fconv[i] for i in range(N_CHIPS)], axis=1)
    gp = _pad_lanes(jnp.concatenate([a_log, dt_bias], axis=1), 128)
    fnw = final_norm_w[None, :]
    early = {}

    early_names = ("w_up", "w_down", "w_out")

    def early_sibling(d_wup, d_wdown, d_wout):
        *early["sibling"], tok = _grad_sibling_start(
            [d_wup, d_wdown.reshape(N_CHIPS, D_FF // N_CHIPS, D_MODEL), d_wout.reshape(N_CHIPS, D_MODEL // N_CHIPS, D_MODEL)],
            "grad_sibling_early_start")
        return tok

    def early_chips(after):
        fams_e, got_e = _grad_sibling_wait(early["sibling"], after, "grad_sibling_early_wait")
        early["parts"] = [_add_sibling(f, r, cq, "add_sibling_" + nm) for f, r, nm in zip(fams_e, got_e, early_names)]
        *early["started"], tok = _grad_chips_start([p[1] for p in early["parts"]], "grad_chips_start")
        return tok

    def late_sibling(d_wp):
        *early["late_sibling"], tok = _grad_sibling_start([d_wp[None]], "grad_sibling_late_start")
        return tok

    loss_l, dx, g = _local_step(x[0], loss_target[0], h1, norm1_w, norm2_w, fnw, gp, gdn_norm_w, wp,
                                conv_f, fcw, rest_weights, (early_sibling, early_chips, late_sibling))
    fams, got = _grad_sibling_wait(early["late_sibling"], [dx], "grad_sibling_late_wait")
    late_part = _add_sibling_split(fams[0], got[0], cq, "add_sibling_w_in")
    *late_started, late_token = _grad_chips_start([late_part[1]], "grad_chips_late_start")
    small = _pack_small(g["n1w"], g["n2w"], g["fnw"], g["gp"], g["gnw"], g["conv_w"], g["fcw_g"], g["fcw_u"], loss_l)
    *small_started, small_token = _grad_chips_start([], "small_gather_start", small)
    got3_e = _grad_chips_wait(early["started"], [late_token, small_token], "grad_chips_wait")
    g_w_up, g_w_down, g_w_out = _grad_share(
        [_add_chips(p[0], r3, cq, "add_chips_" + nm) for p, r3, nm in zip(early["parts"], got3_e, early_names)],
        "grad_share_early")
    big = {}

    def adamw_big(nm, w, gg, m, v):
        d_, m_, v_ = _adamw(w[0], gg, m[0], v[0], "adamw_" + nm)
        big[nm] = (gg[None], d_[None], m_[None], v_[None])

    adamw_big("w_up", w_up, g_w_up, m_w_up, v_w_up)
    adamw_big("w_down", w_down, g_w_down, m_w_down, v_w_down)
    adamw_big("w_out", w_out, g_w_out, m_w_out, v_w_out)
    got3, = _grad_chips_wait(late_started, [big[nm][1] for nm in early_names], "grad_chips_late_wait")
    g_w_in, = _grad_share([_add_chips(late_part[0], got3, cq, "add_chips_w_in")], "grad_share_late")
    g_t, d_t, m_t, v_t = _adamw_transposed(w_in_l, g_w_in, m_w_in_l, v_w_in_l, "adamw_w_in")
    big["w_in"] = tuple(jnp.swapaxes(t[None], 1, 2) for t in (g_t, d_t, m_t, v_t))
    small_all, small = _grad_chips_wait(small_started, [d_t], "small_gather_wait", with_small=True)
    small_names = ["norm1_w", "norm2_w", "final_norm_w", "a_log", "dt_bias", "gdn_norm_w", "conv_qkv_w", "ffn_conv_w"]
    loss_b, *small_out = _small_step(
        jnp.stack([2 * q + c, q]).astype(jnp.int32), small_all, small,
        [norm1_w, norm2_w, final_norm_w[None], a_log, dt_bias, gdn_norm_w, conv_qkv_w[0], ffn_conv_w[0]],
        [m_norm1_w, m_norm2_w, m_final_norm_w[None], m_a_log, m_dt_bias, m_gdn_norm_w, m_conv_qkv_w[0], m_ffn_conv_w[0]],
        [v_norm1_w, v_norm2_w, v_final_norm_w[None], v_a_log, v_dt_bias, v_gdn_norm_w, v_conv_qkv_w[0], v_ffn_conv_w[0]])
    like = dict(final_norm_w=lambda t: t[0], conv_qkv_w=lambda t: t[None], ffn_conv_w=lambda t: t[None])
    for k, nm in enumerate(small_names):
        big[nm] = tuple(like.get(nm, lambda t: t)(t) for t in small_out[4 * k:4 * k + 4])
    names = ["norm1_w", "w_in", "conv_qkv_w", "a_log", "dt_bias", "gdn_norm_w", "w_out", "norm2_w", "w_up",
             "ffn_conv_w", "w_down", "final_norm_w"]
    return (loss_b[0, 0], dx[None], *[big[n][0] for n in names], *[big[n][1] for n in names],
            *[big[n][2] for n in names], *[big[n][3] for n in names])
```

```python
import functools
import math

import numpy as np
import jax
import jax.numpy as jnp
from jax import lax
from jax.experimental import pallas as pl
from jax.experimental.pallas import tpu as pltpu

F32 = jnp.float32
BF16 = jnp.bfloat16
_MXU = jnp.bfloat16
_HI = lax.Precision.HIGHEST
EPS = 1e-6
V7X_VMEM_LIMIT = 56 * 1024 * 1024
MESH = pl.DeviceIdType.MESH

D_MODEL = 1024
GDN_HEADS, GDN_DIM, GDN_CHUNK, GDN_CONV = 4, 128, 64, 4
GDN_WIDTH = GDN_HEADS * GDN_DIM
DIL_HEADS, DIL_DIM = 8, 64
DIL_WIDTH = DIL_HEADS * DIL_DIM
D_FF, FFN_CONV = 2816, 3
FFN_STRIP = 128
IN_COLS = 3592
P_COLS = 3840
P_Z, P_QKVB, P_BA = 1536, 2048, 3584
ATT_T = 1024
ADAM_LR, ADAM_B1, ADAM_B2, ADAM_EPS, ADAM_WD, ADAM_STEP = 0.001, 0.9, 0.999, 1e-08, 0.01, 10
N_CHIPS = 4


def _cparams(sem=None, vmem=None):
    kw = {}
    if sem is not None:
        kw["dimension_semantics"] = sem
    if vmem is not None:
        kw["vmem_limit_bytes"] = vmem
    return pltpu.CompilerParams(**kw)


def _silu(x):
    return x * jax.nn.sigmoid(x)


def _pick_tile(n, cap):
    best = None
    for t in range(128, min(n, cap) + 1, 128):
        if n % t == 0:
            best = t
    return best or n


def _mm(a, b, mode, *, out_dtype=F32, residual=None, name, b_blocks=False, place=None, into=None, tn=None, after=(),
        normed_by=None):
    if mode == "nn":
        M, K = a.shape
        N = b.shape[0] * b.shape[2] if b_blocks else b.shape[1]
    elif mode == "nt":
        (M, K), (N, _) = a.shape, b.shape
    else:
        (K, M), (_, N) = a.shape, b.shape
    tm = _pick_tile(M, 1024)
    tn = b.shape[2] if b_blocks else (tn or _pick_tile(N, 1536))

    def vmem(tm, tn):
        return 2 * (tm * K * a.dtype.itemsize + tn * K * b.dtype.itemsize
                    + tm * tn * (jnp.dtype(out_dtype).itemsize + (4 if residual is not None else 0))) + 3 * tm * tn * 4

    fixed_tn = b_blocks or (place is not None and place[0] == "blocks")
    while vmem(tm, tn) > 40 * 1024 * 1024:
        if (tm >= tn or fixed_tn) and tm % 256 == 0:
            tm //= 2
        elif tn % 256 == 0 and not fixed_tn:
            tn //= 2
        else:
            tm //= 2
    a_spec = pl.BlockSpec((K, tm), lambda j, i: (0, i)) if mode == "tn" else pl.BlockSpec((tm, K), lambda j, i: (i, 0))
    if b_blocks:
        b_spec = pl.BlockSpec((None, K, tn), lambda j, i: (j, 0, 0))
    else:
        b_spec = pl.BlockSpec((tn, K), lambda j, i: (j, 0)) if mode == "nt" else pl.BlockSpec((K, tn), lambda j, i: (0, j))
    r_spec = pl.BlockSpec((tm, tn), lambda j, i: (i, j))
    if place is None:
        o_spec, o_shape = r_spec, (M, N)
    elif place[0] == "rows":
        off = place[2] // tm
        o_spec, o_shape = pl.BlockSpec((tm, tn), lambda j, i: (i + off, j)), (place[1], N)
    else:
        off = place[2]
        o_spec, o_shape = pl.BlockSpec((None, tm, tn), lambda j, i: (j + off, i, 0)), (place[1], M, tn)
    dims = {"nn": (((1,), (0,)), ((), ())), "nt": (((1,), (1,)), ((), ())), "tn": (((0,), (0,)), ((), ()))}[mode]

    def body(*refs):
        a_ref, b_ref = refs[0], refs[1]
        o_ref = refs[-1] if normed_by is None else refs[-2]
        acc = lax.dot_general(a_ref[...].astype(_MXU), b_ref[...].astype(_MXU), dims, preferred_element_type=F32)
        if residual is not None:
            acc = acc + refs[2][...]
        o_ref[...] = acc.astype(out_dtype)
        if normed_by is not None:
            rs = lax.rsqrt(jnp.mean(acc * acc, axis=-1, keepdims=True) + EPS)
            refs[-1][...] = (acc * rs * refs[len(ins0)][...]).astype(_MXU)

    ins, specs, alias = [a, b], [a_spec, b_spec], {}
    if residual is not None:
        ins.append(residual)
        specs.append(r_spec)
    ins0 = list(ins)
    if normed_by is not None:
        assert tn == N and place is None and into is None
        ins.append(normed_by)
        specs.append(pl.BlockSpec((1, N), lambda j, i: (0, 0)))
    if into is not None:
        alias = {len(ins): 0}
        ins.append(into)
        specs.append(pl.BlockSpec(memory_space=pl.ANY))
    ins += list(after)
    specs += [pl.BlockSpec(memory_space=pl.ANY)] * len(after)
    o_shape = jax.ShapeDtypeStruct(o_shape, out_dtype)
    if normed_by is not None:
        o_spec, o_shape = (o_spec, r_spec), (o_shape, jax.ShapeDtypeStruct((M, N), _MXU))
    return pl.pallas_call(
        body, name=name, grid=(N // tn, M // tm), in_specs=specs, out_specs=o_spec,
        out_shape=o_shape, input_output_aliases=alias,
        compiler_params=_cparams(("parallel", "parallel"), V7X_VMEM_LIMIT),
    )(*ins)


def _wp_assemble(g_in, after=()):
    nb, Dm, Wb = g_in.shape
    T = 256
    n_lo = P_QKVB - 2 * Wb

    def body(g_ref, *rest):
        g2 = g_ref[2]
        rest[-1][...] = jnp.concatenate(
            [g_ref[0], g_ref[1], g2[:, :n_lo], g2[:, n_lo + 8:], g_ref[3], g2[:, n_lo:n_lo + 8],
             jnp.zeros((T, P_COLS - P_BA - 8), g_in.dtype)], axis=1)

    return pl.pallas_call(
        body, name="wp_assemble", grid=(Dm // T,),
        in_specs=[pl.BlockSpec((nb, T, Wb), lambda i: (0, i, 0))] + [pl.BlockSpec(memory_space=pl.ANY)] * len(after),
        out_specs=pl.BlockSpec((T, P_COLS), lambda i: (i, 0)), out_shape=jax.ShapeDtypeStruct((Dm, P_COLS), g_in.dtype),
        compiler_params=_cparams(("parallel",)),
    )(g_in, *after)


def _rmsnorm_fwd(x, w, name, after=()):
    S, D = x.shape
    T = _pick_tile(S, 512)

    def body(x_ref, w_ref, *rest):
        xv = x_ref[...]
        rs = lax.rsqrt(jnp.mean(xv * xv, axis=-1, keepdims=True) + EPS)
        rest[-1][...] = (xv * rs * w_ref[...]).astype(rest[-1].dtype)

    return pl.pallas_call(
        body, name=name, grid=(S // T,),
        in_specs=[pl.BlockSpec((T, D), lambda i: (i, 0)), pl.BlockSpec((1, D), lambda i: (0, 0))] + [_ANY] * len(after),
        out_specs=pl.BlockSpec((T, D), lambda i: (i, 0)),
        out_shape=jax.ShapeDtypeStruct((S, D), _MXU),
        compiler_params=_cparams(("parallel",)),
    )(x, w, *after)


def _rmsnorm_bwd(dh, x, w, dres, name, after=()):
    S, D = x.shape
    pair = isinstance(dh, tuple)
    T = _pick_tile(S, 256 if pair else 512)
    dhs = [*dh[0], dh[1]] if pair else [dh]

    def body(*refs):
        x_ref, w_ref, dres_ref = refs[len(dhs):len(dhs) + 3]
        dx_ref, dw_ref = refs[-2:]
        xv = x_ref[...]
        rs = lax.rsqrt(jnp.mean(xv * xv, axis=-1, keepdims=True) + EPS)
        xn = xv * rs
        if pair:
            b_ref = refs[len(dhs) - 1]
            nb, _, Kb = b_ref.shape
            per = nb // (len(dhs) - 1)
            dhv = None
            for blk in range(nb):
                lo = (blk % per) * Kb
                t = lax.dot_general(refs[blk // per][:, lo:lo + Kb].astype(_MXU), b_ref[blk].astype(_MXU), (_NT, ((), ())),
                                    preferred_element_type=F32)
                dhv = t if dhv is None else dhv + t
        else:
            dhv = refs[0][...]
        dxn = dhv * w_ref[...]
        dx_ref[...] = dres_ref[...] + rs * (dxn - xn * jnp.mean(dxn * xn, axis=-1, keepdims=True))

        @pl.when(pl.program_id(0) == 0)
        def _():
            dw_ref[...] = jnp.zeros_like(dw_ref)

        dw_ref[...] += jnp.sum(dhv * xn, axis=0, keepdims=True)

    row = pl.BlockSpec((T, D), lambda i: (i, 0))
    vec = pl.BlockSpec((1, D), lambda i: (0, 0))
    dh_specs = [row] if not pair else (
        [pl.BlockSpec((T, a.shape[1]), lambda i: (i, 0)) for a in dh[0]] + [pl.BlockSpec(dh[1].shape, lambda i: (0, 0, 0))])
    return pl.pallas_call(
        body, name=name, grid=(S // T,), in_specs=dh_specs + [row, vec, row] + [_ANY] * len(after), out_specs=(row, vec),
        out_shape=(jax.ShapeDtypeStruct((S, D), F32), jax.ShapeDtypeStruct((1, D), F32)),
        compiler_params=_cparams(("arbitrary",), V7X_VMEM_LIMIT if pair else None),
    )(*dhs, x, w, dres, *after)


def _loss_head(x3, w, tgt, name):
    S, D = tgt.shape
    fused = isinstance(x3, tuple)
    T = _pick_tile(S, 256 if fused else 512)
    xs = list(x3) if fused else [x3]

    def body(*refs):
        w_ref, t_ref = refs[len(xs):len(xs) + 2]
        loss_ref, dx_ref, dxn_ref, dw_ref = refs[-4:]
        xv = refs[0][...]
        if fused:
            xv = lax.dot_general(xv.astype(_MXU), refs[1][...].astype(_MXU), (_NN, ((), ())),
                                 preferred_element_type=F32) + refs[2][...]
        rs = lax.rsqrt(jnp.mean(xv * xv, axis=-1, keepdims=True) + EPS)
        xn = xv * rs
        err = xn * w_ref[...] - t_ref[...]
        dy = err * (1.0 / D)
        dxn = dy * w_ref[...]
        dxv = rs * (dxn - xn * jnp.mean(dxn * xn, axis=-1, keepdims=True))
        dx_ref[...] = dxv
        dxn_ref[...] = dxv.astype(dxn_ref.dtype)

        @pl.when(pl.program_id(0) == 0)
        def _():
            dw_ref[...] = jnp.zeros_like(dw_ref)
            loss_ref[...] = jnp.zeros_like(loss_ref)

        dw_ref[...] += jnp.sum(dy * xn, axis=0, keepdims=True)
        part = jnp.sum(jnp.sum(err * err, axis=-1, keepdims=True), axis=0, keepdims=True) * (0.5 / D)
        loss_ref[...] += jnp.broadcast_to(part, loss_ref.shape)

    row = pl.BlockSpec((T, D), lambda i: (i, 0))
    vec = pl.BlockSpec((1, D), lambda i: (0, 0))
    x_specs = [row] if not fused else [pl.BlockSpec((T, xs[0].shape[1]), lambda i: (i, 0)),
                                       pl.BlockSpec(xs[1].shape, lambda i: (0, 0)), row]
    return pl.pallas_call(
        body, name=name, grid=(S // T,), in_specs=x_specs + [vec, row],
        out_specs=(pl.BlockSpec((8, 128), lambda i: (0, 0)), row, row, vec),
        out_shape=(jax.ShapeDtypeStruct((8, 128), F32), jax.ShapeDtypeStruct((S, D), F32), jax.ShapeDtypeStruct((S, D), _MXU),
                   jax.ShapeDtypeStruct((1, D), F32)),
        compiler_params=_cparams(("arbitrary",), V7X_VMEM_LIMIT if fused else None),
    )(*xs, w, tgt)


def _shifted(ext, back, lo, n):
    if back == 0:
        return ext[lo:lo + n, :]
    return pltpu.roll(ext, back % ext.shape[0], 0)[lo:lo + n, :]


def _conv_windows(ext, K, T):
    return [_shifted(ext, (K - 1) - i, 8, T) for i in range(K)]


def _conv_taps(ext, w, K, T):
    out = None
    for i, win in enumerate(_conv_windows(ext, K, T)):
        term = win * w[i:i + 1, :]
        out = term if out is None else out + term
    return out


def _conv_taps_t(ext, w, K, T):
    out = None
    for i in range(K):
        term = _shifted(ext, i - (K - 1), 0, T) * w[i:i + 1, :]
        out = term if out is None else out + term
    return out


def _tri_masks(C):
    r = lax.broadcasted_iota(jnp.int32, (C, C), 0)
    c = lax.broadcasted_iota(jnp.int32, (C, C), 1)
    return r == c, r >= c, r > c, r <= c


_NN, _NT, _TN = ((1,), (0,)), ((1,), (1,)), ((0,), (0,))
_GDN_PASSES = dict(qk=1, inv=1, sol=1, scan=1, bwd=1)


def _bdot_raw(a, b, kind, passes):
    dims = ({"NN": ((2,), (1,)), "NT": ((2,), (2,)), "TN": ((1,), (1,))}[kind], ((0,), (0,)))
    if passes == 0:
        return lax.dot_general(a, b, dims, precision=_HI, preferred_element_type=F32)
    ah, bh = a.astype(BF16), b.astype(BF16)
    out = lax.dot_general(ah, bh, dims, preferred_element_type=F32)
    if passes == 3:
        al, bl = (a - ah.astype(F32)).astype(BF16), (b - bh.astype(F32)).astype(BF16)
        out = out + lax.dot_general(ah, bl, dims, preferred_element_type=F32) + lax.dot_general(al, bh, dims, preferred_element_type=F32)
    return out


@functools.partial(jax.custom_vjp, nondiff_argnums=(2, 3))
def _bdot(a, b, kind, passes):
    return _bdot_raw(a, b, kind, passes)


def _bdot_fwd(a, b, kind, passes):
    return _bdot_raw(a, b, kind, passes), (a, b)


def _bdot_bwd(kind, passes, res, ct):
    a, b = res
    if kind == "NN":
        return _bdot_raw(ct, b, "NT", passes), _bdot_raw(a, ct, "TN", passes)
    if kind == "NT":
        return _bdot_raw(ct, b, "NN", passes), _bdot_raw(ct, a, "TN", passes)
    return _bdot_raw(b, ct, "NT", passes), _bdot_raw(a, ct, "NN", passes)


_bdot.defvjp(_bdot_fwd, _bdot_bwd)


def _softplus(x):
    return jnp.maximum(x, 0.0) + jnp.log(1.0 + jnp.exp(-jnp.abs(x)))


def _gdn_stage1(cq, ck, cv, b_col, a_col, alog, dtb, dot=_bdot_raw):
    C = cq.shape[1]
    eye, incl, strict, incl_t = _tri_masks(C)
    qn = cq * lax.rsqrt(jnp.sum(cq * cq, axis=-1, keepdims=True) + EPS) * (GDN_DIM ** -0.5)
    kn = ck * lax.rsqrt(jnp.sum(ck * ck, axis=-1, keepdims=True) + EPS)
    beta = jax.nn.sigmoid(b_col)
    g = -jnp.exp(alog) * _softplus(a_col + dtb)
    g_row = jnp.sum(jnp.where(eye, g, 0.0), axis=1, keepdims=True)
    beta_row = jnp.sum(jnp.where(eye, beta, 0.0), axis=1, keepdims=True)
    gc_col = jnp.sum(jnp.where(incl, g_row, 0.0), axis=2, keepdims=True)
    gc_row = jnp.sum(jnp.where(incl_t, g, 0.0), axis=1, keepdims=True)
    dec = jnp.where(incl, jnp.exp(jnp.where(incl, gc_col - gc_row, 0.0)), 0.0)
    kk = dot(kn, kn, "NT", _GDN_PASSES["qk"])
    qk = dot(qn, kn, "NT", _GDN_PASSES["qk"])
    lmat = jnp.where(strict, dec * kk * beta_row, 0.0)
    attn = dec * qk * beta_row
    gam = jnp.exp(gc_col)
    gc_last = gc_col[:, C - 1:C, :]
    k_end = kn * (jnp.exp(gc_last - gc_col) * beta)
    return lmat, cv, gam * kn, gam * qn, attn, k_end, jnp.exp(gc_last)


def _tri_inv(lmat):
    C = lmat.shape[1]
    eye = _tri_masks(C)[0]
    ps = _GDN_PASSES["inv"]
    p = jnp.where(eye, 1.0, 0.0) - lmat
    lp = _bdot_raw(lmat, lmat, "NN", ps)
    n = int(math.log2(C))
    for s in range(1, n):
        p = p + _bdot_raw(p, lp, "NN", ps)
        if s < n - 1:
            lp = _bdot_raw(lp, lp, "NN", ps)
    return p


def _gated_norm(o, z, gnw):
    on = o * lax.rsqrt(jnp.mean(o * o, axis=-1, keepdims=True) + EPS) * gnw
    return on * _silu(z)


GDN_PG = 4
GDN_SG = 4


def _gdn_pairs(c, ba, gp, G):
    C, W, H = GDN_CHUNK, GDN_WIDTH, GDN_HEADS
    pairs = [(j, h) for j in range(G) for h in range(H)]
    cq, ck, cv = (jnp.stack([c[C * j:C * (j + 1), o + GDN_DIM * h:o + GDN_DIM * (h + 1)] for j, h in pairs]) for o in (0, W, 2 * W))
    b_col = jnp.stack([ba[C * j:C * (j + 1), h:h + 1] for j, h in pairs])
    a_col = jnp.stack([ba[C * j:C * (j + 1), H + h:H + h + 1] for j, h in pairs])
    alog = jnp.stack([gp[0:1, h:h + 1] for j, h in pairs])
    dtb = jnp.stack([gp[0:1, H + h:H + h + 1] for j, h in pairs])
    return pairs, (cq, ck, cv, b_col, a_col, alog, dtb)


def _gdn_pre_specs(S, G):
    C = GDN_CHUNK
    T = C * G
    return dict(
        cur=pl.BlockSpec((T, 3 * GDN_WIDTH), lambda i: (i, 0)),
        prev=pl.BlockSpec((8, 3 * GDN_WIDTH), lambda i: (jnp.maximum(i * (T // 8) - 1, 0), 0)),
        ba=pl.BlockSpec((T, 128), lambda i: (i, P_BA // 128)),
        cw=pl.BlockSpec((GDN_CONV, 3 * GDN_WIDTH), lambda i: (0, 0)),
        vec=pl.BlockSpec((1, 128), lambda i: (0, 0)),
        hd=pl.BlockSpec((GDN_HEADS, T, GDN_DIM), lambda i: (0, i, 0)),
        hc=pl.BlockSpec((GDN_HEADS, T, C), lambda i: (0, i, 0)),
        ge=pl.BlockSpec((G, GDN_HEADS, 8, 128), lambda i: (i, 0, 0, 0)),
    )


def _hd_shape(S, last=GDN_DIM):
    return jax.ShapeDtypeStruct((GDN_HEADS, S, last), F32)


def _gdn_pre(proj, conv_w, gp):
    S = proj.shape[0]
    C, G = GDN_CHUNK, GDN_PG
    nc = S // C
    sp = _gdn_pre_specs(S, G)

    def body(cur_ref, prev_ref, ba_ref, cw_ref, gp_ref, uv_ref, wk_ref, qd_ref, ke_ref, at_ref, ti_ref, ge_ref):
        prev = prev_ref[...] * jnp.where(pl.program_id(0) == 0, 0.0, 1.0)
        c = _silu(_conv_taps(jnp.concatenate([prev, cur_ref[...]], axis=0), cw_ref[...], GDN_CONV, C * G))
        pairs, args = _gdn_pairs(c, ba_ref[...], gp_ref[...], G)
        lmat, v, rk, q_dec, attn, k_end, g_end = _gdn_stage1(*args)
        t = _tri_inv(lmat)
        u_v = _bdot_raw(t, v, "NN", _GDN_PASSES["sol"])
        w_k = _bdot_raw(t, rk, "NN", _GDN_PASSES["sol"])
        for b, (j, h) in enumerate(pairs):
            rows = slice(C * j, C * (j + 1))
            uv_ref[h, rows, :] = u_v[b]
            wk_ref[h, rows, :] = w_k[b]
            qd_ref[h, rows, :] = q_dec[b]
            ke_ref[h, rows, :] = k_end[b]
            at_ref[h, rows, :] = attn[b]
            ti_ref[h, rows, :] = t[b]
            ge_ref[j, h] = jnp.broadcast_to(g_end[b], (8, 128))

    return pl.pallas_call(
        body, name="gdn_pre", grid=(nc // G,),
        in_specs=[sp["cur"], sp["prev"], sp["ba"], sp["cw"], sp["vec"]],
        out_specs=(sp["hd"], sp["hd"], sp["hd"], sp["hd"], sp["hc"], sp["hc"], sp["ge"]),
        out_shape=(_hd_shape(S), _hd_shape(S), _hd_shape(S), _hd_shape(S), _hd_shape(S, C), _hd_shape(S, C),
                   jax.ShapeDtypeStruct((nc, GDN_HEADS, 8, 128), F32)),
        compiler_params=_cparams(("parallel",)),
    )(proj, proj, proj, conv_w, gp)


def _gdn_scan_specs(S, G, rev):
    C = GDN_CHUNK
    T = C * G
    n = S // T
    ci = (lambda i: n - 1 - i) if rev else (lambda i: i)
    return dict(
        hd=pl.BlockSpec((GDN_HEADS, T, GDN_DIM), lambda i: (0, ci(i), 0)),
        hc=pl.BlockSpec((GDN_HEADS, T, C), lambda i: (0, ci(i), 0)),
        ge=pl.BlockSpec((G, GDN_HEADS, 8, 128), lambda i: (ci(i), 0, 0, 0)),
        z=pl.BlockSpec((T, GDN_WIDTH), lambda i: (ci(i), P_Z // GDN_WIDTH)),
        oa=pl.BlockSpec((T, GDN_WIDTH), lambda i: (ci(i), 0)),
        vec=pl.BlockSpec((1, 128), lambda i: (0, 0)),
        st=pl.BlockSpec((G, GDN_HEADS, GDN_DIM, GDN_DIM), lambda i: (ci(i), 0, 0, 0)),
    )


def _gdn_scan(u_v, w_k, q_dec, k_end, attn, g_end, proj, gnw, mix, after=()):
    S = proj.shape[0]
    C, G = GDN_CHUNK, GDN_SG
    nc = S // C
    sp = _gdn_scan_specs(S, G, False)
    ps = _GDN_PASSES["scan"]

    def body(uv_ref, wk_ref, qd_ref, ke_ref, at_ref, ge_ref, z_ref, gnw_ref, *rest):
        oa_ref, st_ref, s_scr = rest[-3:]

        @pl.when(pl.program_id(0) == 0)
        def _():
            s_scr[...] = jnp.zeros_like(s_scr)

        for j in range(G):
            rows = slice(C * j, C * (j + 1))
            st = s_scr[...]
            st_ref[j] = st
            u = uv_ref[:, rows, :] - _bdot_raw(wk_ref[:, rows, :], st, "NN", ps)
            o = _bdot_raw(qd_ref[:, rows, :], st, "NN", ps) + _bdot_raw(at_ref[:, rows, :], u, "NN", ps)
            s_scr[...] = ge_ref[j][:, 0:1, 0:1] * st + _bdot_raw(ke_ref[:, rows, :], u, "TN", ps)
            for h in range(GDN_HEADS):
                cols = slice(GDN_DIM * h, GDN_DIM * (h + 1))
                oa_ref[rows, cols] = _gated_norm(o[h], z_ref[rows, cols], gnw_ref[...])

    return pl.pallas_call(
        body, name="gdn_scan", grid=(nc // G,),
        in_specs=[sp["hd"], sp["hd"], sp["hd"], sp["hd"], sp["hc"], sp["ge"], sp["z"], sp["vec"]] + [_ANY] * (1 + len(after)),
        out_specs=(sp["oa"], sp["st"]),
        out_shape=(jax.ShapeDtypeStruct(mix.shape, F32),
                   jax.ShapeDtypeStruct((nc, GDN_HEADS, GDN_DIM, GDN_DIM), F32)),
        input_output_aliases={8: 0},
        scratch_shapes=[pltpu.VMEM((GDN_HEADS, GDN_DIM, GDN_DIM), F32)],
        compiler_params=_cparams(("arbitrary",)),
    )(u_v, w_k, q_dec, k_end, attn, g_end, proj, gnw, mix, *after)


def _gdn_scan_bwd(u_v, w_k, q_dec, k_end, attn, g_end, proj, gnw, states, d_oa):
    S = proj.shape[0]
    C, G = GDN_CHUNK, GDN_SG
    nc = S // C
    sp = _gdn_scan_specs(S, G, True)
    ps, pb = _GDN_PASSES["scan"], _GDN_PASSES["bwd"]

    def body(uv_ref, wk_ref, qd_ref, ke_ref, at_ref, ge_ref, z_ref, gnw_ref, st_ref, doa_ref,
             duv_ref, dwk_ref, dqd_ref, dke_ref, dat_ref, dge_ref, dz_ref, dgnw_ref, ds_scr):
        @pl.when(pl.program_id(0) == 0)
        def _():
            ds_scr[...] = jnp.zeros_like(ds_scr)
            dgnw_ref[...] = jnp.zeros_like(dgnw_ref)

        dgnw = jnp.zeros((1, 128), F32)
        for j in reversed(range(G)):
            rows = slice(C * j, C * (j + 1))
            st = st_ref[j]
            wk, qd, ke, at = wk_ref[:, rows, :], qd_ref[:, rows, :], ke_ref[:, rows, :], at_ref[:, rows, :]
            u = uv_ref[:, rows, :] - _bdot_raw(wk, st, "NN", ps)
            o = _bdot_raw(qd, st, "NN", ps) + _bdot_raw(at, u, "NN", ps)
            dos = []
            for h in range(GDN_HEADS):
                cols = slice(GDN_DIM * h, GDN_DIM * (h + 1))
                _, vjp2 = jax.vjp(_gated_norm, o[h], z_ref[rows, cols], gnw_ref[...])
                do_h, dz_h, dgn = vjp2(doa_ref[rows, cols])
                dz_ref[rows, cols] = dz_h
                dgnw = dgnw + dgn
                dos.append(do_h)
            do = jnp.stack(dos)
            ds_new = ds_scr[...]
            du = _bdot_raw(at, do, "TN", pb) + _bdot_raw(ke, ds_new, "NN", pb)
            duv_ref[:, rows, :] = du
            dat_ref[:, rows, :] = _bdot_raw(do, u, "NT", pb)
            dqd_ref[:, rows, :] = _bdot_raw(do, st, "NT", pb)
            dke_ref[:, rows, :] = _bdot_raw(u, ds_new, "NT", pb)
            dwk_ref[:, rows, :] = -_bdot_raw(du, st, "NT", pb)
            d_ge = jnp.sum(jnp.sum(st * ds_new, axis=2, keepdims=True), axis=1, keepdims=True)
            dge_ref[j] = jnp.broadcast_to(d_ge, (GDN_HEADS, 8, 128))
            ds_scr[...] = ge_ref[j][:, 0:1, 0:1] * ds_new + _bdot_raw(qd, do, "TN", pb) - _bdot_raw(wk, du, "TN", pb)
        dgnw_ref[...] += dgnw

    return pl.pallas_call(
        body, name="gdn_scan_bwd", grid=(nc // G,),
        in_specs=[sp["hd"], sp["hd"], sp["hd"], sp["hd"], sp["hc"], sp["ge"], sp["z"], sp["vec"], sp["st"], sp["oa"]],
        out_specs=(sp["hd"], sp["hd"], sp["hd"], sp["hd"], sp["hc"], sp["ge"], sp["oa"], sp["vec"]),
        out_shape=(_hd_shape(S), _hd_shape(S), _hd_shape(S), _hd_shape(S), _hd_shape(S, C),
                   jax.ShapeDtypeStruct((nc, GDN_HEADS, 8, 128), F32), jax.ShapeDtypeStruct((S, GDN_WIDTH), F32),
                   jax.ShapeDtypeStruct((1, 128), F32)),
        scratch_shapes=[pltpu.VMEM((GDN_HEADS, GDN_DIM, GDN_DIM), F32)],
        compiler_params=_cparams(("arbitrary",)),
    )(u_v, w_k, q_dec, k_end, attn, g_end, proj, gnw, states, d_oa)


def _gdn_post(proj, conv_w, gp, tinv, u_v, w_k, d_uv, d_wk, d_qd, d_ke, d_at, d_ge):
    S = proj.shape[0]
    C, G = GDN_CHUNK, GDN_PG
    nc = S // C
    sp = _gdn_pre_specs(S, G)
    pb = _GDN_PASSES["bwd"]

    def body(cur_ref, prev_ref, ba_ref, cw_ref, gp_ref, ti_ref, uv_ref, wk_ref, duv_ref, dwk_ref, dqd_ref, dke_ref,
             dat_ref, dge_ref, dpre_ref, dba_ref, dgp_ref):
        i = pl.program_id(0)

        @pl.when(i == 0)
        def _():
            dgp_ref[...] = jnp.zeros_like(dgp_ref)

        prev = prev_ref[...] * jnp.where(i == 0, 0.0, 1.0)
        pre = _conv_taps(jnp.concatenate([prev, cur_ref[...]], axis=0), cw_ref[...], GDN_CONV, C * G)
        sg = jax.nn.sigmoid(pre)
        dsilu = sg * (1.0 + pre * (1.0 - sg))
        pairs, args = _gdn_pairs(pre * sg, ba_ref[...], gp_ref[...], G)
        _, vjp1 = jax.vjp(functools.partial(_gdn_stage1, dot=_bdot), *args)

        def take(ref):
            return jnp.stack([ref[h, C * j:C * (j + 1), :] for j, h in pairs])

        t, u_v, w_k = take(ti_ref), take(uv_ref), take(wk_ref)
        d_v = _bdot_raw(t, take(duv_ref), "TN", pb)
        d_rk = _bdot_raw(t, take(dwk_ref), "TN", pb)
        d_l = -(_bdot_raw(d_v, u_v, "NT", pb) + _bdot_raw(d_rk, w_k, "NT", pb))
        d_ge = jnp.stack([dge_ref[j, h][0:1, 0:1] for j, h in pairs])
        dcq, dck, dcv, db, da, dalog, ddtb = vjp1((d_l, d_v, d_rk, take(dqd_ref), take(dat_ref), take(dke_ref), d_ge))
        lane = lax.broadcasted_iota(jnp.int32, (C, 128), 1)
        lane1 = lax.broadcasted_iota(jnp.int32, (1, 128), 1)
        dgp = jnp.zeros((1, 128), F32)
        for j in range(G):
            rows = slice(C * j, C * (j + 1))
            dba = jnp.zeros((C, 128), F32)
            for h in range(GDN_HEADS):
                b = GDN_HEADS * j + h
                for o_, dcx in ((0, dcq), (GDN_WIDTH, dck), (2 * GDN_WIDTH, dcv)):
                    cols = slice(o_ + GDN_DIM * h, o_ + GDN_DIM * (h + 1))
                    dpre_ref[rows, cols] = dcx[b] * dsilu[rows, cols]
                dba = dba + jnp.where(lane == h, db[b], 0.0) + jnp.where(lane == GDN_HEADS + h, da[b], 0.0)
                dgp = dgp + jnp.where(lane1 == h, dalog[b], 0.0) + jnp.where(lane1 == GDN_HEADS + h, ddtb[b], 0.0)
            dba_ref[rows, :] = dba
        dgp_ref[0:1, :] += dgp

    T = C * G
    return pl.pallas_call(
        body, name="gdn_post", grid=(nc // G,),
        in_specs=[sp["cur"], sp["prev"], sp["ba"], sp["cw"], sp["vec"], sp["hc"], sp["hd"], sp["hd"], sp["hd"], sp["hd"],
                  sp["hd"], sp["hd"], sp["hc"], sp["ge"]],
        out_specs=(sp["cur"], pl.BlockSpec((T, 128), lambda i: (i, 0)), pl.BlockSpec((8, 128), lambda i: (0, 0))),
        out_shape=(jax.ShapeDtypeStruct((S, 3 * GDN_WIDTH), F32), jax.ShapeDtypeStruct((S, 128), F32),
                   jax.ShapeDtypeStruct((8, 128), F32)),
        compiler_params=_cparams(("arbitrary",)),
    )(proj, proj, proj, conv_w, gp, tinv, u_v, w_k, d_uv, d_wk, d_qd, d_ke, d_at, d_ge)


def _conv_bwd(dpre, x, xcol0, w, K, name, tc):
    S, Cc = dpre.shape
    T = _pick_tile(S, 256)
    nt, ncol = S // T, Cc // tc
    xo = xcol0 // tc

    def body(d_ref, dn_ref, x_ref, xp_ref, w_ref, dx_ref, dw_ref):
        i = pl.program_id(1)
        dn = dn_ref[...] * jnp.where(i == nt - 1, 0.0, 1.0)
        dv = d_ref[...]
        ext_d = jnp.concatenate([dv, dn], axis=0)
        dx_ref[...] = _conv_taps_t(ext_d, w_ref[...], K, T).astype(dx_ref.dtype)
        xp = xp_ref[...] * jnp.where(i == 0, 0.0, 1.0)
        ext_x = jnp.concatenate([xp, x_ref[...]], axis=0)

        @pl.when(i == 0)
        def _():
            dw_ref[...] = jnp.zeros_like(dw_ref)

        for k in range(K):
            dw_ref[k:k + 1, :] += jnp.sum(dv * _shifted(ext_x, (K - 1) - k, 8, T), axis=0, keepdims=True)

    r8 = T // 8
    return pl.pallas_call(
        body, name=name, grid=(ncol, nt),
        in_specs=[pl.BlockSpec((T, tc), lambda j, i: (i, j)),
                  pl.BlockSpec((8, tc), lambda j, i: (jnp.minimum((i + 1) * r8, S // 8 - 1), j)),
                  pl.BlockSpec((T, tc), lambda j, i: (i, j + xo)),
                  pl.BlockSpec((8, tc), lambda j, i: (jnp.maximum(i * r8 - 1, 0), j + xo)),
                  pl.BlockSpec((K, tc), lambda j, i: (0, j))],
        out_specs=(pl.BlockSpec((T, tc), lambda j, i: (i, j)), pl.BlockSpec((K, tc), lambda j, i: (0, j))),
        out_shape=(jax.ShapeDtypeStruct((S, Cc), _MXU), jax.ShapeDtypeStruct((K, Cc), F32)),
        compiler_params=_cparams(("parallel", "arbitrary")),
    )(dpre, dpre, x, x, w)


def _dil_bias(nt, T):
    d = (np.arange(nt)[:, None, None] * T + np.arange(T)[None, None, :] - np.arange(T)[None, :, None])
    cnt = ((d >= 0) & (d <= 128)).astype(np.float64) + ((d >= 0) & (d % 4 == 0) & (d <= 512)) + ((d >= 0) & (d % 16 == 0))
    return jnp.asarray(np.where(cnt > 0, np.log(np.maximum(cnt, 1.0)), -1e30), dtype=F32)


def _attn_fwd(proj, after=()):
    S = proj.shape[0]
    T = min(ATT_T, S)
    nt, H = S // T, T // 2
    bias = _dil_bias(nt, T)
    scale = DIL_DIM ** -0.5
    npair = DIL_WIDTH // 128
    qb0, kb0, vb0 = P_QKVB // 128, (P_QKVB + DIL_WIDTH) // 128, (P_QKVB + 2 * DIL_WIDTH) // 128

    def body(q_ref, k_ref, v_ref, b_ref, *rest):
        o_ref, lse_ref = rest[-2:]
        i = pl.program_id(1)
        qs = (q_ref[...] * scale).astype(_MXU)

        def update(carry, kt, vt, qt, bt):
            out = []
            for hh in range(2):
                m, l, acc = carry[hh]
                sl = slice(hh * DIL_DIM, (hh + 1) * DIL_DIM)
                s = lax.dot_general(kt[:, sl], qt[:, sl], (_NT, ((), ())), preferred_element_type=F32) + bt
                m_new = jnp.maximum(m, jnp.max(s, axis=0, keepdims=True))
                p = jnp.exp(s - m_new)
                a = jnp.exp(m - m_new)
                l = a * l + jnp.sum(p, axis=0, keepdims=True)
                acc = a * acc + lax.dot_general(vt[:, sl], p.astype(_MXU), (_TN, ((), ())), preferred_element_type=F32)
                out.append((m_new, l, acc))
            return tuple(out)

        def keys(j):
            rows = pl.ds(pl.multiple_of(j * T, T), T)
            return k_ref[rows, :].astype(_MXU), v_ref[rows, :].astype(_MXU)

        init = tuple((jnp.full((1, T), -1e30, F32), jnp.zeros((1, T), F32), jnp.zeros((DIL_DIM, T), F32)) for _ in range(2))
        res = lax.fori_loop(0, i, lambda j, carry: update(carry, *keys(j), qs, b_ref[i - j]), init)
        kd, vd = keys(i)
        res = update(res, kd[:H], vd[:H], qs, b_ref[0, :H, :])
        late = update(tuple(tuple(t[:, H:] for t in r) for r in res), kd[H:], vd[H:], qs[H:], b_ref[0, H:, H:])
        res = tuple(tuple(jnp.concatenate([t[:, :H], u], axis=1) for t, u in zip(r, r2)) for r, r2 in zip(res, late))
        lse_ref[...] = jnp.zeros_like(lse_ref)
        for hh in range(2):
            m, l, acc = res[hh]
            o_ref[:, hh * DIL_DIM:(hh + 1) * DIL_DIM] = (acc / l).T
            lse_ref[hh:hh + 1, :] = m + jnp.log(l)

    return pl.pallas_call(
        body, name="attn_fwd", grid=(npair, nt),
        in_specs=[pl.BlockSpec((T, 128), lambda p, i: (i, qb0 + p)),
                  pl.BlockSpec((S, 128), lambda p, i: (0, kb0 + p)),
                  pl.BlockSpec((S, 128), lambda p, i: (0, vb0 + p)),
                  pl.BlockSpec((nt, T, T), lambda p, i: (0, 0, 0))] + [_ANY] * len(after),
        out_specs=(pl.BlockSpec((T, 128), lambda p, i: (i, GDN_WIDTH // 128 + p)),
                   pl.BlockSpec((None, None, 8, T), lambda p, i: (p, i, 0, 0))),
        out_shape=(jax.ShapeDtypeStruct((S, GDN_WIDTH + DIL_WIDTH), F32), jax.ShapeDtypeStruct((npair, nt, 8, T), F32)),
        compiler_params=_cparams(("parallel", "parallel")),
    )(proj, proj, proj, bias, *after)


def _attn_bwd(proj, mix, lse, d_mix):
    S = proj.shape[0]
    T = min(ATT_T, S)
    nt, H = S // T, T // 2
    bias = _dil_bias(nt, T)
    scale = DIL_DIM ** -0.5
    npair = DIL_WIDTH // 128
    qb0, kb0, vb0 = P_QKVB // 128, (P_QKVB + DIL_WIDTH) // 128, (P_QKVB + 2 * DIL_WIDTH) // 128

    def body(q_ref, k_ref, v_ref, o_ref, lse_ref, do_ref, b_ref, dq_ref, dk_ref, dv_ref, dq_scr):
        j = pl.program_id(1)

        @pl.when(j == 0)
        def _():
            dq_scr[...] = jnp.zeros_like(dq_scr)

        kt = k_ref[...].astype(_MXU)
        vt = v_ref[...].astype(_MXU)
        ones = jnp.ones((8, DIL_DIM), F32)

        def block(carry, kt, vt, rows, lsev, bt):
            qs = (q_ref[rows, :] * scale).astype(_MXU)
            dov = do_ref[rows, :]
            prod = dov * o_ref[rows, :]
            dob = dov.astype(_MXU)
            out = []
            dqs = []
            for hh in range(2):
                dk, dv = carry[hh]
                sl = slice(hh * DIL_DIM, (hh + 1) * DIL_DIM)
                s = lax.dot_general(kt[:, sl], qs[:, sl], (_NT, ((), ())), preferred_element_type=F32) + bt
                p = jnp.exp(s - lsev[hh:hh + 1, :])
                delta = lax.dot_general(ones, prod[:, sl], (_NT, ((), ())), precision=_HI, preferred_element_type=F32)[0:1, :]
                dp = lax.dot_general(vt[:, sl], dob[:, sl], (_NT, ((), ())), preferred_element_type=F32)
                ds = (p * (dp - delta)).astype(_MXU)
                dv = dv + lax.dot_general(p.astype(_MXU), dob[:, sl], (_NN, ((), ())), preferred_element_type=F32)
                dk = dk + lax.dot_general(ds, qs[:, sl], (_NN, ((), ())), preferred_element_type=F32)
                dqs.append(lax.dot_general(ds, kt[:, sl], (_TN, ((), ())), preferred_element_type=F32) * scale)
                out.append((dk, dv))
            dq_scr[rows, :] += jnp.concatenate(dqs, axis=1)
            return tuple(out)

        def step(i, carry):
            return block(carry, kt, vt, pl.ds(pl.multiple_of(i * T, T), T), lse_ref[i], b_ref[i - j])

        zeros = tuple((jnp.zeros((H, DIL_DIM), F32), jnp.zeros((H, DIL_DIM), F32)) for _ in range(2))
        lsed = lse_ref[j]
        early = block(zeros, kt[:H], vt[:H], pl.ds(pl.multiple_of(j * T, T), T), lsed, b_ref[0, :H, :])
        late = block(zeros, kt[H:], vt[H:], pl.ds(pl.multiple_of(j * T + H, H), H), lsed[:, H:], b_ref[0, H:, H:])
        init = tuple(tuple(jnp.concatenate([t, u], axis=0) for t, u in zip(r, r2)) for r, r2 in zip(early, late))
        res = lax.fori_loop(j + 1, nt, step, init)
        dk_ref[...] = jnp.concatenate([res[0][0], res[1][0]], axis=1).astype(dk_ref.dtype)
        dv_ref[...] = jnp.concatenate([res[0][1], res[1][1]], axis=1).astype(dv_ref.dtype)

        @pl.when(j == nt - 1)
        def _():
            dq_ref[...] = dq_scr[...].astype(dq_ref.dtype)

    full = lambda c0: pl.BlockSpec((S, 128), lambda p, j: (0, c0 + p))
    tile = lambda c0: pl.BlockSpec((T, 128), lambda p, j: (j, c0 + p))
    out3 = jax.ShapeDtypeStruct((S, DIL_WIDTH), _MXU)
    return pl.pallas_call(
        body, name="attn_bwd", grid=(npair, nt),
        in_specs=[full(qb0), tile(kb0), tile(vb0), full(GDN_WIDTH // 128),
                  pl.BlockSpec((None, nt, 8, T), lambda p, j: (p, 0, 0, 0)), full(GDN_WIDTH // 128),
                  pl.BlockSpec((nt, T, T), lambda p, j: (0, 0, 0))],
        out_specs=(full(0), tile(0), tile(0)),
        out_shape=(out3, out3, out3),
        scratch_shapes=[pltpu.VMEM((S, 128), F32)],
        compiler_params=_cparams(("parallel", "arbitrary")),
    )(proj, proj, proj, mix, lse, d_mix, bias)


def _ffn_act(up, cw):
    S, Cc = up.shape[0], up.shape[1] // 2
    T, tc = _pick_tile(S, 256), _pick_tile(Cc, 1536)
    r16 = T // 16
    nct = Cc // tc

    def body(g_ref, gp_ref, u_ref, up_ref, wg_ref, wu_ref, o_ref):
        keep = jnp.where(pl.program_id(1) == 0, 0.0, 1.0)

        def strip(c, carry):
            cs = pl.ds(pl.multiple_of(c * FFN_STRIP, FFN_STRIP), FFN_STRIP)
            cg = _conv_taps(jnp.concatenate([gp_ref[8:16, cs].astype(F32) * keep, g_ref[:, cs].astype(F32)], axis=0),
                            wg_ref[:, cs], FFN_CONV, T)
            cu = _conv_taps(jnp.concatenate([up_ref[8:16, cs].astype(F32) * keep, u_ref[:, cs].astype(F32)], axis=0),
                            wu_ref[:, cs], FFN_CONV, T)
            o_ref[:, cs] = (_silu(cg) * cu).astype(o_ref.dtype)
            return carry

        lax.fori_loop(0, tc // FFN_STRIP, strip, 0)

    cur = lambda o: pl.BlockSpec((T, tc), lambda j, i: (i, j + o))
    prev = lambda o: pl.BlockSpec((16, tc), lambda j, i: (jnp.maximum(i * r16 - 1, 0), j + o))
    wsp = lambda o: pl.BlockSpec((FFN_CONV, tc), lambda j, i: (0, j + o))
    return pl.pallas_call(
        body, name="ffn_act", grid=(nct, S // T),
        in_specs=[cur(0), prev(0), cur(nct), prev(nct), wsp(0), wsp(nct)], out_specs=cur(0),
        out_shape=jax.ShapeDtypeStruct((S, Cc), _MXU),
        compiler_params=_cparams(("parallel", "parallel")),
    )(up, up, up, up, cw, cw)


def _ffn_act_bwd(d_act, up, cw):
    S, Cc = up.shape[0], up.shape[1] // 2
    T, tc = _pick_tile(S, 256), _pick_tile(Cc, 1536)
    r8, r16 = T // 8, T // 16
    nt = S // T
    nct = Cc // tc
    K = FFN_CONV

    def body(da_ref, dan_ref, g_ref, gp_ref, gn_ref, u_ref, up_ref, un_ref, wg_ref, wu_ref,
             dg_ref, du_ref, dwg_ref, dwu_ref):
        i = pl.program_id(1)
        keep_p = jnp.where(i == 0, 0.0, 1.0)
        keep_n = jnp.where(i == nt - 1, 0.0, 1.0)

        @pl.when(i == 0)
        def _():
            dwg_ref[...] = jnp.zeros_like(dwg_ref)
            dwu_ref[...] = jnp.zeros_like(dwu_ref)

        def strip(c, carry):
            cs = pl.ds(pl.multiple_of(c * FFN_STRIP, FFN_STRIP), FFN_STRIP)
            wg, wu = wg_ref[:, cs], wu_ref[:, cs]
            xg = jnp.concatenate([gp_ref[8:16, cs].astype(F32) * keep_p, g_ref[:, cs].astype(F32),
                                  gn_ref[0:8, cs].astype(F32) * keep_n], axis=0)
            xu = jnp.concatenate([up_ref[8:16, cs].astype(F32) * keep_p, u_ref[:, cs].astype(F32),
                                  un_ref[0:8, cs].astype(F32) * keep_n], axis=0)
            cg = _conv_taps(xg, wg, K, T + 8)
            cu = _conv_taps(xu, wu, K, T + 8)
            da = jnp.concatenate([da_ref[:, cs], dan_ref[:, cs] * keep_n], axis=0)
            sg = jax.nn.sigmoid(cg)
            d_cg = da * cu * (sg * (1.0 + cg * (1.0 - sg)))
            d_cu = da * (cg * sg)
            dg_ref[:, cs] = _conv_taps_t(d_cg, wg, K, T).astype(dg_ref.dtype)
            du_ref[:, cs] = _conv_taps_t(d_cu, wu, K, T).astype(du_ref.dtype)
            for k in range(K):
                dwg_ref[k:k + 1, cs] += jnp.sum(d_cg[0:T, :] * _shifted(xg, (K - 1) - k, 8, T), axis=0, keepdims=True)
                dwu_ref[k:k + 1, cs] += jnp.sum(d_cu[0:T, :] * _shifted(xu, (K - 1) - k, 8, T), axis=0, keepdims=True)
            return carry

        lax.fori_loop(0, tc // FFN_STRIP, strip, 0)

    cur = lambda o: pl.BlockSpec((T, tc), lambda j, i: (i, j + o))
    prev = lambda o: pl.BlockSpec((16, tc), lambda j, i: (jnp.maximum(i * r16 - 1, 0), j + o))
    nxt = lambda o: pl.BlockSpec((16, tc), lambda j, i: (jnp.minimum((i + 1) * r16, S // 16 - 1), j + o))
    nxt8 = pl.BlockSpec((8, tc), lambda j, i: (jnp.minimum((i + 1) * r8, S // 8 - 1), j))
    wsp = lambda o: pl.BlockSpec((K, tc), lambda j, i: (0, j + o))
    return pl.pallas_call(
        body, name="ffn_act_bwd", grid=(nct, nt),
        in_specs=[cur(0), nxt8, cur(0), prev(0), nxt(0), cur(nct), prev(nct), nxt(nct), wsp(0), wsp(nct)],
        out_specs=(cur(0), cur(0), wsp(0), wsp(0)),
        out_shape=(jax.ShapeDtypeStruct((S, Cc), _MXU), jax.ShapeDtypeStruct((S, Cc), _MXU),
                   jax.ShapeDtypeStruct((K, Cc), F32), jax.ShapeDtypeStruct((K, Cc), F32)),
        compiler_params=_cparams(("parallel", "arbitrary")),
    )(d_act, d_act, up, up, up, up, up, up, cw, cw)


def _local_step(x, tgt, h1, n1w, n2w, fnw, gp, gnw, wp, conv_w, fcw, rest_weights, early_grads):
    proj = _mm(h1, wp, "nn", name="proj")
    u_v, w_k, q_dec, k_end, attn, tinv, g_end = _gdn_pre(proj, conv_w, gp)
    mix, lse = _attn_fwd(proj)
    mix, states = _gdn_scan(u_v, w_k, q_dec, k_end, attn, g_end, proj, gnw, mix, after=[rest_weights[0]([mix])])
    w_out, w_up4, w_down = rest_weights[1]([mix])
    x2, h2 = _mm(mix, w_out, "nn", residual=x, name="outproj", normed_by=n2w)
    up = _mm(h2, w_up4, "nn", b_blocks=True, out_dtype=_MXU, name="up")
    act = _ffn_act(up, fcw)
    loss, dx3, dx3n, d_fnw = _loss_head((act, w_down, x2), fnw, tgt, "loss_head")
    d_act = _mm(dx3n, w_down, "nt", name="d_act")
    d_wdown = _mm(act, dx3n, "tn", name="d_wdown")
    d_upg, d_upu, d_fcwg, d_fcwu = _ffn_act_bwd(d_act, up, fcw)
    d_wup = _mm(h2, d_upg, "tn", place=("blocks", N_CHIPS, 0), tn=w_up4.shape[2], name="d_wgate")
    d_wup = _mm(h2, d_upu, "tn", place=("blocks", N_CHIPS, N_CHIPS // 2), tn=w_up4.shape[2], into=d_wup, name="d_wup")
    dx2, d_n2w = _rmsnorm_bwd(([d_upg, d_upu], w_up4), x2, n2w, dx3, "norm2_bwd")
    d_wout = _mm(mix, dx2, "tn", name="d_wout")
    token = early_grads[0](d_wup, d_wdown, d_wout)
    d_mix = _mm(dx2, w_out, "nt", name="d_mix", after=[token])
    dq_b, dk_b, dv_b = _attn_bwd(proj, mix, lse, d_mix)
    d_uv, d_wk, d_qd, d_ke, d_at, d_ge, d_z, d_gnw = _gdn_scan_bwd(u_v, w_k, q_dec, k_end, attn, g_end, proj,
                                                                   gnw, states, d_mix)
    d_pre, d_ba, d_gp = _gdn_post(proj, conv_w, gp, tinv, u_v, w_k, d_uv, d_wk, d_qd, d_ke, d_at, d_ge)
    token = early_grads[1]([d_pre])
    d_qkva, d_convw = _conv_bwd(d_pre, proj, 0, conv_w + token[0:1, 0:1], GDN_CONV, "gdn_conv_bwd", 512)
    d_proj = jnp.concatenate([d_qkva, d_z.astype(_MXU), dq_b, dk_b, dv_b, d_ba.astype(_MXU),
                              jnp.zeros((x.shape[0], P_COLS - P_BA - 128), _MXU)], axis=1)
    d_wp = _mm(h1, d_proj, "tn", name="d_wp")
    token = early_grads[2](d_wp)
    dx, d_n1w = _rmsnorm_bwd(([d_proj], wp[None]), x, n1w, dx2, "norm1_bwd", after=[token])
    grads = dict(wp=d_wp, conv_w=d_convw, w_out=d_wout, w_up=d_wup, fcw_g=d_fcwg, fcw_u=d_fcwu, w_down=d_wdown,
                 n1w=d_n1w, n2w=d_n2w, fnw=d_fnw, gp=d_gp, gnw=d_gnw)
    return loss, dx, grads


_HBM = pl.BlockSpec(memory_space=pltpu.HBM)


def _pos():
    return lax.axis_index("x"), lax.axis_index("y"), lax.axis_index("c")


def _other_chips(x, y):
    return [(1 - x, y), (x, 1 - y), (1 - x, 1 - y)]


def _halvable(shape):
    return shape[0] % 32 == 0


def _rows_of_half(shape, half):
    if not _halvable(shape):
        return pl.ds(0, shape[0])
    return pl.ds(pl.multiple_of(half * (shape[0] // 2), 16), shape[0] // 2)


_SEM = pl.BlockSpec(memory_space=pltpu.SEMAPHORE)
_ANY = pl.BlockSpec(memory_space=pl.ANY)
_DATAFLOW = pltpu.SideEffectType.DATAFLOW_SIDE_EFFECTING


def _in_hbm(a):
    return pltpu.with_memory_space_constraint(a, pltpu.HBM)


def _halves_copy(src_refs, land_refs, send_sems, recv_sems, shapes, a, j, block, x, y, c):
    px, py = _other_chips(x, y)[j]
    rows = _rows_of_half(shapes[a], c)
    return pltpu.make_async_remote_copy(
        src_ref=src_refs[a].at[rows, :], dst_ref=land_refs[a].at[block, rows, :], send_sem=send_sems.at[3 * a + j],
        recv_sem=recv_sems.at[3 * a + j], device_id=(px, py, c), device_id_type=MESH)


def _gather_halves_start(shards, after, name):
    n = len(shards)
    shapes = [s.shape for s in shards]

    def body(*refs):
        ins, lands = refs[:n], refs[n:2 * n]
        send_sems, recv_sems = refs[2 * n + 1], refs[2 * n + 2]
        token = refs[-1]
        x, y, c = _pos()
        q = 2 * x + y
        for a in range(n):
            for j in range(3):
                _halves_copy(ins, lands, send_sems, recv_sems, shapes, a, j, q, x, y, c).start()
        token[...] = jnp.zeros_like(token)

    land_shapes = [(N_CHIPS,) + s.shape for s in shards]
    return pl.pallas_call(
        body, name=name,
        out_shape=(pltpu.SemaphoreType.DMA((3 * n,)), pltpu.SemaphoreType.DMA((3 * n,)),
                   *[pltpu.HBM(s.shape, s.dtype) for s in shards],
                   *[pltpu.HBM(ls, s.dtype) for ls, s in zip(land_shapes, shards)],
                   jax.ShapeDtypeStruct((8, 128), F32)),
        in_specs=[_HBM] * (2 * n) + [_ANY],
        out_specs=(_SEM, _SEM, *[_HBM] * (2 * n), pl.BlockSpec(memory_space=pltpu.VMEM)),
        input_output_aliases={a: 2 + a for a in range(2 * n)},
        compiler_params=pltpu.CompilerParams(has_side_effects=_DATAFLOW),
    )(*[_in_hbm(s) for s in shards], *[_in_hbm(lax.empty(ls, s.dtype)) for ls, s in zip(land_shapes, shards)], after)


def _gather_halves_wait(started, after, name):
    send_sems, recv_sems, *thru = started
    n = len(thru) // 2
    shapes = [t.shape for t in thru[:n]]

    def body(*refs):
        ins, lands = refs[:n], refs[n:2 * n]
        send_sems, recv_sems = refs[2 * n], refs[2 * n + 1]
        x, y, c = _pos()
        q = 2 * x + y
        chips = _other_chips(x, y)
        for a in range(n):
            for j, (px, py) in enumerate(chips):
                _halves_copy(ins, lands, send_sems, recv_sems, shapes, a, j, q, x, y, c).wait_send()
                _halves_copy(ins, lands, send_sems, recv_sems, shapes, a, j, 2 * px + py, x, y, c).wait_recv()

    outs = pl.pallas_call(
        body, name=name, out_shape=[pltpu.HBM(t.shape, t.dtype) for t in thru],
        in_specs=[_HBM] * (2 * n) + [_SEM, _SEM] + [_ANY] * len(after), out_specs=[_HBM] * (2 * n),
        input_output_aliases={a: a for a in range(2 * n)},
        compiler_params=pltpu.CompilerParams(has_side_effects=_DATAFLOW),
    )(*thru, send_sems, recv_sems, *after)
    return outs[:n], outs[n:]


def _sibling_fill(gathered, name):
    big = [a for a, g in enumerate(gathered) if _halvable(g.shape[1:])]
    n = len(gathered)

    def body(*refs):
        ins, outs = refs[:n], refs[n:2 * n]
        send_sems, recv_sems = refs[2 * n:]
        x, y, c = _pos()
        chips = _other_chips(x, y)

        def copy(k, j, half):
            a = big[k]
            px, py = chips[j]
            rows = _rows_of_half(gathered[a].shape[1:], half)
            return pltpu.make_async_remote_copy(
                src_ref=ins[a].at[2 * px + py, rows, :], dst_ref=outs[a].at[2 * px + py, rows, :],
                send_sem=send_sems.at[3 * k + j], recv_sem=recv_sems.at[3 * k + j],
                device_id=(x, y, 1 - c), device_id_type=MESH)

        sends = [copy(k, j, c) for k in range(len(big)) for j in range(3)]
        for cp in sends:
            cp.start()
        for k in range(len(big)):
            for j in range(3):
                copy(k, j, 1 - c).wait_recv()
        for cp in sends:
            cp.wait_send()

    return pl.pallas_call(
        body, name=name, in_specs=[_HBM] * n, out_specs=[_HBM] * n,
        out_shape=[jax.ShapeDtypeStruct(g.shape, g.dtype) for g in gathered],
        input_output_aliases={a: a for a in range(n)},
        scratch_shapes=[pltpu.SemaphoreType.DMA((3 * len(big),)), pltpu.SemaphoreType.DMA((3 * len(big),))],
    )(*gathered)


def _fill_copy(refs, send_sems, recv_sems, shapes, a, j, half, x, y, c):
    px, py = _other_chips(x, y)[j]
    rows = _rows_of_half(shapes[a], half)
    return pltpu.make_async_remote_copy(
        src_ref=refs[a].at[2 * px + py, rows, :], dst_ref=refs[a].at[2 * px + py, rows, :],
        send_sem=send_sems.at[3 * a + j], recv_sem=recv_sems.at[3 * a + j],
        device_id=(x, y, 1 - c), device_id_type=MESH)


def _sibling_fill_start(gathered, name):
    n = len(gathered)
    shapes = [g.shape[1:] for g in gathered]

    def body(*refs):
        ins = refs[:n]
        send_sems, recv_sems = refs[n], refs[n + 1]
        token = refs[-1]
        x, y, c = _pos()
        for a in range(n):
            for j in range(3):
                _fill_copy(ins, send_sems, recv_sems, shapes, a, j, c, x, y, c).start()
        token[...] = jnp.zeros_like(token)

    return pl.pallas_call(
        body, name=name,
        out_shape=(pltpu.SemaphoreType.DMA((3 * n,)), pltpu.SemaphoreType.DMA((3 * n,)),
                   *[pltpu.HBM(g.shape, g.dtype) for g in gathered], jax.ShapeDtypeStruct((8, 128), F32)),
        in_specs=[_HBM] * n,
        out_specs=(_SEM, _SEM, *[_HBM] * n, pl.BlockSpec(memory_space=pltpu.VMEM)),
        input_output_aliases={a: 2 + a for a in range(n)},
        compiler_params=pltpu.CompilerParams(has_side_effects=_DATAFLOW),
    )(*[_in_hbm(g) for g in gathered])


def _sibling_fill_wait(started, after, name):
    send_sems, recv_sems, *thru = started
    n = len(thru)
    shapes = [t.shape[1:] for t in thru]

    def body(*refs):
        ins = refs[:n]
        send_sems, recv_sems = refs[n], refs[n + 1]
        x, y, c = _pos()
        for a in range(n):
            for j in range(3):
                _fill_copy(ins, send_sems, recv_sems, shapes, a, j, c, x, y, c).wait_send()
                _fill_copy(ins, send_sems, recv_sems, shapes, a, j, 1 - c, x, y, c).wait_recv()

    return pl.pallas_call(
        body, name=name, out_shape=[pltpu.HBM(t.shape, t.dtype) for t in thru],
        in_specs=[_HBM] * n + [_SEM, _SEM] + [_ANY] * len(after), out_specs=[_HBM] * n,
        input_output_aliases={a: a for a in range(n)},
        compiler_params=pltpu.CompilerParams(has_side_effects=_DATAFLOW),
    )(*thru, send_sems, recv_sems, *after)


def _place_own(shards, gathered, cq, name, carry=()):
    n = len(shards)
    nc = len(carry)
    steps = 4

    def body(cq_ref, *refs):
        for a in range(n):
            refs[2 * n + nc + a][...] = refs[a][...]

    def tile(shape):
        return shape[0] // steps if _halvable(shape) else shape[0]

    in_specs = [pl.BlockSpec((tile(s.shape), s.shape[1]), (lambda i, s_: (i, 0)) if _halvable(s.shape) else (lambda i, s_: (0, 0)))
                for s in shards]
    in_specs += [pl.BlockSpec(memory_space=pl.ANY)] * (n + nc)
    out_specs = [pl.BlockSpec((None, tile(s.shape), s.shape[1]),
                              (lambda i, s_: (s_[1], i, 0)) if _halvable(s.shape) else (lambda i, s_: (s_[1], 0, 0)))
                 for s in shards]
    out_specs += [pl.BlockSpec(memory_space=pl.ANY)] * nc
    gs = pltpu.PrefetchScalarGridSpec(num_scalar_prefetch=1, grid=(steps,), in_specs=in_specs, out_specs=out_specs)
    outs = pl.pallas_call(
        body, name=name, grid_spec=gs,
        out_shape=[jax.ShapeDtypeStruct(g.shape, g.dtype) for g in gathered] + [jax.ShapeDtypeStruct(t.shape, t.dtype) for t in carry],
        input_output_aliases={1 + n + a: a for a in range(n + nc)},
        compiler_params=_cparams(("arbitrary",)),
    )(cq, *shards, *gathered, *carry)
    return (outs[:n], outs[n:]) if nc else outs


def _half_rows(ref, c, rh):
    return ref.at[:, pl.ds(pl.multiple_of(c * rh, 8), rh), :]


def _chips_copy(src_refs, land_refs, send_sems, recv_sems, a, j, x, y, c):
    px, py = _other_chips(x, y)[j]
    return pltpu.make_async_remote_copy(src_ref=src_refs[a].at[2 * px + py], dst_ref=land_refs[a].at[j],
                                        send_sem=send_sems.at[3 * a + j], recv_sem=recv_sems.at[3 * a + j],
                                        device_id=(px, py, c), device_id_type=MESH)


def _peer(r, x, y, c):
    return (x if r & 4 == 0 else 1 - x), (y if r & 2 == 0 else 1 - y), (c if r & 1 == 0 else 1 - c)


def _small_copy(small_ref, all_ref, send_sems, recv_sems, base, r, slot, x, y, c):
    return pltpu.make_async_remote_copy(src_ref=small_ref, dst_ref=all_ref.at[slot], send_sem=send_sems.at[base + r - 1],
                                        recv_sem=recv_sems.at[base + r - 1], device_id=_peer(r, x, y, c), device_id_type=MESH)


def _grad_chips_start(parts, name, small=None):
    n = len(parts)
    srcs = list(parts) + ([] if small is None else [small])
    m = len(srcs)

    def body(*refs):
        ins, lands = refs[:m], refs[m:2 * m]
        send_sems, recv_sems = refs[2 * m], refs[2 * m + 1]
        token = refs[-1]
        x, y, c = _pos()
        for a in range(n):
            for j in range(3):
                _chips_copy(ins, lands, send_sems, recv_sems, a, j, x, y, c).start()
        if small is not None:
            for r in range(1, 8):
                _small_copy(ins[n], lands[n], send_sems, recv_sems, 3 * n, r, 4 * x + 2 * y + c, x, y, c).start()
        token[...] = jnp.zeros_like(token)

    land_shapes = [(3,) + p.shape[1:] for p in parts] + ([] if small is None else [(8,) + small.shape])
    nsem = 3 * n + (0 if small is None else 7)
    return pl.pallas_call(
        body, name=name,
        out_shape=(pltpu.SemaphoreType.DMA((nsem,)), pltpu.SemaphoreType.DMA((nsem,)),
                   *[pltpu.HBM(p.shape, p.dtype) for p in srcs],
                   *[pltpu.HBM(ls, p.dtype) for ls, p in zip(land_shapes, srcs)],
                   jax.ShapeDtypeStruct((8, 128), F32)),
        in_specs=[_HBM] * (2 * m),
        out_specs=(_SEM, _SEM, *[_HBM] * (2 * m), pl.BlockSpec(memory_space=pltpu.VMEM)),
        input_output_aliases={a: 2 + a for a in range(2 * m)},
        compiler_params=pltpu.CompilerParams(has_side_effects=_DATAFLOW),
    )(*[_in_hbm(p) for p in srcs], *[_in_hbm(lax.empty(ls, p.dtype)) for ls, p in zip(land_shapes, srcs)])


def _grad_chips_wait(started, after, name, with_small=False):
    send_sems, recv_sems, *thru = started
    m = len(thru) // 2
    n = m - (1 if with_small else 0)

    def body(*refs):
        ins, lands = refs[:m], refs[m:2 * m]
        send_sems, recv_sems = refs[2 * m], refs[2 * m + 1]
        x, y, c = _pos()
        for a in range(n):
            for j in range(3):
                cp = _chips_copy(ins, lands, send_sems, recv_sems, a, j, x, y, c)
                cp.wait_send()
                cp.wait_recv()
        if with_small:
            for r in range(1, 8):
                px, py, pc = _peer(r, x, y, c)
                _small_copy(ins[n], lands[n], send_sems, recv_sems, 3 * n, r, 4 * x + 2 * y + c, x, y, c).wait_send()
                _small_copy(ins[n], lands[n], send_sems, recv_sems, 3 * n, r, 4 * px + 2 * py + pc, x, y, c).wait_recv()

    outs = pl.pallas_call(
        body, name=name, out_shape=[pltpu.HBM(t.shape, t.dtype) for t in thru],
        in_specs=[_HBM] * (2 * m) + [_SEM, _SEM] + [_ANY] * len(after), out_specs=[_HBM] * (2 * m),
        input_output_aliases={a: a for a in range(2 * m)},
        compiler_params=pltpu.CompilerParams(has_side_effects=_DATAFLOW),
    )(*thru, send_sems, recv_sems, *after)
    return list(outs[m:]) + list(outs[n:m])


def _sibling_copy(src_refs, land_refs, send_sems, recv_sems, rhs, a, c, x, y):
    return pltpu.make_async_remote_copy(src_ref=_half_rows(src_refs[a], 1 - c, rhs[a]), dst_ref=land_refs[a],
                                        send_sem=send_sems.at[a], recv_sem=recv_sems.at[a],
                                        device_id=(x, y, 1 - c), device_id_type=MESH)


def _grad_sibling_start(fams, name):
    n = len(fams)
    rhs = [f.shape[1] // 2 for f in fams]

    def body(*refs):
        ins, lands = refs[:n], refs[n:2 * n]
        send_sems, recv_sems = refs[2 * n], refs[2 * n + 1]
        token = refs[-1]
        x, y, c = _pos()
        for a in range(n):
            _sibling_copy(ins, lands, send_sems, recv_sems, rhs, a, c, x, y).start()
        token[...] = jnp.zeros_like(token)

    land_shapes = [(f.shape[0], f.shape[1] // 2, f.shape[2]) for f in fams]
    return pl.pallas_call(
        body, name=name,
        out_shape=(pltpu.SemaphoreType.DMA((n,)), pltpu.SemaphoreType.DMA((n,)),
                   *[pltpu.HBM(f.shape, f.dtype) for f in fams],
                   *[pltpu.HBM(ls, f.dtype) for ls, f in zip(land_shapes, fams)],
                   jax.ShapeDtypeStruct((8, 128), F32)),
        in_specs=[_HBM] * (2 * n),
        out_specs=(_SEM, _SEM, *[_HBM] * (2 * n), pl.BlockSpec(memory_space=pltpu.VMEM)),
        input_output_aliases={a: 2 + a for a in range(2 * n)},
        compiler_params=pltpu.CompilerParams(has_side_effects=_DATAFLOW),
    )(*[_in_hbm(f) for f in fams], *[_in_hbm(lax.empty(ls, f.dtype)) for ls, f in zip(land_shapes, fams)])


def _grad_sibling_wait(started, after, name):
    send_sems, recv_sems, *thru = started
    n = len(thru) // 2
    rhs = [t.shape[1] // 2 for t in thru[:n]]

    def body(*refs):
        ins, lands = refs[:n], refs[n:2 * n]
        send_sems, recv_sems = refs[2 * n], refs[2 * n + 1]
        x, y, c = _pos()
        for a in range(n):
            cp = _sibling_copy(ins, lands, send_sems, recv_sems, rhs, a, c, x, y)
            cp.wait_send()
            cp.wait_recv()

    outs = pl.pallas_call(
        body, name=name, out_shape=[pltpu.HBM(t.shape, t.dtype) for t in thru],
        in_specs=[_HBM] * (2 * n) + [_SEM, _SEM] + [_ANY] * len(after), out_specs=[_HBM] * (2 * n),
        input_output_aliases={a: a for a in range(2 * n)},
        compiler_params=pltpu.CompilerParams(has_side_effects=_DATAFLOW),
    )(*thru, send_sems, recv_sems, *after)
    return outs[:n], outs[n:]


def _grad_share(fulls, name):
    n = len(fulls)
    rhs = [f.shape[0] // 2 for f in fulls]

    def body(*refs):
        ins, outs = refs[:n], refs[n:2 * n]
        send_sems, recv_sems = refs[2 * n], refs[2 * n + 1]
        x, y, c = _pos()

        def copy(a, half):
            rows = pl.ds(pl.multiple_of(half * rhs[a], 8), rhs[a])
            return pltpu.make_async_remote_copy(src_ref=ins[a].at[rows, :], dst_ref=outs[a].at[rows, :],
                                                send_sem=send_sems.at[a], recv_sem=recv_sems.at[a],
                                                device_id=(x, y, 1 - c), device_id_type=MESH)

        sends = [copy(a, c) for a in range(n)]
        for cp in sends:
            cp.start()
        for a in range(n):
            copy(a, 1 - c).wait_recv()
        for cp in sends:
            cp.wait_send()

    return pl.pallas_call(
        body, name=name, in_specs=[_HBM] * n, out_specs=[_HBM] * n,
        out_shape=[jax.ShapeDtypeStruct(f.shape, f.dtype) for f in fulls],
        input_output_aliases={a: a for a in range(n)},
        scratch_shapes=[pltpu.SemaphoreType.DMA((n,)), pltpu.SemaphoreType.DMA((n,))],
    )(*fulls)


def _add_sibling(own, recv, cq, name):
    nb, R, Cc = own.shape
    Rh = R // 2

    def body(cq_ref, a_ref, b_ref, o32_ref, o16_ref):
        s = a_ref[0] + b_ref[0]
        mine = pl.program_id(0) == cq_ref[1]

        @pl.when(mine)
        def _():
            o32_ref[...] = s

        @pl.when(jnp.logical_not(mine))
        def _():
            o16_ref[0] = s.astype(o16_ref.dtype)

    sp = pl.BlockSpec((1, Rh, Cc), lambda b, s: (b, 0, 0))
    gs = pltpu.PrefetchScalarGridSpec(
        num_scalar_prefetch=1, grid=(nb,),
        in_specs=[pl.BlockSpec((1, Rh, Cc), lambda b, s: (b, s[0], 0)), sp],
        out_specs=[pl.BlockSpec((Rh, Cc), lambda b, s: (0, 0)), sp])
    return pl.pallas_call(
        body, name=name, grid_spec=gs,
        out_shape=[jax.ShapeDtypeStruct((Rh, Cc), F32), jax.ShapeDtypeStruct((nb, Rh, Cc), _MXU)],
        compiler_params=_cparams(("arbitrary",)),
    )(cq, own, recv)


def _add_sibling_split(d_wp, recv, cq, name):
    _, Dm, Pc = d_wp.shape
    Rh = Dm // 2
    Wb = IN_COLS // N_CHIPS
    T = 256

    def body(cq_ref, a_ref, b_ref, o32_ref, o16_ref):
        s = a_ref[0] + b_ref[0]
        blocks = [s[:, 0:Wb], s[:, Wb:2 * Wb],
                  jnp.concatenate([s[:, 2 * Wb:P_QKVB], s[:, P_BA:P_BA + 8], s[:, P_QKVB:3 * Wb - 8]], axis=1),
                  s[:, 3 * Wb - 8:P_BA]]
        q = cq_ref[1]
        own = None
        for j, blk in enumerate(blocks):
            term = jnp.where(q == j, blk, 0.0)
            own = term if own is None else own + term
            o16_ref[j] = blk.astype(o16_ref.dtype)
        o32_ref[...] = own

    gs = pltpu.PrefetchScalarGridSpec(
        num_scalar_prefetch=1, grid=(Rh // T,),
        in_specs=[pl.BlockSpec((1, T, Pc), lambda i, s: (0, s[0] * (Rh // T) + i, 0)), pl.BlockSpec((1, T, Pc), lambda i, s: (0, i, 0))],
        out_specs=[pl.BlockSpec((T, Wb), lambda i, s: (i, 0)), pl.BlockSpec((N_CHIPS, T, Wb), lambda i, s: (0, i, 0))])
    return pl.pallas_call(
        body, name=name, grid_spec=gs,
        out_shape=[jax.ShapeDtypeStruct((Rh, Wb), F32), jax.ShapeDtypeStruct((N_CHIPS, Rh, Wb), _MXU)],
        compiler_params=_cparams(("parallel",)),
    )(cq, d_wp, recv)


def _add_chips(part32, recv3, cq, name):
    Rh, Cc = part32.shape

    def body(cq_ref, a_ref, b_ref, o_ref):
        acc = a_ref[...]
        for j in range(3):
            acc = acc + b_ref[j].astype(F32)
        o_ref[...] = acc

    gs = pltpu.PrefetchScalarGridSpec(
        num_scalar_prefetch=1, grid=(1,),
        in_specs=[pl.BlockSpec((Rh, Cc), lambda i, s: (0, 0)), pl.BlockSpec((3, Rh, Cc), lambda i, s: (0, 0, 0))],
        out_specs=pl.BlockSpec((Rh, Cc), lambda i, s: (s[0], 0)))
    return pl.pallas_call(
        body, name=name, grid_spec=gs, out_shape=jax.ShapeDtypeStruct((2 * Rh, Cc), F32),
        compiler_params=_cparams(("arbitrary",)),
    )(cq, part32, recv3)


def _narrowed_shard(w):
    _, Dm, n = w.shape
    nj = Dm // 128
    pad = -n % 128

    def body(x_ref, o_ref):
        for j in range(nj):
            cols = jnp.concatenate([x_ref[pl.ds(j, n, stride=nj), :], jnp.zeros((pad, 128), F32)], axis=0)
            o_ref[128 * j:128 * (j + 1), :] = cols.T[:, :n].astype(_MXU)

    return pl.pallas_call(body, name="narrowed_shard", out_shape=jax.ShapeDtypeStruct((Dm, n), _MXU),
                          compiler_params=_cparams(vmem=V7X_VMEM_LIMIT))(jnp.swapaxes(w, 1, 2).reshape(n * nj, 128))


def _adamw_transposed(w, g, m, v, name):
    n, Dm = w.shape
    T = 128

    def body(w_ref, g_ref, m_ref, v_ref, gt_ref, d_ref, mo_ref, vo_ref):
        gt = g_ref[...].T
        gt_ref[...] = gt
        d_ref[...], mo_ref[...], vo_ref[...] = _adamw_math(w_ref[...], gt, m_ref[...], v_ref[...])

    row = pl.BlockSpec((T, Dm), lambda i: (i, 0))
    sh = jax.ShapeDtypeStruct((n, Dm), F32)
    return pl.pallas_call(
        body, name=name, grid=(pl.cdiv(n, T),), in_specs=[row, pl.BlockSpec((Dm, T), lambda i: (0, i)), row, row],
        out_specs=(row,) * 4, out_shape=(sh,) * 4, compiler_params=_cparams(("parallel",)),
    )(w, g, m, v)


def _adamw(w, g, m, v, name):
    R, Cc = w.shape
    T = max([t for t in range(8, 257, 8) if R % t == 0], default=R)

    def body(w_ref, g_ref, m_ref, v_ref, d_ref, mo_ref, vo_ref):
        d_ref[...], mo_ref[...], vo_ref[...] = _adamw_math(w_ref[...], g_ref[...], m_ref[...], v_ref[...])

    sp = pl.BlockSpec((T, Cc), lambda i: (i, 0))
    sh = jax.ShapeDtypeStruct((R, Cc), F32)
    return pl.pallas_call(
        body, name=name, grid=(R // T,), in_specs=[sp] * 4, out_specs=(sp, sp, sp), out_shape=(sh, sh, sh),
        compiler_params=_cparams(("parallel",)),
    )(w, g, m, v)


SMALL_ROWS = 32
ROW_CONV, ROW_FCG, ROW_FCU = 5, 13, 22


def _adamw_math(w, g, m, v):
    mn = ADAM_B1 * m + (1.0 - ADAM_B1) * g
    vn = ADAM_B2 * v + (1.0 - ADAM_B2) * (g * g)
    c1 = 1.0 / (1.0 - ADAM_B1 ** ADAM_STEP)
    c2 = 1.0 / (1.0 - ADAM_B2 ** ADAM_STEP)
    return -ADAM_LR * ((mn * c1) / (jnp.sqrt(vn * c2) + ADAM_EPS) + ADAM_WD * w), mn, vn


def _pack_small(n1, n2, fn, gp, gn, conv, fcg, fcu, loss):
    W = D_MODEL

    def body(n1_ref, n2_ref, fn_ref, gp_ref, gn_ref, conv_ref, fcg_ref, fcu_ref, loss_ref, o_ref):
        o_ref[...] = jnp.zeros_like(o_ref)
        o_ref[0:1, :] = n1_ref[...]
        o_ref[1:2, :] = n2_ref[...]
        o_ref[2:3, :] = fn_ref[...]
        o_ref[3:4, 0:8] = gp_ref[0:1, 0:8]
        o_ref[3:4, 8:9] = loss_ref[0:1, 0:1]
        o_ref[4:5, 0:128] = gn_ref[...]
        for i in range(GDN_CONV):
            o_ref[ROW_CONV + 2 * i:ROW_CONV + 2 * i + 1, :] = conv_ref[i:i + 1, 0:W]
            o_ref[ROW_CONV + 2 * i + 1:ROW_CONV + 2 * i + 2, 0:3 * GDN_WIDTH - W] = conv_ref[i:i + 1, W:3 * GDN_WIDTH]
        for r0, ref in ((ROW_FCG, fcg_ref), (ROW_FCU, fcu_ref)):
            for i in range(FFN_CONV):
                for k in range(3):
                    n = min(W, D_FF - k * W)
                    o_ref[r0 + 3 * i + k:r0 + 3 * i + k + 1, 0:n] = ref[i:i + 1, k * W:k * W + n]

    return pl.pallas_call(body, name="pack_small", out_shape=jax.ShapeDtypeStruct((SMALL_ROWS, W), F32))(
        n1, n2, fn, gp, gn, conv, fcg, fcu, loss)


def _small_step(meq, small_all, small, ws, ms, vs):
    W = D_MODEL
    n = len(ws)
    cw, fw = ws[6].shape[1], ws[7].shape[1]

    def body(meq_ref, all_ref, own_ref, *refs):
        w_refs, m_refs, v_refs = refs[:n], refs[n:2 * n], refs[2 * n:3 * n]
        loss_ref = refs[3 * n]
        outs = refs[3 * n + 1:]
        me, q = meq_ref[0], meq_ref[1]
        red = None
        for d in range(8):
            term = jnp.where(me == d, own_ref[...], all_ref[d])
            red = term if red is None else red + term
        loss_ref[...] = jnp.broadcast_to(red[3:4, 8:9], loss_ref.shape)
        conv = [jnp.concatenate([red[ROW_CONV + 2 * i:ROW_CONV + 2 * i + 1, :],
                                 red[ROW_CONV + 2 * i + 1:ROW_CONV + 2 * i + 2, 0:3 * GDN_WIDTH - W]], axis=1)
                for i in range(GDN_CONV)]
        conv = jnp.concatenate(conv, axis=0)

        def fc_rows(r0):
            rows = [jnp.concatenate([red[r0 + 3 * i + k:r0 + 3 * i + k + 1, 0:min(W, D_FF - k * W)] for k in range(3)], axis=1)
                    for i in range(FFN_CONV)]
            return jnp.concatenate(rows, axis=0)

        fc = jnp.concatenate([fc_rows(ROW_FCG), fc_rows(ROW_FCU)], axis=1)

        def chip_block(full, width):
            out = None
            for j in range(N_CHIPS):
                term = jnp.where(q == j, full[:, width * j:width * (j + 1)], 0.0)
                out = term if out is None else out + term
            return out

        grads = [red[0:1, :], red[1:2, :], red[2:3, :], red[3:4, 0:4], red[3:4, 4:8], red[4:5, 0:128],
                 chip_block(conv, cw), chip_block(fc, fw)]
        for k in range(n):
            d_, m_, v_ = _adamw_math(w_refs[k][...], grads[k], m_refs[k][...], v_refs[k][...])
            outs[4 * k][...] = grads[k]
            outs[4 * k + 1][...] = d_
            outs[4 * k + 2][...] = m_
            outs[4 * k + 3][...] = v_

    full = lambda a: pl.BlockSpec(a.shape, lambda i, s_, nd=len(a.shape): (0,) * nd)
    arrays = [small_all, small, *ws, *ms, *vs]
    out_shapes = [jax.ShapeDtypeStruct((8, 128), F32)] + [jax.ShapeDtypeStruct(w.shape, F32) for w in ws for _ in range(4)]
    gs = pltpu.PrefetchScalarGridSpec(
        num_scalar_prefetch=1, grid=(1,), in_specs=[full(a) for a in arrays],
        out_specs=[pl.BlockSpec(o.shape, lambda i, s_, nd=len(o.shape): (0,) * nd) for o in out_shapes])
    return pl.pallas_call(body, name="small_step", grid_spec=gs, out_shape=out_shapes)(meq, *arrays)


def _pad_lanes(v, n=D_MODEL):
    return jnp.pad(v, ((0, 0), (0, n - v.shape[1])))


def kernel(x, norm1_w, w_in, conv_qkv_w, a_log, dt_bias, gdn_norm_w, w_out, norm2_w, w_up, ffn_conv_w, w_down, final_norm_w, loss_target, m_norm1_w, m_w_in, m_conv_qkv_w, m_a_log, m_dt_bias, m_gdn_norm_w, m_w_out, m_norm2_w, m_w_up, m_ffn_conv_w, m_w_down, m_final_norm_w, v_norm1_w, v_w_in, v_conv_qkv_w, v_a_log, v_dt_bias, v_gdn_norm_w, v_w_out, v_norm2_w, v_w_up, v_ffn_conv_w, v_w_down, v_final_norm_w):
    c = lax.axis_index("c")
    q = 2 * lax.axis_index("x") + lax.axis_index("y")
    S = x.shape[1]
    cq = jnp.stack([c, q]).astype(jnp.int32)

    *in_started, in_token = _gather_halves_start([_narrowed_shard(w_in), conv_qkv_w[0], ffn_conv_w[0]], x, "gather_in_start")
    w_in_l, m_w_in_l, v_w_in_l = (jnp.swapaxes(a + in_token[0:1, 0:1], 1, 2)[0] for a in (w_in, m_w_in, v_w_in))
    h1 = _rmsnorm_fwd(x[0], norm1_w, "norm1", after=[in_token])
    rest = [(a[0] + in_token[0:1, 0:1]).astype(_MXU) for a in (w_out, w_up, w_down)]
    in_shards, got_in = _gather_halves_wait(in_started, [w_in_l, m_w_in_l, v_w_in_l, h1, *rest], "gather_in_wait")
    (g_in, g_conv, g_fconv), (w_in_l, m_w_in_l, v_w_in_l) = _place_own(
        in_shards, _sibling_fill(got_in, "fill_in"), cq, "place_in", carry=[w_in_l, m_w_in_l, v_w_in_l])
    *rest_started, token = _gather_halves_start(rest, g_conv, "gather_rest_start")

    rest_state = {}

    def rest_arrived(after):
        rest_state["shards"], got = _gather_halves_wait(rest_started, after, "gather_rest_wait")
        *rest_state["fill"], tok = _sibling_fill_start(got, "fill_rest_start")
        return tok

    def rest_filled(after):
        got = _sibling_fill_wait(rest_state["fill"], after, "fill_rest_wait")
        g_out, g_up, g_down = _place_own(rest_state["shards"], got, cq, "place_rest")
        return g_out.reshape(D_MODEL, D_MODEL), g_up, g_down.reshape(D_FF, D_MODEL)

    rest_weights = (rest_arrived, rest_filled)
    wp = _wp_assemble(g_in, [token])
    conv_f = jnp.concatenate([g_conv[i] for i in range(N_CHIPS)], axis=1)
    fcw = jnp.concatenate([g_fconv[i] for i in range(N_CHIPS)], axis=1)
    gp = _pad_lanes(jnp.concatenate([a_log, dt_bias], axis=1), 128)
    fnw = final_norm_w[None, :]
    early = {}

    early_names = ("w_up", "w_down", "w_out")

    def early_sibling(d_wup, d_wdown, d_wout):
        *early["sibling"], tok = _grad_sibling_start(
            [d_wup, d_wdown.reshape(N_CHIPS, D_FF // N_CHIPS, D_MODEL), d_wout.reshape(N_CHIPS, D_MODEL // N_CHIPS, D_MODEL)],
            "grad_sibling_early_start")
        return tok

    def early_chips(after):
        fams_e, got_e = _grad_sibling_wait(early["sibling"], after, "grad_sibling_early_wait")
        early["parts"] = [_add_sibling(f, r, cq, "add_sibling_" + nm) for f, r, nm in zip(fams_e, got_e, early_names)]
        *early["started"], tok = _grad_chips_start([p[1] for p in early["parts"]], "grad_chips_start")
        return tok

    def late_sibling(d_wp):
        *early["late_sibling"], tok = _grad_sibling_start([d_wp[None]], "grad_sibling_late_start")
        return tok

    loss_l, dx, g = _local_step(x[0], loss_target[0], h1, norm1_w, norm2_w, fnw, gp, gdn_norm_w, wp,
                                conv_f, fcw, rest_weights, (early_sibling, early_chips, late_sibling))
    fams, got = _grad_sibling_wait(early["late_sibling"], [dx], "grad_sibling_late_wait")
    late_part = _add_sibling_split(fams[0], got[0], cq, "add_sibling_w_in")
    *late_started, late_token = _grad_chips_start([late_part[1]], "grad_chips_late_start")
    small = _pack_small(g["n1w"], g["n2w"], g["fnw"], g["gp"], g["gnw"], g["conv_w"], g["fcw_g"], g["fcw_u"], loss_l)
    *small_started, small_token = _grad_chips_start([], "small_gather_start", small)
    got3_e = _grad_chips_wait(early["started"], [late_token, small_token], "grad_chips_wait")
    g_w_up, g_w_down, g_w_out = _grad_share(
        [_add_chips(p[0], r3, cq, "add_chips_" + nm) for p, r3, nm in zip(early["parts"], got3_e, early_names)],
        "grad_share_early")
    big = {}

    def adamw_big(nm, w, gg, m, v):
        d_, m_, v_ = _adamw(w[0], gg, m[0], v[0], "adamw_" + nm)
        big[nm] = (gg[None], d_[None], m_[None], v_[None])

    adamw_big("w_up", w_up, g_w_up, m_w_up, v_w_up)
    adamw_big("w_down", w_down, g_w_down, m_w_down, v_w_down)
    adamw_big("w_out", w_out, g_w_out, m_w_out, v_w_out)
    got3, = _grad_chips_wait(late_started, [big[nm][1] for nm in early_names], "grad_chips_late_wait")
    g_w_in, = _grad_share([_add_chips(late_part[0], got3, cq, "add_chips_w_in")], "grad_share_late")
    g_t, d_t, m_t, v_t = _adamw_transposed(w_in_l, g_w_in, m_w_in_l, v_w_in_l, "adamw_w_in")
    big["w_in"] = tuple(jnp.swapaxes(t[None], 1, 2) for t in (g_t, d_t, m_t, v_t))
    small_all, small = _grad_chips_wait(small_started, [d_t], "small_gather_wait", with_small=True)
    small_names = ["norm1_w", "norm2_w", "final_norm_w", "a_log", "dt_bias", "gdn_norm_w", "conv_qkv_w", "ffn_conv_w"]
    loss_b, *small_out = _small_step(
        jnp.stack([2 * q + c, q]).astype(jnp.int32), small_all, small,
        [norm1_w, norm2_w, final_norm_w[None], a_log, dt_bias, gdn_norm_w, conv_qkv_w[0], ffn_conv_w[0]],
        [m_norm1_w, m_norm2_w, m_final_norm_w[None], m_a_log, m_dt_bias, m_gdn_norm_w, m_conv_qkv_w[0], m_ffn_conv_w[0]],
        [v_norm1_w, v_norm2_w, v_final_norm_w[None], v_a_log, v_dt_bias, v_gdn_norm_w, v_conv_qkv_w[0], v_ffn_conv_w[0]])
    like = dict(final_norm_w=lambda t: t[0], conv_qkv_w=lambda t: t[None], ffn_conv_w=lambda t: t[None])
    for k, nm in enumerate(small_names):
        big[nm] = tuple(like.get(nm, lambda t: t)(t) for t in small_out[4 * k:4 * k + 4])
    names = ["norm1_w", "w_in", "conv_qkv_w", "a_log", "dt_bias", "gdn_norm_w", "w_out", "norm2_w", "w_up",
             "ffn_conv_w", "w_down", "final_norm_w"]
    return (loss_b[0, 0], dx[None], *[big[n][0] for n in names], *[big[n][1] for n in names],
            *[big[n][2] for n in names], *[big[n][3] for n in names])
```
